```python
import jax, jax.numpy as jnp
from jax import lax
import numpy as np


D_MODEL = 1024
BATCH = 8
SEQ = 2048
DEPTH = 2

MIX = D_MODEL
HEAD_DIM = 64
ATT_WIDTH = MIX // 2
N_Q_HEADS = ATT_WIDTH // HEAD_DIM
N_KV_HEADS = N_Q_HEADS // 4
GQA_GROUP = N_Q_HEADS // N_KV_HEADS
WINDOW = 128
BLOCK = 128
CV_CH = MIX // 4
CONV_K = 31
RET_WIDTH = MIX // 4
RET_DK = 64
RET_DV = 64
RET_HEADS = RET_WIDTH // RET_DV
N_META = 16
PAD = BLOCK - N_META
D_FF = ((8 * D_MODEL // 3 + 127) // 128) * 128
FFN_CONV_K = 3
RMS_EPS = 1e-6
LN_EPS = 1e-5
SPLIT_SIZES = (N_Q_HEADS * HEAD_DIM, N_KV_HEADS * HEAD_DIM, N_KV_HEADS * HEAD_DIM,
               CV_CH, CV_CH,
               RET_HEADS * RET_DK, RET_HEADS * RET_DK, RET_HEADS * RET_DV, RET_WIDTH)
IN_WIDTH = sum(SPLIT_SIZES)

kernel_name = 'hymba_swa_conformer_retention_convffn'


def rmsnorm(x, g):
    xf = x.astype(jnp.float32)
    y = xf * lax.rsqrt(jnp.mean(xf * xf, axis=-1, keepdims=True) + RMS_EPS)
    return (y * g.astype(jnp.float32)).astype(x.dtype)


def layernorm(x, g, b):
    xf = x.astype(jnp.float32)
    mu = jnp.mean(xf, axis=-1, keepdims=True)
    var = jnp.mean(jnp.square(xf - mu), axis=-1, keepdims=True)
    y = (xf - mu) * lax.rsqrt(var + LN_EPS)
    return (y * g.astype(jnp.float32) + b.astype(jnp.float32)).astype(x.dtype)


def causal_dwconv(x, w, b):
    ksz, ch = w.shape
    y = lax.conv_general_dilated(
        x, w.astype(x.dtype)[:, None, :], window_strides=(1,), padding=[(ksz - 1, 0)],
        dimension_numbers=('NWC', 'WIO', 'NWC'), feature_group_count=ch)
    return y + b.astype(x.dtype)


def split_cols(z):
    out, start = [], 0
    for w in SPLIT_SIZES:
        out.append(z[..., start:start + w])
        start += w
    return out


def sliding_window_sink_attention(q, k, v, sinks, slopes):
    B, L = q.shape[0], q.shape[1]
    nb = L // BLOCK
    qb = q.reshape(B, nb, BLOCK, N_KV_HEADS, GQA_GROUP, HEAD_DIM)

    def band(z):
        zb = z.reshape(B, nb, BLOCK, N_KV_HEADS, HEAD_DIM)
        prev = jnp.concatenate([jnp.zeros_like(zb[:, :1]), zb[:, :-1]], axis=1)
        return jnp.concatenate([prev, zb], axis=2)

    def meta(z):
        return jnp.broadcast_to(z[:, None, PAD:BLOCK], (B, nb, N_META, N_KV_HEADS, HEAD_DIM))

    keys = jnp.concatenate([meta(k), band(k)], axis=2)
    vals = jnp.concatenate([meta(v), band(v)], axis=2)

    blk = jnp.arange(nb)[:, None]
    t = blk * BLOCK + jnp.arange(BLOCK)[None, :]
    s_band = (blk - 1) * BLOCK + jnp.arange(2 * BLOCK)[None, :]
    s_meta = PAD + jnp.arange(N_META)
    d_band = t[:, :, None] - s_band[:, None, :]
    d_meta = t[:, :, None] - s_meta[None, None, :]
    ok_band = (d_band >= 0) & (d_band < WINDOW) & (s_band[:, None, :] >= BLOCK)
    ok_meta = d_meta >= 0
    ok = jnp.concatenate([ok_meta, ok_band], axis=-1)
    dist = jnp.concatenate([jnp.minimum(d_meta, WINDOW), d_band], axis=-1).astype(jnp.float32)

    scale = HEAD_DIM ** -0.5
    logits = jnp.einsum('bnihgd,bnjhd->bnhgij', qb, keys).astype(jnp.float32) * scale
    sl = slopes.reshape(N_KV_HEADS, GQA_GROUP)[None, None, :, :, None, None]
    logits = logits - sl * dist[None, :, None, None, :, :]
    logits = jnp.where(ok[None, :, None, None], logits, -jnp.inf)
    sink = jnp.broadcast_to(
        sinks.astype(jnp.float32).reshape(N_KV_HEADS, GQA_GROUP)[None, None, :, :, None, None],
        logits.shape[:-1] + (1,))
    probs = jax.nn.softmax(jnp.concatenate([logits, sink], axis=-1), axis=-1)[..., :-1]
    out = jnp.einsum('bnhgij,bnjhd->bnihgd', probs.astype(v.dtype), vals)
    return out.reshape(B, L, N_Q_HEADS * HEAD_DIM)


def conformer_conv(a, b, w_dw, b_dw, ln_g, ln_b, w_pw):
    u = a * jax.nn.sigmoid(b)
    u = causal_dwconv(u, w_dw, b_dw)
    u = layernorm(u, ln_g, ln_b)
    u = jax.nn.silu(u)
    return u @ w_pw.astype(u.dtype)


def chunkwise_retention(q, k, v, g, gn_g, valid):
    B, L = q.shape[0], q.shape[1]
    nc = L // BLOCK
    qf = q.astype(jnp.float32).reshape(B, nc, BLOCK, RET_HEADS, RET_DK)
    kf = (k.astype(jnp.float32) * (RET_DK ** -0.5) * valid[:, :, None, None].astype(jnp.float32))
    kf = kf.reshape(B, nc, BLOCK, RET_HEADS, RET_DK)
    vf = v.astype(jnp.float32).reshape(B, nc, BLOCK, RET_HEADS, RET_DV)

    log_gamma = jnp.log1p(-jnp.exp2(-5.0 - jnp.arange(RET_HEADS, dtype=jnp.float32)))
    idx = jnp.arange(BLOCK, dtype=jnp.float32)
    diff = idx[:, None] - idx[None, :]
    decay = jnp.where(diff[None] >= 0, jnp.exp(jnp.maximum(diff, 0.0)[None] * log_gamma[:, None, None]), 0.0)
    zeta = jnp.exp((BLOCK - 1 - idx)[None, :] * log_gamma[:, None])
    xi = jnp.exp((idx + 1.0)[None, :] * log_gamma[:, None])
    chunk_decay = jnp.exp(BLOCK * log_gamma)

    scores = jnp.einsum('bnihd,bnjhd->bnhij', qf, kf) * decay[None, None]
    y = jnp.einsum('bnhij,bnjhe->bnihe', scores, vf)
    kv = jnp.einsum('bnjhd,bnjhe,hj->bnhde', kf, vf, zeta)

    def step(state, kv_n):
        return chunk_decay[None, :, None, None] * state + kv_n, state

    init = jnp.zeros((B, RET_HEADS, RET_DK, RET_DV), jnp.float32)
    _, r_prev = lax.scan(step, init, jnp.moveaxis(kv, 1, 0))
    r_prev = jnp.moveaxis(r_prev, 0, 1)
    y = y + jnp.einsum('bnihd,bnhde,hi->bnihe', qf, r_prev, xi)

    y = y.reshape(B, L, RET_HEADS, RET_DV)
    mu = jnp.mean(y, axis=-1, keepdims=True)
    var = jnp.mean(jnp.square(y - mu), axis=-1, keepdims=True)
    y = ((y - mu) * lax.rsqrt(var + LN_EPS)).reshape(B, L, RET_WIDTH) * gn_g.astype(jnp.float32)
    return jax.nn.silu(g) * y.astype(g.dtype)


def _fwd_setup_inputs(seed: int = 0) -> dict:
    key = jax.random.key(seed)
    ks = jax.random.split(key, 22)
    f32 = jnp.float32

    def nrm(k, shape, scale):
        return jax.random.normal(k, shape, f32) * scale

    return {
        'x': nrm(ks[0], (BATCH, SEQ, D_MODEL), 1.0),
        'meta': nrm(ks[1], (N_META, D_MODEL), 1.0),
        'norm_mix_g': 1.0 + nrm(ks[2], (DEPTH, D_MODEL), 0.02),
        'w_in': nrm(ks[3], (DEPTH, D_MODEL, IN_WIDTH), D_MODEL ** -0.5),
        'q_norm_g': 1.0 + nrm(ks[4], (DEPTH, HEAD_DIM), 0.02),
        'k_norm_g': 1.0 + nrm(ks[5], (DEPTH, HEAD_DIM), 0.02),
        'attn_sinks': nrm(ks[6], (DEPTH, N_Q_HEADS), 0.5),
        'attn_out_g': 1.0 + nrm(ks[7], (DEPTH, ATT_WIDTH), 0.02),
        'cv_dw_w': nrm(ks[8], (DEPTH, CONV_K, CV_CH), CONV_K ** -0.5),
        'cv_dw_b': nrm(ks[9], (DEPTH, CV_CH), 0.02),
        'cv_ln_g': 1.0 + nrm(ks[10], (DEPTH, CV_CH), 0.02),
        'cv_ln_b': nrm(ks[11], (DEPTH, CV_CH), 0.02),
        'cv_pw': nrm(ks[12], (DEPTH, CV_CH, CV_CH), CV_CH ** -0.5),
        'cv_out_g': 1.0 + nrm(ks[13], (DEPTH, CV_CH), 0.02),
        'ret_gn_g': 1.0 + nrm(ks[14], (DEPTH, RET_WIDTH), 0.02),
        'w_out': nrm(ks[15], (DEPTH, MIX, D_MODEL), MIX ** -0.5),
        'norm_ffn_g': 1.0 + nrm(ks[16], (DEPTH, D_MODEL), 0.02),
        'ffn_up': nrm(ks[17], (DEPTH, D_MODEL, 2 * D_FF), D_MODEL ** -0.5),
        'ffn_dw_w': nrm(ks[18], (DEPTH, FFN_CONV_K, 2 * D_FF), FFN_CONV_K ** -0.5),
        'ffn_dw_b': nrm(ks[19], (DEPTH, 2 * D_FF), 0.02),
        'ffn_down': nrm(ks[20], (DEPTH, D_FF, D_MODEL), D_FF ** -0.5),
    }


def _fwd_reference(x, meta, norm_mix_g, w_in, q_norm_g, k_norm_g, attn_sinks, attn_out_g,
              cv_dw_w, cv_dw_b, cv_ln_g, cv_ln_b, cv_pw, cv_out_g, ret_gn_g, w_out,
              norm_ffn_g, ffn_up, ffn_dw_w, ffn_dw_b, ffn_down):
    B, S, D = x.shape
    L = S + BLOCK
    dt = x.dtype
    h = jnp.concatenate([
        jnp.zeros((B, PAD, D), dt),
        jnp.broadcast_to(meta.astype(dt)[None], (B, N_META, D)),
        x], axis=1)
    valid = (jnp.arange(L) >= PAD).astype(dt)[None, :, None]
    slopes = jnp.exp2(-8.0 * jnp.arange(1, N_Q_HEADS + 1, dtype=jnp.float32) / N_Q_HEADS)

    for l in range(DEPTH):
        u = rmsnorm(h, norm_mix_g[l])
        proj = u @ w_in[l].astype(dt)
        q, k, v, ca, cb, rq, rk, rv, rg = split_cols(proj)
        q = rmsnorm(q.reshape(B, L, N_Q_HEADS, HEAD_DIM), q_norm_g[l])
        k = rmsnorm(k.reshape(B, L, N_KV_HEADS, HEAD_DIM), k_norm_g[l])
        v = v.reshape(B, L, N_KV_HEADS, HEAD_DIM)
        y_att = rmsnorm(sliding_window_sink_attention(q, k, v, attn_sinks[l], slopes), attn_out_g[l])
        y_cv = rmsnorm(conformer_conv(ca, cb, cv_dw_w[l], cv_dw_b[l], cv_ln_g[l], cv_ln_b[l], cv_pw[l]),
                       cv_out_g[l])
        y_ret = chunkwise_retention(rq.reshape(B, L, RET_HEADS, RET_DK),
                                    rk.reshape(B, L, RET_HEADS, RET_DK),
                                    rv.reshape(B, L, RET_HEADS, RET_DV),
                                    rg, ret_gn_g[l], valid[..., 0])
        y = jnp.concatenate([y_att, y_cv, y_ret], axis=-1) @ w_out[l].astype(dt)
        h = h + valid * y
        f = rmsnorm(h, norm_ffn_g[l]) @ ffn_up[l].astype(dt)
        f = causal_dwconv(f, ffn_dw_w[l], ffn_dw_b[l])
        fg, fu = f[..., :D_FF], f[..., D_FF:]
        h = h + valid * ((jax.nn.silu(fg) * fu) @ ffn_down[l].astype(dt))

    return h[:, BLOCK:, :]


import jax as _jax
import jax.numpy as _jnp

TWIN_FORMAT = 'train_step'
FWD_PARAMS = ['x', 'meta', 'norm_mix_g', 'w_in', 'q_norm_g', 'k_norm_g', 'attn_sinks', 'attn_out_g', 'cv_dw_w', 'cv_dw_b', 'cv_ln_g', 'cv_ln_b', 'cv_pw', 'cv_out_g', 'ret_gn_g', 'w_out', 'norm_ffn_g', 'ffn_up', 'ffn_dw_w', 'ffn_dw_b', 'ffn_down']
TWIN_WEIGHTS = ['meta', 'norm_mix_g', 'w_in', 'q_norm_g', 'k_norm_g', 'attn_sinks', 'attn_out_g', 'cv_dw_w', 'cv_dw_b', 'cv_ln_g', 'cv_ln_b', 'cv_pw', 'cv_out_g', 'ret_gn_g', 'w_out', 'norm_ffn_g', 'ffn_up', 'ffn_dw_w', 'ffn_dw_b', 'ffn_down']
TWIN_DIFF_INPUT = 'x'
TWIN_INPUTS = ['x', 'meta', 'norm_mix_g', 'w_in', 'q_norm_g', 'k_norm_g', 'attn_sinks', 'attn_out_g', 'cv_dw_w', 'cv_dw_b', 'cv_ln_g', 'cv_ln_b', 'cv_pw', 'cv_out_g', 'ret_gn_g', 'w_out', 'norm_ffn_g', 'ffn_up', 'ffn_dw_w', 'ffn_dw_b', 'ffn_down', 'loss_target', 'm_meta', 'm_norm_mix_g', 'm_w_in', 'm_q_norm_g', 'm_k_norm_g', 'm_attn_sinks', 'm_attn_out_g', 'm_cv_dw_w', 'm_cv_dw_b', 'm_cv_ln_g', 'm_cv_ln_b', 'm_cv_pw', 'm_cv_out_g', 'm_ret_gn_g', 'm_w_out', 'm_norm_ffn_g', 'm_ffn_up', 'm_ffn_dw_w', 'm_ffn_dw_b', 'm_ffn_down', 'v_meta', 'v_norm_mix_g', 'v_w_in', 'v_q_norm_g', 'v_k_norm_g', 'v_attn_sinks', 'v_attn_out_g', 'v_cv_dw_w', 'v_cv_dw_b', 'v_cv_ln_g', 'v_cv_ln_b', 'v_cv_pw', 'v_cv_out_g', 'v_ret_gn_g', 'v_w_out', 'v_norm_ffn_g', 'v_ffn_up', 'v_ffn_dw_w', 'v_ffn_dw_b', 'v_ffn_down']
TWIN_OUTPUTS = ['loss', 'grad_x', 'grad_meta', 'grad_norm_mix_g', 'grad_w_in', 'grad_q_norm_g', 'grad_k_norm_g', 'grad_attn_sinks', 'grad_attn_out_g', 'grad_cv_dw_w', 'grad_cv_dw_b', 'grad_cv_ln_g', 'grad_cv_ln_b', 'grad_cv_pw', 'grad_cv_out_g', 'grad_ret_gn_g', 'grad_w_out', 'grad_norm_ffn_g', 'grad_ffn_up', 'grad_ffn_dw_w', 'grad_ffn_dw_b', 'grad_ffn_down', 'delta_meta', 'delta_norm_mix_g', 'delta_w_in', 'delta_q_norm_g', 'delta_k_norm_g', 'delta_attn_sinks', 'delta_attn_out_g', 'delta_cv_dw_w', 'delta_cv_dw_b', 'delta_cv_ln_g', 'delta_cv_ln_b', 'delta_cv_pw', 'delta_cv_out_g', 'delta_ret_gn_g', 'delta_w_out', 'delta_norm_ffn_g', 'delta_ffn_up', 'delta_ffn_dw_w', 'delta_ffn_dw_b', 'delta_ffn_down', 'new_m_meta', 'new_m_norm_mix_g', 'new_m_w_in', 'new_m_q_norm_g', 'new_m_k_norm_g', 'new_m_attn_sinks', 'new_m_attn_out_g', 'new_m_cv_dw_w', 'new_m_cv_dw_b', 'new_m_cv_ln_g', 'new_m_cv_ln_b', 'new_m_cv_pw', 'new_m_cv_out_g', 'new_m_ret_gn_g', 'new_m_w_out', 'new_m_norm_ffn_g', 'new_m_ffn_up', 'new_m_ffn_dw_w', 'new_m_ffn_dw_b', 'new_m_ffn_down', 'new_v_meta', 'new_v_norm_mix_g', 'new_v_w_in', 'new_v_q_norm_g', 'new_v_k_norm_g', 'new_v_attn_sinks', 'new_v_attn_out_g', 'new_v_cv_dw_w', 'new_v_cv_dw_b', 'new_v_cv_ln_g', 'new_v_cv_ln_b', 'new_v_cv_pw', 'new_v_cv_out_g', 'new_v_ret_gn_g', 'new_v_w_out', 'new_v_norm_ffn_g', 'new_v_ffn_up', 'new_v_ffn_dw_w', 'new_v_ffn_dw_b', 'new_v_ffn_down']
TWIN_LEAF_KINDS = {'loss': 'loss', 'grad_x': 'grad_x', 'grad_meta': 'grad_w', 'grad_norm_mix_g': 'grad_w', 'grad_w_in': 'grad_w', 'grad_q_norm_g': 'grad_w', 'grad_k_norm_g': 'grad_w', 'grad_attn_sinks': 'grad_w', 'grad_attn_out_g': 'grad_w', 'grad_cv_dw_w': 'grad_w', 'grad_cv_dw_b': 'grad_w', 'grad_cv_ln_g': 'grad_w', 'grad_cv_ln_b': 'grad_w', 'grad_cv_pw': 'grad_w', 'grad_cv_out_g': 'grad_w', 'grad_ret_gn_g': 'grad_w', 'grad_w_out': 'grad_w', 'grad_norm_ffn_g': 'grad_w', 'grad_ffn_up': 'grad_w', 'grad_ffn_dw_w': 'grad_w', 'grad_ffn_dw_b': 'grad_w', 'grad_ffn_down': 'grad_w', 'delta_meta': 'delta_w', 'delta_norm_mix_g': 'delta_w', 'delta_w_in': 'delta_w', 'delta_q_norm_g': 'delta_w', 'delta_k_norm_g': 'delta_w', 'delta_attn_sinks': 'delta_w', 'delta_attn_out_g': 'delta_w', 'delta_cv_dw_w': 'delta_w', 'delta_cv_dw_b': 'delta_w', 'delta_cv_ln_g': 'delta_w', 'delta_cv_ln_b': 'delta_w', 'delta_cv_pw': 'delta_w', 'delta_cv_out_g': 'delta_w', 'delta_ret_gn_g': 'delta_w', 'delta_w_out': 'delta_w', 'delta_norm_ffn_g': 'delta_w', 'delta_ffn_up': 'delta_w', 'delta_ffn_dw_w': 'delta_w', 'delta_ffn_dw_b': 'delta_w', 'delta_ffn_down': 'delta_w', 'new_m_meta': 'new_m', 'new_m_norm_mix_g': 'new_m', 'new_m_w_in': 'new_m', 'new_m_q_norm_g': 'new_m', 'new_m_k_norm_g': 'new_m', 'new_m_attn_sinks': 'new_m', 'new_m_attn_out_g': 'new_m', 'new_m_cv_dw_w': 'new_m', 'new_m_cv_dw_b': 'new_m', 'new_m_cv_ln_g': 'new_m', 'new_m_cv_ln_b': 'new_m', 'new_m_cv_pw': 'new_m', 'new_m_cv_out_g': 'new_m', 'new_m_ret_gn_g': 'new_m', 'new_m_w_out': 'new_m', 'new_m_norm_ffn_g': 'new_m', 'new_m_ffn_up': 'new_m', 'new_m_ffn_dw_w': 'new_m', 'new_m_ffn_dw_b': 'new_m', 'new_m_ffn_down': 'new_m', 'new_v_meta': 'new_v', 'new_v_norm_mix_g': 'new_v', 'new_v_w_in': 'new_v', 'new_v_q_norm_g': 'new_v', 'new_v_k_norm_g': 'new_v', 'new_v_attn_sinks': 'new_v', 'new_v_attn_out_g': 'new_v', 'new_v_cv_dw_w': 'new_v', 'new_v_cv_dw_b': 'new_v', 'new_v_cv_ln_g': 'new_v', 'new_v_cv_ln_b': 'new_v', 'new_v_cv_pw': 'new_v', 'new_v_cv_out_g': 'new_v', 'new_v_ret_gn_g': 'new_v', 'new_v_w_out': 'new_v', 'new_v_norm_ffn_g': 'new_v', 'new_v_ffn_up': 'new_v', 'new_v_ffn_dw_w': 'new_v', 'new_v_ffn_dw_b': 'new_v', 'new_v_ffn_down': 'new_v'}


def _forward(args):
    return _fwd_reference(*[args[k] for k in FWD_PARAMS])


def _output_shape():
    out = _jax.eval_shape(lambda: _forward(_fwd_setup_inputs(0)))
    return out.shape, out.dtype

N_MICROBATCH = 1
ADAM_LR = 0.001
ADAM_B1 = 0.9
ADAM_B2 = 0.999
ADAM_EPS = 1e-08
ADAM_WD = 0.01
ADAM_STEP = 10
PER_EXAMPLE_BATCH_AXIS = {'x': 0, 'loss_target': 0}
SHARED_INPUTS = []
_WEIGHT_DTYPES = {'meta': _jnp.float32, 'norm_mix_g': _jnp.float32, 'w_in': _jnp.float32, 'q_norm_g': _jnp.float32, 'k_norm_g': _jnp.float32, 'attn_sinks': _jnp.float32, 'attn_out_g': _jnp.float32, 'cv_dw_w': _jnp.float32, 'cv_dw_b': _jnp.float32, 'cv_ln_g': _jnp.float32, 'cv_ln_b': _jnp.float32, 'cv_pw': _jnp.float32, 'cv_out_g': _jnp.float32, 'ret_gn_g': _jnp.float32, 'w_out': _jnp.float32, 'norm_ffn_g': _jnp.float32, 'ffn_up': _jnp.float32, 'ffn_dw_w': _jnp.float32, 'ffn_dw_b': _jnp.float32, 'ffn_down': _jnp.float32}
MOMENT_SCALE = {'meta': 8.858024e-02, 'norm_mix_g': 1.747419e+00, 'w_in': 7.169218e-01, 'q_norm_g': 1.100669e+00, 'k_norm_g': 1.082511e+00, 'attn_sinks': 3.167105e+00, 'attn_out_g': 1.948924e+01, 'cv_dw_w': 9.407595e-01, 'cv_dw_b': 1.458496e+01, 'cv_ln_g': 5.441413e+00, 'cv_ln_b': 7.757955e+00, 'cv_pw': 3.046797e+00, 'cv_out_g': 1.637095e+01, 'ret_gn_g': 5.619878e+00, 'w_out': 1.966944e+00, 'norm_ffn_g': 1.254790e+01, 'ffn_up': 4.201541e-01, 'ffn_dw_w': 1.891282e+00, 'ffn_dw_b': 1.848870e+00, 'ffn_down': 3.909232e-01}


def _to_microbatches(a, axis):
    t = _jnp.moveaxis(a, axis, 0)
    t = t.reshape((N_MICROBATCH, t.shape[0] // N_MICROBATCH) + t.shape[1:])
    return _jnp.moveaxis(t, 1, axis + 1)


def setup_inputs(seed: int = 0) -> dict:
    inp = _fwd_setup_inputs(seed)
    key = _jax.random.fold_in(_jax.random.key(seed), 7919)
    shape, _ = _output_shape()
    out = dict(inp)
    out["loss_target"] = _jax.random.normal(_jax.random.fold_in(key, 0), shape, _jnp.float32)
    for i, name in enumerate(TWIN_WEIGHTS):
        w = inp[name].astype(_jnp.float32)
        if MOMENT_SCALE is None:
            s = _jnp.sqrt(_jnp.mean(_jnp.square(w)) + 1e-30)
        else:
            s = MOMENT_SCALE[name]
        km, kv = _jax.random.split(_jax.random.fold_in(key, i + 1))
        out[name] = w
        out["m_" + name] = s * _jax.random.normal(km, w.shape, _jnp.float32)
        out["v_" + name] = (s * s) * _jax.random.uniform(kv, w.shape, _jnp.float32, 0.5, 1.5)
    if N_MICROBATCH > 1:
        for name, axis in PER_EXAMPLE_BATCH_AXIS.items():
            out[name] = _to_microbatches(out[name], axis)
    return {'x': out['x'], 'meta': out['meta'], 'norm_mix_g': out['norm_mix_g'], 'w_in': out['w_in'], 'q_norm_g': out['q_norm_g'], 'k_norm_g': out['k_norm_g'], 'attn_sinks': out['attn_sinks'], 'attn_out_g': out['attn_out_g'], 'cv_dw_w': out['cv_dw_w'], 'cv_dw_b': out['cv_dw_b'], 'cv_ln_g': out['cv_ln_g'], 'cv_ln_b': out['cv_ln_b'], 'cv_pw': out['cv_pw'], 'cv_out_g': out['cv_out_g'], 'ret_gn_g': out['ret_gn_g'], 'w_out': out['w_out'], 'norm_ffn_g': out['norm_ffn_g'], 'ffn_up': out['ffn_up'], 'ffn_dw_w': out['ffn_dw_w'], 'ffn_dw_b': out['ffn_dw_b'], 'ffn_down': out['ffn_down'], 'loss_target': out['loss_target'], 'm_meta': out['m_meta'], 'm_norm_mix_g': out['m_norm_mix_g'], 'm_w_in': out['m_w_in'], 'm_q_norm_g': out['m_q_norm_g'], 'm_k_norm_g': out['m_k_norm_g'], 'm_attn_sinks': out['m_attn_sinks'], 'm_attn_out_g': out['m_attn_out_g'], 'm_cv_dw_w': out['m_cv_dw_w'], 'm_cv_dw_b': out['m_cv_dw_b'], 'm_cv_ln_g': out['m_cv_ln_g'], 'm_cv_ln_b': out['m_cv_ln_b'], 'm_cv_pw': out['m_cv_pw'], 'm_cv_out_g': out['m_cv_out_g'], 'm_ret_gn_g': out['m_ret_gn_g'], 'm_w_out': out['m_w_out'], 'm_norm_ffn_g': out['m_norm_ffn_g'], 'm_ffn_up': out['m_ffn_up'], 'm_ffn_dw_w': out['m_ffn_dw_w'], 'm_ffn_dw_b': out['m_ffn_dw_b'], 'm_ffn_down': out['m_ffn_down'], 'v_meta': out['v_meta'], 'v_norm_mix_g': out['v_norm_mix_g'], 'v_w_in': out['v_w_in'], 'v_q_norm_g': out['v_q_norm_g'], 'v_k_norm_g': out['v_k_norm_g'], 'v_attn_sinks': out['v_attn_sinks'], 'v_attn_out_g': out['v_attn_out_g'], 'v_cv_dw_w': out['v_cv_dw_w'], 'v_cv_dw_b': out['v_cv_dw_b'], 'v_cv_ln_g': out['v_cv_ln_g'], 'v_cv_ln_b': out['v_cv_ln_b'], 'v_cv_pw': out['v_cv_pw'], 'v_cv_out_g': out['v_cv_out_g'], 'v_ret_gn_g': out['v_ret_gn_g'], 'v_w_out': out['v_w_out'], 'v_norm_ffn_g': out['v_norm_ffn_g'], 'v_ffn_up': out['v_ffn_up'], 'v_ffn_dw_w': out['v_ffn_dw_w'], 'v_ffn_dw_b': out['v_ffn_dw_b'], 'v_ffn_down': out['v_ffn_down']}


def _loss(weights, diff, rest, loss_target):
    with _jax.named_scope("forward"):
        args = {**rest, TWIN_DIFF_INPUT: diff, **{k: w.astype(_WEIGHT_DTYPES[k]) for k, w in weights.items()}}
        y = _forward(args)
    with _jax.named_scope("loss_head"):
        err = _jnp.square(y.astype(_jnp.float32) - loss_target)
        return 0.5 * _jnp.sum(_jnp.mean(err, axis=-1)) if err.ndim else 0.5 * err


def _adamw(w, g, m, v):
    m = ADAM_B1 * m + (1.0 - ADAM_B1) * g
    v = ADAM_B2 * v + (1.0 - ADAM_B2) * _jnp.square(g)
    m_hat = m / (1.0 - ADAM_B1 ** ADAM_STEP)
    v_hat = v / (1.0 - ADAM_B2 ** ADAM_STEP)
    delta = -ADAM_LR * (m_hat / (_jnp.sqrt(v_hat) + ADAM_EPS) + ADAM_WD * w)
    return delta, m, v


def reference(x, meta, norm_mix_g, w_in, q_norm_g, k_norm_g, attn_sinks, attn_out_g, cv_dw_w, cv_dw_b, cv_ln_g, cv_ln_b, cv_pw, cv_out_g, ret_gn_g, w_out, norm_ffn_g, ffn_up, ffn_dw_w, ffn_dw_b, ffn_down, loss_target, m_meta, m_norm_mix_g, m_w_in, m_q_norm_g, m_k_norm_g, m_attn_sinks, m_attn_out_g, m_cv_dw_w, m_cv_dw_b, m_cv_ln_g, m_cv_ln_b, m_cv_pw, m_cv_out_g, m_ret_gn_g, m_w_out, m_norm_ffn_g, m_ffn_up, m_ffn_dw_w, m_ffn_dw_b, m_ffn_down, v_meta, v_norm_mix_g, v_w_in, v_q_norm_g, v_k_norm_g, v_attn_sinks, v_attn_out_g, v_cv_dw_w, v_cv_dw_b, v_cv_ln_g, v_cv_ln_b, v_cv_pw, v_cv_out_g, v_ret_gn_g, v_w_out, v_norm_ffn_g, v_ffn_up, v_ffn_dw_w, v_ffn_dw_b, v_ffn_down):
    given = dict(x=x, meta=meta, norm_mix_g=norm_mix_g, w_in=w_in, q_norm_g=q_norm_g, k_norm_g=k_norm_g, attn_sinks=attn_sinks, attn_out_g=attn_out_g, cv_dw_w=cv_dw_w, cv_dw_b=cv_dw_b, cv_ln_g=cv_ln_g, cv_ln_b=cv_ln_b, cv_pw=cv_pw, cv_out_g=cv_out_g, ret_gn_g=ret_gn_g, w_out=w_out, norm_ffn_g=norm_ffn_g, ffn_up=ffn_up, ffn_dw_w=ffn_dw_w, ffn_dw_b=ffn_dw_b, ffn_down=ffn_down, loss_target=loss_target, m_meta=m_meta, m_norm_mix_g=m_norm_mix_g, m_w_in=m_w_in, m_q_norm_g=m_q_norm_g, m_k_norm_g=m_k_norm_g, m_attn_sinks=m_attn_sinks, m_attn_out_g=m_attn_out_g, m_cv_dw_w=m_cv_dw_w, m_cv_dw_b=m_cv_dw_b, m_cv_ln_g=m_cv_ln_g, m_cv_ln_b=m_cv_ln_b, m_cv_pw=m_cv_pw, m_cv_out_g=m_cv_out_g, m_ret_gn_g=m_ret_gn_g, m_w_out=m_w_out, m_norm_ffn_g=m_norm_ffn_g, m_ffn_up=m_ffn_up, m_ffn_dw_w=m_ffn_dw_w, m_ffn_dw_b=m_ffn_dw_b, m_ffn_down=m_ffn_down, v_meta=v_meta, v_norm_mix_g=v_norm_mix_g, v_w_in=v_w_in, v_q_norm_g=v_q_norm_g, v_k_norm_g=v_k_norm_g, v_attn_sinks=v_attn_sinks, v_attn_out_g=v_attn_out_g, v_cv_dw_w=v_cv_dw_w, v_cv_dw_b=v_cv_dw_b, v_cv_ln_g=v_cv_ln_g, v_cv_ln_b=v_cv_ln_b, v_cv_pw=v_cv_pw, v_cv_out_g=v_cv_out_g, v_ret_gn_g=v_ret_gn_g, v_w_out=v_w_out, v_norm_ffn_g=v_norm_ffn_g, v_ffn_up=v_ffn_up, v_ffn_dw_w=v_ffn_dw_w, v_ffn_dw_b=v_ffn_dw_b, v_ffn_down=v_ffn_down)
    weights = {n: given[n] for n in TWIN_WEIGHTS}
    shared = {n: given[n] for n in SHARED_INPUTS}
    per_example = {n: given[n] for n in ['x']}
    grad_fn = _jax.value_and_grad(_loss, argnums=(0, 1))

    def one_microbatch(ex, loss_target):
        ex = dict(ex)
        diff = ex.pop(TWIN_DIFF_INPUT)
        return grad_fn(weights, diff, {**shared, **ex}, loss_target)

    if N_MICROBATCH == 1:
        loss, (grad_w, grad_x) = one_microbatch(per_example, given["loss_target"])
    else:
        def body(carry, xs):
            loss_sum, grad_sum = carry
            l_k, (gw_k, gx_k) = one_microbatch(xs[0], xs[1])
            with _jax.named_scope("update"):
                return (loss_sum + l_k, _jax.tree.map(_jnp.add, grad_sum, gw_k)), gx_k

        init = (_jnp.zeros((), _jnp.float32), _jax.tree.map(_jnp.zeros_like, weights))
        (loss, grad_w), grad_x = _jax.lax.scan(body, init, (per_example, given["loss_target"]))
    with _jax.named_scope("update"):
        delta_w, new_m, new_v = {}, {}, {}
        for n in TWIN_WEIGHTS:
            delta_w[n], new_m[n], new_v[n] = _adamw(weights[n], grad_w[n], given["m_" + n], given["v_" + n])
    return (loss, grad_x, *[grad_w[n] for n in TWIN_WEIGHTS], *[delta_w[n] for n in TWIN_WEIGHTS],
            *[new_m[n] for n in TWIN_WEIGHTS], *[new_v[n] for n in TWIN_WEIGHTS])
```

```python
import functools
import math

import numpy as np
import jax
import jax.numpy as jnp
from jax import lax
from jax.experimental import pallas as pl
from jax.experimental.pallas import tpu as pltpu

F32 = jnp.float32
BF16 = jnp.bfloat16

D = 1024
SEQ = 2048
DEPTH = 2
T = 128
L = SEQ + T
NB = L // T
N_META = 16
PAD = T - N_META
HD = 64
NQ = 8
NKV = 2
GQA = NQ // NKV
CV = 256
CONV_K = 31
RH = 4
D_FF = 2816
FFN_K = 3
IN_W = 2304
RMS_EPS = 1e-6
LN_EPS = 1e-5
NEG = -1e30
NDEV = 8
UPW = 2 * D_FF // NDEV
DNR = D_FF // NDEV
NPAIR = NDEV // 2

ADAM_LR, ADAM_B1, ADAM_B2, ADAM_EPS, ADAM_WD, ADAM_STEP = 0.001, 0.9, 0.999, 1e-08, 0.01, 10

VMEM_BIG = 56 * 1024 * 1024

MESH = pl.DeviceIdType.MESH


def _params(vmem=None):
    return pltpu.CompilerParams(vmem_limit_bytes=vmem) if vmem else None


def _dot(a, b, nt=False):
    return lax.dot_general(a, b, (((1,), (1 if nt else 0,)), ((), ())), preferred_element_type=F32)


def _sig(x):
    return 1.0 / (1.0 + jnp.exp(-x))


def _bf(x):
    return x.astype(BF16)


def _stack_rows(rows):
    idx = lax.broadcasted_iota(jnp.int32, (len(rows), rows[0].shape[1]), 0)
    out = jnp.zeros((len(rows), rows[0].shape[1]), F32)
    for r, v in enumerate(rows):
        out = jnp.where(idx == r, v, out)
    return out


def rms_fwd(x, g, out_dtype, name):
    n, w = x.shape

    def body(x_ref, g_ref, o_ref):
        xv = x_ref[...]
        r = lax.rsqrt(jnp.mean(xv * xv, axis=-1, keepdims=True) + RMS_EPS)
        o_ref[...] = (xv * r * g_ref[...]).astype(o_ref.dtype)

    return pl.pallas_call(
        body, name=name, out_shape=jax.ShapeDtypeStruct((n, w), out_dtype), grid=(n // T,),
        in_specs=[pl.BlockSpec((T, w), lambda i: (i, 0)), pl.BlockSpec((1, w), lambda i: (0, 0))],
        out_specs=pl.BlockSpec((T, w), lambda i: (i, 0)))(x, g)


def rms_bwd(x, g, dy, dres, name):
    n, w = x.shape
    has_res = dres is not None

    def body(x_ref, g_ref, dy_ref, *rest):
        if has_res:
            dres_ref, dx_ref, dxm_ref, dg_ref = rest
        else:
            dx_ref, dxm_ref, dg_ref = rest
        i = pl.program_id(0)
        xv = x_ref[...]
        r = lax.rsqrt(jnp.mean(xv * xv, axis=-1, keepdims=True) + RMS_EPS)
        xh = xv * r
        dyv = dy_ref[...]
        dxh = dyv * g_ref[...]
        dx = r * (dxh - xh * jnp.mean(dxh * xh, axis=-1, keepdims=True))
        if has_res:
            dx = dx + dres_ref[...]
        dx_ref[...] = dx
        rows = i * T + lax.broadcasted_iota(jnp.int32, (T, 1), 0)
        dxm_ref[...] = jnp.where(rows >= PAD, dx, 0.0).astype(BF16)
        part = jnp.sum(dyv * xh, axis=0, keepdims=True)

        @pl.when(i == 0)
        def _():
            dg_ref[...] = part

        @pl.when(i > 0)
        def _():
            dg_ref[...] += part

    row = pl.BlockSpec((T, w), lambda i: (i, 0))
    vec = pl.BlockSpec((1, w), lambda i: (0, 0))
    ins = [x, g, dy] + ([dres] if has_res else [])
    return pl.pallas_call(
        body, name=name,
        out_shape=(jax.ShapeDtypeStruct((n, w), F32), jax.ShapeDtypeStruct((n, w), BF16),
                   jax.ShapeDtypeStruct((1, w), F32)),
        grid=(n // T,),
        in_specs=[row, vec, row] + ([row] if has_res else []),
        out_specs=(row, row, vec))(*ins)


def loss_head(h, target):
    def body(h_ref, t_ref, dh_ref, dhm_ref, loss_ref):
        n = pl.program_id(0)
        e = jnp.where(n > 0, h_ref[...] - t_ref[...], 0.0)
        dh = e * (1.0 / D)
        dh_ref[...] = dh
        dhm_ref[...] = dh.astype(BF16)
        part = jnp.sum(jnp.sum(e * e, axis=1, keepdims=True), axis=0, keepdims=True) * (0.5 / D)

        @pl.when(n == 0)
        def _():
            loss_ref[...] = jnp.zeros_like(loss_ref)

        @pl.when(n > 0)
        def _():
            loss_ref[...] += jnp.broadcast_to(part, loss_ref.shape)

    row = pl.BlockSpec((T, D), lambda n: (n, 0))
    return pl.pallas_call(
        body, name="loss_head",
        out_shape=(jax.ShapeDtypeStruct((L, D), F32), jax.ShapeDtypeStruct((L, D), BF16),
                   jax.ShapeDtypeStruct((8, 128), F32)),
        grid=(NB,),
        in_specs=[row, pl.BlockSpec((T, D), lambda n: (jnp.maximum(n - 1, 0), 0))],
        out_specs=(row, row, pl.BlockSpec((8, 128), lambda n: (0, 0))))(h, target)


def _mm(name, a, b, *, grid, a_spec, b_spec, o_spec, out_shape, nt=False, red=False, res=None, res_spec=None):
    def body(a_ref, b_ref, *rest):
        o_ref = rest[-1]
        av = a_ref[...]
        bv = b_ref[...]
        if bv.ndim == 3:
            bv = bv.reshape(bv.shape[0] * bv.shape[1], bv.shape[2])
        acc = _dot(av, bv, nt)
        if red:
            k = pl.program_id(0)

            @pl.when(k == 0)
            def _():
                o_ref[...] = acc

            @pl.when(k > 0)
            def _():
                o_ref[...] += acc
        else:
            if res is not None:
                rows = lax.broadcasted_iota(jnp.int32, (acc.shape[0], 1), 0)
                acc = rest[0][...] + jnp.where(rows >= PAD, acc, 0.0)
            o_ref[...] = acc.astype(o_ref.dtype)

    ins = [a, b] + ([res] if res is not None else [])
    specs = [a_spec, b_spec] + ([res_spec] if res is not None else [])
    return pl.pallas_call(body, name=name, out_shape=out_shape, grid=grid, in_specs=specs, out_specs=o_spec,
                          compiler_params=_params(VMEM_BIG))(*ins)


def mm_nn(name, a, b, tn, out_dtype=F32, res=None):
    m, k = a.shape
    n = b.shape[1]
    return _mm(name, a, b, grid=(n // tn,),
               a_spec=pl.BlockSpec((m, k), lambda j: (0, 0)), b_spec=pl.BlockSpec((k, tn), lambda j: (0, j)),
               o_spec=pl.BlockSpec((m, tn), lambda j: (0, j)), out_shape=jax.ShapeDtypeStruct((m, n), out_dtype),
               res=res, res_spec=pl.BlockSpec((m, tn), lambda j: (0, j)))


def mm_nt(name, a, b, tn):
    m, k = a.shape
    n = b.shape[0]
    return _mm(name, a, b, grid=(n // tn,), nt=True,
               a_spec=pl.BlockSpec((m, k), lambda j: (0, 0)), b_spec=pl.BlockSpec((tn, k), lambda j: (j, 0)),
               o_spec=pl.BlockSpec((m, tn), lambda j: (0, j)), out_shape=jax.ShapeDtypeStruct((m, n), F32))


def ffn_up_fwd(name, u2, wup, l):
    return _mm(name, u2, wup, grid=(NDEV,),
               a_spec=pl.BlockSpec((L, D), lambda j: (0, 0)),
               b_spec=pl.BlockSpec((None, None, D, UPW), lambda j: (j, l, 0, 0)),
               o_spec=pl.BlockSpec((None, L, UPW), lambda j: (j, 0, 0)),
               out_shape=jax.ShapeDtypeStruct((NDEV, L, UPW), F32))


def ffn_up_dx(name, df, wup, l):
    return _mm(name, df, wup, grid=(NDEV,), nt=True, red=True,
               a_spec=pl.BlockSpec((None, L, UPW), lambda j: (j, 0, 0)),
               b_spec=pl.BlockSpec((None, None, D, UPW), lambda j: (j, l, 0, 0)),
               o_spec=pl.BlockSpec((L, D), lambda j: (0, 0)),
               out_shape=jax.ShapeDtypeStruct((L, D), F32))


def ffn_up_dw(name, u2t, df):
    return _mm(name, u2t, df, grid=(NDEV,),
               a_spec=pl.BlockSpec((D, L), lambda j: (0, 0)),
               b_spec=pl.BlockSpec((None, L, UPW), lambda j: (j, 0, 0)),
               o_spec=pl.BlockSpec((None, D, UPW), lambda j: (j, 0, 0)),
               out_shape=jax.ShapeDtypeStruct((NDEV, D, UPW), F32))


def ffn_down_fwd(name, a, wdn, l, res, tn=256):
    def body(a_ref, b_ref, r_ref, o_ref):
        acc = jnp.zeros((L, tn), F32)
        for g in range(NPAIR):
            bv = b_ref[2 * g:2 * g + 2]
            acc = acc + _dot(a_ref[g], bv.reshape(2 * DNR, tn))
        rows = lax.broadcasted_iota(jnp.int32, (L, 1), 0)
        o_ref[...] = r_ref[...] + jnp.where(rows >= PAD, acc, 0.0)

    return pl.pallas_call(
        body, name=name, out_shape=jax.ShapeDtypeStruct((L, D), F32), grid=(D // tn,),
        in_specs=[pl.BlockSpec((NPAIR, L, UPW), lambda j: (0, 0, 0)),
                  pl.BlockSpec((NDEV, None, DNR, tn), lambda j: (0, l, 0, j)),
                  pl.BlockSpec((L, tn), lambda j: (0, j))],
        out_specs=pl.BlockSpec((L, tn), lambda j: (0, j)),
        compiler_params=_params(VMEM_BIG))(a, wdn, res)


def ffn_down_dx(name, dh, wdn, l):
    return _mm(name, dh, wdn, grid=(NPAIR,), nt=True,
               a_spec=pl.BlockSpec((L, D), lambda g: (0, 0)),
               b_spec=pl.BlockSpec((2, None, DNR, D), lambda g: (g, l, 0, 0)),
               o_spec=pl.BlockSpec((None, L, UPW), lambda g: (g, 0, 0)),
               out_shape=jax.ShapeDtypeStruct((NPAIR, L, UPW), F32))


def ffn_down_dw(name, at, dh):
    return _mm(name, at, dh, grid=(NPAIR,),
               a_spec=pl.BlockSpec((None, UPW, L), lambda g: (g, 0, 0)),
               b_spec=pl.BlockSpec((L, D), lambda g: (0, 0)),
               o_spec=pl.BlockSpec((UPW, D), lambda g: (g, 0)),
               out_shape=jax.ShapeDtypeStruct((D_FF, D), F32))


_SLOPES = [2.0 ** (-8.0 * (h + 1) / NQ) for h in range(NQ)]
_SCALE = HD ** -0.5


def _attn_geometry(n):
    i = lax.broadcasted_iota(jnp.int32, (T, T), 0)
    j = lax.broadcasted_iota(jnp.int32, (T, T), 1)
    d_meta = n * T + i - j
    ok_meta = (j >= PAD) & (d_meta >= 0)
    dist_meta = jnp.minimum(d_meta, T).astype(F32)
    ok_prev = j > i + jnp.where(n >= 2, 0, T)
    dist_prev = (T + i - j).astype(F32)
    ok_cur = j <= i - jnp.where(n >= 1, 0, T)
    dist_cur = (i - j).astype(F32)
    return (ok_meta, dist_meta), (ok_prev, dist_prev), (ok_cur, dist_cur)


def _rms_rows(x, g):
    r = lax.rsqrt(jnp.mean(x * x, axis=-1, keepdims=True) + RMS_EPS)
    xh = x * r
    return xh * g, xh, r


def _rms_rows_bwd(dy, g, xh, r):
    dxh = dy * g
    return r * (dxh - xh * jnp.mean(dxh * xh, axis=-1, keepdims=True))


def _attn_specs():
    qspec = pl.BlockSpec((GQA, T, HD), lambda g, n: (g, n, 0))
    cur = pl.BlockSpec((None, T, HD), lambda g, n: (g, n, 0))
    prev = pl.BlockSpec((None, T, HD), lambda g, n: (g, jnp.maximum(n - 1, 0), 0))
    meta = pl.BlockSpec((None, T, HD), lambda g, n: (g, 0, 0))
    vec = pl.BlockSpec((1, HD), lambda g, n: (0, 0))
    return qspec, cur, prev, meta, vec


def _attn_probs(qn, keys, geo, slope, sink):
    logits = []
    for kb, (ok, dist) in zip(keys, geo):
        s = _dot(qn, kb, nt=True) * _SCALE - slope * dist
        logits.append(jnp.where(ok, s, NEG))
    m = jnp.maximum(jnp.maximum(jnp.max(logits[0], axis=-1, keepdims=True),
                                jnp.max(logits[1], axis=-1, keepdims=True)),
                    jnp.maximum(jnp.max(logits[2], axis=-1, keepdims=True), sink))
    es = [jnp.exp(s - m) for s in logits]
    e_sink = jnp.exp(sink - m)
    z = es[0].sum(axis=-1, keepdims=True) + es[1].sum(axis=-1, keepdims=True) + es[2].sum(axis=-1, keepdims=True) + e_sink
    inv = 1.0 / z
    return es, inv, e_sink * inv


def attn_fwd(q, k, v, qg, kg, sinks, name):
    def body(sink_ref, q_ref, kc_ref, kp_ref, km_ref, vc_ref, vp_ref, vm_ref, qg_ref, kg_ref, o_ref):
        g = pl.program_id(0)
        n = pl.program_id(1)
        geo = _attn_geometry(n)
        kgv = kg_ref[...]
        keys = [_bf(_rms_rows(r[...], kgv)[0]) for r in (km_ref, kp_ref, kc_ref)]
        vals = [_bf(r[...]) for r in (vm_ref, vp_ref, vc_ref)]
        for hh in range(GQA):
            slope = jnp.where(g == 0, _SLOPES[hh], _SLOPES[GQA + hh])
            sink = sink_ref[g * GQA + hh]
            qn = _bf(_rms_rows(q_ref[hh], qg_ref[...])[0])
            es, inv, _ = _attn_probs(qn, keys, geo, slope, sink)
            o = _dot(_bf(es[0]), vals[0]) + _dot(_bf(es[1]), vals[1]) + _dot(_bf(es[2]), vals[2])
            o_ref[hh] = o * inv

    qspec, cur, prev, meta, vec = _attn_specs()
    return pl.pallas_call(
        body, name=name, out_shape=jax.ShapeDtypeStruct((NQ, L, HD), F32), grid=(NKV, NB),
        in_specs=[pl.BlockSpec(memory_space=pltpu.SMEM), qspec, cur, prev, meta, cur, prev, meta, vec, vec],
        out_specs=qspec)(sinks, q, k, k, k, v, v, v, qg, kg)


def attn_bwd(q, k, v, do, qg, kg, sinks, name):
    def body(sink_ref, q_ref, kc_ref, kp_ref, km_ref, vc_ref, vp_ref, vm_ref, do_ref, qg_ref, kg_ref,
             dq_ref, dkc_ref, dkp_ref, dkm_ref, dvc_ref, dvp_ref, dvm_ref, dqg_ref, dkg_ref, dsk_ref):
        g = pl.program_id(0)
        n = pl.program_id(1)
        geo = _attn_geometry(n)
        kgv = kg_ref[...]
        qgv = qg_ref[...]
        knorm = [_rms_rows(r[...], kgv) for r in (km_ref, kp_ref, kc_ref)]
        keys = [_bf(t[0]) for t in knorm]
        vals = [_bf(r[...]) for r in (vm_ref, vp_ref, vc_ref)]
        dkn = [jnp.zeros((T, HD), F32) for _ in range(3)]
        dvv = [jnp.zeros((T, HD), F32) for _ in range(3)]
        dqg_acc = jnp.zeros((1, HD), F32)
        dsk_rows = []
        for hh in range(GQA):
            slope = jnp.where(g == 0, _SLOPES[hh], _SLOPES[GQA + hh])
            sink = sink_ref[g * GQA + hh]
            qn_f, qh, qr = _rms_rows(q_ref[hh], qgv)
            qn = _bf(qn_f)
            es, inv, p_sink = _attn_probs(qn, keys, geo, slope, sink)
            dob = _bf(do_ref[hh])
            ps = [e * inv for e in es]
            dps = [_dot(dob, vb, nt=True) for vb in vals]
            delta = ((ps[0] * dps[0]).sum(axis=-1, keepdims=True) + (ps[1] * dps[1]).sum(axis=-1, keepdims=True)
                     + (ps[2] * dps[2]).sum(axis=-1, keepdims=True))
            dsk_rows.append(jnp.broadcast_to(-jnp.sum(p_sink * delta, axis=0, keepdims=True), (1, 128)))
            dqn = jnp.zeros((T, HD), F32)
            for b in range(3):
                ds = ps[b] * (dps[b] - delta)
                dsb = _bf(ds)
                dqn = dqn + _dot(dsb, keys[b])
                dkn[b] = dkn[b] + _dot(_bf(ds.T), qn)
                dvv[b] = dvv[b] + _dot(_bf(ps[b].T), dob)
            dqn = dqn * _SCALE
            dq_ref[hh] = _rms_rows_bwd(dqn, qgv, qh, qr)
            dqg_acc = dqg_acc + jnp.sum(dqn * qh, axis=0, keepdims=True)
        dkg_acc = jnp.zeros((1, HD), F32)
        dk_raw = []
        for b in range(3):
            dk_b = dkn[b] * _SCALE
            _, kh, kr = knorm[b]
            dk_raw.append(_rms_rows_bwd(dk_b, kgv, kh, kr))
            dkg_acc = dkg_acc + jnp.sum(dk_b * kh, axis=0, keepdims=True)
        dkp_ref[...] = dk_raw[1]
        dkc_ref[...] = dk_raw[2]
        dvp_ref[...] = dvv[1]
        dvc_ref[...] = dvv[2]
        dsk = _stack_rows(dsk_rows)
        first = jnp.logical_and(g == 0, n == 0)

        @pl.when(n == 0)
        def _():
            dkm_ref[...] = dk_raw[0]
            dvm_ref[...] = dvv[0]
            dsk_ref[...] = dsk

        @pl.when(n > 0)
        def _():
            dkm_ref[...] += dk_raw[0]
            dvm_ref[...] += dvv[0]
            dsk_ref[...] += dsk

        @pl.when(first)
        def _():
            dqg_ref[...] = dqg_acc
            dkg_ref[...] = dkg_acc

        @pl.when(jnp.logical_not(first))
        def _():
            dqg_ref[...] += dqg_acc
            dkg_ref[...] += dkg_acc

    qspec, cur, prev, meta, vec = _attn_specs()
    kv_shape = jax.ShapeDtypeStruct((NKV, L, HD), F32)
    meta_shape = jax.ShapeDtypeStruct((NKV, T, HD), F32)
    vec_shape = jax.ShapeDtypeStruct((1, HD), F32)
    return pl.pallas_call(
        body, name=name,
        out_shape=(jax.ShapeDtypeStruct((NQ, L, HD), F32), kv_shape, kv_shape, meta_shape, kv_shape, kv_shape,
                   meta_shape, vec_shape, vec_shape, jax.ShapeDtypeStruct((NKV, GQA, 128), F32)),
        grid=(NKV, NB),
        in_specs=[pl.BlockSpec(memory_space=pltpu.SMEM), qspec, cur, prev, meta, cur, prev, meta, qspec, vec, vec],
        out_specs=(qspec, cur, cur, meta, cur, cur, meta, vec, vec,
                   pl.BlockSpec((None, GQA, 128), lambda g, n: (g, 0, 0))))(sinks, q, k, k, k, v, v, v, do, qg, kg)


def _ret_tables():
    h = np.arange(RH, dtype=np.float64)
    lg = np.log1p(-np.exp2(-5.0 - h))
    idx = np.arange(T, dtype=np.float64)
    diff = idx[:, None] - idx[None, :]
    decay = np.where(diff[None] >= 0, np.exp(np.maximum(diff, 0.0)[None] * lg[:, None, None]), 0.0)
    zeta = np.exp((T - 1 - idx)[None, :] * lg[:, None])
    xi = np.exp((idx + 1.0)[None, :] * lg[:, None])
    cd = np.exp(T * lg)
    f = lambda a: jnp.asarray(a, F32)
    return dict(decay=f(decay), zeta_row=f(zeta[:, None, :]), zeta_col=f(zeta[:, :, None]),
                xi_row=f(xi[:, None, :]), xi_col=f(xi[:, :, None]),
                cd=f(np.broadcast_to(cd[:, None, None], (RH, 1, HD))))


def _gn_rows(y):
    mu = jnp.mean(y, axis=-1, keepdims=True)
    yc = y - mu
    rstd = lax.rsqrt(jnp.mean(yc * yc, axis=-1, keepdims=True) + LN_EPS)
    return yc * rstd, rstd


def ret_fwd(rq, rkt, rv, rg, gng, tb, name):
    def body(q_ref, kt_ref, v_ref, g_ref, dec_ref, zr_ref, xc_ref, cd_ref, gng_ref, y_ref, st_ref, s_scr):
        n = pl.program_id(1)

        @pl.when(n == 0)
        def _():
            s_scr[...] = jnp.zeros_like(s_scr)

        s = s_scr[...]
        st_ref[...] = s
        q = q_ref[...]
        vb = _bf(v_ref[...])
        cols = n * T + lax.broadcasted_iota(jnp.int32, (1, T), 1)
        kft = jnp.where(cols >= PAD, kt_ref[...] * (HD ** -0.5), 0.0)
        a = _dot(_bf(q), _bf(kft)) * dec_ref[...]
        y = _dot(_bf(a), vb) + _dot(_bf(q * xc_ref[...]), _bf(s))
        s_scr[...] = cd_ref[...] * s + _dot(_bf(kft * zr_ref[...]), vb)
        yh, _ = _gn_rows(y)
        gv = g_ref[...]
        y_ref[...] = gv * _sig(gv) * (yh * gng_ref[...])

    hm = pl.BlockSpec((None, T, HD), lambda h, n: (h, n, 0))
    hmt = pl.BlockSpec((None, HD, T), lambda h, n: (h, 0, n))
    per_h = lambda *s: pl.BlockSpec((None,) + s, lambda h, n: (h, 0, 0))
    return pl.pallas_call(
        body, name=name,
        out_shape=(jax.ShapeDtypeStruct((RH, L, HD), F32), jax.ShapeDtypeStruct((RH, NB, HD, HD), F32)),
        grid=(RH, NB),
        in_specs=[hm, hmt, hm, hm, per_h(T, T), per_h(1, T), per_h(T, 1), per_h(1, HD), per_h(1, HD)],
        out_specs=(hm, pl.BlockSpec((None, None, HD, HD), lambda h, n: (h, n, 0, 0))),
        scratch_shapes=[pltpu.VMEM((HD, HD), F32)])(
            rq, rkt, rv, rg, tb["decay"], tb["zeta_row"], tb["xi_col"], tb["cd"], gng)


def ret_bwd(rq, rqt, rk, rkt, rv, rg, dout, states, gng, tb, name):
    def body(q_ref, qt_ref, k_ref, kt_ref, v_ref, g_ref, do_ref, st_ref, dec_ref, zc_ref, xr_ref, xc_ref, cd_ref,
             gng_ref, dq_ref, dk_ref, dv_ref, dg_ref, dgn_ref, ds_scr):
        i = pl.program_id(1)
        n = NB - 1 - i

        @pl.when(i == 0)
        def _():
            ds_scr[...] = jnp.zeros_like(ds_scr)

        dsn = _bf(ds_scr[...])
        sb = _bf(st_ref[...])
        rows = n * T + lax.broadcasted_iota(jnp.int32, (T, 1), 0)
        cols = n * T + lax.broadcasted_iota(jnp.int32, (1, T), 1)
        q = q_ref[...]
        qb = _bf(q)
        v = v_ref[...]
        vb = _bf(v)
        kf = jnp.where(rows >= PAD, k_ref[...] * (HD ** -0.5), 0.0)
        kft = jnp.where(cols >= PAD, kt_ref[...] * (HD ** -0.5), 0.0)
        dec = dec_ref[...]
        xc = xc_ref[...]
        zc = zc_ref[...]
        a = _dot(qb, _bf(kft)) * dec
        y = _dot(_bf(a), vb) + _dot(_bf(q * xc), sb)
        yh, rstd = _gn_rows(y)
        gv = g_ref[...]
        sg = _sig(gv)
        sil = gv * sg
        gn = gng_ref[...]
        dout = do_ref[...]
        dg_ref[...] = dout * (yh * gn) * (sg * (1.0 + gv * (1.0 - sg)))
        dyh = dout * sil * gn
        part = jnp.sum(dout * sil * yh, axis=0, keepdims=True)

        @pl.when(i == 0)
        def _():
            dgn_ref[...] = part

        @pl.when(i > 0)
        def _():
            dgn_ref[...] += part

        dy = rstd * (dyh - jnp.mean(dyh, axis=-1, keepdims=True) - yh * jnp.mean(dyh * yh, axis=-1, keepdims=True))
        dyb = _bf(dy)
        da = _dot(dyb, vb, nt=True) * dec
        dv_ref[...] = _dot(_bf(a.T), dyb) + _dot(_bf(kf * zc), dsn)
        dq_ref[...] = _dot(_bf(da), _bf(kf)) + _dot(dyb, sb, nt=True) * xc
        dkf = _dot(_bf(da.T), qb) + _dot(vb, dsn, nt=True) * zc
        dk_ref[...] = jnp.where(rows >= PAD, dkf * (HD ** -0.5), 0.0)
        ds_scr[...] = cd_ref[...] * ds_scr[...] + _dot(_bf(qt_ref[...] * xr_ref[...]), dyb)

    hm = pl.BlockSpec((None, T, HD), lambda h, i: (h, NB - 1 - i, 0))
    hmt = pl.BlockSpec((None, HD, T), lambda h, i: (h, 0, NB - 1 - i))
    per_h = lambda *s: pl.BlockSpec((None,) + s, lambda h, i: (h, 0, 0))
    hl = jax.ShapeDtypeStruct((RH, L, HD), F32)
    return pl.pallas_call(
        body, name=name,
        out_shape=(hl, hl, hl, hl, jax.ShapeDtypeStruct((RH, 1, HD), F32)),
        grid=(RH, NB),
        in_specs=[hm, hmt, hm, hmt, hm, hm, hm,
                  pl.BlockSpec((None, None, HD, HD), lambda h, i: (h, NB - 1 - i, 0, 0)),
                  per_h(T, T), per_h(T, 1), per_h(1, T), per_h(T, 1), per_h(1, HD), per_h(1, HD)],
        out_specs=(hm, hm, hm, hm, per_h(1, HD)),
        scratch_shapes=[pltpu.VMEM((HD, HD), F32)])(
            rq, rqt, rk, rkt, rv, rg, dout, states, tb["decay"], tb["zeta_col"], tb["xi_row"], tb["xi_col"],
            tb["cd"], gng)


HALO = 32
TAP0 = HALO - (CONV_K - 1)


def _conv_specs():
    cur = pl.BlockSpec((T, CV), lambda c: (c, 0))
    before = pl.BlockSpec((HALO, CV), lambda c: (jnp.maximum(c * (T // HALO) - 1, 0), 0))
    after = pl.BlockSpec((HALO, CV), lambda c: (jnp.minimum((c + 1) * (T // HALO), L // HALO - 1), 0))
    full = lambda r, w: pl.BlockSpec((r, w), lambda c: (0, 0))
    return cur, before, after, full


def _conv_tile_fwd(c, ca_ref, cah_ref, cb_ref, cbh_ref, w_ref, b_ref, lg_ref, lb_ref, u_scr):
    u_scr[0:HALO, :] = jnp.where(c > 0, cah_ref[...] * _sig(cbh_ref[...]), 0.0)
    u_scr[HALO:HALO + T, :] = ca_ref[...] * _sig(cb_ref[...])
    acc = jnp.zeros((T, CV), F32)
    for k in range(CONV_K):
        acc = acc + w_ref[k:k + 1, :] * u_scr[TAP0 + k:TAP0 + k + T, :]
    xc = acc + b_ref[...]
    xh, rstd = _gn_rows(xc)
    z = xh * lg_ref[...] + lb_ref[...]
    return xh, rstd, z, _sig(z)


def conv_fwd(ca, cb, w, b, lg, lb, pw, og, name):
    def body(ca_ref, cah_ref, cb_ref, cbh_ref, w_ref, b_ref, lg_ref, lb_ref, pw_ref, og_ref, y_ref, u_scr):
        c = pl.program_id(0)
        _, _, z, sg = _conv_tile_fwd(c, ca_ref, cah_ref, cb_ref, cbh_ref, w_ref, b_ref, lg_ref, lb_ref, u_scr)
        zp = _dot(_bf(z * sg), pw_ref[...])
        y_ref[...] = _rms_rows(zp, og_ref[...])[0]

    cur, before, _, full = _conv_specs()
    vec = full(1, CV)
    return pl.pallas_call(
        body, name=name, out_shape=jax.ShapeDtypeStruct((L, CV), F32), grid=(NB,),
        in_specs=[cur, before, cur, before, full(CONV_K, CV), vec, vec, vec, full(CV, CV), vec],
        out_specs=cur,
        scratch_shapes=[pltpu.VMEM((HALO + T, CV), F32)])(ca, ca, cb, cb, w, b, lg, lb, pw, og)


def conv_bwd(ca, cb, dy, w, b, lg, lb, pw, og, name):
    def body1(ca_ref, cah_ref, cb_ref, cbh_ref, dy_ref, w_ref, b_ref, lg_ref, lb_ref, pw_ref, og_ref,
              dxc_ref, dw_ref, db_ref, dlg_ref, dlb_ref, dog_ref, dpw_ref, u_scr):
        c = pl.program_id(0)

        @pl.when(c == 0)
        def _():
            for r in (dw_ref, db_ref, dlg_ref, dlb_ref, dog_ref, dpw_ref):
                r[...] = jnp.zeros_like(r)

        xh, rstd, z, sg = _conv_tile_fwd(c, ca_ref, cah_ref, cb_ref, cbh_ref, w_ref, b_ref, lg_ref, lb_ref, u_scr)
        s = z * sg
        zp = _dot(_bf(s), pw_ref[...])
        ogv = og_ref[...]
        _, zph, r2 = _rms_rows(zp, ogv)
        dyv = dy_ref[...]
        dog_ref[...] += jnp.sum(dyv * zph, axis=0, keepdims=True)
        dzpb = _bf(_rms_rows_bwd(dyv, ogv, zph, r2))
        dpw_ref[...] += _dot(_bf(s.T), dzpb)
        dz = _dot(dzpb, pw_ref[...], nt=True) * (sg * (1.0 + z * (1.0 - sg)))
        dlg_ref[...] += jnp.sum(dz * xh, axis=0, keepdims=True)
        dlb_ref[...] += jnp.sum(dz, axis=0, keepdims=True)
        dxh = dz * lg_ref[...]
        dxc = rstd * (dxh - jnp.mean(dxh, axis=-1, keepdims=True) - xh * jnp.mean(dxh * xh, axis=-1, keepdims=True))
        db_ref[...] += jnp.sum(dxc, axis=0, keepdims=True)
        dxc_ref[...] = dxc
        for k in range(CONV_K):
            dw_ref[k:k + 1, :] += jnp.sum(dxc * u_scr[TAP0 + k:TAP0 + k + T, :], axis=0, keepdims=True)

    def body2(dx_ref, dxa_ref, ca_ref, cb_ref, w_ref, dca_ref, dcb_ref, d_scr):
        c = pl.program_id(0)
        d_scr[0:T, :] = dx_ref[...]
        d_scr[T:T + HALO, :] = jnp.where(c < NB - 1, dxa_ref[...], 0.0)
        du = jnp.zeros((T, CV), F32)
        for k in range(CONV_K):
            off = CONV_K - 1 - k
            du = du + w_ref[k:k + 1, :] * d_scr[off:off + T, :]
        sg = _sig(cb_ref[...])
        dca_ref[...] = du * sg
        dcb_ref[...] = du * ca_ref[...] * sg * (1.0 - sg)

    cur, before, after, full = _conv_specs()
    seq = jax.ShapeDtypeStruct((L, CV), F32)
    vsh = jax.ShapeDtypeStruct((1, CV), F32)
    vec = full(1, CV)
    dxc, dw, db, dlg, dlb, dog, dpw = pl.pallas_call(
        body1, name=name + "_a",
        out_shape=(seq, jax.ShapeDtypeStruct((CONV_K, CV), F32), vsh, vsh, vsh, vsh,
                   jax.ShapeDtypeStruct((CV, CV), F32)),
        grid=(NB,),
        in_specs=[cur, before, cur, before, cur, full(CONV_K, CV), vec, vec, vec, full(CV, CV), vec],
        out_specs=(cur, full(CONV_K, CV), vec, vec, vec, vec, full(CV, CV)),
        scratch_shapes=[pltpu.VMEM((HALO + T, CV), F32)])(ca, ca, cb, cb, dy, w, b, lg, lb, pw, og)
    dca, dcb = pl.pallas_call(
        body2, name=name + "_b", out_shape=(seq, seq), grid=(NB,),
        in_specs=[cur, after, cur, cur, full(CONV_K, CV)], out_specs=(cur, cur),
        scratch_shapes=[pltpu.VMEM((T + HALO, CV), F32)])(dxc, dxc, ca, cb, w)
    return dca, dcb, dw, db, dlg, dlb, dog, dpw


HR = 8


def _ffn_specs():
    cur = lambda off: pl.BlockSpec((None, T, UPW), lambda j, c: (j + off, c, 0))
    before = lambda off: pl.BlockSpec((None, HR, UPW), lambda j, c: (j + off, jnp.maximum(c * (T // HR) - 1, 0), 0))
    after = lambda off: pl.BlockSpec(
        (None, HR, UPW), lambda j, c: (j + off, jnp.minimum((c + 1) * (T // HR), L // HR - 1), 0))
    wspec = lambda off, r: pl.BlockSpec((None, r, UPW), lambda j, c: (j + off, 0, 0))
    return cur, before, after, wspec


def ffn_act_fwd(f, w, b, name):
    def body(fg_ref, fgh_ref, fu_ref, fuh_ref, wg_ref, wu_ref, bg_ref, bu_ref, a_ref, scr):
        c = pl.program_id(1)

        def conv(x_ref, xh_ref, w_ref, b_ref):
            scr[0:HR, :] = jnp.where(c > 0, xh_ref[...], 0.0)
            scr[HR:HR + T, :] = x_ref[...]
            acc = b_ref[...]
            for k in range(FFN_K):
                acc = acc + w_ref[k:k + 1, :] * scr[HR - (FFN_K - 1) + k:HR - (FFN_K - 1) + k + T, :]
            return acc

        gate = conv(fg_ref, fgh_ref, wg_ref, bg_ref)
        up = conv(fu_ref, fuh_ref, wu_ref, bu_ref)
        a_ref[...] = (gate * _sig(gate) * up).astype(BF16)

    cur, before, _, wspec = _ffn_specs()
    return pl.pallas_call(
        body, name=name, out_shape=jax.ShapeDtypeStruct((NPAIR, L, UPW), BF16), grid=(NPAIR, NB),
        in_specs=[cur(0), before(0), cur(NPAIR), before(NPAIR), wspec(0, FFN_K), wspec(NPAIR, FFN_K),
                  wspec(0, 1), wspec(NPAIR, 1)],
        out_specs=cur(0),
        scratch_shapes=[pltpu.VMEM((HR + T, UPW), F32)])(f, f, f, f, w, w, b, b)


def ffn_act_bwd(f, da, w, b, name):
    ext = T + HR

    def body(fg_ref, fgb_ref, fga_ref, fu_ref, fub_ref, fua_ref, da_ref, daa_ref, wg_ref, wu_ref, bg_ref, bu_ref,
             dfg_ref, dfu_ref, dwg_ref, dwu_ref, dbg_ref, dbu_ref, xg_scr, xu_scr, d_scr):
        c = pl.program_id(1)

        def load(x_scr, x_ref, xb_ref, xa_ref):
            x_scr[0:HR, :] = jnp.where(c > 0, xb_ref[...], 0.0)
            x_scr[HR:HR + T, :] = x_ref[...]
            x_scr[HR + T:HR + T + HR, :] = xa_ref[...]

        def conv(x_scr, w_ref, b_ref):
            acc = b_ref[...]
            for k in range(FFN_K):
                off = HR - (FFN_K - 1) + k
                acc = acc + w_ref[k:k + 1, :] * x_scr[off:off + ext, :]
            return acc

        load(xg_scr, fg_ref, fgb_ref, fga_ref)
        load(xu_scr, fu_ref, fub_ref, fua_ref)
        gate = conv(xg_scr, wg_ref, bg_ref)
        up = conv(xu_scr, wu_ref, bu_ref)
        d_scr[0:T, :] = da_ref[...]
        d_scr[T:ext, :] = jnp.where(c < NB - 1, daa_ref[...], 0.0)
        dav = d_scr[...]
        sg = _sig(gate)
        d_gate = dav * up * (sg * (1.0 + gate * (1.0 - sg)))
        d_up = dav * gate * sg

        def back(dv, x_scr, w_ref, df_ref, dw_ref, db_ref):
            d_scr[...] = dv
            acc = jnp.zeros((T, UPW), F32)
            for k in range(FFN_K):
                off = FFN_K - 1 - k
                acc = acc + w_ref[k:k + 1, :] * d_scr[off:off + T, :]
            df_ref[...] = acc.astype(BF16)
            dcur = d_scr[0:T, :]
            parts = [jnp.sum(dcur * x_scr[HR - (FFN_K - 1) + k:HR - (FFN_K - 1) + k + T, :], axis=0, keepdims=True)
                     for k in range(FFN_K)]
            dwp = _stack_rows(parts)
            dbp = jnp.sum(dcur, axis=0, keepdims=True)

            @pl.when(c == 0)
            def _():
                dw_ref[...] = dwp
                db_ref[...] = dbp

            @pl.when(c > 0)
            def _():
                dw_ref[...] += dwp
                db_ref[...] += dbp

        back(d_gate, xg_scr, wg_ref, dfg_ref, dwg_ref, dbg_ref)
        back(d_up, xu_scr, wu_ref, dfu_ref, dwu_ref, dbu_ref)

    cur, before, after, wspec = _ffn_specs()
    slab = jax.ShapeDtypeStruct((NPAIR, L, UPW), BF16)
    wsh = jax.ShapeDtypeStruct((NPAIR, FFN_K, UPW), F32)
    bsh = jax.ShapeDtypeStruct((NPAIR, 1, UPW), F32)
    return pl.pallas_call(
        body, name=name, out_shape=(slab, slab, wsh, wsh, bsh, bsh), grid=(NPAIR, NB),
        in_specs=[cur(0), before(0), after(0), cur(NPAIR), before(NPAIR), after(NPAIR), cur(0), after(0),
                  wspec(0, FFN_K), wspec(NPAIR, FFN_K), wspec(0, 1), wspec(NPAIR, 1)],
        out_specs=(cur(0), cur(0), wspec(0, FFN_K), wspec(0, FFN_K), wspec(0, 1), wspec(0, 1)),
        scratch_shapes=[pltpu.VMEM((HR + T + HR, UPW), F32), pltpu.VMEM((HR + T + HR, UPW), F32),
                        pltpu.VMEM((ext, UPW), F32)])(f, f, f, f, f, f, da, da, w, w, b, b)


def _mesh_pos():
    return lax.axis_index("x"), lax.axis_index("y"), lax.axis_index("c")


def _peer(pos, k):
    x, y, c = pos
    px = 1 - x if k & 4 else x
    py = 1 - y if k & 2 else y
    pc = 1 - c if k & 1 else c
    return (px, py, pc), 4 * px + 2 * py + pc


def all_gather(shards, name):
    nt = len(shards)

    def body(*refs):
        ins, outs = refs[:nt], refs[nt:2 * nt]
        send, recv, local = refs[2 * nt:]
        pos = _mesh_pos()
        me = 4 * pos[0] + 2 * pos[1] + pos[2]
        locals_ = [pltpu.make_async_copy(ins[t], outs[t].at[me], local.at[t]) for t in range(nt)]
        for cp in locals_:
            cp.start()
        sends = []
        for t in range(nt):
            for k in range(1, NDEV):
                peer, _ = _peer(pos, k)
                sends.append(pltpu.make_async_remote_copy(
                    src_ref=ins[t], dst_ref=outs[t].at[me], send_sem=send.at[t, k - 1], recv_sem=recv.at[t, k - 1],
                    device_id=peer, device_id_type=MESH))
        for cp in sends:
            cp.start()
        for t in range(nt):
            for k in range(1, NDEV):
                peer, pid = _peer(pos, k)
                pltpu.make_async_remote_copy(
                    src_ref=ins[t], dst_ref=outs[t].at[pid], send_sem=send.at[t, k - 1], recv_sem=recv.at[t, k - 1],
                    device_id=peer, device_id_type=MESH).wait_recv()
        for cp in sends:
            cp.wait_send()
        for cp in locals_:
            cp.wait()

    hbm = pl.BlockSpec(memory_space=pl.ANY)
    return pl.pallas_call(
        body, name=name,
        out_shape=tuple(jax.ShapeDtypeStruct((NDEV,) + s.shape, s.dtype) for s in shards),
        in_specs=[hbm] * nt, out_specs=tuple([hbm] * nt),
        scratch_shapes=[pltpu.SemaphoreType.DMA((nt, NDEV - 1)), pltpu.SemaphoreType.DMA((nt, NDEV - 1)),
                        pltpu.SemaphoreType.DMA((nt,))])(*shards)


def reduce_scatter_parts(grads, name):
    nt = len(grads)
    flat = [g for per_layer in grads for g in per_layer]
    nc = len(flat)

    def body(*refs):
        ins, outs = refs[:nc], refs[nc:nc + nt]
        send, recv, local = refs[nc + nt:]
        pos = _mesh_pos()
        me = 4 * pos[0] + 2 * pos[1] + pos[2]
        pairs = [(t * DEPTH + l, t, l) for t in range(nt) for l in range(DEPTH)]
        locals_ = [pltpu.make_async_copy(ins[i].at[me], outs[t].at[me, l], local.at[i]) for i, t, l in pairs]
        for cp in locals_:
            cp.start()
        sends = []
        for i, t, l in pairs:
            for k in range(1, NDEV):
                peer, pid = _peer(pos, k)
                sends.append(pltpu.make_async_remote_copy(
                    src_ref=ins[i].at[pid], dst_ref=outs[t].at[me, l], send_sem=send.at[i, k - 1],
                    recv_sem=recv.at[i, k - 1], device_id=peer, device_id_type=MESH))
        for cp in sends:
            cp.start()
        for i, t, l in pairs:
            for k in range(1, NDEV):
                peer, pid = _peer(pos, k)
                pltpu.make_async_remote_copy(
                    src_ref=ins[i].at[pid], dst_ref=outs[t].at[pid, l], send_sem=send.at[i, k - 1],
                    recv_sem=recv.at[i, k - 1], device_id=peer, device_id_type=MESH).wait_recv()
        for cp in sends:
            cp.wait_send()
        for cp in locals_:
            cp.wait()

    hbm = pl.BlockSpec(memory_space=pl.ANY)
    return pl.pallas_call(
        body, name=name,
        out_shape=tuple(jax.ShapeDtypeStruct((NDEV, DEPTH) + per_layer[0].shape[1:], F32) for per_layer in grads),
        in_specs=[hbm] * nc, out_specs=tuple([hbm] * nt),
        scratch_shapes=[pltpu.SemaphoreType.DMA((nc, NDEV - 1)), pltpu.SemaphoreType.DMA((nc, NDEV - 1)),
                        pltpu.SemaphoreType.DMA((nc,))])(*flat)


def all_reduce_small(part, name):
    rows = part.shape[0]

    def body(x_ref, o_ref, buf, send, recv):
        pos = _mesh_pos()
        me = 4 * pos[0] + 2 * pos[1] + pos[2]
        sends = []
        for k in range(1, NDEV):
            peer, _ = _peer(pos, k)
            sends.append(pltpu.make_async_remote_copy(
                src_ref=x_ref, dst_ref=buf.at[me], send_sem=send.at[k - 1], recv_sem=recv.at[k - 1],
                device_id=peer, device_id_type=MESH))
        for cp in sends:
            cp.start()
        buf[me] = x_ref[...]
        for k in range(1, NDEV):
            peer, pid = _peer(pos, k)
            pltpu.make_async_remote_copy(
                src_ref=x_ref, dst_ref=buf.at[pid], send_sem=send.at[k - 1], recv_sem=recv.at[k - 1],
                device_id=peer, device_id_type=MESH).wait_recv()
        for cp in sends:
            cp.wait_send()
        acc = buf[0]
        for d in range(1, NDEV):
            acc = acc + buf[d]
        o_ref[...] = acc

    return pl.pallas_call(
        body, name=name, out_shape=jax.ShapeDtypeStruct((rows, 128), F32),
        in_specs=[pl.BlockSpec(memory_space=pltpu.VMEM)], out_specs=pl.BlockSpec(memory_space=pltpu.VMEM),
        scratch_shapes=[pltpu.VMEM((NDEV, rows, 128), F32), pltpu.SemaphoreType.DMA((NDEV - 1,)),
                        pltpu.SemaphoreType.DMA((NDEV - 1,))])(part)


def adamw(parts, w, m, v, tr, name):
    ns, r, c = parts.shape

    def body(p_ref, w_ref, m_ref, v_ref, g_ref, d_ref, nm_ref, nv_ref):
        g = p_ref[0]
        for s in range(1, ns):
            g = g + p_ref[s]
        g_ref[...] = g
        nm = ADAM_B1 * m_ref[...] + (1.0 - ADAM_B1) * g
        nv = ADAM_B2 * v_ref[...] + (1.0 - ADAM_B2) * (g * g)
        nm_ref[...] = nm
        nv_ref[...] = nv
        m_hat = nm / (1.0 - ADAM_B1 ** ADAM_STEP)
        v_hat = nv / (1.0 - ADAM_B2 ** ADAM_STEP)
        d_ref[...] = -ADAM_LR * (m_hat / (jnp.sqrt(v_hat) + ADAM_EPS) + ADAM_WD * w_ref[...])

    blk = pl.BlockSpec((tr, c), lambda i: (i, 0))
    sh = jax.ShapeDtypeStruct((r, c), F32)
    return pl.pallas_call(
        body, name=name, out_shape=(sh, sh, sh, sh), grid=(r // tr,),
        in_specs=[pl.BlockSpec((ns, tr, c), lambda i: (0, i, 0)), blk, blk, blk],
        out_specs=(blk, blk, blk, blk), compiler_params=_params(VMEM_BIG))(parts, w, m, v)


_SPLITS = (NQ * HD, NKV * HD, NKV * HD, CV, CV, RH * HD, RH * HD, RH * HD, RH * HD)
_OFFS = np.concatenate([[0], np.cumsum(_SPLITS)]).tolist()


def _heads(x, nh):
    return x.reshape(L, nh, HD).transpose(1, 0, 2)


def _unheads(x):
    return x.transpose(1, 0, 2).reshape(L, x.shape[0] * HD)


def _split_proj(proj):
    return [proj[:, _OFFS[i]:_OFFS[i + 1]] for i in range(len(_SPLITS))]


def local_step(x, target, P):
    tb = _ret_tables()
    h = jnp.concatenate([jnp.zeros((PAD, D), F32), P["meta"], x], axis=0)
    stash = []
    for l in range(DEPTH):
        s = {"h": h}
        u = rms_fwd(h, P["norm_mix_g"][l], BF16, f"rms_mix_{l}")
        s["ut"] = u.T
        proj = mm_nn(f"in_proj_{l}", u, P["w_in"][l], 256)
        q, k, v, ca, cb, rq, rk, rv, rg = _split_proj(proj)
        s.update(q=_heads(q, NQ), k=_heads(k, NKV), v=_heads(v, NKV), ca=ca, cb=cb, rq=_heads(rq, RH),
                 rk=_heads(rk, RH), rv=_heads(rv, RH), rg=_heads(rg, RH))
        s["rkt"] = s["rk"].transpose(0, 2, 1)
        s["rqt"] = s["rq"].transpose(0, 2, 1)
        o = attn_fwd(s["q"], s["k"], s["v"], P["q_norm_g"][l], P["k_norm_g"][l], P["attn_sinks"][l], f"attn_fwd_{l}")
        s["o"] = _unheads(o)
        y_att = rms_fwd(s["o"], P["attn_out_g"][l], BF16, f"rms_att_{l}")
        y_cv = conv_fwd(ca, cb, P["cv_dw_w"][l], P["cv_dw_b"][l], P["cv_ln_g"][l], P["cv_ln_b"][l], P["cv_pw"][l],
                        P["cv_out_g"][l], f"conv_fwd_{l}")
        y_ret, s["states"] = ret_fwd(s["rq"], s["rkt"], s["rv"], s["rg"], P["ret_gn_g"][l], tb, f"ret_fwd_{l}")
        ycat = jnp.concatenate([y_att, y_cv.astype(BF16), _unheads(y_ret).astype(BF16)], axis=1)
        s["ycat_t"] = ycat.T
        h = mm_nn(f"out_proj_{l}", ycat, P["w_out"][l], 256, res=h)
        s["h1"] = h
        u2 = rms_fwd(h, P["norm_ffn_g"][l], BF16, f"rms_ffn_{l}")
        s["u2t"] = u2.T
        s["f"] = ffn_up_fwd(f"ffn_up_{l}", u2, P["ffn_up"], l)
        a = ffn_act_fwd(s["f"], P["ffn_dw_w"][l], P["ffn_dw_b"][l], f"ffn_act_fwd_{l}")
        s["at"] = a.transpose(0, 2, 1)
        h = ffn_down_fwd(f"ffn_down_{l}", a, P["ffn_down"], l, h)
        stash.append(s)

    dh, dhm, loss = loss_head(h, target)
    G = {n: [None] * DEPTH for n in ("norm_mix_g", "w_in", "q_norm_g", "k_norm_g", "attn_sinks", "attn_out_g",
                                     "cv_dw_w", "cv_dw_b", "cv_ln_g", "cv_ln_b", "cv_pw", "cv_out_g", "ret_gn_g",
                                     "w_out", "norm_ffn_g", "ffn_up", "ffn_dw_w", "ffn_dw_b", "ffn_down")}
    for l in reversed(range(DEPTH)):
        s = stash[l]
        da = ffn_down_dx(f"ffn_down_dx_{l}", dhm, P["ffn_down"], l)
        G["ffn_down"][l] = ffn_down_dw(f"ffn_down_dw_{l}", s["at"], dhm)
        dfg, dfu, dwg, dwu, dbg, dbu = ffn_act_bwd(s["f"], da, P["ffn_dw_w"][l], P["ffn_dw_b"][l], f"ffn_act_bwd_{l}")
        df = jnp.concatenate([dfg, dfu], axis=0)
        G["ffn_dw_w"][l] = jnp.concatenate([dwg, dwu], axis=0)
        G["ffn_dw_b"][l] = jnp.concatenate([dbg, dbu], axis=0)
        du2 = ffn_up_dx(f"ffn_up_dx_{l}", df, P["ffn_up"], l)
        G["ffn_up"][l] = ffn_up_dw(f"ffn_up_dw_{l}", s["u2t"], df)
        dh, dhm, G["norm_ffn_g"][l] = rms_bwd(s["h1"], P["norm_ffn_g"][l], du2, dh, f"rms_ffn_bwd_{l}")
        dycat = mm_nt(f"out_proj_dx_{l}", dhm, P["w_out"][l], 256)
        G["w_out"][l] = mm_nn(f"out_proj_dw_{l}", s["ycat_t"], dhm, 256)
        dy_att, dy_cv, dy_ret = dycat[:, :NQ * HD], dycat[:, NQ * HD:NQ * HD + CV], dycat[:, NQ * HD + CV:]
        do, _, G["attn_out_g"][l] = rms_bwd(s["o"], P["attn_out_g"][l], dy_att, None, f"rms_att_bwd_{l}")
        (dq, dk_c, dk_p, dk_m, dv_c, dv_p, dv_m, G["q_norm_g"][l], G["k_norm_g"][l], dsk) = attn_bwd(
            s["q"], s["k"], s["v"], _heads(do, NQ), P["q_norm_g"][l], P["k_norm_g"][l], P["attn_sinks"][l],
            f"attn_bwd_{l}")
        G["attn_sinks"][l] = dsk[:, :, 0].reshape(NQ)
        shift = lambda z: jnp.concatenate([z[:, T:], jnp.zeros((NKV, T, HD), F32)], axis=1)
        dk = (dk_c + shift(dk_p)).at[:, :T].add(dk_m)
        dv = (dv_c + shift(dv_p)).at[:, :T].add(dv_m)
        (dca, dcb, G["cv_dw_w"][l], G["cv_dw_b"][l], G["cv_ln_g"][l], G["cv_ln_b"][l], G["cv_out_g"][l],
         G["cv_pw"][l]) = conv_bwd(s["ca"], s["cb"], dy_cv, P["cv_dw_w"][l], P["cv_dw_b"][l], P["cv_ln_g"][l],
                                   P["cv_ln_b"][l], P["cv_pw"][l], P["cv_out_g"][l], f"conv_bwd_{l}")
        drq, drk, drv, drg, G["ret_gn_g"][l] = ret_bwd(
            s["rq"], s["rqt"], s["rk"], s["rkt"], s["rv"], s["rg"], _heads(dy_ret, RH), s["states"],
            P["ret_gn_g"][l], tb, f"ret_bwd_{l}")
        dproj = jnp.concatenate([_unheads(dq), _unheads(dk), _unheads(dv), dca, dcb, _unheads(drq), _unheads(drk),
                                 _unheads(drv), _unheads(drg)], axis=1).astype(BF16)
        du = mm_nt(f"in_proj_dx_{l}", dproj, P["w_in"][l], 256)
        G["w_in"][l] = mm_nn(f"in_proj_dw_{l}", s["ut"], dproj, 256)
        dh, dhm, G["norm_mix_g"][l] = rms_bwd(s["h"], P["norm_mix_g"][l], du, dh, f"rms_mix_bwd_{l}")
    return loss[0, 0], dh[T:], dh[PAD:T], G


_SMALL = ("meta", "norm_mix_g", "q_norm_g", "k_norm_g", "attn_sinks", "attn_out_g", "cv_dw_w", "cv_dw_b", "cv_ln_g",
          "cv_ln_b", "cv_out_g", "ret_gn_g", "norm_ffn_g", "ffn_dw_w", "ffn_dw_b")
_BIG = ("w_in", "cv_pw", "w_out", "ffn_up", "ffn_down")
_ORDER = ("meta", "norm_mix_g", "w_in", "q_norm_g", "k_norm_g", "attn_sinks", "attn_out_g", "cv_dw_w", "cv_dw_b",
          "cv_ln_g", "cv_ln_b", "cv_pw", "cv_out_g", "ret_gn_g", "w_out", "norm_ffn_g", "ffn_up", "ffn_dw_w",
          "ffn_dw_b", "ffn_down")
_SMALL_SHARDED = {"meta": D, "cv_dw_w": CV, "ffn_dw_w": 2 * D_FF}


def _pack(arrs):
    flat = jnp.concatenate([a.reshape(-1) for a in arrs])
    n = flat.shape[0]
    rows = -(-n // 1024) * 8
    return jnp.pad(flat, (0, rows * 128 - n)).reshape(rows, 128)


def _unpack(packed, shapes):
    flat = packed.reshape(-1)
    out, off = [], 0
    for s in shapes:
        n = int(np.prod(s))
        out.append(flat[off:off + n].reshape(s))
        off += n
    return out


def kernel(x, meta, norm_mix_g, w_in, q_norm_g, k_norm_g, attn_sinks, attn_out_g, cv_dw_w, cv_dw_b, cv_ln_g, cv_ln_b, cv_pw, cv_out_g, ret_gn_g, w_out, norm_ffn_g, ffn_up, ffn_dw_w, ffn_dw_b, ffn_down, loss_target, m_meta, m_norm_mix_g, m_w_in, m_q_norm_g, m_k_norm_g, m_attn_sinks, m_attn_out_g, m_cv_dw_w, m_cv_dw_b, m_cv_ln_g, m_cv_ln_b, m_cv_pw, m_cv_out_g, m_ret_gn_g, m_w_out, m_norm_ffn_g, m_ffn_up, m_ffn_dw_w, m_ffn_dw_b, m_ffn_down, v_meta, v_norm_mix_g, v_w_in, v_q_norm_g, v_k_norm_g, v_attn_sinks, v_attn_out_g, v_cv_dw_w, v_cv_dw_b, v_cv_ln_g, v_cv_ln_b, v_cv_pw, v_cv_out_g, v_ret_gn_g, v_w_out, v_norm_ffn_g, v_ffn_up, v_ffn_dw_w, v_ffn_dw_b, v_ffn_down):
    W = dict(meta=meta, norm_mix_g=norm_mix_g, w_in=w_in, q_norm_g=q_norm_g, k_norm_g=k_norm_g,
             attn_sinks=attn_sinks, attn_out_g=attn_out_g, cv_dw_w=cv_dw_w, cv_dw_b=cv_dw_b, cv_ln_g=cv_ln_g,
             cv_ln_b=cv_ln_b, cv_pw=cv_pw, cv_out_g=cv_out_g, ret_gn_g=ret_gn_g, w_out=w_out,
             norm_ffn_g=norm_ffn_g, ffn_up=ffn_up, ffn_dw_w=ffn_dw_w, ffn_dw_b=ffn_dw_b, ffn_down=ffn_down)
    M = dict(meta=m_meta, norm_mix_g=m_norm_mix_g, w_in=m_w_in, q_norm_g=m_q_norm_g, k_norm_g=m_k_norm_g,
             attn_sinks=m_attn_sinks, attn_out_g=m_attn_out_g, cv_dw_w=m_cv_dw_w, cv_dw_b=m_cv_dw_b,
             cv_ln_g=m_cv_ln_g, cv_ln_b=m_cv_ln_b, cv_pw=m_cv_pw, cv_out_g=m_cv_out_g, ret_gn_g=m_ret_gn_g,
             w_out=m_w_out, norm_ffn_g=m_norm_ffn_g, ffn_up=m_ffn_up, ffn_dw_w=m_ffn_dw_w, ffn_dw_b=m_ffn_dw_b,
             ffn_down=m_ffn_down)
    V = dict(meta=v_meta, norm_mix_g=v_norm_mix_g, w_in=v_w_in, q_norm_g=v_q_norm_g, k_norm_g=v_k_norm_g,
             attn_sinks=v_attn_sinks, attn_out_g=v_attn_out_g, cv_dw_w=v_cv_dw_w, cv_dw_b=v_cv_dw_b,
             cv_ln_g=v_cv_ln_g, cv_ln_b=v_cv_ln_b, cv_pw=v_cv_pw, cv_out_g=v_cv_out_g, ret_gn_g=v_ret_gn_g,
             w_out=v_w_out, norm_ffn_g=v_norm_ffn_g, ffn_up=v_ffn_up, ffn_dw_w=v_ffn_dw_w, ffn_dw_b=v_ffn_dw_b,
             ffn_down=v_ffn_down)
    me = 4 * lax.axis_index("x") + 2 * lax.axis_index("y") + lax.axis_index("c")

    g_in, g_pw, g_out, g_up, g_dn, g_meta, g_cdw, g_fdw = all_gather(
        [w_in.astype(BF16), cv_pw.astype(BF16), w_out.astype(BF16), ffn_up.astype(BF16), ffn_down.astype(BF16),
         meta, cv_dw_w, ffn_dw_w], "gather_weights")
    P = dict(
        meta=g_meta.transpose(1, 0, 2).reshape(N_META, D),
        w_in=g_in.transpose(1, 2, 0, 3).reshape(DEPTH, D, IN_W),
        cv_pw=g_pw.transpose(1, 0, 2, 3).reshape(DEPTH, CV, CV),
        w_out=g_out.transpose(1, 0, 2, 3).reshape(DEPTH, D, D),
        ffn_up=g_up, ffn_down=g_dn,
        cv_dw_w=g_cdw.transpose(1, 2, 0, 3).reshape(DEPTH, CONV_K, CV),
        ffn_dw_w=g_fdw.transpose(1, 0, 2, 3),
        ffn_dw_b=ffn_dw_b.reshape(DEPTH, NDEV, 1, UPW),
        attn_sinks=attn_sinks,
        ret_gn_g=ret_gn_g.reshape(DEPTH, RH, 1, HD),
    )
    for n in ("norm_mix_g", "q_norm_g", "k_norm_g", "attn_out_g", "cv_dw_b", "cv_ln_g", "cv_ln_b", "cv_out_g",
              "norm_ffn_g"):
        P[n] = W[n].reshape(DEPTH, 1, -1)

    loss_part, dx, dmeta, G = local_step(x[0], loss_target[0], P)
    loss = lax.psum(loss_part, ("x", "y", "c"))

    big = {
        "w_in": [g.reshape(D, NDEV, IN_W // NDEV).transpose(1, 0, 2) for g in G["w_in"]],
        "cv_pw": [g.reshape(NDEV, CV // NDEV, CV) for g in G["cv_pw"]],
        "w_out": [g.reshape(NDEV, D // NDEV, D) for g in G["w_out"]],
        "ffn_up": G["ffn_up"],
        "ffn_down": [g.reshape(NDEV, DNR, D) for g in G["ffn_down"]],
    }
    parts = dict(zip(_BIG, reduce_scatter_parts([big[n] for n in _BIG], "scatter_grads")))

    small_full = {
        "meta": dmeta,
        "cv_dw_w": jnp.stack(G["cv_dw_w"]),
        "ffn_dw_w": jnp.stack([g.transpose(1, 0, 2).reshape(FFN_K, 2 * D_FF) for g in G["ffn_dw_w"]]),
        "ffn_dw_b": jnp.stack([g.reshape(2 * D_FF) for g in G["ffn_dw_b"]]),
        "attn_sinks": jnp.stack(G["attn_sinks"]),
        "ret_gn_g": jnp.stack([g.reshape(RH * HD) for g in G["ret_gn_g"]]),
    }
    for n in _SMALL:
        if n not in small_full:
            small_full[n] = jnp.stack([g.reshape(-1) for g in G[n]])
    shapes = [small_full[n].shape for n in _SMALL]
    summed = _unpack(all_reduce_small(_pack([small_full[n] for n in _SMALL]), "reduce_small"), shapes)
    small_g = {}
    for n, g in zip(_SMALL, summed):
        if n in _SMALL_SHARDED:
            width = _SMALL_SHARDED[n] // NDEV
            g = lax.dynamic_slice_in_dim(g, me * width, width, axis=g.ndim - 1)
        small_g[n] = g

    out = {}
    tiles = {"w_in": 256, "cv_pw": 64, "w_out": 128, "ffn_up": 256, "ffn_down": 176}
    for n in _BIG:
        shard = W[n].shape
        rows, cols = shard[0] * shard[1], shard[2]
        res = adamw(parts[n].reshape(NDEV, rows, cols), W[n].reshape(rows, cols), M[n].reshape(rows, cols),
                    V[n].reshape(rows, cols), tiles[n], f"adamw_{n}")
        out[n] = [r.reshape(shard) for r in res]
    sshapes = [W[n].shape for n in _SMALL]
    packed_g = _pack([small_g[n] for n in _SMALL])
    res = adamw(packed_g[None], _pack([W[n] for n in _SMALL]), _pack([M[n] for n in _SMALL]),
                _pack([V[n] for n in _SMALL]), packed_g.shape[0], "adamw_small")
    for i, r in enumerate(res):
        for n, a in zip(_SMALL, _unpack(r, sshapes)):
            out.setdefault(n, [None] * 4)[i] = a

    return (loss, dx[None], *[out[n][0] for n in _ORDER], *[out[n][1] for n in _ORDER],
            *[out[n][2] for n in _ORDER], *[out[n][3] for n in _ORDER])
```

```python
import functools
import math

import numpy as np
import jax
import jax.numpy as jnp
from jax import lax
from jax.experimental import pallas as pl
from jax.experimental.pallas import tpu as pltpu

F32 = jnp.float32
BF16 = jnp.bfloat16

D = 1024
SEQ = 2048
DEPTH = 2
T = 128
L = SEQ + T
NB = L // T
N_META = 16
PAD = T - N_META
HD = 64
NQ = 8
NKV = 2
GQA = NQ // NKV
CV = 256
CONV_K = 31
RH = 4
D_FF = 2816
FFN_K = 3
IN_W = 2304
RMS_EPS = 1e-6
LN_EPS = 1e-5
NEG = -1e30
NDEV = 8
UPW = 2 * D_FF // NDEV
DNR = D_FF // NDEV
NPAIR = NDEV // 2

ADAM_LR, ADAM_B1, ADAM_B2, ADAM_EPS, ADAM_WD, ADAM_STEP = 0.001, 0.9, 0.999, 1e-08, 0.01, 10

VMEM_BIG = 56 * 1024 * 1024

MESH = pl.DeviceIdType.MESH


def _params(vmem=None):
    return pltpu.CompilerParams(vmem_limit_bytes=vmem) if vmem else None


def _dot(a, b, nt=False):
    return lax.dot_general(a, b, (((1,), (1 if nt else 0,)), ((), ())), preferred_element_type=F32)


def _sig(x):
    return 1.0 / (1.0 + jnp.exp(-x))


def _bf(x):
    return x.astype(BF16)


def _stack_rows(rows):
    idx = lax.broadcasted_iota(jnp.int32, (len(rows), rows[0].shape[1]), 0)
    out = jnp.zeros((len(rows), rows[0].shape[1]), F32)
    for r, v in enumerate(rows):
        out = jnp.where(idx == r, v, out)
    return out


def rms_fwd(x, g, out_dtype, name):
    n, w = x.shape

    def body(x_ref, g_ref, o_ref):
        xv = x_ref[...]
        r = lax.rsqrt(jnp.mean(xv * xv, axis=-1, keepdims=True) + RMS_EPS)
        o_ref[...] = (xv * r * g_ref[...]).astype(o_ref.dtype)

    return pl.pallas_call(
        body, name=name, out_shape=jax.ShapeDtypeStruct((n, w), out_dtype), grid=(n // T,),
        in_specs=[pl.BlockSpec((T, w), lambda i: (i, 0)), pl.BlockSpec((1, w), lambda i: (0, 0))],
        out_specs=pl.BlockSpec((T, w), lambda i: (i, 0)))(x, g)


def rms_bwd(x, g, dy, dres, name):
    n, w = x.shape
    has_res = dres is not None

    def body(x_ref, g_ref, dy_ref, *rest):
        if has_res:
            dres_ref, dx_ref, dxm_ref, dg_ref = rest
        else:
            dx_ref, dxm_ref, dg_ref = rest
        i = pl.program_id(0)
        xv = x_ref[...]
        r = lax.rsqrt(jnp.mean(xv * xv, axis=-1, keepdims=True) + RMS_EPS)
        xh = xv * r
        dyv = dy_ref[...]
        dxh = dyv * g_ref[...]
        dx = r * (dxh - xh * jnp.mean(dxh * xh, axis=-1, keepdims=True))
        if has_res:
            dx = dx + dres_ref[...]
        dx_ref[...] = dx
        rows = i * T + lax.broadcasted_iota(jnp.int32, (T, 1), 0)
        dxm_ref[...] = jnp.where(rows >= PAD, dx, 0.0).astype(BF16)
        part = jnp.sum(dyv * xh, axis=0, keepdims=True)

        @pl.when(i == 0)
        def _():
            dg_ref[...] = part

        @pl.when(i > 0)
        def _():
            dg_ref[...] += part

    row = pl.BlockSpec((T, w), lambda i: (i, 0))
    vec = pl.BlockSpec((1, w), lambda i: (0, 0))
    ins = [x, g, dy] + ([dres] if has_res else [])
    return pl.pallas_call(
        body, name=name,
        out_shape=(jax.ShapeDtypeStruct((n, w), F32), jax.ShapeDtypeStruct((n, w), BF16),
                   jax.ShapeDtypeStruct((1, w), F32)),
        grid=(n // T,),
        in_specs=[row, vec, row] + ([row] if has_res else []),
        out_specs=(row, row, vec))(*ins)


def loss_head(h, target):
    def body(h_ref, t_ref, dh_ref, dhm_ref, loss_ref):
        n = pl.program_id(0)
        e = jnp.where(n > 0, h_ref[...] - t_ref[...], 0.0)
        dh = e * (1.0 / D)
        dh_ref[...] = dh
        dhm_ref[...] = dh.astype(BF16)
        part = jnp.sum(jnp.sum(e * e, axis=1, keepdims=True), axis=0, keepdims=True) * (0.5 / D)

        @pl.when(n == 0)
        def _():
            loss_ref[...] = jnp.zeros_like(loss_ref)

        @pl.when(n > 0)
        def _():
            loss_ref[...] += jnp.broadcast_to(part, loss_ref.shape)

    row = pl.BlockSpec((T, D), lambda n: (n, 0))
    return pl.pallas_call(
        body, name="loss_head",
        out_shape=(jax.ShapeDtypeStruct((L, D), F32), jax.ShapeDtypeStruct((L, D), BF16),
                   jax.ShapeDtypeStruct((8, 128), F32)),
        grid=(NB,),
        in_specs=[row, pl.BlockSpec((T, D), lambda n: (jnp.maximum(n - 1, 0), 0))],
        out_specs=(row, row, pl.BlockSpec((8, 128), lambda n: (0, 0))))(h, target)


def _mm(name, a, b, *, grid, a_spec, b_spec, o_spec, out_shape, nt=False, red=False, res=None, res_spec=None):
    def body(a_ref, b_ref, *rest):
        o_ref = rest[-1]
        av = a_ref[...]
        bv = b_ref[...]
        if bv.ndim == 3:
            bv = bv.reshape(bv.shape[0] * bv.shape[1], bv.shape[2])
        acc = _dot(av, bv, nt)
        if red:
            k = pl.program_id(0)

            @pl.when(k == 0)
            def _():
                o_ref[...] = acc

            @pl.when(k > 0)
            def _():
                o_ref[...] += acc
        else:
            if res is not None:
                rows = lax.broadcasted_iota(jnp.int32, (acc.shape[0], 1), 0)
                acc = rest[0][...] + jnp.where(rows >= PAD, acc, 0.0)
            o_ref[...] = acc.astype(o_ref.dtype)

    ins = [a, b] + ([res] if res is not None else [])
    specs = [a_spec, b_spec] + ([res_spec] if res is not None else [])
    return pl.pallas_call(body, name=name, out_shape=out_shape, grid=grid, in_specs=specs, out_specs=o_spec,
                          compiler_params=_params(VMEM_BIG))(*ins)


def mm_nn(name, a, b, tn, out_dtype=F32, res=None):
    m, k = a.shape
    n = b.shape[1]
    return _mm(name, a, b, grid=(n // tn,),
               a_spec=pl.BlockSpec((m, k), lambda j: (0, 0)), b_spec=pl.BlockSpec((k, tn), lambda j: (0, j)),
               o_spec=pl.BlockSpec((m, tn), lambda j: (0, j)), out_shape=jax.ShapeDtypeStruct((m, n), out_dtype),
               res=res, res_spec=pl.BlockSpec((m, tn), lambda j: (0, j)))


def mm_nt(name, a, b, tn):
    m, k = a.shape
    n = b.shape[0]
    return _mm(name, a, b, grid=(n // tn,), nt=True,
               a_spec=pl.BlockSpec((m, k), lambda j: (0, 0)), b_spec=pl.BlockSpec((tn, k), lambda j: (j, 0)),
               o_spec=pl.BlockSpec((m, tn), lambda j: (0, j)), out_shape=jax.ShapeDtypeStruct((m, n), F32))


def ffn_up_fwd(name, u2, wup, l):
    return _mm(name, u2, wup, grid=(NDEV,),
               a_spec=pl.BlockSpec((L, D), lambda j: (0, 0)),
               b_spec=pl.BlockSpec((None, None, D, UPW), lambda j: (j, l, 0, 0)),
               o_spec=pl.BlockSpec((None, L, UPW), lambda j: (j, 0, 0)),
               out_shape=jax.ShapeDtypeStruct((NDEV, L, UPW), F32))


def ffn_up_dx(name, df, wup, l):
    return _mm(name, df, wup, grid=(NDEV,), nt=True, red=True,
               a_spec=pl.BlockSpec((None, L, UPW), lambda j: (j, 0, 0)),
               b_spec=pl.BlockSpec((None, None, D, UPW), lambda j: (j, l, 0, 0)),
               o_spec=pl.BlockSpec((L, D), lambda j: (0, 0)),
               out_shape=jax.ShapeDtypeStruct((L, D), F32))


def ffn_up_dw(name, u2t, df):
    return _mm(name, u2t, df, grid=(NDEV,),
               a_spec=pl.BlockSpec((D, L), lambda j: (0, 0)),
               b_spec=pl.BlockSpec((None, L, UPW), lambda j: (j, 0, 0)),
               o_spec=pl.BlockSpec((None, D, UPW), lambda j: (j, 0, 0)),
               out_shape=jax.ShapeDtypeStruct((NDEV, D, UPW), BF16))


def ffn_down_fwd(name, a, wdn, l, res, tn=256):
    def body(a_ref, b_ref, r_ref, o_ref):
        acc = jnp.zeros((L, tn), F32)
        for g in range(NPAIR):
            bv = b_ref[2 * g:2 * g + 2]
            acc = acc + _dot(a_ref[g], bv.reshape(2 * DNR, tn))
        rows = lax.broadcasted_iota(jnp.int32, (L, 1), 0)
        o_ref[...] = r_ref[...] + jnp.where(rows >= PAD, acc, 0.0)

    return pl.pallas_call(
        body, name=name, out_shape=jax.ShapeDtypeStruct((L, D), F32), grid=(D // tn,),
        in_specs=[pl.BlockSpec((NPAIR, L, UPW), lambda j: (0, 0, 0)),
                  pl.BlockSpec((NDEV, None, DNR, tn), lambda j: (0, l, 0, j)),
                  pl.BlockSpec((L, tn), lambda j: (0, j))],
        out_specs=pl.BlockSpec((L, tn), lambda j: (0, j)),
        compiler_params=_params(VMEM_BIG))(a, wdn, res)


def ffn_down_dx(name, dh, wdn, l):
    return _mm(name, dh, wdn, grid=(NPAIR,), nt=True,
               a_spec=pl.BlockSpec((L, D), lambda g: (0, 0)),
               b_spec=pl.BlockSpec((2, None, DNR, D), lambda g: (g, l, 0, 0)),
               o_spec=pl.BlockSpec((None, L, UPW), lambda g: (g, 0, 0)),
               out_shape=jax.ShapeDtypeStruct((NPAIR, L, UPW), F32))


def ffn_down_dw(name, at, dh):
    return _mm(name, at, dh, grid=(NPAIR,),
               a_spec=pl.BlockSpec((None, UPW, L), lambda g: (g, 0, 0)),
               b_spec=pl.BlockSpec((L, D), lambda g: (0, 0)),
               o_spec=pl.BlockSpec((UPW, D), lambda g: (g, 0)),
               out_shape=jax.ShapeDtypeStruct((D_FF, D), BF16))


_SLOPES = [2.0 ** (-8.0 * (h + 1) / NQ) for h in range(NQ)]
_SCALE = HD ** -0.5


def _attn_geometry(n):
    i = lax.broadcasted_iota(jnp.int32, (T, T), 0)
    j = lax.broadcasted_iota(jnp.int32, (T, T), 1)
    d_meta = n * T + i - j
    ok_meta = (j >= PAD) & (d_meta >= 0)
    dist_meta = jnp.minimum(d_meta, T).astype(F32)
    ok_prev = j > i + jnp.where(n >= 2, 0, T)
    dist_prev = (T + i - j).astype(F32)
    ok_cur = j <= i - jnp.where(n >= 1, 0, T)
    dist_cur = (i - j).astype(F32)
    return (ok_meta, dist_meta), (ok_prev, dist_prev), (ok_cur, dist_cur)


def _rms_rows(x, g):
    r = lax.rsqrt(jnp.mean(x * x, axis=-1, keepdims=True) + RMS_EPS)
    xh = x * r
    return xh * g, xh, r


def _rms_rows_bwd(dy, g, xh, r):
    dxh = dy * g
    return r * (dxh - xh * jnp.mean(dxh * xh, axis=-1, keepdims=True))


def _attn_specs():
    qspec = pl.BlockSpec((GQA, T, HD), lambda g, n: (g, n, 0))
    cur = pl.BlockSpec((None, T, HD), lambda g, n: (g, n, 0))
    prev = pl.BlockSpec((None, T, HD), lambda g, n: (g, jnp.maximum(n - 1, 0), 0))
    meta = pl.BlockSpec((None, T, HD), lambda g, n: (g, 0, 0))
    vec = pl.BlockSpec((1, HD), lambda g, n: (0, 0))
    return qspec, cur, prev, meta, vec


def _attn_probs(qn, keys, geo, slope, sink):
    logits = []
    for kb, (ok, dist) in zip(keys, geo):
        s = _dot(qn, kb, nt=True) * _SCALE - slope * dist
        logits.append(jnp.where(ok, s, NEG))
    m = jnp.maximum(jnp.maximum(jnp.max(logits[0], axis=-1, keepdims=True),
                                jnp.max(logits[1], axis=-1, keepdims=True)),
                    jnp.maximum(jnp.max(logits[2], axis=-1, keepdims=True), sink))
    es = [jnp.exp(s - m) for s in logits]
    e_sink = jnp.exp(sink - m)
    z = es[0].sum(axis=-1, keepdims=True) + es[1].sum(axis=-1, keepdims=True) + es[2].sum(axis=-1, keepdims=True) + e_sink
    inv = 1.0 / z
    return es, inv, e_sink * inv


def attn_fwd(q, k, v, qg, kg, sinks, name):
    def body(sink_ref, q_ref, kc_ref, kp_ref, km_ref, vc_ref, vp_ref, vm_ref, qg_ref, kg_ref, o_ref):
        g = pl.program_id(0)
        n = pl.program_id(1)
        geo = _attn_geometry(n)
        kgv = kg_ref[...]
        keys = [_bf(_rms_rows(r[...], kgv)[0]) for r in (km_ref, kp_ref, kc_ref)]
        vals = [_bf(r[...]) for r in (vm_ref, vp_ref, vc_ref)]
        for hh in range(GQA):
            slope = jnp.where(g == 0, _SLOPES[hh], _SLOPES[GQA + hh])
            sink = sink_ref[g * GQA + hh]
            qn = _bf(_rms_rows(q_ref[hh], qg_ref[...])[0])
            es, inv, _ = _attn_probs(qn, keys, geo, slope, sink)
            o = _dot(_bf(es[0]), vals[0]) + _dot(_bf(es[1]), vals[1]) + _dot(_bf(es[2]), vals[2])
            o_ref[hh] = o * inv

    qspec, cur, prev, meta, vec = _attn_specs()
    return pl.pallas_call(
        body, name=name, out_shape=jax.ShapeDtypeStruct((NQ, L, HD), F32), grid=(NKV, NB),
        in_specs=[pl.BlockSpec(memory_space=pltpu.SMEM), qspec, cur, prev, meta, cur, prev, meta, vec, vec],
        out_specs=qspec)(sinks, q, k, k, k, v, v, v, qg, kg)


def attn_bwd(q, k, v, do, qg, kg, sinks, name):
    def body(sink_ref, q_ref, kc_ref, kp_ref, km_ref, vc_ref, vp_ref, vm_ref, do_ref, qg_ref, kg_ref,
             dq_ref, dkc_ref, dkp_ref, dkm_ref, dvc_ref, dvp_ref, dvm_ref, dqg_ref, dkg_ref, dsk_ref):
        g = pl.program_id(0)
        n = pl.program_id(1)
        geo = _attn_geometry(n)
        kgv = kg_ref[...]
        qgv = qg_ref[...]
        knorm = [_rms_rows(r[...], kgv) for r in (km_ref, kp_ref, kc_ref)]
        keys = [_bf(t[0]) for t in knorm]
        vals = [_bf(r[...]) for r in (vm_ref, vp_ref, vc_ref)]
        dkn = [jnp.zeros((T, HD), F32) for _ in range(3)]
        dvv = [jnp.zeros((T, HD), F32) for _ in range(3)]
        dqg_acc = jnp.zeros((1, HD), F32)
        dsk_rows = []
        for hh in range(GQA):
            slope = jnp.where(g == 0, _SLOPES[hh], _SLOPES[GQA + hh])
            sink = sink_ref[g * GQA + hh]
            qn_f, qh, qr = _rms_rows(q_ref[hh], qgv)
            qn = _bf(qn_f)
            es, inv, p_sink = _attn_probs(qn, keys, geo, slope, sink)
            dob = _bf(do_ref[hh])
            ps = [e * inv for e in es]
            dps = [_dot(dob, vb, nt=True) for vb in vals]
            delta = ((ps[0] * dps[0]).sum(axis=-1, keepdims=True) + (ps[1] * dps[1]).sum(axis=-1, keepdims=True)
                     + (ps[2] * dps[2]).sum(axis=-1, keepdims=True))
            dsk_rows.append(jnp.broadcast_to(-jnp.sum(p_sink * delta, axis=0, keepdims=True), (1, 128)))
            dqn = jnp.zeros((T, HD), F32)
            for b in range(3):
                ds = ps[b] * (dps[b] - delta)
                dsb = _bf(ds)
                dqn = dqn + _dot(dsb, keys[b])
                dkn[b] = dkn[b] + _dot(_bf(ds.T), qn)
                dvv[b] = dvv[b] + _dot(_bf(ps[b].T), dob)
            dqn = dqn * _SCALE
            dq_ref[hh] = _rms_rows_bwd(dqn, qgv, qh, qr)
            dqg_acc = dqg_acc + jnp.sum(dqn * qh, axis=0, keepdims=True)
        dkg_acc = jnp.zeros((1, HD), F32)
        dk_raw = []
        for b in range(3):
            dk_b = dkn[b] * _SCALE
            _, kh, kr = knorm[b]
            dk_raw.append(_rms_rows_bwd(dk_b, kgv, kh, kr))
            dkg_acc = dkg_acc + jnp.sum(dk_b * kh, axis=0, keepdims=True)
        dkp_ref[...] = dk_raw[1]
        dkc_ref[...] = dk_raw[2]
        dvp_ref[...] = dvv[1]
        dvc_ref[...] = dvv[2]
        dsk = _stack_rows(dsk_rows)
        first = jnp.logical_and(g == 0, n == 0)

        @pl.when(n == 0)
        def _():
            dkm_ref[...] = dk_raw[0]
            dvm_ref[...] = dvv[0]
            dsk_ref[...] = dsk

        @pl.when(n > 0)
        def _():
            dkm_ref[...] += dk_raw[0]
            dvm_ref[...] += dvv[0]
            dsk_ref[...] += dsk

        @pl.when(first)
        def _():
            dqg_ref[...] = dqg_acc
            dkg_ref[...] = dkg_acc

        @pl.when(jnp.logical_not(first))
        def _():
            dqg_ref[...] += dqg_acc
            dkg_ref[...] += dkg_acc

    qspec, cur, prev, meta, vec = _attn_specs()
    kv_shape = jax.ShapeDtypeStruct((NKV, L, HD), F32)
    meta_shape = jax.ShapeDtypeStruct((NKV, T, HD), F32)
    vec_shape = jax.ShapeDtypeStruct((1, HD), F32)
    return pl.pallas_call(
        body, name=name,
        out_shape=(jax.ShapeDtypeStruct((NQ, L, HD), F32), kv_shape, kv_shape, meta_shape, kv_shape, kv_shape,
                   meta_shape, vec_shape, vec_shape, jax.ShapeDtypeStruct((NKV, GQA, 128), F32)),
        grid=(NKV, NB),
        in_specs=[pl.BlockSpec(memory_space=pltpu.SMEM), qspec, cur, prev, meta, cur, prev, meta, qspec, vec, vec],
        out_specs=(qspec, cur, cur, meta, cur, cur, meta, vec, vec,
                   pl.BlockSpec((None, GQA, 128), lambda g, n: (g, 0, 0))))(sinks, q, k, k, k, v, v, v, do, qg, kg)


def _ret_tables():
    h = np.arange(RH, dtype=np.float64)
    lg = np.log1p(-np.exp2(-5.0 - h))
    idx = np.arange(T, dtype=np.float64)
    diff = idx[:, None] - idx[None, :]
    decay = np.where(diff[None] >= 0, np.exp(np.maximum(diff, 0.0)[None] * lg[:, None, None]), 0.0)
    zeta = np.exp((T - 1 - idx)[None, :] * lg[:, None])
    xi = np.exp((idx + 1.0)[None, :] * lg[:, None])
    cd = np.exp(T * lg)
    f = lambda a: jnp.asarray(a, F32)
    return dict(decay=f(decay), zeta_row=f(zeta[:, None, :]), zeta_col=f(zeta[:, :, None]),
                xi_row=f(xi[:, None, :]), xi_col=f(xi[:, :, None]),
                cd=f(np.broadcast_to(cd[:, None, None], (RH, 1, HD))))


def _gn_rows(y):
    mu = jnp.mean(y, axis=-1, keepdims=True)
    yc = y - mu
    rstd = lax.rsqrt(jnp.mean(yc * yc, axis=-1, keepdims=True) + LN_EPS)
    return yc * rstd, rstd


def ret_fwd(rq, rkt, rv, rg, gng, tb, name):
    def body(q_ref, kt_ref, v_ref, g_ref, dec_ref, zr_ref, xc_ref, cd_ref, gng_ref, y_ref, st_ref, s_scr):
        n = pl.program_id(1)

        @pl.when(n == 0)
        def _():
            s_scr[...] = jnp.zeros_like(s_scr)

        s = s_scr[...]
        st_ref[...] = s
        q = q_ref[...]
        vb = _bf(v_ref[...])
        cols = n * T + lax.broadcasted_iota(jnp.int32, (1, T), 1)
        kft = jnp.where(cols >= PAD, kt_ref[...] * (HD ** -0.5), 0.0)
        a = _dot(_bf(q), _bf(kft)) * dec_ref[...]
        y = _dot(_bf(a), vb) + _dot(_bf(q * xc_ref[...]), _bf(s))
        s_scr[...] = cd_ref[...] * s + _dot(_bf(kft * zr_ref[...]), vb)
        yh, _ = _gn_rows(y)
        gv = g_ref[...]
        y_ref[...] = gv * _sig(gv) * (yh * gng_ref[...])

    hm = pl.BlockSpec((None, T, HD), lambda h, n: (h, n, 0))
    hmt = pl.BlockSpec((None, HD, T), lambda h, n: (h, 0, n))
    per_h = lambda *s: pl.BlockSpec((None,) + s, lambda h, n: (h, 0, 0))
    return pl.pallas_call(
        body, name=name,
        out_shape=(jax.ShapeDtypeStruct((RH, L, HD), F32), jax.ShapeDtypeStruct((RH, NB, HD, HD), F32)),
        grid=(RH, NB),
        in_specs=[hm, hmt, hm, hm, per_h(T, T), per_h(1, T), per_h(T, 1), per_h(1, HD), per_h(1, HD)],
        out_specs=(hm, pl.BlockSpec((None, None, HD, HD), lambda h, n: (h, n, 0, 0))),
        scratch_shapes=[pltpu.VMEM((HD, HD), F32)])(
            rq, rkt, rv, rg, tb["decay"], tb["zeta_row"], tb["xi_col"], tb["cd"], gng)


def ret_bwd(rq, rqt, rk, rkt, rv, rg, dout, states, gng, tb, name):
    def body(q_ref, qt_ref, k_ref, kt_ref, v_ref, g_ref, do_ref, st_ref, dec_ref, zc_ref, xr_ref, xc_ref, cd_ref,
             gng_ref, dq_ref, dk_ref, dv_ref, dg_ref, dgn_ref, ds_scr):
        i = pl.program_id(1)
        n = NB - 1 - i

        @pl.when(i == 0)
        def _():
            ds_scr[...] = jnp.zeros_like(ds_scr)

        dsn = _bf(ds_scr[...])
        sb = _bf(st_ref[...])
        rows = n * T + lax.broadcasted_iota(jnp.int32, (T, 1), 0)
        cols = n * T + lax.broadcasted_iota(jnp.int32, (1, T), 1)
        q = q_ref[...]
        qb = _bf(q)
        v = v_ref[...]
        vb = _bf(v)
        kf = jnp.where(rows >= PAD, k_ref[...] * (HD ** -0.5), 0.0)
        kft = jnp.where(cols >= PAD, kt_ref[...] * (HD ** -0.5), 0.0)
        dec = dec_ref[...]
        xc = xc_ref[...]
        zc = zc_ref[...]
        a = _dot(qb, _bf(kft)) * dec
        y = _dot(_bf(a), vb) + _dot(_bf(q * xc), sb)
        yh, rstd = _gn_rows(y)
        gv = g_ref[...]
        sg = _sig(gv)
        sil = gv * sg
        gn = gng_ref[...]
        dout = do_ref[...]
        dg_ref[...] = dout * (yh * gn) * (sg * (1.0 + gv * (1.0 - sg)))
        dyh = dout * sil * gn
        part = jnp.sum(dout * sil * yh, axis=0, keepdims=True)

        @pl.when(i == 0)
        def _():
            dgn_ref[...] = part

        @pl.when(i > 0)
        def _():
            dgn_ref[...] += part

        dy = rstd * (dyh - jnp.mean(dyh, axis=-1, keepdims=True) - yh * jnp.mean(dyh * yh, axis=-1, keepdims=True))
        dyb = _bf(dy)
        da = _dot(dyb, vb, nt=True) * dec
        dv_ref[...] = _dot(_bf(a.T), dyb) + _dot(_bf(kf * zc), dsn)
        dq_ref[...] = _dot(_bf(da), _bf(kf)) + _dot(dyb, sb, nt=True) * xc
        dkf = _dot(_bf(da.T), qb) + _dot(vb, dsn, nt=True) * zc
        dk_ref[...] = jnp.where(rows >= PAD, dkf * (HD ** -0.5), 0.0)
        ds_scr[...] = cd_ref[...] * ds_scr[...] + _dot(_bf(qt_ref[...] * xr_ref[...]), dyb)

    hm = pl.BlockSpec((None, T, HD), lambda h, i: (h, NB - 1 - i, 0))
    hmt = pl.BlockSpec((None, HD, T), lambda h, i: (h, 0, NB - 1 - i))
    per_h = lambda *s: pl.BlockSpec((None,) + s, lambda h, i: (h, 0, 0))
    hl = jax.ShapeDtypeStruct((RH, L, HD), F32)
    return pl.pallas_call(
        body, name=name,
        out_shape=(hl, hl, hl, hl, jax.ShapeDtypeStruct((RH, 1, HD), F32)),
        grid=(RH, NB),
        in_specs=[hm, hmt, hm, hmt, hm, hm, hm,
                  pl.BlockSpec((None, None, HD, HD), lambda h, i: (h, NB - 1 - i, 0, 0)),
                  per_h(T, T), per_h(T, 1), per_h(1, T), per_h(T, 1), per_h(1, HD), per_h(1, HD)],
        out_specs=(hm, hm, hm, hm, per_h(1, HD)),
        scratch_shapes=[pltpu.VMEM((HD, HD), F32)])(
            rq, rqt, rk, rkt, rv, rg, dout, states, tb["decay"], tb["zeta_col"], tb["xi_row"], tb["xi_col"],
            tb["cd"], gng)


HALO = 32
TAP0 = HALO - (CONV_K - 1)


def _conv_specs():
    cur = pl.BlockSpec((T, CV), lambda c: (c, 0))
    before = pl.BlockSpec((HALO, CV), lambda c: (jnp.maximum(c * (T // HALO) - 1, 0), 0))
    after = pl.BlockSpec((HALO, CV), lambda c: (jnp.minimum((c + 1) * (T // HALO), L // HALO - 1), 0))
    full = lambda r, w: pl.BlockSpec((r, w), lambda c: (0, 0))
    return cur, before, after, full


def _conv_tile_fwd(c, ca_ref, cah_ref, cb_ref, cbh_ref, w_ref, b_ref, lg_ref, lb_ref, u_scr):
    u_scr[0:HALO, :] = jnp.where(c > 0, cah_ref[...] * _sig(cbh_ref[...]), 0.0)
    u_scr[HALO:HALO + T, :] = ca_ref[...] * _sig(cb_ref[...])
    acc = jnp.zeros((T, CV), F32)
    for k in range(CONV_K):
        acc = acc + w_ref[k:k + 1, :] * u_scr[TAP0 + k:TAP0 + k + T, :]
    xc = acc + b_ref[...]
    xh, rstd = _gn_rows(xc)
    z = xh * lg_ref[...] + lb_ref[...]
    return xh, rstd, z, _sig(z)


def conv_fwd(ca, cb, w, b, lg, lb, pw, og, name):
    def body(ca_ref, cah_ref, cb_ref, cbh_ref, w_ref, b_ref, lg_ref, lb_ref, pw_ref, og_ref, y_ref, u_scr):
        c = pl.program_id(0)
        _, _, z, sg = _conv_tile_fwd(c, ca_ref, cah_ref, cb_ref, cbh_ref, w_ref, b_ref, lg_ref, lb_ref, u_scr)
        zp = _dot(_bf(z * sg), pw_ref[...])
        y_ref[...] = _rms_rows(zp, og_ref[...])[0]

    cur, before, _, full = _conv_specs()
    vec = full(1, CV)
    return pl.pallas_call(
        body, name=name, out_shape=jax.ShapeDtypeStruct((L, CV), F32), grid=(NB,),
        in_specs=[cur, before, cur, before, full(CONV_K, CV), vec, vec, vec, full(CV, CV), vec],
        out_specs=cur,
        scratch_shapes=[pltpu.VMEM((HALO + T, CV), F32)])(ca, ca, cb, cb, w, b, lg, lb, pw, og)


def conv_bwd(ca, cb, dy, w, b, lg, lb, pw, og, name):
    def body1(ca_ref, cah_ref, cb_ref, cbh_ref, dy_ref, w_ref, b_ref, lg_ref, lb_ref, pw_ref, og_ref,
              dxc_ref, dw_ref, db_ref, dlg_ref, dlb_ref, dog_ref, dpw_ref, u_scr):
        c = pl.program_id(0)

        @pl.when(c == 0)
        def _():
            for r in (dw_ref, db_ref, dlg_ref, dlb_ref, dog_ref, dpw_ref):
                r[...] = jnp.zeros_like(r)

        xh, rstd, z, sg = _conv_tile_fwd(c, ca_ref, cah_ref, cb_ref, cbh_ref, w_ref, b_ref, lg_ref, lb_ref, u_scr)
        s = z * sg
        zp = _dot(_bf(s), pw_ref[...])
        ogv = og_ref[...]
        _, zph, r2 = _rms_rows(zp, ogv)
        dyv = dy_ref[...]
        dog_ref[...] += jnp.sum(dyv * zph, axis=0, keepdims=True)
        dzpb = _bf(_rms_rows_bwd(dyv, ogv, zph, r2))
        dpw_ref[...] += _dot(_bf(s.T), dzpb)
        dz = _dot(dzpb, pw_ref[...], nt=True) * (sg * (1.0 + z * (1.0 - sg)))
        dlg_ref[...] += jnp.sum(dz * xh, axis=0, keepdims=True)
        dlb_ref[...] += jnp.sum(dz, axis=0, keepdims=True)
        dxh = dz * lg_ref[...]
        dxc = rstd * (dxh - jnp.mean(dxh, axis=-1, keepdims=True) - xh * jnp.mean(dxh * xh, axis=-1, keepdims=True))
        db_ref[...] += jnp.sum(dxc, axis=0, keepdims=True)
        dxc_ref[...] = dxc
        for k in range(CONV_K):
            dw_ref[k:k + 1, :] += jnp.sum(dxc * u_scr[TAP0 + k:TAP0 + k + T, :], axis=0, keepdims=True)

    def body2(dx_ref, dxa_ref, ca_ref, cb_ref, w_ref, dca_ref, dcb_ref, d_scr):
        c = pl.program_id(0)
        d_scr[0:T, :] = dx_ref[...]
        d_scr[T:T + HALO, :] = jnp.where(c < NB - 1, dxa_ref[...], 0.0)
        du = jnp.zeros((T, CV), F32)
        for k in range(CONV_K):
            off = CONV_K - 1 - k
            du = du + w_ref[k:k + 1, :] * d_scr[off:off + T, :]
        sg = _sig(cb_ref[...])
        dca_ref[...] = du * sg
        dcb_ref[...] = du * ca_ref[...] * sg * (1.0 - sg)

    cur, before, after, full = _conv_specs()
    seq = jax.ShapeDtypeStruct((L, CV), F32)
    vsh = jax.ShapeDtypeStruct((1, CV), F32)
    vec = full(1, CV)
    dxc, dw, db, dlg, dlb, dog, dpw = pl.pallas_call(
        body1, name=name + "_a",
        out_shape=(seq, jax.ShapeDtypeStruct((CONV_K, CV), F32), vsh, vsh, vsh, vsh,
                   jax.ShapeDtypeStruct((CV, CV), F32)),
        grid=(NB,),
        in_specs=[cur, before, cur, before, cur, full(CONV_K, CV), vec, vec, vec, full(CV, CV), vec],
        out_specs=(cur, full(CONV_K, CV), vec, vec, vec, vec, full(CV, CV)),
        scratch_shapes=[pltpu.VMEM((HALO + T, CV), F32)])(ca, ca, cb, cb, dy, w, b, lg, lb, pw, og)
    dca, dcb = pl.pallas_call(
        body2, name=name + "_b", out_shape=(seq, seq), grid=(NB,),
        in_specs=[cur, after, cur, cur, full(CONV_K, CV)], out_specs=(cur, cur),
        scratch_shapes=[pltpu.VMEM((T + HALO, CV), F32)])(dxc, dxc, ca, cb, w)
    return dca, dcb, dw, db, dlg, dlb, dog, dpw


HR = 8


def _ffn_specs():
    cur = lambda off: pl.BlockSpec((None, T, UPW), lambda j, c: (j + off, c, 0))
    before = lambda off: pl.BlockSpec((None, HR, UPW), lambda j, c: (j + off, jnp.maximum(c * (T // HR) - 1, 0), 0))
    after = lambda off: pl.BlockSpec(
        (None, HR, UPW), lambda j, c: (j + off, jnp.minimum((c + 1) * (T // HR), L // HR - 1), 0))
    wspec = lambda off, r: pl.BlockSpec((None, r, UPW), lambda j, c: (j + off, 0, 0))
    return cur, before, after, wspec


def ffn_act_fwd(f, w, b, name):
    def body(fg_ref, fgh_ref, fu_ref, fuh_ref, wg_ref, wu_ref, bg_ref, bu_ref, a_ref, scr):
        c = pl.program_id(1)

        def conv(x_ref, xh_ref, w_ref, b_ref):
            scr[0:HR, :] = jnp.where(c > 0, xh_ref[...], 0.0)
            scr[HR:HR + T, :] = x_ref[...]
            acc = b_ref[...]
            for k in range(FFN_K):
                acc = acc + w_ref[k:k + 1, :] * scr[HR - (FFN_K - 1) + k:HR - (FFN_K - 1) + k + T, :]
            return acc

        gate = conv(fg_ref, fgh_ref, wg_ref, bg_ref)
        up = conv(fu_ref, fuh_ref, wu_ref, bu_ref)
        a_ref[...] = (gate * _sig(gate) * up).astype(BF16)

    cur, before, _, wspec = _ffn_specs()
    return pl.pallas_call(
        body, name=name, out_shape=jax.ShapeDtypeStruct((NPAIR, L, UPW), BF16), grid=(NPAIR, NB),
        in_specs=[cur(0), before(0), cur(NPAIR), before(NPAIR), wspec(0, FFN_K), wspec(NPAIR, FFN_K),
                  wspec(0, 1), wspec(NPAIR, 1)],
        out_specs=cur(0),
        scratch_shapes=[pltpu.VMEM((HR + T, UPW), F32)])(f, f, f, f, w, w, b, b)


def ffn_act_bwd(f, da, w, b, name):
    ext = T + HR

    def body(fg_ref, fgb_ref, fga_ref, fu_ref, fub_ref, fua_ref, da_ref, daa_ref, wg_ref, wu_ref, bg_ref, bu_ref,
             dfg_ref, dfu_ref, dwg_ref, dwu_ref, dbg_ref, dbu_ref, xg_scr, xu_scr, d_scr):
        c = pl.program_id(1)

        def load(x_scr, x_ref, xb_ref, xa_ref):
            x_scr[0:HR, :] = jnp.where(c > 0, xb_ref[...], 0.0)
            x_scr[HR:HR + T, :] = x_ref[...]
            x_scr[HR + T:HR + T + HR, :] = xa_ref[...]

        def conv(x_scr, w_ref, b_ref):
            acc = b_ref[...]
            for k in range(FFN_K):
                off = HR - (FFN_K - 1) + k
                acc = acc + w_ref[k:k + 1, :] * x_scr[off:off + ext, :]
            return acc

        load(xg_scr, fg_ref, fgb_ref, fga_ref)
        load(xu_scr, fu_ref, fub_ref, fua_ref)
        gate = conv(xg_scr, wg_ref, bg_ref)
        up = conv(xu_scr, wu_ref, bu_ref)
        d_scr[0:T, :] = da_ref[...]
        d_scr[T:ext, :] = jnp.where(c < NB - 1, daa_ref[...], 0.0)
        dav = d_scr[...]
        sg = _sig(gate)
        d_gate = dav * up * (sg * (1.0 + gate * (1.0 - sg)))
        d_up = dav * gate * sg

        def back(dv, x_scr, w_ref, df_ref, dw_ref, db_ref):
            d_scr[...] = dv
            acc = jnp.zeros((T, UPW), F32)
            for k in range(FFN_K):
                off = FFN_K - 1 - k
                acc = acc + w_ref[k:k + 1, :] * d_scr[off:off + T, :]
            df_ref[...] = acc.astype(BF16)
            dcur = d_scr[0:T, :]
            parts = [jnp.sum(dcur * x_scr[HR - (FFN_K - 1) + k:HR - (FFN_K - 1) + k + T, :], axis=0, keepdims=True)
                     for k in range(FFN_K)]
            dwp = _stack_rows(parts)
            dbp = jnp.sum(dcur, axis=0, keepdims=True)

            @pl.when(c == 0)
            def _():
                dw_ref[...] = dwp
                db_ref[...] = dbp

            @pl.when(c > 0)
            def _():
                dw_ref[...] += dwp
                db_ref[...] += dbp

        back(d_gate, xg_scr, wg_ref, dfg_ref, dwg_ref, dbg_ref)
        back(d_up, xu_scr, wu_ref, dfu_ref, dwu_ref, dbu_ref)

    cur, before, after, wspec = _ffn_specs()
    slab = jax.ShapeDtypeStruct((NPAIR, L, UPW), BF16)
    wsh = jax.ShapeDtypeStruct((NPAIR, FFN_K, UPW), F32)
    bsh = jax.ShapeDtypeStruct((NPAIR, 1, UPW), F32)
    return pl.pallas_call(
        body, name=name, out_shape=(slab, slab, wsh, wsh, bsh, bsh), grid=(NPAIR, NB),
        in_specs=[cur(0), before(0), after(0), cur(NPAIR), before(NPAIR), after(NPAIR), cur(0), after(0),
                  wspec(0, FFN_K), wspec(NPAIR, FFN_K), wspec(0, 1), wspec(NPAIR, 1)],
        out_specs=(cur(0), cur(0), wspec(0, FFN_K), wspec(0, FFN_K), wspec(0, 1), wspec(0, 1)),
        scratch_shapes=[pltpu.VMEM((HR + T + HR, UPW), F32), pltpu.VMEM((HR + T + HR, UPW), F32),
                        pltpu.VMEM((ext, UPW), F32)])(f, f, f, f, f, f, da, da, w, w, b, b)


def _mesh_pos():
    return lax.axis_index("x"), lax.axis_index("y"), lax.axis_index("c")


def _peer(pos, k):
    x, y, c = pos
    px = 1 - x if k & 4 else x
    py = 1 - y if k & 2 else y
    pc = 1 - c if k & 1 else c
    return (px, py, pc), 4 * px + 2 * py + pc


_CHIP_FLIPS = (4, 2, 6)
_HBM = pl.BlockSpec(memory_space=pl.ANY)


def all_gather(shards, name):
    nt = len(shards)

    def body(*refs):
        ins, outs = refs[:nt], refs[nt:2 * nt]
        send, recv, local = refs[2 * nt:]
        pos = _mesh_pos()
        me = 4 * pos[0] + 2 * pos[1] + pos[2]
        sib, sib_id = _peer(pos, 1)

        def copy(t, k, block_id, to, src=None):
            dst = outs[t].at[block_id]
            return pltpu.make_async_remote_copy(
                src_ref=dst if src is None else src, dst_ref=dst, send_sem=send.at[t, k], recv_sem=recv.at[t, k],
                device_id=to, device_id_type=MESH)

        locals_ = [pltpu.make_async_copy(ins[t], outs[t].at[me], local.at[t]) for t in range(nt)]
        for cp in locals_:
            cp.start()
        started = []
        for j, flip in enumerate(_CHIP_FLIPS):
            for t in range(nt):
                started.append(copy(t, 1 + j, me, _peer(pos, flip)[0], src=ins[t]))
        for t in range(nt):
            started.append(copy(t, 0, me, sib, src=ins[t]))
        for cp in started:
            cp.start()
        for j, flip in enumerate(_CHIP_FLIPS):
            _, pid = _peer(pos, flip)
            for t in range(nt):
                copy(t, 1 + j, pid, sib).wait_recv()
                fwd = copy(t, 4 + j, pid, sib)
                fwd.start()
                started.append(fwd)
        for t in range(nt):
            copy(t, 0, sib_id, sib).wait_recv()
        for j, flip in enumerate(_CHIP_FLIPS):
            _, pid = _peer(pos, flip | 1)
            for t in range(nt):
                copy(t, 4 + j, pid, sib).wait_recv()
        for cp in started:
            cp.wait_send()
        for cp in locals_:
            cp.wait()

    return pl.pallas_call(
        body, name=name,
        out_shape=tuple(jax.ShapeDtypeStruct((NDEV,) + s.shape, s.dtype) for s in shards),
        in_specs=[_HBM] * nt, out_specs=tuple([_HBM] * nt),
        scratch_shapes=[pltpu.SemaphoreType.DMA((nt, NDEV - 1)), pltpu.SemaphoreType.DMA((nt, NDEV - 1)),
                        pltpu.SemaphoreType.DMA((nt,))])(*shards)


def pair_exchange(grads, name):
    n = len(grads)

    def body(*refs):
        ins, outs = refs[:n], refs[n:2 * n]
        send, recv = refs[2 * n:]
        pos = _mesh_pos()
        sib, _ = _peer(pos, 1)
        cps = [pltpu.make_async_remote_copy(
            src_ref=ins[i].at[:, 1 - pos[2]], dst_ref=outs[i], send_sem=send.at[i], recv_sem=recv.at[i],
            device_id=sib, device_id_type=MESH) for i in range(n)]
        for cp in cps:
            cp.start()
        for cp in cps:
            cp.wait_recv()
        for cp in cps:
            cp.wait_send()

    return pl.pallas_call(
        body, name=name,
        out_shape=tuple(jax.ShapeDtypeStruct((4,) + g.shape[2:], g.dtype) for g in grads),
        in_specs=[_HBM] * n, out_specs=tuple([_HBM] * n),
        scratch_shapes=[pltpu.SemaphoreType.DMA((n,)), pltpu.SemaphoreType.DMA((n,))])(*grads)


def pair_add(g, got, core, tr, name):
    _, _, r, c = g.shape

    def body(core_ref, g_ref, s_ref, o_ref):
        o_ref[...] = (g_ref[...].astype(F32) + s_ref[...].astype(F32)).astype(o_ref.dtype)

    blk = pl.BlockSpec((None, tr, c), lambda p, i, core_ref: (p, i, 0))
    return pl.pallas_call(
        body, name=name, out_shape=jax.ShapeDtypeStruct((4, r, c), BF16),
        grid_spec=pltpu.PrefetchScalarGridSpec(
            num_scalar_prefetch=1, grid=(4, r // tr),
            in_specs=[pl.BlockSpec((None, None, tr, c), lambda p, i, core_ref: (p, core_ref[0], i, 0)), blk],
            out_specs=blk))(core, g, got)


def chip_exchange(partials, name):
    nt = len(partials)
    flat = [g for per_layer in partials for g in per_layer]
    nc = len(flat)

    def body(*refs):
        ins, outs = refs[:nc], refs[nc:nc + nt]
        send, recv, local = refs[nc + nt:]
        pos = _mesh_pos()
        chip = 2 * pos[0] + pos[1]
        pairs = [(t * DEPTH + l, t, l) for t in range(nt) for l in range(DEPTH)]
        locals_ = [pltpu.make_async_copy(ins[i].at[chip], outs[t].at[chip, l], local.at[i]) for i, t, l in pairs]
        for cp in locals_:
            cp.start()
        sends = []
        for j, flip in enumerate(_CHIP_FLIPS):
            peer, _ = _peer(pos, flip)
            for i, t, l in pairs:
                sends.append(pltpu.make_async_remote_copy(
                    src_ref=ins[i].at[2 * peer[0] + peer[1]], dst_ref=outs[t].at[chip, l], send_sem=send.at[i, j],
                    recv_sem=recv.at[i, j], device_id=peer, device_id_type=MESH))
        for cp in sends:
            cp.start()
        for j, flip in enumerate(_CHIP_FLIPS):
            peer, _ = _peer(pos, flip)
            for i, t, l in pairs:
                pchip = 2 * peer[0] + peer[1]
                pltpu.make_async_remote_copy(
                    src_ref=ins[i].at[pchip], dst_ref=outs[t].at[pchip, l], send_sem=send.at[i, j],
                    recv_sem=recv.at[i, j], device_id=peer, device_id_type=MESH).wait_recv()
        for cp in sends:
            cp.wait_send()
        for cp in locals_:
            cp.wait()

    return pl.pallas_call(
        body, name=name,
        out_shape=tuple(jax.ShapeDtypeStruct((4, DEPTH) + per_layer[0].shape[1:], per_layer[0].dtype)
                        for per_layer in partials),
        in_specs=[_HBM] * nc, out_specs=tuple([_HBM] * nt),
        scratch_shapes=[pltpu.SemaphoreType.DMA((nc, 3)), pltpu.SemaphoreType.DMA((nc, 3)),
                        pltpu.SemaphoreType.DMA((nc,))])(*flat)


def all_reduce_small(part, name):
    rows = part.shape[0]

    def body(x_ref, o_ref, buf, send, recv):
        pos = _mesh_pos()
        me = 4 * pos[0] + 2 * pos[1] + pos[2]
        sends = []
        for k in range(1, NDEV):
            peer, _ = _peer(pos, k)
            sends.append(pltpu.make_async_remote_copy(
                src_ref=x_ref, dst_ref=buf.at[me], send_sem=send.at[k - 1], recv_sem=recv.at[k - 1],
                device_id=peer, device_id_type=MESH))
        for cp in sends:
            cp.start()
        buf[me] = x_ref[...]
        for k in range(1, NDEV):
            peer, pid = _peer(pos, k)
            pltpu.make_async_remote_copy(
                src_ref=x_ref, dst_ref=buf.at[pid], send_sem=send.at[k - 1], recv_sem=recv.at[k - 1],
                device_id=peer, device_id_type=MESH).wait_recv()
        for cp in sends:
            cp.wait_send()
        acc = buf[0]
        for d in range(1, NDEV):
            acc = acc + buf[d]
        o_ref[...] = acc

    return pl.pallas_call(
        body, name=name, out_shape=jax.ShapeDtypeStruct((rows, 128), F32),
        in_specs=[pl.BlockSpec(memory_space=pltpu.VMEM)], out_specs=pl.BlockSpec(memory_space=pltpu.VMEM),
        scratch_shapes=[pltpu.VMEM((NDEV, rows, 128), F32), pltpu.SemaphoreType.DMA((NDEV - 1,)),
                        pltpu.SemaphoreType.DMA((NDEV - 1,))])(part)


def adamw(parts, w, m, v, tr, name):
    ns, r, c = parts.shape

    def body(p_ref, w_ref, m_ref, v_ref, g_ref, d_ref, nm_ref, nv_ref):
        g = p_ref[0].astype(F32)
        for s in range(1, ns):
            g = g + p_ref[s].astype(F32)
        g_ref[...] = g
        nm = ADAM_B1 * m_ref[...] + (1.0 - ADAM_B1) * g
        nv = ADAM_B2 * v_ref[...] + (1.0 - ADAM_B2) * (g * g)
        nm_ref[...] = nm
        nv_ref[...] = nv
        m_hat = nm / (1.0 - ADAM_B1 ** ADAM_STEP)
        v_hat = nv / (1.0 - ADAM_B2 ** ADAM_STEP)
        d_ref[...] = -ADAM_LR * (m_hat / (jnp.sqrt(v_hat) + ADAM_EPS) + ADAM_WD * w_ref[...])

    blk = pl.BlockSpec((tr, c), lambda i: (i, 0))
    sh = jax.ShapeDtypeStruct((r, c), F32)
    return pl.pallas_call(
        body, name=name, out_shape=(sh, sh, sh, sh), grid=(r // tr,),
        in_specs=[pl.BlockSpec((ns, tr, c), lambda i: (0, i, 0)), blk, blk, blk],
        out_specs=(blk, blk, blk, blk), compiler_params=_params(VMEM_BIG))(parts, w, m, v)


_SPLITS = (NQ * HD, NKV * HD, NKV * HD, CV, CV, RH * HD, RH * HD, RH * HD, RH * HD)
_OFFS = np.concatenate([[0], np.cumsum(_SPLITS)]).tolist()


def _heads(x, nh):
    return x.reshape(L, nh, HD).transpose(1, 0, 2)


def _unheads(x):
    return x.transpose(1, 0, 2).reshape(L, x.shape[0] * HD)


def _split_proj(proj):
    return [proj[:, _OFFS[i]:_OFFS[i + 1]] for i in range(len(_SPLITS))]


def local_step(x, target, P):
    tb = _ret_tables()
    h = jnp.concatenate([jnp.zeros((PAD, D), F32), P["meta"], x], axis=0)
    stash = []
    for l in range(DEPTH):
        s = {"h": h}
        u = rms_fwd(h, P["norm_mix_g"][l], BF16, f"rms_mix_{l}")
        s["ut"] = u.T
        proj = mm_nn(f"in_proj_{l}", u, P["w_in"][l], 256)
        q, k, v, ca, cb, rq, rk, rv, rg = _split_proj(proj)
        s.update(q=_heads(q, NQ), k=_heads(k, NKV), v=_heads(v, NKV), ca=ca, cb=cb, rq=_heads(rq, RH),
                 rk=_heads(rk, RH), rv=_heads(rv, RH), rg=_heads(rg, RH))
        s["rkt"] = s["rk"].transpose(0, 2, 1)
        s["rqt"] = s["rq"].transpose(0, 2, 1)
        o = attn_fwd(s["q"], s["k"], s["v"], P["q_norm_g"][l], P["k_norm_g"][l], P["attn_sinks"][l], f"attn_fwd_{l}")
        s["o"] = _unheads(o)
        y_att = rms_fwd(s["o"], P["attn_out_g"][l], BF16, f"rms_att_{l}")
        y_cv = conv_fwd(ca, cb, P["cv_dw_w"][l], P["cv_dw_b"][l], P["cv_ln_g"][l], P["cv_ln_b"][l], P["cv_pw"][l],
                        P["cv_out_g"][l], f"conv_fwd_{l}")
        y_ret, s["states"] = ret_fwd(s["rq"], s["rkt"], s["rv"], s["rg"], P["ret_gn_g"][l], tb, f"ret_fwd_{l}")
        ycat = jnp.concatenate([y_att, y_cv.astype(BF16), _unheads(y_ret).astype(BF16)], axis=1)
        s["ycat_t"] = ycat.T
        h = mm_nn(f"out_proj_{l}", ycat, P["w_out"][l], 256, res=h)
        s["h1"] = h
        u2 = rms_fwd(h, P["norm_ffn_g"][l], BF16, f"rms_ffn_{l}")
        s["u2t"] = u2.T
        s["f"] = ffn_up_fwd(f"ffn_up_{l}", u2, P["ffn_up"], l)
        a = ffn_act_fwd(s["f"], P["ffn_dw_w"][l], P["ffn_dw_b"][l], f"ffn_act_fwd_{l}")
        s["at"] = a.transpose(0, 2, 1)
        h = ffn_down_fwd(f"ffn_down_{l}", a, P["ffn_down"], l, h)
        stash.append(s)

    dh, dhm, loss = loss_head(h, target)
    G = {n: [None] * DEPTH for n in ("norm_mix_g", "w_in", "q_norm_g", "k_norm_g", "attn_sinks", "attn_out_g",
                                     "cv_dw_w", "cv_dw_b", "cv_ln_g", "cv_ln_b", "cv_pw", "cv_out_g", "ret_gn_g",
                                     "w_out", "norm_ffn_g", "ffn_up", "ffn_dw_w", "ffn_dw_b", "ffn_down")}
    for l in reversed(range(DEPTH)):
        s = stash[l]
        da = ffn_down_dx(f"ffn_down_dx_{l}", dhm, P["ffn_down"], l)
        G["ffn_down"][l] = ffn_down_dw(f"ffn_down_dw_{l}", s["at"], dhm)
        dfg, dfu, dwg, dwu, dbg, dbu = ffn_act_bwd(s["f"], da, P["ffn_dw_w"][l], P["ffn_dw_b"][l], f"ffn_act_bwd_{l}")
        df = jnp.concatenate([dfg, dfu], axis=0)
        G["ffn_dw_w"][l] = jnp.concatenate([dwg, dwu], axis=0)
        G["ffn_dw_b"][l] = jnp.concatenate([dbg, dbu], axis=0)
        du2 = ffn_up_dx(f"ffn_up_dx_{l}", df, P["ffn_up"], l)
        G["ffn_up"][l] = ffn_up_dw(f"ffn_up_dw_{l}", s["u2t"], df)
        dh, dhm, G["norm_ffn_g"][l] = rms_bwd(s["h1"], P["norm_ffn_g"][l], du2, dh, f"rms_ffn_bwd_{l}")
        dycat = mm_nt(f"out_proj_dx_{l}", dhm, P["w_out"][l], 256)
        G["w_out"][l] = mm_nn(f"out_proj_dw_{l}", s["ycat_t"], dhm, 256, out_dtype=BF16)
        dy_att, dy_cv, dy_ret = dycat[:, :NQ * HD], dycat[:, NQ * HD:NQ * HD + CV], dycat[:, NQ * HD + CV:]
        do, _, G["attn_out_g"][l] = rms_bwd(s["o"], P["attn_out_g"][l], dy_att, None, f"rms_att_bwd_{l}")
        (dq, dk_c, dk_p, dk_m, dv_c, dv_p, dv_m, G["q_norm_g"][l], G["k_norm_g"][l], dsk) = attn_bwd(
            s["q"], s["k"], s["v"], _heads(do, NQ), P["q_norm_g"][l], P["k_norm_g"][l], P["attn_sinks"][l],
            f"attn_bwd_{l}")
        G["attn_sinks"][l] = dsk[:, :, 0].reshape(NQ)
        shift = lambda z: jnp.concatenate([z[:, T:], jnp.zeros((NKV, T, HD), F32)], axis=1)
        dk = (dk_c + shift(dk_p)).at[:, :T].add(dk_m)
        dv = (dv_c + shift(dv_p)).at[:, :T].add(dv_m)
        (dca, dcb, G["cv_dw_w"][l], G["cv_dw_b"][l], G["cv_ln_g"][l], G["cv_ln_b"][l], G["cv_out_g"][l],
         G["cv_pw"][l]) = conv_bwd(s["ca"], s["cb"], dy_cv, P["cv_dw_w"][l], P["cv_dw_b"][l], P["cv_ln_g"][l],
                                   P["cv_ln_b"][l], P["cv_pw"][l], P["cv_out_g"][l], f"conv_bwd_{l}")
        drq, drk, drv, drg, G["ret_gn_g"][l] = ret_bwd(
            s["rq"], s["rqt"], s["rk"], s["rkt"], s["rv"], s["rg"], _heads(dy_ret, RH), s["states"],
            P["ret_gn_g"][l], tb, f"ret_bwd_{l}")
        dproj = jnp.concatenate([_unheads(dq), _unheads(dk), _unheads(dv), dca, dcb, _unheads(drq), _unheads(drk),
                                 _unheads(drv), _unheads(drg)], axis=1).astype(BF16)
        du = mm_nt(f"in_proj_dx_{l}", dproj, P["w_in"][l], 256)
        G["w_in"][l] = mm_nn(f"in_proj_dw_{l}", s["ut"], dproj, 256, out_dtype=BF16)
        dh, dhm, G["norm_mix_g"][l] = rms_bwd(s["h"], P["norm_mix_g"][l], du, dh, f"rms_mix_bwd_{l}")
    return loss[0, 0], dh[T:], dh[PAD:T], G


_SMALL = ("meta", "norm_mix_g", "q_norm_g", "k_norm_g", "attn_sinks", "attn_out_g", "cv_dw_w", "cv_dw_b", "cv_ln_g",
          "cv_ln_b", "cv_out_g", "ret_gn_g", "norm_ffn_g", "ffn_dw_w", "ffn_dw_b")
_BIG = ("w_in", "cv_pw", "w_out", "ffn_up", "ffn_down")
_ORDER = ("meta", "norm_mix_g", "w_in", "q_norm_g", "k_norm_g", "attn_sinks", "attn_out_g", "cv_dw_w", "cv_dw_b",
          "cv_ln_g", "cv_ln_b", "cv_pw", "cv_out_g", "ret_gn_g", "w_out", "norm_ffn_g", "ffn_up", "ffn_dw_w",
          "ffn_dw_b", "ffn_down")
_SMALL_SHARDED = {"meta": D, "cv_dw_w": CV, "ffn_dw_w": 2 * D_FF}


def _pack(arrs):
    flat = jnp.concatenate([a.reshape(-1) for a in arrs])
    n = flat.shape[0]
    rows = -(-n // 1024) * 8
    return jnp.pad(flat, (0, rows * 128 - n)).reshape(rows, 128)


def _unpack(packed, shapes):
    flat = packed.reshape(-1)
    out, off = [], 0
    for s in shapes:
        n = int(np.prod(s))
        out.append(flat[off:off + n].reshape(s))
        off += n
    return out


def kernel(x, meta, norm_mix_g, w_in, q_norm_g, k_norm_g, attn_sinks, attn_out_g, cv_dw_w, cv_dw_b, cv_ln_g, cv_ln_b, cv_pw, cv_out_g, ret_gn_g, w_out, norm_ffn_g, ffn_up, ffn_dw_w, ffn_dw_b, ffn_down, loss_target, m_meta, m_norm_mix_g, m_w_in, m_q_norm_g, m_k_norm_g, m_attn_sinks, m_attn_out_g, m_cv_dw_w, m_cv_dw_b, m_cv_ln_g, m_cv_ln_b, m_cv_pw, m_cv_out_g, m_ret_gn_g, m_w_out, m_norm_ffn_g, m_ffn_up, m_ffn_dw_w, m_ffn_dw_b, m_ffn_down, v_meta, v_norm_mix_g, v_w_in, v_q_norm_g, v_k_norm_g, v_attn_sinks, v_attn_out_g, v_cv_dw_w, v_cv_dw_b, v_cv_ln_g, v_cv_ln_b, v_cv_pw, v_cv_out_g, v_ret_gn_g, v_w_out, v_norm_ffn_g, v_ffn_up, v_ffn_dw_w, v_ffn_dw_b, v_ffn_down):
    W = dict(meta=meta, norm_mix_g=norm_mix_g, w_in=w_in, q_norm_g=q_norm_g, k_norm_g=k_norm_g,
             attn_sinks=attn_sinks, attn_out_g=attn_out_g, cv_dw_w=cv_dw_w, cv_dw_b=cv_dw_b, cv_ln_g=cv_ln_g,
             cv_ln_b=cv_ln_b, cv_pw=cv_pw, cv_out_g=cv_out_g, ret_gn_g=ret_gn_g, w_out=w_out,
             norm_ffn_g=norm_ffn_g, ffn_up=ffn_up, ffn_dw_w=ffn_dw_w, ffn_dw_b=ffn_dw_b, ffn_down=ffn_down)
    M = dict(meta=m_meta, norm_mix_g=m_norm_mix_g, w_in=m_w_in, q_norm_g=m_q_norm_g, k_norm_g=m_k_norm_g,
             attn_sinks=m_attn_sinks, attn_out_g=m_attn_out_g, cv_dw_w=m_cv_dw_w, cv_dw_b=m_cv_dw_b,
             cv_ln_g=m_cv_ln_g, cv_ln_b=m_cv_ln_b, cv_pw=m_cv_pw, cv_out_g=m_cv_out_g, ret_gn_g=m_ret_gn_g,
             w_out=m_w_out, norm_ffn_g=m_norm_ffn_g, ffn_up=m_ffn_up, ffn_dw_w=m_ffn_dw_w, ffn_dw_b=m_ffn_dw_b,
             ffn_down=m_ffn_down)
    V = dict(meta=v_meta, norm_mix_g=v_norm_mix_g, w_in=v_w_in, q_norm_g=v_q_norm_g, k_norm_g=v_k_norm_g,
             attn_sinks=v_attn_sinks, attn_out_g=v_attn_out_g, cv_dw_w=v_cv_dw_w, cv_dw_b=v_cv_dw_b,
             cv_ln_g=v_cv_ln_g, cv_ln_b=v_cv_ln_b, cv_pw=v_cv_pw, cv_out_g=v_cv_out_g, ret_gn_g=v_ret_gn_g,
             w_out=v_w_out, norm_ffn_g=v_norm_ffn_g, ffn_up=v_ffn_up, ffn_dw_w=v_ffn_dw_w, ffn_dw_b=v_ffn_dw_b,
             ffn_down=v_ffn_down)
    me = 4 * lax.axis_index("x") + 2 * lax.axis_index("y") + lax.axis_index("c")

    g_in, g_pw, g_out, g_up, g_dn, g_meta, g_cdw, g_fdw = all_gather(
        [w_in.astype(BF16), cv_pw.astype(BF16), w_out.astype(BF16), ffn_up.astype(BF16), ffn_down.astype(BF16),
         meta, cv_dw_w, ffn_dw_w], "gather_weights")
    P = dict(
        meta=g_meta.transpose(1, 0, 2).reshape(N_META, D),
        w_in=g_in.transpose(1, 2, 0, 3).reshape(DEPTH, D, IN_W),
        cv_pw=g_pw.transpose(1, 0, 2, 3).reshape(DEPTH, CV, CV),
        w_out=g_out.transpose(1, 0, 2, 3).reshape(DEPTH, D, D),
        ffn_up=g_up, ffn_down=g_dn,
        cv_dw_w=g_cdw.transpose(1, 2, 0, 3).reshape(DEPTH, CONV_K, CV),
        ffn_dw_w=g_fdw.transpose(1, 0, 2, 3),
        ffn_dw_b=ffn_dw_b.reshape(DEPTH, NDEV, 1, UPW),
        attn_sinks=attn_sinks,
        ret_gn_g=ret_gn_g.reshape(DEPTH, RH, 1, HD),
    )
    for n in ("norm_mix_g", "q_norm_g", "k_norm_g", "attn_out_g", "cv_dw_b", "cv_ln_g", "cv_ln_b", "cv_out_g",
              "norm_ffn_g"):
        P[n] = W[n].reshape(DEPTH, 1, -1)

    loss_part, dx, dmeta, G = local_step(x[0], loss_target[0], P)
    loss = lax.psum(loss_part, ("x", "y", "c"))

    big = {
        "w_in": [g.reshape(D, NDEV, IN_W // NDEV).transpose(1, 0, 2).reshape(4, 2, D, IN_W // NDEV) for g in G["w_in"]],
        "cv_pw": [g.astype(BF16).reshape(4, 2, CV // NDEV, CV) for g in G["cv_pw"]],
        "w_out": [g.reshape(4, 2, D // NDEV, D) for g in G["w_out"]],
        "ffn_up": [g.reshape(4, 2, D, UPW) for g in G["ffn_up"]],
        "ffn_down": [g.reshape(4, 2, DNR, D) for g in G["ffn_down"]],
    }
    flat = [big[n][l] for n in _BIG for l in range(DEPTH)]
    got = pair_exchange(flat, "pair_exchange")
    core = lax.axis_index("c").astype(jnp.int32).reshape(1)
    add_tiles = {"w_in": 256, "cv_pw": 32, "w_out": 128, "ffn_up": 256, "ffn_down": 176}
    chip_part = [[pair_add(big[n][l], got[i * DEPTH + l], core, add_tiles[n], f"pair_add_{n}_{l}")
                  for l in range(DEPTH)] for i, n in enumerate(_BIG)]
    parts = dict(zip(_BIG, chip_exchange(chip_part, "chip_exchange")))

    small_full = {
        "meta": dmeta,
        "cv_dw_w": jnp.stack(G["cv_dw_w"]),
        "ffn_dw_w": jnp.stack([g.transpose(1, 0, 2).reshape(FFN_K, 2 * D_FF) for g in G["ffn_dw_w"]]),
        "ffn_dw_b": jnp.stack([g.reshape(2 * D_FF) for g in G["ffn_dw_b"]]),
        "attn_sinks": jnp.stack(G["attn_sinks"]),
        "ret_gn_g": jnp.stack([g.reshape(RH * HD) for g in G["ret_gn_g"]]),
    }
    for n in _SMALL:
        if n not in small_full:
            small_full[n] = jnp.stack([g.reshape(-1) for g in G[n]])
    shapes = [small_full[n].shape for n in _SMALL]
    summed = _unpack(all_reduce_small(_pack([small_full[n] for n in _SMALL]), "reduce_small"), shapes)
    small_g = {}
    for n, g in zip(_SMALL, summed):
        if n in _SMALL_SHARDED:
            width = _SMALL_SHARDED[n] // NDEV
            g = lax.dynamic_slice_in_dim(g, me * width, width, axis=g.ndim - 1)
        small_g[n] = g

    out = {}
    tiles = {"w_in": 256, "cv_pw": 64, "w_out": 128, "ffn_up": 256, "ffn_down": 176}
    for n in _BIG:
        shard = W[n].shape
        rows, cols = shard[0] * shard[1], shard[2]
        res = adamw(parts[n].reshape(4, rows, cols), W[n].reshape(rows, cols), M[n].reshape(rows, cols),
                    V[n].reshape(rows, cols), tiles[n], f"adamw_{n}")
        out[n] = [r.reshape(shard) for r in res]
    sshapes = [W[n].shape for n in _SMALL]
    packed_g = _pack([small_g[n] for n in _SMALL])
    res = adamw(packed_g[None], _pack([W[n] for n in _SMALL]), _pack([M[n] for n in _SMALL]),
                _pack([V[n] for n in _SMALL]), packed_g.shape[0], "adamw_small")
    for i, r in enumerate(res):
        for n, a in zip(_SMALL, _unpack(r, sshapes)):
            out.setdefault(n, [None] * 4)[i] = a

    return (loss, dx[None], *[out[n][0] for n in _ORDER], *[out[n][1] for n in _ORDER],
            *[out[n][2] for n in _ORDER], *[out[n][3] for n in _ORDER])
```

```python
import functools
import math

import numpy as np
import jax
import jax.numpy as jnp
from jax import lax
from jax.experimental import pallas as pl
from jax.experimental.pallas import tpu as pltpu

F32 = jnp.float32
BF16 = jnp.bfloat16

D = 1024
SEQ = 2048
DEPTH = 2
T = 128
L = SEQ + T
NB = L // T
N_META = 16
PAD = T - N_META
HD = 64
NQ = 8
NKV = 2
GQA = NQ // NKV
CV = 256
CONV_K = 31
RH = 4
D_FF = 2816
FFN_K = 3
IN_W = 2304
RMS_EPS = 1e-6
LN_EPS = 1e-5
NEG = -1e30
NDEV = 8
UPW = 2 * D_FF // NDEV
DNR = D_FF // NDEV
NPAIR = NDEV // 2

ADAM_LR, ADAM_B1, ADAM_B2, ADAM_EPS, ADAM_WD, ADAM_STEP = 0.001, 0.9, 0.999, 1e-08, 0.01, 10

VMEM_BIG = 56 * 1024 * 1024

MESH = pl.DeviceIdType.MESH


def _params(vmem=None):
    return pltpu.CompilerParams(vmem_limit_bytes=vmem) if vmem else None


def _dot(a, b, nt=False):
    return lax.dot_general(a, b, (((1,), (1 if nt else 0,)), ((), ())), preferred_element_type=F32)


def _sig(x):
    return 1.0 / (1.0 + jnp.exp(-x))


def _bf(x):
    return x.astype(BF16)


def _stack_rows(rows):
    idx = lax.broadcasted_iota(jnp.int32, (len(rows), rows[0].shape[1]), 0)
    out = jnp.zeros((len(rows), rows[0].shape[1]), F32)
    for r, v in enumerate(rows):
        out = jnp.where(idx == r, v, out)
    return out


def rms_fwd(x, g, out_dtype, name, after=None):
    n, w = x.shape

    def body(x_ref, g_ref, *rest):
        o_ref = rest[-1]
        xv = x_ref[...]
        r = lax.rsqrt(jnp.mean(xv * xv, axis=-1, keepdims=True) + RMS_EPS)
        o_ref[...] = (xv * r * g_ref[...]).astype(o_ref.dtype)

    deps = [] if after is None else [after]
    return pl.pallas_call(
        body, name=name, out_shape=jax.ShapeDtypeStruct((n, w), out_dtype), grid=(n // T,),
        in_specs=[pl.BlockSpec((T, w), lambda i: (i, 0)), pl.BlockSpec((1, w), lambda i: (0, 0))] + [_HBM] * len(deps),
        out_specs=pl.BlockSpec((T, w), lambda i: (i, 0)))(x, g, *deps)


def rms_bwd(x, g, dy, dres, name, after=None):
    n, w = x.shape
    has_res = dres is not None
    deps = [] if after is None else [after]

    def body(x_ref, g_ref, dy_ref, *rest):
        rest = rest[:len(rest) - 3 - len(deps)] + rest[len(rest) - 3:]
        if has_res:
            dres_ref, dx_ref, dxm_ref, dg_ref = rest
        else:
            dx_ref, dxm_ref, dg_ref = rest
        i = pl.program_id(0)
        xv = x_ref[...]
        r = lax.rsqrt(jnp.mean(xv * xv, axis=-1, keepdims=True) + RMS_EPS)
        xh = xv * r
        dyv = dy_ref[...]
        dxh = dyv * g_ref[...]
        dx = r * (dxh - xh * jnp.mean(dxh * xh, axis=-1, keepdims=True))
        if has_res:
            dx = dx + dres_ref[...]
        dx_ref[...] = dx
        rows = i * T + lax.broadcasted_iota(jnp.int32, (T, 1), 0)
        dxm_ref[...] = jnp.where(rows >= PAD, dx, 0.0).astype(BF16)
        part = jnp.sum(dyv * xh, axis=0, keepdims=True)

        @pl.when(i == 0)
        def _():
            dg_ref[...] = part

        @pl.when(i > 0)
        def _():
            dg_ref[...] += part

    row = pl.BlockSpec((T, w), lambda i: (i, 0))
    vec = pl.BlockSpec((1, w), lambda i: (0, 0))
    ins = [x, g, dy] + ([dres] if has_res else []) + deps
    return pl.pallas_call(
        body, name=name,
        out_shape=(jax.ShapeDtypeStruct((n, w), F32), jax.ShapeDtypeStruct((n, w), BF16),
                   jax.ShapeDtypeStruct((1, w), F32)),
        grid=(n // T,),
        in_specs=[row, vec, row] + ([row] if has_res else []) + [_HBM] * len(deps),
        out_specs=(row, row, vec))(*ins)


def loss_head(h, target):
    def body(h_ref, t_ref, dh_ref, dhm_ref, loss_ref):
        n = pl.program_id(0)
        e = jnp.where(n > 0, h_ref[...] - t_ref[...], 0.0)
        dh = e * (1.0 / D)
        dh_ref[...] = dh
        dhm_ref[...] = dh.astype(BF16)
        part = jnp.sum(jnp.sum(e * e, axis=1, keepdims=True), axis=0, keepdims=True) * (0.5 / D)

        @pl.when(n == 0)
        def _():
            loss_ref[...] = jnp.zeros_like(loss_ref)

        @pl.when(n > 0)
        def _():
            loss_ref[...] += jnp.broadcast_to(part, loss_ref.shape)

    row = pl.BlockSpec((T, D), lambda n: (n, 0))
    return pl.pallas_call(
        body, name="loss_head",
        out_shape=(jax.ShapeDtypeStruct((L, D), F32), jax.ShapeDtypeStruct((L, D), BF16),
                   jax.ShapeDtypeStruct((8, 128), F32)),
        grid=(NB,),
        in_specs=[row, pl.BlockSpec((T, D), lambda n: (jnp.maximum(n - 1, 0), 0))],
        out_specs=(row, row, pl.BlockSpec((8, 128), lambda n: (0, 0))))(h, target)


def _mm(name, a, b, *, grid, a_spec, b_spec, o_spec, out_shape, nt=False, red=False, res=None, res_spec=None):
    def body(a_ref, b_ref, *rest):
        o_ref = rest[-1]
        av = a_ref[...]
        bv = b_ref[...]
        if bv.ndim == 3:
            bv = bv.reshape(bv.shape[0] * bv.shape[1], bv.shape[2])
        acc = _dot(av, bv, nt)
        if red:
            k = pl.program_id(0)

            @pl.when(k == 0)
            def _():
                o_ref[...] = acc

            @pl.when(k > 0)
            def _():
                o_ref[...] += acc
        else:
            if res is not None:
                rows = lax.broadcasted_iota(jnp.int32, (acc.shape[0], 1), 0)
                acc = rest[0][...] + jnp.where(rows >= PAD, acc, 0.0)
            o_ref[...] = acc.astype(o_ref.dtype)

    ins = [a, b] + ([res] if res is not None else [])
    specs = [a_spec, b_spec] + ([res_spec] if res is not None else [])
    return pl.pallas_call(body, name=name, out_shape=out_shape, grid=grid, in_specs=specs, out_specs=o_spec,
                          compiler_params=_params(VMEM_BIG))(*ins)


def mm_nn(name, a, b, tn, out_dtype=F32, res=None):
    m, k = a.shape
    n = b.shape[1]
    return _mm(name, a, b, grid=(n // tn,),
               a_spec=pl.BlockSpec((m, k), lambda j: (0, 0)), b_spec=pl.BlockSpec((k, tn), lambda j: (0, j)),
               o_spec=pl.BlockSpec((m, tn), lambda j: (0, j)), out_shape=jax.ShapeDtypeStruct((m, n), out_dtype),
               res=res, res_spec=pl.BlockSpec((m, tn), lambda j: (0, j)))


def mm_nt(name, a, b, tn):
    m, k = a.shape
    n = b.shape[0]
    return _mm(name, a, b, grid=(n // tn,), nt=True,
               a_spec=pl.BlockSpec((m, k), lambda j: (0, 0)), b_spec=pl.BlockSpec((tn, k), lambda j: (j, 0)),
               o_spec=pl.BlockSpec((m, tn), lambda j: (0, j)), out_shape=jax.ShapeDtypeStruct((m, n), F32))


def ffn_up_fwd(name, u2, wup):
    return _mm(name, u2, wup, grid=(NDEV,),
               a_spec=pl.BlockSpec((L, D), lambda j: (0, 0)),
               b_spec=pl.BlockSpec((None, D, UPW), lambda j: (j, 0, 0)),
               o_spec=pl.BlockSpec((None, L, UPW), lambda j: (j, 0, 0)),
               out_shape=jax.ShapeDtypeStruct((NDEV, L, UPW), F32))


def ffn_up_dx(name, df, wup):
    return _mm(name, df, wup, grid=(NDEV,), nt=True, red=True,
               a_spec=pl.BlockSpec((None, L, UPW), lambda j: (j, 0, 0)),
               b_spec=pl.BlockSpec((None, D, UPW), lambda j: (j, 0, 0)),
               o_spec=pl.BlockSpec((L, D), lambda j: (0, 0)),
               out_shape=jax.ShapeDtypeStruct((L, D), F32))


def ffn_up_dw(name, u2t, df):
    return _mm(name, u2t, df, grid=(NDEV,),
               a_spec=pl.BlockSpec((D, L), lambda j: (0, 0)),
               b_spec=pl.BlockSpec((None, L, UPW), lambda j: (j, 0, 0)),
               o_spec=pl.BlockSpec((None, D, UPW), lambda j: (j, 0, 0)),
               out_shape=jax.ShapeDtypeStruct((NDEV, D, UPW), BF16))


def ffn_down_fwd(name, a, wdn, res, tn=256):
    def body(a_ref, b_ref, r_ref, o_ref):
        acc = jnp.zeros((L, tn), F32)
        for g in range(NPAIR):
            bv = b_ref[2 * g:2 * g + 2]
            acc = acc + _dot(a_ref[g], bv.reshape(2 * DNR, tn))
        rows = lax.broadcasted_iota(jnp.int32, (L, 1), 0)
        o_ref[...] = r_ref[...] + jnp.where(rows >= PAD, acc, 0.0)

    return pl.pallas_call(
        body, name=name, out_shape=jax.ShapeDtypeStruct((L, D), F32), grid=(D // tn,),
        in_specs=[pl.BlockSpec((NPAIR, L, UPW), lambda j: (0, 0, 0)),
                  pl.BlockSpec((NDEV, DNR, tn), lambda j: (0, 0, j)),
                  pl.BlockSpec((L, tn), lambda j: (0, j))],
        out_specs=pl.BlockSpec((L, tn), lambda j: (0, j)),
        compiler_params=_params(VMEM_BIG))(a, wdn, res)


def ffn_down_dx(name, dh, wdn):
    return _mm(name, dh, wdn, grid=(NPAIR,), nt=True,
               a_spec=pl.BlockSpec((L, D), lambda g: (0, 0)),
               b_spec=pl.BlockSpec((2, DNR, D), lambda g: (g, 0, 0)),
               o_spec=pl.BlockSpec((None, L, UPW), lambda g: (g, 0, 0)),
               out_shape=jax.ShapeDtypeStruct((NPAIR, L, UPW), F32))


def ffn_down_dw(name, at, dh):
    return _mm(name, at, dh, grid=(NPAIR,),
               a_spec=pl.BlockSpec((None, UPW, L), lambda g: (g, 0, 0)),
               b_spec=pl.BlockSpec((L, D), lambda g: (0, 0)),
               o_spec=pl.BlockSpec((UPW, D), lambda g: (g, 0)),
               out_shape=jax.ShapeDtypeStruct((D_FF, D), BF16))


_SLOPES = [2.0 ** (-8.0 * (h + 1) / NQ) for h in range(NQ)]
_SCALE = HD ** -0.5


def _attn_geometry(n):
    i = lax.broadcasted_iota(jnp.int32, (T, T), 0)
    j = lax.broadcasted_iota(jnp.int32, (T, T), 1)
    d_meta = n * T + i - j
    ok_meta = (j >= PAD) & (d_meta >= 0)
    dist_meta = jnp.minimum(d_meta, T).astype(F32)
    ok_prev = j > i + jnp.where(n >= 2, 0, T)
    dist_prev = (T + i - j).astype(F32)
    ok_cur = j <= i - jnp.where(n >= 1, 0, T)
    dist_cur = (i - j).astype(F32)
    return (ok_meta, dist_meta), (ok_prev, dist_prev), (ok_cur, dist_cur)


def _rms_rows(x, g):
    r = lax.rsqrt(jnp.mean(x * x, axis=-1, keepdims=True) + RMS_EPS)
    xh = x * r
    return xh * g, xh, r


def _rms_rows_bwd(dy, g, xh, r):
    dxh = dy * g
    return r * (dxh - xh * jnp.mean(dxh * xh, axis=-1, keepdims=True))


def _attn_specs():
    qspec = pl.BlockSpec((GQA, T, HD), lambda g, n: (g, n, 0))
    cur = pl.BlockSpec((None, T, HD), lambda g, n: (g, n, 0))
    prev = pl.BlockSpec((None, T, HD), lambda g, n: (g, jnp.maximum(n - 1, 0), 0))
    meta = pl.BlockSpec((None, T, HD), lambda g, n: (g, 0, 0))
    vec = pl.BlockSpec((1, HD), lambda g, n: (0, 0))
    return qspec, cur, prev, meta, vec


def _attn_probs(qn, keys, geo, slope, sink):
    logits = []
    for kb, (ok, dist) in zip(keys, geo):
        s = _dot(qn, kb, nt=True) * _SCALE - slope * dist
        logits.append(jnp.where(ok, s, NEG))
    m = jnp.maximum(jnp.maximum(jnp.max(logits[0], axis=-1, keepdims=True),
                                jnp.max(logits[1], axis=-1, keepdims=True)),
                    jnp.maximum(jnp.max(logits[2], axis=-1, keepdims=True), sink))
    es = [jnp.exp(s - m) for s in logits]
    e_sink = jnp.exp(sink - m)
    z = es[0].sum(axis=-1, keepdims=True) + es[1].sum(axis=-1, keepdims=True) + es[2].sum(axis=-1, keepdims=True) + e_sink
    inv = 1.0 / z
    return es, inv, e_sink * inv


def attn_fwd(q, k, v, qg, kg, sinks, name):
    def body(sink_ref, q_ref, kc_ref, kp_ref, km_ref, vc_ref, vp_ref, vm_ref, qg_ref, kg_ref, o_ref):
        g = pl.program_id(0)
        n = pl.program_id(1)
        geo = _attn_geometry(n)
        kgv = kg_ref[...]
        keys = [_bf(_rms_rows(r[...], kgv)[0]) for r in (km_ref, kp_ref, kc_ref)]
        vals = [_bf(r[...]) for r in (vm_ref, vp_ref, vc_ref)]
        for hh in range(GQA):
            slope = jnp.where(g == 0, _SLOPES[hh], _SLOPES[GQA + hh])
            sink = sink_ref[g * GQA + hh]
            qn = _bf(_rms_rows(q_ref[hh], qg_ref[...])[0])
            es, inv, _ = _attn_probs(qn, keys, geo, slope, sink)
            o = _dot(_bf(es[0]), vals[0]) + _dot(_bf(es[1]), vals[1]) + _dot(_bf(es[2]), vals[2])
            o_ref[hh] = o * inv

    qspec, cur, prev, meta, vec = _attn_specs()
    return pl.pallas_call(
        body, name=name, out_shape=jax.ShapeDtypeStruct((NQ, L, HD), F32), grid=(NKV, NB),
        in_specs=[pl.BlockSpec(memory_space=pltpu.SMEM), qspec, cur, prev, meta, cur, prev, meta, vec, vec],
        out_specs=qspec)(sinks, q, k, k, k, v, v, v, qg, kg)


def attn_bwd(q, k, v, do, qg, kg, sinks, name):
    def body(sink_ref, q_ref, kc_ref, kp_ref, km_ref, vc_ref, vp_ref, vm_ref, do_ref, qg_ref, kg_ref,
             dq_ref, dkc_ref, dkp_ref, dkm_ref, dvc_ref, dvp_ref, dvm_ref, dqg_ref, dkg_ref, dsk_ref):
        g = pl.program_id(0)
        n = pl.program_id(1)
        geo = _attn_geometry(n)
        kgv = kg_ref[...]
        qgv = qg_ref[...]
        knorm = [_rms_rows(r[...], kgv) for r in (km_ref, kp_ref, kc_ref)]
        keys = [_bf(t[0]) for t in knorm]
        vals = [_bf(r[...]) for r in (vm_ref, vp_ref, vc_ref)]
        dkn = [jnp.zeros((T, HD), F32) for _ in range(3)]
        dvv = [jnp.zeros((T, HD), F32) for _ in range(3)]
        dqg_acc = jnp.zeros((1, HD), F32)
        dsk_rows = []
        for hh in range(GQA):
            slope = jnp.where(g == 0, _SLOPES[hh], _SLOPES[GQA + hh])
            sink = sink_ref[g * GQA + hh]
            qn_f, qh, qr = _rms_rows(q_ref[hh], qgv)
            qn = _bf(qn_f)
            es, inv, p_sink = _attn_probs(qn, keys, geo, slope, sink)
            dob = _bf(do_ref[hh])
            ps = [e * inv for e in es]
            dps = [_dot(dob, vb, nt=True) for vb in vals]
            delta = ((ps[0] * dps[0]).sum(axis=-1, keepdims=True) + (ps[1] * dps[1]).sum(axis=-1, keepdims=True)
                     + (ps[2] * dps[2]).sum(axis=-1, keepdims=True))
            dsk_rows.append(jnp.broadcast_to(-jnp.sum(p_sink * delta, axis=0, keepdims=True), (1, 128)))
            dqn = jnp.zeros((T, HD), F32)
            for b in range(3):
                ds = ps[b] * (dps[b] - delta)
                dsb = _bf(ds)
                dqn = dqn + _dot(dsb, keys[b])
                dkn[b] = dkn[b] + _dot(_bf(ds.T), qn)
                dvv[b] = dvv[b] + _dot(_bf(ps[b].T), dob)
            dqn = dqn * _SCALE
            dq_ref[hh] = _rms_rows_bwd(dqn, qgv, qh, qr)
            dqg_acc = dqg_acc + jnp.sum(dqn * qh, axis=0, keepdims=True)
        dkg_acc = jnp.zeros((1, HD), F32)
        dk_raw = []
        for b in range(3):
            dk_b = dkn[b] * _SCALE
            _, kh, kr = knorm[b]
            dk_raw.append(_rms_rows_bwd(dk_b, kgv, kh, kr))
            dkg_acc = dkg_acc + jnp.sum(dk_b * kh, axis=0, keepdims=True)
        dkp_ref[...] = dk_raw[1]
        dkc_ref[...] = dk_raw[2]
        dvp_ref[...] = dvv[1]
        dvc_ref[...] = dvv[2]
        dsk = _stack_rows(dsk_rows)
        first = jnp.logical_and(g == 0, n == 0)

        @pl.when(n == 0)
        def _():
            dkm_ref[...] = dk_raw[0]
            dvm_ref[...] = dvv[0]
            dsk_ref[...] = dsk

        @pl.when(n > 0)
        def _():
            dkm_ref[...] += dk_raw[0]
            dvm_ref[...] += dvv[0]
            dsk_ref[...] += dsk

        @pl.when(first)
        def _():
            dqg_ref[...] = dqg_acc
            dkg_ref[...] = dkg_acc

        @pl.when(jnp.logical_not(first))
        def _():
            dqg_ref[...] += dqg_acc
            dkg_ref[...] += dkg_acc

    qspec, cur, prev, meta, vec = _attn_specs()
    kv_shape = jax.ShapeDtypeStruct((NKV, L, HD), F32)
    meta_shape = jax.ShapeDtypeStruct((NKV, T, HD), F32)
    vec_shape = jax.ShapeDtypeStruct((1, HD), F32)
    return pl.pallas_call(
        body, name=name,
        out_shape=(jax.ShapeDtypeStruct((NQ, L, HD), F32), kv_shape, kv_shape, meta_shape, kv_shape, kv_shape,
                   meta_shape, vec_shape, vec_shape, jax.ShapeDtypeStruct((NKV, GQA, 128), F32)),
        grid=(NKV, NB),
        in_specs=[pl.BlockSpec(memory_space=pltpu.SMEM), qspec, cur, prev, meta, cur, prev, meta, qspec, vec, vec],
        out_specs=(qspec, cur, cur, meta, cur, cur, meta, vec, vec,
                   pl.BlockSpec((None, GQA, 128), lambda g, n: (g, 0, 0))))(sinks, q, k, k, k, v, v, v, do, qg, kg)


def _ret_tables():
    h = np.arange(RH, dtype=np.float64)
    lg = np.log1p(-np.exp2(-5.0 - h))
    idx = np.arange(T, dtype=np.float64)
    diff = idx[:, None] - idx[None, :]
    decay = np.where(diff[None] >= 0, np.exp(np.maximum(diff, 0.0)[None] * lg[:, None, None]), 0.0)
    zeta = np.exp((T - 1 - idx)[None, :] * lg[:, None])
    xi = np.exp((idx + 1.0)[None, :] * lg[:, None])
    cd = np.exp(T * lg)
    f = lambda a: jnp.asarray(a, F32)
    return dict(decay=f(decay), zeta_row=f(zeta[:, None, :]), zeta_col=f(zeta[:, :, None]),
                xi_row=f(xi[:, None, :]), xi_col=f(xi[:, :, None]),
                cd=f(np.broadcast_to(cd[:, None, None], (RH, 1, HD))))


def _gn_rows(y):
    mu = jnp.mean(y, axis=-1, keepdims=True)
    yc = y - mu
    rstd = lax.rsqrt(jnp.mean(yc * yc, axis=-1, keepdims=True) + LN_EPS)
    return yc * rstd, rstd


def ret_fwd(rq, rkt, rv, rg, gng, tb, name):
    def body(q_ref, kt_ref, v_ref, g_ref, dec_ref, zr_ref, xc_ref, cd_ref, gng_ref, y_ref, st_ref, s_scr):
        n = pl.program_id(1)

        @pl.when(n == 0)
        def _():
            s_scr[...] = jnp.zeros_like(s_scr)

        s = s_scr[...]
        st_ref[...] = s
        q = q_ref[...]
        vb = _bf(v_ref[...])
        cols = n * T + lax.broadcasted_iota(jnp.int32, (1, T), 1)
        kft = jnp.where(cols >= PAD, kt_ref[...] * (HD ** -0.5), 0.0)
        a = _dot(_bf(q), _bf(kft)) * dec_ref[...]
        y = _dot(_bf(a), vb) + _dot(_bf(q * xc_ref[...]), _bf(s))
        s_scr[...] = cd_ref[...] * s + _dot(_bf(kft * zr_ref[...]), vb)
        yh, _ = _gn_rows(y)
        gv = g_ref[...]
        y_ref[...] = gv * _sig(gv) * (yh * gng_ref[...])

    hm = pl.BlockSpec((None, T, HD), lambda h, n: (h, n, 0))
    hmt = pl.BlockSpec((None, HD, T), lambda h, n: (h, 0, n))
    per_h = lambda *s: pl.BlockSpec((None,) + s, lambda h, n: (h, 0, 0))
    return pl.pallas_call(
        body, name=name,
        out_shape=(jax.ShapeDtypeStruct((RH, L, HD), F32), jax.ShapeDtypeStruct((RH, NB, HD, HD), F32)),
        grid=(RH, NB),
        in_specs=[hm, hmt, hm, hm, per_h(T, T), per_h(1, T), per_h(T, 1), per_h(1, HD), per_h(1, HD)],
        out_specs=(hm, pl.BlockSpec((None, None, HD, HD), lambda h, n: (h, n, 0, 0))),
        scratch_shapes=[pltpu.VMEM((HD, HD), F32)])(
            rq, rkt, rv, rg, tb["decay"], tb["zeta_row"], tb["xi_col"], tb["cd"], gng)


def ret_bwd(rq, rqt, rk, rkt, rv, rg, dout, states, gng, tb, name):
    def body(q_ref, qt_ref, k_ref, kt_ref, v_ref, g_ref, do_ref, st_ref, dec_ref, zc_ref, xr_ref, xc_ref, cd_ref,
             gng_ref, dq_ref, dk_ref, dv_ref, dg_ref, dgn_ref, ds_scr):
        i = pl.program_id(1)
        n = NB - 1 - i

        @pl.when(i == 0)
        def _():
            ds_scr[...] = jnp.zeros_like(ds_scr)

        dsn = _bf(ds_scr[...])
        sb = _bf(st_ref[...])
        rows = n * T + lax.broadcasted_iota(jnp.int32, (T, 1), 0)
        cols = n * T + lax.broadcasted_iota(jnp.int32, (1, T), 1)
        q = q_ref[...]
        qb = _bf(q)
        v = v_ref[...]
        vb = _bf(v)
        kf = jnp.where(rows >= PAD, k_ref[...] * (HD ** -0.5), 0.0)
        kft = jnp.where(cols >= PAD, kt_ref[...] * (HD ** -0.5), 0.0)
        dec = dec_ref[...]
        xc = xc_ref[...]
        zc = zc_ref[...]
        a = _dot(qb, _bf(kft)) * dec
        y = _dot(_bf(a), vb) + _dot(_bf(q * xc), sb)
        yh, rstd = _gn_rows(y)
        gv = g_ref[...]
        sg = _sig(gv)
        sil = gv * sg
        gn = gng_ref[...]
        dout = do_ref[...]
        dg_ref[...] = dout * (yh * gn) * (sg * (1.0 + gv * (1.0 - sg)))
        dyh = dout * sil * gn
        part = jnp.sum(dout * sil * yh, axis=0, keepdims=True)

        @pl.when(i == 0)
        def _():
            dgn_ref[...] = part

        @pl.when(i > 0)
        def _():
            dgn_ref[...] += part

        dy = rstd * (dyh - jnp.mean(dyh, axis=-1, keepdims=True) - yh * jnp.mean(dyh * yh, axis=-1, keepdims=True))
        dyb = _bf(dy)
        da = _dot(dyb, vb, nt=True) * dec
        dv_ref[...] = _dot(_bf(a.T), dyb) + _dot(_bf(kf * zc), dsn)
        dq_ref[...] = _dot(_bf(da), _bf(kf)) + _dot(dyb, sb, nt=True) * xc
        dkf = _dot(_bf(da.T), qb) + _dot(vb, dsn, nt=True) * zc
        dk_ref[...] = jnp.where(rows >= PAD, dkf * (HD ** -0.5), 0.0)
        ds_scr[...] = cd_ref[...] * ds_scr[...] + _dot(_bf(qt_ref[...] * xr_ref[...]), dyb)

    hm = pl.BlockSpec((None, T, HD), lambda h, i: (h, NB - 1 - i, 0))
    hmt = pl.BlockSpec((None, HD, T), lambda h, i: (h, 0, NB - 1 - i))
    per_h = lambda *s: pl.BlockSpec((None,) + s, lambda h, i: (h, 0, 0))
    hl = jax.ShapeDtypeStruct((RH, L, HD), F32)
    return pl.pallas_call(
        body, name=name,
        out_shape=(hl, hl, hl, hl, jax.ShapeDtypeStruct((RH, 1, HD), F32)),
        grid=(RH, NB),
        in_specs=[hm, hmt, hm, hmt, hm, hm, hm,
                  pl.BlockSpec((None, None, HD, HD), lambda h, i: (h, NB - 1 - i, 0, 0)),
                  per_h(T, T), per_h(T, 1), per_h(1, T), per_h(T, 1), per_h(1, HD), per_h(1, HD)],
        out_specs=(hm, hm, hm, hm, per_h(1, HD)),
        scratch_shapes=[pltpu.VMEM((HD, HD), F32)])(
            rq, rqt, rk, rkt, rv, rg, dout, states, tb["decay"], tb["zeta_col"], tb["xi_row"], tb["xi_col"],
            tb["cd"], gng)


HALO = 32
TAP0 = HALO - (CONV_K - 1)


def _conv_specs():
    cur = pl.BlockSpec((T, CV), lambda c: (c, 0))
    before = pl.BlockSpec((HALO, CV), lambda c: (jnp.maximum(c * (T // HALO) - 1, 0), 0))
    after = pl.BlockSpec((HALO, CV), lambda c: (jnp.minimum((c + 1) * (T // HALO), L // HALO - 1), 0))
    full = lambda r, w: pl.BlockSpec((r, w), lambda c: (0, 0))
    return cur, before, after, full


def _conv_tile_fwd(c, ca_ref, cah_ref, cb_ref, cbh_ref, w_ref, b_ref, lg_ref, lb_ref, u_scr):
    u_scr[0:HALO, :] = jnp.where(c > 0, cah_ref[...] * _sig(cbh_ref[...]), 0.0)
    u_scr[HALO:HALO + T, :] = ca_ref[...] * _sig(cb_ref[...])
    acc = jnp.zeros((T, CV), F32)
    for k in range(CONV_K):
        acc = acc + w_ref[k:k + 1, :] * u_scr[TAP0 + k:TAP0 + k + T, :]
    xc = acc + b_ref[...]
    xh, rstd = _gn_rows(xc)
    z = xh * lg_ref[...] + lb_ref[...]
    return xh, rstd, z, _sig(z)


def conv_fwd(ca, cb, w, b, lg, lb, pw, og, name):
    def body(ca_ref, cah_ref, cb_ref, cbh_ref, w_ref, b_ref, lg_ref, lb_ref, pw_ref, og_ref, y_ref, u_scr):
        c = pl.program_id(0)
        _, _, z, sg = _conv_tile_fwd(c, ca_ref, cah_ref, cb_ref, cbh_ref, w_ref, b_ref, lg_ref, lb_ref, u_scr)
        zp = _dot(_bf(z * sg), pw_ref[...])
        y_ref[...] = _rms_rows(zp, og_ref[...])[0]

    cur, before, _, full = _conv_specs()
    vec = full(1, CV)
    return pl.pallas_call(
        body, name=name, out_shape=jax.ShapeDtypeStruct((L, CV), F32), grid=(NB,),
        in_specs=[cur, before, cur, before, full(CONV_K, CV), vec, vec, vec, full(CV, CV), vec],
        out_specs=cur,
        scratch_shapes=[pltpu.VMEM((HALO + T, CV), F32)])(ca, ca, cb, cb, w, b, lg, lb, pw, og)


def conv_bwd(ca, cb, dy, w, b, lg, lb, pw, og, name):
    def body1(ca_ref, cah_ref, cb_ref, cbh_ref, dy_ref, w_ref, b_ref, lg_ref, lb_ref, pw_ref, og_ref,
              dxc_ref, dw_ref, db_ref, dlg_ref, dlb_ref, dog_ref, dpw_ref, u_scr):
        c = pl.program_id(0)

        @pl.when(c == 0)
        def _():
            for r in (dw_ref, db_ref, dlg_ref, dlb_ref, dog_ref, dpw_ref):
                r[...] = jnp.zeros_like(r)

        xh, rstd, z, sg = _conv_tile_fwd(c, ca_ref, cah_ref, cb_ref, cbh_ref, w_ref, b_ref, lg_ref, lb_ref, u_scr)
        s = z * sg
        zp = _dot(_bf(s), pw_ref[...])
        ogv = og_ref[...]
        _, zph, r2 = _rms_rows(zp, ogv)
        dyv = dy_ref[...]
        dog_ref[...] += jnp.sum(dyv * zph, axis=0, keepdims=True)
        dzpb = _bf(_rms_rows_bwd(dyv, ogv, zph, r2))
        dpw_ref[...] += _dot(_bf(s.T), dzpb)
        dz = _dot(dzpb, pw_ref[...], nt=True) * (sg * (1.0 + z * (1.0 - sg)))
        dlg_ref[...] += jnp.sum(dz * xh, axis=0, keepdims=True)
        dlb_ref[...] += jnp.sum(dz, axis=0, keepdims=True)
        dxh = dz * lg_ref[...]
        dxc = rstd * (dxh - jnp.mean(dxh, axis=-1, keepdims=True) - xh * jnp.mean(dxh * xh, axis=-1, keepdims=True))
        db_ref[...] += jnp.sum(dxc, axis=0, keepdims=True)
        dxc_ref[...] = dxc
        for k in range(CONV_K):
            dw_ref[k:k + 1, :] += jnp.sum(dxc * u_scr[TAP0 + k:TAP0 + k + T, :], axis=0, keepdims=True)

    def body2(dx_ref, dxa_ref, ca_ref, cb_ref, w_ref, dca_ref, dcb_ref, d_scr):
        c = pl.program_id(0)
        d_scr[0:T, :] = dx_ref[...]
        d_scr[T:T + HALO, :] = jnp.where(c < NB - 1, dxa_ref[...], 0.0)
        du = jnp.zeros((T, CV), F32)
        for k in range(CONV_K):
            off = CONV_K - 1 - k
            du = du + w_ref[k:k + 1, :] * d_scr[off:off + T, :]
        sg = _sig(cb_ref[...])
        dca_ref[...] = du * sg
        dcb_ref[...] = du * ca_ref[...] * sg * (1.0 - sg)

    cur, before, after, full = _conv_specs()
    seq = jax.ShapeDtypeStruct((L, CV), F32)
    vsh = jax.ShapeDtypeStruct((1, CV), F32)
    vec = full(1, CV)
    dxc, dw, db, dlg, dlb, dog, dpw = pl.pallas_call(
        body1, name=name + "_a",
        out_shape=(seq, jax.ShapeDtypeStruct((CONV_K, CV), F32), vsh, vsh, vsh, vsh,
                   jax.ShapeDtypeStruct((CV, CV), F32)),
        grid=(NB,),
        in_specs=[cur, before, cur, before, cur, full(CONV_K, CV), vec, vec, vec, full(CV, CV), vec],
        out_specs=(cur, full(CONV_K, CV), vec, vec, vec, vec, full(CV, CV)),
        scratch_shapes=[pltpu.VMEM((HALO + T, CV), F32)])(ca, ca, cb, cb, dy, w, b, lg, lb, pw, og)
    dca, dcb = pl.pallas_call(
        body2, name=name + "_b", out_shape=(seq, seq), grid=(NB,),
        in_specs=[cur, after, cur, cur, full(CONV_K, CV)], out_specs=(cur, cur),
        scratch_shapes=[pltpu.VMEM((T + HALO, CV), F32)])(dxc, dxc, ca, cb, w)
    return dca, dcb, dw, db, dlg, dlb, dog, dpw


HR = 8


def _ffn_specs():
    cur = lambda off: pl.BlockSpec((None, T, UPW), lambda j, c: (j + off, c, 0))
    before = lambda off: pl.BlockSpec((None, HR, UPW), lambda j, c: (j + off, jnp.maximum(c * (T // HR) - 1, 0), 0))
    after = lambda off: pl.BlockSpec(
        (None, HR, UPW), lambda j, c: (j + off, jnp.minimum((c + 1) * (T // HR), L // HR - 1), 0))
    wspec = lambda off, r: pl.BlockSpec((None, r, UPW), lambda j, c: (j + off, 0, 0))
    return cur, before, after, wspec


def ffn_act_fwd(f, w, b, name):
    def body(fg_ref, fgh_ref, fu_ref, fuh_ref, wg_ref, wu_ref, bg_ref, bu_ref, a_ref, scr):
        c = pl.program_id(1)

        def conv(x_ref, xh_ref, w_ref, b_ref):
            scr[0:HR, :] = jnp.where(c > 0, xh_ref[...], 0.0)
            scr[HR:HR + T, :] = x_ref[...]
            acc = b_ref[...]
            for k in range(FFN_K):
                acc = acc + w_ref[k:k + 1, :] * scr[HR - (FFN_K - 1) + k:HR - (FFN_K - 1) + k + T, :]
            return acc

        gate = conv(fg_ref, fgh_ref, wg_ref, bg_ref)
        up = conv(fu_ref, fuh_ref, wu_ref, bu_ref)
        a_ref[...] = (gate * _sig(gate) * up).astype(BF16)

    cur, before, _, wspec = _ffn_specs()
    return pl.pallas_call(
        body, name=name, out_shape=jax.ShapeDtypeStruct((NPAIR, L, UPW), BF16), grid=(NPAIR, NB),
        in_specs=[cur(0), before(0), cur(NPAIR), before(NPAIR), wspec(0, FFN_K), wspec(NPAIR, FFN_K),
                  wspec(0, 1), wspec(NPAIR, 1)],
        out_specs=cur(0),
        scratch_shapes=[pltpu.VMEM((HR + T, UPW), F32)])(f, f, f, f, w, w, b, b)


def ffn_act_bwd(f, da, w, b, name):
    ext = T + HR

    def body(fg_ref, fgb_ref, fga_ref, fu_ref, fub_ref, fua_ref, da_ref, daa_ref, wg_ref, wu_ref, bg_ref, bu_ref,
             dfg_ref, dfu_ref, dwg_ref, dwu_ref, dbg_ref, dbu_ref, xg_scr, xu_scr, d_scr):
        c = pl.program_id(1)

        def load(x_scr, x_ref, xb_ref, xa_ref):
            x_scr[0:HR, :] = jnp.where(c > 0, xb_ref[...], 0.0)
            x_scr[HR:HR + T, :] = x_ref[...]
            x_scr[HR + T:HR + T + HR, :] = xa_ref[...]

        def conv(x_scr, w_ref, b_ref):
            acc = b_ref[...]
            for k in range(FFN_K):
                off = HR - (FFN_K - 1) + k
                acc = acc + w_ref[k:k + 1, :] * x_scr[off:off + ext, :]
            return acc

        load(xg_scr, fg_ref, fgb_ref, fga_ref)
        load(xu_scr, fu_ref, fub_ref, fua_ref)
        gate = conv(xg_scr, wg_ref, bg_ref)
        up = conv(xu_scr, wu_ref, bu_ref)
        d_scr[0:T, :] = da_ref[...]
        d_scr[T:ext, :] = jnp.where(c < NB - 1, daa_ref[...], 0.0)
        dav = d_scr[...]
        sg = _sig(gate)
        d_gate = dav * up * (sg * (1.0 + gate * (1.0 - sg)))
        d_up = dav * gate * sg

        def back(dv, x_scr, w_ref, df_ref, dw_ref, db_ref):
            d_scr[...] = dv
            acc = jnp.zeros((T, UPW), F32)
            for k in range(FFN_K):
                off = FFN_K - 1 - k
                acc = acc + w_ref[k:k + 1, :] * d_scr[off:off + T, :]
            df_ref[...] = acc.astype(BF16)
            dcur = d_scr[0:T, :]
            parts = [jnp.sum(dcur * x_scr[HR - (FFN_K - 1) + k:HR - (FFN_K - 1) + k + T, :], axis=0, keepdims=True)
                     for k in range(FFN_K)]
            dwp = _stack_rows(parts)
            dbp = jnp.sum(dcur, axis=0, keepdims=True)

            @pl.when(c == 0)
            def _():
                dw_ref[...] = dwp
                db_ref[...] = dbp

            @pl.when(c > 0)
            def _():
                dw_ref[...] += dwp
                db_ref[...] += dbp

        back(d_gate, xg_scr, wg_ref, dfg_ref, dwg_ref, dbg_ref)
        back(d_up, xu_scr, wu_ref, dfu_ref, dwu_ref, dbu_ref)

    cur, before, after, wspec = _ffn_specs()
    slab = jax.ShapeDtypeStruct((NPAIR, L, UPW), BF16)
    wsh = jax.ShapeDtypeStruct((NPAIR, FFN_K, UPW), F32)
    bsh = jax.ShapeDtypeStruct((NPAIR, 1, UPW), F32)
    return pl.pallas_call(
        body, name=name, out_shape=(slab, slab, wsh, wsh, bsh, bsh), grid=(NPAIR, NB),
        in_specs=[cur(0), before(0), after(0), cur(NPAIR), before(NPAIR), after(NPAIR), cur(0), after(0),
                  wspec(0, FFN_K), wspec(NPAIR, FFN_K), wspec(0, 1), wspec(NPAIR, 1)],
        out_specs=(cur(0), cur(0), wspec(0, FFN_K), wspec(0, FFN_K), wspec(0, 1), wspec(0, 1)),
        scratch_shapes=[pltpu.VMEM((HR + T + HR, UPW), F32), pltpu.VMEM((HR + T + HR, UPW), F32),
                        pltpu.VMEM((ext, UPW), F32)])(f, f, f, f, f, f, da, da, w, w, b, b)


def _mesh_pos():
    return lax.axis_index("x"), lax.axis_index("y"), lax.axis_index("c")


def _peer(pos, k):
    x, y, c = pos
    px = 1 - x if k & 4 else x
    py = 1 - y if k & 2 else y
    pc = 1 - c if k & 1 else c
    return (px, py, pc), 4 * px + 2 * py + pc


_CHIP_FLIPS = (4, 2, 6)
_HBM = pl.BlockSpec(memory_space=pl.ANY)


def all_gather(shards, name, place=()):
    nt = len(shards)
    na = nt + len(place)

    def body(*refs):
        ins, outs = refs[:na], refs[na:2 * na]
        send, recv, local = refs[2 * na:]
        pos = _mesh_pos()
        me = 4 * pos[0] + 2 * pos[1] + pos[2]
        sib, sib_id = _peer(pos, 1)

        def copy(t, k, block_id, to, src=None):
            dst = outs[t].at[block_id]
            return pltpu.make_async_remote_copy(
                src_ref=dst if src is None else src, dst_ref=dst, send_sem=send.at[t, k], recv_sem=recv.at[t, k],
                device_id=to, device_id_type=MESH)

        locals_ = [pltpu.make_async_copy(ins[t], outs[t].at[me], local.at[t]) for t in range(na)]
        for cp in locals_:
            cp.start()
        started = []
        for j, flip in enumerate(_CHIP_FLIPS):
            for t in range(nt):
                started.append(copy(t, 1 + j, me, _peer(pos, flip)[0], src=ins[t]))
        for t in range(nt):
            started.append(copy(t, 0, me, sib, src=ins[t]))
        for cp in started:
            cp.start()
        for j, flip in enumerate(_CHIP_FLIPS):
            _, pid = _peer(pos, flip)
            for t in range(nt):
                copy(t, 1 + j, pid, sib).wait_recv()
                fwd = copy(t, 4 + j, pid, sib)
                fwd.start()
                started.append(fwd)
        for t in range(nt):
            copy(t, 0, sib_id, sib).wait_recv()
        for j, flip in enumerate(_CHIP_FLIPS):
            _, pid = _peer(pos, flip | 1)
            for t in range(nt):
                copy(t, 4 + j, pid, sib).wait_recv()
        for cp in started:
            cp.wait_send()
        for cp in locals_:
            cp.wait()

    every = list(shards) + list(place)
    return pl.pallas_call(
        body, name=name,
        out_shape=tuple(jax.ShapeDtypeStruct((NDEV,) + s.shape, s.dtype) for s in every),
        in_specs=[_HBM] * na, out_specs=tuple([_HBM] * na),
        scratch_shapes=[pltpu.SemaphoreType.DMA((nt, NDEV - 1)), pltpu.SemaphoreType.DMA((nt, NDEV - 1)),
                        pltpu.SemaphoreType.DMA((na,))])(*every)


_SPLIT = dict(has_side_effects=pltpu.SideEffectType.DATAFLOW_SIDE_EFFECTING)
_SEM = pl.BlockSpec(memory_space=pltpu.SEMAPHORE)


def _exchange_start(name, srcs, lands, copies, after):
    ns, nl = len(srcs), len(lands)
    ncopy = len(copies(None, [None] * ns, [None] * nl))

    def body(*refs):
        src_refs, land_refs = refs[:ns], refs[ns:ns + nl]
        send, recv = refs[ns + nl + len(after):ns + nl + len(after) + 2]
        token = refs[-1]
        for i, (src, dst, peer) in enumerate(copies(_mesh_pos(), src_refs, land_refs)):
            pltpu.make_async_remote_copy(src_ref=src, dst_ref=dst, send_sem=send.at[i], recv_sem=recv.at[i],
                                         device_id=peer, device_id_type=MESH).start()
        token[...] = jnp.zeros_like(token)

    hbm = lambda a: pltpu.HBM(a.shape, a.dtype)
    out = pl.pallas_call(
        body, name=name,
        out_shape=(pltpu.SemaphoreType.DMA((ncopy,)), pltpu.SemaphoreType.DMA((ncopy,)),
                   *[hbm(a) for a in srcs], *[hbm(a) for a in lands], jax.ShapeDtypeStruct((8, 128), F32)),
        in_specs=[pl.BlockSpec(memory_space=pltpu.HBM)] * (ns + nl) + [_HBM] * len(after),
        out_specs=(_SEM, _SEM, *[pl.BlockSpec(memory_space=pltpu.HBM)] * (ns + nl),
                   pl.BlockSpec(memory_space=pltpu.VMEM)),
        input_output_aliases={i: 2 + i for i in range(ns + nl)},
        compiler_params=pltpu.CompilerParams(**_SPLIT))(
            *[pltpu.with_memory_space_constraint(a, pltpu.HBM) for a in list(srcs) + list(lands)], *after)
    return out[0], out[1], list(out[2:2 + ns]), list(out[2 + ns:2 + ns + nl]), out[-1]


def _exchange_wait(name, send, recv, srcs, lands, copies, after):
    ns, nl = len(srcs), len(lands)

    def body(*refs):
        src_refs, land_refs = refs[:ns], refs[ns:ns + nl]
        send_ref, recv_ref = refs[ns + nl:ns + nl + 2]
        pos = _mesh_pos()
        for i, (src, dst, peer) in enumerate(copies(pos, src_refs, land_refs, arriving=True)):
            cp = pltpu.make_async_remote_copy(src_ref=src, dst_ref=dst, send_sem=send_ref.at[i], recv_sem=recv_ref.at[i],
                                              device_id=peer, device_id_type=MESH)
            cp.wait_send()
            cp.wait_recv()

    hbm = lambda a: pltpu.HBM(a.shape, a.dtype)
    out = pl.pallas_call(
        body, name=name, out_shape=tuple(hbm(a) for a in list(srcs) + list(lands)),
        in_specs=[pl.BlockSpec(memory_space=pltpu.HBM)] * (ns + nl) + [_SEM, _SEM] + [_HBM] * len(after),
        out_specs=tuple([pl.BlockSpec(memory_space=pltpu.HBM)] * (ns + nl)),
        input_output_aliases={i: i for i in range(ns + nl)},
        compiler_params=pltpu.CompilerParams(**_SPLIT))(*srcs, *lands, send, recv, *after)
    return list(out[:ns]), list(out[ns:])


def _gather_copies(pos, srcs, lands, arriving=False):
    if pos is None:
        return [None] * (len(srcs) * (NDEV - 1))
    me = 4 * pos[0] + 2 * pos[1] + pos[2]
    out = []
    for src, land in zip(srcs, lands):
        for k in range(1, NDEV):
            peer, pid = _peer(pos, k)
            out.append((src, land.at[pid if arriving else me], peer))
    return out


def _scatter_copies(pos, srcs, lands, arriving=False):
    if pos is None:
        return [None] * (len(srcs) * (NDEV - 1))
    out = []
    for src, land in zip(srcs, lands):
        for k in range(1, NDEV):
            peer, pid = _peer(pos, k)
            out.append((src.at[pid], land.at[k - 1], peer))
    return out


def gather_start(name, shards, lands, after=()):
    return _exchange_start(name, shards, lands, _gather_copies, list(after))


def gather_wait(name, handle, after=()):
    send, recv, srcs, lands, _ = handle
    return _exchange_wait(name, send, recv, srcs, lands, _gather_copies, list(after))


def scatter_start(name, grads, after=()):
    lands = [pltpu.with_memory_space_constraint(lax.empty((NDEV - 1,) + g.shape[1:], g.dtype), pltpu.HBM)
             for g in grads]
    return _exchange_start(name, grads, lands, _scatter_copies, list(after))


def scatter_wait(name, handle, after=()):
    send, recv, srcs, lands, _ = handle
    return _exchange_wait(name, send, recv, srcs, lands, _scatter_copies, list(after))


def scatter_fused(grads, name):
    n = len(grads)

    def body(*refs):
        ins, outs = refs[:n], refs[n:2 * n]
        send, recv = refs[2 * n:]
        cps = [pltpu.make_async_remote_copy(src_ref=src, dst_ref=dst, send_sem=send.at[i], recv_sem=recv.at[i],
                                            device_id=peer, device_id_type=MESH)
               for i, (src, dst, peer) in enumerate(_scatter_copies(_mesh_pos(), ins, outs))]
        for cp in cps:
            cp.start()
        for cp in cps:
            cp.wait_recv()
        for cp in cps:
            cp.wait_send()

    return pl.pallas_call(
        body, name=name,
        out_shape=tuple(jax.ShapeDtypeStruct((NDEV - 1,) + g.shape[1:], g.dtype) for g in grads),
        in_specs=[_HBM] * n, out_specs=tuple([_HBM] * n),
        scratch_shapes=[pltpu.SemaphoreType.DMA((n * (NDEV - 1),)), pltpu.SemaphoreType.DMA((n * (NDEV - 1),))])(*grads)


def all_reduce_small(part, name):
    rows = part.shape[0]

    def body(x_ref, o_ref, buf, send, recv):
        pos = _mesh_pos()
        me = 4 * pos[0] + 2 * pos[1] + pos[2]
        sends = []
        for k in range(1, NDEV):
            peer, _ = _peer(pos, k)
            sends.append(pltpu.make_async_remote_copy(
                src_ref=x_ref, dst_ref=buf.at[me], send_sem=send.at[k - 1], recv_sem=recv.at[k - 1],
                device_id=peer, device_id_type=MESH))
        for cp in sends:
            cp.start()
        buf[me] = x_ref[...]
        for k in range(1, NDEV):
            peer, pid = _peer(pos, k)
            pltpu.make_async_remote_copy(
                src_ref=x_ref, dst_ref=buf.at[pid], send_sem=send.at[k - 1], recv_sem=recv.at[k - 1],
                device_id=peer, device_id_type=MESH).wait_recv()
        for cp in sends:
            cp.wait_send()
        acc = buf[0]
        for d in range(1, NDEV):
            acc = acc + buf[d]
        o_ref[...] = acc

    return pl.pallas_call(
        body, name=name, out_shape=jax.ShapeDtypeStruct((rows, 128), F32),
        in_specs=[pl.BlockSpec(memory_space=pltpu.VMEM)], out_specs=pl.BlockSpec(memory_space=pltpu.VMEM),
        scratch_shapes=[pltpu.VMEM((NDEV, rows, 128), F32), pltpu.SemaphoreType.DMA((NDEV - 1,)),
                        pltpu.SemaphoreType.DMA((NDEV - 1,))])(part)


def _adamw_update(g, w_ref, m_ref, v_ref, g_ref, d_ref, nm_ref, nv_ref):
    g_ref[...] = g
    nm = ADAM_B1 * m_ref[...] + (1.0 - ADAM_B1) * g
    nv = ADAM_B2 * v_ref[...] + (1.0 - ADAM_B2) * (g * g)
    nm_ref[...] = nm
    nv_ref[...] = nv
    m_hat = nm / (1.0 - ADAM_B1 ** ADAM_STEP)
    v_hat = nv / (1.0 - ADAM_B2 ** ADAM_STEP)
    d_ref[...] = -ADAM_LR * (m_hat / (jnp.sqrt(v_hat) + ADAM_EPS) + ADAM_WD * w_ref[...])


def adamw_big(own, lands, w, m, v, me, tr, name):
    _, r, c = own[0].shape
    nt = r // tr

    def body(me_ref, *refs):
        ins, (w_ref, m_ref, v_ref), outs = refs[:2 * DEPTH], refs[2 * DEPTH:2 * DEPTH + 3], refs[2 * DEPTH + 3:]
        layer = pl.program_id(0)
        for l in range(DEPTH):
            @pl.when(layer == l)
            def _(l=l):
                g = ins[2 * l][...].astype(F32)
                for s in range(NDEV - 1):
                    g = g + ins[2 * l + 1][s].astype(F32)
                _adamw_update(g, w_ref, m_ref, v_ref, *outs)

    pick = lambda l: (lambda layer, i, me_ref: jnp.where(layer == l, i, 0))
    in_specs = []
    for l in range(DEPTH):
        in_specs.append(pl.BlockSpec((None, tr, c), lambda layer, i, me_ref, f=pick(l): (me_ref[0], f(layer, i, me_ref), 0)))
        in_specs.append(pl.BlockSpec((NDEV - 1, tr, c), lambda layer, i, me_ref, f=pick(l): (0, f(layer, i, me_ref), 0)))
    blk = pl.BlockSpec((tr, c), lambda layer, i, me_ref: (layer * nt + i, 0))
    sh = jax.ShapeDtypeStruct((DEPTH * r, c), F32)
    args = [a for l in range(DEPTH) for a in (own[l], lands[l])]
    return pl.pallas_call(
        body, name=name, out_shape=(sh, sh, sh, sh),
        grid_spec=pltpu.PrefetchScalarGridSpec(
            num_scalar_prefetch=1, grid=(DEPTH, nt), in_specs=in_specs + [blk, blk, blk],
            out_specs=(blk, blk, blk, blk)),
        compiler_params=_params(VMEM_BIG))(me, *args, w, m, v)


def adamw(parts, w, m, v, tr, name):
    ns, r, c = parts.shape

    def body(p_ref, w_ref, m_ref, v_ref, g_ref, d_ref, nm_ref, nv_ref):
        g = p_ref[0].astype(F32)
        for s in range(1, ns):
            g = g + p_ref[s].astype(F32)
        _adamw_update(g, w_ref, m_ref, v_ref, g_ref, d_ref, nm_ref, nv_ref)

    blk = pl.BlockSpec((tr, c), lambda i: (i, 0))
    sh = jax.ShapeDtypeStruct((r, c), F32)
    return pl.pallas_call(
        body, name=name, out_shape=(sh, sh, sh, sh), grid=(r // tr,),
        in_specs=[pl.BlockSpec((ns, tr, c), lambda i: (0, i, 0)), blk, blk, blk],
        out_specs=(blk, blk, blk, blk), compiler_params=_params(VMEM_BIG))(parts, w, m, v)


_SPLITS = (NQ * HD, NKV * HD, NKV * HD, CV, CV, RH * HD, RH * HD, RH * HD, RH * HD)
_OFFS = np.concatenate([[0], np.cumsum(_SPLITS)]).tolist()


def _heads(x, nh):
    return x.reshape(L, nh, HD).transpose(1, 0, 2)


def _unheads(x):
    return x.transpose(1, 0, 2).reshape(L, x.shape[0] * HD)


def _split_proj(proj):
    return [proj[:, _OFFS[i]:_OFFS[i + 1]] for i in range(len(_SPLITS))]


def local_step(x, target, P, hooks):
    tb = _ret_tables()
    h = jnp.concatenate([jnp.zeros((PAD, D), F32), P["meta"], x], axis=0)
    stash = []
    for l in range(DEPTH):
        s = {"h": h}
        s["w_in"], s["cv_pw"], s["w_out"] = hooks["mixer_w"](l, h)
        u = rms_fwd(h, P["norm_mix_g"][l], BF16, f"rms_mix_{l}", after=hooks["first_after"] if l == 0 else None)
        s["ut"] = u.T
        proj = mm_nn(f"in_proj_{l}", u, s["w_in"], 256)
        q, k, v, ca, cb, rq, rk, rv, rg = _split_proj(proj)
        s.update(q=_heads(q, NQ), k=_heads(k, NKV), v=_heads(v, NKV), ca=ca, cb=cb, rq=_heads(rq, RH),
                 rk=_heads(rk, RH), rv=_heads(rv, RH), rg=_heads(rg, RH))
        s["rkt"] = s["rk"].transpose(0, 2, 1)
        s["rqt"] = s["rq"].transpose(0, 2, 1)
        o = attn_fwd(s["q"], s["k"], s["v"], P["q_norm_g"][l], P["k_norm_g"][l], P["attn_sinks"][l], f"attn_fwd_{l}")
        s["o"] = _unheads(o)
        y_att = rms_fwd(s["o"], P["attn_out_g"][l], BF16, f"rms_att_{l}")
        y_cv = conv_fwd(ca, cb, P["cv_dw_w"][l], P["cv_dw_b"][l], P["cv_ln_g"][l], P["cv_ln_b"][l], s["cv_pw"],
                        P["cv_out_g"][l], f"conv_fwd_{l}")
        y_ret, s["states"] = ret_fwd(s["rq"], s["rkt"], s["rv"], s["rg"], P["ret_gn_g"][l], tb, f"ret_fwd_{l}")
        ycat = jnp.concatenate([y_att, y_cv.astype(BF16), _unheads(y_ret).astype(BF16)], axis=1)
        s["ycat_t"] = ycat.T
        h = mm_nn(f"out_proj_{l}", ycat, s["w_out"], 256, res=h)
        s["h1"] = h
        s["ffn_up"], s["ffn_down"] = hooks["ffn_w"](l, h)
        u2 = rms_fwd(h, P["norm_ffn_g"][l], BF16, f"rms_ffn_{l}")
        s["u2t"] = u2.T
        s["f"] = ffn_up_fwd(f"ffn_up_{l}", u2, s["ffn_up"])
        a = ffn_act_fwd(s["f"], P["ffn_dw_w"][l], P["ffn_dw_b"][l], f"ffn_act_fwd_{l}")
        s["at"] = a.transpose(0, 2, 1)
        h = ffn_down_fwd(f"ffn_down_{l}", a, s["ffn_down"], h)
        stash.append(s)

    dh, dhm, loss = loss_head(h, target)
    G = {n: [None] * DEPTH for n in ("norm_mix_g", "q_norm_g", "k_norm_g", "attn_sinks", "attn_out_g", "cv_dw_w",
                                     "cv_dw_b", "cv_ln_g", "cv_ln_b", "cv_out_g", "ret_gn_g", "norm_ffn_g",
                                     "ffn_dw_w", "ffn_dw_b")}
    for l in reversed(range(DEPTH)):
        s = stash[l]
        da = ffn_down_dx(f"ffn_down_dx_{l}", dhm, s["ffn_down"])
        dw_down = ffn_down_dw(f"ffn_down_dw_{l}", s["at"], dhm)
        dfg, dfu, dwg, dwu, dbg, dbu = ffn_act_bwd(s["f"], da, P["ffn_dw_w"][l], P["ffn_dw_b"][l], f"ffn_act_bwd_{l}")
        df = jnp.concatenate([dfg, dfu], axis=0)
        G["ffn_dw_w"][l] = jnp.concatenate([dwg, dwu], axis=0)
        G["ffn_dw_b"][l] = jnp.concatenate([dbg, dbu], axis=0)
        du2 = ffn_up_dx(f"ffn_up_dx_{l}", df, s["ffn_up"])
        dw_up = ffn_up_dw(f"ffn_up_dw_{l}", s["u2t"], df)
        after = hooks["ffn_grads"](l, dw_down, dw_up)
        dh, dhm, G["norm_ffn_g"][l] = rms_bwd(s["h1"], P["norm_ffn_g"][l], du2, dh, f"rms_ffn_bwd_{l}", after=after)
        dycat = mm_nt(f"out_proj_dx_{l}", dhm, s["w_out"], 256)
        dw_out = mm_nn(f"out_proj_dw_{l}", s["ycat_t"], dhm, 256, out_dtype=BF16)
        dy_att, dy_cv, dy_ret = dycat[:, :NQ * HD], dycat[:, NQ * HD:NQ * HD + CV], dycat[:, NQ * HD + CV:]
        do, _, G["attn_out_g"][l] = rms_bwd(s["o"], P["attn_out_g"][l], dy_att, None, f"rms_att_bwd_{l}")
        (dq, dk_c, dk_p, dk_m, dv_c, dv_p, dv_m, G["q_norm_g"][l], G["k_norm_g"][l], dsk) = attn_bwd(
            s["q"], s["k"], s["v"], _heads(do, NQ), P["q_norm_g"][l], P["k_norm_g"][l], P["attn_sinks"][l],
            f"attn_bwd_{l}")
        G["attn_sinks"][l] = dsk[:, :, 0].reshape(NQ)
        shift = lambda z: jnp.concatenate([z[:, T:], jnp.zeros((NKV, T, HD), F32)], axis=1)
        dk = (dk_c + shift(dk_p)).at[:, :T].add(dk_m)
        dv = (dv_c + shift(dv_p)).at[:, :T].add(dv_m)
        (dca, dcb, G["cv_dw_w"][l], G["cv_dw_b"][l], G["cv_ln_g"][l], G["cv_ln_b"][l], G["cv_out_g"][l],
         dpw) = conv_bwd(s["ca"], s["cb"], dy_cv, P["cv_dw_w"][l], P["cv_dw_b"][l], P["cv_ln_g"][l],
                         P["cv_ln_b"][l], s["cv_pw"], P["cv_out_g"][l], f"conv_bwd_{l}")
        drq, drk, drv, drg, G["ret_gn_g"][l] = ret_bwd(
            s["rq"], s["rqt"], s["rk"], s["rkt"], s["rv"], s["rg"], _heads(dy_ret, RH), s["states"],
            P["ret_gn_g"][l], tb, f"ret_bwd_{l}")
        dproj = jnp.concatenate([_unheads(dq), _unheads(dk), _unheads(dv), dca, dcb, _unheads(drq), _unheads(drk),
                                 _unheads(drv), _unheads(drg)], axis=1).astype(BF16)
        du = mm_nt(f"in_proj_dx_{l}", dproj, s["w_in"], 256)
        dw_in = mm_nn(f"in_proj_dw_{l}", s["ut"], dproj, 256, out_dtype=BF16)
        after = hooks["mixer_grads"](l, dw_out, dpw, dw_in)
        dh, dhm, G["norm_mix_g"][l] = rms_bwd(s["h"], P["norm_mix_g"][l], du, dh, f"rms_mix_bwd_{l}", after=after)
    return loss[0, 0], dh[T:], dh[PAD:T], G


_SMALL = ("meta", "norm_mix_g", "q_norm_g", "k_norm_g", "attn_sinks", "attn_out_g", "cv_dw_w", "cv_dw_b", "cv_ln_g",
          "cv_ln_b", "cv_out_g", "ret_gn_g", "norm_ffn_g", "ffn_dw_w", "ffn_dw_b")
_BIG = ("w_in", "cv_pw", "w_out", "ffn_up", "ffn_down")
_ORDER = ("meta", "norm_mix_g", "w_in", "q_norm_g", "k_norm_g", "attn_sinks", "attn_out_g", "cv_dw_w", "cv_dw_b",
          "cv_ln_g", "cv_ln_b", "cv_pw", "cv_out_g", "ret_gn_g", "w_out", "norm_ffn_g", "ffn_up", "ffn_dw_w",
          "ffn_dw_b", "ffn_down")
_SMALL_SHARDED = {"meta": D, "cv_dw_w": CV, "ffn_dw_w": 2 * D_FF}


def _pack(arrs):
    flat = jnp.concatenate([a.reshape(-1) for a in arrs])
    n = flat.shape[0]
    rows = -(-n // 1024) * 8
    return jnp.pad(flat, (0, rows * 128 - n)).reshape(rows, 128)


def _unpack(packed, shapes):
    flat = packed.reshape(-1)
    out, off = [], 0
    for s in shapes:
        n = int(np.prod(s))
        out.append(flat[off:off + n].reshape(s))
        off += n
    return out


def kernel(x, meta, norm_mix_g, w_in, q_norm_g, k_norm_g, attn_sinks, attn_out_g, cv_dw_w, cv_dw_b, cv_ln_g, cv_ln_b, cv_pw, cv_out_g, ret_gn_g, w_out, norm_ffn_g, ffn_up, ffn_dw_w, ffn_dw_b, ffn_down, loss_target, m_meta, m_norm_mix_g, m_w_in, m_q_norm_g, m_k_norm_g, m_attn_sinks, m_attn_out_g, m_cv_dw_w, m_cv_dw_b, m_cv_ln_g, m_cv_ln_b, m_cv_pw, m_cv_out_g, m_ret_gn_g, m_w_out, m_norm_ffn_g, m_ffn_up, m_ffn_dw_w, m_ffn_dw_b, m_ffn_down, v_meta, v_norm_mix_g, v_w_in, v_q_norm_g, v_k_norm_g, v_attn_sinks, v_attn_out_g, v_cv_dw_w, v_cv_dw_b, v_cv_ln_g, v_cv_ln_b, v_cv_pw, v_cv_out_g, v_ret_gn_g, v_w_out, v_norm_ffn_g, v_ffn_up, v_ffn_dw_w, v_ffn_dw_b, v_ffn_down):
    W = dict(meta=meta, norm_mix_g=norm_mix_g, w_in=w_in, q_norm_g=q_norm_g, k_norm_g=k_norm_g,
             attn_sinks=attn_sinks, attn_out_g=attn_out_g, cv_dw_w=cv_dw_w, cv_dw_b=cv_dw_b, cv_ln_g=cv_ln_g,
             cv_ln_b=cv_ln_b, cv_pw=cv_pw, cv_out_g=cv_out_g, ret_gn_g=ret_gn_g, w_out=w_out,
             norm_ffn_g=norm_ffn_g, ffn_up=ffn_up, ffn_dw_w=ffn_dw_w, ffn_dw_b=ffn_dw_b, ffn_down=ffn_down)
    M = dict(meta=m_meta, norm_mix_g=m_norm_mix_g, w_in=m_w_in, q_norm_g=m_q_norm_g, k_norm_g=m_k_norm_g,
             attn_sinks=m_attn_sinks, attn_out_g=m_attn_out_g, cv_dw_w=m_cv_dw_w, cv_dw_b=m_cv_dw_b,
             cv_ln_g=m_cv_ln_g, cv_ln_b=m_cv_ln_b, cv_pw=m_cv_pw, cv_out_g=m_cv_out_g, ret_gn_g=m_ret_gn_g,
             w_out=m_w_out, norm_ffn_g=m_norm_ffn_g, ffn_up=m_ffn_up, ffn_dw_w=m_ffn_dw_w, ffn_dw_b=m_ffn_dw_b,
             ffn_down=m_ffn_down)
    V = dict(meta=v_meta, norm_mix_g=v_norm_mix_g, w_in=v_w_in, q_norm_g=v_q_norm_g, k_norm_g=v_k_norm_g,
             attn_sinks=v_attn_sinks, attn_out_g=v_attn_out_g, cv_dw_w=v_cv_dw_w, cv_dw_b=v_cv_dw_b,
             cv_ln_g=v_cv_ln_g, cv_ln_b=v_cv_ln_b, cv_pw=v_cv_pw, cv_out_g=v_cv_out_g, ret_gn_g=v_ret_gn_g,
             w_out=v_w_out, norm_ffn_g=v_norm_ffn_g, ffn_up=v_ffn_up, ffn_dw_w=v_ffn_dw_w, ffn_dw_b=v_ffn_dw_b,
             ffn_down=v_ffn_down)
    me = 4 * lax.axis_index("x") + 2 * lax.axis_index("y") + lax.axis_index("c")

    sh = {n: [W[n][l].astype(BF16) for l in range(DEPTH)] for n in _BIG}
    mix = lambda l: [sh["w_in"][l], sh["cv_pw"][l], sh["w_out"][l]]
    ffn = lambda l: [sh["ffn_up"][l], sh["ffn_down"][l]]
    first = all_gather(mix(0) + [meta, cv_dw_w, ffn_dw_w], "gather_first", place=ffn(0) + mix(1) + ffn(1))
    g_meta, g_cdw, g_fdw = first[3:6]
    gathers = {("ffn", 0): gather_start("gather_ffn0_start", ffn(0), first[6:8])}
    gathers["mix", 1] = gather_start("gather_mix1_start", mix(1), first[8:11], after=[gathers["ffn", 0][4]])
    gathers["ffn", 1] = gather_start("gather_ffn1_start", ffn(1), first[11:13], after=[gathers["mix", 1][4]])

    def mixer_w(l, h):
        g_in, g_pw, g_out = first[0:3] if l == 0 else gather_wait(f"gather_mix{l}_wait", gathers["mix", l], after=[h])[1]
        return g_in.transpose(1, 0, 2).reshape(D, IN_W), g_pw.reshape(CV, CV), g_out.reshape(D, D)

    def ffn_w(l, h1):
        return gather_wait(f"gather_ffn{l}_wait", gathers["ffn", l], after=[h1])[1]

    scatters, last = {}, {}

    def ffn_grads(l, dw_down, dw_up):
        scatters["ffn", l] = scatter_start(f"scatter_ffn{l}_start", [dw_up, dw_down.reshape(NDEV, DNR, D)])
        return scatters["ffn", l][4]

    def mixer_grads(l, dw_out, dpw, dw_in):
        grads = [dw_in.reshape(D, NDEV, IN_W // NDEV).transpose(1, 0, 2), dpw.astype(BF16).reshape(NDEV, CV // NDEV, CV),
                 dw_out.reshape(NDEV, D // NDEV, D)]
        if l == 0:
            last["grads"] = grads
            return None
        scatters["mix", l] = scatter_start(f"scatter_mix{l}_start", grads)
        return scatters["mix", l][4]

    P = dict(
        meta=g_meta.transpose(1, 0, 2).reshape(N_META, D),
        cv_dw_w=g_cdw.transpose(1, 2, 0, 3).reshape(DEPTH, CONV_K, CV),
        ffn_dw_w=g_fdw.transpose(1, 0, 2, 3),
        ffn_dw_b=ffn_dw_b.reshape(DEPTH, NDEV, 1, UPW),
        attn_sinks=attn_sinks,
        ret_gn_g=ret_gn_g.reshape(DEPTH, RH, 1, HD),
    )
    for n in ("norm_mix_g", "q_norm_g", "k_norm_g", "attn_out_g", "cv_dw_b", "cv_ln_g", "cv_ln_b", "cv_out_g",
              "norm_ffn_g"):
        P[n] = W[n].reshape(DEPTH, 1, -1)

    hooks = dict(mixer_w=mixer_w, ffn_w=ffn_w, ffn_grads=ffn_grads, mixer_grads=mixer_grads,
                 first_after=gathers["ffn", 1][4])
    loss_part, dx, dmeta, G = local_step(x[0], loss_target[0], P, hooks)
    loss = lax.psum(loss_part, ("x", "y", "c"))

    own, lands = {n: [None] * DEPTH for n in _BIG}, {n: [None] * DEPTH for n in _BIG}
    for (kind, l), handle in scatters.items():
        names = ("ffn_up", "ffn_down") if kind == "ffn" else ("w_in", "cv_pw", "w_out")
        srcs, got = scatter_wait(f"scatter_{kind}{l}_wait", handle, after=[dx])
        for n, s_, g_ in zip(names, srcs, got):
            own[n][l], lands[n][l] = s_, g_
    for n, s_, g_ in zip(("w_in", "cv_pw", "w_out"), last["grads"], scatter_fused(last["grads"], "scatter_mix0")):
        own[n][0], lands[n][0] = s_, g_

    small_full = {
        "meta": dmeta,
        "cv_dw_w": jnp.stack(G["cv_dw_w"]),
        "ffn_dw_w": jnp.stack([g.transpose(1, 0, 2).reshape(FFN_K, 2 * D_FF) for g in G["ffn_dw_w"]]),
        "ffn_dw_b": jnp.stack([g.reshape(2 * D_FF) for g in G["ffn_dw_b"]]),
        "attn_sinks": jnp.stack(G["attn_sinks"]),
        "ret_gn_g": jnp.stack([g.reshape(RH * HD) for g in G["ret_gn_g"]]),
    }
    for n in _SMALL:
        if n not in small_full:
            small_full[n] = jnp.stack([g.reshape(-1) for g in G[n]])
    shapes = [small_full[n].shape for n in _SMALL]
    summed = _unpack(all_reduce_small(_pack([small_full[n] for n in _SMALL]), "reduce_small"), shapes)
    small_g = {}
    for n, g in zip(_SMALL, summed):
        if n in _SMALL_SHARDED:
            width = _SMALL_SHARDED[n] // NDEV
            g = lax.dynamic_slice_in_dim(g, me * width, width, axis=g.ndim - 1)
        small_g[n] = g

    out = {}
    tiles = {"w_in": 256, "cv_pw": 32, "w_out": 128, "ffn_up": 256, "ffn_down": 176}
    me1 = me.astype(jnp.int32).reshape(1)
    for n in _BIG:
        shard = W[n].shape
        rows, cols = shard[0] * shard[1], shard[2]
        res = adamw_big(own[n], lands[n], W[n].reshape(rows, cols), M[n].reshape(rows, cols),
                        V[n].reshape(rows, cols), me1, tiles[n], f"adamw_{n}")
        out[n] = [r.reshape(shard) for r in res]
    sshapes = [W[n].shape for n in _SMALL]
    packed_g = _pack([small_g[n] for n in _SMALL])
    res = adamw(packed_g[None], _pack([W[n] for n in _SMALL]), _pack([M[n] for n in _SMALL]),
                _pack([V[n] for n in _SMALL]), packed_g.shape[0], "adamw_small")
    for i, r in enumerate(res):
        for n, a in zip(_SMALL, _unpack(r, sshapes)):
            out.setdefault(n, [None] * 4)[i] = a

    return (loss, dx[None], *[out[n][0] for n in _ORDER], *[out[n][1] for n in _ORDER],
            *[out[n][2] for n in _ORDER], *[out[n][3] for n in _ORDER])
```

```python
import functools
import math

import numpy as np
import jax
import jax.numpy as jnp
from jax import lax
from jax.experimental import pallas as pl
from jax.experimental.pallas import tpu as pltpu

F32 = jnp.float32
BF16 = jnp.bfloat16

D = 1024
SEQ = 2048
DEPTH = 2
T = 128
L = SEQ + T
NB = L // T
N_META = 16
PAD = T - N_META
HD = 64
NQ = 8
NKV = 2
GQA = NQ // NKV
CV = 256
CONV_K = 31
RH = 4
D_FF = 2816
FFN_K = 3
IN_W = 2304
RMS_EPS = 1e-6
LN_EPS = 1e-5
NEG = -1e30
NDEV = 8
UPW = 2 * D_FF // NDEV
DNR = D_FF // NDEV
NPAIR = NDEV // 2

ADAM_LR, ADAM_B1, ADAM_B2, ADAM_EPS, ADAM_WD, ADAM_STEP = 0.001, 0.9, 0.999, 1e-08, 0.01, 10

VMEM_BIG = 56 * 1024 * 1024

MESH = pl.DeviceIdType.MESH


def _params(vmem=None):
    return pltpu.CompilerParams(vmem_limit_bytes=vmem) if vmem else None


def _dot(a, b, nt=False):
    return lax.dot_general(a, b, (((1,), (1 if nt else 0,)), ((), ())), preferred_element_type=F32)


def _sig(x):
    return 1.0 / (1.0 + jnp.exp(-x))


def _bf(x):
    return x.astype(BF16)


def _stack_rows(rows):
    idx = lax.broadcasted_iota(jnp.int32, (len(rows), rows[0].shape[1]), 0)
    out = jnp.zeros((len(rows), rows[0].shape[1]), F32)
    for r, v in enumerate(rows):
        out = jnp.where(idx == r, v, out)
    return out


def rms_fwd(x, g, out_dtype, name, after=None):
    n, w = x.shape

    def body(x_ref, g_ref, *rest):
        o_ref = rest[-1]
        xv = x_ref[...]
        r = lax.rsqrt(jnp.mean(xv * xv, axis=-1, keepdims=True) + RMS_EPS)
        o_ref[...] = (xv * r * g_ref[...]).astype(o_ref.dtype)

    deps = [] if after is None else [after]
    return pl.pallas_call(
        body, name=name, out_shape=jax.ShapeDtypeStruct((n, w), out_dtype), grid=(n // T,),
        in_specs=[pl.BlockSpec((T, w), lambda i: (i, 0)), pl.BlockSpec((1, w), lambda i: (0, 0))] + [_HBM] * len(deps),
        out_specs=pl.BlockSpec((T, w), lambda i: (i, 0)))(x, g, *deps)


def rms_bwd(x, g, dy, dres, name, after=None):
    n, w = x.shape
    has_res = dres is not None
    deps = [] if after is None else [after]

    def body(x_ref, g_ref, dy_ref, *rest):
        rest = rest[:len(rest) - 3 - len(deps)] + rest[len(rest) - 3:]
        if has_res:
            dres_ref, dx_ref, dxm_ref, dg_ref = rest
        else:
            dx_ref, dxm_ref, dg_ref = rest
        i = pl.program_id(0)
        xv = x_ref[...]
        r = lax.rsqrt(jnp.mean(xv * xv, axis=-1, keepdims=True) + RMS_EPS)
        xh = xv * r
        dyv = dy_ref[...]
        dxh = dyv * g_ref[...]
        dx = r * (dxh - xh * jnp.mean(dxh * xh, axis=-1, keepdims=True))
        if has_res:
            dx = dx + dres_ref[...]
        dx_ref[...] = dx
        rows = i * T + lax.broadcasted_iota(jnp.int32, (T, 1), 0)
        dxm_ref[...] = jnp.where(rows >= PAD, dx, 0.0).astype(BF16)
        part = jnp.sum(dyv * xh, axis=0, keepdims=True)

        @pl.when(i == 0)
        def _():
            dg_ref[...] = part

        @pl.when(i > 0)
        def _():
            dg_ref[...] += part

    row = pl.BlockSpec((T, w), lambda i: (i, 0))
    vec = pl.BlockSpec((1, w), lambda i: (0, 0))
    ins = [x, g, dy] + ([dres] if has_res else []) + deps
    return pl.pallas_call(
        body, name=name,
        out_shape=(jax.ShapeDtypeStruct((n, w), F32), jax.ShapeDtypeStruct((n, w), BF16),
                   jax.ShapeDtypeStruct((1, w), F32)),
        grid=(n // T,),
        in_specs=[row, vec, row] + ([row] if has_res else []) + [_HBM] * len(deps),
        out_specs=(row, row, vec))(*ins)


def loss_head(h, target):
    def body(h_ref, t_ref, dh_ref, dhm_ref, loss_ref):
        n = pl.program_id(0)
        e = jnp.where(n > 0, h_ref[...] - t_ref[...], 0.0)
        dh = e * (1.0 / D)
        dh_ref[...] = dh
        dhm_ref[...] = dh.astype(BF16)
        part = jnp.sum(jnp.sum(e * e, axis=1, keepdims=True), axis=0, keepdims=True) * (0.5 / D)

        @pl.when(n == 0)
        def _():
            loss_ref[...] = jnp.zeros_like(loss_ref)

        @pl.when(n > 0)
        def _():
            loss_ref[...] += jnp.broadcast_to(part, loss_ref.shape)

    row = pl.BlockSpec((T, D), lambda n: (n, 0))
    return pl.pallas_call(
        body, name="loss_head",
        out_shape=(jax.ShapeDtypeStruct((L, D), F32), jax.ShapeDtypeStruct((L, D), BF16),
                   jax.ShapeDtypeStruct((8, 128), F32)),
        grid=(NB,),
        in_specs=[row, pl.BlockSpec((T, D), lambda n: (jnp.maximum(n - 1, 0), 0))],
        out_specs=(row, row, pl.BlockSpec((8, 128), lambda n: (0, 0))))(h, target)


def _mm(name, a, b, *, grid, a_spec, b_spec, o_spec, out_shape, nt=False, red=False, res=None, res_spec=None):
    def body(a_ref, b_ref, *rest):
        o_ref = rest[-1]
        av = a_ref[...]
        bv = b_ref[...]
        if bv.ndim == 3:
            bv = bv.reshape(bv.shape[0] * bv.shape[1], bv.shape[2])
        acc = _dot(av, bv, nt)
        if red:
            k = pl.program_id(0)

            @pl.when(k == 0)
            def _():
                o_ref[...] = acc

            @pl.when(k > 0)
            def _():
                o_ref[...] += acc
        else:
            if res is not None:
                rows = lax.broadcasted_iota(jnp.int32, (acc.shape[0], 1), 0)
                acc = rest[0][...] + jnp.where(rows >= PAD, acc, 0.0)
            o_ref[...] = acc.astype(o_ref.dtype)

    ins = [a, b] + ([res] if res is not None else [])
    specs = [a_spec, b_spec] + ([res_spec] if res is not None else [])
    return pl.pallas_call(body, name=name, out_shape=out_shape, grid=grid, in_specs=specs, out_specs=o_spec,
                          compiler_params=_params(VMEM_BIG))(*ins)


def mm_nn(name, a, b, tn, out_dtype=F32, res=None):
    m, k = a.shape
    n = b.shape[1]
    return _mm(name, a, b, grid=(n // tn,),
               a_spec=pl.BlockSpec((m, k), lambda j: (0, 0)), b_spec=pl.BlockSpec((k, tn), lambda j: (0, j)),
               o_spec=pl.BlockSpec((m, tn), lambda j: (0, j)), out_shape=jax.ShapeDtypeStruct((m, n), out_dtype),
               res=res, res_spec=pl.BlockSpec((m, tn), lambda j: (0, j)))


def mm_nt(name, a, b, tn):
    m, k = a.shape
    n = b.shape[0]
    return _mm(name, a, b, grid=(n // tn,), nt=True,
               a_spec=pl.BlockSpec((m, k), lambda j: (0, 0)), b_spec=pl.BlockSpec((tn, k), lambda j: (j, 0)),
               o_spec=pl.BlockSpec((m, tn), lambda j: (0, j)), out_shape=jax.ShapeDtypeStruct((m, n), F32))


def ffn_up_fwd(name, u2, wup):
    return _mm(name, u2, wup, grid=(NDEV,),
               a_spec=pl.BlockSpec((L, D), lambda j: (0, 0)),
               b_spec=pl.BlockSpec((None, D, UPW), lambda j: (j, 0, 0)),
               o_spec=pl.BlockSpec((None, L, UPW), lambda j: (j, 0, 0)),
               out_shape=jax.ShapeDtypeStruct((NDEV, L, UPW), F32))


def ffn_up_dx(name, df, wup):
    return _mm(name, df, wup, grid=(NDEV,), nt=True, red=True,
               a_spec=pl.BlockSpec((None, L, UPW), lambda j: (j, 0, 0)),
               b_spec=pl.BlockSpec((None, D, UPW), lambda j: (j, 0, 0)),
               o_spec=pl.BlockSpec((L, D), lambda j: (0, 0)),
               out_shape=jax.ShapeDtypeStruct((L, D), F32))


def ffn_up_dw(name, u2t, df):
    return _mm(name, u2t, df, grid=(NDEV,),
               a_spec=pl.BlockSpec((D, L), lambda j: (0, 0)),
               b_spec=pl.BlockSpec((None, L, UPW), lambda j: (j, 0, 0)),
               o_spec=pl.BlockSpec((None, D, UPW), lambda j: (j, 0, 0)),
               out_shape=jax.ShapeDtypeStruct((NDEV, D, UPW), BF16))


def ffn_down_fwd(name, a, wdn, res, tn=256):
    def body(a_ref, b_ref, r_ref, o_ref):
        acc = jnp.zeros((L, tn), F32)
        for g in range(NPAIR):
            bv = b_ref[2 * g:2 * g + 2]
            acc = acc + _dot(a_ref[g], bv.reshape(2 * DNR, tn))
        rows = lax.broadcasted_iota(jnp.int32, (L, 1), 0)
        o_ref[...] = r_ref[...] + jnp.where(rows >= PAD, acc, 0.0)

    return pl.pallas_call(
        body, name=name, out_shape=jax.ShapeDtypeStruct((L, D), F32), grid=(D // tn,),
        in_specs=[pl.BlockSpec((NPAIR, L, UPW), lambda j: (0, 0, 0)),
                  pl.BlockSpec((NDEV, DNR, tn), lambda j: (0, 0, j)),
                  pl.BlockSpec((L, tn), lambda j: (0, j))],
        out_specs=pl.BlockSpec((L, tn), lambda j: (0, j)),
        compiler_params=_params(VMEM_BIG))(a, wdn, res)


def ffn_down_dx(name, dh, wdn):
    return _mm(name, dh, wdn, grid=(NPAIR,), nt=True,
               a_spec=pl.BlockSpec((L, D), lambda g: (0, 0)),
               b_spec=pl.BlockSpec((2, DNR, D), lambda g: (g, 0, 0)),
               o_spec=pl.BlockSpec((None, L, UPW), lambda g: (g, 0, 0)),
               out_shape=jax.ShapeDtypeStruct((NPAIR, L, UPW), F32))


def ffn_down_dw(name, at, dh):
    return _mm(name, at, dh, grid=(NPAIR,),
               a_spec=pl.BlockSpec((None, UPW, L), lambda g: (g, 0, 0)),
               b_spec=pl.BlockSpec((L, D), lambda g: (0, 0)),
               o_spec=pl.BlockSpec((UPW, D), lambda g: (g, 0)),
               out_shape=jax.ShapeDtypeStruct((D_FF, D), BF16))


_SLOPES = [2.0 ** (-8.0 * (h + 1) / NQ) for h in range(NQ)]
_SCALE = HD ** -0.5


QR = GQA * T
KC = 3 * T


def _attn_geometry(g, n, sink_ref):
    row = lax.broadcasted_iota(jnp.int32, (QR, KC), 0)
    col = lax.broadcasted_iota(jnp.int32, (QR, KC), 1)
    i = row & (T - 1)
    j = col & (T - 1)
    blk = col // T
    d_meta = n * T + i - j
    ok_meta = (j >= PAD) & (d_meta >= 0)
    ok_prev = j > i + jnp.where(n >= 2, 0, T)
    ok_cur = j <= i - jnp.where(n >= 1, 0, T)
    ok = ((blk == 0) & ok_meta) | ((blk == 1) & ok_prev) | ((blk == 2) & ok_cur)
    dist = jnp.where(blk == 0, jnp.minimum(d_meta, T), jnp.where(blk == 1, T + i - j, i - j)).astype(F32)
    head = lax.broadcasted_iota(jnp.int32, (QR, 1), 0) // T
    slope = jnp.zeros((QR, 1), F32)
    sink = jnp.zeros((QR, 1), F32)
    for hh in range(GQA):
        slope = jnp.where(head == hh, jnp.where(g == 0, _SLOPES[hh], _SLOPES[GQA + hh]), slope)
        sink = jnp.where(head == hh, sink_ref[g * GQA + hh], sink)
    return ok, slope * dist, sink


def _rms_rows(x, g):
    r = lax.rsqrt(jnp.mean(x * x, axis=-1, keepdims=True) + RMS_EPS)
    xh = x * r
    return xh * g, xh, r


def _rms_rows_bwd(dy, g, xh, r):
    dxh = dy * g
    return r * (dxh - xh * jnp.mean(dxh * xh, axis=-1, keepdims=True))


def _attn_specs():
    qspec = pl.BlockSpec((GQA, T, HD), lambda g, n: (g, n, 0))
    cur = pl.BlockSpec((None, T, HD), lambda g, n: (g, n, 0))
    prev = pl.BlockSpec((None, T, HD), lambda g, n: (g, jnp.maximum(n - 1, 0), 0))
    meta = pl.BlockSpec((None, T, HD), lambda g, n: (g, 0, 0))
    vec = pl.BlockSpec((1, HD), lambda g, n: (0, 0))
    return qspec, cur, prev, meta, vec


def _attn_probs(qn, keys, geo):
    ok, penalty, sink = geo
    s = jnp.where(ok, _dot(qn, keys, nt=True) * _SCALE - penalty, NEG)
    m = jnp.maximum(jnp.max(s, axis=-1, keepdims=True), sink)
    e = jnp.exp(s - m)
    e_sink = jnp.exp(sink - m)
    inv = 1.0 / (e.sum(axis=-1, keepdims=True) + e_sink)
    return e, inv, e_sink * inv


def _rows3(m_ref, p_ref, c_ref):
    return jnp.concatenate([m_ref[...], p_ref[...], c_ref[...]], axis=0)


def attn_fwd(q, k, v, qg, kg, sinks, name):
    def body(sink_ref, q_ref, kc_ref, kp_ref, km_ref, vc_ref, vp_ref, vm_ref, qg_ref, kg_ref, o_ref):
        g = pl.program_id(0)
        n = pl.program_id(1)
        geo = _attn_geometry(g, n, sink_ref)
        keys = _bf(_rms_rows(_rows3(km_ref, kp_ref, kc_ref), kg_ref[...])[0])
        vals = _bf(_rows3(vm_ref, vp_ref, vc_ref))
        qn = _bf(_rms_rows(q_ref[...].reshape(QR, HD), qg_ref[...])[0])
        e, inv, _ = _attn_probs(qn, keys, geo)
        o_ref[...] = (_dot(_bf(e), vals) * inv).reshape(GQA, T, HD)

    qspec, cur, prev, meta, vec = _attn_specs()
    return pl.pallas_call(
        body, name=name, out_shape=jax.ShapeDtypeStruct((NQ, L, HD), F32), grid=(NKV, NB),
        in_specs=[pl.BlockSpec(memory_space=pltpu.SMEM), qspec, cur, prev, meta, cur, prev, meta, vec, vec],
        out_specs=qspec)(sinks, q, k, k, k, v, v, v, qg, kg)


def attn_bwd(q, k, v, do, qg, kg, sinks, name):
    def body(sink_ref, q_ref, kc_ref, kp_ref, km_ref, vc_ref, vp_ref, vm_ref, do_ref, qg_ref, kg_ref,
             dq_ref, dkc_ref, dkp_ref, dkm_ref, dvc_ref, dvp_ref, dvm_ref, dqg_ref, dkg_ref, dsk_ref):
        g = pl.program_id(0)
        n = pl.program_id(1)
        geo = _attn_geometry(g, n, sink_ref)
        kgv = kg_ref[...]
        qgv = qg_ref[...]
        kn_f, kh, kr = _rms_rows(_rows3(km_ref, kp_ref, kc_ref), kgv)
        keys = _bf(kn_f)
        vals = _bf(_rows3(vm_ref, vp_ref, vc_ref))
        qn_f, qh, qr = _rms_rows(q_ref[...].reshape(QR, HD), qgv)
        qn = _bf(qn_f)
        e, inv, p_sink = _attn_probs(qn, keys, geo)
        dob = _bf(do_ref[...].reshape(QR, HD))
        p = e * inv
        dp = _dot(dob, vals, nt=True)
        delta = (p * dp).sum(axis=-1, keepdims=True)
        dsk = _stack_rows([jnp.broadcast_to(-jnp.sum((p_sink * delta)[hh * T:(hh + 1) * T], axis=0, keepdims=True),
                                            (1, 128)) for hh in range(GQA)])
        ds = p * (dp - delta)
        dqn = _dot(_bf(ds), keys) * _SCALE
        dq_ref[...] = _rms_rows_bwd(dqn, qgv, qh, qr).reshape(GQA, T, HD)
        dqg_acc = jnp.sum(dqn * qh, axis=0, keepdims=True)
        dkn = _dot(_bf(ds.T), qn) * _SCALE
        dkg_acc = jnp.sum(dkn * kh, axis=0, keepdims=True)
        dk_all = _rms_rows_bwd(dkn, kgv, kh, kr)
        dv_all = _dot(_bf(p.T), dob)
        dk_raw = [dk_all[b * T:(b + 1) * T] for b in range(3)]
        dvv = [dv_all[b * T:(b + 1) * T] for b in range(3)]
        dkp_ref[...] = dk_raw[1]
        dkc_ref[...] = dk_raw[2]
        dvp_ref[...] = dvv[1]
        dvc_ref[...] = dvv[2]
        first = jnp.logical_and(g == 0, n == 0)

        @pl.when(n == 0)
        def _():
            dkm_ref[...] = dk_raw[0]
            dvm_ref[...] = dvv[0]
            dsk_ref[...] = dsk

        @pl.when(n > 0)
        def _():
            dkm_ref[...] += dk_raw[0]
            dvm_ref[...] += dvv[0]
            dsk_ref[...] += dsk

        @pl.when(first)
        def _():
            dqg_ref[...] = dqg_acc
            dkg_ref[...] = dkg_acc

        @pl.when(jnp.logical_not(first))
        def _():
            dqg_ref[...] += dqg_acc
            dkg_ref[...] += dkg_acc

    qspec, cur, prev, meta, vec = _attn_specs()
    kv_shape = jax.ShapeDtypeStruct((NKV, L, HD), F32)
    meta_shape = jax.ShapeDtypeStruct((NKV, T, HD), F32)
    vec_shape = jax.ShapeDtypeStruct((1, HD), F32)
    return pl.pallas_call(
        body, name=name,
        out_shape=(jax.ShapeDtypeStruct((NQ, L, HD), F32), kv_shape, kv_shape, meta_shape, kv_shape, kv_shape,
                   meta_shape, vec_shape, vec_shape, jax.ShapeDtypeStruct((NKV, GQA, 128), F32)),
        grid=(NKV, NB),
        in_specs=[pl.BlockSpec(memory_space=pltpu.SMEM), qspec, cur, prev, meta, cur, prev, meta, qspec, vec, vec],
        out_specs=(qspec, cur, cur, meta, cur, cur, meta, vec, vec,
                   pl.BlockSpec((None, GQA, 128), lambda g, n: (g, 0, 0))))(sinks, q, k, k, k, v, v, v, do, qg, kg)


def _ret_tables():
    h = np.arange(RH, dtype=np.float64)
    lg = np.log1p(-np.exp2(-5.0 - h))
    idx = np.arange(T, dtype=np.float64)
    diff = idx[:, None] - idx[None, :]
    decay = np.where(diff[None] >= 0, np.exp(np.maximum(diff, 0.0)[None] * lg[:, None, None]), 0.0)
    zeta = np.exp((T - 1 - idx)[None, :] * lg[:, None])
    xi = np.exp((idx + 1.0)[None, :] * lg[:, None])
    cd = np.exp(T * lg)
    f = lambda a: jnp.asarray(a, F32)
    return dict(decay=f(decay), zeta_row=f(zeta[:, None, :]), zeta_col=f(zeta[:, :, None]),
                xi_row=f(xi[:, None, :]), xi_col=f(xi[:, :, None]),
                cd=f(np.broadcast_to(cd[:, None, None], (RH, 1, HD))))


def _gn_rows(y):
    mu = jnp.mean(y, axis=-1, keepdims=True)
    yc = y - mu
    rstd = lax.rsqrt(jnp.mean(yc * yc, axis=-1, keepdims=True) + LN_EPS)
    return yc * rstd, rstd


def ret_fwd(rq, rkt, rv, rg, gng, tb, name):
    def body(q_ref, kt_ref, v_ref, g_ref, dec_ref, zr_ref, xc_ref, cd_ref, gng_ref, y_ref, st_ref, s_scr):
        n = pl.program_id(1)

        @pl.when(n == 0)
        def _():
            s_scr[...] = jnp.zeros_like(s_scr)

        s = s_scr[...]
        st_ref[...] = s
        q = q_ref[...]
        vb = _bf(v_ref[...])
        cols = n * T + lax.broadcasted_iota(jnp.int32, (1, T), 1)
        kft = jnp.where(cols >= PAD, kt_ref[...] * (HD ** -0.5), 0.0)
        a = _dot(_bf(q), _bf(kft)) * dec_ref[...]
        y = _dot(_bf(a), vb) + _dot(_bf(q * xc_ref[...]), _bf(s))
        s_scr[...] = cd_ref[...] * s + _dot(_bf(kft * zr_ref[...]), vb)
        yh, _ = _gn_rows(y)
        gv = g_ref[...]
        y_ref[...] = gv * _sig(gv) * (yh * gng_ref[...])

    hm = pl.BlockSpec((None, T, HD), lambda h, n: (h, n, 0))
    hmt = pl.BlockSpec((None, HD, T), lambda h, n: (h, 0, n))
    per_h = lambda *s: pl.BlockSpec((None,) + s, lambda h, n: (h, 0, 0))
    return pl.pallas_call(
        body, name=name,
        out_shape=(jax.ShapeDtypeStruct((RH, L, HD), F32), jax.ShapeDtypeStruct((RH, NB, HD, HD), F32)),
        grid=(RH, NB),
        in_specs=[hm, hmt, hm, hm, per_h(T, T), per_h(1, T), per_h(T, 1), per_h(1, HD), per_h(1, HD)],
        out_specs=(hm, pl.BlockSpec((None, None, HD, HD), lambda h, n: (h, n, 0, 0))),
        scratch_shapes=[pltpu.VMEM((HD, HD), F32)])(
            rq, rkt, rv, rg, tb["decay"], tb["zeta_row"], tb["xi_col"], tb["cd"], gng)


def ret_bwd(rq, rqt, rk, rkt, rv, rg, dout, states, gng, tb, name):
    def body(q_ref, qt_ref, k_ref, kt_ref, v_ref, g_ref, do_ref, st_ref, dec_ref, zc_ref, xr_ref, xc_ref, cd_ref,
             gng_ref, dq_ref, dk_ref, dv_ref, dg_ref, dgn_ref, ds_scr):
        i = pl.program_id(1)
        n = NB - 1 - i

        @pl.when(i == 0)
        def _():
            ds_scr[...] = jnp.zeros_like(ds_scr)

        dsn = _bf(ds_scr[...])
        sb = _bf(st_ref[...])
        rows = n * T + lax.broadcasted_iota(jnp.int32, (T, 1), 0)
        cols = n * T + lax.broadcasted_iota(jnp.int32, (1, T), 1)
        q = q_ref[...]
        qb = _bf(q)
        v = v_ref[...]
        vb = _bf(v)
        kf = jnp.where(rows >= PAD, k_ref[...] * (HD ** -0.5), 0.0)
        kft = jnp.where(cols >= PAD, kt_ref[...] * (HD ** -0.5), 0.0)
        dec = dec_ref[...]
        xc = xc_ref[...]
        zc = zc_ref[...]
        a = _dot(qb, _bf(kft)) * dec
        y = _dot(_bf(a), vb) + _dot(_bf(q * xc), sb)
        yh, rstd = _gn_rows(y)
        gv = g_ref[...]
        sg = _sig(gv)
        sil = gv * sg
        gn = gng_ref[...]
        dout = do_ref[...]
        dg_ref[...] = dout * (yh * gn) * (sg * (1.0 + gv * (1.0 - sg)))
        dyh = dout * sil * gn
        part = jnp.sum(dout * sil * yh, axis=0, keepdims=True)

        @pl.when(i == 0)
        def _():
            dgn_ref[...] = part

        @pl.when(i > 0)
        def _():
            dgn_ref[...] += part

        dy = rstd * (dyh - jnp.mean(dyh, axis=-1, keepdims=True) - yh * jnp.mean(dyh * yh, axis=-1, keepdims=True))
        dyb = _bf(dy)
        da = _dot(dyb, vb, nt=True) * dec
        dv_ref[...] = _dot(_bf(a.T), dyb) + _dot(_bf(kf * zc), dsn)
        dq_ref[...] = _dot(_bf(da), _bf(kf)) + _dot(dyb, sb, nt=True) * xc
        dkf = _dot(_bf(da.T), qb) + _dot(vb, dsn, nt=True) * zc
        dk_ref[...] = jnp.where(rows >= PAD, dkf * (HD ** -0.5), 0.0)
        ds_scr[...] = cd_ref[...] * ds_scr[...] + _dot(_bf(qt_ref[...] * xr_ref[...]), dyb)

    hm = pl.BlockSpec((None, T, HD), lambda h, i: (h, NB - 1 - i, 0))
    hmt = pl.BlockSpec((None, HD, T), lambda h, i: (h, 0, NB - 1 - i))
    per_h = lambda *s: pl.BlockSpec((None,) + s, lambda h, i: (h, 0, 0))
    hl = jax.ShapeDtypeStruct((RH, L, HD), F32)
    return pl.pallas_call(
        body, name=name,
        out_shape=(hl, hl, hl, hl, jax.ShapeDtypeStruct((RH, 1, HD), F32)),
        grid=(RH, NB),
        in_specs=[hm, hmt, hm, hmt, hm, hm, hm,
                  pl.BlockSpec((None, None, HD, HD), lambda h, i: (h, NB - 1 - i, 0, 0)),
                  per_h(T, T), per_h(T, 1), per_h(1, T), per_h(T, 1), per_h(1, HD), per_h(1, HD)],
        out_specs=(hm, hm, hm, hm, per_h(1, HD)),
        scratch_shapes=[pltpu.VMEM((HD, HD), F32)])(
            rq, rqt, rk, rkt, rv, rg, dout, states, tb["decay"], tb["zeta_col"], tb["xi_row"], tb["xi_col"],
            tb["cd"], gng)


HALO = 32
TAP0 = HALO - (CONV_K - 1)


def _conv_specs():
    cur = pl.BlockSpec((T, CV), lambda c: (c, 0))
    before = pl.BlockSpec((HALO, CV), lambda c: (jnp.maximum(c * (T // HALO) - 1, 0), 0))
    after = pl.BlockSpec((HALO, CV), lambda c: (jnp.minimum((c + 1) * (T // HALO), L // HALO - 1), 0))
    full = lambda r, w: pl.BlockSpec((r, w), lambda c: (0, 0))
    return cur, before, after, full


def _conv_tile_fwd(c, ca_ref, cah_ref, cb_ref, cbh_ref, w_ref, b_ref, lg_ref, lb_ref, u_scr):
    u_scr[0:HALO, :] = jnp.where(c > 0, cah_ref[...] * _sig(cbh_ref[...]), 0.0)
    u_scr[HALO:HALO + T, :] = ca_ref[...] * _sig(cb_ref[...])
    acc = jnp.zeros((T, CV), F32)
    for k in range(CONV_K):
        acc = acc + w_ref[k:k + 1, :] * u_scr[TAP0 + k:TAP0 + k + T, :]
    xc = acc + b_ref[...]
    xh, rstd = _gn_rows(xc)
    z = xh * lg_ref[...] + lb_ref[...]
    return xh, rstd, z, _sig(z)


def conv_fwd(ca, cb, w, b, lg, lb, pw, og, name):
    def body(ca_ref, cah_ref, cb_ref, cbh_ref, w_ref, b_ref, lg_ref, lb_ref, pw_ref, og_ref, y_ref, u_scr):
        c = pl.program_id(0)
        _, _, z, sg = _conv_tile_fwd(c, ca_ref, cah_ref, cb_ref, cbh_ref, w_ref, b_ref, lg_ref, lb_ref, u_scr)
        zp = _dot(_bf(z * sg), pw_ref[...])
        y_ref[...] = _rms_rows(zp, og_ref[...])[0]

    cur, before, _, full = _conv_specs()
    vec = full(1, CV)
    return pl.pallas_call(
        body, name=name, out_shape=jax.ShapeDtypeStruct((L, CV), F32), grid=(NB,),
        in_specs=[cur, before, cur, before, full(CONV_K, CV), vec, vec, vec, full(CV, CV), vec],
        out_specs=cur,
        scratch_shapes=[pltpu.VMEM((HALO + T, CV), F32)])(ca, ca, cb, cb, w, b, lg, lb, pw, og)


def conv_bwd(ca, cb, dy, w, b, lg, lb, pw, og, name):
    def body1(ca_ref, cah_ref, cb_ref, cbh_ref, dy_ref, w_ref, b_ref, lg_ref, lb_ref, pw_ref, og_ref,
              dxc_ref, dw_ref, db_ref, dlg_ref, dlb_ref, dog_ref, dpw_ref, u_scr):
        c = pl.program_id(0)

        @pl.when(c == 0)
        def _():
            for r in (dw_ref, db_ref, dlg_ref, dlb_ref, dog_ref, dpw_ref):
                r[...] = jnp.zeros_like(r)

        xh, rstd, z, sg = _conv_tile_fwd(c, ca_ref, cah_ref, cb_ref, cbh_ref, w_ref, b_ref, lg_ref, lb_ref, u_scr)
        s = z * sg
        zp = _dot(_bf(s), pw_ref[...])
        ogv = og_ref[...]
        _, zph, r2 = _rms_rows(zp, ogv)
        dyv = dy_ref[...]
        dog_ref[...] += jnp.sum(dyv * zph, axis=0, keepdims=True)
        dzpb = _bf(_rms_rows_bwd(dyv, ogv, zph, r2))
        dpw_ref[...] += _dot(_bf(s.T), dzpb)
        dz = _dot(dzpb, pw_ref[...], nt=True) * (sg * (1.0 + z * (1.0 - sg)))
        dlg_ref[...] += jnp.sum(dz * xh, axis=0, keepdims=True)
        dlb_ref[...] += jnp.sum(dz, axis=0, keepdims=True)
        dxh = dz * lg_ref[...]
        dxc = rstd * (dxh - jnp.mean(dxh, axis=-1, keepdims=True) - xh * jnp.mean(dxh * xh, axis=-1, keepdims=True))
        db_ref[...] += jnp.sum(dxc, axis=0, keepdims=True)
        dxc_ref[...] = dxc
        for k in range(CONV_K):
            dw_ref[k:k + 1, :] += jnp.sum(dxc * u_scr[TAP0 + k:TAP0 + k + T, :], axis=0, keepdims=True)

    def body2(dx_ref, dxa_ref, ca_ref, cb_ref, w_ref, dca_ref, dcb_ref, d_scr):
        c = pl.program_id(0)
        d_scr[0:T, :] = dx_ref[...]
        d_scr[T:T + HALO, :] = jnp.where(c < NB - 1, dxa_ref[...], 0.0)
        du = jnp.zeros((T, CV), F32)
        for k in range(CONV_K):
            off = CONV_K - 1 - k
            du = du + w_ref[k:k + 1, :] * d_scr[off:off + T, :]
        sg = _sig(cb_ref[...])
        dca_ref[...] = du * sg
        dcb_ref[...] = du * ca_ref[...] * sg * (1.0 - sg)

    cur, before, after, full = _conv_specs()
    seq = jax.ShapeDtypeStruct((L, CV), F32)
    vsh = jax.ShapeDtypeStruct((1, CV), F32)
    vec = full(1, CV)
    dxc, dw, db, dlg, dlb, dog, dpw = pl.pallas_call(
        body1, name=name + "_a",
        out_shape=(seq, jax.ShapeDtypeStruct((CONV_K, CV), F32), vsh, vsh, vsh, vsh,
                   jax.ShapeDtypeStruct((CV, CV), F32)),
        grid=(NB,),
        in_specs=[cur, before, cur, before, cur, full(CONV_K, CV), vec, vec, vec, full(CV, CV), vec],
        out_specs=(cur, full(CONV_K, CV), vec, vec, vec, vec, full(CV, CV)),
        scratch_shapes=[pltpu.VMEM((HALO + T, CV), F32)])(ca, ca, cb, cb, dy, w, b, lg, lb, pw, og)
    dca, dcb = pl.pallas_call(
        body2, name=name + "_b", out_shape=(seq, seq), grid=(NB,),
        in_specs=[cur, after, cur, cur, full(CONV_K, CV)], out_specs=(cur, cur),
        scratch_shapes=[pltpu.VMEM((T + HALO, CV), F32)])(dxc, dxc, ca, cb, w)
    return dca, dcb, dw, db, dlg, dlb, dog, dpw


HR = 8


def _ffn_specs():
    cur = lambda off: pl.BlockSpec((None, T, UPW), lambda j, c: (j + off, c, 0))
    before = lambda off: pl.BlockSpec((None, HR, UPW), lambda j, c: (j + off, jnp.maximum(c * (T // HR) - 1, 0), 0))
    after = lambda off: pl.BlockSpec(
        (None, HR, UPW), lambda j, c: (j + off, jnp.minimum((c + 1) * (T // HR), L // HR - 1), 0))
    wspec = lambda off, r: pl.BlockSpec((None, r, UPW), lambda j, c: (j + off, 0, 0))
    return cur, before, after, wspec


def ffn_act_fwd(f, w, b, name):
    def body(fg_ref, fgh_ref, fu_ref, fuh_ref, wg_ref, wu_ref, bg_ref, bu_ref, a_ref, scr):
        c = pl.program_id(1)

        def conv(x_ref, xh_ref, w_ref, b_ref):
            scr[0:HR, :] = jnp.where(c > 0, xh_ref[...], 0.0)
            scr[HR:HR + T, :] = x_ref[...]
            acc = b_ref[...]
            for k in range(FFN_K):
                acc = acc + w_ref[k:k + 1, :] * scr[HR - (FFN_K - 1) + k:HR - (FFN_K - 1) + k + T, :]
            return acc

        gate = conv(fg_ref, fgh_ref, wg_ref, bg_ref)
        up = conv(fu_ref, fuh_ref, wu_ref, bu_ref)
        a_ref[...] = (gate * _sig(gate) * up).astype(BF16)

    cur, before, _, wspec = _ffn_specs()
    return pl.pallas_call(
        body, name=name, out_shape=jax.ShapeDtypeStruct((NPAIR, L, UPW), BF16), grid=(NPAIR, NB),
        in_specs=[cur(0), before(0), cur(NPAIR), before(NPAIR), wspec(0, FFN_K), wspec(NPAIR, FFN_K),
                  wspec(0, 1), wspec(NPAIR, 1)],
        out_specs=cur(0),
        scratch_shapes=[pltpu.VMEM((HR + T, UPW), F32)])(f, f, f, f, w, w, b, b)


def ffn_act_bwd(f, da, w, b, name):
    ext = T + HR

    def body(fg_ref, fgb_ref, fga_ref, fu_ref, fub_ref, fua_ref, da_ref, daa_ref, wg_ref, wu_ref, bg_ref, bu_ref,
             dfg_ref, dfu_ref, dwg_ref, dwu_ref, dbg_ref, dbu_ref, xg_scr, xu_scr, d_scr):
        c = pl.program_id(1)

        def load(x_scr, x_ref, xb_ref, xa_ref):
            x_scr[0:HR, :] = jnp.where(c > 0, xb_ref[...], 0.0)
            x_scr[HR:HR + T, :] = x_ref[...]
            x_scr[HR + T:HR + T + HR, :] = xa_ref[...]

        def conv(x_scr, w_ref, b_ref):
            acc = b_ref[...]
            for k in range(FFN_K):
                off = HR - (FFN_K - 1) + k
                acc = acc + w_ref[k:k + 1, :] * x_scr[off:off + ext, :]
            return acc

        load(xg_scr, fg_ref, fgb_ref, fga_ref)
        load(xu_scr, fu_ref, fub_ref, fua_ref)
        gate = conv(xg_scr, wg_ref, bg_ref)
        up = conv(xu_scr, wu_ref, bu_ref)
        d_scr[0:T, :] = da_ref[...]
        d_scr[T:ext, :] = jnp.where(c < NB - 1, daa_ref[...], 0.0)
        dav = d_scr[...]
        sg = _sig(gate)
        d_gate = dav * up * (sg * (1.0 + gate * (1.0 - sg)))
        d_up = dav * gate * sg

        def back(dv, x_scr, w_ref, df_ref, dw_ref, db_ref):
            d_scr[...] = dv
            acc = jnp.zeros((T, UPW), F32)
            for k in range(FFN_K):
                off = FFN_K - 1 - k
                acc = acc + w_ref[k:k + 1, :] * d_scr[off:off + T, :]
            df_ref[...] = acc.astype(BF16)
            dcur = d_scr[0:T, :]
            parts = [jnp.sum(dcur * x_scr[HR - (FFN_K - 1) + k:HR - (FFN_K - 1) + k + T, :], axis=0, keepdims=True)
                     for k in range(FFN_K)]
            dwp = _stack_rows(parts)
            dbp = jnp.sum(dcur, axis=0, keepdims=True)

            @pl.when(c == 0)
            def _():
                dw_ref[...] = dwp
                db_ref[...] = dbp

            @pl.when(c > 0)
            def _():
                dw_ref[...] += dwp
                db_ref[...] += dbp

        back(d_gate, xg_scr, wg_ref, dfg_ref, dwg_ref, dbg_ref)
        back(d_up, xu_scr, wu_ref, dfu_ref, dwu_ref, dbu_ref)

    cur, before, after, wspec = _ffn_specs()
    slab = jax.ShapeDtypeStruct((NPAIR, L, UPW), BF16)
    wsh = jax.ShapeDtypeStruct((NPAIR, FFN_K, UPW), F32)
    bsh = jax.ShapeDtypeStruct((NPAIR, 1, UPW), F32)
    return pl.pallas_call(
        body, name=name, out_shape=(slab, slab, wsh, wsh, bsh, bsh), grid=(NPAIR, NB),
        in_specs=[cur(0), before(0), after(0), cur(NPAIR), before(NPAIR), after(NPAIR), cur(0), after(0),
                  wspec(0, FFN_K), wspec(NPAIR, FFN_K), wspec(0, 1), wspec(NPAIR, 1)],
        out_specs=(cur(0), cur(0), wspec(0, FFN_K), wspec(0, FFN_K), wspec(0, 1), wspec(0, 1)),
        scratch_shapes=[pltpu.VMEM((HR + T + HR, UPW), F32), pltpu.VMEM((HR + T + HR, UPW), F32),
                        pltpu.VMEM((ext, UPW), F32)])(f, f, f, f, f, f, da, da, w, w, b, b)


def _mesh_pos():
    return lax.axis_index("x"), lax.axis_index("y"), lax.axis_index("c")


def _peer(pos, k):
    x, y, c = pos
    px = 1 - x if k & 4 else x
    py = 1 - y if k & 2 else y
    pc = 1 - c if k & 1 else c
    return (px, py, pc), 4 * px + 2 * py + pc


_CHIP_FLIPS = (4, 2, 6)
_HBM = pl.BlockSpec(memory_space=pl.ANY)


def all_gather(shards, name):
    nt = len(shards)

    def body(*refs):
        ins, outs = refs[:nt], refs[nt:2 * nt]
        send, recv, local = refs[2 * nt:]
        pos = _mesh_pos()
        me = 4 * pos[0] + 2 * pos[1] + pos[2]
        sib, sib_id = _peer(pos, 1)

        def copy(t, k, block_id, to, src=None):
            dst = outs[t].at[block_id]
            return pltpu.make_async_remote_copy(
                src_ref=dst if src is None else src, dst_ref=dst, send_sem=send.at[t, k], recv_sem=recv.at[t, k],
                device_id=to, device_id_type=MESH)

        locals_ = [pltpu.make_async_copy(ins[t], outs[t].at[me], local.at[t]) for t in range(nt)]
        for cp in locals_:
            cp.start()
        started = []
        for j, flip in enumerate(_CHIP_FLIPS):
            for t in range(nt):
                started.append(copy(t, 1 + j, me, _peer(pos, flip)[0], src=ins[t]))
        for t in range(nt):
            started.append(copy(t, 0, me, sib, src=ins[t]))
        for cp in started:
            cp.start()
        for j, flip in enumerate(_CHIP_FLIPS):
            _, pid = _peer(pos, flip)
            for t in range(nt):
                copy(t, 1 + j, pid, sib).wait_recv()
                fwd = copy(t, 4 + j, pid, sib)
                fwd.start()
                started.append(fwd)
        for t in range(nt):
            copy(t, 0, sib_id, sib).wait_recv()
        for j, flip in enumerate(_CHIP_FLIPS):
            _, pid = _peer(pos, flip | 1)
            for t in range(nt):
                copy(t, 4 + j, pid, sib).wait_recv()
        for cp in started:
            cp.wait_send()
        for cp in locals_:
            cp.wait()

    return pl.pallas_call(
        body, name=name,
        out_shape=tuple(jax.ShapeDtypeStruct((NDEV,) + s.shape, s.dtype) for s in shards),
        in_specs=[_HBM] * nt, out_specs=tuple([_HBM] * nt),
        scratch_shapes=[pltpu.SemaphoreType.DMA((nt, NDEV - 1)), pltpu.SemaphoreType.DMA((nt, NDEV - 1)),
                        pltpu.SemaphoreType.DMA((nt,))])(*shards)


_SPLIT = dict(has_side_effects=pltpu.SideEffectType.DATAFLOW_SIDE_EFFECTING)
_SEM = pl.BlockSpec(memory_space=pltpu.SEMAPHORE)


def _exchange_start(name, srcs, lands, copies, after):
    ns, nl = len(srcs), len(lands)
    ncopy = len(copies(None, [None] * ns, [None] * nl))

    def body(*refs):
        src_refs, land_refs = refs[:ns], refs[ns:ns + nl]
        send, recv = refs[ns + nl + len(after):ns + nl + len(after) + 2]
        token = refs[-1]
        for i, (src, dst, peer) in enumerate(copies(_mesh_pos(), src_refs, land_refs)):
            pltpu.make_async_remote_copy(src_ref=src, dst_ref=dst, send_sem=send.at[i], recv_sem=recv.at[i],
                                         device_id=peer, device_id_type=MESH).start()
        token[...] = jnp.zeros_like(token)

    hbm = lambda a: pltpu.HBM(a.shape, a.dtype)
    out = pl.pallas_call(
        body, name=name,
        out_shape=(pltpu.SemaphoreType.DMA((ncopy,)), pltpu.SemaphoreType.DMA((ncopy,)),
                   *[hbm(a) for a in srcs], *[hbm(a) for a in lands], jax.ShapeDtypeStruct((8, 128), F32)),
        in_specs=[pl.BlockSpec(memory_space=pltpu.HBM)] * (ns + nl) + [_HBM] * len(after),
        out_specs=(_SEM, _SEM, *[pl.BlockSpec(memory_space=pltpu.HBM)] * (ns + nl),
                   pl.BlockSpec(memory_space=pltpu.VMEM)),
        input_output_aliases={i: 2 + i for i in range(ns + nl)},
        compiler_params=pltpu.CompilerParams(**_SPLIT))(
            *[pltpu.with_memory_space_constraint(a, pltpu.HBM) for a in list(srcs) + list(lands)], *after)
    return out[0], out[1], list(out[2:2 + ns]), list(out[2 + ns:2 + ns + nl]), out[-1]


def _exchange_wait(name, send, recv, srcs, lands, copies, after):
    ns, nl = len(srcs), len(lands)

    def body(*refs):
        src_refs, land_refs = refs[:ns], refs[ns:ns + nl]
        send_ref, recv_ref = refs[ns + nl:ns + nl + 2]
        pos = _mesh_pos()
        for i, (src, dst, peer) in enumerate(copies(pos, src_refs, land_refs, arriving=True)):
            cp = pltpu.make_async_remote_copy(src_ref=src, dst_ref=dst, send_sem=send_ref.at[i], recv_sem=recv_ref.at[i],
                                              device_id=peer, device_id_type=MESH)
            cp.wait_send()
            cp.wait_recv()

    hbm = lambda a: pltpu.HBM(a.shape, a.dtype)
    out = pl.pallas_call(
        body, name=name, out_shape=tuple(hbm(a) for a in list(srcs) + list(lands)),
        in_specs=[pl.BlockSpec(memory_space=pltpu.HBM)] * (ns + nl) + [_SEM, _SEM] + [_HBM] * len(after),
        out_specs=tuple([pl.BlockSpec(memory_space=pltpu.HBM)] * (ns + nl)),
        input_output_aliases={i: i for i in range(ns + nl)},
        compiler_params=pltpu.CompilerParams(**_SPLIT))(*srcs, *lands, send, recv, *after)
    return list(out[:ns]), list(out[ns:])


def _gather_copies(pos, srcs, lands, arriving=False):
    if pos is None:
        return [None] * (len(srcs) * (NDEV - 1))
    me = 4 * pos[0] + 2 * pos[1] + pos[2]
    out = []
    for src, land in zip(srcs, lands):
        for k in range(1, NDEV):
            peer, pid = _peer(pos, k)
            out.append((src, land.at[pid if arriving else me], peer))
    return out


def _scatter_copies(pos, srcs, lands, arriving=False):
    if pos is None:
        return [None] * (len(srcs) * (NDEV - 1))
    out = []
    for src, land in zip(srcs, lands):
        for k in range(1, NDEV):
            peer, pid = _peer(pos, k)
            out.append((src.at[pid], land.at[k - 1], peer))
    return out


def gather_start(name, shards, lands, after=()):
    return _exchange_start(name, shards, lands, _gather_copies, list(after))


def gather_wait(name, handle, after=()):
    send, recv, srcs, lands, _ = handle
    return _exchange_wait(name, send, recv, srcs, lands, _gather_copies, list(after))


def scatter_start(name, grads, after=()):
    lands = [pltpu.with_memory_space_constraint(lax.empty((NDEV - 1,) + g.shape[1:], g.dtype), pltpu.HBM)
             for g in grads]
    return _exchange_start(name, grads, lands, _scatter_copies, list(after))


def scatter_wait(name, handle, after=()):
    send, recv, srcs, lands, _ = handle
    return _exchange_wait(name, send, recv, srcs, lands, _scatter_copies, list(after))


def all_reduce_small(part, name):
    rows = part.shape[0]

    def body(x_ref, o_ref, buf, send, recv):
        pos = _mesh_pos()
        me = 4 * pos[0] + 2 * pos[1] + pos[2]
        sends = []
        for k in range(1, NDEV):
            peer, _ = _peer(pos, k)
            sends.append(pltpu.make_async_remote_copy(
                src_ref=x_ref, dst_ref=buf.at[me], send_sem=send.at[k - 1], recv_sem=recv.at[k - 1],
                device_id=peer, device_id_type=MESH))
        for cp in sends:
            cp.start()
        buf[me] = x_ref[...]
        for k in range(1, NDEV):
            peer, pid = _peer(pos, k)
            pltpu.make_async_remote_copy(
                src_ref=x_ref, dst_ref=buf.at[pid], send_sem=send.at[k - 1], recv_sem=recv.at[k - 1],
                device_id=peer, device_id_type=MESH).wait_recv()
        for cp in sends:
            cp.wait_send()
        acc = buf[0]
        for d in range(1, NDEV):
            acc = acc + buf[d]
        o_ref[...] = acc

    return pl.pallas_call(
        body, name=name, out_shape=jax.ShapeDtypeStruct((rows, 128), F32),
        in_specs=[pl.BlockSpec(memory_space=pltpu.VMEM)], out_specs=pl.BlockSpec(memory_space=pltpu.VMEM),
        scratch_shapes=[pltpu.VMEM((NDEV, rows, 128), F32), pltpu.SemaphoreType.DMA((NDEV - 1,)),
                        pltpu.SemaphoreType.DMA((NDEV - 1,))])(part)


def _adamw_update(g, w_ref, m_ref, v_ref, g_ref, d_ref, nm_ref, nv_ref):
    g_ref[...] = g
    nm = ADAM_B1 * m_ref[...] + (1.0 - ADAM_B1) * g
    nv = ADAM_B2 * v_ref[...] + (1.0 - ADAM_B2) * (g * g)
    nm_ref[...] = nm
    nv_ref[...] = nv
    m_hat = nm / (1.0 - ADAM_B1 ** ADAM_STEP)
    v_hat = nv / (1.0 - ADAM_B2 ** ADAM_STEP)
    d_ref[...] = -ADAM_LR * (m_hat / (jnp.sqrt(v_hat) + ADAM_EPS) + ADAM_WD * w_ref[...])


def adamw_big(own, lands, w, m, v, me, tr, name):
    _, r, c = own[0].shape
    nt = r // tr

    def body(me_ref, *refs):
        ins, (w_ref, m_ref, v_ref), outs = refs[:2 * DEPTH], refs[2 * DEPTH:2 * DEPTH + 3], refs[2 * DEPTH + 3:]
        layer = pl.program_id(0)
        for l in range(DEPTH):
            @pl.when(layer == l)
            def _(l=l):
                g = ins[2 * l][...].astype(F32)
                for s in range(NDEV - 1):
                    g = g + ins[2 * l + 1][s].astype(F32)
                _adamw_update(g, w_ref, m_ref, v_ref, *outs)

    pick = lambda l: (lambda layer, i, me_ref: jnp.where(layer == l, i, 0))
    in_specs = []
    for l in range(DEPTH):
        in_specs.append(pl.BlockSpec((None, tr, c), lambda layer, i, me_ref, f=pick(l): (me_ref[0], f(layer, i, me_ref), 0)))
        in_specs.append(pl.BlockSpec((NDEV - 1, tr, c), lambda layer, i, me_ref, f=pick(l): (0, f(layer, i, me_ref), 0)))
    blk = pl.BlockSpec((tr, c), lambda layer, i, me_ref: (layer * nt + i, 0))
    sh = jax.ShapeDtypeStruct((DEPTH * r, c), F32)
    args = [a for l in range(DEPTH) for a in (own[l], lands[l])]
    return pl.pallas_call(
        body, name=name, out_shape=(sh, sh, sh, sh),
        grid_spec=pltpu.PrefetchScalarGridSpec(
            num_scalar_prefetch=1, grid=(DEPTH, nt), in_specs=in_specs + [blk, blk, blk],
            out_specs=(blk, blk, blk, blk)),
        compiler_params=_params(VMEM_BIG))(me, *args, w, m, v)


def adamw(parts, w, m, v, tr, name):
    ns, r, c = parts.shape

    def body(p_ref, w_ref, m_ref, v_ref, g_ref, d_ref, nm_ref, nv_ref):
        g = p_ref[0].astype(F32)
        for s in range(1, ns):
            g = g + p_ref[s].astype(F32)
        _adamw_update(g, w_ref, m_ref, v_ref, g_ref, d_ref, nm_ref, nv_ref)

    blk = pl.BlockSpec((tr, c), lambda i: (i, 0))
    sh = jax.ShapeDtypeStruct((r, c), F32)
    return pl.pallas_call(
        body, name=name, out_shape=(sh, sh, sh, sh), grid=(r // tr,),
        in_specs=[pl.BlockSpec((ns, tr, c), lambda i: (0, i, 0)), blk, blk, blk],
        out_specs=(blk, blk, blk, blk), compiler_params=_params(VMEM_BIG))(parts, w, m, v)


_SPLITS = (NQ * HD, NKV * HD, NKV * HD, CV, CV, RH * HD, RH * HD, RH * HD, RH * HD)
_OFFS = np.concatenate([[0], np.cumsum(_SPLITS)]).tolist()


def _heads(x, nh):
    return x.reshape(L, nh, HD).transpose(1, 0, 2)


def _unheads(x):
    return x.transpose(1, 0, 2).reshape(L, x.shape[0] * HD)


def _split_proj(proj):
    return [proj[:, _OFFS[i]:_OFFS[i + 1]] for i in range(len(_SPLITS))]


def local_step(x, target, P, hooks):
    tb = _ret_tables()
    h = jnp.concatenate([jnp.zeros((PAD, D), F32), P["meta"], x], axis=0)
    stash = []
    for l in range(DEPTH):
        s = {"h": h}
        s["w_in"], s["cv_pw"], s["w_out"] = hooks["mixer_w"](l, h)
        u = rms_fwd(h, P["norm_mix_g"][l], BF16, f"rms_mix_{l}", after=hooks["first_after"] if l == 0 else None)
        s["ut"] = u.T
        proj = mm_nn(f"in_proj_{l}", u, s["w_in"], 256)
        q, k, v, ca, cb, rq, rk, rv, rg = _split_proj(proj)
        s.update(q=_heads(q, NQ), k=_heads(k, NKV), v=_heads(v, NKV), ca=ca, cb=cb, rq=_heads(rq, RH),
                 rk=_heads(rk, RH), rv=_heads(rv, RH), rg=_heads(rg, RH))
        s["rkt"] = s["rk"].transpose(0, 2, 1)
        s["rqt"] = s["rq"].transpose(0, 2, 1)
        o = attn_fwd(s["q"], s["k"], s["v"], P["q_norm_g"][l], P["k_norm_g"][l], P["attn_sinks"][l], f"attn_fwd_{l}")
        s["o"] = _unheads(o)
        y_att = rms_fwd(s["o"], P["attn_out_g"][l], BF16, f"rms_att_{l}")
        y_cv = conv_fwd(ca, cb, P["cv_dw_w"][l], P["cv_dw_b"][l], P["cv_ln_g"][l], P["cv_ln_b"][l], s["cv_pw"],
                        P["cv_out_g"][l], f"conv_fwd_{l}")
        y_ret, s["states"] = ret_fwd(s["rq"], s["rkt"], s["rv"], s["rg"], P["ret_gn_g"][l], tb, f"ret_fwd_{l}")
        ycat = jnp.concatenate([y_att, y_cv.astype(BF16), _unheads(y_ret).astype(BF16)], axis=1)
        s["ycat_t"] = ycat.T
        h = mm_nn(f"out_proj_{l}", ycat, s["w_out"], 256, res=h)
        s["h1"] = h
        s["ffn_up"], s["ffn_down"] = hooks["ffn_w"](l, h)
        u2 = rms_fwd(h, P["norm_ffn_g"][l], BF16, f"rms_ffn_{l}")
        s["u2t"] = u2.T
        s["f"] = ffn_up_fwd(f"ffn_up_{l}", u2, s["ffn_up"])
        a = ffn_act_fwd(s["f"], P["ffn_dw_w"][l], P["ffn_dw_b"][l], f"ffn_act_fwd_{l}")
        s["at"] = a.transpose(0, 2, 1)
        h = ffn_down_fwd(f"ffn_down_{l}", a, s["ffn_down"], h)
        stash.append(s)

    dh, dhm, loss = loss_head(h, target)
    G = {n: [None] * DEPTH for n in ("norm_mix_g", "q_norm_g", "k_norm_g", "attn_sinks", "attn_out_g", "cv_dw_w",
                                     "cv_dw_b", "cv_ln_g", "cv_ln_b", "cv_out_g", "ret_gn_g", "norm_ffn_g",
                                     "ffn_dw_w", "ffn_dw_b")}
    for l in reversed(range(DEPTH)):
        s = stash[l]
        da = ffn_down_dx(f"ffn_down_dx_{l}", dhm, s["ffn_down"])
        dw_down = ffn_down_dw(f"ffn_down_dw_{l}", s["at"], dhm)
        dfg, dfu, dwg, dwu, dbg, dbu = ffn_act_bwd(s["f"], da, P["ffn_dw_w"][l], P["ffn_dw_b"][l], f"ffn_act_bwd_{l}")
        df = jnp.concatenate([dfg, dfu], axis=0)
        G["ffn_dw_w"][l] = jnp.concatenate([dwg, dwu], axis=0)
        G["ffn_dw_b"][l] = jnp.concatenate([dbg, dbu], axis=0)
        du2 = ffn_up_dx(f"ffn_up_dx_{l}", df, s["ffn_up"])
        dw_up = ffn_up_dw(f"ffn_up_dw_{l}", s["u2t"], df)
        after = hooks["ffn_grads"](l, dw_down, dw_up)
        dh, dhm, G["norm_ffn_g"][l] = rms_bwd(s["h1"], P["norm_ffn_g"][l], du2, dh, f"rms_ffn_bwd_{l}", after=after)
        dycat = mm_nt(f"out_proj_dx_{l}", dhm, s["w_out"], 256)
        dw_out = mm_nn(f"out_proj_dw_{l}", s["ycat_t"], dhm, 256, out_dtype=BF16)
        dy_att, dy_cv, dy_ret = dycat[:, :NQ * HD], dycat[:, NQ * HD:NQ * HD + CV], dycat[:, NQ * HD + CV:]
        do, _, G["attn_out_g"][l] = rms_bwd(s["o"], P["attn_out_g"][l], dy_att, None, f"rms_att_bwd_{l}")
        (dq, dk_c, dk_p, dk_m, dv_c, dv_p, dv_m, G["q_norm_g"][l], G["k_norm_g"][l], dsk) = attn_bwd(
            s["q"], s["k"], s["v"], _heads(do, NQ), P["q_norm_g"][l], P["k_norm_g"][l], P["attn_sinks"][l],
            f"attn_bwd_{l}")
        G["attn_sinks"][l] = dsk[:, :, 0].reshape(NQ)
        shift = lambda z: jnp.concatenate([z[:, T:], jnp.zeros((NKV, T, HD), F32)], axis=1)
        dk = (dk_c + shift(dk_p)).at[:, :T].add(dk_m)
        dv = (dv_c + shift(dv_p)).at[:, :T].add(dv_m)
        (dca, dcb, G["cv_dw_w"][l], G["cv_dw_b"][l], G["cv_ln_g"][l], G["cv_ln_b"][l], G["cv_out_g"][l],
         dpw) = conv_bwd(s["ca"], s["cb"], dy_cv, P["cv_dw_w"][l], P["cv_dw_b"][l], P["cv_ln_g"][l],
                         P["cv_ln_b"][l], s["cv_pw"], P["cv_out_g"][l], f"conv_bwd_{l}")
        drq, drk, drv, drg, G["ret_gn_g"][l] = ret_bwd(
            s["rq"], s["rqt"], s["rk"], s["rkt"], s["rv"], s["rg"], _heads(dy_ret, RH), s["states"],
            P["ret_gn_g"][l], tb, f"ret_bwd_{l}")
        dproj = jnp.concatenate([_unheads(dq), _unheads(dk), _unheads(dv), dca, dcb, _unheads(drq), _unheads(drk),
                                 _unheads(drv), _unheads(drg)], axis=1).astype(BF16)
        du = mm_nt(f"in_proj_dx_{l}", dproj, s["w_in"], 256)
        dw_in = mm_nn(f"in_proj_dw_{l}", s["ut"], dproj, 256, out_dtype=BF16)
        after = hooks["mixer_grads"](l, dw_out, dpw, dw_in)
        dh, dhm, G["norm_mix_g"][l] = rms_bwd(s["h"], P["norm_mix_g"][l], du, dh, f"rms_mix_bwd_{l}", after=after)
    return loss[0, 0], dh[T:], dh[PAD:T], G


_SMALL = ("meta", "norm_mix_g", "q_norm_g", "k_norm_g", "attn_sinks", "attn_out_g", "cv_dw_w", "cv_dw_b", "cv_ln_g",
          "cv_ln_b", "cv_out_g", "ret_gn_g", "norm_ffn_g", "ffn_dw_w", "ffn_dw_b")
_BIG = ("w_in", "cv_pw", "w_out", "ffn_up", "ffn_down")
_ORDER = ("meta", "norm_mix_g", "w_in", "q_norm_g", "k_norm_g", "attn_sinks", "attn_out_g", "cv_dw_w", "cv_dw_b",
          "cv_ln_g", "cv_ln_b", "cv_pw", "cv_out_g", "ret_gn_g", "w_out", "norm_ffn_g", "ffn_up", "ffn_dw_w",
          "ffn_dw_b", "ffn_down")
_SMALL_SHARDED = {"meta": D, "cv_dw_w": CV, "ffn_dw_w": 2 * D_FF}


def _pack(arrs):
    flat = jnp.concatenate([a.reshape(-1) for a in arrs])
    n = flat.shape[0]
    rows = -(-n // 1024) * 8
    return jnp.pad(flat, (0, rows * 128 - n)).reshape(rows, 128)


def _unpack(packed, shapes):
    flat = packed.reshape(-1)
    out, off = [], 0
    for s in shapes:
        n = int(np.prod(s))
        out.append(flat[off:off + n].reshape(s))
        off += n
    return out


def kernel(x, meta, norm_mix_g, w_in, q_norm_g, k_norm_g, attn_sinks, attn_out_g, cv_dw_w, cv_dw_b, cv_ln_g, cv_ln_b, cv_pw, cv_out_g, ret_gn_g, w_out, norm_ffn_g, ffn_up, ffn_dw_w, ffn_dw_b, ffn_down, loss_target, m_meta, m_norm_mix_g, m_w_in, m_q_norm_g, m_k_norm_g, m_attn_sinks, m_attn_out_g, m_cv_dw_w, m_cv_dw_b, m_cv_ln_g, m_cv_ln_b, m_cv_pw, m_cv_out_g, m_ret_gn_g, m_w_out, m_norm_ffn_g, m_ffn_up, m_ffn_dw_w, m_ffn_dw_b, m_ffn_down, v_meta, v_norm_mix_g, v_w_in, v_q_norm_g, v_k_norm_g, v_attn_sinks, v_attn_out_g, v_cv_dw_w, v_cv_dw_b, v_cv_ln_g, v_cv_ln_b, v_cv_pw, v_cv_out_g, v_ret_gn_g, v_w_out, v_norm_ffn_g, v_ffn_up, v_ffn_dw_w, v_ffn_dw_b, v_ffn_down):
    W = dict(meta=meta, norm_mix_g=norm_mix_g, w_in=w_in, q_norm_g=q_norm_g, k_norm_g=k_norm_g,
             attn_sinks=attn_sinks, attn_out_g=attn_out_g, cv_dw_w=cv_dw_w, cv_dw_b=cv_dw_b, cv_ln_g=cv_ln_g,
             cv_ln_b=cv_ln_b, cv_pw=cv_pw, cv_out_g=cv_out_g, ret_gn_g=ret_gn_g, w_out=w_out,
             norm_ffn_g=norm_ffn_g, ffn_up=ffn_up, ffn_dw_w=ffn_dw_w, ffn_dw_b=ffn_dw_b, ffn_down=ffn_down)
    M = dict(meta=m_meta, norm_mix_g=m_norm_mix_g, w_in=m_w_in, q_norm_g=m_q_norm_g, k_norm_g=m_k_norm_g,
             attn_sinks=m_attn_sinks, attn_out_g=m_attn_out_g, cv_dw_w=m_cv_dw_w, cv_dw_b=m_cv_dw_b,
             cv_ln_g=m_cv_ln_g, cv_ln_b=m_cv_ln_b, cv_pw=m_cv_pw, cv_out_g=m_cv_out_g, ret_gn_g=m_ret_gn_g,
             w_out=m_w_out, norm_ffn_g=m_norm_ffn_g, ffn_up=m_ffn_up, ffn_dw_w=m_ffn_dw_w, ffn_dw_b=m_ffn_dw_b,
             ffn_down=m_ffn_down)
    V = dict(meta=v_meta, norm_mix_g=v_norm_mix_g, w_in=v_w_in, q_norm_g=v_q_norm_g, k_norm_g=v_k_norm_g,
             attn_sinks=v_attn_sinks, attn_out_g=v_attn_out_g, cv_dw_w=v_cv_dw_w, cv_dw_b=v_cv_dw_b,
             cv_ln_g=v_cv_ln_g, cv_ln_b=v_cv_ln_b, cv_pw=v_cv_pw, cv_out_g=v_cv_out_g, ret_gn_g=v_ret_gn_g,
             w_out=v_w_out, norm_ffn_g=v_norm_ffn_g, ffn_up=v_ffn_up, ffn_dw_w=v_ffn_dw_w, ffn_dw_b=v_ffn_dw_b,
             ffn_down=v_ffn_down)
    me = 4 * lax.axis_index("x") + 2 * lax.axis_index("y") + lax.axis_index("c")

    sh = {n: [W[n][l].astype(BF16) for l in range(DEPTH)] for n in _BIG}
    mix = lambda l: [sh["w_in"][l], sh["cv_pw"][l], sh["w_out"][l]]
    ffn = lambda l: [sh["ffn_up"][l], sh["ffn_down"][l]]
    first = all_gather(mix(0) + [meta, cv_dw_w, ffn_dw_w], "gather_first")
    g_meta, g_cdw, g_fdw = first[3:6]

    def landing(shards):
        return [lax.dynamic_update_slice(lax.empty((NDEV,) + s.shape, s.dtype), s[None], (me,) + (0,) * s.ndim)
                for s in shards]

    gathers = {("ffn", 0): gather_start("gather_ffn0_start", ffn(0), landing(ffn(0)), after=[first[0]])}
    gathers["mix", 1] = gather_start("gather_mix1_start", mix(1), landing(mix(1)), after=[gathers["ffn", 0][4]])
    gathers["ffn", 1] = gather_start("gather_ffn1_start", ffn(1), landing(ffn(1)), after=[gathers["mix", 1][4]])

    def mixer_w(l, h):
        g_in, g_pw, g_out = first[0:3] if l == 0 else gather_wait(f"gather_mix{l}_wait", gathers["mix", l], after=[h])[1]
        return g_in.transpose(1, 0, 2).reshape(D, IN_W), g_pw.reshape(CV, CV), g_out.reshape(D, D)

    def ffn_w(l, h1):
        return gather_wait(f"gather_ffn{l}_wait", gathers["ffn", l], after=[h1])[1]

    scatters = {}

    def ffn_grads(l, dw_down, dw_up):
        scatters["ffn", l] = scatter_start(f"scatter_ffn{l}_start", [dw_up, dw_down.reshape(NDEV, DNR, D)])
        return scatters["ffn", l][4]

    def mixer_grads(l, dw_out, dpw, dw_in):
        grads = [dw_in.reshape(D, NDEV, IN_W // NDEV).transpose(1, 0, 2), dpw.astype(BF16).reshape(NDEV, CV // NDEV, CV),
                 dw_out.reshape(NDEV, D // NDEV, D)]
        scatters["mix", l] = scatter_start(f"scatter_mix{l}_start", grads)
        return scatters["mix", l][4]

    P = dict(
        meta=g_meta.transpose(1, 0, 2).reshape(N_META, D),
        cv_dw_w=g_cdw.transpose(1, 2, 0, 3).reshape(DEPTH, CONV_K, CV),
        ffn_dw_w=g_fdw.transpose(1, 0, 2, 3),
        ffn_dw_b=ffn_dw_b.reshape(DEPTH, NDEV, 1, UPW),
        attn_sinks=attn_sinks,
        ret_gn_g=ret_gn_g.reshape(DEPTH, RH, 1, HD),
    )
    for n in ("norm_mix_g", "q_norm_g", "k_norm_g", "attn_out_g", "cv_dw_b", "cv_ln_g", "cv_ln_b", "cv_out_g",
              "norm_ffn_g"):
        P[n] = W[n].reshape(DEPTH, 1, -1)

    hooks = dict(mixer_w=mixer_w, ffn_w=ffn_w, ffn_grads=ffn_grads, mixer_grads=mixer_grads,
                 first_after=gathers["ffn", 1][4])
    loss_part, dx, dmeta, G = local_step(x[0], loss_target[0], P, hooks)
    loss = lax.psum(loss_part, ("x", "y", "c"))

    small_full = {
        "meta": dmeta,
        "cv_dw_w": jnp.stack(G["cv_dw_w"]),
        "ffn_dw_w": jnp.stack([g.transpose(1, 0, 2).reshape(FFN_K, 2 * D_FF) for g in G["ffn_dw_w"]]),
        "ffn_dw_b": jnp.stack([g.reshape(2 * D_FF) for g in G["ffn_dw_b"]]),
        "attn_sinks": jnp.stack(G["attn_sinks"]),
        "ret_gn_g": jnp.stack([g.reshape(RH * HD) for g in G["ret_gn_g"]]),
    }
    for n in _SMALL:
        if n not in small_full:
            small_full[n] = jnp.stack([g.reshape(-1) for g in G[n]])
    shapes = [small_full[n].shape for n in _SMALL]
    summed = _unpack(all_reduce_small(_pack([small_full[n] for n in _SMALL]), "reduce_small"), shapes)
    small_g = {}
    for n, g in zip(_SMALL, summed):
        if n in _SMALL_SHARDED:
            width = _SMALL_SHARDED[n] // NDEV
            g = lax.dynamic_slice_in_dim(g, me * width, width, axis=g.ndim - 1)
        small_g[n] = g

    out = {}
    tiles = {"w_in": 256, "cv_pw": 32, "w_out": 128, "ffn_up": 256, "ffn_down": 176}
    me1 = me.astype(jnp.int32).reshape(1)
    own, lands = {n: [None] * DEPTH for n in _BIG}, {n: [None] * DEPTH for n in _BIG}

    def arrived(kind, names, after):
        for l in range(DEPTH):
            srcs, got = scatter_wait(f"scatter_{kind}{l}_wait", scatters[kind, l], after=after)
            for n, s_, g_ in zip(names, srcs, got):
                own[n][l], lands[n][l] = s_, g_

    def update(names):
        for n in names:
            shard = W[n].shape
            rows, cols = shard[0] * shard[1], shard[2]
            res = adamw_big(own[n], lands[n], W[n].reshape(rows, cols), M[n].reshape(rows, cols),
                            V[n].reshape(rows, cols), me1, tiles[n], f"adamw_{n}")
            out[n] = [r.reshape(shard) for r in res]

    arrived("ffn", ("ffn_up", "ffn_down"), [dx])
    update(("ffn_up", "ffn_down"))
    sshapes = [W[n].shape for n in _SMALL]
    packed_g = _pack([small_g[n] for n in _SMALL])
    res = adamw(packed_g[None], _pack([W[n] for n in _SMALL]), _pack([M[n] for n in _SMALL]),
                _pack([V[n] for n in _SMALL]), packed_g.shape[0], "adamw_small")
    for i, r in enumerate(res):
        for n, a in zip(_SMALL, _unpack(r, sshapes)):
            out.setdefault(n, [None] * 4)[i] = a
    arrived("mix", ("w_in", "cv_pw", "w_out"), [out["ffn_down"][0], res[0]])
    update(("w_in", "cv_pw", "w_out"))

    return (loss, dx[None], *[out[n][0] for n in _ORDER], *[out[n][1] for n in _ORDER],
            *[out[n][2] for n in _ORDER], *[out[n][3] for n in _ORDER])
```

```python
import functools
import math

import numpy as np
import jax
import jax.numpy as jnp
from jax import lax
from jax.experimental import pallas as pl
from jax.experimental.pallas import tpu as pltpu

F32 = jnp.float32
BF16 = jnp.bfloat16

D = 1024
SEQ = 2048
DEPTH = 2
T = 128
L = SEQ + T
NB = L // T
N_META = 16
PAD = T - N_META
HD = 64
NQ = 8
NKV = 2
GQA = NQ // NKV
CV = 256
CONV_K = 31
RH = 4
D_FF = 2816
FFN_K = 3
IN_W = 2304
RMS_EPS = 1e-6
LN_EPS = 1e-5
NEG = -1e30
NDEV = 8
UPW = 2 * D_FF // NDEV
DNR = D_FF // NDEV
NPAIR = NDEV // 2

ADAM_LR, ADAM_B1, ADAM_B2, ADAM_EPS, ADAM_WD, ADAM_STEP = 0.001, 0.9, 0.999, 1e-08, 0.01, 10

VMEM_BIG = 56 * 1024 * 1024

MESH = pl.DeviceIdType.MESH


def _params(vmem=None):
    return pltpu.CompilerParams(vmem_limit_bytes=vmem) if vmem else None


def _dot(a, b, nt=False):
    return lax.dot_general(a, b, (((1,), (1 if nt else 0,)), ((), ())), preferred_element_type=F32)


def _sig(x):
    return 1.0 / (1.0 + jnp.exp(-x))


def _bf(x):
    return x.astype(BF16)


def _stack_rows(rows):
    idx = lax.broadcasted_iota(jnp.int32, (len(rows), rows[0].shape[1]), 0)
    out = jnp.zeros((len(rows), rows[0].shape[1]), F32)
    for r, v in enumerate(rows):
        out = jnp.where(idx == r, v, out)
    return out


def rms_fwd(x, g, out_dtype, name, after=None):
    n, w = x.shape

    def body(x_ref, g_ref, *rest):
        o_ref = rest[-1]
        xv = x_ref[...]
        r = lax.rsqrt(jnp.mean(xv * xv, axis=-1, keepdims=True) + RMS_EPS)
        o_ref[...] = (xv * r * g_ref[...]).astype(o_ref.dtype)

    deps = [] if after is None else [after]
    return pl.pallas_call(
        body, name=name, out_shape=jax.ShapeDtypeStruct((n, w), out_dtype), grid=(n // T,),
        in_specs=[pl.BlockSpec((T, w), lambda i: (i, 0)), pl.BlockSpec((1, w), lambda i: (0, 0))] + [_HBM] * len(deps),
        out_specs=pl.BlockSpec((T, w), lambda i: (i, 0)))(x, g, *deps)


def rms_bwd(x, g, dy, dres, name, after=None, dy_col=0):
    n, w = x.shape
    has_res = dres is not None
    deps = [] if after is None else [after]

    def body(x_ref, g_ref, dy_ref, *rest):
        rest = rest[:len(rest) - 3 - len(deps)] + rest[len(rest) - 3:]
        if has_res:
            dres_ref, dx_ref, dxm_ref, dg_ref = rest
        else:
            dx_ref, dxm_ref, dg_ref = rest
        i = pl.program_id(0)
        xv = x_ref[...]
        r = lax.rsqrt(jnp.mean(xv * xv, axis=-1, keepdims=True) + RMS_EPS)
        xh = xv * r
        dyv = dy_ref[...]
        dxh = dyv * g_ref[...]
        dx = r * (dxh - xh * jnp.mean(dxh * xh, axis=-1, keepdims=True))
        if has_res:
            dx = dx + dres_ref[...]
        dx_ref[...] = dx
        rows = i * T + lax.broadcasted_iota(jnp.int32, (T, 1), 0)
        dxm_ref[...] = jnp.where(rows >= PAD, dx, 0.0).astype(BF16)
        part = jnp.sum(dyv * xh, axis=0, keepdims=True)

        @pl.when(i == 0)
        def _():
            dg_ref[...] = part

        @pl.when(i > 0)
        def _():
            dg_ref[...] += part

    row = pl.BlockSpec((T, w), lambda i: (i, 0))
    vec = pl.BlockSpec((1, w), lambda i: (0, 0))
    ins = [x, g, dy] + ([dres] if has_res else []) + deps
    return pl.pallas_call(
        body, name=name,
        out_shape=(jax.ShapeDtypeStruct((n, w), F32), jax.ShapeDtypeStruct((n, w), BF16),
                   jax.ShapeDtypeStruct((1, w), F32)),
        grid=(n // T,),
        in_specs=[row, vec, pl.BlockSpec((T, w), lambda i: (i, dy_col))] + ([row] if has_res else [])
        + [_HBM] * len(deps),
        out_specs=(row, row, vec))(*ins)


def loss_head(h, target):
    def body(h_ref, t_ref, dh_ref, dhm_ref, loss_ref):
        n = pl.program_id(0)
        e = jnp.where(n > 0, h_ref[...] - t_ref[...], 0.0)
        dh = e * (1.0 / D)
        dh_ref[...] = dh
        dhm_ref[...] = dh.astype(BF16)
        part = jnp.sum(jnp.sum(e * e, axis=1, keepdims=True), axis=0, keepdims=True) * (0.5 / D)

        @pl.when(n == 0)
        def _():
            loss_ref[...] = jnp.zeros_like(loss_ref)

        @pl.when(n > 0)
        def _():
            loss_ref[...] += jnp.broadcast_to(part, loss_ref.shape)

    row = pl.BlockSpec((T, D), lambda n: (n, 0))
    return pl.pallas_call(
        body, name="loss_head",
        out_shape=(jax.ShapeDtypeStruct((L, D), F32), jax.ShapeDtypeStruct((L, D), BF16),
                   jax.ShapeDtypeStruct((8, 128), F32)),
        grid=(NB,),
        in_specs=[row, pl.BlockSpec((T, D), lambda n: (jnp.maximum(n - 1, 0), 0))],
        out_specs=(row, row, pl.BlockSpec((8, 128), lambda n: (0, 0))))(h, target)


def _mm(name, a, b, *, grid, a_spec, b_spec, o_spec, out_shape, nt=False, red=False, res=None, res_spec=None):
    def body(a_ref, b_ref, *rest):
        o_ref = rest[-1]
        av = a_ref[...]
        bv = b_ref[...]
        if bv.ndim == 3:
            bv = bv.reshape(bv.shape[0] * bv.shape[1], bv.shape[2])
        acc = _dot(av, bv, nt)
        if red:
            k = pl.program_id(0)

            @pl.when(k == 0)
            def _():
                o_ref[...] = acc

            @pl.when(k > 0)
            def _():
                o_ref[...] += acc
        else:
            if res is not None:
                rows = lax.broadcasted_iota(jnp.int32, (acc.shape[0], 1), 0)
                acc = rest[0][...] + jnp.where(rows >= PAD, acc, 0.0)
            o_ref[...] = acc.astype(o_ref.dtype)

    ins = [a, b] + ([res] if res is not None else [])
    specs = [a_spec, b_spec] + ([res_spec] if res is not None else [])
    return pl.pallas_call(body, name=name, out_shape=out_shape, grid=grid, in_specs=specs, out_specs=o_spec,
                          compiler_params=_params(VMEM_BIG))(*ins)


def mm_nn(name, a, b, tn, out_dtype=F32, res=None):
    m, k = a.shape
    n = b.shape[1]
    return _mm(name, a, b, grid=(n // tn,),
               a_spec=pl.BlockSpec((m, k), lambda j: (0, 0)), b_spec=pl.BlockSpec((k, tn), lambda j: (0, j)),
               o_spec=pl.BlockSpec((m, tn), lambda j: (0, j)), out_shape=jax.ShapeDtypeStruct((m, n), out_dtype),
               res=res, res_spec=pl.BlockSpec((m, tn), lambda j: (0, j)))


def mm_nt(name, a, b, tn):
    m, k = a.shape
    n = b.shape[0]
    return _mm(name, a, b, grid=(n // tn,), nt=True,
               a_spec=pl.BlockSpec((m, k), lambda j: (0, 0)), b_spec=pl.BlockSpec((tn, k), lambda j: (j, 0)),
               o_spec=pl.BlockSpec((m, tn), lambda j: (0, j)), out_shape=jax.ShapeDtypeStruct((m, n), F32))


def ffn_up_fwd(name, u2, wup):
    return _mm(name, u2, wup, grid=(NDEV,),
               a_spec=pl.BlockSpec((L, D), lambda j: (0, 0)),
               b_spec=pl.BlockSpec((None, D, UPW), lambda j: (j, 0, 0)),
               o_spec=pl.BlockSpec((None, L, UPW), lambda j: (j, 0, 0)),
               out_shape=jax.ShapeDtypeStruct((NDEV, L, UPW), F32))


def _slab_specs():
    gate = pl.BlockSpec((None, L, UPW), lambda j: (jnp.minimum(j, NPAIR - 1), 0, 0))
    up = pl.BlockSpec((None, L, UPW), lambda j: (jnp.maximum(j - NPAIR, 0), 0, 0))
    return gate, up


def ffn_up_dx(name, dfg, dfu, wup):
    def body(g_ref, u_ref, b_ref, o_ref):
        j = pl.program_id(0)

        @pl.when(j == 0)
        def _():
            o_ref[...] = _dot(g_ref[...], b_ref[...], nt=True)

        @pl.when(jnp.logical_and(j > 0, j < NPAIR))
        def _():
            o_ref[...] += _dot(g_ref[...], b_ref[...], nt=True)

        @pl.when(j >= NPAIR)
        def _():
            o_ref[...] += _dot(u_ref[...], b_ref[...], nt=True)

    gate, up = _slab_specs()
    return pl.pallas_call(
        body, name=name, out_shape=jax.ShapeDtypeStruct((L, D), F32), grid=(NDEV,),
        in_specs=[gate, up, pl.BlockSpec((None, D, UPW), lambda j: (j, 0, 0))],
        out_specs=pl.BlockSpec((L, D), lambda j: (0, 0)), compiler_params=_params(VMEM_BIG))(dfg, dfu, wup)


def ffn_up_dw(name, u2t, dfg, dfu):
    def body(a_ref, g_ref, u_ref, o_ref):
        j = pl.program_id(0)

        @pl.when(j < NPAIR)
        def _():
            o_ref[...] = _dot(a_ref[...], g_ref[...]).astype(o_ref.dtype)

        @pl.when(j >= NPAIR)
        def _():
            o_ref[...] = _dot(a_ref[...], u_ref[...]).astype(o_ref.dtype)

    gate, up = _slab_specs()
    return pl.pallas_call(
        body, name=name, out_shape=jax.ShapeDtypeStruct((NDEV, D, UPW), BF16), grid=(NDEV,),
        in_specs=[pl.BlockSpec((D, L), lambda j: (0, 0)), gate, up],
        out_specs=pl.BlockSpec((None, D, UPW), lambda j: (j, 0, 0)), compiler_params=_params(VMEM_BIG))(u2t, dfg, dfu)


def ffn_down_fwd(name, a, wdn, res, tn=256):
    def body(a_ref, b_ref, r_ref, o_ref):
        acc = jnp.zeros((L, tn), F32)
        for g in range(NPAIR):
            bv = b_ref[2 * g:2 * g + 2]
            acc = acc + _dot(a_ref[g], bv.reshape(2 * DNR, tn))
        rows = lax.broadcasted_iota(jnp.int32, (L, 1), 0)
        o_ref[...] = r_ref[...] + jnp.where(rows >= PAD, acc, 0.0)

    return pl.pallas_call(
        body, name=name, out_shape=jax.ShapeDtypeStruct((L, D), F32), grid=(D // tn,),
        in_specs=[pl.BlockSpec((NPAIR, L, UPW), lambda j: (0, 0, 0)),
                  pl.BlockSpec((NDEV, DNR, tn), lambda j: (0, 0, j)),
                  pl.BlockSpec((L, tn), lambda j: (0, j))],
        out_specs=pl.BlockSpec((L, tn), lambda j: (0, j)),
        compiler_params=_params(VMEM_BIG))(a, wdn, res)


def ffn_down_dx(name, dh, wdn):
    return _mm(name, dh, wdn, grid=(NPAIR,), nt=True,
               a_spec=pl.BlockSpec((L, D), lambda g: (0, 0)),
               b_spec=pl.BlockSpec((2, DNR, D), lambda g: (g, 0, 0)),
               o_spec=pl.BlockSpec((None, L, UPW), lambda g: (g, 0, 0)),
               out_shape=jax.ShapeDtypeStruct((NPAIR, L, UPW), F32))


def ffn_down_dw(name, at, dh):
    return _mm(name, at, dh, grid=(NPAIR,),
               a_spec=pl.BlockSpec((None, UPW, L), lambda g: (g, 0, 0)),
               b_spec=pl.BlockSpec((L, D), lambda g: (0, 0)),
               o_spec=pl.BlockSpec((UPW, D), lambda g: (g, 0)),
               out_shape=jax.ShapeDtypeStruct((D_FF, D), BF16))


_SLOPES = [2.0 ** (-8.0 * (h + 1) / NQ) for h in range(NQ)]
_SCALE = HD ** -0.5


QR = GQA * T
KC = 3 * T


def _attn_geometry(g, n, sink_ref):
    row = lax.broadcasted_iota(jnp.int32, (QR, KC), 0)
    col = lax.broadcasted_iota(jnp.int32, (QR, KC), 1)
    i = row & (T - 1)
    j = col & (T - 1)
    blk = col // T
    d_meta = n * T + i - j
    ok_meta = (j >= PAD) & (d_meta >= 0)
    ok_prev = j > i + jnp.where(n >= 2, 0, T)
    ok_cur = j <= i - jnp.where(n >= 1, 0, T)
    ok = ((blk == 0) & ok_meta) | ((blk == 1) & ok_prev) | ((blk == 2) & ok_cur)
    dist = jnp.where(blk == 0, jnp.minimum(d_meta, T), jnp.where(blk == 1, T + i - j, i - j)).astype(F32)
    head = lax.broadcasted_iota(jnp.int32, (QR, 1), 0) // T
    slope = jnp.zeros((QR, 1), F32)
    sink = jnp.zeros((QR, 1), F32)
    for hh in range(GQA):
        slope = jnp.where(head == hh, jnp.where(g == 0, _SLOPES[hh], _SLOPES[GQA + hh]), slope)
        sink = jnp.where(head == hh, sink_ref[g * GQA + hh], sink)
    return ok, slope * dist, sink


def _rms_rows(x, g):
    r = lax.rsqrt(jnp.mean(x * x, axis=-1, keepdims=True) + RMS_EPS)
    xh = x * r
    return xh * g, xh, r


def _rms_rows_bwd(dy, g, xh, r):
    dxh = dy * g
    return r * (dxh - xh * jnp.mean(dxh * xh, axis=-1, keepdims=True))


def _attn_specs():
    qspec = pl.BlockSpec((GQA, T, HD), lambda g, n: (g, n, 0))
    cur = pl.BlockSpec((None, T, HD), lambda g, n: (g, n, 0))
    prev = pl.BlockSpec((None, T, HD), lambda g, n: (g, jnp.maximum(n - 1, 0), 0))
    meta = pl.BlockSpec((None, T, HD), lambda g, n: (g, 0, 0))
    vec = pl.BlockSpec((1, HD), lambda g, n: (0, 0))
    return qspec, cur, prev, meta, vec


def _attn_probs(qn, keys, geo):
    ok, penalty, sink = geo
    s = jnp.where(ok, _dot(qn, keys, nt=True) * _SCALE - penalty, NEG)
    m = jnp.maximum(jnp.max(s, axis=-1, keepdims=True), sink)
    e = jnp.exp(s - m)
    e_sink = jnp.exp(sink - m)
    inv = 1.0 / (e.sum(axis=-1, keepdims=True) + e_sink)
    return e, inv, e_sink * inv


def _rows3(m_ref, p_ref, c_ref):
    return jnp.concatenate([m_ref[...], p_ref[...], c_ref[...]], axis=0)


def attn_fwd(q, k, v, qg, kg, sinks, name):
    def body(sink_ref, q_ref, kc_ref, kp_ref, km_ref, vc_ref, vp_ref, vm_ref, qg_ref, kg_ref, o_ref):
        g = pl.program_id(0)
        n = pl.program_id(1)
        geo = _attn_geometry(g, n, sink_ref)
        keys = _bf(_rms_rows(_rows3(km_ref, kp_ref, kc_ref), kg_ref[...])[0])
        vals = _bf(_rows3(vm_ref, vp_ref, vc_ref))
        qn = _bf(_rms_rows(q_ref[...].reshape(QR, HD), qg_ref[...])[0])
        e, inv, _ = _attn_probs(qn, keys, geo)
        o_ref[...] = (_dot(_bf(e), vals) * inv).reshape(GQA, T, HD)

    qspec, cur, prev, meta, vec = _attn_specs()
    return pl.pallas_call(
        body, name=name, out_shape=jax.ShapeDtypeStruct((NQ, L, HD), F32), grid=(NKV, NB),
        in_specs=[pl.BlockSpec(memory_space=pltpu.SMEM), qspec, cur, prev, meta, cur, prev, meta, vec, vec],
        out_specs=qspec)(sinks, q, k, k, k, v, v, v, qg, kg)


def attn_bwd(q, k, v, do, qg, kg, sinks, name):
    def body(sink_ref, q_ref, kc_ref, kp_ref, km_ref, vc_ref, vp_ref, vm_ref, do_ref, qg_ref, kg_ref,
             dq_ref, dkc_ref, dkp_ref, dkm_ref, dvc_ref, dvp_ref, dvm_ref, dqg_ref, dkg_ref, dsk_ref):
        g = pl.program_id(0)
        n = pl.program_id(1)
        geo = _attn_geometry(g, n, sink_ref)
        kgv = kg_ref[...]
        qgv = qg_ref[...]
        kn_f, kh, kr = _rms_rows(_rows3(km_ref, kp_ref, kc_ref), kgv)
        keys = _bf(kn_f)
        vals = _bf(_rows3(vm_ref, vp_ref, vc_ref))
        qn_f, qh, qr = _rms_rows(q_ref[...].reshape(QR, HD), qgv)
        qn = _bf(qn_f)
        e, inv, p_sink = _attn_probs(qn, keys, geo)
        dob = _bf(do_ref[...].reshape(QR, HD))
        p = e * inv
        dp = _dot(dob, vals, nt=True)
        delta = (p * dp).sum(axis=-1, keepdims=True)
        dsk = _stack_rows([jnp.broadcast_to(-jnp.sum((p_sink * delta)[hh * T:(hh + 1) * T], axis=0, keepdims=True),
                                            (1, 128)) for hh in range(GQA)])
        ds = p * (dp - delta)
        dqn = _dot(_bf(ds), keys) * _SCALE
        dq_ref[...] = _rms_rows_bwd(dqn, qgv, qh, qr).reshape(GQA, T, HD)
        dqg_acc = jnp.sum(dqn * qh, axis=0, keepdims=True)
        dkn = _dot(_bf(ds.T), qn) * _SCALE
        dkg_acc = jnp.sum(dkn * kh, axis=0, keepdims=True)
        dk_all = _rms_rows_bwd(dkn, kgv, kh, kr)
        dv_all = _dot(_bf(p.T), dob)
        dk_raw = [dk_all[b * T:(b + 1) * T] for b in range(3)]
        dvv = [dv_all[b * T:(b + 1) * T] for b in range(3)]
        dkp_ref[...] = dk_raw[1]
        dkc_ref[...] = dk_raw[2]
        dvp_ref[...] = dvv[1]
        dvc_ref[...] = dvv[2]
        first = jnp.logical_and(g == 0, n == 0)

        @pl.when(n == 0)
        def _():
            dkm_ref[...] = dk_raw[0]
            dvm_ref[...] = dvv[0]
            dsk_ref[...] = dsk

        @pl.when(n > 0)
        def _():
            dkm_ref[...] += dk_raw[0]
            dvm_ref[...] += dvv[0]
            dsk_ref[...] += dsk

        @pl.when(first)
        def _():
            dqg_ref[...] = dqg_acc
            dkg_ref[...] = dkg_acc

        @pl.when(jnp.logical_not(first))
        def _():
            dqg_ref[...] += dqg_acc
            dkg_ref[...] += dkg_acc

    qspec, cur, prev, meta, vec = _attn_specs()
    kv_shape = jax.ShapeDtypeStruct((NKV, L, HD), F32)
    meta_shape = jax.ShapeDtypeStruct((NKV, T, HD), F32)
    vec_shape = jax.ShapeDtypeStruct((1, HD), F32)
    return pl.pallas_call(
        body, name=name,
        out_shape=(jax.ShapeDtypeStruct((NQ, L, HD), F32), kv_shape, kv_shape, meta_shape, kv_shape, kv_shape,
                   meta_shape, vec_shape, vec_shape, jax.ShapeDtypeStruct((NKV, GQA, 128), F32)),
        grid=(NKV, NB),
        in_specs=[pl.BlockSpec(memory_space=pltpu.SMEM), qspec, cur, prev, meta, cur, prev, meta, qspec, vec, vec],
        out_specs=(qspec, cur, cur, meta, cur, cur, meta, vec, vec,
                   pl.BlockSpec((None, GQA, 128), lambda g, n: (g, 0, 0))))(sinks, q, k, k, k, v, v, v, do, qg, kg)


RW = RH * HD


def _ret_tables():
    h = np.arange(RH, dtype=np.float64)
    lg = np.log1p(-np.exp2(-5.0 - h))
    idx = np.arange(T, dtype=np.float64)
    diff = idx[:, None] - idx[None, :]
    decay = np.where(diff[None] >= 0, np.exp(np.maximum(diff, 0.0)[None] * lg[:, None, None]), 0.0)
    zeta = np.exp((T - 1 - idx)[None, :] * lg[:, None])
    xi = np.exp((idx + 1.0)[None, :] * lg[:, None])
    cd = np.exp(T * lg)
    lanes = lambda a: np.repeat(a.T, HD, axis=1)
    head_of = np.arange(RW) // HD
    same = (head_of[:, None] == head_of[None, :]).astype(np.float64)
    f = lambda a: jnp.asarray(a, F32)
    return dict(decay=f(decay), zeta=f(lanes(zeta)), xi=f(lanes(xi)), cd=f(np.repeat(cd, HD)[None, :]),
                head=f((head_of[None, :] == np.arange(RH)[:, None]).astype(np.float64)[:, None, :]),
                same=f(same), avg=jnp.asarray(same / HD, BF16))


def _seg_mean(x, avg):
    hi = _bf(x)
    lo = _bf(x - hi.astype(F32))
    return _dot(hi, avg) + _dot(lo, avg)


def _ret_specs(col0, order):
    return lambda col: pl.BlockSpec((T, RW), lambda i: (order(i), col0 + col))


def _ret_chunk(q, kf, v, s, tb):
    dec, xi, head = tb
    vb = _bf(v)
    kb = _bf(kf)
    y = _dot(_bf(q * xi), _bf(s))
    a = []
    for h in range(RH):
        a.append(_dot(_bf(q * head[h]), kb, nt=True) * dec[h])
        y = y + head[h] * _dot(_bf(a[h]), vb)
    return a, y


def _gn_rows(y):
    mu = jnp.mean(y, axis=-1, keepdims=True)
    yc = y - mu
    rstd = lax.rsqrt(jnp.mean(yc * yc, axis=-1, keepdims=True) + LN_EPS)
    return yc * rstd, rstd


_RET_COL0 = _RET_Q = 5


def _ret_consts(tb):
    names = ("decay", "zeta", "xi", "cd", "head", "same", "avg")
    full = lambda a: pl.BlockSpec(a.shape, lambda i: (0,) * a.ndim)
    return [tb[n] for n in names], [full(tb[n]) for n in names]


def _ret_inputs(n, q_ref, k_ref, v_ref):
    rows = n * T + lax.broadcasted_iota(jnp.int32, (T, 1), 0)
    valid = rows >= PAD
    return q_ref[...], jnp.where(valid, k_ref[...] * (HD ** -0.5), 0.0), v_ref[...], valid


def ret_fwd(proj, gng, tb, name):
    def body(q_ref, k_ref, v_ref, g_ref, dec_ref, zeta_ref, xi_ref, cd_ref, head_ref, same_ref, avg_ref, gng_ref,
             y_ref, st_ref, s_scr):
        n = pl.program_id(0)

        @pl.when(n == 0)
        def _():
            s_scr[...] = jnp.zeros_like(s_scr)

        s = s_scr[...]
        st_ref[...] = s
        q, kf, v, _ = _ret_inputs(n, q_ref, k_ref, v_ref)
        _, y = _ret_chunk(q, kf, v, s, (dec_ref, xi_ref[...], head_ref))
        s_scr[...] = cd_ref[...] * s + same_ref[...] * _dot(_bf((kf * zeta_ref[...]).T), _bf(v))
        avg = avg_ref[...]
        yc = y - _seg_mean(y, avg)
        yh = yc * lax.rsqrt(_seg_mean(yc * yc, avg) + LN_EPS)
        gv = g_ref[...]
        y_ref[...] = gv * _sig(gv) * (yh * gng_ref[...])

    col = _ret_specs(_RET_COL0, lambda i: i)
    consts, cspecs = _ret_consts(tb)
    return pl.pallas_call(
        body, name=name,
        out_shape=(jax.ShapeDtypeStruct((L, RW), F32), jax.ShapeDtypeStruct((NB, RW, RW), F32)),
        grid=(NB,),
        in_specs=[col(0), col(1), col(2), col(3)] + cspecs + [pl.BlockSpec((1, RW), lambda i: (0, 0))],
        out_specs=(pl.BlockSpec((T, RW), lambda i: (i, 0)), pl.BlockSpec((None, RW, RW), lambda i: (i, 0, 0))),
        scratch_shapes=[pltpu.VMEM((RW, RW), F32)])(proj, proj, proj, proj, *consts, gng)


def ret_bwd(proj, dycat, states, gng, tb, name):
    def body(q_ref, k_ref, v_ref, g_ref, do_ref, st_ref, dec_ref, zeta_ref, xi_ref, cd_ref, head_ref, same_ref,
             avg_ref, gng_ref, d_ref, dgn_ref, ds_scr):
        i = pl.program_id(0)
        n = NB - 1 - i

        @pl.when(i == 0)
        def _():
            ds_scr[...] = jnp.zeros_like(ds_scr)

        dsn = _bf(ds_scr[...])
        s = st_ref[...]
        sb = _bf(s)
        q, kf, v, valid = _ret_inputs(n, q_ref, k_ref, v_ref)
        xi, zeta, avg = xi_ref[...], zeta_ref[...], avg_ref[...]
        a, y = _ret_chunk(q, kf, v, s, (dec_ref, xi, head_ref))
        yc = y - _seg_mean(y, avg)
        rstd = lax.rsqrt(_seg_mean(yc * yc, avg) + LN_EPS)
        yh = yc * rstd
        gv = g_ref[...]
        sg = _sig(gv)
        sil = gv * sg
        gn = gng_ref[...]
        dout = do_ref[...]
        d_ref[:, 3 * RW:4 * RW] = dout * (yh * gn) * (sg * (1.0 + gv * (1.0 - sg)))
        dyh = dout * sil * gn
        part = jnp.sum(dout * sil * yh, axis=0, keepdims=True)

        @pl.when(i == 0)
        def _():
            dgn_ref[...] = part

        @pl.when(i > 0)
        def _():
            dgn_ref[...] += part

        dy = rstd * (dyh - _seg_mean(dyh, avg) - yh * _seg_mean(dyh * yh, avg))
        dyb = _bf(dy)
        vb, kb, qb = _bf(v), _bf(kf), _bf(q)
        dq = _dot(dyb, sb, nt=True) * xi
        dkf = _dot(vb, dsn, nt=True) * zeta
        dv = _dot(_bf(kf * zeta), dsn)
        for h in range(RH):
            m = head_ref[h]
            da = _dot(_bf(dy * m), vb, nt=True) * dec_ref[h]
            dv = dv + m * _dot(_bf(a[h].T), dyb)
            dq = dq + m * _dot(_bf(da), kb)
            dkf = dkf + m * _dot(_bf(da.T), qb)
        d_ref[:, 0:RW] = dq
        d_ref[:, RW:2 * RW] = jnp.where(valid, dkf * (HD ** -0.5), 0.0)
        d_ref[:, 2 * RW:3 * RW] = dv
        ds_scr[...] = cd_ref[...] * ds_scr[...] + same_ref[...] * _dot(_bf((q * xi).T), dyb)

    back = lambda i: NB - 1 - i
    col = _ret_specs(_RET_COL0, back)
    consts, cspecs = _ret_consts(tb)
    vec = pl.BlockSpec((1, RW), lambda i: (0, 0))
    return pl.pallas_call(
        body, name=name,
        out_shape=(jax.ShapeDtypeStruct((L, 4 * RW), F32), jax.ShapeDtypeStruct((1, RW), F32)),
        grid=(NB,),
        in_specs=[col(0), col(1), col(2), col(3), _ret_specs(3, back)(0),
                  pl.BlockSpec((None, RW, RW), lambda i: (back(i), 0, 0))] + cspecs + [vec],
        out_specs=(pl.BlockSpec((T, 4 * RW), lambda i: (back(i), 0)), vec),
        scratch_shapes=[pltpu.VMEM((RW, RW), F32)])(proj, proj, proj, proj, dycat, states, *consts, gng)


HALO = 32
TAP0 = HALO - (CONV_K - 1)


def _conv_specs():
    cur = lambda col=0: pl.BlockSpec((T, CV), lambda c: (c, col))
    before = lambda col=0: pl.BlockSpec((HALO, CV), lambda c: (jnp.maximum(c * (T // HALO) - 1, 0), col))
    after = pl.BlockSpec((HALO, CV), lambda c: (jnp.minimum((c + 1) * (T // HALO), L // HALO - 1), 0))
    full = lambda r, w: pl.BlockSpec((r, w), lambda c: (0, 0))
    return cur, before, after, full


_CONV_A, _CONV_B = 3, 4
_DY_CONV = 2


def _conv_tile_fwd(c, ca_ref, cah_ref, cb_ref, cbh_ref, w_ref, b_ref, lg_ref, lb_ref, u_scr):
    u_scr[0:HALO, :] = jnp.where(c > 0, cah_ref[...] * _sig(cbh_ref[...]), 0.0)
    u_scr[HALO:HALO + T, :] = ca_ref[...] * _sig(cb_ref[...])
    acc = jnp.zeros((T, CV), F32)
    for k in range(CONV_K):
        acc = acc + w_ref[k:k + 1, :] * u_scr[TAP0 + k:TAP0 + k + T, :]
    xc = acc + b_ref[...]
    xh, rstd = _gn_rows(xc)
    z = xh * lg_ref[...] + lb_ref[...]
    return xh, rstd, z, _sig(z)


def conv_fwd(proj, w, b, lg, lb, pw, og, name):
    def body(ca_ref, cah_ref, cb_ref, cbh_ref, w_ref, b_ref, lg_ref, lb_ref, pw_ref, og_ref, y_ref, u_scr):
        c = pl.program_id(0)
        _, _, z, sg = _conv_tile_fwd(c, ca_ref, cah_ref, cb_ref, cbh_ref, w_ref, b_ref, lg_ref, lb_ref, u_scr)
        zp = _dot(_bf(z * sg), pw_ref[...])
        y_ref[...] = _rms_rows(zp, og_ref[...])[0]

    cur, before, _, full = _conv_specs()
    vec = full(1, CV)
    return pl.pallas_call(
        body, name=name, out_shape=jax.ShapeDtypeStruct((L, CV), F32), grid=(NB,),
        in_specs=[cur(_CONV_A), before(_CONV_A), cur(_CONV_B), before(_CONV_B), full(CONV_K, CV), vec, vec, vec,
                  full(CV, CV), vec],
        out_specs=cur(),
        scratch_shapes=[pltpu.VMEM((HALO + T, CV), F32)])(proj, proj, proj, proj, w, b, lg, lb, pw, og)


def conv_bwd(proj, dycat, w, b, lg, lb, pw, og, name):
    def body1(ca_ref, cah_ref, cb_ref, cbh_ref, dy_ref, w_ref, b_ref, lg_ref, lb_ref, pw_ref, og_ref,
              dxc_ref, dw_ref, db_ref, dlg_ref, dlb_ref, dog_ref, dpw_ref, u_scr):
        c = pl.program_id(0)

        @pl.when(c == 0)
        def _():
            for r in (dw_ref, db_ref, dlg_ref, dlb_ref, dog_ref, dpw_ref):
                r[...] = jnp.zeros_like(r)

        xh, rstd, z, sg = _conv_tile_fwd(c, ca_ref, cah_ref, cb_ref, cbh_ref, w_ref, b_ref, lg_ref, lb_ref, u_scr)
        s = z * sg
        zp = _dot(_bf(s), pw_ref[...])
        ogv = og_ref[...]
        _, zph, r2 = _rms_rows(zp, ogv)
        dyv = dy_ref[...]
        dog_ref[...] += jnp.sum(dyv * zph, axis=0, keepdims=True)
        dzpb = _bf(_rms_rows_bwd(dyv, ogv, zph, r2))
        dpw_ref[...] += _dot(_bf(s.T), dzpb)
        dz = _dot(dzpb, pw_ref[...], nt=True) * (sg * (1.0 + z * (1.0 - sg)))
        dlg_ref[...] += jnp.sum(dz * xh, axis=0, keepdims=True)
        dlb_ref[...] += jnp.sum(dz, axis=0, keepdims=True)
        dxh = dz * lg_ref[...]
        dxc = rstd * (dxh - jnp.mean(dxh, axis=-1, keepdims=True) - xh * jnp.mean(dxh * xh, axis=-1, keepdims=True))
        db_ref[...] += jnp.sum(dxc, axis=0, keepdims=True)
        dxc_ref[...] = dxc
        for k in range(CONV_K):
            dw_ref[k:k + 1, :] += jnp.sum(dxc * u_scr[TAP0 + k:TAP0 + k + T, :], axis=0, keepdims=True)

    def body2(dx_ref, dxa_ref, ca_ref, cb_ref, w_ref, dca_ref, dcb_ref, d_scr):
        c = pl.program_id(0)
        d_scr[0:T, :] = dx_ref[...]
        d_scr[T:T + HALO, :] = jnp.where(c < NB - 1, dxa_ref[...], 0.0)
        du = jnp.zeros((T, CV), F32)
        for k in range(CONV_K):
            off = CONV_K - 1 - k
            du = du + w_ref[k:k + 1, :] * d_scr[off:off + T, :]
        sg = _sig(cb_ref[...])
        dca_ref[...] = du * sg
        dcb_ref[...] = du * ca_ref[...] * sg * (1.0 - sg)

    cur, before, after, full = _conv_specs()
    seq = jax.ShapeDtypeStruct((L, CV), F32)
    vsh = jax.ShapeDtypeStruct((1, CV), F32)
    vec = full(1, CV)
    dxc, dw, db, dlg, dlb, dog, dpw = pl.pallas_call(
        body1, name=name + "_a",
        out_shape=(seq, jax.ShapeDtypeStruct((CONV_K, CV), F32), vsh, vsh, vsh, vsh,
                   jax.ShapeDtypeStruct((CV, CV), F32)),
        grid=(NB,),
        in_specs=[cur(_CONV_A), before(_CONV_A), cur(_CONV_B), before(_CONV_B), cur(_DY_CONV), full(CONV_K, CV), vec,
                  vec, vec, full(CV, CV), vec],
        out_specs=(cur(), full(CONV_K, CV), vec, vec, vec, vec, full(CV, CV)),
        scratch_shapes=[pltpu.VMEM((HALO + T, CV), F32)])(proj, proj, proj, proj, dycat, w, b, lg, lb, pw, og)
    dca, dcb = pl.pallas_call(
        body2, name=name + "_b", out_shape=(seq, seq), grid=(NB,),
        in_specs=[cur(), after, cur(_CONV_A), cur(_CONV_B), full(CONV_K, CV)], out_specs=(cur(), cur()),
        scratch_shapes=[pltpu.VMEM((T + HALO, CV), F32)])(dxc, dxc, proj, proj, w)
    return dca, dcb, dw, db, dlg, dlb, dog, dpw


HR = 8


def _ffn_specs():
    cur = lambda off: pl.BlockSpec((None, T, UPW), lambda j, c: (j + off, c, 0))
    before = lambda off: pl.BlockSpec((None, HR, UPW), lambda j, c: (j + off, jnp.maximum(c * (T // HR) - 1, 0), 0))
    after = lambda off: pl.BlockSpec(
        (None, HR, UPW), lambda j, c: (j + off, jnp.minimum((c + 1) * (T // HR), L // HR - 1), 0))
    wspec = lambda off, r: pl.BlockSpec((None, r, UPW), lambda j, c: (j + off, 0, 0))
    return cur, before, after, wspec


def ffn_act_fwd(f, w, b, name):
    def body(fg_ref, fgh_ref, fu_ref, fuh_ref, wg_ref, wu_ref, bg_ref, bu_ref, a_ref, scr):
        c = pl.program_id(1)

        def conv(x_ref, xh_ref, w_ref, b_ref):
            scr[0:HR, :] = jnp.where(c > 0, xh_ref[...], 0.0)
            scr[HR:HR + T, :] = x_ref[...]
            acc = b_ref[...]
            for k in range(FFN_K):
                acc = acc + w_ref[k:k + 1, :] * scr[HR - (FFN_K - 1) + k:HR - (FFN_K - 1) + k + T, :]
            return acc

        gate = conv(fg_ref, fgh_ref, wg_ref, bg_ref)
        up = conv(fu_ref, fuh_ref, wu_ref, bu_ref)
        a_ref[...] = (gate * _sig(gate) * up).astype(BF16)

    cur, before, _, wspec = _ffn_specs()
    return pl.pallas_call(
        body, name=name, out_shape=jax.ShapeDtypeStruct((NPAIR, L, UPW), BF16), grid=(NPAIR, NB),
        in_specs=[cur(0), before(0), cur(NPAIR), before(NPAIR), wspec(0, FFN_K), wspec(NPAIR, FFN_K),
                  wspec(0, 1), wspec(NPAIR, 1)],
        out_specs=cur(0),
        scratch_shapes=[pltpu.VMEM((HR + T, UPW), F32)])(f, f, f, f, w, w, b, b)


def ffn_act_bwd(f, da, w, b, name):
    ext = T + HR

    def body(fg_ref, fgb_ref, fga_ref, fu_ref, fub_ref, fua_ref, da_ref, daa_ref, wg_ref, wu_ref, bg_ref, bu_ref,
             dfg_ref, dfu_ref, dwg_ref, dwu_ref, dbg_ref, dbu_ref, xg_scr, xu_scr, d_scr):
        c = pl.program_id(1)

        def load(x_scr, x_ref, xb_ref, xa_ref):
            x_scr[0:HR, :] = jnp.where(c > 0, xb_ref[...], 0.0)
            x_scr[HR:HR + T, :] = x_ref[...]
            x_scr[HR + T:HR + T + HR, :] = xa_ref[...]

        def conv(x_scr, w_ref, b_ref):
            acc = b_ref[...]
            for k in range(FFN_K):
                off = HR - (FFN_K - 1) + k
                acc = acc + w_ref[k:k + 1, :] * x_scr[off:off + ext, :]
            return acc

        load(xg_scr, fg_ref, fgb_ref, fga_ref)
        load(xu_scr, fu_ref, fub_ref, fua_ref)
        gate = conv(xg_scr, wg_ref, bg_ref)
        up = conv(xu_scr, wu_ref, bu_ref)
        d_scr[0:T, :] = da_ref[...]
        d_scr[T:ext, :] = jnp.where(c < NB - 1, daa_ref[...], 0.0)
        dav = d_scr[...]
        sg = _sig(gate)
        d_gate = dav * up * (sg * (1.0 + gate * (1.0 - sg)))
        d_up = dav * gate * sg

        def back(dv, x_scr, w_ref, df_ref, dw_ref, db_ref):
            d_scr[...] = dv
            acc = jnp.zeros((T, UPW), F32)
            for k in range(FFN_K):
                off = FFN_K - 1 - k
                acc = acc + w_ref[k:k + 1, :] * d_scr[off:off + T, :]
            df_ref[...] = acc.astype(BF16)
            dcur = d_scr[0:T, :]
            parts = [jnp.sum(dcur * x_scr[HR - (FFN_K - 1) + k:HR - (FFN_K - 1) + k + T, :], axis=0, keepdims=True)
                     for k in range(FFN_K)]
            dwp = _stack_rows(parts)
            dbp = jnp.sum(dcur, axis=0, keepdims=True)

            @pl.when(c == 0)
            def _():
                dw_ref[...] = dwp
                db_ref[...] = dbp

            @pl.when(c > 0)
            def _():
                dw_ref[...] += dwp
                db_ref[...] += dbp

        back(d_gate, xg_scr, wg_ref, dfg_ref, dwg_ref, dbg_ref)
        back(d_up, xu_scr, wu_ref, dfu_ref, dwu_ref, dbu_ref)

    cur, before, after, wspec = _ffn_specs()
    slab = jax.ShapeDtypeStruct((NPAIR, L, UPW), BF16)
    wsh = jax.ShapeDtypeStruct((NPAIR, FFN_K, UPW), F32)
    bsh = jax.ShapeDtypeStruct((NPAIR, 1, UPW), F32)
    return pl.pallas_call(
        body, name=name, out_shape=(slab, slab, wsh, wsh, bsh, bsh), grid=(NPAIR, NB),
        in_specs=[cur(0), before(0), after(0), cur(NPAIR), before(NPAIR), after(NPAIR), cur(0), after(0),
                  wspec(0, FFN_K), wspec(NPAIR, FFN_K), wspec(0, 1), wspec(NPAIR, 1)],
        out_specs=(cur(0), cur(0), wspec(0, FFN_K), wspec(0, FFN_K), wspec(0, 1), wspec(0, 1)),
        scratch_shapes=[pltpu.VMEM((HR + T + HR, UPW), F32), pltpu.VMEM((HR + T + HR, UPW), F32),
                        pltpu.VMEM((ext, UPW), F32)])(f, f, f, f, f, f, da, da, w, w, b, b)


def _mesh_pos():
    return lax.axis_index("x"), lax.axis_index("y"), lax.axis_index("c")


def _peer(pos, k):
    x, y, c = pos
    px = 1 - x if k & 4 else x
    py = 1 - y if k & 2 else y
    pc = 1 - c if k & 1 else c
    return (px, py, pc), 4 * px + 2 * py + pc


_CHIP_FLIPS = (4, 2, 6)
_HBM = pl.BlockSpec(memory_space=pl.ANY)


def all_gather(shards, name):
    nt = len(shards)

    def body(*refs):
        ins, outs = refs[:nt], refs[nt:2 * nt]
        send, recv, local = refs[2 * nt:]
        pos = _mesh_pos()
        me = 4 * pos[0] + 2 * pos[1] + pos[2]
        sib, sib_id = _peer(pos, 1)

        def copy(t, k, block_id, to, src=None):
            dst = outs[t].at[block_id]
            return pltpu.make_async_remote_copy(
                src_ref=dst if src is None else src, dst_ref=dst, send_sem=send.at[t, k], recv_sem=recv.at[t, k],
                device_id=to, device_id_type=MESH)

        locals_ = [pltpu.make_async_copy(ins[t], outs[t].at[me], local.at[t]) for t in range(nt)]
        for cp in locals_:
            cp.start()
        started = []
        for j, flip in enumerate(_CHIP_FLIPS):
            for t in range(nt):
                started.append(copy(t, 1 + j, me, _peer(pos, flip)[0], src=ins[t]))
        for t in range(nt):
            started.append(copy(t, 0, me, sib, src=ins[t]))
        for cp in started:
            cp.start()
        for j, flip in enumerate(_CHIP_FLIPS):
            _, pid = _peer(pos, flip)
            for t in range(nt):
                copy(t, 1 + j, pid, sib).wait_recv()
                fwd = copy(t, 4 + j, pid, sib)
                fwd.start()
                started.append(fwd)
        for t in range(nt):
            copy(t, 0, sib_id, sib).wait_recv()
        for j, flip in enumerate(_CHIP_FLIPS):
            _, pid = _peer(pos, flip | 1)
            for t in range(nt):
                copy(t, 4 + j, pid, sib).wait_recv()
        for cp in started:
            cp.wait_send()
        for cp in locals_:
            cp.wait()

    return pl.pallas_call(
        body, name=name,
        out_shape=tuple(jax.ShapeDtypeStruct((NDEV,) + s.shape, s.dtype) for s in shards),
        in_specs=[_HBM] * nt, out_specs=tuple([_HBM] * nt),
        scratch_shapes=[pltpu.SemaphoreType.DMA((nt, NDEV - 1)), pltpu.SemaphoreType.DMA((nt, NDEV - 1)),
                        pltpu.SemaphoreType.DMA((nt,))])(*shards)


_SPLIT = dict(has_side_effects=pltpu.SideEffectType.DATAFLOW_SIDE_EFFECTING)
_SEM = pl.BlockSpec(memory_space=pltpu.SEMAPHORE)


def _exchange_start(name, srcs, lands, copies, after):
    ns, nl = len(srcs), len(lands)
    ncopy = len(copies(None, [None] * ns, [None] * nl))

    def body(*refs):
        src_refs, land_refs = refs[:ns], refs[ns:ns + nl]
        send, recv = refs[ns + nl + len(after):ns + nl + len(after) + 2]
        token = refs[-1]
        for i, (src, dst, peer) in enumerate(copies(_mesh_pos(), src_refs, land_refs)):
            pltpu.make_async_remote_copy(src_ref=src, dst_ref=dst, send_sem=send.at[i], recv_sem=recv.at[i],
                                         device_id=peer, device_id_type=MESH).start()
        token[...] = jnp.zeros_like(token)

    hbm = lambda a: pltpu.HBM(a.shape, a.dtype)
    out = pl.pallas_call(
        body, name=name,
        out_shape=(pltpu.SemaphoreType.DMA((ncopy,)), pltpu.SemaphoreType.DMA((ncopy,)),
                   *[hbm(a) for a in srcs], *[hbm(a) for a in lands], jax.ShapeDtypeStruct((8, 128), F32)),
        in_specs=[pl.BlockSpec(memory_space=pltpu.HBM)] * (ns + nl) + [_HBM] * len(after),
        out_specs=(_SEM, _SEM, *[pl.BlockSpec(memory_space=pltpu.HBM)] * (ns + nl),
                   pl.BlockSpec(memory_space=pltpu.VMEM)),
        input_output_aliases={i: 2 + i for i in range(ns + nl)},
        compiler_params=pltpu.CompilerParams(**_SPLIT))(
            *[pltpu.with_memory_space_constraint(a, pltpu.HBM) for a in list(srcs) + list(lands)], *after)
    return out[0], out[1], list(out[2:2 + ns]), list(out[2 + ns:2 + ns + nl]), out[-1]


def _exchange_wait(name, send, recv, srcs, lands, copies, after):
    ns, nl = len(srcs), len(lands)

    def body(*refs):
        src_refs, land_refs = refs[:ns], refs[ns:ns + nl]
        send_ref, recv_ref = refs[ns + nl:ns + nl + 2]
        pos = _mesh_pos()
        for i, (src, dst, peer) in enumerate(copies(pos, src_refs, land_refs, arriving=True)):
            cp = pltpu.make_async_remote_copy(src_ref=src, dst_ref=dst, send_sem=send_ref.at[i], recv_sem=recv_ref.at[i],
                                              device_id=peer, device_id_type=MESH)
            cp.wait_send()
            cp.wait_recv()

    hbm = lambda a: pltpu.HBM(a.shape, a.dtype)
    out = pl.pallas_call(
        body, name=name, out_shape=tuple(hbm(a) for a in list(srcs) + list(lands)),
        in_specs=[pl.BlockSpec(memory_space=pltpu.HBM)] * (ns + nl) + [_SEM, _SEM] + [_HBM] * len(after),
        out_specs=tuple([pl.BlockSpec(memory_space=pltpu.HBM)] * (ns + nl)),
        input_output_aliases={i: i for i in range(ns + nl)},
        compiler_params=pltpu.CompilerParams(**_SPLIT))(*srcs, *lands, send, recv, *after)
    return list(out[:ns]), list(out[ns:])


def _gather_copies(pos, srcs, lands, arriving=False):
    if pos is None:
        return [None] * (len(srcs) * (NDEV - 1))
    me = 4 * pos[0] + 2 * pos[1] + pos[2]
    out = []
    for src, land in zip(srcs, lands):
        for k in range(1, NDEV):
            peer, pid = _peer(pos, k)
            out.append((src, land.at[pid if arriving else me], peer))
    return out


def _scatter_copies(pos, srcs, lands, arriving=False):
    if pos is None:
        return [None] * (len(srcs) * (NDEV - 1))
    out = []
    for src, land in zip(srcs, lands):
        for k in range(1, NDEV):
            peer, pid = _peer(pos, k)
            out.append((src.at[pid], land.at[k - 1], peer))
    return out


def gather_start(name, shards, lands, after=()):
    return _exchange_start(name, shards, lands, _gather_copies, list(after))


def gather_wait(name, handle, after=()):
    send, recv, srcs, lands, _ = handle
    return _exchange_wait(name, send, recv, srcs, lands, _gather_copies, list(after))


def scatter_start(name, grads, after=()):
    lands = [pltpu.with_memory_space_constraint(lax.empty((NDEV - 1,) + g.shape[1:], g.dtype), pltpu.HBM)
             for g in grads]
    return _exchange_start(name, grads, lands, _scatter_copies, list(after))


def scatter_wait(name, handle, after=()):
    send, recv, srcs, lands, _ = handle
    return _exchange_wait(name, send, recv, srcs, lands, _scatter_copies, list(after))


def all_reduce_small(part, name):
    rows = part.shape[0]

    def body(x_ref, o_ref, buf, send, recv):
        pos = _mesh_pos()
        me = 4 * pos[0] + 2 * pos[1] + pos[2]
        sends = []
        for k in range(1, NDEV):
            peer, _ = _peer(pos, k)
            sends.append(pltpu.make_async_remote_copy(
                src_ref=x_ref, dst_ref=buf.at[me], send_sem=send.at[k - 1], recv_sem=recv.at[k - 1],
                device_id=peer, device_id_type=MESH))
        for cp in sends:
            cp.start()
        buf[me] = x_ref[...]
        for k in range(1, NDEV):
            peer, pid = _peer(pos, k)
            pltpu.make_async_remote_copy(
                src_ref=x_ref, dst_ref=buf.at[pid], send_sem=send.at[k - 1], recv_sem=recv.at[k - 1],
                device_id=peer, device_id_type=MESH).wait_recv()
        for cp in sends:
            cp.wait_send()
        acc = buf[0]
        for d in range(1, NDEV):
            acc = acc + buf[d]
        o_ref[...] = acc

    return pl.pallas_call(
        body, name=name, out_shape=jax.ShapeDtypeStruct((rows, 128), F32),
        in_specs=[pl.BlockSpec(memory_space=pltpu.VMEM)], out_specs=pl.BlockSpec(memory_space=pltpu.VMEM),
        scratch_shapes=[pltpu.VMEM((NDEV, rows, 128), F32), pltpu.SemaphoreType.DMA((NDEV - 1,)),
                        pltpu.SemaphoreType.DMA((NDEV - 1,))])(part)


def _adamw_update(g, w_ref, m_ref, v_ref, g_ref, d_ref, nm_ref, nv_ref):
    g_ref[...] = g
    nm = ADAM_B1 * m_ref[...] + (1.0 - ADAM_B1) * g
    nv = ADAM_B2 * v_ref[...] + (1.0 - ADAM_B2) * (g * g)
    nm_ref[...] = nm
    nv_ref[...] = nv
    m_hat = nm / (1.0 - ADAM_B1 ** ADAM_STEP)
    v_hat = nv / (1.0 - ADAM_B2 ** ADAM_STEP)
    d_ref[...] = -ADAM_LR * (m_hat / (jnp.sqrt(v_hat) + ADAM_EPS) + ADAM_WD * w_ref[...])


def adamw_big(own, lands, w, m, v, me, tr, name):
    _, r, c = own[0].shape
    nt = r // tr

    def body(me_ref, *refs):
        ins, (w_ref, m_ref, v_ref), outs = refs[:2 * DEPTH], refs[2 * DEPTH:2 * DEPTH + 3], refs[2 * DEPTH + 3:]
        layer = pl.program_id(0)
        for l in range(DEPTH):
            @pl.when(layer == l)
            def _(l=l):
                g = ins[2 * l][...].astype(F32)
                for s in range(NDEV - 1):
                    g = g + ins[2 * l + 1][s].astype(F32)
                _adamw_update(g, w_ref, m_ref, v_ref, *outs)

    pick = lambda l: (lambda layer, i, me_ref: jnp.where(layer == l, i, 0))
    in_specs = []
    for l in range(DEPTH):
        in_specs.append(pl.BlockSpec((None, tr, c), lambda layer, i, me_ref, f=pick(l): (me_ref[0], f(layer, i, me_ref), 0)))
        in_specs.append(pl.BlockSpec((NDEV - 1, tr, c), lambda layer, i, me_ref, f=pick(l): (0, f(layer, i, me_ref), 0)))
    blk = pl.BlockSpec((tr, c), lambda layer, i, me_ref: (layer * nt + i, 0))
    sh = jax.ShapeDtypeStruct((DEPTH * r, c), F32)
    args = [a for l in range(DEPTH) for a in (own[l], lands[l])]
    return pl.pallas_call(
        body, name=name, out_shape=(sh, sh, sh, sh),
        grid_spec=pltpu.PrefetchScalarGridSpec(
            num_scalar_prefetch=1, grid=(DEPTH, nt), in_specs=in_specs + [blk, blk, blk],
            out_specs=(blk, blk, blk, blk)),
        compiler_params=_params(VMEM_BIG))(me, *args, w, m, v)


def adamw(parts, w, m, v, tr, name):
    ns, r, c = parts.shape

    def body(p_ref, w_ref, m_ref, v_ref, g_ref, d_ref, nm_ref, nv_ref):
        g = p_ref[0].astype(F32)
        for s in range(1, ns):
            g = g + p_ref[s].astype(F32)
        _adamw_update(g, w_ref, m_ref, v_ref, g_ref, d_ref, nm_ref, nv_ref)

    blk = pl.BlockSpec((tr, c), lambda i: (i, 0))
    sh = jax.ShapeDtypeStruct((r, c), F32)
    return pl.pallas_call(
        body, name=name, out_shape=(sh, sh, sh, sh), grid=(r // tr,),
        in_specs=[pl.BlockSpec((ns, tr, c), lambda i: (0, i, 0)), blk, blk, blk],
        out_specs=(blk, blk, blk, blk), compiler_params=_params(VMEM_BIG))(parts, w, m, v)


_SPLITS = (NQ * HD, NKV * HD, NKV * HD, CV, CV, RH * HD, RH * HD, RH * HD, RH * HD)
_OFFS = np.concatenate([[0], np.cumsum(_SPLITS)]).tolist()


def _heads(x, nh):
    return x.reshape(L, nh, HD).transpose(1, 0, 2)


def _unheads(x):
    return x.transpose(1, 0, 2).reshape(L, x.shape[0] * HD)


def local_step(x, target, P, hooks):
    tb = _ret_tables()
    h = jnp.concatenate([jnp.zeros((PAD, D), F32), P["meta"], x], axis=0)
    stash = []
    for l in range(DEPTH):
        s = {"h": h}
        s["w_in"], s["cv_pw"], s["w_out"] = hooks["mixer_w"](l, h)
        u = rms_fwd(h, P["norm_mix_g"][l], BF16, f"rms_mix_{l}", after=hooks["first_after"] if l == 0 else None)
        s["ut"] = u.T
        proj = s["proj"] = mm_nn(f"in_proj_{l}", u, s["w_in"], 256)
        s.update(q=_heads(proj[:, :_OFFS[1]], NQ), k=_heads(proj[:, _OFFS[1]:_OFFS[2]], NKV),
                 v=_heads(proj[:, _OFFS[2]:_OFFS[3]], NKV))
        o = attn_fwd(s["q"], s["k"], s["v"], P["q_norm_g"][l], P["k_norm_g"][l], P["attn_sinks"][l], f"attn_fwd_{l}")
        s["o"] = _unheads(o)
        y_att = rms_fwd(s["o"], P["attn_out_g"][l], BF16, f"rms_att_{l}")
        y_cv = conv_fwd(proj, P["cv_dw_w"][l], P["cv_dw_b"][l], P["cv_ln_g"][l], P["cv_ln_b"][l], s["cv_pw"],
                        P["cv_out_g"][l], f"conv_fwd_{l}")
        y_ret, s["states"] = ret_fwd(proj, P["ret_gn_g"][l], tb, f"ret_fwd_{l}")
        ycat = jnp.concatenate([y_att, y_cv.astype(BF16), y_ret.astype(BF16)], axis=1)
        s["ycat_t"] = ycat.T
        h = mm_nn(f"out_proj_{l}", ycat, s["w_out"], 256, res=h)
        s["h1"] = h
        s["ffn_up"], s["ffn_down"] = hooks["ffn_w"](l, h)
        u2 = rms_fwd(h, P["norm_ffn_g"][l], BF16, f"rms_ffn_{l}")
        s["u2t"] = u2.T
        s["f"] = ffn_up_fwd(f"ffn_up_{l}", u2, s["ffn_up"])
        a = ffn_act_fwd(s["f"], P["ffn_dw_w"][l], P["ffn_dw_b"][l], f"ffn_act_fwd_{l}")
        s["at"] = a.transpose(0, 2, 1)
        h = ffn_down_fwd(f"ffn_down_{l}", a, s["ffn_down"], h)
        stash.append(s)

    dh, dhm, loss = loss_head(h, target)
    G = {n: [None] * DEPTH for n in ("norm_mix_g", "q_norm_g", "k_norm_g", "attn_sinks", "attn_out_g", "cv_dw_w",
                                     "cv_dw_b", "cv_ln_g", "cv_ln_b", "cv_out_g", "ret_gn_g", "norm_ffn_g",
                                     "ffn_dw_w", "ffn_dw_b")}
    for l in reversed(range(DEPTH)):
        s = stash[l]
        da = ffn_down_dx(f"ffn_down_dx_{l}", dhm, s["ffn_down"])
        dw_down = ffn_down_dw(f"ffn_down_dw_{l}", s["at"], dhm)
        dfg, dfu, dwg, dwu, dbg, dbu = ffn_act_bwd(s["f"], da, P["ffn_dw_w"][l], P["ffn_dw_b"][l], f"ffn_act_bwd_{l}")
        G["ffn_dw_w"][l] = jnp.concatenate([dwg, dwu], axis=0)
        G["ffn_dw_b"][l] = jnp.concatenate([dbg, dbu], axis=0)
        du2 = ffn_up_dx(f"ffn_up_dx_{l}", dfg, dfu, s["ffn_up"])
        dw_up = ffn_up_dw(f"ffn_up_dw_{l}", s["u2t"], dfg, dfu)
        after = hooks["ffn_grads"](l, dw_down, dw_up)
        dh, dhm, G["norm_ffn_g"][l] = rms_bwd(s["h1"], P["norm_ffn_g"][l], du2, dh, f"rms_ffn_bwd_{l}", after=after)
        dycat = mm_nt(f"out_proj_dx_{l}", dhm, s["w_out"], 256)
        dw_out = mm_nn(f"out_proj_dw_{l}", s["ycat_t"], dhm, 256, out_dtype=BF16)
        do, _, G["attn_out_g"][l] = rms_bwd(s["o"], P["attn_out_g"][l], dycat, None, f"rms_att_bwd_{l}")
        (dq, dk_c, dk_p, dk_m, dv_c, dv_p, dv_m, G["q_norm_g"][l], G["k_norm_g"][l], dsk) = attn_bwd(
            s["q"], s["k"], s["v"], _heads(do, NQ), P["q_norm_g"][l], P["k_norm_g"][l], P["attn_sinks"][l],
            f"attn_bwd_{l}")
        G["attn_sinks"][l] = dsk[:, :, 0].reshape(NQ)
        shift = lambda z: jnp.concatenate([z[:, T:], jnp.zeros((NKV, T, HD), F32)], axis=1)
        dk = (dk_c + shift(dk_p)).at[:, :T].add(dk_m)
        dv = (dv_c + shift(dv_p)).at[:, :T].add(dv_m)
        (dca, dcb, G["cv_dw_w"][l], G["cv_dw_b"][l], G["cv_ln_g"][l], G["cv_ln_b"][l], G["cv_out_g"][l],
         dpw) = conv_bwd(s["proj"], dycat, P["cv_dw_w"][l], P["cv_dw_b"][l], P["cv_ln_g"][l],
                         P["cv_ln_b"][l], s["cv_pw"], P["cv_out_g"][l], f"conv_bwd_{l}")
        dret, G["ret_gn_g"][l] = ret_bwd(s["proj"], dycat, s["states"], P["ret_gn_g"][l], tb, f"ret_bwd_{l}")
        dproj = jnp.concatenate([_unheads(dq), _unheads(dk), _unheads(dv), dca, dcb, dret], axis=1).astype(BF16)
        du = mm_nt(f"in_proj_dx_{l}", dproj, s["w_in"], 256)
        dw_in = mm_nn(f"in_proj_dw_{l}", s["ut"], dproj, 256, out_dtype=BF16)
        after = hooks["mixer_grads"](l, dw_out, dpw, dw_in)
        dh, dhm, G["norm_mix_g"][l] = rms_bwd(s["h"], P["norm_mix_g"][l], du, dh, f"rms_mix_bwd_{l}", after=after)
    return loss[0, 0], dh[T:], dh[PAD:T], G


_SMALL = ("meta", "norm_mix_g", "q_norm_g", "k_norm_g", "attn_sinks", "attn_out_g", "cv_dw_w", "cv_dw_b", "cv_ln_g",
          "cv_ln_b", "cv_out_g", "ret_gn_g", "norm_ffn_g", "ffn_dw_w", "ffn_dw_b")
_BIG = ("w_in", "cv_pw", "w_out", "ffn_up", "ffn_down")
_ORDER = ("meta", "norm_mix_g", "w_in", "q_norm_g", "k_norm_g", "attn_sinks", "attn_out_g", "cv_dw_w", "cv_dw_b",
          "cv_ln_g", "cv_ln_b", "cv_pw", "cv_out_g", "ret_gn_g", "w_out", "norm_ffn_g", "ffn_up", "ffn_dw_w",
          "ffn_dw_b", "ffn_down")
_SMALL_SHARDED = {"meta": D, "cv_dw_w": CV, "ffn_dw_w": 2 * D_FF}


def _pack(arrs):
    flat = jnp.concatenate([a.reshape(-1) for a in arrs])
    n = flat.shape[0]
    rows = -(-n // 1024) * 8
    return jnp.pad(flat, (0, rows * 128 - n)).reshape(rows, 128)


def _unpack(packed, shapes):
    flat = packed.reshape(-1)
    out, off = [], 0
    for s in shapes:
        n = int(np.prod(s))
        out.append(flat[off:off + n].reshape(s))
        off += n
    return out


def kernel(x, meta, norm_mix_g, w_in, q_norm_g, k_norm_g, attn_sinks, attn_out_g, cv_dw_w, cv_dw_b, cv_ln_g, cv_ln_b, cv_pw, cv_out_g, ret_gn_g, w_out, norm_ffn_g, ffn_up, ffn_dw_w, ffn_dw_b, ffn_down, loss_target, m_meta, m_norm_mix_g, m_w_in, m_q_norm_g, m_k_norm_g, m_attn_sinks, m_attn_out_g, m_cv_dw_w, m_cv_dw_b, m_cv_ln_g, m_cv_ln_b, m_cv_pw, m_cv_out_g, m_ret_gn_g, m_w_out, m_norm_ffn_g, m_ffn_up, m_ffn_dw_w, m_ffn_dw_b, m_ffn_down, v_meta, v_norm_mix_g, v_w_in, v_q_norm_g, v_k_norm_g, v_attn_sinks, v_attn_out_g, v_cv_dw_w, v_cv_dw_b, v_cv_ln_g, v_cv_ln_b, v_cv_pw, v_cv_out_g, v_ret_gn_g, v_w_out, v_norm_ffn_g, v_ffn_up, v_ffn_dw_w, v_ffn_dw_b, v_ffn_down):
    W = dict(meta=meta, norm_mix_g=norm_mix_g, w_in=w_in, q_norm_g=q_norm_g, k_norm_g=k_norm_g,
             attn_sinks=attn_sinks, attn_out_g=attn_out_g, cv_dw_w=cv_dw_w, cv_dw_b=cv_dw_b, cv_ln_g=cv_ln_g,
             cv_ln_b=cv_ln_b, cv_pw=cv_pw, cv_out_g=cv_out_g, ret_gn_g=ret_gn_g, w_out=w_out,
             norm_ffn_g=norm_ffn_g, ffn_up=ffn_up, ffn_dw_w=ffn_dw_w, ffn_dw_b=ffn_dw_b, ffn_down=ffn_down)
    M = dict(meta=m_meta, norm_mix_g=m_norm_mix_g, w_in=m_w_in, q_norm_g=m_q_norm_g, k_norm_g=m_k_norm_g,
             attn_sinks=m_attn_sinks, attn_out_g=m_attn_out_g, cv_dw_w=m_cv_dw_w, cv_dw_b=m_cv_dw_b,
             cv_ln_g=m_cv_ln_g, cv_ln_b=m_cv_ln_b, cv_pw=m_cv_pw, cv_out_g=m_cv_out_g, ret_gn_g=m_ret_gn_g,
             w_out=m_w_out, norm_ffn_g=m_norm_ffn_g, ffn_up=m_ffn_up, ffn_dw_w=m_ffn_dw_w, ffn_dw_b=m_ffn_dw_b,
             ffn_down=m_ffn_down)
    V = dict(meta=v_meta, norm_mix_g=v_norm_mix_g, w_in=v_w_in, q_norm_g=v_q_norm_g, k_norm_g=v_k_norm_g,
             attn_sinks=v_attn_sinks, attn_out_g=v_attn_out_g, cv_dw_w=v_cv_dw_w, cv_dw_b=v_cv_dw_b,
             cv_ln_g=v_cv_ln_g, cv_ln_b=v_cv_ln_b, cv_pw=v_cv_pw, cv_out_g=v_cv_out_g, ret_gn_g=v_ret_gn_g,
             w_out=v_w_out, norm_ffn_g=v_norm_ffn_g, ffn_up=v_ffn_up, ffn_dw_w=v_ffn_dw_w, ffn_dw_b=v_ffn_dw_b,
             ffn_down=v_ffn_down)
    me = 4 * lax.axis_index("x") + 2 * lax.axis_index("y") + lax.axis_index("c")

    sh = {n: [W[n][l].astype(BF16) for l in range(DEPTH)] for n in _BIG}
    mix = lambda l: [sh["w_in"][l], sh["cv_pw"][l], sh["w_out"][l]]
    ffn = lambda l: [sh["ffn_up"][l], sh["ffn_down"][l]]
    first = all_gather(mix(0) + [meta, cv_dw_w, ffn_dw_w], "gather_first")
    g_meta, g_cdw, g_fdw = first[3:6]

    def landing(shards):
        return [lax.dynamic_update_slice(lax.empty((NDEV,) + s.shape, s.dtype), s[None], (me,) + (0,) * s.ndim)
                for s in shards]

    gathers = {("ffn", 0): gather_start("gather_ffn0_start", ffn(0), landing(ffn(0)), after=[first[0]])}
    gathers["mix", 1] = gather_start("gather_mix1_start", mix(1), landing(mix(1)), after=[gathers["ffn", 0][4]])
    gathers["ffn", 1] = gather_start("gather_ffn1_start", ffn(1), landing(ffn(1)), after=[gathers["mix", 1][4]])

    def mixer_w(l, h):
        g_in, g_pw, g_out = first[0:3] if l == 0 else gather_wait(f"gather_mix{l}_wait", gathers["mix", l], after=[h])[1]
        return g_in.transpose(1, 0, 2).reshape(D, IN_W), g_pw.reshape(CV, CV), g_out.reshape(D, D)

    def ffn_w(l, h1):
        return gather_wait(f"gather_ffn{l}_wait", gathers["ffn", l], after=[h1])[1]

    scatters = {}

    def ffn_grads(l, dw_down, dw_up):
        scatters["ffn", l] = scatter_start(f"scatter_ffn{l}_start", [dw_up, dw_down.reshape(NDEV, DNR, D)])
        return scatters["ffn", l][4]

    def mixer_grads(l, dw_out, dpw, dw_in):
        grads = [dw_in.reshape(D, NDEV, IN_W // NDEV).transpose(1, 0, 2), dpw.astype(BF16).reshape(NDEV, CV // NDEV, CV),
                 dw_out.reshape(NDEV, D // NDEV, D)]
        scatters["mix", l] = scatter_start(f"scatter_mix{l}_start", grads)
        return scatters["mix", l][4]

    P = dict(
        meta=g_meta.transpose(1, 0, 2).reshape(N_META, D),
        cv_dw_w=g_cdw.transpose(1, 2, 0, 3).reshape(DEPTH, CONV_K, CV),
        ffn_dw_w=g_fdw.transpose(1, 0, 2, 3),
        ffn_dw_b=ffn_dw_b.reshape(DEPTH, NDEV, 1, UPW),
        attn_sinks=attn_sinks,
    )
    for n in ("norm_mix_g", "q_norm_g", "k_norm_g", "attn_out_g", "cv_dw_b", "cv_ln_g", "cv_ln_b", "cv_out_g",
              "ret_gn_g", "norm_ffn_g"):
        P[n] = W[n].reshape(DEPTH, 1, -1)

    hooks = dict(mixer_w=mixer_w, ffn_w=ffn_w, ffn_grads=ffn_grads, mixer_grads=mixer_grads,
                 first_after=gathers["ffn", 1][4])
    loss_part, dx, dmeta, G = local_step(x[0], loss_target[0], P, hooks)
    loss = lax.psum(loss_part, ("x", "y", "c"))

    small_full = {
        "meta": dmeta,
        "cv_dw_w": jnp.stack(G["cv_dw_w"]),
        "ffn_dw_w": jnp.stack([g.transpose(1, 0, 2).reshape(FFN_K, 2 * D_FF) for g in G["ffn_dw_w"]]),
        "ffn_dw_b": jnp.stack([g.reshape(2 * D_FF) for g in G["ffn_dw_b"]]),
        "attn_sinks": jnp.stack(G["attn_sinks"]),
    }
    for n in _SMALL:
        if n not in small_full:
            small_full[n] = jnp.stack([g.reshape(-1) for g in G[n]])
    shapes = [small_full[n].shape for n in _SMALL]
    summed = _unpack(all_reduce_small(_pack([small_full[n] for n in _SMALL]), "reduce_small"), shapes)
    small_g = {}
    for n, g in zip(_SMALL, summed):
        if n in _SMALL_SHARDED:
            width = _SMALL_SHARDED[n] // NDEV
            g = lax.dynamic_slice_in_dim(g, me * width, width, axis=g.ndim - 1)
        small_g[n] = g

    out = {}
    tiles = {"w_in": 256, "cv_pw": 32, "w_out": 128, "ffn_up": 256, "ffn_down": 176}
    me1 = me.astype(jnp.int32).reshape(1)
    own, lands = {n: [None] * DEPTH for n in _BIG}, {n: [None] * DEPTH for n in _BIG}

    def arrived(kind, names, after):
        for l in range(DEPTH):
            srcs, got = scatter_wait(f"scatter_{kind}{l}_wait", scatters[kind, l], after=after)
            for n, s_, g_ in zip(names, srcs, got):
                own[n][l], lands[n][l] = s_, g_

    def update(names):
        for n in names:
            shard = W[n].shape
            rows, cols = shard[0] * shard[1], shard[2]
            res = adamw_big(own[n], lands[n], W[n].reshape(rows, cols), M[n].reshape(rows, cols),
                            V[n].reshape(rows, cols), me1, tiles[n], f"adamw_{n}")
            out[n] = [r.reshape(shard) for r in res]

    arrived("ffn", ("ffn_up", "ffn_down"), [dx])
    update(("ffn_up", "ffn_down"))
    sshapes = [W[n].shape for n in _SMALL]
    packed_g = _pack([small_g[n] for n in _SMALL])
    res = adamw(packed_g[None], _pack([W[n] for n in _SMALL]), _pack([M[n] for n in _SMALL]),
                _pack([V[n] for n in _SMALL]), packed_g.shape[0], "adamw_small")
    for i, r in enumerate(res):
        for n, a in zip(_SMALL, _unpack(r, sshapes)):
            out.setdefault(n, [None] * 4)[i] = a
    arrived("mix", ("w_in", "cv_pw", "w_out"), [out["ffn_down"][0], res[0]])
    update(("w_in", "cv_pw", "w_out"))

    return (loss, dx[None], *[out[n][0] for n in _ORDER], *[out[n][1] for n in _ORDER],
            *[out[n][2] for n in _ORDER], *[out[n][3] for n in _ORDER])
```

```python
import functools
import math

import numpy as np
import jax
import jax.numpy as jnp
from jax import lax
from jax.experimental import pallas as pl
from jax.experimental.pallas import tpu as pltpu

F32 = jnp.float32
BF16 = jnp.bfloat16

D = 1024
SEQ = 2048
DEPTH = 2
T = 128
L = SEQ + T
NB = L // T
N_META = 16
PAD = T - N_META
HD = 64
NQ = 8
NKV = 2
GQA = NQ // NKV
CV = 256
CONV_K = 31
RH = 4
D_FF = 2816
FFN_K = 3
IN_W = 2304
RMS_EPS = 1e-6
LN_EPS = 1e-5
NEG = -1e30
NDEV = 8
UPW = 2 * D_FF // NDEV
DNR = D_FF // NDEV
NPAIR = NDEV // 2

ADAM_LR, ADAM_B1, ADAM_B2, ADAM_EPS, ADAM_WD, ADAM_STEP = 0.001, 0.9, 0.999, 1e-08, 0.01, 10

VMEM_BIG = 56 * 1024 * 1024

MESH = pl.DeviceIdType.MESH


def _params(vmem=None):
    return pltpu.CompilerParams(vmem_limit_bytes=vmem) if vmem else None


def _dot(a, b, nt=False):
    return lax.dot_general(a, b, (((1,), (1 if nt else 0,)), ((), ())), preferred_element_type=F32)


def _sig(x):
    return 1.0 / (1.0 + jnp.exp(-x))


def _bf(x):
    return x.astype(BF16)


def _stack_rows(rows):
    idx = lax.broadcasted_iota(jnp.int32, (len(rows), rows[0].shape[1]), 0)
    out = jnp.zeros((len(rows), rows[0].shape[1]), F32)
    for r, v in enumerate(rows):
        out = jnp.where(idx == r, v, out)
    return out


STRIP = 16


def _strip(s):
    return pl.ds(pl.multiple_of(s * STRIP, STRIP), STRIP)


def rms_fwd(x, g, out_dtype, name, after=None):
    n, w = x.shape

    def body(x_ref, g_ref, *rest):
        o_ref = rest[-1]
        gv = g_ref[...]

        def strip(s, carry):
            rows = _strip(s)
            xv = x_ref[rows, :]
            r = lax.rsqrt(jnp.mean(xv * xv, axis=-1, keepdims=True) + RMS_EPS)
            o_ref[rows, :] = (xv * r * gv).astype(o_ref.dtype)
            return carry

        lax.fori_loop(0, T // STRIP, strip, 0)

    deps = [] if after is None else [after]
    return pl.pallas_call(
        body, name=name, out_shape=jax.ShapeDtypeStruct((n, w), out_dtype), grid=(n // T,),
        in_specs=[pl.BlockSpec((T, w), lambda i: (i, 0)), pl.BlockSpec((1, w), lambda i: (0, 0))] + [_HBM] * len(deps),
        out_specs=pl.BlockSpec((T, w), lambda i: (i, 0)))(x, g, *deps)


def rms_bwd(x, g, dy, dres, name, after=None, dy_col=0):
    n, w = x.shape
    has_res = dres is not None
    deps = [] if after is None else [after]

    def body(x_ref, g_ref, dy_ref, *rest):
        rest = rest[:len(rest) - 3 - len(deps)] + rest[len(rest) - 3:]
        if has_res:
            dres_ref, dx_ref, dxm_ref, dg_ref = rest
        else:
            dx_ref, dxm_ref, dg_ref = rest
        i = pl.program_id(0)
        gv = g_ref[...]

        def strip(s, acc):
            rows = _strip(s)
            xv = x_ref[rows, :]
            r = lax.rsqrt(jnp.mean(xv * xv, axis=-1, keepdims=True) + RMS_EPS)
            xh = xv * r
            dyv = dy_ref[rows, :]
            dxh = dyv * gv
            dx = r * (dxh - xh * jnp.mean(dxh * xh, axis=-1, keepdims=True))
            if has_res:
                dx = dx + dres_ref[rows, :]
            dx_ref[rows, :] = dx
            row_id = i * T + s * STRIP + lax.broadcasted_iota(jnp.int32, (STRIP, 1), 0)
            dxm_ref[rows, :] = jnp.where(row_id >= PAD, dx, 0.0).astype(BF16)
            return acc + dyv * xh

        acc = lax.fori_loop(0, T // STRIP, strip, jnp.zeros((STRIP, w), F32))
        part = jnp.sum(acc, axis=0, keepdims=True)

        @pl.when(i == 0)
        def _():
            dg_ref[...] = part

        @pl.when(i > 0)
        def _():
            dg_ref[...] += part

    row = pl.BlockSpec((T, w), lambda i: (i, 0))
    vec = pl.BlockSpec((1, w), lambda i: (0, 0))
    ins = [x, g, dy] + ([dres] if has_res else []) + deps
    return pl.pallas_call(
        body, name=name,
        out_shape=(jax.ShapeDtypeStruct((n, w), F32), jax.ShapeDtypeStruct((n, w), BF16),
                   jax.ShapeDtypeStruct((1, w), F32)),
        grid=(n // T,),
        in_specs=[row, vec, pl.BlockSpec((T, w), lambda i: (i, dy_col))] + ([row] if has_res else [])
        + [_HBM] * len(deps),
        out_specs=(row, row, vec))(*ins)


def loss_head(h, target):
    def body(h_ref, t_ref, dh_ref, dhm_ref, loss_ref):
        n = pl.program_id(0)

        def strip(s, acc):
            rows = _strip(s)
            e = jnp.where(n > 0, h_ref[rows, :] - t_ref[rows, :], 0.0)
            dh = e * (1.0 / D)
            dh_ref[rows, :] = dh
            dhm_ref[rows, :] = dh.astype(BF16)
            return acc + e * e

        acc = lax.fori_loop(0, T // STRIP, strip, jnp.zeros((STRIP, D), F32))
        part = jnp.sum(jnp.sum(acc, axis=1, keepdims=True), axis=0, keepdims=True) * (0.5 / D)

        @pl.when(n == 0)
        def _():
            loss_ref[...] = jnp.zeros_like(loss_ref)

        @pl.when(n > 0)
        def _():
            loss_ref[...] += jnp.broadcast_to(part, loss_ref.shape)

    row = pl.BlockSpec((T, D), lambda n: (n, 0))
    return pl.pallas_call(
        body, name="loss_head",
        out_shape=(jax.ShapeDtypeStruct((L, D), F32), jax.ShapeDtypeStruct((L, D), BF16),
                   jax.ShapeDtypeStruct((8, 128), F32)),
        grid=(NB,),
        in_specs=[row, pl.BlockSpec((T, D), lambda n: (jnp.maximum(n - 1, 0), 0))],
        out_specs=(row, row, pl.BlockSpec((8, 128), lambda n: (0, 0))))(h, target)


def _mm(name, a, b, *, grid, a_spec, b_spec, o_spec, out_shape, nt=False, red=False, res=None, res_spec=None):
    def body(a_ref, b_ref, *rest):
        o_ref = rest[-1]
        av = a_ref[...]
        bv = b_ref[...]
        if bv.ndim == 3:
            bv = bv.reshape(bv.shape[0] * bv.shape[1], bv.shape[2])
        acc = _dot(av, bv, nt)
        if red:
            k = pl.program_id(0)

            @pl.when(k == 0)
            def _():
                o_ref[...] = acc

            @pl.when(k > 0)
            def _():
                o_ref[...] += acc
        else:
            if res is not None:
                rows = lax.broadcasted_iota(jnp.int32, (acc.shape[0], 1), 0)
                acc = rest[0][...] + jnp.where(rows >= PAD, acc, 0.0)
            o_ref[...] = acc.astype(o_ref.dtype)

    ins = [a, b] + ([res] if res is not None else [])
    specs = [a_spec, b_spec] + ([res_spec] if res is not None else [])
    return pl.pallas_call(body, name=name, out_shape=out_shape, grid=grid, in_specs=specs, out_specs=o_spec,
                          compiler_params=_params(VMEM_BIG))(*ins)


def mm_nn(name, a, b, tn, out_dtype=F32, res=None):
    m, k = a.shape
    n = b.shape[1]
    return _mm(name, a, b, grid=(n // tn,),
               a_spec=pl.BlockSpec((m, k), lambda j: (0, 0)), b_spec=pl.BlockSpec((k, tn), lambda j: (0, j)),
               o_spec=pl.BlockSpec((m, tn), lambda j: (0, j)), out_shape=jax.ShapeDtypeStruct((m, n), out_dtype),
               res=res, res_spec=pl.BlockSpec((m, tn), lambda j: (0, j)))


def mm_nt(name, a, b, tn):
    m, k = a.shape
    n = b.shape[0]
    return _mm(name, a, b, grid=(n // tn,), nt=True,
               a_spec=pl.BlockSpec((m, k), lambda j: (0, 0)), b_spec=pl.BlockSpec((tn, k), lambda j: (j, 0)),
               o_spec=pl.BlockSpec((m, tn), lambda j: (0, j)), out_shape=jax.ShapeDtypeStruct((m, n), F32))


def ffn_up_fwd(name, u2, wup):
    return _mm(name, u2, wup, grid=(NDEV,),
               a_spec=pl.BlockSpec((L, D), lambda j: (0, 0)),
               b_spec=pl.BlockSpec((None, D, UPW), lambda j: (j, 0, 0)),
               o_spec=pl.BlockSpec((None, L, UPW), lambda j: (j, 0, 0)),
               out_shape=jax.ShapeDtypeStruct((NDEV, L, UPW), F32))


def _slab_specs():
    gate = pl.BlockSpec((None, L, UPW), lambda j: (jnp.minimum(j, NPAIR - 1), 0, 0))
    up = pl.BlockSpec((None, L, UPW), lambda j: (jnp.maximum(j - NPAIR, 0), 0, 0))
    return gate, up


def ffn_up_dx(name, dfg, dfu, wup):
    def body(g_ref, u_ref, b_ref, o_ref):
        j = pl.program_id(0)

        @pl.when(j == 0)
        def _():
            o_ref[...] = _dot(g_ref[...], b_ref[...], nt=True)

        @pl.when(jnp.logical_and(j > 0, j < NPAIR))
        def _():
            o_ref[...] += _dot(g_ref[...], b_ref[...], nt=True)

        @pl.when(j >= NPAIR)
        def _():
            o_ref[...] += _dot(u_ref[...], b_ref[...], nt=True)

    gate, up = _slab_specs()
    return pl.pallas_call(
        body, name=name, out_shape=jax.ShapeDtypeStruct((L, D), F32), grid=(NDEV,),
        in_specs=[gate, up, pl.BlockSpec((None, D, UPW), lambda j: (j, 0, 0))],
        out_specs=pl.BlockSpec((L, D), lambda j: (0, 0)), compiler_params=_params(VMEM_BIG))(dfg, dfu, wup)


def ffn_up_dw(name, u2t, dfg, dfu):
    def body(a_ref, g_ref, u_ref, o_ref):
        j = pl.program_id(0)

        @pl.when(j < NPAIR)
        def _():
            o_ref[...] = _dot(a_ref[...], g_ref[...]).astype(o_ref.dtype)

        @pl.when(j >= NPAIR)
        def _():
            o_ref[...] = _dot(a_ref[...], u_ref[...]).astype(o_ref.dtype)

    gate, up = _slab_specs()
    return pl.pallas_call(
        body, name=name, out_shape=jax.ShapeDtypeStruct((NDEV, D, UPW), BF16), grid=(NDEV,),
        in_specs=[pl.BlockSpec((D, L), lambda j: (0, 0)), gate, up],
        out_specs=pl.BlockSpec((None, D, UPW), lambda j: (j, 0, 0)), compiler_params=_params(VMEM_BIG))(u2t, dfg, dfu)


def ffn_down_fwd(name, a, wdn, res, tn=256):
    def body(a_ref, b_ref, r_ref, o_ref):
        acc = jnp.zeros((L, tn), F32)
        for g in range(NPAIR):
            bv = b_ref[2 * g:2 * g + 2]
            acc = acc + _dot(a_ref[g], bv.reshape(2 * DNR, tn))
        rows = lax.broadcasted_iota(jnp.int32, (L, 1), 0)
        o_ref[...] = r_ref[...] + jnp.where(rows >= PAD, acc, 0.0)

    return pl.pallas_call(
        body, name=name, out_shape=jax.ShapeDtypeStruct((L, D), F32), grid=(D // tn,),
        in_specs=[pl.BlockSpec((NPAIR, L, UPW), lambda j: (0, 0, 0)),
                  pl.BlockSpec((NDEV, DNR, tn), lambda j: (0, 0, j)),
                  pl.BlockSpec((L, tn), lambda j: (0, j))],
        out_specs=pl.BlockSpec((L, tn), lambda j: (0, j)),
        compiler_params=_params(VMEM_BIG))(a, wdn, res)


def ffn_down_dx(name, dh, wdn):
    return _mm(name, dh, wdn, grid=(NPAIR,), nt=True,
               a_spec=pl.BlockSpec((L, D), lambda g: (0, 0)),
               b_spec=pl.BlockSpec((2, DNR, D), lambda g: (g, 0, 0)),
               o_spec=pl.BlockSpec((None, L, UPW), lambda g: (g, 0, 0)),
               out_shape=jax.ShapeDtypeStruct((NPAIR, L, UPW), F32))


def ffn_down_dw(name, at, dh):
    return _mm(name, at, dh, grid=(NPAIR,),
               a_spec=pl.BlockSpec((None, UPW, L), lambda g: (g, 0, 0)),
               b_spec=pl.BlockSpec((L, D), lambda g: (0, 0)),
               o_spec=pl.BlockSpec((UPW, D), lambda g: (g, 0)),
               out_shape=jax.ShapeDtypeStruct((D_FF, D), BF16))


_SLOPES = [2.0 ** (-8.0 * (h + 1) / NQ) for h in range(NQ)]
_SCALE = HD ** -0.5


QR = GQA * T
KC = 3 * T


def _attn_softmax(g, n, sink_ref, s_scr, on_strip):
    col = lax.broadcasted_iota(jnp.int32, (STRIP, KC), 1)
    j = col & (T - 1)
    blk = col // T
    accs = []
    for hh in range(GQA):
        slope = jnp.where(g == 0, _SLOPES[hh], _SLOPES[GQA + hh])
        sink = sink_ref[g * GQA + hh]

        def strip(t, acc, hh=hh, slope=slope, sink=sink):
            rows = pl.ds(pl.multiple_of(hh * T + t * STRIP, STRIP), STRIP)
            i = t * STRIP + lax.broadcasted_iota(jnp.int32, (STRIP, KC), 0)
            d_meta = n * T + i - j
            ok_meta = (j >= PAD) & (d_meta >= 0)
            ok_prev = j > i + jnp.where(n >= 2, 0, T)
            ok_cur = j <= i - jnp.where(n >= 1, 0, T)
            ok = ((blk == 0) & ok_meta) | ((blk == 1) & ok_prev) | ((blk == 2) & ok_cur)
            dist = jnp.where(blk == 0, jnp.minimum(d_meta, T), jnp.where(blk == 1, T + i - j, i - j)).astype(F32)
            s = jnp.where(ok, s_scr[rows, :] * _SCALE - slope * dist, NEG)
            m = jnp.maximum(jnp.max(s, axis=-1, keepdims=True), sink)
            e = jnp.exp(s - m)
            e_sink = jnp.exp(sink - m)
            inv = 1.0 / (e.sum(axis=-1, keepdims=True) + e_sink)
            return on_strip(rows, e * inv, e_sink * inv, acc)

        accs.append(lax.fori_loop(0, T // STRIP, strip, jnp.zeros((STRIP, 1), F32)))
    return accs


def _rms_rows(x, g):
    r = lax.rsqrt(jnp.mean(x * x, axis=-1, keepdims=True) + RMS_EPS)
    xh = x * r
    return xh * g, xh, r


def _rms_rows_bwd(dy, g, xh, r):
    dxh = dy * g
    return r * (dxh - xh * jnp.mean(dxh * xh, axis=-1, keepdims=True))


def _attn_specs():
    qspec = pl.BlockSpec((GQA, T, HD), lambda g, n: (g, n, 0))
    cur = pl.BlockSpec((None, T, HD), lambda g, n: (g, n, 0))
    prev = pl.BlockSpec((None, T, HD), lambda g, n: (g, jnp.maximum(n - 1, 0), 0))
    meta = pl.BlockSpec((None, T, HD), lambda g, n: (g, 0, 0))
    vec = pl.BlockSpec((1, HD), lambda g, n: (0, 0))
    return qspec, cur, prev, meta, vec


def _rows3(m_ref, p_ref, c_ref):
    return jnp.concatenate([m_ref[...], p_ref[...], c_ref[...]], axis=0)


def attn_fwd(q, k, v, qg, kg, sinks, name):
    def body(sink_ref, q_ref, kc_ref, kp_ref, km_ref, vc_ref, vp_ref, vm_ref, qg_ref, kg_ref, o_ref, s_scr, p_scr):
        g = pl.program_id(0)
        n = pl.program_id(1)
        keys = _bf(_rms_rows(_rows3(km_ref, kp_ref, kc_ref), kg_ref[...])[0])
        vals = _bf(_rows3(vm_ref, vp_ref, vc_ref))
        qn = _bf(_rms_rows(q_ref[...].reshape(QR, HD), qg_ref[...])[0])
        s_scr[...] = _dot(qn, keys, nt=True)

        def keep(rows, p, p_sink, acc):
            p_scr[rows, :] = p.astype(BF16)
            return acc

        _attn_softmax(g, n, sink_ref, s_scr, keep)
        o_ref[...] = _dot(p_scr[...], vals).reshape(GQA, T, HD)

    qspec, cur, prev, meta, vec = _attn_specs()
    return pl.pallas_call(
        body, name=name, out_shape=jax.ShapeDtypeStruct((NQ, L, HD), F32), grid=(NKV, NB),
        in_specs=[pl.BlockSpec(memory_space=pltpu.SMEM), qspec, cur, prev, meta, cur, prev, meta, vec, vec],
        out_specs=qspec,
        scratch_shapes=[pltpu.VMEM((QR, KC), F32), pltpu.VMEM((QR, KC), BF16)])(sinks, q, k, k, k, v, v, v, qg, kg)


def attn_bwd(q, k, v, do, qg, kg, sinks, name):
    def body(sink_ref, q_ref, kc_ref, kp_ref, km_ref, vc_ref, vp_ref, vm_ref, do_ref, qg_ref, kg_ref,
             dq_ref, dkc_ref, dkp_ref, dkm_ref, dvc_ref, dvp_ref, dvm_ref, dqg_ref, dkg_ref, dsk_ref, s_scr, p_scr):
        g = pl.program_id(0)
        n = pl.program_id(1)
        kgv = kg_ref[...]
        qgv = qg_ref[...]
        kn_f, kh, kr = _rms_rows(_rows3(km_ref, kp_ref, kc_ref), kgv)
        keys = _bf(kn_f)
        vals = _bf(_rows3(vm_ref, vp_ref, vc_ref))
        qn_f, qh, qr = _rms_rows(q_ref[...].reshape(QR, HD), qgv)
        qn = _bf(qn_f)
        dob = _bf(do_ref[...].reshape(QR, HD))
        s_scr[...] = _dot(qn, keys, nt=True)
        p_scr[...] = _dot(dob, vals, nt=True)

        def grads(rows, p, p_sink, acc):
            dp = p_scr[rows, :]
            delta = (p * dp).sum(axis=-1, keepdims=True)
            s_scr[rows, :] = p * (dp - delta)
            p_scr[rows, :] = p
            return acc - p_sink * delta

        dsk = _stack_rows([jnp.broadcast_to(jnp.sum(a, axis=0, keepdims=True), (1, 128))
                           for a in _attn_softmax(g, n, sink_ref, s_scr, grads)])
        ds = s_scr[...]
        p = p_scr[...]
        dqn = _dot(_bf(ds), keys) * _SCALE
        dq_ref[...] = _rms_rows_bwd(dqn, qgv, qh, qr).reshape(GQA, T, HD)
        dqg_acc = jnp.sum(dqn * qh, axis=0, keepdims=True)
        dkn = _dot(_bf(ds.T), qn) * _SCALE
        dkg_acc = jnp.sum(dkn * kh, axis=0, keepdims=True)
        dk_all = _rms_rows_bwd(dkn, kgv, kh, kr)
        dv_all = _dot(_bf(p.T), dob)
        dk_raw = [dk_all[b * T:(b + 1) * T] for b in range(3)]
        dvv = [dv_all[b * T:(b + 1) * T] for b in range(3)]
        dkp_ref[...] = dk_raw[1]
        dkc_ref[...] = dk_raw[2]
        dvp_ref[...] = dvv[1]
        dvc_ref[...] = dvv[2]
        first = jnp.logical_and(g == 0, n == 0)

        @pl.when(n == 0)
        def _():
            dkm_ref[...] = dk_raw[0]
            dvm_ref[...] = dvv[0]
            dsk_ref[...] = dsk

        @pl.when(n > 0)
        def _():
            dkm_ref[...] += dk_raw[0]
            dvm_ref[...] += dvv[0]
            dsk_ref[...] += dsk

        @pl.when(first)
        def _():
            dqg_ref[...] = dqg_acc
            dkg_ref[...] = dkg_acc

        @pl.when(jnp.logical_not(first))
        def _():
            dqg_ref[...] += dqg_acc
            dkg_ref[...] += dkg_acc

    qspec, cur, prev, meta, vec = _attn_specs()
    kv_shape = jax.ShapeDtypeStruct((NKV, L, HD), F32)
    meta_shape = jax.ShapeDtypeStruct((NKV, T, HD), F32)
    vec_shape = jax.ShapeDtypeStruct((1, HD), F32)
    return pl.pallas_call(
        body, name=name,
        out_shape=(jax.ShapeDtypeStruct((NQ, L, HD), F32), kv_shape, kv_shape, meta_shape, kv_shape, kv_shape,
                   meta_shape, vec_shape, vec_shape, jax.ShapeDtypeStruct((NKV, GQA, 128), F32)),
        grid=(NKV, NB),
        in_specs=[pl.BlockSpec(memory_space=pltpu.SMEM), qspec, cur, prev, meta, cur, prev, meta, qspec, vec, vec],
        out_specs=(qspec, cur, cur, meta, cur, cur, meta, vec, vec,
                   pl.BlockSpec((None, GQA, 128), lambda g, n: (g, 0, 0))),
        scratch_shapes=[pltpu.VMEM((QR, KC), F32), pltpu.VMEM((QR, KC), F32)])(
            sinks, q, k, k, k, v, v, v, do, qg, kg)


RW = RH * HD


def _ret_tables():
    h = np.arange(RH, dtype=np.float64)
    lg = np.log1p(-np.exp2(-5.0 - h))
    idx = np.arange(T, dtype=np.float64)
    diff = idx[:, None] - idx[None, :]
    decay = np.where(diff[None] >= 0, np.exp(np.maximum(diff, 0.0)[None] * lg[:, None, None]), 0.0)
    zeta = np.exp((T - 1 - idx)[None, :] * lg[:, None])
    xi = np.exp((idx + 1.0)[None, :] * lg[:, None])
    cd = np.exp(T * lg)
    lanes = lambda a: np.repeat(a.T, HD, axis=1)
    head_of = np.arange(RW) // HD
    same = (head_of[:, None] == head_of[None, :]).astype(np.float64)
    f = lambda a: jnp.asarray(a, F32)
    return dict(decay=f(decay), zeta=f(lanes(zeta)), xi=f(lanes(xi)), cd=f(np.repeat(cd, HD)[None, :]),
                head=f((head_of[None, :] == np.arange(RH)[:, None]).astype(np.float64)[:, None, :]),
                same=f(same), avg=jnp.asarray(same / HD, BF16))


def _seg_mean(x, avg):
    hi = _bf(x)
    lo = _bf(x - hi.astype(F32))
    return _dot(hi, avg) + _dot(lo, avg)


def _ret_specs(col0, order):
    return lambda col: pl.BlockSpec((T, RW), lambda i: (order(i), col0 + col))


def _ret_chunk(q, kf, v, s, tb):
    dec, xi, head = tb
    vb = _bf(v)
    kb = _bf(kf)
    y = _dot(_bf(q * xi), _bf(s))
    a = []
    for h in range(RH):
        a.append(_dot(_bf(q * head[h]), kb, nt=True) * dec[h])
        y = y + head[h] * _dot(_bf(a[h]), vb)
    return a, y


def _gn_rows(y):
    mu = jnp.mean(y, axis=-1, keepdims=True)
    yc = y - mu
    rstd = lax.rsqrt(jnp.mean(yc * yc, axis=-1, keepdims=True) + LN_EPS)
    return yc * rstd, rstd


_RET_COL0 = _RET_Q = 5


def _ret_consts(tb):
    names = ("decay", "zeta", "xi", "cd", "head", "same", "avg")
    full = lambda a: pl.BlockSpec(a.shape, lambda i: (0,) * a.ndim)
    return [tb[n] for n in names], [full(tb[n]) for n in names]


def _ret_inputs(n, q_ref, k_ref, v_ref):
    rows = n * T + lax.broadcasted_iota(jnp.int32, (T, 1), 0)
    valid = rows >= PAD
    return q_ref[...], jnp.where(valid, k_ref[...] * (HD ** -0.5), 0.0), v_ref[...], valid


def ret_fwd(proj, gng, tb, name):
    def body(q_ref, k_ref, v_ref, g_ref, dec_ref, zeta_ref, xi_ref, cd_ref, head_ref, same_ref, avg_ref, gng_ref,
             y_ref, st_ref, s_scr):
        n = pl.program_id(0)

        @pl.when(n == 0)
        def _():
            s_scr[...] = jnp.zeros_like(s_scr)

        s = s_scr[...]
        st_ref[...] = s
        q, kf, v, _ = _ret_inputs(n, q_ref, k_ref, v_ref)
        _, y = _ret_chunk(q, kf, v, s, (dec_ref, xi_ref[...], head_ref))
        s_scr[...] = cd_ref[...] * s + same_ref[...] * _dot(_bf((kf * zeta_ref[...]).T), _bf(v))
        avg = avg_ref[...]
        yc = y - _seg_mean(y, avg)
        yh = yc * lax.rsqrt(_seg_mean(yc * yc, avg) + LN_EPS)
        gv = g_ref[...]
        y_ref[...] = gv * _sig(gv) * (yh * gng_ref[...])

    col = _ret_specs(_RET_COL0, lambda i: i)
    consts, cspecs = _ret_consts(tb)
    return pl.pallas_call(
        body, name=name,
        out_shape=(jax.ShapeDtypeStruct((L, RW), F32), jax.ShapeDtypeStruct((NB, RW, RW), F32)),
        grid=(NB,),
        in_specs=[col(0), col(1), col(2), col(3)] + cspecs + [pl.BlockSpec((1, RW), lambda i: (0, 0))],
        out_specs=(pl.BlockSpec((T, RW), lambda i: (i, 0)), pl.BlockSpec((None, RW, RW), lambda i: (i, 0, 0))),
        scratch_shapes=[pltpu.VMEM((RW, RW), F32)])(proj, proj, proj, proj, *consts, gng)


def ret_bwd(proj, dycat, states, gng, tb, name):
    def body(q_ref, k_ref, v_ref, g_ref, do_ref, st_ref, dec_ref, zeta_ref, xi_ref, cd_ref, head_ref, same_ref,
             avg_ref, gng_ref, d_ref, dgn_ref, ds_scr):
        i = pl.program_id(0)
        n = NB - 1 - i

        @pl.when(i == 0)
        def _():
            ds_scr[...] = jnp.zeros_like(ds_scr)

        dsn = _bf(ds_scr[...])
        s = st_ref[...]
        sb = _bf(s)
        q, kf, v, valid = _ret_inputs(n, q_ref, k_ref, v_ref)
        xi, zeta, avg = xi_ref[...], zeta_ref[...], avg_ref[...]
        a, y = _ret_chunk(q, kf, v, s, (dec_ref, xi, head_ref))
        yc = y - _seg_mean(y, avg)
        rstd = lax.rsqrt(_seg_mean(yc * yc, avg) + LN_EPS)
        yh = yc * rstd
        gv = g_ref[...]
        sg = _sig(gv)
        sil = gv * sg
        gn = gng_ref[...]
        dout = do_ref[...]
        d_ref[:, 3 * RW:4 * RW] = dout * (yh * gn) * (sg * (1.0 + gv * (1.0 - sg)))
        dyh = dout * sil * gn
        part = jnp.sum(dout * sil * yh, axis=0, keepdims=True)

        @pl.when(i == 0)
        def _():
            dgn_ref[...] = part

        @pl.when(i > 0)
        def _():
            dgn_ref[...] += part

        dy = rstd * (dyh - _seg_mean(dyh, avg) - yh * _seg_mean(dyh * yh, avg))
        dyb = _bf(dy)
        vb, kb, qb = _bf(v), _bf(kf), _bf(q)
        dq = _dot(dyb, sb, nt=True) * xi
        dkf = _dot(vb, dsn, nt=True) * zeta
        dv = _dot(_bf(kf * zeta), dsn)
        for h in range(RH):
            m = head_ref[h]
            da = _dot(_bf(dy * m), vb, nt=True) * dec_ref[h]
            dv = dv + m * _dot(_bf(a[h].T), dyb)
            dq = dq + m * _dot(_bf(da), kb)
            dkf = dkf + m * _dot(_bf(da.T), qb)
        d_ref[:, 0:RW] = dq
        d_ref[:, RW:2 * RW] = jnp.where(valid, dkf * (HD ** -0.5), 0.0)
        d_ref[:, 2 * RW:3 * RW] = dv
        ds_scr[...] = cd_ref[...] * ds_scr[...] + same_ref[...] * _dot(_bf((q * xi).T), dyb)

    back = lambda i: NB - 1 - i
    col = _ret_specs(_RET_COL0, back)
    consts, cspecs = _ret_consts(tb)
    vec = pl.BlockSpec((1, RW), lambda i: (0, 0))
    return pl.pallas_call(
        body, name=name,
        out_shape=(jax.ShapeDtypeStruct((L, 4 * RW), F32), jax.ShapeDtypeStruct((1, RW), F32)),
        grid=(NB,),
        in_specs=[col(0), col(1), col(2), col(3), _ret_specs(3, back)(0),
                  pl.BlockSpec((None, RW, RW), lambda i: (back(i), 0, 0))] + cspecs + [vec],
        out_specs=(pl.BlockSpec((T, 4 * RW), lambda i: (back(i), 0)), vec),
        scratch_shapes=[pltpu.VMEM((RW, RW), F32)])(proj, proj, proj, proj, dycat, states, *consts, gng)


HALO = 32
TAP0 = HALO - (CONV_K - 1)


def _conv_specs():
    cur = lambda col=0: pl.BlockSpec((T, CV), lambda c: (c, col))
    before = lambda col=0: pl.BlockSpec((HALO, CV), lambda c: (jnp.maximum(c * (T // HALO) - 1, 0), col))
    after = pl.BlockSpec((HALO, CV), lambda c: (jnp.minimum((c + 1) * (T // HALO), L // HALO - 1), 0))
    full = lambda r, w: pl.BlockSpec((r, w), lambda c: (0, 0))
    return cur, before, after, full


_CONV_A, _CONV_B = 3, 4
_DY_CONV = 2


def _conv_tile_fwd(c, ca_ref, cah_ref, cb_ref, cbh_ref, w_ref, b_ref, lg_ref, lb_ref, u_scr):
    u_scr[0:HALO, :] = jnp.where(c > 0, cah_ref[...] * _sig(cbh_ref[...]), 0.0)
    u_scr[HALO:HALO + T, :] = ca_ref[...] * _sig(cb_ref[...])
    acc = jnp.zeros((T, CV), F32)
    for k in range(CONV_K):
        acc = acc + w_ref[k:k + 1, :] * u_scr[TAP0 + k:TAP0 + k + T, :]
    xc = acc + b_ref[...]
    xh, rstd = _gn_rows(xc)
    z = xh * lg_ref[...] + lb_ref[...]
    return xh, rstd, z, _sig(z)


def conv_fwd(proj, w, b, lg, lb, pw, og, name):
    def body(ca_ref, cah_ref, cb_ref, cbh_ref, w_ref, b_ref, lg_ref, lb_ref, pw_ref, og_ref, y_ref, u_scr):
        c = pl.program_id(0)
        _, _, z, sg = _conv_tile_fwd(c, ca_ref, cah_ref, cb_ref, cbh_ref, w_ref, b_ref, lg_ref, lb_ref, u_scr)
        zp = _dot(_bf(z * sg), pw_ref[...])
        y_ref[...] = _rms_rows(zp, og_ref[...])[0]

    cur, before, _, full = _conv_specs()
    vec = full(1, CV)
    return pl.pallas_call(
        body, name=name, out_shape=jax.ShapeDtypeStruct((L, CV), F32), grid=(NB,),
        in_specs=[cur(_CONV_A), before(_CONV_A), cur(_CONV_B), before(_CONV_B), full(CONV_K, CV), vec, vec, vec,
                  full(CV, CV), vec],
        out_specs=cur(),
        scratch_shapes=[pltpu.VMEM((HALO + T, CV), F32)])(proj, proj, proj, proj, w, b, lg, lb, pw, og)


def conv_bwd(proj, dycat, w, b, lg, lb, pw, og, name):
    def body1(ca_ref, cah_ref, cb_ref, cbh_ref, dy_ref, w_ref, b_ref, lg_ref, lb_ref, pw_ref, og_ref,
              dxc_ref, dw_ref, db_ref, dlg_ref, dlb_ref, dog_ref, dpw_ref, u_scr):
        c = pl.program_id(0)

        @pl.when(c == 0)
        def _():
            for r in (dw_ref, db_ref, dlg_ref, dlb_ref, dog_ref, dpw_ref):
                r[...] = jnp.zeros_like(r)

        xh, rstd, z, sg = _conv_tile_fwd(c, ca_ref, cah_ref, cb_ref, cbh_ref, w_ref, b_ref, lg_ref, lb_ref, u_scr)
        s = z * sg
        zp = _dot(_bf(s), pw_ref[...])
        ogv = og_ref[...]
        _, zph, r2 = _rms_rows(zp, ogv)
        dyv = dy_ref[...]
        dog_ref[...] += jnp.sum(dyv * zph, axis=0, keepdims=True)
        dzpb = _bf(_rms_rows_bwd(dyv, ogv, zph, r2))
        dpw_ref[...] += _dot(_bf(s.T), dzpb)
        dz = _dot(dzpb, pw_ref[...], nt=True) * (sg * (1.0 + z * (1.0 - sg)))
        dlg_ref[...] += jnp.sum(dz * xh, axis=0, keepdims=True)
        dlb_ref[...] += jnp.sum(dz, axis=0, keepdims=True)
        dxh = dz * lg_ref[...]
        dxc = rstd * (dxh - jnp.mean(dxh, axis=-1, keepdims=True) - xh * jnp.mean(dxh * xh, axis=-1, keepdims=True))
        db_ref[...] += jnp.sum(dxc, axis=0, keepdims=True)
        dxc_ref[...] = dxc
        for k in range(CONV_K):
            dw_ref[k:k + 1, :] += jnp.sum(dxc * u_scr[TAP0 + k:TAP0 + k + T, :], axis=0, keepdims=True)

    def body2(dx_ref, dxa_ref, ca_ref, cb_ref, w_ref, dca_ref, dcb_ref, d_scr):
        c = pl.program_id(0)
        d_scr[0:T, :] = dx_ref[...]
        d_scr[T:T + HALO, :] = jnp.where(c < NB - 1, dxa_ref[...], 0.0)
        du = jnp.zeros((T, CV), F32)
        for k in range(CONV_K):
            off = CONV_K - 1 - k
            du = du + w_ref[k:k + 1, :] * d_scr[off:off + T, :]
        sg = _sig(cb_ref[...])
        dca_ref[...] = du * sg
        dcb_ref[...] = du * ca_ref[...] * sg * (1.0 - sg)

    cur, before, after, full = _conv_specs()
    seq = jax.ShapeDtypeStruct((L, CV), F32)
    vsh = jax.ShapeDtypeStruct((1, CV), F32)
    vec = full(1, CV)
    dxc, dw, db, dlg, dlb, dog, dpw = pl.pallas_call(
        body1, name=name + "_a",
        out_shape=(seq, jax.ShapeDtypeStruct((CONV_K, CV), F32), vsh, vsh, vsh, vsh,
                   jax.ShapeDtypeStruct((CV, CV), F32)),
        grid=(NB,),
        in_specs=[cur(_CONV_A), before(_CONV_A), cur(_CONV_B), before(_CONV_B), cur(_DY_CONV), full(CONV_K, CV), vec,
                  vec, vec, full(CV, CV), vec],
        out_specs=(cur(), full(CONV_K, CV), vec, vec, vec, vec, full(CV, CV)),
        scratch_shapes=[pltpu.VMEM((HALO + T, CV), F32)])(proj, proj, proj, proj, dycat, w, b, lg, lb, pw, og)
    dca, dcb = pl.pallas_call(
        body2, name=name + "_b", out_shape=(seq, seq), grid=(NB,),
        in_specs=[cur(), after, cur(_CONV_A), cur(_CONV_B), full(CONV_K, CV)], out_specs=(cur(), cur()),
        scratch_shapes=[pltpu.VMEM((T + HALO, CV), F32)])(dxc, dxc, proj, proj, w)
    return dca, dcb, dw, db, dlg, dlb, dog, dpw


HR = 8


def _ffn_specs():
    cur = lambda off: pl.BlockSpec((None, T, UPW), lambda j, c: (j + off, c, 0))
    before = lambda off: pl.BlockSpec((None, HR, UPW), lambda j, c: (j + off, jnp.maximum(c * (T // HR) - 1, 0), 0))
    after = lambda off: pl.BlockSpec(
        (None, HR, UPW), lambda j, c: (j + off, jnp.minimum((c + 1) * (T // HR), L // HR - 1), 0))
    wspec = lambda off, r: pl.BlockSpec((None, r, UPW), lambda j, c: (j + off, 0, 0))
    return cur, before, after, wspec


def ffn_act_fwd(f, w, b, name):
    strip = 16

    def body(fg_ref, fgh_ref, fu_ref, fuh_ref, wg_ref, wu_ref, bg_ref, bu_ref, a_ref, gc_ref, uc_ref, hg_scr, hu_scr):
        c = pl.program_id(1)
        for scr, x_ref, xh_ref in ((hg_scr, fg_ref, fgh_ref), (hu_scr, fu_ref, fuh_ref)):
            scr[0:HR, :] = jnp.where(c > 0, xh_ref[...], 0.0)
            scr[HR:HR + strip, :] = x_ref[0:strip, :]
        wg = [wg_ref[k:k + 1, :] for k in range(FFN_K)]
        wu = [wu_ref[k:k + 1, :] for k in range(FFN_K)]
        bg, bu = bg_ref[...], bu_ref[...]

        def conv(src, base, w, b):
            acc = b
            for k in range(FFN_K):
                o = base - (FFN_K - 1) + k
                acc = acc + w[k] * src[o:o + strip, :]
            return acc

        for s in range(T // strip):
            r = s * strip
            gate = conv(hg_scr, HR, wg, bg) if s == 0 else conv(fg_ref, r, wg, bg)
            up = conv(hu_scr, HR, wu, bu) if s == 0 else conv(fu_ref, r, wu, bu)
            gc_ref[r:r + strip, :] = gate
            uc_ref[r:r + strip, :] = up
            a_ref[r:r + strip, :] = (gate * _sig(gate) * up).astype(BF16)

    cur, before, _, wspec = _ffn_specs()
    pre = jax.ShapeDtypeStruct((NPAIR, L, UPW), F32)
    return pl.pallas_call(
        body, name=name, out_shape=(jax.ShapeDtypeStruct((NPAIR, L, UPW), BF16), pre, pre), grid=(NPAIR, NB),
        in_specs=[cur(0), before(0), cur(NPAIR), before(NPAIR), wspec(0, FFN_K), wspec(NPAIR, FFN_K),
                  wspec(0, 1), wspec(NPAIR, 1)],
        out_specs=(cur(0), cur(0), cur(0)),
        scratch_shapes=[pltpu.VMEM((HR + strip, UPW), F32), pltpu.VMEM((HR + strip, UPW), F32)])(
            f, f, f, f, w, w, b, b)


def ffn_act_bwd(f, gc, uc, da, w, name):
    ext = T + HR
    sub = 8

    def body(fg_ref, fu_ref, gc_ref, gca_ref, uc_ref, uca_ref, da_ref, daa_ref, wg_ref, wu_ref,
             dfg_ref, dfu_ref, dwg_ref, dwu_ref, dbg_ref, dbu_ref, dg_scr, du_scr):
        c = pl.program_id(1)
        wg = [wg_ref[k:k + 1, :] for k in range(FFN_K)]
        wu = [wu_ref[k:k + 1, :] for k in range(FFN_K)]

        for s in range(ext // sub):
            r = s * sub
            if r < T:
                gate, up, dav = gc_ref[r:r + sub, :], uc_ref[r:r + sub, :], da_ref[r:r + sub, :]
            else:
                gate, up, dav = gca_ref[...], uca_ref[...], jnp.where(c < NB - 1, daa_ref[...], 0.0)
            sg = _sig(gate)
            dg_scr[r:r + sub, :] = dav * up * (sg * (1.0 + gate * (1.0 - sg)))
            du_scr[r:r + sub, :] = dav * gate * sg

        def back(d_scr, x_ref, w, df_ref, dw_ref, db_ref):
            accs = [jnp.zeros((sub, UPW), F32) for _ in range(FFN_K + 1)]
            for s in range(T // (2 * sub)):
                pieces = []
                for r in (2 * s * sub, (2 * s + 1) * sub):
                    xv = x_ref[r:r + sub, :]
                    df = jnp.zeros((sub, UPW), F32)
                    for k in range(FFN_K):
                        off = FFN_K - 1 - k
                        dj = d_scr[r + off:r + off + sub, :]
                        df = df + w[k] * dj
                        accs[k] = accs[k] + xv * dj
                        if off == 0:
                            accs[FFN_K] = accs[FFN_K] + dj
                    pieces.append(df)
                df_ref[2 * s * sub:2 * (s + 1) * sub, :] = jnp.concatenate(pieces, axis=0).astype(BF16)
            dwp = _stack_rows([jnp.sum(a, axis=0, keepdims=True) for a in accs[:FFN_K]])
            dbp = jnp.sum(accs[FFN_K], axis=0, keepdims=True)

            @pl.when(c == 0)
            def _():
                dw_ref[...] = dwp
                db_ref[...] = dbp

            @pl.when(c > 0)
            def _():
                dw_ref[...] += dwp
                db_ref[...] += dbp

        back(dg_scr, fg_ref, wg, dfg_ref, dwg_ref, dbg_ref)
        back(du_scr, fu_ref, wu, dfu_ref, dwu_ref, dbu_ref)

    cur, _, after, wspec = _ffn_specs()
    slab = jax.ShapeDtypeStruct((NPAIR, L, UPW), BF16)
    wsh = jax.ShapeDtypeStruct((NPAIR, FFN_K, UPW), F32)
    bsh = jax.ShapeDtypeStruct((NPAIR, 1, UPW), F32)
    return pl.pallas_call(
        body, name=name, out_shape=(slab, slab, wsh, wsh, bsh, bsh), grid=(NPAIR, NB),
        in_specs=[cur(0), cur(NPAIR), cur(0), after(0), cur(0), after(0), cur(0), after(0),
                  wspec(0, FFN_K), wspec(NPAIR, FFN_K)],
        out_specs=(cur(0), cur(0), wspec(0, FFN_K), wspec(0, FFN_K), wspec(0, 1), wspec(0, 1)),
        scratch_shapes=[pltpu.VMEM((ext, UPW), F32), pltpu.VMEM((ext, UPW), F32)])(
            f, f, gc, gc, uc, uc, da, da, w, w)


def _mesh_pos():
    return lax.axis_index("x"), lax.axis_index("y"), lax.axis_index("c")


def _peer(pos, k):
    x, y, c = pos
    px = 1 - x if k & 4 else x
    py = 1 - y if k & 2 else y
    pc = 1 - c if k & 1 else c
    return (px, py, pc), 4 * px + 2 * py + pc


_CHIP_FLIPS = (4, 2, 6)
_HBM = pl.BlockSpec(memory_space=pl.ANY)


def all_gather(shards, name):
    nt = len(shards)

    def body(*refs):
        ins, outs = refs[:nt], refs[nt:2 * nt]
        send, recv, local = refs[2 * nt:]
        pos = _mesh_pos()
        me = 4 * pos[0] + 2 * pos[1] + pos[2]
        sib, sib_id = _peer(pos, 1)

        def copy(t, k, block_id, to, src=None):
            dst = outs[t].at[block_id]
            return pltpu.make_async_remote_copy(
                src_ref=dst if src is None else src, dst_ref=dst, send_sem=send.at[t, k], recv_sem=recv.at[t, k],
                device_id=to, device_id_type=MESH)

        locals_ = [pltpu.make_async_copy(ins[t], outs[t].at[me], local.at[t]) for t in range(nt)]
        for cp in locals_:
            cp.start()
        started = []
        for j, flip in enumerate(_CHIP_FLIPS):
            for t in range(nt):
                started.append(copy(t, 1 + j, me, _peer(pos, flip)[0], src=ins[t]))
        for t in range(nt):
            started.append(copy(t, 0, me, sib, src=ins[t]))
        for cp in started:
            cp.start()
        for j, flip in enumerate(_CHIP_FLIPS):
            _, pid = _peer(pos, flip)
            for t in range(nt):
                copy(t, 1 + j, pid, sib).wait_recv()
                fwd = copy(t, 4 + j, pid, sib)
                fwd.start()
                started.append(fwd)
        for t in range(nt):
            copy(t, 0, sib_id, sib).wait_recv()
        for j, flip in enumerate(_CHIP_FLIPS):
            _, pid = _peer(pos, flip | 1)
            for t in range(nt):
                copy(t, 4 + j, pid, sib).wait_recv()
        for cp in started:
            cp.wait_send()
        for cp in locals_:
            cp.wait()

    return pl.pallas_call(
        body, name=name,
        out_shape=tuple(jax.ShapeDtypeStruct((NDEV,) + s.shape, s.dtype) for s in shards),
        in_specs=[_HBM] * nt, out_specs=tuple([_HBM] * nt),
        scratch_shapes=[pltpu.SemaphoreType.DMA((nt, NDEV - 1)), pltpu.SemaphoreType.DMA((nt, NDEV - 1)),
                        pltpu.SemaphoreType.DMA((nt,))])(*shards)


_SPLIT = dict(has_side_effects=pltpu.SideEffectType.DATAFLOW_SIDE_EFFECTING)
_SEM = pl.BlockSpec(memory_space=pltpu.SEMAPHORE)


def _exchange_start(name, srcs, lands, copies, after):
    ns, nl = len(srcs), len(lands)
    ncopy = len(copies(None, [None] * ns, [None] * nl))

    def body(*refs):
        src_refs, land_refs = refs[:ns], refs[ns:ns + nl]
        send, recv = refs[ns + nl + len(after):ns + nl + len(after) + 2]
        token = refs[-1]
        for i, (src, dst, peer) in enumerate(copies(_mesh_pos(), src_refs, land_refs)):
            pltpu.make_async_remote_copy(src_ref=src, dst_ref=dst, send_sem=send.at[i], recv_sem=recv.at[i],
                                         device_id=peer, device_id_type=MESH).start()
        token[...] = jnp.zeros_like(token)

    hbm = lambda a: pltpu.HBM(a.shape, a.dtype)
    out = pl.pallas_call(
        body, name=name,
        out_shape=(pltpu.SemaphoreType.DMA((ncopy,)), pltpu.SemaphoreType.DMA((ncopy,)),
                   *[hbm(a) for a in srcs], *[hbm(a) for a in lands], jax.ShapeDtypeStruct((8, 128), F32)),
        in_specs=[pl.BlockSpec(memory_space=pltpu.HBM)] * (ns + nl) + [_HBM] * len(after),
        out_specs=(_SEM, _SEM, *[pl.BlockSpec(memory_space=pltpu.HBM)] * (ns + nl),
                   pl.BlockSpec(memory_space=pltpu.VMEM)),
        input_output_aliases={i: 2 + i for i in range(ns + nl)},
        compiler_params=pltpu.CompilerParams(**_SPLIT))(
            *[pltpu.with_memory_space_constraint(a, pltpu.HBM) for a in list(srcs) + list(lands)], *after)
    return out[0], out[1], list(out[2:2 + ns]), list(out[2 + ns:2 + ns + nl]), out[-1]


def _exchange_wait(name, send, recv, srcs, lands, copies, after):
    ns, nl = len(srcs), len(lands)

    def body(*refs):
        src_refs, land_refs = refs[:ns], refs[ns:ns + nl]
        send_ref, recv_ref = refs[ns + nl:ns + nl + 2]
        pos = _mesh_pos()
        for i, (src, dst, peer) in enumerate(copies(pos, src_refs, land_refs, arriving=True)):
            cp = pltpu.make_async_remote_copy(src_ref=src, dst_ref=dst, send_sem=send_ref.at[i], recv_sem=recv_ref.at[i],
                                              device_id=peer, device_id_type=MESH)
            cp.wait_send()
            cp.wait_recv()

    hbm = lambda a: pltpu.HBM(a.shape, a.dtype)
    out = pl.pallas_call(
        body, name=name, out_shape=tuple(hbm(a) for a in list(srcs) + list(lands)),
        in_specs=[pl.BlockSpec(memory_space=pltpu.HBM)] * (ns + nl) + [_SEM, _SEM] + [_HBM] * len(after),
        out_specs=tuple([pl.BlockSpec(memory_space=pltpu.HBM)] * (ns + nl)),
        input_output_aliases={i: i for i in range(ns + nl)},
        compiler_params=pltpu.CompilerParams(**_SPLIT))(*srcs, *lands, send, recv, *after)
    return list(out[:ns]), list(out[ns:])


def _gather_copies(pos, srcs, lands, arriving=False):
    if pos is None:
        return [None] * (len(srcs) * (NDEV - 1))
    me = 4 * pos[0] + 2 * pos[1] + pos[2]
    out = []
    for src, land in zip(srcs, lands):
        for k in range(1, NDEV):
            peer, pid = _peer(pos, k)
            out.append((src, land.at[pid if arriving else me], peer))
    return out


def _scatter_copies(pos, srcs, lands, arriving=False):
    if pos is None:
        return [None] * (len(srcs) * (NDEV - 1))
    out = []
    for src, land in zip(srcs, lands):
        for k in range(1, NDEV):
            peer, pid = _peer(pos, k)
            out.append((src.at[pid], land.at[k - 1], peer))
    return out


def gather_start(name, shards, lands, after=()):
    return _exchange_start(name, shards, lands, _gather_copies, list(after))


def gather_wait(name, handle, after=()):
    send, recv, srcs, lands, _ = handle
    return _exchange_wait(name, send, recv, srcs, lands, _gather_copies, list(after))


def scatter_start(name, grads, after=()):
    lands = [pltpu.with_memory_space_constraint(lax.empty((NDEV - 1,) + g.shape[1:], g.dtype), pltpu.HBM)
             for g in grads]
    return _exchange_start(name, grads, lands, _scatter_copies, list(after))


def scatter_wait(name, handle, after=()):
    send, recv, srcs, lands, _ = handle
    return _exchange_wait(name, send, recv, srcs, lands, _scatter_copies, list(after))


def all_reduce_small(part, name):
    rows = part.shape[0]

    def body(x_ref, o_ref, buf, send, recv):
        pos = _mesh_pos()
        me = 4 * pos[0] + 2 * pos[1] + pos[2]
        sends = []
        for k in range(1, NDEV):
            peer, _ = _peer(pos, k)
            sends.append(pltpu.make_async_remote_copy(
                src_ref=x_ref, dst_ref=buf.at[me], send_sem=send.at[k - 1], recv_sem=recv.at[k - 1],
                device_id=peer, device_id_type=MESH))
        for cp in sends:
            cp.start()
        buf[me] = x_ref[...]
        for k in range(1, NDEV):
            peer, pid = _peer(pos, k)
            pltpu.make_async_remote_copy(
                src_ref=x_ref, dst_ref=buf.at[pid], send_sem=send.at[k - 1], recv_sem=recv.at[k - 1],
                device_id=peer, device_id_type=MESH).wait_recv()
        for cp in sends:
            cp.wait_send()
        acc = buf[0]
        for d in range(1, NDEV):
            acc = acc + buf[d]
        o_ref[...] = acc

    return pl.pallas_call(
        body, name=name, out_shape=jax.ShapeDtypeStruct((rows, 128), F32),
        in_specs=[pl.BlockSpec(memory_space=pltpu.VMEM)], out_specs=pl.BlockSpec(memory_space=pltpu.VMEM),
        scratch_shapes=[pltpu.VMEM((NDEV, rows, 128), F32), pltpu.SemaphoreType.DMA((NDEV - 1,)),
                        pltpu.SemaphoreType.DMA((NDEV - 1,))])(part)


def _adamw_update(g, w_ref, m_ref, v_ref, g_ref, d_ref, nm_ref, nv_ref):
    g_ref[...] = g
    nm = ADAM_B1 * m_ref[...] + (1.0 - ADAM_B1) * g
    nv = ADAM_B2 * v_ref[...] + (1.0 - ADAM_B2) * (g * g)
    nm_ref[...] = nm
    nv_ref[...] = nv
    m_hat = nm / (1.0 - ADAM_B1 ** ADAM_STEP)
    v_hat = nv / (1.0 - ADAM_B2 ** ADAM_STEP)
    d_ref[...] = -ADAM_LR * (m_hat / (jnp.sqrt(v_hat) + ADAM_EPS) + ADAM_WD * w_ref[...])


def adamw_big(own, lands, w, m, v, me, tr, name):
    _, r, c = own[0].shape
    nt = r // tr

    def body(me_ref, *refs):
        ins, (w_ref, m_ref, v_ref), outs = refs[:2 * DEPTH], refs[2 * DEPTH:2 * DEPTH + 3], refs[2 * DEPTH + 3:]
        layer = pl.program_id(0)
        for l in range(DEPTH):
            @pl.when(layer == l)
            def _(l=l):
                g = ins[2 * l][...].astype(F32)
                for s in range(NDEV - 1):
                    g = g + ins[2 * l + 1][s].astype(F32)
                _adamw_update(g, w_ref, m_ref, v_ref, *outs)

    pick = lambda l: (lambda layer, i, me_ref: jnp.where(layer == l, i, 0))
    in_specs = []
    for l in range(DEPTH):
        in_specs.append(pl.BlockSpec((None, tr, c), lambda layer, i, me_ref, f=pick(l): (me_ref[0], f(layer, i, me_ref), 0)))
        in_specs.append(pl.BlockSpec((NDEV - 1, tr, c), lambda layer, i, me_ref, f=pick(l): (0, f(layer, i, me_ref), 0)))
    blk = pl.BlockSpec((tr, c), lambda layer, i, me_ref: (layer * nt + i, 0))
    sh = jax.ShapeDtypeStruct((DEPTH * r, c), F32)
    args = [a for l in range(DEPTH) for a in (own[l], lands[l])]
    return pl.pallas_call(
        body, name=name, out_shape=(sh, sh, sh, sh),
        grid_spec=pltpu.PrefetchScalarGridSpec(
            num_scalar_prefetch=1, grid=(DEPTH, nt), in_specs=in_specs + [blk, blk, blk],
            out_specs=(blk, blk, blk, blk)),
        compiler_params=_params(VMEM_BIG))(me, *args, w, m, v)


def adamw(parts, w, m, v, tr, name):
    ns, r, c = parts.shape

    def body(p_ref, w_ref, m_ref, v_ref, g_ref, d_ref, nm_ref, nv_ref):
        g = p_ref[0].astype(F32)
        for s in range(1, ns):
            g = g + p_ref[s].astype(F32)
        _adamw_update(g, w_ref, m_ref, v_ref, g_ref, d_ref, nm_ref, nv_ref)

    blk = pl.BlockSpec((tr, c), lambda i: (i, 0))
    sh = jax.ShapeDtypeStruct((r, c), F32)
    return pl.pallas_call(
        body, name=name, out_shape=(sh, sh, sh, sh), grid=(r // tr,),
        in_specs=[pl.BlockSpec((ns, tr, c), lambda i: (0, i, 0)), blk, blk, blk],
        out_specs=(blk, blk, blk, blk), compiler_params=_params(VMEM_BIG))(parts, w, m, v)


_SPLITS = (NQ * HD, NKV * HD, NKV * HD, CV, CV, RH * HD, RH * HD, RH * HD, RH * HD)
_OFFS = np.concatenate([[0], np.cumsum(_SPLITS)]).tolist()


def _heads(x, nh):
    return x.reshape(L, nh, HD).transpose(1, 0, 2)


def _unheads(x):
    return x.transpose(1, 0, 2).reshape(L, x.shape[0] * HD)


def local_step(x, target, P, hooks):
    tb = _ret_tables()
    h = jnp.concatenate([jnp.zeros((PAD, D), F32), P["meta"], x], axis=0)
    stash = []
    for l in range(DEPTH):
        s = {"h": h}
        s["w_in"], s["cv_pw"], s["w_out"] = hooks["mixer_w"](l, h)
        u = rms_fwd(h, P["norm_mix_g"][l], BF16, f"rms_mix_{l}", after=hooks["first_after"] if l == 0 else None)
        s["ut"] = u.T
        proj = s["proj"] = mm_nn(f"in_proj_{l}", u, s["w_in"], 256)
        s.update(q=_heads(proj[:, :_OFFS[1]], NQ), k=_heads(proj[:, _OFFS[1]:_OFFS[2]], NKV),
                 v=_heads(proj[:, _OFFS[2]:_OFFS[3]], NKV))
        o = attn_fwd(s["q"], s["k"], s["v"], P["q_norm_g"][l], P["k_norm_g"][l], P["attn_sinks"][l], f"attn_fwd_{l}")
        s["o"] = _unheads(o)
        y_att = rms_fwd(s["o"], P["attn_out_g"][l], BF16, f"rms_att_{l}")
        y_cv = conv_fwd(proj, P["cv_dw_w"][l], P["cv_dw_b"][l], P["cv_ln_g"][l], P["cv_ln_b"][l], s["cv_pw"],
                        P["cv_out_g"][l], f"conv_fwd_{l}")
        y_ret, s["states"] = ret_fwd(proj, P["ret_gn_g"][l], tb, f"ret_fwd_{l}")
        ycat = jnp.concatenate([y_att, y_cv.astype(BF16), y_ret.astype(BF16)], axis=1)
        s["ycat_t"] = ycat.T
        h = mm_nn(f"out_proj_{l}", ycat, s["w_out"], 256, res=h)
        s["h1"] = h
        s["ffn_up"], s["ffn_down"] = hooks["ffn_w"](l, h)
        u2 = rms_fwd(h, P["norm_ffn_g"][l], BF16, f"rms_ffn_{l}")
        s["u2t"] = u2.T
        s["f"] = ffn_up_fwd(f"ffn_up_{l}", u2, s["ffn_up"])
        a, s["gc"], s["uc"] = ffn_act_fwd(s["f"], P["ffn_dw_w"][l], P["ffn_dw_b"][l], f"ffn_act_fwd_{l}")
        s["at"] = a.transpose(0, 2, 1)
        h = ffn_down_fwd(f"ffn_down_{l}", a, s["ffn_down"], h)
        stash.append(s)

    dh, dhm, loss = loss_head(h, target)
    G = {n: [None] * DEPTH for n in ("norm_mix_g", "q_norm_g", "k_norm_g", "attn_sinks", "attn_out_g", "cv_dw_w",
                                     "cv_dw_b", "cv_ln_g", "cv_ln_b", "cv_out_g", "ret_gn_g", "norm_ffn_g",
                                     "ffn_dw_w", "ffn_dw_b")}
    for l in reversed(range(DEPTH)):
        s = stash[l]
        da = ffn_down_dx(f"ffn_down_dx_{l}", dhm, s["ffn_down"])
        dw_down = ffn_down_dw(f"ffn_down_dw_{l}", s["at"], dhm)
        dfg, dfu, dwg, dwu, dbg, dbu = ffn_act_bwd(s["f"], s["gc"], s["uc"], da, P["ffn_dw_w"][l], f"ffn_act_bwd_{l}")
        G["ffn_dw_w"][l] = jnp.concatenate([dwg, dwu], axis=0)
        G["ffn_dw_b"][l] = jnp.concatenate([dbg, dbu], axis=0)
        du2 = ffn_up_dx(f"ffn_up_dx_{l}", dfg, dfu, s["ffn_up"])
        dw_up = ffn_up_dw(f"ffn_up_dw_{l}", s["u2t"], dfg, dfu)
        after = hooks["ffn_grads"](l, dw_down, dw_up)
        dh, dhm, G["norm_ffn_g"][l] = rms_bwd(s["h1"], P["norm_ffn_g"][l], du2, dh, f"rms_ffn_bwd_{l}", after=after)
        dycat = mm_nt(f"out_proj_dx_{l}", dhm, s["w_out"], 256)
        dw_out = mm_nn(f"out_proj_dw_{l}", s["ycat_t"], dhm, 256, out_dtype=BF16)
        do, _, G["attn_out_g"][l] = rms_bwd(s["o"], P["attn_out_g"][l], dycat, None, f"rms_att_bwd_{l}")
        (dq, dk_c, dk_p, dk_m, dv_c, dv_p, dv_m, G["q_norm_g"][l], G["k_norm_g"][l], dsk) = attn_bwd(
            s["q"], s["k"], s["v"], _heads(do, NQ), P["q_norm_g"][l], P["k_norm_g"][l], P["attn_sinks"][l],
            f"attn_bwd_{l}")
        G["attn_sinks"][l] = dsk[:, :, 0].reshape(NQ)
        shift = lambda z: jnp.concatenate([z[:, T:], jnp.zeros((NKV, T, HD), F32)], axis=1)
        dk = (dk_c + shift(dk_p)).at[:, :T].add(dk_m)
        dv = (dv_c + shift(dv_p)).at[:, :T].add(dv_m)
        (dca, dcb, G["cv_dw_w"][l], G["cv_dw_b"][l], G["cv_ln_g"][l], G["cv_ln_b"][l], G["cv_out_g"][l],
         dpw) = conv_bwd(s["proj"], dycat, P["cv_dw_w"][l], P["cv_dw_b"][l], P["cv_ln_g"][l],
                         P["cv_ln_b"][l], s["cv_pw"], P["cv_out_g"][l], f"conv_bwd_{l}")
        dret, G["ret_gn_g"][l] = ret_bwd(s["proj"], dycat, s["states"], P["ret_gn_g"][l], tb, f"ret_bwd_{l}")
        dproj = jnp.concatenate([_unheads(dq), _unheads(dk), _unheads(dv), dca, dcb, dret], axis=1).astype(BF16)
        du = mm_nt(f"in_proj_dx_{l}", dproj, s["w_in"], 256)
        dw_in = mm_nn(f"in_proj_dw_{l}", s["ut"], dproj, 256, out_dtype=BF16)
        after = hooks["mixer_grads"](l, dw_out, dpw, dw_in)
        dh, dhm, G["norm_mix_g"][l] = rms_bwd(s["h"], P["norm_mix_g"][l], du, dh, f"rms_mix_bwd_{l}", after=after)
    return loss[0, 0], dh[T:], dh[PAD:T], G


_SMALL = ("meta", "norm_mix_g", "q_norm_g", "k_norm_g", "attn_sinks", "attn_out_g", "cv_dw_w", "cv_dw_b", "cv_ln_g",
          "cv_ln_b", "cv_out_g", "ret_gn_g", "norm_ffn_g", "ffn_dw_w", "ffn_dw_b")
_BIG = ("w_in", "cv_pw", "w_out", "ffn_up", "ffn_down")
_ORDER = ("meta", "norm_mix_g", "w_in", "q_norm_g", "k_norm_g", "attn_sinks", "attn_out_g", "cv_dw_w", "cv_dw_b",
          "cv_ln_g", "cv_ln_b", "cv_pw", "cv_out_g", "ret_gn_g", "w_out", "norm_ffn_g", "ffn_up", "ffn_dw_w",
          "ffn_dw_b", "ffn_down")
_SMALL_SHARDED = {"meta": D, "cv_dw_w": CV, "ffn_dw_w": 2 * D_FF}


def _pack(arrs):
    flat = jnp.concatenate([a.reshape(-1) for a in arrs])
    n = flat.shape[0]
    rows = -(-n // 1024) * 8
    return jnp.pad(flat, (0, rows * 128 - n)).reshape(rows, 128)


def _unpack(packed, shapes):
    flat = packed.reshape(-1)
    out, off = [], 0
    for s in shapes:
        n = int(np.prod(s))
        out.append(flat[off:off + n].reshape(s))
        off += n
    return out


def kernel(x, meta, norm_mix_g, w_in, q_norm_g, k_norm_g, attn_sinks, attn_out_g, cv_dw_w, cv_dw_b, cv_ln_g, cv_ln_b, cv_pw, cv_out_g, ret_gn_g, w_out, norm_ffn_g, ffn_up, ffn_dw_w, ffn_dw_b, ffn_down, loss_target, m_meta, m_norm_mix_g, m_w_in, m_q_norm_g, m_k_norm_g, m_attn_sinks, m_attn_out_g, m_cv_dw_w, m_cv_dw_b, m_cv_ln_g, m_cv_ln_b, m_cv_pw, m_cv_out_g, m_ret_gn_g, m_w_out, m_norm_ffn_g, m_ffn_up, m_ffn_dw_w, m_ffn_dw_b, m_ffn_down, v_meta, v_norm_mix_g, v_w_in, v_q_norm_g, v_k_norm_g, v_attn_sinks, v_attn_out_g, v_cv_dw_w, v_cv_dw_b, v_cv_ln_g, v_cv_ln_b, v_cv_pw, v_cv_out_g, v_ret_gn_g, v_w_out, v_norm_ffn_g, v_ffn_up, v_ffn_dw_w, v_ffn_dw_b, v_ffn_down):
    W = dict(meta=meta, norm_mix_g=norm_mix_g, w_in=w_in, q_norm_g=q_norm_g, k_norm_g=k_norm_g,
             attn_sinks=attn_sinks, attn_out_g=attn_out_g, cv_dw_w=cv_dw_w, cv_dw_b=cv_dw_b, cv_ln_g=cv_ln_g,
             cv_ln_b=cv_ln_b, cv_pw=cv_pw, cv_out_g=cv_out_g, ret_gn_g=ret_gn_g, w_out=w_out,
             norm_ffn_g=norm_ffn_g, ffn_up=ffn_up, ffn_dw_w=ffn_dw_w, ffn_dw_b=ffn_dw_b, ffn_down=ffn_down)
    M = dict(meta=m_meta, norm_mix_g=m_norm_mix_g, w_in=m_w_in, q_norm_g=m_q_norm_g, k_norm_g=m_k_norm_g,
             attn_sinks=m_attn_sinks, attn_out_g=m_attn_out_g, cv_dw_w=m_cv_dw_w, cv_dw_b=m_cv_dw_b,
             cv_ln_g=m_cv_ln_g, cv_ln_b=m_cv_ln_b, cv_pw=m_cv_pw, cv_out_g=m_cv_out_g, ret_gn_g=m_ret_gn_g,
             w_out=m_w_out, norm_ffn_g=m_norm_ffn_g, ffn_up=m_ffn_up, ffn_dw_w=m_ffn_dw_w, ffn_dw_b=m_ffn_dw_b,
             ffn_down=m_ffn_down)
    V = dict(meta=v_meta, norm_mix_g=v_norm_mix_g, w_in=v_w_in, q_norm_g=v_q_norm_g, k_norm_g=v_k_norm_g,
             attn_sinks=v_attn_sinks, attn_out_g=v_attn_out_g, cv_dw_w=v_cv_dw_w, cv_dw_b=v_cv_dw_b,
             cv_ln_g=v_cv_ln_g, cv_ln_b=v_cv_ln_b, cv_pw=v_cv_pw, cv_out_g=v_cv_out_g, ret_gn_g=v_ret_gn_g,
             w_out=v_w_out, norm_ffn_g=v_norm_ffn_g, ffn_up=v_ffn_up, ffn_dw_w=v_ffn_dw_w, ffn_dw_b=v_ffn_dw_b,
             ffn_down=v_ffn_down)
    me = 4 * lax.axis_index("x") + 2 * lax.axis_index("y") + lax.axis_index("c")

    sh = {n: [W[n][l].astype(BF16) for l in range(DEPTH)] for n in _BIG}
    mix = lambda l: [sh["w_in"][l], sh["cv_pw"][l], sh["w_out"][l]]
    ffn = lambda l: [sh["ffn_up"][l], sh["ffn_down"][l]]
    first = all_gather(mix(0) + [meta, cv_dw_w, ffn_dw_w], "gather_first")
    g_meta, g_cdw, g_fdw = first[3:6]

    def landing(shards):
        return [lax.dynamic_update_slice(lax.empty((NDEV,) + s.shape, s.dtype), s[None], (me,) + (0,) * s.ndim)
                for s in shards]

    gathers = {("ffn", 0): gather_start("gather_ffn0_start", ffn(0), landing(ffn(0)), after=[first[0]])}
    gathers["mix", 1] = gather_start("gather_mix1_start", mix(1), landing(mix(1)), after=[gathers["ffn", 0][4]])
    gathers["ffn", 1] = gather_start("gather_ffn1_start", ffn(1), landing(ffn(1)), after=[gathers["mix", 1][4]])

    def mixer_w(l, h):
        g_in, g_pw, g_out = first[0:3] if l == 0 else gather_wait(f"gather_mix{l}_wait", gathers["mix", l], after=[h])[1]
        return g_in.transpose(1, 0, 2).reshape(D, IN_W), g_pw.reshape(CV, CV), g_out.reshape(D, D)

    def ffn_w(l, h1):
        return gather_wait(f"gather_ffn{l}_wait", gathers["ffn", l], after=[h1])[1]

    scatters = {}

    def ffn_grads(l, dw_down, dw_up):
        scatters["ffn", l] = scatter_start(f"scatter_ffn{l}_start", [dw_up, dw_down.reshape(NDEV, DNR, D)])
        return scatters["ffn", l][4]

    def mixer_grads(l, dw_out, dpw, dw_in):
        grads = [dw_in.reshape(D, NDEV, IN_W // NDEV).transpose(1, 0, 2), dpw.astype(BF16).reshape(NDEV, CV // NDEV, CV),
                 dw_out.reshape(NDEV, D // NDEV, D)]
        scatters["mix", l] = scatter_start(f"scatter_mix{l}_start", grads)
        return scatters["mix", l][4]

    P = dict(
        meta=g_meta.transpose(1, 0, 2).reshape(N_META, D),
        cv_dw_w=g_cdw.transpose(1, 2, 0, 3).reshape(DEPTH, CONV_K, CV),
        ffn_dw_w=g_fdw.transpose(1, 0, 2, 3),
        ffn_dw_b=ffn_dw_b.reshape(DEPTH, NDEV, 1, UPW),
        attn_sinks=attn_sinks,
    )
    for n in ("norm_mix_g", "q_norm_g", "k_norm_g", "attn_out_g", "cv_dw_b", "cv_ln_g", "cv_ln_b", "cv_out_g",
              "ret_gn_g", "norm_ffn_g"):
        P[n] = W[n].reshape(DEPTH, 1, -1)

    hooks = dict(mixer_w=mixer_w, ffn_w=ffn_w, ffn_grads=ffn_grads, mixer_grads=mixer_grads,
                 first_after=gathers["ffn", 1][4])
    loss_part, dx, dmeta, G = local_step(x[0], loss_target[0], P, hooks)
    loss = lax.psum(loss_part, ("x", "y", "c"))

    small_full = {
        "meta": dmeta,
        "cv_dw_w": jnp.stack(G["cv_dw_w"]),
        "ffn_dw_w": jnp.stack([g.transpose(1, 0, 2).reshape(FFN_K, 2 * D_FF) for g in G["ffn_dw_w"]]),
        "ffn_dw_b": jnp.stack([g.reshape(2 * D_FF) for g in G["ffn_dw_b"]]),
        "attn_sinks": jnp.stack(G["attn_sinks"]),
    }
    for n in _SMALL:
        if n not in small_full:
            small_full[n] = jnp.stack([g.reshape(-1) for g in G[n]])
    shapes = [small_full[n].shape for n in _SMALL]
    summed = _unpack(all_reduce_small(_pack([small_full[n] for n in _SMALL]), "reduce_small"), shapes)
    small_g = {}
    for n, g in zip(_SMALL, summed):
        if n in _SMALL_SHARDED:
            width = _SMALL_SHARDED[n] // NDEV
            g = lax.dynamic_slice_in_dim(g, me * width, width, axis=g.ndim - 1)
        small_g[n] = g

    out = {}
    tiles = {"w_in": 256, "cv_pw": 32, "w_out": 128, "ffn_up": 256, "ffn_down": 176}
    me1 = me.astype(jnp.int32).reshape(1)
    own, lands = {n: [None] * DEPTH for n in _BIG}, {n: [None] * DEPTH for n in _BIG}

    def arrived(kind, names, after):
        for l in range(DEPTH):
            srcs, got = scatter_wait(f"scatter_{kind}{l}_wait", scatters[kind, l], after=after)
            for n, s_, g_ in zip(names, srcs, got):
                own[n][l], lands[n][l] = s_, g_

    def update(names):
        for n in names:
            shard = W[n].shape
            rows, cols = shard[0] * shard[1], shard[2]
            res = adamw_big(own[n], lands[n], W[n].reshape(rows, cols), M[n].reshape(rows, cols),
                            V[n].reshape(rows, cols), me1, tiles[n], f"adamw_{n}")
            out[n] = [r.reshape(shard) for r in res]

    arrived("ffn", ("ffn_up", "ffn_down"), [dx])
    update(("ffn_up", "ffn_down"))
    sshapes = [W[n].shape for n in _SMALL]
    packed_g = _pack([small_g[n] for n in _SMALL])
    res = adamw(packed_g[None], _pack([W[n] for n in _SMALL]), _pack([M[n] for n in _SMALL]),
                _pack([V[n] for n in _SMALL]), packed_g.shape[0], "adamw_small")
    for i, r in enumerate(res):
        for n, a in zip(_SMALL, _unpack(r, sshapes)):
            out.setdefault(n, [None] * 4)[i] = a
    arrived("mix", ("w_in", "cv_pw", "w_out"), [out["ffn_down"][0], res[0]])
    update(("w_in", "cv_pw", "w_out"))

    return (loss, dx[None], *[out[n][0] for n in _ORDER], *[out[n][1] for n in _ORDER],
            *[out[n][2] for n in _ORDER], *[out[n][3] for n in _ORDER])
```

```python
import functools
import math

import numpy as np
import jax
import jax.numpy as jnp
from jax import lax
from jax.experimental import pallas as pl
from jax.experimental.pallas import tpu as pltpu

F32 = jnp.float32
BF16 = jnp.bfloat16

D = 1024
SEQ = 2048
DEPTH = 2
T = 128
L = SEQ + T
NB = L // T
N_META = 16
PAD = T - N_META
HD = 64
NQ = 8
NKV = 2
GQA = NQ // NKV
CV = 256
CONV_K = 31
RH = 4
D_FF = 2816
FFN_K = 3
IN_W = 2304
RMS_EPS = 1e-6
LN_EPS = 1e-5
NEG = -1e30
NDEV = 8
UPW = 2 * D_FF // NDEV
DNR = D_FF // NDEV
NPAIR = NDEV // 2

ADAM_LR, ADAM_B1, ADAM_B2, ADAM_EPS, ADAM_WD, ADAM_STEP = 0.001, 0.9, 0.999, 1e-08, 0.01, 10

VMEM_BIG = 56 * 1024 * 1024

MESH = pl.DeviceIdType.MESH


def _params(vmem=None):
    return pltpu.CompilerParams(vmem_limit_bytes=vmem) if vmem else None


def _dot(a, b, nt=False):
    return lax.dot_general(a, b, (((1,), (1 if nt else 0,)), ((), ())), preferred_element_type=F32)


def _sig(x):
    return 1.0 / (1.0 + jnp.exp(-x))


def _bf(x):
    return x.astype(BF16)


def _stack_rows(rows):
    idx = lax.broadcasted_iota(jnp.int32, (len(rows), rows[0].shape[1]), 0)
    out = jnp.zeros((len(rows), rows[0].shape[1]), F32)
    for r, v in enumerate(rows):
        out = jnp.where(idx == r, v, out)
    return out


def rms_fwd(x, g, out_dtype, name, after=None):
    n, w = x.shape

    def body(x_ref, g_ref, *rest):
        o_ref = rest[-1]
        xv = x_ref[...]
        r = lax.rsqrt(jnp.mean(xv * xv, axis=-1, keepdims=True) + RMS_EPS)
        o_ref[...] = (xv * r * g_ref[...]).astype(o_ref.dtype)

    deps = [] if after is None else [after]
    return pl.pallas_call(
        body, name=name, out_shape=jax.ShapeDtypeStruct((n, w), out_dtype), grid=(n // T,),
        in_specs=[pl.BlockSpec((T, w), lambda i: (i, 0)), pl.BlockSpec((1, w), lambda i: (0, 0))] + [_HBM] * len(deps),
        out_specs=pl.BlockSpec((T, w), lambda i: (i, 0)))(x, g, *deps)


def rms_bwd(x, g, dy, dres, name, after=None, dy_col=0):
    n, w = x.shape
    has_res = dres is not None
    deps = [] if after is None else [after]

    def body(x_ref, g_ref, dy_ref, *rest):
        rest = rest[:len(rest) - 3 - len(deps)] + rest[len(rest) - 3:]
        if has_res:
            dres_ref, dx_ref, dxm_ref, dg_ref = rest
        else:
            dx_ref, dxm_ref, dg_ref = rest
        i = pl.program_id(0)
        xv = x_ref[...]
        r = lax.rsqrt(jnp.mean(xv * xv, axis=-1, keepdims=True) + RMS_EPS)
        xh = xv * r
        dyv = dy_ref[...]
        dxh = dyv * g_ref[...]
        dx = r * (dxh - xh * jnp.mean(dxh * xh, axis=-1, keepdims=True))
        if has_res:
            dx = dx + dres_ref[...]
        dx_ref[...] = dx
        rows = i * T + lax.broadcasted_iota(jnp.int32, (T, 1), 0)
        dxm_ref[...] = jnp.where(rows >= PAD, dx, 0.0).astype(BF16)
        part = jnp.sum(dyv * xh, axis=0, keepdims=True)

        @pl.when(i == 0)
        def _():
            dg_ref[...] = part

        @pl.when(i > 0)
        def _():
            dg_ref[...] += part

    row = pl.BlockSpec((T, w), lambda i: (i, 0))
    vec = pl.BlockSpec((1, w), lambda i: (0, 0))
    ins = [x, g, dy] + ([dres] if has_res else []) + deps
    return pl.pallas_call(
        body, name=name,
        out_shape=(jax.ShapeDtypeStruct((n, w), F32), jax.ShapeDtypeStruct((n, w), BF16),
                   jax.ShapeDtypeStruct((1, w), F32)),
        grid=(n // T,),
        in_specs=[row, vec, pl.BlockSpec((T, w), lambda i: (i, dy_col))] + ([row] if has_res else [])
        + [_HBM] * len(deps),
        out_specs=(row, row, vec))(*ins)


def loss_head(h, target):
    def body(h_ref, t_ref, dh_ref, dhm_ref, loss_ref):
        n = pl.program_id(0)
        e = jnp.where(n > 0, h_ref[...] - t_ref[...], 0.0)
        dh = e * (1.0 / D)
        dh_ref[...] = dh
        dhm_ref[...] = dh.astype(BF16)
        part = jnp.sum(jnp.sum(e * e, axis=1, keepdims=True), axis=0, keepdims=True) * (0.5 / D)

        @pl.when(n == 0)
        def _():
            loss_ref[...] = jnp.zeros_like(loss_ref)

        @pl.when(n > 0)
        def _():
            loss_ref[...] += jnp.broadcast_to(part, loss_ref.shape)

    row = pl.BlockSpec((T, D), lambda n: (n, 0))
    return pl.pallas_call(
        body, name="loss_head",
        out_shape=(jax.ShapeDtypeStruct((L, D), F32), jax.ShapeDtypeStruct((L, D), BF16),
                   jax.ShapeDtypeStruct((8, 128), F32)),
        grid=(NB,),
        in_specs=[row, pl.BlockSpec((T, D), lambda n: (jnp.maximum(n - 1, 0), 0))],
        out_specs=(row, row, pl.BlockSpec((8, 128), lambda n: (0, 0))))(h, target)


def _mm(name, a, b, *, grid, a_spec, b_spec, o_spec, out_shape, nt=False, ta=False, red=False, res=None,
        res_spec=None):
    def body(a_ref, b_ref, *rest):
        o_ref = rest[-1]
        av = a_ref[...]
        bv = b_ref[...]
        if bv.ndim == 3:
            bv = bv.reshape(bv.shape[0] * bv.shape[1], bv.shape[2])
        if ta:
            acc = lax.dot_general(av, bv, (((0,), (0,)), ((), ())), preferred_element_type=F32)
        else:
            acc = _dot(av, bv, nt)
        if red:
            k = pl.program_id(0)

            @pl.when(k == 0)
            def _():
                o_ref[...] = acc

            @pl.when(k > 0)
            def _():
                o_ref[...] += acc
        else:
            if res is not None:
                rows = lax.broadcasted_iota(jnp.int32, (acc.shape[0], 1), 0)
                acc = rest[0][...] + jnp.where(rows >= PAD, acc, 0.0)
            o_ref[...] = acc.astype(o_ref.dtype)

    ins = [a, b] + ([res] if res is not None else [])
    specs = [a_spec, b_spec] + ([res_spec] if res is not None else [])
    return pl.pallas_call(body, name=name, out_shape=out_shape, grid=grid, in_specs=specs, out_specs=o_spec,
                          compiler_params=_params(VMEM_BIG))(*ins)


def mm_nn(name, a, b, tn, out_dtype=F32, res=None):
    m, k = a.shape
    n = b.shape[1]
    return _mm(name, a, b, grid=(n // tn,),
               a_spec=pl.BlockSpec((m, k), lambda j: (0, 0)), b_spec=pl.BlockSpec((k, tn), lambda j: (0, j)),
               o_spec=pl.BlockSpec((m, tn), lambda j: (0, j)), out_shape=jax.ShapeDtypeStruct((m, n), out_dtype),
               res=res, res_spec=pl.BlockSpec((m, tn), lambda j: (0, j)))


def mm_nt(name, a, b, tn):
    m, k = a.shape
    n = b.shape[0]
    return _mm(name, a, b, grid=(n // tn,), nt=True,
               a_spec=pl.BlockSpec((m, k), lambda j: (0, 0)), b_spec=pl.BlockSpec((tn, k), lambda j: (j, 0)),
               o_spec=pl.BlockSpec((m, tn), lambda j: (0, j)), out_shape=jax.ShapeDtypeStruct((m, n), F32))


def ffn_up_fwd(name, u2, wup):
    return _mm(name, u2, wup, grid=(NDEV,),
               a_spec=pl.BlockSpec((L, D), lambda j: (0, 0)),
               b_spec=pl.BlockSpec((None, D, UPW), lambda j: (j, 0, 0)),
               o_spec=pl.BlockSpec((None, L, UPW), lambda j: (j, 0, 0)),
               out_shape=jax.ShapeDtypeStruct((NDEV, L, UPW), F32))


def _slab_specs():
    gate = pl.BlockSpec((None, L, UPW), lambda j: (jnp.minimum(j, NPAIR - 1), 0, 0))
    up = pl.BlockSpec((None, L, UPW), lambda j: (jnp.maximum(j - NPAIR, 0), 0, 0))
    return gate, up


def ffn_up_dx(name, dfg, dfu, wup):
    def body(g_ref, u_ref, b_ref, o_ref):
        j = pl.program_id(0)

        @pl.when(j == 0)
        def _():
            o_ref[...] = _dot(g_ref[...], b_ref[...], nt=True)

        @pl.when(jnp.logical_and(j > 0, j < NPAIR))
        def _():
            o_ref[...] += _dot(g_ref[...], b_ref[...], nt=True)

        @pl.when(j >= NPAIR)
        def _():
            o_ref[...] += _dot(u_ref[...], b_ref[...], nt=True)

    gate, up = _slab_specs()
    return pl.pallas_call(
        body, name=name, out_shape=jax.ShapeDtypeStruct((L, D), F32), grid=(NDEV,),
        in_specs=[gate, up, pl.BlockSpec((None, D, UPW), lambda j: (j, 0, 0))],
        out_specs=pl.BlockSpec((L, D), lambda j: (0, 0)), compiler_params=_params(VMEM_BIG))(dfg, dfu, wup)


def ffn_up_dw(name, u2t, dfg, dfu):
    def body(a_ref, g_ref, u_ref, o_ref):
        j = pl.program_id(0)

        @pl.when(j < NPAIR)
        def _():
            o_ref[...] = _dot(a_ref[...], g_ref[...]).astype(o_ref.dtype)

        @pl.when(j >= NPAIR)
        def _():
            o_ref[...] = _dot(a_ref[...], u_ref[...]).astype(o_ref.dtype)

    gate, up = _slab_specs()
    return pl.pallas_call(
        body, name=name, out_shape=jax.ShapeDtypeStruct((NDEV, D, UPW), BF16), grid=(NDEV,),
        in_specs=[pl.BlockSpec((D, L), lambda j: (0, 0)), gate, up],
        out_specs=pl.BlockSpec((None, D, UPW), lambda j: (j, 0, 0)), compiler_params=_params(VMEM_BIG))(u2t, dfg, dfu)


def ffn_down_fwd(name, a, wdn, res, tn=256):
    def body(a_ref, b_ref, r_ref, o_ref):
        acc = jnp.zeros((L, tn), F32)
        for g in range(NPAIR):
            bv = b_ref[2 * g:2 * g + 2]
            acc = acc + _dot(a_ref[g], bv.reshape(2 * DNR, tn))
        rows = lax.broadcasted_iota(jnp.int32, (L, 1), 0)
        o_ref[...] = r_ref[...] + jnp.where(rows >= PAD, acc, 0.0)

    return pl.pallas_call(
        body, name=name, out_shape=jax.ShapeDtypeStruct((L, D), F32), grid=(D // tn,),
        in_specs=[pl.BlockSpec((NPAIR, L, UPW), lambda j: (0, 0, 0)),
                  pl.BlockSpec((NDEV, DNR, tn), lambda j: (0, 0, j)),
                  pl.BlockSpec((L, tn), lambda j: (0, j))],
        out_specs=pl.BlockSpec((L, tn), lambda j: (0, j)),
        compiler_params=_params(VMEM_BIG))(a, wdn, res)


def ffn_down_dx(name, dh, wdn):
    return _mm(name, dh, wdn, grid=(NPAIR,), nt=True,
               a_spec=pl.BlockSpec((L, D), lambda g: (0, 0)),
               b_spec=pl.BlockSpec((2, DNR, D), lambda g: (g, 0, 0)),
               o_spec=pl.BlockSpec((None, L, UPW), lambda g: (g, 0, 0)),
               out_shape=jax.ShapeDtypeStruct((NPAIR, L, UPW), F32))


def ffn_down_dw(name, a, dh):
    return _mm(name, a, dh, grid=(NPAIR,), ta=True,
               a_spec=pl.BlockSpec((None, L, UPW), lambda g: (g, 0, 0)),
               b_spec=pl.BlockSpec((L, D), lambda g: (0, 0)),
               o_spec=pl.BlockSpec((UPW, D), lambda g: (g, 0)),
               out_shape=jax.ShapeDtypeStruct((D_FF, D), BF16))


_SLOPES = [2.0 ** (-8.0 * (h + 1) / NQ) for h in range(NQ)]
_SCALE = HD ** -0.5


QR = GQA * T
KC = 3 * T


def _attn_geometry(g, n, sink_ref):
    row = lax.broadcasted_iota(jnp.int32, (QR, KC), 0)
    col = lax.broadcasted_iota(jnp.int32, (QR, KC), 1)
    i = row & (T - 1)
    j = col & (T - 1)
    blk = col // T
    d_meta = n * T + i - j
    ok_meta = (j >= PAD) & (d_meta >= 0)
    ok_prev = j > i + jnp.where(n >= 2, 0, T)
    ok_cur = j <= i - jnp.where(n >= 1, 0, T)
    ok = ((blk == 0) & ok_meta) | ((blk == 1) & ok_prev) | ((blk == 2) & ok_cur)
    dist = jnp.where(blk == 0, jnp.minimum(d_meta, T), jnp.where(blk == 1, T + i - j, i - j)).astype(F32)
    head = lax.broadcasted_iota(jnp.int32, (QR, 1), 0) // T
    slope = jnp.zeros((QR, 1), F32)
    sink = jnp.zeros((QR, 1), F32)
    for hh in range(GQA):
        slope = jnp.where(head == hh, jnp.where(g == 0, _SLOPES[hh], _SLOPES[GQA + hh]), slope)
        sink = jnp.where(head == hh, sink_ref[g * GQA + hh], sink)
    return ok, slope * dist, sink


def _attn_probs(qn, keys, geo):
    ok, penalty, sink = geo
    s = jnp.where(ok, _dot(qn, keys, nt=True) * _SCALE - penalty, NEG)
    m = jnp.maximum(jnp.max(s, axis=-1, keepdims=True), sink)
    e = jnp.exp(s - m)
    e_sink = jnp.exp(sink - m)
    inv = 1.0 / (e.sum(axis=-1, keepdims=True) + e_sink)
    return e, inv, e_sink * inv


def _rms_rows(x, g):
    r = lax.rsqrt(jnp.mean(x * x, axis=-1, keepdims=True) + RMS_EPS)
    xh = x * r
    return xh * g, xh, r


def _rms_rows_bwd(dy, g, xh, r):
    dxh = dy * g
    return r * (dxh - xh * jnp.mean(dxh * xh, axis=-1, keepdims=True))


def _attn_specs():
    qspec = pl.BlockSpec((GQA, T, HD), lambda g, n: (g, n, 0))
    cur = pl.BlockSpec((None, T, HD), lambda g, n: (g, n, 0))
    prev = pl.BlockSpec((None, T, HD), lambda g, n: (g, jnp.maximum(n - 1, 0), 0))
    meta = pl.BlockSpec((None, T, HD), lambda g, n: (g, 0, 0))
    vec = pl.BlockSpec((1, HD), lambda g, n: (0, 0))
    return qspec, cur, prev, meta, vec


def _rows3(m_ref, p_ref, c_ref):
    return jnp.concatenate([m_ref[...], p_ref[...], c_ref[...]], axis=0)


def attn_fwd(q, k, v, qg, kg, sinks, name):
    def body(sink_ref, q_ref, kc_ref, kp_ref, km_ref, vc_ref, vp_ref, vm_ref, qg_ref, kg_ref, o_ref):
        g = pl.program_id(0)
        n = pl.program_id(1)
        geo = _attn_geometry(g, n, sink_ref)
        keys = _bf(_rms_rows(_rows3(km_ref, kp_ref, kc_ref), kg_ref[...])[0])
        vals = _bf(_rows3(vm_ref, vp_ref, vc_ref))
        qn = _bf(_rms_rows(q_ref[...].reshape(QR, HD), qg_ref[...])[0])
        e, inv, _ = _attn_probs(qn, keys, geo)
        o_ref[...] = (_dot(_bf(e), vals) * inv).reshape(GQA, T, HD)

    qspec, cur, prev, meta, vec = _attn_specs()
    return pl.pallas_call(
        body, name=name, out_shape=jax.ShapeDtypeStruct((NQ, L, HD), F32), grid=(NKV, NB),
        in_specs=[pl.BlockSpec(memory_space=pltpu.SMEM), qspec, cur, prev, meta, cur, prev, meta, vec, vec],
        out_specs=qspec)(sinks, q, k, k, k, v, v, v, qg, kg)


def attn_bwd(q, k, v, do, qg, kg, sinks, name):
    def body(sink_ref, q_ref, kc_ref, kp_ref, km_ref, vc_ref, vp_ref, vm_ref, do_ref, qg_ref, kg_ref,
             dq_ref, dkc_ref, dkp_ref, dkm_ref, dvc_ref, dvp_ref, dvm_ref, dqg_ref, dkg_ref, dsk_ref):
        g = pl.program_id(0)
        n = pl.program_id(1)
        geo = _attn_geometry(g, n, sink_ref)
        kgv = kg_ref[...]
        qgv = qg_ref[...]
        kn_f, kh, kr = _rms_rows(_rows3(km_ref, kp_ref, kc_ref), kgv)
        keys = _bf(kn_f)
        vals = _bf(_rows3(vm_ref, vp_ref, vc_ref))
        qn_f, qh, qr = _rms_rows(q_ref[...].reshape(QR, HD), qgv)
        qn = _bf(qn_f)
        e, inv, p_sink = _attn_probs(qn, keys, geo)
        dob = _bf(do_ref[...].reshape(QR, HD))
        p = e * inv
        dp = _dot(dob, vals, nt=True)
        delta = (p * dp).sum(axis=-1, keepdims=True)
        dsk = _stack_rows([jnp.broadcast_to(-jnp.sum((p_sink * delta)[hh * T:(hh + 1) * T], axis=0, keepdims=True),
                                            (1, 128)) for hh in range(GQA)])
        ds = p * (dp - delta)
        dqn = _dot(_bf(ds), keys) * _SCALE
        dq_ref[...] = _rms_rows_bwd(dqn, qgv, qh, qr).reshape(GQA, T, HD)
        dqg_acc = jnp.sum(dqn * qh, axis=0, keepdims=True)
        dkn = _dot(_bf(ds.T), qn) * _SCALE
        dkg_acc = jnp.sum(dkn * kh, axis=0, keepdims=True)
        dk_all = _rms_rows_bwd(dkn, kgv, kh, kr)
        dv_all = _dot(_bf(p.T), dob)
        dk_raw = [dk_all[b * T:(b + 1) * T] for b in range(3)]
        dvv = [dv_all[b * T:(b + 1) * T] for b in range(3)]
        dkp_ref[...] = dk_raw[1]
        dkc_ref[...] = dk_raw[2]
        dvp_ref[...] = dvv[1]
        dvc_ref[...] = dvv[2]
        first = jnp.logical_and(g == 0, n == 0)

        @pl.when(n == 0)
        def _():
            dkm_ref[...] = dk_raw[0]
            dvm_ref[...] = dvv[0]
            dsk_ref[...] = dsk

        @pl.when(n > 0)
        def _():
            dkm_ref[...] += dk_raw[0]
            dvm_ref[...] += dvv[0]
            dsk_ref[...] += dsk

        @pl.when(first)
        def _():
            dqg_ref[...] = dqg_acc
            dkg_ref[...] = dkg_acc

        @pl.when(jnp.logical_not(first))
        def _():
            dqg_ref[...] += dqg_acc
            dkg_ref[...] += dkg_acc

    qspec, cur, prev, meta, vec = _attn_specs()
    kv_shape = jax.ShapeDtypeStruct((NKV, L, HD), F32)
    meta_shape = jax.ShapeDtypeStruct((NKV, T, HD), F32)
    vec_shape = jax.ShapeDtypeStruct((1, HD), F32)
    return pl.pallas_call(
        body, name=name,
        out_shape=(jax.ShapeDtypeStruct((NQ, L, HD), F32), kv_shape, kv_shape, meta_shape, kv_shape, kv_shape,
                   meta_shape, vec_shape, vec_shape, jax.ShapeDtypeStruct((NKV, GQA, 128), F32)),
        grid=(NKV, NB),
        in_specs=[pl.BlockSpec(memory_space=pltpu.SMEM), qspec, cur, prev, meta, cur, prev, meta, qspec, vec, vec],
        out_specs=(qspec, cur, cur, meta, cur, cur, meta, vec, vec,
                   pl.BlockSpec((None, GQA, 128), lambda g, n: (g, 0, 0))))(sinks, q, k, k, k, v, v, v, do, qg, kg)


RW = RH * HD


def _ret_tables():
    h = np.arange(RH, dtype=np.float64)
    lg = np.log1p(-np.exp2(-5.0 - h))
    idx = np.arange(T, dtype=np.float64)
    diff = idx[:, None] - idx[None, :]
    decay = np.where(diff[None] >= 0, np.exp(np.maximum(diff, 0.0)[None] * lg[:, None, None]), 0.0)
    zeta = np.exp((T - 1 - idx)[None, :] * lg[:, None])
    xi = np.exp((idx + 1.0)[None, :] * lg[:, None])
    cd = np.exp(T * lg)
    lanes = lambda a: np.repeat(a.T, HD, axis=1)
    head_of = np.arange(RW) // HD
    same = (head_of[:, None] == head_of[None, :]).astype(np.float64)
    f = lambda a: jnp.asarray(a, F32)
    return dict(decay=f(decay), zeta=f(lanes(zeta)), xi=f(lanes(xi)), cd=f(np.repeat(cd, HD)[None, :]),
                head=f((head_of[None, :] == np.arange(RH)[:, None]).astype(np.float64)[:, None, :]),
                same=f(same), avg=jnp.asarray(same / HD, BF16))


def _seg_mean(x, avg):
    hi = _bf(x)
    lo = _bf(x - hi.astype(F32))
    return _dot(hi, avg) + _dot(lo, avg)


def _ret_specs(col0, order):
    return lambda col: pl.BlockSpec((T, RW), lambda i: (order(i), col0 + col))


def _ret_chunk(q, kf, v, s, tb):
    dec, xi, head = tb
    vb = _bf(v)
    kb = _bf(kf)
    y = _dot(_bf(q * xi), _bf(s))
    a = []
    for h in range(RH):
        a.append(_dot(_bf(q * head[h]), kb, nt=True) * dec[h])
        y = y + head[h] * _dot(_bf(a[h]), vb)
    return a, y


def _gn_rows(y):
    mu = jnp.mean(y, axis=-1, keepdims=True)
    yc = y - mu
    rstd = lax.rsqrt(jnp.mean(yc * yc, axis=-1, keepdims=True) + LN_EPS)
    return yc * rstd, rstd


_RET_COL0 = _RET_Q = 5


def _ret_consts(tb):
    names = ("decay", "zeta", "xi", "cd", "head", "same", "avg")
    full = lambda a: pl.BlockSpec(a.shape, lambda i: (0,) * a.ndim)
    return [tb[n] for n in names], [full(tb[n]) for n in names]


def _ret_inputs(n, q_ref, k_ref, v_ref):
    rows = n * T + lax.broadcasted_iota(jnp.int32, (T, 1), 0)
    valid = rows >= PAD
    return q_ref[...], jnp.where(valid, k_ref[...] * (HD ** -0.5), 0.0), v_ref[...], valid


def ret_fwd(proj, gng, tb, name):
    def body(q_ref, k_ref, v_ref, g_ref, dec_ref, zeta_ref, xi_ref, cd_ref, head_ref, same_ref, avg_ref, gng_ref,
             y_ref, st_ref, s_scr):
        n = pl.program_id(0)

        @pl.when(n == 0)
        def _():
            s_scr[...] = jnp.zeros_like(s_scr)

        s = s_scr[...]
        st_ref[...] = s
        q, kf, v, _ = _ret_inputs(n, q_ref, k_ref, v_ref)
        _, y = _ret_chunk(q, kf, v, s, (dec_ref, xi_ref[...], head_ref))
        s_scr[...] = cd_ref[...] * s + same_ref[...] * _dot(_bf((kf * zeta_ref[...]).T), _bf(v))
        avg = avg_ref[...]
        yc = y - _seg_mean(y, avg)
        yh = yc * lax.rsqrt(_seg_mean(yc * yc, avg) + LN_EPS)
        gv = g_ref[...]
        y_ref[...] = gv * _sig(gv) * (yh * gng_ref[...])

    col = _ret_specs(_RET_COL0, lambda i: i)
    consts, cspecs = _ret_consts(tb)
    return pl.pallas_call(
        body, name=name,
        out_shape=(jax.ShapeDtypeStruct((L, RW), F32), jax.ShapeDtypeStruct((NB, RW, RW), F32)),
        grid=(NB,),
        in_specs=[col(0), col(1), col(2), col(3)] + cspecs + [pl.BlockSpec((1, RW), lambda i: (0, 0))],
        out_specs=(pl.BlockSpec((T, RW), lambda i: (i, 0)), pl.BlockSpec((None, RW, RW), lambda i: (i, 0, 0))),
        scratch_shapes=[pltpu.VMEM((RW, RW), F32)])(proj, proj, proj, proj, *consts, gng)


def ret_bwd(proj, dycat, states, gng, tb, name):
    def body(q_ref, k_ref, v_ref, g_ref, do_ref, st_ref, dec_ref, zeta_ref, xi_ref, cd_ref, head_ref, same_ref,
             avg_ref, gng_ref, d_ref, dgn_ref, ds_scr):
        i = pl.program_id(0)
        n = NB - 1 - i

        @pl.when(i == 0)
        def _():
            ds_scr[...] = jnp.zeros_like(ds_scr)

        dsn = _bf(ds_scr[...])
        s = st_ref[...]
        sb = _bf(s)
        q, kf, v, valid = _ret_inputs(n, q_ref, k_ref, v_ref)
        xi, zeta, avg = xi_ref[...], zeta_ref[...], avg_ref[...]
        a, y = _ret_chunk(q, kf, v, s, (dec_ref, xi, head_ref))
        yc = y - _seg_mean(y, avg)
        rstd = lax.rsqrt(_seg_mean(yc * yc, avg) + LN_EPS)
        yh = yc * rstd
        gv = g_ref[...]
        sg = _sig(gv)
        sil = gv * sg
        gn = gng_ref[...]
        dout = do_ref[...]
        d_ref[:, 3 * RW:4 * RW] = dout * (yh * gn) * (sg * (1.0 + gv * (1.0 - sg)))
        dyh = dout * sil * gn
        part = jnp.sum(dout * sil * yh, axis=0, keepdims=True)

        @pl.when(i == 0)
        def _():
            dgn_ref[...] = part

        @pl.when(i > 0)
        def _():
            dgn_ref[...] += part

        dy = rstd * (dyh - _seg_mean(dyh, avg) - yh * _seg_mean(dyh * yh, avg))
        dyb = _bf(dy)
        vb, kb, qb = _bf(v), _bf(kf), _bf(q)
        dq = _dot(dyb, sb, nt=True) * xi
        dkf = _dot(vb, dsn, nt=True) * zeta
        dv = _dot(_bf(kf * zeta), dsn)
        for h in range(RH):
            m = head_ref[h]
            da = _dot(_bf(dy * m), vb, nt=True) * dec_ref[h]
            dv = dv + m * _dot(_bf(a[h].T), dyb)
            dq = dq + m * _dot(_bf(da), kb)
            dkf = dkf + m * _dot(_bf(da.T), qb)
        d_ref[:, 0:RW] = dq
        d_ref[:, RW:2 * RW] = jnp.where(valid, dkf * (HD ** -0.5), 0.0)
        d_ref[:, 2 * RW:3 * RW] = dv
        ds_scr[...] = cd_ref[...] * ds_scr[...] + same_ref[...] * _dot(_bf((q * xi).T), dyb)

    back = lambda i: NB - 1 - i
    col = _ret_specs(_RET_COL0, back)
    consts, cspecs = _ret_consts(tb)
    vec = pl.BlockSpec((1, RW), lambda i: (0, 0))
    return pl.pallas_call(
        body, name=name,
        out_shape=(jax.ShapeDtypeStruct((L, 4 * RW), F32), jax.ShapeDtypeStruct((1, RW), F32)),
        grid=(NB,),
        in_specs=[col(0), col(1), col(2), col(3), _ret_specs(3, back)(0),
                  pl.BlockSpec((None, RW, RW), lambda i: (back(i), 0, 0))] + cspecs + [vec],
        out_specs=(pl.BlockSpec((T, 4 * RW), lambda i: (back(i), 0)), vec),
        scratch_shapes=[pltpu.VMEM((RW, RW), F32)])(proj, proj, proj, proj, dycat, states, *consts, gng)


HALO = 32
TAP0 = HALO - (CONV_K - 1)


def _conv_specs():
    cur = lambda col=0: pl.BlockSpec((T, CV), lambda c: (c, col))
    before = lambda col=0: pl.BlockSpec((HALO, CV), lambda c: (jnp.maximum(c * (T // HALO) - 1, 0), col))
    after = pl.BlockSpec((HALO, CV), lambda c: (jnp.minimum((c + 1) * (T // HALO), L // HALO - 1), 0))
    full = lambda r, w: pl.BlockSpec((r, w), lambda c: (0, 0))
    return cur, before, after, full


_CONV_A, _CONV_B = 3, 4
_DY_CONV = 2


def _conv_tile_fwd(c, ca_ref, cah_ref, cb_ref, cbh_ref, w_ref, b_ref, lg_ref, lb_ref, u_scr):
    u_scr[0:HALO, :] = jnp.where(c > 0, cah_ref[...] * _sig(cbh_ref[...]), 0.0)
    u_scr[HALO:HALO + T, :] = ca_ref[...] * _sig(cb_ref[...])
    acc = jnp.zeros((T, CV), F32)
    for k in range(CONV_K):
        acc = acc + w_ref[k:k + 1, :] * u_scr[TAP0 + k:TAP0 + k + T, :]
    xc = acc + b_ref[...]
    xh, rstd = _gn_rows(xc)
    z = xh * lg_ref[...] + lb_ref[...]
    return xh, rstd, z, _sig(z)


def conv_fwd(proj, w, b, lg, lb, pw, og, name):
    def body(ca_ref, cah_ref, cb_ref, cbh_ref, w_ref, b_ref, lg_ref, lb_ref, pw_ref, og_ref, y_ref, u_scr):
        c = pl.program_id(0)
        _, _, z, sg = _conv_tile_fwd(c, ca_ref, cah_ref, cb_ref, cbh_ref, w_ref, b_ref, lg_ref, lb_ref, u_scr)
        zp = _dot(_bf(z * sg), pw_ref[...])
        y_ref[...] = _rms_rows(zp, og_ref[...])[0]

    cur, before, _, full = _conv_specs()
    vec = full(1, CV)
    return pl.pallas_call(
        body, name=name, out_shape=jax.ShapeDtypeStruct((L, CV), F32), grid=(NB,),
        in_specs=[cur(_CONV_A), before(_CONV_A), cur(_CONV_B), before(_CONV_B), full(CONV_K, CV), vec, vec, vec,
                  full(CV, CV), vec],
        out_specs=cur(),
        scratch_shapes=[pltpu.VMEM((HALO + T, CV), F32)])(proj, proj, proj, proj, w, b, lg, lb, pw, og)


def conv_bwd(proj, dycat, w, b, lg, lb, pw, og, name):
    def body1(ca_ref, cah_ref, cb_ref, cbh_ref, dy_ref, w_ref, b_ref, lg_ref, lb_ref, pw_ref, og_ref,
              dxc_ref, dw_ref, db_ref, dlg_ref, dlb_ref, dog_ref, dpw_ref, u_scr):
        c = pl.program_id(0)

        @pl.when(c == 0)
        def _():
            for r in (dw_ref, db_ref, dlg_ref, dlb_ref, dog_ref, dpw_ref):
                r[...] = jnp.zeros_like(r)

        xh, rstd, z, sg = _conv_tile_fwd(c, ca_ref, cah_ref, cb_ref, cbh_ref, w_ref, b_ref, lg_ref, lb_ref, u_scr)
        s = z * sg
        zp = _dot(_bf(s), pw_ref[...])
        ogv = og_ref[...]
        _, zph, r2 = _rms_rows(zp, ogv)
        dyv = dy_ref[...]
        dog_ref[...] += jnp.sum(dyv * zph, axis=0, keepdims=True)
        dzpb = _bf(_rms_rows_bwd(dyv, ogv, zph, r2))
        dpw_ref[...] += _dot(_bf(s.T), dzpb)
        dz = _dot(dzpb, pw_ref[...], nt=True) * (sg * (1.0 + z * (1.0 - sg)))
        dlg_ref[...] += jnp.sum(dz * xh, axis=0, keepdims=True)
        dlb_ref[...] += jnp.sum(dz, axis=0, keepdims=True)
        dxh = dz * lg_ref[...]
        dxc = rstd * (dxh - jnp.mean(dxh, axis=-1, keepdims=True) - xh * jnp.mean(dxh * xh, axis=-1, keepdims=True))
        db_ref[...] += jnp.sum(dxc, axis=0, keepdims=True)
        dxc_ref[...] = dxc
        for k in range(CONV_K):
            dw_ref[k:k + 1, :] += jnp.sum(dxc * u_scr[TAP0 + k:TAP0 + k + T, :], axis=0, keepdims=True)

    def body2(dx_ref, dxa_ref, ca_ref, cb_ref, w_ref, dca_ref, dcb_ref, d_scr):
        c = pl.program_id(0)
        d_scr[0:T, :] = dx_ref[...]
        d_scr[T:T + HALO, :] = jnp.where(c < NB - 1, dxa_ref[...], 0.0)
        du = jnp.zeros((T, CV), F32)
        for k in range(CONV_K):
            off = CONV_K - 1 - k
            du = du + w_ref[k:k + 1, :] * d_scr[off:off + T, :]
        sg = _sig(cb_ref[...])
        dca_ref[...] = du * sg
        dcb_ref[...] = du * ca_ref[...] * sg * (1.0 - sg)

    cur, before, after, full = _conv_specs()
    seq = jax.ShapeDtypeStruct((L, CV), F32)
    vsh = jax.ShapeDtypeStruct((1, CV), F32)
    vec = full(1, CV)
    dxc, dw, db, dlg, dlb, dog, dpw = pl.pallas_call(
        body1, name=name + "_a",
        out_shape=(seq, jax.ShapeDtypeStruct((CONV_K, CV), F32), vsh, vsh, vsh, vsh,
                   jax.ShapeDtypeStruct((CV, CV), F32)),
        grid=(NB,),
        in_specs=[cur(_CONV_A), before(_CONV_A), cur(_CONV_B), before(_CONV_B), cur(_DY_CONV), full(CONV_K, CV), vec,
                  vec, vec, full(CV, CV), vec],
        out_specs=(cur(), full(CONV_K, CV), vec, vec, vec, vec, full(CV, CV)),
        scratch_shapes=[pltpu.VMEM((HALO + T, CV), F32)])(proj, proj, proj, proj, dycat, w, b, lg, lb, pw, og)
    dca, dcb = pl.pallas_call(
        body2, name=name + "_b", out_shape=(seq, seq), grid=(NB,),
        in_specs=[cur(), after, cur(_CONV_A), cur(_CONV_B), full(CONV_K, CV)], out_specs=(cur(), cur()),
        scratch_shapes=[pltpu.VMEM((T + HALO, CV), F32)])(dxc, dxc, proj, proj, w)
    return dca, dcb, dw, db, dlg, dlb, dog, dpw


HR = 8


def _ffn_specs():
    cur = lambda off: pl.BlockSpec((None, T, UPW), lambda j, c: (j + off, c, 0))
    before = lambda off: pl.BlockSpec((None, HR, UPW), lambda j, c: (j + off, jnp.maximum(c * (T // HR) - 1, 0), 0))
    after = lambda off: pl.BlockSpec(
        (None, HR, UPW), lambda j, c: (j + off, jnp.minimum((c + 1) * (T // HR), L // HR - 1), 0))
    wspec = lambda off, r: pl.BlockSpec((None, r, UPW), lambda j, c: (j + off, 0, 0))
    return cur, before, after, wspec


def ffn_act_fwd(f, w, b, name):
    strip = 16

    def body(fg_ref, fgh_ref, fu_ref, fuh_ref, wg_ref, wu_ref, bg_ref, bu_ref, a_ref, gc_ref, uc_ref, hg_scr, hu_scr):
        c = pl.program_id(1)
        for scr, x_ref, xh_ref in ((hg_scr, fg_ref, fgh_ref), (hu_scr, fu_ref, fuh_ref)):
            scr[0:HR, :] = jnp.where(c > 0, xh_ref[...], 0.0)
            scr[HR:HR + strip, :] = x_ref[0:strip, :]
        wg = [wg_ref[k:k + 1, :] for k in range(FFN_K)]
        wu = [wu_ref[k:k + 1, :] for k in range(FFN_K)]
        bg, bu = bg_ref[...], bu_ref[...]

        def conv(src, base, w, b):
            acc = b
            for k in range(FFN_K):
                o = base - (FFN_K - 1) + k
                acc = acc + w[k] * src[o:o + strip, :]
            return acc

        for s in range(T // strip):
            r = s * strip
            gate = conv(hg_scr, HR, wg, bg) if s == 0 else conv(fg_ref, r, wg, bg)
            up = conv(hu_scr, HR, wu, bu) if s == 0 else conv(fu_ref, r, wu, bu)
            gc_ref[r:r + strip, :] = gate
            uc_ref[r:r + strip, :] = up
            a_ref[r:r + strip, :] = (gate * _sig(gate) * up).astype(BF16)

    cur, before, _, wspec = _ffn_specs()
    pre = jax.ShapeDtypeStruct((NPAIR, L, UPW), F32)
    return pl.pallas_call(
        body, name=name, out_shape=(jax.ShapeDtypeStruct((NPAIR, L, UPW), BF16), pre, pre), grid=(NPAIR, NB),
        in_specs=[cur(0), before(0), cur(NPAIR), before(NPAIR), wspec(0, FFN_K), wspec(NPAIR, FFN_K),
                  wspec(0, 1), wspec(NPAIR, 1)],
        out_specs=(cur(0), cur(0), cur(0)),
        scratch_shapes=[pltpu.VMEM((HR + strip, UPW), F32), pltpu.VMEM((HR + strip, UPW), F32)])(
            f, f, f, f, w, w, b, b)


def ffn_act_bwd(f, gc, uc, da, w, name):
    ext = T + HR
    sub = 8

    def body(fg_ref, fu_ref, gc_ref, gca_ref, uc_ref, uca_ref, da_ref, daa_ref, wg_ref, wu_ref,
             dfg_ref, dfu_ref, dwg_ref, dwu_ref, dbg_ref, dbu_ref, dg_scr, du_scr):
        c = pl.program_id(1)
        wg = [wg_ref[k:k + 1, :] for k in range(FFN_K)]
        wu = [wu_ref[k:k + 1, :] for k in range(FFN_K)]

        for s in range(ext // sub):
            r = s * sub
            if r < T:
                gate, up, dav = gc_ref[r:r + sub, :], uc_ref[r:r + sub, :], da_ref[r:r + sub, :]
            else:
                gate, up, dav = gca_ref[...], uca_ref[...], jnp.where(c < NB - 1, daa_ref[...], 0.0)
            sg = _sig(gate)
            dg_scr[r:r + sub, :] = dav * up * (sg * (1.0 + gate * (1.0 - sg)))
            du_scr[r:r + sub, :] = dav * gate * sg

        def back(d_scr, x_ref, w, df_ref, dw_ref, db_ref):
            accs = [jnp.zeros((sub, UPW), F32) for _ in range(FFN_K + 1)]
            for s in range(T // (2 * sub)):
                pieces = []
                for r in (2 * s * sub, (2 * s + 1) * sub):
                    xv = x_ref[r:r + sub, :]
                    df = jnp.zeros((sub, UPW), F32)
                    for k in range(FFN_K):
                        off = FFN_K - 1 - k
                        dj = d_scr[r + off:r + off + sub, :]
                        df = df + w[k] * dj
                        accs[k] = accs[k] + xv * dj
                        if off == 0:
                            accs[FFN_K] = accs[FFN_K] + dj
                    pieces.append(df)
                df_ref[2 * s * sub:2 * (s + 1) * sub, :] = jnp.concatenate(pieces, axis=0).astype(BF16)
            dwp = _stack_rows([jnp.sum(a, axis=0, keepdims=True) for a in accs[:FFN_K]])
            dbp = jnp.sum(accs[FFN_K], axis=0, keepdims=True)

            @pl.when(c == 0)
            def _():
                dw_ref[...] = dwp
                db_ref[...] = dbp

            @pl.when(c > 0)
            def _():
                dw_ref[...] += dwp
                db_ref[...] += dbp

        back(dg_scr, fg_ref, wg, dfg_ref, dwg_ref, dbg_ref)
        back(du_scr, fu_ref, wu, dfu_ref, dwu_ref, dbu_ref)

    cur, _, after, wspec = _ffn_specs()
    slab = jax.ShapeDtypeStruct((NPAIR, L, UPW), BF16)
    wsh = jax.ShapeDtypeStruct((NPAIR, FFN_K, UPW), F32)
    bsh = jax.ShapeDtypeStruct((NPAIR, 1, UPW), F32)
    return pl.pallas_call(
        body, name=name, out_shape=(slab, slab, wsh, wsh, bsh, bsh), grid=(NPAIR, NB),
        in_specs=[cur(0), cur(NPAIR), cur(0), after(0), cur(0), after(0), cur(0), after(0),
                  wspec(0, FFN_K), wspec(NPAIR, FFN_K)],
        out_specs=(cur(0), cur(0), wspec(0, FFN_K), wspec(0, FFN_K), wspec(0, 1), wspec(0, 1)),
        scratch_shapes=[pltpu.VMEM((ext, UPW), F32), pltpu.VMEM((ext, UPW), F32)])(
            f, f, gc, gc, uc, uc, da, da, w, w)


def _mesh_pos():
    return lax.axis_index("x"), lax.axis_index("y"), lax.axis_index("c")


def _peer(pos, k):
    x, y, c = pos
    px = 1 - x if k & 4 else x
    py = 1 - y if k & 2 else y
    pc = 1 - c if k & 1 else c
    return (px, py, pc), 4 * px + 2 * py + pc


_CHIP_FLIPS = (4, 2, 6)
_HBM = pl.BlockSpec(memory_space=pl.ANY)


def all_gather(shards, name):
    nt = len(shards)

    def body(*refs):
        ins, outs = refs[:nt], refs[nt:2 * nt]
        send, recv, local = refs[2 * nt:]
        pos = _mesh_pos()
        me = 4 * pos[0] + 2 * pos[1] + pos[2]
        sib, sib_id = _peer(pos, 1)

        def copy(t, k, block_id, to, src=None):
            dst = outs[t].at[block_id]
            return pltpu.make_async_remote_copy(
                src_ref=dst if src is None else src, dst_ref=dst, send_sem=send.at[t, k], recv_sem=recv.at[t, k],
                device_id=to, device_id_type=MESH)

        locals_ = [pltpu.make_async_copy(ins[t], outs[t].at[me], local.at[t]) for t in range(nt)]
        for cp in locals_:
            cp.start()
        started = []
        for j, flip in enumerate(_CHIP_FLIPS):
            for t in range(nt):
                started.append(copy(t, 1 + j, me, _peer(pos, flip)[0], src=ins[t]))
        for t in range(nt):
            started.append(copy(t, 0, me, sib, src=ins[t]))
        for cp in started:
            cp.start()
        for j, flip in enumerate(_CHIP_FLIPS):
            _, pid = _peer(pos, flip)
            for t in range(nt):
                copy(t, 1 + j, pid, sib).wait_recv()
                fwd = copy(t, 4 + j, pid, sib)
                fwd.start()
                started.append(fwd)
        for t in range(nt):
            copy(t, 0, sib_id, sib).wait_recv()
        for j, flip in enumerate(_CHIP_FLIPS):
            _, pid = _peer(pos, flip | 1)
            for t in range(nt):
                copy(t, 4 + j, pid, sib).wait_recv()
        for cp in started:
            cp.wait_send()
        for cp in locals_:
            cp.wait()

    return pl.pallas_call(
        body, name=name,
        out_shape=tuple(jax.ShapeDtypeStruct((NDEV,) + s.shape, s.dtype) for s in shards),
        in_specs=[_HBM] * nt, out_specs=tuple([_HBM] * nt),
        scratch_shapes=[pltpu.SemaphoreType.DMA((nt, NDEV - 1)), pltpu.SemaphoreType.DMA((nt, NDEV - 1)),
                        pltpu.SemaphoreType.DMA((nt,))])(*shards)


_SPLIT = dict(has_side_effects=pltpu.SideEffectType.DATAFLOW_SIDE_EFFECTING)
_SEM = pl.BlockSpec(memory_space=pltpu.SEMAPHORE)


def _exchange_start(name, srcs, lands, copies, after):
    ns, nl = len(srcs), len(lands)
    ncopy = len(copies(None, [None] * ns, [None] * nl))

    def body(*refs):
        src_refs, land_refs = refs[:ns], refs[ns:ns + nl]
        send, recv = refs[ns + nl + len(after):ns + nl + len(after) + 2]
        token = refs[-1]
        for i, (src, dst, peer) in enumerate(copies(_mesh_pos(), src_refs, land_refs)):
            pltpu.make_async_remote_copy(src_ref=src, dst_ref=dst, send_sem=send.at[i], recv_sem=recv.at[i],
                                         device_id=peer, device_id_type=MESH).start()
        token[...] = jnp.zeros_like(token)

    hbm = lambda a: pltpu.HBM(a.shape, a.dtype)
    out = pl.pallas_call(
        body, name=name,
        out_shape=(pltpu.SemaphoreType.DMA((ncopy,)), pltpu.SemaphoreType.DMA((ncopy,)),
                   *[hbm(a) for a in srcs], *[hbm(a) for a in lands], jax.ShapeDtypeStruct((8, 128), F32)),
        in_specs=[pl.BlockSpec(memory_space=pltpu.HBM)] * (ns + nl) + [_HBM] * len(after),
        out_specs=(_SEM, _SEM, *[pl.BlockSpec(memory_space=pltpu.HBM)] * (ns + nl),
                   pl.BlockSpec(memory_space=pltpu.VMEM)),
        input_output_aliases={i: 2 + i for i in range(ns + nl)},
        compiler_params=pltpu.CompilerParams(**_SPLIT))(
            *[pltpu.with_memory_space_constraint(a, pltpu.HBM) for a in list(srcs) + list(lands)], *after)
    return out[0], out[1], list(out[2:2 + ns]), list(out[2 + ns:2 + ns + nl]), out[-1]


def _exchange_wait(name, send, recv, srcs, lands, copies, after):
    ns, nl = len(srcs), len(lands)

    def body(*refs):
        src_refs, land_refs = refs[:ns], refs[ns:ns + nl]
        send_ref, recv_ref = refs[ns + nl:ns + nl + 2]
        pos = _mesh_pos()
        for i, (src, dst, peer) in enumerate(copies(pos, src_refs, land_refs, arriving=True)):
            cp = pltpu.make_async_remote_copy(src_ref=src, dst_ref=dst, send_sem=send_ref.at[i], recv_sem=recv_ref.at[i],
                                              device_id=peer, device_id_type=MESH)
            cp.wait_send()
            cp.wait_recv()

    hbm = lambda a: pltpu.HBM(a.shape, a.dtype)
    out = pl.pallas_call(
        body, name=name, out_shape=tuple(hbm(a) for a in list(srcs) + list(lands)),
        in_specs=[pl.BlockSpec(memory_space=pltpu.HBM)] * (ns + nl) + [_SEM, _SEM] + [_HBM] * len(after),
        out_specs=tuple([pl.BlockSpec(memory_space=pltpu.HBM)] * (ns + nl)),
        input_output_aliases={i: i for i in range(ns + nl)},
        compiler_params=pltpu.CompilerParams(**_SPLIT))(*srcs, *lands, send, recv, *after)
    return list(out[:ns]), list(out[ns:])


def _gather_copies(pos, srcs, lands, arriving=False):
    if pos is None:
        return [None] * (len(srcs) * (NDEV - 1))
    me = 4 * pos[0] + 2 * pos[1] + pos[2]
    out = []
    for src, land in zip(srcs, lands):
        for k in range(1, NDEV):
            peer, pid = _peer(pos, k)
            out.append((src, land.at[pid if arriving else me], peer))
    return out


def _scatter_copies(pos, srcs, lands, arriving=False):
    if pos is None:
        return [None] * (len(srcs) * (NDEV - 1))
    out = []
    for src, land in zip(srcs, lands):
        for k in range(1, NDEV):
            peer, pid = _peer(pos, k)
            out.append((src.at[pid], land.at[k - 1], peer))
    return out


def gather_start(name, shards, lands, after=()):
    return _exchange_start(name, shards, lands, _gather_copies, list(after))


def gather_wait(name, handle, after=()):
    send, recv, srcs, lands, _ = handle
    return _exchange_wait(name, send, recv, srcs, lands, _gather_copies, list(after))


def scatter_start(name, grads, after=()):
    lands = [pltpu.with_memory_space_constraint(lax.empty((NDEV - 1,) + g.shape[1:], g.dtype), pltpu.HBM)
             for g in grads]
    return _exchange_start(name, grads, lands, _scatter_copies, list(after))


def scatter_wait(name, handle, after=()):
    send, recv, srcs, lands, _ = handle
    return _exchange_wait(name, send, recv, srcs, lands, _scatter_copies, list(after))


def all_reduce_small(part, name):
    rows = part.shape[0]

    def body(x_ref, o_ref, buf, send, recv):
        pos = _mesh_pos()
        me = 4 * pos[0] + 2 * pos[1] + pos[2]
        sends = []
        for k in range(1, NDEV):
            peer, _ = _peer(pos, k)
            sends.append(pltpu.make_async_remote_copy(
                src_ref=x_ref, dst_ref=buf.at[me], send_sem=send.at[k - 1], recv_sem=recv.at[k - 1],
                device_id=peer, device_id_type=MESH))
        for cp in sends:
            cp.start()
        buf[me] = x_ref[...]
        for k in range(1, NDEV):
            peer, pid = _peer(pos, k)
            pltpu.make_async_remote_copy(
                src_ref=x_ref, dst_ref=buf.at[pid], send_sem=send.at[k - 1], recv_sem=recv.at[k - 1],
                device_id=peer, device_id_type=MESH).wait_recv()
        for cp in sends:
            cp.wait_send()
        acc = buf[0]
        for d in range(1, NDEV):
            acc = acc + buf[d]
        o_ref[...] = acc

    return pl.pallas_call(
        body, name=name, out_shape=jax.ShapeDtypeStruct((rows, 128), F32),
        in_specs=[pl.BlockSpec(memory_space=pltpu.VMEM)], out_specs=pl.BlockSpec(memory_space=pltpu.VMEM),
        scratch_shapes=[pltpu.VMEM((NDEV, rows, 128), F32), pltpu.SemaphoreType.DMA((NDEV - 1,)),
                        pltpu.SemaphoreType.DMA((NDEV - 1,))])(part)


def _adamw_update(g, w_ref, m_ref, v_ref, g_ref, d_ref, nm_ref, nv_ref):
    g_ref[...] = g
    nm = ADAM_B1 * m_ref[...] + (1.0 - ADAM_B1) * g
    nv = ADAM_B2 * v_ref[...] + (1.0 - ADAM_B2) * (g * g)
    nm_ref[...] = nm
    nv_ref[...] = nv
    m_hat = nm / (1.0 - ADAM_B1 ** ADAM_STEP)
    v_hat = nv / (1.0 - ADAM_B2 ** ADAM_STEP)
    d_ref[...] = -ADAM_LR * (m_hat / (jnp.sqrt(v_hat) + ADAM_EPS) + ADAM_WD * w_ref[...])


def adamw_big(own, lands, w, m, v, me, tr, name):
    _, r, c = own[0].shape
    nt = r // tr

    def body(me_ref, *refs):
        ins, (w_ref, m_ref, v_ref), outs = refs[:2 * DEPTH], refs[2 * DEPTH:2 * DEPTH + 3], refs[2 * DEPTH + 3:]
        layer = pl.program_id(0)
        for l in range(DEPTH):
            @pl.when(layer == l)
            def _(l=l):
                g = ins[2 * l][...].astype(F32)
                for s in range(NDEV - 1):
                    g = g + ins[2 * l + 1][s].astype(F32)
                _adamw_update(g, w_ref, m_ref, v_ref, *outs)

    pick = lambda l: (lambda layer, i, me_ref: jnp.where(layer == l, i, 0))
    in_specs = []
    for l in range(DEPTH):
        in_specs.append(pl.BlockSpec((None, tr, c), lambda layer, i, me_ref, f=pick(l): (me_ref[0], f(layer, i, me_ref), 0)))
        in_specs.append(pl.BlockSpec((NDEV - 1, tr, c), lambda layer, i, me_ref, f=pick(l): (0, f(layer, i, me_ref), 0)))
    blk = pl.BlockSpec((tr, c), lambda layer, i, me_ref: (layer * nt + i, 0))
    sh = jax.ShapeDtypeStruct((DEPTH * r, c), F32)
    args = [a for l in range(DEPTH) for a in (own[l], lands[l])]
    return pl.pallas_call(
        body, name=name, out_shape=(sh, sh, sh, sh),
        grid_spec=pltpu.PrefetchScalarGridSpec(
            num_scalar_prefetch=1, grid=(DEPTH, nt), in_specs=in_specs + [blk, blk, blk],
            out_specs=(blk, blk, blk, blk)),
        compiler_params=_params(VMEM_BIG))(me, *args, w, m, v)


def adamw(parts, w, m, v, tr, name):
    ns, r, c = parts.shape

    def body(p_ref, w_ref, m_ref, v_ref, g_ref, d_ref, nm_ref, nv_ref):
        g = p_ref[0].astype(F32)
        for s in range(1, ns):
            g = g + p_ref[s].astype(F32)
        _adamw_update(g, w_ref, m_ref, v_ref, g_ref, d_ref, nm_ref, nv_ref)

    blk = pl.BlockSpec((tr, c), lambda i: (i, 0))
    sh = jax.ShapeDtypeStruct((r, c), F32)
    return pl.pallas_call(
        body, name=name, out_shape=(sh, sh, sh, sh), grid=(r // tr,),
        in_specs=[pl.BlockSpec((ns, tr, c), lambda i: (0, i, 0)), blk, blk, blk],
        out_specs=(blk, blk, blk, blk), compiler_params=_params(VMEM_BIG))(parts, w, m, v)


_SPLITS = (NQ * HD, NKV * HD, NKV * HD, CV, CV, RH * HD, RH * HD, RH * HD, RH * HD)
_OFFS = np.concatenate([[0], np.cumsum(_SPLITS)]).tolist()


def _heads(x, nh):
    return x.reshape(L, nh, HD).transpose(1, 0, 2)


def _unheads(x):
    return x.transpose(1, 0, 2).reshape(L, x.shape[0] * HD)


def local_step(x, target, P, hooks):
    tb = _ret_tables()
    h = jnp.concatenate([jnp.zeros((PAD, D), F32), P["meta"], x], axis=0)
    stash = []
    for l in range(DEPTH):
        s = {"h": h}
        s["w_in"], s["cv_pw"], s["w_out"] = hooks["mixer_w"](l, h)
        u = rms_fwd(h, P["norm_mix_g"][l], BF16, f"rms_mix_{l}", after=hooks["first_after"] if l == 0 else None)
        s["ut"] = u.T
        proj = s["proj"] = mm_nn(f"in_proj_{l}", u, s["w_in"], 768)
        s.update(q=_heads(proj[:, :_OFFS[1]], NQ), k=_heads(proj[:, _OFFS[1]:_OFFS[2]], NKV),
                 v=_heads(proj[:, _OFFS[2]:_OFFS[3]], NKV))
        o = attn_fwd(s["q"], s["k"], s["v"], P["q_norm_g"][l], P["k_norm_g"][l], P["attn_sinks"][l], f"attn_fwd_{l}")
        s["o"] = _unheads(o)
        y_att = rms_fwd(s["o"], P["attn_out_g"][l], BF16, f"rms_att_{l}")
        y_cv = conv_fwd(proj, P["cv_dw_w"][l], P["cv_dw_b"][l], P["cv_ln_g"][l], P["cv_ln_b"][l], s["cv_pw"],
                        P["cv_out_g"][l], f"conv_fwd_{l}")
        y_ret, s["states"] = ret_fwd(proj, P["ret_gn_g"][l], tb, f"ret_fwd_{l}")
        ycat = jnp.concatenate([y_att, y_cv.astype(BF16), y_ret.astype(BF16)], axis=1)
        s["ycat_t"] = ycat.T
        h = mm_nn(f"out_proj_{l}", ycat, s["w_out"], 512, res=h)
        s["h1"] = h
        s["ffn_up"], s["ffn_down"] = hooks["ffn_w"](l, h)
        u2 = rms_fwd(h, P["norm_ffn_g"][l], BF16, f"rms_ffn_{l}")
        s["u2t"] = u2.T
        s["f"] = ffn_up_fwd(f"ffn_up_{l}", u2, s["ffn_up"])
        a, s["gc"], s["uc"] = ffn_act_fwd(s["f"], P["ffn_dw_w"][l], P["ffn_dw_b"][l], f"ffn_act_fwd_{l}")
        s["a"] = a
        h = ffn_down_fwd(f"ffn_down_{l}", a, s["ffn_down"], h)
        stash.append(s)

    dh, dhm, loss = loss_head(h, target)
    G = {n: [None] * DEPTH for n in ("norm_mix_g", "q_norm_g", "k_norm_g", "attn_sinks", "attn_out_g", "cv_dw_w",
                                     "cv_dw_b", "cv_ln_g", "cv_ln_b", "cv_out_g", "ret_gn_g", "norm_ffn_g",
                                     "ffn_dw_w", "ffn_dw_b")}
    for l in reversed(range(DEPTH)):
        s = stash[l]
        da = ffn_down_dx(f"ffn_down_dx_{l}", dhm, s["ffn_down"])
        dw_down = ffn_down_dw(f"ffn_down_dw_{l}", s["a"], dhm)
        dfg, dfu, dwg, dwu, dbg, dbu = ffn_act_bwd(s["f"], s["gc"], s["uc"], da, P["ffn_dw_w"][l], f"ffn_act_bwd_{l}")
        G["ffn_dw_w"][l] = jnp.concatenate([dwg, dwu], axis=0)
        G["ffn_dw_b"][l] = jnp.concatenate([dbg, dbu], axis=0)
        du2 = ffn_up_dx(f"ffn_up_dx_{l}", dfg, dfu, s["ffn_up"])
        dw_up = ffn_up_dw(f"ffn_up_dw_{l}", s["u2t"], dfg, dfu)
        after = hooks["ffn_grads"](l, dw_down, dw_up)
        dh, dhm, G["norm_ffn_g"][l] = rms_bwd(s["h1"], P["norm_ffn_g"][l], du2, dh, f"rms_ffn_bwd_{l}", after=after)
        dycat = mm_nt(f"out_proj_dx_{l}", dhm, s["w_out"], 512)
        dw_out = mm_nn(f"out_proj_dw_{l}", s["ycat_t"], dhm, 512, out_dtype=BF16)
        do, _, G["attn_out_g"][l] = rms_bwd(s["o"], P["attn_out_g"][l], dycat, None, f"rms_att_bwd_{l}")
        (dq, dk_c, dk_p, dk_m, dv_c, dv_p, dv_m, G["q_norm_g"][l], G["k_norm_g"][l], dsk) = attn_bwd(
            s["q"], s["k"], s["v"], _heads(do, NQ), P["q_norm_g"][l], P["k_norm_g"][l], P["attn_sinks"][l],
            f"attn_bwd_{l}")
        G["attn_sinks"][l] = dsk[:, :, 0].reshape(NQ)
        shift = lambda z: jnp.concatenate([z[:, T:], jnp.zeros((NKV, T, HD), F32)], axis=1)
        dk = (dk_c + shift(dk_p)).at[:, :T].add(dk_m)
        dv = (dv_c + shift(dv_p)).at[:, :T].add(dv_m)
        (dca, dcb, G["cv_dw_w"][l], G["cv_dw_b"][l], G["cv_ln_g"][l], G["cv_ln_b"][l], G["cv_out_g"][l],
         dpw) = conv_bwd(s["proj"], dycat, P["cv_dw_w"][l], P["cv_dw_b"][l], P["cv_ln_g"][l],
                         P["cv_ln_b"][l], s["cv_pw"], P["cv_out_g"][l], f"conv_bwd_{l}")
        dret, G["ret_gn_g"][l] = ret_bwd(s["proj"], dycat, s["states"], P["ret_gn_g"][l], tb, f"ret_bwd_{l}")
        dproj = jnp.concatenate([_unheads(dq), _unheads(dk), _unheads(dv), dca, dcb, dret], axis=1).astype(BF16)
        du = mm_nt(f"in_proj_dx_{l}", dproj, s["w_in"], 512)
        dw_in = mm_nn(f"in_proj_dw_{l}", s["ut"], dproj, 768, out_dtype=BF16)
        after = hooks["mixer_grads"](l, dw_out, dpw, dw_in)
        dh, dhm, G["norm_mix_g"][l] = rms_bwd(s["h"], P["norm_mix_g"][l], du, dh, f"rms_mix_bwd_{l}", after=after)
    return loss[0, 0], dh[T:], dh[PAD:T], G


_SMALL = ("meta", "norm_mix_g", "q_norm_g", "k_norm_g", "attn_sinks", "attn_out_g", "cv_dw_w", "cv_dw_b", "cv_ln_g",
          "cv_ln_b", "cv_out_g", "ret_gn_g", "norm_ffn_g", "ffn_dw_w", "ffn_dw_b")
_BIG = ("w_in", "cv_pw", "w_out", "ffn_up", "ffn_down")
_ORDER = ("meta", "norm_mix_g", "w_in", "q_norm_g", "k_norm_g", "attn_sinks", "attn_out_g", "cv_dw_w", "cv_dw_b",
          "cv_ln_g", "cv_ln_b", "cv_pw", "cv_out_g", "ret_gn_g", "w_out", "norm_ffn_g", "ffn_up", "ffn_dw_w",
          "ffn_dw_b", "ffn_down")
_SMALL_SHARDED = {"meta": D, "cv_dw_w": CV, "ffn_dw_w": 2 * D_FF}


def _pack(arrs):
    flat = jnp.concatenate([a.reshape(-1) for a in arrs])
    n = flat.shape[0]
    rows = -(-n // 1024) * 8
    return jnp.pad(flat, (0, rows * 128 - n)).reshape(rows, 128)


def _unpack(packed, shapes):
    flat = packed.reshape(-1)
    out, off = [], 0
    for s in shapes:
        n = int(np.prod(s))
        out.append(flat[off:off + n].reshape(s))
        off += n
    return out


def kernel(x, meta, norm_mix_g, w_in, q_norm_g, k_norm_g, attn_sinks, attn_out_g, cv_dw_w, cv_dw_b, cv_ln_g, cv_ln_b, cv_pw, cv_out_g, ret_gn_g, w_out, norm_ffn_g, ffn_up, ffn_dw_w, ffn_dw_b, ffn_down, loss_target, m_meta, m_norm_mix_g, m_w_in, m_q_norm_g, m_k_norm_g, m_attn_sinks, m_attn_out_g, m_cv_dw_w, m_cv_dw_b, m_cv_ln_g, m_cv_ln_b, m_cv_pw, m_cv_out_g, m_ret_gn_g, m_w_out, m_norm_ffn_g, m_ffn_up, m_ffn_dw_w, m_ffn_dw_b, m_ffn_down, v_meta, v_norm_mix_g, v_w_in, v_q_norm_g, v_k_norm_g, v_attn_sinks, v_attn_out_g, v_cv_dw_w, v_cv_dw_b, v_cv_ln_g, v_cv_ln_b, v_cv_pw, v_cv_out_g, v_ret_gn_g, v_w_out, v_norm_ffn_g, v_ffn_up, v_ffn_dw_w, v_ffn_dw_b, v_ffn_down):
    W = dict(meta=meta, norm_mix_g=norm_mix_g, w_in=w_in, q_norm_g=q_norm_g, k_norm_g=k_norm_g,
             attn_sinks=attn_sinks, attn_out_g=attn_out_g, cv_dw_w=cv_dw_w, cv_dw_b=cv_dw_b, cv_ln_g=cv_ln_g,
             cv_ln_b=cv_ln_b, cv_pw=cv_pw, cv_out_g=cv_out_g, ret_gn_g=ret_gn_g, w_out=w_out,
             norm_ffn_g=norm_ffn_g, ffn_up=ffn_up, ffn_dw_w=ffn_dw_w, ffn_dw_b=ffn_dw_b, ffn_down=ffn_down)
    M = dict(meta=m_meta, norm_mix_g=m_norm_mix_g, w_in=m_w_in, q_norm_g=m_q_norm_g, k_norm_g=m_k_norm_g,
             attn_sinks=m_attn_sinks, attn_out_g=m_attn_out_g, cv_dw_w=m_cv_dw_w, cv_dw_b=m_cv_dw_b,
             cv_ln_g=m_cv_ln_g, cv_ln_b=m_cv_ln_b, cv_pw=m_cv_pw, cv_out_g=m_cv_out_g, ret_gn_g=m_ret_gn_g,
             w_out=m_w_out, norm_ffn_g=m_norm_ffn_g, ffn_up=m_ffn_up, ffn_dw_w=m_ffn_dw_w, ffn_dw_b=m_ffn_dw_b,
             ffn_down=m_ffn_down)
    V = dict(meta=v_meta, norm_mix_g=v_norm_mix_g, w_in=v_w_in, q_norm_g=v_q_norm_g, k_norm_g=v_k_norm_g,
             attn_sinks=v_attn_sinks, attn_out_g=v_attn_out_g, cv_dw_w=v_cv_dw_w, cv_dw_b=v_cv_dw_b,
             cv_ln_g=v_cv_ln_g, cv_ln_b=v_cv_ln_b, cv_pw=v_cv_pw, cv_out_g=v_cv_out_g, ret_gn_g=v_ret_gn_g,
             w_out=v_w_out, norm_ffn_g=v_norm_ffn_g, ffn_up=v_ffn_up, ffn_dw_w=v_ffn_dw_w, ffn_dw_b=v_ffn_dw_b,
             ffn_down=v_ffn_down)
    me = 4 * lax.axis_index("x") + 2 * lax.axis_index("y") + lax.axis_index("c")

    sh = {n: [W[n][l].astype(BF16) for l in range(DEPTH)] for n in _BIG}
    mix = lambda l: [sh["w_in"][l], sh["cv_pw"][l], sh["w_out"][l]]
    ffn = lambda l: [sh["ffn_up"][l], sh["ffn_down"][l]]
    first = all_gather(mix(0) + [meta, cv_dw_w, ffn_dw_w], "gather_first")
    g_meta, g_cdw, g_fdw = first[3:6]

    def landing(shards):
        return [lax.dynamic_update_slice(lax.empty((NDEV,) + s.shape, s.dtype), s[None], (me,) + (0,) * s.ndim)
                for s in shards]

    gathers = {("ffn", 0): gather_start("gather_ffn0_start", ffn(0), landing(ffn(0)), after=[first[0]])}
    gathers["mix", 1] = gather_start("gather_mix1_start", mix(1), landing(mix(1)), after=[gathers["ffn", 0][4]])
    gathers["ffn", 1] = gather_start("gather_ffn1_start", ffn(1), landing(ffn(1)), after=[gathers["mix", 1][4]])

    def mixer_w(l, h):
        g_in, g_pw, g_out = first[0:3] if l == 0 else gather_wait(f"gather_mix{l}_wait", gathers["mix", l], after=[h])[1]
        return g_in.transpose(1, 0, 2).reshape(D, IN_W), g_pw.reshape(CV, CV), g_out.reshape(D, D)

    def ffn_w(l, h1):
        return gather_wait(f"gather_ffn{l}_wait", gathers["ffn", l], after=[h1])[1]

    scatters = {}

    def ffn_grads(l, dw_down, dw_up):
        scatters["ffn", l] = scatter_start(f"scatter_ffn{l}_start", [dw_up, dw_down.reshape(NDEV, DNR, D)])
        return scatters["ffn", l][4]

    def mixer_grads(l, dw_out, dpw, dw_in):
        grads = [dw_in.reshape(D, NDEV, IN_W // NDEV).transpose(1, 0, 2), dpw.astype(BF16).reshape(NDEV, CV // NDEV, CV),
                 dw_out.reshape(NDEV, D // NDEV, D)]
        scatters["mix", l] = scatter_start(f"scatter_mix{l}_start", grads)
        return scatters["mix", l][4]

    P = dict(
        meta=g_meta.transpose(1, 0, 2).reshape(N_META, D),
        cv_dw_w=g_cdw.transpose(1, 2, 0, 3).reshape(DEPTH, CONV_K, CV),
        ffn_dw_w=g_fdw.transpose(1, 0, 2, 3),
        ffn_dw_b=ffn_dw_b.reshape(DEPTH, NDEV, 1, UPW),
        attn_sinks=attn_sinks,
    )
    for n in ("norm_mix_g", "q_norm_g", "k_norm_g", "attn_out_g", "cv_dw_b", "cv_ln_g", "cv_ln_b", "cv_out_g",
              "ret_gn_g", "norm_ffn_g"):
        P[n] = W[n].reshape(DEPTH, 1, -1)

    hooks = dict(mixer_w=mixer_w, ffn_w=ffn_w, ffn_grads=ffn_grads, mixer_grads=mixer_grads,
                 first_after=gathers["ffn", 1][4])
    loss_part, dx, dmeta, G = local_step(x[0], loss_target[0], P, hooks)
    loss = lax.psum(loss_part, ("x", "y", "c"))

    small_full = {
        "meta": dmeta,
        "cv_dw_w": jnp.stack(G["cv_dw_w"]),
        "ffn_dw_w": jnp.stack([g.transpose(1, 0, 2).reshape(FFN_K, 2 * D_FF) for g in G["ffn_dw_w"]]),
        "ffn_dw_b": jnp.stack([g.reshape(2 * D_FF) for g in G["ffn_dw_b"]]),
        "attn_sinks": jnp.stack(G["attn_sinks"]),
    }
    for n in _SMALL:
        if n not in small_full:
            small_full[n] = jnp.stack([g.reshape(-1) for g in G[n]])
    shapes = [small_full[n].shape for n in _SMALL]
    summed = _unpack(all_reduce_small(_pack([small_full[n] for n in _SMALL]), "reduce_small"), shapes)
    small_g = {}
    for n, g in zip(_SMALL, summed):
        if n in _SMALL_SHARDED:
            width = _SMALL_SHARDED[n] // NDEV
            g = lax.dynamic_slice_in_dim(g, me * width, width, axis=g.ndim - 1)
        small_g[n] = g

    out = {}
    tiles = {"w_in": 256, "cv_pw": 32, "w_out": 128, "ffn_up": 256, "ffn_down": 176}
    me1 = me.astype(jnp.int32).reshape(1)
    own, lands = {n: [None] * DEPTH for n in _BIG}, {n: [None] * DEPTH for n in _BIG}

    def arrived(kind, names, after):
        for l in range(DEPTH):
            srcs, got = scatter_wait(f"scatter_{kind}{l}_wait", scatters[kind, l], after=after)
            for n, s_, g_ in zip(names, srcs, got):
                own[n][l], lands[n][l] = s_, g_

    def update(names):
        for n in names:
            shard = W[n].shape
            rows, cols = shard[0] * shard[1], shard[2]
            res = adamw_big(own[n], lands[n], W[n].reshape(rows, cols), M[n].reshape(rows, cols),
                            V[n].reshape(rows, cols), me1, tiles[n], f"adamw_{n}")
            out[n] = [r.reshape(shard) for r in res]

    arrived("ffn", ("ffn_up", "ffn_down"), [dx])
    update(("ffn_up", "ffn_down"))
    sshapes = [W[n].shape for n in _SMALL]
    packed_g = _pack([small_g[n] for n in _SMALL])
    res = adamw(packed_g[None], _pack([W[n] for n in _SMALL]), _pack([M[n] for n in _SMALL]),
                _pack([V[n] for n in _SMALL]), packed_g.shape[0], "adamw_small")
    for i, r in enumerate(res):
        for n, a in zip(_SMALL, _unpack(r, sshapes)):
            out.setdefault(n, [None] * 4)[i] = a
    arrived("mix", ("w_in", "cv_pw", "w_out"), [out["ffn_down"][0], res[0]])
    update(("w_in", "cv_pw", "w_out"))

    return (loss, dx[None], *[out[n][0] for n in _ORDER], *[out[n][1] for n in _ORDER],
            *[out[n][2] for n in _ORDER], *[out[n][3] for n in _ORDER])
```

```python
import functools
import math

import numpy as np
import jax
import jax.numpy as jnp
from jax import lax
from jax.experimental import pallas as pl
from jax.experimental.pallas import tpu as pltpu

F32 = jnp.float32
BF16 = jnp.bfloat16

D = 1024
SEQ = 2048
DEPTH = 2
T = 128
L = SEQ + T
NB = L // T
N_META = 16
PAD = T - N_META
HD = 64
NQ = 8
NKV = 2
GQA = NQ // NKV
CV = 256
CONV_K = 31
RH = 4
D_FF = 2816
FFN_K = 3
IN_W = 2304
RMS_EPS = 1e-6
LN_EPS = 1e-5
NEG = -1e30
NDEV = 8
UPW = 2 * D_FF // NDEV
DNR = D_FF // NDEV
NPAIR = NDEV // 2

ADAM_LR, ADAM_B1, ADAM_B2, ADAM_EPS, ADAM_WD, ADAM_STEP = 0.001, 0.9, 0.999, 1e-08, 0.01, 10

VMEM_BIG = 56 * 1024 * 1024

MESH = pl.DeviceIdType.MESH


def _params(vmem=None):
    return pltpu.CompilerParams(vmem_limit_bytes=vmem) if vmem else None


def _dot(a, b, nt=False):
    return lax.dot_general(a, b, (((1,), (1 if nt else 0,)), ((), ())), preferred_element_type=F32)


def _sig(x):
    return 1.0 / (1.0 + jnp.exp(-x))


def _bf(x):
    return x.astype(BF16)


def _stack_rows(rows):
    idx = lax.broadcasted_iota(jnp.int32, (len(rows), rows[0].shape[1]), 0)
    out = jnp.zeros((len(rows), rows[0].shape[1]), F32)
    for r, v in enumerate(rows):
        out = jnp.where(idx == r, v, out)
    return out


def rms_fwd(x, g, out_dtype, name, after=None):
    n, w = x.shape

    def body(x_ref, g_ref, *rest):
        o_ref = rest[-1]
        xv = x_ref[...]
        r = lax.rsqrt(jnp.mean(xv * xv, axis=-1, keepdims=True) + RMS_EPS)
        o_ref[...] = (xv * r * g_ref[...]).astype(o_ref.dtype)

    deps = [] if after is None else [after]
    return pl.pallas_call(
        body, name=name, out_shape=jax.ShapeDtypeStruct((n, w), out_dtype), grid=(n // T,),
        in_specs=[pl.BlockSpec((T, w), lambda i: (i, 0)), pl.BlockSpec((1, w), lambda i: (0, 0))] + [_HBM] * len(deps),
        out_specs=pl.BlockSpec((T, w), lambda i: (i, 0)))(x, g, *deps)


def rms_bwd(x, g, dy, dres, name, after=None, dy_col=0):
    n, w = x.shape
    has_res = dres is not None
    deps = [] if after is None else [after]

    def body(x_ref, g_ref, dy_ref, *rest):
        rest = rest[:len(rest) - 3 - len(deps)] + rest[len(rest) - 3:]
        if has_res:
            dres_ref, dx_ref, dxm_ref, dg_ref = rest
        else:
            dx_ref, dxm_ref, dg_ref = rest
        i = pl.program_id(0)
        xv = x_ref[...]
        r = lax.rsqrt(jnp.mean(xv * xv, axis=-1, keepdims=True) + RMS_EPS)
        xh = xv * r
        dyv = dy_ref[...]
        dxh = dyv * g_ref[...]
        dx = r * (dxh - xh * jnp.mean(dxh * xh, axis=-1, keepdims=True))
        if has_res:
            dx = dx + dres_ref[...]
        dx_ref[...] = dx
        rows = i * T + lax.broadcasted_iota(jnp.int32, (T, 1), 0)
        dxm_ref[...] = jnp.where(rows >= PAD, dx, 0.0).astype(BF16)
        part = jnp.sum(dyv * xh, axis=0, keepdims=True)

        @pl.when(i == 0)
        def _():
            dg_ref[...] = part

        @pl.when(i > 0)
        def _():
            dg_ref[...] += part

    row = pl.BlockSpec((T, w), lambda i: (i, 0))
    vec = pl.BlockSpec((1, w), lambda i: (0, 0))
    ins = [x, g, dy] + ([dres] if has_res else []) + deps
    return pl.pallas_call(
        body, name=name,
        out_shape=(jax.ShapeDtypeStruct((n, w), F32), jax.ShapeDtypeStruct((n, w), BF16),
                   jax.ShapeDtypeStruct((1, w), F32)),
        grid=(n // T,),
        in_specs=[row, vec, pl.BlockSpec((T, w), lambda i: (i, dy_col))] + ([row] if has_res else [])
        + [_HBM] * len(deps),
        out_specs=(row, row, vec))(*ins)


def loss_head(h, target):
    def body(h_ref, t_ref, dh_ref, dhm_ref, loss_ref):
        n = pl.program_id(0)
        e = jnp.where(n > 0, h_ref[...] - t_ref[...], 0.0)
        dh = e * (1.0 / D)
        dh_ref[...] = dh
        dhm_ref[...] = dh.astype(BF16)
        part = jnp.sum(jnp.sum(e * e, axis=1, keepdims=True), axis=0, keepdims=True) * (0.5 / D)

        @pl.when(n == 0)
        def _():
            loss_ref[...] = jnp.zeros_like(loss_ref)

        @pl.when(n > 0)
        def _():
            loss_ref[...] += jnp.broadcast_to(part, loss_ref.shape)

    row = pl.BlockSpec((T, D), lambda n: (n, 0))
    return pl.pallas_call(
        body, name="loss_head",
        out_shape=(jax.ShapeDtypeStruct((L, D), F32), jax.ShapeDtypeStruct((L, D), BF16),
                   jax.ShapeDtypeStruct((8, 128), F32)),
        grid=(NB,),
        in_specs=[row, pl.BlockSpec((T, D), lambda n: (jnp.maximum(n - 1, 0), 0))],
        out_specs=(row, row, pl.BlockSpec((8, 128), lambda n: (0, 0))))(h, target)


def _mm(name, a, b, *, grid, a_spec, b_spec, o_spec, out_shape, nt=False, ta=False, red=False, res=None,
        res_spec=None):
    def body(a_ref, b_ref, *rest):
        o_ref = rest[-1]
        av = a_ref[...]
        bv = b_ref[...]
        if bv.ndim == 3:
            bv = bv.reshape(bv.shape[0] * bv.shape[1], bv.shape[2])
        if ta:
            acc = lax.dot_general(av, bv, (((0,), (0,)), ((), ())), preferred_element_type=F32)
        else:
            acc = _dot(av, bv, nt)
        if red:
            k = pl.program_id(0)

            @pl.when(k == 0)
            def _():
                o_ref[...] = acc

            @pl.when(k > 0)
            def _():
                o_ref[...] += acc
        else:
            if res is not None:
                rows = lax.broadcasted_iota(jnp.int32, (acc.shape[0], 1), 0)
                acc = rest[0][...] + jnp.where(rows >= PAD, acc, 0.0)
            o_ref[...] = acc.astype(o_ref.dtype)

    ins = [a, b] + ([res] if res is not None else [])
    specs = [a_spec, b_spec] + ([res_spec] if res is not None else [])
    return pl.pallas_call(body, name=name, out_shape=out_shape, grid=grid, in_specs=specs, out_specs=o_spec,
                          compiler_params=_params(VMEM_BIG))(*ins)


def mm_nn(name, a, b, tn, out_dtype=F32, res=None):
    m, k = a.shape
    n = b.shape[1]
    return _mm(name, a, b, grid=(n // tn,),
               a_spec=pl.BlockSpec((m, k), lambda j: (0, 0)), b_spec=pl.BlockSpec((k, tn), lambda j: (0, j)),
               o_spec=pl.BlockSpec((m, tn), lambda j: (0, j)), out_shape=jax.ShapeDtypeStruct((m, n), out_dtype),
               res=res, res_spec=pl.BlockSpec((m, tn), lambda j: (0, j)))


def mm_nt(name, a, b, tn):
    m, k = a.shape
    n = b.shape[0]
    return _mm(name, a, b, grid=(n // tn,), nt=True,
               a_spec=pl.BlockSpec((m, k), lambda j: (0, 0)), b_spec=pl.BlockSpec((tn, k), lambda j: (j, 0)),
               o_spec=pl.BlockSpec((m, tn), lambda j: (0, j)), out_shape=jax.ShapeDtypeStruct((m, n), F32))


def mm_ta(name, a, b, tm, out_dtype=BF16):
    k, m = a.shape
    n = b.shape[1]
    return _mm(name, a, b, grid=(m // tm,), ta=True,
               a_spec=pl.BlockSpec((k, tm), lambda j: (0, j)), b_spec=pl.BlockSpec((k, n), lambda j: (0, 0)),
               o_spec=pl.BlockSpec((tm, n), lambda j: (j, 0)), out_shape=jax.ShapeDtypeStruct((m, n), out_dtype))


def ffn_up_fwd(name, u2, wupt):
    return _mm(name, u2, wupt, grid=(NDEV,), nt=True,
               a_spec=pl.BlockSpec((L, D), lambda j: (0, 0)),
               b_spec=pl.BlockSpec((None, UPW, D), lambda j: (j, 0, 0)),
               o_spec=pl.BlockSpec((None, L, UPW), lambda j: (j, 0, 0)),
               out_shape=jax.ShapeDtypeStruct((NDEV, L, UPW), F32))


def _slab_specs():
    gate = pl.BlockSpec((None, L, UPW), lambda j: (jnp.minimum(j, NPAIR - 1), 0, 0))
    up = pl.BlockSpec((None, L, UPW), lambda j: (jnp.maximum(j - NPAIR, 0), 0, 0))
    return gate, up


def ffn_up_dx(name, dfg, dfu, wupt):
    def body(g_ref, u_ref, b_ref, o_ref):
        j = pl.program_id(0)

        @pl.when(j == 0)
        def _():
            o_ref[...] = _dot(g_ref[...], b_ref[...])

        @pl.when(jnp.logical_and(j > 0, j < NPAIR))
        def _():
            o_ref[...] += _dot(g_ref[...], b_ref[...])

        @pl.when(j >= NPAIR)
        def _():
            o_ref[...] += _dot(u_ref[...], b_ref[...])

    gate, up = _slab_specs()
    return pl.pallas_call(
        body, name=name, out_shape=jax.ShapeDtypeStruct((L, D), F32), grid=(NDEV,),
        in_specs=[gate, up, pl.BlockSpec((None, UPW, D), lambda j: (j, 0, 0))],
        out_specs=pl.BlockSpec((L, D), lambda j: (0, 0)), compiler_params=_params(VMEM_BIG))(dfg, dfu, wupt)


def ffn_up_dw(name, u2, dfg, dfu):
    tdot = lambda a, b: lax.dot_general(a, b, (((0,), (0,)), ((), ())), preferred_element_type=F32)

    def body(a_ref, g_ref, u_ref, o_ref):
        j = pl.program_id(0)

        @pl.when(j < NPAIR)
        def _():
            o_ref[...] = tdot(g_ref[...], a_ref[...]).astype(o_ref.dtype)

        @pl.when(j >= NPAIR)
        def _():
            o_ref[...] = tdot(u_ref[...], a_ref[...]).astype(o_ref.dtype)

    gate, up = _slab_specs()
    return pl.pallas_call(
        body, name=name, out_shape=jax.ShapeDtypeStruct((NDEV, UPW, D), BF16), grid=(NDEV,),
        in_specs=[pl.BlockSpec((L, D), lambda j: (0, 0)), gate, up],
        out_specs=pl.BlockSpec((None, UPW, D), lambda j: (j, 0, 0)), compiler_params=_params(VMEM_BIG))(u2, dfg, dfu)


def ffn_down_fwd(name, a, wdn, res, tn=256):
    def body(a_ref, b_ref, r_ref, o_ref):
        acc = jnp.zeros((L, tn), F32)
        for g in range(NPAIR):
            bv = b_ref[2 * g:2 * g + 2]
            acc = acc + _dot(a_ref[g], bv.reshape(2 * DNR, tn))
        rows = lax.broadcasted_iota(jnp.int32, (L, 1), 0)
        o_ref[...] = r_ref[...] + jnp.where(rows >= PAD, acc, 0.0)

    return pl.pallas_call(
        body, name=name, out_shape=jax.ShapeDtypeStruct((L, D), F32), grid=(D // tn,),
        in_specs=[pl.BlockSpec((NPAIR, L, UPW), lambda j: (0, 0, 0)),
                  pl.BlockSpec((NDEV, DNR, tn), lambda j: (0, 0, j)),
                  pl.BlockSpec((L, tn), lambda j: (0, j))],
        out_specs=pl.BlockSpec((L, tn), lambda j: (0, j)),
        compiler_params=_params(VMEM_BIG))(a, wdn, res)


def ffn_down_dx(name, dh, wdn):
    return _mm(name, dh, wdn, grid=(NPAIR,), nt=True,
               a_spec=pl.BlockSpec((L, D), lambda g: (0, 0)),
               b_spec=pl.BlockSpec((2, DNR, D), lambda g: (g, 0, 0)),
               o_spec=pl.BlockSpec((None, L, UPW), lambda g: (g, 0, 0)),
               out_shape=jax.ShapeDtypeStruct((NPAIR, L, UPW), F32))


def ffn_down_dw(name, a, dh):
    return _mm(name, a, dh, grid=(NPAIR,), ta=True,
               a_spec=pl.BlockSpec((None, L, UPW), lambda g: (g, 0, 0)),
               b_spec=pl.BlockSpec((L, D), lambda g: (0, 0)),
               o_spec=pl.BlockSpec((UPW, D), lambda g: (g, 0)),
               out_shape=jax.ShapeDtypeStruct((D_FF, D), BF16))


_SLOPES = [2.0 ** (-8.0 * (h + 1) / NQ) for h in range(NQ)]
_SCALE = HD ** -0.5


QR = GQA * T
KC = 3 * T


def _attn_geometry(g, n, sink_ref):
    row = lax.broadcasted_iota(jnp.int32, (QR, KC), 0)
    col = lax.broadcasted_iota(jnp.int32, (QR, KC), 1)
    i = row & (T - 1)
    j = col & (T - 1)
    blk = col // T
    d_meta = n * T + i - j
    ok_meta = (j >= PAD) & (d_meta >= 0)
    ok_prev = j > i + jnp.where(n >= 2, 0, T)
    ok_cur = j <= i - jnp.where(n >= 1, 0, T)
    ok = ((blk == 0) & ok_meta) | ((blk == 1) & ok_prev) | ((blk == 2) & ok_cur)
    dist = jnp.where(blk == 0, jnp.minimum(d_meta, T), jnp.where(blk == 1, T + i - j, i - j)).astype(F32)
    head = lax.broadcasted_iota(jnp.int32, (QR, 1), 0) // T
    slope = jnp.zeros((QR, 1), F32)
    sink = jnp.zeros((QR, 1), F32)
    for hh in range(GQA):
        slope = jnp.where(head == hh, jnp.where(g == 0, _SLOPES[hh], _SLOPES[GQA + hh]), slope)
        sink = jnp.where(head == hh, sink_ref[g * GQA + hh], sink)
    return ok, slope * dist, sink


def _attn_probs(qn, keys, geo):
    ok, penalty, sink = geo
    s = jnp.where(ok, _dot(qn, keys, nt=True) * _SCALE - penalty, NEG)
    m = jnp.maximum(jnp.max(s, axis=-1, keepdims=True), sink)
    e = jnp.exp(s - m)
    e_sink = jnp.exp(sink - m)
    inv = 1.0 / (e.sum(axis=-1, keepdims=True) + e_sink)
    return e, inv, e_sink * inv


def _rms_rows(x, g):
    r = lax.rsqrt(jnp.mean(x * x, axis=-1, keepdims=True) + RMS_EPS)
    xh = x * r
    return xh * g, xh, r


def _rms_rows_bwd(dy, g, xh, r):
    dxh = dy * g
    return r * (dxh - xh * jnp.mean(dxh * xh, axis=-1, keepdims=True))


def _attn_specs():
    qspec = pl.BlockSpec((GQA, T, HD), lambda g, n: (g, n, 0))
    cur = pl.BlockSpec((None, T, HD), lambda g, n: (g, n, 0))
    prev = pl.BlockSpec((None, T, HD), lambda g, n: (g, jnp.maximum(n - 1, 0), 0))
    meta = pl.BlockSpec((None, T, HD), lambda g, n: (g, 0, 0))
    vec = pl.BlockSpec((1, HD), lambda g, n: (0, 0))
    return qspec, cur, prev, meta, vec


def _rows3(m_ref, p_ref, c_ref):
    return jnp.concatenate([m_ref[...], p_ref[...], c_ref[...]], axis=0)


def attn_fwd(q, k, v, qg, kg, sinks, name):
    def body(sink_ref, q_ref, kc_ref, kp_ref, km_ref, vc_ref, vp_ref, vm_ref, qg_ref, kg_ref, o_ref):
        g = pl.program_id(0)
        n = pl.program_id(1)
        geo = _attn_geometry(g, n, sink_ref)
        keys = _bf(_rms_rows(_rows3(km_ref, kp_ref, kc_ref), kg_ref[...])[0])
        vals = _bf(_rows3(vm_ref, vp_ref, vc_ref))
        qn = _bf(_rms_rows(q_ref[...].reshape(QR, HD), qg_ref[...])[0])
        e, inv, _ = _attn_probs(qn, keys, geo)
        o_ref[...] = (_dot(_bf(e), vals) * inv).reshape(GQA, T, HD)

    qspec, cur, prev, meta, vec = _attn_specs()
    return pl.pallas_call(
        body, name=name, out_shape=jax.ShapeDtypeStruct((NQ, L, HD), F32), grid=(NKV, NB),
        in_specs=[pl.BlockSpec(memory_space=pltpu.SMEM), qspec, cur, prev, meta, cur, prev, meta, vec, vec],
        out_specs=qspec)(sinks, q, k, k, k, v, v, v, qg, kg)


def attn_bwd(q, k, v, do, qg, kg, sinks, name):
    def body(sink_ref, q_ref, kc_ref, kp_ref, km_ref, vc_ref, vp_ref, vm_ref, do_ref, qg_ref, kg_ref,
             dq_ref, dkc_ref, dkp_ref, dkm_ref, dvc_ref, dvp_ref, dvm_ref, dqg_ref, dkg_ref, dsk_ref):
        g = pl.program_id(0)
        n = pl.program_id(1)
        geo = _attn_geometry(g, n, sink_ref)
        kgv = kg_ref[...]
        qgv = qg_ref[...]
        kn_f, kh, kr = _rms_rows(_rows3(km_ref, kp_ref, kc_ref), kgv)
        keys = _bf(kn_f)
        vals = _bf(_rows3(vm_ref, vp_ref, vc_ref))
        qn_f, qh, qr = _rms_rows(q_ref[...].reshape(QR, HD), qgv)
        qn = _bf(qn_f)
        e, inv, p_sink = _attn_probs(qn, keys, geo)
        dob = _bf(do_ref[...].reshape(QR, HD))
        p = e * inv
        dp = _dot(dob, vals, nt=True)
        delta = (p * dp).sum(axis=-1, keepdims=True)
        dsk = _stack_rows([jnp.broadcast_to(-jnp.sum((p_sink * delta)[hh * T:(hh + 1) * T], axis=0, keepdims=True),
                                            (1, 128)) for hh in range(GQA)])
        ds = p * (dp - delta)
        dqn = _dot(_bf(ds), keys) * _SCALE
        dq_ref[...] = _rms_rows_bwd(dqn, qgv, qh, qr).reshape(GQA, T, HD)
        dqg_acc = jnp.sum(dqn * qh, axis=0, keepdims=True)
        dkn = _dot(_bf(ds.T), qn) * _SCALE
        dkg_acc = jnp.sum(dkn * kh, axis=0, keepdims=True)
        dk_all = _rms_rows_bwd(dkn, kgv, kh, kr)
        dv_all = _dot(_bf(p.T), dob)
        dk_raw = [dk_all[b * T:(b + 1) * T] for b in range(3)]
        dvv = [dv_all[b * T:(b + 1) * T] for b in range(3)]
        dkp_ref[...] = dk_raw[1]
        dkc_ref[...] = dk_raw[2]
        dvp_ref[...] = dvv[1]
        dvc_ref[...] = dvv[2]
        first = jnp.logical_and(g == 0, n == 0)

        @pl.when(n == 0)
        def _():
            dkm_ref[...] = dk_raw[0]
            dvm_ref[...] = dvv[0]
            dsk_ref[...] = dsk

        @pl.when(n > 0)
        def _():
            dkm_ref[...] += dk_raw[0]
            dvm_ref[...] += dvv[0]
            dsk_ref[...] += dsk

        @pl.when(first)
        def _():
            dqg_ref[...] = dqg_acc
            dkg_ref[...] = dkg_acc

        @pl.when(jnp.logical_not(first))
        def _():
            dqg_ref[...] += dqg_acc
            dkg_ref[...] += dkg_acc

    qspec, cur, prev, meta, vec = _attn_specs()
    kv_shape = jax.ShapeDtypeStruct((NKV, L, HD), F32)
    meta_shape = jax.ShapeDtypeStruct((NKV, T, HD), F32)
    vec_shape = jax.ShapeDtypeStruct((1, HD), F32)
    return pl.pallas_call(
        body, name=name,
        out_shape=(jax.ShapeDtypeStruct((NQ, L, HD), F32), kv_shape, kv_shape, meta_shape, kv_shape, kv_shape,
                   meta_shape, vec_shape, vec_shape, jax.ShapeDtypeStruct((NKV, GQA, 128), F32)),
        grid=(NKV, NB),
        in_specs=[pl.BlockSpec(memory_space=pltpu.SMEM), qspec, cur, prev, meta, cur, prev, meta, qspec, vec, vec],
        out_specs=(qspec, cur, cur, meta, cur, cur, meta, vec, vec,
                   pl.BlockSpec((None, GQA, 128), lambda g, n: (g, 0, 0))))(sinks, q, k, k, k, v, v, v, do, qg, kg)


RW = RH * HD


def _ret_tables():
    h = np.arange(RH, dtype=np.float64)
    lg = np.log1p(-np.exp2(-5.0 - h))
    idx = np.arange(T, dtype=np.float64)
    diff = idx[:, None] - idx[None, :]
    decay = np.where(diff[None] >= 0, np.exp(np.maximum(diff, 0.0)[None] * lg[:, None, None]), 0.0)
    zeta = np.exp((T - 1 - idx)[None, :] * lg[:, None])
    xi = np.exp((idx + 1.0)[None, :] * lg[:, None])
    cd = np.exp(T * lg)
    lanes = lambda a: np.repeat(a.T, HD, axis=1)
    head_of = np.arange(RW) // HD
    same = (head_of[:, None] == head_of[None, :]).astype(np.float64)
    f = lambda a: jnp.asarray(a, F32)
    return dict(decay=f(decay), zeta=f(lanes(zeta)), xi=f(lanes(xi)), cd=f(np.repeat(cd, HD)[None, :]),
                head=f((head_of[None, :] == np.arange(RH)[:, None]).astype(np.float64)[:, None, :]),
                same=f(same), avg=jnp.asarray(same / HD, BF16))


def _seg_mean(x, avg):
    hi = _bf(x)
    lo = _bf(x - hi.astype(F32))
    return _dot(hi, avg) + _dot(lo, avg)


def _ret_specs(col0, order):
    return lambda col: pl.BlockSpec((T, RW), lambda i: (order(i), col0 + col))


def _ret_chunk(q, kf, v, s, tb):
    dec, xi, head = tb
    vb = _bf(v)
    kb = _bf(kf)
    y = _dot(_bf(q * xi), _bf(s))
    a = []
    for h in range(RH):
        a.append(_dot(_bf(q * head[h]), kb, nt=True) * dec[h])
        y = y + head[h] * _dot(_bf(a[h]), vb)
    return a, y


def _gn_rows(y):
    mu = jnp.mean(y, axis=-1, keepdims=True)
    yc = y - mu
    rstd = lax.rsqrt(jnp.mean(yc * yc, axis=-1, keepdims=True) + LN_EPS)
    return yc * rstd, rstd


_RET_COL0 = _RET_Q = 5


def _ret_consts(tb):
    names = ("decay", "zeta", "xi", "cd", "head", "same", "avg")
    full = lambda a: pl.BlockSpec(a.shape, lambda i: (0,) * a.ndim)
    return [tb[n] for n in names], [full(tb[n]) for n in names]


def _ret_inputs(n, q_ref, k_ref, v_ref):
    rows = n * T + lax.broadcasted_iota(jnp.int32, (T, 1), 0)
    valid = rows >= PAD
    return q_ref[...], jnp.where(valid, k_ref[...] * (HD ** -0.5), 0.0), v_ref[...], valid


def ret_fwd(proj, gng, tb, name):
    def body(q_ref, k_ref, v_ref, g_ref, dec_ref, zeta_ref, xi_ref, cd_ref, head_ref, same_ref, avg_ref, gng_ref,
             y_ref, st_ref, s_scr):
        n = pl.program_id(0)

        @pl.when(n == 0)
        def _():
            s_scr[...] = jnp.zeros_like(s_scr)

        s = s_scr[...]
        st_ref[...] = s
        q, kf, v, _ = _ret_inputs(n, q_ref, k_ref, v_ref)
        _, y = _ret_chunk(q, kf, v, s, (dec_ref, xi_ref[...], head_ref))
        s_scr[...] = cd_ref[...] * s + same_ref[...] * _dot(_bf((kf * zeta_ref[...]).T), _bf(v))
        avg = avg_ref[...]
        yc = y - _seg_mean(y, avg)
        yh = yc * lax.rsqrt(_seg_mean(yc * yc, avg) + LN_EPS)
        gv = g_ref[...]
        y_ref[...] = gv * _sig(gv) * (yh * gng_ref[...])

    col = _ret_specs(_RET_COL0, lambda i: i)
    consts, cspecs = _ret_consts(tb)
    return pl.pallas_call(
        body, name=name,
        out_shape=(jax.ShapeDtypeStruct((L, RW), F32), jax.ShapeDtypeStruct((NB, RW, RW), F32)),
        grid=(NB,),
        in_specs=[col(0), col(1), col(2), col(3)] + cspecs + [pl.BlockSpec((1, RW), lambda i: (0, 0))],
        out_specs=(pl.BlockSpec((T, RW), lambda i: (i, 0)), pl.BlockSpec((None, RW, RW), lambda i: (i, 0, 0))),
        scratch_shapes=[pltpu.VMEM((RW, RW), F32)])(proj, proj, proj, proj, *consts, gng)


def ret_bwd(proj, dycat, states, gng, tb, name):
    def body(q_ref, k_ref, v_ref, g_ref, do_ref, st_ref, dec_ref, zeta_ref, xi_ref, cd_ref, head_ref, same_ref,
             avg_ref, gng_ref, d_ref, dgn_ref, ds_scr):
        i = pl.program_id(0)
        n = NB - 1 - i

        @pl.when(i == 0)
        def _():
            ds_scr[...] = jnp.zeros_like(ds_scr)

        dsn = _bf(ds_scr[...])
        s = st_ref[...]
        sb = _bf(s)
        q, kf, v, valid = _ret_inputs(n, q_ref, k_ref, v_ref)
        xi, zeta, avg = xi_ref[...], zeta_ref[...], avg_ref[...]
        a, y = _ret_chunk(q, kf, v, s, (dec_ref, xi, head_ref))
        yc = y - _seg_mean(y, avg)
        rstd = lax.rsqrt(_seg_mean(yc * yc, avg) + LN_EPS)
        yh = yc * rstd
        gv = g_ref[...]
        sg = _sig(gv)
        sil = gv * sg
        gn = gng_ref[...]
        dout = do_ref[...]
        d_ref[:, 3 * RW:4 * RW] = dout * (yh * gn) * (sg * (1.0 + gv * (1.0 - sg)))
        dyh = dout * sil * gn
        part = jnp.sum(dout * sil * yh, axis=0, keepdims=True)

        @pl.when(i == 0)
        def _():
            dgn_ref[...] = part

        @pl.when(i > 0)
        def _():
            dgn_ref[...] += part

        dy = rstd * (dyh - _seg_mean(dyh, avg) - yh * _seg_mean(dyh * yh, avg))
        dyb = _bf(dy)
        vb, kb, qb = _bf(v), _bf(kf), _bf(q)
        dq = _dot(dyb, sb, nt=True) * xi
        dkf = _dot(vb, dsn, nt=True) * zeta
        dv = _dot(_bf(kf * zeta), dsn)
        for h in range(RH):
            m = head_ref[h]
            da = _dot(_bf(dy * m), vb, nt=True) * dec_ref[h]
            dv = dv + m * _dot(_bf(a[h].T), dyb)
            dq = dq + m * _dot(_bf(da), kb)
            dkf = dkf + m * _dot(_bf(da.T), qb)
        d_ref[:, 0:RW] = dq
        d_ref[:, RW:2 * RW] = jnp.where(valid, dkf * (HD ** -0.5), 0.0)
        d_ref[:, 2 * RW:3 * RW] = dv
        ds_scr[...] = cd_ref[...] * ds_scr[...] + same_ref[...] * _dot(_bf((q * xi).T), dyb)

    back = lambda i: NB - 1 - i
    col = _ret_specs(_RET_COL0, back)
    consts, cspecs = _ret_consts(tb)
    vec = pl.BlockSpec((1, RW), lambda i: (0, 0))
    return pl.pallas_call(
        body, name=name,
        out_shape=(jax.ShapeDtypeStruct((L, 4 * RW), F32), jax.ShapeDtypeStruct((1, RW), F32)),
        grid=(NB,),
        in_specs=[col(0), col(1), col(2), col(3), _ret_specs(3, back)(0),
                  pl.BlockSpec((None, RW, RW), lambda i: (back(i), 0, 0))] + cspecs + [vec],
        out_specs=(pl.BlockSpec((T, 4 * RW), lambda i: (back(i), 0)), vec),
        scratch_shapes=[pltpu.VMEM((RW, RW), F32)])(proj, proj, proj, proj, dycat, states, *consts, gng)


HALO = 32
TAP0 = HALO - (CONV_K - 1)


def _conv_specs():
    cur = lambda col=0: pl.BlockSpec((T, CV), lambda c: (c, col))
    before = lambda col=0: pl.BlockSpec((HALO, CV), lambda c: (jnp.maximum(c * (T // HALO) - 1, 0), col))
    after = pl.BlockSpec((HALO, CV), lambda c: (jnp.minimum((c + 1) * (T // HALO), L // HALO - 1), 0))
    full = lambda r, w: pl.BlockSpec((r, w), lambda c: (0, 0))
    return cur, before, after, full


_CONV_A, _CONV_B = 3, 4
_DY_CONV = 2


def _conv_tile_fwd(c, ca_ref, cah_ref, cb_ref, cbh_ref, w_ref, b_ref, lg_ref, lb_ref, u_scr):
    u_scr[0:HALO, :] = jnp.where(c > 0, cah_ref[...] * _sig(cbh_ref[...]), 0.0)
    u_scr[HALO:HALO + T, :] = ca_ref[...] * _sig(cb_ref[...])
    acc = jnp.zeros((T, CV), F32)
    for k in range(CONV_K):
        acc = acc + w_ref[k:k + 1, :] * u_scr[TAP0 + k:TAP0 + k + T, :]
    xc = acc + b_ref[...]
    xh, rstd = _gn_rows(xc)
    z = xh * lg_ref[...] + lb_ref[...]
    return xh, rstd, z, _sig(z)


def conv_fwd(proj, w, b, lg, lb, pw, og, name):
    def body(ca_ref, cah_ref, cb_ref, cbh_ref, w_ref, b_ref, lg_ref, lb_ref, pw_ref, og_ref, y_ref, u_scr):
        c = pl.program_id(0)
        _, _, z, sg = _conv_tile_fwd(c, ca_ref, cah_ref, cb_ref, cbh_ref, w_ref, b_ref, lg_ref, lb_ref, u_scr)
        zp = _dot(_bf(z * sg), pw_ref[...])
        y_ref[...] = _rms_rows(zp, og_ref[...])[0]

    cur, before, _, full = _conv_specs()
    vec = full(1, CV)
    return pl.pallas_call(
        body, name=name, out_shape=jax.ShapeDtypeStruct((L, CV), F32), grid=(NB,),
        in_specs=[cur(_CONV_A), before(_CONV_A), cur(_CONV_B), before(_CONV_B), full(CONV_K, CV), vec, vec, vec,
                  full(CV, CV), vec],
        out_specs=cur(),
        scratch_shapes=[pltpu.VMEM((HALO + T, CV), F32)])(proj, proj, proj, proj, w, b, lg, lb, pw, og)


def conv_bwd(proj, dycat, w, b, lg, lb, pw, og, name):
    def body1(ca_ref, cah_ref, cb_ref, cbh_ref, dy_ref, w_ref, b_ref, lg_ref, lb_ref, pw_ref, og_ref,
              dxc_ref, dw_ref, db_ref, dlg_ref, dlb_ref, dog_ref, dpw_ref, u_scr):
        c = pl.program_id(0)

        @pl.when(c == 0)
        def _():
            for r in (dw_ref, db_ref, dlg_ref, dlb_ref, dog_ref, dpw_ref):
                r[...] = jnp.zeros_like(r)

        xh, rstd, z, sg = _conv_tile_fwd(c, ca_ref, cah_ref, cb_ref, cbh_ref, w_ref, b_ref, lg_ref, lb_ref, u_scr)
        s = z * sg
        zp = _dot(_bf(s), pw_ref[...])
        ogv = og_ref[...]
        _, zph, r2 = _rms_rows(zp, ogv)
        dyv = dy_ref[...]
        dog_ref[...] += jnp.sum(dyv * zph, axis=0, keepdims=True)
        dzpb = _bf(_rms_rows_bwd(dyv, ogv, zph, r2))
        dpw_ref[...] += _dot(_bf(s.T), dzpb)
        dz = _dot(dzpb, pw_ref[...], nt=True) * (sg * (1.0 + z * (1.0 - sg)))
        dlg_ref[...] += jnp.sum(dz * xh, axis=0, keepdims=True)
        dlb_ref[...] += jnp.sum(dz, axis=0, keepdims=True)
        dxh = dz * lg_ref[...]
        dxc = rstd * (dxh - jnp.mean(dxh, axis=-1, keepdims=True) - xh * jnp.mean(dxh * xh, axis=-1, keepdims=True))
        db_ref[...] += jnp.sum(dxc, axis=0, keepdims=True)
        dxc_ref[...] = dxc
        for k in range(CONV_K):
            dw_ref[k:k + 1, :] += jnp.sum(dxc * u_scr[TAP0 + k:TAP0 + k + T, :], axis=0, keepdims=True)

    def body2(dx_ref, dxa_ref, ca_ref, cb_ref, w_ref, dca_ref, dcb_ref, d_scr):
        c = pl.program_id(0)
        d_scr[0:T, :] = dx_ref[...]
        d_scr[T:T + HALO, :] = jnp.where(c < NB - 1, dxa_ref[...], 0.0)
        du = jnp.zeros((T, CV), F32)
        for k in range(CONV_K):
            off = CONV_K - 1 - k
            du = du + w_ref[k:k + 1, :] * d_scr[off:off + T, :]
        sg = _sig(cb_ref[...])
        dca_ref[...] = du * sg
        dcb_ref[...] = du * ca_ref[...] * sg * (1.0 - sg)

    cur, before, after, full = _conv_specs()
    seq = jax.ShapeDtypeStruct((L, CV), F32)
    vsh = jax.ShapeDtypeStruct((1, CV), F32)
    vec = full(1, CV)
    dxc, dw, db, dlg, dlb, dog, dpw = pl.pallas_call(
        body1, name=name + "_a",
        out_shape=(seq, jax.ShapeDtypeStruct((CONV_K, CV), F32), vsh, vsh, vsh, vsh,
                   jax.ShapeDtypeStruct((CV, CV), F32)),
        grid=(NB,),
        in_specs=[cur(_CONV_A), before(_CONV_A), cur(_CONV_B), before(_CONV_B), cur(_DY_CONV), full(CONV_K, CV), vec,
                  vec, vec, full(CV, CV), vec],
        out_specs=(cur(), full(CONV_K, CV), vec, vec, vec, vec, full(CV, CV)),
        scratch_shapes=[pltpu.VMEM((HALO + T, CV), F32)])(proj, proj, proj, proj, dycat, w, b, lg, lb, pw, og)
    dca, dcb = pl.pallas_call(
        body2, name=name + "_b", out_shape=(seq, seq), grid=(NB,),
        in_specs=[cur(), after, cur(_CONV_A), cur(_CONV_B), full(CONV_K, CV)], out_specs=(cur(), cur()),
        scratch_shapes=[pltpu.VMEM((T + HALO, CV), F32)])(dxc, dxc, proj, proj, w)
    return dca, dcb, dw, db, dlg, dlb, dog, dpw


HR = 8


def _ffn_specs():
    cur = lambda off: pl.BlockSpec((None, T, UPW), lambda j, c: (j + off, c, 0))
    before = lambda off: pl.BlockSpec((None, HR, UPW), lambda j, c: (j + off, jnp.maximum(c * (T // HR) - 1, 0), 0))
    after = lambda off: pl.BlockSpec(
        (None, HR, UPW), lambda j, c: (j + off, jnp.minimum((c + 1) * (T // HR), L // HR - 1), 0))
    wspec = lambda off, r: pl.BlockSpec((None, r, UPW), lambda j, c: (j + off, 0, 0))
    return cur, before, after, wspec


def ffn_act_fwd(f, w, b, name):
    strip = 16

    def body(fg_ref, fgh_ref, fu_ref, fuh_ref, wg_ref, wu_ref, bg_ref, bu_ref, a_ref, gc_ref, uc_ref, hg_scr, hu_scr):
        c = pl.program_id(1)
        for scr, x_ref, xh_ref in ((hg_scr, fg_ref, fgh_ref), (hu_scr, fu_ref, fuh_ref)):
            scr[0:HR, :] = jnp.where(c > 0, xh_ref[...], 0.0)
            scr[HR:HR + strip, :] = x_ref[0:strip, :]
        wg = [wg_ref[k:k + 1, :] for k in range(FFN_K)]
        wu = [wu_ref[k:k + 1, :] for k in range(FFN_K)]
        bg, bu = bg_ref[...], bu_ref[...]

        def conv(src, base, w, b):
            acc = b
            for k in range(FFN_K):
                o = base - (FFN_K - 1) + k
                acc = acc + w[k] * src[o:o + strip, :]
            return acc

        for s in range(T // strip):
            r = s * strip
            gate = conv(hg_scr, HR, wg, bg) if s == 0 else conv(fg_ref, r, wg, bg)
            up = conv(hu_scr, HR, wu, bu) if s == 0 else conv(fu_ref, r, wu, bu)
            gc_ref[r:r + strip, :] = gate
            uc_ref[r:r + strip, :] = up
            a_ref[r:r + strip, :] = (gate * _sig(gate) * up).astype(BF16)

    cur, before, _, wspec = _ffn_specs()
    pre = jax.ShapeDtypeStruct((NPAIR, L, UPW), F32)
    return pl.pallas_call(
        body, name=name, out_shape=(jax.ShapeDtypeStruct((NPAIR, L, UPW), BF16), pre, pre), grid=(NPAIR, NB),
        in_specs=[cur(0), before(0), cur(NPAIR), before(NPAIR), wspec(0, FFN_K), wspec(NPAIR, FFN_K),
                  wspec(0, 1), wspec(NPAIR, 1)],
        out_specs=(cur(0), cur(0), cur(0)),
        scratch_shapes=[pltpu.VMEM((HR + strip, UPW), F32), pltpu.VMEM((HR + strip, UPW), F32)])(
            f, f, f, f, w, w, b, b)


def ffn_act_bwd(f, gc, uc, da, w, name):
    ext = T + HR
    sub = 8

    def body(fg_ref, fu_ref, gc_ref, gca_ref, uc_ref, uca_ref, da_ref, daa_ref, wg_ref, wu_ref,
             dfg_ref, dfu_ref, dwg_ref, dwu_ref, dbg_ref, dbu_ref, dg_scr, du_scr):
        c = pl.program_id(1)
        wg = [wg_ref[k:k + 1, :] for k in range(FFN_K)]
        wu = [wu_ref[k:k + 1, :] for k in range(FFN_K)]

        for s in range(ext // sub):
            r = s * sub
            if r < T:
                gate, up, dav = gc_ref[r:r + sub, :], uc_ref[r:r + sub, :], da_ref[r:r + sub, :]
            else:
                gate, up, dav = gca_ref[...], uca_ref[...], jnp.where(c < NB - 1, daa_ref[...], 0.0)
            sg = _sig(gate)
            dg_scr[r:r + sub, :] = dav * up * (sg * (1.0 + gate * (1.0 - sg)))
            du_scr[r:r + sub, :] = dav * gate * sg

        def back(d_scr, x_ref, w, df_ref, dw_ref, db_ref):
            accs = [jnp.zeros((sub, UPW), F32) for _ in range(FFN_K + 1)]
            for s in range(T // (2 * sub)):
                pieces = []
                for r in (2 * s * sub, (2 * s + 1) * sub):
                    xv = x_ref[r:r + sub, :]
                    df = jnp.zeros((sub, UPW), F32)
                    for k in range(FFN_K):
                        off = FFN_K - 1 - k
                        dj = d_scr[r + off:r + off + sub, :]
                        df = df + w[k] * dj
                        accs[k] = accs[k] + xv * dj
                        if off == 0:
                            accs[FFN_K] = accs[FFN_K] + dj
                    pieces.append(df)
                df_ref[2 * s * sub:2 * (s + 1) * sub, :] = jnp.concatenate(pieces, axis=0).astype(BF16)
            dwp = _stack_rows([jnp.sum(a, axis=0, keepdims=True) for a in accs[:FFN_K]])
            dbp = jnp.sum(accs[FFN_K], axis=0, keepdims=True)

            @pl.when(c == 0)
            def _():
                dw_ref[...] = dwp
                db_ref[...] = dbp

            @pl.when(c > 0)
            def _():
                dw_ref[...] += dwp
                db_ref[...] += dbp

        back(dg_scr, fg_ref, wg, dfg_ref, dwg_ref, dbg_ref)
        back(du_scr, fu_ref, wu, dfu_ref, dwu_ref, dbu_ref)

    cur, _, after, wspec = _ffn_specs()
    slab = jax.ShapeDtypeStruct((NPAIR, L, UPW), BF16)
    wsh = jax.ShapeDtypeStruct((NPAIR, FFN_K, UPW), F32)
    bsh = jax.ShapeDtypeStruct((NPAIR, 1, UPW), F32)
    return pl.pallas_call(
        body, name=name, out_shape=(slab, slab, wsh, wsh, bsh, bsh), grid=(NPAIR, NB),
        in_specs=[cur(0), cur(NPAIR), cur(0), after(0), cur(0), after(0), cur(0), after(0),
                  wspec(0, FFN_K), wspec(NPAIR, FFN_K)],
        out_specs=(cur(0), cur(0), wspec(0, FFN_K), wspec(0, FFN_K), wspec(0, 1), wspec(0, 1)),
        scratch_shapes=[pltpu.VMEM((ext, UPW), F32), pltpu.VMEM((ext, UPW), F32)])(
            f, f, gc, gc, uc, uc, da, da, w, w)


def _mesh_pos():
    return lax.axis_index("x"), lax.axis_index("y"), lax.axis_index("c")


def _peer(pos, k):
    x, y, c = pos
    px = 1 - x if k & 4 else x
    py = 1 - y if k & 2 else y
    pc = 1 - c if k & 1 else c
    return (px, py, pc), 4 * px + 2 * py + pc


_CHIP_FLIPS = (4, 2, 6)
_HBM = pl.BlockSpec(memory_space=pl.ANY)


def all_gather(shards, name):
    nt = len(shards)

    def body(*refs):
        ins, outs = refs[:nt], refs[nt:2 * nt]
        send, recv, local = refs[2 * nt:]
        pos = _mesh_pos()
        me = 4 * pos[0] + 2 * pos[1] + pos[2]
        sib, sib_id = _peer(pos, 1)

        def copy(t, k, block_id, to, src=None):
            dst = outs[t].at[block_id]
            return pltpu.make_async_remote_copy(
                src_ref=dst if src is None else src, dst_ref=dst, send_sem=send.at[t, k], recv_sem=recv.at[t, k],
                device_id=to, device_id_type=MESH)

        locals_ = [pltpu.make_async_copy(ins[t], outs[t].at[me], local.at[t]) for t in range(nt)]
        for cp in locals_:
            cp.start()
        started = []
        for j, flip in enumerate(_CHIP_FLIPS):
            for t in range(nt):
                started.append(copy(t, 1 + j, me, _peer(pos, flip)[0], src=ins[t]))
        for t in range(nt):
            started.append(copy(t, 0, me, sib, src=ins[t]))
        for cp in started:
            cp.start()
        for j, flip in enumerate(_CHIP_FLIPS):
            _, pid = _peer(pos, flip)
            for t in range(nt):
                copy(t, 1 + j, pid, sib).wait_recv()
                fwd = copy(t, 4 + j, pid, sib)
                fwd.start()
                started.append(fwd)
        for t in range(nt):
            copy(t, 0, sib_id, sib).wait_recv()
        for j, flip in enumerate(_CHIP_FLIPS):
            _, pid = _peer(pos, flip | 1)
            for t in range(nt):
                copy(t, 4 + j, pid, sib).wait_recv()
        for cp in started:
            cp.wait_send()
        for cp in locals_:
            cp.wait()

    return pl.pallas_call(
        body, name=name,
        out_shape=tuple(jax.ShapeDtypeStruct((NDEV,) + s.shape, s.dtype) for s in shards),
        in_specs=[_HBM] * nt, out_specs=tuple([_HBM] * nt),
        scratch_shapes=[pltpu.SemaphoreType.DMA((nt, NDEV - 1)), pltpu.SemaphoreType.DMA((nt, NDEV - 1)),
                        pltpu.SemaphoreType.DMA((nt,))])(*shards)


_SPLIT = dict(has_side_effects=pltpu.SideEffectType.DATAFLOW_SIDE_EFFECTING)
_SEM = pl.BlockSpec(memory_space=pltpu.SEMAPHORE)


def _exchange_start(name, srcs, lands, copies, after):
    ns, nl = len(srcs), len(lands)
    ncopy = len(copies(None, [None] * ns, [None] * nl))

    def body(*refs):
        src_refs, land_refs = refs[:ns], refs[ns:ns + nl]
        send, recv = refs[ns + nl + len(after):ns + nl + len(after) + 2]
        token = refs[-1]
        for i, (src, dst, peer) in enumerate(copies(_mesh_pos(), src_refs, land_refs)):
            pltpu.make_async_remote_copy(src_ref=src, dst_ref=dst, send_sem=send.at[i], recv_sem=recv.at[i],
                                         device_id=peer, device_id_type=MESH).start()
        token[...] = jnp.zeros_like(token)

    hbm = lambda a: pltpu.HBM(a.shape, a.dtype)
    out = pl.pallas_call(
        body, name=name,
        out_shape=(pltpu.SemaphoreType.DMA((ncopy,)), pltpu.SemaphoreType.DMA((ncopy,)),
                   *[hbm(a) for a in srcs], *[hbm(a) for a in lands], jax.ShapeDtypeStruct((8, 128), F32)),
        in_specs=[pl.BlockSpec(memory_space=pltpu.HBM)] * (ns + nl) + [_HBM] * len(after),
        out_specs=(_SEM, _SEM, *[pl.BlockSpec(memory_space=pltpu.HBM)] * (ns + nl),
                   pl.BlockSpec(memory_space=pltpu.VMEM)),
        input_output_aliases={i: 2 + i for i in range(ns + nl)},
        compiler_params=pltpu.CompilerParams(**_SPLIT))(
            *[pltpu.with_memory_space_constraint(a, pltpu.HBM) for a in list(srcs) + list(lands)], *after)
    return out[0], out[1], list(out[2:2 + ns]), list(out[2 + ns:2 + ns + nl]), out[-1]


def _exchange_wait(name, send, recv, srcs, lands, copies, after):
    ns, nl = len(srcs), len(lands)

    def body(*refs):
        src_refs, land_refs = refs[:ns], refs[ns:ns + nl]
        send_ref, recv_ref = refs[ns + nl:ns + nl + 2]
        pos = _mesh_pos()
        for i, (src, dst, peer) in enumerate(copies(pos, src_refs, land_refs, arriving=True)):
            cp = pltpu.make_async_remote_copy(src_ref=src, dst_ref=dst, send_sem=send_ref.at[i], recv_sem=recv_ref.at[i],
                                              device_id=peer, device_id_type=MESH)
            cp.wait_send()
            cp.wait_recv()

    hbm = lambda a: pltpu.HBM(a.shape, a.dtype)
    out = pl.pallas_call(
        body, name=name, out_shape=tuple(hbm(a) for a in list(srcs) + list(lands)),
        in_specs=[pl.BlockSpec(memory_space=pltpu.HBM)] * (ns + nl) + [_SEM, _SEM] + [_HBM] * len(after),
        out_specs=tuple([pl.BlockSpec(memory_space=pltpu.HBM)] * (ns + nl)),
        input_output_aliases={i: i for i in range(ns + nl)},
        compiler_params=pltpu.CompilerParams(**_SPLIT))(*srcs, *lands, send, recv, *after)
    return list(out[:ns]), list(out[ns:])


def _gather_copies(pos, srcs, lands, arriving=False):
    if pos is None:
        return [None] * (len(srcs) * (NDEV - 1))
    me = 4 * pos[0] + 2 * pos[1] + pos[2]
    out = []
    for src, land in zip(srcs, lands):
        for k in range(1, NDEV):
            peer, pid = _peer(pos, k)
            out.append((src, land.at[pid if arriving else me], peer))
    return out


def _scatter_copies(pos, srcs, lands, arriving=False):
    if pos is None:
        return [None] * (len(srcs) * (NDEV - 1))
    out = []
    for src, land in zip(srcs, lands):
        for k in range(1, NDEV):
            peer, pid = _peer(pos, k)
            out.append((src.at[pid], land.at[k - 1], peer))
    return out


def gather_start(name, shards, lands, after=()):
    return _exchange_start(name, shards, lands, _gather_copies, list(after))


def gather_wait(name, handle, after=()):
    send, recv, srcs, lands, _ = handle
    return _exchange_wait(name, send, recv, srcs, lands, _gather_copies, list(after))


def scatter_start(name, grads, after=()):
    lands = [pltpu.with_memory_space_constraint(lax.empty((NDEV - 1,) + g.shape[1:], g.dtype), pltpu.HBM)
             for g in grads]
    return _exchange_start(name, grads, lands, _scatter_copies, list(after))


def scatter_wait(name, handle, after=()):
    send, recv, srcs, lands, _ = handle
    return _exchange_wait(name, send, recv, srcs, lands, _scatter_copies, list(after))


def all_reduce_small(part, name):
    rows = part.shape[0]

    def body(x_ref, o_ref, buf, send, recv):
        pos = _mesh_pos()
        me = 4 * pos[0] + 2 * pos[1] + pos[2]
        sends = []
        for k in range(1, NDEV):
            peer, _ = _peer(pos, k)
            sends.append(pltpu.make_async_remote_copy(
                src_ref=x_ref, dst_ref=buf.at[me], send_sem=send.at[k - 1], recv_sem=recv.at[k - 1],
                device_id=peer, device_id_type=MESH))
        for cp in sends:
            cp.start()
        buf[me] = x_ref[...]
        for k in range(1, NDEV):
            peer, pid = _peer(pos, k)
            pltpu.make_async_remote_copy(
                src_ref=x_ref, dst_ref=buf.at[pid], send_sem=send.at[k - 1], recv_sem=recv.at[k - 1],
                device_id=peer, device_id_type=MESH).wait_recv()
        for cp in sends:
            cp.wait_send()
        acc = buf[0]
        for d in range(1, NDEV):
            acc = acc + buf[d]
        o_ref[...] = acc

    return pl.pallas_call(
        body, name=name, out_shape=jax.ShapeDtypeStruct((rows, 128), F32),
        in_specs=[pl.BlockSpec(memory_space=pltpu.VMEM)], out_specs=pl.BlockSpec(memory_space=pltpu.VMEM),
        scratch_shapes=[pltpu.VMEM((NDEV, rows, 128), F32), pltpu.SemaphoreType.DMA((NDEV - 1,)),
                        pltpu.SemaphoreType.DMA((NDEV - 1,))])(part)


def _adamw_update(g, w_ref, m_ref, v_ref, g_ref, d_ref, nm_ref, nv_ref):
    g_ref[...] = g
    nm = ADAM_B1 * m_ref[...] + (1.0 - ADAM_B1) * g
    nv = ADAM_B2 * v_ref[...] + (1.0 - ADAM_B2) * (g * g)
    nm_ref[...] = nm
    nv_ref[...] = nv
    m_hat = nm / (1.0 - ADAM_B1 ** ADAM_STEP)
    v_hat = nv / (1.0 - ADAM_B2 ** ADAM_STEP)
    d_ref[...] = -ADAM_LR * (m_hat / (jnp.sqrt(v_hat) + ADAM_EPS) + ADAM_WD * w_ref[...])


def adamw_big(own, lands, w, m, v, me, tr, name):
    _, r, c = own[0].shape
    nt = r // tr

    def body(me_ref, *refs):
        ins, (w_ref, m_ref, v_ref), outs = refs[:2 * DEPTH], refs[2 * DEPTH:2 * DEPTH + 3], refs[2 * DEPTH + 3:]
        layer = pl.program_id(0)
        for l in range(DEPTH):
            @pl.when(layer == l)
            def _(l=l):
                g = ins[2 * l][...].astype(F32)
                for s in range(NDEV - 1):
                    g = g + ins[2 * l + 1][s].astype(F32)
                _adamw_update(g, w_ref, m_ref, v_ref, *outs)

    pick = lambda l: (lambda layer, i, me_ref: jnp.where(layer == l, i, 0))
    in_specs = []
    for l in range(DEPTH):
        in_specs.append(pl.BlockSpec((None, tr, c), lambda layer, i, me_ref, f=pick(l): (me_ref[0], f(layer, i, me_ref), 0)))
        in_specs.append(pl.BlockSpec((NDEV - 1, tr, c), lambda layer, i, me_ref, f=pick(l): (0, f(layer, i, me_ref), 0)))
    blk = pl.BlockSpec((tr, c), lambda layer, i, me_ref: (layer * nt + i, 0))
    sh = jax.ShapeDtypeStruct((DEPTH * r, c), F32)
    args = [a for l in range(DEPTH) for a in (own[l], lands[l])]
    return pl.pallas_call(
        body, name=name, out_shape=(sh, sh, sh, sh),
        grid_spec=pltpu.PrefetchScalarGridSpec(
            num_scalar_prefetch=1, grid=(DEPTH, nt), in_specs=in_specs + [blk, blk, blk],
            out_specs=(blk, blk, blk, blk)),
        compiler_params=_params(VMEM_BIG))(me, *args, w, m, v)


def adamw(parts, w, m, v, tr, name):
    ns, r, c = parts.shape

    def body(p_ref, w_ref, m_ref, v_ref, g_ref, d_ref, nm_ref, nv_ref):
        g = p_ref[0].astype(F32)
        for s in range(1, ns):
            g = g + p_ref[s].astype(F32)
        _adamw_update(g, w_ref, m_ref, v_ref, g_ref, d_ref, nm_ref, nv_ref)

    blk = pl.BlockSpec((tr, c), lambda i: (i, 0))
    sh = jax.ShapeDtypeStruct((r, c), F32)
    return pl.pallas_call(
        body, name=name, out_shape=(sh, sh, sh, sh), grid=(r // tr,),
        in_specs=[pl.BlockSpec((ns, tr, c), lambda i: (0, i, 0)), blk, blk, blk],
        out_specs=(blk, blk, blk, blk), compiler_params=_params(VMEM_BIG))(parts, w, m, v)


_SPLITS = (NQ * HD, NKV * HD, NKV * HD, CV, CV, RH * HD, RH * HD, RH * HD, RH * HD)
_OFFS = np.concatenate([[0], np.cumsum(_SPLITS)]).tolist()


def _heads(x, nh):
    return x.reshape(L, nh, HD).transpose(1, 0, 2)


def _unheads(x):
    return x.transpose(1, 0, 2).reshape(L, x.shape[0] * HD)


def local_step(x, target, P, hooks):
    tb = _ret_tables()
    h = jnp.concatenate([jnp.zeros((PAD, D), F32), P["meta"], x], axis=0)
    stash = []
    for l in range(DEPTH):
        s = {"h": h}
        s["w_in"], s["cv_pw"], s["w_out"] = hooks["mixer_w"](l, h)
        u = s["u"] = rms_fwd(h, P["norm_mix_g"][l], BF16, f"rms_mix_{l}", after=hooks["first_after"] if l == 0 else None)
        proj = s["proj"] = mm_nt(f"in_proj_{l}", u, s["w_in"], 768)
        s.update(q=_heads(proj[:, :_OFFS[1]], NQ), k=_heads(proj[:, _OFFS[1]:_OFFS[2]], NKV),
                 v=_heads(proj[:, _OFFS[2]:_OFFS[3]], NKV))
        o = attn_fwd(s["q"], s["k"], s["v"], P["q_norm_g"][l], P["k_norm_g"][l], P["attn_sinks"][l], f"attn_fwd_{l}")
        s["o"] = _unheads(o)
        y_att = rms_fwd(s["o"], P["attn_out_g"][l], BF16, f"rms_att_{l}")
        y_cv = conv_fwd(proj, P["cv_dw_w"][l], P["cv_dw_b"][l], P["cv_ln_g"][l], P["cv_ln_b"][l], s["cv_pw"],
                        P["cv_out_g"][l], f"conv_fwd_{l}")
        y_ret, s["states"] = ret_fwd(proj, P["ret_gn_g"][l], tb, f"ret_fwd_{l}")
        ycat = s["ycat"] = jnp.concatenate([y_att, y_cv.astype(BF16), y_ret.astype(BF16)], axis=1)
        h = mm_nn(f"out_proj_{l}", ycat, s["w_out"], 512, res=h)
        s["h1"] = h
        s["ffn_up"], s["ffn_down"] = hooks["ffn_w"](l, h)
        u2 = s["u2"] = rms_fwd(h, P["norm_ffn_g"][l], BF16, f"rms_ffn_{l}")
        s["f"] = ffn_up_fwd(f"ffn_up_{l}", u2, s["ffn_up"])
        a, s["gc"], s["uc"] = ffn_act_fwd(s["f"], P["ffn_dw_w"][l], P["ffn_dw_b"][l], f"ffn_act_fwd_{l}")
        s["a"] = a
        h = ffn_down_fwd(f"ffn_down_{l}", a, s["ffn_down"], h)
        stash.append(s)

    dh, dhm, loss = loss_head(h, target)
    G = {n: [None] * DEPTH for n in ("norm_mix_g", "q_norm_g", "k_norm_g", "attn_sinks", "attn_out_g", "cv_dw_w",
                                     "cv_dw_b", "cv_ln_g", "cv_ln_b", "cv_out_g", "ret_gn_g", "norm_ffn_g",
                                     "ffn_dw_w", "ffn_dw_b")}
    for l in reversed(range(DEPTH)):
        s = stash[l]
        da = ffn_down_dx(f"ffn_down_dx_{l}", dhm, s["ffn_down"])
        dw_down = ffn_down_dw(f"ffn_down_dw_{l}", s["a"], dhm)
        dfg, dfu, dwg, dwu, dbg, dbu = ffn_act_bwd(s["f"], s["gc"], s["uc"], da, P["ffn_dw_w"][l], f"ffn_act_bwd_{l}")
        G["ffn_dw_w"][l] = jnp.concatenate([dwg, dwu], axis=0)
        G["ffn_dw_b"][l] = jnp.concatenate([dbg, dbu], axis=0)
        du2 = ffn_up_dx(f"ffn_up_dx_{l}", dfg, dfu, s["ffn_up"])
        dw_up = ffn_up_dw(f"ffn_up_dw_{l}", s["u2"], dfg, dfu)
        after = hooks["ffn_grads"](l, dw_down, dw_up)
        dh, dhm, G["norm_ffn_g"][l] = rms_bwd(s["h1"], P["norm_ffn_g"][l], du2, dh, f"rms_ffn_bwd_{l}", after=after)
        dycat = mm_nt(f"out_proj_dx_{l}", dhm, s["w_out"], 512)
        dw_out = mm_ta(f"out_proj_dw_{l}", s["ycat"], dhm, 512)
        do, _, G["attn_out_g"][l] = rms_bwd(s["o"], P["attn_out_g"][l], dycat, None, f"rms_att_bwd_{l}")
        (dq, dk_c, dk_p, dk_m, dv_c, dv_p, dv_m, G["q_norm_g"][l], G["k_norm_g"][l], dsk) = attn_bwd(
            s["q"], s["k"], s["v"], _heads(do, NQ), P["q_norm_g"][l], P["k_norm_g"][l], P["attn_sinks"][l],
            f"attn_bwd_{l}")
        G["attn_sinks"][l] = dsk[:, :, 0].reshape(NQ)
        shift = lambda z: jnp.concatenate([z[:, T:], jnp.zeros((NKV, T, HD), F32)], axis=1)
        dk = (dk_c + shift(dk_p)).at[:, :T].add(dk_m)
        dv = (dv_c + shift(dv_p)).at[:, :T].add(dv_m)
        (dca, dcb, G["cv_dw_w"][l], G["cv_dw_b"][l], G["cv_ln_g"][l], G["cv_ln_b"][l], G["cv_out_g"][l],
         dpw) = conv_bwd(s["proj"], dycat, P["cv_dw_w"][l], P["cv_dw_b"][l], P["cv_ln_g"][l],
                         P["cv_ln_b"][l], s["cv_pw"], P["cv_out_g"][l], f"conv_bwd_{l}")
        dret, G["ret_gn_g"][l] = ret_bwd(s["proj"], dycat, s["states"], P["ret_gn_g"][l], tb, f"ret_bwd_{l}")
        dproj = jnp.concatenate([_unheads(dq), _unheads(dk), _unheads(dv), dca, dcb, dret], axis=1).astype(BF16)
        du = mm_nn(f"in_proj_dx_{l}", dproj, s["w_in"], 512)
        dw_in = mm_ta(f"in_proj_dw_{l}", dproj, s["u"], 768)
        after = hooks["mixer_grads"](l, dw_out, dpw, dw_in)
        dh, dhm, G["norm_mix_g"][l] = rms_bwd(s["h"], P["norm_mix_g"][l], du, dh, f"rms_mix_bwd_{l}", after=after)
    return loss[0, 0], dh[T:], dh[PAD:T], G


_SMALL = ("meta", "norm_mix_g", "q_norm_g", "k_norm_g", "attn_sinks", "attn_out_g", "cv_dw_w", "cv_dw_b", "cv_ln_g",
          "cv_ln_b", "cv_out_g", "ret_gn_g", "norm_ffn_g", "ffn_dw_w", "ffn_dw_b")
_BIG = ("w_in", "cv_pw", "w_out", "ffn_up", "ffn_down")
_TRANSPOSED = ("w_in", "ffn_up")
_ORDER = ("meta", "norm_mix_g", "w_in", "q_norm_g", "k_norm_g", "attn_sinks", "attn_out_g", "cv_dw_w", "cv_dw_b",
          "cv_ln_g", "cv_ln_b", "cv_pw", "cv_out_g", "ret_gn_g", "w_out", "norm_ffn_g", "ffn_up", "ffn_dw_w",
          "ffn_dw_b", "ffn_down")
_SMALL_SHARDED = {"meta": D, "cv_dw_w": CV, "ffn_dw_w": 2 * D_FF}


def _pack(arrs):
    flat = jnp.concatenate([a.reshape(-1) for a in arrs])
    n = flat.shape[0]
    rows = -(-n // 1024) * 8
    return jnp.pad(flat, (0, rows * 128 - n)).reshape(rows, 128)


def _unpack(packed, shapes):
    flat = packed.reshape(-1)
    out, off = [], 0
    for s in shapes:
        n = int(np.prod(s))
        out.append(flat[off:off + n].reshape(s))
        off += n
    return out


def kernel(x, meta, norm_mix_g, w_in, q_norm_g, k_norm_g, attn_sinks, attn_out_g, cv_dw_w, cv_dw_b, cv_ln_g, cv_ln_b, cv_pw, cv_out_g, ret_gn_g, w_out, norm_ffn_g, ffn_up, ffn_dw_w, ffn_dw_b, ffn_down, loss_target, m_meta, m_norm_mix_g, m_w_in, m_q_norm_g, m_k_norm_g, m_attn_sinks, m_attn_out_g, m_cv_dw_w, m_cv_dw_b, m_cv_ln_g, m_cv_ln_b, m_cv_pw, m_cv_out_g, m_ret_gn_g, m_w_out, m_norm_ffn_g, m_ffn_up, m_ffn_dw_w, m_ffn_dw_b, m_ffn_down, v_meta, v_norm_mix_g, v_w_in, v_q_norm_g, v_k_norm_g, v_attn_sinks, v_attn_out_g, v_cv_dw_w, v_cv_dw_b, v_cv_ln_g, v_cv_ln_b, v_cv_pw, v_cv_out_g, v_ret_gn_g, v_w_out, v_norm_ffn_g, v_ffn_up, v_ffn_dw_w, v_ffn_dw_b, v_ffn_down):
    W = dict(meta=meta, norm_mix_g=norm_mix_g, w_in=w_in, q_norm_g=q_norm_g, k_norm_g=k_norm_g,
             attn_sinks=attn_sinks, attn_out_g=attn_out_g, cv_dw_w=cv_dw_w, cv_dw_b=cv_dw_b, cv_ln_g=cv_ln_g,
             cv_ln_b=cv_ln_b, cv_pw=cv_pw, cv_out_g=cv_out_g, ret_gn_g=ret_gn_g, w_out=w_out,
             norm_ffn_g=norm_ffn_g, ffn_up=ffn_up, ffn_dw_w=ffn_dw_w, ffn_dw_b=ffn_dw_b, ffn_down=ffn_down)
    M = dict(meta=m_meta, norm_mix_g=m_norm_mix_g, w_in=m_w_in, q_norm_g=m_q_norm_g, k_norm_g=m_k_norm_g,
             attn_sinks=m_attn_sinks, attn_out_g=m_attn_out_g, cv_dw_w=m_cv_dw_w, cv_dw_b=m_cv_dw_b,
             cv_ln_g=m_cv_ln_g, cv_ln_b=m_cv_ln_b, cv_pw=m_cv_pw, cv_out_g=m_cv_out_g, ret_gn_g=m_ret_gn_g,
             w_out=m_w_out, norm_ffn_g=m_norm_ffn_g, ffn_up=m_ffn_up, ffn_dw_w=m_ffn_dw_w, ffn_dw_b=m_ffn_dw_b,
             ffn_down=m_ffn_down)
    V = dict(meta=v_meta, norm_mix_g=v_norm_mix_g, w_in=v_w_in, q_norm_g=v_q_norm_g, k_norm_g=v_k_norm_g,
             attn_sinks=v_attn_sinks, attn_out_g=v_attn_out_g, cv_dw_w=v_cv_dw_w, cv_dw_b=v_cv_dw_b,
             cv_ln_g=v_cv_ln_g, cv_ln_b=v_cv_ln_b, cv_pw=v_cv_pw, cv_out_g=v_cv_out_g, ret_gn_g=v_ret_gn_g,
             w_out=v_w_out, norm_ffn_g=v_norm_ffn_g, ffn_up=v_ffn_up, ffn_dw_w=v_ffn_dw_w, ffn_dw_b=v_ffn_dw_b,
             ffn_down=v_ffn_down)
    me = 4 * lax.axis_index("x") + 2 * lax.axis_index("y") + lax.axis_index("c")
    for n in _TRANSPOSED:
        W[n], M[n], V[n] = (a.transpose(0, 2, 1) for a in (W[n], M[n], V[n]))

    sh = {n: [W[n][l].astype(BF16) for l in range(DEPTH)] for n in _BIG}
    mix = lambda l: [sh["w_in"][l], sh["cv_pw"][l], sh["w_out"][l]]
    ffn = lambda l: [sh["ffn_up"][l], sh["ffn_down"][l]]
    first = all_gather(mix(0) + [meta, cv_dw_w, ffn_dw_w], "gather_first")
    g_meta, g_cdw, g_fdw = first[3:6]

    def landing(shards):
        return [lax.dynamic_update_slice(lax.empty((NDEV,) + s.shape, s.dtype), s[None], (me,) + (0,) * s.ndim)
                for s in shards]

    gathers = {("ffn", 0): gather_start("gather_ffn0_start", ffn(0), landing(ffn(0)), after=[first[0]])}
    gathers["mix", 1] = gather_start("gather_mix1_start", mix(1), landing(mix(1)), after=[gathers["ffn", 0][4]])
    gathers["ffn", 1] = gather_start("gather_ffn1_start", ffn(1), landing(ffn(1)), after=[gathers["mix", 1][4]])

    def mixer_w(l, h):
        g_in, g_pw, g_out = first[0:3] if l == 0 else gather_wait(f"gather_mix{l}_wait", gathers["mix", l], after=[h])[1]
        return g_in.reshape(IN_W, D), g_pw.reshape(CV, CV), g_out.reshape(D, D)

    def ffn_w(l, h1):
        return gather_wait(f"gather_ffn{l}_wait", gathers["ffn", l], after=[h1])[1]

    scatters = {}

    def ffn_grads(l, dw_down, dw_up):
        scatters["ffn", l] = scatter_start(f"scatter_ffn{l}_start", [dw_up, dw_down.reshape(NDEV, DNR, D)])
        return scatters["ffn", l][4]

    def mixer_grads(l, dw_out, dpw, dw_in):
        grads = [dw_in.reshape(NDEV, IN_W // NDEV, D), dpw.astype(BF16).reshape(NDEV, CV // NDEV, CV),
                 dw_out.reshape(NDEV, D // NDEV, D)]
        scatters["mix", l] = scatter_start(f"scatter_mix{l}_start", grads)
        return scatters["mix", l][4]

    P = dict(
        meta=g_meta.transpose(1, 0, 2).reshape(N_META, D),
        cv_dw_w=g_cdw.transpose(1, 2, 0, 3).reshape(DEPTH, CONV_K, CV),
        ffn_dw_w=g_fdw.transpose(1, 0, 2, 3),
        ffn_dw_b=ffn_dw_b.reshape(DEPTH, NDEV, 1, UPW),
        attn_sinks=attn_sinks,
    )
    for n in ("norm_mix_g", "q_norm_g", "k_norm_g", "attn_out_g", "cv_dw_b", "cv_ln_g", "cv_ln_b", "cv_out_g",
              "ret_gn_g", "norm_ffn_g"):
        P[n] = W[n].reshape(DEPTH, 1, -1)

    hooks = dict(mixer_w=mixer_w, ffn_w=ffn_w, ffn_grads=ffn_grads, mixer_grads=mixer_grads,
                 first_after=gathers["ffn", 1][4])
    loss_part, dx, dmeta, G = local_step(x[0], loss_target[0], P, hooks)
    loss = lax.psum(loss_part, ("x", "y", "c"))

    small_full = {
        "meta": dmeta,
        "cv_dw_w": jnp.stack(G["cv_dw_w"]),
        "ffn_dw_w": jnp.stack([g.transpose(1, 0, 2).reshape(FFN_K, 2 * D_FF) for g in G["ffn_dw_w"]]),
        "ffn_dw_b": jnp.stack([g.reshape(2 * D_FF) for g in G["ffn_dw_b"]]),
        "attn_sinks": jnp.stack(G["attn_sinks"]),
    }
    for n in _SMALL:
        if n not in small_full:
            small_full[n] = jnp.stack([g.reshape(-1) for g in G[n]])
    shapes = [small_full[n].shape for n in _SMALL]
    summed = _unpack(all_reduce_small(_pack([small_full[n] for n in _SMALL]), "reduce_small"), shapes)
    small_g = {}
    for n, g in zip(_SMALL, summed):
        if n in _SMALL_SHARDED:
            width = _SMALL_SHARDED[n] // NDEV
            g = lax.dynamic_slice_in_dim(g, me * width, width, axis=g.ndim - 1)
        small_g[n] = g

    out = {}
    tiles = {"w_in": 144, "cv_pw": 32, "w_out": 128, "ffn_up": 176, "ffn_down": 176}
    me1 = me.astype(jnp.int32).reshape(1)
    own, lands = {n: [None] * DEPTH for n in _BIG}, {n: [None] * DEPTH for n in _BIG}

    def arrived(kind, names, after):
        for l in range(DEPTH):
            srcs, got = scatter_wait(f"scatter_{kind}{l}_wait", scatters[kind, l], after=after)
            for n, s_, g_ in zip(names, srcs, got):
                own[n][l], lands[n][l] = s_, g_

    def update(names):
        for n in names:
            shard = W[n].shape
            rows, cols = shard[0] * shard[1], shard[2]
            res = adamw_big(own[n], lands[n], W[n].reshape(rows, cols), M[n].reshape(rows, cols),
                            V[n].reshape(rows, cols), me1, tiles[n], f"adamw_{n}")
            out[n] = [r.reshape(shard) for r in res]

    arrived("ffn", ("ffn_up", "ffn_down"), [dx])
    update(("ffn_up", "ffn_down"))
    sshapes = [W[n].shape for n in _SMALL]
    packed_g = _pack([small_g[n] for n in _SMALL])
    res = adamw(packed_g[None], _pack([W[n] for n in _SMALL]), _pack([M[n] for n in _SMALL]),
                _pack([V[n] for n in _SMALL]), packed_g.shape[0], "adamw_small")
    for i, r in enumerate(res):
        for n, a in zip(_SMALL, _unpack(r, sshapes)):
            out.setdefault(n, [None] * 4)[i] = a
    arrived("mix", ("w_in", "cv_pw", "w_out"), [out["ffn_down"][0], res[0]])
    update(("w_in", "cv_pw", "w_out"))
    for n in _TRANSPOSED:
        out[n] = [r.transpose(0, 2, 1) for r in out[n]]

    return (loss, dx[None], *[out[n][0] for n in _ORDER], *[out[n][1] for n in _ORDER],
            *[out[n][2] for n in _ORDER], *[out[n][3] for n in _ORDER])
```

```python
import functools
import math

import numpy as np
import jax
import jax.numpy as jnp
from jax import lax
from jax.experimental import pallas as pl
from jax.experimental.pallas import tpu as pltpu

F32 = jnp.float32
BF16 = jnp.bfloat16

D = 1024
SEQ = 2048
DEPTH = 2
T = 128
L = SEQ + T
NB = L // T
N_META = 16
PAD = T - N_META
HD = 64
NQ = 8
NKV = 2
GQA = NQ // NKV
CV = 256
CONV_K = 31
RH = 4
D_FF = 2816
FFN_K = 3
IN_W = 2304
RMS_EPS = 1e-6
LN_EPS = 1e-5
NEG = -1e30
NDEV = 8
UPW = 2 * D_FF // NDEV
DNR = D_FF // NDEV
NPAIR = NDEV // 2

ADAM_LR, ADAM_B1, ADAM_B2, ADAM_EPS, ADAM_WD, ADAM_STEP = 0.001, 0.9, 0.999, 1e-08, 0.01, 10

VMEM_BIG = 56 * 1024 * 1024

MESH = pl.DeviceIdType.MESH


def _params(vmem=None):
    return pltpu.CompilerParams(vmem_limit_bytes=vmem) if vmem else None


def _dot(a, b, nt=False):
    return lax.dot_general(a, b, (((1,), (1 if nt else 0,)), ((), ())), preferred_element_type=F32)


def _sig(x):
    return 1.0 / (1.0 + jnp.exp(-x))


def _bf(x):
    return x.astype(BF16)


def _stack_rows(rows):
    idx = lax.broadcasted_iota(jnp.int32, (len(rows), rows[0].shape[1]), 0)
    out = jnp.zeros((len(rows), rows[0].shape[1]), F32)
    for r, v in enumerate(rows):
        out = jnp.where(idx == r, v, out)
    return out


def rms_fwd(x, g, out_dtype, name, after=None):
    n, w = x.shape

    def body(x_ref, g_ref, *rest):
        o_ref = rest[-1]
        xv = x_ref[...]
        r = lax.rsqrt(jnp.mean(xv * xv, axis=-1, keepdims=True) + RMS_EPS)
        o_ref[...] = (xv * r * g_ref[...]).astype(o_ref.dtype)

    deps = [] if after is None else [after]
    return pl.pallas_call(
        body, name=name, out_shape=jax.ShapeDtypeStruct((n, w), out_dtype), grid=(n // T,),
        in_specs=[pl.BlockSpec((T, w), lambda i: (i, 0)), pl.BlockSpec((1, w), lambda i: (0, 0))] + [_HBM] * len(deps),
        out_specs=pl.BlockSpec((T, w), lambda i: (i, 0)))(x, g, *deps)


def rms_bwd(x, g, dy, dres, name, after=None, dy_col=0, last=False):
    n, w = x.shape
    has_res = dres is not None
    deps = [] if after is None else [after]

    def body(x_ref, g_ref, dy_ref, *rest):
        rest = rest[:len(rest) - 3 - len(deps)] + rest[len(rest) - 3:]
        if has_res:
            dres_ref, dx_ref, dxm_ref, dg_ref = rest
        else:
            dx_ref, dxm_ref, dg_ref = rest
        i = pl.program_id(0)
        xv = x_ref[...]
        r = lax.rsqrt(jnp.mean(xv * xv, axis=-1, keepdims=True) + RMS_EPS)
        xh = xv * r
        dyv = dy_ref[...]
        dxh = dyv * g_ref[...]
        dx = r * (dxh - xh * jnp.mean(dxh * xh, axis=-1, keepdims=True))
        if has_res:
            dx = dx + dres_ref[...]
        dx_ref[...] = dx
        if last:
            @pl.when(i == 0)
            def _():
                dxm_ref[...] = dx
        else:
            rows = i * T + lax.broadcasted_iota(jnp.int32, (T, 1), 0)
            dxm_ref[...] = jnp.where(rows >= PAD, dx, 0.0).astype(BF16)
        part = jnp.sum(dyv * xh, axis=0, keepdims=True)

        @pl.when(i == 0)
        def _():
            dg_ref[...] = part

        @pl.when(i > 0)
        def _():
            dg_ref[...] += part

    row = pl.BlockSpec((T, w), lambda i: (i, 0))
    vec = pl.BlockSpec((1, w), lambda i: (0, 0))
    ins = [x, g, dy] + ([dres] if has_res else []) + deps
    if last:
        out_shape = (jax.ShapeDtypeStruct((n - T, w), F32), jax.ShapeDtypeStruct((T, w), F32))
        out_specs = (pl.BlockSpec((T, w), lambda i: (jnp.maximum(i - 1, 0), 0)), pl.BlockSpec((T, w), lambda i: (0, 0)))
    else:
        out_shape = (jax.ShapeDtypeStruct((n, w), F32), jax.ShapeDtypeStruct((n, w), BF16))
        out_specs = (row, row)
    return pl.pallas_call(
        body, name=name, out_shape=(*out_shape, jax.ShapeDtypeStruct((1, w), F32)), grid=(n // T,),
        in_specs=[row, vec, pl.BlockSpec((T, w), lambda i: (i, dy_col))] + ([row] if has_res else [])
        + [_HBM] * len(deps),
        out_specs=(*out_specs, vec))(*ins)


def loss_head(h, target):
    def body(h_ref, t_ref, dh_ref, dhm_ref, loss_ref):
        n = pl.program_id(0)
        e = jnp.where(n > 0, h_ref[...] - t_ref[...], 0.0)
        dh = e * (1.0 / D)
        dh_ref[...] = dh
        dhm_ref[...] = dh.astype(BF16)
        part = jnp.sum(jnp.sum(e * e, axis=1, keepdims=True), axis=0, keepdims=True) * (0.5 / D)

        @pl.when(n == 0)
        def _():
            loss_ref[...] = jnp.zeros_like(loss_ref)

        @pl.when(n > 0)
        def _():
            loss_ref[...] += jnp.broadcast_to(part, loss_ref.shape)

    row = pl.BlockSpec((T, D), lambda n: (n, 0))
    return pl.pallas_call(
        body, name="loss_head",
        out_shape=(jax.ShapeDtypeStruct((L, D), F32), jax.ShapeDtypeStruct((L, D), BF16),
                   jax.ShapeDtypeStruct((8, 128), F32)),
        grid=(NB,),
        in_specs=[row, pl.BlockSpec((T, D), lambda n: (jnp.maximum(n - 1, 0), 0))],
        out_specs=(row, row, pl.BlockSpec((8, 128), lambda n: (0, 0))))(h, target)


def _mm(name, a, b, *, grid, a_spec, b_spec, o_spec, out_shape, nt=False, ta=False, red=False, res=None,
        res_spec=None):
    def body(a_ref, b_ref, *rest):
        o_ref = rest[-1]
        av = a_ref[...]
        bv = b_ref[...]
        if bv.ndim == 3:
            bv = bv.reshape(bv.shape[0] * bv.shape[1], bv.shape[2])
        if ta:
            acc = lax.dot_general(av, bv, (((0,), (0,)), ((), ())), preferred_element_type=F32)
        else:
            acc = _dot(av, bv, nt)
        if red:
            k = pl.program_id(0)

            @pl.when(k == 0)
            def _():
                o_ref[...] = acc

            @pl.when(k > 0)
            def _():
                o_ref[...] += acc
        else:
            if res is not None:
                rows = lax.broadcasted_iota(jnp.int32, (acc.shape[0], 1), 0)
                acc = rest[0][...] + jnp.where(rows >= PAD, acc, 0.0)
            o_ref[...] = acc.astype(o_ref.dtype)

    ins = [a, b] + ([res] if res is not None else [])
    specs = [a_spec, b_spec] + ([res_spec] if res is not None else [])
    return pl.pallas_call(body, name=name, out_shape=out_shape, grid=grid, in_specs=specs, out_specs=o_spec,
                          compiler_params=_params(VMEM_BIG))(*ins)


def mm_nn(name, a, b, tn, out_dtype=F32, res=None):
    m, k = a.shape
    n = b.shape[1]
    return _mm(name, a, b, grid=(n // tn,),
               a_spec=pl.BlockSpec((m, k), lambda j: (0, 0)), b_spec=pl.BlockSpec((k, tn), lambda j: (0, j)),
               o_spec=pl.BlockSpec((m, tn), lambda j: (0, j)), out_shape=jax.ShapeDtypeStruct((m, n), out_dtype),
               res=res, res_spec=pl.BlockSpec((m, tn), lambda j: (0, j)))


def mm_nt(name, a, b, tn):
    m, k = a.shape
    n = b.shape[0]
    return _mm(name, a, b, grid=(n // tn,), nt=True,
               a_spec=pl.BlockSpec((m, k), lambda j: (0, 0)), b_spec=pl.BlockSpec((tn, k), lambda j: (j, 0)),
               o_spec=pl.BlockSpec((m, tn), lambda j: (0, j)), out_shape=jax.ShapeDtypeStruct((m, n), F32))


def mm_ta(name, a, b, tm, out_dtype=BF16):
    k, m = a.shape
    n = b.shape[1]
    return _mm(name, a, b, grid=(m // tm,), ta=True,
               a_spec=pl.BlockSpec((k, tm), lambda j: (0, j)), b_spec=pl.BlockSpec((k, n), lambda j: (0, 0)),
               o_spec=pl.BlockSpec((tm, n), lambda j: (j, 0)), out_shape=jax.ShapeDtypeStruct((m, n), out_dtype))


def ffn_up_fwd(name, u2, wupt):
    return _mm(name, u2, wupt, grid=(NDEV,), nt=True,
               a_spec=pl.BlockSpec((L, D), lambda j: (0, 0)),
               b_spec=pl.BlockSpec((None, UPW, D), lambda j: (j, 0, 0)),
               o_spec=pl.BlockSpec((None, L, UPW), lambda j: (j, 0, 0)),
               out_shape=jax.ShapeDtypeStruct((NDEV, L, UPW), F32))


def _slab_specs():
    gate = pl.BlockSpec((None, L, UPW), lambda j: (jnp.minimum(j, NPAIR - 1), 0, 0))
    up = pl.BlockSpec((None, L, UPW), lambda j: (jnp.maximum(j - NPAIR, 0), 0, 0))
    return gate, up


def ffn_up_dx(name, dfg, dfu, wupt):
    def body(g_ref, u_ref, b_ref, o_ref):
        j = pl.program_id(0)

        @pl.when(j == 0)
        def _():
            o_ref[...] = _dot(g_ref[...], b_ref[...])

        @pl.when(jnp.logical_and(j > 0, j < NPAIR))
        def _():
            o_ref[...] += _dot(g_ref[...], b_ref[...])

        @pl.when(j >= NPAIR)
        def _():
            o_ref[...] += _dot(u_ref[...], b_ref[...])

    gate, up = _slab_specs()
    return pl.pallas_call(
        body, name=name, out_shape=jax.ShapeDtypeStruct((L, D), F32), grid=(NDEV,),
        in_specs=[gate, up, pl.BlockSpec((None, UPW, D), lambda j: (j, 0, 0))],
        out_specs=pl.BlockSpec((L, D), lambda j: (0, 0)), compiler_params=_params(VMEM_BIG))(dfg, dfu, wupt)


def ffn_up_dw(name, u2, dfg, dfu):
    tdot = lambda a, b: lax.dot_general(a, b, (((0,), (0,)), ((), ())), preferred_element_type=F32)

    def body(a_ref, g_ref, u_ref, o_ref):
        j = pl.program_id(0)

        @pl.when(j < NPAIR)
        def _():
            o_ref[...] = tdot(g_ref[...], a_ref[...]).astype(o_ref.dtype)

        @pl.when(j >= NPAIR)
        def _():
            o_ref[...] = tdot(u_ref[...], a_ref[...]).astype(o_ref.dtype)

    gate, up = _slab_specs()
    return pl.pallas_call(
        body, name=name, out_shape=jax.ShapeDtypeStruct((NDEV, UPW, D), BF16), grid=(NDEV,),
        in_specs=[pl.BlockSpec((L, D), lambda j: (0, 0)), gate, up],
        out_specs=pl.BlockSpec((None, UPW, D), lambda j: (j, 0, 0)), compiler_params=_params(VMEM_BIG))(u2, dfg, dfu)


def ffn_down_fwd(name, a, wdn, res, tn=256):
    def body(a_ref, b_ref, r_ref, o_ref):
        acc = jnp.zeros((L, tn), F32)
        for g in range(NPAIR):
            bv = b_ref[2 * g:2 * g + 2]
            acc = acc + _dot(a_ref[g], bv.reshape(2 * DNR, tn))
        rows = lax.broadcasted_iota(jnp.int32, (L, 1), 0)
        o_ref[...] = r_ref[...] + jnp.where(rows >= PAD, acc, 0.0)

    return pl.pallas_call(
        body, name=name, out_shape=jax.ShapeDtypeStruct((L, D), F32), grid=(D // tn,),
        in_specs=[pl.BlockSpec((NPAIR, L, UPW), lambda j: (0, 0, 0)),
                  pl.BlockSpec((NDEV, DNR, tn), lambda j: (0, 0, j)),
                  pl.BlockSpec((L, tn), lambda j: (0, j))],
        out_specs=pl.BlockSpec((L, tn), lambda j: (0, j)),
        compiler_params=_params(VMEM_BIG))(a, wdn, res)


def ffn_down_dx(name, dh, wdn):
    return _mm(name, dh, wdn, grid=(NPAIR,), nt=True,
               a_spec=pl.BlockSpec((L, D), lambda g: (0, 0)),
               b_spec=pl.BlockSpec((2, DNR, D), lambda g: (g, 0, 0)),
               o_spec=pl.BlockSpec((None, L, UPW), lambda g: (g, 0, 0)),
               out_shape=jax.ShapeDtypeStruct((NPAIR, L, UPW), F32))


def ffn_down_dw(name, a, dh):
    return _mm(name, a, dh, grid=(NPAIR,), ta=True,
               a_spec=pl.BlockSpec((None, L, UPW), lambda g: (g, 0, 0)),
               b_spec=pl.BlockSpec((L, D), lambda g: (0, 0)),
               o_spec=pl.BlockSpec((UPW, D), lambda g: (g, 0)),
               out_shape=jax.ShapeDtypeStruct((D_FF, D), BF16))


_SLOPES = [2.0 ** (-8.0 * (h + 1) / NQ) for h in range(NQ)]
_SCALE = HD ** -0.5


QR = GQA * T
KC = 3 * T


def _attn_geometry(g, n, sink_ref):
    row = lax.broadcasted_iota(jnp.int32, (QR, KC), 0)
    col = lax.broadcasted_iota(jnp.int32, (QR, KC), 1)
    i = row & (T - 1)
    j = col & (T - 1)
    blk = col // T
    d_meta = n * T + i - j
    ok_meta = (j >= PAD) & (d_meta >= 0)
    ok_prev = j > i + jnp.where(n >= 2, 0, T)
    ok_cur = j <= i - jnp.where(n >= 1, 0, T)
    ok = ((blk == 0) & ok_meta) | ((blk == 1) & ok_prev) | ((blk == 2) & ok_cur)
    dist = jnp.where(blk == 0, jnp.minimum(d_meta, T), jnp.where(blk == 1, T + i - j, i - j)).astype(F32)
    head = lax.broadcasted_iota(jnp.int32, (QR, 1), 0) // T
    slope = jnp.zeros((QR, 1), F32)
    sink = jnp.zeros((QR, 1), F32)
    for hh in range(GQA):
        slope = jnp.where(head == hh, jnp.where(g == 0, _SLOPES[hh], _SLOPES[GQA + hh]), slope)
        sink = jnp.where(head == hh, sink_ref[g * GQA + hh], sink)
    return ok, slope * dist, sink


def _attn_probs(qn, keys, geo):
    ok, penalty, sink = geo
    s = jnp.where(ok, _dot(qn, keys, nt=True) * _SCALE - penalty, NEG)
    m = jnp.maximum(jnp.max(s, axis=-1, keepdims=True), sink)
    e = jnp.exp(s - m)
    e_sink = jnp.exp(sink - m)
    inv = 1.0 / (e.sum(axis=-1, keepdims=True) + e_sink)
    return e, inv, e_sink * inv


def _rms_rows(x, g):
    r = lax.rsqrt(jnp.mean(x * x, axis=-1, keepdims=True) + RMS_EPS)
    xh = x * r
    return xh * g, xh, r


def _rms_rows_bwd(dy, g, xh, r):
    dxh = dy * g
    return r * (dxh - xh * jnp.mean(dxh * xh, axis=-1, keepdims=True))


def _rows3(m_ref, p_ref, c_ref):
    return jnp.concatenate([m_ref[...], p_ref[...], c_ref[...]], axis=0)


_ATT_K, _ATT_V = 4, 5


def _nat_specs():
    qspec = pl.BlockSpec((T, GQA * HD), lambda n, g: (n, g))
    kv = lambda col: (pl.BlockSpec((T, 2 * HD), lambda n, g: (n, col)),
                      pl.BlockSpec((T, 2 * HD), lambda n, g: (jnp.maximum(n - 1, 0), col)),
                      pl.BlockSpec((T, 2 * HD), lambda n, g: (0, col)))
    return qspec, kv(_ATT_K), kv(_ATT_V)


def _group_lanes(g, x):
    return jnp.where(g == 0, x[:, :HD], x[:, HD:])


def _stack_heads(x):
    return jnp.concatenate([x[:, hh * HD:(hh + 1) * HD] for hh in range(GQA)], axis=0)


def _unstack_heads(x):
    return jnp.concatenate([x[hh * T:(hh + 1) * T] for hh in range(GQA)], axis=1)


def attn_fwd(proj, qg, kg, sinks, name):
    def body(sink_ref, q_ref, kc_ref, kp_ref, km_ref, vc_ref, vp_ref, vm_ref, qg_ref, kg_ref, o_ref):
        n = pl.program_id(0)
        g = pl.program_id(1)
        geo = _attn_geometry(g, n, sink_ref)
        keys = _bf(_rms_rows(_group_lanes(g, _rows3(km_ref, kp_ref, kc_ref)), kg_ref[...])[0])
        vals = _bf(_group_lanes(g, _rows3(vm_ref, vp_ref, vc_ref)))
        qn = _bf(_rms_rows(_stack_heads(q_ref[...]), qg_ref[...])[0])
        e, inv, _ = _attn_probs(qn, keys, geo)
        o_ref[...] = _unstack_heads(_dot(_bf(e), vals) * inv)

    qspec, (kc, kp, km), (vc, vp, vm) = _nat_specs()
    vec = pl.BlockSpec((1, HD), lambda n, g: (0, 0))
    return pl.pallas_call(
        body, name=name, out_shape=jax.ShapeDtypeStruct((L, NQ * HD), F32), grid=(NB, NKV),
        in_specs=[pl.BlockSpec(memory_space=pltpu.SMEM), qspec, kc, kp, km, vc, vp, vm, vec, vec],
        out_specs=qspec)(sinks, proj, proj, proj, proj, proj, proj, proj, qg, kg)


def attn_bwd(proj, do, qg, kg, sinks, name):
    def body(sink_ref, q_ref, kc_ref, kp_ref, km_ref, vc_ref, vp_ref, vm_ref, do_ref, qg_ref, kg_ref,
             dq_ref, dkc_ref, dkp_ref, dkm_ref, dvc_ref, dvp_ref, dvm_ref, dqg_ref, dkg_ref, dsk_ref):
        n = pl.program_id(0)
        g = pl.program_id(1)

        @pl.when(jnp.logical_and(g == 0, n == 0))
        def _():
            for r in (dkm_ref, dvm_ref, dqg_ref, dkg_ref, dsk_ref):
                r[...] = jnp.zeros_like(r)

        geo = _attn_geometry(g, n, sink_ref)
        kgv = kg_ref[...]
        qgv = qg_ref[...]
        kn_f, kh, kr = _rms_rows(_group_lanes(g, _rows3(km_ref, kp_ref, kc_ref)), kgv)
        keys = _bf(kn_f)
        vals = _bf(_group_lanes(g, _rows3(vm_ref, vp_ref, vc_ref)))
        qn_f, qh, qr = _rms_rows(_stack_heads(q_ref[...]), qgv)
        qn = _bf(qn_f)
        e, inv, p_sink = _attn_probs(qn, keys, geo)
        dob = _bf(_stack_heads(do_ref[...]))
        p = e * inv
        dp = _dot(dob, vals, nt=True)
        delta = (p * dp).sum(axis=-1, keepdims=True)
        head_row = lax.broadcasted_iota(jnp.int32, (NQ, 128), 0)
        dsk = jnp.zeros((NQ, 128), F32)
        for hh in range(GQA):
            part = -jnp.sum((p_sink * delta)[hh * T:(hh + 1) * T], axis=0, keepdims=True)
            dsk = jnp.where(head_row == g * GQA + hh, part, dsk)
        dsk_ref[...] += dsk
        ds = p * (dp - delta)
        dqn = _dot(_bf(ds), keys) * _SCALE
        dq_ref[...] = _unstack_heads(_rms_rows_bwd(dqn, qgv, qh, qr))
        dqg_ref[...] += jnp.sum(dqn * qh, axis=0, keepdims=True)
        dkn = _dot(_bf(ds.T), qn) * _SCALE
        dkg_ref[...] += jnp.sum(dkn * kh, axis=0, keepdims=True)
        dk_all = _rms_rows_bwd(dkn, kgv, kh, kr)
        dv_all = _dot(_bf(p.T), dob)

        for gg in range(NKV):
            @pl.when(g == gg)
            def _(gg=gg):
                lanes = slice(gg * HD, (gg + 1) * HD)
                dkm_ref[:, lanes] += dk_all[0:T]
                dkp_ref[:, lanes] = dk_all[T:2 * T]
                dkc_ref[:, lanes] = dk_all[2 * T:3 * T]
                dvm_ref[:, lanes] += dv_all[0:T]
                dvp_ref[:, lanes] = dv_all[T:2 * T]
                dvc_ref[:, lanes] = dv_all[2 * T:3 * T]

    qspec, (kc, kp, km), (vc, vp, vm) = _nat_specs()
    vec = pl.BlockSpec((1, HD), lambda n, g: (0, 0))
    cur = pl.BlockSpec((T, 2 * HD), lambda n, g: (n, 0))
    meta = pl.BlockSpec((T, 2 * HD), lambda n, g: (0, 0))
    kv_shape = jax.ShapeDtypeStruct((L, 2 * HD), F32)
    meta_shape = jax.ShapeDtypeStruct((T, 2 * HD), F32)
    vec_shape = jax.ShapeDtypeStruct((1, HD), F32)
    return pl.pallas_call(
        body, name=name,
        out_shape=(jax.ShapeDtypeStruct((L, NQ * HD), F32), kv_shape, kv_shape, meta_shape, kv_shape, kv_shape,
                   meta_shape, vec_shape, vec_shape, jax.ShapeDtypeStruct((NQ, 128), F32)),
        grid=(NB, NKV),
        in_specs=[pl.BlockSpec(memory_space=pltpu.SMEM), qspec, kc, kp, km, vc, vp, vm, qspec, vec, vec],
        out_specs=(qspec, cur, cur, meta, cur, cur, meta, vec, vec,
                   pl.BlockSpec((NQ, 128), lambda n, g: (0, 0))))(
            sinks, proj, proj, proj, proj, proj, proj, proj, do, qg, kg)


RW = RH * HD


def _ret_tables():
    h = np.arange(RH, dtype=np.float64)
    lg = np.log1p(-np.exp2(-5.0 - h))
    idx = np.arange(T, dtype=np.float64)
    diff = idx[:, None] - idx[None, :]
    decay = np.where(diff[None] >= 0, np.exp(np.maximum(diff, 0.0)[None] * lg[:, None, None]), 0.0)
    zeta = np.exp((T - 1 - idx)[None, :] * lg[:, None])
    xi = np.exp((idx + 1.0)[None, :] * lg[:, None])
    cd = np.exp(T * lg)
    lanes = lambda a: np.repeat(a.T, HD, axis=1)
    head_of = np.arange(RW) // HD
    same = (head_of[:, None] == head_of[None, :]).astype(np.float64)
    f = lambda a: jnp.asarray(a, F32)
    return dict(decay=f(decay), zeta=f(lanes(zeta)), xi=f(lanes(xi)), cd=f(np.repeat(cd, HD)[None, :]),
                head=f((head_of[None, :] == np.arange(RH)[:, None]).astype(np.float64)[:, None, :]),
                same=f(same), avg=jnp.asarray(same / HD, BF16))


def _seg_mean(x, avg):
    hi = _bf(x)
    lo = _bf(x - hi.astype(F32))
    return _dot(hi, avg) + _dot(lo, avg)


def _ret_specs(col0, order):
    return lambda col: pl.BlockSpec((T, RW), lambda i: (order(i), col0 + col))


def _ret_chunk(q, kf, v, s, tb):
    dec, xi, head = tb
    vb = _bf(v)
    kb = _bf(kf)
    y = _dot(_bf(q * xi), _bf(s))
    a = []
    for h in range(RH):
        a.append(_dot(_bf(q * head[h]), kb, nt=True) * dec[h])
        y = y + head[h] * _dot(_bf(a[h]), vb)
    return a, y


def _gn_rows(y):
    mu = jnp.mean(y, axis=-1, keepdims=True)
    yc = y - mu
    rstd = lax.rsqrt(jnp.mean(yc * yc, axis=-1, keepdims=True) + LN_EPS)
    return yc * rstd, rstd


_RET_COL0 = _RET_Q = 5


def _ret_consts(tb):
    names = ("decay", "zeta", "xi", "cd", "head", "same", "avg")
    full = lambda a: pl.BlockSpec(a.shape, lambda i: (0,) * a.ndim)
    return [tb[n] for n in names], [full(tb[n]) for n in names]


def _ret_inputs(n, q_ref, k_ref, v_ref):
    rows = n * T + lax.broadcasted_iota(jnp.int32, (T, 1), 0)
    valid = rows >= PAD
    return q_ref[...], jnp.where(valid, k_ref[...] * (HD ** -0.5), 0.0), v_ref[...], valid


def ret_fwd(proj, gng, tb, name):
    def body(q_ref, k_ref, v_ref, g_ref, dec_ref, zeta_ref, xi_ref, cd_ref, head_ref, same_ref, avg_ref, gng_ref,
             y_ref, st_ref, s_scr):
        n = pl.program_id(0)

        @pl.when(n == 0)
        def _():
            s_scr[...] = jnp.zeros_like(s_scr)

        s = s_scr[...]
        st_ref[...] = s
        q, kf, v, _ = _ret_inputs(n, q_ref, k_ref, v_ref)
        _, y = _ret_chunk(q, kf, v, s, (dec_ref, xi_ref[...], head_ref))
        s_scr[...] = cd_ref[...] * s + same_ref[...] * _dot(_bf((kf * zeta_ref[...]).T), _bf(v))
        avg = avg_ref[...]
        yc = y - _seg_mean(y, avg)
        yh = yc * lax.rsqrt(_seg_mean(yc * yc, avg) + LN_EPS)
        gv = g_ref[...]
        y_ref[...] = gv * _sig(gv) * (yh * gng_ref[...])

    col = _ret_specs(_RET_COL0, lambda i: i)
    consts, cspecs = _ret_consts(tb)
    return pl.pallas_call(
        body, name=name,
        out_shape=(jax.ShapeDtypeStruct((L, RW), F32), jax.ShapeDtypeStruct((NB, RW, RW), F32)),
        grid=(NB,),
        in_specs=[col(0), col(1), col(2), col(3)] + cspecs + [pl.BlockSpec((1, RW), lambda i: (0, 0))],
        out_specs=(pl.BlockSpec((T, RW), lambda i: (i, 0)), pl.BlockSpec((None, RW, RW), lambda i: (i, 0, 0))),
        scratch_shapes=[pltpu.VMEM((RW, RW), F32)])(proj, proj, proj, proj, *consts, gng)


def ret_bwd(proj, dycat, states, gng, tb, name):
    def body(q_ref, k_ref, v_ref, g_ref, do_ref, st_ref, dec_ref, zeta_ref, xi_ref, cd_ref, head_ref, same_ref,
             avg_ref, gng_ref, d_ref, dgn_ref, ds_scr):
        i = pl.program_id(0)
        n = NB - 1 - i

        @pl.when(i == 0)
        def _():
            ds_scr[...] = jnp.zeros_like(ds_scr)

        dsn = _bf(ds_scr[...])
        s = st_ref[...]
        sb = _bf(s)
        q, kf, v, valid = _ret_inputs(n, q_ref, k_ref, v_ref)
        xi, zeta, avg = xi_ref[...], zeta_ref[...], avg_ref[...]
        a, y = _ret_chunk(q, kf, v, s, (dec_ref, xi, head_ref))
        yc = y - _seg_mean(y, avg)
        rstd = lax.rsqrt(_seg_mean(yc * yc, avg) + LN_EPS)
        yh = yc * rstd
        gv = g_ref[...]
        sg = _sig(gv)
        sil = gv * sg
        gn = gng_ref[...]
        dout = do_ref[...]
        d_ref[:, 3 * RW:4 * RW] = dout * (yh * gn) * (sg * (1.0 + gv * (1.0 - sg)))
        dyh = dout * sil * gn
        part = jnp.sum(dout * sil * yh, axis=0, keepdims=True)

        @pl.when(i == 0)
        def _():
            dgn_ref[...] = part

        @pl.when(i > 0)
        def _():
            dgn_ref[...] += part

        dy = rstd * (dyh - _seg_mean(dyh, avg) - yh * _seg_mean(dyh * yh, avg))
        dyb = _bf(dy)
        vb, kb, qb = _bf(v), _bf(kf), _bf(q)
        dq = _dot(dyb, sb, nt=True) * xi
        dkf = _dot(vb, dsn, nt=True) * zeta
        dv = _dot(_bf(kf * zeta), dsn)
        for h in range(RH):
            m = head_ref[h]
            da = _dot(_bf(dy * m), vb, nt=True) * dec_ref[h]
            dv = dv + m * _dot(_bf(a[h].T), dyb)
            dq = dq + m * _dot(_bf(da), kb)
            dkf = dkf + m * _dot(_bf(da.T), qb)
        d_ref[:, 0:RW] = dq
        d_ref[:, RW:2 * RW] = jnp.where(valid, dkf * (HD ** -0.5), 0.0)
        d_ref[:, 2 * RW:3 * RW] = dv
        ds_scr[...] = cd_ref[...] * ds_scr[...] + same_ref[...] * _dot(_bf((q * xi).T), dyb)

    back = lambda i: NB - 1 - i
    col = _ret_specs(_RET_COL0, back)
    consts, cspecs = _ret_consts(tb)
    vec = pl.BlockSpec((1, RW), lambda i: (0, 0))
    return pl.pallas_call(
        body, name=name,
        out_shape=(jax.ShapeDtypeStruct((L, 4 * RW), F32), jax.ShapeDtypeStruct((1, RW), F32)),
        grid=(NB,),
        in_specs=[col(0), col(1), col(2), col(3), _ret_specs(3, back)(0),
                  pl.BlockSpec((None, RW, RW), lambda i: (back(i), 0, 0))] + cspecs + [vec],
        out_specs=(pl.BlockSpec((T, 4 * RW), lambda i: (back(i), 0)), vec),
        scratch_shapes=[pltpu.VMEM((RW, RW), F32)])(proj, proj, proj, proj, dycat, states, *consts, gng)


HALO = 32
TAP0 = HALO - (CONV_K - 1)


def _conv_specs():
    cur = lambda col=0: pl.BlockSpec((T, CV), lambda c: (c, col))
    before = lambda col=0: pl.BlockSpec((HALO, CV), lambda c: (jnp.maximum(c * (T // HALO) - 1, 0), col))
    after = pl.BlockSpec((HALO, CV), lambda c: (jnp.minimum((c + 1) * (T // HALO), L // HALO - 1), 0))
    full = lambda r, w: pl.BlockSpec((r, w), lambda c: (0, 0))
    return cur, before, after, full


_CONV_A, _CONV_B = 3, 4
_DY_CONV = 2


def _conv_tile_fwd(c, ca_ref, cah_ref, cb_ref, cbh_ref, w_ref, b_ref, lg_ref, lb_ref, u_scr):
    u_scr[0:HALO, :] = jnp.where(c > 0, cah_ref[...] * _sig(cbh_ref[...]), 0.0)
    u_scr[HALO:HALO + T, :] = ca_ref[...] * _sig(cb_ref[...])
    acc = jnp.zeros((T, CV), F32)
    for k in range(CONV_K):
        acc = acc + w_ref[k:k + 1, :] * u_scr[TAP0 + k:TAP0 + k + T, :]
    xc = acc + b_ref[...]
    xh, rstd = _gn_rows(xc)
    z = xh * lg_ref[...] + lb_ref[...]
    return xh, rstd, z, _sig(z)


def conv_fwd(proj, w, b, lg, lb, pw, og, name):
    def body(ca_ref, cah_ref, cb_ref, cbh_ref, w_ref, b_ref, lg_ref, lb_ref, pw_ref, og_ref, y_ref, u_scr):
        c = pl.program_id(0)
        _, _, z, sg = _conv_tile_fwd(c, ca_ref, cah_ref, cb_ref, cbh_ref, w_ref, b_ref, lg_ref, lb_ref, u_scr)
        zp = _dot(_bf(z * sg), pw_ref[...])
        y_ref[...] = _rms_rows(zp, og_ref[...])[0]

    cur, before, _, full = _conv_specs()
    vec = full(1, CV)
    return pl.pallas_call(
        body, name=name, out_shape=jax.ShapeDtypeStruct((L, CV), F32), grid=(NB,),
        in_specs=[cur(_CONV_A), before(_CONV_A), cur(_CONV_B), before(_CONV_B), full(CONV_K, CV), vec, vec, vec,
                  full(CV, CV), vec],
        out_specs=cur(),
        scratch_shapes=[pltpu.VMEM((HALO + T, CV), F32)])(proj, proj, proj, proj, w, b, lg, lb, pw, og)


def conv_bwd(proj, dycat, w, b, lg, lb, pw, og, name):
    def body1(ca_ref, cah_ref, cb_ref, cbh_ref, dy_ref, w_ref, b_ref, lg_ref, lb_ref, pw_ref, og_ref,
              dxc_ref, dw_ref, db_ref, dlg_ref, dlb_ref, dog_ref, dpw_ref, u_scr):
        c = pl.program_id(0)

        @pl.when(c == 0)
        def _():
            for r in (dw_ref, db_ref, dlg_ref, dlb_ref, dog_ref, dpw_ref):
                r[...] = jnp.zeros_like(r)

        xh, rstd, z, sg = _conv_tile_fwd(c, ca_ref, cah_ref, cb_ref, cbh_ref, w_ref, b_ref, lg_ref, lb_ref, u_scr)
        s = z * sg
        zp = _dot(_bf(s), pw_ref[...])
        ogv = og_ref[...]
        _, zph, r2 = _rms_rows(zp, ogv)
        dyv = dy_ref[...]
        dog_ref[...] += jnp.sum(dyv * zph, axis=0, keepdims=True)
        dzpb = _bf(_rms_rows_bwd(dyv, ogv, zph, r2))
        dpw_ref[...] += _dot(_bf(s.T), dzpb)
        dz = _dot(dzpb, pw_ref[...], nt=True) * (sg * (1.0 + z * (1.0 - sg)))
        dlg_ref[...] += jnp.sum(dz * xh, axis=0, keepdims=True)
        dlb_ref[...] += jnp.sum(dz, axis=0, keepdims=True)
        dxh = dz * lg_ref[...]
        dxc = rstd * (dxh - jnp.mean(dxh, axis=-1, keepdims=True) - xh * jnp.mean(dxh * xh, axis=-1, keepdims=True))
        db_ref[...] += jnp.sum(dxc, axis=0, keepdims=True)
        dxc_ref[...] = dxc
        for k in range(CONV_K):
            dw_ref[k:k + 1, :] += jnp.sum(dxc * u_scr[TAP0 + k:TAP0 + k + T, :], axis=0, keepdims=True)

    def body2(dx_ref, dxa_ref, ca_ref, cb_ref, w_ref, dca_ref, dcb_ref, d_scr):
        c = pl.program_id(0)
        d_scr[0:T, :] = dx_ref[...]
        d_scr[T:T + HALO, :] = jnp.where(c < NB - 1, dxa_ref[...], 0.0)
        du = jnp.zeros((T, CV), F32)
        for k in range(CONV_K):
            off = CONV_K - 1 - k
            du = du + w_ref[k:k + 1, :] * d_scr[off:off + T, :]
        sg = _sig(cb_ref[...])
        dca_ref[...] = du * sg
        dcb_ref[...] = du * ca_ref[...] * sg * (1.0 - sg)

    cur, before, after, full = _conv_specs()
    seq = jax.ShapeDtypeStruct((L, CV), F32)
    vsh = jax.ShapeDtypeStruct((1, CV), F32)
    vec = full(1, CV)
    dxc, dw, db, dlg, dlb, dog, dpw = pl.pallas_call(
        body1, name=name + "_a",
        out_shape=(seq, jax.ShapeDtypeStruct((CONV_K, CV), F32), vsh, vsh, vsh, vsh,
                   jax.ShapeDtypeStruct((CV, CV), F32)),
        grid=(NB,),
        in_specs=[cur(_CONV_A), before(_CONV_A), cur(_CONV_B), before(_CONV_B), cur(_DY_CONV), full(CONV_K, CV), vec,
                  vec, vec, full(CV, CV), vec],
        out_specs=(cur(), full(CONV_K, CV), vec, vec, vec, vec, full(CV, CV)),
        scratch_shapes=[pltpu.VMEM((HALO + T, CV), F32)])(proj, proj, proj, proj, dycat, w, b, lg, lb, pw, og)
    dca, dcb = pl.pallas_call(
        body2, name=name + "_b", out_shape=(seq, seq), grid=(NB,),
        in_specs=[cur(), after, cur(_CONV_A), cur(_CONV_B), full(CONV_K, CV)], out_specs=(cur(), cur()),
        scratch_shapes=[pltpu.VMEM((T + HALO, CV), F32)])(dxc, dxc, proj, proj, w)
    return dca, dcb, dw, db, dlg, dlb, dog, dpw


HR = 8


def _ffn_specs():
    cur = lambda off: pl.BlockSpec((None, T, UPW), lambda j, c: (j + off, c, 0))
    before = lambda off: pl.BlockSpec((None, HR, UPW), lambda j, c: (j + off, jnp.maximum(c * (T // HR) - 1, 0), 0))
    after = lambda off: pl.BlockSpec(
        (None, HR, UPW), lambda j, c: (j + off, jnp.minimum((c + 1) * (T // HR), L // HR - 1), 0))
    wspec = lambda off, r: pl.BlockSpec((None, r, UPW), lambda j, c: (j + off, 0, 0))
    return cur, before, after, wspec


def ffn_act_fwd(f, w, b, name):
    strip = 16

    def body(fg_ref, fgh_ref, fu_ref, fuh_ref, wg_ref, wu_ref, bg_ref, bu_ref, a_ref, gc_ref, uc_ref, hg_scr, hu_scr):
        c = pl.program_id(1)
        for scr, x_ref, xh_ref in ((hg_scr, fg_ref, fgh_ref), (hu_scr, fu_ref, fuh_ref)):
            scr[0:HR, :] = jnp.where(c > 0, xh_ref[...], 0.0)
            scr[HR:HR + strip, :] = x_ref[0:strip, :]
        wg = [wg_ref[k:k + 1, :] for k in range(FFN_K)]
        wu = [wu_ref[k:k + 1, :] for k in range(FFN_K)]
        bg, bu = bg_ref[...], bu_ref[...]

        def conv(src, base, w, b):
            acc = b
            for k in range(FFN_K):
                o = base - (FFN_K - 1) + k
                acc = acc + w[k] * src[o:o + strip, :]
            return acc

        for s in range(T // strip):
            r = s * strip
            gate = conv(hg_scr, HR, wg, bg) if s == 0 else conv(fg_ref, r, wg, bg)
            up = conv(hu_scr, HR, wu, bu) if s == 0 else conv(fu_ref, r, wu, bu)
            gc_ref[r:r + strip, :] = gate
            uc_ref[r:r + strip, :] = up
            a_ref[r:r + strip, :] = (gate * _sig(gate) * up).astype(BF16)

    cur, before, _, wspec = _ffn_specs()
    pre = jax.ShapeDtypeStruct((NPAIR, L, UPW), F32)
    return pl.pallas_call(
        body, name=name, out_shape=(jax.ShapeDtypeStruct((NPAIR, L, UPW), BF16), pre, pre), grid=(NPAIR, NB),
        in_specs=[cur(0), before(0), cur(NPAIR), before(NPAIR), wspec(0, FFN_K), wspec(NPAIR, FFN_K),
                  wspec(0, 1), wspec(NPAIR, 1)],
        out_specs=(cur(0), cur(0), cur(0)),
        scratch_shapes=[pltpu.VMEM((HR + strip, UPW), F32), pltpu.VMEM((HR + strip, UPW), F32)])(
            f, f, f, f, w, w, b, b)


def ffn_act_bwd(f, gc, uc, da, w, name):
    ext = T + HR
    sub = 8

    def body(fg_ref, fu_ref, gc_ref, gca_ref, uc_ref, uca_ref, da_ref, daa_ref, wg_ref, wu_ref,
             dfg_ref, dfu_ref, dwg_ref, dwu_ref, dbg_ref, dbu_ref, dg_scr, du_scr):
        c = pl.program_id(1)
        wg = [wg_ref[k:k + 1, :] for k in range(FFN_K)]
        wu = [wu_ref[k:k + 1, :] for k in range(FFN_K)]

        for s in range(ext // sub):
            r = s * sub
            if r < T:
                gate, up, dav = gc_ref[r:r + sub, :], uc_ref[r:r + sub, :], da_ref[r:r + sub, :]
            else:
                gate, up, dav = gca_ref[...], uca_ref[...], jnp.where(c < NB - 1, daa_ref[...], 0.0)
            sg = _sig(gate)
            dg_scr[r:r + sub, :] = dav * up * (sg * (1.0 + gate * (1.0 - sg)))
            du_scr[r:r + sub, :] = dav * gate * sg

        def back(d_scr, x_ref, w, df_ref, dw_ref, db_ref):
            accs = [jnp.zeros((sub, UPW), F32) for _ in range(FFN_K + 1)]
            for s in range(T // (2 * sub)):
                pieces = []
                for r in (2 * s * sub, (2 * s + 1) * sub):
                    xv = x_ref[r:r + sub, :]
                    df = jnp.zeros((sub, UPW), F32)
                    for k in range(FFN_K):
                        off = FFN_K - 1 - k
                        dj = d_scr[r + off:r + off + sub, :]
                        df = df + w[k] * dj
                        accs[k] = accs[k] + xv * dj
                        if off == 0:
                            accs[FFN_K] = accs[FFN_K] + dj
                    pieces.append(df)
                df_ref[2 * s * sub:2 * (s + 1) * sub, :] = jnp.concatenate(pieces, axis=0).astype(BF16)
            dwp = _stack_rows([jnp.sum(a, axis=0, keepdims=True) for a in accs[:FFN_K]])
            dbp = jnp.sum(accs[FFN_K], axis=0, keepdims=True)

            @pl.when(c == 0)
            def _():
                dw_ref[...] = dwp
                db_ref[...] = dbp

            @pl.when(c > 0)
            def _():
                dw_ref[...] += dwp
                db_ref[...] += dbp

        back(dg_scr, fg_ref, wg, dfg_ref, dwg_ref, dbg_ref)
        back(du_scr, fu_ref, wu, dfu_ref, dwu_ref, dbu_ref)

    cur, _, after, wspec = _ffn_specs()
    slab = jax.ShapeDtypeStruct((NPAIR, L, UPW), BF16)
    wsh = jax.ShapeDtypeStruct((NPAIR, FFN_K, UPW), F32)
    bsh = jax.ShapeDtypeStruct((NPAIR, 1, UPW), F32)
    return pl.pallas_call(
        body, name=name, out_shape=(slab, slab, wsh, wsh, bsh, bsh), grid=(NPAIR, NB),
        in_specs=[cur(0), cur(NPAIR), cur(0), after(0), cur(0), after(0), cur(0), after(0),
                  wspec(0, FFN_K), wspec(NPAIR, FFN_K)],
        out_specs=(cur(0), cur(0), wspec(0, FFN_K), wspec(0, FFN_K), wspec(0, 1), wspec(0, 1)),
        scratch_shapes=[pltpu.VMEM((ext, UPW), F32), pltpu.VMEM((ext, UPW), F32)])(
            f, f, gc, gc, uc, uc, da, da, w, w)


def _mesh_pos():
    return lax.axis_index("x"), lax.axis_index("y"), lax.axis_index("c")


def _peer(pos, k):
    x, y, c = pos
    px = 1 - x if k & 4 else x
    py = 1 - y if k & 2 else y
    pc = 1 - c if k & 1 else c
    return (px, py, pc), 4 * px + 2 * py + pc


_CHIP_FLIPS = (4, 2, 6)
_HBM = pl.BlockSpec(memory_space=pl.ANY)


def all_gather(shards, name):
    nt = len(shards)

    def body(*refs):
        ins, outs = refs[:nt], refs[nt:2 * nt]
        send, recv, local = refs[2 * nt:]
        pos = _mesh_pos()
        me = 4 * pos[0] + 2 * pos[1] + pos[2]
        sib, sib_id = _peer(pos, 1)

        def copy(t, k, block_id, to, src=None):
            dst = outs[t].at[block_id]
            return pltpu.make_async_remote_copy(
                src_ref=dst if src is None else src, dst_ref=dst, send_sem=send.at[t, k], recv_sem=recv.at[t, k],
                device_id=to, device_id_type=MESH)

        locals_ = [pltpu.make_async_copy(ins[t], outs[t].at[me], local.at[t]) for t in range(nt)]
        for cp in locals_:
            cp.start()
        started = []
        for j, flip in enumerate(_CHIP_FLIPS):
            for t in range(nt):
                started.append(copy(t, 1 + j, me, _peer(pos, flip)[0], src=ins[t]))
        for t in range(nt):
            started.append(copy(t, 0, me, sib, src=ins[t]))
        for cp in started:
            cp.start()
        for j, flip in enumerate(_CHIP_FLIPS):
            _, pid = _peer(pos, flip)
            for t in range(nt):
                copy(t, 1 + j, pid, sib).wait_recv()
                fwd = copy(t, 4 + j, pid, sib)
                fwd.start()
                started.append(fwd)
        for t in range(nt):
            copy(t, 0, sib_id, sib).wait_recv()
        for j, flip in enumerate(_CHIP_FLIPS):
            _, pid = _peer(pos, flip | 1)
            for t in range(nt):
                copy(t, 4 + j, pid, sib).wait_recv()
        for cp in started:
            cp.wait_send()
        for cp in locals_:
            cp.wait()

    return pl.pallas_call(
        body, name=name,
        out_shape=tuple(jax.ShapeDtypeStruct((NDEV,) + s.shape, s.dtype) for s in shards),
        in_specs=[_HBM] * nt, out_specs=tuple([_HBM] * nt),
        scratch_shapes=[pltpu.SemaphoreType.DMA((nt, NDEV - 1)), pltpu.SemaphoreType.DMA((nt, NDEV - 1)),
                        pltpu.SemaphoreType.DMA((nt,))])(*shards)


_SPLIT = dict(has_side_effects=pltpu.SideEffectType.DATAFLOW_SIDE_EFFECTING)
_SEM = pl.BlockSpec(memory_space=pltpu.SEMAPHORE)


def _exchange_start(name, srcs, lands, copies, after):
    ns, nl = len(srcs), len(lands)
    ncopy = len(copies(None, [None] * ns, [None] * nl))

    def body(*refs):
        src_refs, land_refs = refs[:ns], refs[ns:ns + nl]
        send, recv = refs[ns + nl + len(after):ns + nl + len(after) + 2]
        token = refs[-1]
        for i, (src, dst, peer) in enumerate(copies(_mesh_pos(), src_refs, land_refs)):
            pltpu.make_async_remote_copy(src_ref=src, dst_ref=dst, send_sem=send.at[i], recv_sem=recv.at[i],
                                         device_id=peer, device_id_type=MESH).start()
        token[...] = jnp.zeros_like(token)

    hbm = lambda a: pltpu.HBM(a.shape, a.dtype)
    out = pl.pallas_call(
        body, name=name,
        out_shape=(pltpu.SemaphoreType.DMA((ncopy,)), pltpu.SemaphoreType.DMA((ncopy,)),
                   *[hbm(a) for a in srcs], *[hbm(a) for a in lands], jax.ShapeDtypeStruct((8, 128), F32)),
        in_specs=[pl.BlockSpec(memory_space=pltpu.HBM)] * (ns + nl) + [_HBM] * len(after),
        out_specs=(_SEM, _SEM, *[pl.BlockSpec(memory_space=pltpu.HBM)] * (ns + nl),
                   pl.BlockSpec(memory_space=pltpu.VMEM)),
        input_output_aliases={i: 2 + i for i in range(ns + nl)},
        compiler_params=pltpu.CompilerParams(**_SPLIT))(
            *[pltpu.with_memory_space_constraint(a, pltpu.HBM) for a in list(srcs) + list(lands)], *after)
    return out[0], out[1], list(out[2:2 + ns]), list(out[2 + ns:2 + ns + nl]), out[-1]


def _exchange_wait(name, send, recv, srcs, lands, copies, after):
    ns, nl = len(srcs), len(lands)

    def body(*refs):
        src_refs, land_refs = refs[:ns], refs[ns:ns + nl]
        send_ref, recv_ref = refs[ns + nl:ns + nl + 2]
        pos = _mesh_pos()
        for i, (src, dst, peer) in enumerate(copies(pos, src_refs, land_refs, arriving=True)):
            cp = pltpu.make_async_remote_copy(src_ref=src, dst_ref=dst, send_sem=send_ref.at[i], recv_sem=recv_ref.at[i],
                                              device_id=peer, device_id_type=MESH)
            cp.wait_send()
            cp.wait_recv()

    hbm = lambda a: pltpu.HBM(a.shape, a.dtype)
    out = pl.pallas_call(
        body, name=name, out_shape=tuple(hbm(a) for a in list(srcs) + list(lands)),
        in_specs=[pl.BlockSpec(memory_space=pltpu.HBM)] * (ns + nl) + [_SEM, _SEM] + [_HBM] * len(after),
        out_specs=tuple([pl.BlockSpec(memory_space=pltpu.HBM)] * (ns + nl)),
        input_output_aliases={i: i for i in range(ns + nl)},
        compiler_params=pltpu.CompilerParams(**_SPLIT))(*srcs, *lands, send, recv, *after)
    return list(out[:ns]), list(out[ns:])


def _gather_copies(pos, srcs, lands, arriving=False):
    if pos is None:
        return [None] * (len(srcs) * (NDEV - 1))
    me = 4 * pos[0] + 2 * pos[1] + pos[2]
    out = []
    for src, land in zip(srcs, lands):
        for k in range(1, NDEV):
            peer, pid = _peer(pos, k)
            out.append((src, land.at[pid if arriving else me], peer))
    return out


def _scatter_copies(pos, srcs, lands, arriving=False):
    if pos is None:
        return [None] * (len(srcs) * (NDEV - 1))
    out = []
    for src, land in zip(srcs, lands):
        for k in range(1, NDEV):
            peer, pid = _peer(pos, k)
            out.append((src.at[pid], land.at[k - 1], peer))
    return out


def gather_start(name, shards, lands, after=()):
    return _exchange_start(name, shards, lands, _gather_copies, list(after))


def gather_wait(name, handle, after=()):
    send, recv, srcs, lands, _ = handle
    return _exchange_wait(name, send, recv, srcs, lands, _gather_copies, list(after))


def scatter_start(name, grads, after=()):
    lands = [pltpu.with_memory_space_constraint(lax.empty((NDEV - 1,) + g.shape[1:], g.dtype), pltpu.HBM)
             for g in grads]
    return _exchange_start(name, grads, lands, _scatter_copies, list(after))


def scatter_wait(name, handle, after=()):
    send, recv, srcs, lands, _ = handle
    return _exchange_wait(name, send, recv, srcs, lands, _scatter_copies, list(after))


def all_reduce_small(part, name):
    rows = part.shape[0]

    def body(x_ref, o_ref, buf, send, recv):
        pos = _mesh_pos()
        me = 4 * pos[0] + 2 * pos[1] + pos[2]
        sends = []
        for k in range(1, NDEV):
            peer, _ = _peer(pos, k)
            sends.append(pltpu.make_async_remote_copy(
                src_ref=x_ref, dst_ref=buf.at[me], send_sem=send.at[k - 1], recv_sem=recv.at[k - 1],
                device_id=peer, device_id_type=MESH))
        for cp in sends:
            cp.start()
        buf[me] = x_ref[...]
        for k in range(1, NDEV):
            peer, pid = _peer(pos, k)
            pltpu.make_async_remote_copy(
                src_ref=x_ref, dst_ref=buf.at[pid], send_sem=send.at[k - 1], recv_sem=recv.at[k - 1],
                device_id=peer, device_id_type=MESH).wait_recv()
        for cp in sends:
            cp.wait_send()
        acc = buf[0]
        for d in range(1, NDEV):
            acc = acc + buf[d]
        o_ref[...] = acc

    return pl.pallas_call(
        body, name=name, out_shape=jax.ShapeDtypeStruct((rows, 128), F32),
        in_specs=[pl.BlockSpec(memory_space=pltpu.VMEM)], out_specs=pl.BlockSpec(memory_space=pltpu.VMEM),
        scratch_shapes=[pltpu.VMEM((NDEV, rows, 128), F32), pltpu.SemaphoreType.DMA((NDEV - 1,)),
                        pltpu.SemaphoreType.DMA((NDEV - 1,))])(part)


def _adamw_update(g, w_ref, m_ref, v_ref, g_ref, d_ref, nm_ref, nv_ref):
    g_ref[...] = g
    nm = ADAM_B1 * m_ref[...] + (1.0 - ADAM_B1) * g
    nv = ADAM_B2 * v_ref[...] + (1.0 - ADAM_B2) * (g * g)
    nm_ref[...] = nm
    nv_ref[...] = nv
    m_hat = nm / (1.0 - ADAM_B1 ** ADAM_STEP)
    v_hat = nv / (1.0 - ADAM_B2 ** ADAM_STEP)
    d_ref[...] = -ADAM_LR * (m_hat / (jnp.sqrt(v_hat) + ADAM_EPS) + ADAM_WD * w_ref[...])


def adamw_big(own, lands, w, m, v, me, tr, name):
    _, r, c = own[0].shape
    nt = r // tr

    def body(me_ref, *refs):
        ins, (w_ref, m_ref, v_ref), outs = refs[:2 * DEPTH], refs[2 * DEPTH:2 * DEPTH + 3], refs[2 * DEPTH + 3:]
        layer = pl.program_id(0)
        for l in range(DEPTH):
            @pl.when(layer == l)
            def _(l=l):
                g = ins[2 * l][...].astype(F32)
                for s in range(NDEV - 1):
                    g = g + ins[2 * l + 1][s].astype(F32)
                _adamw_update(g, w_ref, m_ref, v_ref, *outs)

    pick = lambda l: (lambda layer, i, me_ref: jnp.where(layer == l, i, 0))
    in_specs = []
    for l in range(DEPTH):
        in_specs.append(pl.BlockSpec((None, tr, c), lambda layer, i, me_ref, f=pick(l): (me_ref[0], f(layer, i, me_ref), 0)))
        in_specs.append(pl.BlockSpec((NDEV - 1, tr, c), lambda layer, i, me_ref, f=pick(l): (0, f(layer, i, me_ref), 0)))
    blk = pl.BlockSpec((tr, c), lambda layer, i, me_ref: (layer * nt + i, 0))
    sh = jax.ShapeDtypeStruct((DEPTH * r, c), F32)
    args = [a for l in range(DEPTH) for a in (own[l], lands[l])]
    return pl.pallas_call(
        body, name=name, out_shape=(sh, sh, sh, sh),
        grid_spec=pltpu.PrefetchScalarGridSpec(
            num_scalar_prefetch=1, grid=(DEPTH, nt), in_specs=in_specs + [blk, blk, blk],
            out_specs=(blk, blk, blk, blk)),
        compiler_params=_params(VMEM_BIG))(me, *args, w, m, v)


def adamw(parts, w, m, v, tr, name):
    ns, r, c = parts.shape

    def body(p_ref, w_ref, m_ref, v_ref, g_ref, d_ref, nm_ref, nv_ref):
        g = p_ref[0].astype(F32)
        for s in range(1, ns):
            g = g + p_ref[s].astype(F32)
        _adamw_update(g, w_ref, m_ref, v_ref, g_ref, d_ref, nm_ref, nv_ref)

    blk = pl.BlockSpec((tr, c), lambda i: (i, 0))
    sh = jax.ShapeDtypeStruct((r, c), F32)
    return pl.pallas_call(
        body, name=name, out_shape=(sh, sh, sh, sh), grid=(r // tr,),
        in_specs=[pl.BlockSpec((ns, tr, c), lambda i: (0, i, 0)), blk, blk, blk],
        out_specs=(blk, blk, blk, blk), compiler_params=_params(VMEM_BIG))(parts, w, m, v)


def local_step(x, target, P, hooks):
    tb = _ret_tables()
    h = jnp.concatenate([jnp.zeros((PAD, D), F32), P["meta"], x], axis=0)
    stash = []
    for l in range(DEPTH):
        s = {"h": h}
        s["w_in"], s["cv_pw"], s["w_out"] = hooks["mixer_w"](l, h)
        u = s["u"] = rms_fwd(h, P["norm_mix_g"][l], BF16, f"rms_mix_{l}", after=hooks["first_after"] if l == 0 else None)
        proj = s["proj"] = mm_nt(f"in_proj_{l}", u, s["w_in"], 768)
        s["o"] = attn_fwd(proj, P["q_norm_g"][l], P["k_norm_g"][l], P["attn_sinks"][l], f"attn_fwd_{l}")
        y_att = rms_fwd(s["o"], P["attn_out_g"][l], BF16, f"rms_att_{l}")
        y_cv = conv_fwd(proj, P["cv_dw_w"][l], P["cv_dw_b"][l], P["cv_ln_g"][l], P["cv_ln_b"][l], s["cv_pw"],
                        P["cv_out_g"][l], f"conv_fwd_{l}")
        y_ret, s["states"] = ret_fwd(proj, P["ret_gn_g"][l], tb, f"ret_fwd_{l}")
        ycat = s["ycat"] = jnp.concatenate([y_att, y_cv.astype(BF16), y_ret.astype(BF16)], axis=1)
        h = mm_nn(f"out_proj_{l}", ycat, s["w_out"], 512, res=h)
        s["h1"] = h
        s["ffn_up"], s["ffn_down"] = hooks["ffn_w"](l, h)
        u2 = s["u2"] = rms_fwd(h, P["norm_ffn_g"][l], BF16, f"rms_ffn_{l}")
        s["f"] = ffn_up_fwd(f"ffn_up_{l}", u2, s["ffn_up"])
        a, s["gc"], s["uc"] = ffn_act_fwd(s["f"], P["ffn_dw_w"][l], P["ffn_dw_b"][l], f"ffn_act_fwd_{l}")
        s["a"] = a
        h = ffn_down_fwd(f"ffn_down_{l}", a, s["ffn_down"], h)
        stash.append(s)

    dh, dhm, loss = loss_head(h, target)
    G = {n: [None] * DEPTH for n in ("norm_mix_g", "q_norm_g", "k_norm_g", "attn_sinks", "attn_out_g", "cv_dw_w",
                                     "cv_dw_b", "cv_ln_g", "cv_ln_b", "cv_out_g", "ret_gn_g", "norm_ffn_g",
                                     "ffn_dw_w", "ffn_dw_b")}
    for l in reversed(range(DEPTH)):
        s = stash[l]
        da = ffn_down_dx(f"ffn_down_dx_{l}", dhm, s["ffn_down"])
        dw_down = ffn_down_dw(f"ffn_down_dw_{l}", s["a"], dhm)
        dfg, dfu, dwg, dwu, dbg, dbu = ffn_act_bwd(s["f"], s["gc"], s["uc"], da, P["ffn_dw_w"][l], f"ffn_act_bwd_{l}")
        G["ffn_dw_w"][l] = jnp.concatenate([dwg, dwu], axis=0)
        G["ffn_dw_b"][l] = jnp.concatenate([dbg, dbu], axis=0)
        du2 = ffn_up_dx(f"ffn_up_dx_{l}", dfg, dfu, s["ffn_up"])
        dw_up = ffn_up_dw(f"ffn_up_dw_{l}", s["u2"], dfg, dfu)
        after = hooks["ffn_grads"](l, dw_down, dw_up)
        dh, dhm, G["norm_ffn_g"][l] = rms_bwd(s["h1"], P["norm_ffn_g"][l], du2, dh, f"rms_ffn_bwd_{l}", after=after)
        dycat = mm_nt(f"out_proj_dx_{l}", dhm, s["w_out"], 512)
        dw_out = mm_ta(f"out_proj_dw_{l}", s["ycat"], dhm, 512)
        do, _, G["attn_out_g"][l] = rms_bwd(s["o"], P["attn_out_g"][l], dycat, None, f"rms_att_bwd_{l}")
        (dq, dk_c, dk_p, dk_m, dv_c, dv_p, dv_m, G["q_norm_g"][l], G["k_norm_g"][l], dsk) = attn_bwd(
            s["proj"], do, P["q_norm_g"][l], P["k_norm_g"][l], P["attn_sinks"][l], f"attn_bwd_{l}")
        G["attn_sinks"][l] = dsk[:, 0]
        shift = lambda z: jnp.concatenate([z[T:], jnp.zeros((T, NKV * HD), F32)], axis=0)
        dk = (dk_c + shift(dk_p)).at[:T].add(dk_m)
        dv = (dv_c + shift(dv_p)).at[:T].add(dv_m)
        (dca, dcb, G["cv_dw_w"][l], G["cv_dw_b"][l], G["cv_ln_g"][l], G["cv_ln_b"][l], G["cv_out_g"][l],
         dpw) = conv_bwd(s["proj"], dycat, P["cv_dw_w"][l], P["cv_dw_b"][l], P["cv_ln_g"][l],
                         P["cv_ln_b"][l], s["cv_pw"], P["cv_out_g"][l], f"conv_bwd_{l}")
        dret, G["ret_gn_g"][l] = ret_bwd(s["proj"], dycat, s["states"], P["ret_gn_g"][l], tb, f"ret_bwd_{l}")
        dproj = jnp.concatenate([dq, dk, dv, dca, dcb, dret], axis=1).astype(BF16)
        du = mm_nn(f"in_proj_dx_{l}", dproj, s["w_in"], 512)
        dw_in = mm_ta(f"in_proj_dw_{l}", dproj, s["u"], 768)
        after = hooks["mixer_grads"](l, dw_out, dpw, dw_in)
        dh, dhm, G["norm_mix_g"][l] = rms_bwd(s["h"], P["norm_mix_g"][l], du, dh, f"rms_mix_bwd_{l}", after=after,
                                              last=(l == 0))
    return loss[0, 0], dh, dhm[PAD:T], G


_SMALL = ("meta", "norm_mix_g", "q_norm_g", "k_norm_g", "attn_sinks", "attn_out_g", "cv_dw_w", "cv_dw_b", "cv_ln_g",
          "cv_ln_b", "cv_out_g", "ret_gn_g", "norm_ffn_g", "ffn_dw_w", "ffn_dw_b")
_BIG = ("w_in", "cv_pw", "w_out", "ffn_up", "ffn_down")
_TRANSPOSED = ("w_in", "ffn_up")
_ORDER = ("meta", "norm_mix_g", "w_in", "q_norm_g", "k_norm_g", "attn_sinks", "attn_out_g", "cv_dw_w", "cv_dw_b",
          "cv_ln_g", "cv_ln_b", "cv_pw", "cv_out_g", "ret_gn_g", "w_out", "norm_ffn_g", "ffn_up", "ffn_dw_w",
          "ffn_dw_b", "ffn_down")
_SMALL_SHARDED = {"meta": D, "cv_dw_w": CV, "ffn_dw_w": 2 * D_FF}


def _pack(arrs):
    flat = jnp.concatenate([a.reshape(-1) for a in arrs])
    n = flat.shape[0]
    rows = -(-n // 1024) * 8
    return jnp.pad(flat, (0, rows * 128 - n)).reshape(rows, 128)


def _unpack(packed, shapes):
    flat = packed.reshape(-1)
    out, off = [], 0
    for s in shapes:
        n = int(np.prod(s))
        out.append(flat[off:off + n].reshape(s))
        off += n
    return out


def kernel(x, meta, norm_mix_g, w_in, q_norm_g, k_norm_g, attn_sinks, attn_out_g, cv_dw_w, cv_dw_b, cv_ln_g, cv_ln_b, cv_pw, cv_out_g, ret_gn_g, w_out, norm_ffn_g, ffn_up, ffn_dw_w, ffn_dw_b, ffn_down, loss_target, m_meta, m_norm_mix_g, m_w_in, m_q_norm_g, m_k_norm_g, m_attn_sinks, m_attn_out_g, m_cv_dw_w, m_cv_dw_b, m_cv_ln_g, m_cv_ln_b, m_cv_pw, m_cv_out_g, m_ret_gn_g, m_w_out, m_norm_ffn_g, m_ffn_up, m_ffn_dw_w, m_ffn_dw_b, m_ffn_down, v_meta, v_norm_mix_g, v_w_in, v_q_norm_g, v_k_norm_g, v_attn_sinks, v_attn_out_g, v_cv_dw_w, v_cv_dw_b, v_cv_ln_g, v_cv_ln_b, v_cv_pw, v_cv_out_g, v_ret_gn_g, v_w_out, v_norm_ffn_g, v_ffn_up, v_ffn_dw_w, v_ffn_dw_b, v_ffn_down):
    W = dict(meta=meta, norm_mix_g=norm_mix_g, w_in=w_in, q_norm_g=q_norm_g, k_norm_g=k_norm_g,
             attn_sinks=attn_sinks, attn_out_g=attn_out_g, cv_dw_w=cv_dw_w, cv_dw_b=cv_dw_b, cv_ln_g=cv_ln_g,
             cv_ln_b=cv_ln_b, cv_pw=cv_pw, cv_out_g=cv_out_g, ret_gn_g=ret_gn_g, w_out=w_out,
             norm_ffn_g=norm_ffn_g, ffn_up=ffn_up, ffn_dw_w=ffn_dw_w, ffn_dw_b=ffn_dw_b, ffn_down=ffn_down)
    M = dict(meta=m_meta, norm_mix_g=m_norm_mix_g, w_in=m_w_in, q_norm_g=m_q_norm_g, k_norm_g=m_k_norm_g,
             attn_sinks=m_attn_sinks, attn_out_g=m_attn_out_g, cv_dw_w=m_cv_dw_w, cv_dw_b=m_cv_dw_b,
             cv_ln_g=m_cv_ln_g, cv_ln_b=m_cv_ln_b, cv_pw=m_cv_pw, cv_out_g=m_cv_out_g, ret_gn_g=m_ret_gn_g,
             w_out=m_w_out, norm_ffn_g=m_norm_ffn_g, ffn_up=m_ffn_up, ffn_dw_w=m_ffn_dw_w, ffn_dw_b=m_ffn_dw_b,
             ffn_down=m_ffn_down)
    V = dict(meta=v_meta, norm_mix_g=v_norm_mix_g, w_in=v_w_in, q_norm_g=v_q_norm_g, k_norm_g=v_k_norm_g,
             attn_sinks=v_attn_sinks, attn_out_g=v_attn_out_g, cv_dw_w=v_cv_dw_w, cv_dw_b=v_cv_dw_b,
             cv_ln_g=v_cv_ln_g, cv_ln_b=v_cv_ln_b, cv_pw=v_cv_pw, cv_out_g=v_cv_out_g, ret_gn_g=v_ret_gn_g,
             w_out=v_w_out, norm_ffn_g=v_norm_ffn_g, ffn_up=v_ffn_up, ffn_dw_w=v_ffn_dw_w, ffn_dw_b=v_ffn_dw_b,
             ffn_down=v_ffn_down)
    me = 4 * lax.axis_index("x") + 2 * lax.axis_index("y") + lax.axis_index("c")
    for n in _TRANSPOSED:
        W[n], M[n], V[n] = (a.transpose(0, 2, 1) for a in (W[n], M[n], V[n]))

    sh = {n: [W[n][l].astype(BF16) for l in range(DEPTH)] for n in _BIG}
    mix = lambda l: [sh["w_in"][l], sh["cv_pw"][l], sh["w_out"][l]]
    ffn = lambda l: [sh["ffn_up"][l], sh["ffn_down"][l]]
    first = all_gather(mix(0) + [meta, cv_dw_w, ffn_dw_w], "gather_first")
    g_meta, g_cdw, g_fdw = first[3:6]

    def landing(shards):
        return [lax.dynamic_update_slice(lax.empty((NDEV,) + s.shape, s.dtype), s[None], (me,) + (0,) * s.ndim)
                for s in shards]

    gathers = {("ffn", 0): gather_start("gather_ffn0_start", ffn(0), landing(ffn(0)), after=[first[0]])}
    gathers["mix", 1] = gather_start("gather_mix1_start", mix(1), landing(mix(1)), after=[gathers["ffn", 0][4]])
    gathers["ffn", 1] = gather_start("gather_ffn1_start", ffn(1), landing(ffn(1)), after=[gathers["mix", 1][4]])

    def mixer_w(l, h):
        g_in, g_pw, g_out = first[0:3] if l == 0 else gather_wait(f"gather_mix{l}_wait", gathers["mix", l], after=[h])[1]
        return g_in.reshape(IN_W, D), g_pw.reshape(CV, CV), g_out.reshape(D, D)

    def ffn_w(l, h1):
        return gather_wait(f"gather_ffn{l}_wait", gathers["ffn", l], after=[h1])[1]

    scatters = {}

    def ffn_grads(l, dw_down, dw_up):
        scatters["ffn", l] = scatter_start(f"scatter_ffn{l}_start", [dw_up, dw_down.reshape(NDEV, DNR, D)])
        return scatters["ffn", l][4]

    def mixer_grads(l, dw_out, dpw, dw_in):
        grads = [dw_in.reshape(NDEV, IN_W // NDEV, D), dpw.astype(BF16).reshape(NDEV, CV // NDEV, CV),
                 dw_out.reshape(NDEV, D // NDEV, D)]
        scatters["mix", l] = scatter_start(f"scatter_mix{l}_start", grads)
        return scatters["mix", l][4]

    P = dict(
        meta=g_meta.transpose(1, 0, 2).reshape(N_META, D),
        cv_dw_w=g_cdw.transpose(1, 2, 0, 3).reshape(DEPTH, CONV_K, CV),
        ffn_dw_w=g_fdw.transpose(1, 0, 2, 3),
        ffn_dw_b=ffn_dw_b.reshape(DEPTH, NDEV, 1, UPW),
        attn_sinks=attn_sinks,
    )
    for n in ("norm_mix_g", "q_norm_g", "k_norm_g", "attn_out_g", "cv_dw_b", "cv_ln_g", "cv_ln_b", "cv_out_g",
              "ret_gn_g", "norm_ffn_g"):
        P[n] = W[n].reshape(DEPTH, 1, -1)

    hooks = dict(mixer_w=mixer_w, ffn_w=ffn_w, ffn_grads=ffn_grads, mixer_grads=mixer_grads,
                 first_after=gathers["ffn", 1][4])
    loss_part, dx, dmeta, G = local_step(x[0], loss_target[0], P, hooks)
    loss = lax.psum(loss_part, ("x", "y", "c"))

    small_full = {
        "meta": dmeta,
        "cv_dw_w": jnp.stack(G["cv_dw_w"]),
        "ffn_dw_w": jnp.stack([g.transpose(1, 0, 2).reshape(FFN_K, 2 * D_FF) for g in G["ffn_dw_w"]]),
        "ffn_dw_b": jnp.stack([g.reshape(2 * D_FF) for g in G["ffn_dw_b"]]),
        "attn_sinks": jnp.stack(G["attn_sinks"]),
    }
    for n in _SMALL:
        if n not in small_full:
            small_full[n] = jnp.stack([g.reshape(-1) for g in G[n]])
    shapes = [small_full[n].shape for n in _SMALL]
    summed = _unpack(all_reduce_small(_pack([small_full[n] for n in _SMALL]), "reduce_small"), shapes)
    small_g = {}
    for n, g in zip(_SMALL, summed):
        if n in _SMALL_SHARDED:
            width = _SMALL_SHARDED[n] // NDEV
            g = lax.dynamic_slice_in_dim(g, me * width, width, axis=g.ndim - 1)
        small_g[n] = g

    out = {}
    tiles = {"w_in": 144, "cv_pw": 32, "w_out": 128, "ffn_up": 176, "ffn_down": 176}
    me1 = me.astype(jnp.int32).reshape(1)
    own, lands = {n: [None] * DEPTH for n in _BIG}, {n: [None] * DEPTH for n in _BIG}

    def arrived(kind, names, after):
        for l in range(DEPTH):
            srcs, got = scatter_wait(f"scatter_{kind}{l}_wait", scatters[kind, l], after=after)
            for n, s_, g_ in zip(names, srcs, got):
                own[n][l], lands[n][l] = s_, g_

    def update(names):
        for n in names:
            shard = W[n].shape
            rows, cols = shard[0] * shard[1], shard[2]
            res = adamw_big(own[n], lands[n], W[n].reshape(rows, cols), M[n].reshape(rows, cols),
                            V[n].reshape(rows, cols), me1, tiles[n], f"adamw_{n}")
            out[n] = [r.reshape(shard) for r in res]

    arrived("ffn", ("ffn_up", "ffn_down"), [dx])
    update(("ffn_up", "ffn_down"))
    sshapes = [W[n].shape for n in _SMALL]
    packed_g = _pack([small_g[n] for n in _SMALL])
    res = adamw(packed_g[None], _pack([W[n] for n in _SMALL]), _pack([M[n] for n in _SMALL]),
                _pack([V[n] for n in _SMALL]), packed_g.shape[0], "adamw_small")
    for i, r in enumerate(res):
        for n, a in zip(_SMALL, _unpack(r, sshapes)):
            out.setdefault(n, [None] * 4)[i] = a
    arrived("mix", ("w_in", "cv_pw", "w_out"), [out["ffn_down"][0], res[0]])
    update(("w_in", "cv_pw", "w_out"))
    for n in _TRANSPOSED:
        out[n] = [r.transpose(0, 2, 1) for r in out[n]]

    return (loss, dx[None], *[out[n][0] for n in _ORDER], *[out[n][1] for n in _ORDER],
            *[out[n][2] for n in _ORDER], *[out[n][3] for n in _ORDER])
```

```python
import functools
import math

import numpy as np
import jax
import jax.numpy as jnp
from jax import lax
from jax.experimental import pallas as pl
from jax.experimental.pallas import tpu as pltpu

F32 = jnp.float32
BF16 = jnp.bfloat16

D = 1024
SEQ = 2048
DEPTH = 2
T = 128
L = SEQ + T
NB = L // T
N_META = 16
PAD = T - N_META
HD = 64
NQ = 8
NKV = 2
GQA = NQ // NKV
CV = 256
CONV_K = 31
RH = 4
D_FF = 2816
FFN_K = 3
IN_W = 2304
RMS_EPS = 1e-6
LN_EPS = 1e-5
NEG = -1e30
NDEV = 8
UPW = 2 * D_FF // NDEV
DNR = D_FF // NDEV
NPAIR = NDEV // 2

ADAM_LR, ADAM_B1, ADAM_B2, ADAM_EPS, ADAM_WD, ADAM_STEP = 0.001, 0.9, 0.999, 1e-08, 0.01, 10

VMEM_BIG = 56 * 1024 * 1024

MESH = pl.DeviceIdType.MESH


def _params(vmem=None):
    return pltpu.CompilerParams(vmem_limit_bytes=vmem) if vmem else None


def _dot(a, b, nt=False):
    return lax.dot_general(a, b, (((1,), (1 if nt else 0,)), ((), ())), preferred_element_type=F32)


def _sig(x):
    return 1.0 / (1.0 + jnp.exp(-x))


def _bf(x):
    return x.astype(BF16)


def _stack_rows(rows):
    idx = lax.broadcasted_iota(jnp.int32, (len(rows), rows[0].shape[1]), 0)
    out = jnp.zeros((len(rows), rows[0].shape[1]), F32)
    for r, v in enumerate(rows):
        out = jnp.where(idx == r, v, out)
    return out


def rms_fwd(x, g, out_dtype, name, after=None):
    n, w = x.shape

    def body(x_ref, g_ref, *rest):
        o_ref = rest[-1]
        xv = x_ref[...]
        r = lax.rsqrt(jnp.mean(xv * xv, axis=-1, keepdims=True) + RMS_EPS)
        o_ref[...] = (xv * r * g_ref[...]).astype(o_ref.dtype)

    deps = [] if after is None else [after]
    return pl.pallas_call(
        body, name=name, out_shape=jax.ShapeDtypeStruct((n, w), out_dtype), grid=(n // T,),
        in_specs=[pl.BlockSpec((T, w), lambda i: (i, 0)), pl.BlockSpec((1, w), lambda i: (0, 0))] + [_HBM] * len(deps),
        out_specs=pl.BlockSpec((T, w), lambda i: (i, 0)))(x, g, *deps)


def rms_bwd(x, g, dy, dres, name, after=None, dy_col=0, last=False):
    n, w = x.shape
    has_res = dres is not None
    deps = [] if after is None else [after]

    def body(x_ref, g_ref, dy_ref, *rest):
        rest = rest[:len(rest) - 3 - len(deps)] + rest[len(rest) - 3:]
        if has_res:
            dres_ref, dx_ref, dxm_ref, dg_ref = rest
        else:
            dx_ref, dxm_ref, dg_ref = rest
        i = pl.program_id(0)
        xv = x_ref[...]
        r = lax.rsqrt(jnp.mean(xv * xv, axis=-1, keepdims=True) + RMS_EPS)
        xh = xv * r
        dyv = dy_ref[...]
        dxh = dyv * g_ref[...]
        dx = r * (dxh - xh * jnp.mean(dxh * xh, axis=-1, keepdims=True))
        if has_res:
            dx = dx + dres_ref[...]
        dx_ref[...] = dx
        if last:
            @pl.when(i == 0)
            def _():
                dxm_ref[...] = dx
        else:
            rows = i * T + lax.broadcasted_iota(jnp.int32, (T, 1), 0)
            dxm_ref[...] = jnp.where(rows >= PAD, dx, 0.0).astype(BF16)
        part = jnp.sum(dyv * xh, axis=0, keepdims=True)

        @pl.when(i == 0)
        def _():
            dg_ref[...] = part

        @pl.when(i > 0)
        def _():
            dg_ref[...] += part

    row = pl.BlockSpec((T, w), lambda i: (i, 0))
    vec = pl.BlockSpec((1, w), lambda i: (0, 0))
    ins = [x, g, dy] + ([dres] if has_res else []) + deps
    if last:
        out_shape = (jax.ShapeDtypeStruct((n - T, w), F32), jax.ShapeDtypeStruct((T, w), F32))
        out_specs = (pl.BlockSpec((T, w), lambda i: (jnp.maximum(i - 1, 0), 0)), pl.BlockSpec((T, w), lambda i: (0, 0)))
    else:
        out_shape = (jax.ShapeDtypeStruct((n, w), F32), jax.ShapeDtypeStruct((n, w), BF16))
        out_specs = (row, row)
    return pl.pallas_call(
        body, name=name, out_shape=(*out_shape, jax.ShapeDtypeStruct((1, w), F32)), grid=(n // T,),
        in_specs=[row, vec, pl.BlockSpec((T, w), lambda i: (i, dy_col))] + ([row] if has_res else [])
        + [_HBM] * len(deps),
        out_specs=(*out_specs, vec))(*ins)


def loss_head(h, target):
    def body(h_ref, t_ref, dh_ref, dhm_ref, loss_ref):
        n = pl.program_id(0)
        e = jnp.where(n > 0, h_ref[...] - t_ref[...], 0.0)
        dh = e * (1.0 / D)
        dh_ref[...] = dh
        dhm_ref[...] = dh.astype(BF16)
        part = jnp.sum(jnp.sum(e * e, axis=1, keepdims=True), axis=0, keepdims=True) * (0.5 / D)

        @pl.when(n == 0)
        def _():
            loss_ref[...] = jnp.zeros_like(loss_ref)

        @pl.when(n > 0)
        def _():
            loss_ref[...] += jnp.broadcast_to(part, loss_ref.shape)

    row = pl.BlockSpec((T, D), lambda n: (n, 0))
    return pl.pallas_call(
        body, name="loss_head",
        out_shape=(jax.ShapeDtypeStruct((L, D), F32), jax.ShapeDtypeStruct((L, D), BF16),
                   jax.ShapeDtypeStruct((8, 128), F32)),
        grid=(NB,),
        in_specs=[row, pl.BlockSpec((T, D), lambda n: (jnp.maximum(n - 1, 0), 0))],
        out_specs=(row, row, pl.BlockSpec((8, 128), lambda n: (0, 0))))(h, target)


def _mm(name, a, b, *, grid, a_spec, b_spec, o_spec, out_shape, nt=False, ta=False, red=False, res=None,
        res_spec=None):
    def body(a_ref, b_ref, *rest):
        o_ref = rest[-1]
        av = a_ref[...]
        bv = b_ref[...]
        if bv.ndim == 3:
            bv = bv.reshape(bv.shape[0] * bv.shape[1], bv.shape[2])
        if ta:
            acc = lax.dot_general(av, bv, (((0,), (0,)), ((), ())), preferred_element_type=F32)
        else:
            acc = _dot(av, bv, nt)
        if red:
            k = pl.program_id(0)

            @pl.when(k == 0)
            def _():
                o_ref[...] = acc

            @pl.when(k > 0)
            def _():
                o_ref[...] += acc
        else:
            if res is not None:
                rows = lax.broadcasted_iota(jnp.int32, (acc.shape[0], 1), 0)
                acc = rest[0][...] + jnp.where(rows >= PAD, acc, 0.0)
            o_ref[...] = acc.astype(o_ref.dtype)

    ins = [a, b] + ([res] if res is not None else [])
    specs = [a_spec, b_spec] + ([res_spec] if res is not None else [])
    return pl.pallas_call(body, name=name, out_shape=out_shape, grid=grid, in_specs=specs, out_specs=o_spec,
                          compiler_params=_params(VMEM_BIG))(*ins)


def mm_nn(name, a, b, tn, out_dtype=F32, res=None):
    m, k = a.shape
    n = b.shape[1]
    return _mm(name, a, b, grid=(n // tn,),
               a_spec=pl.BlockSpec((m, k), lambda j: (0, 0)), b_spec=pl.BlockSpec((k, tn), lambda j: (0, j)),
               o_spec=pl.BlockSpec((m, tn), lambda j: (0, j)), out_shape=jax.ShapeDtypeStruct((m, n), out_dtype),
               res=res, res_spec=pl.BlockSpec((m, tn), lambda j: (0, j)))


def mm_nt(name, a, b, tn):
    m, k = a.shape
    n = b.shape[0]
    return _mm(name, a, b, grid=(n // tn,), nt=True,
               a_spec=pl.BlockSpec((m, k), lambda j: (0, 0)), b_spec=pl.BlockSpec((tn, k), lambda j: (j, 0)),
               o_spec=pl.BlockSpec((m, tn), lambda j: (0, j)), out_shape=jax.ShapeDtypeStruct((m, n), F32))


def mm_ta(name, a, b, tm, out_dtype=BF16):
    k, m = a.shape
    n = b.shape[1]
    return _mm(name, a, b, grid=(m // tm,), ta=True,
               a_spec=pl.BlockSpec((k, tm), lambda j: (0, j)), b_spec=pl.BlockSpec((k, n), lambda j: (0, 0)),
               o_spec=pl.BlockSpec((tm, n), lambda j: (j, 0)), out_shape=jax.ShapeDtypeStruct((m, n), out_dtype))


def ffn_up_fwd(name, u2, wupt):
    return _mm(name, u2, wupt, grid=(NDEV,), nt=True,
               a_spec=pl.BlockSpec((L, D), lambda j: (0, 0)),
               b_spec=pl.BlockSpec((None, UPW, D), lambda j: (j, 0, 0)),
               o_spec=pl.BlockSpec((None, L, UPW), lambda j: (j, 0, 0)),
               out_shape=jax.ShapeDtypeStruct((NDEV, L, UPW), F32))


def _slab_specs():
    gate = pl.BlockSpec((None, L, UPW), lambda j: (jnp.minimum(j, NPAIR - 1), 0, 0))
    up = pl.BlockSpec((None, L, UPW), lambda j: (jnp.maximum(j - NPAIR, 0), 0, 0))
    return gate, up


def ffn_up_dx(name, dfg, dfu, wupt):
    def body(g_ref, u_ref, b_ref, o_ref):
        j = pl.program_id(0)

        @pl.when(j == 0)
        def _():
            o_ref[...] = _dot(g_ref[...], b_ref[...])

        @pl.when(jnp.logical_and(j > 0, j < NPAIR))
        def _():
            o_ref[...] += _dot(g_ref[...], b_ref[...])

        @pl.when(j >= NPAIR)
        def _():
            o_ref[...] += _dot(u_ref[...], b_ref[...])

    gate, up = _slab_specs()
    return pl.pallas_call(
        body, name=name, out_shape=jax.ShapeDtypeStruct((L, D), F32), grid=(NDEV,),
        in_specs=[gate, up, pl.BlockSpec((None, UPW, D), lambda j: (j, 0, 0))],
        out_specs=pl.BlockSpec((L, D), lambda j: (0, 0)), compiler_params=_params(VMEM_BIG))(dfg, dfu, wupt)


def ffn_up_dw(name, u2, dfg, dfu):
    tdot = lambda a, b: lax.dot_general(a, b, (((0,), (0,)), ((), ())), preferred_element_type=F32)

    def body(a_ref, g_ref, u_ref, o_ref):
        j = pl.program_id(0)

        @pl.when(j < NPAIR)
        def _():
            o_ref[...] = tdot(g_ref[...], a_ref[...]).astype(o_ref.dtype)

        @pl.when(j >= NPAIR)
        def _():
            o_ref[...] = tdot(u_ref[...], a_ref[...]).astype(o_ref.dtype)

    gate, up = _slab_specs()
    return pl.pallas_call(
        body, name=name, out_shape=jax.ShapeDtypeStruct((NDEV, UPW, D), BF16), grid=(NDEV,),
        in_specs=[pl.BlockSpec((L, D), lambda j: (0, 0)), gate, up],
        out_specs=pl.BlockSpec((None, UPW, D), lambda j: (j, 0, 0)), compiler_params=_params(VMEM_BIG))(u2, dfg, dfu)


def ffn_down_fwd(name, a, wdn, res, tn=256):
    def body(a_ref, b_ref, r_ref, o_ref):
        acc = jnp.zeros((L, tn), F32)
        for g in range(NPAIR):
            bv = b_ref[2 * g:2 * g + 2]
            acc = acc + _dot(a_ref[g], bv.reshape(2 * DNR, tn))
        rows = lax.broadcasted_iota(jnp.int32, (L, 1), 0)
        o_ref[...] = r_ref[...] + jnp.where(rows >= PAD, acc, 0.0)

    return pl.pallas_call(
        body, name=name, out_shape=jax.ShapeDtypeStruct((L, D), F32), grid=(D // tn,),
        in_specs=[pl.BlockSpec((NPAIR, L, UPW), lambda j: (0, 0, 0)),
                  pl.BlockSpec((NDEV, DNR, tn), lambda j: (0, 0, j)),
                  pl.BlockSpec((L, tn), lambda j: (0, j))],
        out_specs=pl.BlockSpec((L, tn), lambda j: (0, j)),
        compiler_params=_params(VMEM_BIG))(a, wdn, res)


def ffn_down_dx(name, dh, wdn):
    return _mm(name, dh, wdn, grid=(NPAIR,), nt=True,
               a_spec=pl.BlockSpec((L, D), lambda g: (0, 0)),
               b_spec=pl.BlockSpec((2, DNR, D), lambda g: (g, 0, 0)),
               o_spec=pl.BlockSpec((None, L, UPW), lambda g: (g, 0, 0)),
               out_shape=jax.ShapeDtypeStruct((NPAIR, L, UPW), F32))


def ffn_down_dw(name, a, dh):
    return _mm(name, a, dh, grid=(NPAIR,), ta=True,
               a_spec=pl.BlockSpec((None, L, UPW), lambda g: (g, 0, 0)),
               b_spec=pl.BlockSpec((L, D), lambda g: (0, 0)),
               o_spec=pl.BlockSpec((UPW, D), lambda g: (g, 0)),
               out_shape=jax.ShapeDtypeStruct((D_FF, D), BF16))


_SLOPES = [2.0 ** (-8.0 * (h + 1) / NQ) for h in range(NQ)]
_SCALE = HD ** -0.5


QR = GQA * T
KC = 3 * T


def _attn_geometry(g, n, sink_ref):
    row = lax.broadcasted_iota(jnp.int32, (QR, KC), 0)
    col = lax.broadcasted_iota(jnp.int32, (QR, KC), 1)
    i = row & (T - 1)
    j = col & (T - 1)
    blk = col // T
    d_meta = n * T + i - j
    ok_meta = (j >= PAD) & (d_meta >= 0)
    ok_prev = j > i + jnp.where(n >= 2, 0, T)
    ok_cur = j <= i - jnp.where(n >= 1, 0, T)
    ok = ((blk == 0) & ok_meta) | ((blk == 1) & ok_prev) | ((blk == 2) & ok_cur)
    dist = jnp.where(blk == 0, jnp.minimum(d_meta, T), jnp.where(blk == 1, T + i - j, i - j)).astype(F32)
    head = lax.broadcasted_iota(jnp.int32, (QR, 1), 0) // T
    slope = jnp.zeros((QR, 1), F32)
    sink = jnp.zeros((QR, 1), F32)
    for hh in range(GQA):
        slope = jnp.where(head == hh, jnp.where(g == 0, _SLOPES[hh], _SLOPES[GQA + hh]), slope)
        sink = jnp.where(head == hh, sink_ref[g * GQA + hh], sink)
    return ok, slope * dist, sink


def _attn_probs(qn, keys, geo):
    ok, penalty, sink = geo
    s = jnp.where(ok, _dot(qn, keys, nt=True) * _SCALE - penalty, NEG)
    m = jnp.maximum(jnp.max(s, axis=-1, keepdims=True), sink)
    e = jnp.exp(s - m)
    e_sink = jnp.exp(sink - m)
    inv = 1.0 / (e.sum(axis=-1, keepdims=True) + e_sink)
    return e, inv, e_sink * inv


def _rms_rows(x, g):
    r = lax.rsqrt(jnp.mean(x * x, axis=-1, keepdims=True) + RMS_EPS)
    xh = x * r
    return xh * g, xh, r


def _rms_rows_bwd(dy, g, xh, r):
    dxh = dy * g
    return r * (dxh - xh * jnp.mean(dxh * xh, axis=-1, keepdims=True))


def _rows3(m_ref, p_ref, c_ref):
    return jnp.concatenate([m_ref[...], p_ref[...], c_ref[...]], axis=0)


_ATT_K, _ATT_V = 4, 5


def _nat_specs():
    qspec = pl.BlockSpec((T, GQA * HD), lambda n, g: (n, g))
    kv = lambda col: (pl.BlockSpec((T, 2 * HD), lambda n, g: (n, col)),
                      pl.BlockSpec((T, 2 * HD), lambda n, g: (jnp.maximum(n - 1, 0), col)),
                      pl.BlockSpec((T, 2 * HD), lambda n, g: (0, col)))
    return qspec, kv(_ATT_K), kv(_ATT_V)


def _group_lanes(g, x):
    return jnp.where(g == 0, x[:, :HD], x[:, HD:])


def _stack_heads(x):
    return jnp.concatenate([x[:, hh * HD:(hh + 1) * HD] for hh in range(GQA)], axis=0)


def _unstack_heads(x):
    return jnp.concatenate([x[hh * T:(hh + 1) * T] for hh in range(GQA)], axis=1)


def attn_fwd(proj, qg, kg, sinks, name):
    def body(sink_ref, q_ref, kc_ref, kp_ref, km_ref, vc_ref, vp_ref, vm_ref, qg_ref, kg_ref, o_ref):
        n = pl.program_id(0)
        g = pl.program_id(1)
        geo = _attn_geometry(g, n, sink_ref)
        keys = _bf(_rms_rows(_group_lanes(g, _rows3(km_ref, kp_ref, kc_ref)), kg_ref[...])[0])
        vals = _bf(_group_lanes(g, _rows3(vm_ref, vp_ref, vc_ref)))
        qn = _bf(_rms_rows(_stack_heads(q_ref[...]), qg_ref[...])[0])
        e, inv, _ = _attn_probs(qn, keys, geo)
        o_ref[...] = _unstack_heads(_dot(_bf(e), vals) * inv)

    qspec, (kc, kp, km), (vc, vp, vm) = _nat_specs()
    vec = pl.BlockSpec((1, HD), lambda n, g: (0, 0))
    return pl.pallas_call(
        body, name=name, out_shape=jax.ShapeDtypeStruct((L, NQ * HD), F32), grid=(NB, NKV),
        in_specs=[pl.BlockSpec(memory_space=pltpu.SMEM), qspec, kc, kp, km, vc, vp, vm, vec, vec],
        out_specs=qspec)(sinks, proj, proj, proj, proj, proj, proj, proj, qg, kg)


def attn_bwd(proj, do, qg, kg, sinks, name):
    def body(sink_ref, q_ref, kc_ref, kp_ref, km_ref, vc_ref, vp_ref, vm_ref, do_ref, qg_ref, kg_ref,
             dq_ref, dkc_ref, dkp_ref, dkm_ref, dvc_ref, dvp_ref, dvm_ref, dqg_ref, dkg_ref, dsk_ref):
        n = pl.program_id(0)
        g = pl.program_id(1)

        @pl.when(jnp.logical_and(g == 0, n == 0))
        def _():
            for r in (dkm_ref, dvm_ref, dqg_ref, dkg_ref, dsk_ref):
                r[...] = jnp.zeros_like(r)

        geo = _attn_geometry(g, n, sink_ref)
        kgv = kg_ref[...]
        qgv = qg_ref[...]
        kn_f, kh, kr = _rms_rows(_group_lanes(g, _rows3(km_ref, kp_ref, kc_ref)), kgv)
        keys = _bf(kn_f)
        vals = _bf(_group_lanes(g, _rows3(vm_ref, vp_ref, vc_ref)))
        qn_f, qh, qr = _rms_rows(_stack_heads(q_ref[...]), qgv)
        qn = _bf(qn_f)
        e, inv, p_sink = _attn_probs(qn, keys, geo)
        dob = _bf(_stack_heads(do_ref[...]))
        p = e * inv
        dp = _dot(dob, vals, nt=True)
        delta = (p * dp).sum(axis=-1, keepdims=True)
        head_row = lax.broadcasted_iota(jnp.int32, (NQ, 128), 0)
        dsk = jnp.zeros((NQ, 128), F32)
        for hh in range(GQA):
            part = -jnp.sum((p_sink * delta)[hh * T:(hh + 1) * T], axis=0, keepdims=True)
            dsk = jnp.where(head_row == g * GQA + hh, part, dsk)
        dsk_ref[...] += dsk
        ds = p * (dp - delta)
        dqn = _dot(_bf(ds), keys) * _SCALE
        dq_ref[...] = _unstack_heads(_rms_rows_bwd(dqn, qgv, qh, qr))
        dqg_ref[...] += jnp.sum(dqn * qh, axis=0, keepdims=True)
        dkn = _dot(_bf(ds.T), qn) * _SCALE
        dkg_ref[...] += jnp.sum(dkn * kh, axis=0, keepdims=True)
        dk_all = _rms_rows_bwd(dkn, kgv, kh, kr)
        dv_all = _dot(_bf(p.T), dob)

        for gg in range(NKV):
            @pl.when(g == gg)
            def _(gg=gg):
                lanes = slice(gg * HD, (gg + 1) * HD)
                dkm_ref[:, lanes] += dk_all[0:T]
                dkp_ref[:, lanes] = dk_all[T:2 * T]
                dkc_ref[:, lanes] = dk_all[2 * T:3 * T]
                dvm_ref[:, lanes] += dv_all[0:T]
                dvp_ref[:, lanes] = dv_all[T:2 * T]
                dvc_ref[:, lanes] = dv_all[2 * T:3 * T]

    qspec, (kc, kp, km), (vc, vp, vm) = _nat_specs()
    vec = pl.BlockSpec((1, HD), lambda n, g: (0, 0))
    cur = pl.BlockSpec((T, 2 * HD), lambda n, g: (n, 0))
    meta = pl.BlockSpec((T, 2 * HD), lambda n, g: (0, 0))
    kv_shape = jax.ShapeDtypeStruct((L, 2 * HD), F32)
    meta_shape = jax.ShapeDtypeStruct((T, 2 * HD), F32)
    vec_shape = jax.ShapeDtypeStruct((1, HD), F32)
    return pl.pallas_call(
        body, name=name,
        out_shape=(jax.ShapeDtypeStruct((L, NQ * HD), F32), kv_shape, kv_shape, meta_shape, kv_shape, kv_shape,
                   meta_shape, vec_shape, vec_shape, jax.ShapeDtypeStruct((NQ, 128), F32)),
        grid=(NB, NKV),
        in_specs=[pl.BlockSpec(memory_space=pltpu.SMEM), qspec, kc, kp, km, vc, vp, vm, qspec, vec, vec],
        out_specs=(qspec, cur, cur, meta, cur, cur, meta, vec, vec,
                   pl.BlockSpec((NQ, 128), lambda n, g: (0, 0))))(
            sinks, proj, proj, proj, proj, proj, proj, proj, do, qg, kg)


RW = RH * HD


def _ret_tables():
    h = np.arange(RH, dtype=np.float64)
    lg = np.log1p(-np.exp2(-5.0 - h))
    idx = np.arange(T, dtype=np.float64)
    diff = idx[:, None] - idx[None, :]
    decay = np.where(diff[None] >= 0, np.exp(np.maximum(diff, 0.0)[None] * lg[:, None, None]), 0.0)
    zeta = np.exp((T - 1 - idx)[None, :] * lg[:, None])
    xi = np.exp((idx + 1.0)[None, :] * lg[:, None])
    cd = np.exp(T * lg)
    lanes = lambda a: np.repeat(a.T, HD, axis=1)
    head_of = np.arange(RW) // HD
    same = (head_of[:, None] == head_of[None, :]).astype(np.float64)
    f = lambda a: jnp.asarray(a, F32)
    return dict(decay=f(decay), zeta=f(lanes(zeta)), xi=f(lanes(xi)), cd=f(np.repeat(cd, HD)[None, :]),
                head=f((head_of[None, :] == np.arange(RH)[:, None]).astype(np.float64)[:, None, :]),
                same=f(same), avg=jnp.asarray(same / HD, BF16))


def _seg_mean(x, avg):
    hi = _bf(x)
    lo = _bf(x - hi.astype(F32))
    return _dot(hi, avg) + _dot(lo, avg)


def _ret_specs(col0, order):
    return lambda col: pl.BlockSpec((T, RW), lambda i: (order(i), col0 + col))


def _ret_chunk(q, kf, v, s, tb):
    dec, xi, head = tb
    vb = _bf(v)
    kb = _bf(kf)
    y = _dot(_bf(q * xi), _bf(s))
    a = []
    for h in range(RH):
        a.append(_dot(_bf(q * head[h]), kb, nt=True) * dec[h])
        y = y + head[h] * _dot(_bf(a[h]), vb)
    return a, y


def _gn_rows(y):
    mu = jnp.mean(y, axis=-1, keepdims=True)
    yc = y - mu
    rstd = lax.rsqrt(jnp.mean(yc * yc, axis=-1, keepdims=True) + LN_EPS)
    return yc * rstd, rstd


_RET_COL0 = _RET_Q = 5


def _ret_consts(tb):
    names = ("decay", "zeta", "xi", "cd", "head", "same", "avg")
    full = lambda a: pl.BlockSpec(a.shape, lambda i: (0,) * a.ndim)
    return [tb[n] for n in names], [full(tb[n]) for n in names]


def _ret_inputs(n, q_ref, k_ref, v_ref):
    rows = n * T + lax.broadcasted_iota(jnp.int32, (T, 1), 0)
    valid = rows >= PAD
    return q_ref[...], jnp.where(valid, k_ref[...] * (HD ** -0.5), 0.0), v_ref[...], valid


def ret_fwd(proj, gng, tb, name):
    def body(q_ref, k_ref, v_ref, g_ref, dec_ref, zeta_ref, xi_ref, cd_ref, head_ref, same_ref, avg_ref, gng_ref,
             y_ref, st_ref, s_scr):
        n = pl.program_id(0)

        @pl.when(n == 0)
        def _():
            s_scr[...] = jnp.zeros_like(s_scr)

        s = s_scr[...]
        st_ref[...] = s
        q, kf, v, _ = _ret_inputs(n, q_ref, k_ref, v_ref)
        _, y = _ret_chunk(q, kf, v, s, (dec_ref, xi_ref[...], head_ref))
        s_scr[...] = cd_ref[...] * s + same_ref[...] * _dot(_bf((kf * zeta_ref[...]).T), _bf(v))
        avg = avg_ref[...]
        yc = y - _seg_mean(y, avg)
        yh = yc * lax.rsqrt(_seg_mean(yc * yc, avg) + LN_EPS)
        gv = g_ref[...]
        y_ref[...] = gv * _sig(gv) * (yh * gng_ref[...])

    col = _ret_specs(_RET_COL0, lambda i: i)
    consts, cspecs = _ret_consts(tb)
    return pl.pallas_call(
        body, name=name,
        out_shape=(jax.ShapeDtypeStruct((L, RW), F32), jax.ShapeDtypeStruct((NB, RW, RW), F32)),
        grid=(NB,),
        in_specs=[col(0), col(1), col(2), col(3)] + cspecs + [pl.BlockSpec((1, RW), lambda i: (0, 0))],
        out_specs=(pl.BlockSpec((T, RW), lambda i: (i, 0)), pl.BlockSpec((None, RW, RW), lambda i: (i, 0, 0))),
        scratch_shapes=[pltpu.VMEM((RW, RW), F32)])(proj, proj, proj, proj, *consts, gng)


def ret_bwd(proj, dycat, states, gng, tb, name):
    def body(q_ref, k_ref, v_ref, g_ref, do_ref, st_ref, dec_ref, zeta_ref, xi_ref, cd_ref, head_ref, same_ref,
             avg_ref, gng_ref, d_ref, dgn_ref, ds_scr):
        i = pl.program_id(0)
        n = NB - 1 - i

        @pl.when(i == 0)
        def _():
            ds_scr[...] = jnp.zeros_like(ds_scr)

        dsn = _bf(ds_scr[...])
        s = st_ref[...]
        sb = _bf(s)
        q, kf, v, valid = _ret_inputs(n, q_ref, k_ref, v_ref)
        xi, zeta, avg = xi_ref[...], zeta_ref[...], avg_ref[...]
        a, y = _ret_chunk(q, kf, v, s, (dec_ref, xi, head_ref))
        yc = y - _seg_mean(y, avg)
        rstd = lax.rsqrt(_seg_mean(yc * yc, avg) + LN_EPS)
        yh = yc * rstd
        gv = g_ref[...]
        sg = _sig(gv)
        sil = gv * sg
        gn = gng_ref[...]
        dout = do_ref[...]
        d_ref[:, 3 * RW:4 * RW] = dout * (yh * gn) * (sg * (1.0 + gv * (1.0 - sg)))
        dyh = dout * sil * gn
        part = jnp.sum(dout * sil * yh, axis=0, keepdims=True)

        @pl.when(i == 0)
        def _():
            dgn_ref[...] = part

        @pl.when(i > 0)
        def _():
            dgn_ref[...] += part

        dy = rstd * (dyh - _seg_mean(dyh, avg) - yh * _seg_mean(dyh * yh, avg))
        dyb = _bf(dy)
        vb, kb, qb = _bf(v), _bf(kf), _bf(q)
        dq = _dot(dyb, sb, nt=True) * xi
        dkf = _dot(vb, dsn, nt=True) * zeta
        dv = _dot(_bf(kf * zeta), dsn)
        for h in range(RH):
            m = head_ref[h]
            da = _dot(_bf(dy * m), vb, nt=True) * dec_ref[h]
            dv = dv + m * _dot(_bf(a[h].T), dyb)
            dq = dq + m * _dot(_bf(da), kb)
            dkf = dkf + m * _dot(_bf(da.T), qb)
        d_ref[:, 0:RW] = dq
        d_ref[:, RW:2 * RW] = jnp.where(valid, dkf * (HD ** -0.5), 0.0)
        d_ref[:, 2 * RW:3 * RW] = dv
        ds_scr[...] = cd_ref[...] * ds_scr[...] + same_ref[...] * _dot(_bf((q * xi).T), dyb)

    back = lambda i: NB - 1 - i
    col = _ret_specs(_RET_COL0, back)
    consts, cspecs = _ret_consts(tb)
    vec = pl.BlockSpec((1, RW), lambda i: (0, 0))
    return pl.pallas_call(
        body, name=name,
        out_shape=(jax.ShapeDtypeStruct((L, 4 * RW), F32), jax.ShapeDtypeStruct((1, RW), F32)),
        grid=(NB,),
        in_specs=[col(0), col(1), col(2), col(3), _ret_specs(3, back)(0),
                  pl.BlockSpec((None, RW, RW), lambda i: (back(i), 0, 0))] + cspecs + [vec],
        out_specs=(pl.BlockSpec((T, 4 * RW), lambda i: (back(i), 0)), vec),
        scratch_shapes=[pltpu.VMEM((RW, RW), F32)])(proj, proj, proj, proj, dycat, states, *consts, gng)


HALO = 32
TAP0 = HALO - (CONV_K - 1)


def _conv_specs():
    cur = lambda col=0: pl.BlockSpec((T, CV), lambda c: (c, col))
    before = lambda col=0: pl.BlockSpec((HALO, CV), lambda c: (jnp.maximum(c * (T // HALO) - 1, 0), col))
    after = pl.BlockSpec((HALO, CV), lambda c: (jnp.minimum((c + 1) * (T // HALO), L // HALO - 1), 0))
    full = lambda r, w: pl.BlockSpec((r, w), lambda c: (0, 0))
    return cur, before, after, full


_CONV_A, _CONV_B = 3, 4
_DY_CONV = 2


def _conv_post(xc, lg_ref, lb_ref):
    xh, rstd = _gn_rows(xc)
    z = xh * lg_ref[...] + lb_ref[...]
    return xh, rstd, z, _sig(z)


def conv_fwd(proj, w, b, lg, lb, pw, og, name):
    def body(ca_ref, cah_ref, cb_ref, cbh_ref, w_ref, b_ref, lg_ref, lb_ref, pw_ref, og_ref, y_ref, xc_ref, u_scr):
        c = pl.program_id(0)
        u_scr[0:HALO, :] = jnp.where(c > 0, cah_ref[...] * _sig(cbh_ref[...]), 0.0)
        u_scr[HALO:HALO + T, :] = ca_ref[...] * _sig(cb_ref[...])
        acc = jnp.zeros((T, CV), F32)
        for k in range(CONV_K):
            acc = acc + w_ref[k:k + 1, :] * u_scr[TAP0 + k:TAP0 + k + T, :]
        xc = acc + b_ref[...]
        xc_ref[...] = xc
        _, _, z, sg = _conv_post(xc, lg_ref, lb_ref)
        zp = _dot(_bf(z * sg), pw_ref[...])
        y_ref[...] = _rms_rows(zp, og_ref[...])[0]

    cur, before, _, full = _conv_specs()
    vec = full(1, CV)
    seq = jax.ShapeDtypeStruct((L, CV), F32)
    return pl.pallas_call(
        body, name=name, out_shape=(seq, seq), grid=(NB,),
        in_specs=[cur(_CONV_A), before(_CONV_A), cur(_CONV_B), before(_CONV_B), full(CONV_K, CV), vec, vec, vec,
                  full(CV, CV), vec],
        out_specs=(cur(), cur()),
        scratch_shapes=[pltpu.VMEM((HALO + T, CV), F32)])(proj, proj, proj, proj, w, b, lg, lb, pw, og)


def conv_bwd(proj, xc, dycat, w, lg, lb, pw, og, name):
    def body1(xc_ref, dy_ref, lg_ref, lb_ref, pw_ref, og_ref, dxc_ref, db_ref, dlg_ref, dlb_ref, dog_ref, dpw_ref):
        c = pl.program_id(0)

        @pl.when(c == 0)
        def _():
            for r in (db_ref, dlg_ref, dlb_ref, dog_ref, dpw_ref):
                r[...] = jnp.zeros_like(r)

        xh, rstd, z, sg = _conv_post(xc_ref[...], lg_ref, lb_ref)
        s = z * sg
        zp = _dot(_bf(s), pw_ref[...])
        ogv = og_ref[...]
        _, zph, r2 = _rms_rows(zp, ogv)
        dyv = dy_ref[...]
        dog_ref[...] += jnp.sum(dyv * zph, axis=0, keepdims=True)
        dzpb = _bf(_rms_rows_bwd(dyv, ogv, zph, r2))
        dpw_ref[...] += _dot(_bf(s.T), dzpb)
        dz = _dot(dzpb, pw_ref[...], nt=True) * (sg * (1.0 + z * (1.0 - sg)))
        dlg_ref[...] += jnp.sum(dz * xh, axis=0, keepdims=True)
        dlb_ref[...] += jnp.sum(dz, axis=0, keepdims=True)
        dxh = dz * lg_ref[...]
        dxc = rstd * (dxh - jnp.mean(dxh, axis=-1, keepdims=True) - xh * jnp.mean(dxh * xh, axis=-1, keepdims=True))
        db_ref[...] += jnp.sum(dxc, axis=0, keepdims=True)
        dxc_ref[...] = dxc

    def body2(dx_ref, dxa_ref, ca_ref, cb_ref, w_ref, dca_ref, dcb_ref, dw_ref, d_scr):
        c = pl.program_id(0)

        @pl.when(c == 0)
        def _():
            dw_ref[...] = jnp.zeros_like(dw_ref)

        d_scr[0:T, :] = dx_ref[...]
        d_scr[T:T + HALO, :] = jnp.where(c < NB - 1, dxa_ref[...], 0.0)
        ca = ca_ref[...]
        sg = _sig(cb_ref[...])
        u = ca * sg
        du = jnp.zeros((T, CV), F32)
        for k in range(CONV_K):
            off = CONV_K - 1 - k
            dj = d_scr[off:off + T, :]
            du = du + w_ref[k:k + 1, :] * dj
            dw_ref[k:k + 1, :] += jnp.sum(u * dj, axis=0, keepdims=True)
        dca_ref[...] = du * sg
        dcb_ref[...] = du * ca * sg * (1.0 - sg)

    cur, _, after, full = _conv_specs()
    seq = jax.ShapeDtypeStruct((L, CV), F32)
    vsh = jax.ShapeDtypeStruct((1, CV), F32)
    vec = full(1, CV)
    dxc, db, dlg, dlb, dog, dpw = pl.pallas_call(
        body1, name=name + "_a",
        out_shape=(seq, vsh, vsh, vsh, vsh, jax.ShapeDtypeStruct((CV, CV), F32)),
        grid=(NB,),
        in_specs=[cur(), cur(_DY_CONV), vec, vec, full(CV, CV), vec],
        out_specs=(cur(), vec, vec, vec, vec, full(CV, CV)))(xc, dycat, lg, lb, pw, og)
    dca, dcb, dw = pl.pallas_call(
        body2, name=name + "_b", out_shape=(seq, seq, jax.ShapeDtypeStruct((CONV_K, CV), F32)), grid=(NB,),
        in_specs=[cur(), after, cur(_CONV_A), cur(_CONV_B), full(CONV_K, CV)],
        out_specs=(cur(), cur(), full(CONV_K, CV)),
        scratch_shapes=[pltpu.VMEM((T + HALO, CV), F32)])(dxc, dxc, proj, proj, w)
    return dca, dcb, dw, db, dlg, dlb, dog, dpw


HR = 8


def _ffn_specs():
    cur = lambda off: pl.BlockSpec((None, T, UPW), lambda j, c: (j + off, c, 0))
    before = lambda off: pl.BlockSpec((None, HR, UPW), lambda j, c: (j + off, jnp.maximum(c * (T // HR) - 1, 0), 0))
    after = lambda off: pl.BlockSpec(
        (None, HR, UPW), lambda j, c: (j + off, jnp.minimum((c + 1) * (T // HR), L // HR - 1), 0))
    wspec = lambda off, r: pl.BlockSpec((None, r, UPW), lambda j, c: (j + off, 0, 0))
    return cur, before, after, wspec


def ffn_act_fwd(f, w, b, name):
    strip = 16

    def body(fg_ref, fgh_ref, fu_ref, fuh_ref, wg_ref, wu_ref, bg_ref, bu_ref, a_ref, gc_ref, uc_ref, hg_scr, hu_scr):
        c = pl.program_id(1)
        for scr, x_ref, xh_ref in ((hg_scr, fg_ref, fgh_ref), (hu_scr, fu_ref, fuh_ref)):
            scr[0:HR, :] = jnp.where(c > 0, xh_ref[...], 0.0)
            scr[HR:HR + strip, :] = x_ref[0:strip, :]
        wg = [wg_ref[k:k + 1, :] for k in range(FFN_K)]
        wu = [wu_ref[k:k + 1, :] for k in range(FFN_K)]
        bg, bu = bg_ref[...], bu_ref[...]

        def conv(src, base, w, b):
            acc = b
            for k in range(FFN_K):
                o = base - (FFN_K - 1) + k
                acc = acc + w[k] * src[o:o + strip, :]
            return acc

        for s in range(T // strip):
            r = s * strip
            gate = conv(hg_scr, HR, wg, bg) if s == 0 else conv(fg_ref, r, wg, bg)
            up = conv(hu_scr, HR, wu, bu) if s == 0 else conv(fu_ref, r, wu, bu)
            gc_ref[r:r + strip, :] = gate
            uc_ref[r:r + strip, :] = up
            a_ref[r:r + strip, :] = (gate * _sig(gate) * up).astype(BF16)

    cur, before, _, wspec = _ffn_specs()
    pre = jax.ShapeDtypeStruct((NPAIR, L, UPW), F32)
    return pl.pallas_call(
        body, name=name, out_shape=(jax.ShapeDtypeStruct((NPAIR, L, UPW), BF16), pre, pre), grid=(NPAIR, NB),
        in_specs=[cur(0), before(0), cur(NPAIR), before(NPAIR), wspec(0, FFN_K), wspec(NPAIR, FFN_K),
                  wspec(0, 1), wspec(NPAIR, 1)],
        out_specs=(cur(0), cur(0), cur(0)),
        scratch_shapes=[pltpu.VMEM((HR + strip, UPW), F32), pltpu.VMEM((HR + strip, UPW), F32)])(
            f, f, f, f, w, w, b, b)


def ffn_act_bwd(f, gc, uc, da, w, name):
    ext = T + HR
    sub = 8

    def body(fg_ref, fu_ref, gc_ref, gca_ref, uc_ref, uca_ref, da_ref, daa_ref, wg_ref, wu_ref,
             dfg_ref, dfu_ref, dwg_ref, dwu_ref, dbg_ref, dbu_ref, dg_scr, du_scr):
        c = pl.program_id(1)
        wg = [wg_ref[k:k + 1, :] for k in range(FFN_K)]
        wu = [wu_ref[k:k + 1, :] for k in range(FFN_K)]

        for s in range(ext // sub):
            r = s * sub
            if r < T:
                gate, up, dav = gc_ref[r:r + sub, :], uc_ref[r:r + sub, :], da_ref[r:r + sub, :]
            else:
                gate, up, dav = gca_ref[...], uca_ref[...], jnp.where(c < NB - 1, daa_ref[...], 0.0)
            sg = _sig(gate)
            dg_scr[r:r + sub, :] = dav * up * (sg * (1.0 + gate * (1.0 - sg)))
            du_scr[r:r + sub, :] = dav * gate * sg

        def back(d_scr, x_ref, w, df_ref, dw_ref, db_ref):
            accs = [jnp.zeros((sub, UPW), F32) for _ in range(FFN_K + 1)]
            for s in range(T // (2 * sub)):
                pieces = []
                for r in (2 * s * sub, (2 * s + 1) * sub):
                    xv = x_ref[r:r + sub, :]
                    df = jnp.zeros((sub, UPW), F32)
                    for k in range(FFN_K):
                        off = FFN_K - 1 - k
                        dj = d_scr[r + off:r + off + sub, :]
                        df = df + w[k] * dj
                        accs[k] = accs[k] + xv * dj
                        if off == 0:
                            accs[FFN_K] = accs[FFN_K] + dj
                    pieces.append(df)
                df_ref[2 * s * sub:2 * (s + 1) * sub, :] = jnp.concatenate(pieces, axis=0).astype(BF16)
            dwp = _stack_rows([jnp.sum(a, axis=0, keepdims=True) for a in accs[:FFN_K]])
            dbp = jnp.sum(accs[FFN_K], axis=0, keepdims=True)

            @pl.when(c == 0)
            def _():
                dw_ref[...] = dwp
                db_ref[...] = dbp

            @pl.when(c > 0)
            def _():
                dw_ref[...] += dwp
                db_ref[...] += dbp

        back(dg_scr, fg_ref, wg, dfg_ref, dwg_ref, dbg_ref)
        back(du_scr, fu_ref, wu, dfu_ref, dwu_ref, dbu_ref)

    cur, _, after, wspec = _ffn_specs()
    slab = jax.ShapeDtypeStruct((NPAIR, L, UPW), BF16)
    wsh = jax.ShapeDtypeStruct((NPAIR, FFN_K, UPW), F32)
    bsh = jax.ShapeDtypeStruct((NPAIR, 1, UPW), F32)
    return pl.pallas_call(
        body, name=name, out_shape=(slab, slab, wsh, wsh, bsh, bsh), grid=(NPAIR, NB),
        in_specs=[cur(0), cur(NPAIR), cur(0), after(0), cur(0), after(0), cur(0), after(0),
                  wspec(0, FFN_K), wspec(NPAIR, FFN_K)],
        out_specs=(cur(0), cur(0), wspec(0, FFN_K), wspec(0, FFN_K), wspec(0, 1), wspec(0, 1)),
        scratch_shapes=[pltpu.VMEM((ext, UPW), F32), pltpu.VMEM((ext, UPW), F32)])(
            f, f, gc, gc, uc, uc, da, da, w, w)


def _mesh_pos():
    return lax.axis_index("x"), lax.axis_index("y"), lax.axis_index("c")


def _peer(pos, k):
    x, y, c = pos
    px = 1 - x if k & 4 else x
    py = 1 - y if k & 2 else y
    pc = 1 - c if k & 1 else c
    return (px, py, pc), 4 * px + 2 * py + pc


_CHIP_FLIPS = (4, 2, 6)
_HBM = pl.BlockSpec(memory_space=pl.ANY)


def all_gather(shards, name):
    nt = len(shards)

    def body(*refs):
        ins, outs = refs[:nt], refs[nt:2 * nt]
        send, recv, local = refs[2 * nt:]
        pos = _mesh_pos()
        me = 4 * pos[0] + 2 * pos[1] + pos[2]
        sib, sib_id = _peer(pos, 1)

        def copy(t, k, block_id, to, src=None):
            dst = outs[t].at[block_id]
            return pltpu.make_async_remote_copy(
                src_ref=dst if src is None else src, dst_ref=dst, send_sem=send.at[t, k], recv_sem=recv.at[t, k],
                device_id=to, device_id_type=MESH)

        locals_ = [pltpu.make_async_copy(ins[t], outs[t].at[me], local.at[t]) for t in range(nt)]
        for cp in locals_:
            cp.start()
        started = []
        for j, flip in enumerate(_CHIP_FLIPS):
            for t in range(nt):
                started.append(copy(t, 1 + j, me, _peer(pos, flip)[0], src=ins[t]))
        for t in range(nt):
            started.append(copy(t, 0, me, sib, src=ins[t]))
        for cp in started:
            cp.start()
        for j, flip in enumerate(_CHIP_FLIPS):
            _, pid = _peer(pos, flip)
            for t in range(nt):
                copy(t, 1 + j, pid, sib).wait_recv()
                fwd = copy(t, 4 + j, pid, sib)
                fwd.start()
                started.append(fwd)
        for t in range(nt):
            copy(t, 0, sib_id, sib).wait_recv()
        for j, flip in enumerate(_CHIP_FLIPS):
            _, pid = _peer(pos, flip | 1)
            for t in range(nt):
                copy(t, 4 + j, pid, sib).wait_recv()
        for cp in started:
            cp.wait_send()
        for cp in locals_:
            cp.wait()

    return pl.pallas_call(
        body, name=name,
        out_shape=tuple(jax.ShapeDtypeStruct((NDEV,) + s.shape, s.dtype) for s in shards),
        in_specs=[_HBM] * nt, out_specs=tuple([_HBM] * nt),
        scratch_shapes=[pltpu.SemaphoreType.DMA((nt, NDEV - 1)), pltpu.SemaphoreType.DMA((nt, NDEV - 1)),
                        pltpu.SemaphoreType.DMA((nt,))])(*shards)


_SPLIT = dict(has_side_effects=pltpu.SideEffectType.DATAFLOW_SIDE_EFFECTING)
_SEM = pl.BlockSpec(memory_space=pltpu.SEMAPHORE)


def _exchange_start(name, srcs, lands, copies, after):
    ns, nl = len(srcs), len(lands)
    ncopy = len(copies(None, [None] * ns, [None] * nl))

    def body(*refs):
        src_refs, land_refs = refs[:ns], refs[ns:ns + nl]
        send, recv = refs[ns + nl + len(after):ns + nl + len(after) + 2]
        token = refs[-1]
        for i, (src, dst, peer) in enumerate(copies(_mesh_pos(), src_refs, land_refs)):
            pltpu.make_async_remote_copy(src_ref=src, dst_ref=dst, send_sem=send.at[i], recv_sem=recv.at[i],
                                         device_id=peer, device_id_type=MESH).start()
        token[...] = jnp.zeros_like(token)

    hbm = lambda a: pltpu.HBM(a.shape, a.dtype)
    out = pl.pallas_call(
        body, name=name,
        out_shape=(pltpu.SemaphoreType.DMA((ncopy,)), pltpu.SemaphoreType.DMA((ncopy,)),
                   *[hbm(a) for a in srcs], *[hbm(a) for a in lands], jax.ShapeDtypeStruct((8, 128), F32)),
        in_specs=[pl.BlockSpec(memory_space=pltpu.HBM)] * (ns + nl) + [_HBM] * len(after),
        out_specs=(_SEM, _SEM, *[pl.BlockSpec(memory_space=pltpu.HBM)] * (ns + nl),
                   pl.BlockSpec(memory_space=pltpu.VMEM)),
        input_output_aliases={i: 2 + i for i in range(ns + nl)},
        compiler_params=pltpu.CompilerParams(**_SPLIT))(
            *[pltpu.with_memory_space_constraint(a, pltpu.HBM) for a in list(srcs) + list(lands)], *after)
    return out[0], out[1], list(out[2:2 + ns]), list(out[2 + ns:2 + ns + nl]), out[-1]


def _exchange_wait(name, send, recv, srcs, lands, copies, after):
    ns, nl = len(srcs), len(lands)

    def body(*refs):
        src_refs, land_refs = refs[:ns], refs[ns:ns + nl]
        send_ref, recv_ref = refs[ns + nl:ns + nl + 2]
        pos = _mesh_pos()
        for i, (src, dst, peer) in enumerate(copies(pos, src_refs, land_refs, arriving=True)):
            cp = pltpu.make_async_remote_copy(src_ref=src, dst_ref=dst, send_sem=send_ref.at[i], recv_sem=recv_ref.at[i],
                                              device_id=peer, device_id_type=MESH)
            cp.wait_send()
            cp.wait_recv()

    hbm = lambda a: pltpu.HBM(a.shape, a.dtype)
    out = pl.pallas_call(
        body, name=name, out_shape=tuple(hbm(a) for a in list(srcs) + list(lands)),
        in_specs=[pl.BlockSpec(memory_space=pltpu.HBM)] * (ns + nl) + [_SEM, _SEM] + [_HBM] * len(after),
        out_specs=tuple([pl.BlockSpec(memory_space=pltpu.HBM)] * (ns + nl)),
        input_output_aliases={i: i for i in range(ns + nl)},
        compiler_params=pltpu.CompilerParams(**_SPLIT))(*srcs, *lands, send, recv, *after)
    return list(out[:ns]), list(out[ns:])


def _gather_copies(pos, srcs, lands, arriving=False):
    if pos is None:
        return [None] * (len(srcs) * (NDEV - 1))
    me = 4 * pos[0] + 2 * pos[1] + pos[2]
    out = []
    for src, land in zip(srcs, lands):
        for k in range(1, NDEV):
            peer, pid = _peer(pos, k)
            out.append((src, land.at[pid if arriving else me], peer))
    return out


def _scatter_copies(pos, srcs, lands, arriving=False):
    if pos is None:
        return [None] * (len(srcs) * (NDEV - 1))
    out = []
    for src, land in zip(srcs, lands):
        for k in range(1, NDEV):
            peer, pid = _peer(pos, k)
            out.append((src.at[pid], land.at[k - 1], peer))
    return out


def gather_start(name, shards, lands, after=()):
    return _exchange_start(name, shards, lands, _gather_copies, list(after))


def gather_wait(name, handle, after=()):
    send, recv, srcs, lands, _ = handle
    return _exchange_wait(name, send, recv, srcs, lands, _gather_copies, list(after))


def scatter_start(name, grads, after=()):
    lands = [pltpu.with_memory_space_constraint(lax.empty((NDEV - 1,) + g.shape[1:], g.dtype), pltpu.HBM)
             for g in grads]
    return _exchange_start(name, grads, lands, _scatter_copies, list(after))


def scatter_wait(name, handle, after=()):
    send, recv, srcs, lands, _ = handle
    return _exchange_wait(name, send, recv, srcs, lands, _scatter_copies, list(after))


def all_reduce_small(part, name):
    rows = part.shape[0]

    def body(x_ref, o_ref, buf, send, recv):
        pos = _mesh_pos()
        me = 4 * pos[0] + 2 * pos[1] + pos[2]
        sends = []
        for k in range(1, NDEV):
            peer, _ = _peer(pos, k)
            sends.append(pltpu.make_async_remote_copy(
                src_ref=x_ref, dst_ref=buf.at[me], send_sem=send.at[k - 1], recv_sem=recv.at[k - 1],
                device_id=peer, device_id_type=MESH))
        for cp in sends:
            cp.start()
        buf[me] = x_ref[...]
        for k in range(1, NDEV):
            peer, pid = _peer(pos, k)
            pltpu.make_async_remote_copy(
                src_ref=x_ref, dst_ref=buf.at[pid], send_sem=send.at[k - 1], recv_sem=recv.at[k - 1],
                device_id=peer, device_id_type=MESH).wait_recv()
        for cp in sends:
            cp.wait_send()
        acc = buf[0]
        for d in range(1, NDEV):
            acc = acc + buf[d]
        o_ref[...] = acc

    return pl.pallas_call(
        body, name=name, out_shape=jax.ShapeDtypeStruct((rows, 128), F32),
        in_specs=[pl.BlockSpec(memory_space=pltpu.VMEM)], out_specs=pl.BlockSpec(memory_space=pltpu.VMEM),
        scratch_shapes=[pltpu.VMEM((NDEV, rows, 128), F32), pltpu.SemaphoreType.DMA((NDEV - 1,)),
                        pltpu.SemaphoreType.DMA((NDEV - 1,))])(part)


def _adamw_update(g, w_ref, m_ref, v_ref, g_ref, d_ref, nm_ref, nv_ref):
    g_ref[...] = g
    nm = ADAM_B1 * m_ref[...] + (1.0 - ADAM_B1) * g
    nv = ADAM_B2 * v_ref[...] + (1.0 - ADAM_B2) * (g * g)
    nm_ref[...] = nm
    nv_ref[...] = nv
    m_hat = nm / (1.0 - ADAM_B1 ** ADAM_STEP)
    v_hat = nv / (1.0 - ADAM_B2 ** ADAM_STEP)
    d_ref[...] = -ADAM_LR * (m_hat / (jnp.sqrt(v_hat) + ADAM_EPS) + ADAM_WD * w_ref[...])


def adamw_big(own, lands, w, m, v, me, tr, name):
    _, r, c = own[0].shape
    nt = r // tr

    def body(me_ref, *refs):
        ins, (w_ref, m_ref, v_ref), outs = refs[:2 * DEPTH], refs[2 * DEPTH:2 * DEPTH + 3], refs[2 * DEPTH + 3:]
        layer = pl.program_id(0)
        for l in range(DEPTH):
            @pl.when(layer == l)
            def _(l=l):
                g = ins[2 * l][...].astype(F32)
                for s in range(NDEV - 1):
                    g = g + ins[2 * l + 1][s].astype(F32)
                _adamw_update(g, w_ref, m_ref, v_ref, *outs)

    pick = lambda l: (lambda layer, i, me_ref: jnp.where(layer == l, i, 0))
    in_specs = []
    for l in range(DEPTH):
        in_specs.append(pl.BlockSpec((None, tr, c), lambda layer, i, me_ref, f=pick(l): (me_ref[0], f(layer, i, me_ref), 0)))
        in_specs.append(pl.BlockSpec((NDEV - 1, tr, c), lambda layer, i, me_ref, f=pick(l): (0, f(layer, i, me_ref), 0)))
    blk = pl.BlockSpec((tr, c), lambda layer, i, me_ref: (layer * nt + i, 0))
    sh = jax.ShapeDtypeStruct((DEPTH * r, c), F32)
    args = [a for l in range(DEPTH) for a in (own[l], lands[l])]
    return pl.pallas_call(
        body, name=name, out_shape=(sh, sh, sh, sh),
        grid_spec=pltpu.PrefetchScalarGridSpec(
            num_scalar_prefetch=1, grid=(DEPTH, nt), in_specs=in_specs + [blk, blk, blk],
            out_specs=(blk, blk, blk, blk)),
        compiler_params=_params(VMEM_BIG))(me, *args, w, m, v)


def adamw(parts, w, m, v, tr, name):
    ns, r, c = parts.shape

    def body(p_ref, w_ref, m_ref, v_ref, g_ref, d_ref, nm_ref, nv_ref):
        g = p_ref[0].astype(F32)
        for s in range(1, ns):
            g = g + p_ref[s].astype(F32)
        _adamw_update(g, w_ref, m_ref, v_ref, g_ref, d_ref, nm_ref, nv_ref)

    blk = pl.BlockSpec((tr, c), lambda i: (i, 0))
    sh = jax.ShapeDtypeStruct((r, c), F32)
    return pl.pallas_call(
        body, name=name, out_shape=(sh, sh, sh, sh), grid=(r // tr,),
        in_specs=[pl.BlockSpec((ns, tr, c), lambda i: (0, i, 0)), blk, blk, blk],
        out_specs=(blk, blk, blk, blk), compiler_params=_params(VMEM_BIG))(parts, w, m, v)


def local_step(x, target, P, hooks):
    tb = _ret_tables()
    h = jnp.concatenate([jnp.zeros((PAD, D), F32), P["meta"], x], axis=0)
    stash = []
    for l in range(DEPTH):
        s = {"h": h}
        s["w_in"], s["cv_pw"], s["w_out"] = hooks["mixer_w"](l, h)
        u = s["u"] = rms_fwd(h, P["norm_mix_g"][l], BF16, f"rms_mix_{l}", after=hooks["first_after"] if l == 0 else None)
        proj = s["proj"] = mm_nt(f"in_proj_{l}", u, s["w_in"], 768)
        s["o"] = attn_fwd(proj, P["q_norm_g"][l], P["k_norm_g"][l], P["attn_sinks"][l], f"attn_fwd_{l}")
        y_att = rms_fwd(s["o"], P["attn_out_g"][l], BF16, f"rms_att_{l}")
        y_cv, s["xc"] = conv_fwd(proj, P["cv_dw_w"][l], P["cv_dw_b"][l], P["cv_ln_g"][l], P["cv_ln_b"][l],
                                 s["cv_pw"], P["cv_out_g"][l], f"conv_fwd_{l}")
        y_ret, s["states"] = ret_fwd(proj, P["ret_gn_g"][l], tb, f"ret_fwd_{l}")
        ycat = s["ycat"] = jnp.concatenate([y_att, y_cv.astype(BF16), y_ret.astype(BF16)], axis=1)
        h = mm_nn(f"out_proj_{l}", ycat, s["w_out"], 512, res=h)
        s["h1"] = h
        s["ffn_up"], s["ffn_down"] = hooks["ffn_w"](l, h)
        u2 = s["u2"] = rms_fwd(h, P["norm_ffn_g"][l], BF16, f"rms_ffn_{l}")
        s["f"] = ffn_up_fwd(f"ffn_up_{l}", u2, s["ffn_up"])
        a, s["gc"], s["uc"] = ffn_act_fwd(s["f"], P["ffn_dw_w"][l], P["ffn_dw_b"][l], f"ffn_act_fwd_{l}")
        s["a"] = a
        h = ffn_down_fwd(f"ffn_down_{l}", a, s["ffn_down"], h)
        stash.append(s)

    dh, dhm, loss = loss_head(h, target)
    G = {n: [None] * DEPTH for n in ("norm_mix_g", "q_norm_g", "k_norm_g", "attn_sinks", "attn_out_g", "cv_dw_w",
                                     "cv_dw_b", "cv_ln_g", "cv_ln_b", "cv_out_g", "ret_gn_g", "norm_ffn_g",
                                     "ffn_dw_w", "ffn_dw_b")}
    for l in reversed(range(DEPTH)):
        s = stash[l]
        da = ffn_down_dx(f"ffn_down_dx_{l}", dhm, s["ffn_down"])
        dw_down = ffn_down_dw(f"ffn_down_dw_{l}", s["a"], dhm)
        dfg, dfu, dwg, dwu, dbg, dbu = ffn_act_bwd(s["f"], s["gc"], s["uc"], da, P["ffn_dw_w"][l], f"ffn_act_bwd_{l}")
        G["ffn_dw_w"][l] = jnp.concatenate([dwg, dwu], axis=0)
        G["ffn_dw_b"][l] = jnp.concatenate([dbg, dbu], axis=0)
        du2 = ffn_up_dx(f"ffn_up_dx_{l}", dfg, dfu, s["ffn_up"])
        dw_up = ffn_up_dw(f"ffn_up_dw_{l}", s["u2"], dfg, dfu)
        after = hooks["ffn_grads"](l, dw_down, dw_up)
        dh, dhm, G["norm_ffn_g"][l] = rms_bwd(s["h1"], P["norm_ffn_g"][l], du2, dh, f"rms_ffn_bwd_{l}", after=after)
        dycat = mm_nt(f"out_proj_dx_{l}", dhm, s["w_out"], 512)
        dw_out = mm_ta(f"out_proj_dw_{l}", s["ycat"], dhm, 512)
        do, _, G["attn_out_g"][l] = rms_bwd(s["o"], P["attn_out_g"][l], dycat, None, f"rms_att_bwd_{l}")
        (dq, dk_c, dk_p, dk_m, dv_c, dv_p, dv_m, G["q_norm_g"][l], G["k_norm_g"][l], dsk) = attn_bwd(
            s["proj"], do, P["q_norm_g"][l], P["k_norm_g"][l], P["attn_sinks"][l], f"attn_bwd_{l}")
        G["attn_sinks"][l] = dsk[:, 0]
        shift = lambda z: jnp.concatenate([z[T:], jnp.zeros((T, NKV * HD), F32)], axis=0)
        dk = (dk_c + shift(dk_p)).at[:T].add(dk_m)
        dv = (dv_c + shift(dv_p)).at[:T].add(dv_m)
        (dca, dcb, G["cv_dw_w"][l], G["cv_dw_b"][l], G["cv_ln_g"][l], G["cv_ln_b"][l], G["cv_out_g"][l],
         dpw) = conv_bwd(s["proj"], s["xc"], dycat, P["cv_dw_w"][l], P["cv_ln_g"][l], P["cv_ln_b"][l], s["cv_pw"],
                         P["cv_out_g"][l], f"conv_bwd_{l}")
        dret, G["ret_gn_g"][l] = ret_bwd(s["proj"], dycat, s["states"], P["ret_gn_g"][l], tb, f"ret_bwd_{l}")
        dproj = jnp.concatenate([dq, dk, dv, dca, dcb, dret], axis=1).astype(BF16)
        du = mm_nn(f"in_proj_dx_{l}", dproj, s["w_in"], 512)
        dw_in = mm_ta(f"in_proj_dw_{l}", dproj, s["u"], 768)
        after = hooks["mixer_grads"](l, dw_out, dpw, dw_in)
        dh, dhm, G["norm_mix_g"][l] = rms_bwd(s["h"], P["norm_mix_g"][l], du, dh, f"rms_mix_bwd_{l}", after=after,
                                              last=(l == 0))
    return loss[0, 0], dh, dhm[PAD:T], G


_SMALL = ("meta", "norm_mix_g", "q_norm_g", "k_norm_g", "attn_sinks", "attn_out_g", "cv_dw_w", "cv_dw_b", "cv_ln_g",
          "cv_ln_b", "cv_out_g", "ret_gn_g", "norm_ffn_g", "ffn_dw_w", "ffn_dw_b")
_BIG = ("w_in", "cv_pw", "w_out", "ffn_up", "ffn_down")
_TRANSPOSED = ("w_in", "ffn_up")
_ORDER = ("meta", "norm_mix_g", "w_in", "q_norm_g", "k_norm_g", "attn_sinks", "attn_out_g", "cv_dw_w", "cv_dw_b",
          "cv_ln_g", "cv_ln_b", "cv_pw", "cv_out_g", "ret_gn_g", "w_out", "norm_ffn_g", "ffn_up", "ffn_dw_w",
          "ffn_dw_b", "ffn_down")
_SMALL_SHARDED = {"meta": D, "cv_dw_w": CV, "ffn_dw_w": 2 * D_FF}


def _pack(arrs):
    flat = jnp.concatenate([a.reshape(-1) for a in arrs])
    n = flat.shape[0]
    rows = -(-n // 1024) * 8
    return jnp.pad(flat, (0, rows * 128 - n)).reshape(rows, 128)


def _unpack(packed, shapes):
    flat = packed.reshape(-1)
    out, off = [], 0
    for s in shapes:
        n = int(np.prod(s))
        out.append(flat[off:off + n].reshape(s))
        off += n
    return out


def kernel(x, meta, norm_mix_g, w_in, q_norm_g, k_norm_g, attn_sinks, attn_out_g, cv_dw_w, cv_dw_b, cv_ln_g, cv_ln_b, cv_pw, cv_out_g, ret_gn_g, w_out, norm_ffn_g, ffn_up, ffn_dw_w, ffn_dw_b, ffn_down, loss_target, m_meta, m_norm_mix_g, m_w_in, m_q_norm_g, m_k_norm_g, m_attn_sinks, m_attn_out_g, m_cv_dw_w, m_cv_dw_b, m_cv_ln_g, m_cv_ln_b, m_cv_pw, m_cv_out_g, m_ret_gn_g, m_w_out, m_norm_ffn_g, m_ffn_up, m_ffn_dw_w, m_ffn_dw_b, m_ffn_down, v_meta, v_norm_mix_g, v_w_in, v_q_norm_g, v_k_norm_g, v_attn_sinks, v_attn_out_g, v_cv_dw_w, v_cv_dw_b, v_cv_ln_g, v_cv_ln_b, v_cv_pw, v_cv_out_g, v_ret_gn_g, v_w_out, v_norm_ffn_g, v_ffn_up, v_ffn_dw_w, v_ffn_dw_b, v_ffn_down):
    W = dict(meta=meta, norm_mix_g=norm_mix_g, w_in=w_in, q_norm_g=q_norm_g, k_norm_g=k_norm_g,
             attn_sinks=attn_sinks, attn_out_g=attn_out_g, cv_dw_w=cv_dw_w, cv_dw_b=cv_dw_b, cv_ln_g=cv_ln_g,
             cv_ln_b=cv_ln_b, cv_pw=cv_pw, cv_out_g=cv_out_g, ret_gn_g=ret_gn_g, w_out=w_out,
             norm_ffn_g=norm_ffn_g, ffn_up=ffn_up, ffn_dw_w=ffn_dw_w, ffn_dw_b=ffn_dw_b, ffn_down=ffn_down)
    M = dict(meta=m_meta, norm_mix_g=m_norm_mix_g, w_in=m_w_in, q_norm_g=m_q_norm_g, k_norm_g=m_k_norm_g,
             attn_sinks=m_attn_sinks, attn_out_g=m_attn_out_g, cv_dw_w=m_cv_dw_w, cv_dw_b=m_cv_dw_b,
             cv_ln_g=m_cv_ln_g, cv_ln_b=m_cv_ln_b, cv_pw=m_cv_pw, cv_out_g=m_cv_out_g, ret_gn_g=m_ret_gn_g,
             w_out=m_w_out, norm_ffn_g=m_norm_ffn_g, ffn_up=m_ffn_up, ffn_dw_w=m_ffn_dw_w, ffn_dw_b=m_ffn_dw_b,
             ffn_down=m_ffn_down)
    V = dict(meta=v_meta, norm_mix_g=v_norm_mix_g, w_in=v_w_in, q_norm_g=v_q_norm_g, k_norm_g=v_k_norm_g,
             attn_sinks=v_attn_sinks, attn_out_g=v_attn_out_g, cv_dw_w=v_cv_dw_w, cv_dw_b=v_cv_dw_b,
             cv_ln_g=v_cv_ln_g, cv_ln_b=v_cv_ln_b, cv_pw=v_cv_pw, cv_out_g=v_cv_out_g, ret_gn_g=v_ret_gn_g,
             w_out=v_w_out, norm_ffn_g=v_norm_ffn_g, ffn_up=v_ffn_up, ffn_dw_w=v_ffn_dw_w, ffn_dw_b=v_ffn_dw_b,
             ffn_down=v_ffn_down)
    me = 4 * lax.axis_index("x") + 2 * lax.axis_index("y") + lax.axis_index("c")
    for n in _TRANSPOSED:
        W[n], M[n], V[n] = (a.transpose(0, 2, 1) for a in (W[n], M[n], V[n]))

    sh = {n: [W[n][l].astype(BF16) for l in range(DEPTH)] for n in _BIG}
    mix = lambda l: [sh["w_in"][l], sh["cv_pw"][l], sh["w_out"][l]]
    ffn = lambda l: [sh["ffn_up"][l], sh["ffn_down"][l]]
    first = all_gather(mix(0) + [meta, cv_dw_w, ffn_dw_w], "gather_first")
    g_meta, g_cdw, g_fdw = first[3:6]

    def landing(shards):
        return [lax.dynamic_update_slice(lax.empty((NDEV,) + s.shape, s.dtype), s[None], (me,) + (0,) * s.ndim)
                for s in shards]

    gathers = {("ffn", 0): gather_start("gather_ffn0_start", ffn(0), landing(ffn(0)), after=[first[0]])}
    gathers["mix", 1] = gather_start("gather_mix1_start", mix(1), landing(mix(1)), after=[gathers["ffn", 0][4]])
    gathers["ffn", 1] = gather_start("gather_ffn1_start", ffn(1), landing(ffn(1)), after=[gathers["mix", 1][4]])

    def mixer_w(l, h):
        g_in, g_pw, g_out = first[0:3] if l == 0 else gather_wait(f"gather_mix{l}_wait", gathers["mix", l], after=[h])[1]
        return g_in.reshape(IN_W, D), g_pw.reshape(CV, CV), g_out.reshape(D, D)

    def ffn_w(l, h1):
        return gather_wait(f"gather_ffn{l}_wait", gathers["ffn", l], after=[h1])[1]

    scatters = {}

    def ffn_grads(l, dw_down, dw_up):
        scatters["ffn", l] = scatter_start(f"scatter_ffn{l}_start", [dw_up, dw_down.reshape(NDEV, DNR, D)])
        return scatters["ffn", l][4]

    def mixer_grads(l, dw_out, dpw, dw_in):
        grads = [dw_in.reshape(NDEV, IN_W // NDEV, D), dpw.astype(BF16).reshape(NDEV, CV // NDEV, CV),
                 dw_out.reshape(NDEV, D // NDEV, D)]
        scatters["mix", l] = scatter_start(f"scatter_mix{l}_start", grads)
        return scatters["mix", l][4]

    P = dict(
        meta=g_meta.transpose(1, 0, 2).reshape(N_META, D),
        cv_dw_w=g_cdw.transpose(1, 2, 0, 3).reshape(DEPTH, CONV_K, CV),
        ffn_dw_w=g_fdw.transpose(1, 0, 2, 3),
        ffn_dw_b=ffn_dw_b.reshape(DEPTH, NDEV, 1, UPW),
        attn_sinks=attn_sinks,
    )
    for n in ("norm_mix_g", "q_norm_g", "k_norm_g", "attn_out_g", "cv_dw_b", "cv_ln_g", "cv_ln_b", "cv_out_g",
              "ret_gn_g", "norm_ffn_g"):
        P[n] = W[n].reshape(DEPTH, 1, -1)

    hooks = dict(mixer_w=mixer_w, ffn_w=ffn_w, ffn_grads=ffn_grads, mixer_grads=mixer_grads,
                 first_after=gathers["ffn", 1][4])
    loss_part, dx, dmeta, G = local_step(x[0], loss_target[0], P, hooks)

    small_full = {
        "meta": dmeta,
        "cv_dw_w": jnp.stack(G["cv_dw_w"]),
        "ffn_dw_w": jnp.stack([g.transpose(1, 0, 2).reshape(FFN_K, 2 * D_FF) for g in G["ffn_dw_w"]]),
        "ffn_dw_b": jnp.stack([g.reshape(2 * D_FF) for g in G["ffn_dw_b"]]),
        "attn_sinks": jnp.stack(G["attn_sinks"]),
    }
    for n in _SMALL:
        if n not in small_full:
            small_full[n] = jnp.stack([g.reshape(-1) for g in G[n]])
    shapes = [small_full[n].shape for n in _SMALL] + [(1,)]
    summed = _unpack(all_reduce_small(_pack([small_full[n] for n in _SMALL] + [loss_part.reshape(1)]), "reduce_small"),
                     shapes)
    loss = summed[-1][0]
    small_g = {}
    for n, g in zip(_SMALL, summed):
        if n in _SMALL_SHARDED:
            width = _SMALL_SHARDED[n] // NDEV
            g = lax.dynamic_slice_in_dim(g, me * width, width, axis=g.ndim - 1)
        small_g[n] = g

    out = {}
    tiles = {"w_in": 144, "cv_pw": 32, "w_out": 128, "ffn_up": 176, "ffn_down": 176}
    me1 = me.astype(jnp.int32).reshape(1)
    own, lands = {n: [None] * DEPTH for n in _BIG}, {n: [None] * DEPTH for n in _BIG}

    def arrived(kind, names, after):
        for l in range(DEPTH):
            srcs, got = scatter_wait(f"scatter_{kind}{l}_wait", scatters[kind, l], after=after)
            for n, s_, g_ in zip(names, srcs, got):
                own[n][l], lands[n][l] = s_, g_

    def update(names):
        for n in names:
            shard = W[n].shape
            rows, cols = shard[0] * shard[1], shard[2]
            res = adamw_big(own[n], lands[n], W[n].reshape(rows, cols), M[n].reshape(rows, cols),
                            V[n].reshape(rows, cols), me1, tiles[n], f"adamw_{n}")
            out[n] = [r.reshape(shard) for r in res]

    arrived("ffn", ("ffn_up", "ffn_down"), [dx])
    update(("ffn_up", "ffn_down"))
    sshapes = [W[n].shape for n in _SMALL]
    packed_g = _pack([small_g[n] for n in _SMALL])
    res = adamw(packed_g[None], _pack([W[n] for n in _SMALL]), _pack([M[n] for n in _SMALL]),
                _pack([V[n] for n in _SMALL]), packed_g.shape[0], "adamw_small")
    for i, r in enumerate(res):
        for n, a in zip(_SMALL, _unpack(r, sshapes)):
            out.setdefault(n, [None] * 4)[i] = a
    arrived("mix", ("w_in", "cv_pw", "w_out"), [out["ffn_down"][0], res[0]])
    update(("w_in", "cv_pw", "w_out"))
    for n in _TRANSPOSED:
        out[n] = [r.transpose(0, 2, 1) for r in out[n]]

    return (loss, dx[None], *[out[n][0] for n in _ORDER], *[out[n][1] for n in _ORDER],
            *[out[n][2] for n in _ORDER], *[out[n][3] for n in _ORDER])
```

```python
import functools
import math

import numpy as np
import jax
import jax.numpy as jnp
from jax import lax
from jax.experimental import pallas as pl
from jax.experimental.pallas import tpu as pltpu

F32 = jnp.float32
BF16 = jnp.bfloat16

D = 1024
SEQ = 2048
DEPTH = 2
T = 128
L = SEQ + T
NB = L // T
N_META = 16
PAD = T - N_META
HD = 64
NQ = 8
NKV = 2
GQA = NQ // NKV
CV = 256
CONV_K = 31
RH = 4
D_FF = 2816
FFN_K = 3
IN_W = 2304
RMS_EPS = 1e-6
LN_EPS = 1e-5
NEG = -1e30
NDEV = 8
UPW = 2 * D_FF // NDEV
DNR = D_FF // NDEV
NPAIR = NDEV // 2

ADAM_LR, ADAM_B1, ADAM_B2, ADAM_EPS, ADAM_WD, ADAM_STEP = 0.001, 0.9, 0.999, 1e-08, 0.01, 10

VMEM_BIG = 56 * 1024 * 1024

MESH = pl.DeviceIdType.MESH


def _params(vmem=None):
    return pltpu.CompilerParams(vmem_limit_bytes=vmem) if vmem else None


def _dot(a, b, nt=False):
    return lax.dot_general(a, b, (((1,), (1 if nt else 0,)), ((), ())), preferred_element_type=F32)


def _sig(x):
    return 1.0 / (1.0 + jnp.exp(-x))


def _bf(x):
    return x.astype(BF16)


def _stack_rows(rows):
    idx = lax.broadcasted_iota(jnp.int32, (len(rows), rows[0].shape[1]), 0)
    out = jnp.zeros((len(rows), rows[0].shape[1]), F32)
    for r, v in enumerate(rows):
        out = jnp.where(idx == r, v, out)
    return out


def rms_fwd(x, g, out_dtype, name, after=None):
    n, w = x.shape

    def body(x_ref, g_ref, *rest):
        o_ref = rest[-1]
        xv = x_ref[...]
        r = lax.rsqrt(jnp.mean(xv * xv, axis=-1, keepdims=True) + RMS_EPS)
        o_ref[...] = (xv * r * g_ref[...]).astype(o_ref.dtype)

    deps = [] if after is None else [after]
    return pl.pallas_call(
        body, name=name, out_shape=jax.ShapeDtypeStruct((n, w), out_dtype), grid=(n // T,),
        in_specs=[pl.BlockSpec((T, w), lambda i: (i, 0)), pl.BlockSpec((1, w), lambda i: (0, 0))] + [_HBM] * len(deps),
        out_specs=pl.BlockSpec((T, w), lambda i: (i, 0)))(x, g, *deps)


def rms_bwd(x, g, dy, dres, name, after=None, dy_col=0, last=False):
    n, w = x.shape
    has_res = dres is not None
    deps = [] if after is None else [after]

    def body(x_ref, g_ref, dy_ref, *rest):
        rest = rest[:len(rest) - 3 - len(deps)] + rest[len(rest) - 3:]
        if has_res:
            dres_ref, dx_ref, dxm_ref, dg_ref = rest
        else:
            dx_ref, dxm_ref, dg_ref = rest
        i = pl.program_id(0)
        xv = x_ref[...]
        r = lax.rsqrt(jnp.mean(xv * xv, axis=-1, keepdims=True) + RMS_EPS)
        xh = xv * r
        dyv = dy_ref[...]
        dxh = dyv * g_ref[...]
        dx = r * (dxh - xh * jnp.mean(dxh * xh, axis=-1, keepdims=True))
        if has_res:
            dx = dx + dres_ref[...]
        dx_ref[...] = dx
        if last:
            @pl.when(i == 0)
            def _():
                dxm_ref[...] = dx
        else:
            rows = i * T + lax.broadcasted_iota(jnp.int32, (T, 1), 0)
            dxm_ref[...] = jnp.where(rows >= PAD, dx, 0.0).astype(BF16)
        part = jnp.sum(dyv * xh, axis=0, keepdims=True)

        @pl.when(i == 0)
        def _():
            dg_ref[...] = part

        @pl.when(i > 0)
        def _():
            dg_ref[...] += part

    row = pl.BlockSpec((T, w), lambda i: (i, 0))
    vec = pl.BlockSpec((1, w), lambda i: (0, 0))
    ins = [x, g, dy] + ([dres] if has_res else []) + deps
    if last:
        out_shape = (jax.ShapeDtypeStruct((n - T, w), F32), jax.ShapeDtypeStruct((T, w), F32))
        out_specs = (pl.BlockSpec((T, w), lambda i: (jnp.maximum(i - 1, 0), 0)), pl.BlockSpec((T, w), lambda i: (0, 0)))
    else:
        out_shape = (jax.ShapeDtypeStruct((n, w), F32), jax.ShapeDtypeStruct((n, w), BF16))
        out_specs = (row, row)
    return pl.pallas_call(
        body, name=name, out_shape=(*out_shape, jax.ShapeDtypeStruct((1, w), F32)), grid=(n // T,),
        in_specs=[row, vec, pl.BlockSpec((T, w), lambda i: (i, dy_col))] + ([row] if has_res else [])
        + [_HBM] * len(deps),
        out_specs=(*out_specs, vec))(*ins)


def loss_head(h, target):
    def body(h_ref, t_ref, dh_ref, dhm_ref, loss_ref):
        n = pl.program_id(0)
        e = jnp.where(n > 0, h_ref[...] - t_ref[...], 0.0)
        dh = e * (1.0 / D)
        dh_ref[...] = dh
        dhm_ref[...] = dh.astype(BF16)
        part = jnp.sum(jnp.sum(e * e, axis=1, keepdims=True), axis=0, keepdims=True) * (0.5 / D)

        @pl.when(n == 0)
        def _():
            loss_ref[...] = jnp.zeros_like(loss_ref)

        @pl.when(n > 0)
        def _():
            loss_ref[...] += jnp.broadcast_to(part, loss_ref.shape)

    row = pl.BlockSpec((T, D), lambda n: (n, 0))
    return pl.pallas_call(
        body, name="loss_head",
        out_shape=(jax.ShapeDtypeStruct((L, D), F32), jax.ShapeDtypeStruct((L, D), BF16),
                   jax.ShapeDtypeStruct((8, 128), F32)),
        grid=(NB,),
        in_specs=[row, pl.BlockSpec((T, D), lambda n: (jnp.maximum(n - 1, 0), 0))],
        out_specs=(row, row, pl.BlockSpec((8, 128), lambda n: (0, 0))))(h, target)


def _mm(name, a, b, *, grid, a_spec, b_spec, o_spec, out_shape, nt=False, ta=False, red=False, res=None,
        res_spec=None):
    def body(a_ref, b_ref, *rest):
        o_ref = rest[-1]
        av = a_ref[...]
        bv = b_ref[...]
        if bv.ndim == 3:
            bv = bv.reshape(bv.shape[0] * bv.shape[1], bv.shape[2])
        if ta:
            acc = lax.dot_general(av, bv, (((0,), (0,)), ((), ())), preferred_element_type=F32)
        else:
            acc = _dot(av, bv, nt)
        if red:
            k = pl.program_id(0)

            @pl.when(k == 0)
            def _():
                o_ref[...] = acc

            @pl.when(k > 0)
            def _():
                o_ref[...] += acc
        else:
            if res is not None:
                rows = lax.broadcasted_iota(jnp.int32, (acc.shape[0], 1), 0)
                acc = rest[0][...] + jnp.where(rows >= PAD, acc, 0.0)
            o_ref[...] = acc.astype(o_ref.dtype)

    ins = [a, b] + ([res] if res is not None else [])
    specs = [a_spec, b_spec] + ([res_spec] if res is not None else [])
    return pl.pallas_call(body, name=name, out_shape=out_shape, grid=grid, in_specs=specs, out_specs=o_spec,
                          compiler_params=_params(VMEM_BIG))(*ins)


def mm_nn(name, a, b, tn, out_dtype=F32, res=None):
    m, k = a.shape
    n = b.shape[1]
    return _mm(name, a, b, grid=(n // tn,),
               a_spec=pl.BlockSpec((m, k), lambda j: (0, 0)), b_spec=pl.BlockSpec((k, tn), lambda j: (0, j)),
               o_spec=pl.BlockSpec((m, tn), lambda j: (0, j)), out_shape=jax.ShapeDtypeStruct((m, n), out_dtype),
               res=res, res_spec=pl.BlockSpec((m, tn), lambda j: (0, j)))


def mm_nt(name, a, b, tn):
    m, k = a.shape
    n = b.shape[0]
    return _mm(name, a, b, grid=(n // tn,), nt=True,
               a_spec=pl.BlockSpec((m, k), lambda j: (0, 0)), b_spec=pl.BlockSpec((tn, k), lambda j: (j, 0)),
               o_spec=pl.BlockSpec((m, tn), lambda j: (0, j)), out_shape=jax.ShapeDtypeStruct((m, n), F32))


def mm_ta(name, a, b, tm, out_dtype=BF16):
    k, m = a.shape
    n = b.shape[1]
    return _mm(name, a, b, grid=(m // tm,), ta=True,
               a_spec=pl.BlockSpec((k, tm), lambda j: (0, j)), b_spec=pl.BlockSpec((k, n), lambda j: (0, 0)),
               o_spec=pl.BlockSpec((tm, n), lambda j: (j, 0)), out_shape=jax.ShapeDtypeStruct((m, n), out_dtype))


def ffn_up_fwd(name, u2, wupt):
    return _mm(name, u2, wupt, grid=(NDEV,), nt=True,
               a_spec=pl.BlockSpec((L, D), lambda j: (0, 0)),
               b_spec=pl.BlockSpec((None, UPW, D), lambda j: (j, 0, 0)),
               o_spec=pl.BlockSpec((None, L, UPW), lambda j: (j, 0, 0)),
               out_shape=jax.ShapeDtypeStruct((NDEV, L, UPW), F32))


def _slab_specs():
    gate = pl.BlockSpec((None, L, UPW), lambda j: (jnp.minimum(j, NPAIR - 1), 0, 0))
    up = pl.BlockSpec((None, L, UPW), lambda j: (jnp.maximum(j - NPAIR, 0), 0, 0))
    return gate, up


def ffn_up_dx(name, dfg, dfu, wupt):
    def body(g_ref, u_ref, b_ref, o_ref):
        j = pl.program_id(0)

        @pl.when(j == 0)
        def _():
            o_ref[...] = _dot(g_ref[...], b_ref[...])

        @pl.when(jnp.logical_and(j > 0, j < NPAIR))
        def _():
            o_ref[...] += _dot(g_ref[...], b_ref[...])

        @pl.when(j >= NPAIR)
        def _():
            o_ref[...] += _dot(u_ref[...], b_ref[...])

    gate, up = _slab_specs()
    return pl.pallas_call(
        body, name=name, out_shape=jax.ShapeDtypeStruct((L, D), F32), grid=(NDEV,),
        in_specs=[gate, up, pl.BlockSpec((None, UPW, D), lambda j: (j, 0, 0))],
        out_specs=pl.BlockSpec((L, D), lambda j: (0, 0)), compiler_params=_params(VMEM_BIG))(dfg, dfu, wupt)


def ffn_up_dw(name, u2, dfg, dfu):
    tdot = lambda a, b: lax.dot_general(a, b, (((0,), (0,)), ((), ())), preferred_element_type=F32)

    def body(a_ref, g_ref, u_ref, o_ref):
        j = pl.program_id(0)

        @pl.when(j < NPAIR)
        def _():
            o_ref[...] = tdot(g_ref[...], a_ref[...]).astype(o_ref.dtype)

        @pl.when(j >= NPAIR)
        def _():
            o_ref[...] = tdot(u_ref[...], a_ref[...]).astype(o_ref.dtype)

    gate, up = _slab_specs()
    return pl.pallas_call(
        body, name=name, out_shape=jax.ShapeDtypeStruct((NDEV, UPW, D), BF16), grid=(NDEV,),
        in_specs=[pl.BlockSpec((L, D), lambda j: (0, 0)), gate, up],
        out_specs=pl.BlockSpec((None, UPW, D), lambda j: (j, 0, 0)), compiler_params=_params(VMEM_BIG))(u2, dfg, dfu)


def ffn_down_fwd(name, a, wdn, res, tn=256):
    def body(a_ref, b_ref, r_ref, o_ref):
        acc = jnp.zeros((L, tn), F32)
        for g in range(NPAIR):
            bv = b_ref[2 * g:2 * g + 2]
            acc = acc + _dot(a_ref[g], bv.reshape(2 * DNR, tn))
        rows = lax.broadcasted_iota(jnp.int32, (L, 1), 0)
        o_ref[...] = r_ref[...] + jnp.where(rows >= PAD, acc, 0.0)

    return pl.pallas_call(
        body, name=name, out_shape=jax.ShapeDtypeStruct((L, D), F32), grid=(D // tn,),
        in_specs=[pl.BlockSpec((NPAIR, L, UPW), lambda j: (0, 0, 0)),
                  pl.BlockSpec((NDEV, DNR, tn), lambda j: (0, 0, j)),
                  pl.BlockSpec((L, tn), lambda j: (0, j))],
        out_specs=pl.BlockSpec((L, tn), lambda j: (0, j)),
        compiler_params=_params(VMEM_BIG))(a, wdn, res)


def ffn_down_dx(name, dh, wdn):
    return _mm(name, dh, wdn, grid=(NPAIR,), nt=True,
               a_spec=pl.BlockSpec((L, D), lambda g: (0, 0)),
               b_spec=pl.BlockSpec((2, DNR, D), lambda g: (g, 0, 0)),
               o_spec=pl.BlockSpec((None, L, UPW), lambda g: (g, 0, 0)),
               out_shape=jax.ShapeDtypeStruct((NPAIR, L, UPW), F32))


def ffn_down_dw(name, a, dh):
    return _mm(name, a, dh, grid=(NPAIR,), ta=True,
               a_spec=pl.BlockSpec((None, L, UPW), lambda g: (g, 0, 0)),
               b_spec=pl.BlockSpec((L, D), lambda g: (0, 0)),
               o_spec=pl.BlockSpec((UPW, D), lambda g: (g, 0)),
               out_shape=jax.ShapeDtypeStruct((D_FF, D), BF16))


_SLOPES = [2.0 ** (-8.0 * (h + 1) / NQ) for h in range(NQ)]
_SCALE = HD ** -0.5


QR = GQA * T
KC = 3 * T


def _attn_geometry(g, n, sink_ref):
    row = lax.broadcasted_iota(jnp.int32, (QR, KC), 0)
    col = lax.broadcasted_iota(jnp.int32, (QR, KC), 1)
    i = row & (T - 1)
    j = col & (T - 1)
    blk = col // T
    d_meta = n * T + i - j
    ok_meta = (j >= PAD) & (d_meta >= 0)
    ok_prev = j > i + jnp.where(n >= 2, 0, T)
    ok_cur = j <= i - jnp.where(n >= 1, 0, T)
    ok = ((blk == 0) & ok_meta) | ((blk == 1) & ok_prev) | ((blk == 2) & ok_cur)
    dist = jnp.where(blk == 0, jnp.minimum(d_meta, T), jnp.where(blk == 1, T + i - j, i - j)).astype(F32)
    head = lax.broadcasted_iota(jnp.int32, (QR, 1), 0) // T
    slope = jnp.zeros((QR, 1), F32)
    sink = jnp.zeros((QR, 1), F32)
    for hh in range(GQA):
        slope = jnp.where(head == hh, jnp.where(g == 0, _SLOPES[hh], _SLOPES[GQA + hh]), slope)
        sink = jnp.where(head == hh, sink_ref[g * GQA + hh], sink)
    return ok, slope * dist, sink


def _attn_probs(qn, keys, geo):
    ok, penalty, sink = geo
    s = jnp.where(ok, _dot(qn, keys, nt=True) * _SCALE - penalty, NEG)
    m = jnp.maximum(jnp.max(s, axis=-1, keepdims=True), sink)
    e = jnp.exp(s - m)
    e_sink = jnp.exp(sink - m)
    inv = 1.0 / (e.sum(axis=-1, keepdims=True) + e_sink)
    return e, inv, e_sink * inv


def _rms_rows(x, g):
    r = lax.rsqrt(jnp.mean(x * x, axis=-1, keepdims=True) + RMS_EPS)
    xh = x * r
    return xh * g, xh, r


def _rms_rows_bwd(dy, g, xh, r):
    dxh = dy * g
    return r * (dxh - xh * jnp.mean(dxh * xh, axis=-1, keepdims=True))


def _rows3(m_ref, p_ref, c_ref):
    return jnp.concatenate([m_ref[...], p_ref[...], c_ref[...]], axis=0)


_ATT_K, _ATT_V = 4, 5


def _nat_specs():
    qspec = pl.BlockSpec((T, GQA * HD), lambda n, g: (n, g))
    kv = lambda col: (pl.BlockSpec((T, 2 * HD), lambda n, g: (n, col)),
                      pl.BlockSpec((T, 2 * HD), lambda n, g: (jnp.maximum(n - 1, 0), col)),
                      pl.BlockSpec((T, 2 * HD), lambda n, g: (0, col)))
    return qspec, kv(_ATT_K), kv(_ATT_V)


def _group_lanes(g, x):
    return jnp.where(g == 0, x[:, :HD], x[:, HD:])


def _stack_heads(x):
    return jnp.concatenate([x[:, hh * HD:(hh + 1) * HD] for hh in range(GQA)], axis=0)


def _unstack_heads(x):
    return jnp.concatenate([x[hh * T:(hh + 1) * T] for hh in range(GQA)], axis=1)


def attn_fwd(proj, qg, kg, sinks, name):
    def body(sink_ref, q_ref, kc_ref, kp_ref, km_ref, vc_ref, vp_ref, vm_ref, qg_ref, kg_ref, o_ref):
        n = pl.program_id(0)
        g = pl.program_id(1)
        geo = _attn_geometry(g, n, sink_ref)
        keys = _bf(_rms_rows(_group_lanes(g, _rows3(km_ref, kp_ref, kc_ref)), kg_ref[...])[0])
        vals = _bf(_group_lanes(g, _rows3(vm_ref, vp_ref, vc_ref)))
        qn = _bf(_rms_rows(_stack_heads(q_ref[...]), qg_ref[...])[0])
        e, inv, _ = _attn_probs(qn, keys, geo)
        o_ref[...] = _unstack_heads(_dot(_bf(e), vals) * inv)

    qspec, (kc, kp, km), (vc, vp, vm) = _nat_specs()
    vec = pl.BlockSpec((1, HD), lambda n, g: (0, 0))
    return pl.pallas_call(
        body, name=name, out_shape=jax.ShapeDtypeStruct((L, NQ * HD), F32), grid=(NB, NKV),
        in_specs=[pl.BlockSpec(memory_space=pltpu.SMEM), qspec, kc, kp, km, vc, vp, vm, vec, vec],
        out_specs=qspec)(sinks, proj, proj, proj, proj, proj, proj, proj, qg, kg)


def attn_bwd(proj, do, qg, kg, sinks, name):
    def body(sink_ref, q_ref, kc_ref, kp_ref, km_ref, vc_ref, vp_ref, vm_ref, do_ref, qg_ref, kg_ref,
             dq_ref, dkc_ref, dkp_ref, dkm_ref, dvc_ref, dvp_ref, dvm_ref, dqg_ref, dkg_ref, dsk_ref):
        n = pl.program_id(0)
        g = pl.program_id(1)

        @pl.when(jnp.logical_and(g == 0, n == 0))
        def _():
            for r in (dkm_ref, dvm_ref, dqg_ref, dkg_ref, dsk_ref):
                r[...] = jnp.zeros_like(r)

        geo = _attn_geometry(g, n, sink_ref)
        kgv = kg_ref[...]
        qgv = qg_ref[...]
        kn_f, kh, kr = _rms_rows(_group_lanes(g, _rows3(km_ref, kp_ref, kc_ref)), kgv)
        keys = _bf(kn_f)
        vals = _bf(_group_lanes(g, _rows3(vm_ref, vp_ref, vc_ref)))
        qn_f, qh, qr = _rms_rows(_stack_heads(q_ref[...]), qgv)
        qn = _bf(qn_f)
        e, inv, p_sink = _attn_probs(qn, keys, geo)
        dob = _bf(_stack_heads(do_ref[...]))
        p = e * inv
        dp = _dot(dob, vals, nt=True)
        delta = (p * dp).sum(axis=-1, keepdims=True)
        head_row = lax.broadcasted_iota(jnp.int32, (NQ, 128), 0)
        dsk = jnp.zeros((NQ, 128), F32)
        for hh in range(GQA):
            part = -jnp.sum((p_sink * delta)[hh * T:(hh + 1) * T], axis=0, keepdims=True)
            dsk = jnp.where(head_row == g * GQA + hh, part, dsk)
        dsk_ref[...] += dsk
        ds = p * (dp - delta)
        dqn = _dot(_bf(ds), keys) * _SCALE
        dq_ref[...] = _unstack_heads(_rms_rows_bwd(dqn, qgv, qh, qr))
        dqg_ref[...] += jnp.sum(dqn * qh, axis=0, keepdims=True)
        dkn = _dot(_bf(ds.T), qn) * _SCALE
        dkg_ref[...] += jnp.sum(dkn * kh, axis=0, keepdims=True)
        dk_all = _rms_rows_bwd(dkn, kgv, kh, kr)
        dv_all = _dot(_bf(p.T), dob)

        for gg in range(NKV):
            @pl.when(g == gg)
            def _(gg=gg):
                lanes = slice(gg * HD, (gg + 1) * HD)
                dkm_ref[:, lanes] += dk_all[0:T]
                dkp_ref[:, lanes] = dk_all[T:2 * T]
                dkc_ref[:, lanes] = dk_all[2 * T:3 * T]
                dvm_ref[:, lanes] += dv_all[0:T]
                dvp_ref[:, lanes] = dv_all[T:2 * T]
                dvc_ref[:, lanes] = dv_all[2 * T:3 * T]

    qspec, (kc, kp, km), (vc, vp, vm) = _nat_specs()
    vec = pl.BlockSpec((1, HD), lambda n, g: (0, 0))
    cur = pl.BlockSpec((T, 2 * HD), lambda n, g: (n, 0))
    meta = pl.BlockSpec((T, 2 * HD), lambda n, g: (0, 0))
    kv_shape = jax.ShapeDtypeStruct((L, 2 * HD), F32)
    meta_shape = jax.ShapeDtypeStruct((T, 2 * HD), F32)
    vec_shape = jax.ShapeDtypeStruct((1, HD), F32)
    return pl.pallas_call(
        body, name=name,
        out_shape=(jax.ShapeDtypeStruct((L, NQ * HD), F32), kv_shape, kv_shape, meta_shape, kv_shape, kv_shape,
                   meta_shape, vec_shape, vec_shape, jax.ShapeDtypeStruct((NQ, 128), F32)),
        grid=(NB, NKV),
        in_specs=[pl.BlockSpec(memory_space=pltpu.SMEM), qspec, kc, kp, km, vc, vp, vm, qspec, vec, vec],
        out_specs=(qspec, cur, cur, meta, cur, cur, meta, vec, vec,
                   pl.BlockSpec((NQ, 128), lambda n, g: (0, 0))))(
            sinks, proj, proj, proj, proj, proj, proj, proj, do, qg, kg)


RW = RH * HD


def _ret_tables():
    h = np.arange(RH, dtype=np.float64)
    lg = np.log1p(-np.exp2(-5.0 - h))
    idx = np.arange(T, dtype=np.float64)
    diff = idx[:, None] - idx[None, :]
    decay = np.where(diff[None] >= 0, np.exp(np.maximum(diff, 0.0)[None] * lg[:, None, None]), 0.0)
    zeta = np.exp((T - 1 - idx)[None, :] * lg[:, None])
    xi = np.exp((idx + 1.0)[None, :] * lg[:, None])
    cd = np.exp(T * lg)
    lanes = lambda a: np.repeat(a.T, HD, axis=1)
    head_of = np.arange(RW) // HD
    same = (head_of[:, None] == head_of[None, :]).astype(np.float64)
    f = lambda a: jnp.asarray(a, F32)
    return dict(decay=f(decay), zeta=f(lanes(zeta)), xi=f(lanes(xi)), cd=f(np.repeat(cd, HD)[None, :]),
                head=f((head_of[None, :] == np.arange(RH)[:, None]).astype(np.float64)[:, None, :]),
                same=f(same), avg=jnp.asarray(same / HD, BF16))


def _seg_mean(x, avg):
    hi = _bf(x)
    lo = _bf(x - hi.astype(F32))
    return _dot(hi, avg) + _dot(lo, avg)


def _ret_specs(col0, order):
    return lambda col: pl.BlockSpec((T, RW), lambda i: (order(i), col0 + col))


def _ret_chunk(q, kf, v, s, tb):
    dec, xi, head = tb
    vb = _bf(v)
    kb = _bf(kf)
    y = _dot(_bf(q * xi), _bf(s))
    a = []
    for h in range(RH):
        a.append(_dot(_bf(q * head[h]), kb, nt=True) * dec[h])
        y = y + head[h] * _dot(_bf(a[h]), vb)
    return a, y


def _gn_rows(y):
    mu = jnp.mean(y, axis=-1, keepdims=True)
    yc = y - mu
    rstd = lax.rsqrt(jnp.mean(yc * yc, axis=-1, keepdims=True) + LN_EPS)
    return yc * rstd, rstd


_RET_COL0 = _RET_Q = 5


def _ret_consts(tb):
    names = ("decay", "zeta", "xi", "cd", "head", "same", "avg")
    full = lambda a: pl.BlockSpec(a.shape, lambda i: (0,) * a.ndim)
    return [tb[n] for n in names], [full(tb[n]) for n in names]


def _ret_inputs(n, q_ref, k_ref, v_ref):
    rows = n * T + lax.broadcasted_iota(jnp.int32, (T, 1), 0)
    valid = rows >= PAD
    return q_ref[...], jnp.where(valid, k_ref[...] * (HD ** -0.5), 0.0), v_ref[...], valid


def ret_fwd(proj, gng, tb, name):
    def body(q_ref, k_ref, v_ref, g_ref, dec_ref, zeta_ref, xi_ref, cd_ref, head_ref, same_ref, avg_ref, gng_ref,
             y_ref, st_ref, s_scr):
        n = pl.program_id(0)

        @pl.when(n == 0)
        def _():
            s_scr[...] = jnp.zeros_like(s_scr)

        s = s_scr[...]
        st_ref[...] = s
        q, kf, v, _ = _ret_inputs(n, q_ref, k_ref, v_ref)
        _, y = _ret_chunk(q, kf, v, s, (dec_ref, xi_ref[...], head_ref))
        s_scr[...] = cd_ref[...] * s + same_ref[...] * _dot(_bf((kf * zeta_ref[...]).T), _bf(v))
        avg = avg_ref[...]
        yc = y - _seg_mean(y, avg)
        yh = yc * lax.rsqrt(_seg_mean(yc * yc, avg) + LN_EPS)
        gv = g_ref[...]
        y_ref[...] = gv * _sig(gv) * (yh * gng_ref[...])

    col = _ret_specs(_RET_COL0, lambda i: i)
    consts, cspecs = _ret_consts(tb)
    return pl.pallas_call(
        body, name=name,
        out_shape=(jax.ShapeDtypeStruct((L, RW), F32), jax.ShapeDtypeStruct((NB, RW, RW), F32)),
        grid=(NB,),
        in_specs=[col(0), col(1), col(2), col(3)] + cspecs + [pl.BlockSpec((1, RW), lambda i: (0, 0))],
        out_specs=(pl.BlockSpec((T, RW), lambda i: (i, 0)), pl.BlockSpec((None, RW, RW), lambda i: (i, 0, 0))),
        scratch_shapes=[pltpu.VMEM((RW, RW), F32)])(proj, proj, proj, proj, *consts, gng)


def ret_bwd(proj, dycat, states, gng, tb, name):
    def body(q_ref, k_ref, v_ref, g_ref, do_ref, st_ref, dec_ref, zeta_ref, xi_ref, cd_ref, head_ref, same_ref,
             avg_ref, gng_ref, d_ref, dgn_ref, ds_scr):
        i = pl.program_id(0)
        n = NB - 1 - i

        @pl.when(i == 0)
        def _():
            ds_scr[...] = jnp.zeros_like(ds_scr)

        dsn = _bf(ds_scr[...])
        s = st_ref[...]
        sb = _bf(s)
        q, kf, v, valid = _ret_inputs(n, q_ref, k_ref, v_ref)
        xi, zeta, avg = xi_ref[...], zeta_ref[...], avg_ref[...]
        a, y = _ret_chunk(q, kf, v, s, (dec_ref, xi, head_ref))
        yc = y - _seg_mean(y, avg)
        rstd = lax.rsqrt(_seg_mean(yc * yc, avg) + LN_EPS)
        yh = yc * rstd
        gv = g_ref[...]
        sg = _sig(gv)
        sil = gv * sg
        gn = gng_ref[...]
        dout = do_ref[...]
        d_ref[:, 3 * RW:4 * RW] = dout * (yh * gn) * (sg * (1.0 + gv * (1.0 - sg)))
        dyh = dout * sil * gn
        part = jnp.sum(dout * sil * yh, axis=0, keepdims=True)

        @pl.when(i == 0)
        def _():
            dgn_ref[...] = part

        @pl.when(i > 0)
        def _():
            dgn_ref[...] += part

        dy = rstd * (dyh - _seg_mean(dyh, avg) - yh * _seg_mean(dyh * yh, avg))
        dyb = _bf(dy)
        vb, kb, qb = _bf(v), _bf(kf), _bf(q)
        dq = _dot(dyb, sb, nt=True) * xi
        dkf = _dot(vb, dsn, nt=True) * zeta
        dv = _dot(_bf(kf * zeta), dsn)
        for h in range(RH):
            m = head_ref[h]
            da = _dot(_bf(dy * m), vb, nt=True) * dec_ref[h]
            dv = dv + m * _dot(_bf(a[h].T), dyb)
            dq = dq + m * _dot(_bf(da), kb)
            dkf = dkf + m * _dot(_bf(da.T), qb)
        d_ref[:, 0:RW] = dq
        d_ref[:, RW:2 * RW] = jnp.where(valid, dkf * (HD ** -0.5), 0.0)
        d_ref[:, 2 * RW:3 * RW] = dv
        ds_scr[...] = cd_ref[...] * ds_scr[...] + same_ref[...] * _dot(_bf((q * xi).T), dyb)

    back = lambda i: NB - 1 - i
    col = _ret_specs(_RET_COL0, back)
    consts, cspecs = _ret_consts(tb)
    vec = pl.BlockSpec((1, RW), lambda i: (0, 0))
    return pl.pallas_call(
        body, name=name,
        out_shape=(jax.ShapeDtypeStruct((L, 4 * RW), F32), jax.ShapeDtypeStruct((1, RW), F32)),
        grid=(NB,),
        in_specs=[col(0), col(1), col(2), col(3), _ret_specs(3, back)(0),
                  pl.BlockSpec((None, RW, RW), lambda i: (back(i), 0, 0))] + cspecs + [vec],
        out_specs=(pl.BlockSpec((T, 4 * RW), lambda i: (back(i), 0)), vec),
        scratch_shapes=[pltpu.VMEM((RW, RW), F32)])(proj, proj, proj, proj, dycat, states, *consts, gng)


HALO = 32
TAP0 = HALO - (CONV_K - 1)


def _conv_specs():
    cur = lambda col=0: pl.BlockSpec((T, CV), lambda c: (c, col))
    before = lambda col=0: pl.BlockSpec((HALO, CV), lambda c: (jnp.maximum(c * (T // HALO) - 1, 0), col))
    after = pl.BlockSpec((HALO, CV), lambda c: (jnp.minimum((c + 1) * (T // HALO), L // HALO - 1), 0))
    full = lambda r, w: pl.BlockSpec((r, w), lambda c: (0, 0))
    return cur, before, after, full


_CONV_A, _CONV_B = 3, 4
_DY_CONV = 2


def _conv_post(xc, lg_ref, lb_ref):
    xh, rstd = _gn_rows(xc)
    z = xh * lg_ref[...] + lb_ref[...]
    return xh, rstd, z, _sig(z)


def conv_fwd(proj, w, b, lg, lb, pw, og, name):
    def body(ca_ref, cah_ref, cb_ref, cbh_ref, w_ref, b_ref, lg_ref, lb_ref, pw_ref, og_ref, y_ref, xc_ref, u_scr):
        c = pl.program_id(0)
        u_scr[0:HALO, :] = jnp.where(c > 0, cah_ref[...] * _sig(cbh_ref[...]), 0.0)
        u_scr[HALO:HALO + T, :] = ca_ref[...] * _sig(cb_ref[...])
        acc = jnp.zeros((T, CV), F32)
        for k in range(CONV_K):
            acc = acc + w_ref[k:k + 1, :] * u_scr[TAP0 + k:TAP0 + k + T, :]
        xc = acc + b_ref[...]
        xc_ref[...] = xc
        _, _, z, sg = _conv_post(xc, lg_ref, lb_ref)
        zp = _dot(_bf(z * sg), pw_ref[...])
        y_ref[...] = _rms_rows(zp, og_ref[...])[0]

    cur, before, _, full = _conv_specs()
    vec = full(1, CV)
    seq = jax.ShapeDtypeStruct((L, CV), F32)
    return pl.pallas_call(
        body, name=name, out_shape=(seq, seq), grid=(NB,),
        in_specs=[cur(_CONV_A), before(_CONV_A), cur(_CONV_B), before(_CONV_B), full(CONV_K, CV), vec, vec, vec,
                  full(CV, CV), vec],
        out_specs=(cur(), cur()),
        scratch_shapes=[pltpu.VMEM((HALO + T, CV), F32)])(proj, proj, proj, proj, w, b, lg, lb, pw, og)


def conv_bwd(proj, xc, dycat, w, lg, lb, pw, og, name):
    def body1(xc_ref, dy_ref, lg_ref, lb_ref, pw_ref, og_ref, dxc_ref, db_ref, dlg_ref, dlb_ref, dog_ref, dpw_ref):
        c = pl.program_id(0)

        @pl.when(c == 0)
        def _():
            for r in (db_ref, dlg_ref, dlb_ref, dog_ref, dpw_ref):
                r[...] = jnp.zeros_like(r)

        xh, rstd, z, sg = _conv_post(xc_ref[...], lg_ref, lb_ref)
        s = z * sg
        zp = _dot(_bf(s), pw_ref[...])
        ogv = og_ref[...]
        _, zph, r2 = _rms_rows(zp, ogv)
        dyv = dy_ref[...]
        dog_ref[...] += jnp.sum(dyv * zph, axis=0, keepdims=True)
        dzpb = _bf(_rms_rows_bwd(dyv, ogv, zph, r2))
        dpw_ref[...] += _dot(_bf(s.T), dzpb)
        dz = _dot(dzpb, pw_ref[...], nt=True) * (sg * (1.0 + z * (1.0 - sg)))
        dlg_ref[...] += jnp.sum(dz * xh, axis=0, keepdims=True)
        dlb_ref[...] += jnp.sum(dz, axis=0, keepdims=True)
        dxh = dz * lg_ref[...]
        dxc = rstd * (dxh - jnp.mean(dxh, axis=-1, keepdims=True) - xh * jnp.mean(dxh * xh, axis=-1, keepdims=True))
        db_ref[...] += jnp.sum(dxc, axis=0, keepdims=True)
        dxc_ref[...] = dxc

    def body2(dx_ref, dxa_ref, ca_ref, cb_ref, w_ref, dca_ref, dcb_ref, dw_ref, d_scr):
        c = pl.program_id(0)

        @pl.when(c == 0)
        def _():
            dw_ref[...] = jnp.zeros_like(dw_ref)

        d_scr[0:T, :] = dx_ref[...]
        d_scr[T:T + HALO, :] = jnp.where(c < NB - 1, dxa_ref[...], 0.0)
        ca = ca_ref[...]
        sg = _sig(cb_ref[...])
        u = ca * sg
        du = jnp.zeros((T, CV), F32)
        for k in range(CONV_K):
            off = CONV_K - 1 - k
            dj = d_scr[off:off + T, :]
            du = du + w_ref[k:k + 1, :] * dj
            dw_ref[k:k + 1, :] += jnp.sum(u * dj, axis=0, keepdims=True)
        dca_ref[...] = du * sg
        dcb_ref[...] = du * ca * sg * (1.0 - sg)

    cur, _, after, full = _conv_specs()
    seq = jax.ShapeDtypeStruct((L, CV), F32)
    vsh = jax.ShapeDtypeStruct((1, CV), F32)
    vec = full(1, CV)
    dxc, db, dlg, dlb, dog, dpw = pl.pallas_call(
        body1, name=name + "_a",
        out_shape=(seq, vsh, vsh, vsh, vsh, jax.ShapeDtypeStruct((CV, CV), F32)),
        grid=(NB,),
        in_specs=[cur(), cur(_DY_CONV), vec, vec, full(CV, CV), vec],
        out_specs=(cur(), vec, vec, vec, vec, full(CV, CV)))(xc, dycat, lg, lb, pw, og)
    dca, dcb, dw = pl.pallas_call(
        body2, name=name + "_b", out_shape=(seq, seq, jax.ShapeDtypeStruct((CONV_K, CV), F32)), grid=(NB,),
        in_specs=[cur(), after, cur(_CONV_A), cur(_CONV_B), full(CONV_K, CV)],
        out_specs=(cur(), cur(), full(CONV_K, CV)),
        scratch_shapes=[pltpu.VMEM((T + HALO, CV), F32)])(dxc, dxc, proj, proj, w)
    return dca, dcb, dw, db, dlg, dlb, dog, dpw


HR = 8


def _ffn_specs():
    cur = lambda off: pl.BlockSpec((None, T, UPW), lambda j, c: (j + off, c, 0))
    before = lambda off: pl.BlockSpec((None, HR, UPW), lambda j, c: (j + off, jnp.maximum(c * (T // HR) - 1, 0), 0))
    after = lambda off: pl.BlockSpec(
        (None, HR, UPW), lambda j, c: (j + off, jnp.minimum((c + 1) * (T // HR), L // HR - 1), 0))
    wspec = lambda off, r: pl.BlockSpec((None, r, UPW), lambda j, c: (j + off, 0, 0))
    return cur, before, after, wspec


def ffn_act_fwd(f, w, b, name):
    strip = 16

    def body(fg_ref, fgh_ref, fu_ref, fuh_ref, wg_ref, wu_ref, bg_ref, bu_ref, a_ref, gc_ref, uc_ref, hg_scr, hu_scr):
        c = pl.program_id(1)
        for scr, x_ref, xh_ref in ((hg_scr, fg_ref, fgh_ref), (hu_scr, fu_ref, fuh_ref)):
            scr[0:HR, :] = jnp.where(c > 0, xh_ref[...], 0.0)
            scr[HR:HR + strip, :] = x_ref[0:strip, :]
        wg = [wg_ref[k:k + 1, :] for k in range(FFN_K)]
        wu = [wu_ref[k:k + 1, :] for k in range(FFN_K)]
        bg, bu = bg_ref[...], bu_ref[...]

        def conv(src, base, w, b):
            acc = b
            for k in range(FFN_K):
                o = base - (FFN_K - 1) + k
                acc = acc + w[k] * src[o:o + strip, :]
            return acc

        for s in range(T // strip):
            r = s * strip
            gate = conv(hg_scr, HR, wg, bg) if s == 0 else conv(fg_ref, r, wg, bg)
            up = conv(hu_scr, HR, wu, bu) if s == 0 else conv(fu_ref, r, wu, bu)
            gc_ref[r:r + strip, :] = gate
            uc_ref[r:r + strip, :] = up
            a_ref[r:r + strip, :] = (gate * _sig(gate) * up).astype(BF16)

    cur, before, _, wspec = _ffn_specs()
    pre = jax.ShapeDtypeStruct((NPAIR, L, UPW), F32)
    return pl.pallas_call(
        body, name=name, out_shape=(jax.ShapeDtypeStruct((NPAIR, L, UPW), BF16), pre, pre), grid=(NPAIR, NB),
        in_specs=[cur(0), before(0), cur(NPAIR), before(NPAIR), wspec(0, FFN_K), wspec(NPAIR, FFN_K),
                  wspec(0, 1), wspec(NPAIR, 1)],
        out_specs=(cur(0), cur(0), cur(0)),
        scratch_shapes=[pltpu.VMEM((HR + strip, UPW), F32), pltpu.VMEM((HR + strip, UPW), F32)])(
            f, f, f, f, w, w, b, b)


def ffn_act_bwd(f, gc, uc, da, w, name):
    ext = T + HR
    sub = 8

    def body(fg_ref, fu_ref, gc_ref, gca_ref, uc_ref, uca_ref, da_ref, daa_ref, wg_ref, wu_ref,
             dfg_ref, dfu_ref, dwg_ref, dwu_ref, dbg_ref, dbu_ref, dg_scr, du_scr):
        c = pl.program_id(1)
        wg = [wg_ref[k:k + 1, :] for k in range(FFN_K)]
        wu = [wu_ref[k:k + 1, :] for k in range(FFN_K)]

        for s in range(ext // sub):
            r = s * sub
            if r < T:
                gate, up, dav = gc_ref[r:r + sub, :], uc_ref[r:r + sub, :], da_ref[r:r + sub, :]
            else:
                gate, up, dav = gca_ref[...], uca_ref[...], jnp.where(c < NB - 1, daa_ref[...], 0.0)
            sg = _sig(gate)
            dg_scr[r:r + sub, :] = dav * up * (sg * (1.0 + gate * (1.0 - sg)))
            du_scr[r:r + sub, :] = dav * gate * sg

        def back(d_scr, x_ref, w, df_ref, dw_ref, db_ref):
            accs = [jnp.zeros((sub, UPW), F32) for _ in range(FFN_K + 1)]
            for s in range(T // (2 * sub)):
                pieces = []
                for r in (2 * s * sub, (2 * s + 1) * sub):
                    xv = x_ref[r:r + sub, :]
                    df = jnp.zeros((sub, UPW), F32)
                    for k in range(FFN_K):
                        off = FFN_K - 1 - k
                        dj = d_scr[r + off:r + off + sub, :]
                        df = df + w[k] * dj
                        accs[k] = accs[k] + xv * dj
                        if off == 0:
                            accs[FFN_K] = accs[FFN_K] + dj
                    pieces.append(df)
                df_ref[2 * s * sub:2 * (s + 1) * sub, :] = jnp.concatenate(pieces, axis=0).astype(BF16)
            dwp = _stack_rows([jnp.sum(a, axis=0, keepdims=True) for a in accs[:FFN_K]])
            dbp = jnp.sum(accs[FFN_K], axis=0, keepdims=True)

            @pl.when(c == 0)
            def _():
                dw_ref[...] = dwp
                db_ref[...] = dbp

            @pl.when(c > 0)
            def _():
                dw_ref[...] += dwp
                db_ref[...] += dbp

        back(dg_scr, fg_ref, wg, dfg_ref, dwg_ref, dbg_ref)
        back(du_scr, fu_ref, wu, dfu_ref, dwu_ref, dbu_ref)

    cur, _, after, wspec = _ffn_specs()
    slab = jax.ShapeDtypeStruct((NPAIR, L, UPW), BF16)
    wsh = jax.ShapeDtypeStruct((NPAIR, FFN_K, UPW), F32)
    bsh = jax.ShapeDtypeStruct((NPAIR, 1, UPW), F32)
    return pl.pallas_call(
        body, name=name, out_shape=(slab, slab, wsh, wsh, bsh, bsh), grid=(NPAIR, NB),
        in_specs=[cur(0), cur(NPAIR), cur(0), after(0), cur(0), after(0), cur(0), after(0),
                  wspec(0, FFN_K), wspec(NPAIR, FFN_K)],
        out_specs=(cur(0), cur(0), wspec(0, FFN_K), wspec(0, FFN_K), wspec(0, 1), wspec(0, 1)),
        scratch_shapes=[pltpu.VMEM((ext, UPW), F32), pltpu.VMEM((ext, UPW), F32)])(
            f, f, gc, gc, uc, uc, da, da, w, w)


def _mesh_pos():
    return lax.axis_index("x"), lax.axis_index("y"), lax.axis_index("c")


def _peer(pos, k):
    x, y, c = pos
    px = 1 - x if k & 4 else x
    py = 1 - y if k & 2 else y
    pc = 1 - c if k & 1 else c
    return (px, py, pc), 4 * px + 2 * py + pc


_CHIP_FLIPS = (4, 2, 6)
_HBM = pl.BlockSpec(memory_space=pl.ANY)


def all_gather(shards, name):
    nt = len(shards)

    def body(*refs):
        ins, outs = refs[:nt], refs[nt:2 * nt]
        send, recv, local = refs[2 * nt:]
        pos = _mesh_pos()
        me = 4 * pos[0] + 2 * pos[1] + pos[2]
        sib, sib_id = _peer(pos, 1)

        def copy(t, k, block_id, to, src=None):
            dst = outs[t].at[block_id]
            return pltpu.make_async_remote_copy(
                src_ref=dst if src is None else src, dst_ref=dst, send_sem=send.at[t, k], recv_sem=recv.at[t, k],
                device_id=to, device_id_type=MESH)

        locals_ = [pltpu.make_async_copy(ins[t], outs[t].at[me], local.at[t]) for t in range(nt)]
        for cp in locals_:
            cp.start()
        started = []
        for j, flip in enumerate(_CHIP_FLIPS):
            for t in range(nt):
                started.append(copy(t, 1 + j, me, _peer(pos, flip)[0], src=ins[t]))
        for t in range(nt):
            started.append(copy(t, 0, me, sib, src=ins[t]))
        for cp in started:
            cp.start()
        for j, flip in enumerate(_CHIP_FLIPS):
            _, pid = _peer(pos, flip)
            for t in range(nt):
                copy(t, 1 + j, pid, sib).wait_recv()
                fwd = copy(t, 4 + j, pid, sib)
                fwd.start()
                started.append(fwd)
        for t in range(nt):
            copy(t, 0, sib_id, sib).wait_recv()
        for j, flip in enumerate(_CHIP_FLIPS):
            _, pid = _peer(pos, flip | 1)
            for t in range(nt):
                copy(t, 4 + j, pid, sib).wait_recv()
        for cp in started:
            cp.wait_send()
        for cp in locals_:
            cp.wait()

    return pl.pallas_call(
        body, name=name,
        out_shape=tuple(jax.ShapeDtypeStruct((NDEV,) + s.shape, s.dtype) for s in shards),
        in_specs=[_HBM] * nt, out_specs=tuple([_HBM] * nt),
        scratch_shapes=[pltpu.SemaphoreType.DMA((nt, NDEV - 1)), pltpu.SemaphoreType.DMA((nt, NDEV - 1)),
                        pltpu.SemaphoreType.DMA((nt,))])(*shards)


_SPLIT = dict(has_side_effects=pltpu.SideEffectType.DATAFLOW_SIDE_EFFECTING)
_SEM = pl.BlockSpec(memory_space=pltpu.SEMAPHORE)


def _exchange_start(name, srcs, lands, copies, after):
    ns, nl = len(srcs), len(lands)
    ncopy = len(copies(None, [None] * ns, [None] * nl))

    def body(*refs):
        src_refs, land_refs = refs[:ns], refs[ns:ns + nl]
        send, recv = refs[ns + nl + len(after):ns + nl + len(after) + 2]
        token = refs[-1]
        for i, (src, dst, peer) in enumerate(copies(_mesh_pos(), src_refs, land_refs)):
            pltpu.make_async_remote_copy(src_ref=src, dst_ref=dst, send_sem=send.at[i], recv_sem=recv.at[i],
                                         device_id=peer, device_id_type=MESH).start()
        token[...] = jnp.zeros_like(token)

    hbm = lambda a: pltpu.HBM(a.shape, a.dtype)
    out = pl.pallas_call(
        body, name=name,
        out_shape=(pltpu.SemaphoreType.DMA((ncopy,)), pltpu.SemaphoreType.DMA((ncopy,)),
                   *[hbm(a) for a in srcs], *[hbm(a) for a in lands], jax.ShapeDtypeStruct((8, 128), F32)),
        in_specs=[pl.BlockSpec(memory_space=pltpu.HBM)] * (ns + nl) + [_HBM] * len(after),
        out_specs=(_SEM, _SEM, *[pl.BlockSpec(memory_space=pltpu.HBM)] * (ns + nl),
                   pl.BlockSpec(memory_space=pltpu.VMEM)),
        input_output_aliases={i: 2 + i for i in range(ns + nl)},
        compiler_params=pltpu.CompilerParams(**_SPLIT))(
            *[pltpu.with_memory_space_constraint(a, pltpu.HBM) for a in list(srcs) + list(lands)], *after)
    return out[0], out[1], list(out[2:2 + ns]), list(out[2 + ns:2 + ns + nl]), out[-1]


def _exchange_wait(name, send, recv, srcs, lands, copies, after):
    ns, nl = len(srcs), len(lands)

    def body(*refs):
        src_refs, land_refs = refs[:ns], refs[ns:ns + nl]
        send_ref, recv_ref = refs[ns + nl:ns + nl + 2]
        pos = _mesh_pos()
        for i, (src, dst, peer) in enumerate(copies(pos, src_refs, land_refs, arriving=True)):
            cp = pltpu.make_async_remote_copy(src_ref=src, dst_ref=dst, send_sem=send_ref.at[i], recv_sem=recv_ref.at[i],
                                              device_id=peer, device_id_type=MESH)
            cp.wait_send()
            cp.wait_recv()

    hbm = lambda a: pltpu.HBM(a.shape, a.dtype)
    out = pl.pallas_call(
        body, name=name, out_shape=tuple(hbm(a) for a in list(srcs) + list(lands)),
        in_specs=[pl.BlockSpec(memory_space=pltpu.HBM)] * (ns + nl) + [_SEM, _SEM] + [_HBM] * len(after),
        out_specs=tuple([pl.BlockSpec(memory_space=pltpu.HBM)] * (ns + nl)),
        input_output_aliases={i: i for i in range(ns + nl)},
        compiler_params=pltpu.CompilerParams(**_SPLIT))(*srcs, *lands, send, recv, *after)
    return list(out[:ns]), list(out[ns:])


def _gather_copies(pos, srcs, lands, arriving=False):
    if pos is None:
        return [None] * (len(srcs) * (NDEV - 1))
    me = 4 * pos[0] + 2 * pos[1] + pos[2]
    out = []
    for src, land in zip(srcs, lands):
        for k in range(1, NDEV):
            peer, pid = _peer(pos, k)
            out.append((src, land.at[pid if arriving else me], peer))
    return out


def _scatter_copies(pos, srcs, lands, arriving=False):
    if pos is None:
        return [None] * (len(srcs) * (NDEV - 1))
    out = []
    for src, land in zip(srcs, lands):
        for k in range(1, NDEV):
            peer, pid = _peer(pos, k)
            out.append((src.at[pid], land.at[k - 1], peer))
    return out


def gather_start(name, shards, lands, after=()):
    return _exchange_start(name, shards, lands, _gather_copies, list(after))


def gather_wait(name, handle, after=()):
    send, recv, srcs, lands, _ = handle
    return _exchange_wait(name, send, recv, srcs, lands, _gather_copies, list(after))


def scatter_start(name, grads, after=()):
    lands = [pltpu.with_memory_space_constraint(lax.empty((NDEV - 1,) + g.shape[1:], g.dtype), pltpu.HBM)
             for g in grads]
    return _exchange_start(name, grads, lands, _scatter_copies, list(after))


def scatter_wait(name, handle, after=()):
    send, recv, srcs, lands, _ = handle
    return _exchange_wait(name, send, recv, srcs, lands, _scatter_copies, list(after))


def sum_slots(parts, name):
    def body(p_ref, o_ref):
        acc = p_ref[0]
        for d in range(1, NDEV):
            acc = acc + p_ref[d]
        o_ref[...] = acc

    return pl.pallas_call(body, name=name, out_shape=jax.ShapeDtypeStruct(parts.shape[1:], F32))(parts)


def _adamw_update(g, w_ref, m_ref, v_ref, g_ref, d_ref, nm_ref, nv_ref):
    g_ref[...] = g
    nm = ADAM_B1 * m_ref[...] + (1.0 - ADAM_B1) * g
    nv = ADAM_B2 * v_ref[...] + (1.0 - ADAM_B2) * (g * g)
    nm_ref[...] = nm
    nv_ref[...] = nv
    m_hat = nm / (1.0 - ADAM_B1 ** ADAM_STEP)
    v_hat = nv / (1.0 - ADAM_B2 ** ADAM_STEP)
    d_ref[...] = -ADAM_LR * (m_hat / (jnp.sqrt(v_hat) + ADAM_EPS) + ADAM_WD * w_ref[...])


def adamw_big(own, lands, w, m, v, me, tr, name):
    _, r, c = own[0].shape
    nt = r // tr

    def body(me_ref, *refs):
        ins, (w_ref, m_ref, v_ref), outs = refs[:2 * DEPTH], refs[2 * DEPTH:2 * DEPTH + 3], refs[2 * DEPTH + 3:]
        layer = pl.program_id(0)
        for l in range(DEPTH):
            @pl.when(layer == l)
            def _(l=l):
                g = ins[2 * l][...].astype(F32)
                for s in range(NDEV - 1):
                    g = g + ins[2 * l + 1][s].astype(F32)
                _adamw_update(g, w_ref, m_ref, v_ref, *outs)

    pick = lambda l: (lambda layer, i, me_ref: jnp.where(layer == l, i, 0))
    in_specs = []
    for l in range(DEPTH):
        in_specs.append(pl.BlockSpec((None, tr, c), lambda layer, i, me_ref, f=pick(l): (me_ref[0], f(layer, i, me_ref), 0)))
        in_specs.append(pl.BlockSpec((NDEV - 1, tr, c), lambda layer, i, me_ref, f=pick(l): (0, f(layer, i, me_ref), 0)))
    blk = pl.BlockSpec((tr, c), lambda layer, i, me_ref: (layer * nt + i, 0))
    sh = jax.ShapeDtypeStruct((DEPTH * r, c), F32)
    args = [a for l in range(DEPTH) for a in (own[l], lands[l])]
    return pl.pallas_call(
        body, name=name, out_shape=(sh, sh, sh, sh),
        grid_spec=pltpu.PrefetchScalarGridSpec(
            num_scalar_prefetch=1, grid=(DEPTH, nt), in_specs=in_specs + [blk, blk, blk],
            out_specs=(blk, blk, blk, blk)),
        compiler_params=_params(VMEM_BIG))(me, *args, w, m, v)


def adamw_many(gs, ws, ms, vs, name):
    n = len(gs)

    def body(*refs):
        outs = refs[4 * n:]
        for i in range(n):
            _adamw_update(refs[i][...], refs[n + i], refs[2 * n + i], refs[3 * n + i], *outs[4 * i:4 * i + 4])

    out_shape = tuple(jax.ShapeDtypeStruct(w.shape, F32) for w in ws for _ in range(4))
    res = pl.pallas_call(body, name=name, out_shape=out_shape)(*gs, *ws, *ms, *vs)
    return [res[4 * i:4 * i + 4] for i in range(n)]


def local_step(x, target, P, hooks):
    tb = _ret_tables()
    h = jnp.concatenate([jnp.zeros((PAD, D), F32), P["meta"], x], axis=0)
    stash = []
    for l in range(DEPTH):
        s = {"h": h}
        s["w_in"], s["cv_pw"], s["w_out"] = hooks["mixer_w"](l, h)
        u = s["u"] = rms_fwd(h, P["norm_mix_g"][l], BF16, f"rms_mix_{l}", after=hooks["first_after"] if l == 0 else None)
        proj = s["proj"] = mm_nt(f"in_proj_{l}", u, s["w_in"], 768)
        s["o"] = attn_fwd(proj, P["q_norm_g"][l], P["k_norm_g"][l], P["attn_sinks"][l], f"attn_fwd_{l}")
        y_att = rms_fwd(s["o"], P["attn_out_g"][l], BF16, f"rms_att_{l}")
        y_cv, s["xc"] = conv_fwd(proj, P["cv_dw_w"][l], P["cv_dw_b"][l], P["cv_ln_g"][l], P["cv_ln_b"][l],
                                 s["cv_pw"], P["cv_out_g"][l], f"conv_fwd_{l}")
        y_ret, s["states"] = ret_fwd(proj, P["ret_gn_g"][l], tb, f"ret_fwd_{l}")
        ycat = s["ycat"] = jnp.concatenate([y_att, y_cv.astype(BF16), y_ret.astype(BF16)], axis=1)
        h = mm_nn(f"out_proj_{l}", ycat, s["w_out"], 512, res=h)
        s["h1"] = h
        s["ffn_up"], s["ffn_down"] = hooks["ffn_w"](l, h)
        u2 = s["u2"] = rms_fwd(h, P["norm_ffn_g"][l], BF16, f"rms_ffn_{l}")
        s["f"] = ffn_up_fwd(f"ffn_up_{l}", u2, s["ffn_up"])
        a, s["gc"], s["uc"] = ffn_act_fwd(s["f"], P["ffn_dw_w"][l], P["ffn_dw_b"][l], f"ffn_act_fwd_{l}")
        s["a"] = a
        h = ffn_down_fwd(f"ffn_down_{l}", a, s["ffn_down"], h)
        stash.append(s)

    dh, dhm, loss = loss_head(h, target)
    G = {n: [None] * DEPTH for n in ("norm_mix_g", "q_norm_g", "k_norm_g", "attn_sinks", "attn_out_g", "cv_dw_w",
                                     "cv_dw_b", "cv_ln_g", "cv_ln_b", "cv_out_g", "ret_gn_g", "norm_ffn_g",
                                     "ffn_dw_w", "ffn_dw_b")}
    for l in reversed(range(DEPTH)):
        s = stash[l]
        da = ffn_down_dx(f"ffn_down_dx_{l}", dhm, s["ffn_down"])
        dw_down = ffn_down_dw(f"ffn_down_dw_{l}", s["a"], dhm)
        dfg, dfu, dwg, dwu, dbg, dbu = ffn_act_bwd(s["f"], s["gc"], s["uc"], da, P["ffn_dw_w"][l], f"ffn_act_bwd_{l}")
        G["ffn_dw_w"][l] = jnp.concatenate([dwg, dwu], axis=0)
        G["ffn_dw_b"][l] = jnp.concatenate([dbg, dbu], axis=0)
        du2 = ffn_up_dx(f"ffn_up_dx_{l}", dfg, dfu, s["ffn_up"])
        dw_up = ffn_up_dw(f"ffn_up_dw_{l}", s["u2"], dfg, dfu)
        after = hooks["ffn_grads"](l, dw_down, dw_up)
        dh, dhm, G["norm_ffn_g"][l] = rms_bwd(s["h1"], P["norm_ffn_g"][l], du2, dh, f"rms_ffn_bwd_{l}", after=after)
        dycat = mm_nt(f"out_proj_dx_{l}", dhm, s["w_out"], 512)
        dw_out = mm_ta(f"out_proj_dw_{l}", s["ycat"], dhm, 512)
        do, _, G["attn_out_g"][l] = rms_bwd(s["o"], P["attn_out_g"][l], dycat, None, f"rms_att_bwd_{l}")
        (dq, dk_c, dk_p, dk_m, dv_c, dv_p, dv_m, G["q_norm_g"][l], G["k_norm_g"][l], dsk) = attn_bwd(
            s["proj"], do, P["q_norm_g"][l], P["k_norm_g"][l], P["attn_sinks"][l], f"attn_bwd_{l}")
        G["attn_sinks"][l] = dsk[:, 0]
        shift = lambda z: jnp.concatenate([z[T:], jnp.zeros((T, NKV * HD), F32)], axis=0)
        dk = (dk_c + shift(dk_p)).at[:T].add(dk_m)
        dv = (dv_c + shift(dv_p)).at[:T].add(dv_m)
        (dca, dcb, G["cv_dw_w"][l], G["cv_dw_b"][l], G["cv_ln_g"][l], G["cv_ln_b"][l], G["cv_out_g"][l],
         dpw) = conv_bwd(s["proj"], s["xc"], dycat, P["cv_dw_w"][l], P["cv_ln_g"][l], P["cv_ln_b"][l], s["cv_pw"],
                         P["cv_out_g"][l], f"conv_bwd_{l}")
        dret, G["ret_gn_g"][l] = ret_bwd(s["proj"], dycat, s["states"], P["ret_gn_g"][l], tb, f"ret_bwd_{l}")
        dproj = jnp.concatenate([dq, dk, dv, dca, dcb, dret], axis=1).astype(BF16)
        du = mm_nn(f"in_proj_dx_{l}", dproj, s["w_in"], 512)
        dw_in = mm_ta(f"in_proj_dw_{l}", dproj, s["u"], 768)
        after = hooks["mixer_grads"](l, dw_out, dpw, dw_in)
        dh, dhm, G["norm_mix_g"][l] = rms_bwd(s["h"], P["norm_mix_g"][l], du, dh, f"rms_mix_bwd_{l}", after=after,
                                              last=(l == 0))
    return loss[0, 0], dh, dhm[PAD:T], G


_SMALL = ("meta", "norm_mix_g", "q_norm_g", "k_norm_g", "attn_sinks", "attn_out_g", "cv_dw_w", "cv_dw_b", "cv_ln_g",
          "cv_ln_b", "cv_out_g", "ret_gn_g", "norm_ffn_g", "ffn_dw_w", "ffn_dw_b")
_BIG = ("w_in", "cv_pw", "w_out", "ffn_up", "ffn_down")
_TRANSPOSED = ("w_in", "ffn_up")
_ORDER = ("meta", "norm_mix_g", "w_in", "q_norm_g", "k_norm_g", "attn_sinks", "attn_out_g", "cv_dw_w", "cv_dw_b",
          "cv_ln_g", "cv_ln_b", "cv_pw", "cv_out_g", "ret_gn_g", "w_out", "norm_ffn_g", "ffn_up", "ffn_dw_w",
          "ffn_dw_b", "ffn_down")
_SMALL_SHARDED = {"meta": D, "cv_dw_w": CV, "ffn_dw_w": 2 * D_FF}


def _pack(arrs):
    flat = jnp.concatenate([a.reshape(-1) for a in arrs])
    n = flat.shape[0]
    rows = -(-n // 1024) * 8
    return jnp.pad(flat, (0, rows * 128 - n)).reshape(rows, 128)


def _unpack(packed, shapes):
    flat = packed.reshape(-1)
    out, off = [], 0
    for s in shapes:
        n = int(np.prod(s))
        out.append(flat[off:off + n].reshape(s))
        off += n
    return out


def kernel(x, meta, norm_mix_g, w_in, q_norm_g, k_norm_g, attn_sinks, attn_out_g, cv_dw_w, cv_dw_b, cv_ln_g, cv_ln_b, cv_pw, cv_out_g, ret_gn_g, w_out, norm_ffn_g, ffn_up, ffn_dw_w, ffn_dw_b, ffn_down, loss_target, m_meta, m_norm_mix_g, m_w_in, m_q_norm_g, m_k_norm_g, m_attn_sinks, m_attn_out_g, m_cv_dw_w, m_cv_dw_b, m_cv_ln_g, m_cv_ln_b, m_cv_pw, m_cv_out_g, m_ret_gn_g, m_w_out, m_norm_ffn_g, m_ffn_up, m_ffn_dw_w, m_ffn_dw_b, m_ffn_down, v_meta, v_norm_mix_g, v_w_in, v_q_norm_g, v_k_norm_g, v_attn_sinks, v_attn_out_g, v_cv_dw_w, v_cv_dw_b, v_cv_ln_g, v_cv_ln_b, v_cv_pw, v_cv_out_g, v_ret_gn_g, v_w_out, v_norm_ffn_g, v_ffn_up, v_ffn_dw_w, v_ffn_dw_b, v_ffn_down):
    W = dict(meta=meta, norm_mix_g=norm_mix_g, w_in=w_in, q_norm_g=q_norm_g, k_norm_g=k_norm_g,
             attn_sinks=attn_sinks, attn_out_g=attn_out_g, cv_dw_w=cv_dw_w, cv_dw_b=cv_dw_b, cv_ln_g=cv_ln_g,
             cv_ln_b=cv_ln_b, cv_pw=cv_pw, cv_out_g=cv_out_g, ret_gn_g=ret_gn_g, w_out=w_out,
             norm_ffn_g=norm_ffn_g, ffn_up=ffn_up, ffn_dw_w=ffn_dw_w, ffn_dw_b=ffn_dw_b, ffn_down=ffn_down)
    M = dict(meta=m_meta, norm_mix_g=m_norm_mix_g, w_in=m_w_in, q_norm_g=m_q_norm_g, k_norm_g=m_k_norm_g,
             attn_sinks=m_attn_sinks, attn_out_g=m_attn_out_g, cv_dw_w=m_cv_dw_w, cv_dw_b=m_cv_dw_b,
             cv_ln_g=m_cv_ln_g, cv_ln_b=m_cv_ln_b, cv_pw=m_cv_pw, cv_out_g=m_cv_out_g, ret_gn_g=m_ret_gn_g,
             w_out=m_w_out, norm_ffn_g=m_norm_ffn_g, ffn_up=m_ffn_up, ffn_dw_w=m_ffn_dw_w, ffn_dw_b=m_ffn_dw_b,
             ffn_down=m_ffn_down)
    V = dict(meta=v_meta, norm_mix_g=v_norm_mix_g, w_in=v_w_in, q_norm_g=v_q_norm_g, k_norm_g=v_k_norm_g,
             attn_sinks=v_attn_sinks, attn_out_g=v_attn_out_g, cv_dw_w=v_cv_dw_w, cv_dw_b=v_cv_dw_b,
             cv_ln_g=v_cv_ln_g, cv_ln_b=v_cv_ln_b, cv_pw=v_cv_pw, cv_out_g=v_cv_out_g, ret_gn_g=v_ret_gn_g,
             w_out=v_w_out, norm_ffn_g=v_norm_ffn_g, ffn_up=v_ffn_up, ffn_dw_w=v_ffn_dw_w, ffn_dw_b=v_ffn_dw_b,
             ffn_down=v_ffn_down)
    me = 4 * lax.axis_index("x") + 2 * lax.axis_index("y") + lax.axis_index("c")
    for n in _TRANSPOSED:
        W[n], M[n], V[n] = (a.transpose(0, 2, 1) for a in (W[n], M[n], V[n]))

    sh = {n: [W[n][l].astype(BF16) for l in range(DEPTH)] for n in _BIG}
    mix = lambda l: [sh["w_in"][l], sh["cv_pw"][l], sh["w_out"][l]]
    ffn = lambda l: [sh["ffn_up"][l], sh["ffn_down"][l]]
    first = all_gather(mix(0) + [meta, cv_dw_w, ffn_dw_w], "gather_first")
    g_meta, g_cdw, g_fdw = first[3:6]

    def landing(shards):
        return [lax.dynamic_update_slice(lax.empty((NDEV,) + s.shape, s.dtype), s[None], (me,) + (0,) * s.ndim)
                for s in shards]

    gathers = {("ffn", 0): gather_start("gather_ffn0_start", ffn(0), landing(ffn(0)), after=[first[0]])}
    gathers["mix", 1] = gather_start("gather_mix1_start", mix(1), landing(mix(1)), after=[gathers["ffn", 0][4]])
    gathers["ffn", 1] = gather_start("gather_ffn1_start", ffn(1), landing(ffn(1)), after=[gathers["mix", 1][4]])

    def mixer_w(l, h):
        g_in, g_pw, g_out = first[0:3] if l == 0 else gather_wait(f"gather_mix{l}_wait", gathers["mix", l], after=[h])[1]
        return g_in.reshape(IN_W, D), g_pw.reshape(CV, CV), g_out.reshape(D, D)

    def ffn_w(l, h1):
        return gather_wait(f"gather_ffn{l}_wait", gathers["ffn", l], after=[h1])[1]

    scatters = {}

    def ffn_grads(l, dw_down, dw_up):
        scatters["ffn", l] = scatter_start(f"scatter_ffn{l}_start", [dw_up, dw_down.reshape(NDEV, DNR, D)])
        return scatters["ffn", l][4]

    def mixer_grads(l, dw_out, dpw, dw_in):
        grads = [dw_in.reshape(NDEV, IN_W // NDEV, D), dpw.astype(BF16).reshape(NDEV, CV // NDEV, CV),
                 dw_out.reshape(NDEV, D // NDEV, D)]
        scatters["mix", l] = scatter_start(f"scatter_mix{l}_start", grads)
        return scatters["mix", l][4]

    P = dict(
        meta=g_meta.transpose(1, 0, 2).reshape(N_META, D),
        cv_dw_w=g_cdw.transpose(1, 2, 0, 3).reshape(DEPTH, CONV_K, CV),
        ffn_dw_w=g_fdw.transpose(1, 0, 2, 3),
        ffn_dw_b=ffn_dw_b.reshape(DEPTH, NDEV, 1, UPW),
        attn_sinks=attn_sinks,
    )
    for n in ("norm_mix_g", "q_norm_g", "k_norm_g", "attn_out_g", "cv_dw_b", "cv_ln_g", "cv_ln_b", "cv_out_g",
              "ret_gn_g", "norm_ffn_g"):
        P[n] = W[n].reshape(DEPTH, 1, -1)

    hooks = dict(mixer_w=mixer_w, ffn_w=ffn_w, ffn_grads=ffn_grads, mixer_grads=mixer_grads,
                 first_after=gathers["ffn", 1][4])
    loss_part, dx, dmeta, G = local_step(x[0], loss_target[0], P, hooks)

    small_full = {
        "meta": dmeta,
        "cv_dw_w": jnp.stack(G["cv_dw_w"]),
        "ffn_dw_w": jnp.stack([g.transpose(1, 0, 2).reshape(FFN_K, 2 * D_FF) for g in G["ffn_dw_w"]]),
        "ffn_dw_b": jnp.stack([g.reshape(2 * D_FF) for g in G["ffn_dw_b"]]),
        "attn_sinks": jnp.stack(G["attn_sinks"]),
    }
    for n in _SMALL:
        if n not in small_full:
            small_full[n] = jnp.stack([g.reshape(-1) for g in G[n]])
    shapes = [small_full[n].shape for n in _SMALL] + [(1,)]
    packed = _pack([small_full[n] for n in _SMALL] + [loss_part.reshape(1)])
    small_exchange = gather_start("reduce_small_start", [packed], landing([packed]), after=[dx])

    out = {}
    tiles = {"w_in": 144, "cv_pw": 32, "w_out": 128, "ffn_up": 176, "ffn_down": 176}
    me1 = me.astype(jnp.int32).reshape(1)
    own, lands = {n: [None] * DEPTH for n in _BIG}, {n: [None] * DEPTH for n in _BIG}

    def arrived(kind, names, after):
        for l in range(DEPTH):
            srcs, got = scatter_wait(f"scatter_{kind}{l}_wait", scatters[kind, l], after=after)
            for n, s_, g_ in zip(names, srcs, got):
                own[n][l], lands[n][l] = s_, g_

    def update(names):
        for n in names:
            shard = W[n].shape
            rows, cols = shard[0] * shard[1], shard[2]
            res = adamw_big(own[n], lands[n], W[n].reshape(rows, cols), M[n].reshape(rows, cols),
                            V[n].reshape(rows, cols), me1, tiles[n], f"adamw_{n}")
            out[n] = [r.reshape(shard) for r in res]

    arrived("ffn", ("ffn_up", "ffn_down"), [small_exchange[4]])
    update(("ffn_up", "ffn_down"))
    slots = gather_wait("reduce_small_wait", small_exchange, after=[out["ffn_down"][0]])[1][0]
    summed = _unpack(sum_slots(slots, "reduce_small_sum"), shapes)
    loss = summed[-1][0]
    small_g = []
    for n, g in zip(_SMALL, summed):
        if n in _SMALL_SHARDED:
            width = _SMALL_SHARDED[n] // NDEV
            g = lax.dynamic_slice_in_dim(g, me * width, width, axis=g.ndim - 1)
        small_g.append(g)
    flat2 = lambda a: a.reshape(-1, a.shape[-1])
    res = adamw_many([flat2(g) for g in small_g], *[[flat2(X[n]) for n in _SMALL] for X in (W, M, V)], "adamw_small")
    for n, r in zip(_SMALL, res):
        out[n] = [a.reshape(W[n].shape) for a in r]
    arrived("mix", ("w_in", "cv_pw", "w_out"), [out["ffn_down"][0], res[0][0]])
    update(("w_in", "cv_pw", "w_out"))
    for n in _TRANSPOSED:
        out[n] = [r.transpose(0, 2, 1) for r in out[n]]

    return (loss, dx[None], *[out[n][0] for n in _ORDER], *[out[n][1] for n in _ORDER],
            *[out[n][2] for n in _ORDER], *[out[n][3] for n in _ORDER])
```

```python
import functools
import math

import numpy as np
import jax
import jax.numpy as jnp
from jax import lax
from jax.experimental import pallas as pl
from jax.experimental.pallas import tpu as pltpu

F32 = jnp.float32
BF16 = jnp.bfloat16

D = 1024
SEQ = 2048
DEPTH = 2
T = 128
L = SEQ + T
NB = L // T
N_META = 16
PAD = T - N_META
HD = 64
NQ = 8
NKV = 2
GQA = NQ // NKV
CV = 256
CONV_K = 31
RH = 4
D_FF = 2816
FFN_K = 3
IN_W = 2304
RMS_EPS = 1e-6
LN_EPS = 1e-5
NEG = -1e30
NDEV = 8
UPW = 2 * D_FF // NDEV
DNR = D_FF // NDEV
NPAIR = NDEV // 2

ADAM_LR, ADAM_B1, ADAM_B2, ADAM_EPS, ADAM_WD, ADAM_STEP = 0.001, 0.9, 0.999, 1e-08, 0.01, 10

VMEM_BIG = 56 * 1024 * 1024

MESH = pl.DeviceIdType.MESH


def _params(vmem=None):
    return pltpu.CompilerParams(vmem_limit_bytes=vmem) if vmem else None


def _dot(a, b, nt=False):
    return lax.dot_general(a, b, (((1,), (1 if nt else 0,)), ((), ())), preferred_element_type=F32)


def _sig(x):
    return 1.0 / (1.0 + jnp.exp(-x))


def _bf(x):
    return x.astype(BF16)


def _stack_rows(rows):
    idx = lax.broadcasted_iota(jnp.int32, (len(rows), rows[0].shape[1]), 0)
    out = jnp.zeros((len(rows), rows[0].shape[1]), F32)
    for r, v in enumerate(rows):
        out = jnp.where(idx == r, v, out)
    return out


def rms_fwd(x, g, out_dtype, name, after=None):
    n, w = x.shape

    def body(x_ref, g_ref, *rest):
        o_ref = rest[-1]
        xv = x_ref[...]
        r = lax.rsqrt(jnp.mean(xv * xv, axis=-1, keepdims=True) + RMS_EPS)
        o_ref[...] = (xv * r * g_ref[...]).astype(o_ref.dtype)

    deps = [] if after is None else [after]
    return pl.pallas_call(
        body, name=name, out_shape=jax.ShapeDtypeStruct((n, w), out_dtype), grid=(n // T,),
        in_specs=[pl.BlockSpec((T, w), lambda i: (i, 0)), pl.BlockSpec((1, w), lambda i: (0, 0))] + [_HBM] * len(deps),
        out_specs=pl.BlockSpec((T, w), lambda i: (i, 0)))(x, g, *deps)


def rms_bwd(x, g, dy, dres, name, after=None, dy_col=0, last=False):
    n, w = x.shape
    has_res = dres is not None
    deps = [] if after is None else [after]

    def body(x_ref, g_ref, dy_ref, *rest):
        rest = rest[:len(rest) - 3 - len(deps)] + rest[len(rest) - 3:]
        if has_res:
            dres_ref, dx_ref, dxm_ref, dg_ref = rest
        else:
            dx_ref, dxm_ref, dg_ref = rest
        i = pl.program_id(0)
        xv = x_ref[...]
        r = lax.rsqrt(jnp.mean(xv * xv, axis=-1, keepdims=True) + RMS_EPS)
        xh = xv * r
        dyv = dy_ref[...]
        dxh = dyv * g_ref[...]
        dx = r * (dxh - xh * jnp.mean(dxh * xh, axis=-1, keepdims=True))
        if has_res:
            dx = dx + dres_ref[...]
        dx_ref[...] = dx
        if last:
            @pl.when(i == 0)
            def _():
                dxm_ref[...] = dx
        else:
            rows = i * T + lax.broadcasted_iota(jnp.int32, (T, 1), 0)
            dxm_ref[...] = jnp.where(rows >= PAD, dx, 0.0).astype(BF16)
        part = jnp.sum(dyv * xh, axis=0, keepdims=True)

        @pl.when(i == 0)
        def _():
            dg_ref[...] = part

        @pl.when(i > 0)
        def _():
            dg_ref[...] += part

    row = pl.BlockSpec((T, w), lambda i: (i, 0))
    vec = pl.BlockSpec((1, w), lambda i: (0, 0))
    ins = [x, g, dy] + ([dres] if has_res else []) + deps
    if last:
        out_shape = (jax.ShapeDtypeStruct((n - T, w), F32), jax.ShapeDtypeStruct((T, w), F32))
        out_specs = (pl.BlockSpec((T, w), lambda i: (jnp.maximum(i - 1, 0), 0)), pl.BlockSpec((T, w), lambda i: (0, 0)))
    else:
        out_shape = (jax.ShapeDtypeStruct((n, w), F32), jax.ShapeDtypeStruct((n, w), BF16))
        out_specs = (row, row)
    return pl.pallas_call(
        body, name=name, out_shape=(*out_shape, jax.ShapeDtypeStruct((1, w), F32)), grid=(n // T,),
        in_specs=[row, vec, pl.BlockSpec((T, w), lambda i: (i, dy_col))] + ([row] if has_res else [])
        + [_HBM] * len(deps),
        out_specs=(*out_specs, vec))(*ins)


def loss_head(h, target):
    def body(h_ref, t_ref, dh_ref, dhm_ref, loss_ref):
        n = pl.program_id(0)
        e = jnp.where(n > 0, h_ref[...] - t_ref[...], 0.0)
        dh = e * (1.0 / D)
        dh_ref[...] = dh
        dhm_ref[...] = dh.astype(BF16)
        part = jnp.sum(jnp.sum(e * e, axis=1, keepdims=True), axis=0, keepdims=True) * (0.5 / D)

        @pl.when(n == 0)
        def _():
            loss_ref[...] = jnp.zeros_like(loss_ref)

        @pl.when(n > 0)
        def _():
            loss_ref[...] += jnp.broadcast_to(part, loss_ref.shape)

    row = pl.BlockSpec((T, D), lambda n: (n, 0))
    return pl.pallas_call(
        body, name="loss_head",
        out_shape=(jax.ShapeDtypeStruct((L, D), F32), jax.ShapeDtypeStruct((L, D), BF16),
                   jax.ShapeDtypeStruct((8, 128), F32)),
        grid=(NB,),
        in_specs=[row, pl.BlockSpec((T, D), lambda n: (jnp.maximum(n - 1, 0), 0))],
        out_specs=(row, row, pl.BlockSpec((8, 128), lambda n: (0, 0))))(h, target)


def _mm(name, a, b, *, grid, a_spec, b_spec, o_spec, out_shape, nt=False, ta=False, red=False, res=None,
        res_spec=None):
    def body(a_ref, b_ref, *rest):
        o_ref = rest[-1]
        av = a_ref[...]
        bv = b_ref[...]
        if bv.ndim == 3:
            bv = bv.reshape(bv.shape[0] * bv.shape[1], bv.shape[2])
        if ta:
            acc = lax.dot_general(av, bv, (((0,), (0,)), ((), ())), preferred_element_type=F32)
        else:
            acc = _dot(av, bv, nt)
        if red:
            k = pl.program_id(0)

            @pl.when(k == 0)
            def _():
                o_ref[...] = acc

            @pl.when(k > 0)
            def _():
                o_ref[...] += acc
        else:
            if res is not None:
                rows = lax.broadcasted_iota(jnp.int32, (acc.shape[0], 1), 0)
                acc = rest[0][...] + jnp.where(rows >= PAD, acc, 0.0)
            o_ref[...] = acc.astype(o_ref.dtype)

    ins = [a, b] + ([res] if res is not None else [])
    specs = [a_spec, b_spec] + ([res_spec] if res is not None else [])
    return pl.pallas_call(body, name=name, out_shape=out_shape, grid=grid, in_specs=specs, out_specs=o_spec,
                          compiler_params=_params(VMEM_BIG))(*ins)


def mm_nn(name, a, b, tn, out_dtype=F32, res=None):
    m, k = a.shape
    n = b.shape[1]
    return _mm(name, a, b, grid=(n // tn,),
               a_spec=pl.BlockSpec((m, k), lambda j: (0, 0)), b_spec=pl.BlockSpec((k, tn), lambda j: (0, j)),
               o_spec=pl.BlockSpec((m, tn), lambda j: (0, j)), out_shape=jax.ShapeDtypeStruct((m, n), out_dtype),
               res=res, res_spec=pl.BlockSpec((m, tn), lambda j: (0, j)))


def mm_nt(name, a, b, tn):
    m, k = a.shape
    n = b.shape[0]
    return _mm(name, a, b, grid=(n // tn,), nt=True,
               a_spec=pl.BlockSpec((m, k), lambda j: (0, 0)), b_spec=pl.BlockSpec((tn, k), lambda j: (j, 0)),
               o_spec=pl.BlockSpec((m, tn), lambda j: (0, j)), out_shape=jax.ShapeDtypeStruct((m, n), F32))


def mm_ta(name, a, b, tm, out_dtype=BF16):
    k, m = a.shape
    n = b.shape[1]
    return _mm(name, a, b, grid=(m // tm,), ta=True,
               a_spec=pl.BlockSpec((k, tm), lambda j: (0, j)), b_spec=pl.BlockSpec((k, n), lambda j: (0, 0)),
               o_spec=pl.BlockSpec((tm, n), lambda j: (j, 0)), out_shape=jax.ShapeDtypeStruct((m, n), out_dtype))


def ffn_up_fwd(name, u2, wupt):
    return _mm(name, u2, wupt, grid=(NDEV,), nt=True,
               a_spec=pl.BlockSpec((L, D), lambda j: (0, 0)),
               b_spec=pl.BlockSpec((None, UPW, D), lambda j: (j, 0, 0)),
               o_spec=pl.BlockSpec((None, L, UPW), lambda j: (j, 0, 0)),
               out_shape=jax.ShapeDtypeStruct((NDEV, L, UPW), F32))


def _slab_specs():
    gate = pl.BlockSpec((None, L, UPW), lambda j: (jnp.minimum(j, NPAIR - 1), 0, 0))
    up = pl.BlockSpec((None, L, UPW), lambda j: (jnp.maximum(j - NPAIR, 0), 0, 0))
    return gate, up


def ffn_up_dx(name, dfg, dfu, wupt):
    def body(g_ref, u_ref, b_ref, o_ref):
        j = pl.program_id(0)

        @pl.when(j == 0)
        def _():
            o_ref[...] = _dot(g_ref[...], b_ref[...])

        @pl.when(jnp.logical_and(j > 0, j < NPAIR))
        def _():
            o_ref[...] += _dot(g_ref[...], b_ref[...])

        @pl.when(j >= NPAIR)
        def _():
            o_ref[...] += _dot(u_ref[...], b_ref[...])

    gate, up = _slab_specs()
    return pl.pallas_call(
        body, name=name, out_shape=jax.ShapeDtypeStruct((L, D), F32), grid=(NDEV,),
        in_specs=[gate, up, pl.BlockSpec((None, UPW, D), lambda j: (j, 0, 0))],
        out_specs=pl.BlockSpec((L, D), lambda j: (0, 0)), compiler_params=_params(VMEM_BIG))(dfg, dfu, wupt)


def ffn_up_dw(name, u2, dfg, dfu):
    tdot = lambda a, b: lax.dot_general(a, b, (((0,), (0,)), ((), ())), preferred_element_type=F32)

    def body(a_ref, g_ref, u_ref, o_ref):
        j = pl.program_id(0)

        @pl.when(j < NPAIR)
        def _():
            o_ref[...] = tdot(g_ref[...], a_ref[...]).astype(o_ref.dtype)

        @pl.when(j >= NPAIR)
        def _():
            o_ref[...] = tdot(u_ref[...], a_ref[...]).astype(o_ref.dtype)

    gate, up = _slab_specs()
    return pl.pallas_call(
        body, name=name, out_shape=jax.ShapeDtypeStruct((NDEV, UPW, D), BF16), grid=(NDEV,),
        in_specs=[pl.BlockSpec((L, D), lambda j: (0, 0)), gate, up],
        out_specs=pl.BlockSpec((None, UPW, D), lambda j: (j, 0, 0)), compiler_params=_params(VMEM_BIG))(u2, dfg, dfu)


def ffn_down_fwd(name, a, wdn, res, tn=256):
    def body(a_ref, b_ref, r_ref, o_ref):
        acc = jnp.zeros((L, tn), F32)
        for g in range(NPAIR):
            bv = b_ref[2 * g:2 * g + 2]
            acc = acc + _dot(a_ref[g], bv.reshape(2 * DNR, tn))
        rows = lax.broadcasted_iota(jnp.int32, (L, 1), 0)
        o_ref[...] = r_ref[...] + jnp.where(rows >= PAD, acc, 0.0)

    return pl.pallas_call(
        body, name=name, out_shape=jax.ShapeDtypeStruct((L, D), F32), grid=(D // tn,),
        in_specs=[pl.BlockSpec((NPAIR, L, UPW), lambda j: (0, 0, 0)),
                  pl.BlockSpec((NDEV, DNR, tn), lambda j: (0, 0, j)),
                  pl.BlockSpec((L, tn), lambda j: (0, j))],
        out_specs=pl.BlockSpec((L, tn), lambda j: (0, j)),
        compiler_params=_params(VMEM_BIG))(a, wdn, res)


def ffn_down_dx(name, dh, wdn):
    return _mm(name, dh, wdn, grid=(NPAIR,), nt=True,
               a_spec=pl.BlockSpec((L, D), lambda g: (0, 0)),
               b_spec=pl.BlockSpec((2, DNR, D), lambda g: (g, 0, 0)),
               o_spec=pl.BlockSpec((None, L, UPW), lambda g: (g, 0, 0)),
               out_shape=jax.ShapeDtypeStruct((NPAIR, L, UPW), F32))


def ffn_down_dw(name, a, dh):
    return _mm(name, a, dh, grid=(NPAIR,), ta=True,
               a_spec=pl.BlockSpec((None, L, UPW), lambda g: (g, 0, 0)),
               b_spec=pl.BlockSpec((L, D), lambda g: (0, 0)),
               o_spec=pl.BlockSpec((UPW, D), lambda g: (g, 0)),
               out_shape=jax.ShapeDtypeStruct((D_FF, D), BF16))


_SLOPES = [2.0 ** (-8.0 * (h + 1) / NQ) for h in range(NQ)]
_SCALE = HD ** -0.5


QR = GQA * T
KC = 3 * T


N_BIAS = 3


def attn_bias_table():
    row = lax.broadcasted_iota(jnp.int32, (QR, KC), 0)
    col = lax.broadcasted_iota(jnp.int32, (QR, KC), 1)
    i = row & (T - 1)
    j = col & (T - 1)
    blk = col // T
    head = row // T
    never = jnp.zeros((QR, KC), jnp.bool_)
    tables = []
    for n in range(N_BIAS):
        d_meta = n * T + i - j
        ok_meta = (j >= PAD) & (d_meta >= 0)
        ok_prev = (j > i) if n >= 2 else never
        ok_cur = (j <= i) if n >= 1 else never
        ok = ((blk == 0) & ok_meta) | ((blk == 1) & ok_prev) | ((blk == 2) & ok_cur)
        dist = jnp.where(blk == 0, jnp.minimum(d_meta, T), jnp.where(blk == 1, T + i - j, i - j)).astype(F32)
        for g in range(NKV):
            slope = jnp.zeros((QR, KC), F32)
            for hh in range(GQA):
                slope = jnp.where(head == hh, _SLOPES[g * GQA + hh], slope)
            tables.append(jnp.where(ok, -slope * dist, NEG))
    return jnp.stack(tables)


def _bias_spec():
    return pl.BlockSpec((None, QR, KC), lambda n, g: (jnp.minimum(n, N_BIAS - 1) * NKV + g, 0, 0))


def _sink_rows(g, sink_ref):
    head = lax.broadcasted_iota(jnp.int32, (QR, 1), 0) // T
    sink = jnp.zeros((QR, 1), F32)
    for hh in range(GQA):
        sink = jnp.where(head == hh, sink_ref[g * GQA + hh], sink)
    return sink


def _attn_probs(qn, keys, bias, sink):
    s = _dot(qn, keys, nt=True) * _SCALE + bias
    m = jnp.maximum(jnp.max(s, axis=-1, keepdims=True), sink)
    e = jnp.exp(s - m)
    e_sink = jnp.exp(sink - m)
    inv = 1.0 / (e.sum(axis=-1, keepdims=True) + e_sink)
    return e, inv, e_sink * inv


def _rms_rows(x, g):
    r = lax.rsqrt(jnp.mean(x * x, axis=-1, keepdims=True) + RMS_EPS)
    xh = x * r
    return xh * g, xh, r


def _rms_rows_bwd(dy, g, xh, r):
    dxh = dy * g
    return r * (dxh - xh * jnp.mean(dxh * xh, axis=-1, keepdims=True))


def _rows3(m_ref, p_ref, c_ref):
    return jnp.concatenate([m_ref[...], p_ref[...], c_ref[...]], axis=0)


_ATT_K, _ATT_V = 4, 5


def _nat_specs():
    qspec = pl.BlockSpec((T, GQA * HD), lambda n, g: (n, g))
    kv = lambda col: (pl.BlockSpec((T, 2 * HD), lambda n, g: (n, col)),
                      pl.BlockSpec((T, 2 * HD), lambda n, g: (jnp.maximum(n - 1, 0), col)),
                      pl.BlockSpec((T, 2 * HD), lambda n, g: (0, col)))
    return qspec, kv(_ATT_K), kv(_ATT_V)


def _group_lanes(g, x):
    return jnp.where(g == 0, x[:, :HD], x[:, HD:])


def _stack_heads(x):
    return jnp.concatenate([x[:, hh * HD:(hh + 1) * HD] for hh in range(GQA)], axis=0)


def _unstack_heads(x):
    return jnp.concatenate([x[hh * T:(hh + 1) * T] for hh in range(GQA)], axis=1)


def attn_fwd(proj, bias, qg, kg, sinks, name):
    def body(sink_ref, q_ref, kc_ref, kp_ref, km_ref, vc_ref, vp_ref, vm_ref, b_ref, qg_ref, kg_ref, o_ref):
        g = pl.program_id(1)
        keys = _bf(_rms_rows(_group_lanes(g, _rows3(km_ref, kp_ref, kc_ref)), kg_ref[...])[0])
        vals = _bf(_group_lanes(g, _rows3(vm_ref, vp_ref, vc_ref)))
        qn = _bf(_rms_rows(_stack_heads(q_ref[...]), qg_ref[...])[0])
        e, inv, _ = _attn_probs(qn, keys, b_ref[...], _sink_rows(g, sink_ref))
        o_ref[...] = _unstack_heads(_dot(_bf(e), vals) * inv)

    qspec, (kc, kp, km), (vc, vp, vm) = _nat_specs()
    vec = pl.BlockSpec((1, HD), lambda n, g: (0, 0))
    return pl.pallas_call(
        body, name=name, out_shape=jax.ShapeDtypeStruct((L, NQ * HD), F32), grid=(NB, NKV),
        in_specs=[pl.BlockSpec(memory_space=pltpu.SMEM), qspec, kc, kp, km, vc, vp, vm, _bias_spec(), vec, vec],
        out_specs=qspec)(sinks, proj, proj, proj, proj, proj, proj, proj, bias, qg, kg)


def attn_bwd(proj, do, bias, qg, kg, sinks, name):
    def body(sink_ref, q_ref, kc_ref, kp_ref, km_ref, vc_ref, vp_ref, vm_ref, do_ref, b_ref, qg_ref, kg_ref,
             dq_ref, dkc_ref, dkp_ref, dkm_ref, dvc_ref, dvp_ref, dvm_ref, dqg_ref, dkg_ref, dsk_ref):
        n = pl.program_id(0)
        g = pl.program_id(1)

        @pl.when(jnp.logical_and(g == 0, n == 0))
        def _():
            for r in (dkm_ref, dvm_ref, dqg_ref, dkg_ref, dsk_ref):
                r[...] = jnp.zeros_like(r)

        kgv = kg_ref[...]
        qgv = qg_ref[...]
        kn_f, kh, kr = _rms_rows(_group_lanes(g, _rows3(km_ref, kp_ref, kc_ref)), kgv)
        keys = _bf(kn_f)
        vals = _bf(_group_lanes(g, _rows3(vm_ref, vp_ref, vc_ref)))
        qn_f, qh, qr = _rms_rows(_stack_heads(q_ref[...]), qgv)
        qn = _bf(qn_f)
        e, inv, p_sink = _attn_probs(qn, keys, b_ref[...], _sink_rows(g, sink_ref))
        dob = _bf(_stack_heads(do_ref[...]))
        p = e * inv
        dp = _dot(dob, vals, nt=True)
        delta = (p * dp).sum(axis=-1, keepdims=True)
        head_row = lax.broadcasted_iota(jnp.int32, (NQ, 128), 0)
        dsk = jnp.zeros((NQ, 128), F32)
        for hh in range(GQA):
            part = -jnp.sum((p_sink * delta)[hh * T:(hh + 1) * T], axis=0, keepdims=True)
            dsk = jnp.where(head_row == g * GQA + hh, part, dsk)
        dsk_ref[...] += dsk
        ds = p * (dp - delta)
        dqn = _dot(_bf(ds), keys) * _SCALE
        dq_ref[...] = _unstack_heads(_rms_rows_bwd(dqn, qgv, qh, qr))
        dqg_ref[...] += jnp.sum(dqn * qh, axis=0, keepdims=True)
        dkn = _dot(_bf(ds.T), qn) * _SCALE
        dkg_ref[...] += jnp.sum(dkn * kh, axis=0, keepdims=True)
        dk_all = _rms_rows_bwd(dkn, kgv, kh, kr)
        dv_all = _dot(_bf(p.T), dob)

        for gg in range(NKV):
            @pl.when(g == gg)
            def _(gg=gg):
                lanes = slice(gg * HD, (gg + 1) * HD)
                dkm_ref[:, lanes] += dk_all[0:T]
                dkp_ref[:, lanes] = dk_all[T:2 * T]
                dkc_ref[:, lanes] = dk_all[2 * T:3 * T]
                dvm_ref[:, lanes] += dv_all[0:T]
                dvp_ref[:, lanes] = dv_all[T:2 * T]
                dvc_ref[:, lanes] = dv_all[2 * T:3 * T]

    qspec, (kc, kp, km), (vc, vp, vm) = _nat_specs()
    vec = pl.BlockSpec((1, HD), lambda n, g: (0, 0))
    cur = pl.BlockSpec((T, 2 * HD), lambda n, g: (n, 0))
    meta = pl.BlockSpec((T, 2 * HD), lambda n, g: (0, 0))
    kv_shape = jax.ShapeDtypeStruct((L, 2 * HD), F32)
    meta_shape = jax.ShapeDtypeStruct((T, 2 * HD), F32)
    vec_shape = jax.ShapeDtypeStruct((1, HD), F32)
    return pl.pallas_call(
        body, name=name,
        out_shape=(jax.ShapeDtypeStruct((L, NQ * HD), F32), kv_shape, kv_shape, meta_shape, kv_shape, kv_shape,
                   meta_shape, vec_shape, vec_shape, jax.ShapeDtypeStruct((NQ, 128), F32)),
        grid=(NB, NKV),
        in_specs=[pl.BlockSpec(memory_space=pltpu.SMEM), qspec, kc, kp, km, vc, vp, vm, qspec, _bias_spec(), vec, vec],
        out_specs=(qspec, cur, cur, meta, cur, cur, meta, vec, vec,
                   pl.BlockSpec((NQ, 128), lambda n, g: (0, 0))))(
            sinks, proj, proj, proj, proj, proj, proj, proj, do, bias, qg, kg)


RW = RH * HD


def _ret_tables():
    h = np.arange(RH, dtype=np.float64)
    lg = np.log1p(-np.exp2(-5.0 - h))
    idx = np.arange(T, dtype=np.float64)
    diff = idx[:, None] - idx[None, :]
    decay = np.where(diff[None] >= 0, np.exp(np.maximum(diff, 0.0)[None] * lg[:, None, None]), 0.0)
    zeta = np.exp((T - 1 - idx)[None, :] * lg[:, None])
    xi = np.exp((idx + 1.0)[None, :] * lg[:, None])
    cd = np.exp(T * lg)
    lanes = lambda a: np.repeat(a.T, HD, axis=1)
    head_of = np.arange(RW) // HD
    same = (head_of[:, None] == head_of[None, :]).astype(np.float64)
    f = lambda a: jnp.asarray(a, F32)
    return dict(decay=f(decay), zeta=f(lanes(zeta)), xi=f(lanes(xi)), cd=f(np.repeat(cd, HD)[None, :]),
                head=f((head_of[None, :] == np.arange(RH)[:, None]).astype(np.float64)[:, None, :]),
                same=f(same), avg=jnp.asarray(same / HD, BF16))


def _seg_mean(x, avg):
    hi = _bf(x)
    lo = _bf(x - hi.astype(F32))
    return _dot(hi, avg) + _dot(lo, avg)


def _ret_specs(col0, order):
    return lambda col: pl.BlockSpec((T, RW), lambda i: (order(i), col0 + col))


def _ret_chunk(q, kf, v, s, tb):
    dec, xi, head = tb
    vb = _bf(v)
    kb = _bf(kf)
    y = _dot(_bf(q * xi), _bf(s))
    a = []
    for h in range(RH):
        a.append(_dot(_bf(q * head[h]), kb, nt=True) * dec[h])
        y = y + head[h] * _dot(_bf(a[h]), vb)
    return a, y


def _gn_rows(y):
    mu = jnp.mean(y, axis=-1, keepdims=True)
    yc = y - mu
    rstd = lax.rsqrt(jnp.mean(yc * yc, axis=-1, keepdims=True) + LN_EPS)
    return yc * rstd, rstd


_RET_COL0 = _RET_Q = 5


def _ret_consts(tb):
    names = ("decay", "zeta", "xi", "cd", "head", "same", "avg")
    full = lambda a: pl.BlockSpec(a.shape, lambda i: (0,) * a.ndim)
    return [tb[n] for n in names], [full(tb[n]) for n in names]


def _ret_inputs(n, q_ref, k_ref, v_ref):
    rows = n * T + lax.broadcasted_iota(jnp.int32, (T, 1), 0)
    valid = rows >= PAD
    return q_ref[...], jnp.where(valid, k_ref[...] * (HD ** -0.5), 0.0), v_ref[...], valid


def ret_fwd(proj, gng, tb, name):
    def body(q_ref, k_ref, v_ref, g_ref, dec_ref, zeta_ref, xi_ref, cd_ref, head_ref, same_ref, avg_ref, gng_ref,
             y_ref, st_ref, s_scr):
        n = pl.program_id(0)

        @pl.when(n == 0)
        def _():
            s_scr[...] = jnp.zeros_like(s_scr)

        s = s_scr[...]
        st_ref[...] = s
        q, kf, v, _ = _ret_inputs(n, q_ref, k_ref, v_ref)
        _, y = _ret_chunk(q, kf, v, s, (dec_ref, xi_ref[...], head_ref))
        s_scr[...] = cd_ref[...] * s + same_ref[...] * _dot(_bf((kf * zeta_ref[...]).T), _bf(v))
        avg = avg_ref[...]
        yc = y - _seg_mean(y, avg)
        yh = yc * lax.rsqrt(_seg_mean(yc * yc, avg) + LN_EPS)
        gv = g_ref[...]
        y_ref[...] = gv * _sig(gv) * (yh * gng_ref[...])

    col = _ret_specs(_RET_COL0, lambda i: i)
    consts, cspecs = _ret_consts(tb)
    return pl.pallas_call(
        body, name=name,
        out_shape=(jax.ShapeDtypeStruct((L, RW), F32), jax.ShapeDtypeStruct((NB, RW, RW), F32)),
        grid=(NB,),
        in_specs=[col(0), col(1), col(2), col(3)] + cspecs + [pl.BlockSpec((1, RW), lambda i: (0, 0))],
        out_specs=(pl.BlockSpec((T, RW), lambda i: (i, 0)), pl.BlockSpec((None, RW, RW), lambda i: (i, 0, 0))),
        scratch_shapes=[pltpu.VMEM((RW, RW), F32)])(proj, proj, proj, proj, *consts, gng)


def ret_bwd(proj, dycat, states, gng, tb, name):
    def body(q_ref, k_ref, v_ref, g_ref, do_ref, st_ref, dec_ref, zeta_ref, xi_ref, cd_ref, head_ref, same_ref,
             avg_ref, gng_ref, d_ref, dgn_ref, ds_scr):
        i = pl.program_id(0)
        n = NB - 1 - i

        @pl.when(i == 0)
        def _():
            ds_scr[...] = jnp.zeros_like(ds_scr)

        dsn = _bf(ds_scr[...])
        s = st_ref[...]
        sb = _bf(s)
        q, kf, v, valid = _ret_inputs(n, q_ref, k_ref, v_ref)
        xi, zeta, avg = xi_ref[...], zeta_ref[...], avg_ref[...]
        a, y = _ret_chunk(q, kf, v, s, (dec_ref, xi, head_ref))
        yc = y - _seg_mean(y, avg)
        rstd = lax.rsqrt(_seg_mean(yc * yc, avg) + LN_EPS)
        yh = yc * rstd
        gv = g_ref[...]
        sg = _sig(gv)
        sil = gv * sg
        gn = gng_ref[...]
        dout = do_ref[...]
        d_ref[:, 3 * RW:4 * RW] = dout * (yh * gn) * (sg * (1.0 + gv * (1.0 - sg)))
        dyh = dout * sil * gn
        part = jnp.sum(dout * sil * yh, axis=0, keepdims=True)

        @pl.when(i == 0)
        def _():
            dgn_ref[...] = part

        @pl.when(i > 0)
        def _():
            dgn_ref[...] += part

        dy = rstd * (dyh - _seg_mean(dyh, avg) - yh * _seg_mean(dyh * yh, avg))
        dyb = _bf(dy)
        vb, kb, qb = _bf(v), _bf(kf), _bf(q)
        dq = _dot(dyb, sb, nt=True) * xi
        dkf = _dot(vb, dsn, nt=True) * zeta
        dv = _dot(_bf(kf * zeta), dsn)
        for h in range(RH):
            m = head_ref[h]
            da = _dot(_bf(dy * m), vb, nt=True) * dec_ref[h]
            dv = dv + m * _dot(_bf(a[h].T), dyb)
            dq = dq + m * _dot(_bf(da), kb)
            dkf = dkf + m * _dot(_bf(da.T), qb)
        d_ref[:, 0:RW] = dq
        d_ref[:, RW:2 * RW] = jnp.where(valid, dkf * (HD ** -0.5), 0.0)
        d_ref[:, 2 * RW:3 * RW] = dv
        ds_scr[...] = cd_ref[...] * ds_scr[...] + same_ref[...] * _dot(_bf((q * xi).T), dyb)

    back = lambda i: NB - 1 - i
    col = _ret_specs(_RET_COL0, back)
    consts, cspecs = _ret_consts(tb)
    vec = pl.BlockSpec((1, RW), lambda i: (0, 0))
    return pl.pallas_call(
        body, name=name,
        out_shape=(jax.ShapeDtypeStruct((L, 4 * RW), F32), jax.ShapeDtypeStruct((1, RW), F32)),
        grid=(NB,),
        in_specs=[col(0), col(1), col(2), col(3), _ret_specs(3, back)(0),
                  pl.BlockSpec((None, RW, RW), lambda i: (back(i), 0, 0))] + cspecs + [vec],
        out_specs=(pl.BlockSpec((T, 4 * RW), lambda i: (back(i), 0)), vec),
        scratch_shapes=[pltpu.VMEM((RW, RW), F32)])(proj, proj, proj, proj, dycat, states, *consts, gng)


HALO = 32
TAP0 = HALO - (CONV_K - 1)


def _conv_specs():
    cur = lambda col=0: pl.BlockSpec((T, CV), lambda c: (c, col))
    before = lambda col=0: pl.BlockSpec((HALO, CV), lambda c: (jnp.maximum(c * (T // HALO) - 1, 0), col))
    after = pl.BlockSpec((HALO, CV), lambda c: (jnp.minimum((c + 1) * (T // HALO), L // HALO - 1), 0))
    full = lambda r, w: pl.BlockSpec((r, w), lambda c: (0, 0))
    return cur, before, after, full


_CONV_A, _CONV_B = 3, 4
_DY_CONV = 2


def _conv_post(xc, lg_ref, lb_ref):
    xh, rstd = _gn_rows(xc)
    z = xh * lg_ref[...] + lb_ref[...]
    return xh, rstd, z, _sig(z)


def conv_fwd(proj, w, b, lg, lb, pw, og, name):
    def body(ca_ref, cah_ref, cb_ref, cbh_ref, w_ref, b_ref, lg_ref, lb_ref, pw_ref, og_ref, y_ref, xc_ref, u_scr):
        c = pl.program_id(0)
        u_scr[0:HALO, :] = jnp.where(c > 0, cah_ref[...] * _sig(cbh_ref[...]), 0.0)
        u_scr[HALO:HALO + T, :] = ca_ref[...] * _sig(cb_ref[...])
        acc = jnp.zeros((T, CV), F32)
        for k in range(CONV_K):
            acc = acc + w_ref[k:k + 1, :] * u_scr[TAP0 + k:TAP0 + k + T, :]
        xc = acc + b_ref[...]
        xc_ref[...] = xc
        _, _, z, sg = _conv_post(xc, lg_ref, lb_ref)
        zp = _dot(_bf(z * sg), pw_ref[...])
        y_ref[...] = _rms_rows(zp, og_ref[...])[0]

    cur, before, _, full = _conv_specs()
    vec = full(1, CV)
    seq = jax.ShapeDtypeStruct((L, CV), F32)
    return pl.pallas_call(
        body, name=name, out_shape=(seq, seq), grid=(NB,),
        in_specs=[cur(_CONV_A), before(_CONV_A), cur(_CONV_B), before(_CONV_B), full(CONV_K, CV), vec, vec, vec,
                  full(CV, CV), vec],
        out_specs=(cur(), cur()),
        scratch_shapes=[pltpu.VMEM((HALO + T, CV), F32)])(proj, proj, proj, proj, w, b, lg, lb, pw, og)


def conv_bwd(proj, xc, dycat, w, lg, lb, pw, og, name):
    def body1(xc_ref, dy_ref, lg_ref, lb_ref, pw_ref, og_ref, dxc_ref, db_ref, dlg_ref, dlb_ref, dog_ref, dpw_ref):
        c = pl.program_id(0)

        @pl.when(c == 0)
        def _():
            for r in (db_ref, dlg_ref, dlb_ref, dog_ref, dpw_ref):
                r[...] = jnp.zeros_like(r)

        xh, rstd, z, sg = _conv_post(xc_ref[...], lg_ref, lb_ref)
        s = z * sg
        zp = _dot(_bf(s), pw_ref[...])
        ogv = og_ref[...]
        _, zph, r2 = _rms_rows(zp, ogv)
        dyv = dy_ref[...]
        dog_ref[...] += jnp.sum(dyv * zph, axis=0, keepdims=True)
        dzpb = _bf(_rms_rows_bwd(dyv, ogv, zph, r2))
        dpw_ref[...] += _dot(_bf(s.T), dzpb)
        dz = _dot(dzpb, pw_ref[...], nt=True) * (sg * (1.0 + z * (1.0 - sg)))
        dlg_ref[...] += jnp.sum(dz * xh, axis=0, keepdims=True)
        dlb_ref[...] += jnp.sum(dz, axis=0, keepdims=True)
        dxh = dz * lg_ref[...]
        dxc = rstd * (dxh - jnp.mean(dxh, axis=-1, keepdims=True) - xh * jnp.mean(dxh * xh, axis=-1, keepdims=True))
        db_ref[...] += jnp.sum(dxc, axis=0, keepdims=True)
        dxc_ref[...] = dxc

    def body2(dx_ref, dxa_ref, ca_ref, cb_ref, w_ref, dca_ref, dcb_ref, dw_ref, d_scr):
        c = pl.program_id(0)

        @pl.when(c == 0)
        def _():
            dw_ref[...] = jnp.zeros_like(dw_ref)

        d_scr[0:T, :] = dx_ref[...]
        d_scr[T:T + HALO, :] = jnp.where(c < NB - 1, dxa_ref[...], 0.0)
        ca = ca_ref[...]
        sg = _sig(cb_ref[...])
        u = ca * sg
        du = jnp.zeros((T, CV), F32)
        for k in range(CONV_K):
            off = CONV_K - 1 - k
            dj = d_scr[off:off + T, :]
            du = du + w_ref[k:k + 1, :] * dj
            dw_ref[k:k + 1, :] += jnp.sum(u * dj, axis=0, keepdims=True)
        dca_ref[...] = du * sg
        dcb_ref[...] = du * ca * sg * (1.0 - sg)

    cur, _, after, full = _conv_specs()
    seq = jax.ShapeDtypeStruct((L, CV), F32)
    vsh = jax.ShapeDtypeStruct((1, CV), F32)
    vec = full(1, CV)
    dxc, db, dlg, dlb, dog, dpw = pl.pallas_call(
        body1, name=name + "_a",
        out_shape=(seq, vsh, vsh, vsh, vsh, jax.ShapeDtypeStruct((CV, CV), F32)),
        grid=(NB,),
        in_specs=[cur(), cur(_DY_CONV), vec, vec, full(CV, CV), vec],
        out_specs=(cur(), vec, vec, vec, vec, full(CV, CV)))(xc, dycat, lg, lb, pw, og)
    dca, dcb, dw = pl.pallas_call(
        body2, name=name + "_b", out_shape=(seq, seq, jax.ShapeDtypeStruct((CONV_K, CV), F32)), grid=(NB,),
        in_specs=[cur(), after, cur(_CONV_A), cur(_CONV_B), full(CONV_K, CV)],
        out_specs=(cur(), cur(), full(CONV_K, CV)),
        scratch_shapes=[pltpu.VMEM((T + HALO, CV), F32)])(dxc, dxc, proj, proj, w)
    return dca, dcb, dw, db, dlg, dlb, dog, dpw


HR = 8


def _ffn_specs():
    cur = lambda off: pl.BlockSpec((None, T, UPW), lambda j, c: (j + off, c, 0))
    before = lambda off: pl.BlockSpec((None, HR, UPW), lambda j, c: (j + off, jnp.maximum(c * (T // HR) - 1, 0), 0))
    after = lambda off: pl.BlockSpec(
        (None, HR, UPW), lambda j, c: (j + off, jnp.minimum((c + 1) * (T // HR), L // HR - 1), 0))
    wspec = lambda off, r: pl.BlockSpec((None, r, UPW), lambda j, c: (j + off, 0, 0))
    return cur, before, after, wspec


def ffn_act_fwd(f, w, b, name):
    strip = 16

    def body(fg_ref, fgh_ref, fu_ref, fuh_ref, wg_ref, wu_ref, bg_ref, bu_ref, a_ref, gc_ref, uc_ref, hg_scr, hu_scr):
        c = pl.program_id(1)
        for scr, x_ref, xh_ref in ((hg_scr, fg_ref, fgh_ref), (hu_scr, fu_ref, fuh_ref)):
            scr[0:HR, :] = jnp.where(c > 0, xh_ref[...], 0.0)
            scr[HR:HR + strip, :] = x_ref[0:strip, :]
        wg = [wg_ref[k:k + 1, :] for k in range(FFN_K)]
        wu = [wu_ref[k:k + 1, :] for k in range(FFN_K)]
        bg, bu = bg_ref[...], bu_ref[...]

        def conv(src, base, w, b):
            acc = b
            for k in range(FFN_K):
                o = base - (FFN_K - 1) + k
                acc = acc + w[k] * src[o:o + strip, :]
            return acc

        for s in range(T // strip):
            r = s * strip
            gate = conv(hg_scr, HR, wg, bg) if s == 0 else conv(fg_ref, r, wg, bg)
            up = conv(hu_scr, HR, wu, bu) if s == 0 else conv(fu_ref, r, wu, bu)
            gc_ref[r:r + strip, :] = gate
            uc_ref[r:r + strip, :] = up
            a_ref[r:r + strip, :] = (gate * _sig(gate) * up).astype(BF16)

    cur, before, _, wspec = _ffn_specs()
    pre = jax.ShapeDtypeStruct((NPAIR, L, UPW), F32)
    return pl.pallas_call(
        body, name=name, out_shape=(jax.ShapeDtypeStruct((NPAIR, L, UPW), BF16), pre, pre), grid=(NPAIR, NB),
        in_specs=[cur(0), before(0), cur(NPAIR), before(NPAIR), wspec(0, FFN_K), wspec(NPAIR, FFN_K),
                  wspec(0, 1), wspec(NPAIR, 1)],
        out_specs=(cur(0), cur(0), cur(0)),
        scratch_shapes=[pltpu.VMEM((HR + strip, UPW), F32), pltpu.VMEM((HR + strip, UPW), F32)])(
            f, f, f, f, w, w, b, b)


def ffn_act_bwd(f, gc, uc, da, w, name):
    ext = T + HR
    sub = 8

    def body(fg_ref, fu_ref, gc_ref, gca_ref, uc_ref, uca_ref, da_ref, daa_ref, wg_ref, wu_ref,
             dfg_ref, dfu_ref, dwg_ref, dwu_ref, dbg_ref, dbu_ref, dg_scr, du_scr):
        c = pl.program_id(1)
        wg = [wg_ref[k:k + 1, :] for k in range(FFN_K)]
        wu = [wu_ref[k:k + 1, :] for k in range(FFN_K)]

        for s in range(ext // sub):
            r = s * sub
            if r < T:
                gate, up, dav = gc_ref[r:r + sub, :], uc_ref[r:r + sub, :], da_ref[r:r + sub, :]
            else:
                gate, up, dav = gca_ref[...], uca_ref[...], jnp.where(c < NB - 1, daa_ref[...], 0.0)
            sg = _sig(gate)
            dg_scr[r:r + sub, :] = dav * up * (sg * (1.0 + gate * (1.0 - sg)))
            du_scr[r:r + sub, :] = dav * gate * sg

        def back(d_scr, x_ref, w, df_ref, dw_ref, db_ref):
            accs = [jnp.zeros((sub, UPW), F32) for _ in range(FFN_K + 1)]
            for s in range(T // (2 * sub)):
                pieces = []
                for r in (2 * s * sub, (2 * s + 1) * sub):
                    xv = x_ref[r:r + sub, :]
                    df = jnp.zeros((sub, UPW), F32)
                    for k in range(FFN_K):
                        off = FFN_K - 1 - k
                        dj = d_scr[r + off:r + off + sub, :]
                        df = df + w[k] * dj
                        accs[k] = accs[k] + xv * dj
                        if off == 0:
                            accs[FFN_K] = accs[FFN_K] + dj
                    pieces.append(df)
                df_ref[2 * s * sub:2 * (s + 1) * sub, :] = jnp.concatenate(pieces, axis=0).astype(BF16)
            dwp = _stack_rows([jnp.sum(a, axis=0, keepdims=True) for a in accs[:FFN_K]])
            dbp = jnp.sum(accs[FFN_K], axis=0, keepdims=True)

            @pl.when(c == 0)
            def _():
                dw_ref[...] = dwp
                db_ref[...] = dbp

            @pl.when(c > 0)
            def _():
                dw_ref[...] += dwp
                db_ref[...] += dbp

        back(dg_scr, fg_ref, wg, dfg_ref, dwg_ref, dbg_ref)
        back(du_scr, fu_ref, wu, dfu_ref, dwu_ref, dbu_ref)

    cur, _, after, wspec = _ffn_specs()
    slab = jax.ShapeDtypeStruct((NPAIR, L, UPW), BF16)
    wsh = jax.ShapeDtypeStruct((NPAIR, FFN_K, UPW), F32)
    bsh = jax.ShapeDtypeStruct((NPAIR, 1, UPW), F32)
    return pl.pallas_call(
        body, name=name, out_shape=(slab, slab, wsh, wsh, bsh, bsh), grid=(NPAIR, NB),
        in_specs=[cur(0), cur(NPAIR), cur(0), after(0), cur(0), after(0), cur(0), after(0),
                  wspec(0, FFN_K), wspec(NPAIR, FFN_K)],
        out_specs=(cur(0), cur(0), wspec(0, FFN_K), wspec(0, FFN_K), wspec(0, 1), wspec(0, 1)),
        scratch_shapes=[pltpu.VMEM((ext, UPW), F32), pltpu.VMEM((ext, UPW), F32)])(
            f, f, gc, gc, uc, uc, da, da, w, w)


def _mesh_pos():
    return lax.axis_index("x"), lax.axis_index("y"), lax.axis_index("c")


def _peer(pos, k):
    x, y, c = pos
    px = 1 - x if k & 4 else x
    py = 1 - y if k & 2 else y
    pc = 1 - c if k & 1 else c
    return (px, py, pc), 4 * px + 2 * py + pc


_CHIP_FLIPS = (4, 2, 6)
_HBM = pl.BlockSpec(memory_space=pl.ANY)


def all_gather(shards, name):
    nt = len(shards)

    def body(*refs):
        ins, outs = refs[:nt], refs[nt:2 * nt]
        send, recv, local = refs[2 * nt:]
        pos = _mesh_pos()
        me = 4 * pos[0] + 2 * pos[1] + pos[2]
        sib, sib_id = _peer(pos, 1)

        def copy(t, k, block_id, to, src=None):
            dst = outs[t].at[block_id]
            return pltpu.make_async_remote_copy(
                src_ref=dst if src is None else src, dst_ref=dst, send_sem=send.at[t, k], recv_sem=recv.at[t, k],
                device_id=to, device_id_type=MESH)

        locals_ = [pltpu.make_async_copy(ins[t], outs[t].at[me], local.at[t]) for t in range(nt)]
        for cp in locals_:
            cp.start()
        started = []
        for j, flip in enumerate(_CHIP_FLIPS):
            for t in range(nt):
                started.append(copy(t, 1 + j, me, _peer(pos, flip)[0], src=ins[t]))
        for t in range(nt):
            started.append(copy(t, 0, me, sib, src=ins[t]))
        for cp in started:
            cp.start()
        for j, flip in enumerate(_CHIP_FLIPS):
            _, pid = _peer(pos, flip)
            for t in range(nt):
                copy(t, 1 + j, pid, sib).wait_recv()
                fwd = copy(t, 4 + j, pid, sib)
                fwd.start()
                started.append(fwd)
        for t in range(nt):
            copy(t, 0, sib_id, sib).wait_recv()
        for j, flip in enumerate(_CHIP_FLIPS):
            _, pid = _peer(pos, flip | 1)
            for t in range(nt):
                copy(t, 4 + j, pid, sib).wait_recv()
        for cp in started:
            cp.wait_send()
        for cp in locals_:
            cp.wait()

    return pl.pallas_call(
        body, name=name,
        out_shape=tuple(jax.ShapeDtypeStruct((NDEV,) + s.shape, s.dtype) for s in shards),
        in_specs=[_HBM] * nt, out_specs=tuple([_HBM] * nt),
        scratch_shapes=[pltpu.SemaphoreType.DMA((nt, NDEV - 1)), pltpu.SemaphoreType.DMA((nt, NDEV - 1)),
                        pltpu.SemaphoreType.DMA((nt,))])(*shards)


_SPLIT = dict(has_side_effects=pltpu.SideEffectType.DATAFLOW_SIDE_EFFECTING)
_SEM = pl.BlockSpec(memory_space=pltpu.SEMAPHORE)


def _exchange_start(name, srcs, lands, copies, after):
    ns, nl = len(srcs), len(lands)
    ncopy = len(copies(None, [None] * ns, [None] * nl))

    def body(*refs):
        src_refs, land_refs = refs[:ns], refs[ns:ns + nl]
        send, recv = refs[ns + nl + len(after):ns + nl + len(after) + 2]
        token = refs[-1]
        for i, (src, dst, peer) in enumerate(copies(_mesh_pos(), src_refs, land_refs)):
            pltpu.make_async_remote_copy(src_ref=src, dst_ref=dst, send_sem=send.at[i], recv_sem=recv.at[i],
                                         device_id=peer, device_id_type=MESH).start()
        token[...] = jnp.zeros_like(token)

    hbm = lambda a: pltpu.HBM(a.shape, a.dtype)
    out = pl.pallas_call(
        body, name=name,
        out_shape=(pltpu.SemaphoreType.DMA((ncopy,)), pltpu.SemaphoreType.DMA((ncopy,)),
                   *[hbm(a) for a in srcs], *[hbm(a) for a in lands], jax.ShapeDtypeStruct((8, 128), F32)),
        in_specs=[pl.BlockSpec(memory_space=pltpu.HBM)] * (ns + nl) + [_HBM] * len(after),
        out_specs=(_SEM, _SEM, *[pl.BlockSpec(memory_space=pltpu.HBM)] * (ns + nl),
                   pl.BlockSpec(memory_space=pltpu.VMEM)),
        input_output_aliases={i: 2 + i for i in range(ns + nl)},
        compiler_params=pltpu.CompilerParams(**_SPLIT))(
            *[pltpu.with_memory_space_constraint(a, pltpu.HBM) for a in list(srcs) + list(lands)], *after)
    return out[0], out[1], list(out[2:2 + ns]), list(out[2 + ns:2 + ns + nl]), out[-1]


def _exchange_wait(name, send, recv, srcs, lands, copies, after):
    ns, nl = len(srcs), len(lands)

    def body(*refs):
        src_refs, land_refs = refs[:ns], refs[ns:ns + nl]
        send_ref, recv_ref = refs[ns + nl:ns + nl + 2]
        pos = _mesh_pos()
        for i, (src, dst, peer) in enumerate(copies(pos, src_refs, land_refs, arriving=True)):
            cp = pltpu.make_async_remote_copy(src_ref=src, dst_ref=dst, send_sem=send_ref.at[i], recv_sem=recv_ref.at[i],
                                              device_id=peer, device_id_type=MESH)
            cp.wait_send()
            cp.wait_recv()

    hbm = lambda a: pltpu.HBM(a.shape, a.dtype)
    out = pl.pallas_call(
        body, name=name, out_shape=tuple(hbm(a) for a in list(srcs) + list(lands)),
        in_specs=[pl.BlockSpec(memory_space=pltpu.HBM)] * (ns + nl) + [_SEM, _SEM] + [_HBM] * len(after),
        out_specs=tuple([pl.BlockSpec(memory_space=pltpu.HBM)] * (ns + nl)),
        input_output_aliases={i: i for i in range(ns + nl)},
        compiler_params=pltpu.CompilerParams(**_SPLIT))(*srcs, *lands, send, recv, *after)
    return list(out[:ns]), list(out[ns:])


def _gather_copies(pos, srcs, lands, arriving=False):
    if pos is None:
        return [None] * (len(srcs) * (NDEV - 1))
    me = 4 * pos[0] + 2 * pos[1] + pos[2]
    out = []
    for src, land in zip(srcs, lands):
        for k in range(1, NDEV):
            peer, pid = _peer(pos, k)
            out.append((src, land.at[pid if arriving else me], peer))
    return out


def _scatter_copies(pos, srcs, lands, arriving=False):
    if pos is None:
        return [None] * (len(srcs) * (NDEV - 1))
    out = []
    for src, land in zip(srcs, lands):
        for k in range(1, NDEV):
            peer, pid = _peer(pos, k)
            out.append((src.at[pid], land.at[k - 1], peer))
    return out


def gather_start(name, shards, lands, after=()):
    return _exchange_start(name, shards, lands, _gather_copies, list(after))


def gather_wait(name, handle, after=()):
    send, recv, srcs, lands, _ = handle
    return _exchange_wait(name, send, recv, srcs, lands, _gather_copies, list(after))


def scatter_start(name, grads, after=()):
    lands = [pltpu.with_memory_space_constraint(lax.empty((NDEV - 1,) + g.shape[1:], g.dtype), pltpu.HBM)
             for g in grads]
    return _exchange_start(name, grads, lands, _scatter_copies, list(after))


def scatter_wait(name, handle, after=()):
    send, recv, srcs, lands, _ = handle
    return _exchange_wait(name, send, recv, srcs, lands, _scatter_copies, list(after))


def sum_slots(parts, name):
    def body(p_ref, o_ref):
        acc = p_ref[0]
        for d in range(1, NDEV):
            acc = acc + p_ref[d]
        o_ref[...] = acc

    return pl.pallas_call(body, name=name, out_shape=jax.ShapeDtypeStruct(parts.shape[1:], F32))(parts)


def _adamw_update(g, w_ref, m_ref, v_ref, g_ref, d_ref, nm_ref, nv_ref):
    g_ref[...] = g
    nm = ADAM_B1 * m_ref[...] + (1.0 - ADAM_B1) * g
    nv = ADAM_B2 * v_ref[...] + (1.0 - ADAM_B2) * (g * g)
    nm_ref[...] = nm
    nv_ref[...] = nv
    m_hat = nm / (1.0 - ADAM_B1 ** ADAM_STEP)
    v_hat = nv / (1.0 - ADAM_B2 ** ADAM_STEP)
    d_ref[...] = -ADAM_LR * (m_hat / (jnp.sqrt(v_hat) + ADAM_EPS) + ADAM_WD * w_ref[...])


def adamw_big(own, lands, w, m, v, me, tr, name):
    _, r, c = own[0].shape
    nt = r // tr

    def body(me_ref, *refs):
        ins, (w_ref, m_ref, v_ref), outs = refs[:2 * DEPTH], refs[2 * DEPTH:2 * DEPTH + 3], refs[2 * DEPTH + 3:]
        layer = pl.program_id(0)
        for l in range(DEPTH):
            @pl.when(layer == l)
            def _(l=l):
                g = ins[2 * l][...].astype(F32)
                for s in range(NDEV - 1):
                    g = g + ins[2 * l + 1][s].astype(F32)
                _adamw_update(g, w_ref, m_ref, v_ref, *outs)

    pick = lambda l: (lambda layer, i, me_ref: jnp.where(layer == l, i, 0))
    in_specs = []
    for l in range(DEPTH):
        in_specs.append(pl.BlockSpec((None, tr, c), lambda layer, i, me_ref, f=pick(l): (me_ref[0], f(layer, i, me_ref), 0)))
        in_specs.append(pl.BlockSpec((NDEV - 1, tr, c), lambda layer, i, me_ref, f=pick(l): (0, f(layer, i, me_ref), 0)))
    blk = pl.BlockSpec((tr, c), lambda layer, i, me_ref: (layer * nt + i, 0))
    sh = jax.ShapeDtypeStruct((DEPTH * r, c), F32)
    args = [a for l in range(DEPTH) for a in (own[l], lands[l])]
    return pl.pallas_call(
        body, name=name, out_shape=(sh, sh, sh, sh),
        grid_spec=pltpu.PrefetchScalarGridSpec(
            num_scalar_prefetch=1, grid=(DEPTH, nt), in_specs=in_specs + [blk, blk, blk],
            out_specs=(blk, blk, blk, blk)),
        compiler_params=_params(VMEM_BIG))(me, *args, w, m, v)


def adamw_many(gs, ws, ms, vs, name):
    n = len(gs)

    def body(*refs):
        outs = refs[4 * n:]
        for i in range(n):
            _adamw_update(refs[i][...], refs[n + i], refs[2 * n + i], refs[3 * n + i], *outs[4 * i:4 * i + 4])

    out_shape = tuple(jax.ShapeDtypeStruct(w.shape, F32) for w in ws for _ in range(4))
    res = pl.pallas_call(body, name=name, out_shape=out_shape)(*gs, *ws, *ms, *vs)
    return [res[4 * i:4 * i + 4] for i in range(n)]


def local_step(x, target, P, hooks):
    tb = _ret_tables()
    bias = attn_bias_table()
    h = jnp.concatenate([jnp.zeros((PAD, D), F32), P["meta"], x], axis=0)
    stash = []
    for l in range(DEPTH):
        s = {"h": h}
        s["w_in"], s["cv_pw"], s["w_out"] = hooks["mixer_w"](l, h)
        u = s["u"] = rms_fwd(h, P["norm_mix_g"][l], BF16, f"rms_mix_{l}", after=hooks["first_after"] if l == 0 else None)
        proj = s["proj"] = mm_nt(f"in_proj_{l}", u, s["w_in"], 768)
        s["o"] = attn_fwd(proj, bias, P["q_norm_g"][l], P["k_norm_g"][l], P["attn_sinks"][l], f"attn_fwd_{l}")
        y_att = rms_fwd(s["o"], P["attn_out_g"][l], BF16, f"rms_att_{l}")
        y_cv, s["xc"] = conv_fwd(proj, P["cv_dw_w"][l], P["cv_dw_b"][l], P["cv_ln_g"][l], P["cv_ln_b"][l],
                                 s["cv_pw"], P["cv_out_g"][l], f"conv_fwd_{l}")
        y_ret, s["states"] = ret_fwd(proj, P["ret_gn_g"][l], tb, f"ret_fwd_{l}")
        ycat = s["ycat"] = jnp.concatenate([y_att, y_cv.astype(BF16), y_ret.astype(BF16)], axis=1)
        h = mm_nn(f"out_proj_{l}", ycat, s["w_out"], 512, res=h)
        s["h1"] = h
        s["ffn_up"], s["ffn_down"] = hooks["ffn_w"](l, h)
        u2 = s["u2"] = rms_fwd(h, P["norm_ffn_g"][l], BF16, f"rms_ffn_{l}")
        s["f"] = ffn_up_fwd(f"ffn_up_{l}", u2, s["ffn_up"])
        a, s["gc"], s["uc"] = ffn_act_fwd(s["f"], P["ffn_dw_w"][l], P["ffn_dw_b"][l], f"ffn_act_fwd_{l}")
        s["a"] = a
        h = ffn_down_fwd(f"ffn_down_{l}", a, s["ffn_down"], h)
        stash.append(s)

    dh, dhm, loss = loss_head(h, target)
    G = {n: [None] * DEPTH for n in ("norm_mix_g", "q_norm_g", "k_norm_g", "attn_sinks", "attn_out_g", "cv_dw_w",
                                     "cv_dw_b", "cv_ln_g", "cv_ln_b", "cv_out_g", "ret_gn_g", "norm_ffn_g",
                                     "ffn_dw_w", "ffn_dw_b")}
    for l in reversed(range(DEPTH)):
        s = stash[l]
        da = ffn_down_dx(f"ffn_down_dx_{l}", dhm, s["ffn_down"])
        dw_down = ffn_down_dw(f"ffn_down_dw_{l}", s["a"], dhm)
        dfg, dfu, dwg, dwu, dbg, dbu = ffn_act_bwd(s["f"], s["gc"], s["uc"], da, P["ffn_dw_w"][l], f"ffn_act_bwd_{l}")
        G["ffn_dw_w"][l] = jnp.concatenate([dwg, dwu], axis=0)
        G["ffn_dw_b"][l] = jnp.concatenate([dbg, dbu], axis=0)
        du2 = ffn_up_dx(f"ffn_up_dx_{l}", dfg, dfu, s["ffn_up"])
        dw_up = ffn_up_dw(f"ffn_up_dw_{l}", s["u2"], dfg, dfu)
        after = hooks["ffn_grads"](l, dw_down, dw_up)
        dh, dhm, G["norm_ffn_g"][l] = rms_bwd(s["h1"], P["norm_ffn_g"][l], du2, dh, f"rms_ffn_bwd_{l}", after=after)
        dycat = mm_nt(f"out_proj_dx_{l}", dhm, s["w_out"], 512)
        dw_out = mm_ta(f"out_proj_dw_{l}", s["ycat"], dhm, 512)
        do, _, G["attn_out_g"][l] = rms_bwd(s["o"], P["attn_out_g"][l], dycat, None, f"rms_att_bwd_{l}")
        (dq, dk_c, dk_p, dk_m, dv_c, dv_p, dv_m, G["q_norm_g"][l], G["k_norm_g"][l], dsk) = attn_bwd(
            s["proj"], do, bias, P["q_norm_g"][l], P["k_norm_g"][l], P["attn_sinks"][l], f"attn_bwd_{l}")
        G["attn_sinks"][l] = dsk[:, 0]
        shift = lambda z: jnp.concatenate([z[T:], jnp.zeros((T, NKV * HD), F32)], axis=0)
        dk = (dk_c + shift(dk_p)).at[:T].add(dk_m)
        dv = (dv_c + shift(dv_p)).at[:T].add(dv_m)
        (dca, dcb, G["cv_dw_w"][l], G["cv_dw_b"][l], G["cv_ln_g"][l], G["cv_ln_b"][l], G["cv_out_g"][l],
         dpw) = conv_bwd(s["proj"], s["xc"], dycat, P["cv_dw_w"][l], P["cv_ln_g"][l], P["cv_ln_b"][l], s["cv_pw"],
                         P["cv_out_g"][l], f"conv_bwd_{l}")
        dret, G["ret_gn_g"][l] = ret_bwd(s["proj"], dycat, s["states"], P["ret_gn_g"][l], tb, f"ret_bwd_{l}")
        dproj = jnp.concatenate([dq, dk, dv, dca, dcb, dret], axis=1).astype(BF16)
        du = mm_nn(f"in_proj_dx_{l}", dproj, s["w_in"], 512)
        dw_in = mm_ta(f"in_proj_dw_{l}", dproj, s["u"], 768)
        after = hooks["mixer_grads"](l, dw_out, dpw, dw_in)
        dh, dhm, G["norm_mix_g"][l] = rms_bwd(s["h"], P["norm_mix_g"][l], du, dh, f"rms_mix_bwd_{l}", after=after,
                                              last=(l == 0))
    return loss[0, 0], dh, dhm[PAD:T], G


_SMALL = ("meta", "norm_mix_g", "q_norm_g", "k_norm_g", "attn_sinks", "attn_out_g", "cv_dw_w", "cv_dw_b", "cv_ln_g",
          "cv_ln_b", "cv_out_g", "ret_gn_g", "norm_ffn_g", "ffn_dw_w", "ffn_dw_b")
_BIG = ("w_in", "cv_pw", "w_out", "ffn_up", "ffn_down")
_TRANSPOSED = ("w_in", "ffn_up")
_ORDER = ("meta", "norm_mix_g", "w_in", "q_norm_g", "k_norm_g", "attn_sinks", "attn_out_g", "cv_dw_w", "cv_dw_b",
          "cv_ln_g", "cv_ln_b", "cv_pw", "cv_out_g", "ret_gn_g", "w_out", "norm_ffn_g", "ffn_up", "ffn_dw_w",
          "ffn_dw_b", "ffn_down")
_SMALL_SHARDED = {"meta": D, "cv_dw_w": CV, "ffn_dw_w": 2 * D_FF}


def _pack(arrs):
    flat = jnp.concatenate([a.reshape(-1) for a in arrs])
    n = flat.shape[0]
    rows = -(-n // 1024) * 8
    return jnp.pad(flat, (0, rows * 128 - n)).reshape(rows, 128)


def _unpack(packed, shapes):
    flat = packed.reshape(-1)
    out, off = [], 0
    for s in shapes:
        n = int(np.prod(s))
        out.append(flat[off:off + n].reshape(s))
        off += n
    return out


def kernel(x, meta, norm_mix_g, w_in, q_norm_g, k_norm_g, attn_sinks, attn_out_g, cv_dw_w, cv_dw_b, cv_ln_g, cv_ln_b, cv_pw, cv_out_g, ret_gn_g, w_out, norm_ffn_g, ffn_up, ffn_dw_w, ffn_dw_b, ffn_down, loss_target, m_meta, m_norm_mix_g, m_w_in, m_q_norm_g, m_k_norm_g, m_attn_sinks, m_attn_out_g, m_cv_dw_w, m_cv_dw_b, m_cv_ln_g, m_cv_ln_b, m_cv_pw, m_cv_out_g, m_ret_gn_g, m_w_out, m_norm_ffn_g, m_ffn_up, m_ffn_dw_w, m_ffn_dw_b, m_ffn_down, v_meta, v_norm_mix_g, v_w_in, v_q_norm_g, v_k_norm_g, v_attn_sinks, v_attn_out_g, v_cv_dw_w, v_cv_dw_b, v_cv_ln_g, v_cv_ln_b, v_cv_pw, v_cv_out_g, v_ret_gn_g, v_w_out, v_norm_ffn_g, v_ffn_up, v_ffn_dw_w, v_ffn_dw_b, v_ffn_down):
    W = dict(meta=meta, norm_mix_g=norm_mix_g, w_in=w_in, q_norm_g=q_norm_g, k_norm_g=k_norm_g,
             attn_sinks=attn_sinks, attn_out_g=attn_out_g, cv_dw_w=cv_dw_w, cv_dw_b=cv_dw_b, cv_ln_g=cv_ln_g,
             cv_ln_b=cv_ln_b, cv_pw=cv_pw, cv_out_g=cv_out_g, ret_gn_g=ret_gn_g, w_out=w_out,
             norm_ffn_g=norm_ffn_g, ffn_up=ffn_up, ffn_dw_w=ffn_dw_w, ffn_dw_b=ffn_dw_b, ffn_down=ffn_down)
    M = dict(meta=m_meta, norm_mix_g=m_norm_mix_g, w_in=m_w_in, q_norm_g=m_q_norm_g, k_norm_g=m_k_norm_g,
             attn_sinks=m_attn_sinks, attn_out_g=m_attn_out_g, cv_dw_w=m_cv_dw_w, cv_dw_b=m_cv_dw_b,
             cv_ln_g=m_cv_ln_g, cv_ln_b=m_cv_ln_b, cv_pw=m_cv_pw, cv_out_g=m_cv_out_g, ret_gn_g=m_ret_gn_g,
             w_out=m_w_out, norm_ffn_g=m_norm_ffn_g, ffn_up=m_ffn_up, ffn_dw_w=m_ffn_dw_w, ffn_dw_b=m_ffn_dw_b,
             ffn_down=m_ffn_down)
    V = dict(meta=v_meta, norm_mix_g=v_norm_mix_g, w_in=v_w_in, q_norm_g=v_q_norm_g, k_norm_g=v_k_norm_g,
             attn_sinks=v_attn_sinks, attn_out_g=v_attn_out_g, cv_dw_w=v_cv_dw_w, cv_dw_b=v_cv_dw_b,
             cv_ln_g=v_cv_ln_g, cv_ln_b=v_cv_ln_b, cv_pw=v_cv_pw, cv_out_g=v_cv_out_g, ret_gn_g=v_ret_gn_g,
             w_out=v_w_out, norm_ffn_g=v_norm_ffn_g, ffn_up=v_ffn_up, ffn_dw_w=v_ffn_dw_w, ffn_dw_b=v_ffn_dw_b,
             ffn_down=v_ffn_down)
    me = 4 * lax.axis_index("x") + 2 * lax.axis_index("y") + lax.axis_index("c")
    for n in _TRANSPOSED:
        W[n], M[n], V[n] = (a.transpose(0, 2, 1) for a in (W[n], M[n], V[n]))

    sh = {n: [W[n][l].astype(BF16) for l in range(DEPTH)] for n in _BIG}
    mix = lambda l: [sh["w_in"][l], sh["cv_pw"][l], sh["w_out"][l]]
    ffn = lambda l: [sh["ffn_up"][l], sh["ffn_down"][l]]
    first = all_gather(mix(0) + [meta, cv_dw_w, ffn_dw_w], "gather_first")
    g_meta, g_cdw, g_fdw = first[3:6]

    def landing(shards):
        return [lax.dynamic_update_slice(lax.empty((NDEV,) + s.shape, s.dtype), s[None], (me,) + (0,) * s.ndim)
                for s in shards]

    gathers = {("ffn", 0): gather_start("gather_ffn0_start", ffn(0), landing(ffn(0)), after=[first[0]])}
    gathers["mix", 1] = gather_start("gather_mix1_start", mix(1), landing(mix(1)), after=[gathers["ffn", 0][4]])
    gathers["ffn", 1] = gather_start("gather_ffn1_start", ffn(1), landing(ffn(1)), after=[gathers["mix", 1][4]])

    def mixer_w(l, h):
        g_in, g_pw, g_out = first[0:3] if l == 0 else gather_wait(f"gather_mix{l}_wait", gathers["mix", l], after=[h])[1]
        return g_in.reshape(IN_W, D), g_pw.reshape(CV, CV), g_out.reshape(D, D)

    def ffn_w(l, h1):
        return gather_wait(f"gather_ffn{l}_wait", gathers["ffn", l], after=[h1])[1]

    scatters = {}

    def ffn_grads(l, dw_down, dw_up):
        scatters["ffn", l] = scatter_start(f"scatter_ffn{l}_start", [dw_up, dw_down.reshape(NDEV, DNR, D)])
        return scatters["ffn", l][4]

    def mixer_grads(l, dw_out, dpw, dw_in):
        grads = [dw_in.reshape(NDEV, IN_W // NDEV, D), dpw.astype(BF16).reshape(NDEV, CV // NDEV, CV),
                 dw_out.reshape(NDEV, D // NDEV, D)]
        scatters["mix", l] = scatter_start(f"scatter_mix{l}_start", grads)
        return scatters["mix", l][4]

    P = dict(
        meta=g_meta.transpose(1, 0, 2).reshape(N_META, D),
        cv_dw_w=g_cdw.transpose(1, 2, 0, 3).reshape(DEPTH, CONV_K, CV),
        ffn_dw_w=g_fdw.transpose(1, 0, 2, 3),
        ffn_dw_b=ffn_dw_b.reshape(DEPTH, NDEV, 1, UPW),
        attn_sinks=attn_sinks,
    )
    for n in ("norm_mix_g", "q_norm_g", "k_norm_g", "attn_out_g", "cv_dw_b", "cv_ln_g", "cv_ln_b", "cv_out_g",
              "ret_gn_g", "norm_ffn_g"):
        P[n] = W[n].reshape(DEPTH, 1, -1)

    hooks = dict(mixer_w=mixer_w, ffn_w=ffn_w, ffn_grads=ffn_grads, mixer_grads=mixer_grads,
                 first_after=gathers["ffn", 1][4])
    loss_part, dx, dmeta, G = local_step(x[0], loss_target[0], P, hooks)

    small_full = {
        "meta": dmeta,
        "cv_dw_w": jnp.stack(G["cv_dw_w"]),
        "ffn_dw_w": jnp.stack([g.transpose(1, 0, 2).reshape(FFN_K, 2 * D_FF) for g in G["ffn_dw_w"]]),
        "ffn_dw_b": jnp.stack([g.reshape(2 * D_FF) for g in G["ffn_dw_b"]]),
        "attn_sinks": jnp.stack(G["attn_sinks"]),
    }
    for n in _SMALL:
        if n not in small_full:
            small_full[n] = jnp.stack([g.reshape(-1) for g in G[n]])
    shapes = [small_full[n].shape for n in _SMALL] + [(1,)]
    packed = _pack([small_full[n] for n in _SMALL] + [loss_part.reshape(1)])
    small_exchange = gather_start("reduce_small_start", [packed], landing([packed]), after=[dx])

    out = {}
    tiles = {"w_in": 144, "cv_pw": 32, "w_out": 128, "ffn_up": 176, "ffn_down": 176}
    me1 = me.astype(jnp.int32).reshape(1)
    own, lands = {n: [None] * DEPTH for n in _BIG}, {n: [None] * DEPTH for n in _BIG}

    def arrived(kind, names, after):
        for l in range(DEPTH):
            srcs, got = scatter_wait(f"scatter_{kind}{l}_wait", scatters[kind, l], after=after)
            for n, s_, g_ in zip(names, srcs, got):
                own[n][l], lands[n][l] = s_, g_

    def update(names):
        for n in names:
            shard = W[n].shape
            rows, cols = shard[0] * shard[1], shard[2]
            res = adamw_big(own[n], lands[n], W[n].reshape(rows, cols), M[n].reshape(rows, cols),
                            V[n].reshape(rows, cols), me1, tiles[n], f"adamw_{n}")
            out[n] = [r.reshape(shard) for r in res]

    arrived("ffn", ("ffn_up", "ffn_down"), [small_exchange[4]])
    update(("ffn_up", "ffn_down"))
    slots = gather_wait("reduce_small_wait", small_exchange, after=[out["ffn_down"][0]])[1][0]
    summed = _unpack(sum_slots(slots, "reduce_small_sum"), shapes)
    loss = summed[-1][0]
    small_g = []
    for n, g in zip(_SMALL, summed):
        if n in _SMALL_SHARDED:
            width = _SMALL_SHARDED[n] // NDEV
            g = lax.dynamic_slice_in_dim(g, me * width, width, axis=g.ndim - 1)
        small_g.append(g)
    flat2 = lambda a: a.reshape(-1, a.shape[-1])
    res = adamw_many([flat2(g) for g in small_g], *[[flat2(X[n]) for n in _SMALL] for X in (W, M, V)], "adamw_small")
    for n, r in zip(_SMALL, res):
        out[n] = [a.reshape(W[n].shape) for a in r]
    arrived("mix", ("w_in", "cv_pw", "w_out"), [out["ffn_down"][0], res[0][0]])
    update(("w_in", "cv_pw", "w_out"))
    for n in _TRANSPOSED:
        out[n] = [r.transpose(0, 2, 1) for r in out[n]]

    return (loss, dx[None], *[out[n][0] for n in _ORDER], *[out[n][1] for n in _ORDER],
            *[out[n][2] for n in _ORDER], *[out[n][3] for n in _ORDER])
```

```python
import functools
import math

import numpy as np
import jax
import jax.numpy as jnp
from jax import lax
from jax.experimental import pallas as pl
from jax.experimental.pallas import tpu as pltpu

F32 = jnp.float32
BF16 = jnp.bfloat16

D = 1024
SEQ = 2048
DEPTH = 2
T = 128
L = SEQ + T
NB = L // T
N_META = 16
PAD = T - N_META
HD = 64
NQ = 8
NKV = 2
GQA = NQ // NKV
CV = 256
CONV_K = 31
RH = 4
D_FF = 2816
FFN_K = 3
IN_W = 2304
RMS_EPS = 1e-6
LN_EPS = 1e-5
NEG = -1e30
NDEV = 8
UPW = 2 * D_FF // NDEV
DNR = D_FF // NDEV
NPAIR = NDEV // 2

ADAM_LR, ADAM_B1, ADAM_B2, ADAM_EPS, ADAM_WD, ADAM_STEP = 0.001, 0.9, 0.999, 1e-08, 0.01, 10

VMEM_BIG = 56 * 1024 * 1024

MESH = pl.DeviceIdType.MESH


def _params(vmem=None):
    return pltpu.CompilerParams(vmem_limit_bytes=vmem) if vmem else None


def _dot(a, b, nt=False):
    return lax.dot_general(a, b, (((1,), (1 if nt else 0,)), ((), ())), preferred_element_type=F32)


def _sig(x):
    return 1.0 / (1.0 + jnp.exp(-x))


def _bf(x):
    return x.astype(BF16)


def _stack_rows(rows):
    idx = lax.broadcasted_iota(jnp.int32, (len(rows), rows[0].shape[1]), 0)
    out = jnp.zeros((len(rows), rows[0].shape[1]), F32)
    for r, v in enumerate(rows):
        out = jnp.where(idx == r, v, out)
    return out


def rms_fwd(x, g, out_dtype, name, after=None):
    n, w = x.shape

    def body(x_ref, g_ref, *rest):
        o_ref = rest[-1]
        xv = x_ref[...]
        r = lax.rsqrt(jnp.mean(xv * xv, axis=-1, keepdims=True) + RMS_EPS)
        o_ref[...] = (xv * r * g_ref[...]).astype(o_ref.dtype)

    deps = [] if after is None else [after]
    return pl.pallas_call(
        body, name=name, out_shape=jax.ShapeDtypeStruct((n, w), out_dtype), grid=(n // T,),
        in_specs=[pl.BlockSpec((T, w), lambda i: (i, 0)), pl.BlockSpec((1, w), lambda i: (0, 0))] + [_HBM] * len(deps),
        out_specs=pl.BlockSpec((T, w), lambda i: (i, 0)))(x, g, *deps)


def rms_bwd(x, g, dy, dres, name, after=None, dy_col=0, last=False):
    n, w = x.shape
    has_res = dres is not None
    deps = [] if after is None else [after]

    def body(x_ref, g_ref, dy_ref, *rest):
        rest = rest[:len(rest) - 3 - len(deps)] + rest[len(rest) - 3:]
        if has_res:
            dres_ref, dx_ref, dxm_ref, dg_ref = rest
        else:
            dx_ref, dxm_ref, dg_ref = rest
        i = pl.program_id(0)
        xv = x_ref[...]
        r = lax.rsqrt(jnp.mean(xv * xv, axis=-1, keepdims=True) + RMS_EPS)
        xh = xv * r
        dyv = dy_ref[...]
        dxh = dyv * g_ref[...]
        dx = r * (dxh - xh * jnp.mean(dxh * xh, axis=-1, keepdims=True))
        if has_res:
            dx = dx + dres_ref[...]
        dx_ref[...] = dx
        if last:
            @pl.when(i == 0)
            def _():
                dxm_ref[...] = dx
        else:
            rows = i * T + lax.broadcasted_iota(jnp.int32, (T, 1), 0)
            dxm_ref[...] = jnp.where(rows >= PAD, dx, 0.0).astype(BF16)
        part = jnp.sum(dyv * xh, axis=0, keepdims=True)

        @pl.when(i == 0)
        def _():
            dg_ref[...] = part

        @pl.when(i > 0)
        def _():
            dg_ref[...] += part

    row = pl.BlockSpec((T, w), lambda i: (i, 0))
    vec = pl.BlockSpec((1, w), lambda i: (0, 0))
    ins = [x, g, dy] + ([dres] if has_res else []) + deps
    if last:
        out_shape = (jax.ShapeDtypeStruct((n - T, w), F32), jax.ShapeDtypeStruct((T, w), F32))
        out_specs = (pl.BlockSpec((T, w), lambda i: (jnp.maximum(i - 1, 0), 0)), pl.BlockSpec((T, w), lambda i: (0, 0)))
    else:
        out_shape = (jax.ShapeDtypeStruct((n, w), F32), jax.ShapeDtypeStruct((n, w), BF16))
        out_specs = (row, row)
    return pl.pallas_call(
        body, name=name, out_shape=(*out_shape, jax.ShapeDtypeStruct((1, w), F32)), grid=(n // T,),
        in_specs=[row, vec, pl.BlockSpec((T, w), lambda i: (i, dy_col))] + ([row] if has_res else [])
        + [_HBM] * len(deps),
        out_specs=(*out_specs, vec))(*ins)


def loss_head(h, target):
    def body(h_ref, t_ref, dh_ref, dhm_ref, loss_ref):
        n = pl.program_id(0)
        e = jnp.where(n > 0, h_ref[...] - t_ref[...], 0.0)
        dh = e * (1.0 / D)
        dh_ref[...] = dh
        dhm_ref[...] = dh.astype(BF16)
        part = jnp.sum(jnp.sum(e * e, axis=1, keepdims=True), axis=0, keepdims=True) * (0.5 / D)

        @pl.when(n == 0)
        def _():
            loss_ref[...] = jnp.zeros_like(loss_ref)

        @pl.when(n > 0)
        def _():
            loss_ref[...] += jnp.broadcast_to(part, loss_ref.shape)

    row = pl.BlockSpec((T, D), lambda n: (n, 0))
    return pl.pallas_call(
        body, name="loss_head",
        out_shape=(jax.ShapeDtypeStruct((L, D), F32), jax.ShapeDtypeStruct((L, D), BF16),
                   jax.ShapeDtypeStruct((8, 128), F32)),
        grid=(NB,),
        in_specs=[row, pl.BlockSpec((T, D), lambda n: (jnp.maximum(n - 1, 0), 0))],
        out_specs=(row, row, pl.BlockSpec((8, 128), lambda n: (0, 0))))(h, target)


def _mm(name, a, b, *, grid, a_spec, b_spec, o_spec, out_shape, nt=False, ta=False, red=False, res=None,
        res_spec=None):
    def body(a_ref, b_ref, *rest):
        o_ref = rest[-1]
        av = a_ref[...]
        bv = b_ref[...]
        if bv.ndim == 3:
            bv = bv.reshape(bv.shape[0] * bv.shape[1], bv.shape[2])
        if ta:
            acc = lax.dot_general(av, bv, (((0,), (0,)), ((), ())), preferred_element_type=F32)
        else:
            acc = _dot(av, bv, nt)
        if red:
            k = pl.program_id(0)

            @pl.when(k == 0)
            def _():
                o_ref[...] = acc

            @pl.when(k > 0)
            def _():
                o_ref[...] += acc
        else:
            if res is not None:
                rows = lax.broadcasted_iota(jnp.int32, (acc.shape[0], 1), 0)
                acc = rest[0][...] + jnp.where(rows >= PAD, acc, 0.0)
            o_ref[...] = acc.astype(o_ref.dtype)

    ins = [a, b] + ([res] if res is not None else [])
    specs = [a_spec, b_spec] + ([res_spec] if res is not None else [])
    return pl.pallas_call(body, name=name, out_shape=out_shape, grid=grid, in_specs=specs, out_specs=o_spec,
                          compiler_params=_params(VMEM_BIG))(*ins)


def mm_nn(name, a, b, tn, out_dtype=F32, res=None):
    m, k = a.shape
    n = b.shape[1]
    return _mm(name, a, b, grid=(n // tn,),
               a_spec=pl.BlockSpec((m, k), lambda j: (0, 0)), b_spec=pl.BlockSpec((k, tn), lambda j: (0, j)),
               o_spec=pl.BlockSpec((m, tn), lambda j: (0, j)), out_shape=jax.ShapeDtypeStruct((m, n), out_dtype),
               res=res, res_spec=pl.BlockSpec((m, tn), lambda j: (0, j)))


def mm_nt(name, a, b, tn):
    m, k = a.shape
    n = b.shape[0]
    return _mm(name, a, b, grid=(n // tn,), nt=True,
               a_spec=pl.BlockSpec((m, k), lambda j: (0, 0)), b_spec=pl.BlockSpec((tn, k), lambda j: (j, 0)),
               o_spec=pl.BlockSpec((m, tn), lambda j: (0, j)), out_shape=jax.ShapeDtypeStruct((m, n), F32))


def mm_ta(name, a, b, tm, out_dtype=BF16):
    k, m = a.shape
    n = b.shape[1]
    return _mm(name, a, b, grid=(m // tm,), ta=True,
               a_spec=pl.BlockSpec((k, tm), lambda j: (0, j)), b_spec=pl.BlockSpec((k, n), lambda j: (0, 0)),
               o_spec=pl.BlockSpec((tm, n), lambda j: (j, 0)), out_shape=jax.ShapeDtypeStruct((m, n), out_dtype))


def _rms_mm(name, h, g, b, *, grid, b_spec, o_spec, out_shape, after=None):
    deps = [] if after is None else [after]

    def body(h_ref, g_ref, b_ref, *rest):
        o_ref, u_ref = rest[len(deps):]

        @pl.when(pl.program_id(0) == 0)
        def _():
            gv = g_ref[...]
            for c in range(NB):
                rows = slice(c * T, (c + 1) * T)
                xv = h_ref[rows, :]
                r = lax.rsqrt(jnp.mean(xv * xv, axis=-1, keepdims=True) + RMS_EPS)
                u_ref[rows, :] = (xv * r * gv).astype(BF16)

        o_ref[...] = _dot(u_ref[...], b_ref[...], nt=True)

    whole = lambda r: pl.BlockSpec((r, D), lambda j: (0, 0))
    return pl.pallas_call(
        body, name=name, out_shape=(out_shape, jax.ShapeDtypeStruct((L, D), BF16)), grid=grid,
        in_specs=[whole(L), whole(1), b_spec] + [_HBM] * len(deps), out_specs=(o_spec, whole(L)),
        compiler_params=_params(VMEM_BIG))(h, g, b, *deps)


def in_proj_fwd(name, h, g, wt, tn, after=None):
    return _rms_mm(name, h, g, wt, grid=(IN_W // tn,), b_spec=pl.BlockSpec((tn, D), lambda j: (j, 0)),
                   o_spec=pl.BlockSpec((L, tn), lambda j: (0, j)), out_shape=jax.ShapeDtypeStruct((L, IN_W), F32),
                   after=after)


def ffn_up_fwd(name, h, g, wupt):
    return _rms_mm(name, h, g, wupt, grid=(NDEV,), b_spec=pl.BlockSpec((None, UPW, D), lambda j: (j, 0, 0)),
                   o_spec=pl.BlockSpec((None, L, UPW), lambda j: (j, 0, 0)),
                   out_shape=jax.ShapeDtypeStruct((NDEV, L, UPW), F32))


def _slab_specs():
    gate = pl.BlockSpec((None, L, UPW), lambda j: (jnp.minimum(j, NPAIR - 1), 0, 0))
    up = pl.BlockSpec((None, L, UPW), lambda j: (jnp.maximum(j - NPAIR, 0), 0, 0))
    return gate, up


def ffn_up_dx(name, dfg, dfu, wupt):
    def body(g_ref, u_ref, b_ref, o_ref):
        j = pl.program_id(0)

        @pl.when(j == 0)
        def _():
            o_ref[...] = _dot(g_ref[...], b_ref[...])

        @pl.when(jnp.logical_and(j > 0, j < NPAIR))
        def _():
            o_ref[...] += _dot(g_ref[...], b_ref[...])

        @pl.when(j >= NPAIR)
        def _():
            o_ref[...] += _dot(u_ref[...], b_ref[...])

    gate, up = _slab_specs()
    return pl.pallas_call(
        body, name=name, out_shape=jax.ShapeDtypeStruct((L, D), F32), grid=(NDEV,),
        in_specs=[gate, up, pl.BlockSpec((None, UPW, D), lambda j: (j, 0, 0))],
        out_specs=pl.BlockSpec((L, D), lambda j: (0, 0)), compiler_params=_params(VMEM_BIG))(dfg, dfu, wupt)


def ffn_up_dw(name, u2, dfg, dfu):
    tdot = lambda a, b: lax.dot_general(a, b, (((0,), (0,)), ((), ())), preferred_element_type=F32)

    def body(a_ref, g_ref, u_ref, o_ref):
        j = pl.program_id(0)

        @pl.when(j < NPAIR)
        def _():
            o_ref[...] = tdot(g_ref[...], a_ref[...]).astype(o_ref.dtype)

        @pl.when(j >= NPAIR)
        def _():
            o_ref[...] = tdot(u_ref[...], a_ref[...]).astype(o_ref.dtype)

    gate, up = _slab_specs()
    return pl.pallas_call(
        body, name=name, out_shape=jax.ShapeDtypeStruct((NDEV, UPW, D), BF16), grid=(NDEV,),
        in_specs=[pl.BlockSpec((L, D), lambda j: (0, 0)), gate, up],
        out_specs=pl.BlockSpec((None, UPW, D), lambda j: (j, 0, 0)), compiler_params=_params(VMEM_BIG))(u2, dfg, dfu)


def ffn_down_fwd(name, a, wdn, res, tn=256):
    def body(a_ref, b_ref, r_ref, o_ref):
        acc = jnp.zeros((L, tn), F32)
        for g in range(NPAIR):
            bv = b_ref[2 * g:2 * g + 2]
            acc = acc + _dot(a_ref[g], bv.reshape(2 * DNR, tn))
        rows = lax.broadcasted_iota(jnp.int32, (L, 1), 0)
        o_ref[...] = r_ref[...] + jnp.where(rows >= PAD, acc, 0.0)

    return pl.pallas_call(
        body, name=name, out_shape=jax.ShapeDtypeStruct((L, D), F32), grid=(D // tn,),
        in_specs=[pl.BlockSpec((NPAIR, L, UPW), lambda j: (0, 0, 0)),
                  pl.BlockSpec((NDEV, DNR, tn), lambda j: (0, 0, j)),
                  pl.BlockSpec((L, tn), lambda j: (0, j))],
        out_specs=pl.BlockSpec((L, tn), lambda j: (0, j)),
        compiler_params=_params(VMEM_BIG))(a, wdn, res)


def ffn_down_dx(name, dh, wdn):
    return _mm(name, dh, wdn, grid=(NPAIR,), nt=True,
               a_spec=pl.BlockSpec((L, D), lambda g: (0, 0)),
               b_spec=pl.BlockSpec((2, DNR, D), lambda g: (g, 0, 0)),
               o_spec=pl.BlockSpec((None, L, UPW), lambda g: (g, 0, 0)),
               out_shape=jax.ShapeDtypeStruct((NPAIR, L, UPW), F32))


def ffn_down_dw(name, a, dh):
    return _mm(name, a, dh, grid=(NPAIR,), ta=True,
               a_spec=pl.BlockSpec((None, L, UPW), lambda g: (g, 0, 0)),
               b_spec=pl.BlockSpec((L, D), lambda g: (0, 0)),
               o_spec=pl.BlockSpec((UPW, D), lambda g: (g, 0)),
               out_shape=jax.ShapeDtypeStruct((D_FF, D), BF16))


_SLOPES = [2.0 ** (-8.0 * (h + 1) / NQ) for h in range(NQ)]
_SCALE = HD ** -0.5


QR = GQA * T
KC = 3 * T


def _attn_geometry(g, n, sink_ref):
    row = lax.broadcasted_iota(jnp.int32, (QR, KC), 0)
    col = lax.broadcasted_iota(jnp.int32, (QR, KC), 1)
    i = row & (T - 1)
    j = col & (T - 1)
    blk = col // T
    d_meta = n * T + i - j
    ok_meta = (j >= PAD) & (d_meta >= 0)
    ok_prev = j > i + jnp.where(n >= 2, 0, T)
    ok_cur = j <= i - jnp.where(n >= 1, 0, T)
    ok = ((blk == 0) & ok_meta) | ((blk == 1) & ok_prev) | ((blk == 2) & ok_cur)
    dist = jnp.where(blk == 0, jnp.minimum(d_meta, T), jnp.where(blk == 1, T + i - j, i - j)).astype(F32)
    head = lax.broadcasted_iota(jnp.int32, (QR, 1), 0) // T
    slope = jnp.zeros((QR, 1), F32)
    sink = jnp.zeros((QR, 1), F32)
    for hh in range(GQA):
        slope = jnp.where(head == hh, jnp.where(g == 0, _SLOPES[hh], _SLOPES[GQA + hh]), slope)
        sink = jnp.where(head == hh, sink_ref[g * GQA + hh], sink)
    return ok, slope * dist, sink


def _attn_probs(qn, keys, geo):
    ok, penalty, sink = geo
    s = jnp.where(ok, _dot(qn, keys, nt=True) * _SCALE - penalty, NEG)
    m = jnp.maximum(jnp.max(s, axis=-1, keepdims=True), sink)
    e = jnp.exp(s - m)
    e_sink = jnp.exp(sink - m)
    inv = 1.0 / (e.sum(axis=-1, keepdims=True) + e_sink)
    return e, inv, e_sink * inv


def _rms_rows(x, g):
    r = lax.rsqrt(jnp.mean(x * x, axis=-1, keepdims=True) + RMS_EPS)
    xh = x * r
    return xh * g, xh, r


def _rms_rows_bwd(dy, g, xh, r):
    dxh = dy * g
    return r * (dxh - xh * jnp.mean(dxh * xh, axis=-1, keepdims=True))


def _rows3(m_ref, p_ref, c_ref):
    return jnp.concatenate([m_ref[...], p_ref[...], c_ref[...]], axis=0)


_ATT_K, _ATT_V = 4, 5


def _nat_specs():
    qspec = pl.BlockSpec((T, GQA * HD), lambda n, g: (n, g))
    kv = lambda col: (pl.BlockSpec((T, 2 * HD), lambda n, g: (n, col)),
                      pl.BlockSpec((T, 2 * HD), lambda n, g: (jnp.maximum(n - 1, 0), col)),
                      pl.BlockSpec((T, 2 * HD), lambda n, g: (0, col)))
    return qspec, kv(_ATT_K), kv(_ATT_V)


def _group_lanes(g, x):
    return jnp.where(g == 0, x[:, :HD], x[:, HD:])


def _stack_heads(x):
    return jnp.concatenate([x[:, hh * HD:(hh + 1) * HD] for hh in range(GQA)], axis=0)


def _unstack_heads(x):
    return jnp.concatenate([x[hh * T:(hh + 1) * T] for hh in range(GQA)], axis=1)


def attn_fwd(proj, qg, kg, sinks, name):
    def body(sink_ref, q_ref, kc_ref, kp_ref, km_ref, vc_ref, vp_ref, vm_ref, qg_ref, kg_ref, o_ref):
        n = pl.program_id(0)
        g = pl.program_id(1)
        geo = _attn_geometry(g, n, sink_ref)
        keys = _bf(_rms_rows(_group_lanes(g, _rows3(km_ref, kp_ref, kc_ref)), kg_ref[...])[0])
        vals = _bf(_group_lanes(g, _rows3(vm_ref, vp_ref, vc_ref)))
        qn = _bf(_rms_rows(_stack_heads(q_ref[...]), qg_ref[...])[0])
        e, inv, _ = _attn_probs(qn, keys, geo)
        o_ref[...] = _unstack_heads(_dot(_bf(e), vals) * inv)

    qspec, (kc, kp, km), (vc, vp, vm) = _nat_specs()
    vec = pl.BlockSpec((1, HD), lambda n, g: (0, 0))
    return pl.pallas_call(
        body, name=name, out_shape=jax.ShapeDtypeStruct((L, NQ * HD), F32), grid=(NB, NKV),
        in_specs=[pl.BlockSpec(memory_space=pltpu.SMEM), qspec, kc, kp, km, vc, vp, vm, vec, vec],
        out_specs=qspec)(sinks, proj, proj, proj, proj, proj, proj, proj, qg, kg)


def attn_bwd(proj, do, qg, kg, sinks, name):
    def body(sink_ref, q_ref, kc_ref, kp_ref, km_ref, vc_ref, vp_ref, vm_ref, do_ref, qg_ref, kg_ref,
             dq_ref, dkc_ref, dkp_ref, dkm_ref, dvc_ref, dvp_ref, dvm_ref, dqg_ref, dkg_ref, dsk_ref):
        n = pl.program_id(0)
        g = pl.program_id(1)

        @pl.when(jnp.logical_and(g == 0, n == 0))
        def _():
            for r in (dkm_ref, dvm_ref, dqg_ref, dkg_ref, dsk_ref):
                r[...] = jnp.zeros_like(r)

        geo = _attn_geometry(g, n, sink_ref)
        kgv = kg_ref[...]
        qgv = qg_ref[...]
        kn_f, kh, kr = _rms_rows(_group_lanes(g, _rows3(km_ref, kp_ref, kc_ref)), kgv)
        keys = _bf(kn_f)
        vals = _bf(_group_lanes(g, _rows3(vm_ref, vp_ref, vc_ref)))
        qn_f, qh, qr = _rms_rows(_stack_heads(q_ref[...]), qgv)
        qn = _bf(qn_f)
        e, inv, p_sink = _attn_probs(qn, keys, geo)
        dob = _bf(_stack_heads(do_ref[...]))
        p = e * inv
        dp = _dot(dob, vals, nt=True)
        delta = (p * dp).sum(axis=-1, keepdims=True)
        head_row = lax.broadcasted_iota(jnp.int32, (NQ, 128), 0)
        dsk = jnp.zeros((NQ, 128), F32)
        for hh in range(GQA):
            part = -jnp.sum((p_sink * delta)[hh * T:(hh + 1) * T], axis=0, keepdims=True)
            dsk = jnp.where(head_row == g * GQA + hh, part, dsk)
        dsk_ref[...] += dsk
        ds = p * (dp - delta)
        dqn = _dot(_bf(ds), keys) * _SCALE
        dq_ref[...] = _unstack_heads(_rms_rows_bwd(dqn, qgv, qh, qr))
        dqg_ref[...] += jnp.sum(dqn * qh, axis=0, keepdims=True)
        dkn = _dot(_bf(ds.T), qn) * _SCALE
        dkg_ref[...] += jnp.sum(dkn * kh, axis=0, keepdims=True)
        dk_all = _rms_rows_bwd(dkn, kgv, kh, kr)
        dv_all = _dot(_bf(p.T), dob)

        for gg in range(NKV):
            @pl.when(g == gg)
            def _(gg=gg):
                lanes = slice(gg * HD, (gg + 1) * HD)
                dkm_ref[:, lanes] += dk_all[0:T]
                dkp_ref[:, lanes] = dk_all[T:2 * T]
                dkc_ref[:, lanes] = dk_all[2 * T:3 * T]
                dvm_ref[:, lanes] += dv_all[0:T]
                dvp_ref[:, lanes] = dv_all[T:2 * T]
                dvc_ref[:, lanes] = dv_all[2 * T:3 * T]

    qspec, (kc, kp, km), (vc, vp, vm) = _nat_specs()
    vec = pl.BlockSpec((1, HD), lambda n, g: (0, 0))
    cur = pl.BlockSpec((T, 2 * HD), lambda n, g: (n, 0))
    meta = pl.BlockSpec((T, 2 * HD), lambda n, g: (0, 0))
    kv_shape = jax.ShapeDtypeStruct((L, 2 * HD), F32)
    meta_shape = jax.ShapeDtypeStruct((T, 2 * HD), F32)
    vec_shape = jax.ShapeDtypeStruct((1, HD), F32)
    return pl.pallas_call(
        body, name=name,
        out_shape=(jax.ShapeDtypeStruct((L, NQ * HD), F32), kv_shape, kv_shape, meta_shape, kv_shape, kv_shape,
                   meta_shape, vec_shape, vec_shape, jax.ShapeDtypeStruct((NQ, 128), F32)),
        grid=(NB, NKV),
        in_specs=[pl.BlockSpec(memory_space=pltpu.SMEM), qspec, kc, kp, km, vc, vp, vm, qspec, vec, vec],
        out_specs=(qspec, cur, cur, meta, cur, cur, meta, vec, vec,
                   pl.BlockSpec((NQ, 128), lambda n, g: (0, 0))))(
            sinks, proj, proj, proj, proj, proj, proj, proj, do, qg, kg)


RW = RH * HD


def _ret_tables():
    h = np.arange(RH, dtype=np.float64)
    lg = np.log1p(-np.exp2(-5.0 - h))
    idx = np.arange(T, dtype=np.float64)
    diff = idx[:, None] - idx[None, :]
    decay = np.where(diff[None] >= 0, np.exp(np.maximum(diff, 0.0)[None] * lg[:, None, None]), 0.0)
    zeta = np.exp((T - 1 - idx)[None, :] * lg[:, None])
    xi = np.exp((idx + 1.0)[None, :] * lg[:, None])
    cd = np.exp(T * lg)
    lanes = lambda a: np.repeat(a.T, HD, axis=1)
    head_of = np.arange(RW) // HD
    same = (head_of[:, None] == head_of[None, :]).astype(np.float64)
    f = lambda a: jnp.asarray(a, F32)
    return dict(decay=f(decay), zeta=f(lanes(zeta)), xi=f(lanes(xi)), cd=f(np.repeat(cd, HD)[None, :]),
                head=f((head_of[None, :] == np.arange(RH)[:, None]).astype(np.float64)[:, None, :]),
                same=f(same), avg=jnp.asarray(same / HD, BF16))


def _seg_mean(x, avg):
    hi = _bf(x)
    lo = _bf(x - hi.astype(F32))
    return _dot(hi, avg) + _dot(lo, avg)


def _ret_specs(col0, order):
    return lambda col: pl.BlockSpec((T, RW), lambda i: (order(i), col0 + col))


def _ret_chunk(q, kf, v, s, tb):
    dec, xi, head = tb
    vb = _bf(v)
    kb = _bf(kf)
    y = _dot(_bf(q * xi), _bf(s))
    a = []
    for h in range(RH):
        a.append(_dot(_bf(q * head[h]), kb, nt=True) * dec[h])
        y = y + head[h] * _dot(_bf(a[h]), vb)
    return a, y


def _gn_rows(y):
    mu = jnp.mean(y, axis=-1, keepdims=True)
    yc = y - mu
    rstd = lax.rsqrt(jnp.mean(yc * yc, axis=-1, keepdims=True) + LN_EPS)
    return yc * rstd, rstd


_RET_COL0 = _RET_Q = 5


def _ret_consts(tb):
    names = ("decay", "zeta", "xi", "cd", "head", "same", "avg")
    full = lambda a: pl.BlockSpec(a.shape, lambda i: (0,) * a.ndim)
    return [tb[n] for n in names], [full(tb[n]) for n in names]


def _ret_inputs(n, q_ref, k_ref, v_ref):
    rows = n * T + lax.broadcasted_iota(jnp.int32, (T, 1), 0)
    valid = rows >= PAD
    return q_ref[...], jnp.where(valid, k_ref[...] * (HD ** -0.5), 0.0), v_ref[...], valid


def ret_fwd(proj, gng, tb, name):
    def body(q_ref, k_ref, v_ref, g_ref, dec_ref, zeta_ref, xi_ref, cd_ref, head_ref, same_ref, avg_ref, gng_ref,
             y_ref, st_ref, s_scr):
        n = pl.program_id(0)

        @pl.when(n == 0)
        def _():
            s_scr[...] = jnp.zeros_like(s_scr)

        s = s_scr[...]
        st_ref[...] = s
        q, kf, v, _ = _ret_inputs(n, q_ref, k_ref, v_ref)
        _, y = _ret_chunk(q, kf, v, s, (dec_ref, xi_ref[...], head_ref))
        s_scr[...] = cd_ref[...] * s + same_ref[...] * _dot(_bf((kf * zeta_ref[...]).T), _bf(v))
        avg = avg_ref[...]
        yc = y - _seg_mean(y, avg)
        yh = yc * lax.rsqrt(_seg_mean(yc * yc, avg) + LN_EPS)
        gv = g_ref[...]
        y_ref[...] = gv * _sig(gv) * (yh * gng_ref[...])

    col = _ret_specs(_RET_COL0, lambda i: i)
    consts, cspecs = _ret_consts(tb)
    return pl.pallas_call(
        body, name=name,
        out_shape=(jax.ShapeDtypeStruct((L, RW), F32), jax.ShapeDtypeStruct((NB, RW, RW), F32)),
        grid=(NB,),
        in_specs=[col(0), col(1), col(2), col(3)] + cspecs + [pl.BlockSpec((1, RW), lambda i: (0, 0))],
        out_specs=(pl.BlockSpec((T, RW), lambda i: (i, 0)), pl.BlockSpec((None, RW, RW), lambda i: (i, 0, 0))),
        scratch_shapes=[pltpu.VMEM((RW, RW), F32)])(proj, proj, proj, proj, *consts, gng)


def ret_bwd(proj, dycat, states, gng, tb, name):
    def body(q_ref, k_ref, v_ref, g_ref, do_ref, st_ref, dec_ref, zeta_ref, xi_ref, cd_ref, head_ref, same_ref,
             avg_ref, gng_ref, d_ref, dgn_ref, ds_scr):
        i = pl.program_id(0)
        n = NB - 1 - i

        @pl.when(i == 0)
        def _():
            ds_scr[...] = jnp.zeros_like(ds_scr)

        dsn = _bf(ds_scr[...])
        s = st_ref[...]
        sb = _bf(s)
        q, kf, v, valid = _ret_inputs(n, q_ref, k_ref, v_ref)
        xi, zeta, avg = xi_ref[...], zeta_ref[...], avg_ref[...]
        a, y = _ret_chunk(q, kf, v, s, (dec_ref, xi, head_ref))
        yc = y - _seg_mean(y, avg)
        rstd = lax.rsqrt(_seg_mean(yc * yc, avg) + LN_EPS)
        yh = yc * rstd
        gv = g_ref[...]
        sg = _sig(gv)
        sil = gv * sg
        gn = gng_ref[...]
        dout = do_ref[...]
        d_ref[:, 3 * RW:4 * RW] = dout * (yh * gn) * (sg * (1.0 + gv * (1.0 - sg)))
        dyh = dout * sil * gn
        part = jnp.sum(dout * sil * yh, axis=0, keepdims=True)

        @pl.when(i == 0)
        def _():
            dgn_ref[...] = part

        @pl.when(i > 0)
        def _():
            dgn_ref[...] += part

        dy = rstd * (dyh - _seg_mean(dyh, avg) - yh * _seg_mean(dyh * yh, avg))
        dyb = _bf(dy)
        vb, kb, qb = _bf(v), _bf(kf), _bf(q)
        dq = _dot(dyb, sb, nt=True) * xi
        dkf = _dot(vb, dsn, nt=True) * zeta
        dv = _dot(_bf(kf * zeta), dsn)
        for h in range(RH):
            m = head_ref[h]
            da = _dot(_bf(dy * m), vb, nt=True) * dec_ref[h]
            dv = dv + m * _dot(_bf(a[h].T), dyb)
            dq = dq + m * _dot(_bf(da), kb)
            dkf = dkf + m * _dot(_bf(da.T), qb)
        d_ref[:, 0:RW] = dq
        d_ref[:, RW:2 * RW] = jnp.where(valid, dkf * (HD ** -0.5), 0.0)
        d_ref[:, 2 * RW:3 * RW] = dv
        ds_scr[...] = cd_ref[...] * ds_scr[...] + same_ref[...] * _dot(_bf((q * xi).T), dyb)

    back = lambda i: NB - 1 - i
    col = _ret_specs(_RET_COL0, back)
    consts, cspecs = _ret_consts(tb)
    vec = pl.BlockSpec((1, RW), lambda i: (0, 0))
    return pl.pallas_call(
        body, name=name,
        out_shape=(jax.ShapeDtypeStruct((L, 4 * RW), F32), jax.ShapeDtypeStruct((1, RW), F32)),
        grid=(NB,),
        in_specs=[col(0), col(1), col(2), col(3), _ret_specs(3, back)(0),
                  pl.BlockSpec((None, RW, RW), lambda i: (back(i), 0, 0))] + cspecs + [vec],
        out_specs=(pl.BlockSpec((T, 4 * RW), lambda i: (back(i), 0)), vec),
        scratch_shapes=[pltpu.VMEM((RW, RW), F32)])(proj, proj, proj, proj, dycat, states, *consts, gng)


HALO = 32
TAP0 = HALO - (CONV_K - 1)


def _conv_specs():
    cur = lambda col=0: pl.BlockSpec((T, CV), lambda c: (c, col))
    before = lambda col=0: pl.BlockSpec((HALO, CV), lambda c: (jnp.maximum(c * (T // HALO) - 1, 0), col))
    after = pl.BlockSpec((HALO, CV), lambda c: (jnp.minimum((c + 1) * (T // HALO), L // HALO - 1), 0))
    full = lambda r, w: pl.BlockSpec((r, w), lambda c: (0, 0))
    return cur, before, after, full


_CONV_A, _CONV_B = 3, 4
_DY_CONV = 2


def _conv_post(xc, lg_ref, lb_ref):
    xh, rstd = _gn_rows(xc)
    z = xh * lg_ref[...] + lb_ref[...]
    return xh, rstd, z, _sig(z)


def conv_fwd(proj, w, b, lg, lb, pw, og, name):
    def body(ca_ref, cah_ref, cb_ref, cbh_ref, w_ref, b_ref, lg_ref, lb_ref, pw_ref, og_ref, y_ref, xc_ref, u_scr):
        c = pl.program_id(0)
        u_scr[0:HALO, :] = jnp.where(c > 0, cah_ref[...] * _sig(cbh_ref[...]), 0.0)
        u_scr[HALO:HALO + T, :] = ca_ref[...] * _sig(cb_ref[...])
        acc = jnp.zeros((T, CV), F32)
        for k in range(CONV_K):
            acc = acc + w_ref[k:k + 1, :] * u_scr[TAP0 + k:TAP0 + k + T, :]
        xc = acc + b_ref[...]
        xc_ref[...] = xc
        _, _, z, sg = _conv_post(xc, lg_ref, lb_ref)
        zp = _dot(_bf(z * sg), pw_ref[...])
        y_ref[...] = _rms_rows(zp, og_ref[...])[0]

    cur, before, _, full = _conv_specs()
    vec = full(1, CV)
    seq = jax.ShapeDtypeStruct((L, CV), F32)
    return pl.pallas_call(
        body, name=name, out_shape=(seq, seq), grid=(NB,),
        in_specs=[cur(_CONV_A), before(_CONV_A), cur(_CONV_B), before(_CONV_B), full(CONV_K, CV), vec, vec, vec,
                  full(CV, CV), vec],
        out_specs=(cur(), cur()),
        scratch_shapes=[pltpu.VMEM((HALO + T, CV), F32)])(proj, proj, proj, proj, w, b, lg, lb, pw, og)


def conv_bwd(proj, xc, dycat, w, lg, lb, pw, og, name):
    def body1(xc_ref, dy_ref, lg_ref, lb_ref, pw_ref, og_ref, dxc_ref, db_ref, dlg_ref, dlb_ref, dog_ref, dpw_ref):
        c = pl.program_id(0)

        @pl.when(c == 0)
        def _():
            for r in (db_ref, dlg_ref, dlb_ref, dog_ref, dpw_ref):
                r[...] = jnp.zeros_like(r)

        xh, rstd, z, sg = _conv_post(xc_ref[...], lg_ref, lb_ref)
        s = z * sg
        zp = _dot(_bf(s), pw_ref[...])
        ogv = og_ref[...]
        _, zph, r2 = _rms_rows(zp, ogv)
        dyv = dy_ref[...]
        dog_ref[...] += jnp.sum(dyv * zph, axis=0, keepdims=True)
        dzpb = _bf(_rms_rows_bwd(dyv, ogv, zph, r2))
        dpw_ref[...] += _dot(_bf(s.T), dzpb)
        dz = _dot(dzpb, pw_ref[...], nt=True) * (sg * (1.0 + z * (1.0 - sg)))
        dlg_ref[...] += jnp.sum(dz * xh, axis=0, keepdims=True)
        dlb_ref[...] += jnp.sum(dz, axis=0, keepdims=True)
        dxh = dz * lg_ref[...]
        dxc = rstd * (dxh - jnp.mean(dxh, axis=-1, keepdims=True) - xh * jnp.mean(dxh * xh, axis=-1, keepdims=True))
        db_ref[...] += jnp.sum(dxc, axis=0, keepdims=True)
        dxc_ref[...] = dxc

    def body2(dx_ref, dxa_ref, ca_ref, cb_ref, w_ref, dca_ref, dcb_ref, dw_ref, d_scr):
        c = pl.program_id(0)

        @pl.when(c == 0)
        def _():
            dw_ref[...] = jnp.zeros_like(dw_ref)

        d_scr[0:T, :] = dx_ref[...]
        d_scr[T:T + HALO, :] = jnp.where(c < NB - 1, dxa_ref[...], 0.0)
        ca = ca_ref[...]
        sg = _sig(cb_ref[...])
        u = ca * sg
        du = jnp.zeros((T, CV), F32)
        for k in range(CONV_K):
            off = CONV_K - 1 - k
            dj = d_scr[off:off + T, :]
            du = du + w_ref[k:k + 1, :] * dj
            dw_ref[k:k + 1, :] += jnp.sum(u * dj, axis=0, keepdims=True)
        dca_ref[...] = du * sg
        dcb_ref[...] = du * ca * sg * (1.0 - sg)

    cur, _, after, full = _conv_specs()
    seq = jax.ShapeDtypeStruct((L, CV), F32)
    vsh = jax.ShapeDtypeStruct((1, CV), F32)
    vec = full(1, CV)
    dxc, db, dlg, dlb, dog, dpw = pl.pallas_call(
        body1, name=name + "_a",
        out_shape=(seq, vsh, vsh, vsh, vsh, jax.ShapeDtypeStruct((CV, CV), F32)),
        grid=(NB,),
        in_specs=[cur(), cur(_DY_CONV), vec, vec, full(CV, CV), vec],
        out_specs=(cur(), vec, vec, vec, vec, full(CV, CV)))(xc, dycat, lg, lb, pw, og)
    dca, dcb, dw = pl.pallas_call(
        body2, name=name + "_b", out_shape=(seq, seq, jax.ShapeDtypeStruct((CONV_K, CV), F32)), grid=(NB,),
        in_specs=[cur(), after, cur(_CONV_A), cur(_CONV_B), full(CONV_K, CV)],
        out_specs=(cur(), cur(), full(CONV_K, CV)),
        scratch_shapes=[pltpu.VMEM((T + HALO, CV), F32)])(dxc, dxc, proj, proj, w)
    return dca, dcb, dw, db, dlg, dlb, dog, dpw


HR = 8


def _ffn_specs():
    cur = lambda off: pl.BlockSpec((None, T, UPW), lambda j, c: (j + off, c, 0))
    before = lambda off: pl.BlockSpec((None, HR, UPW), lambda j, c: (j + off, jnp.maximum(c * (T // HR) - 1, 0), 0))
    after = lambda off: pl.BlockSpec(
        (None, HR, UPW), lambda j, c: (j + off, jnp.minimum((c + 1) * (T // HR), L // HR - 1), 0))
    wspec = lambda off, r: pl.BlockSpec((None, r, UPW), lambda j, c: (j + off, 0, 0))
    return cur, before, after, wspec


def ffn_act_fwd(f, w, b, name):
    strip = 16

    def body(fg_ref, fgh_ref, fu_ref, fuh_ref, wg_ref, wu_ref, bg_ref, bu_ref, a_ref, gc_ref, uc_ref, hg_scr, hu_scr):
        c = pl.program_id(1)
        for scr, x_ref, xh_ref in ((hg_scr, fg_ref, fgh_ref), (hu_scr, fu_ref, fuh_ref)):
            scr[0:HR, :] = jnp.where(c > 0, xh_ref[...], 0.0)
            scr[HR:HR + strip, :] = x_ref[0:strip, :]
        wg = [wg_ref[k:k + 1, :] for k in range(FFN_K)]
        wu = [wu_ref[k:k + 1, :] for k in range(FFN_K)]
        bg, bu = bg_ref[...], bu_ref[...]

        def conv(src, base, w, b):
            acc = b
            for k in range(FFN_K):
                o = base - (FFN_K - 1) + k
                acc = acc + w[k] * src[o:o + strip, :]
            return acc

        for s in range(T // strip):
            r = s * strip
            gate = conv(hg_scr, HR, wg, bg) if s == 0 else conv(fg_ref, r, wg, bg)
            up = conv(hu_scr, HR, wu, bu) if s == 0 else conv(fu_ref, r, wu, bu)
            gc_ref[r:r + strip, :] = gate
            uc_ref[r:r + strip, :] = up
            a_ref[r:r + strip, :] = (gate * _sig(gate) * up).astype(BF16)

    cur, before, _, wspec = _ffn_specs()
    pre = jax.ShapeDtypeStruct((NPAIR, L, UPW), F32)
    return pl.pallas_call(
        body, name=name, out_shape=(jax.ShapeDtypeStruct((NPAIR, L, UPW), BF16), pre, pre), grid=(NPAIR, NB),
        in_specs=[cur(0), before(0), cur(NPAIR), before(NPAIR), wspec(0, FFN_K), wspec(NPAIR, FFN_K),
                  wspec(0, 1), wspec(NPAIR, 1)],
        out_specs=(cur(0), cur(0), cur(0)),
        scratch_shapes=[pltpu.VMEM((HR + strip, UPW), F32), pltpu.VMEM((HR + strip, UPW), F32)])(
            f, f, f, f, w, w, b, b)


def ffn_act_bwd(f, gc, uc, da, w, name):
    ext = T + HR
    sub = 8

    def body(fg_ref, fu_ref, gc_ref, gca_ref, uc_ref, uca_ref, da_ref, daa_ref, wg_ref, wu_ref,
             dfg_ref, dfu_ref, dwg_ref, dwu_ref, dbg_ref, dbu_ref, dg_scr, du_scr):
        c = pl.program_id(1)
        wg = [wg_ref[k:k + 1, :] for k in range(FFN_K)]
        wu = [wu_ref[k:k + 1, :] for k in range(FFN_K)]

        for s in range(ext // sub):
            r = s * sub
            if r < T:
                gate, up, dav = gc_ref[r:r + sub, :], uc_ref[r:r + sub, :], da_ref[r:r + sub, :]
            else:
                gate, up, dav = gca_ref[...], uca_ref[...], jnp.where(c < NB - 1, daa_ref[...], 0.0)
            sg = _sig(gate)
            dg_scr[r:r + sub, :] = dav * up * (sg * (1.0 + gate * (1.0 - sg)))
            du_scr[r:r + sub, :] = dav * gate * sg

        def back(d_scr, x_ref, w, df_ref, dw_ref, db_ref):
            accs = [jnp.zeros((sub, UPW), F32) for _ in range(FFN_K + 1)]
            for s in range(T // (2 * sub)):
                pieces = []
                for r in (2 * s * sub, (2 * s + 1) * sub):
                    xv = x_ref[r:r + sub, :]
                    df = jnp.zeros((sub, UPW), F32)
                    for k in range(FFN_K):
                        off = FFN_K - 1 - k
                        dj = d_scr[r + off:r + off + sub, :]
                        df = df + w[k] * dj
                        accs[k] = accs[k] + xv * dj
                        if off == 0:
                            accs[FFN_K] = accs[FFN_K] + dj
                    pieces.append(df)
                df_ref[2 * s * sub:2 * (s + 1) * sub, :] = jnp.concatenate(pieces, axis=0).astype(BF16)
            dwp = _stack_rows([jnp.sum(a, axis=0, keepdims=True) for a in accs[:FFN_K]])
            dbp = jnp.sum(accs[FFN_K], axis=0, keepdims=True)

            @pl.when(c == 0)
            def _():
                dw_ref[...] = dwp
                db_ref[...] = dbp

            @pl.when(c > 0)
            def _():
                dw_ref[...] += dwp
                db_ref[...] += dbp

        back(dg_scr, fg_ref, wg, dfg_ref, dwg_ref, dbg_ref)
        back(du_scr, fu_ref, wu, dfu_ref, dwu_ref, dbu_ref)

    cur, _, after, wspec = _ffn_specs()
    slab = jax.ShapeDtypeStruct((NPAIR, L, UPW), BF16)
    wsh = jax.ShapeDtypeStruct((NPAIR, FFN_K, UPW), F32)
    bsh = jax.ShapeDtypeStruct((NPAIR, 1, UPW), F32)
    return pl.pallas_call(
        body, name=name, out_shape=(slab, slab, wsh, wsh, bsh, bsh), grid=(NPAIR, NB),
        in_specs=[cur(0), cur(NPAIR), cur(0), after(0), cur(0), after(0), cur(0), after(0),
                  wspec(0, FFN_K), wspec(NPAIR, FFN_K)],
        out_specs=(cur(0), cur(0), wspec(0, FFN_K), wspec(0, FFN_K), wspec(0, 1), wspec(0, 1)),
        scratch_shapes=[pltpu.VMEM((ext, UPW), F32), pltpu.VMEM((ext, UPW), F32)])(
            f, f, gc, gc, uc, uc, da, da, w, w)


def _mesh_pos():
    return lax.axis_index("x"), lax.axis_index("y"), lax.axis_index("c")


def _peer(pos, k):
    x, y, c = pos
    px = 1 - x if k & 4 else x
    py = 1 - y if k & 2 else y
    pc = 1 - c if k & 1 else c
    return (px, py, pc), 4 * px + 2 * py + pc


_CHIP_FLIPS = (4, 2, 6)
_HBM = pl.BlockSpec(memory_space=pl.ANY)


def all_gather(shards, name):
    nt = len(shards)

    def body(*refs):
        ins, outs = refs[:nt], refs[nt:2 * nt]
        send, recv, local = refs[2 * nt:]
        pos = _mesh_pos()
        me = 4 * pos[0] + 2 * pos[1] + pos[2]
        sib, sib_id = _peer(pos, 1)

        def copy(t, k, block_id, to, src=None):
            dst = outs[t].at[block_id]
            return pltpu.make_async_remote_copy(
                src_ref=dst if src is None else src, dst_ref=dst, send_sem=send.at[t, k], recv_sem=recv.at[t, k],
                device_id=to, device_id_type=MESH)

        locals_ = [pltpu.make_async_copy(ins[t], outs[t].at[me], local.at[t]) for t in range(nt)]
        for cp in locals_:
            cp.start()
        started = []
        for j, flip in enumerate(_CHIP_FLIPS):
            for t in range(nt):
                started.append(copy(t, 1 + j, me, _peer(pos, flip)[0], src=ins[t]))
        for t in range(nt):
            started.append(copy(t, 0, me, sib, src=ins[t]))
        for cp in started:
            cp.start()
        for j, flip in enumerate(_CHIP_FLIPS):
            _, pid = _peer(pos, flip)
            for t in range(nt):
                copy(t, 1 + j, pid, sib).wait_recv()
                fwd = copy(t, 4 + j, pid, sib)
                fwd.start()
                started.append(fwd)
        for t in range(nt):
            copy(t, 0, sib_id, sib).wait_recv()
        for j, flip in enumerate(_CHIP_FLIPS):
            _, pid = _peer(pos, flip | 1)
            for t in range(nt):
                copy(t, 4 + j, pid, sib).wait_recv()
        for cp in started:
            cp.wait_send()
        for cp in locals_:
            cp.wait()

    return pl.pallas_call(
        body, name=name,
        out_shape=tuple(jax.ShapeDtypeStruct((NDEV,) + s.shape, s.dtype) for s in shards),
        in_specs=[_HBM] * nt, out_specs=tuple([_HBM] * nt),
        scratch_shapes=[pltpu.SemaphoreType.DMA((nt, NDEV - 1)), pltpu.SemaphoreType.DMA((nt, NDEV - 1)),
                        pltpu.SemaphoreType.DMA((nt,))])(*shards)


_SPLIT = dict(has_side_effects=pltpu.SideEffectType.DATAFLOW_SIDE_EFFECTING)
_SEM = pl.BlockSpec(memory_space=pltpu.SEMAPHORE)


def _exchange_start(name, srcs, lands, copies, after):
    ns, nl = len(srcs), len(lands)
    ncopy = len(copies(None, [None] * ns, [None] * nl))

    def body(*refs):
        src_refs, land_refs = refs[:ns], refs[ns:ns + nl]
        send, recv = refs[ns + nl + len(after):ns + nl + len(after) + 2]
        token = refs[-1]
        for i, (src, dst, peer) in enumerate(copies(_mesh_pos(), src_refs, land_refs)):
            pltpu.make_async_remote_copy(src_ref=src, dst_ref=dst, send_sem=send.at[i], recv_sem=recv.at[i],
                                         device_id=peer, device_id_type=MESH).start()
        token[...] = jnp.zeros_like(token)

    hbm = lambda a: pltpu.HBM(a.shape, a.dtype)
    out = pl.pallas_call(
        body, name=name,
        out_shape=(pltpu.SemaphoreType.DMA((ncopy,)), pltpu.SemaphoreType.DMA((ncopy,)),
                   *[hbm(a) for a in srcs], *[hbm(a) for a in lands], jax.ShapeDtypeStruct((8, 128), F32)),
        in_specs=[pl.BlockSpec(memory_space=pltpu.HBM)] * (ns + nl) + [_HBM] * len(after),
        out_specs=(_SEM, _SEM, *[pl.BlockSpec(memory_space=pltpu.HBM)] * (ns + nl),
                   pl.BlockSpec(memory_space=pltpu.VMEM)),
        input_output_aliases={i: 2 + i for i in range(ns + nl)},
        compiler_params=pltpu.CompilerParams(**_SPLIT))(
            *[pltpu.with_memory_space_constraint(a, pltpu.HBM) for a in list(srcs) + list(lands)], *after)
    return out[0], out[1], list(out[2:2 + ns]), list(out[2 + ns:2 + ns + nl]), out[-1]


def _exchange_wait(name, send, recv, srcs, lands, copies, after):
    ns, nl = len(srcs), len(lands)

    def body(*refs):
        src_refs, land_refs = refs[:ns], refs[ns:ns + nl]
        send_ref, recv_ref = refs[ns + nl:ns + nl + 2]
        pos = _mesh_pos()
        for i, (src, dst, peer) in enumerate(copies(pos, src_refs, land_refs, arriving=True)):
            cp = pltpu.make_async_remote_copy(src_ref=src, dst_ref=dst, send_sem=send_ref.at[i], recv_sem=recv_ref.at[i],
                                              device_id=peer, device_id_type=MESH)
            cp.wait_send()
            cp.wait_recv()

    hbm = lambda a: pltpu.HBM(a.shape, a.dtype)
    out = pl.pallas_call(
        body, name=name, out_shape=tuple(hbm(a) for a in list(srcs) + list(lands)),
        in_specs=[pl.BlockSpec(memory_space=pltpu.HBM)] * (ns + nl) + [_SEM, _SEM] + [_HBM] * len(after),
        out_specs=tuple([pl.BlockSpec(memory_space=pltpu.HBM)] * (ns + nl)),
        input_output_aliases={i: i for i in range(ns + nl)},
        compiler_params=pltpu.CompilerParams(**_SPLIT))(*srcs, *lands, send, recv, *after)
    return list(out[:ns]), list(out[ns:])


def _gather_copies(pos, srcs, lands, arriving=False):
    if pos is None:
        return [None] * (len(srcs) * (NDEV - 1))
    me = 4 * pos[0] + 2 * pos[1] + pos[2]
    out = []
    for src, land in zip(srcs, lands):
        for k in range(1, NDEV):
            peer, pid = _peer(pos, k)
            out.append((src, land.at[pid if arriving else me], peer))
    return out


def _scatter_copies(pos, srcs, lands, arriving=False):
    if pos is None:
        return [None] * (len(srcs) * (NDEV - 1))
    out = []
    for src, land in zip(srcs, lands):
        for k in range(1, NDEV):
            peer, pid = _peer(pos, k)
            out.append((src.at[pid], land.at[k - 1], peer))
    return out


def gather_start(name, shards, lands, after=()):
    return _exchange_start(name, shards, lands, _gather_copies, list(after))


def gather_wait(name, handle, after=()):
    send, recv, srcs, lands, _ = handle
    return _exchange_wait(name, send, recv, srcs, lands, _gather_copies, list(after))


def scatter_start(name, grads, after=()):
    lands = [pltpu.with_memory_space_constraint(lax.empty((NDEV - 1,) + g.shape[1:], g.dtype), pltpu.HBM)
             for g in grads]
    return _exchange_start(name, grads, lands, _scatter_copies, list(after))


def scatter_wait(name, handle, after=()):
    send, recv, srcs, lands, _ = handle
    return _exchange_wait(name, send, recv, srcs, lands, _scatter_copies, list(after))


def sum_slots(parts, name):
    def body(p_ref, o_ref):
        acc = p_ref[0]
        for d in range(1, NDEV):
            acc = acc + p_ref[d]
        o_ref[...] = acc

    return pl.pallas_call(body, name=name, out_shape=jax.ShapeDtypeStruct(parts.shape[1:], F32))(parts)


def _adamw_update(g, w_ref, m_ref, v_ref, g_ref, d_ref, nm_ref, nv_ref):
    g_ref[...] = g
    nm = ADAM_B1 * m_ref[...] + (1.0 - ADAM_B1) * g
    nv = ADAM_B2 * v_ref[...] + (1.0 - ADAM_B2) * (g * g)
    nm_ref[...] = nm
    nv_ref[...] = nv
    m_hat = nm / (1.0 - ADAM_B1 ** ADAM_STEP)
    v_hat = nv / (1.0 - ADAM_B2 ** ADAM_STEP)
    d_ref[...] = -ADAM_LR * (m_hat / (jnp.sqrt(v_hat) + ADAM_EPS) + ADAM_WD * w_ref[...])


def adamw_big(own, lands, w, m, v, me, tr, name):
    _, r, c = own[0].shape
    nt = r // tr

    def body(me_ref, *refs):
        ins, (w_ref, m_ref, v_ref), outs = refs[:2 * DEPTH], refs[2 * DEPTH:2 * DEPTH + 3], refs[2 * DEPTH + 3:]
        layer = pl.program_id(0)
        for l in range(DEPTH):
            @pl.when(layer == l)
            def _(l=l):
                g = ins[2 * l][...].astype(F32)
                for s in range(NDEV - 1):
                    g = g + ins[2 * l + 1][s].astype(F32)
                _adamw_update(g, w_ref, m_ref, v_ref, *outs)

    pick = lambda l: (lambda layer, i, me_ref: jnp.where(layer == l, i, 0))
    in_specs = []
    for l in range(DEPTH):
        in_specs.append(pl.BlockSpec((None, tr, c), lambda layer, i, me_ref, f=pick(l): (me_ref[0], f(layer, i, me_ref), 0)))
        in_specs.append(pl.BlockSpec((NDEV - 1, tr, c), lambda layer, i, me_ref, f=pick(l): (0, f(layer, i, me_ref), 0)))
    blk = pl.BlockSpec((tr, c), lambda layer, i, me_ref: (layer * nt + i, 0))
    sh = jax.ShapeDtypeStruct((DEPTH * r, c), F32)
    args = [a for l in range(DEPTH) for a in (own[l], lands[l])]
    return pl.pallas_call(
        body, name=name, out_shape=(sh, sh, sh, sh),
        grid_spec=pltpu.PrefetchScalarGridSpec(
            num_scalar_prefetch=1, grid=(DEPTH, nt), in_specs=in_specs + [blk, blk, blk],
            out_specs=(blk, blk, blk, blk)),
        compiler_params=_params(VMEM_BIG))(me, *args, w, m, v)


def adamw_many(gs, ws, ms, vs, name):
    n = len(gs)

    def body(*refs):
        outs = refs[4 * n:]
        for i in range(n):
            _adamw_update(refs[i][...], refs[n + i], refs[2 * n + i], refs[3 * n + i], *outs[4 * i:4 * i + 4])

    out_shape = tuple(jax.ShapeDtypeStruct(w.shape, F32) for w in ws for _ in range(4))
    res = pl.pallas_call(body, name=name, out_shape=out_shape)(*gs, *ws, *ms, *vs)
    return [res[4 * i:4 * i + 4] for i in range(n)]


def local_step(x, target, P, hooks):
    tb = _ret_tables()
    h = jnp.concatenate([jnp.zeros((PAD, D), F32), P["meta"], x], axis=0)
    stash = []
    for l in range(DEPTH):
        s = {"h": h}
        s["w_in"], s["cv_pw"], s["w_out"] = hooks["mixer_w"](l, h)
        proj, s["u"] = in_proj_fwd(f"in_proj_{l}", h, P["norm_mix_g"][l], s["w_in"], 256,
                                   after=hooks["first_after"] if l == 0 else None)
        s["proj"] = proj
        s["o"] = attn_fwd(proj, P["q_norm_g"][l], P["k_norm_g"][l], P["attn_sinks"][l], f"attn_fwd_{l}")
        y_att = rms_fwd(s["o"], P["attn_out_g"][l], BF16, f"rms_att_{l}")
        y_cv, s["xc"] = conv_fwd(proj, P["cv_dw_w"][l], P["cv_dw_b"][l], P["cv_ln_g"][l], P["cv_ln_b"][l],
                                 s["cv_pw"], P["cv_out_g"][l], f"conv_fwd_{l}")
        y_ret, s["states"] = ret_fwd(proj, P["ret_gn_g"][l], tb, f"ret_fwd_{l}")
        ycat = s["ycat"] = jnp.concatenate([y_att, y_cv.astype(BF16), y_ret.astype(BF16)], axis=1)
        h = mm_nn(f"out_proj_{l}", ycat, s["w_out"], 512, res=h)
        s["h1"] = h
        s["ffn_up"], s["ffn_down"] = hooks["ffn_w"](l, h)
        s["f"], s["u2"] = ffn_up_fwd(f"ffn_up_{l}", h, P["norm_ffn_g"][l], s["ffn_up"])
        a, s["gc"], s["uc"] = ffn_act_fwd(s["f"], P["ffn_dw_w"][l], P["ffn_dw_b"][l], f"ffn_act_fwd_{l}")
        s["a"] = a
        h = ffn_down_fwd(f"ffn_down_{l}", a, s["ffn_down"], h)
        stash.append(s)

    dh, dhm, loss = loss_head(h, target)
    G = {n: [None] * DEPTH for n in ("norm_mix_g", "q_norm_g", "k_norm_g", "attn_sinks", "attn_out_g", "cv_dw_w",
                                     "cv_dw_b", "cv_ln_g", "cv_ln_b", "cv_out_g", "ret_gn_g", "norm_ffn_g",
                                     "ffn_dw_w", "ffn_dw_b")}
    for l in reversed(range(DEPTH)):
        s = stash[l]
        da = ffn_down_dx(f"ffn_down_dx_{l}", dhm, s["ffn_down"])
        dw_down = ffn_down_dw(f"ffn_down_dw_{l}", s["a"], dhm)
        dfg, dfu, dwg, dwu, dbg, dbu = ffn_act_bwd(s["f"], s["gc"], s["uc"], da, P["ffn_dw_w"][l], f"ffn_act_bwd_{l}")
        G["ffn_dw_w"][l] = jnp.concatenate([dwg, dwu], axis=0)
        G["ffn_dw_b"][l] = jnp.concatenate([dbg, dbu], axis=0)
        du2 = ffn_up_dx(f"ffn_up_dx_{l}", dfg, dfu, s["ffn_up"])
        dw_up = ffn_up_dw(f"ffn_up_dw_{l}", s["u2"], dfg, dfu)
        after = hooks["ffn_grads"](l, dw_down, dw_up)
        dh, dhm, G["norm_ffn_g"][l] = rms_bwd(s["h1"], P["norm_ffn_g"][l], du2, dh, f"rms_ffn_bwd_{l}", after=after)
        dycat = mm_nt(f"out_proj_dx_{l}", dhm, s["w_out"], 512)
        dw_out = mm_ta(f"out_proj_dw_{l}", s["ycat"], dhm, 512)
        do, _, G["attn_out_g"][l] = rms_bwd(s["o"], P["attn_out_g"][l], dycat, None, f"rms_att_bwd_{l}")
        (dq, dk_c, dk_p, dk_m, dv_c, dv_p, dv_m, G["q_norm_g"][l], G["k_norm_g"][l], dsk) = attn_bwd(
            s["proj"], do, P["q_norm_g"][l], P["k_norm_g"][l], P["attn_sinks"][l], f"attn_bwd_{l}")
        G["attn_sinks"][l] = dsk[:, 0]
        shift = lambda z: jnp.concatenate([z[T:], jnp.zeros((T, NKV * HD), F32)], axis=0)
        dk = (dk_c + shift(dk_p)).at[:T].add(dk_m)
        dv = (dv_c + shift(dv_p)).at[:T].add(dv_m)
        (dca, dcb, G["cv_dw_w"][l], G["cv_dw_b"][l], G["cv_ln_g"][l], G["cv_ln_b"][l], G["cv_out_g"][l],
         dpw) = conv_bwd(s["proj"], s["xc"], dycat, P["cv_dw_w"][l], P["cv_ln_g"][l], P["cv_ln_b"][l], s["cv_pw"],
                         P["cv_out_g"][l], f"conv_bwd_{l}")
        dret, G["ret_gn_g"][l] = ret_bwd(s["proj"], dycat, s["states"], P["ret_gn_g"][l], tb, f"ret_bwd_{l}")
        dproj = jnp.concatenate([dq, dk, dv, dca, dcb, dret], axis=1).astype(BF16)
        du = mm_nn(f"in_proj_dx_{l}", dproj, s["w_in"], 512)
        dw_in = mm_ta(f"in_proj_dw_{l}", dproj, s["u"], 768)
        after = hooks["mixer_grads"](l, dw_out, dpw, dw_in)
        dh, dhm, G["norm_mix_g"][l] = rms_bwd(s["h"], P["norm_mix_g"][l], du, dh, f"rms_mix_bwd_{l}", after=after,
                                              last=(l == 0))
    return loss[0, 0], dh, dhm[PAD:T], G


_SMALL = ("meta", "norm_mix_g", "q_norm_g", "k_norm_g", "attn_sinks", "attn_out_g", "cv_dw_w", "cv_dw_b", "cv_ln_g",
          "cv_ln_b", "cv_out_g", "ret_gn_g", "norm_ffn_g", "ffn_dw_w", "ffn_dw_b")
_BIG = ("w_in", "cv_pw", "w_out", "ffn_up", "ffn_down")
_TRANSPOSED = ("w_in", "ffn_up")
_ORDER = ("meta", "norm_mix_g", "w_in", "q_norm_g", "k_norm_g", "attn_sinks", "attn_out_g", "cv_dw_w", "cv_dw_b",
          "cv_ln_g", "cv_ln_b", "cv_pw", "cv_out_g", "ret_gn_g", "w_out", "norm_ffn_g", "ffn_up", "ffn_dw_w",
          "ffn_dw_b", "ffn_down")
_SMALL_SHARDED = {"meta": D, "cv_dw_w": CV, "ffn_dw_w": 2 * D_FF}


def _pack(arrs):
    flat = jnp.concatenate([a.reshape(-1) for a in arrs])
    n = flat.shape[0]
    rows = -(-n // 1024) * 8
    return jnp.pad(flat, (0, rows * 128 - n)).reshape(rows, 128)


def _unpack(packed, shapes):
    flat = packed.reshape(-1)
    out, off = [], 0
    for s in shapes:
        n = int(np.prod(s))
        out.append(flat[off:off + n].reshape(s))
        off += n
    return out


def kernel(x, meta, norm_mix_g, w_in, q_norm_g, k_norm_g, attn_sinks, attn_out_g, cv_dw_w, cv_dw_b, cv_ln_g, cv_ln_b, cv_pw, cv_out_g, ret_gn_g, w_out, norm_ffn_g, ffn_up, ffn_dw_w, ffn_dw_b, ffn_down, loss_target, m_meta, m_norm_mix_g, m_w_in, m_q_norm_g, m_k_norm_g, m_attn_sinks, m_attn_out_g, m_cv_dw_w, m_cv_dw_b, m_cv_ln_g, m_cv_ln_b, m_cv_pw, m_cv_out_g, m_ret_gn_g, m_w_out, m_norm_ffn_g, m_ffn_up, m_ffn_dw_w, m_ffn_dw_b, m_ffn_down, v_meta, v_norm_mix_g, v_w_in, v_q_norm_g, v_k_norm_g, v_attn_sinks, v_attn_out_g, v_cv_dw_w, v_cv_dw_b, v_cv_ln_g, v_cv_ln_b, v_cv_pw, v_cv_out_g, v_ret_gn_g, v_w_out, v_norm_ffn_g, v_ffn_up, v_ffn_dw_w, v_ffn_dw_b, v_ffn_down):
    W = dict(meta=meta, norm_mix_g=norm_mix_g, w_in=w_in, q_norm_g=q_norm_g, k_norm_g=k_norm_g,
             attn_sinks=attn_sinks, attn_out_g=attn_out_g, cv_dw_w=cv_dw_w, cv_dw_b=cv_dw_b, cv_ln_g=cv_ln_g,
             cv_ln_b=cv_ln_b, cv_pw=cv_pw, cv_out_g=cv_out_g, ret_gn_g=ret_gn_g, w_out=w_out,
             norm_ffn_g=norm_ffn_g, ffn_up=ffn_up, ffn_dw_w=ffn_dw_w, ffn_dw_b=ffn_dw_b, ffn_down=ffn_down)
    M = dict(meta=m_meta, norm_mix_g=m_norm_mix_g, w_in=m_w_in, q_norm_g=m_q_norm_g, k_norm_g=m_k_norm_g,
             attn_sinks=m_attn_sinks, attn_out_g=m_attn_out_g, cv_dw_w=m_cv_dw_w, cv_dw_b=m_cv_dw_b,
             cv_ln_g=m_cv_ln_g, cv_ln_b=m_cv_ln_b, cv_pw=m_cv_pw, cv_out_g=m_cv_out_g, ret_gn_g=m_ret_gn_g,
             w_out=m_w_out, norm_ffn_g=m_norm_ffn_g, ffn_up=m_ffn_up, ffn_dw_w=m_ffn_dw_w, ffn_dw_b=m_ffn_dw_b,
             ffn_down=m_ffn_down)
    V = dict(meta=v_meta, norm_mix_g=v_norm_mix_g, w_in=v_w_in, q_norm_g=v_q_norm_g, k_norm_g=v_k_norm_g,
             attn_sinks=v_attn_sinks, attn_out_g=v_attn_out_g, cv_dw_w=v_cv_dw_w, cv_dw_b=v_cv_dw_b,
             cv_ln_g=v_cv_ln_g, cv_ln_b=v_cv_ln_b, cv_pw=v_cv_pw, cv_out_g=v_cv_out_g, ret_gn_g=v_ret_gn_g,
             w_out=v_w_out, norm_ffn_g=v_norm_ffn_g, ffn_up=v_ffn_up, ffn_dw_w=v_ffn_dw_w, ffn_dw_b=v_ffn_dw_b,
             ffn_down=v_ffn_down)
    me = 4 * lax.axis_index("x") + 2 * lax.axis_index("y") + lax.axis_index("c")
    for n in _TRANSPOSED:
        W[n], M[n], V[n] = (a.transpose(0, 2, 1) for a in (W[n], M[n], V[n]))

    sh = {n: [W[n][l].astype(BF16) for l in range(DEPTH)] for n in _BIG}
    mix = lambda l: [sh["w_in"][l], sh["cv_pw"][l], sh["w_out"][l]]
    ffn = lambda l: [sh["ffn_up"][l], sh["ffn_down"][l]]
    first = all_gather(mix(0) + [meta, cv_dw_w, ffn_dw_w], "gather_first")
    g_meta, g_cdw, g_fdw = first[3:6]

    def landing(shards):
        return [lax.dynamic_update_slice(lax.empty((NDEV,) + s.shape, s.dtype), s[None], (me,) + (0,) * s.ndim)
                for s in shards]

    gathers = {("ffn", 0): gather_start("gather_ffn0_start", ffn(0), landing(ffn(0)), after=[first[0]])}
    gathers["mix", 1] = gather_start("gather_mix1_start", mix(1), landing(mix(1)), after=[gathers["ffn", 0][4]])
    gathers["ffn", 1] = gather_start("gather_ffn1_start", ffn(1), landing(ffn(1)), after=[gathers["mix", 1][4]])

    def mixer_w(l, h):
        g_in, g_pw, g_out = first[0:3] if l == 0 else gather_wait(f"gather_mix{l}_wait", gathers["mix", l], after=[h])[1]
        return g_in.reshape(IN_W, D), g_pw.reshape(CV, CV), g_out.reshape(D, D)

    def ffn_w(l, h1):
        return gather_wait(f"gather_ffn{l}_wait", gathers["ffn", l], after=[h1])[1]

    scatters = {}

    def ffn_grads(l, dw_down, dw_up):
        scatters["ffn", l] = scatter_start(f"scatter_ffn{l}_start", [dw_up, dw_down.reshape(NDEV, DNR, D)])
        return scatters["ffn", l][4]

    def mixer_grads(l, dw_out, dpw, dw_in):
        grads = [dw_in.reshape(NDEV, IN_W // NDEV, D), dpw.astype(BF16).reshape(NDEV, CV // NDEV, CV),
                 dw_out.reshape(NDEV, D // NDEV, D)]
        scatters["mix", l] = scatter_start(f"scatter_mix{l}_start", grads)
        return scatters["mix", l][4]

    P = dict(
        meta=g_meta.transpose(1, 0, 2).reshape(N_META, D),
        cv_dw_w=g_cdw.transpose(1, 2, 0, 3).reshape(DEPTH, CONV_K, CV),
        ffn_dw_w=g_fdw.transpose(1, 0, 2, 3),
        ffn_dw_b=ffn_dw_b.reshape(DEPTH, NDEV, 1, UPW),
        attn_sinks=attn_sinks,
    )
    for n in ("norm_mix_g", "q_norm_g", "k_norm_g", "attn_out_g", "cv_dw_b", "cv_ln_g", "cv_ln_b", "cv_out_g",
              "ret_gn_g", "norm_ffn_g"):
        P[n] = W[n].reshape(DEPTH, 1, -1)

    hooks = dict(mixer_w=mixer_w, ffn_w=ffn_w, ffn_grads=ffn_grads, mixer_grads=mixer_grads,
                 first_after=gathers["ffn", 1][4])
    loss_part, dx, dmeta, G = local_step(x[0], loss_target[0], P, hooks)

    small_full = {
        "meta": dmeta,
        "cv_dw_w": jnp.stack(G["cv_dw_w"]),
        "ffn_dw_w": jnp.stack([g.transpose(1, 0, 2).reshape(FFN_K, 2 * D_FF) for g in G["ffn_dw_w"]]),
        "ffn_dw_b": jnp.stack([g.reshape(2 * D_FF) for g in G["ffn_dw_b"]]),
        "attn_sinks": jnp.stack(G["attn_sinks"]),
    }
    for n in _SMALL:
        if n not in small_full:
            small_full[n] = jnp.stack([g.reshape(-1) for g in G[n]])
    shapes = [small_full[n].shape for n in _SMALL] + [(1,)]
    packed = _pack([small_full[n] for n in _SMALL] + [loss_part.reshape(1)])
    small_exchange = gather_start("reduce_small_start", [packed], landing([packed]), after=[dx])

    out = {}
    tiles = {"w_in": 144, "cv_pw": 32, "w_out": 128, "ffn_up": 176, "ffn_down": 176}
    me1 = me.astype(jnp.int32).reshape(1)
    own, lands = {n: [None] * DEPTH for n in _BIG}, {n: [None] * DEPTH for n in _BIG}

    def arrived(kind, names, after):
        for l in range(DEPTH):
            srcs, got = scatter_wait(f"scatter_{kind}{l}_wait", scatters[kind, l], after=after)
            for n, s_, g_ in zip(names, srcs, got):
                own[n][l], lands[n][l] = s_, g_

    def update(names):
        for n in names:
            shard = W[n].shape
            rows, cols = shard[0] * shard[1], shard[2]
            res = adamw_big(own[n], lands[n], W[n].reshape(rows, cols), M[n].reshape(rows, cols),
                            V[n].reshape(rows, cols), me1, tiles[n], f"adamw_{n}")
            out[n] = [r.reshape(shard) for r in res]

    arrived("ffn", ("ffn_up", "ffn_down"), [small_exchange[4]])
    update(("ffn_up", "ffn_down"))
    slots = gather_wait("reduce_small_wait", small_exchange, after=[out["ffn_down"][0]])[1][0]
    summed = _unpack(sum_slots(slots, "reduce_small_sum"), shapes)
    loss = summed[-1][0]
    small_g = []
    for n, g in zip(_SMALL, summed):
        if n in _SMALL_SHARDED:
            width = _SMALL_SHARDED[n] // NDEV
            g = lax.dynamic_slice_in_dim(g, me * width, width, axis=g.ndim - 1)
        small_g.append(g)
    flat2 = lambda a: a.reshape(-1, a.shape[-1])
    res = adamw_many([flat2(g) for g in small_g], *[[flat2(X[n]) for n in _SMALL] for X in (W, M, V)], "adamw_small")
    for n, r in zip(_SMALL, res):
        out[n] = [a.reshape(W[n].shape) for a in r]
    arrived("mix", ("w_in", "cv_pw", "w_out"), [out["ffn_down"][0], res[0][0]])
    update(("w_in", "cv_pw", "w_out"))
    for n in _TRANSPOSED:
        out[n] = [r.transpose(0, 2, 1) for r in out[n]]

    return (loss, dx[None], *[out[n][0] for n in _ORDER], *[out[n][1] for n in _ORDER],
            *[out[n][2] for n in _ORDER], *[out[n][3] for n in _ORDER])
```

```python
import functools
import math

import numpy as np
import jax
import jax.numpy as jnp
from jax import lax
from jax.experimental import pallas as pl
from jax.experimental.pallas import tpu as pltpu

F32 = jnp.float32
BF16 = jnp.bfloat16

D = 1024
SEQ = 2048
DEPTH = 2
T = 128
L = SEQ + T
NB = L // T
N_META = 16
PAD = T - N_META
HD = 64
NQ = 8
NKV = 2
GQA = NQ // NKV
CV = 256
CONV_K = 31
RH = 4
D_FF = 2816
FFN_K = 3
IN_W = 2304
RMS_EPS = 1e-6
LN_EPS = 1e-5
NEG = -1e30
NDEV = 8
UPW = 2 * D_FF // NDEV
DNR = D_FF // NDEV
NPAIR = NDEV // 2

ADAM_LR, ADAM_B1, ADAM_B2, ADAM_EPS, ADAM_WD, ADAM_STEP = 0.001, 0.9, 0.999, 1e-08, 0.01, 10

VMEM_BIG = 56 * 1024 * 1024

MESH = pl.DeviceIdType.MESH


def _params(vmem=None):
    return pltpu.CompilerParams(vmem_limit_bytes=vmem) if vmem else None


def _dot(a, b, nt=False):
    return lax.dot_general(a, b, (((1,), (1 if nt else 0,)), ((), ())), preferred_element_type=F32)


def _sig(x):
    return 1.0 / (1.0 + jnp.exp(-x))


def _bf(x):
    return x.astype(BF16)


def _stack_rows(rows):
    idx = lax.broadcasted_iota(jnp.int32, (len(rows), rows[0].shape[1]), 0)
    out = jnp.zeros((len(rows), rows[0].shape[1]), F32)
    for r, v in enumerate(rows):
        out = jnp.where(idx == r, v, out)
    return out


def rms_fwd(x, g, out_dtype, name, after=None):
    n, w = x.shape

    def body(x_ref, g_ref, *rest):
        o_ref = rest[-1]
        xv = x_ref[...]
        r = lax.rsqrt(jnp.mean(xv * xv, axis=-1, keepdims=True) + RMS_EPS)
        o_ref[...] = (xv * r * g_ref[...]).astype(o_ref.dtype)

    deps = [] if after is None else [after]
    return pl.pallas_call(
        body, name=name, out_shape=jax.ShapeDtypeStruct((n, w), out_dtype), grid=(n // T,),
        in_specs=[pl.BlockSpec((T, w), lambda i: (i, 0)), pl.BlockSpec((1, w), lambda i: (0, 0))] + [_HBM] * len(deps),
        out_specs=pl.BlockSpec((T, w), lambda i: (i, 0)))(x, g, *deps)


def rms_bwd(x, g, dy, dres, name, after=None, dy_col=0, last=False):
    n, w = x.shape
    has_res = dres is not None
    deps = [] if after is None else [after]

    def body(x_ref, g_ref, dy_ref, *rest):
        rest = rest[:len(rest) - 3 - len(deps)] + rest[len(rest) - 3:]
        if has_res:
            dres_ref, dx_ref, dxm_ref, dg_ref = rest
        else:
            dx_ref, dxm_ref, dg_ref = rest
        i = pl.program_id(0)
        xv = x_ref[...]
        r = lax.rsqrt(jnp.mean(xv * xv, axis=-1, keepdims=True) + RMS_EPS)
        xh = xv * r
        dyv = dy_ref[...]
        dxh = dyv * g_ref[...]
        dx = r * (dxh - xh * jnp.mean(dxh * xh, axis=-1, keepdims=True))
        if has_res:
            dx = dx + dres_ref[...]
        dx_ref[...] = dx
        if last:
            @pl.when(i == 0)
            def _():
                dxm_ref[...] = dx
        else:
            rows = i * T + lax.broadcasted_iota(jnp.int32, (T, 1), 0)
            dxm_ref[...] = jnp.where(rows >= PAD, dx, 0.0).astype(BF16)
        part = jnp.sum(dyv * xh, axis=0, keepdims=True)

        @pl.when(i == 0)
        def _():
            dg_ref[...] = part

        @pl.when(i > 0)
        def _():
            dg_ref[...] += part

    row = pl.BlockSpec((T, w), lambda i: (i, 0))
    vec = pl.BlockSpec((1, w), lambda i: (0, 0))
    ins = [x, g, dy] + ([dres] if has_res else []) + deps
    if last:
        out_shape = (jax.ShapeDtypeStruct((n - T, w), F32), jax.ShapeDtypeStruct((T, w), F32))
        out_specs = (pl.BlockSpec((T, w), lambda i: (jnp.maximum(i - 1, 0), 0)), pl.BlockSpec((T, w), lambda i: (0, 0)))
    else:
        out_shape = (jax.ShapeDtypeStruct((n, w), F32), jax.ShapeDtypeStruct((n, w), BF16))
        out_specs = (row, row)
    return pl.pallas_call(
        body, name=name, out_shape=(*out_shape, jax.ShapeDtypeStruct((1, w), F32)), grid=(n // T,),
        in_specs=[row, vec, pl.BlockSpec((T, w), lambda i: (i, dy_col))] + ([row] if has_res else [])
        + [_HBM] * len(deps),
        out_specs=(*out_specs, vec))(*ins)


def loss_head(h, target):
    def body(h_ref, t_ref, dh_ref, dhm_ref, loss_ref):
        n = pl.program_id(0)
        e = jnp.where(n > 0, h_ref[...] - t_ref[...], 0.0)
        dh = e * (1.0 / D)
        dh_ref[...] = dh
        dhm_ref[...] = dh.astype(BF16)
        part = jnp.sum(jnp.sum(e * e, axis=1, keepdims=True), axis=0, keepdims=True) * (0.5 / D)

        @pl.when(n == 0)
        def _():
            loss_ref[...] = jnp.zeros_like(loss_ref)

        @pl.when(n > 0)
        def _():
            loss_ref[...] += jnp.broadcast_to(part, loss_ref.shape)

    row = pl.BlockSpec((T, D), lambda n: (n, 0))
    return pl.pallas_call(
        body, name="loss_head",
        out_shape=(jax.ShapeDtypeStruct((L, D), F32), jax.ShapeDtypeStruct((L, D), BF16),
                   jax.ShapeDtypeStruct((8, 128), F32)),
        grid=(NB,),
        in_specs=[row, pl.BlockSpec((T, D), lambda n: (jnp.maximum(n - 1, 0), 0))],
        out_specs=(row, row, pl.BlockSpec((8, 128), lambda n: (0, 0))))(h, target)


def _mm(name, a, b, *, grid, a_spec, b_spec, o_spec, out_shape, nt=False, ta=False, red=False, res=None,
        res_spec=None):
    def body(a_ref, b_ref, *rest):
        o_ref = rest[-1]
        av = a_ref[...]
        bv = b_ref[...]
        if bv.ndim == 3:
            bv = bv.reshape(bv.shape[0] * bv.shape[1], bv.shape[2])
        if ta:
            acc = lax.dot_general(av, bv, (((0,), (0,)), ((), ())), preferred_element_type=F32)
        else:
            acc = _dot(av, bv, nt)
        if red:
            k = pl.program_id(0)

            @pl.when(k == 0)
            def _():
                o_ref[...] = acc

            @pl.when(k > 0)
            def _():
                o_ref[...] += acc
        else:
            if res is not None:
                rows = lax.broadcasted_iota(jnp.int32, (acc.shape[0], 1), 0)
                acc = rest[0][...] + jnp.where(rows >= PAD, acc, 0.0)
            o_ref[...] = acc.astype(o_ref.dtype)

    ins = [a, b] + ([res] if res is not None else [])
    specs = [a_spec, b_spec] + ([res_spec] if res is not None else [])
    return pl.pallas_call(body, name=name, out_shape=out_shape, grid=grid, in_specs=specs, out_specs=o_spec,
                          compiler_params=_params(VMEM_BIG))(*ins)


def mm_nn(name, a, b, tn, out_dtype=F32, res=None):
    m, k = a.shape
    n = b.shape[1]
    return _mm(name, a, b, grid=(n // tn,),
               a_spec=pl.BlockSpec((m, k), lambda j: (0, 0)), b_spec=pl.BlockSpec((k, tn), lambda j: (0, j)),
               o_spec=pl.BlockSpec((m, tn), lambda j: (0, j)), out_shape=jax.ShapeDtypeStruct((m, n), out_dtype),
               res=res, res_spec=pl.BlockSpec((m, tn), lambda j: (0, j)))


def mm_nt(name, a, b, tn):
    m, k = a.shape
    n = b.shape[0]
    return _mm(name, a, b, grid=(n // tn,), nt=True,
               a_spec=pl.BlockSpec((m, k), lambda j: (0, 0)), b_spec=pl.BlockSpec((tn, k), lambda j: (j, 0)),
               o_spec=pl.BlockSpec((m, tn), lambda j: (0, j)), out_shape=jax.ShapeDtypeStruct((m, n), F32))


def mm_ta(name, a, b, tm, out_dtype=BF16):
    k, m = a.shape
    n = b.shape[1]
    return _mm(name, a, b, grid=(m // tm,), ta=True,
               a_spec=pl.BlockSpec((k, tm), lambda j: (0, j)), b_spec=pl.BlockSpec((k, n), lambda j: (0, 0)),
               o_spec=pl.BlockSpec((tm, n), lambda j: (j, 0)), out_shape=jax.ShapeDtypeStruct((m, n), out_dtype))


def _rms_mm(name, h, g, b, *, grid, b_spec, o_spec, out_shape, after=None):
    deps = [] if after is None else [after]

    def body(h_ref, g_ref, b_ref, *rest):
        o_ref, u_ref = rest[len(deps):]

        @pl.when(pl.program_id(0) == 0)
        def _():
            gv = g_ref[...]
            for c in range(NB):
                rows = slice(c * T, (c + 1) * T)
                xv = h_ref[rows, :]
                r = lax.rsqrt(jnp.mean(xv * xv, axis=-1, keepdims=True) + RMS_EPS)
                u_ref[rows, :] = (xv * r * gv).astype(BF16)

        o_ref[...] = _dot(u_ref[...], b_ref[...], nt=True)

    whole = lambda r: pl.BlockSpec((r, D), lambda j: (0, 0))
    return pl.pallas_call(
        body, name=name, out_shape=(out_shape, jax.ShapeDtypeStruct((L, D), BF16)), grid=grid,
        in_specs=[whole(L), whole(1), b_spec] + [_HBM] * len(deps), out_specs=(o_spec, whole(L)),
        compiler_params=_params(VMEM_BIG))(h, g, b, *deps)


def in_proj_fwd(name, h, g, wt, tn, after=None):
    return _rms_mm(name, h, g, wt, grid=(IN_W // tn,), b_spec=pl.BlockSpec((tn, D), lambda j: (j, 0)),
                   o_spec=pl.BlockSpec((L, tn), lambda j: (0, j)), out_shape=jax.ShapeDtypeStruct((L, IN_W), F32),
                   after=after)


def ffn_up_fwd(name, h, g, wupt):
    return _rms_mm(name, h, g, wupt, grid=(NDEV,), b_spec=pl.BlockSpec((None, UPW, D), lambda j: (j, 0, 0)),
                   o_spec=pl.BlockSpec((None, L, UPW), lambda j: (j, 0, 0)),
                   out_shape=jax.ShapeDtypeStruct((NDEV, L, UPW), F32))


def _slab_specs():
    gate = pl.BlockSpec((None, L, UPW), lambda j: (jnp.minimum(j, NPAIR - 1), 0, 0))
    up = pl.BlockSpec((None, L, UPW), lambda j: (jnp.maximum(j - NPAIR, 0), 0, 0))
    return gate, up


def ffn_up_dx(name, dfg, dfu, wupt):
    def body(g_ref, u_ref, b_ref, o_ref):
        j = pl.program_id(0)

        @pl.when(j == 0)
        def _():
            o_ref[...] = _dot(g_ref[...], b_ref[...])

        @pl.when(jnp.logical_and(j > 0, j < NPAIR))
        def _():
            o_ref[...] += _dot(g_ref[...], b_ref[...])

        @pl.when(j >= NPAIR)
        def _():
            o_ref[...] += _dot(u_ref[...], b_ref[...])

    gate, up = _slab_specs()
    return pl.pallas_call(
        body, name=name, out_shape=jax.ShapeDtypeStruct((L, D), F32), grid=(NDEV,),
        in_specs=[gate, up, pl.BlockSpec((None, UPW, D), lambda j: (j, 0, 0))],
        out_specs=pl.BlockSpec((L, D), lambda j: (0, 0)), compiler_params=_params(VMEM_BIG))(dfg, dfu, wupt)


def ffn_up_dw(name, u2, dfg, dfu):
    tdot = lambda a, b: lax.dot_general(a, b, (((0,), (0,)), ((), ())), preferred_element_type=F32)

    def body(a_ref, g_ref, u_ref, o_ref):
        j = pl.program_id(0)

        @pl.when(j < NPAIR)
        def _():
            o_ref[...] = tdot(g_ref[...], a_ref[...]).astype(o_ref.dtype)

        @pl.when(j >= NPAIR)
        def _():
            o_ref[...] = tdot(u_ref[...], a_ref[...]).astype(o_ref.dtype)

    gate, up = _slab_specs()
    return pl.pallas_call(
        body, name=name, out_shape=jax.ShapeDtypeStruct((NDEV, UPW, D), BF16), grid=(NDEV,),
        in_specs=[pl.BlockSpec((L, D), lambda j: (0, 0)), gate, up],
        out_specs=pl.BlockSpec((None, UPW, D), lambda j: (j, 0, 0)), compiler_params=_params(VMEM_BIG))(u2, dfg, dfu)


def ffn_down_fwd(name, a, wdn, res, tn=256):
    def body(a_ref, b_ref, r_ref, o_ref):
        acc = jnp.zeros((L, tn), F32)
        for g in range(NPAIR):
            bv = b_ref[2 * g:2 * g + 2]
            acc = acc + _dot(a_ref[g], bv.reshape(2 * DNR, tn))
        rows = lax.broadcasted_iota(jnp.int32, (L, 1), 0)
        o_ref[...] = r_ref[...] + jnp.where(rows >= PAD, acc, 0.0)

    return pl.pallas_call(
        body, name=name, out_shape=jax.ShapeDtypeStruct((L, D), F32), grid=(D // tn,),
        in_specs=[pl.BlockSpec((NPAIR, L, UPW), lambda j: (0, 0, 0)),
                  pl.BlockSpec((NDEV, DNR, tn), lambda j: (0, 0, j)),
                  pl.BlockSpec((L, tn), lambda j: (0, j))],
        out_specs=pl.BlockSpec((L, tn), lambda j: (0, j)),
        compiler_params=_params(VMEM_BIG))(a, wdn, res)


def ffn_down_dx(name, dh, wdn):
    return _mm(name, dh, wdn, grid=(NPAIR,), nt=True,
               a_spec=pl.BlockSpec((L, D), lambda g: (0, 0)),
               b_spec=pl.BlockSpec((2, DNR, D), lambda g: (g, 0, 0)),
               o_spec=pl.BlockSpec((None, L, UPW), lambda g: (g, 0, 0)),
               out_shape=jax.ShapeDtypeStruct((NPAIR, L, UPW), F32))


def ffn_down_dw(name, a, dh):
    return _mm(name, a, dh, grid=(NPAIR,), ta=True,
               a_spec=pl.BlockSpec((None, L, UPW), lambda g: (g, 0, 0)),
               b_spec=pl.BlockSpec((L, D), lambda g: (0, 0)),
               o_spec=pl.BlockSpec((UPW, D), lambda g: (g, 0)),
               out_shape=jax.ShapeDtypeStruct((D_FF, D), BF16))


_SLOPES = [2.0 ** (-8.0 * (h + 1) / NQ) for h in range(NQ)]
_SCALE = HD ** -0.5


QR = GQA * T
KC = 3 * T


def _attn_geometry(g, n, sink_ref):
    row = lax.broadcasted_iota(jnp.int32, (QR, KC), 0)
    col = lax.broadcasted_iota(jnp.int32, (QR, KC), 1)
    i = row & (T - 1)
    j = col & (T - 1)
    blk = col // T
    d_meta = n * T + i - j
    ok_meta = (j >= PAD) & (d_meta >= 0)
    ok_prev = j > i + jnp.where(n >= 2, 0, T)
    ok_cur = j <= i - jnp.where(n >= 1, 0, T)
    ok = ((blk == 0) & ok_meta) | ((blk == 1) & ok_prev) | ((blk == 2) & ok_cur)
    dist = jnp.where(blk == 0, jnp.minimum(d_meta, T), jnp.where(blk == 1, T + i - j, i - j)).astype(F32)
    head = lax.broadcasted_iota(jnp.int32, (QR, 1), 0) // T
    slope = jnp.zeros((QR, 1), F32)
    sink = jnp.zeros((QR, 1), F32)
    for hh in range(GQA):
        slope = jnp.where(head == hh, jnp.where(g == 0, _SLOPES[hh], _SLOPES[GQA + hh]), slope)
        sink = jnp.where(head == hh, sink_ref[g * GQA + hh], sink)
    return ok, slope * dist, sink


def _attn_probs(qn, keys, geo):
    ok, penalty, sink = geo
    s = jnp.where(ok, _dot(qn, keys, nt=True) * _SCALE - penalty, NEG)
    m = jnp.maximum(jnp.max(s, axis=-1, keepdims=True), sink)
    e = jnp.exp(s - m)
    e_sink = jnp.exp(sink - m)
    inv = 1.0 / (e.sum(axis=-1, keepdims=True) + e_sink)
    return e, inv, e_sink * inv


def _rms_rows(x, g):
    r = lax.rsqrt(jnp.mean(x * x, axis=-1, keepdims=True) + RMS_EPS)
    xh = x * r
    return xh * g, xh, r


def _rms_rows_bwd(dy, g, xh, r):
    dxh = dy * g
    return r * (dxh - xh * jnp.mean(dxh * xh, axis=-1, keepdims=True))


def _rows3(m_ref, p_ref, c_ref):
    return jnp.concatenate([m_ref[...], p_ref[...], c_ref[...]], axis=0)


_ATT_K, _ATT_V = 4, 5


def _nat_specs():
    qspec = pl.BlockSpec((T, GQA * HD), lambda n, g: (n, g))
    kv = lambda col: (pl.BlockSpec((T, 2 * HD), lambda n, g: (n, col)),
                      pl.BlockSpec((T, 2 * HD), lambda n, g: (jnp.maximum(n - 1, 0), col)),
                      pl.BlockSpec((T, 2 * HD), lambda n, g: (0, col)))
    return qspec, kv(_ATT_K), kv(_ATT_V)


def _group_lanes(g, x):
    return jnp.where(g == 0, x[:, :HD], x[:, HD:])


def _stack_heads(x):
    return jnp.concatenate([x[:, hh * HD:(hh + 1) * HD] for hh in range(GQA)], axis=0)


def _unstack_heads(x):
    return jnp.concatenate([x[hh * T:(hh + 1) * T] for hh in range(GQA)], axis=1)


def attn_fwd(proj, qg, kg, sinks, name):
    def body(sink_ref, q_ref, kc_ref, kp_ref, km_ref, vc_ref, vp_ref, vm_ref, qg_ref, kg_ref, o_ref):
        n = pl.program_id(0)
        g = pl.program_id(1)
        geo = _attn_geometry(g, n, sink_ref)
        keys = _bf(_rms_rows(_group_lanes(g, _rows3(km_ref, kp_ref, kc_ref)), kg_ref[...])[0])
        vals = _bf(_group_lanes(g, _rows3(vm_ref, vp_ref, vc_ref)))
        qn = _bf(_rms_rows(_stack_heads(q_ref[...]), qg_ref[...])[0])
        e, inv, _ = _attn_probs(qn, keys, geo)
        o_ref[...] = _unstack_heads(_dot(_bf(e), vals) * inv)

    qspec, (kc, kp, km), (vc, vp, vm) = _nat_specs()
    vec = pl.BlockSpec((1, HD), lambda n, g: (0, 0))
    return pl.pallas_call(
        body, name=name, out_shape=jax.ShapeDtypeStruct((L, NQ * HD), F32), grid=(NB, NKV),
        in_specs=[pl.BlockSpec(memory_space=pltpu.SMEM), qspec, kc, kp, km, vc, vp, vm, vec, vec],
        out_specs=qspec)(sinks, proj, proj, proj, proj, proj, proj, proj, qg, kg)


def attn_bwd(proj, do, qg, kg, sinks, name):
    def body(sink_ref, q_ref, kc_ref, kp_ref, km_ref, vc_ref, vp_ref, vm_ref, do_ref, qg_ref, kg_ref,
             dq_ref, dkc_ref, dkp_ref, dkm_ref, dvc_ref, dvp_ref, dvm_ref, dqg_ref, dkg_ref, dsk_ref):
        n = pl.program_id(0)
        g = pl.program_id(1)

        @pl.when(jnp.logical_and(g == 0, n == 0))
        def _():
            for r in (dkm_ref, dvm_ref, dqg_ref, dkg_ref, dsk_ref):
                r[...] = jnp.zeros_like(r)

        geo = _attn_geometry(g, n, sink_ref)
        kgv = kg_ref[...]
        qgv = qg_ref[...]
        kn_f, kh, kr = _rms_rows(_group_lanes(g, _rows3(km_ref, kp_ref, kc_ref)), kgv)
        keys = _bf(kn_f)
        vals = _bf(_group_lanes(g, _rows3(vm_ref, vp_ref, vc_ref)))
        qn_f, qh, qr = _rms_rows(_stack_heads(q_ref[...]), qgv)
        qn = _bf(qn_f)
        e, inv, p_sink = _attn_probs(qn, keys, geo)
        dob = _bf(_stack_heads(do_ref[...]))
        p = e * inv
        dp = _dot(dob, vals, nt=True)
        delta = (p * dp).sum(axis=-1, keepdims=True)
        head_row = lax.broadcasted_iota(jnp.int32, (NQ, 128), 0)
        dsk = jnp.zeros((NQ, 128), F32)
        for hh in range(GQA):
            part = -jnp.sum((p_sink * delta)[hh * T:(hh + 1) * T], axis=0, keepdims=True)
            dsk = jnp.where(head_row == g * GQA + hh, part, dsk)
        dsk_ref[...] += dsk
        ds = p * (dp - delta)
        dqn = _dot(_bf(ds), keys) * _SCALE
        dq_ref[...] = _unstack_heads(_rms_rows_bwd(dqn, qgv, qh, qr)).astype(dq_ref.dtype)
        dqg_ref[...] += jnp.sum(dqn * qh, axis=0, keepdims=True)
        dkn = _dot(_bf(ds.T), qn) * _SCALE
        dkg_ref[...] += jnp.sum(dkn * kh, axis=0, keepdims=True)
        dk_all = _rms_rows_bwd(dkn, kgv, kh, kr)
        dv_all = _dot(_bf(p.T), dob)

        for gg in range(NKV):
            @pl.when(g == gg)
            def _(gg=gg):
                lanes = slice(gg * HD, (gg + 1) * HD)
                dkm_ref[:, lanes] += dk_all[0:T]
                dkp_ref[:, lanes] = dk_all[T:2 * T]
                dkc_ref[:, lanes] = dk_all[2 * T:3 * T]
                dvm_ref[:, lanes] += dv_all[0:T]
                dvp_ref[:, lanes] = dv_all[T:2 * T]
                dvc_ref[:, lanes] = dv_all[2 * T:3 * T]

    qspec, (kc, kp, km), (vc, vp, vm) = _nat_specs()
    vec = pl.BlockSpec((1, HD), lambda n, g: (0, 0))
    cur = pl.BlockSpec((T, 2 * HD), lambda n, g: (n, 0))
    meta = pl.BlockSpec((T, 2 * HD), lambda n, g: (0, 0))
    kv_shape = jax.ShapeDtypeStruct((L, 2 * HD), F32)
    meta_shape = jax.ShapeDtypeStruct((T, 2 * HD), F32)
    vec_shape = jax.ShapeDtypeStruct((1, HD), F32)
    return pl.pallas_call(
        body, name=name,
        out_shape=(jax.ShapeDtypeStruct((L, NQ * HD), BF16), kv_shape, kv_shape, meta_shape, kv_shape, kv_shape,
                   meta_shape, vec_shape, vec_shape, jax.ShapeDtypeStruct((NQ, 128), F32)),
        grid=(NB, NKV),
        in_specs=[pl.BlockSpec(memory_space=pltpu.SMEM), qspec, kc, kp, km, vc, vp, vm, qspec, vec, vec],
        out_specs=(qspec, cur, cur, meta, cur, cur, meta, vec, vec,
                   pl.BlockSpec((NQ, 128), lambda n, g: (0, 0))))(
            sinks, proj, proj, proj, proj, proj, proj, proj, do, qg, kg)


RW = RH * HD


def _ret_tables():
    h = np.arange(RH, dtype=np.float64)
    lg = np.log1p(-np.exp2(-5.0 - h))
    idx = np.arange(T, dtype=np.float64)
    diff = idx[:, None] - idx[None, :]
    decay = np.where(diff[None] >= 0, np.exp(np.maximum(diff, 0.0)[None] * lg[:, None, None]), 0.0)
    zeta = np.exp((T - 1 - idx)[None, :] * lg[:, None])
    xi = np.exp((idx + 1.0)[None, :] * lg[:, None])
    cd = np.exp(T * lg)
    lanes = lambda a: np.repeat(a.T, HD, axis=1)
    head_of = np.arange(RW) // HD
    same = (head_of[:, None] == head_of[None, :]).astype(np.float64)
    f = lambda a: jnp.asarray(a, F32)
    return dict(decay=f(decay), zeta=f(lanes(zeta)), xi=f(lanes(xi)), cd=f(np.repeat(cd, HD)[None, :]),
                head=f((head_of[None, :] == np.arange(RH)[:, None]).astype(np.float64)[:, None, :]),
                same=f(same), avg=jnp.asarray(same / HD, BF16))


def _seg_mean(x, avg):
    hi = _bf(x)
    lo = _bf(x - hi.astype(F32))
    return _dot(hi, avg) + _dot(lo, avg)


def _ret_specs(col0, order):
    return lambda col: pl.BlockSpec((T, RW), lambda i: (order(i), col0 + col))


def _ret_chunk(q, kf, v, s, tb):
    dec, xi, head = tb
    vb = _bf(v)
    kb = _bf(kf)
    y = _dot(_bf(q * xi), _bf(s))
    a = []
    for h in range(RH):
        a.append(_dot(_bf(q * head[h]), kb, nt=True) * dec[h])
        y = y + head[h] * _dot(_bf(a[h]), vb)
    return a, y


def _gn_rows(y):
    mu = jnp.mean(y, axis=-1, keepdims=True)
    yc = y - mu
    rstd = lax.rsqrt(jnp.mean(yc * yc, axis=-1, keepdims=True) + LN_EPS)
    return yc * rstd, rstd


_RET_COL0 = _RET_Q = 5


def _ret_consts(tb):
    names = ("decay", "zeta", "xi", "cd", "head", "same", "avg")
    full = lambda a: pl.BlockSpec(a.shape, lambda i: (0,) * a.ndim)
    return [tb[n] for n in names], [full(tb[n]) for n in names]


def _ret_inputs(n, q_ref, k_ref, v_ref):
    rows = n * T + lax.broadcasted_iota(jnp.int32, (T, 1), 0)
    valid = rows >= PAD
    return q_ref[...], jnp.where(valid, k_ref[...] * (HD ** -0.5), 0.0), v_ref[...], valid


def ret_fwd(proj, gng, tb, name):
    def body(q_ref, k_ref, v_ref, g_ref, dec_ref, zeta_ref, xi_ref, cd_ref, head_ref, same_ref, avg_ref, gng_ref,
             y_ref, st_ref, s_scr):
        n = pl.program_id(0)

        @pl.when(n == 0)
        def _():
            s_scr[...] = jnp.zeros_like(s_scr)

        s = s_scr[...]
        st_ref[...] = s
        q, kf, v, _ = _ret_inputs(n, q_ref, k_ref, v_ref)
        _, y = _ret_chunk(q, kf, v, s, (dec_ref, xi_ref[...], head_ref))
        s_scr[...] = cd_ref[...] * s + same_ref[...] * _dot(_bf((kf * zeta_ref[...]).T), _bf(v))
        avg = avg_ref[...]
        yc = y - _seg_mean(y, avg)
        yh = yc * lax.rsqrt(_seg_mean(yc * yc, avg) + LN_EPS)
        gv = g_ref[...]
        y_ref[...] = (gv * _sig(gv) * (yh * gng_ref[...])).astype(y_ref.dtype)

    col = _ret_specs(_RET_COL0, lambda i: i)
    consts, cspecs = _ret_consts(tb)
    return pl.pallas_call(
        body, name=name,
        out_shape=(jax.ShapeDtypeStruct((L, RW), BF16), jax.ShapeDtypeStruct((NB, RW, RW), F32)),
        grid=(NB,),
        in_specs=[col(0), col(1), col(2), col(3)] + cspecs + [pl.BlockSpec((1, RW), lambda i: (0, 0))],
        out_specs=(pl.BlockSpec((T, RW), lambda i: (i, 0)), pl.BlockSpec((None, RW, RW), lambda i: (i, 0, 0))),
        scratch_shapes=[pltpu.VMEM((RW, RW), F32)])(proj, proj, proj, proj, *consts, gng)


def ret_bwd(proj, dycat, states, gng, tb, name):
    def body(q_ref, k_ref, v_ref, g_ref, do_ref, st_ref, dec_ref, zeta_ref, xi_ref, cd_ref, head_ref, same_ref,
             avg_ref, gng_ref, d_ref, dgn_ref, ds_scr):
        i = pl.program_id(0)
        n = NB - 1 - i

        @pl.when(i == 0)
        def _():
            ds_scr[...] = jnp.zeros_like(ds_scr)

        dsn = _bf(ds_scr[...])
        s = st_ref[...]
        sb = _bf(s)
        q, kf, v, valid = _ret_inputs(n, q_ref, k_ref, v_ref)
        xi, zeta, avg = xi_ref[...], zeta_ref[...], avg_ref[...]
        a, y = _ret_chunk(q, kf, v, s, (dec_ref, xi, head_ref))
        yc = y - _seg_mean(y, avg)
        rstd = lax.rsqrt(_seg_mean(yc * yc, avg) + LN_EPS)
        yh = yc * rstd
        gv = g_ref[...]
        sg = _sig(gv)
        sil = gv * sg
        gn = gng_ref[...]
        dout = do_ref[...]
        d_ref[:, 3 * RW:4 * RW] = (dout * (yh * gn) * (sg * (1.0 + gv * (1.0 - sg)))).astype(d_ref.dtype)
        dyh = dout * sil * gn
        part = jnp.sum(dout * sil * yh, axis=0, keepdims=True)

        @pl.when(i == 0)
        def _():
            dgn_ref[...] = part

        @pl.when(i > 0)
        def _():
            dgn_ref[...] += part

        dy = rstd * (dyh - _seg_mean(dyh, avg) - yh * _seg_mean(dyh * yh, avg))
        dyb = _bf(dy)
        vb, kb, qb = _bf(v), _bf(kf), _bf(q)
        dq = _dot(dyb, sb, nt=True) * xi
        dkf = _dot(vb, dsn, nt=True) * zeta
        dv = _dot(_bf(kf * zeta), dsn)
        for h in range(RH):
            m = head_ref[h]
            da = _dot(_bf(dy * m), vb, nt=True) * dec_ref[h]
            dv = dv + m * _dot(_bf(a[h].T), dyb)
            dq = dq + m * _dot(_bf(da), kb)
            dkf = dkf + m * _dot(_bf(da.T), qb)
        d_ref[:, 0:RW] = dq.astype(d_ref.dtype)
        d_ref[:, RW:2 * RW] = jnp.where(valid, dkf * (HD ** -0.5), 0.0).astype(d_ref.dtype)
        d_ref[:, 2 * RW:3 * RW] = dv.astype(d_ref.dtype)
        ds_scr[...] = cd_ref[...] * ds_scr[...] + same_ref[...] * _dot(_bf((q * xi).T), dyb)

    back = lambda i: NB - 1 - i
    col = _ret_specs(_RET_COL0, back)
    consts, cspecs = _ret_consts(tb)
    vec = pl.BlockSpec((1, RW), lambda i: (0, 0))
    return pl.pallas_call(
        body, name=name,
        out_shape=(jax.ShapeDtypeStruct((L, 4 * RW), BF16), jax.ShapeDtypeStruct((1, RW), F32)),
        grid=(NB,),
        in_specs=[col(0), col(1), col(2), col(3), _ret_specs(3, back)(0),
                  pl.BlockSpec((None, RW, RW), lambda i: (back(i), 0, 0))] + cspecs + [vec],
        out_specs=(pl.BlockSpec((T, 4 * RW), lambda i: (back(i), 0)), vec),
        scratch_shapes=[pltpu.VMEM((RW, RW), F32)])(proj, proj, proj, proj, dycat, states, *consts, gng)


HALO = 32
TAP0 = HALO - (CONV_K - 1)


def _conv_specs():
    cur = lambda col=0: pl.BlockSpec((T, CV), lambda c: (c, col))
    before = lambda col=0: pl.BlockSpec((HALO, CV), lambda c: (jnp.maximum(c * (T // HALO) - 1, 0), col))
    after = pl.BlockSpec((HALO, CV), lambda c: (jnp.minimum((c + 1) * (T // HALO), L // HALO - 1), 0))
    full = lambda r, w: pl.BlockSpec((r, w), lambda c: (0, 0))
    return cur, before, after, full


_CONV_A, _CONV_B = 3, 4
_DY_CONV = 2


def _conv_post(xc, lg_ref, lb_ref):
    xh, rstd = _gn_rows(xc)
    z = xh * lg_ref[...] + lb_ref[...]
    return xh, rstd, z, _sig(z)


def conv_fwd(proj, w, b, lg, lb, pw, og, name):
    def body(ca_ref, cah_ref, cb_ref, cbh_ref, w_ref, b_ref, lg_ref, lb_ref, pw_ref, og_ref, y_ref, xc_ref, u_scr):
        c = pl.program_id(0)
        u_scr[0:HALO, :] = jnp.where(c > 0, cah_ref[...] * _sig(cbh_ref[...]), 0.0)
        u_scr[HALO:HALO + T, :] = ca_ref[...] * _sig(cb_ref[...])
        acc = jnp.zeros((T, CV), F32)
        for k in range(CONV_K):
            acc = acc + w_ref[k:k + 1, :] * u_scr[TAP0 + k:TAP0 + k + T, :]
        xc = acc + b_ref[...]
        xc_ref[...] = xc
        _, _, z, sg = _conv_post(xc, lg_ref, lb_ref)
        zp = _dot(_bf(z * sg), pw_ref[...])
        y_ref[...] = _rms_rows(zp, og_ref[...])[0].astype(y_ref.dtype)

    cur, before, _, full = _conv_specs()
    vec = full(1, CV)
    seq = jax.ShapeDtypeStruct((L, CV), F32)
    return pl.pallas_call(
        body, name=name, out_shape=(jax.ShapeDtypeStruct((L, CV), BF16), seq), grid=(NB,),
        in_specs=[cur(_CONV_A), before(_CONV_A), cur(_CONV_B), before(_CONV_B), full(CONV_K, CV), vec, vec, vec,
                  full(CV, CV), vec],
        out_specs=(cur(), cur()),
        scratch_shapes=[pltpu.VMEM((HALO + T, CV), F32)])(proj, proj, proj, proj, w, b, lg, lb, pw, og)


def conv_bwd(proj, xc, dycat, w, lg, lb, pw, og, name):
    def body1(xc_ref, dy_ref, lg_ref, lb_ref, pw_ref, og_ref, dxc_ref, db_ref, dlg_ref, dlb_ref, dog_ref, dpw_ref):
        c = pl.program_id(0)

        @pl.when(c == 0)
        def _():
            for r in (db_ref, dlg_ref, dlb_ref, dog_ref, dpw_ref):
                r[...] = jnp.zeros_like(r)

        xh, rstd, z, sg = _conv_post(xc_ref[...], lg_ref, lb_ref)
        s = z * sg
        zp = _dot(_bf(s), pw_ref[...])
        ogv = og_ref[...]
        _, zph, r2 = _rms_rows(zp, ogv)
        dyv = dy_ref[...]
        dog_ref[...] += jnp.sum(dyv * zph, axis=0, keepdims=True)
        dzpb = _bf(_rms_rows_bwd(dyv, ogv, zph, r2))
        dpw_ref[...] += _dot(_bf(s.T), dzpb)
        dz = _dot(dzpb, pw_ref[...], nt=True) * (sg * (1.0 + z * (1.0 - sg)))
        dlg_ref[...] += jnp.sum(dz * xh, axis=0, keepdims=True)
        dlb_ref[...] += jnp.sum(dz, axis=0, keepdims=True)
        dxh = dz * lg_ref[...]
        dxc = rstd * (dxh - jnp.mean(dxh, axis=-1, keepdims=True) - xh * jnp.mean(dxh * xh, axis=-1, keepdims=True))
        db_ref[...] += jnp.sum(dxc, axis=0, keepdims=True)
        dxc_ref[...] = dxc

    def body2(dx_ref, dxa_ref, ca_ref, cb_ref, w_ref, dca_ref, dcb_ref, dw_ref, d_scr):
        c = pl.program_id(0)

        @pl.when(c == 0)
        def _():
            dw_ref[...] = jnp.zeros_like(dw_ref)

        d_scr[0:T, :] = dx_ref[...]
        d_scr[T:T + HALO, :] = jnp.where(c < NB - 1, dxa_ref[...], 0.0)
        ca = ca_ref[...]
        sg = _sig(cb_ref[...])
        u = ca * sg
        du = jnp.zeros((T, CV), F32)
        for k in range(CONV_K):
            off = CONV_K - 1 - k
            dj = d_scr[off:off + T, :]
            du = du + w_ref[k:k + 1, :] * dj
            dw_ref[k:k + 1, :] += jnp.sum(u * dj, axis=0, keepdims=True)
        dca_ref[...] = (du * sg).astype(dca_ref.dtype)
        dcb_ref[...] = (du * ca * sg * (1.0 - sg)).astype(dcb_ref.dtype)

    cur, _, after, full = _conv_specs()
    seq = jax.ShapeDtypeStruct((L, CV), F32)
    vsh = jax.ShapeDtypeStruct((1, CV), F32)
    vec = full(1, CV)
    dxc, db, dlg, dlb, dog, dpw = pl.pallas_call(
        body1, name=name + "_a",
        out_shape=(seq, vsh, vsh, vsh, vsh, jax.ShapeDtypeStruct((CV, CV), F32)),
        grid=(NB,),
        in_specs=[cur(), cur(_DY_CONV), vec, vec, full(CV, CV), vec],
        out_specs=(cur(), vec, vec, vec, vec, full(CV, CV)))(xc, dycat, lg, lb, pw, og)
    dca, dcb, dw = pl.pallas_call(
        body2, name=name + "_b", grid=(NB,),
        out_shape=(jax.ShapeDtypeStruct((L, CV), BF16), jax.ShapeDtypeStruct((L, CV), BF16),
                   jax.ShapeDtypeStruct((CONV_K, CV), F32)),
        in_specs=[cur(), after, cur(_CONV_A), cur(_CONV_B), full(CONV_K, CV)],
        out_specs=(cur(), cur(), full(CONV_K, CV)),
        scratch_shapes=[pltpu.VMEM((T + HALO, CV), F32)])(dxc, dxc, proj, proj, w)
    return dca, dcb, dw, db, dlg, dlb, dog, dpw


HR = 8


def _ffn_specs():
    cur = lambda off: pl.BlockSpec((None, T, UPW), lambda j, c: (j + off, c, 0))
    before = lambda off: pl.BlockSpec((None, HR, UPW), lambda j, c: (j + off, jnp.maximum(c * (T // HR) - 1, 0), 0))
    after = lambda off: pl.BlockSpec(
        (None, HR, UPW), lambda j, c: (j + off, jnp.minimum((c + 1) * (T // HR), L // HR - 1), 0))
    wspec = lambda off, r: pl.BlockSpec((None, r, UPW), lambda j, c: (j + off, 0, 0))
    return cur, before, after, wspec


HR16 = 16


def _ffn_after16():
    return pl.BlockSpec((None, HR16, UPW), lambda j, c: (j, jnp.minimum((c + 1) * (T // HR16), L // HR16 - 1), 0))


def ffn_act_fwd(f, w, b, name):
    strip = 16

    def body(fg_ref, fgh_ref, fu_ref, fuh_ref, wg_ref, wu_ref, bg_ref, bu_ref, a_ref, gc_ref, uc_ref, hg_scr, hu_scr):
        c = pl.program_id(1)
        for scr, x_ref, xh_ref in ((hg_scr, fg_ref, fgh_ref), (hu_scr, fu_ref, fuh_ref)):
            scr[0:HR, :] = jnp.where(c > 0, xh_ref[...], 0.0)
            scr[HR:HR + strip, :] = x_ref[0:strip, :]
        wg = [wg_ref[k:k + 1, :] for k in range(FFN_K)]
        wu = [wu_ref[k:k + 1, :] for k in range(FFN_K)]
        bg, bu = bg_ref[...], bu_ref[...]

        def conv(src, base, w, b):
            acc = b
            for k in range(FFN_K):
                o = base - (FFN_K - 1) + k
                acc = acc + w[k] * src[o:o + strip, :]
            return acc

        for s in range(T // strip):
            r = s * strip
            gate = conv(hg_scr, HR, wg, bg) if s == 0 else conv(fg_ref, r, wg, bg)
            up = conv(hu_scr, HR, wu, bu) if s == 0 else conv(fu_ref, r, wu, bu)
            gc_ref[r:r + strip, :] = gate.astype(BF16)
            uc_ref[r:r + strip, :] = up.astype(BF16)
            a_ref[r:r + strip, :] = (gate * _sig(gate) * up).astype(BF16)

    cur, before, _, wspec = _ffn_specs()
    pre = jax.ShapeDtypeStruct((NPAIR, L, UPW), BF16)
    return pl.pallas_call(
        body, name=name, out_shape=(jax.ShapeDtypeStruct((NPAIR, L, UPW), BF16), pre, pre), grid=(NPAIR, NB),
        in_specs=[cur(0), before(0), cur(NPAIR), before(NPAIR), wspec(0, FFN_K), wspec(NPAIR, FFN_K),
                  wspec(0, 1), wspec(NPAIR, 1)],
        out_specs=(cur(0), cur(0), cur(0)),
        scratch_shapes=[pltpu.VMEM((HR + strip, UPW), F32), pltpu.VMEM((HR + strip, UPW), F32)])(
            f, f, f, f, w, w, b, b)


def ffn_act_bwd(f, gc, uc, da, w, name):
    ext = T + HR
    sub = 8

    def body(fg_ref, fu_ref, gc_ref, gca_ref, uc_ref, uca_ref, da_ref, daa_ref, wg_ref, wu_ref,
             dfg_ref, dfu_ref, dwg_ref, dwu_ref, dbg_ref, dbu_ref, dg_scr, du_scr):
        c = pl.program_id(1)
        wg = [wg_ref[k:k + 1, :] for k in range(FFN_K)]
        wu = [wu_ref[k:k + 1, :] for k in range(FFN_K)]

        for r in range(0, ext, HR16):
            rows = min(HR16, ext - r)
            if r < T:
                gate, up, dav = gc_ref[r:r + HR16, :].astype(F32), uc_ref[r:r + HR16, :].astype(F32), da_ref[r:r + HR16, :]
            else:
                gate, up = gca_ref[...].astype(F32)[0:rows], uca_ref[...].astype(F32)[0:rows]
                dav = jnp.where(c < NB - 1, daa_ref[...], 0.0)
            sg = _sig(gate)
            dg_scr[r:r + rows, :] = dav * up * (sg * (1.0 + gate * (1.0 - sg)))
            du_scr[r:r + rows, :] = dav * gate * sg

        def back(d_scr, x_ref, w, df_ref, dw_ref, db_ref):
            accs = [jnp.zeros((sub, UPW), F32) for _ in range(FFN_K + 1)]
            for s in range(T // (2 * sub)):
                pieces = []
                for r in (2 * s * sub, (2 * s + 1) * sub):
                    xv = x_ref[r:r + sub, :]
                    df = jnp.zeros((sub, UPW), F32)
                    for k in range(FFN_K):
                        off = FFN_K - 1 - k
                        dj = d_scr[r + off:r + off + sub, :]
                        df = df + w[k] * dj
                        accs[k] = accs[k] + xv * dj
                        if off == 0:
                            accs[FFN_K] = accs[FFN_K] + dj
                    pieces.append(df)
                df_ref[2 * s * sub:2 * (s + 1) * sub, :] = jnp.concatenate(pieces, axis=0).astype(BF16)
            dwp = _stack_rows([jnp.sum(a, axis=0, keepdims=True) for a in accs[:FFN_K]])
            dbp = jnp.sum(accs[FFN_K], axis=0, keepdims=True)

            @pl.when(c == 0)
            def _():
                dw_ref[...] = dwp
                db_ref[...] = dbp

            @pl.when(c > 0)
            def _():
                dw_ref[...] += dwp
                db_ref[...] += dbp

        back(dg_scr, fg_ref, wg, dfg_ref, dwg_ref, dbg_ref)
        back(du_scr, fu_ref, wu, dfu_ref, dwu_ref, dbu_ref)

    cur, _, after, wspec = _ffn_specs()
    slab = jax.ShapeDtypeStruct((NPAIR, L, UPW), BF16)
    wsh = jax.ShapeDtypeStruct((NPAIR, FFN_K, UPW), F32)
    bsh = jax.ShapeDtypeStruct((NPAIR, 1, UPW), F32)
    return pl.pallas_call(
        body, name=name, out_shape=(slab, slab, wsh, wsh, bsh, bsh), grid=(NPAIR, NB),
        in_specs=[cur(0), cur(NPAIR), cur(0), _ffn_after16(), cur(0), _ffn_after16(), cur(0), after(0),
                  wspec(0, FFN_K), wspec(NPAIR, FFN_K)],
        out_specs=(cur(0), cur(0), wspec(0, FFN_K), wspec(0, FFN_K), wspec(0, 1), wspec(0, 1)),
        scratch_shapes=[pltpu.VMEM((ext, UPW), F32), pltpu.VMEM((ext, UPW), F32)])(
            f, f, gc, gc, uc, uc, da, da, w, w)


def _mesh_pos():
    return lax.axis_index("x"), lax.axis_index("y"), lax.axis_index("c")


def _peer(pos, k):
    x, y, c = pos
    px = 1 - x if k & 4 else x
    py = 1 - y if k & 2 else y
    pc = 1 - c if k & 1 else c
    return (px, py, pc), 4 * px + 2 * py + pc


_CHIP_FLIPS = (4, 2, 6)
_HBM = pl.BlockSpec(memory_space=pl.ANY)


def all_gather(shards, name):
    nt = len(shards)

    def body(*refs):
        ins, outs = refs[:nt], refs[nt:2 * nt]
        send, recv, local = refs[2 * nt:]
        pos = _mesh_pos()
        me = 4 * pos[0] + 2 * pos[1] + pos[2]
        sib, sib_id = _peer(pos, 1)

        def copy(t, k, block_id, to, src=None):
            dst = outs[t].at[block_id]
            return pltpu.make_async_remote_copy(
                src_ref=dst if src is None else src, dst_ref=dst, send_sem=send.at[t, k], recv_sem=recv.at[t, k],
                device_id=to, device_id_type=MESH)

        locals_ = [pltpu.make_async_copy(ins[t], outs[t].at[me], local.at[t]) for t in range(nt)]
        for cp in locals_:
            cp.start()
        started = []
        for j, flip in enumerate(_CHIP_FLIPS):
            for t in range(nt):
                started.append(copy(t, 1 + j, me, _peer(pos, flip)[0], src=ins[t]))
        for t in range(nt):
            started.append(copy(t, 0, me, sib, src=ins[t]))
        for cp in started:
            cp.start()
        for j, flip in enumerate(_CHIP_FLIPS):
            _, pid = _peer(pos, flip)
            for t in range(nt):
                copy(t, 1 + j, pid, sib).wait_recv()
                fwd = copy(t, 4 + j, pid, sib)
                fwd.start()
                started.append(fwd)
        for t in range(nt):
            copy(t, 0, sib_id, sib).wait_recv()
        for j, flip in enumerate(_CHIP_FLIPS):
            _, pid = _peer(pos, flip | 1)
            for t in range(nt):
                copy(t, 4 + j, pid, sib).wait_recv()
        for cp in started:
            cp.wait_send()
        for cp in locals_:
            cp.wait()

    return pl.pallas_call(
        body, name=name,
        out_shape=tuple(jax.ShapeDtypeStruct((NDEV,) + s.shape, s.dtype) for s in shards),
        in_specs=[_HBM] * nt, out_specs=tuple([_HBM] * nt),
        scratch_shapes=[pltpu.SemaphoreType.DMA((nt, NDEV - 1)), pltpu.SemaphoreType.DMA((nt, NDEV - 1)),
                        pltpu.SemaphoreType.DMA((nt,))])(*shards)


_SPLIT = dict(has_side_effects=pltpu.SideEffectType.DATAFLOW_SIDE_EFFECTING)
_SEM = pl.BlockSpec(memory_space=pltpu.SEMAPHORE)


def _exchange_start(name, srcs, lands, copies, after):
    ns, nl = len(srcs), len(lands)
    ncopy = len(copies(None, [None] * ns, [None] * nl))

    def body(*refs):
        src_refs, land_refs = refs[:ns], refs[ns:ns + nl]
        send, recv = refs[ns + nl + len(after):ns + nl + len(after) + 2]
        token = refs[-1]
        for i, (src, dst, peer) in enumerate(copies(_mesh_pos(), src_refs, land_refs)):
            pltpu.make_async_remote_copy(src_ref=src, dst_ref=dst, send_sem=send.at[i], recv_sem=recv.at[i],
                                         device_id=peer, device_id_type=MESH).start()
        token[...] = jnp.zeros_like(token)

    hbm = lambda a: pltpu.HBM(a.shape, a.dtype)
    out = pl.pallas_call(
        body, name=name,
        out_shape=(pltpu.SemaphoreType.DMA((ncopy,)), pltpu.SemaphoreType.DMA((ncopy,)),
                   *[hbm(a) for a in srcs], *[hbm(a) for a in lands], jax.ShapeDtypeStruct((8, 128), F32)),
        in_specs=[pl.BlockSpec(memory_space=pltpu.HBM)] * (ns + nl) + [_HBM] * len(after),
        out_specs=(_SEM, _SEM, *[pl.BlockSpec(memory_space=pltpu.HBM)] * (ns + nl),
                   pl.BlockSpec(memory_space=pltpu.VMEM)),
        input_output_aliases={i: 2 + i for i in range(ns + nl)},
        compiler_params=pltpu.CompilerParams(**_SPLIT))(
            *[pltpu.with_memory_space_constraint(a, pltpu.HBM) for a in list(srcs) + list(lands)], *after)
    return out[0], out[1], list(out[2:2 + ns]), list(out[2 + ns:2 + ns + nl]), out[-1]


def _exchange_wait(name, send, recv, srcs, lands, copies, after):
    ns, nl = len(srcs), len(lands)

    def body(*refs):
        src_refs, land_refs = refs[:ns], refs[ns:ns + nl]
        send_ref, recv_ref = refs[ns + nl:ns + nl + 2]
        pos = _mesh_pos()
        for i, (src, dst, peer) in enumerate(copies(pos, src_refs, land_refs, arriving=True)):
            cp = pltpu.make_async_remote_copy(src_ref=src, dst_ref=dst, send_sem=send_ref.at[i], recv_sem=recv_ref.at[i],
                                              device_id=peer, device_id_type=MESH)
            cp.wait_send()
            cp.wait_recv()

    hbm = lambda a: pltpu.HBM(a.shape, a.dtype)
    out = pl.pallas_call(
        body, name=name, out_shape=tuple(hbm(a) for a in list(srcs) + list(lands)),
        in_specs=[pl.BlockSpec(memory_space=pltpu.HBM)] * (ns + nl) + [_SEM, _SEM] + [_HBM] * len(after),
        out_specs=tuple([pl.BlockSpec(memory_space=pltpu.HBM)] * (ns + nl)),
        input_output_aliases={i: i for i in range(ns + nl)},
        compiler_params=pltpu.CompilerParams(**_SPLIT))(*srcs, *lands, send, recv, *after)
    return list(out[:ns]), list(out[ns:])


def _gather_copies(pos, srcs, lands, arriving=False):
    if pos is None:
        return [None] * (len(srcs) * (NDEV - 1))
    me = 4 * pos[0] + 2 * pos[1] + pos[2]
    out = []
    for src, land in zip(srcs, lands):
        for k in range(1, NDEV):
            peer, pid = _peer(pos, k)
            out.append((src, land.at[pid if arriving else me], peer))
    return out


def _scatter_copies(pos, srcs, lands, arriving=False):
    if pos is None:
        return [None] * (len(srcs) * (NDEV - 1))
    out = []
    for src, land in zip(srcs, lands):
        for k in range(1, NDEV):
            peer, pid = _peer(pos, k)
            out.append((src.at[pid], land.at[k - 1], peer))
    return out


def gather_start(name, shards, lands, after=()):
    return _exchange_start(name, shards, lands, _gather_copies, list(after))


def gather_wait(name, handle, after=()):
    send, recv, srcs, lands, _ = handle
    return _exchange_wait(name, send, recv, srcs, lands, _gather_copies, list(after))


def scatter_start(name, grads, after=()):
    lands = [pltpu.with_memory_space_constraint(lax.empty((NDEV - 1,) + g.shape[1:], g.dtype), pltpu.HBM)
             for g in grads]
    return _exchange_start(name, grads, lands, _scatter_copies, list(after))


def scatter_wait(name, handle, after=()):
    send, recv, srcs, lands, _ = handle
    return _exchange_wait(name, send, recv, srcs, lands, _scatter_copies, list(after))


def sum_slots(parts, name):
    def body(p_ref, o_ref):
        acc = p_ref[0]
        for d in range(1, NDEV):
            acc = acc + p_ref[d]
        o_ref[...] = acc

    return pl.pallas_call(body, name=name, out_shape=jax.ShapeDtypeStruct(parts.shape[1:], F32))(parts)


def _adamw_update(g, w_ref, m_ref, v_ref, g_ref, d_ref, nm_ref, nv_ref):
    g_ref[...] = g
    nm = ADAM_B1 * m_ref[...] + (1.0 - ADAM_B1) * g
    nv = ADAM_B2 * v_ref[...] + (1.0 - ADAM_B2) * (g * g)
    nm_ref[...] = nm
    nv_ref[...] = nv
    m_hat = nm / (1.0 - ADAM_B1 ** ADAM_STEP)
    v_hat = nv / (1.0 - ADAM_B2 ** ADAM_STEP)
    d_ref[...] = -ADAM_LR * (m_hat / (jnp.sqrt(v_hat) + ADAM_EPS) + ADAM_WD * w_ref[...])


def adamw_big(own, lands, w, m, v, me, tr, name):
    _, r, c = own[0].shape
    nt = r // tr

    def body(me_ref, *refs):
        ins, (w_ref, m_ref, v_ref), outs = refs[:2 * DEPTH], refs[2 * DEPTH:2 * DEPTH + 3], refs[2 * DEPTH + 3:]
        layer = pl.program_id(0)
        for l in range(DEPTH):
            @pl.when(layer == l)
            def _(l=l):
                g = ins[2 * l][...].astype(F32)
                for s in range(NDEV - 1):
                    g = g + ins[2 * l + 1][s].astype(F32)
                _adamw_update(g, w_ref, m_ref, v_ref, *outs)

    pick = lambda l: (lambda layer, i, me_ref: jnp.where(layer == l, i, 0))
    in_specs = []
    for l in range(DEPTH):
        in_specs.append(pl.BlockSpec((None, tr, c), lambda layer, i, me_ref, f=pick(l): (me_ref[0], f(layer, i, me_ref), 0)))
        in_specs.append(pl.BlockSpec((NDEV - 1, tr, c), lambda layer, i, me_ref, f=pick(l): (0, f(layer, i, me_ref), 0)))
    blk = pl.BlockSpec((tr, c), lambda layer, i, me_ref: (layer * nt + i, 0))
    sh = jax.ShapeDtypeStruct((DEPTH * r, c), F32)
    args = [a for l in range(DEPTH) for a in (own[l], lands[l])]
    return pl.pallas_call(
        body, name=name, out_shape=(sh, sh, sh, sh),
        grid_spec=pltpu.PrefetchScalarGridSpec(
            num_scalar_prefetch=1, grid=(DEPTH, nt), in_specs=in_specs + [blk, blk, blk],
            out_specs=(blk, blk, blk, blk)),
        compiler_params=_params(VMEM_BIG))(me, *args, w, m, v)


def adamw_many(gs, ws, ms, vs, name):
    n = len(gs)

    def body(*refs):
        outs = refs[4 * n:]
        for i in range(n):
            _adamw_update(refs[i][...], refs[n + i], refs[2 * n + i], refs[3 * n + i], *outs[4 * i:4 * i + 4])

    out_shape = tuple(jax.ShapeDtypeStruct(w.shape, F32) for w in ws for _ in range(4))
    res = pl.pallas_call(body, name=name, out_shape=out_shape)(*gs, *ws, *ms, *vs)
    return [res[4 * i:4 * i + 4] for i in range(n)]


def local_step(x, target, P, hooks):
    tb = _ret_tables()
    h = jnp.concatenate([jnp.zeros((PAD, D), F32), P["meta"], x], axis=0)
    stash = []
    for l in range(DEPTH):
        s = {"h": h}
        s["w_in"], s["cv_pw"], s["w_out"] = hooks["mixer_w"](l, h)
        proj, s["u"] = in_proj_fwd(f"in_proj_{l}", h, P["norm_mix_g"][l], s["w_in"], 256,
                                   after=hooks["first_after"] if l == 0 else None)
        s["proj"] = proj
        s["o"] = attn_fwd(proj, P["q_norm_g"][l], P["k_norm_g"][l], P["attn_sinks"][l], f"attn_fwd_{l}")
        y_att = rms_fwd(s["o"], P["attn_out_g"][l], BF16, f"rms_att_{l}")
        y_cv, s["xc"] = conv_fwd(proj, P["cv_dw_w"][l], P["cv_dw_b"][l], P["cv_ln_g"][l], P["cv_ln_b"][l],
                                 s["cv_pw"], P["cv_out_g"][l], f"conv_fwd_{l}")
        y_ret, s["states"] = ret_fwd(proj, P["ret_gn_g"][l], tb, f"ret_fwd_{l}")
        ycat = s["ycat"] = jnp.concatenate([y_att, y_cv, y_ret], axis=1)
        h = mm_nn(f"out_proj_{l}", ycat, s["w_out"], 512, res=h)
        s["h1"] = h
        s["ffn_up"], s["ffn_down"] = hooks["ffn_w"](l, h)
        s["f"], s["u2"] = ffn_up_fwd(f"ffn_up_{l}", h, P["norm_ffn_g"][l], s["ffn_up"])
        a, s["gc"], s["uc"] = ffn_act_fwd(s["f"], P["ffn_dw_w"][l], P["ffn_dw_b"][l], f"ffn_act_fwd_{l}")
        s["a"] = a
        h = ffn_down_fwd(f"ffn_down_{l}", a, s["ffn_down"], h)
        stash.append(s)

    dh, dhm, loss = loss_head(h, target)
    G = {n: [None] * DEPTH for n in ("norm_mix_g", "q_norm_g", "k_norm_g", "attn_sinks", "attn_out_g", "cv_dw_w",
                                     "cv_dw_b", "cv_ln_g", "cv_ln_b", "cv_out_g", "ret_gn_g", "norm_ffn_g",
                                     "ffn_dw_w", "ffn_dw_b")}
    for l in reversed(range(DEPTH)):
        s = stash[l]
        da = ffn_down_dx(f"ffn_down_dx_{l}", dhm, s["ffn_down"])
        dw_down = ffn_down_dw(f"ffn_down_dw_{l}", s["a"], dhm)
        dfg, dfu, dwg, dwu, dbg, dbu = ffn_act_bwd(s["f"], s["gc"], s["uc"], da, P["ffn_dw_w"][l], f"ffn_act_bwd_{l}")
        G["ffn_dw_w"][l] = jnp.concatenate([dwg, dwu], axis=0)
        G["ffn_dw_b"][l] = jnp.concatenate([dbg, dbu], axis=0)
        du2 = ffn_up_dx(f"ffn_up_dx_{l}", dfg, dfu, s["ffn_up"])
        dw_up = ffn_up_dw(f"ffn_up_dw_{l}", s["u2"], dfg, dfu)
        after = hooks["ffn_grads"](l, dw_down, dw_up)
        dh, dhm, G["norm_ffn_g"][l] = rms_bwd(s["h1"], P["norm_ffn_g"][l], du2, dh, f"rms_ffn_bwd_{l}", after=after)
        dycat = mm_nt(f"out_proj_dx_{l}", dhm, s["w_out"], 512)
        dw_out = mm_ta(f"out_proj_dw_{l}", s["ycat"], dhm, 512)
        do, _, G["attn_out_g"][l] = rms_bwd(s["o"], P["attn_out_g"][l], dycat, None, f"rms_att_bwd_{l}")
        (dq, dk_c, dk_p, dk_m, dv_c, dv_p, dv_m, G["q_norm_g"][l], G["k_norm_g"][l], dsk) = attn_bwd(
            s["proj"], do, P["q_norm_g"][l], P["k_norm_g"][l], P["attn_sinks"][l], f"attn_bwd_{l}")
        G["attn_sinks"][l] = dsk[:, 0]
        shift = lambda z: jnp.concatenate([z[T:], jnp.zeros((T, NKV * HD), F32)], axis=0)
        dk = (dk_c + shift(dk_p)).at[:T].add(dk_m)
        dv = (dv_c + shift(dv_p)).at[:T].add(dv_m)
        (dca, dcb, G["cv_dw_w"][l], G["cv_dw_b"][l], G["cv_ln_g"][l], G["cv_ln_b"][l], G["cv_out_g"][l],
         dpw) = conv_bwd(s["proj"], s["xc"], dycat, P["cv_dw_w"][l], P["cv_ln_g"][l], P["cv_ln_b"][l], s["cv_pw"],
                         P["cv_out_g"][l], f"conv_bwd_{l}")
        dret, G["ret_gn_g"][l] = ret_bwd(s["proj"], dycat, s["states"], P["ret_gn_g"][l], tb, f"ret_bwd_{l}")
        dproj = jnp.concatenate([dq, dk.astype(BF16), dv.astype(BF16), dca, dcb, dret], axis=1)
        du = mm_nn(f"in_proj_dx_{l}", dproj, s["w_in"], 512)
        dw_in = mm_ta(f"in_proj_dw_{l}", dproj, s["u"], 768)
        after = hooks["mixer_grads"](l, dw_out, dpw, dw_in)
        dh, dhm, G["norm_mix_g"][l] = rms_bwd(s["h"], P["norm_mix_g"][l], du, dh, f"rms_mix_bwd_{l}", after=after,
                                              last=(l == 0))
    return loss[0, 0], dh, dhm[PAD:T], G


_SMALL = ("meta", "norm_mix_g", "q_norm_g", "k_norm_g", "attn_sinks", "attn_out_g", "cv_dw_w", "cv_dw_b", "cv_ln_g",
          "cv_ln_b", "cv_out_g", "ret_gn_g", "norm_ffn_g", "ffn_dw_w", "ffn_dw_b")
_BIG = ("w_in", "cv_pw", "w_out", "ffn_up", "ffn_down")
_TRANSPOSED = ("w_in", "ffn_up")
_ORDER = ("meta", "norm_mix_g", "w_in", "q_norm_g", "k_norm_g", "attn_sinks", "attn_out_g", "cv_dw_w", "cv_dw_b",
          "cv_ln_g", "cv_ln_b", "cv_pw", "cv_out_g", "ret_gn_g", "w_out", "norm_ffn_g", "ffn_up", "ffn_dw_w",
          "ffn_dw_b", "ffn_down")
_SMALL_SHARDED = {"meta": D, "cv_dw_w": CV, "ffn_dw_w": 2 * D_FF}


def _pack(arrs):
    flat = jnp.concatenate([a.reshape(-1) for a in arrs])
    n = flat.shape[0]
    rows = -(-n // 1024) * 8
    return jnp.pad(flat, (0, rows * 128 - n)).reshape(rows, 128)


def _unpack(packed, shapes):
    flat = packed.reshape(-1)
    out, off = [], 0
    for s in shapes:
        n = int(np.prod(s))
        out.append(flat[off:off + n].reshape(s))
        off += n
    return out


def kernel(x, meta, norm_mix_g, w_in, q_norm_g, k_norm_g, attn_sinks, attn_out_g, cv_dw_w, cv_dw_b, cv_ln_g, cv_ln_b, cv_pw, cv_out_g, ret_gn_g, w_out, norm_ffn_g, ffn_up, ffn_dw_w, ffn_dw_b, ffn_down, loss_target, m_meta, m_norm_mix_g, m_w_in, m_q_norm_g, m_k_norm_g, m_attn_sinks, m_attn_out_g, m_cv_dw_w, m_cv_dw_b, m_cv_ln_g, m_cv_ln_b, m_cv_pw, m_cv_out_g, m_ret_gn_g, m_w_out, m_norm_ffn_g, m_ffn_up, m_ffn_dw_w, m_ffn_dw_b, m_ffn_down, v_meta, v_norm_mix_g, v_w_in, v_q_norm_g, v_k_norm_g, v_attn_sinks, v_attn_out_g, v_cv_dw_w, v_cv_dw_b, v_cv_ln_g, v_cv_ln_b, v_cv_pw, v_cv_out_g, v_ret_gn_g, v_w_out, v_norm_ffn_g, v_ffn_up, v_ffn_dw_w, v_ffn_dw_b, v_ffn_down):
    W = dict(meta=meta, norm_mix_g=norm_mix_g, w_in=w_in, q_norm_g=q_norm_g, k_norm_g=k_norm_g,
             attn_sinks=attn_sinks, attn_out_g=attn_out_g, cv_dw_w=cv_dw_w, cv_dw_b=cv_dw_b, cv_ln_g=cv_ln_g,
             cv_ln_b=cv_ln_b, cv_pw=cv_pw, cv_out_g=cv_out_g, ret_gn_g=ret_gn_g, w_out=w_out,
             norm_ffn_g=norm_ffn_g, ffn_up=ffn_up, ffn_dw_w=ffn_dw_w, ffn_dw_b=ffn_dw_b, ffn_down=ffn_down)
    M = dict(meta=m_meta, norm_mix_g=m_norm_mix_g, w_in=m_w_in, q_norm_g=m_q_norm_g, k_norm_g=m_k_norm_g,
             attn_sinks=m_attn_sinks, attn_out_g=m_attn_out_g, cv_dw_w=m_cv_dw_w, cv_dw_b=m_cv_dw_b,
             cv_ln_g=m_cv_ln_g, cv_ln_b=m_cv_ln_b, cv_pw=m_cv_pw, cv_out_g=m_cv_out_g, ret_gn_g=m_ret_gn_g,
             w_out=m_w_out, norm_ffn_g=m_norm_ffn_g, ffn_up=m_ffn_up, ffn_dw_w=m_ffn_dw_w, ffn_dw_b=m_ffn_dw_b,
             ffn_down=m_ffn_down)
    V = dict(meta=v_meta, norm_mix_g=v_norm_mix_g, w_in=v_w_in, q_norm_g=v_q_norm_g, k_norm_g=v_k_norm_g,
             attn_sinks=v_attn_sinks, attn_out_g=v_attn_out_g, cv_dw_w=v_cv_dw_w, cv_dw_b=v_cv_dw_b,
             cv_ln_g=v_cv_ln_g, cv_ln_b=v_cv_ln_b, cv_pw=v_cv_pw, cv_out_g=v_cv_out_g, ret_gn_g=v_ret_gn_g,
             w_out=v_w_out, norm_ffn_g=v_norm_ffn_g, ffn_up=v_ffn_up, ffn_dw_w=v_ffn_dw_w, ffn_dw_b=v_ffn_dw_b,
             ffn_down=v_ffn_down)
    me = 4 * lax.axis_index("x") + 2 * lax.axis_index("y") + lax.axis_index("c")
    for n in _TRANSPOSED:
        W[n], M[n], V[n] = (a.transpose(0, 2, 1) for a in (W[n], M[n], V[n]))

    sh = {n: [W[n][l].astype(BF16) for l in range(DEPTH)] for n in _BIG}
    mix = lambda l: [sh["w_in"][l], sh["cv_pw"][l], sh["w_out"][l]]
    ffn = lambda l: [sh["ffn_up"][l], sh["ffn_down"][l]]
    first = all_gather(mix(0) + [meta, cv_dw_w, ffn_dw_w], "gather_first")
    g_meta, g_cdw, g_fdw = first[3:6]

    def landing(shards):
        return [lax.dynamic_update_slice(lax.empty((NDEV,) + s.shape, s.dtype), s[None], (me,) + (0,) * s.ndim)
                for s in shards]

    gathers = {("ffn", 0): gather_start("gather_ffn0_start", ffn(0), landing(ffn(0)), after=[first[0]])}
    gathers["mix", 1] = gather_start("gather_mix1_start", mix(1), landing(mix(1)), after=[gathers["ffn", 0][4]])
    gathers["ffn", 1] = gather_start("gather_ffn1_start", ffn(1), landing(ffn(1)), after=[gathers["mix", 1][4]])

    def mixer_w(l, h):
        g_in, g_pw, g_out = first[0:3] if l == 0 else gather_wait(f"gather_mix{l}_wait", gathers["mix", l], after=[h])[1]
        return g_in.reshape(IN_W, D), g_pw.reshape(CV, CV), g_out.reshape(D, D)

    def ffn_w(l, h1):
        return gather_wait(f"gather_ffn{l}_wait", gathers["ffn", l], after=[h1])[1]

    scatters = {}

    def ffn_grads(l, dw_down, dw_up):
        scatters["ffn", l] = scatter_start(f"scatter_ffn{l}_start", [dw_up, dw_down.reshape(NDEV, DNR, D)])
        return scatters["ffn", l][4]

    def mixer_grads(l, dw_out, dpw, dw_in):
        grads = [dw_in.reshape(NDEV, IN_W // NDEV, D), dpw.astype(BF16).reshape(NDEV, CV // NDEV, CV),
                 dw_out.reshape(NDEV, D // NDEV, D)]
        scatters["mix", l] = scatter_start(f"scatter_mix{l}_start", grads)
        return scatters["mix", l][4]

    P = dict(
        meta=g_meta.transpose(1, 0, 2).reshape(N_META, D),
        cv_dw_w=g_cdw.transpose(1, 2, 0, 3).reshape(DEPTH, CONV_K, CV),
        ffn_dw_w=g_fdw.transpose(1, 0, 2, 3),
        ffn_dw_b=ffn_dw_b.reshape(DEPTH, NDEV, 1, UPW),
        attn_sinks=attn_sinks,
    )
    for n in ("norm_mix_g", "q_norm_g", "k_norm_g", "attn_out_g", "cv_dw_b", "cv_ln_g", "cv_ln_b", "cv_out_g",
              "ret_gn_g", "norm_ffn_g"):
        P[n] = W[n].reshape(DEPTH, 1, -1)

    hooks = dict(mixer_w=mixer_w, ffn_w=ffn_w, ffn_grads=ffn_grads, mixer_grads=mixer_grads,
                 first_after=gathers["ffn", 1][4])
    loss_part, dx, dmeta, G = local_step(x[0], loss_target[0], P, hooks)

    small_full = {
        "meta": dmeta,
        "cv_dw_w": jnp.stack(G["cv_dw_w"]),
        "ffn_dw_w": jnp.stack([g.transpose(1, 0, 2).reshape(FFN_K, 2 * D_FF) for g in G["ffn_dw_w"]]),
        "ffn_dw_b": jnp.stack([g.reshape(2 * D_FF) for g in G["ffn_dw_b"]]),
        "attn_sinks": jnp.stack(G["attn_sinks"]),
    }
    for n in _SMALL:
        if n not in small_full:
            small_full[n] = jnp.stack([g.reshape(-1) for g in G[n]])
    shapes = [small_full[n].shape for n in _SMALL] + [(1,)]
    packed = _pack([small_full[n] for n in _SMALL] + [loss_part.reshape(1)])
    small_exchange = gather_start("reduce_small_start", [packed], landing([packed]), after=[dx])

    out = {}
    tiles = {"w_in": 144, "cv_pw": 32, "w_out": 128, "ffn_up": 176, "ffn_down": 176}
    me1 = me.astype(jnp.int32).reshape(1)
    own, lands = {n: [None] * DEPTH for n in _BIG}, {n: [None] * DEPTH for n in _BIG}

    def arrived(kind, names, after):
        for l in range(DEPTH):
            srcs, got = scatter_wait(f"scatter_{kind}{l}_wait", scatters[kind, l], after=after)
            for n, s_, g_ in zip(names, srcs, got):
                own[n][l], lands[n][l] = s_, g_

    def update(names):
        for n in names:
            shard = W[n].shape
            rows, cols = shard[0] * shard[1], shard[2]
            res = adamw_big(own[n], lands[n], W[n].reshape(rows, cols), M[n].reshape(rows, cols),
                            V[n].reshape(rows, cols), me1, tiles[n], f"adamw_{n}")
            out[n] = [r.reshape(shard) for r in res]

    arrived("ffn", ("ffn_up", "ffn_down"), [small_exchange[4]])
    update(("ffn_up", "ffn_down"))
    slots = gather_wait("reduce_small_wait", small_exchange, after=[out["ffn_down"][0]])[1][0]
    summed = _unpack(sum_slots(slots, "reduce_small_sum"), shapes)
    loss = summed[-1][0]
    small_g = []
    for n, g in zip(_SMALL, summed):
        if n in _SMALL_SHARDED:
            width = _SMALL_SHARDED[n] // NDEV
            g = lax.dynamic_slice_in_dim(g, me * width, width, axis=g.ndim - 1)
        small_g.append(g)
    flat2 = lambda a: a.reshape(-1, a.shape[-1])
    res = adamw_many([flat2(g) for g in small_g], *[[flat2(X[n]) for n in _SMALL] for X in (W, M, V)], "adamw_small")
    for n, r in zip(_SMALL, res):
        out[n] = [a.reshape(W[n].shape) for a in r]
    arrived("mix", ("w_in", "cv_pw", "w_out"), [out["ffn_down"][0], res[0][0]])
    update(("w_in", "cv_pw", "w_out"))
    for n in _TRANSPOSED:
        out[n] = [r.transpose(0, 2, 1) for r in out[n]]

    return (loss, dx[None], *[out[n][0] for n in _ORDER], *[out[n][1] for n in _ORDER],
            *[out[n][2] for n in _ORDER], *[out[n][3] for n in _ORDER])
```

```python
import functools
import math

import numpy as np
import jax
import jax.numpy as jnp
from jax import lax
from jax.experimental import pallas as pl
from jax.experimental.pallas import tpu as pltpu

F32 = jnp.float32
BF16 = jnp.bfloat16

D = 1024
SEQ = 2048
DEPTH = 2
T = 128
L = SEQ + T
NB = L // T
N_META = 16
PAD = T - N_META
HD = 64
NQ = 8
NKV = 2
GQA = NQ // NKV
CV = 256
CONV_K = 31
RH = 4
D_FF = 2816
FFN_K = 3
IN_W = 2304
RMS_EPS = 1e-6
LN_EPS = 1e-5
NEG = -1e30
NDEV = 8
UPW = 2 * D_FF // NDEV
DNR = D_FF // NDEV
NPAIR = NDEV // 2

ADAM_LR, ADAM_B1, ADAM_B2, ADAM_EPS, ADAM_WD, ADAM_STEP = 0.001, 0.9, 0.999, 1e-08, 0.01, 10

VMEM_BIG = 56 * 1024 * 1024

MESH = pl.DeviceIdType.MESH


def _params(vmem=None):
    return pltpu.CompilerParams(vmem_limit_bytes=vmem) if vmem else None


def _dot(a, b, nt=False):
    return lax.dot_general(a, b, (((1,), (1 if nt else 0,)), ((), ())), preferred_element_type=F32)


def _sig(x):
    return 1.0 / (1.0 + jnp.exp(-x))


def _bf(x):
    return x.astype(BF16)


def _stack_rows(rows):
    idx = lax.broadcasted_iota(jnp.int32, (len(rows), rows[0].shape[1]), 0)
    out = jnp.zeros((len(rows), rows[0].shape[1]), F32)
    for r, v in enumerate(rows):
        out = jnp.where(idx == r, v, out)
    return out


def rms_bwd(x, g, dy, dres, name, after=None, dy_col=0, last=False):
    n, w = x.shape
    has_res = dres is not None
    deps = [] if after is None else [after]

    def body(x_ref, g_ref, dy_ref, *rest):
        rest = rest[:len(rest) - 3 - len(deps)] + rest[len(rest) - 3:]
        if has_res:
            dres_ref, dx_ref, dxm_ref, dg_ref = rest
        else:
            dx_ref, dxm_ref, dg_ref = rest
        i = pl.program_id(0)
        xv = x_ref[...]
        r = lax.rsqrt(jnp.mean(xv * xv, axis=-1, keepdims=True) + RMS_EPS)
        xh = xv * r
        dyv = dy_ref[...]
        dxh = dyv * g_ref[...]
        dx = r * (dxh - xh * jnp.mean(dxh * xh, axis=-1, keepdims=True))
        if has_res:
            dx = dx + dres_ref[...]
        dx_ref[...] = dx
        if last:
            @pl.when(i == 0)
            def _():
                dxm_ref[...] = dx
        else:
            rows = i * T + lax.broadcasted_iota(jnp.int32, (T, 1), 0)
            dxm_ref[...] = jnp.where(rows >= PAD, dx, 0.0).astype(BF16)
        part = jnp.sum(dyv * xh, axis=0, keepdims=True)

        @pl.when(i == 0)
        def _():
            dg_ref[...] = part

        @pl.when(i > 0)
        def _():
            dg_ref[...] += part

    row = pl.BlockSpec((T, w), lambda i: (i, 0))
    vec = pl.BlockSpec((1, w), lambda i: (0, 0))
    ins = [x, g, dy] + ([dres] if has_res else []) + deps
    if last:
        out_shape = (jax.ShapeDtypeStruct((n - T, w), F32), jax.ShapeDtypeStruct((T, w), F32))
        out_specs = (pl.BlockSpec((T, w), lambda i: (jnp.maximum(i - 1, 0), 0)), pl.BlockSpec((T, w), lambda i: (0, 0)))
    else:
        out_shape = (jax.ShapeDtypeStruct((n, w), F32), jax.ShapeDtypeStruct((n, w), BF16))
        out_specs = (row, row)
    return pl.pallas_call(
        body, name=name, out_shape=(*out_shape, jax.ShapeDtypeStruct((1, w), F32)), grid=(n // T,),
        in_specs=[row, vec, pl.BlockSpec((T, w), lambda i: (i, dy_col))] + ([row] if has_res else [])
        + [_HBM] * len(deps),
        out_specs=(*out_specs, vec))(*ins)


def loss_head(h, target):
    def body(h_ref, t_ref, dh_ref, dhm_ref, loss_ref):
        n = pl.program_id(0)
        e = jnp.where(n > 0, h_ref[...] - t_ref[...], 0.0)
        dh = e * (1.0 / D)
        dh_ref[...] = dh
        dhm_ref[...] = dh.astype(BF16)
        part = jnp.sum(jnp.sum(e * e, axis=1, keepdims=True), axis=0, keepdims=True) * (0.5 / D)

        @pl.when(n == 0)
        def _():
            loss_ref[...] = jnp.zeros_like(loss_ref)

        @pl.when(n > 0)
        def _():
            loss_ref[...] += jnp.broadcast_to(part, loss_ref.shape)

    row = pl.BlockSpec((T, D), lambda n: (n, 0))
    return pl.pallas_call(
        body, name="loss_head",
        out_shape=(jax.ShapeDtypeStruct((L, D), F32), jax.ShapeDtypeStruct((L, D), BF16),
                   jax.ShapeDtypeStruct((8, 128), F32)),
        grid=(NB,),
        in_specs=[row, pl.BlockSpec((T, D), lambda n: (jnp.maximum(n - 1, 0), 0))],
        out_specs=(row, row, pl.BlockSpec((8, 128), lambda n: (0, 0))))(h, target)


def _mm(name, a, b, *, grid, a_spec, b_spec, o_spec, out_shape, nt=False, ta=False, red=False, res=None,
        res_spec=None):
    def body(a_ref, b_ref, *rest):
        o_ref = rest[-1]
        av = a_ref[...]
        bv = b_ref[...]
        if bv.ndim == 3:
            bv = bv.reshape(bv.shape[0] * bv.shape[1], bv.shape[2])
        if ta:
            acc = lax.dot_general(av, bv, (((0,), (0,)), ((), ())), preferred_element_type=F32)
        else:
            acc = _dot(av, bv, nt)
        if red:
            k = pl.program_id(0)

            @pl.when(k == 0)
            def _():
                o_ref[...] = acc

            @pl.when(k > 0)
            def _():
                o_ref[...] += acc
        else:
            if res is not None:
                rows = lax.broadcasted_iota(jnp.int32, (acc.shape[0], 1), 0)
                acc = rest[0][...] + jnp.where(rows >= PAD, acc, 0.0)
            o_ref[...] = acc.astype(o_ref.dtype)

    ins = [a, b] + ([res] if res is not None else [])
    specs = [a_spec, b_spec] + ([res_spec] if res is not None else [])
    return pl.pallas_call(body, name=name, out_shape=out_shape, grid=grid, in_specs=specs, out_specs=o_spec,
                          compiler_params=_params(VMEM_BIG))(*ins)


def mm_nn(name, a, b, tn, out_dtype=F32, res=None):
    m, k = a.shape
    n = b.shape[1]
    return _mm(name, a, b, grid=(n // tn,),
               a_spec=pl.BlockSpec((m, k), lambda j: (0, 0)), b_spec=pl.BlockSpec((k, tn), lambda j: (0, j)),
               o_spec=pl.BlockSpec((m, tn), lambda j: (0, j)), out_shape=jax.ShapeDtypeStruct((m, n), out_dtype),
               res=res, res_spec=pl.BlockSpec((m, tn), lambda j: (0, j)))


def mm_nt(name, a, b, tn):
    m, k = a.shape
    n = b.shape[0]
    return _mm(name, a, b, grid=(n // tn,), nt=True,
               a_spec=pl.BlockSpec((m, k), lambda j: (0, 0)), b_spec=pl.BlockSpec((tn, k), lambda j: (j, 0)),
               o_spec=pl.BlockSpec((m, tn), lambda j: (0, j)), out_shape=jax.ShapeDtypeStruct((m, n), F32))


def mm_ta(name, a, b, tm, out_dtype=BF16):
    k, m = a.shape
    n = b.shape[1]
    return _mm(name, a, b, grid=(m // tm,), ta=True,
               a_spec=pl.BlockSpec((k, tm), lambda j: (0, j)), b_spec=pl.BlockSpec((k, n), lambda j: (0, 0)),
               o_spec=pl.BlockSpec((tm, n), lambda j: (j, 0)), out_shape=jax.ShapeDtypeStruct((m, n), out_dtype))


def _rms_mm(name, h, g, b, *, grid, b_spec, o_spec, out_shape, after=None):
    deps = [] if after is None else [after]

    def body(h_ref, g_ref, b_ref, *rest):
        o_ref, u_ref = rest[len(deps):]

        @pl.when(pl.program_id(0) == 0)
        def _():
            gv = g_ref[...]
            for c in range(NB):
                rows = slice(c * T, (c + 1) * T)
                xv = h_ref[rows, :]
                r = lax.rsqrt(jnp.mean(xv * xv, axis=-1, keepdims=True) + RMS_EPS)
                u_ref[rows, :] = (xv * r * gv).astype(BF16)

        o_ref[...] = _dot(u_ref[...], b_ref[...], nt=True)

    whole = lambda r: pl.BlockSpec((r, D), lambda j: (0, 0))
    return pl.pallas_call(
        body, name=name, out_shape=(out_shape, jax.ShapeDtypeStruct((L, D), BF16)), grid=grid,
        in_specs=[whole(L), whole(1), b_spec] + [_HBM] * len(deps), out_specs=(o_spec, whole(L)),
        compiler_params=_params(VMEM_BIG))(h, g, b, *deps)


def out_proj_fwd(name, o, og, y_cv, y_ret, w, res, tn):
    wa = o.shape[1]

    def body(o_ref, og_ref, cv_ref, ret_ref, w_ref, r_ref, out_ref, ycat_ref):
        @pl.when(pl.program_id(0) == 0)
        def _():
            gv = og_ref[...]
            for c in range(NB):
                rows = slice(c * T, (c + 1) * T)
                ycat_ref[rows, 0:wa] = _rms_rows(o_ref[rows, :], gv)[0].astype(BF16)
                ycat_ref[rows, wa:wa + CV] = cv_ref[rows, :]
                ycat_ref[rows, wa + CV:D] = ret_ref[rows, :]

        acc = _dot(ycat_ref[...], w_ref[...])
        rows = lax.broadcasted_iota(jnp.int32, (L, 1), 0)
        out_ref[...] = r_ref[...] + jnp.where(rows >= PAD, acc, 0.0)

    whole = lambda w_: pl.BlockSpec((L, w_), lambda j: (0, 0))
    tile = pl.BlockSpec((L, tn), lambda j: (0, j))
    return pl.pallas_call(
        body, name=name, out_shape=(jax.ShapeDtypeStruct((L, D), F32), jax.ShapeDtypeStruct((L, D), BF16)),
        grid=(D // tn,),
        in_specs=[whole(wa), pl.BlockSpec((1, wa), lambda j: (0, 0)), whole(CV), whole(RW),
                  pl.BlockSpec((D, tn), lambda j: (0, j)), tile],
        out_specs=(tile, whole(D)), compiler_params=_params(VMEM_BIG))(o, og, y_cv, y_ret, w, res)


def in_proj_fwd(name, h, g, wt, tn, after=None):
    return _rms_mm(name, h, g, wt, grid=(IN_W // tn,), b_spec=pl.BlockSpec((tn, D), lambda j: (j, 0)),
                   o_spec=pl.BlockSpec((L, tn), lambda j: (0, j)), out_shape=jax.ShapeDtypeStruct((L, IN_W), F32),
                   after=after)


def ffn_up_fwd(name, h, g, wupt):
    return _rms_mm(name, h, g, wupt, grid=(NDEV,), b_spec=pl.BlockSpec((None, UPW, D), lambda j: (j, 0, 0)),
                   o_spec=pl.BlockSpec((None, L, UPW), lambda j: (j, 0, 0)),
                   out_shape=jax.ShapeDtypeStruct((NDEV, L, UPW), F32))


def _slab_specs():
    gate = pl.BlockSpec((None, L, UPW), lambda j: (jnp.minimum(j, NPAIR - 1), 0, 0))
    up = pl.BlockSpec((None, L, UPW), lambda j: (jnp.maximum(j - NPAIR, 0), 0, 0))
    return gate, up


def ffn_up_dx(name, dfg, dfu, wupt):
    def body(g_ref, u_ref, b_ref, o_ref):
        j = pl.program_id(0)

        @pl.when(j == 0)
        def _():
            o_ref[...] = _dot(g_ref[...], b_ref[...])

        @pl.when(jnp.logical_and(j > 0, j < NPAIR))
        def _():
            o_ref[...] += _dot(g_ref[...], b_ref[...])

        @pl.when(j >= NPAIR)
        def _():
            o_ref[...] += _dot(u_ref[...], b_ref[...])

    gate, up = _slab_specs()
    return pl.pallas_call(
        body, name=name, out_shape=jax.ShapeDtypeStruct((L, D), F32), grid=(NDEV,),
        in_specs=[gate, up, pl.BlockSpec((None, UPW, D), lambda j: (j, 0, 0))],
        out_specs=pl.BlockSpec((L, D), lambda j: (0, 0)), compiler_params=_params(VMEM_BIG))(dfg, dfu, wupt)


def ffn_up_dw(name, u2, dfg, dfu):
    tdot = lambda a, b: lax.dot_general(a, b, (((0,), (0,)), ((), ())), preferred_element_type=F32)

    def body(a_ref, g_ref, u_ref, o_ref):
        j = pl.program_id(0)

        @pl.when(j < NPAIR)
        def _():
            o_ref[...] = tdot(g_ref[...], a_ref[...]).astype(o_ref.dtype)

        @pl.when(j >= NPAIR)
        def _():
            o_ref[...] = tdot(u_ref[...], a_ref[...]).astype(o_ref.dtype)

    gate, up = _slab_specs()
    return pl.pallas_call(
        body, name=name, out_shape=jax.ShapeDtypeStruct((NDEV, UPW, D), BF16), grid=(NDEV,),
        in_specs=[pl.BlockSpec((L, D), lambda j: (0, 0)), gate, up],
        out_specs=pl.BlockSpec((None, UPW, D), lambda j: (j, 0, 0)), compiler_params=_params(VMEM_BIG))(u2, dfg, dfu)


def ffn_down_fwd(name, a, wdn, res, tn=256):
    def body(a_ref, b_ref, r_ref, o_ref):
        acc = jnp.zeros((L, tn), F32)
        for g in range(NPAIR):
            bv = b_ref[2 * g:2 * g + 2]
            acc = acc + _dot(a_ref[g], bv.reshape(2 * DNR, tn))
        rows = lax.broadcasted_iota(jnp.int32, (L, 1), 0)
        o_ref[...] = r_ref[...] + jnp.where(rows >= PAD, acc, 0.0)

    return pl.pallas_call(
        body, name=name, out_shape=jax.ShapeDtypeStruct((L, D), F32), grid=(D // tn,),
        in_specs=[pl.BlockSpec((NPAIR, L, UPW), lambda j: (0, 0, 0)),
                  pl.BlockSpec((NDEV, DNR, tn), lambda j: (0, 0, j)),
                  pl.BlockSpec((L, tn), lambda j: (0, j))],
        out_specs=pl.BlockSpec((L, tn), lambda j: (0, j)),
        compiler_params=_params(VMEM_BIG))(a, wdn, res)


def ffn_down_dx(name, dh, wdn):
    return _mm(name, dh, wdn, grid=(NPAIR,), nt=True,
               a_spec=pl.BlockSpec((L, D), lambda g: (0, 0)),
               b_spec=pl.BlockSpec((2, DNR, D), lambda g: (g, 0, 0)),
               o_spec=pl.BlockSpec((None, L, UPW), lambda g: (g, 0, 0)),
               out_shape=jax.ShapeDtypeStruct((NPAIR, L, UPW), F32))


def ffn_down_dw(name, a, dh):
    return _mm(name, a, dh, grid=(NPAIR,), ta=True,
               a_spec=pl.BlockSpec((None, L, UPW), lambda g: (g, 0, 0)),
               b_spec=pl.BlockSpec((L, D), lambda g: (0, 0)),
               o_spec=pl.BlockSpec((UPW, D), lambda g: (g, 0)),
               out_shape=jax.ShapeDtypeStruct((D_FF, D), BF16))


_SLOPES = [2.0 ** (-8.0 * (h + 1) / NQ) for h in range(NQ)]
_SCALE = HD ** -0.5


QR = GQA * T
KC = 3 * T


def _attn_geometry(g, n, sink_ref):
    row = lax.broadcasted_iota(jnp.int32, (QR, KC), 0)
    col = lax.broadcasted_iota(jnp.int32, (QR, KC), 1)
    i = row & (T - 1)
    j = col & (T - 1)
    blk = col // T
    d_meta = n * T + i - j
    ok_meta = (j >= PAD) & (d_meta >= 0)
    ok_prev = j > i + jnp.where(n >= 2, 0, T)
    ok_cur = j <= i - jnp.where(n >= 1, 0, T)
    ok = ((blk == 0) & ok_meta) | ((blk == 1) & ok_prev) | ((blk == 2) & ok_cur)
    dist = jnp.where(blk == 0, jnp.minimum(d_meta, T), jnp.where(blk == 1, T + i - j, i - j)).astype(F32)
    head = lax.broadcasted_iota(jnp.int32, (QR, 1), 0) // T
    slope = jnp.zeros((QR, 1), F32)
    sink = jnp.zeros((QR, 1), F32)
    for hh in range(GQA):
        slope = jnp.where(head == hh, jnp.where(g == 0, _SLOPES[hh], _SLOPES[GQA + hh]), slope)
        sink = jnp.where(head == hh, sink_ref[g * GQA + hh], sink)
    return ok, slope * dist, sink


def _attn_probs(qn, keys, geo):
    ok, penalty, sink = geo
    s = jnp.where(ok, _dot(qn, keys, nt=True) * _SCALE - penalty, NEG)
    m = jnp.maximum(jnp.max(s, axis=-1, keepdims=True), sink)
    e = jnp.exp(s - m)
    e_sink = jnp.exp(sink - m)
    inv = 1.0 / (e.sum(axis=-1, keepdims=True) + e_sink)
    return e, inv, e_sink * inv


def _rms_rows(x, g):
    r = lax.rsqrt(jnp.mean(x * x, axis=-1, keepdims=True) + RMS_EPS)
    xh = x * r
    return xh * g, xh, r


def _rms_rows_bwd(dy, g, xh, r):
    dxh = dy * g
    return r * (dxh - xh * jnp.mean(dxh * xh, axis=-1, keepdims=True))


def _rows3(m_ref, p_ref, c_ref):
    return jnp.concatenate([m_ref[...], p_ref[...], c_ref[...]], axis=0)


_ATT_K, _ATT_V = 4, 5


def _nat_specs():
    qspec = pl.BlockSpec((T, GQA * HD), lambda n, g: (n, g))
    kv = lambda col: (pl.BlockSpec((T, 2 * HD), lambda n, g: (n, col)),
                      pl.BlockSpec((T, 2 * HD), lambda n, g: (jnp.maximum(n - 1, 0), col)),
                      pl.BlockSpec((T, 2 * HD), lambda n, g: (0, col)))
    return qspec, kv(_ATT_K), kv(_ATT_V)


def _group_lanes(g, x):
    return jnp.where(g == 0, x[:, :HD], x[:, HD:])


def _stack_heads(x):
    return jnp.concatenate([x[:, hh * HD:(hh + 1) * HD] for hh in range(GQA)], axis=0)


def _unstack_heads(x):
    return jnp.concatenate([x[hh * T:(hh + 1) * T] for hh in range(GQA)], axis=1)


def attn_fwd(proj, qg, kg, sinks, name):
    def body(sink_ref, q_ref, kc_ref, kp_ref, km_ref, vc_ref, vp_ref, vm_ref, qg_ref, kg_ref, o_ref):
        n = pl.program_id(0)
        g = pl.program_id(1)
        geo = _attn_geometry(g, n, sink_ref)
        keys = _bf(_rms_rows(_group_lanes(g, _rows3(km_ref, kp_ref, kc_ref)), kg_ref[...])[0])
        vals = _bf(_group_lanes(g, _rows3(vm_ref, vp_ref, vc_ref)))
        qn = _bf(_rms_rows(_stack_heads(q_ref[...]), qg_ref[...])[0])
        e, inv, _ = _attn_probs(qn, keys, geo)
        o_ref[...] = _unstack_heads(_dot(_bf(e), vals) * inv)

    qspec, (kc, kp, km), (vc, vp, vm) = _nat_specs()
    vec = pl.BlockSpec((1, HD), lambda n, g: (0, 0))
    return pl.pallas_call(
        body, name=name, out_shape=jax.ShapeDtypeStruct((L, NQ * HD), F32), grid=(NB, NKV),
        in_specs=[pl.BlockSpec(memory_space=pltpu.SMEM), qspec, kc, kp, km, vc, vp, vm, vec, vec],
        out_specs=qspec)(sinks, proj, proj, proj, proj, proj, proj, proj, qg, kg)


def attn_bwd(proj, do, qg, kg, sinks, name):
    def body(sink_ref, q_ref, kc_ref, kp_ref, km_ref, vc_ref, vp_ref, vm_ref, do_ref, qg_ref, kg_ref,
             dq_ref, dkc_ref, dkp_ref, dkm_ref, dvc_ref, dvp_ref, dvm_ref, dqg_ref, dkg_ref, dsk_ref):
        n = pl.program_id(0)
        g = pl.program_id(1)

        @pl.when(jnp.logical_and(g == 0, n == 0))
        def _():
            for r in (dkm_ref, dvm_ref, dqg_ref, dkg_ref, dsk_ref):
                r[...] = jnp.zeros_like(r)

        geo = _attn_geometry(g, n, sink_ref)
        kgv = kg_ref[...]
        qgv = qg_ref[...]
        kn_f, kh, kr = _rms_rows(_group_lanes(g, _rows3(km_ref, kp_ref, kc_ref)), kgv)
        keys = _bf(kn_f)
        vals = _bf(_group_lanes(g, _rows3(vm_ref, vp_ref, vc_ref)))
        qn_f, qh, qr = _rms_rows(_stack_heads(q_ref[...]), qgv)
        qn = _bf(qn_f)
        e, inv, p_sink = _attn_probs(qn, keys, geo)
        dob = _bf(_stack_heads(do_ref[...]))
        p = e * inv
        dp = _dot(dob, vals, nt=True)
        delta = (p * dp).sum(axis=-1, keepdims=True)
        head_row = lax.broadcasted_iota(jnp.int32, (NQ, 128), 0)
        dsk = jnp.zeros((NQ, 128), F32)
        for hh in range(GQA):
            part = -jnp.sum((p_sink * delta)[hh * T:(hh + 1) * T], axis=0, keepdims=True)
            dsk = jnp.where(head_row == g * GQA + hh, part, dsk)
        dsk_ref[...] += dsk
        ds = p * (dp - delta)
        dqn = _dot(_bf(ds), keys) * _SCALE
        dq_ref[...] = _unstack_heads(_rms_rows_bwd(dqn, qgv, qh, qr)).astype(dq_ref.dtype)
        dqg_ref[...] += jnp.sum(dqn * qh, axis=0, keepdims=True)
        dkn = _dot(_bf(ds.T), qn) * _SCALE
        dkg_ref[...] += jnp.sum(dkn * kh, axis=0, keepdims=True)
        dk_all = _rms_rows_bwd(dkn, kgv, kh, kr)
        dv_all = _dot(_bf(p.T), dob)

        for gg in range(NKV):
            @pl.when(g == gg)
            def _(gg=gg):
                lanes = slice(gg * HD, (gg + 1) * HD)
                dkm_ref[:, lanes] += dk_all[0:T]
                dkp_ref[:, lanes] = dk_all[T:2 * T]
                dkc_ref[:, lanes] = dk_all[2 * T:3 * T]
                dvm_ref[:, lanes] += dv_all[0:T]
                dvp_ref[:, lanes] = dv_all[T:2 * T]
                dvc_ref[:, lanes] = dv_all[2 * T:3 * T]

    qspec, (kc, kp, km), (vc, vp, vm) = _nat_specs()
    vec = pl.BlockSpec((1, HD), lambda n, g: (0, 0))
    cur = pl.BlockSpec((T, 2 * HD), lambda n, g: (n, 0))
    meta = pl.BlockSpec((T, 2 * HD), lambda n, g: (0, 0))
    kv_shape = jax.ShapeDtypeStruct((L, 2 * HD), F32)
    meta_shape = jax.ShapeDtypeStruct((T, 2 * HD), F32)
    vec_shape = jax.ShapeDtypeStruct((1, HD), F32)
    return pl.pallas_call(
        body, name=name,
        out_shape=(jax.ShapeDtypeStruct((L, NQ * HD), BF16), kv_shape, kv_shape, meta_shape, kv_shape, kv_shape,
                   meta_shape, vec_shape, vec_shape, jax.ShapeDtypeStruct((NQ, 128), F32)),
        grid=(NB, NKV),
        in_specs=[pl.BlockSpec(memory_space=pltpu.SMEM), qspec, kc, kp, km, vc, vp, vm, qspec, vec, vec],
        out_specs=(qspec, cur, cur, meta, cur, cur, meta, vec, vec,
                   pl.BlockSpec((NQ, 128), lambda n, g: (0, 0))))(
            sinks, proj, proj, proj, proj, proj, proj, proj, do, qg, kg)


RW = RH * HD


def _ret_tables():
    h = np.arange(RH, dtype=np.float64)
    lg = np.log1p(-np.exp2(-5.0 - h))
    idx = np.arange(T, dtype=np.float64)
    diff = idx[:, None] - idx[None, :]
    decay = np.where(diff[None] >= 0, np.exp(np.maximum(diff, 0.0)[None] * lg[:, None, None]), 0.0)
    zeta = np.exp((T - 1 - idx)[None, :] * lg[:, None])
    xi = np.exp((idx + 1.0)[None, :] * lg[:, None])
    cd = np.exp(T * lg)
    lanes = lambda a: np.repeat(a.T, HD, axis=1)
    head_of = np.arange(RW) // HD
    same = (head_of[:, None] == head_of[None, :]).astype(np.float64)
    f = lambda a: jnp.asarray(a, F32)
    return dict(decay=f(decay), zeta=f(lanes(zeta)), xi=f(lanes(xi)), cd=f(np.repeat(cd, HD)[None, :]),
                head=f((head_of[None, :] == np.arange(RH)[:, None]).astype(np.float64)[:, None, :]),
                same=f(same), avg=jnp.asarray(same / HD, BF16))


def _seg_mean(x, avg):
    hi = _bf(x)
    lo = _bf(x - hi.astype(F32))
    return _dot(hi, avg) + _dot(lo, avg)


def _ret_specs(col0, order):
    return lambda col: pl.BlockSpec((T, RW), lambda i: (order(i), col0 + col))


def _ret_chunk(q, kf, v, s, tb):
    dec, xi, head = tb
    vb = _bf(v)
    kb = _bf(kf)
    y = _dot(_bf(q * xi), _bf(s))
    a = []
    for h in range(RH):
        a.append(_dot(_bf(q * head[h]), kb, nt=True) * dec[h])
        y = y + head[h] * _dot(_bf(a[h]), vb)
    return a, y


def _gn_rows(y):
    mu = jnp.mean(y, axis=-1, keepdims=True)
    yc = y - mu
    rstd = lax.rsqrt(jnp.mean(yc * yc, axis=-1, keepdims=True) + LN_EPS)
    return yc * rstd, rstd


_RET_COL0 = _RET_Q = 5


def _ret_consts(tb):
    names = ("decay", "zeta", "xi", "cd", "head", "same", "avg")
    full = lambda a: pl.BlockSpec(a.shape, lambda i: (0,) * a.ndim)
    return [tb[n] for n in names], [full(tb[n]) for n in names]


def _ret_inputs(n, q_ref, k_ref, v_ref):
    rows = n * T + lax.broadcasted_iota(jnp.int32, (T, 1), 0)
    valid = rows >= PAD
    return q_ref[...], jnp.where(valid, k_ref[...] * (HD ** -0.5), 0.0), v_ref[...], valid


def ret_fwd(proj, gng, tb, name):
    def body(q_ref, k_ref, v_ref, g_ref, dec_ref, zeta_ref, xi_ref, cd_ref, head_ref, same_ref, avg_ref, gng_ref,
             y_ref, st_ref, s_scr):
        n = pl.program_id(0)

        @pl.when(n == 0)
        def _():
            s_scr[...] = jnp.zeros_like(s_scr)

        s = s_scr[...]
        st_ref[...] = s
        q, kf, v, _ = _ret_inputs(n, q_ref, k_ref, v_ref)
        _, y = _ret_chunk(q, kf, v, s, (dec_ref, xi_ref[...], head_ref))
        s_scr[...] = cd_ref[...] * s + same_ref[...] * _dot(_bf((kf * zeta_ref[...]).T), _bf(v))
        avg = avg_ref[...]
        yc = y - _seg_mean(y, avg)
        yh = yc * lax.rsqrt(_seg_mean(yc * yc, avg) + LN_EPS)
        gv = g_ref[...]
        y_ref[...] = (gv * _sig(gv) * (yh * gng_ref[...])).astype(y_ref.dtype)

    col = _ret_specs(_RET_COL0, lambda i: i)
    consts, cspecs = _ret_consts(tb)
    return pl.pallas_call(
        body, name=name,
        out_shape=(jax.ShapeDtypeStruct((L, RW), BF16), jax.ShapeDtypeStruct((NB, RW, RW), F32)),
        grid=(NB,),
        in_specs=[col(0), col(1), col(2), col(3)] + cspecs + [pl.BlockSpec((1, RW), lambda i: (0, 0))],
        out_specs=(pl.BlockSpec((T, RW), lambda i: (i, 0)), pl.BlockSpec((None, RW, RW), lambda i: (i, 0, 0))),
        scratch_shapes=[pltpu.VMEM((RW, RW), F32)])(proj, proj, proj, proj, *consts, gng)


def ret_bwd(proj, dycat, states, gng, tb, name):
    def body(q_ref, k_ref, v_ref, g_ref, do_ref, st_ref, dec_ref, zeta_ref, xi_ref, cd_ref, head_ref, same_ref,
             avg_ref, gng_ref, d_ref, dgn_ref, ds_scr):
        i = pl.program_id(0)
        n = NB - 1 - i

        @pl.when(i == 0)
        def _():
            ds_scr[...] = jnp.zeros_like(ds_scr)

        dsn = _bf(ds_scr[...])
        s = st_ref[...]
        sb = _bf(s)
        q, kf, v, valid = _ret_inputs(n, q_ref, k_ref, v_ref)
        xi, zeta, avg = xi_ref[...], zeta_ref[...], avg_ref[...]
        a, y = _ret_chunk(q, kf, v, s, (dec_ref, xi, head_ref))
        yc = y - _seg_mean(y, avg)
        rstd = lax.rsqrt(_seg_mean(yc * yc, avg) + LN_EPS)
        yh = yc * rstd
        gv = g_ref[...]
        sg = _sig(gv)
        sil = gv * sg
        gn = gng_ref[...]
        dout = do_ref[...]
        d_ref[:, 3 * RW:4 * RW] = (dout * (yh * gn) * (sg * (1.0 + gv * (1.0 - sg)))).astype(d_ref.dtype)
        dyh = dout * sil * gn
        part = jnp.sum(dout * sil * yh, axis=0, keepdims=True)

        @pl.when(i == 0)
        def _():
            dgn_ref[...] = part

        @pl.when(i > 0)
        def _():
            dgn_ref[...] += part

        dy = rstd * (dyh - _seg_mean(dyh, avg) - yh * _seg_mean(dyh * yh, avg))
        dyb = _bf(dy)
        vb, kb, qb = _bf(v), _bf(kf), _bf(q)
        dq = _dot(dyb, sb, nt=True) * xi
        dkf = _dot(vb, dsn, nt=True) * zeta
        dv = _dot(_bf(kf * zeta), dsn)
        for h in range(RH):
            m = head_ref[h]
            da = _dot(_bf(dy * m), vb, nt=True) * dec_ref[h]
            dv = dv + m * _dot(_bf(a[h].T), dyb)
            dq = dq + m * _dot(_bf(da), kb)
            dkf = dkf + m * _dot(_bf(da.T), qb)
        d_ref[:, 0:RW] = dq.astype(d_ref.dtype)
        d_ref[:, RW:2 * RW] = jnp.where(valid, dkf * (HD ** -0.5), 0.0).astype(d_ref.dtype)
        d_ref[:, 2 * RW:3 * RW] = dv.astype(d_ref.dtype)
        ds_scr[...] = cd_ref[...] * ds_scr[...] + same_ref[...] * _dot(_bf((q * xi).T), dyb)

    back = lambda i: NB - 1 - i
    col = _ret_specs(_RET_COL0, back)
    consts, cspecs = _ret_consts(tb)
    vec = pl.BlockSpec((1, RW), lambda i: (0, 0))
    return pl.pallas_call(
        body, name=name,
        out_shape=(jax.ShapeDtypeStruct((L, 4 * RW), BF16), jax.ShapeDtypeStruct((1, RW), F32)),
        grid=(NB,),
        in_specs=[col(0), col(1), col(2), col(3), _ret_specs(3, back)(0),
                  pl.BlockSpec((None, RW, RW), lambda i: (back(i), 0, 0))] + cspecs + [vec],
        out_specs=(pl.BlockSpec((T, 4 * RW), lambda i: (back(i), 0)), vec),
        scratch_shapes=[pltpu.VMEM((RW, RW), F32)])(proj, proj, proj, proj, dycat, states, *consts, gng)


HALO = 32
TAP0 = HALO - (CONV_K - 1)


def _conv_specs():
    cur = lambda col=0: pl.BlockSpec((T, CV), lambda c: (c, col))
    before = lambda col=0: pl.BlockSpec((HALO, CV), lambda c: (jnp.maximum(c * (T // HALO) - 1, 0), col))
    after = pl.BlockSpec((HALO, CV), lambda c: (jnp.minimum((c + 1) * (T // HALO), L // HALO - 1), 0))
    full = lambda r, w: pl.BlockSpec((r, w), lambda c: (0, 0))
    return cur, before, after, full


_CONV_A, _CONV_B = 3, 4
_DY_CONV = 2


def _conv_post(xc, lg_ref, lb_ref):
    xh, rstd = _gn_rows(xc)
    z = xh * lg_ref[...] + lb_ref[...]
    return xh, rstd, z, _sig(z)


def conv_fwd(proj, w, b, lg, lb, pw, og, name):
    def body(ca_ref, cah_ref, cb_ref, cbh_ref, w_ref, b_ref, lg_ref, lb_ref, pw_ref, og_ref, y_ref, xc_ref, u_scr):
        c = pl.program_id(0)
        u_scr[0:HALO, :] = jnp.where(c > 0, cah_ref[...] * _sig(cbh_ref[...]), 0.0)
        u_scr[HALO:HALO + T, :] = ca_ref[...] * _sig(cb_ref[...])
        acc = jnp.zeros((T, CV), F32)
        for k in range(CONV_K):
            acc = acc + w_ref[k:k + 1, :] * u_scr[TAP0 + k:TAP0 + k + T, :]
        xc = acc + b_ref[...]
        xc_ref[...] = xc
        _, _, z, sg = _conv_post(xc, lg_ref, lb_ref)
        zp = _dot(_bf(z * sg), pw_ref[...])
        y_ref[...] = _rms_rows(zp, og_ref[...])[0].astype(y_ref.dtype)

    cur, before, _, full = _conv_specs()
    vec = full(1, CV)
    seq = jax.ShapeDtypeStruct((L, CV), F32)
    return pl.pallas_call(
        body, name=name, out_shape=(jax.ShapeDtypeStruct((L, CV), BF16), seq), grid=(NB,),
        in_specs=[cur(_CONV_A), before(_CONV_A), cur(_CONV_B), before(_CONV_B), full(CONV_K, CV), vec, vec, vec,
                  full(CV, CV), vec],
        out_specs=(cur(), cur()),
        scratch_shapes=[pltpu.VMEM((HALO + T, CV), F32)])(proj, proj, proj, proj, w, b, lg, lb, pw, og)


def conv_bwd(proj, xc, dycat, w, lg, lb, pw, og, name):
    def body1(xc_ref, dy_ref, lg_ref, lb_ref, pw_ref, og_ref, dxc_ref, db_ref, dlg_ref, dlb_ref, dog_ref, dpw_ref):
        c = pl.program_id(0)

        @pl.when(c == 0)
        def _():
            for r in (db_ref, dlg_ref, dlb_ref, dog_ref, dpw_ref):
                r[...] = jnp.zeros_like(r)

        xh, rstd, z, sg = _conv_post(xc_ref[...], lg_ref, lb_ref)
        s = z * sg
        zp = _dot(_bf(s), pw_ref[...])
        ogv = og_ref[...]
        _, zph, r2 = _rms_rows(zp, ogv)
        dyv = dy_ref[...]
        dog_ref[...] += jnp.sum(dyv * zph, axis=0, keepdims=True)
        dzpb = _bf(_rms_rows_bwd(dyv, ogv, zph, r2))
        dpw_ref[...] += _dot(_bf(s.T), dzpb)
        dz = _dot(dzpb, pw_ref[...], nt=True) * (sg * (1.0 + z * (1.0 - sg)))
        dlg_ref[...] += jnp.sum(dz * xh, axis=0, keepdims=True)
        dlb_ref[...] += jnp.sum(dz, axis=0, keepdims=True)
        dxh = dz * lg_ref[...]
        dxc = rstd * (dxh - jnp.mean(dxh, axis=-1, keepdims=True) - xh * jnp.mean(dxh * xh, axis=-1, keepdims=True))
        db_ref[...] += jnp.sum(dxc, axis=0, keepdims=True)
        dxc_ref[...] = dxc

    def body2(dx_ref, dxa_ref, ca_ref, cb_ref, w_ref, dca_ref, dcb_ref, dw_ref, d_scr):
        c = pl.program_id(0)

        @pl.when(c == 0)
        def _():
            dw_ref[...] = jnp.zeros_like(dw_ref)

        d_scr[0:T, :] = dx_ref[...]
        d_scr[T:T + HALO, :] = jnp.where(c < NB - 1, dxa_ref[...], 0.0)
        ca = ca_ref[...]
        sg = _sig(cb_ref[...])
        u = ca * sg
        du = jnp.zeros((T, CV), F32)
        for k in range(CONV_K):
            off = CONV_K - 1 - k
            dj = d_scr[off:off + T, :]
            du = du + w_ref[k:k + 1, :] * dj
            dw_ref[k:k + 1, :] += jnp.sum(u * dj, axis=0, keepdims=True)
        dca_ref[...] = (du * sg).astype(dca_ref.dtype)
        dcb_ref[...] = (du * ca * sg * (1.0 - sg)).astype(dcb_ref.dtype)

    cur, _, after, full = _conv_specs()
    seq = jax.ShapeDtypeStruct((L, CV), F32)
    vsh = jax.ShapeDtypeStruct((1, CV), F32)
    vec = full(1, CV)
    dxc, db, dlg, dlb, dog, dpw = pl.pallas_call(
        body1, name=name + "_a",
        out_shape=(seq, vsh, vsh, vsh, vsh, jax.ShapeDtypeStruct((CV, CV), F32)),
        grid=(NB,),
        in_specs=[cur(), cur(_DY_CONV), vec, vec, full(CV, CV), vec],
        out_specs=(cur(), vec, vec, vec, vec, full(CV, CV)))(xc, dycat, lg, lb, pw, og)
    dca, dcb, dw = pl.pallas_call(
        body2, name=name + "_b", grid=(NB,),
        out_shape=(jax.ShapeDtypeStruct((L, CV), BF16), jax.ShapeDtypeStruct((L, CV), BF16),
                   jax.ShapeDtypeStruct((CONV_K, CV), F32)),
        in_specs=[cur(), after, cur(_CONV_A), cur(_CONV_B), full(CONV_K, CV)],
        out_specs=(cur(), cur(), full(CONV_K, CV)),
        scratch_shapes=[pltpu.VMEM((T + HALO, CV), F32)])(dxc, dxc, proj, proj, w)
    return dca, dcb, dw, db, dlg, dlb, dog, dpw


HR = 8


def _ffn_specs():
    cur = lambda off: pl.BlockSpec((None, T, UPW), lambda j, c: (j + off, c, 0))
    before = lambda off: pl.BlockSpec((None, HR, UPW), lambda j, c: (j + off, jnp.maximum(c * (T // HR) - 1, 0), 0))
    after = lambda off: pl.BlockSpec(
        (None, HR, UPW), lambda j, c: (j + off, jnp.minimum((c + 1) * (T // HR), L // HR - 1), 0))
    wspec = lambda off, r: pl.BlockSpec((None, r, UPW), lambda j, c: (j + off, 0, 0))
    return cur, before, after, wspec


HR16 = 16


def _ffn_after16():
    return pl.BlockSpec((None, HR16, UPW), lambda j, c: (j, jnp.minimum((c + 1) * (T // HR16), L // HR16 - 1), 0))


def ffn_act_fwd(f, w, b, name):
    strip = 16

    def body(fg_ref, fgh_ref, fu_ref, fuh_ref, wg_ref, wu_ref, bg_ref, bu_ref, a_ref, gc_ref, uc_ref, hg_scr, hu_scr):
        c = pl.program_id(1)
        for scr, x_ref, xh_ref in ((hg_scr, fg_ref, fgh_ref), (hu_scr, fu_ref, fuh_ref)):
            scr[0:HR, :] = jnp.where(c > 0, xh_ref[...], 0.0)
            scr[HR:HR + strip, :] = x_ref[0:strip, :]
        wg = [wg_ref[k:k + 1, :] for k in range(FFN_K)]
        wu = [wu_ref[k:k + 1, :] for k in range(FFN_K)]
        bg, bu = bg_ref[...], bu_ref[...]

        def conv(src, base, w, b):
            acc = b
            for k in range(FFN_K):
                o = base - (FFN_K - 1) + k
                acc = acc + w[k] * src[o:o + strip, :]
            return acc

        for s in range(T // strip):
            r = s * strip
            gate = conv(hg_scr, HR, wg, bg) if s == 0 else conv(fg_ref, r, wg, bg)
            up = conv(hu_scr, HR, wu, bu) if s == 0 else conv(fu_ref, r, wu, bu)
            gc_ref[r:r + strip, :] = gate.astype(BF16)
            uc_ref[r:r + strip, :] = up.astype(BF16)
            a_ref[r:r + strip, :] = (gate * _sig(gate) * up).astype(BF16)

    cur, before, _, wspec = _ffn_specs()
    pre = jax.ShapeDtypeStruct((NPAIR, L, UPW), BF16)
    return pl.pallas_call(
        body, name=name, out_shape=(jax.ShapeDtypeStruct((NPAIR, L, UPW), BF16), pre, pre), grid=(NPAIR, NB),
        in_specs=[cur(0), before(0), cur(NPAIR), before(NPAIR), wspec(0, FFN_K), wspec(NPAIR, FFN_K),
                  wspec(0, 1), wspec(NPAIR, 1)],
        out_specs=(cur(0), cur(0), cur(0)),
        scratch_shapes=[pltpu.VMEM((HR + strip, UPW), F32), pltpu.VMEM((HR + strip, UPW), F32)])(
            f, f, f, f, w, w, b, b)


def ffn_act_bwd(f, gc, uc, da, w, name):
    ext = T + HR
    sub = 8

    def body(fg_ref, fu_ref, gc_ref, gca_ref, uc_ref, uca_ref, da_ref, daa_ref, wg_ref, wu_ref,
             dfg_ref, dfu_ref, dwg_ref, dwu_ref, dbg_ref, dbu_ref, dg_scr, du_scr):
        c = pl.program_id(1)
        wg = [wg_ref[k:k + 1, :] for k in range(FFN_K)]
        wu = [wu_ref[k:k + 1, :] for k in range(FFN_K)]

        for r in range(0, ext, HR16):
            rows = min(HR16, ext - r)
            if r < T:
                gate, up, dav = gc_ref[r:r + HR16, :].astype(F32), uc_ref[r:r + HR16, :].astype(F32), da_ref[r:r + HR16, :]
            else:
                gate, up = gca_ref[...].astype(F32)[0:rows], uca_ref[...].astype(F32)[0:rows]
                dav = jnp.where(c < NB - 1, daa_ref[...], 0.0)
            sg = _sig(gate)
            dg_scr[r:r + rows, :] = dav * up * (sg * (1.0 + gate * (1.0 - sg)))
            du_scr[r:r + rows, :] = dav * gate * sg

        def back(d_scr, x_ref, w, df_ref, dw_ref, db_ref):
            accs = [jnp.zeros((sub, UPW), F32) for _ in range(FFN_K + 1)]
            for s in range(T // (2 * sub)):
                pieces = []
                for r in (2 * s * sub, (2 * s + 1) * sub):
                    xv = x_ref[r:r + sub, :]
                    df = jnp.zeros((sub, UPW), F32)
                    for k in range(FFN_K):
                        off = FFN_K - 1 - k
                        dj = d_scr[r + off:r + off + sub, :]
                        df = df + w[k] * dj
                        accs[k] = accs[k] + xv * dj
                        if off == 0:
                            accs[FFN_K] = accs[FFN_K] + dj
                    pieces.append(df)
                df_ref[2 * s * sub:2 * (s + 1) * sub, :] = jnp.concatenate(pieces, axis=0).astype(BF16)
            dwp = _stack_rows([jnp.sum(a, axis=0, keepdims=True) for a in accs[:FFN_K]])
            dbp = jnp.sum(accs[FFN_K], axis=0, keepdims=True)

            @pl.when(c == 0)
            def _():
                dw_ref[...] = dwp
                db_ref[...] = dbp

            @pl.when(c > 0)
            def _():
                dw_ref[...] += dwp
                db_ref[...] += dbp

        back(dg_scr, fg_ref, wg, dfg_ref, dwg_ref, dbg_ref)
        back(du_scr, fu_ref, wu, dfu_ref, dwu_ref, dbu_ref)

    cur, _, after, wspec = _ffn_specs()
    slab = jax.ShapeDtypeStruct((NPAIR, L, UPW), BF16)
    wsh = jax.ShapeDtypeStruct((NPAIR, FFN_K, UPW), F32)
    bsh = jax.ShapeDtypeStruct((NPAIR, 1, UPW), F32)
    return pl.pallas_call(
        body, name=name, out_shape=(slab, slab, wsh, wsh, bsh, bsh), grid=(NPAIR, NB),
        in_specs=[cur(0), cur(NPAIR), cur(0), _ffn_after16(), cur(0), _ffn_after16(), cur(0), after(0),
                  wspec(0, FFN_K), wspec(NPAIR, FFN_K)],
        out_specs=(cur(0), cur(0), wspec(0, FFN_K), wspec(0, FFN_K), wspec(0, 1), wspec(0, 1)),
        scratch_shapes=[pltpu.VMEM((ext, UPW), F32), pltpu.VMEM((ext, UPW), F32)])(
            f, f, gc, gc, uc, uc, da, da, w, w)


def _mesh_pos():
    return lax.axis_index("x"), lax.axis_index("y"), lax.axis_index("c")


def _peer(pos, k):
    x, y, c = pos
    px = 1 - x if k & 4 else x
    py = 1 - y if k & 2 else y
    pc = 1 - c if k & 1 else c
    return (px, py, pc), 4 * px + 2 * py + pc


_CHIP_FLIPS = (4, 2, 6)
_HBM = pl.BlockSpec(memory_space=pl.ANY)


def all_gather(shards, name):
    nt = len(shards)

    def body(*refs):
        ins, outs = refs[:nt], refs[nt:2 * nt]
        send, recv, local = refs[2 * nt:]
        pos = _mesh_pos()
        me = 4 * pos[0] + 2 * pos[1] + pos[2]
        sib, sib_id = _peer(pos, 1)

        def copy(t, k, block_id, to, src=None):
            dst = outs[t].at[block_id]
            return pltpu.make_async_remote_copy(
                src_ref=dst if src is None else src, dst_ref=dst, send_sem=send.at[t, k], recv_sem=recv.at[t, k],
                device_id=to, device_id_type=MESH)

        locals_ = [pltpu.make_async_copy(ins[t], outs[t].at[me], local.at[t]) for t in range(nt)]
        for cp in locals_:
            cp.start()
        started = []
        for j, flip in enumerate(_CHIP_FLIPS):
            for t in range(nt):
                started.append(copy(t, 1 + j, me, _peer(pos, flip)[0], src=ins[t]))
        for t in range(nt):
            started.append(copy(t, 0, me, sib, src=ins[t]))
        for cp in started:
            cp.start()
        for j, flip in enumerate(_CHIP_FLIPS):
            _, pid = _peer(pos, flip)
            for t in range(nt):
                copy(t, 1 + j, pid, sib).wait_recv()
                fwd = copy(t, 4 + j, pid, sib)
                fwd.start()
                started.append(fwd)
        for t in range(nt):
            copy(t, 0, sib_id, sib).wait_recv()
        for j, flip in enumerate(_CHIP_FLIPS):
            _, pid = _peer(pos, flip | 1)
            for t in range(nt):
                copy(t, 4 + j, pid, sib).wait_recv()
        for cp in started:
            cp.wait_send()
        for cp in locals_:
            cp.wait()

    return pl.pallas_call(
        body, name=name,
        out_shape=tuple(jax.ShapeDtypeStruct((NDEV,) + s.shape, s.dtype) for s in shards),
        in_specs=[_HBM] * nt, out_specs=tuple([_HBM] * nt),
        scratch_shapes=[pltpu.SemaphoreType.DMA((nt, NDEV - 1)), pltpu.SemaphoreType.DMA((nt, NDEV - 1)),
                        pltpu.SemaphoreType.DMA((nt,))])(*shards)


_SPLIT = dict(has_side_effects=pltpu.SideEffectType.DATAFLOW_SIDE_EFFECTING)
_SEM = pl.BlockSpec(memory_space=pltpu.SEMAPHORE)


def _exchange_start(name, srcs, lands, copies, after):
    ns, nl = len(srcs), len(lands)
    ncopy = len(copies(None, [None] * ns, [None] * nl))

    def body(*refs):
        src_refs, land_refs = refs[:ns], refs[ns:ns + nl]
        send, recv = refs[ns + nl + len(after):ns + nl + len(after) + 2]
        token = refs[-1]
        for i, (src, dst, peer) in enumerate(copies(_mesh_pos(), src_refs, land_refs)):
            pltpu.make_async_remote_copy(src_ref=src, dst_ref=dst, send_sem=send.at[i], recv_sem=recv.at[i],
                                         device_id=peer, device_id_type=MESH).start()
        token[...] = jnp.zeros_like(token)

    hbm = lambda a: pltpu.HBM(a.shape, a.dtype)
    out = pl.pallas_call(
        body, name=name,
        out_shape=(pltpu.SemaphoreType.DMA((ncopy,)), pltpu.SemaphoreType.DMA((ncopy,)),
                   *[hbm(a) for a in srcs], *[hbm(a) for a in lands], jax.ShapeDtypeStruct((8, 128), F32)),
        in_specs=[pl.BlockSpec(memory_space=pltpu.HBM)] * (ns + nl) + [_HBM] * len(after),
        out_specs=(_SEM, _SEM, *[pl.BlockSpec(memory_space=pltpu.HBM)] * (ns + nl),
                   pl.BlockSpec(memory_space=pltpu.VMEM)),
        input_output_aliases={i: 2 + i for i in range(ns + nl)},
        compiler_params=pltpu.CompilerParams(**_SPLIT))(
            *[pltpu.with_memory_space_constraint(a, pltpu.HBM) for a in list(srcs) + list(lands)], *after)
    return out[0], out[1], list(out[2:2 + ns]), list(out[2 + ns:2 + ns + nl]), out[-1]


def _exchange_wait(name, send, recv, srcs, lands, copies, after):
    ns, nl = len(srcs), len(lands)

    def body(*refs):
        src_refs, land_refs = refs[:ns], refs[ns:ns + nl]
        send_ref, recv_ref = refs[ns + nl:ns + nl + 2]
        pos = _mesh_pos()
        for i, (src, dst, peer) in enumerate(copies(pos, src_refs, land_refs, arriving=True)):
            cp = pltpu.make_async_remote_copy(src_ref=src, dst_ref=dst, send_sem=send_ref.at[i], recv_sem=recv_ref.at[i],
                                              device_id=peer, device_id_type=MESH)
            cp.wait_send()
            cp.wait_recv()

    hbm = lambda a: pltpu.HBM(a.shape, a.dtype)
    out = pl.pallas_call(
        body, name=name, out_shape=tuple(hbm(a) for a in list(srcs) + list(lands)),
        in_specs=[pl.BlockSpec(memory_space=pltpu.HBM)] * (ns + nl) + [_SEM, _SEM] + [_HBM] * len(after),
        out_specs=tuple([pl.BlockSpec(memory_space=pltpu.HBM)] * (ns + nl)),
        input_output_aliases={i: i for i in range(ns + nl)},
        compiler_params=pltpu.CompilerParams(**_SPLIT))(*srcs, *lands, send, recv, *after)
    return list(out[:ns]), list(out[ns:])


def _gather_copies(pos, srcs, lands, arriving=False):
    if pos is None:
        return [None] * (len(srcs) * (NDEV - 1))
    me = 4 * pos[0] + 2 * pos[1] + pos[2]
    out = []
    for src, land in zip(srcs, lands):
        for k in range(1, NDEV):
            peer, pid = _peer(pos, k)
            out.append((src, land.at[pid if arriving else me], peer))
    return out


def _scatter_copies(pos, srcs, lands, arriving=False):
    if pos is None:
        return [None] * (len(srcs) * (NDEV - 1))
    out = []
    for src, land in zip(srcs, lands):
        for k in range(1, NDEV):
            peer, pid = _peer(pos, k)
            out.append((src.at[pid], land.at[k - 1], peer))
    return out


def gather_start(name, shards, lands, after=()):
    return _exchange_start(name, shards, lands, _gather_copies, list(after))


def gather_wait(name, handle, after=()):
    send, recv, srcs, lands, _ = handle
    return _exchange_wait(name, send, recv, srcs, lands, _gather_copies, list(after))


def scatter_start(name, grads, after=()):
    lands = [pltpu.with_memory_space_constraint(lax.empty((NDEV - 1,) + g.shape[1:], g.dtype), pltpu.HBM)
             for g in grads]
    return _exchange_start(name, grads, lands, _scatter_copies, list(after))


def scatter_wait(name, handle, after=()):
    send, recv, srcs, lands, _ = handle
    return _exchange_wait(name, send, recv, srcs, lands, _scatter_copies, list(after))


def sum_slots(parts, name):
    def body(p_ref, o_ref):
        acc = p_ref[0]
        for d in range(1, NDEV):
            acc = acc + p_ref[d]
        o_ref[...] = acc

    return pl.pallas_call(body, name=name, out_shape=jax.ShapeDtypeStruct(parts.shape[1:], F32))(parts)


def _adamw_update(g, w_ref, m_ref, v_ref, g_ref, d_ref, nm_ref, nv_ref):
    g_ref[...] = g
    nm = ADAM_B1 * m_ref[...] + (1.0 - ADAM_B1) * g
    nv = ADAM_B2 * v_ref[...] + (1.0 - ADAM_B2) * (g * g)
    nm_ref[...] = nm
    nv_ref[...] = nv
    m_hat = nm / (1.0 - ADAM_B1 ** ADAM_STEP)
    v_hat = nv / (1.0 - ADAM_B2 ** ADAM_STEP)
    d_ref[...] = -ADAM_LR * (m_hat / (jnp.sqrt(v_hat) + ADAM_EPS) + ADAM_WD * w_ref[...])


def adamw_big(own, lands, w, m, v, me, tr, name):
    _, r, c = own[0].shape
    nt = r // tr

    def body(me_ref, *refs):
        ins, (w_ref, m_ref, v_ref), outs = refs[:2 * DEPTH], refs[2 * DEPTH:2 * DEPTH + 3], refs[2 * DEPTH + 3:]
        layer = pl.program_id(0)
        for l in range(DEPTH):
            @pl.when(layer == l)
            def _(l=l):
                g = ins[2 * l][...].astype(F32)
                for s in range(NDEV - 1):
                    g = g + ins[2 * l + 1][s].astype(F32)
                _adamw_update(g, w_ref, m_ref, v_ref, *outs)

    pick = lambda l: (lambda layer, i, me_ref: jnp.where(layer == l, i, 0))
    in_specs = []
    for l in range(DEPTH):
        in_specs.append(pl.BlockSpec((None, tr, c), lambda layer, i, me_ref, f=pick(l): (me_ref[0], f(layer, i, me_ref), 0)))
        in_specs.append(pl.BlockSpec((NDEV - 1, tr, c), lambda layer, i, me_ref, f=pick(l): (0, f(layer, i, me_ref), 0)))
    blk = pl.BlockSpec((tr, c), lambda layer, i, me_ref: (layer * nt + i, 0))
    sh = jax.ShapeDtypeStruct((DEPTH * r, c), F32)
    args = [a for l in range(DEPTH) for a in (own[l], lands[l])]
    return pl.pallas_call(
        body, name=name, out_shape=(sh, sh, sh, sh),
        grid_spec=pltpu.PrefetchScalarGridSpec(
            num_scalar_prefetch=1, grid=(DEPTH, nt), in_specs=in_specs + [blk, blk, blk],
            out_specs=(blk, blk, blk, blk)),
        compiler_params=_params(VMEM_BIG))(me, *args, w, m, v)


def adamw_many(gs, ws, ms, vs, name):
    n = len(gs)

    def body(*refs):
        outs = refs[4 * n:]
        for i in range(n):
            _adamw_update(refs[i][...], refs[n + i], refs[2 * n + i], refs[3 * n + i], *outs[4 * i:4 * i + 4])

    out_shape = tuple(jax.ShapeDtypeStruct(w.shape, F32) for w in ws for _ in range(4))
    res = pl.pallas_call(body, name=name, out_shape=out_shape)(*gs, *ws, *ms, *vs)
    return [res[4 * i:4 * i + 4] for i in range(n)]


def local_step(x, target, P, hooks):
    tb = _ret_tables()
    h = jnp.concatenate([jnp.zeros((PAD, D), F32), P["meta"], x], axis=0)
    stash = []
    for l in range(DEPTH):
        s = {"h": h}
        s["w_in"], s["cv_pw"], s["w_out"] = hooks["mixer_w"](l, h)
        proj, s["u"] = in_proj_fwd(f"in_proj_{l}", h, P["norm_mix_g"][l], s["w_in"], 256,
                                   after=hooks["first_after"] if l == 0 else None)
        s["proj"] = proj
        s["o"] = attn_fwd(proj, P["q_norm_g"][l], P["k_norm_g"][l], P["attn_sinks"][l], f"attn_fwd_{l}")
        y_cv, s["xc"] = conv_fwd(proj, P["cv_dw_w"][l], P["cv_dw_b"][l], P["cv_ln_g"][l], P["cv_ln_b"][l],
                                 s["cv_pw"], P["cv_out_g"][l], f"conv_fwd_{l}")
        y_ret, s["states"] = ret_fwd(proj, P["ret_gn_g"][l], tb, f"ret_fwd_{l}")
        h, s["ycat"] = out_proj_fwd(f"out_proj_{l}", s["o"], P["attn_out_g"][l], y_cv, y_ret, s["w_out"], h, 512)
        s["h1"] = h
        s["ffn_up"], s["ffn_down"] = hooks["ffn_w"](l, h)
        s["f"], s["u2"] = ffn_up_fwd(f"ffn_up_{l}", h, P["norm_ffn_g"][l], s["ffn_up"])
        a, s["gc"], s["uc"] = ffn_act_fwd(s["f"], P["ffn_dw_w"][l], P["ffn_dw_b"][l], f"ffn_act_fwd_{l}")
        s["a"] = a
        h = ffn_down_fwd(f"ffn_down_{l}", a, s["ffn_down"], h)
        stash.append(s)

    dh, dhm, loss = loss_head(h, target)
    G = {n: [None] * DEPTH for n in ("norm_mix_g", "q_norm_g", "k_norm_g", "attn_sinks", "attn_out_g", "cv_dw_w",
                                     "cv_dw_b", "cv_ln_g", "cv_ln_b", "cv_out_g", "ret_gn_g", "norm_ffn_g",
                                     "ffn_dw_w", "ffn_dw_b")}
    for l in reversed(range(DEPTH)):
        s = stash[l]
        da = ffn_down_dx(f"ffn_down_dx_{l}", dhm, s["ffn_down"])
        dw_down = ffn_down_dw(f"ffn_down_dw_{l}", s["a"], dhm)
        dfg, dfu, dwg, dwu, dbg, dbu = ffn_act_bwd(s["f"], s["gc"], s["uc"], da, P["ffn_dw_w"][l], f"ffn_act_bwd_{l}")
        G["ffn_dw_w"][l] = jnp.concatenate([dwg, dwu], axis=0)
        G["ffn_dw_b"][l] = jnp.concatenate([dbg, dbu], axis=0)
        du2 = ffn_up_dx(f"ffn_up_dx_{l}", dfg, dfu, s["ffn_up"])
        dw_up = ffn_up_dw(f"ffn_up_dw_{l}", s["u2"], dfg, dfu)
        after = hooks["ffn_grads"](l, dw_down, dw_up)
        dh, dhm, G["norm_ffn_g"][l] = rms_bwd(s["h1"], P["norm_ffn_g"][l], du2, dh, f"rms_ffn_bwd_{l}", after=after)
        dycat = mm_nt(f"out_proj_dx_{l}", dhm, s["w_out"], 512)
        dw_out = mm_ta(f"out_proj_dw_{l}", s["ycat"], dhm, 512)
        do, _, G["attn_out_g"][l] = rms_bwd(s["o"], P["attn_out_g"][l], dycat, None, f"rms_att_bwd_{l}")
        (dq, dk_c, dk_p, dk_m, dv_c, dv_p, dv_m, G["q_norm_g"][l], G["k_norm_g"][l], dsk) = attn_bwd(
            s["proj"], do, P["q_norm_g"][l], P["k_norm_g"][l], P["attn_sinks"][l], f"attn_bwd_{l}")
        G["attn_sinks"][l] = dsk[:, 0]
        shift = lambda z: jnp.concatenate([z[T:], jnp.zeros((T, NKV * HD), F32)], axis=0)
        dk = (dk_c + shift(dk_p)).at[:T].add(dk_m)
        dv = (dv_c + shift(dv_p)).at[:T].add(dv_m)
        (dca, dcb, G["cv_dw_w"][l], G["cv_dw_b"][l], G["cv_ln_g"][l], G["cv_ln_b"][l], G["cv_out_g"][l],
         dpw) = conv_bwd(s["proj"], s["xc"], dycat, P["cv_dw_w"][l], P["cv_ln_g"][l], P["cv_ln_b"][l], s["cv_pw"],
                         P["cv_out_g"][l], f"conv_bwd_{l}")
        dret, G["ret_gn_g"][l] = ret_bwd(s["proj"], dycat, s["states"], P["ret_gn_g"][l], tb, f"ret_bwd_{l}")
        dproj = jnp.concatenate([dq, dk.astype(BF16), dv.astype(BF16), dca, dcb, dret], axis=1)
        du = mm_nn(f"in_proj_dx_{l}", dproj, s["w_in"], 512)
        dw_in = mm_ta(f"in_proj_dw_{l}", dproj, s["u"], 768)
        after = hooks["mixer_grads"](l, dw_out, dpw, dw_in)
        dh, dhm, G["norm_mix_g"][l] = rms_bwd(s["h"], P["norm_mix_g"][l], du, dh, f"rms_mix_bwd_{l}", after=after,
                                              last=(l == 0))
    return loss[0, 0], dh, dhm[PAD:T], G


_SMALL = ("meta", "norm_mix_g", "q_norm_g", "k_norm_g", "attn_sinks", "attn_out_g", "cv_dw_w", "cv_dw_b", "cv_ln_g",
          "cv_ln_b", "cv_out_g", "ret_gn_g", "norm_ffn_g", "ffn_dw_w", "ffn_dw_b")
_BIG = ("w_in", "cv_pw", "w_out", "ffn_up", "ffn_down")
_TRANSPOSED = ("w_in", "ffn_up")
_ORDER = ("meta", "norm_mix_g", "w_in", "q_norm_g", "k_norm_g", "attn_sinks", "attn_out_g", "cv_dw_w", "cv_dw_b",
          "cv_ln_g", "cv_ln_b", "cv_pw", "cv_out_g", "ret_gn_g", "w_out", "norm_ffn_g", "ffn_up", "ffn_dw_w",
          "ffn_dw_b", "ffn_down")
_SMALL_SHARDED = {"meta": D, "cv_dw_w": CV, "ffn_dw_w": 2 * D_FF}


def _pack(arrs):
    flat = jnp.concatenate([a.reshape(-1) for a in arrs])
    n = flat.shape[0]
    rows = -(-n // 1024) * 8
    return jnp.pad(flat, (0, rows * 128 - n)).reshape(rows, 128)


def _unpack(packed, shapes):
    flat = packed.reshape(-1)
    out, off = [], 0
    for s in shapes:
        n = int(np.prod(s))
        out.append(flat[off:off + n].reshape(s))
        off += n
    return out


def kernel(x, meta, norm_mix_g, w_in, q_norm_g, k_norm_g, attn_sinks, attn_out_g, cv_dw_w, cv_dw_b, cv_ln_g, cv_ln_b, cv_pw, cv_out_g, ret_gn_g, w_out, norm_ffn_g, ffn_up, ffn_dw_w, ffn_dw_b, ffn_down, loss_target, m_meta, m_norm_mix_g, m_w_in, m_q_norm_g, m_k_norm_g, m_attn_sinks, m_attn_out_g, m_cv_dw_w, m_cv_dw_b, m_cv_ln_g, m_cv_ln_b, m_cv_pw, m_cv_out_g, m_ret_gn_g, m_w_out, m_norm_ffn_g, m_ffn_up, m_ffn_dw_w, m_ffn_dw_b, m_ffn_down, v_meta, v_norm_mix_g, v_w_in, v_q_norm_g, v_k_norm_g, v_attn_sinks, v_attn_out_g, v_cv_dw_w, v_cv_dw_b, v_cv_ln_g, v_cv_ln_b, v_cv_pw, v_cv_out_g, v_ret_gn_g, v_w_out, v_norm_ffn_g, v_ffn_up, v_ffn_dw_w, v_ffn_dw_b, v_ffn_down):
    W = dict(meta=meta, norm_mix_g=norm_mix_g, w_in=w_in, q_norm_g=q_norm_g, k_norm_g=k_norm_g,
             attn_sinks=attn_sinks, attn_out_g=attn_out_g, cv_dw_w=cv_dw_w, cv_dw_b=cv_dw_b, cv_ln_g=cv_ln_g,
             cv_ln_b=cv_ln_b, cv_pw=cv_pw, cv_out_g=cv_out_g, ret_gn_g=ret_gn_g, w_out=w_out,
             norm_ffn_g=norm_ffn_g, ffn_up=ffn_up, ffn_dw_w=ffn_dw_w, ffn_dw_b=ffn_dw_b, ffn_down=ffn_down)
    M = dict(meta=m_meta, norm_mix_g=m_norm_mix_g, w_in=m_w_in, q_norm_g=m_q_norm_g, k_norm_g=m_k_norm_g,
             attn_sinks=m_attn_sinks, attn_out_g=m_attn_out_g, cv_dw_w=m_cv_dw_w, cv_dw_b=m_cv_dw_b,
             cv_ln_g=m_cv_ln_g, cv_ln_b=m_cv_ln_b, cv_pw=m_cv_pw, cv_out_g=m_cv_out_g, ret_gn_g=m_ret_gn_g,
             w_out=m_w_out, norm_ffn_g=m_norm_ffn_g, ffn_up=m_ffn_up, ffn_dw_w=m_ffn_dw_w, ffn_dw_b=m_ffn_dw_b,
             ffn_down=m_ffn_down)
    V = dict(meta=v_meta, norm_mix_g=v_norm_mix_g, w_in=v_w_in, q_norm_g=v_q_norm_g, k_norm_g=v_k_norm_g,
             attn_sinks=v_attn_sinks, attn_out_g=v_attn_out_g, cv_dw_w=v_cv_dw_w, cv_dw_b=v_cv_dw_b,
             cv_ln_g=v_cv_ln_g, cv_ln_b=v_cv_ln_b, cv_pw=v_cv_pw, cv_out_g=v_cv_out_g, ret_gn_g=v_ret_gn_g,
             w_out=v_w_out, norm_ffn_g=v_norm_ffn_g, ffn_up=v_ffn_up, ffn_dw_w=v_ffn_dw_w, ffn_dw_b=v_ffn_dw_b,
             ffn_down=v_ffn_down)
    me = 4 * lax.axis_index("x") + 2 * lax.axis_index("y") + lax.axis_index("c")
    for n in _TRANSPOSED:
        W[n], M[n], V[n] = (a.transpose(0, 2, 1) for a in (W[n], M[n], V[n]))

    sh = {n: [W[n][l].astype(BF16) for l in range(DEPTH)] for n in _BIG}
    mix = lambda l: [sh["w_in"][l], sh["cv_pw"][l], sh["w_out"][l]]
    ffn = lambda l: [sh["ffn_up"][l], sh["ffn_down"][l]]
    first = all_gather(mix(0) + [meta, cv_dw_w, ffn_dw_w], "gather_first")
    g_meta, g_cdw, g_fdw = first[3:6]

    def landing(shards):
        return [lax.dynamic_update_slice(lax.empty((NDEV,) + s.shape, s.dtype), s[None], (me,) + (0,) * s.ndim)
                for s in shards]

    gathers = {("ffn", 0): gather_start("gather_ffn0_start", ffn(0), landing(ffn(0)), after=[first[0]])}
    gathers["mix", 1] = gather_start("gather_mix1_start", mix(1), landing(mix(1)), after=[gathers["ffn", 0][4]])
    gathers["ffn", 1] = gather_start("gather_ffn1_start", ffn(1), landing(ffn(1)), after=[gathers["mix", 1][4]])

    def mixer_w(l, h):
        g_in, g_pw, g_out = first[0:3] if l == 0 else gather_wait(f"gather_mix{l}_wait", gathers["mix", l], after=[h])[1]
        return g_in.reshape(IN_W, D), g_pw.reshape(CV, CV), g_out.reshape(D, D)

    def ffn_w(l, h1):
        return gather_wait(f"gather_ffn{l}_wait", gathers["ffn", l], after=[h1])[1]

    scatters = {}

    def ffn_grads(l, dw_down, dw_up):
        scatters["ffn", l] = scatter_start(f"scatter_ffn{l}_start", [dw_up, dw_down.reshape(NDEV, DNR, D)])
        return scatters["ffn", l][4]

    def mixer_grads(l, dw_out, dpw, dw_in):
        grads = [dw_in.reshape(NDEV, IN_W // NDEV, D), dpw.astype(BF16).reshape(NDEV, CV // NDEV, CV),
                 dw_out.reshape(NDEV, D // NDEV, D)]
        scatters["mix", l] = scatter_start(f"scatter_mix{l}_start", grads)
        return scatters["mix", l][4]

    P = dict(
        meta=g_meta.transpose(1, 0, 2).reshape(N_META, D),
        cv_dw_w=g_cdw.transpose(1, 2, 0, 3).reshape(DEPTH, CONV_K, CV),
        ffn_dw_w=g_fdw.transpose(1, 0, 2, 3),
        ffn_dw_b=ffn_dw_b.reshape(DEPTH, NDEV, 1, UPW),
        attn_sinks=attn_sinks,
    )
    for n in ("norm_mix_g", "q_norm_g", "k_norm_g", "attn_out_g", "cv_dw_b", "cv_ln_g", "cv_ln_b", "cv_out_g",
              "ret_gn_g", "norm_ffn_g"):
        P[n] = W[n].reshape(DEPTH, 1, -1)

    hooks = dict(mixer_w=mixer_w, ffn_w=ffn_w, ffn_grads=ffn_grads, mixer_grads=mixer_grads,
                 first_after=gathers["ffn", 1][4])
    loss_part, dx, dmeta, G = local_step(x[0], loss_target[0], P, hooks)

    small_full = {
        "meta": dmeta,
        "cv_dw_w": jnp.stack(G["cv_dw_w"]),
        "ffn_dw_w": jnp.stack([g.transpose(1, 0, 2).reshape(FFN_K, 2 * D_FF) for g in G["ffn_dw_w"]]),
        "ffn_dw_b": jnp.stack([g.reshape(2 * D_FF) for g in G["ffn_dw_b"]]),
        "attn_sinks": jnp.stack(G["attn_sinks"]),
    }
    for n in _SMALL:
        if n not in small_full:
            small_full[n] = jnp.stack([g.reshape(-1) for g in G[n]])
    shapes = [small_full[n].shape for n in _SMALL] + [(1,)]
    packed = _pack([small_full[n] for n in _SMALL] + [loss_part.reshape(1)])
    small_exchange = gather_start("reduce_small_start", [packed], landing([packed]), after=[dx])

    out = {}
    tiles = {"w_in": 144, "cv_pw": 32, "w_out": 128, "ffn_up": 176, "ffn_down": 176}
    me1 = me.astype(jnp.int32).reshape(1)
    own, lands = {n: [None] * DEPTH for n in _BIG}, {n: [None] * DEPTH for n in _BIG}

    def arrived(kind, names, after):
        for l in range(DEPTH):
            srcs, got = scatter_wait(f"scatter_{kind}{l}_wait", scatters[kind, l], after=after)
            for n, s_, g_ in zip(names, srcs, got):
                own[n][l], lands[n][l] = s_, g_

    def update(names):
        for n in names:
            shard = W[n].shape
            rows, cols = shard[0] * shard[1], shard[2]
            res = adamw_big(own[n], lands[n], W[n].reshape(rows, cols), M[n].reshape(rows, cols),
                            V[n].reshape(rows, cols), me1, tiles[n], f"adamw_{n}")
            out[n] = [r.reshape(shard) for r in res]

    arrived("ffn", ("ffn_up", "ffn_down"), [small_exchange[4]])
    update(("ffn_up", "ffn_down"))
    slots = gather_wait("reduce_small_wait", small_exchange, after=[out["ffn_down"][0]])[1][0]
    summed = _unpack(sum_slots(slots, "reduce_small_sum"), shapes)
    loss = summed[-1][0]
    small_g = []
    for n, g in zip(_SMALL, summed):
        if n in _SMALL_SHARDED:
            width = _SMALL_SHARDED[n] // NDEV
            g = lax.dynamic_slice_in_dim(g, me * width, width, axis=g.ndim - 1)
        small_g.append(g)
    flat2 = lambda a: a.reshape(-1, a.shape[-1])
    res = adamw_many([flat2(g) for g in small_g], *[[flat2(X[n]) for n in _SMALL] for X in (W, M, V)], "adamw_small")
    for n, r in zip(_SMALL, res):
        out[n] = [a.reshape(W[n].shape) for a in r]
    arrived("mix", ("w_in", "cv_pw", "w_out"), [out["ffn_down"][0], res[0][0]])
    update(("w_in", "cv_pw", "w_out"))
    for n in _TRANSPOSED:
        out[n] = [r.transpose(0, 2, 1) for r in out[n]]

    return (loss, dx[None], *[out[n][0] for n in _ORDER], *[out[n][1] for n in _ORDER],
            *[out[n][2] for n in _ORDER], *[out[n][3] for n in _ORDER])
```

```python
import functools
import math

import numpy as np
import jax
import jax.numpy as jnp
from jax import lax
from jax.experimental import pallas as pl
from jax.experimental.pallas import tpu as pltpu

F32 = jnp.float32
BF16 = jnp.bfloat16

D = 1024
SEQ = 2048
DEPTH = 2
T = 128
L = SEQ + T
NB = L // T
N_META = 16
PAD = T - N_META
HD = 64
NQ = 8
NKV = 2
GQA = NQ // NKV
CV = 256
CONV_K = 31
RH = 4
D_FF = 2816
FFN_K = 3
IN_W = 2304
RMS_EPS = 1e-6
LN_EPS = 1e-5
NEG = -1e30
NDEV = 8
UPW = 2 * D_FF // NDEV
DNR = D_FF // NDEV
NPAIR = NDEV // 2

ADAM_LR, ADAM_B1, ADAM_B2, ADAM_EPS, ADAM_WD, ADAM_STEP = 0.001, 0.9, 0.999, 1e-08, 0.01, 10

VMEM_BIG = 56 * 1024 * 1024

MESH = pl.DeviceIdType.MESH


def _params(vmem=None):
    return pltpu.CompilerParams(vmem_limit_bytes=vmem) if vmem else None


def _dot(a, b, nt=False):
    return lax.dot_general(a, b, (((1,), (1 if nt else 0,)), ((), ())), preferred_element_type=F32)


def _sig(x):
    return 1.0 / (1.0 + jnp.exp(-x))


def _bf(x):
    return x.astype(BF16)


def _stack_rows(rows):
    idx = lax.broadcasted_iota(jnp.int32, (len(rows), rows[0].shape[1]), 0)
    out = jnp.zeros((len(rows), rows[0].shape[1]), F32)
    for r, v in enumerate(rows):
        out = jnp.where(idx == r, v, out)
    return out


def rms_bwd(x, g, dy, dres, name, after=None, last=False):
    n, w = x.shape
    has_res = dres is not None
    deps = [] if after is None else [after]

    def body(x_ref, g_ref, dy_ref, *rest):
        rest = rest[:len(rest) - 3 - len(deps)] + rest[len(rest) - 3:]
        if has_res:
            dres_ref, dx_ref, dxm_ref, dg_ref = rest
        else:
            dx_ref, dxm_ref, dg_ref = rest
        i = pl.program_id(0)
        xv = x_ref[...]
        r = lax.rsqrt(jnp.mean(xv * xv, axis=-1, keepdims=True) + RMS_EPS)
        xh = xv * r
        dyv = dy_ref[...]
        dxh = dyv * g_ref[...]
        dx = r * (dxh - xh * jnp.mean(dxh * xh, axis=-1, keepdims=True))
        if has_res:
            dx = dx + dres_ref[...]
        dx_ref[...] = dx
        if last:
            @pl.when(i == 0)
            def _():
                dxm_ref[...] = dx
        else:
            rows = i * T + lax.broadcasted_iota(jnp.int32, (T, 1), 0)
            dxm_ref[...] = jnp.where(rows >= PAD, dx, 0.0).astype(BF16)
        part = jnp.sum(dyv * xh, axis=0, keepdims=True)

        @pl.when(i == 0)
        def _():
            dg_ref[...] = part

        @pl.when(i > 0)
        def _():
            dg_ref[...] += part

    row = pl.BlockSpec((T, w), lambda i: (i, 0))
    vec = pl.BlockSpec((1, w), lambda i: (0, 0))
    ins = [x, g, dy] + ([dres] if has_res else []) + deps
    if last:
        out_shape = (jax.ShapeDtypeStruct((n - T, w), F32), jax.ShapeDtypeStruct((T, w), F32))
        out_specs = (pl.BlockSpec((T, w), lambda i: (jnp.maximum(i - 1, 0), 0)), pl.BlockSpec((T, w), lambda i: (0, 0)))
    else:
        out_shape = (jax.ShapeDtypeStruct((n, w), F32), jax.ShapeDtypeStruct((n, w), BF16))
        out_specs = (row, row)
    return pl.pallas_call(
        body, name=name, out_shape=(*out_shape, jax.ShapeDtypeStruct((1, w), F32)), grid=(n // T,),
        in_specs=[row, vec, row] + ([row] if has_res else [])
        + [_HBM] * len(deps),
        out_specs=(*out_specs, vec))(*ins)


def loss_head(h, target):
    def body(h_ref, t_ref, dh_ref, dhm_ref, loss_ref):
        n = pl.program_id(0)
        e = jnp.where(n > 0, h_ref[...] - t_ref[...], 0.0)
        dh = e * (1.0 / D)
        dh_ref[...] = dh
        dhm_ref[...] = dh.astype(BF16)
        part = jnp.sum(jnp.sum(e * e, axis=1, keepdims=True), axis=0, keepdims=True) * (0.5 / D)

        @pl.when(n == 0)
        def _():
            loss_ref[...] = jnp.zeros_like(loss_ref)

        @pl.when(n > 0)
        def _():
            loss_ref[...] += jnp.broadcast_to(part, loss_ref.shape)

    row = pl.BlockSpec((T, D), lambda n: (n, 0))
    return pl.pallas_call(
        body, name="loss_head",
        out_shape=(jax.ShapeDtypeStruct((L, D), F32), jax.ShapeDtypeStruct((L, D), BF16),
                   jax.ShapeDtypeStruct((8, 128), F32)),
        grid=(NB,),
        in_specs=[row, pl.BlockSpec((T, D), lambda n: (jnp.maximum(n - 1, 0), 0))],
        out_specs=(row, row, pl.BlockSpec((8, 128), lambda n: (0, 0))))(h, target)


def _mm(name, a, b, *, grid, a_spec, b_spec, o_spec, out_shape, nt=False, ta=False, red=False, res=None,
        res_spec=None):
    def body(a_ref, b_ref, *rest):
        o_ref = rest[-1]
        av = a_ref[...]
        bv = b_ref[...]
        if bv.ndim == 3:
            bv = bv.reshape(bv.shape[0] * bv.shape[1], bv.shape[2])
        if ta:
            acc = lax.dot_general(av, bv, (((0,), (0,)), ((), ())), preferred_element_type=F32)
        else:
            acc = _dot(av, bv, nt)
        if red:
            k = pl.program_id(0)

            @pl.when(k == 0)
            def _():
                o_ref[...] = acc

            @pl.when(k > 0)
            def _():
                o_ref[...] += acc
        else:
            if res is not None:
                rows = lax.broadcasted_iota(jnp.int32, (acc.shape[0], 1), 0)
                acc = rest[0][...] + jnp.where(rows >= PAD, acc, 0.0)
            o_ref[...] = acc.astype(o_ref.dtype)

    ins = [a, b] + ([res] if res is not None else [])
    specs = [a_spec, b_spec] + ([res_spec] if res is not None else [])
    return pl.pallas_call(body, name=name, out_shape=out_shape, grid=grid, in_specs=specs, out_specs=o_spec,
                          compiler_params=_params(VMEM_BIG))(*ins)


def mm_nn(name, a, b, tn, out_dtype=F32, res=None):
    m, k = a.shape
    n = b.shape[1]
    return _mm(name, a, b, grid=(n // tn,),
               a_spec=pl.BlockSpec((m, k), lambda j: (0, 0)), b_spec=pl.BlockSpec((k, tn), lambda j: (0, j)),
               o_spec=pl.BlockSpec((m, tn), lambda j: (0, j)), out_shape=jax.ShapeDtypeStruct((m, n), out_dtype),
               res=res, res_spec=pl.BlockSpec((m, tn), lambda j: (0, j)))


def mm_nt(name, a, b, tn):
    m, k = a.shape
    n = b.shape[0]
    return _mm(name, a, b, grid=(n // tn,), nt=True,
               a_spec=pl.BlockSpec((m, k), lambda j: (0, 0)), b_spec=pl.BlockSpec((tn, k), lambda j: (j, 0)),
               o_spec=pl.BlockSpec((m, tn), lambda j: (0, j)), out_shape=jax.ShapeDtypeStruct((m, n), F32))


def mm_ta(name, a, b, tm, out_dtype=BF16):
    k, m = a.shape
    n = b.shape[1]
    return _mm(name, a, b, grid=(m // tm,), ta=True,
               a_spec=pl.BlockSpec((k, tm), lambda j: (0, j)), b_spec=pl.BlockSpec((k, n), lambda j: (0, 0)),
               o_spec=pl.BlockSpec((tm, n), lambda j: (j, 0)), out_shape=jax.ShapeDtypeStruct((m, n), out_dtype))


def _rms_mm(name, h, g, b, *, grid, b_spec, o_spec, out_shape, after=None):
    deps = [] if after is None else [after]

    def body(h_ref, g_ref, b_ref, *rest):
        o_ref, u_ref = rest[len(deps):]

        @pl.when(pl.program_id(0) == 0)
        def _():
            gv = g_ref[...]
            for c in range(NB):
                rows = slice(c * T, (c + 1) * T)
                xv = h_ref[rows, :]
                r = lax.rsqrt(jnp.mean(xv * xv, axis=-1, keepdims=True) + RMS_EPS)
                u_ref[rows, :] = (xv * r * gv).astype(BF16)

        o_ref[...] = _dot(u_ref[...], b_ref[...], nt=True)

    whole = lambda r: pl.BlockSpec((r, D), lambda j: (0, 0))
    return pl.pallas_call(
        body, name=name, out_shape=(out_shape, jax.ShapeDtypeStruct((L, D), BF16)), grid=grid,
        in_specs=[whole(L), whole(1), b_spec] + [_HBM] * len(deps), out_specs=(o_spec, whole(L)),
        compiler_params=_params(VMEM_BIG))(h, g, b, *deps)


def out_proj_fwd(name, o, og, y_cv, y_ret, w, res, tn):
    wa = o.shape[1]

    def body(o_ref, og_ref, cv_ref, ret_ref, w_ref, r_ref, out_ref, ycat_ref):
        @pl.when(pl.program_id(0) == 0)
        def _():
            gv = og_ref[...]
            for c in range(NB):
                rows = slice(c * T, (c + 1) * T)
                ycat_ref[rows, 0:wa] = _rms_rows(o_ref[rows, :], gv)[0].astype(BF16)
                ycat_ref[rows, wa:wa + CV] = cv_ref[rows, :]
                ycat_ref[rows, wa + CV:D] = ret_ref[rows, :]

        acc = _dot(ycat_ref[...], w_ref[...])
        rows = lax.broadcasted_iota(jnp.int32, (L, 1), 0)
        out_ref[...] = r_ref[...] + jnp.where(rows >= PAD, acc, 0.0)

    whole = lambda w_: pl.BlockSpec((L, w_), lambda j: (0, 0))
    tile = pl.BlockSpec((L, tn), lambda j: (0, j))
    return pl.pallas_call(
        body, name=name, out_shape=(jax.ShapeDtypeStruct((L, D), F32), jax.ShapeDtypeStruct((L, D), BF16)),
        grid=(D // tn,),
        in_specs=[whole(wa), pl.BlockSpec((1, wa), lambda j: (0, 0)), whole(CV), whole(RW),
                  pl.BlockSpec((D, tn), lambda j: (0, j)), tile],
        out_specs=(tile, whole(D)), compiler_params=_params(VMEM_BIG))(o, og, y_cv, y_ret, w, res)


def in_proj_fwd(name, h, g, wt, tn, after=None):
    return _rms_mm(name, h, g, wt, grid=(IN_W // tn,), b_spec=pl.BlockSpec((tn, D), lambda j: (j, 0)),
                   o_spec=pl.BlockSpec((L, tn), lambda j: (0, j)), out_shape=jax.ShapeDtypeStruct((L, IN_W), F32),
                   after=after)


def ffn_up_fwd(name, h, g, wupt):
    return _rms_mm(name, h, g, wupt, grid=(NDEV,), b_spec=pl.BlockSpec((None, UPW, D), lambda j: (j, 0, 0)),
                   o_spec=pl.BlockSpec((None, L, UPW), lambda j: (j, 0, 0)),
                   out_shape=jax.ShapeDtypeStruct((NDEV, L, UPW), F32))


def _slab_specs():
    gate = pl.BlockSpec((None, L, UPW), lambda j: (jnp.minimum(j, NPAIR - 1), 0, 0))
    up = pl.BlockSpec((None, L, UPW), lambda j: (jnp.maximum(j - NPAIR, 0), 0, 0))
    return gate, up


def ffn_up_dx(name, dfg, dfu, wupt):
    def body(g_ref, u_ref, b_ref, o_ref):
        j = pl.program_id(0)

        @pl.when(j == 0)
        def _():
            o_ref[...] = _dot(g_ref[...], b_ref[...])

        @pl.when(jnp.logical_and(j > 0, j < NPAIR))
        def _():
            o_ref[...] += _dot(g_ref[...], b_ref[...])

        @pl.when(j >= NPAIR)
        def _():
            o_ref[...] += _dot(u_ref[...], b_ref[...])

    gate, up = _slab_specs()
    return pl.pallas_call(
        body, name=name, out_shape=jax.ShapeDtypeStruct((L, D), F32), grid=(NDEV,),
        in_specs=[gate, up, pl.BlockSpec((None, UPW, D), lambda j: (j, 0, 0))],
        out_specs=pl.BlockSpec((L, D), lambda j: (0, 0)), compiler_params=_params(VMEM_BIG))(dfg, dfu, wupt)


def ffn_up_dw(name, u2, dfg, dfu):
    tdot = lambda a, b: lax.dot_general(a, b, (((0,), (0,)), ((), ())), preferred_element_type=F32)

    def body(a_ref, g_ref, u_ref, o_ref):
        j = pl.program_id(0)

        @pl.when(j < NPAIR)
        def _():
            o_ref[...] = tdot(g_ref[...], a_ref[...]).astype(o_ref.dtype)

        @pl.when(j >= NPAIR)
        def _():
            o_ref[...] = tdot(u_ref[...], a_ref[...]).astype(o_ref.dtype)

    gate, up = _slab_specs()
    return pl.pallas_call(
        body, name=name, out_shape=jax.ShapeDtypeStruct((NDEV, UPW, D), BF16), grid=(NDEV,),
        in_specs=[pl.BlockSpec((L, D), lambda j: (0, 0)), gate, up],
        out_specs=pl.BlockSpec((None, UPW, D), lambda j: (j, 0, 0)), compiler_params=_params(VMEM_BIG))(u2, dfg, dfu)


def ffn_down_fwd(name, a, wdn, res, tn=256):
    def body(a_ref, b_ref, r_ref, o_ref):
        acc = jnp.zeros((L, tn), F32)
        for g in range(NPAIR):
            bv = b_ref[2 * g:2 * g + 2]
            acc = acc + _dot(a_ref[g], bv.reshape(2 * DNR, tn))
        rows = lax.broadcasted_iota(jnp.int32, (L, 1), 0)
        o_ref[...] = r_ref[...] + jnp.where(rows >= PAD, acc, 0.0)

    return pl.pallas_call(
        body, name=name, out_shape=jax.ShapeDtypeStruct((L, D), F32), grid=(D // tn,),
        in_specs=[pl.BlockSpec((NPAIR, L, UPW), lambda j: (0, 0, 0)),
                  pl.BlockSpec((NDEV, DNR, tn), lambda j: (0, 0, j)),
                  pl.BlockSpec((L, tn), lambda j: (0, j))],
        out_specs=pl.BlockSpec((L, tn), lambda j: (0, j)),
        compiler_params=_params(VMEM_BIG))(a, wdn, res)


def ffn_down_dx(name, dh, wdn):
    return _mm(name, dh, wdn, grid=(NPAIR,), nt=True,
               a_spec=pl.BlockSpec((L, D), lambda g: (0, 0)),
               b_spec=pl.BlockSpec((2, DNR, D), lambda g: (g, 0, 0)),
               o_spec=pl.BlockSpec((None, L, UPW), lambda g: (g, 0, 0)),
               out_shape=jax.ShapeDtypeStruct((NPAIR, L, UPW), F32))


def ffn_down_dw(name, a, dh):
    return _mm(name, a, dh, grid=(NPAIR,), ta=True,
               a_spec=pl.BlockSpec((None, L, UPW), lambda g: (g, 0, 0)),
               b_spec=pl.BlockSpec((L, D), lambda g: (0, 0)),
               o_spec=pl.BlockSpec((UPW, D), lambda g: (g, 0)),
               out_shape=jax.ShapeDtypeStruct((D_FF, D), BF16))


_SLOPES = [2.0 ** (-8.0 * (h + 1) / NQ) for h in range(NQ)]
_SCALE = HD ** -0.5


QR = GQA * T
KC = 3 * T


def _attn_geometry(g, n, sink_ref):
    row = lax.broadcasted_iota(jnp.int32, (QR, KC), 0)
    col = lax.broadcasted_iota(jnp.int32, (QR, KC), 1)
    i = row & (T - 1)
    j = col & (T - 1)
    blk = col // T
    d_meta = n * T + i - j
    ok_meta = (j >= PAD) & (d_meta >= 0)
    ok_prev = j > i + jnp.where(n >= 2, 0, T)
    ok_cur = j <= i - jnp.where(n >= 1, 0, T)
    ok = ((blk == 0) & ok_meta) | ((blk == 1) & ok_prev) | ((blk == 2) & ok_cur)
    dist = jnp.where(blk == 0, jnp.minimum(d_meta, T), jnp.where(blk == 1, T + i - j, i - j)).astype(F32)
    head = lax.broadcasted_iota(jnp.int32, (QR, 1), 0) // T
    slope = jnp.zeros((QR, 1), F32)
    sink = jnp.zeros((QR, 1), F32)
    for hh in range(GQA):
        slope = jnp.where(head == hh, jnp.where(g == 0, _SLOPES[hh], _SLOPES[GQA + hh]), slope)
        sink = jnp.where(head == hh, sink_ref[g * GQA + hh], sink)
    return ok, slope * dist, sink


def _attn_probs(qn, keys, geo):
    ok, penalty, sink = geo
    s = jnp.where(ok, _dot(qn, keys, nt=True) * _SCALE - penalty, NEG)
    m = jnp.maximum(jnp.max(s, axis=-1, keepdims=True), sink)
    e = jnp.exp(s - m)
    e_sink = jnp.exp(sink - m)
    inv = 1.0 / (e.sum(axis=-1, keepdims=True) + e_sink)
    return e, inv, e_sink * inv


def _rms_rows(x, g):
    r = lax.rsqrt(jnp.mean(x * x, axis=-1, keepdims=True) + RMS_EPS)
    xh = x * r
    return xh * g, xh, r


def _rms_rows_bwd(dy, g, xh, r):
    dxh = dy * g
    return r * (dxh - xh * jnp.mean(dxh * xh, axis=-1, keepdims=True))


def _rows3(m_ref, p_ref, c_ref):
    return jnp.concatenate([m_ref[...], p_ref[...], c_ref[...]], axis=0)


_ATT_K, _ATT_V = 4, 5


def _nat_specs():
    qspec = pl.BlockSpec((T, GQA * HD), lambda n, g: (n, g))
    kv = lambda col: (pl.BlockSpec((T, 2 * HD), lambda n, g: (n, col)),
                      pl.BlockSpec((T, 2 * HD), lambda n, g: (jnp.maximum(n - 1, 0), col)),
                      pl.BlockSpec((T, 2 * HD), lambda n, g: (0, col)))
    return qspec, kv(_ATT_K), kv(_ATT_V)


def _group_lanes(g, x):
    return jnp.where(g == 0, x[:, :HD], x[:, HD:])


def _stack_heads(x):
    return jnp.concatenate([x[:, hh * HD:(hh + 1) * HD] for hh in range(GQA)], axis=0)


def _unstack_heads(x):
    return jnp.concatenate([x[hh * T:(hh + 1) * T] for hh in range(GQA)], axis=1)


def attn_fwd(proj, qg, kg, sinks, name):
    def body(sink_ref, q_ref, kc_ref, kp_ref, km_ref, vc_ref, vp_ref, vm_ref, qg_ref, kg_ref, o_ref):
        n = pl.program_id(0)
        g = pl.program_id(1)
        geo = _attn_geometry(g, n, sink_ref)
        keys = _bf(_rms_rows(_group_lanes(g, _rows3(km_ref, kp_ref, kc_ref)), kg_ref[...])[0])
        vals = _bf(_group_lanes(g, _rows3(vm_ref, vp_ref, vc_ref)))
        qn = _bf(_rms_rows(_stack_heads(q_ref[...]), qg_ref[...])[0])
        e, inv, _ = _attn_probs(qn, keys, geo)
        o_ref[...] = _unstack_heads(_dot(_bf(e), vals) * inv)

    qspec, (kc, kp, km), (vc, vp, vm) = _nat_specs()
    vec = pl.BlockSpec((1, HD), lambda n, g: (0, 0))
    return pl.pallas_call(
        body, name=name, out_shape=jax.ShapeDtypeStruct((L, NQ * HD), F32), grid=(NB, NKV),
        in_specs=[pl.BlockSpec(memory_space=pltpu.SMEM), qspec, kc, kp, km, vc, vp, vm, vec, vec],
        out_specs=qspec)(sinks, proj, proj, proj, proj, proj, proj, proj, qg, kg)


def attn_bwd(proj, o, dycat, og, qg, kg, sinks, name):
    def body(sink_ref, q_ref, kc_ref, kp_ref, km_ref, vc_ref, vp_ref, vm_ref, o_ref, dy_ref, og_ref, qg_ref, kg_ref,
             dq_ref, dkc_ref, dkp_ref, dkm_ref, dvc_ref, dvp_ref, dvm_ref, dqg_ref, dkg_ref, dsk_ref, dog_ref):
        n = pl.program_id(0)
        g = pl.program_id(1)

        @pl.when(jnp.logical_and(g == 0, n == 0))
        def _():
            for r in (dkm_ref, dvm_ref, dqg_ref, dkg_ref, dsk_ref, dog_ref):
                r[...] = jnp.zeros_like(r)

        ogv = og_ref[...]
        dyv = dy_ref[...]
        _, oh, orr = _rms_rows(o_ref[...], ogv)
        do_all = _rms_rows_bwd(dyv, ogv, oh, orr)
        do_grp = jnp.where(g == 0, do_all[:, :GQA * HD], do_all[:, GQA * HD:])

        @pl.when(g == 0)
        def _():
            dog_ref[...] += jnp.sum(dyv * oh, axis=0, keepdims=True)

        geo = _attn_geometry(g, n, sink_ref)
        kgv = kg_ref[...]
        qgv = qg_ref[...]
        kn_f, kh, kr = _rms_rows(_group_lanes(g, _rows3(km_ref, kp_ref, kc_ref)), kgv)
        keys = _bf(kn_f)
        vals = _bf(_group_lanes(g, _rows3(vm_ref, vp_ref, vc_ref)))
        qn_f, qh, qr = _rms_rows(_stack_heads(q_ref[...]), qgv)
        qn = _bf(qn_f)
        e, inv, p_sink = _attn_probs(qn, keys, geo)
        dob = _bf(_stack_heads(do_grp))
        p = e * inv
        dp = _dot(dob, vals, nt=True)
        delta = (p * dp).sum(axis=-1, keepdims=True)
        head_row = lax.broadcasted_iota(jnp.int32, (NQ, 128), 0)
        dsk = jnp.zeros((NQ, 128), F32)
        for hh in range(GQA):
            part = -jnp.sum((p_sink * delta)[hh * T:(hh + 1) * T], axis=0, keepdims=True)
            dsk = jnp.where(head_row == g * GQA + hh, part, dsk)
        dsk_ref[...] += dsk
        ds = p * (dp - delta)
        dqn = _dot(_bf(ds), keys) * _SCALE
        dq_ref[...] = _unstack_heads(_rms_rows_bwd(dqn, qgv, qh, qr)).astype(dq_ref.dtype)
        dqg_ref[...] += jnp.sum(dqn * qh, axis=0, keepdims=True)
        dkn = _dot(_bf(ds.T), qn) * _SCALE
        dkg_ref[...] += jnp.sum(dkn * kh, axis=0, keepdims=True)
        dk_all = _rms_rows_bwd(dkn, kgv, kh, kr)
        dv_all = _dot(_bf(p.T), dob)

        for gg in range(NKV):
            @pl.when(g == gg)
            def _(gg=gg):
                lanes = slice(gg * HD, (gg + 1) * HD)
                dkm_ref[:, lanes] += dk_all[0:T]
                dkp_ref[:, lanes] = dk_all[T:2 * T]
                dkc_ref[:, lanes] = dk_all[2 * T:3 * T]
                dvm_ref[:, lanes] += dv_all[0:T]
                dvp_ref[:, lanes] = dv_all[T:2 * T]
                dvc_ref[:, lanes] = dv_all[2 * T:3 * T]

    qspec, (kc, kp, km), (vc, vp, vm) = _nat_specs()
    vec = pl.BlockSpec((1, HD), lambda n, g: (0, 0))
    cur = pl.BlockSpec((T, 2 * HD), lambda n, g: (n, 0))
    meta = pl.BlockSpec((T, 2 * HD), lambda n, g: (0, 0))
    kv_shape = jax.ShapeDtypeStruct((L, 2 * HD), F32)
    meta_shape = jax.ShapeDtypeStruct((T, 2 * HD), F32)
    vec_shape = jax.ShapeDtypeStruct((1, HD), F32)
    allq = pl.BlockSpec((T, NQ * HD), lambda n, g: (n, 0))
    ogvec = pl.BlockSpec((1, NQ * HD), lambda n, g: (0, 0))
    return pl.pallas_call(
        body, name=name,
        out_shape=(jax.ShapeDtypeStruct((L, NQ * HD), BF16), kv_shape, kv_shape, meta_shape, kv_shape, kv_shape,
                   meta_shape, vec_shape, vec_shape, jax.ShapeDtypeStruct((NQ, 128), F32),
                   jax.ShapeDtypeStruct((1, NQ * HD), F32)),
        grid=(NB, NKV),
        in_specs=[pl.BlockSpec(memory_space=pltpu.SMEM), qspec, kc, kp, km, vc, vp, vm, allq, allq, ogvec, vec, vec],
        out_specs=(qspec, cur, cur, meta, cur, cur, meta, vec, vec,
                   pl.BlockSpec((NQ, 128), lambda n, g: (0, 0)), ogvec))(
            sinks, proj, proj, proj, proj, proj, proj, proj, o, dycat, og, qg, kg)


RW = RH * HD


def _ret_tables():
    h = np.arange(RH, dtype=np.float64)
    lg = np.log1p(-np.exp2(-5.0 - h))
    idx = np.arange(T, dtype=np.float64)
    diff = idx[:, None] - idx[None, :]
    decay = np.where(diff[None] >= 0, np.exp(np.maximum(diff, 0.0)[None] * lg[:, None, None]), 0.0)
    zeta = np.exp((T - 1 - idx)[None, :] * lg[:, None])
    xi = np.exp((idx + 1.0)[None, :] * lg[:, None])
    cd = np.exp(T * lg)
    lanes = lambda a: np.repeat(a.T, HD, axis=1)
    head_of = np.arange(RW) // HD
    same = (head_of[:, None] == head_of[None, :]).astype(np.float64)
    f = lambda a: jnp.asarray(a, F32)
    return dict(decay=f(decay), zeta=f(lanes(zeta)), xi=f(lanes(xi)), cd=f(np.repeat(cd, HD)[None, :]),
                head=f((head_of[None, :] == np.arange(RH)[:, None]).astype(np.float64)[:, None, :]),
                same=f(same), avg=jnp.asarray(same / HD, BF16))


def _seg_mean(x, avg):
    hi = _bf(x)
    lo = _bf(x - hi.astype(F32))
    return _dot(hi, avg) + _dot(lo, avg)


def _ret_specs(col0, order):
    return lambda col: pl.BlockSpec((T, RW), lambda i: (order(i), col0 + col))


def _ret_chunk(q, kf, v, s, tb):
    dec, xi, head = tb
    vb = _bf(v)
    kb = _bf(kf)
    y = _dot(_bf(q * xi), _bf(s))
    a = []
    for h in range(RH):
        a.append(_dot(_bf(q * head[h]), kb, nt=True) * dec[h])
        y = y + head[h] * _dot(_bf(a[h]), vb)
    return a, y


def _gn_rows(y):
    mu = jnp.mean(y, axis=-1, keepdims=True)
    yc = y - mu
    rstd = lax.rsqrt(jnp.mean(yc * yc, axis=-1, keepdims=True) + LN_EPS)
    return yc * rstd, rstd


_RET_COL0 = _RET_Q = 5


def _ret_consts(tb):
    names = ("decay", "zeta", "xi", "cd", "head", "same", "avg")
    full = lambda a: pl.BlockSpec(a.shape, lambda i: (0,) * a.ndim)
    return [tb[n] for n in names], [full(tb[n]) for n in names]


def _ret_inputs(n, q_ref, k_ref, v_ref):
    rows = n * T + lax.broadcasted_iota(jnp.int32, (T, 1), 0)
    valid = rows >= PAD
    return q_ref[...], jnp.where(valid, k_ref[...] * (HD ** -0.5), 0.0), v_ref[...], valid


def ret_fwd(proj, gng, tb, name):
    def body(q_ref, k_ref, v_ref, g_ref, dec_ref, zeta_ref, xi_ref, cd_ref, head_ref, same_ref, avg_ref, gng_ref,
             y_ref, st_ref, s_scr):
        n = pl.program_id(0)

        @pl.when(n == 0)
        def _():
            s_scr[...] = jnp.zeros_like(s_scr)

        s = s_scr[...]
        st_ref[...] = s
        q, kf, v, _ = _ret_inputs(n, q_ref, k_ref, v_ref)
        _, y = _ret_chunk(q, kf, v, s, (dec_ref, xi_ref[...], head_ref))
        s_scr[...] = cd_ref[...] * s + same_ref[...] * _dot(_bf((kf * zeta_ref[...]).T), _bf(v))
        avg = avg_ref[...]
        yc = y - _seg_mean(y, avg)
        yh = yc * lax.rsqrt(_seg_mean(yc * yc, avg) + LN_EPS)
        gv = g_ref[...]
        y_ref[...] = (gv * _sig(gv) * (yh * gng_ref[...])).astype(y_ref.dtype)

    col = _ret_specs(_RET_COL0, lambda i: i)
    consts, cspecs = _ret_consts(tb)
    return pl.pallas_call(
        body, name=name,
        out_shape=(jax.ShapeDtypeStruct((L, RW), BF16), jax.ShapeDtypeStruct((NB, RW, RW), F32)),
        grid=(NB,),
        in_specs=[col(0), col(1), col(2), col(3)] + cspecs + [pl.BlockSpec((1, RW), lambda i: (0, 0))],
        out_specs=(pl.BlockSpec((T, RW), lambda i: (i, 0)), pl.BlockSpec((None, RW, RW), lambda i: (i, 0, 0))),
        scratch_shapes=[pltpu.VMEM((RW, RW), F32)])(proj, proj, proj, proj, *consts, gng)


def ret_bwd(proj, dycat, states, gng, tb, name):
    def body(q_ref, k_ref, v_ref, g_ref, do_ref, st_ref, dec_ref, zeta_ref, xi_ref, cd_ref, head_ref, same_ref,
             avg_ref, gng_ref, d_ref, dgn_ref, ds_scr):
        i = pl.program_id(0)
        n = NB - 1 - i

        @pl.when(i == 0)
        def _():
            ds_scr[...] = jnp.zeros_like(ds_scr)

        dsn = _bf(ds_scr[...])
        s = st_ref[...]
        sb = _bf(s)
        q, kf, v, valid = _ret_inputs(n, q_ref, k_ref, v_ref)
        xi, zeta, avg = xi_ref[...], zeta_ref[...], avg_ref[...]
        a, y = _ret_chunk(q, kf, v, s, (dec_ref, xi, head_ref))
        yc = y - _seg_mean(y, avg)
        rstd = lax.rsqrt(_seg_mean(yc * yc, avg) + LN_EPS)
        yh = yc * rstd
        gv = g_ref[...]
        sg = _sig(gv)
        sil = gv * sg
        gn = gng_ref[...]
        dout = do_ref[...]
        d_ref[:, 3 * RW:4 * RW] = (dout * (yh * gn) * (sg * (1.0 + gv * (1.0 - sg)))).astype(d_ref.dtype)
        dyh = dout * sil * gn
        part = jnp.sum(dout * sil * yh, axis=0, keepdims=True)

        @pl.when(i == 0)
        def _():
            dgn_ref[...] = part

        @pl.when(i > 0)
        def _():
            dgn_ref[...] += part

        dy = rstd * (dyh - _seg_mean(dyh, avg) - yh * _seg_mean(dyh * yh, avg))
        dyb = _bf(dy)
        vb, kb, qb = _bf(v), _bf(kf), _bf(q)
        dq = _dot(dyb, sb, nt=True) * xi
        dkf = _dot(vb, dsn, nt=True) * zeta
        dv = _dot(_bf(kf * zeta), dsn)
        for h in range(RH):
            m = head_ref[h]
            da = _dot(_bf(dy * m), vb, nt=True) * dec_ref[h]
            dv = dv + m * _dot(_bf(a[h].T), dyb)
            dq = dq + m * _dot(_bf(da), kb)
            dkf = dkf + m * _dot(_bf(da.T), qb)
        d_ref[:, 0:RW] = dq.astype(d_ref.dtype)
        d_ref[:, RW:2 * RW] = jnp.where(valid, dkf * (HD ** -0.5), 0.0).astype(d_ref.dtype)
        d_ref[:, 2 * RW:3 * RW] = dv.astype(d_ref.dtype)
        ds_scr[...] = cd_ref[...] * ds_scr[...] + same_ref[...] * _dot(_bf((q * xi).T), dyb)

    back = lambda i: NB - 1 - i
    col = _ret_specs(_RET_COL0, back)
    consts, cspecs = _ret_consts(tb)
    vec = pl.BlockSpec((1, RW), lambda i: (0, 0))
    return pl.pallas_call(
        body, name=name,
        out_shape=(jax.ShapeDtypeStruct((L, 4 * RW), BF16), jax.ShapeDtypeStruct((1, RW), F32)),
        grid=(NB,),
        in_specs=[col(0), col(1), col(2), col(3), _ret_specs(3, back)(0),
                  pl.BlockSpec((None, RW, RW), lambda i: (back(i), 0, 0))] + cspecs + [vec],
        out_specs=(pl.BlockSpec((T, 4 * RW), lambda i: (back(i), 0)), vec),
        scratch_shapes=[pltpu.VMEM((RW, RW), F32)])(proj, proj, proj, proj, dycat, states, *consts, gng)


HALO = 32
TAP0 = HALO - (CONV_K - 1)


def _conv_specs():
    cur = lambda col=0: pl.BlockSpec((T, CV), lambda c: (c, col))
    before = lambda col=0: pl.BlockSpec((HALO, CV), lambda c: (jnp.maximum(c * (T // HALO) - 1, 0), col))
    after = pl.BlockSpec((HALO, CV), lambda c: (jnp.minimum((c + 1) * (T // HALO), L // HALO - 1), 0))
    full = lambda r, w: pl.BlockSpec((r, w), lambda c: (0, 0))
    return cur, before, after, full


_CONV_A, _CONV_B = 3, 4
_DY_CONV = 2


def _conv_post(xc, lg_ref, lb_ref):
    xh, rstd = _gn_rows(xc)
    z = xh * lg_ref[...] + lb_ref[...]
    return xh, rstd, z, _sig(z)


def conv_fwd(proj, w, b, lg, lb, pw, og, name):
    def body(ca_ref, cah_ref, cb_ref, cbh_ref, w_ref, b_ref, lg_ref, lb_ref, pw_ref, og_ref, y_ref, xc_ref, u_scr):
        c = pl.program_id(0)
        u_scr[0:HALO, :] = jnp.where(c > 0, cah_ref[...] * _sig(cbh_ref[...]), 0.0)
        u_scr[HALO:HALO + T, :] = ca_ref[...] * _sig(cb_ref[...])
        acc = jnp.zeros((T, CV), F32)
        for k in range(CONV_K):
            acc = acc + w_ref[k:k + 1, :] * u_scr[TAP0 + k:TAP0 + k + T, :]
        xc = acc + b_ref[...]
        xc_ref[...] = xc
        _, _, z, sg = _conv_post(xc, lg_ref, lb_ref)
        zp = _dot(_bf(z * sg), pw_ref[...])
        y_ref[...] = _rms_rows(zp, og_ref[...])[0].astype(y_ref.dtype)

    cur, before, _, full = _conv_specs()
    vec = full(1, CV)
    seq = jax.ShapeDtypeStruct((L, CV), F32)
    return pl.pallas_call(
        body, name=name, out_shape=(jax.ShapeDtypeStruct((L, CV), BF16), seq), grid=(NB,),
        in_specs=[cur(_CONV_A), before(_CONV_A), cur(_CONV_B), before(_CONV_B), full(CONV_K, CV), vec, vec, vec,
                  full(CV, CV), vec],
        out_specs=(cur(), cur()),
        scratch_shapes=[pltpu.VMEM((HALO + T, CV), F32)])(proj, proj, proj, proj, w, b, lg, lb, pw, og)


def conv_bwd(proj, xc, dycat, w, lg, lb, pw, og, name):
    def body1(xc_ref, dy_ref, lg_ref, lb_ref, pw_ref, og_ref, dxc_ref, db_ref, dlg_ref, dlb_ref, dog_ref, dpw_ref):
        c = pl.program_id(0)

        @pl.when(c == 0)
        def _():
            for r in (db_ref, dlg_ref, dlb_ref, dog_ref, dpw_ref):
                r[...] = jnp.zeros_like(r)

        xh, rstd, z, sg = _conv_post(xc_ref[...], lg_ref, lb_ref)
        s = z * sg
        zp = _dot(_bf(s), pw_ref[...])
        ogv = og_ref[...]
        _, zph, r2 = _rms_rows(zp, ogv)
        dyv = dy_ref[...]
        dog_ref[...] += jnp.sum(dyv * zph, axis=0, keepdims=True)
        dzpb = _bf(_rms_rows_bwd(dyv, ogv, zph, r2))
        dpw_ref[...] += _dot(_bf(s.T), dzpb)
        dz = _dot(dzpb, pw_ref[...], nt=True) * (sg * (1.0 + z * (1.0 - sg)))
        dlg_ref[...] += jnp.sum(dz * xh, axis=0, keepdims=True)
        dlb_ref[...] += jnp.sum(dz, axis=0, keepdims=True)
        dxh = dz * lg_ref[...]
        dxc = rstd * (dxh - jnp.mean(dxh, axis=-1, keepdims=True) - xh * jnp.mean(dxh * xh, axis=-1, keepdims=True))
        db_ref[...] += jnp.sum(dxc, axis=0, keepdims=True)
        dxc_ref[...] = dxc

    def body2(dx_ref, dxa_ref, ca_ref, cb_ref, w_ref, dca_ref, dcb_ref, dw_ref, d_scr):
        c = pl.program_id(0)

        @pl.when(c == 0)
        def _():
            dw_ref[...] = jnp.zeros_like(dw_ref)

        d_scr[0:T, :] = dx_ref[...]
        d_scr[T:T + HALO, :] = jnp.where(c < NB - 1, dxa_ref[...], 0.0)
        ca = ca_ref[...]
        sg = _sig(cb_ref[...])
        u = ca * sg
        du = jnp.zeros((T, CV), F32)
        for k in range(CONV_K):
            off = CONV_K - 1 - k
            dj = d_scr[off:off + T, :]
            du = du + w_ref[k:k + 1, :] * dj
            dw_ref[k:k + 1, :] += jnp.sum(u * dj, axis=0, keepdims=True)
        dca_ref[...] = (du * sg).astype(dca_ref.dtype)
        dcb_ref[...] = (du * ca * sg * (1.0 - sg)).astype(dcb_ref.dtype)

    cur, _, after, full = _conv_specs()
    seq = jax.ShapeDtypeStruct((L, CV), F32)
    vsh = jax.ShapeDtypeStruct((1, CV), F32)
    vec = full(1, CV)
    dxc, db, dlg, dlb, dog, dpw = pl.pallas_call(
        body1, name=name + "_a",
        out_shape=(seq, vsh, vsh, vsh, vsh, jax.ShapeDtypeStruct((CV, CV), F32)),
        grid=(NB,),
        in_specs=[cur(), cur(_DY_CONV), vec, vec, full(CV, CV), vec],
        out_specs=(cur(), vec, vec, vec, vec, full(CV, CV)))(xc, dycat, lg, lb, pw, og)
    dca, dcb, dw = pl.pallas_call(
        body2, name=name + "_b", grid=(NB,),
        out_shape=(jax.ShapeDtypeStruct((L, CV), BF16), jax.ShapeDtypeStruct((L, CV), BF16),
                   jax.ShapeDtypeStruct((CONV_K, CV), F32)),
        in_specs=[cur(), after, cur(_CONV_A), cur(_CONV_B), full(CONV_K, CV)],
        out_specs=(cur(), cur(), full(CONV_K, CV)),
        scratch_shapes=[pltpu.VMEM((T + HALO, CV), F32)])(dxc, dxc, proj, proj, w)
    return dca, dcb, dw, db, dlg, dlb, dog, dpw


HR = 8


def _ffn_specs():
    cur = lambda off: pl.BlockSpec((None, T, UPW), lambda j, c: (j + off, c, 0))
    before = lambda off: pl.BlockSpec((None, HR, UPW), lambda j, c: (j + off, jnp.maximum(c * (T // HR) - 1, 0), 0))
    after = lambda off: pl.BlockSpec(
        (None, HR, UPW), lambda j, c: (j + off, jnp.minimum((c + 1) * (T // HR), L // HR - 1), 0))
    wspec = lambda off, r: pl.BlockSpec((None, r, UPW), lambda j, c: (j + off, 0, 0))
    return cur, before, after, wspec


HR16 = 16


def _ffn_after16():
    return pl.BlockSpec((None, HR16, UPW), lambda j, c: (j, jnp.minimum((c + 1) * (T // HR16), L // HR16 - 1), 0))


def ffn_act_fwd(f, w, b, name):
    strip = 16

    def body(fg_ref, fgh_ref, fu_ref, fuh_ref, wg_ref, wu_ref, bg_ref, bu_ref, a_ref, gc_ref, uc_ref, hg_scr, hu_scr):
        c = pl.program_id(1)
        for scr, x_ref, xh_ref in ((hg_scr, fg_ref, fgh_ref), (hu_scr, fu_ref, fuh_ref)):
            scr[0:HR, :] = jnp.where(c > 0, xh_ref[...], 0.0)
            scr[HR:HR + strip, :] = x_ref[0:strip, :]
        wg = [wg_ref[k:k + 1, :] for k in range(FFN_K)]
        wu = [wu_ref[k:k + 1, :] for k in range(FFN_K)]
        bg, bu = bg_ref[...], bu_ref[...]

        def conv(src, base, w, b):
            acc = b
            for k in range(FFN_K):
                o = base - (FFN_K - 1) + k
                acc = acc + w[k] * src[o:o + strip, :]
            return acc

        for s in range(T // strip):
            r = s * strip
            gate = conv(hg_scr, HR, wg, bg) if s == 0 else conv(fg_ref, r, wg, bg)
            up = conv(hu_scr, HR, wu, bu) if s == 0 else conv(fu_ref, r, wu, bu)
            gc_ref[r:r + strip, :] = gate.astype(BF16)
            uc_ref[r:r + strip, :] = up.astype(BF16)
            a_ref[r:r + strip, :] = (gate * _sig(gate) * up).astype(BF16)

    cur, before, _, wspec = _ffn_specs()
    pre = jax.ShapeDtypeStruct((NPAIR, L, UPW), BF16)
    return pl.pallas_call(
        body, name=name, out_shape=(jax.ShapeDtypeStruct((NPAIR, L, UPW), BF16), pre, pre), grid=(NPAIR, NB),
        in_specs=[cur(0), before(0), cur(NPAIR), before(NPAIR), wspec(0, FFN_K), wspec(NPAIR, FFN_K),
                  wspec(0, 1), wspec(NPAIR, 1)],
        out_specs=(cur(0), cur(0), cur(0)),
        scratch_shapes=[pltpu.VMEM((HR + strip, UPW), F32), pltpu.VMEM((HR + strip, UPW), F32)])(
            f, f, f, f, w, w, b, b)


def ffn_act_bwd(f, gc, uc, da, w, name):
    ext = T + HR
    sub = 8

    def body(fg_ref, fu_ref, gc_ref, gca_ref, uc_ref, uca_ref, da_ref, daa_ref, wg_ref, wu_ref,
             dfg_ref, dfu_ref, dwg_ref, dwu_ref, dbg_ref, dbu_ref, dg_scr, du_scr):
        c = pl.program_id(1)
        wg = [wg_ref[k:k + 1, :] for k in range(FFN_K)]
        wu = [wu_ref[k:k + 1, :] for k in range(FFN_K)]

        for r in range(0, ext, HR16):
            rows = min(HR16, ext - r)
            if r < T:
                gate, up, dav = gc_ref[r:r + HR16, :].astype(F32), uc_ref[r:r + HR16, :].astype(F32), da_ref[r:r + HR16, :]
            else:
                gate, up = gca_ref[...].astype(F32)[0:rows], uca_ref[...].astype(F32)[0:rows]
                dav = jnp.where(c < NB - 1, daa_ref[...], 0.0)
            sg = _sig(gate)
            dg_scr[r:r + rows, :] = dav * up * (sg * (1.0 + gate * (1.0 - sg)))
            du_scr[r:r + rows, :] = dav * gate * sg

        def back(d_scr, x_ref, w, df_ref, dw_ref, db_ref):
            accs = [jnp.zeros((sub, UPW), F32) for _ in range(FFN_K + 1)]
            for s in range(T // (2 * sub)):
                pieces = []
                for r in (2 * s * sub, (2 * s + 1) * sub):
                    xv = x_ref[r:r + sub, :]
                    df = jnp.zeros((sub, UPW), F32)
                    for k in range(FFN_K):
                        off = FFN_K - 1 - k
                        dj = d_scr[r + off:r + off + sub, :]
                        df = df + w[k] * dj
                        accs[k] = accs[k] + xv * dj
                        if off == 0:
                            accs[FFN_K] = accs[FFN_K] + dj
                    pieces.append(df)
                df_ref[2 * s * sub:2 * (s + 1) * sub, :] = jnp.concatenate(pieces, axis=0).astype(BF16)
            dwp = _stack_rows([jnp.sum(a, axis=0, keepdims=True) for a in accs[:FFN_K]])
            dbp = jnp.sum(accs[FFN_K], axis=0, keepdims=True)

            @pl.when(c == 0)
            def _():
                dw_ref[...] = dwp
                db_ref[...] = dbp

            @pl.when(c > 0)
            def _():
                dw_ref[...] += dwp
                db_ref[...] += dbp

        back(dg_scr, fg_ref, wg, dfg_ref, dwg_ref, dbg_ref)
        back(du_scr, fu_ref, wu, dfu_ref, dwu_ref, dbu_ref)

    cur, _, after, wspec = _ffn_specs()
    slab = jax.ShapeDtypeStruct((NPAIR, L, UPW), BF16)
    wsh = jax.ShapeDtypeStruct((NPAIR, FFN_K, UPW), F32)
    bsh = jax.ShapeDtypeStruct((NPAIR, 1, UPW), F32)
    return pl.pallas_call(
        body, name=name, out_shape=(slab, slab, wsh, wsh, bsh, bsh), grid=(NPAIR, NB),
        in_specs=[cur(0), cur(NPAIR), cur(0), _ffn_after16(), cur(0), _ffn_after16(), cur(0), after(0),
                  wspec(0, FFN_K), wspec(NPAIR, FFN_K)],
        out_specs=(cur(0), cur(0), wspec(0, FFN_K), wspec(0, FFN_K), wspec(0, 1), wspec(0, 1)),
        scratch_shapes=[pltpu.VMEM((ext, UPW), F32), pltpu.VMEM((ext, UPW), F32)])(
            f, f, gc, gc, uc, uc, da, da, w, w)


def _mesh_pos():
    return lax.axis_index("x"), lax.axis_index("y"), lax.axis_index("c")


def _peer(pos, k):
    x, y, c = pos
    px = 1 - x if k & 4 else x
    py = 1 - y if k & 2 else y
    pc = 1 - c if k & 1 else c
    return (px, py, pc), 4 * px + 2 * py + pc


_CHIP_FLIPS = (4, 2, 6)
_HBM = pl.BlockSpec(memory_space=pl.ANY)


def all_gather(shards, name):
    nt = len(shards)

    def body(*refs):
        ins, outs = refs[:nt], refs[nt:2 * nt]
        send, recv, local = refs[2 * nt:]
        pos = _mesh_pos()
        me = 4 * pos[0] + 2 * pos[1] + pos[2]
        sib, sib_id = _peer(pos, 1)

        def copy(t, k, block_id, to, src=None):
            dst = outs[t].at[block_id]
            return pltpu.make_async_remote_copy(
                src_ref=dst if src is None else src, dst_ref=dst, send_sem=send.at[t, k], recv_sem=recv.at[t, k],
                device_id=to, device_id_type=MESH)

        locals_ = [pltpu.make_async_copy(ins[t], outs[t].at[me], local.at[t]) for t in range(nt)]
        for cp in locals_:
            cp.start()
        started = []
        for j, flip in enumerate(_CHIP_FLIPS):
            for t in range(nt):
                started.append(copy(t, 1 + j, me, _peer(pos, flip)[0], src=ins[t]))
        for t in range(nt):
            started.append(copy(t, 0, me, sib, src=ins[t]))
        for cp in started:
            cp.start()
        for j, flip in enumerate(_CHIP_FLIPS):
            _, pid = _peer(pos, flip)
            for t in range(nt):
                copy(t, 1 + j, pid, sib).wait_recv()
                fwd = copy(t, 4 + j, pid, sib)
                fwd.start()
                started.append(fwd)
        for t in range(nt):
            copy(t, 0, sib_id, sib).wait_recv()
        for j, flip in enumerate(_CHIP_FLIPS):
            _, pid = _peer(pos, flip | 1)
            for t in range(nt):
                copy(t, 4 + j, pid, sib).wait_recv()
        for cp in started:
            cp.wait_send()
        for cp in locals_:
            cp.wait()

    return pl.pallas_call(
        body, name=name,
        out_shape=tuple(jax.ShapeDtypeStruct((NDEV,) + s.shape, s.dtype) for s in shards),
        in_specs=[_HBM] * nt, out_specs=tuple([_HBM] * nt),
        scratch_shapes=[pltpu.SemaphoreType.DMA((nt, NDEV - 1)), pltpu.SemaphoreType.DMA((nt, NDEV - 1)),
                        pltpu.SemaphoreType.DMA((nt,))])(*shards)


_SPLIT = dict(has_side_effects=pltpu.SideEffectType.DATAFLOW_SIDE_EFFECTING)
_SEM = pl.BlockSpec(memory_space=pltpu.SEMAPHORE)


def _exchange_start(name, srcs, lands, copies, after):
    ns, nl = len(srcs), len(lands)
    ncopy = len(copies(None, [None] * ns, [None] * nl))

    def body(*refs):
        src_refs, land_refs = refs[:ns], refs[ns:ns + nl]
        send, recv = refs[ns + nl + len(after):ns + nl + len(after) + 2]
        token = refs[-1]
        for i, (src, dst, peer) in enumerate(copies(_mesh_pos(), src_refs, land_refs)):
            pltpu.make_async_remote_copy(src_ref=src, dst_ref=dst, send_sem=send.at[i], recv_sem=recv.at[i],
                                         device_id=peer, device_id_type=MESH).start()
        token[...] = jnp.zeros_like(token)

    hbm = lambda a: pltpu.HBM(a.shape, a.dtype)
    out = pl.pallas_call(
        body, name=name,
        out_shape=(pltpu.SemaphoreType.DMA((ncopy,)), pltpu.SemaphoreType.DMA((ncopy,)),
                   *[hbm(a) for a in srcs], *[hbm(a) for a in lands], jax.ShapeDtypeStruct((8, 128), F32)),
        in_specs=[pl.BlockSpec(memory_space=pltpu.HBM)] * (ns + nl) + [_HBM] * len(after),
        out_specs=(_SEM, _SEM, *[pl.BlockSpec(memory_space=pltpu.HBM)] * (ns + nl),
                   pl.BlockSpec(memory_space=pltpu.VMEM)),
        input_output_aliases={i: 2 + i for i in range(ns + nl)},
        compiler_params=pltpu.CompilerParams(**_SPLIT))(
            *[pltpu.with_memory_space_constraint(a, pltpu.HBM) for a in list(srcs) + list(lands)], *after)
    return out[0], out[1], list(out[2:2 + ns]), list(out[2 + ns:2 + ns + nl]), out[-1]


def _exchange_wait(name, send, recv, srcs, lands, copies, after):
    ns, nl = len(srcs), len(lands)

    def body(*refs):
        src_refs, land_refs = refs[:ns], refs[ns:ns + nl]
        send_ref, recv_ref = refs[ns + nl:ns + nl + 2]
        pos = _mesh_pos()
        for i, (src, dst, peer) in enumerate(copies(pos, src_refs, land_refs, arriving=True)):
            cp = pltpu.make_async_remote_copy(src_ref=src, dst_ref=dst, send_sem=send_ref.at[i], recv_sem=recv_ref.at[i],
                                              device_id=peer, device_id_type=MESH)
            cp.wait_send()
            cp.wait_recv()

    hbm = lambda a: pltpu.HBM(a.shape, a.dtype)
    out = pl.pallas_call(
        body, name=name, out_shape=tuple(hbm(a) for a in list(srcs) + list(lands)),
        in_specs=[pl.BlockSpec(memory_space=pltpu.HBM)] * (ns + nl) + [_SEM, _SEM] + [_HBM] * len(after),
        out_specs=tuple([pl.BlockSpec(memory_space=pltpu.HBM)] * (ns + nl)),
        input_output_aliases={i: i for i in range(ns + nl)},
        compiler_params=pltpu.CompilerParams(**_SPLIT))(*srcs, *lands, send, recv, *after)
    return list(out[:ns]), list(out[ns:])


def _gather_copies(pos, srcs, lands, arriving=False):
    if pos is None:
        return [None] * (len(srcs) * (NDEV - 1))
    me = 4 * pos[0] + 2 * pos[1] + pos[2]
    out = []
    for src, land in zip(srcs, lands):
        for k in range(1, NDEV):
            peer, pid = _peer(pos, k)
            out.append((src, land.at[pid if arriving else me], peer))
    return out


def _scatter_copies(pos, srcs, lands, arriving=False):
    if pos is None:
        return [None] * (len(srcs) * (NDEV - 1))
    out = []
    for src, land in zip(srcs, lands):
        for k in range(1, NDEV):
            peer, pid = _peer(pos, k)
            out.append((src.at[pid], land.at[k - 1], peer))
    return out


def gather_start(name, shards, lands, after=()):
    return _exchange_start(name, shards, lands, _gather_copies, list(after))


def gather_wait(name, handle, after=()):
    send, recv, srcs, lands, _ = handle
    return _exchange_wait(name, send, recv, srcs, lands, _gather_copies, list(after))


def scatter_start(name, grads, after=()):
    lands = [pltpu.with_memory_space_constraint(lax.empty((NDEV - 1,) + g.shape[1:], g.dtype), pltpu.HBM)
             for g in grads]
    return _exchange_start(name, grads, lands, _scatter_copies, list(after))


def scatter_wait(name, handle, after=()):
    send, recv, srcs, lands, _ = handle
    return _exchange_wait(name, send, recv, srcs, lands, _scatter_copies, list(after))


def sum_slots(parts, name):
    def body(p_ref, o_ref):
        acc = p_ref[0]
        for d in range(1, NDEV):
            acc = acc + p_ref[d]
        o_ref[...] = acc

    return pl.pallas_call(body, name=name, out_shape=jax.ShapeDtypeStruct(parts.shape[1:], F32))(parts)


def _adamw_update(g, w_ref, m_ref, v_ref, g_ref, d_ref, nm_ref, nv_ref):
    g_ref[...] = g
    nm = ADAM_B1 * m_ref[...] + (1.0 - ADAM_B1) * g
    nv = ADAM_B2 * v_ref[...] + (1.0 - ADAM_B2) * (g * g)
    nm_ref[...] = nm
    nv_ref[...] = nv
    m_hat = nm / (1.0 - ADAM_B1 ** ADAM_STEP)
    v_hat = nv / (1.0 - ADAM_B2 ** ADAM_STEP)
    d_ref[...] = -ADAM_LR * (m_hat / (jnp.sqrt(v_hat) + ADAM_EPS) + ADAM_WD * w_ref[...])


def adamw_big(own, lands, w, m, v, me, tr, name):
    _, r, c = own[0].shape
    nt = r // tr

    def body(me_ref, *refs):
        ins, (w_ref, m_ref, v_ref), outs = refs[:2 * DEPTH], refs[2 * DEPTH:2 * DEPTH + 3], refs[2 * DEPTH + 3:]
        layer = pl.program_id(0)
        for l in range(DEPTH):
            @pl.when(layer == l)
            def _(l=l):
                g = ins[2 * l][...].astype(F32)
                for s in range(NDEV - 1):
                    g = g + ins[2 * l + 1][s].astype(F32)
                _adamw_update(g, w_ref, m_ref, v_ref, *outs)

    pick = lambda l: (lambda layer, i, me_ref: jnp.where(layer == l, i, 0))
    in_specs = []
    for l in range(DEPTH):
        in_specs.append(pl.BlockSpec((None, tr, c), lambda layer, i, me_ref, f=pick(l): (me_ref[0], f(layer, i, me_ref), 0)))
        in_specs.append(pl.BlockSpec((NDEV - 1, tr, c), lambda layer, i, me_ref, f=pick(l): (0, f(layer, i, me_ref), 0)))
    blk = pl.BlockSpec((tr, c), lambda layer, i, me_ref: (layer * nt + i, 0))
    sh = jax.ShapeDtypeStruct((DEPTH * r, c), F32)
    args = [a for l in range(DEPTH) for a in (own[l], lands[l])]
    return pl.pallas_call(
        body, name=name, out_shape=(sh, sh, sh, sh),
        grid_spec=pltpu.PrefetchScalarGridSpec(
            num_scalar_prefetch=1, grid=(DEPTH, nt), in_specs=in_specs + [blk, blk, blk],
            out_specs=(blk, blk, blk, blk)),
        compiler_params=_params(VMEM_BIG))(me, *args, w, m, v)


def adamw_many(gs, ws, ms, vs, name):
    n = len(gs)

    def body(*refs):
        outs = refs[4 * n:]
        for i in range(n):
            _adamw_update(refs[i][...], refs[n + i], refs[2 * n + i], refs[3 * n + i], *outs[4 * i:4 * i + 4])

    out_shape = tuple(jax.ShapeDtypeStruct(w.shape, F32) for w in ws for _ in range(4))
    res = pl.pallas_call(body, name=name, out_shape=out_shape)(*gs, *ws, *ms, *vs)
    return [res[4 * i:4 * i + 4] for i in range(n)]


def local_step(x, target, P, hooks):
    tb = _ret_tables()
    h = jnp.concatenate([jnp.zeros((PAD, D), F32), P["meta"], x], axis=0)
    stash = []
    for l in range(DEPTH):
        s = {"h": h}
        s["w_in"], s["cv_pw"], s["w_out"] = hooks["mixer_w"](l, h)
        proj, s["u"] = in_proj_fwd(f"in_proj_{l}", h, P["norm_mix_g"][l], s["w_in"], 256,
                                   after=hooks["first_after"] if l == 0 else None)
        s["proj"] = proj
        s["o"] = attn_fwd(proj, P["q_norm_g"][l], P["k_norm_g"][l], P["attn_sinks"][l], f"attn_fwd_{l}")
        y_cv, s["xc"] = conv_fwd(proj, P["cv_dw_w"][l], P["cv_dw_b"][l], P["cv_ln_g"][l], P["cv_ln_b"][l],
                                 s["cv_pw"], P["cv_out_g"][l], f"conv_fwd_{l}")
        y_ret, s["states"] = ret_fwd(proj, P["ret_gn_g"][l], tb, f"ret_fwd_{l}")
        h, s["ycat"] = out_proj_fwd(f"out_proj_{l}", s["o"], P["attn_out_g"][l], y_cv, y_ret, s["w_out"], h, 512)
        s["h1"] = h
        s["ffn_up"], s["ffn_down"] = hooks["ffn_w"](l, h)
        s["f"], s["u2"] = ffn_up_fwd(f"ffn_up_{l}", h, P["norm_ffn_g"][l], s["ffn_up"])
        a, s["gc"], s["uc"] = ffn_act_fwd(s["f"], P["ffn_dw_w"][l], P["ffn_dw_b"][l], f"ffn_act_fwd_{l}")
        s["a"] = a
        h = ffn_down_fwd(f"ffn_down_{l}", a, s["ffn_down"], h)
        stash.append(s)

    dh, dhm, loss = loss_head(h, target)
    G = {n: [None] * DEPTH for n in ("norm_mix_g", "q_norm_g", "k_norm_g", "attn_sinks", "attn_out_g", "cv_dw_w",
                                     "cv_dw_b", "cv_ln_g", "cv_ln_b", "cv_out_g", "ret_gn_g", "norm_ffn_g",
                                     "ffn_dw_w", "ffn_dw_b")}
    for l in reversed(range(DEPTH)):
        s = stash[l]
        da = ffn_down_dx(f"ffn_down_dx_{l}", dhm, s["ffn_down"])
        dw_down = ffn_down_dw(f"ffn_down_dw_{l}", s["a"], dhm)
        dfg, dfu, dwg, dwu, dbg, dbu = ffn_act_bwd(s["f"], s["gc"], s["uc"], da, P["ffn_dw_w"][l], f"ffn_act_bwd_{l}")
        G["ffn_dw_w"][l] = jnp.concatenate([dwg, dwu], axis=0)
        G["ffn_dw_b"][l] = jnp.concatenate([dbg, dbu], axis=0)
        du2 = ffn_up_dx(f"ffn_up_dx_{l}", dfg, dfu, s["ffn_up"])
        dw_up = ffn_up_dw(f"ffn_up_dw_{l}", s["u2"], dfg, dfu)
        after = hooks["ffn_grads"](l, dw_down, dw_up)
        dh, dhm, G["norm_ffn_g"][l] = rms_bwd(s["h1"], P["norm_ffn_g"][l], du2, dh, f"rms_ffn_bwd_{l}", after=after)
        dycat = mm_nt(f"out_proj_dx_{l}", dhm, s["w_out"], 512)
        dw_out = mm_ta(f"out_proj_dw_{l}", s["ycat"], dhm, 512)
        (dq, dk_c, dk_p, dk_m, dv_c, dv_p, dv_m, G["q_norm_g"][l], G["k_norm_g"][l], dsk,
         G["attn_out_g"][l]) = attn_bwd(s["proj"], s["o"], dycat, P["attn_out_g"][l], P["q_norm_g"][l],
                                        P["k_norm_g"][l], P["attn_sinks"][l], f"attn_bwd_{l}")
        G["attn_sinks"][l] = dsk[:, 0]
        shift = lambda z: jnp.concatenate([z[T:], jnp.zeros((T, NKV * HD), F32)], axis=0)
        dk = (dk_c + shift(dk_p)).at[:T].add(dk_m)
        dv = (dv_c + shift(dv_p)).at[:T].add(dv_m)
        (dca, dcb, G["cv_dw_w"][l], G["cv_dw_b"][l], G["cv_ln_g"][l], G["cv_ln_b"][l], G["cv_out_g"][l],
         dpw) = conv_bwd(s["proj"], s["xc"], dycat, P["cv_dw_w"][l], P["cv_ln_g"][l], P["cv_ln_b"][l], s["cv_pw"],
                         P["cv_out_g"][l], f"conv_bwd_{l}")
        dret, G["ret_gn_g"][l] = ret_bwd(s["proj"], dycat, s["states"], P["ret_gn_g"][l], tb, f"ret_bwd_{l}")
        dproj = jnp.concatenate([dq, dk.astype(BF16), dv.astype(BF16), dca, dcb, dret], axis=1)
        du = mm_nn(f"in_proj_dx_{l}", dproj, s["w_in"], 512)
        dw_in = mm_ta(f"in_proj_dw_{l}", dproj, s["u"], 768)
        after = hooks["mixer_grads"](l, dw_out, dpw, dw_in)
        dh, dhm, G["norm_mix_g"][l] = rms_bwd(s["h"], P["norm_mix_g"][l], du, dh, f"rms_mix_bwd_{l}", after=after,
                                              last=(l == 0))
    return loss[0, 0], dh, dhm[PAD:T], G


_SMALL = ("meta", "norm_mix_g", "q_norm_g", "k_norm_g", "attn_sinks", "attn_out_g", "cv_dw_w", "cv_dw_b", "cv_ln_g",
          "cv_ln_b", "cv_out_g", "ret_gn_g", "norm_ffn_g", "ffn_dw_w", "ffn_dw_b")
_BIG = ("w_in", "cv_pw", "w_out", "ffn_up", "ffn_down")
_TRANSPOSED = ("w_in", "ffn_up")
_ORDER = ("meta", "norm_mix_g", "w_in", "q_norm_g", "k_norm_g", "attn_sinks", "attn_out_g", "cv_dw_w", "cv_dw_b",
          "cv_ln_g", "cv_ln_b", "cv_pw", "cv_out_g", "ret_gn_g", "w_out", "norm_ffn_g", "ffn_up", "ffn_dw_w",
          "ffn_dw_b", "ffn_down")
_SMALL_SHARDED = {"meta": D, "cv_dw_w": CV, "ffn_dw_w": 2 * D_FF}


def _pack(arrs):
    flat = jnp.concatenate([a.reshape(-1) for a in arrs])
    n = flat.shape[0]
    rows = -(-n // 1024) * 8
    return jnp.pad(flat, (0, rows * 128 - n)).reshape(rows, 128)


def _unpack(packed, shapes):
    flat = packed.reshape(-1)
    out, off = [], 0
    for s in shapes:
        n = int(np.prod(s))
        out.append(flat[off:off + n].reshape(s))
        off += n
    return out


def kernel(x, meta, norm_mix_g, w_in, q_norm_g, k_norm_g, attn_sinks, attn_out_g, cv_dw_w, cv_dw_b, cv_ln_g, cv_ln_b, cv_pw, cv_out_g, ret_gn_g, w_out, norm_ffn_g, ffn_up, ffn_dw_w, ffn_dw_b, ffn_down, loss_target, m_meta, m_norm_mix_g, m_w_in, m_q_norm_g, m_k_norm_g, m_attn_sinks, m_attn_out_g, m_cv_dw_w, m_cv_dw_b, m_cv_ln_g, m_cv_ln_b, m_cv_pw, m_cv_out_g, m_ret_gn_g, m_w_out, m_norm_ffn_g, m_ffn_up, m_ffn_dw_w, m_ffn_dw_b, m_ffn_down, v_meta, v_norm_mix_g, v_w_in, v_q_norm_g, v_k_norm_g, v_attn_sinks, v_attn_out_g, v_cv_dw_w, v_cv_dw_b, v_cv_ln_g, v_cv_ln_b, v_cv_pw, v_cv_out_g, v_ret_gn_g, v_w_out, v_norm_ffn_g, v_ffn_up, v_ffn_dw_w, v_ffn_dw_b, v_ffn_down):
    W = dict(meta=meta, norm_mix_g=norm_mix_g, w_in=w_in, q_norm_g=q_norm_g, k_norm_g=k_norm_g,
             attn_sinks=attn_sinks, attn_out_g=attn_out_g, cv_dw_w=cv_dw_w, cv_dw_b=cv_dw_b, cv_ln_g=cv_ln_g,
             cv_ln_b=cv_ln_b, cv_pw=cv_pw, cv_out_g=cv_out_g, ret_gn_g=ret_gn_g, w_out=w_out,
             norm_ffn_g=norm_ffn_g, ffn_up=ffn_up, ffn_dw_w=ffn_dw_w, ffn_dw_b=ffn_dw_b, ffn_down=ffn_down)
    M = dict(meta=m_meta, norm_mix_g=m_norm_mix_g, w_in=m_w_in, q_norm_g=m_q_norm_g, k_norm_g=m_k_norm_g,
             attn_sinks=m_attn_sinks, attn_out_g=m_attn_out_g, cv_dw_w=m_cv_dw_w, cv_dw_b=m_cv_dw_b,
             cv_ln_g=m_cv_ln_g, cv_ln_b=m_cv_ln_b, cv_pw=m_cv_pw, cv_out_g=m_cv_out_g, ret_gn_g=m_ret_gn_g,
             w_out=m_w_out, norm_ffn_g=m_norm_ffn_g, ffn_up=m_ffn_up, ffn_dw_w=m_ffn_dw_w, ffn_dw_b=m_ffn_dw_b,
             ffn_down=m_ffn_down)
    V = dict(meta=v_meta, norm_mix_g=v_norm_mix_g, w_in=v_w_in, q_norm_g=v_q_norm_g, k_norm_g=v_k_norm_g,
             attn_sinks=v_attn_sinks, attn_out_g=v_attn_out_g, cv_dw_w=v_cv_dw_w, cv_dw_b=v_cv_dw_b,
             cv_ln_g=v_cv_ln_g, cv_ln_b=v_cv_ln_b, cv_pw=v_cv_pw, cv_out_g=v_cv_out_g, ret_gn_g=v_ret_gn_g,
             w_out=v_w_out, norm_ffn_g=v_norm_ffn_g, ffn_up=v_ffn_up, ffn_dw_w=v_ffn_dw_w, ffn_dw_b=v_ffn_dw_b,
             ffn_down=v_ffn_down)
    me = 4 * lax.axis_index("x") + 2 * lax.axis_index("y") + lax.axis_index("c")
    for n in _TRANSPOSED:
        W[n], M[n], V[n] = (a.transpose(0, 2, 1) for a in (W[n], M[n], V[n]))

    sh = {n: [W[n][l].astype(BF16) for l in range(DEPTH)] for n in _BIG}
    mix = lambda l: [sh["w_in"][l], sh["cv_pw"][l], sh["w_out"][l]]
    ffn = lambda l: [sh["ffn_up"][l], sh["ffn_down"][l]]
    first = all_gather(mix(0) + [meta, cv_dw_w, ffn_dw_w], "gather_first")
    g_meta, g_cdw, g_fdw = first[3:6]

    def landing(shards):
        return [lax.dynamic_update_slice(lax.empty((NDEV,) + s.shape, s.dtype), s[None], (me,) + (0,) * s.ndim)
                for s in shards]

    gathers = {("ffn", 0): gather_start("gather_ffn0_start", ffn(0), landing(ffn(0)), after=[first[0]])}
    gathers["mix", 1] = gather_start("gather_mix1_start", mix(1), landing(mix(1)), after=[gathers["ffn", 0][4]])
    gathers["ffn", 1] = gather_start("gather_ffn1_start", ffn(1), landing(ffn(1)), after=[gathers["mix", 1][4]])

    def mixer_w(l, h):
        g_in, g_pw, g_out = first[0:3] if l == 0 else gather_wait(f"gather_mix{l}_wait", gathers["mix", l], after=[h])[1]
        return g_in.reshape(IN_W, D), g_pw.reshape(CV, CV), g_out.reshape(D, D)

    def ffn_w(l, h1):
        return gather_wait(f"gather_ffn{l}_wait", gathers["ffn", l], after=[h1])[1]

    scatters = {}

    def ffn_grads(l, dw_down, dw_up):
        scatters["ffn", l] = scatter_start(f"scatter_ffn{l}_start", [dw_up, dw_down.reshape(NDEV, DNR, D)])
        return scatters["ffn", l][4]

    def mixer_grads(l, dw_out, dpw, dw_in):
        grads = [dw_in.reshape(NDEV, IN_W // NDEV, D), dpw.astype(BF16).reshape(NDEV, CV // NDEV, CV),
                 dw_out.reshape(NDEV, D // NDEV, D)]
        scatters["mix", l] = scatter_start(f"scatter_mix{l}_start", grads)
        return scatters["mix", l][4]

    P = dict(
        meta=g_meta.transpose(1, 0, 2).reshape(N_META, D),
        cv_dw_w=g_cdw.transpose(1, 2, 0, 3).reshape(DEPTH, CONV_K, CV),
        ffn_dw_w=g_fdw.transpose(1, 0, 2, 3),
        ffn_dw_b=ffn_dw_b.reshape(DEPTH, NDEV, 1, UPW),
        attn_sinks=attn_sinks,
    )
    for n in ("norm_mix_g", "q_norm_g", "k_norm_g", "attn_out_g", "cv_dw_b", "cv_ln_g", "cv_ln_b", "cv_out_g",
              "ret_gn_g", "norm_ffn_g"):
        P[n] = W[n].reshape(DEPTH, 1, -1)

    hooks = dict(mixer_w=mixer_w, ffn_w=ffn_w, ffn_grads=ffn_grads, mixer_grads=mixer_grads,
                 first_after=gathers["ffn", 1][4])
    loss_part, dx, dmeta, G = local_step(x[0], loss_target[0], P, hooks)

    small_full = {
        "meta": dmeta,
        "cv_dw_w": jnp.stack(G["cv_dw_w"]),
        "ffn_dw_w": jnp.stack([g.transpose(1, 0, 2).reshape(FFN_K, 2 * D_FF) for g in G["ffn_dw_w"]]),
        "ffn_dw_b": jnp.stack([g.reshape(2 * D_FF) for g in G["ffn_dw_b"]]),
        "attn_sinks": jnp.stack(G["attn_sinks"]),
    }
    for n in _SMALL:
        if n not in small_full:
            small_full[n] = jnp.stack([g.reshape(-1) for g in G[n]])
    shapes = [small_full[n].shape for n in _SMALL] + [(1,)]
    packed = _pack([small_full[n] for n in _SMALL] + [loss_part.reshape(1)])
    small_exchange = gather_start("reduce_small_start", [packed], landing([packed]), after=[dx])

    out = {}
    tiles = {"w_in": 144, "cv_pw": 32, "w_out": 128, "ffn_up": 176, "ffn_down": 176}
    me1 = me.astype(jnp.int32).reshape(1)
    own, lands = {n: [None] * DEPTH for n in _BIG}, {n: [None] * DEPTH for n in _BIG}

    def arrived(kind, names, after):
        for l in range(DEPTH):
            srcs, got = scatter_wait(f"scatter_{kind}{l}_wait", scatters[kind, l], after=after)
            for n, s_, g_ in zip(names, srcs, got):
                own[n][l], lands[n][l] = s_, g_

    def update(names):
        for n in names:
            shard = W[n].shape
            rows, cols = shard[0] * shard[1], shard[2]
            res = adamw_big(own[n], lands[n], W[n].reshape(rows, cols), M[n].reshape(rows, cols),
                            V[n].reshape(rows, cols), me1, tiles[n], f"adamw_{n}")
            out[n] = [r.reshape(shard) for r in res]

    arrived("ffn", ("ffn_up", "ffn_down"), [small_exchange[4]])
    update(("ffn_up", "ffn_down"))
    slots = gather_wait("reduce_small_wait", small_exchange, after=[out["ffn_down"][0]])[1][0]
    summed = _unpack(sum_slots(slots, "reduce_small_sum"), shapes)
    loss = summed[-1][0]
    small_g = []
    for n, g in zip(_SMALL, summed):
        if n in _SMALL_SHARDED:
            width = _SMALL_SHARDED[n] // NDEV
            g = lax.dynamic_slice_in_dim(g, me * width, width, axis=g.ndim - 1)
        small_g.append(g)
    flat2 = lambda a: a.reshape(-1, a.shape[-1])
    res = adamw_many([flat2(g) for g in small_g], *[[flat2(X[n]) for n in _SMALL] for X in (W, M, V)], "adamw_small")
    for n, r in zip(_SMALL, res):
        out[n] = [a.reshape(W[n].shape) for a in r]
    arrived("mix", ("w_in", "cv_pw", "w_out"), [out["ffn_down"][0], res[0][0]])
    update(("w_in", "cv_pw", "w_out"))
    for n in _TRANSPOSED:
        out[n] = [r.transpose(0, 2, 1) for r in out[n]]

    return (loss, dx[None], *[out[n][0] for n in _ORDER], *[out[n][1] for n in _ORDER],
            *[out[n][2] for n in _ORDER], *[out[n][3] for n in _ORDER])
```

```python
import functools
import math

import numpy as np
import jax
import jax.numpy as jnp
from jax import lax
from jax.experimental import pallas as pl
from jax.experimental.pallas import tpu as pltpu

F32 = jnp.float32
BF16 = jnp.bfloat16

D = 1024
SEQ = 2048
DEPTH = 2
T = 128
L = SEQ + T
NB = L // T
N_META = 16
PAD = T - N_META
HD = 64
NQ = 8
NKV = 2
GQA = NQ // NKV
CV = 256
CONV_K = 31
RH = 4
D_FF = 2816
FFN_K = 3
IN_W = 2304
RMS_EPS = 1e-6
LN_EPS = 1e-5
NEG = -1e30
NDEV = 8
UPW = 2 * D_FF // NDEV
DNR = D_FF // NDEV
NPAIR = NDEV // 2

ADAM_LR, ADAM_B1, ADAM_B2, ADAM_EPS, ADAM_WD, ADAM_STEP = 0.001, 0.9, 0.999, 1e-08, 0.01, 10

VMEM_BIG = 56 * 1024 * 1024

MESH = pl.DeviceIdType.MESH


def _params(vmem=None):
    return pltpu.CompilerParams(vmem_limit_bytes=vmem) if vmem else None


def _dot(a, b, nt=False):
    return lax.dot_general(a, b, (((1,), (1 if nt else 0,)), ((), ())), preferred_element_type=F32)


def _sig(x):
    return 1.0 / (1.0 + jnp.exp(-x))


def _bf(x):
    return x.astype(BF16)


def _stack_rows(rows):
    idx = lax.broadcasted_iota(jnp.int32, (len(rows), rows[0].shape[1]), 0)
    out = jnp.zeros((len(rows), rows[0].shape[1]), F32)
    for r, v in enumerate(rows):
        out = jnp.where(idx == r, v, out)
    return out


def rms_bwd(x, g, dy, dres, name, after=None, last=False):
    n, w = x.shape
    has_res = dres is not None
    deps = [] if after is None else [after]

    def body(x_ref, g_ref, dy_ref, *rest):
        rest = rest[:len(rest) - 3 - len(deps)] + rest[len(rest) - 3:]
        if has_res:
            dres_ref, dx_ref, dxm_ref, dg_ref = rest
        else:
            dx_ref, dxm_ref, dg_ref = rest
        i = pl.program_id(0)
        xv = x_ref[...]
        r = lax.rsqrt(jnp.mean(xv * xv, axis=-1, keepdims=True) + RMS_EPS)
        xh = xv * r
        dyv = dy_ref[...]
        dxh = dyv * g_ref[...]
        dx = r * (dxh - xh * jnp.mean(dxh * xh, axis=-1, keepdims=True))
        if has_res:
            dx = dx + dres_ref[...]
        dx_ref[...] = dx
        if last:
            @pl.when(i == 0)
            def _():
                dxm_ref[...] = dx
        else:
            rows = i * T + lax.broadcasted_iota(jnp.int32, (T, 1), 0)
            dxm_ref[...] = jnp.where(rows >= PAD, dx, 0.0).astype(BF16)
        part = jnp.sum(dyv * xh, axis=0, keepdims=True)

        @pl.when(i == 0)
        def _():
            dg_ref[...] = part

        @pl.when(i > 0)
        def _():
            dg_ref[...] += part

    row = pl.BlockSpec((T, w), lambda i: (i, 0))
    vec = pl.BlockSpec((1, w), lambda i: (0, 0))
    ins = [x, g, dy] + ([dres] if has_res else []) + deps
    if last:
        out_shape = (jax.ShapeDtypeStruct((n - T, w), F32), jax.ShapeDtypeStruct((T, w), F32))
        out_specs = (pl.BlockSpec((T, w), lambda i: (jnp.maximum(i - 1, 0), 0)), pl.BlockSpec((T, w), lambda i: (0, 0)))
    else:
        out_shape = (jax.ShapeDtypeStruct((n, w), F32), jax.ShapeDtypeStruct((n, w), BF16))
        out_specs = (row, row)
    return pl.pallas_call(
        body, name=name, out_shape=(*out_shape, jax.ShapeDtypeStruct((1, w), F32)), grid=(n // T,),
        in_specs=[row, vec, row] + ([row] if has_res else [])
        + [_HBM] * len(deps),
        out_specs=(*out_specs, vec))(*ins)


def _mm(name, a, b, *, grid, a_spec, b_spec, o_spec, out_shape, nt=False, ta=False, red=False, res=None,
        res_spec=None):
    def body(a_ref, b_ref, *rest):
        o_ref = rest[-1]
        av = a_ref[...]
        bv = b_ref[...]
        if bv.ndim == 3:
            bv = bv.reshape(bv.shape[0] * bv.shape[1], bv.shape[2])
        if ta:
            acc = lax.dot_general(av, bv, (((0,), (0,)), ((), ())), preferred_element_type=F32)
        else:
            acc = _dot(av, bv, nt)
        if red:
            k = pl.program_id(0)

            @pl.when(k == 0)
            def _():
                o_ref[...] = acc

            @pl.when(k > 0)
            def _():
                o_ref[...] += acc
        else:
            if res is not None:
                rows = lax.broadcasted_iota(jnp.int32, (acc.shape[0], 1), 0)
                acc = rest[0][...] + jnp.where(rows >= PAD, acc, 0.0)
            o_ref[...] = acc.astype(o_ref.dtype)

    ins = [a, b] + ([res] if res is not None else [])
    specs = [a_spec, b_spec] + ([res_spec] if res is not None else [])
    return pl.pallas_call(body, name=name, out_shape=out_shape, grid=grid, in_specs=specs, out_specs=o_spec,
                          compiler_params=_params(VMEM_BIG))(*ins)


def mm_nn(name, a, b, tn, out_dtype=F32, res=None):
    m, k = a.shape
    n = b.shape[1]
    return _mm(name, a, b, grid=(n // tn,),
               a_spec=pl.BlockSpec((m, k), lambda j: (0, 0)), b_spec=pl.BlockSpec((k, tn), lambda j: (0, j)),
               o_spec=pl.BlockSpec((m, tn), lambda j: (0, j)), out_shape=jax.ShapeDtypeStruct((m, n), out_dtype),
               res=res, res_spec=pl.BlockSpec((m, tn), lambda j: (0, j)))


def mm_nt(name, a, b, tn):
    m, k = a.shape
    n = b.shape[0]
    return _mm(name, a, b, grid=(n // tn,), nt=True,
               a_spec=pl.BlockSpec((m, k), lambda j: (0, 0)), b_spec=pl.BlockSpec((tn, k), lambda j: (j, 0)),
               o_spec=pl.BlockSpec((m, tn), lambda j: (0, j)), out_shape=jax.ShapeDtypeStruct((m, n), F32))


def mm_ta(name, a, b, tm, out_dtype=BF16):
    k, m = a.shape
    n = b.shape[1]
    return _mm(name, a, b, grid=(m // tm,), ta=True,
               a_spec=pl.BlockSpec((k, tm), lambda j: (0, j)), b_spec=pl.BlockSpec((k, n), lambda j: (0, 0)),
               o_spec=pl.BlockSpec((tm, n), lambda j: (j, 0)), out_shape=jax.ShapeDtypeStruct((m, n), out_dtype))


def _rms_mm(name, h, g, b, *, grid, b_spec, o_spec, out_shape, after=None):
    deps = [] if after is None else [after]

    def body(h_ref, g_ref, b_ref, *rest):
        o_ref, u_ref = rest[len(deps):]

        @pl.when(pl.program_id(0) == 0)
        def _():
            gv = g_ref[...]
            for c in range(NB):
                rows = slice(c * T, (c + 1) * T)
                xv = h_ref[rows, :]
                r = lax.rsqrt(jnp.mean(xv * xv, axis=-1, keepdims=True) + RMS_EPS)
                u_ref[rows, :] = (xv * r * gv).astype(BF16)

        o_ref[...] = _dot(u_ref[...], b_ref[...], nt=True)

    whole = lambda r: pl.BlockSpec((r, D), lambda j: (0, 0))
    return pl.pallas_call(
        body, name=name, out_shape=(out_shape, jax.ShapeDtypeStruct((L, D), BF16)), grid=grid,
        in_specs=[whole(L), whole(1), b_spec] + [_HBM] * len(deps), out_specs=(o_spec, whole(L)),
        compiler_params=_params(VMEM_BIG))(h, g, b, *deps)


def out_proj_fwd(name, o, og, y_cv, y_ret, w, res, tn):
    wa = o.shape[1]

    def body(o_ref, og_ref, cv_ref, ret_ref, w_ref, r_ref, out_ref, ycat_ref):
        @pl.when(pl.program_id(0) == 0)
        def _():
            gv = og_ref[...]
            for c in range(NB):
                rows = slice(c * T, (c + 1) * T)
                ycat_ref[rows, 0:wa] = _rms_rows(o_ref[rows, :], gv)[0].astype(BF16)
                ycat_ref[rows, wa:wa + CV] = cv_ref[rows, :]
                ycat_ref[rows, wa + CV:D] = ret_ref[rows, :]

        acc = _dot(ycat_ref[...], w_ref[...])
        rows = lax.broadcasted_iota(jnp.int32, (L, 1), 0)
        out_ref[...] = r_ref[...] + jnp.where(rows >= PAD, acc, 0.0)

    whole = lambda w_: pl.BlockSpec((L, w_), lambda j: (0, 0))
    tile = pl.BlockSpec((L, tn), lambda j: (0, j))
    return pl.pallas_call(
        body, name=name, out_shape=(jax.ShapeDtypeStruct((L, D), F32), jax.ShapeDtypeStruct((L, D), BF16)),
        grid=(D // tn,),
        in_specs=[whole(wa), pl.BlockSpec((1, wa), lambda j: (0, 0)), whole(CV), whole(RW),
                  pl.BlockSpec((D, tn), lambda j: (0, j)), tile],
        out_specs=(tile, whole(D)), compiler_params=_params(VMEM_BIG))(o, og, y_cv, y_ret, w, res)


def in_proj_fwd(name, h, g, wt, tn, after=None):
    return _rms_mm(name, h, g, wt, grid=(IN_W // tn,), b_spec=pl.BlockSpec((tn, D), lambda j: (j, 0)),
                   o_spec=pl.BlockSpec((L, tn), lambda j: (0, j)), out_shape=jax.ShapeDtypeStruct((L, IN_W), F32),
                   after=after)


def ffn_up_fwd(name, h, g, wupt):
    return _rms_mm(name, h, g, wupt, grid=(NDEV,), b_spec=pl.BlockSpec((None, UPW, D), lambda j: (j, 0, 0)),
                   o_spec=pl.BlockSpec((None, L, UPW), lambda j: (j, 0, 0)),
                   out_shape=jax.ShapeDtypeStruct((NDEV, L, UPW), F32))


def _slab_specs():
    gate = pl.BlockSpec((None, L, UPW), lambda j: (jnp.minimum(j, NPAIR - 1), 0, 0))
    up = pl.BlockSpec((None, L, UPW), lambda j: (jnp.maximum(j - NPAIR, 0), 0, 0))
    return gate, up


def ffn_up_dx(name, dfg, dfu, wupt):
    def body(g_ref, u_ref, b_ref, o_ref):
        j = pl.program_id(0)

        @pl.when(j == 0)
        def _():
            o_ref[...] = _dot(g_ref[...], b_ref[...])

        @pl.when(jnp.logical_and(j > 0, j < NPAIR))
        def _():
            o_ref[...] += _dot(g_ref[...], b_ref[...])

        @pl.when(j >= NPAIR)
        def _():
            o_ref[...] += _dot(u_ref[...], b_ref[...])

    gate, up = _slab_specs()
    return pl.pallas_call(
        body, name=name, out_shape=jax.ShapeDtypeStruct((L, D), F32), grid=(NDEV,),
        in_specs=[gate, up, pl.BlockSpec((None, UPW, D), lambda j: (j, 0, 0))],
        out_specs=pl.BlockSpec((L, D), lambda j: (0, 0)), compiler_params=_params(VMEM_BIG))(dfg, dfu, wupt)


def ffn_up_dw(name, u2, dfg, dfu):
    tdot = lambda a, b: lax.dot_general(a, b, (((0,), (0,)), ((), ())), preferred_element_type=F32)

    def body(a_ref, g_ref, u_ref, o_ref):
        j = pl.program_id(0)

        @pl.when(j < NPAIR)
        def _():
            o_ref[...] = tdot(g_ref[...], a_ref[...]).astype(o_ref.dtype)

        @pl.when(j >= NPAIR)
        def _():
            o_ref[...] = tdot(u_ref[...], a_ref[...]).astype(o_ref.dtype)

    gate, up = _slab_specs()
    return pl.pallas_call(
        body, name=name, out_shape=jax.ShapeDtypeStruct((NDEV, UPW, D), BF16), grid=(NDEV,),
        in_specs=[pl.BlockSpec((L, D), lambda j: (0, 0)), gate, up],
        out_specs=pl.BlockSpec((None, UPW, D), lambda j: (j, 0, 0)), compiler_params=_params(VMEM_BIG))(u2, dfg, dfu)


def _ffn_down_tile(a_ref, b_ref, r_ref, tn):
    acc = jnp.zeros((L, tn), F32)
    for g in range(NPAIR):
        bv = b_ref[2 * g:2 * g + 2]
        acc = acc + _dot(a_ref[g], bv.reshape(2 * DNR, tn))
    rows = lax.broadcasted_iota(jnp.int32, (L, 1), 0)
    return r_ref[...] + jnp.where(rows >= PAD, acc, 0.0)


def _ffn_down_specs(tn):
    return [pl.BlockSpec((NPAIR, L, UPW), lambda j: (0, 0, 0)), pl.BlockSpec((NDEV, DNR, tn), lambda j: (0, 0, j)),
            pl.BlockSpec((L, tn), lambda j: (0, j))]


def ffn_down_fwd(name, a, wdn, res, tn=256):
    def body(a_ref, b_ref, r_ref, o_ref):
        o_ref[...] = _ffn_down_tile(a_ref, b_ref, r_ref, tn)

    return pl.pallas_call(
        body, name=name, out_shape=jax.ShapeDtypeStruct((L, D), F32), grid=(D // tn,),
        in_specs=_ffn_down_specs(tn), out_specs=pl.BlockSpec((L, tn), lambda j: (0, j)),
        compiler_params=_params(VMEM_BIG))(a, wdn, res)


def ffn_down_loss(name, a, wdn, res, target, tn=256):
    def body(a_ref, b_ref, r_ref, t_ref, dh_ref, dhm_ref, loss_ref):
        j = pl.program_id(0)
        hv = _ffn_down_tile(a_ref, b_ref, r_ref, tn)
        e = hv[T:] - t_ref[...]
        dh = e * (1.0 / D)
        dh_ref[0:T, :] = jnp.zeros((T, tn), F32)
        dh_ref[T:, :] = dh
        dhm_ref[0:T, :] = jnp.zeros((T, tn), BF16)
        dhm_ref[T:, :] = dh.astype(BF16)
        part = jnp.sum(jnp.sum(e * e, axis=1, keepdims=True), axis=0, keepdims=True) * (0.5 / D)

        @pl.when(j == 0)
        def _():
            loss_ref[...] = jnp.broadcast_to(part, loss_ref.shape)

        @pl.when(j > 0)
        def _():
            loss_ref[...] += jnp.broadcast_to(part, loss_ref.shape)

    tile = pl.BlockSpec((L, tn), lambda j: (0, j))
    return pl.pallas_call(
        body, name=name,
        out_shape=(jax.ShapeDtypeStruct((L, D), F32), jax.ShapeDtypeStruct((L, D), BF16),
                   jax.ShapeDtypeStruct((8, 128), F32)),
        grid=(D // tn,),
        in_specs=_ffn_down_specs(tn) + [pl.BlockSpec((SEQ, tn), lambda j: (0, j))],
        out_specs=(tile, tile, pl.BlockSpec((8, 128), lambda j: (0, 0))),
        compiler_params=_params(VMEM_BIG))(a, wdn, res, target)


def ffn_down_dx(name, dh, wdn):
    return _mm(name, dh, wdn, grid=(NPAIR,), nt=True,
               a_spec=pl.BlockSpec((L, D), lambda g: (0, 0)),
               b_spec=pl.BlockSpec((2, DNR, D), lambda g: (g, 0, 0)),
               o_spec=pl.BlockSpec((None, L, UPW), lambda g: (g, 0, 0)),
               out_shape=jax.ShapeDtypeStruct((NPAIR, L, UPW), F32))


def ffn_down_dw(name, a, dh):
    return _mm(name, a, dh, grid=(NPAIR,), ta=True,
               a_spec=pl.BlockSpec((None, L, UPW), lambda g: (g, 0, 0)),
               b_spec=pl.BlockSpec((L, D), lambda g: (0, 0)),
               o_spec=pl.BlockSpec((UPW, D), lambda g: (g, 0)),
               out_shape=jax.ShapeDtypeStruct((D_FF, D), BF16))


_SLOPES = [2.0 ** (-8.0 * (h + 1) / NQ) for h in range(NQ)]
_SCALE = HD ** -0.5


QR = GQA * T
KC = 3 * T


def _attn_geometry(g, n, sink_ref):
    row = lax.broadcasted_iota(jnp.int32, (QR, KC), 0)
    col = lax.broadcasted_iota(jnp.int32, (QR, KC), 1)
    i = row & (T - 1)
    j = col & (T - 1)
    blk = col // T
    d_meta = n * T + i - j
    ok_meta = (j >= PAD) & (d_meta >= 0)
    ok_prev = j > i + jnp.where(n >= 2, 0, T)
    ok_cur = j <= i - jnp.where(n >= 1, 0, T)
    ok = ((blk == 0) & ok_meta) | ((blk == 1) & ok_prev) | ((blk == 2) & ok_cur)
    dist = jnp.where(blk == 0, jnp.minimum(d_meta, T), jnp.where(blk == 1, T + i - j, i - j)).astype(F32)
    head = lax.broadcasted_iota(jnp.int32, (QR, 1), 0) // T
    slope = jnp.zeros((QR, 1), F32)
    sink = jnp.zeros((QR, 1), F32)
    for hh in range(GQA):
        slope = jnp.where(head == hh, jnp.where(g == 0, _SLOPES[hh], _SLOPES[GQA + hh]), slope)
        sink = jnp.where(head == hh, sink_ref[g * GQA + hh], sink)
    return ok, slope * dist, sink


def _attn_probs(qn, keys, geo):
    ok, penalty, sink = geo
    s = jnp.where(ok, _dot(qn, keys, nt=True) * _SCALE - penalty, NEG)
    m = jnp.maximum(jnp.max(s, axis=-1, keepdims=True), sink)
    e = jnp.exp(s - m)
    e_sink = jnp.exp(sink - m)
    inv = 1.0 / (e.sum(axis=-1, keepdims=True) + e_sink)
    return e, inv, e_sink * inv


def _rms_rows(x, g):
    r = lax.rsqrt(jnp.mean(x * x, axis=-1, keepdims=True) + RMS_EPS)
    xh = x * r
    return xh * g, xh, r


def _rms_rows_bwd(dy, g, xh, r):
    dxh = dy * g
    return r * (dxh - xh * jnp.mean(dxh * xh, axis=-1, keepdims=True))


def _rows3(m_ref, p_ref, c_ref):
    return jnp.concatenate([m_ref[...], p_ref[...], c_ref[...]], axis=0)


_ATT_K, _ATT_V = 4, 5


def _nat_specs():
    qspec = pl.BlockSpec((T, GQA * HD), lambda n, g: (n, g))
    kv = lambda col: (pl.BlockSpec((T, 2 * HD), lambda n, g: (n, col)),
                      pl.BlockSpec((T, 2 * HD), lambda n, g: (jnp.maximum(n - 1, 0), col)),
                      pl.BlockSpec((T, 2 * HD), lambda n, g: (0, col)))
    return qspec, kv(_ATT_K), kv(_ATT_V)


def _group_lanes(g, x):
    return jnp.where(g == 0, x[:, :HD], x[:, HD:])


def _stack_heads(x):
    return jnp.concatenate([x[:, hh * HD:(hh + 1) * HD] for hh in range(GQA)], axis=0)


def _unstack_heads(x):
    return jnp.concatenate([x[hh * T:(hh + 1) * T] for hh in range(GQA)], axis=1)


def attn_fwd(proj, qg, kg, sinks, name):
    def body(sink_ref, q_ref, kc_ref, kp_ref, km_ref, vc_ref, vp_ref, vm_ref, qg_ref, kg_ref, o_ref):
        n = pl.program_id(0)
        g = pl.program_id(1)
        geo = _attn_geometry(g, n, sink_ref)
        keys = _bf(_rms_rows(_group_lanes(g, _rows3(km_ref, kp_ref, kc_ref)), kg_ref[...])[0])
        vals = _bf(_group_lanes(g, _rows3(vm_ref, vp_ref, vc_ref)))
        qn = _bf(_rms_rows(_stack_heads(q_ref[...]), qg_ref[...])[0])
        e, inv, _ = _attn_probs(qn, keys, geo)
        o_ref[...] = _unstack_heads(_dot(_bf(e), vals) * inv)

    qspec, (kc, kp, km), (vc, vp, vm) = _nat_specs()
    vec = pl.BlockSpec((1, HD), lambda n, g: (0, 0))
    return pl.pallas_call(
        body, name=name, out_shape=jax.ShapeDtypeStruct((L, NQ * HD), F32), grid=(NB, NKV),
        in_specs=[pl.BlockSpec(memory_space=pltpu.SMEM), qspec, kc, kp, km, vc, vp, vm, vec, vec],
        out_specs=qspec)(sinks, proj, proj, proj, proj, proj, proj, proj, qg, kg)


def attn_bwd(proj, o, dycat, og, qg, kg, sinks, name):
    def body(sink_ref, q_ref, kc_ref, kp_ref, km_ref, vc_ref, vp_ref, vm_ref, o_ref, dy_ref, og_ref, qg_ref, kg_ref,
             dq_ref, dkc_ref, dkp_ref, dkm_ref, dvc_ref, dvp_ref, dvm_ref, dqg_ref, dkg_ref, dsk_ref, dog_ref):
        n = pl.program_id(0)
        g = pl.program_id(1)

        @pl.when(jnp.logical_and(g == 0, n == 0))
        def _():
            for r in (dkm_ref, dvm_ref, dqg_ref, dkg_ref, dsk_ref, dog_ref):
                r[...] = jnp.zeros_like(r)

        ogv = og_ref[...]
        dyv = dy_ref[...]
        _, oh, orr = _rms_rows(o_ref[...], ogv)
        do_all = _rms_rows_bwd(dyv, ogv, oh, orr)
        do_grp = jnp.where(g == 0, do_all[:, :GQA * HD], do_all[:, GQA * HD:])

        @pl.when(g == 0)
        def _():
            dog_ref[...] += jnp.sum(dyv * oh, axis=0, keepdims=True)

        geo = _attn_geometry(g, n, sink_ref)
        kgv = kg_ref[...]
        qgv = qg_ref[...]
        kn_f, kh, kr = _rms_rows(_group_lanes(g, _rows3(km_ref, kp_ref, kc_ref)), kgv)
        keys = _bf(kn_f)
        vals = _bf(_group_lanes(g, _rows3(vm_ref, vp_ref, vc_ref)))
        qn_f, qh, qr = _rms_rows(_stack_heads(q_ref[...]), qgv)
        qn = _bf(qn_f)
        e, inv, p_sink = _attn_probs(qn, keys, geo)
        dob = _bf(_stack_heads(do_grp))
        p = e * inv
        dp = _dot(dob, vals, nt=True)
        delta = (p * dp).sum(axis=-1, keepdims=True)
        head_row = lax.broadcasted_iota(jnp.int32, (NQ, 128), 0)
        dsk = jnp.zeros((NQ, 128), F32)
        for hh in range(GQA):
            part = -jnp.sum((p_sink * delta)[hh * T:(hh + 1) * T], axis=0, keepdims=True)
            dsk = jnp.where(head_row == g * GQA + hh, part, dsk)
        dsk_ref[...] += dsk
        ds = p * (dp - delta)
        dqn = _dot(_bf(ds), keys) * _SCALE
        dq_ref[...] = _unstack_heads(_rms_rows_bwd(dqn, qgv, qh, qr)).astype(dq_ref.dtype)
        dqg_ref[...] += jnp.sum(dqn * qh, axis=0, keepdims=True)
        dkn = _dot(_bf(ds.T), qn) * _SCALE
        dkg_ref[...] += jnp.sum(dkn * kh, axis=0, keepdims=True)
        dk_all = _rms_rows_bwd(dkn, kgv, kh, kr)
        dv_all = _dot(_bf(p.T), dob)

        for gg in range(NKV):
            @pl.when(g == gg)
            def _(gg=gg):
                lanes = slice(gg * HD, (gg + 1) * HD)
                dkm_ref[:, lanes] += dk_all[0:T]
                dkp_ref[:, lanes] = dk_all[T:2 * T]
                dkc_ref[:, lanes] = dk_all[2 * T:3 * T]
                dvm_ref[:, lanes] += dv_all[0:T]
                dvp_ref[:, lanes] = dv_all[T:2 * T]
                dvc_ref[:, lanes] = dv_all[2 * T:3 * T]

    qspec, (kc, kp, km), (vc, vp, vm) = _nat_specs()
    vec = pl.BlockSpec((1, HD), lambda n, g: (0, 0))
    cur = pl.BlockSpec((T, 2 * HD), lambda n, g: (n, 0))
    meta = pl.BlockSpec((T, 2 * HD), lambda n, g: (0, 0))
    kv_shape = jax.ShapeDtypeStruct((L, 2 * HD), F32)
    meta_shape = jax.ShapeDtypeStruct((T, 2 * HD), F32)
    vec_shape = jax.ShapeDtypeStruct((1, HD), F32)
    allq = pl.BlockSpec((T, NQ * HD), lambda n, g: (n, 0))
    ogvec = pl.BlockSpec((1, NQ * HD), lambda n, g: (0, 0))
    return pl.pallas_call(
        body, name=name,
        out_shape=(jax.ShapeDtypeStruct((L, NQ * HD), BF16), kv_shape, kv_shape, meta_shape, kv_shape, kv_shape,
                   meta_shape, vec_shape, vec_shape, jax.ShapeDtypeStruct((NQ, 128), F32),
                   jax.ShapeDtypeStruct((1, NQ * HD), F32)),
        grid=(NB, NKV),
        in_specs=[pl.BlockSpec(memory_space=pltpu.SMEM), qspec, kc, kp, km, vc, vp, vm, allq, allq, ogvec, vec, vec],
        out_specs=(qspec, cur, cur, meta, cur, cur, meta, vec, vec,
                   pl.BlockSpec((NQ, 128), lambda n, g: (0, 0)), ogvec))(
            sinks, proj, proj, proj, proj, proj, proj, proj, o, dycat, og, qg, kg)


RW = RH * HD


def _ret_tables():
    h = np.arange(RH, dtype=np.float64)
    lg = np.log1p(-np.exp2(-5.0 - h))
    idx = np.arange(T, dtype=np.float64)
    diff = idx[:, None] - idx[None, :]
    decay = np.where(diff[None] >= 0, np.exp(np.maximum(diff, 0.0)[None] * lg[:, None, None]), 0.0)
    zeta = np.exp((T - 1 - idx)[None, :] * lg[:, None])
    xi = np.exp((idx + 1.0)[None, :] * lg[:, None])
    cd = np.exp(T * lg)
    lanes = lambda a: np.repeat(a.T, HD, axis=1)
    head_of = np.arange(RW) // HD
    same = (head_of[:, None] == head_of[None, :]).astype(np.float64)
    f = lambda a: jnp.asarray(a, F32)
    return dict(decay=f(decay), zeta=f(lanes(zeta)), xi=f(lanes(xi)), cd=f(np.repeat(cd, HD)[None, :]),
                head=f((head_of[None, :] == np.arange(RH)[:, None]).astype(np.float64)[:, None, :]),
                same=f(same), avg=jnp.asarray(same / HD, BF16))


def _seg_mean(x, avg):
    hi = _bf(x)
    lo = _bf(x - hi.astype(F32))
    return _dot(hi, avg) + _dot(lo, avg)


def _ret_specs(col0, order):
    return lambda col: pl.BlockSpec((T, RW), lambda i: (order(i), col0 + col))


def _ret_chunk(q, kf, v, s, tb):
    dec, xi, head = tb
    vb = _bf(v)
    kb = _bf(kf)
    y = _dot(_bf(q * xi), _bf(s))
    a = []
    for h in range(RH):
        a.append(_dot(_bf(q * head[h]), kb, nt=True) * dec[h])
        y = y + head[h] * _dot(_bf(a[h]), vb)
    return a, y


def _gn_rows(y):
    mu = jnp.mean(y, axis=-1, keepdims=True)
    yc = y - mu
    rstd = lax.rsqrt(jnp.mean(yc * yc, axis=-1, keepdims=True) + LN_EPS)
    return yc * rstd, rstd


_RET_COL0 = _RET_Q = 5


def _ret_consts(tb):
    names = ("decay", "zeta", "xi", "cd", "head", "same", "avg")
    full = lambda a: pl.BlockSpec(a.shape, lambda i: (0,) * a.ndim)
    return [tb[n] for n in names], [full(tb[n]) for n in names]


def _ret_inputs(n, q_ref, k_ref, v_ref):
    rows = n * T + lax.broadcasted_iota(jnp.int32, (T, 1), 0)
    valid = rows >= PAD
    return q_ref[...], jnp.where(valid, k_ref[...] * (HD ** -0.5), 0.0), v_ref[...], valid


def ret_fwd(proj, gng, tb, name):
    def body(q_ref, k_ref, v_ref, g_ref, dec_ref, zeta_ref, xi_ref, cd_ref, head_ref, same_ref, avg_ref, gng_ref,
             y_ref, st_ref, s_scr):
        n = pl.program_id(0)

        @pl.when(n == 0)
        def _():
            s_scr[...] = jnp.zeros_like(s_scr)

        s = s_scr[...]
        st_ref[...] = s
        q, kf, v, _ = _ret_inputs(n, q_ref, k_ref, v_ref)
        _, y = _ret_chunk(q, kf, v, s, (dec_ref, xi_ref[...], head_ref))
        s_scr[...] = cd_ref[...] * s + same_ref[...] * _dot(_bf((kf * zeta_ref[...]).T), _bf(v))
        avg = avg_ref[...]
        yc = y - _seg_mean(y, avg)
        yh = yc * lax.rsqrt(_seg_mean(yc * yc, avg) + LN_EPS)
        gv = g_ref[...]
        y_ref[...] = (gv * _sig(gv) * (yh * gng_ref[...])).astype(y_ref.dtype)

    col = _ret_specs(_RET_COL0, lambda i: i)
    consts, cspecs = _ret_consts(tb)
    return pl.pallas_call(
        body, name=name,
        out_shape=(jax.ShapeDtypeStruct((L, RW), BF16), jax.ShapeDtypeStruct((NB, RW, RW), F32)),
        grid=(NB,),
        in_specs=[col(0), col(1), col(2), col(3)] + cspecs + [pl.BlockSpec((1, RW), lambda i: (0, 0))],
        out_specs=(pl.BlockSpec((T, RW), lambda i: (i, 0)), pl.BlockSpec((None, RW, RW), lambda i: (i, 0, 0))),
        scratch_shapes=[pltpu.VMEM((RW, RW), F32)])(proj, proj, proj, proj, *consts, gng)


def ret_bwd(proj, dycat, states, gng, tb, name):
    def body(q_ref, k_ref, v_ref, g_ref, do_ref, st_ref, dec_ref, zeta_ref, xi_ref, cd_ref, head_ref, same_ref,
             avg_ref, gng_ref, d_ref, dgn_ref, ds_scr):
        i = pl.program_id(0)
        n = NB - 1 - i

        @pl.when(i == 0)
        def _():
            ds_scr[...] = jnp.zeros_like(ds_scr)

        dsn = _bf(ds_scr[...])
        s = st_ref[...]
        sb = _bf(s)
        q, kf, v, valid = _ret_inputs(n, q_ref, k_ref, v_ref)
        xi, zeta, avg = xi_ref[...], zeta_ref[...], avg_ref[...]
        a, y = _ret_chunk(q, kf, v, s, (dec_ref, xi, head_ref))
        yc = y - _seg_mean(y, avg)
        rstd = lax.rsqrt(_seg_mean(yc * yc, avg) + LN_EPS)
        yh = yc * rstd
        gv = g_ref[...]
        sg = _sig(gv)
        sil = gv * sg
        gn = gng_ref[...]
        dout = do_ref[...]
        d_ref[:, 3 * RW:4 * RW] = (dout * (yh * gn) * (sg * (1.0 + gv * (1.0 - sg)))).astype(d_ref.dtype)
        dyh = dout * sil * gn
        part = jnp.sum(dout * sil * yh, axis=0, keepdims=True)

        @pl.when(i == 0)
        def _():
            dgn_ref[...] = part

        @pl.when(i > 0)
        def _():
            dgn_ref[...] += part

        dy = rstd * (dyh - _seg_mean(dyh, avg) - yh * _seg_mean(dyh * yh, avg))
        dyb = _bf(dy)
        vb, kb, qb = _bf(v), _bf(kf), _bf(q)
        dq = _dot(dyb, sb, nt=True) * xi
        dkf = _dot(vb, dsn, nt=True) * zeta
        dv = _dot(_bf(kf * zeta), dsn)
        for h in range(RH):
            m = head_ref[h]
            da = _dot(_bf(dy * m), vb, nt=True) * dec_ref[h]
            dv = dv + m * _dot(_bf(a[h].T), dyb)
            dq = dq + m * _dot(_bf(da), kb)
            dkf = dkf + m * _dot(_bf(da.T), qb)
        d_ref[:, 0:RW] = dq.astype(d_ref.dtype)
        d_ref[:, RW:2 * RW] = jnp.where(valid, dkf * (HD ** -0.5), 0.0).astype(d_ref.dtype)
        d_ref[:, 2 * RW:3 * RW] = dv.astype(d_ref.dtype)
        ds_scr[...] = cd_ref[...] * ds_scr[...] + same_ref[...] * _dot(_bf((q * xi).T), dyb)

    back = lambda i: NB - 1 - i
    col = _ret_specs(_RET_COL0, back)
    consts, cspecs = _ret_consts(tb)
    vec = pl.BlockSpec((1, RW), lambda i: (0, 0))
    return pl.pallas_call(
        body, name=name,
        out_shape=(jax.ShapeDtypeStruct((L, 4 * RW), BF16), jax.ShapeDtypeStruct((1, RW), F32)),
        grid=(NB,),
        in_specs=[col(0), col(1), col(2), col(3), _ret_specs(3, back)(0),
                  pl.BlockSpec((None, RW, RW), lambda i: (back(i), 0, 0))] + cspecs + [vec],
        out_specs=(pl.BlockSpec((T, 4 * RW), lambda i: (back(i), 0)), vec),
        scratch_shapes=[pltpu.VMEM((RW, RW), F32)])(proj, proj, proj, proj, dycat, states, *consts, gng)


HALO = 32
TAP0 = HALO - (CONV_K - 1)


def _conv_specs():
    cur = lambda col=0: pl.BlockSpec((T, CV), lambda c: (c, col))
    before = lambda col=0: pl.BlockSpec((HALO, CV), lambda c: (jnp.maximum(c * (T // HALO) - 1, 0), col))
    after = pl.BlockSpec((HALO, CV), lambda c: (jnp.minimum((c + 1) * (T // HALO), L // HALO - 1), 0))
    full = lambda r, w: pl.BlockSpec((r, w), lambda c: (0, 0))
    return cur, before, after, full


_CONV_A, _CONV_B = 3, 4
_DY_CONV = 2


def _conv_post(xc, lg_ref, lb_ref):
    xh, rstd = _gn_rows(xc)
    z = xh * lg_ref[...] + lb_ref[...]
    return xh, rstd, z, _sig(z)


def conv_fwd(proj, w, b, lg, lb, pw, og, name):
    def body(ca_ref, cah_ref, cb_ref, cbh_ref, w_ref, b_ref, lg_ref, lb_ref, pw_ref, og_ref, y_ref, xc_ref, u_scr):
        c = pl.program_id(0)
        u_scr[0:HALO, :] = jnp.where(c > 0, cah_ref[...] * _sig(cbh_ref[...]), 0.0)
        u_scr[HALO:HALO + T, :] = ca_ref[...] * _sig(cb_ref[...])
        acc = jnp.zeros((T, CV), F32)
        for k in range(CONV_K):
            acc = acc + w_ref[k:k + 1, :] * u_scr[TAP0 + k:TAP0 + k + T, :]
        xc = acc + b_ref[...]
        xc_ref[...] = xc
        _, _, z, sg = _conv_post(xc, lg_ref, lb_ref)
        zp = _dot(_bf(z * sg), pw_ref[...])
        y_ref[...] = _rms_rows(zp, og_ref[...])[0].astype(y_ref.dtype)

    cur, before, _, full = _conv_specs()
    vec = full(1, CV)
    seq = jax.ShapeDtypeStruct((L, CV), F32)
    return pl.pallas_call(
        body, name=name, out_shape=(jax.ShapeDtypeStruct((L, CV), BF16), seq), grid=(NB,),
        in_specs=[cur(_CONV_A), before(_CONV_A), cur(_CONV_B), before(_CONV_B), full(CONV_K, CV), vec, vec, vec,
                  full(CV, CV), vec],
        out_specs=(cur(), cur()),
        scratch_shapes=[pltpu.VMEM((HALO + T, CV), F32)])(proj, proj, proj, proj, w, b, lg, lb, pw, og)


def conv_bwd(proj, xc, dycat, w, lg, lb, pw, og, name):
    def body1(xc_ref, dy_ref, lg_ref, lb_ref, pw_ref, og_ref, dxc_ref, db_ref, dlg_ref, dlb_ref, dog_ref, dpw_ref):
        c = pl.program_id(0)

        @pl.when(c == 0)
        def _():
            for r in (db_ref, dlg_ref, dlb_ref, dog_ref, dpw_ref):
                r[...] = jnp.zeros_like(r)

        xh, rstd, z, sg = _conv_post(xc_ref[...], lg_ref, lb_ref)
        s = z * sg
        zp = _dot(_bf(s), pw_ref[...])
        ogv = og_ref[...]
        _, zph, r2 = _rms_rows(zp, ogv)
        dyv = dy_ref[...]
        dog_ref[...] += jnp.sum(dyv * zph, axis=0, keepdims=True)
        dzpb = _bf(_rms_rows_bwd(dyv, ogv, zph, r2))
        dpw_ref[...] += _dot(_bf(s.T), dzpb)
        dz = _dot(dzpb, pw_ref[...], nt=True) * (sg * (1.0 + z * (1.0 - sg)))
        dlg_ref[...] += jnp.sum(dz * xh, axis=0, keepdims=True)
        dlb_ref[...] += jnp.sum(dz, axis=0, keepdims=True)
        dxh = dz * lg_ref[...]
        dxc = rstd * (dxh - jnp.mean(dxh, axis=-1, keepdims=True) - xh * jnp.mean(dxh * xh, axis=-1, keepdims=True))
        db_ref[...] += jnp.sum(dxc, axis=0, keepdims=True)
        dxc_ref[...] = dxc

    def body2(dx_ref, dxa_ref, ca_ref, cb_ref, w_ref, dca_ref, dcb_ref, dw_ref, d_scr):
        c = pl.program_id(0)

        @pl.when(c == 0)
        def _():
            dw_ref[...] = jnp.zeros_like(dw_ref)

        d_scr[0:T, :] = dx_ref[...]
        d_scr[T:T + HALO, :] = jnp.where(c < NB - 1, dxa_ref[...], 0.0)
        ca = ca_ref[...]
        sg = _sig(cb_ref[...])
        u = ca * sg
        du = jnp.zeros((T, CV), F32)
        for k in range(CONV_K):
            off = CONV_K - 1 - k
            dj = d_scr[off:off + T, :]
            du = du + w_ref[k:k + 1, :] * dj
            dw_ref[k:k + 1, :] += jnp.sum(u * dj, axis=0, keepdims=True)
        dca_ref[...] = (du * sg).astype(dca_ref.dtype)
        dcb_ref[...] = (du * ca * sg * (1.0 - sg)).astype(dcb_ref.dtype)

    cur, _, after, full = _conv_specs()
    seq = jax.ShapeDtypeStruct((L, CV), F32)
    vsh = jax.ShapeDtypeStruct((1, CV), F32)
    vec = full(1, CV)
    dxc, db, dlg, dlb, dog, dpw = pl.pallas_call(
        body1, name=name + "_a",
        out_shape=(seq, vsh, vsh, vsh, vsh, jax.ShapeDtypeStruct((CV, CV), F32)),
        grid=(NB,),
        in_specs=[cur(), cur(_DY_CONV), vec, vec, full(CV, CV), vec],
        out_specs=(cur(), vec, vec, vec, vec, full(CV, CV)))(xc, dycat, lg, lb, pw, og)
    dca, dcb, dw = pl.pallas_call(
        body2, name=name + "_b", grid=(NB,),
        out_shape=(jax.ShapeDtypeStruct((L, CV), BF16), jax.ShapeDtypeStruct((L, CV), BF16),
                   jax.ShapeDtypeStruct((CONV_K, CV), F32)),
        in_specs=[cur(), after, cur(_CONV_A), cur(_CONV_B), full(CONV_K, CV)],
        out_specs=(cur(), cur(), full(CONV_K, CV)),
        scratch_shapes=[pltpu.VMEM((T + HALO, CV), F32)])(dxc, dxc, proj, proj, w)
    return dca, dcb, dw, db, dlg, dlb, dog, dpw


HR = 8


def _ffn_specs():
    cur = lambda off: pl.BlockSpec((None, T, UPW), lambda j, c: (j + off, c, 0))
    before = lambda off: pl.BlockSpec((None, HR, UPW), lambda j, c: (j + off, jnp.maximum(c * (T // HR) - 1, 0), 0))
    after = lambda off: pl.BlockSpec(
        (None, HR, UPW), lambda j, c: (j + off, jnp.minimum((c + 1) * (T // HR), L // HR - 1), 0))
    wspec = lambda off, r: pl.BlockSpec((None, r, UPW), lambda j, c: (j + off, 0, 0))
    return cur, before, after, wspec


HR16 = 16


def _ffn_after16():
    return pl.BlockSpec((None, HR16, UPW), lambda j, c: (j, jnp.minimum((c + 1) * (T // HR16), L // HR16 - 1), 0))


def ffn_act_fwd(f, w, b, name):
    strip = 16

    def body(fg_ref, fgh_ref, fu_ref, fuh_ref, wg_ref, wu_ref, bg_ref, bu_ref, a_ref, gc_ref, uc_ref, hg_scr, hu_scr):
        c = pl.program_id(1)
        for scr, x_ref, xh_ref in ((hg_scr, fg_ref, fgh_ref), (hu_scr, fu_ref, fuh_ref)):
            scr[0:HR, :] = jnp.where(c > 0, xh_ref[...], 0.0)
            scr[HR:HR + strip, :] = x_ref[0:strip, :]
        wg = [wg_ref[k:k + 1, :] for k in range(FFN_K)]
        wu = [wu_ref[k:k + 1, :] for k in range(FFN_K)]
        bg, bu = bg_ref[...], bu_ref[...]

        def conv(src, base, w, b):
            acc = b
            for k in range(FFN_K):
                o = base - (FFN_K - 1) + k
                acc = acc + w[k] * src[o:o + strip, :]
            return acc

        for s in range(T // strip):
            r = s * strip
            gate = conv(hg_scr, HR, wg, bg) if s == 0 else conv(fg_ref, r, wg, bg)
            up = conv(hu_scr, HR, wu, bu) if s == 0 else conv(fu_ref, r, wu, bu)
            gc_ref[r:r + strip, :] = gate.astype(BF16)
            uc_ref[r:r + strip, :] = up.astype(BF16)
            a_ref[r:r + strip, :] = (gate * _sig(gate) * up).astype(BF16)

    cur, before, _, wspec = _ffn_specs()
    pre = jax.ShapeDtypeStruct((NPAIR, L, UPW), BF16)
    return pl.pallas_call(
        body, name=name, out_shape=(jax.ShapeDtypeStruct((NPAIR, L, UPW), BF16), pre, pre), grid=(NPAIR, NB),
        in_specs=[cur(0), before(0), cur(NPAIR), before(NPAIR), wspec(0, FFN_K), wspec(NPAIR, FFN_K),
                  wspec(0, 1), wspec(NPAIR, 1)],
        out_specs=(cur(0), cur(0), cur(0)),
        scratch_shapes=[pltpu.VMEM((HR + strip, UPW), F32), pltpu.VMEM((HR + strip, UPW), F32)])(
            f, f, f, f, w, w, b, b)


def ffn_act_bwd(f, gc, uc, da, w, name):
    ext = T + HR
    sub = 8

    def body(fg_ref, fu_ref, gc_ref, gca_ref, uc_ref, uca_ref, da_ref, daa_ref, wg_ref, wu_ref,
             dfg_ref, dfu_ref, dwg_ref, dwu_ref, dbg_ref, dbu_ref, dg_scr, du_scr):
        c = pl.program_id(1)
        wg = [wg_ref[k:k + 1, :] for k in range(FFN_K)]
        wu = [wu_ref[k:k + 1, :] for k in range(FFN_K)]

        for r in range(0, ext, HR16):
            rows = min(HR16, ext - r)
            if r < T:
                gate, up, dav = gc_ref[r:r + HR16, :].astype(F32), uc_ref[r:r + HR16, :].astype(F32), da_ref[r:r + HR16, :]
            else:
                gate, up = gca_ref[...].astype(F32)[0:rows], uca_ref[...].astype(F32)[0:rows]
                dav = jnp.where(c < NB - 1, daa_ref[...], 0.0)
            sg = _sig(gate)
            dg_scr[r:r + rows, :] = dav * up * (sg * (1.0 + gate * (1.0 - sg)))
            du_scr[r:r + rows, :] = dav * gate * sg

        def back(d_scr, x_ref, w, df_ref, dw_ref, db_ref):
            accs = [jnp.zeros((sub, UPW), F32) for _ in range(FFN_K + 1)]
            for s in range(T // (2 * sub)):
                pieces = []
                for r in (2 * s * sub, (2 * s + 1) * sub):
                    xv = x_ref[r:r + sub, :]
                    df = jnp.zeros((sub, UPW), F32)
                    for k in range(FFN_K):
                        off = FFN_K - 1 - k
                        dj = d_scr[r + off:r + off + sub, :]
                        df = df + w[k] * dj
                        accs[k] = accs[k] + xv * dj
                        if off == 0:
                            accs[FFN_K] = accs[FFN_K] + dj
                    pieces.append(df)
                df_ref[2 * s * sub:2 * (s + 1) * sub, :] = jnp.concatenate(pieces, axis=0).astype(BF16)
            dwp = _stack_rows([jnp.sum(a, axis=0, keepdims=True) for a in accs[:FFN_K]])
            dbp = jnp.sum(accs[FFN_K], axis=0, keepdims=True)

            @pl.when(c == 0)
            def _():
                dw_ref[...] = dwp
                db_ref[...] = dbp

            @pl.when(c > 0)
            def _():
                dw_ref[...] += dwp
                db_ref[...] += dbp

        back(dg_scr, fg_ref, wg, dfg_ref, dwg_ref, dbg_ref)
        back(du_scr, fu_ref, wu, dfu_ref, dwu_ref, dbu_ref)

    cur, _, after, wspec = _ffn_specs()
    slab = jax.ShapeDtypeStruct((NPAIR, L, UPW), BF16)
    wsh = jax.ShapeDtypeStruct((NPAIR, FFN_K, UPW), F32)
    bsh = jax.ShapeDtypeStruct((NPAIR, 1, UPW), F32)
    return pl.pallas_call(
        body, name=name, out_shape=(slab, slab, wsh, wsh, bsh, bsh), grid=(NPAIR, NB),
        in_specs=[cur(0), cur(NPAIR), cur(0), _ffn_after16(), cur(0), _ffn_after16(), cur(0), after(0),
                  wspec(0, FFN_K), wspec(NPAIR, FFN_K)],
        out_specs=(cur(0), cur(0), wspec(0, FFN_K), wspec(0, FFN_K), wspec(0, 1), wspec(0, 1)),
        scratch_shapes=[pltpu.VMEM((ext, UPW), F32), pltpu.VMEM((ext, UPW), F32)])(
            f, f, gc, gc, uc, uc, da, da, w, w)


def _mesh_pos():
    return lax.axis_index("x"), lax.axis_index("y"), lax.axis_index("c")


def _peer(pos, k):
    x, y, c = pos
    px = 1 - x if k & 4 else x
    py = 1 - y if k & 2 else y
    pc = 1 - c if k & 1 else c
    return (px, py, pc), 4 * px + 2 * py + pc


_CHIP_FLIPS = (4, 2, 6)
_HBM = pl.BlockSpec(memory_space=pl.ANY)


def all_gather(shards, name):
    nt = len(shards)

    def body(*refs):
        ins, outs = refs[:nt], refs[nt:2 * nt]
        send, recv, local = refs[2 * nt:]
        pos = _mesh_pos()
        me = 4 * pos[0] + 2 * pos[1] + pos[2]
        sib, sib_id = _peer(pos, 1)

        def copy(t, k, block_id, to, src=None):
            dst = outs[t].at[block_id]
            return pltpu.make_async_remote_copy(
                src_ref=dst if src is None else src, dst_ref=dst, send_sem=send.at[t, k], recv_sem=recv.at[t, k],
                device_id=to, device_id_type=MESH)

        locals_ = [pltpu.make_async_copy(ins[t], outs[t].at[me], local.at[t]) for t in range(nt)]
        for cp in locals_:
            cp.start()
        started = []
        for j, flip in enumerate(_CHIP_FLIPS):
            for t in range(nt):
                started.append(copy(t, 1 + j, me, _peer(pos, flip)[0], src=ins[t]))
        for t in range(nt):
            started.append(copy(t, 0, me, sib, src=ins[t]))
        for cp in started:
            cp.start()
        for j, flip in enumerate(_CHIP_FLIPS):
            _, pid = _peer(pos, flip)
            for t in range(nt):
                copy(t, 1 + j, pid, sib).wait_recv()
                fwd = copy(t, 4 + j, pid, sib)
                fwd.start()
                started.append(fwd)
        for t in range(nt):
            copy(t, 0, sib_id, sib).wait_recv()
        for j, flip in enumerate(_CHIP_FLIPS):
            _, pid = _peer(pos, flip | 1)
            for t in range(nt):
                copy(t, 4 + j, pid, sib).wait_recv()
        for cp in started:
            cp.wait_send()
        for cp in locals_:
            cp.wait()

    return pl.pallas_call(
        body, name=name,
        out_shape=tuple(jax.ShapeDtypeStruct((NDEV,) + s.shape, s.dtype) for s in shards),
        in_specs=[_HBM] * nt, out_specs=tuple([_HBM] * nt),
        scratch_shapes=[pltpu.SemaphoreType.DMA((nt, NDEV - 1)), pltpu.SemaphoreType.DMA((nt, NDEV - 1)),
                        pltpu.SemaphoreType.DMA((nt,))])(*shards)


_SPLIT = dict(has_side_effects=pltpu.SideEffectType.DATAFLOW_SIDE_EFFECTING)
_SEM = pl.BlockSpec(memory_space=pltpu.SEMAPHORE)


def _exchange_start(name, srcs, lands, copies, after):
    ns, nl = len(srcs), len(lands)
    ncopy = len(copies(None, [None] * ns, [None] * nl))

    def body(*refs):
        src_refs, land_refs = refs[:ns], refs[ns:ns + nl]
        send, recv = refs[ns + nl + len(after):ns + nl + len(after) + 2]
        token = refs[-1]
        for i, (src, dst, peer) in enumerate(copies(_mesh_pos(), src_refs, land_refs)):
            pltpu.make_async_remote_copy(src_ref=src, dst_ref=dst, send_sem=send.at[i], recv_sem=recv.at[i],
                                         device_id=peer, device_id_type=MESH).start()
        token[...] = jnp.zeros_like(token)

    hbm = lambda a: pltpu.HBM(a.shape, a.dtype)
    out = pl.pallas_call(
        body, name=name,
        out_shape=(pltpu.SemaphoreType.DMA((ncopy,)), pltpu.SemaphoreType.DMA((ncopy,)),
                   *[hbm(a) for a in srcs], *[hbm(a) for a in lands], jax.ShapeDtypeStruct((8, 128), F32)),
        in_specs=[pl.BlockSpec(memory_space=pltpu.HBM)] * (ns + nl) + [_HBM] * len(after),
        out_specs=(_SEM, _SEM, *[pl.BlockSpec(memory_space=pltpu.HBM)] * (ns + nl),
                   pl.BlockSpec(memory_space=pltpu.VMEM)),
        input_output_aliases={i: 2 + i for i in range(ns + nl)},
        compiler_params=pltpu.CompilerParams(**_SPLIT))(
            *[pltpu.with_memory_space_constraint(a, pltpu.HBM) for a in list(srcs) + list(lands)], *after)
    return out[0], out[1], list(out[2:2 + ns]), list(out[2 + ns:2 + ns + nl]), out[-1]


def _exchange_wait(name, send, recv, srcs, lands, copies, after):
    ns, nl = len(srcs), len(lands)

    def body(*refs):
        src_refs, land_refs = refs[:ns], refs[ns:ns + nl]
        send_ref, recv_ref = refs[ns + nl:ns + nl + 2]
        pos = _mesh_pos()
        for i, (src, dst, peer) in enumerate(copies(pos, src_refs, land_refs, arriving=True)):
            cp = pltpu.make_async_remote_copy(src_ref=src, dst_ref=dst, send_sem=send_ref.at[i], recv_sem=recv_ref.at[i],
                                              device_id=peer, device_id_type=MESH)
            cp.wait_send()
            cp.wait_recv()

    hbm = lambda a: pltpu.HBM(a.shape, a.dtype)
    out = pl.pallas_call(
        body, name=name, out_shape=tuple(hbm(a) for a in list(srcs) + list(lands)),
        in_specs=[pl.BlockSpec(memory_space=pltpu.HBM)] * (ns + nl) + [_SEM, _SEM] + [_HBM] * len(after),
        out_specs=tuple([pl.BlockSpec(memory_space=pltpu.HBM)] * (ns + nl)),
        input_output_aliases={i: i for i in range(ns + nl)},
        compiler_params=pltpu.CompilerParams(**_SPLIT))(*srcs, *lands, send, recv, *after)
    return list(out[:ns]), list(out[ns:])


def _gather_copies(pos, srcs, lands, arriving=False):
    if pos is None:
        return [None] * (len(srcs) * (NDEV - 1))
    me = 4 * pos[0] + 2 * pos[1] + pos[2]
    out = []
    for src, land in zip(srcs, lands):
        for k in range(1, NDEV):
            peer, pid = _peer(pos, k)
            out.append((src, land.at[pid if arriving else me], peer))
    return out


def _scatter_copies(pos, srcs, lands, arriving=False):
    if pos is None:
        return [None] * (len(srcs) * (NDEV - 1))
    out = []
    for src, land in zip(srcs, lands):
        for k in range(1, NDEV):
            peer, pid = _peer(pos, k)
            out.append((src.at[pid], land.at[k - 1], peer))
    return out


def gather_start(name, shards, lands, after=()):
    return _exchange_start(name, shards, lands, _gather_copies, list(after))


def gather_wait(name, handle, after=()):
    send, recv, srcs, lands, _ = handle
    return _exchange_wait(name, send, recv, srcs, lands, _gather_copies, list(after))


def scatter_start(name, grads, after=()):
    lands = [pltpu.with_memory_space_constraint(lax.empty((NDEV - 1,) + g.shape[1:], g.dtype), pltpu.HBM)
             for g in grads]
    return _exchange_start(name, grads, lands, _scatter_copies, list(after))


def scatter_wait(name, handle, after=()):
    send, recv, srcs, lands, _ = handle
    return _exchange_wait(name, send, recv, srcs, lands, _scatter_copies, list(after))


def sum_slots(parts, name):
    def body(p_ref, o_ref):
        acc = p_ref[0]
        for d in range(1, NDEV):
            acc = acc + p_ref[d]
        o_ref[...] = acc

    return pl.pallas_call(body, name=name, out_shape=jax.ShapeDtypeStruct(parts.shape[1:], F32))(parts)


def _adamw_update(g, w_ref, m_ref, v_ref, g_ref, d_ref, nm_ref, nv_ref):
    g_ref[...] = g
    nm = ADAM_B1 * m_ref[...] + (1.0 - ADAM_B1) * g
    nv = ADAM_B2 * v_ref[...] + (1.0 - ADAM_B2) * (g * g)
    nm_ref[...] = nm
    nv_ref[...] = nv
    m_hat = nm / (1.0 - ADAM_B1 ** ADAM_STEP)
    v_hat = nv / (1.0 - ADAM_B2 ** ADAM_STEP)
    d_ref[...] = -ADAM_LR * (m_hat / (jnp.sqrt(v_hat) + ADAM_EPS) + ADAM_WD * w_ref[...])


def adamw_big(own, lands, w, m, v, me, tr, name):
    _, r, c = own[0].shape
    nt = r // tr

    def body(me_ref, *refs):
        ins, (w_ref, m_ref, v_ref), outs = refs[:2 * DEPTH], refs[2 * DEPTH:2 * DEPTH + 3], refs[2 * DEPTH + 3:]
        layer = pl.program_id(0)
        for l in range(DEPTH):
            @pl.when(layer == l)
            def _(l=l):
                g = ins[2 * l][...].astype(F32)
                for s in range(NDEV - 1):
                    g = g + ins[2 * l + 1][s].astype(F32)
                _adamw_update(g, w_ref, m_ref, v_ref, *outs)

    pick = lambda l: (lambda layer, i, me_ref: jnp.where(layer == l, i, 0))
    in_specs = []
    for l in range(DEPTH):
        in_specs.append(pl.BlockSpec((None, tr, c), lambda layer, i, me_ref, f=pick(l): (me_ref[0], f(layer, i, me_ref), 0)))
        in_specs.append(pl.BlockSpec((NDEV - 1, tr, c), lambda layer, i, me_ref, f=pick(l): (0, f(layer, i, me_ref), 0)))
    blk = pl.BlockSpec((tr, c), lambda layer, i, me_ref: (layer * nt + i, 0))
    sh = jax.ShapeDtypeStruct((DEPTH * r, c), F32)
    args = [a for l in range(DEPTH) for a in (own[l], lands[l])]
    return pl.pallas_call(
        body, name=name, out_shape=(sh, sh, sh, sh),
        grid_spec=pltpu.PrefetchScalarGridSpec(
            num_scalar_prefetch=1, grid=(DEPTH, nt), in_specs=in_specs + [blk, blk, blk],
            out_specs=(blk, blk, blk, blk)),
        compiler_params=_params(VMEM_BIG))(me, *args, w, m, v)


def adamw_many(gs, ws, ms, vs, name):
    n = len(gs)

    def body(*refs):
        outs = refs[4 * n:]
        for i in range(n):
            _adamw_update(refs[i][...], refs[n + i], refs[2 * n + i], refs[3 * n + i], *outs[4 * i:4 * i + 4])

    out_shape = tuple(jax.ShapeDtypeStruct(w.shape, F32) for w in ws for _ in range(4))
    res = pl.pallas_call(body, name=name, out_shape=out_shape)(*gs, *ws, *ms, *vs)
    return [res[4 * i:4 * i + 4] for i in range(n)]


def local_step(x, target, P, hooks):
    tb = _ret_tables()
    h = jnp.concatenate([jnp.zeros((PAD, D), F32), P["meta"], x], axis=0)
    stash = []
    for l in range(DEPTH):
        s = {"h": h}
        s["w_in"], s["cv_pw"], s["w_out"] = hooks["mixer_w"](l, h)
        proj, s["u"] = in_proj_fwd(f"in_proj_{l}", h, P["norm_mix_g"][l], s["w_in"], 256,
                                   after=hooks["first_after"] if l == 0 else None)
        s["proj"] = proj
        s["o"] = attn_fwd(proj, P["q_norm_g"][l], P["k_norm_g"][l], P["attn_sinks"][l], f"attn_fwd_{l}")
        y_cv, s["xc"] = conv_fwd(proj, P["cv_dw_w"][l], P["cv_dw_b"][l], P["cv_ln_g"][l], P["cv_ln_b"][l],
                                 s["cv_pw"], P["cv_out_g"][l], f"conv_fwd_{l}")
        y_ret, s["states"] = ret_fwd(proj, P["ret_gn_g"][l], tb, f"ret_fwd_{l}")
        h, s["ycat"] = out_proj_fwd(f"out_proj_{l}", s["o"], P["attn_out_g"][l], y_cv, y_ret, s["w_out"], h, 512)
        s["h1"] = h
        s["ffn_up"], s["ffn_down"] = hooks["ffn_w"](l, h)
        s["f"], s["u2"] = ffn_up_fwd(f"ffn_up_{l}", h, P["norm_ffn_g"][l], s["ffn_up"])
        a, s["gc"], s["uc"] = ffn_act_fwd(s["f"], P["ffn_dw_w"][l], P["ffn_dw_b"][l], f"ffn_act_fwd_{l}")
        s["a"] = a
        if l < DEPTH - 1:
            h = ffn_down_fwd(f"ffn_down_{l}", a, s["ffn_down"], h)
        else:
            dh, dhm, loss = ffn_down_loss(f"ffn_down_{l}", a, s["ffn_down"], h, target)
        stash.append(s)

    G = {n: [None] * DEPTH for n in ("norm_mix_g", "q_norm_g", "k_norm_g", "attn_sinks", "attn_out_g", "cv_dw_w",
                                     "cv_dw_b", "cv_ln_g", "cv_ln_b", "cv_out_g", "ret_gn_g", "norm_ffn_g",
                                     "ffn_dw_w", "ffn_dw_b")}
    for l in reversed(range(DEPTH)):
        s = stash[l]
        da = ffn_down_dx(f"ffn_down_dx_{l}", dhm, s["ffn_down"])
        dw_down = ffn_down_dw(f"ffn_down_dw_{l}", s["a"], dhm)
        dfg, dfu, dwg, dwu, dbg, dbu = ffn_act_bwd(s["f"], s["gc"], s["uc"], da, P["ffn_dw_w"][l], f"ffn_act_bwd_{l}")
        G["ffn_dw_w"][l] = jnp.concatenate([dwg, dwu], axis=0)
        G["ffn_dw_b"][l] = jnp.concatenate([dbg, dbu], axis=0)
        du2 = ffn_up_dx(f"ffn_up_dx_{l}", dfg, dfu, s["ffn_up"])
        dw_up = ffn_up_dw(f"ffn_up_dw_{l}", s["u2"], dfg, dfu)
        after = hooks["ffn_grads"](l, dw_down, dw_up)
        dh, dhm, G["norm_ffn_g"][l] = rms_bwd(s["h1"], P["norm_ffn_g"][l], du2, dh, f"rms_ffn_bwd_{l}", after=after)
        dycat = mm_nt(f"out_proj_dx_{l}", dhm, s["w_out"], 512)
        dw_out = mm_ta(f"out_proj_dw_{l}", s["ycat"], dhm, 512)
        (dq, dk_c, dk_p, dk_m, dv_c, dv_p, dv_m, G["q_norm_g"][l], G["k_norm_g"][l], dsk,
         G["attn_out_g"][l]) = attn_bwd(s["proj"], s["o"], dycat, P["attn_out_g"][l], P["q_norm_g"][l],
                                        P["k_norm_g"][l], P["attn_sinks"][l], f"attn_bwd_{l}")
        G["attn_sinks"][l] = dsk[:, 0]
        shift = lambda z: jnp.concatenate([z[T:], jnp.zeros((T, NKV * HD), F32)], axis=0)
        dk = (dk_c + shift(dk_p)).at[:T].add(dk_m)
        dv = (dv_c + shift(dv_p)).at[:T].add(dv_m)
        (dca, dcb, G["cv_dw_w"][l], G["cv_dw_b"][l], G["cv_ln_g"][l], G["cv_ln_b"][l], G["cv_out_g"][l],
         dpw) = conv_bwd(s["proj"], s["xc"], dycat, P["cv_dw_w"][l], P["cv_ln_g"][l], P["cv_ln_b"][l], s["cv_pw"],
                         P["cv_out_g"][l], f"conv_bwd_{l}")
        dret, G["ret_gn_g"][l] = ret_bwd(s["proj"], dycat, s["states"], P["ret_gn_g"][l], tb, f"ret_bwd_{l}")
        dproj = jnp.concatenate([dq, dk.astype(BF16), dv.astype(BF16), dca, dcb, dret], axis=1)
        du = mm_nn(f"in_proj_dx_{l}", dproj, s["w_in"], 512)
        dw_in = mm_ta(f"in_proj_dw_{l}", dproj, s["u"], 768)
        after = hooks["mixer_grads"](l, dw_out, dpw, dw_in)
        dh, dhm, G["norm_mix_g"][l] = rms_bwd(s["h"], P["norm_mix_g"][l], du, dh, f"rms_mix_bwd_{l}", after=after,
                                              last=(l == 0))
    return loss[0, 0], dh, dhm[PAD:T], G


_SMALL = ("meta", "norm_mix_g", "q_norm_g", "k_norm_g", "attn_sinks", "attn_out_g", "cv_dw_w", "cv_dw_b", "cv_ln_g",
          "cv_ln_b", "cv_out_g", "ret_gn_g", "norm_ffn_g", "ffn_dw_w", "ffn_dw_b")
_BIG = ("w_in", "cv_pw", "w_out", "ffn_up", "ffn_down")
_TRANSPOSED = ("w_in", "ffn_up")
_ORDER = ("meta", "norm_mix_g", "w_in", "q_norm_g", "k_norm_g", "attn_sinks", "attn_out_g", "cv_dw_w", "cv_dw_b",
          "cv_ln_g", "cv_ln_b", "cv_pw", "cv_out_g", "ret_gn_g", "w_out", "norm_ffn_g", "ffn_up", "ffn_dw_w",
          "ffn_dw_b", "ffn_down")
_SMALL_SHARDED = {"meta": D, "cv_dw_w": CV, "ffn_dw_w": 2 * D_FF}


def _pack(arrs):
    flat = jnp.concatenate([a.reshape(-1) for a in arrs])
    n = flat.shape[0]
    rows = -(-n // 1024) * 8
    return jnp.pad(flat, (0, rows * 128 - n)).reshape(rows, 128)


def _unpack(packed, shapes):
    flat = packed.reshape(-1)
    out, off = [], 0
    for s in shapes:
        n = int(np.prod(s))
        out.append(flat[off:off + n].reshape(s))
        off += n
    return out


def kernel(x, meta, norm_mix_g, w_in, q_norm_g, k_norm_g, attn_sinks, attn_out_g, cv_dw_w, cv_dw_b, cv_ln_g, cv_ln_b, cv_pw, cv_out_g, ret_gn_g, w_out, norm_ffn_g, ffn_up, ffn_dw_w, ffn_dw_b, ffn_down, loss_target, m_meta, m_norm_mix_g, m_w_in, m_q_norm_g, m_k_norm_g, m_attn_sinks, m_attn_out_g, m_cv_dw_w, m_cv_dw_b, m_cv_ln_g, m_cv_ln_b, m_cv_pw, m_cv_out_g, m_ret_gn_g, m_w_out, m_norm_ffn_g, m_ffn_up, m_ffn_dw_w, m_ffn_dw_b, m_ffn_down, v_meta, v_norm_mix_g, v_w_in, v_q_norm_g, v_k_norm_g, v_attn_sinks, v_attn_out_g, v_cv_dw_w, v_cv_dw_b, v_cv_ln_g, v_cv_ln_b, v_cv_pw, v_cv_out_g, v_ret_gn_g, v_w_out, v_norm_ffn_g, v_ffn_up, v_ffn_dw_w, v_ffn_dw_b, v_ffn_down):
    W = dict(meta=meta, norm_mix_g=norm_mix_g, w_in=w_in, q_norm_g=q_norm_g, k_norm_g=k_norm_g,
             attn_sinks=attn_sinks, attn_out_g=attn_out_g, cv_dw_w=cv_dw_w, cv_dw_b=cv_dw_b, cv_ln_g=cv_ln_g,
             cv_ln_b=cv_ln_b, cv_pw=cv_pw, cv_out_g=cv_out_g, ret_gn_g=ret_gn_g, w_out=w_out,
             norm_ffn_g=norm_ffn_g, ffn_up=ffn_up, ffn_dw_w=ffn_dw_w, ffn_dw_b=ffn_dw_b, ffn_down=ffn_down)
    M = dict(meta=m_meta, norm_mix_g=m_norm_mix_g, w_in=m_w_in, q_norm_g=m_q_norm_g, k_norm_g=m_k_norm_g,
             attn_sinks=m_attn_sinks, attn_out_g=m_attn_out_g, cv_dw_w=m_cv_dw_w, cv_dw_b=m_cv_dw_b,
             cv_ln_g=m_cv_ln_g, cv_ln_b=m_cv_ln_b, cv_pw=m_cv_pw, cv_out_g=m_cv_out_g, ret_gn_g=m_ret_gn_g,
             w_out=m_w_out, norm_ffn_g=m_norm_ffn_g, ffn_up=m_ffn_up, ffn_dw_w=m_ffn_dw_w, ffn_dw_b=m_ffn_dw_b,
             ffn_down=m_ffn_down)
    V = dict(meta=v_meta, norm_mix_g=v_norm_mix_g, w_in=v_w_in, q_norm_g=v_q_norm_g, k_norm_g=v_k_norm_g,
             attn_sinks=v_attn_sinks, attn_out_g=v_attn_out_g, cv_dw_w=v_cv_dw_w, cv_dw_b=v_cv_dw_b,
             cv_ln_g=v_cv_ln_g, cv_ln_b=v_cv_ln_b, cv_pw=v_cv_pw, cv_out_g=v_cv_out_g, ret_gn_g=v_ret_gn_g,
             w_out=v_w_out, norm_ffn_g=v_norm_ffn_g, ffn_up=v_ffn_up, ffn_dw_w=v_ffn_dw_w, ffn_dw_b=v_ffn_dw_b,
             ffn_down=v_ffn_down)
    me = 4 * lax.axis_index("x") + 2 * lax.axis_index("y") + lax.axis_index("c")
    for n in _TRANSPOSED:
        W[n], M[n], V[n] = (a.transpose(0, 2, 1) for a in (W[n], M[n], V[n]))

    sh = {n: [W[n][l].astype(BF16) for l in range(DEPTH)] for n in _BIG}
    mix = lambda l: [sh["w_in"][l], sh["cv_pw"][l], sh["w_out"][l]]
    ffn = lambda l: [sh["ffn_up"][l], sh["ffn_down"][l]]
    first = all_gather(mix(0) + [meta, cv_dw_w, ffn_dw_w], "gather_first")
    g_meta, g_cdw, g_fdw = first[3:6]

    def landing(shards):
        return [lax.dynamic_update_slice(lax.empty((NDEV,) + s.shape, s.dtype), s[None], (me,) + (0,) * s.ndim)
                for s in shards]

    gathers = {("ffn", 0): gather_start("gather_ffn0_start", ffn(0), landing(ffn(0)), after=[first[0]])}
    gathers["mix", 1] = gather_start("gather_mix1_start", mix(1), landing(mix(1)), after=[gathers["ffn", 0][4]])
    gathers["ffn", 1] = gather_start("gather_ffn1_start", ffn(1), landing(ffn(1)), after=[gathers["mix", 1][4]])

    def mixer_w(l, h):
        g_in, g_pw, g_out = first[0:3] if l == 0 else gather_wait(f"gather_mix{l}_wait", gathers["mix", l], after=[h])[1]
        return g_in.reshape(IN_W, D), g_pw.reshape(CV, CV), g_out.reshape(D, D)

    def ffn_w(l, h1):
        return gather_wait(f"gather_ffn{l}_wait", gathers["ffn", l], after=[h1])[1]

    scatters = {}

    def ffn_grads(l, dw_down, dw_up):
        scatters["ffn", l] = scatter_start(f"scatter_ffn{l}_start", [dw_up, dw_down.reshape(NDEV, DNR, D)])
        return scatters["ffn", l][4]

    def mixer_grads(l, dw_out, dpw, dw_in):
        grads = [dw_in.reshape(NDEV, IN_W // NDEV, D), dpw.astype(BF16).reshape(NDEV, CV // NDEV, CV),
                 dw_out.reshape(NDEV, D // NDEV, D)]
        scatters["mix", l] = scatter_start(f"scatter_mix{l}_start", grads)
        return scatters["mix", l][4]

    P = dict(
        meta=g_meta.transpose(1, 0, 2).reshape(N_META, D),
        cv_dw_w=g_cdw.transpose(1, 2, 0, 3).reshape(DEPTH, CONV_K, CV),
        ffn_dw_w=g_fdw.transpose(1, 0, 2, 3),
        ffn_dw_b=ffn_dw_b.reshape(DEPTH, NDEV, 1, UPW),
        attn_sinks=attn_sinks,
    )
    for n in ("norm_mix_g", "q_norm_g", "k_norm_g", "attn_out_g", "cv_dw_b", "cv_ln_g", "cv_ln_b", "cv_out_g",
              "ret_gn_g", "norm_ffn_g"):
        P[n] = W[n].reshape(DEPTH, 1, -1)

    hooks = dict(mixer_w=mixer_w, ffn_w=ffn_w, ffn_grads=ffn_grads, mixer_grads=mixer_grads,
                 first_after=gathers["ffn", 1][4])
    loss_part, dx, dmeta, G = local_step(x[0], loss_target[0], P, hooks)

    small_full = {
        "meta": dmeta,
        "cv_dw_w": jnp.stack(G["cv_dw_w"]),
        "ffn_dw_w": jnp.stack([g.transpose(1, 0, 2).reshape(FFN_K, 2 * D_FF) for g in G["ffn_dw_w"]]),
        "ffn_dw_b": jnp.stack([g.reshape(2 * D_FF) for g in G["ffn_dw_b"]]),
        "attn_sinks": jnp.stack(G["attn_sinks"]),
    }
    for n in _SMALL:
        if n not in small_full:
            small_full[n] = jnp.stack([g.reshape(-1) for g in G[n]])
    shapes = [small_full[n].shape for n in _SMALL] + [(1,)]
    packed = _pack([small_full[n] for n in _SMALL] + [loss_part.reshape(1)])
    small_exchange = gather_start("reduce_small_start", [packed], landing([packed]), after=[dx])

    out = {}
    tiles = {"w_in": 144, "cv_pw": 32, "w_out": 128, "ffn_up": 176, "ffn_down": 176}
    me1 = me.astype(jnp.int32).reshape(1)
    own, lands = {n: [None] * DEPTH for n in _BIG}, {n: [None] * DEPTH for n in _BIG}

    def arrived(kind, names, after):
        for l in range(DEPTH):
            srcs, got = scatter_wait(f"scatter_{kind}{l}_wait", scatters[kind, l], after=after)
            for n, s_, g_ in zip(names, srcs, got):
                own[n][l], lands[n][l] = s_, g_

    def update(names):
        for n in names:
            shard = W[n].shape
            rows, cols = shard[0] * shard[1], shard[2]
            res = adamw_big(own[n], lands[n], W[n].reshape(rows, cols), M[n].reshape(rows, cols),
                            V[n].reshape(rows, cols), me1, tiles[n], f"adamw_{n}")
            out[n] = [r.reshape(shard) for r in res]

    arrived("ffn", ("ffn_up", "ffn_down"), [small_exchange[4]])
    update(("ffn_up", "ffn_down"))
    slots = gather_wait("reduce_small_wait", small_exchange, after=[out["ffn_down"][0]])[1][0]
    summed = _unpack(sum_slots(slots, "reduce_small_sum"), shapes)
    loss = summed[-1][0]
    small_g = []
    for n, g in zip(_SMALL, summed):
        if n in _SMALL_SHARDED:
            width = _SMALL_SHARDED[n] // NDEV
            g = lax.dynamic_slice_in_dim(g, me * width, width, axis=g.ndim - 1)
        small_g.append(g)
    flat2 = lambda a: a.reshape(-1, a.shape[-1])
    res = adamw_many([flat2(g) for g in small_g], *[[flat2(X[n]) for n in _SMALL] for X in (W, M, V)], "adamw_small")
    for n, r in zip(_SMALL, res):
        out[n] = [a.reshape(W[n].shape) for a in r]
    arrived("mix", ("w_in", "cv_pw", "w_out"), [out["ffn_down"][0], res[0][0]])
    update(("w_in", "cv_pw", "w_out"))
    for n in _TRANSPOSED:
        out[n] = [r.transpose(0, 2, 1) for r in out[n]]

    return (loss, dx[None], *[out[n][0] for n in _ORDER], *[out[n][1] for n in _ORDER],
            *[out[n][2] for n in _ORDER], *[out[n][3] for n in _ORDER])
```

```python
import functools
import math

import numpy as np
import jax
import jax.numpy as jnp
from jax import lax
from jax.experimental import pallas as pl
from jax.experimental.pallas import tpu as pltpu

F32 = jnp.float32
BF16 = jnp.bfloat16

D = 1024
SEQ = 2048
DEPTH = 2
T = 128
L = SEQ + T
NB = L // T
N_META = 16
PAD = T - N_META
HD = 64
NQ = 8
NKV = 2
GQA = NQ // NKV
CV = 256
CONV_K = 31
RH = 4
D_FF = 2816
FFN_K = 3
IN_W = 2304
RMS_EPS = 1e-6
LN_EPS = 1e-5
NEG = -1e30
NDEV = 8
UPW = 2 * D_FF // NDEV
DNR = D_FF // NDEV
NPAIR = NDEV // 2

ADAM_LR, ADAM_B1, ADAM_B2, ADAM_EPS, ADAM_WD, ADAM_STEP = 0.001, 0.9, 0.999, 1e-08, 0.01, 10

VMEM_BIG = 56 * 1024 * 1024

MESH = pl.DeviceIdType.MESH


def _params(vmem=None):
    return pltpu.CompilerParams(vmem_limit_bytes=vmem) if vmem else None


def _dot(a, b, nt=False):
    return lax.dot_general(a, b, (((1,), (1 if nt else 0,)), ((), ())), preferred_element_type=F32)


def _sig(x):
    return 1.0 / (1.0 + jnp.exp(-x))


def _bf(x):
    return x.astype(BF16)


def _stack_rows(rows):
    idx = lax.broadcasted_iota(jnp.int32, (len(rows), rows[0].shape[1]), 0)
    out = jnp.zeros((len(rows), rows[0].shape[1]), F32)
    for r, v in enumerate(rows):
        out = jnp.where(idx == r, v, out)
    return out


def rms_bwd(x, g, dy, dres, name, after=None, last=False):
    n, w = x.shape
    has_res = dres is not None
    deps = [] if after is None else [after]

    def body(x_ref, g_ref, dy_ref, *rest):
        rest = rest[:len(rest) - 3 - len(deps)] + rest[len(rest) - 3:]
        if has_res:
            dres_ref, dx_ref, dxm_ref, dg_ref = rest
        else:
            dx_ref, dxm_ref, dg_ref = rest
        i = pl.program_id(0)
        xv = x_ref[...]
        r = lax.rsqrt(jnp.mean(xv * xv, axis=-1, keepdims=True) + RMS_EPS)
        xh = xv * r
        dyv = dy_ref[...]
        dxh = dyv * g_ref[...]
        dx = r * (dxh - xh * jnp.mean(dxh * xh, axis=-1, keepdims=True))
        if has_res:
            dx = dx + dres_ref[...]
        dx_ref[...] = dx
        if last:
            @pl.when(i == 0)
            def _():
                dxm_ref[...] = dx
        else:
            rows = i * T + lax.broadcasted_iota(jnp.int32, (T, 1), 0)
            dxm_ref[...] = jnp.where(rows >= PAD, dx, 0.0).astype(BF16)
        part = jnp.sum(dyv * xh, axis=0, keepdims=True)

        @pl.when(i == 0)
        def _():
            dg_ref[...] = part

        @pl.when(i > 0)
        def _():
            dg_ref[...] += part

    row = pl.BlockSpec((T, w), lambda i: (i, 0))
    vec = pl.BlockSpec((1, w), lambda i: (0, 0))
    ins = [x, g, dy] + ([dres] if has_res else []) + deps
    if last:
        out_shape = (jax.ShapeDtypeStruct((n - T, w), F32), jax.ShapeDtypeStruct((T, w), F32))
        out_specs = (pl.BlockSpec((T, w), lambda i: (jnp.maximum(i - 1, 0), 0)), pl.BlockSpec((T, w), lambda i: (0, 0)))
    else:
        out_shape = (jax.ShapeDtypeStruct((n, w), F32), jax.ShapeDtypeStruct((n, w), BF16))
        out_specs = (row, row)
    return pl.pallas_call(
        body, name=name, out_shape=(*out_shape, jax.ShapeDtypeStruct((1, w), F32)), grid=(n // T,),
        in_specs=[row, vec, row] + ([row] if has_res else [])
        + [_HBM] * len(deps),
        out_specs=(*out_specs, vec))(*ins)


def _mm(name, a, b, *, grid, a_spec, b_spec, o_spec, out_shape, nt=False, ta=False, red=False, res=None,
        res_spec=None):
    def body(a_ref, b_ref, *rest):
        o_ref = rest[-1]
        av = a_ref[...]
        bv = b_ref[...]
        if bv.ndim == 3:
            bv = bv.reshape(bv.shape[0] * bv.shape[1], bv.shape[2])
        if ta:
            acc = lax.dot_general(av, bv, (((0,), (0,)), ((), ())), preferred_element_type=F32)
        else:
            acc = _dot(av, bv, nt)
        if red:
            k = pl.program_id(0)

            @pl.when(k == 0)
            def _():
                o_ref[...] = acc

            @pl.when(k > 0)
            def _():
                o_ref[...] += acc
        else:
            if res is not None:
                rows = lax.broadcasted_iota(jnp.int32, (acc.shape[0], 1), 0)
                acc = rest[0][...] + jnp.where(rows >= PAD, acc, 0.0)
            o_ref[...] = acc.astype(o_ref.dtype)

    ins = [a, b] + ([res] if res is not None else [])
    specs = [a_spec, b_spec] + ([res_spec] if res is not None else [])
    return pl.pallas_call(body, name=name, out_shape=out_shape, grid=grid, in_specs=specs, out_specs=o_spec,
                          compiler_params=_params(VMEM_BIG))(*ins)


def mm_nn(name, a, b, tn, out_dtype=F32, res=None):
    m, k = a.shape
    n = b.shape[1]
    return _mm(name, a, b, grid=(n // tn,),
               a_spec=pl.BlockSpec((m, k), lambda j: (0, 0)), b_spec=pl.BlockSpec((k, tn), lambda j: (0, j)),
               o_spec=pl.BlockSpec((m, tn), lambda j: (0, j)), out_shape=jax.ShapeDtypeStruct((m, n), out_dtype),
               res=res, res_spec=pl.BlockSpec((m, tn), lambda j: (0, j)))


def mm_nt(name, a, b, tn):
    m, k = a.shape
    n = b.shape[0]
    return _mm(name, a, b, grid=(n // tn,), nt=True,
               a_spec=pl.BlockSpec((m, k), lambda j: (0, 0)), b_spec=pl.BlockSpec((tn, k), lambda j: (j, 0)),
               o_spec=pl.BlockSpec((m, tn), lambda j: (0, j)), out_shape=jax.ShapeDtypeStruct((m, n), F32))


def mm_ta(name, a, b, tm, out_dtype=BF16):
    k, m = a.shape
    n = b.shape[1]
    return _mm(name, a, b, grid=(m // tm,), ta=True,
               a_spec=pl.BlockSpec((k, tm), lambda j: (0, j)), b_spec=pl.BlockSpec((k, n), lambda j: (0, 0)),
               o_spec=pl.BlockSpec((tm, n), lambda j: (j, 0)), out_shape=jax.ShapeDtypeStruct((m, n), out_dtype))


def _rms_mm(name, h, g, b, *, grid, b_spec, o_spec, out_shape, after=None):
    deps = [] if after is None else [after]

    def body(h_ref, g_ref, b_ref, *rest):
        o_ref, u_ref = rest[len(deps):]

        @pl.when(pl.program_id(0) == 0)
        def _():
            gv = g_ref[...]
            for c in range(NB):
                rows = slice(c * T, (c + 1) * T)
                xv = h_ref[rows, :]
                r = lax.rsqrt(jnp.mean(xv * xv, axis=-1, keepdims=True) + RMS_EPS)
                u_ref[rows, :] = (xv * r * gv).astype(BF16)

        o_ref[...] = _dot(u_ref[...], b_ref[...], nt=True)

    whole = lambda r: pl.BlockSpec((r, D), lambda j: (0, 0))
    return pl.pallas_call(
        body, name=name, out_shape=(out_shape, jax.ShapeDtypeStruct((L, D), BF16)), grid=grid,
        in_specs=[whole(L), whole(1), b_spec] + [_HBM] * len(deps), out_specs=(o_spec, whole(L)),
        compiler_params=_params(VMEM_BIG))(h, g, b, *deps)


def out_proj_fwd(name, o, og, y_cv, y_ret, w, res, tn):
    wa = o.shape[1]

    def body(o_ref, og_ref, cv_ref, ret_ref, w_ref, r_ref, out_ref, ycat_ref):
        @pl.when(pl.program_id(0) == 0)
        def _():
            gv = og_ref[...]
            for c in range(NB):
                rows = slice(c * T, (c + 1) * T)
                ycat_ref[rows, 0:wa] = _rms_rows(o_ref[rows, :], gv)[0].astype(BF16)
                ycat_ref[rows, wa:wa + CV] = cv_ref[rows, :]
                ycat_ref[rows, wa + CV:D] = ret_ref[rows, :]

        acc = _dot(ycat_ref[...], w_ref[...])
        rows = lax.broadcasted_iota(jnp.int32, (L, 1), 0)
        out_ref[...] = r_ref[...] + jnp.where(rows >= PAD, acc, 0.0)

    whole = lambda w_: pl.BlockSpec((L, w_), lambda j: (0, 0))
    tile = pl.BlockSpec((L, tn), lambda j: (0, j))
    return pl.pallas_call(
        body, name=name, out_shape=(jax.ShapeDtypeStruct((L, D), F32), jax.ShapeDtypeStruct((L, D), BF16)),
        grid=(D // tn,),
        in_specs=[whole(wa), pl.BlockSpec((1, wa), lambda j: (0, 0)), whole(CV), whole(RW),
                  pl.BlockSpec((D, tn), lambda j: (0, j)), tile],
        out_specs=(tile, whole(D)), compiler_params=_params(VMEM_BIG))(o, og, y_cv, y_ret, w, res)


def in_proj_fwd(name, h, g, wt, tn, after=None):
    return _rms_mm(name, h, g, wt, grid=(IN_W // tn,), b_spec=pl.BlockSpec((tn, D), lambda j: (j, 0)),
                   o_spec=pl.BlockSpec((L, tn), lambda j: (0, j)), out_shape=jax.ShapeDtypeStruct((L, IN_W), F32),
                   after=after)


def ffn_up_fwd(name, h, g, wupt):
    return _rms_mm(name, h, g, wupt, grid=(NDEV,), b_spec=pl.BlockSpec((None, UPW, D), lambda j: (j, 0, 0)),
                   o_spec=pl.BlockSpec((None, L, UPW), lambda j: (j, 0, 0)),
                   out_shape=jax.ShapeDtypeStruct((NDEV, L, UPW), F32))


def _slab_specs():
    gate = pl.BlockSpec((None, L, UPW), lambda j: (jnp.minimum(j, NPAIR - 1), 0, 0))
    up = pl.BlockSpec((None, L, UPW), lambda j: (jnp.maximum(j - NPAIR, 0), 0, 0))
    return gate, up


def ffn_up_dx(name, dfg, dfu, wupt):
    def body(g_ref, u_ref, b_ref, o_ref):
        j = pl.program_id(0)

        @pl.when(j == 0)
        def _():
            o_ref[...] = _dot(g_ref[...], b_ref[...])

        @pl.when(jnp.logical_and(j > 0, j < NPAIR))
        def _():
            o_ref[...] += _dot(g_ref[...], b_ref[...])

        @pl.when(j >= NPAIR)
        def _():
            o_ref[...] += _dot(u_ref[...], b_ref[...])

    gate, up = _slab_specs()
    return pl.pallas_call(
        body, name=name, out_shape=jax.ShapeDtypeStruct((L, D), F32), grid=(NDEV,),
        in_specs=[gate, up, pl.BlockSpec((None, UPW, D), lambda j: (j, 0, 0))],
        out_specs=pl.BlockSpec((L, D), lambda j: (0, 0)), compiler_params=_params(VMEM_BIG))(dfg, dfu, wupt)


def ffn_up_dw(name, u2, dfg, dfu):
    tdot = lambda a, b: lax.dot_general(a, b, (((0,), (0,)), ((), ())), preferred_element_type=F32)

    def body(a_ref, g_ref, u_ref, o_ref):
        j = pl.program_id(0)

        @pl.when(j < NPAIR)
        def _():
            o_ref[...] = tdot(g_ref[...], a_ref[...]).astype(o_ref.dtype)

        @pl.when(j >= NPAIR)
        def _():
            o_ref[...] = tdot(u_ref[...], a_ref[...]).astype(o_ref.dtype)

    gate, up = _slab_specs()
    return pl.pallas_call(
        body, name=name, out_shape=jax.ShapeDtypeStruct((NDEV, UPW, D), BF16), grid=(NDEV,),
        in_specs=[pl.BlockSpec((L, D), lambda j: (0, 0)), gate, up],
        out_specs=pl.BlockSpec((None, UPW, D), lambda j: (j, 0, 0)), compiler_params=_params(VMEM_BIG))(u2, dfg, dfu)


def _ffn_down_tile(a_ref, b_ref, r_ref, tn):
    acc = jnp.zeros((L, tn), F32)
    for g in range(NPAIR):
        bv = b_ref[2 * g:2 * g + 2]
        acc = acc + _dot(a_ref[g], bv.reshape(2 * DNR, tn))
    rows = lax.broadcasted_iota(jnp.int32, (L, 1), 0)
    return r_ref[...] + jnp.where(rows >= PAD, acc, 0.0)


def _ffn_down_specs(tn):
    return [pl.BlockSpec((NPAIR, L, UPW), lambda j: (0, 0, 0)), pl.BlockSpec((NDEV, DNR, tn), lambda j: (0, 0, j)),
            pl.BlockSpec((L, tn), lambda j: (0, j))]


def ffn_down_fwd(name, a, wdn, res, tn=256):
    def body(a_ref, b_ref, r_ref, o_ref):
        o_ref[...] = _ffn_down_tile(a_ref, b_ref, r_ref, tn)

    return pl.pallas_call(
        body, name=name, out_shape=jax.ShapeDtypeStruct((L, D), F32), grid=(D // tn,),
        in_specs=_ffn_down_specs(tn), out_specs=pl.BlockSpec((L, tn), lambda j: (0, j)),
        compiler_params=_params(VMEM_BIG))(a, wdn, res)


def ffn_down_loss(name, a, wdn, res, target, tn=256):
    def body(a_ref, b_ref, r_ref, t_ref, dh_ref, dhm_ref, loss_ref):
        j = pl.program_id(0)
        hv = _ffn_down_tile(a_ref, b_ref, r_ref, tn)
        e = hv[T:] - t_ref[...]
        dh = e * (1.0 / D)
        dh_ref[0:T, :] = jnp.zeros((T, tn), F32)
        dh_ref[T:, :] = dh
        dhm_ref[0:T, :] = jnp.zeros((T, tn), BF16)
        dhm_ref[T:, :] = dh.astype(BF16)
        part = jnp.sum(jnp.sum(e * e, axis=1, keepdims=True), axis=0, keepdims=True) * (0.5 / D)

        @pl.when(j == 0)
        def _():
            loss_ref[...] = jnp.broadcast_to(part, loss_ref.shape)

        @pl.when(j > 0)
        def _():
            loss_ref[...] += jnp.broadcast_to(part, loss_ref.shape)

    tile = pl.BlockSpec((L, tn), lambda j: (0, j))
    return pl.pallas_call(
        body, name=name,
        out_shape=(jax.ShapeDtypeStruct((L, D), F32), jax.ShapeDtypeStruct((L, D), BF16),
                   jax.ShapeDtypeStruct((8, 128), F32)),
        grid=(D // tn,),
        in_specs=_ffn_down_specs(tn) + [pl.BlockSpec((SEQ, tn), lambda j: (0, j))],
        out_specs=(tile, tile, pl.BlockSpec((8, 128), lambda j: (0, 0))),
        compiler_params=_params(VMEM_BIG))(a, wdn, res, target)


def ffn_down_dx(name, dh, wdn):
    return _mm(name, dh, wdn, grid=(NPAIR,), nt=True,
               a_spec=pl.BlockSpec((L, D), lambda g: (0, 0)),
               b_spec=pl.BlockSpec((2, DNR, D), lambda g: (g, 0, 0)),
               o_spec=pl.BlockSpec((None, L, UPW), lambda g: (g, 0, 0)),
               out_shape=jax.ShapeDtypeStruct((NPAIR, L, UPW), F32))


def ffn_down_dw(name, a, dh):
    return _mm(name, a, dh, grid=(NPAIR,), ta=True,
               a_spec=pl.BlockSpec((None, L, UPW), lambda g: (g, 0, 0)),
               b_spec=pl.BlockSpec((L, D), lambda g: (0, 0)),
               o_spec=pl.BlockSpec((UPW, D), lambda g: (g, 0)),
               out_shape=jax.ShapeDtypeStruct((D_FF, D), BF16))


_SLOPES = [2.0 ** (-8.0 * (h + 1) / NQ) for h in range(NQ)]
_SCALE = HD ** -0.5


QR = GQA * T
KC = 3 * T


def _attn_geometry(g, n, sink_ref):
    row = lax.broadcasted_iota(jnp.int32, (QR, KC), 0)
    col = lax.broadcasted_iota(jnp.int32, (QR, KC), 1)
    i = row & (T - 1)
    j = col & (T - 1)
    blk = col // T
    d_meta = n * T + i - j
    ok_meta = (j >= PAD) & (d_meta >= 0)
    ok_prev = j > i + jnp.where(n >= 2, 0, T)
    ok_cur = j <= i - jnp.where(n >= 1, 0, T)
    ok = ((blk == 0) & ok_meta) | ((blk == 1) & ok_prev) | ((blk == 2) & ok_cur)
    dist = jnp.where(blk == 0, jnp.minimum(d_meta, T), jnp.where(blk == 1, T + i - j, i - j)).astype(F32)
    head = lax.broadcasted_iota(jnp.int32, (QR, 1), 0) // T
    slope = jnp.zeros((QR, 1), F32)
    sink = jnp.zeros((QR, 1), F32)
    for hh in range(GQA):
        slope = jnp.where(head == hh, jnp.where(g == 0, _SLOPES[hh], _SLOPES[GQA + hh]), slope)
        sink = jnp.where(head == hh, sink_ref[g * GQA + hh], sink)
    return ok, slope * dist, sink


def _attn_probs(qn, keys, geo):
    ok, penalty, sink = geo
    s = jnp.where(ok, _dot(qn, keys, nt=True) * _SCALE - penalty, NEG)
    m = jnp.maximum(jnp.max(s, axis=-1, keepdims=True), sink)
    e = jnp.exp(s - m)
    e_sink = jnp.exp(sink - m)
    inv = 1.0 / (e.sum(axis=-1, keepdims=True) + e_sink)
    return e, inv, e_sink * inv


def _rms_rows(x, g):
    r = lax.rsqrt(jnp.mean(x * x, axis=-1, keepdims=True) + RMS_EPS)
    xh = x * r
    return xh * g, xh, r


def _rms_rows_bwd(dy, g, xh, r):
    dxh = dy * g
    return r * (dxh - xh * jnp.mean(dxh * xh, axis=-1, keepdims=True))


def _rows3(m_ref, p_ref, c_ref):
    return jnp.concatenate([m_ref[...], p_ref[...], c_ref[...]], axis=0)


_ATT_K, _ATT_V = 4, 5


def _nat_specs():
    qspec = pl.BlockSpec((T, NQ * HD), lambda n: (n, 0))
    kv = lambda col: (pl.BlockSpec((T, NKV * HD), lambda n: (n, col)),
                      pl.BlockSpec((T, NKV * HD), lambda n: (jnp.maximum(n - 1, 0), col)),
                      pl.BlockSpec((T, NKV * HD), lambda n: (0, col)))
    return qspec, kv(_ATT_K), kv(_ATT_V)


def _group(g, x, w=HD):
    return x[:, g * w:(g + 1) * w]


def _stack_heads(x):
    return jnp.concatenate([x[:, hh * HD:(hh + 1) * HD] for hh in range(GQA)], axis=0)


def _unstack_heads(x):
    return jnp.concatenate([x[hh * T:(hh + 1) * T] for hh in range(GQA)], axis=1)


def attn_fwd(proj, qg, kg, sinks, name):
    def body(sink_ref, q_ref, kc_ref, kp_ref, km_ref, vc_ref, vp_ref, vm_ref, qg_ref, kg_ref, o_ref):
        n = pl.program_id(0)
        k_all = _rows3(km_ref, kp_ref, kc_ref)
        v_all = _rows3(vm_ref, vp_ref, vc_ref)
        q_all = q_ref[...]
        outs = []
        for g in range(NKV):
            geo = _attn_geometry(g, n, sink_ref)
            keys = _bf(_rms_rows(_group(g, k_all), kg_ref[...])[0])
            vals = _bf(_group(g, v_all))
            qn = _bf(_rms_rows(_stack_heads(_group(g, q_all, GQA * HD)), qg_ref[...])[0])
            e, inv, _ = _attn_probs(qn, keys, geo)
            outs.append(_unstack_heads(_dot(_bf(e), vals) * inv))
        o_ref[...] = jnp.concatenate(outs, axis=1)

    qspec, (kc, kp, km), (vc, vp, vm) = _nat_specs()
    vec = pl.BlockSpec((1, HD), lambda n: (0, 0))
    return pl.pallas_call(
        body, name=name, out_shape=jax.ShapeDtypeStruct((L, NQ * HD), F32), grid=(NB,),
        in_specs=[pl.BlockSpec(memory_space=pltpu.SMEM), qspec, kc, kp, km, vc, vp, vm, vec, vec],
        out_specs=qspec)(sinks, proj, proj, proj, proj, proj, proj, proj, qg, kg)


def attn_bwd(proj, o, dycat, og, qg, kg, sinks, name):
    def body(sink_ref, q_ref, kc_ref, kp_ref, km_ref, vc_ref, vp_ref, vm_ref, o_ref, dy_ref, og_ref, qg_ref, kg_ref,
             dq_ref, dkc_ref, dkp_ref, dkm_ref, dvc_ref, dvp_ref, dvm_ref, dqg_ref, dkg_ref, dsk_ref, dog_ref):
        n = pl.program_id(0)

        @pl.when(n == 0)
        def _():
            for r in (dkm_ref, dvm_ref, dqg_ref, dkg_ref, dsk_ref, dog_ref):
                r[...] = jnp.zeros_like(r)

        ogv = og_ref[...]
        dyv = dy_ref[...]
        _, oh, orr = _rms_rows(o_ref[...], ogv)
        do_all = _rms_rows_bwd(dyv, ogv, oh, orr)
        dog_ref[...] += jnp.sum(dyv * oh, axis=0, keepdims=True)

        kgv = kg_ref[...]
        qgv = qg_ref[...]
        k_all = _rows3(km_ref, kp_ref, kc_ref)
        v_all = _rows3(vm_ref, vp_ref, vc_ref)
        q_all = q_ref[...]
        dq, dk, dv, dsk_rows = [], [], [], []
        dqg_acc = jnp.zeros((1, HD), F32)
        dkg_acc = jnp.zeros((1, HD), F32)
        for g in range(NKV):
            geo = _attn_geometry(g, n, sink_ref)
            kn_f, kh, kr = _rms_rows(_group(g, k_all), kgv)
            keys = _bf(kn_f)
            vals = _bf(_group(g, v_all))
            qn_f, qh, qr = _rms_rows(_stack_heads(_group(g, q_all, GQA * HD)), qgv)
            qn = _bf(qn_f)
            e, inv, p_sink = _attn_probs(qn, keys, geo)
            dob = _bf(_stack_heads(_group(g, do_all, GQA * HD)))
            p = e * inv
            dp = _dot(dob, vals, nt=True)
            delta = (p * dp).sum(axis=-1, keepdims=True)
            for hh in range(GQA):
                part = -jnp.sum((p_sink * delta)[hh * T:(hh + 1) * T], axis=0, keepdims=True)
                dsk_rows.append(jnp.broadcast_to(part, (1, 128)))
            ds = p * (dp - delta)
            dqn = _dot(_bf(ds), keys) * _SCALE
            dq.append(_unstack_heads(_rms_rows_bwd(dqn, qgv, qh, qr)))
            dqg_acc = dqg_acc + jnp.sum(dqn * qh, axis=0, keepdims=True)
            dkn = _dot(_bf(ds.T), qn) * _SCALE
            dkg_acc = dkg_acc + jnp.sum(dkn * kh, axis=0, keepdims=True)
            dk.append(_rms_rows_bwd(dkn, kgv, kh, kr))
            dv.append(_dot(_bf(p.T), dob))
        dq_ref[...] = jnp.concatenate(dq, axis=1).astype(dq_ref.dtype)
        dk_all = jnp.concatenate(dk, axis=1)
        dv_all = jnp.concatenate(dv, axis=1)
        dkm_ref[...] += dk_all[0:T]
        dkp_ref[...] = dk_all[T:2 * T]
        dkc_ref[...] = dk_all[2 * T:3 * T]
        dvm_ref[...] += dv_all[0:T]
        dvp_ref[...] = dv_all[T:2 * T]
        dvc_ref[...] = dv_all[2 * T:3 * T]
        dsk_ref[...] += _stack_rows(dsk_rows)
        dqg_ref[...] += dqg_acc
        dkg_ref[...] += dkg_acc

    qspec, (kc, kp, km), (vc, vp, vm) = _nat_specs()
    vec = pl.BlockSpec((1, HD), lambda n: (0, 0))
    cur = pl.BlockSpec((T, NKV * HD), lambda n: (n, 0))
    meta = pl.BlockSpec((T, NKV * HD), lambda n: (0, 0))
    kv_shape = jax.ShapeDtypeStruct((L, NKV * HD), F32)
    meta_shape = jax.ShapeDtypeStruct((T, NKV * HD), F32)
    vec_shape = jax.ShapeDtypeStruct((1, HD), F32)
    ogvec = pl.BlockSpec((1, NQ * HD), lambda n: (0, 0))
    return pl.pallas_call(
        body, name=name,
        out_shape=(jax.ShapeDtypeStruct((L, NQ * HD), BF16), kv_shape, kv_shape, meta_shape, kv_shape, kv_shape,
                   meta_shape, vec_shape, vec_shape, jax.ShapeDtypeStruct((NQ, 128), F32),
                   jax.ShapeDtypeStruct((1, NQ * HD), F32)),
        grid=(NB,),
        in_specs=[pl.BlockSpec(memory_space=pltpu.SMEM), qspec, kc, kp, km, vc, vp, vm, qspec, qspec, ogvec, vec, vec],
        out_specs=(qspec, cur, cur, meta, cur, cur, meta, vec, vec,
                   pl.BlockSpec((NQ, 128), lambda n: (0, 0)), ogvec))(
            sinks, proj, proj, proj, proj, proj, proj, proj, o, dycat, og, qg, kg)


RW = RH * HD


def _ret_tables():
    h = np.arange(RH, dtype=np.float64)
    lg = np.log1p(-np.exp2(-5.0 - h))
    idx = np.arange(T, dtype=np.float64)
    diff = idx[:, None] - idx[None, :]
    decay = np.where(diff[None] >= 0, np.exp(np.maximum(diff, 0.0)[None] * lg[:, None, None]), 0.0)
    zeta = np.exp((T - 1 - idx)[None, :] * lg[:, None])
    xi = np.exp((idx + 1.0)[None, :] * lg[:, None])
    cd = np.exp(T * lg)
    lanes = lambda a: np.repeat(a.T, HD, axis=1)
    head_of = np.arange(RW) // HD
    same = (head_of[:, None] == head_of[None, :]).astype(np.float64)
    f = lambda a: jnp.asarray(a, F32)
    return dict(decay=f(decay), zeta=f(lanes(zeta)), xi=f(lanes(xi)), cd=f(np.repeat(cd, HD)[None, :]),
                head=f((head_of[None, :] == np.arange(RH)[:, None]).astype(np.float64)[:, None, :]),
                same=f(same), avg=jnp.asarray(same / HD, BF16))


def _seg_mean(x, avg):
    hi = _bf(x)
    lo = _bf(x - hi.astype(F32))
    return _dot(hi, avg) + _dot(lo, avg)


def _ret_specs(col0, order):
    return lambda col: pl.BlockSpec((T, RW), lambda i: (order(i), col0 + col))


def _ret_chunk(q, kf, v, s, tb):
    dec, xi, head = tb
    vb = _bf(v)
    kb = _bf(kf)
    y = _dot(_bf(q * xi), _bf(s))
    a = []
    for h in range(RH):
        a.append(_dot(_bf(q * head[h]), kb, nt=True) * dec[h])
        y = y + head[h] * _dot(_bf(a[h]), vb)
    return a, y


def _gn_rows(y):
    mu = jnp.mean(y, axis=-1, keepdims=True)
    yc = y - mu
    rstd = lax.rsqrt(jnp.mean(yc * yc, axis=-1, keepdims=True) + LN_EPS)
    return yc * rstd, rstd


_RET_COL0 = _RET_Q = 5


def _ret_consts(tb):
    names = ("decay", "zeta", "xi", "cd", "head", "same", "avg")
    full = lambda a: pl.BlockSpec(a.shape, lambda i: (0,) * a.ndim)
    return [tb[n] for n in names], [full(tb[n]) for n in names]


def _ret_inputs(n, q_ref, k_ref, v_ref):
    rows = n * T + lax.broadcasted_iota(jnp.int32, (T, 1), 0)
    valid = rows >= PAD
    return q_ref[...], jnp.where(valid, k_ref[...] * (HD ** -0.5), 0.0), v_ref[...], valid


def ret_fwd(proj, gng, tb, name):
    def body(q_ref, k_ref, v_ref, g_ref, dec_ref, zeta_ref, xi_ref, cd_ref, head_ref, same_ref, avg_ref, gng_ref,
             y_ref, st_ref, s_scr):
        n = pl.program_id(0)

        @pl.when(n == 0)
        def _():
            s_scr[...] = jnp.zeros_like(s_scr)

        s = s_scr[...]
        st_ref[...] = s
        q, kf, v, _ = _ret_inputs(n, q_ref, k_ref, v_ref)
        _, y = _ret_chunk(q, kf, v, s, (dec_ref, xi_ref[...], head_ref))
        s_scr[...] = cd_ref[...] * s + same_ref[...] * _dot(_bf((kf * zeta_ref[...]).T), _bf(v))
        avg = avg_ref[...]
        yc = y - _seg_mean(y, avg)
        yh = yc * lax.rsqrt(_seg_mean(yc * yc, avg) + LN_EPS)
        gv = g_ref[...]
        y_ref[...] = (gv * _sig(gv) * (yh * gng_ref[...])).astype(y_ref.dtype)

    col = _ret_specs(_RET_COL0, lambda i: i)
    consts, cspecs = _ret_consts(tb)
    return pl.pallas_call(
        body, name=name,
        out_shape=(jax.ShapeDtypeStruct((L, RW), BF16), jax.ShapeDtypeStruct((NB, RW, RW), F32)),
        grid=(NB,),
        in_specs=[col(0), col(1), col(2), col(3)] + cspecs + [pl.BlockSpec((1, RW), lambda i: (0, 0))],
        out_specs=(pl.BlockSpec((T, RW), lambda i: (i, 0)), pl.BlockSpec((None, RW, RW), lambda i: (i, 0, 0))),
        scratch_shapes=[pltpu.VMEM((RW, RW), F32)])(proj, proj, proj, proj, *consts, gng)


def ret_bwd(proj, dycat, states, gng, tb, name):
    def body(q_ref, k_ref, v_ref, g_ref, do_ref, st_ref, dec_ref, zeta_ref, xi_ref, cd_ref, head_ref, same_ref,
             avg_ref, gng_ref, d_ref, dgn_ref, ds_scr):
        i = pl.program_id(0)
        n = NB - 1 - i

        @pl.when(i == 0)
        def _():
            ds_scr[...] = jnp.zeros_like(ds_scr)

        dsn = _bf(ds_scr[...])
        s = st_ref[...]
        sb = _bf(s)
        q, kf, v, valid = _ret_inputs(n, q_ref, k_ref, v_ref)
        xi, zeta, avg = xi_ref[...], zeta_ref[...], avg_ref[...]
        a, y = _ret_chunk(q, kf, v, s, (dec_ref, xi, head_ref))
        yc = y - _seg_mean(y, avg)
        rstd = lax.rsqrt(_seg_mean(yc * yc, avg) + LN_EPS)
        yh = yc * rstd
        gv = g_ref[...]
        sg = _sig(gv)
        sil = gv * sg
        gn = gng_ref[...]
        dout = do_ref[...]
        d_ref[:, 3 * RW:4 * RW] = (dout * (yh * gn) * (sg * (1.0 + gv * (1.0 - sg)))).astype(d_ref.dtype)
        dyh = dout * sil * gn
        part = jnp.sum(dout * sil * yh, axis=0, keepdims=True)

        @pl.when(i == 0)
        def _():
            dgn_ref[...] = part

        @pl.when(i > 0)
        def _():
            dgn_ref[...] += part

        dy = rstd * (dyh - _seg_mean(dyh, avg) - yh * _seg_mean(dyh * yh, avg))
        dyb = _bf(dy)
        vb, kb, qb = _bf(v), _bf(kf), _bf(q)
        dq = _dot(dyb, sb, nt=True) * xi
        dkf = _dot(vb, dsn, nt=True) * zeta
        dv = _dot(_bf(kf * zeta), dsn)
        for h in range(RH):
            m = head_ref[h]
            da = _dot(_bf(dy * m), vb, nt=True) * dec_ref[h]
            dv = dv + m * _dot(_bf(a[h].T), dyb)
            dq = dq + m * _dot(_bf(da), kb)
            dkf = dkf + m * _dot(_bf(da.T), qb)
        d_ref[:, 0:RW] = dq.astype(d_ref.dtype)
        d_ref[:, RW:2 * RW] = jnp.where(valid, dkf * (HD ** -0.5), 0.0).astype(d_ref.dtype)
        d_ref[:, 2 * RW:3 * RW] = dv.astype(d_ref.dtype)
        ds_scr[...] = cd_ref[...] * ds_scr[...] + same_ref[...] * _dot(_bf((q * xi).T), dyb)

    back = lambda i: NB - 1 - i
    col = _ret_specs(_RET_COL0, back)
    consts, cspecs = _ret_consts(tb)
    vec = pl.BlockSpec((1, RW), lambda i: (0, 0))
    return pl.pallas_call(
        body, name=name,
        out_shape=(jax.ShapeDtypeStruct((L, 4 * RW), BF16), jax.ShapeDtypeStruct((1, RW), F32)),
        grid=(NB,),
        in_specs=[col(0), col(1), col(2), col(3), _ret_specs(3, back)(0),
                  pl.BlockSpec((None, RW, RW), lambda i: (back(i), 0, 0))] + cspecs + [vec],
        out_specs=(pl.BlockSpec((T, 4 * RW), lambda i: (back(i), 0)), vec),
        scratch_shapes=[pltpu.VMEM((RW, RW), F32)])(proj, proj, proj, proj, dycat, states, *consts, gng)


HALO = 32
TAP0 = HALO - (CONV_K - 1)


def _conv_specs():
    cur = lambda col=0: pl.BlockSpec((T, CV), lambda c: (c, col))
    before = lambda col=0: pl.BlockSpec((HALO, CV), lambda c: (jnp.maximum(c * (T // HALO) - 1, 0), col))
    after = pl.BlockSpec((HALO, CV), lambda c: (jnp.minimum((c + 1) * (T // HALO), L // HALO - 1), 0))
    full = lambda r, w: pl.BlockSpec((r, w), lambda c: (0, 0))
    return cur, before, after, full


_CONV_A, _CONV_B = 3, 4
_DY_CONV = 2


def _conv_post(xc, lg_ref, lb_ref):
    xh, rstd = _gn_rows(xc)
    z = xh * lg_ref[...] + lb_ref[...]
    return xh, rstd, z, _sig(z)


def conv_fwd(proj, w, b, lg, lb, pw, og, name):
    def body(ca_ref, cah_ref, cb_ref, cbh_ref, w_ref, b_ref, lg_ref, lb_ref, pw_ref, og_ref, y_ref, xc_ref, u_scr):
        c = pl.program_id(0)
        u_scr[0:HALO, :] = jnp.where(c > 0, cah_ref[...] * _sig(cbh_ref[...]), 0.0)
        u_scr[HALO:HALO + T, :] = ca_ref[...] * _sig(cb_ref[...])
        acc = jnp.zeros((T, CV), F32)
        for k in range(CONV_K):
            acc = acc + w_ref[k:k + 1, :] * u_scr[TAP0 + k:TAP0 + k + T, :]
        xc = acc + b_ref[...]
        xc_ref[...] = xc
        _, _, z, sg = _conv_post(xc, lg_ref, lb_ref)
        zp = _dot(_bf(z * sg), pw_ref[...])
        y_ref[...] = _rms_rows(zp, og_ref[...])[0].astype(y_ref.dtype)

    cur, before, _, full = _conv_specs()
    vec = full(1, CV)
    seq = jax.ShapeDtypeStruct((L, CV), F32)
    return pl.pallas_call(
        body, name=name, out_shape=(jax.ShapeDtypeStruct((L, CV), BF16), seq), grid=(NB,),
        in_specs=[cur(_CONV_A), before(_CONV_A), cur(_CONV_B), before(_CONV_B), full(CONV_K, CV), vec, vec, vec,
                  full(CV, CV), vec],
        out_specs=(cur(), cur()),
        scratch_shapes=[pltpu.VMEM((HALO + T, CV), F32)])(proj, proj, proj, proj, w, b, lg, lb, pw, og)


def conv_bwd(proj, xc, dycat, w, lg, lb, pw, og, name):
    def body1(xc_ref, dy_ref, lg_ref, lb_ref, pw_ref, og_ref, dxc_ref, db_ref, dlg_ref, dlb_ref, dog_ref, dpw_ref):
        c = pl.program_id(0)

        @pl.when(c == 0)
        def _():
            for r in (db_ref, dlg_ref, dlb_ref, dog_ref, dpw_ref):
                r[...] = jnp.zeros_like(r)

        xh, rstd, z, sg = _conv_post(xc_ref[...], lg_ref, lb_ref)
        s = z * sg
        zp = _dot(_bf(s), pw_ref[...])
        ogv = og_ref[...]
        _, zph, r2 = _rms_rows(zp, ogv)
        dyv = dy_ref[...]
        dog_ref[...] += jnp.sum(dyv * zph, axis=0, keepdims=True)
        dzpb = _bf(_rms_rows_bwd(dyv, ogv, zph, r2))
        dpw_ref[...] += _dot(_bf(s.T), dzpb)
        dz = _dot(dzpb, pw_ref[...], nt=True) * (sg * (1.0 + z * (1.0 - sg)))
        dlg_ref[...] += jnp.sum(dz * xh, axis=0, keepdims=True)
        dlb_ref[...] += jnp.sum(dz, axis=0, keepdims=True)
        dxh = dz * lg_ref[...]
        dxc = rstd * (dxh - jnp.mean(dxh, axis=-1, keepdims=True) - xh * jnp.mean(dxh * xh, axis=-1, keepdims=True))
        db_ref[...] += jnp.sum(dxc, axis=0, keepdims=True)
        dxc_ref[...] = dxc

    def body2(dx_ref, dxa_ref, ca_ref, cb_ref, w_ref, dca_ref, dcb_ref, dw_ref, d_scr):
        c = pl.program_id(0)

        @pl.when(c == 0)
        def _():
            dw_ref[...] = jnp.zeros_like(dw_ref)

        d_scr[0:T, :] = dx_ref[...]
        d_scr[T:T + HALO, :] = jnp.where(c < NB - 1, dxa_ref[...], 0.0)
        ca = ca_ref[...]
        sg = _sig(cb_ref[...])
        u = ca * sg
        du = jnp.zeros((T, CV), F32)
        for k in range(CONV_K):
            off = CONV_K - 1 - k
            dj = d_scr[off:off + T, :]
            du = du + w_ref[k:k + 1, :] * dj
            dw_ref[k:k + 1, :] += jnp.sum(u * dj, axis=0, keepdims=True)
        dca_ref[...] = (du * sg).astype(dca_ref.dtype)
        dcb_ref[...] = (du * ca * sg * (1.0 - sg)).astype(dcb_ref.dtype)

    cur, _, after, full = _conv_specs()
    seq = jax.ShapeDtypeStruct((L, CV), F32)
    vsh = jax.ShapeDtypeStruct((1, CV), F32)
    vec = full(1, CV)
    dxc, db, dlg, dlb, dog, dpw = pl.pallas_call(
        body1, name=name + "_a",
        out_shape=(seq, vsh, vsh, vsh, vsh, jax.ShapeDtypeStruct((CV, CV), F32)),
        grid=(NB,),
        in_specs=[cur(), cur(_DY_CONV), vec, vec, full(CV, CV), vec],
        out_specs=(cur(), vec, vec, vec, vec, full(CV, CV)))(xc, dycat, lg, lb, pw, og)
    dca, dcb, dw = pl.pallas_call(
        body2, name=name + "_b", grid=(NB,),
        out_shape=(jax.ShapeDtypeStruct((L, CV), BF16), jax.ShapeDtypeStruct((L, CV), BF16),
                   jax.ShapeDtypeStruct((CONV_K, CV), F32)),
        in_specs=[cur(), after, cur(_CONV_A), cur(_CONV_B), full(CONV_K, CV)],
        out_specs=(cur(), cur(), full(CONV_K, CV)),
        scratch_shapes=[pltpu.VMEM((T + HALO, CV), F32)])(dxc, dxc, proj, proj, w)
    return dca, dcb, dw, db, dlg, dlb, dog, dpw


HR = 8


def _ffn_specs():
    cur = lambda off: pl.BlockSpec((None, T, UPW), lambda j, c: (j + off, c, 0))
    before = lambda off: pl.BlockSpec((None, HR, UPW), lambda j, c: (j + off, jnp.maximum(c * (T // HR) - 1, 0), 0))
    after = lambda off: pl.BlockSpec(
        (None, HR, UPW), lambda j, c: (j + off, jnp.minimum((c + 1) * (T // HR), L // HR - 1), 0))
    wspec = lambda off, r: pl.BlockSpec((None, r, UPW), lambda j, c: (j + off, 0, 0))
    return cur, before, after, wspec


HR16 = 16


def _ffn_after16():
    return pl.BlockSpec((None, HR16, UPW), lambda j, c: (j, jnp.minimum((c + 1) * (T // HR16), L // HR16 - 1), 0))


def ffn_act_fwd(f, w, b, name):
    strip = 16

    def body(fg_ref, fgh_ref, fu_ref, fuh_ref, wg_ref, wu_ref, bg_ref, bu_ref, a_ref, gc_ref, uc_ref, hg_scr, hu_scr):
        c = pl.program_id(1)
        for scr, x_ref, xh_ref in ((hg_scr, fg_ref, fgh_ref), (hu_scr, fu_ref, fuh_ref)):
            scr[0:HR, :] = jnp.where(c > 0, xh_ref[...], 0.0)
            scr[HR:HR + strip, :] = x_ref[0:strip, :]
        wg = [wg_ref[k:k + 1, :] for k in range(FFN_K)]
        wu = [wu_ref[k:k + 1, :] for k in range(FFN_K)]
        bg, bu = bg_ref[...], bu_ref[...]

        def conv(src, base, w, b):
            acc = b
            for k in range(FFN_K):
                o = base - (FFN_K - 1) + k
                acc = acc + w[k] * src[o:o + strip, :]
            return acc

        for s in range(T // strip):
            r = s * strip
            gate = conv(hg_scr, HR, wg, bg) if s == 0 else conv(fg_ref, r, wg, bg)
            up = conv(hu_scr, HR, wu, bu) if s == 0 else conv(fu_ref, r, wu, bu)
            gc_ref[r:r + strip, :] = gate.astype(BF16)
            uc_ref[r:r + strip, :] = up.astype(BF16)
            a_ref[r:r + strip, :] = (gate * _sig(gate) * up).astype(BF16)

    cur, before, _, wspec = _ffn_specs()
    pre = jax.ShapeDtypeStruct((NPAIR, L, UPW), BF16)
    return pl.pallas_call(
        body, name=name, out_shape=(jax.ShapeDtypeStruct((NPAIR, L, UPW), BF16), pre, pre), grid=(NPAIR, NB),
        in_specs=[cur(0), before(0), cur(NPAIR), before(NPAIR), wspec(0, FFN_K), wspec(NPAIR, FFN_K),
                  wspec(0, 1), wspec(NPAIR, 1)],
        out_specs=(cur(0), cur(0), cur(0)),
        scratch_shapes=[pltpu.VMEM((HR + strip, UPW), F32), pltpu.VMEM((HR + strip, UPW), F32)])(
            f, f, f, f, w, w, b, b)


def ffn_act_bwd(f, gc, uc, da, w, name):
    ext = T + HR
    sub = 8

    def body(fg_ref, fu_ref, gc_ref, gca_ref, uc_ref, uca_ref, da_ref, daa_ref, wg_ref, wu_ref,
             dfg_ref, dfu_ref, dwg_ref, dwu_ref, dbg_ref, dbu_ref, dg_scr, du_scr):
        c = pl.program_id(1)
        wg = [wg_ref[k:k + 1, :] for k in range(FFN_K)]
        wu = [wu_ref[k:k + 1, :] for k in range(FFN_K)]

        for r in range(0, ext, HR16):
            rows = min(HR16, ext - r)
            if r < T:
                gate, up, dav = gc_ref[r:r + HR16, :].astype(F32), uc_ref[r:r + HR16, :].astype(F32), da_ref[r:r + HR16, :]
            else:
                gate, up = gca_ref[...].astype(F32)[0:rows], uca_ref[...].astype(F32)[0:rows]
                dav = jnp.where(c < NB - 1, daa_ref[...], 0.0)
            sg = _sig(gate)
            dg_scr[r:r + rows, :] = dav * up * (sg * (1.0 + gate * (1.0 - sg)))
            du_scr[r:r + rows, :] = dav * gate * sg

        def back(d_scr, x_ref, w, df_ref, dw_ref, db_ref):
            accs = [jnp.zeros((sub, UPW), F32) for _ in range(FFN_K + 1)]
            for s in range(T // (2 * sub)):
                pieces = []
                for r in (2 * s * sub, (2 * s + 1) * sub):
                    xv = x_ref[r:r + sub, :]
                    df = jnp.zeros((sub, UPW), F32)
                    for k in range(FFN_K):
                        off = FFN_K - 1 - k
                        dj = d_scr[r + off:r + off + sub, :]
                        df = df + w[k] * dj
                        accs[k] = accs[k] + xv * dj
                        if off == 0:
                            accs[FFN_K] = accs[FFN_K] + dj
                    pieces.append(df)
                df_ref[2 * s * sub:2 * (s + 1) * sub, :] = jnp.concatenate(pieces, axis=0).astype(BF16)
            dwp = _stack_rows([jnp.sum(a, axis=0, keepdims=True) for a in accs[:FFN_K]])
            dbp = jnp.sum(accs[FFN_K], axis=0, keepdims=True)

            @pl.when(c == 0)
            def _():
                dw_ref[...] = dwp
                db_ref[...] = dbp

            @pl.when(c > 0)
            def _():
                dw_ref[...] += dwp
                db_ref[...] += dbp

        back(dg_scr, fg_ref, wg, dfg_ref, dwg_ref, dbg_ref)
        back(du_scr, fu_ref, wu, dfu_ref, dwu_ref, dbu_ref)

    cur, _, after, wspec = _ffn_specs()
    slab = jax.ShapeDtypeStruct((NPAIR, L, UPW), BF16)
    wsh = jax.ShapeDtypeStruct((NPAIR, FFN_K, UPW), F32)
    bsh = jax.ShapeDtypeStruct((NPAIR, 1, UPW), F32)
    return pl.pallas_call(
        body, name=name, out_shape=(slab, slab, wsh, wsh, bsh, bsh), grid=(NPAIR, NB),
        in_specs=[cur(0), cur(NPAIR), cur(0), _ffn_after16(), cur(0), _ffn_after16(), cur(0), after(0),
                  wspec(0, FFN_K), wspec(NPAIR, FFN_K)],
        out_specs=(cur(0), cur(0), wspec(0, FFN_K), wspec(0, FFN_K), wspec(0, 1), wspec(0, 1)),
        scratch_shapes=[pltpu.VMEM((ext, UPW), F32), pltpu.VMEM((ext, UPW), F32)])(
            f, f, gc, gc, uc, uc, da, da, w, w)


def _mesh_pos():
    return lax.axis_index("x"), lax.axis_index("y"), lax.axis_index("c")


def _peer(pos, k):
    x, y, c = pos
    px = 1 - x if k & 4 else x
    py = 1 - y if k & 2 else y
    pc = 1 - c if k & 1 else c
    return (px, py, pc), 4 * px + 2 * py + pc


_CHIP_FLIPS = (4, 2, 6)
_HBM = pl.BlockSpec(memory_space=pl.ANY)


def all_gather(shards, name):
    nt = len(shards)

    def body(*refs):
        ins, outs = refs[:nt], refs[nt:2 * nt]
        send, recv, local = refs[2 * nt:]
        pos = _mesh_pos()
        me = 4 * pos[0] + 2 * pos[1] + pos[2]
        sib, sib_id = _peer(pos, 1)

        def copy(t, k, block_id, to, src=None):
            dst = outs[t].at[block_id]
            return pltpu.make_async_remote_copy(
                src_ref=dst if src is None else src, dst_ref=dst, send_sem=send.at[t, k], recv_sem=recv.at[t, k],
                device_id=to, device_id_type=MESH)

        locals_ = [pltpu.make_async_copy(ins[t], outs[t].at[me], local.at[t]) for t in range(nt)]
        for cp in locals_:
            cp.start()
        started = []
        for j, flip in enumerate(_CHIP_FLIPS):
            for t in range(nt):
                started.append(copy(t, 1 + j, me, _peer(pos, flip)[0], src=ins[t]))
        for t in range(nt):
            started.append(copy(t, 0, me, sib, src=ins[t]))
        for cp in started:
            cp.start()
        for j, flip in enumerate(_CHIP_FLIPS):
            _, pid = _peer(pos, flip)
            for t in range(nt):
                copy(t, 1 + j, pid, sib).wait_recv()
                fwd = copy(t, 4 + j, pid, sib)
                fwd.start()
                started.append(fwd)
        for t in range(nt):
            copy(t, 0, sib_id, sib).wait_recv()
        for j, flip in enumerate(_CHIP_FLIPS):
            _, pid = _peer(pos, flip | 1)
            for t in range(nt):
                copy(t, 4 + j, pid, sib).wait_recv()
        for cp in started:
            cp.wait_send()
        for cp in locals_:
            cp.wait()

    return pl.pallas_call(
        body, name=name,
        out_shape=tuple(jax.ShapeDtypeStruct((NDEV,) + s.shape, s.dtype) for s in shards),
        in_specs=[_HBM] * nt, out_specs=tuple([_HBM] * nt),
        scratch_shapes=[pltpu.SemaphoreType.DMA((nt, NDEV - 1)), pltpu.SemaphoreType.DMA((nt, NDEV - 1)),
                        pltpu.SemaphoreType.DMA((nt,))])(*shards)


_SPLIT = dict(has_side_effects=pltpu.SideEffectType.DATAFLOW_SIDE_EFFECTING)
_SEM = pl.BlockSpec(memory_space=pltpu.SEMAPHORE)


def _exchange_start(name, srcs, lands, copies, after):
    ns, nl = len(srcs), len(lands)
    ncopy = len(copies(None, [None] * ns, [None] * nl))

    def body(*refs):
        src_refs, land_refs = refs[:ns], refs[ns:ns + nl]
        send, recv = refs[ns + nl + len(after):ns + nl + len(after) + 2]
        token = refs[-1]
        for i, (src, dst, peer) in enumerate(copies(_mesh_pos(), src_refs, land_refs)):
            pltpu.make_async_remote_copy(src_ref=src, dst_ref=dst, send_sem=send.at[i], recv_sem=recv.at[i],
                                         device_id=peer, device_id_type=MESH).start()
        token[...] = jnp.zeros_like(token)

    hbm = lambda a: pltpu.HBM(a.shape, a.dtype)
    out = pl.pallas_call(
        body, name=name,
        out_shape=(pltpu.SemaphoreType.DMA((ncopy,)), pltpu.SemaphoreType.DMA((ncopy,)),
                   *[hbm(a) for a in srcs], *[hbm(a) for a in lands], jax.ShapeDtypeStruct((8, 128), F32)),
        in_specs=[pl.BlockSpec(memory_space=pltpu.HBM)] * (ns + nl) + [_HBM] * len(after),
        out_specs=(_SEM, _SEM, *[pl.BlockSpec(memory_space=pltpu.HBM)] * (ns + nl),
                   pl.BlockSpec(memory_space=pltpu.VMEM)),
        input_output_aliases={i: 2 + i for i in range(ns + nl)},
        compiler_params=pltpu.CompilerParams(**_SPLIT))(
            *[pltpu.with_memory_space_constraint(a, pltpu.HBM) for a in list(srcs) + list(lands)], *after)
    return out[0], out[1], list(out[2:2 + ns]), list(out[2 + ns:2 + ns + nl]), out[-1]


def _exchange_wait(name, send, recv, srcs, lands, copies, after):
    ns, nl = len(srcs), len(lands)

    def body(*refs):
        src_refs, land_refs = refs[:ns], refs[ns:ns + nl]
        send_ref, recv_ref = refs[ns + nl:ns + nl + 2]
        pos = _mesh_pos()
        for i, (src, dst, peer) in enumerate(copies(pos, src_refs, land_refs, arriving=True)):
            cp = pltpu.make_async_remote_copy(src_ref=src, dst_ref=dst, send_sem=send_ref.at[i], recv_sem=recv_ref.at[i],
                                              device_id=peer, device_id_type=MESH)
            cp.wait_send()
            cp.wait_recv()

    hbm = lambda a: pltpu.HBM(a.shape, a.dtype)
    out = pl.pallas_call(
        body, name=name, out_shape=tuple(hbm(a) for a in list(srcs) + list(lands)),
        in_specs=[pl.BlockSpec(memory_space=pltpu.HBM)] * (ns + nl) + [_SEM, _SEM] + [_HBM] * len(after),
        out_specs=tuple([pl.BlockSpec(memory_space=pltpu.HBM)] * (ns + nl)),
        input_output_aliases={i: i for i in range(ns + nl)},
        compiler_params=pltpu.CompilerParams(**_SPLIT))(*srcs, *lands, send, recv, *after)
    return list(out[:ns]), list(out[ns:])


def _gather_copies(pos, srcs, lands, arriving=False):
    if pos is None:
        return [None] * (len(srcs) * (NDEV - 1))
    me = 4 * pos[0] + 2 * pos[1] + pos[2]
    out = []
    for src, land in zip(srcs, lands):
        for k in range(1, NDEV):
            peer, pid = _peer(pos, k)
            out.append((src, land.at[pid if arriving else me], peer))
    return out


def _scatter_copies(pos, srcs, lands, arriving=False):
    if pos is None:
        return [None] * (len(srcs) * (NDEV - 1))
    out = []
    for src, land in zip(srcs, lands):
        for k in range(1, NDEV):
            peer, pid = _peer(pos, k)
            out.append((src.at[pid], land.at[k - 1], peer))
    return out


def gather_start(name, shards, lands, after=()):
    return _exchange_start(name, shards, lands, _gather_copies, list(after))


def gather_wait(name, handle, after=()):
    send, recv, srcs, lands, _ = handle
    return _exchange_wait(name, send, recv, srcs, lands, _gather_copies, list(after))


def scatter_start(name, grads, after=()):
    lands = [pltpu.with_memory_space_constraint(lax.empty((NDEV - 1,) + g.shape[1:], g.dtype), pltpu.HBM)
             for g in grads]
    return _exchange_start(name, grads, lands, _scatter_copies, list(after))


def scatter_wait(name, handle, after=()):
    send, recv, srcs, lands, _ = handle
    return _exchange_wait(name, send, recv, srcs, lands, _scatter_copies, list(after))


def sum_slots(parts, name):
    def body(p_ref, o_ref):
        acc = p_ref[0]
        for d in range(1, NDEV):
            acc = acc + p_ref[d]
        o_ref[...] = acc

    return pl.pallas_call(body, name=name, out_shape=jax.ShapeDtypeStruct(parts.shape[1:], F32))(parts)


def _adamw_update(g, w_ref, m_ref, v_ref, g_ref, d_ref, nm_ref, nv_ref):
    g_ref[...] = g
    nm = ADAM_B1 * m_ref[...] + (1.0 - ADAM_B1) * g
    nv = ADAM_B2 * v_ref[...] + (1.0 - ADAM_B2) * (g * g)
    nm_ref[...] = nm
    nv_ref[...] = nv
    m_hat = nm / (1.0 - ADAM_B1 ** ADAM_STEP)
    v_hat = nv / (1.0 - ADAM_B2 ** ADAM_STEP)
    d_ref[...] = -ADAM_LR * (m_hat / (jnp.sqrt(v_hat) + ADAM_EPS) + ADAM_WD * w_ref[...])


def adamw_big(own, lands, w, m, v, me, tr, name):
    _, r, c = own[0].shape
    nt = r // tr

    def body(me_ref, *refs):
        ins, (w_ref, m_ref, v_ref), outs = refs[:2 * DEPTH], refs[2 * DEPTH:2 * DEPTH + 3], refs[2 * DEPTH + 3:]
        layer = pl.program_id(0)
        for l in range(DEPTH):
            @pl.when(layer == l)
            def _(l=l):
                g = ins[2 * l][...].astype(F32)
                for s in range(NDEV - 1):
                    g = g + ins[2 * l + 1][s].astype(F32)
                _adamw_update(g, w_ref, m_ref, v_ref, *outs)

    pick = lambda l: (lambda layer, i, me_ref: jnp.where(layer == l, i, 0))
    in_specs = []
    for l in range(DEPTH):
        in_specs.append(pl.BlockSpec((None, tr, c), lambda layer, i, me_ref, f=pick(l): (me_ref[0], f(layer, i, me_ref), 0)))
        in_specs.append(pl.BlockSpec((NDEV - 1, tr, c), lambda layer, i, me_ref, f=pick(l): (0, f(layer, i, me_ref), 0)))
    blk = pl.BlockSpec((tr, c), lambda layer, i, me_ref: (layer * nt + i, 0))
    sh = jax.ShapeDtypeStruct((DEPTH * r, c), F32)
    args = [a for l in range(DEPTH) for a in (own[l], lands[l])]
    return pl.pallas_call(
        body, name=name, out_shape=(sh, sh, sh, sh),
        grid_spec=pltpu.PrefetchScalarGridSpec(
            num_scalar_prefetch=1, grid=(DEPTH, nt), in_specs=in_specs + [blk, blk, blk],
            out_specs=(blk, blk, blk, blk)),
        compiler_params=_params(VMEM_BIG))(me, *args, w, m, v)


def adamw_many(gs, ws, ms, vs, name):
    n = len(gs)

    def body(*refs):
        outs = refs[4 * n:]
        for i in range(n):
            _adamw_update(refs[i][...], refs[n + i], refs[2 * n + i], refs[3 * n + i], *outs[4 * i:4 * i + 4])

    out_shape = tuple(jax.ShapeDtypeStruct(w.shape, F32) for w in ws for _ in range(4))
    res = pl.pallas_call(body, name=name, out_shape=out_shape)(*gs, *ws, *ms, *vs)
    return [res[4 * i:4 * i + 4] for i in range(n)]


def local_step(x, target, P, hooks):
    tb = _ret_tables()
    h = jnp.concatenate([jnp.zeros((PAD, D), F32), P["meta"], x], axis=0)
    stash = []
    for l in range(DEPTH):
        s = {"h": h}
        s["w_in"], s["cv_pw"], s["w_out"] = hooks["mixer_w"](l, h)
        proj, s["u"] = in_proj_fwd(f"in_proj_{l}", h, P["norm_mix_g"][l], s["w_in"], 256,
                                   after=hooks["first_after"] if l == 0 else None)
        s["proj"] = proj
        s["o"] = attn_fwd(proj, P["q_norm_g"][l], P["k_norm_g"][l], P["attn_sinks"][l], f"attn_fwd_{l}")
        y_cv, s["xc"] = conv_fwd(proj, P["cv_dw_w"][l], P["cv_dw_b"][l], P["cv_ln_g"][l], P["cv_ln_b"][l],
                                 s["cv_pw"], P["cv_out_g"][l], f"conv_fwd_{l}")
        y_ret, s["states"] = ret_fwd(proj, P["ret_gn_g"][l], tb, f"ret_fwd_{l}")
        h, s["ycat"] = out_proj_fwd(f"out_proj_{l}", s["o"], P["attn_out_g"][l], y_cv, y_ret, s["w_out"], h, 512)
        s["h1"] = h
        s["ffn_up"], s["ffn_down"] = hooks["ffn_w"](l, h)
        s["f"], s["u2"] = ffn_up_fwd(f"ffn_up_{l}", h, P["norm_ffn_g"][l], s["ffn_up"])
        a, s["gc"], s["uc"] = ffn_act_fwd(s["f"], P["ffn_dw_w"][l], P["ffn_dw_b"][l], f"ffn_act_fwd_{l}")
        s["a"] = a
        if l < DEPTH - 1:
            h = ffn_down_fwd(f"ffn_down_{l}", a, s["ffn_down"], h)
        else:
            dh, dhm, loss = ffn_down_loss(f"ffn_down_{l}", a, s["ffn_down"], h, target)
        stash.append(s)

    G = {n: [None] * DEPTH for n in ("norm_mix_g", "q_norm_g", "k_norm_g", "attn_sinks", "attn_out_g", "cv_dw_w",
                                     "cv_dw_b", "cv_ln_g", "cv_ln_b", "cv_out_g", "ret_gn_g", "norm_ffn_g",
                                     "ffn_dw_w", "ffn_dw_b")}
    for l in reversed(range(DEPTH)):
        s = stash[l]
        da = ffn_down_dx(f"ffn_down_dx_{l}", dhm, s["ffn_down"])
        dw_down = ffn_down_dw(f"ffn_down_dw_{l}", s["a"], dhm)
        dfg, dfu, dwg, dwu, dbg, dbu = ffn_act_bwd(s["f"], s["gc"], s["uc"], da, P["ffn_dw_w"][l], f"ffn_act_bwd_{l}")
        G["ffn_dw_w"][l] = jnp.concatenate([dwg, dwu], axis=0)
        G["ffn_dw_b"][l] = jnp.concatenate([dbg, dbu], axis=0)
        du2 = ffn_up_dx(f"ffn_up_dx_{l}", dfg, dfu, s["ffn_up"])
        dw_up = ffn_up_dw(f"ffn_up_dw_{l}", s["u2"], dfg, dfu)
        after = hooks["ffn_grads"](l, dw_down, dw_up)
        dh, dhm, G["norm_ffn_g"][l] = rms_bwd(s["h1"], P["norm_ffn_g"][l], du2, dh, f"rms_ffn_bwd_{l}", after=after)
        dycat = mm_nt(f"out_proj_dx_{l}", dhm, s["w_out"], 512)
        dw_out = mm_ta(f"out_proj_dw_{l}", s["ycat"], dhm, 512)
        (dq, dk_c, dk_p, dk_m, dv_c, dv_p, dv_m, G["q_norm_g"][l], G["k_norm_g"][l], dsk,
         G["attn_out_g"][l]) = attn_bwd(s["proj"], s["o"], dycat, P["attn_out_g"][l], P["q_norm_g"][l],
                                        P["k_norm_g"][l], P["attn_sinks"][l], f"attn_bwd_{l}")
        G["attn_sinks"][l] = dsk[:, 0]
        shift = lambda z: jnp.concatenate([z[T:], jnp.zeros((T, NKV * HD), F32)], axis=0)
        dk = (dk_c + shift(dk_p)).at[:T].add(dk_m)
        dv = (dv_c + shift(dv_p)).at[:T].add(dv_m)
        (dca, dcb, G["cv_dw_w"][l], G["cv_dw_b"][l], G["cv_ln_g"][l], G["cv_ln_b"][l], G["cv_out_g"][l],
         dpw) = conv_bwd(s["proj"], s["xc"], dycat, P["cv_dw_w"][l], P["cv_ln_g"][l], P["cv_ln_b"][l], s["cv_pw"],
                         P["cv_out_g"][l], f"conv_bwd_{l}")
        dret, G["ret_gn_g"][l] = ret_bwd(s["proj"], dycat, s["states"], P["ret_gn_g"][l], tb, f"ret_bwd_{l}")
        dproj = jnp.concatenate([dq, dk.astype(BF16), dv.astype(BF16), dca, dcb, dret], axis=1)
        du = mm_nn(f"in_proj_dx_{l}", dproj, s["w_in"], 512)
        dw_in = mm_ta(f"in_proj_dw_{l}", dproj, s["u"], 768)
        after = hooks["mixer_grads"](l, dw_out, dpw, dw_in)
        dh, dhm, G["norm_mix_g"][l] = rms_bwd(s["h"], P["norm_mix_g"][l], du, dh, f"rms_mix_bwd_{l}", after=after,
                                              last=(l == 0))
    return loss[0, 0], dh, dhm[PAD:T], G


_SMALL = ("meta", "norm_mix_g", "q_norm_g", "k_norm_g", "attn_sinks", "attn_out_g", "cv_dw_w", "cv_dw_b", "cv_ln_g",
          "cv_ln_b", "cv_out_g", "ret_gn_g", "norm_ffn_g", "ffn_dw_w", "ffn_dw_b")
_BIG = ("w_in", "cv_pw", "w_out", "ffn_up", "ffn_down")
_TRANSPOSED = ("w_in", "ffn_up")
_ORDER = ("meta", "norm_mix_g", "w_in", "q_norm_g", "k_norm_g", "attn_sinks", "attn_out_g", "cv_dw_w", "cv_dw_b",
          "cv_ln_g", "cv_ln_b", "cv_pw", "cv_out_g", "ret_gn_g", "w_out", "norm_ffn_g", "ffn_up", "ffn_dw_w",
          "ffn_dw_b", "ffn_down")
_SMALL_SHARDED = {"meta": D, "cv_dw_w": CV, "ffn_dw_w": 2 * D_FF}


def _pack(arrs):
    flat = jnp.concatenate([a.reshape(-1) for a in arrs])
    n = flat.shape[0]
    rows = -(-n // 1024) * 8
    return jnp.pad(flat, (0, rows * 128 - n)).reshape(rows, 128)


def _unpack(packed, shapes):
    flat = packed.reshape(-1)
    out, off = [], 0
    for s in shapes:
        n = int(np.prod(s))
        out.append(flat[off:off + n].reshape(s))
        off += n
    return out


def kernel(x, meta, norm_mix_g, w_in, q_norm_g, k_norm_g, attn_sinks, attn_out_g, cv_dw_w, cv_dw_b, cv_ln_g, cv_ln_b, cv_pw, cv_out_g, ret_gn_g, w_out, norm_ffn_g, ffn_up, ffn_dw_w, ffn_dw_b, ffn_down, loss_target, m_meta, m_norm_mix_g, m_w_in, m_q_norm_g, m_k_norm_g, m_attn_sinks, m_attn_out_g, m_cv_dw_w, m_cv_dw_b, m_cv_ln_g, m_cv_ln_b, m_cv_pw, m_cv_out_g, m_ret_gn_g, m_w_out, m_norm_ffn_g, m_ffn_up, m_ffn_dw_w, m_ffn_dw_b, m_ffn_down, v_meta, v_norm_mix_g, v_w_in, v_q_norm_g, v_k_norm_g, v_attn_sinks, v_attn_out_g, v_cv_dw_w, v_cv_dw_b, v_cv_ln_g, v_cv_ln_b, v_cv_pw, v_cv_out_g, v_ret_gn_g, v_w_out, v_norm_ffn_g, v_ffn_up, v_ffn_dw_w, v_ffn_dw_b, v_ffn_down):
    W = dict(meta=meta, norm_mix_g=norm_mix_g, w_in=w_in, q_norm_g=q_norm_g, k_norm_g=k_norm_g,
             attn_sinks=attn_sinks, attn_out_g=attn_out_g, cv_dw_w=cv_dw_w, cv_dw_b=cv_dw_b, cv_ln_g=cv_ln_g,
             cv_ln_b=cv_ln_b, cv_pw=cv_pw, cv_out_g=cv_out_g, ret_gn_g=ret_gn_g, w_out=w_out,
             norm_ffn_g=norm_ffn_g, ffn_up=ffn_up, ffn_dw_w=ffn_dw_w, ffn_dw_b=ffn_dw_b, ffn_down=ffn_down)
    M = dict(meta=m_meta, norm_mix_g=m_norm_mix_g, w_in=m_w_in, q_norm_g=m_q_norm_g, k_norm_g=m_k_norm_g,
             attn_sinks=m_attn_sinks, attn_out_g=m_attn_out_g, cv_dw_w=m_cv_dw_w, cv_dw_b=m_cv_dw_b,
             cv_ln_g=m_cv_ln_g, cv_ln_b=m_cv_ln_b, cv_pw=m_cv_pw, cv_out_g=m_cv_out_g, ret_gn_g=m_ret_gn_g,
             w_out=m_w_out, norm_ffn_g=m_norm_ffn_g, ffn_up=m_ffn_up, ffn_dw_w=m_ffn_dw_w, ffn_dw_b=m_ffn_dw_b,
             ffn_down=m_ffn_down)
    V = dict(meta=v_meta, norm_mix_g=v_norm_mix_g, w_in=v_w_in, q_norm_g=v_q_norm_g, k_norm_g=v_k_norm_g,
             attn_sinks=v_attn_sinks, attn_out_g=v_attn_out_g, cv_dw_w=v_cv_dw_w, cv_dw_b=v_cv_dw_b,
             cv_ln_g=v_cv_ln_g, cv_ln_b=v_cv_ln_b, cv_pw=v_cv_pw, cv_out_g=v_cv_out_g, ret_gn_g=v_ret_gn_g,
             w_out=v_w_out, norm_ffn_g=v_norm_ffn_g, ffn_up=v_ffn_up, ffn_dw_w=v_ffn_dw_w, ffn_dw_b=v_ffn_dw_b,
             ffn_down=v_ffn_down)
    me = 4 * lax.axis_index("x") + 2 * lax.axis_index("y") + lax.axis_index("c")
    for n in _TRANSPOSED:
        W[n], M[n], V[n] = (a.transpose(0, 2, 1) for a in (W[n], M[n], V[n]))

    sh = {n: [W[n][l].astype(BF16) for l in range(DEPTH)] for n in _BIG}
    mix = lambda l: [sh["w_in"][l], sh["cv_pw"][l], sh["w_out"][l]]
    ffn = lambda l: [sh["ffn_up"][l], sh["ffn_down"][l]]
    first = all_gather(mix(0) + [meta, cv_dw_w, ffn_dw_w], "gather_first")
    g_meta, g_cdw, g_fdw = first[3:6]

    def landing(shards):
        return [lax.dynamic_update_slice(lax.empty((NDEV,) + s.shape, s.dtype), s[None], (me,) + (0,) * s.ndim)
                for s in shards]

    gathers = {("ffn", 0): gather_start("gather_ffn0_start", ffn(0), landing(ffn(0)), after=[first[0]])}
    gathers["mix", 1] = gather_start("gather_mix1_start", mix(1), landing(mix(1)), after=[gathers["ffn", 0][4]])
    gathers["ffn", 1] = gather_start("gather_ffn1_start", ffn(1), landing(ffn(1)), after=[gathers["mix", 1][4]])

    def mixer_w(l, h):
        g_in, g_pw, g_out = first[0:3] if l == 0 else gather_wait(f"gather_mix{l}_wait", gathers["mix", l], after=[h])[1]
        return g_in.reshape(IN_W, D), g_pw.reshape(CV, CV), g_out.reshape(D, D)

    def ffn_w(l, h1):
        return gather_wait(f"gather_ffn{l}_wait", gathers["ffn", l], after=[h1])[1]

    scatters = {}

    def ffn_grads(l, dw_down, dw_up):
        scatters["ffn", l] = scatter_start(f"scatter_ffn{l}_start", [dw_up, dw_down.reshape(NDEV, DNR, D)])
        return scatters["ffn", l][4]

    def mixer_grads(l, dw_out, dpw, dw_in):
        grads = [dw_in.reshape(NDEV, IN_W // NDEV, D), dpw.astype(BF16).reshape(NDEV, CV // NDEV, CV),
                 dw_out.reshape(NDEV, D // NDEV, D)]
        scatters["mix", l] = scatter_start(f"scatter_mix{l}_start", grads)
        return scatters["mix", l][4]

    P = dict(
        meta=g_meta.transpose(1, 0, 2).reshape(N_META, D),
        cv_dw_w=g_cdw.transpose(1, 2, 0, 3).reshape(DEPTH, CONV_K, CV),
        ffn_dw_w=g_fdw.transpose(1, 0, 2, 3),
        ffn_dw_b=ffn_dw_b.reshape(DEPTH, NDEV, 1, UPW),
        attn_sinks=attn_sinks,
    )
    for n in ("norm_mix_g", "q_norm_g", "k_norm_g", "attn_out_g", "cv_dw_b", "cv_ln_g", "cv_ln_b", "cv_out_g",
              "ret_gn_g", "norm_ffn_g"):
        P[n] = W[n].reshape(DEPTH, 1, -1)

    hooks = dict(mixer_w=mixer_w, ffn_w=ffn_w, ffn_grads=ffn_grads, mixer_grads=mixer_grads,
                 first_after=gathers["ffn", 1][4])
    loss_part, dx, dmeta, G = local_step(x[0], loss_target[0], P, hooks)

    small_full = {
        "meta": dmeta,
        "cv_dw_w": jnp.stack(G["cv_dw_w"]),
        "ffn_dw_w": jnp.stack([g.transpose(1, 0, 2).reshape(FFN_K, 2 * D_FF) for g in G["ffn_dw_w"]]),
        "ffn_dw_b": jnp.stack([g.reshape(2 * D_FF) for g in G["ffn_dw_b"]]),
        "attn_sinks": jnp.stack(G["attn_sinks"]),
    }
    for n in _SMALL:
        if n not in small_full:
            small_full[n] = jnp.stack([g.reshape(-1) for g in G[n]])
    shapes = [small_full[n].shape for n in _SMALL] + [(1,)]
    packed = _pack([small_full[n] for n in _SMALL] + [loss_part.reshape(1)])
    small_exchange = gather_start("reduce_small_start", [packed], landing([packed]), after=[dx])

    out = {}
    tiles = {"w_in": 144, "cv_pw": 32, "w_out": 128, "ffn_up": 176, "ffn_down": 176}
    me1 = me.astype(jnp.int32).reshape(1)
    own, lands = {n: [None] * DEPTH for n in _BIG}, {n: [None] * DEPTH for n in _BIG}

    def arrived(kind, names, after):
        for l in range(DEPTH):
            srcs, got = scatter_wait(f"scatter_{kind}{l}_wait", scatters[kind, l], after=after)
            for n, s_, g_ in zip(names, srcs, got):
                own[n][l], lands[n][l] = s_, g_

    def update(names):
        for n in names:
            shard = W[n].shape
            rows, cols = shard[0] * shard[1], shard[2]
            res = adamw_big(own[n], lands[n], W[n].reshape(rows, cols), M[n].reshape(rows, cols),
                            V[n].reshape(rows, cols), me1, tiles[n], f"adamw_{n}")
            out[n] = [r.reshape(shard) for r in res]

    arrived("ffn", ("ffn_up", "ffn_down"), [small_exchange[4]])
    update(("ffn_up", "ffn_down"))
    slots = gather_wait("reduce_small_wait", small_exchange, after=[out["ffn_down"][0]])[1][0]
    summed = _unpack(sum_slots(slots, "reduce_small_sum"), shapes)
    loss = summed[-1][0]
    small_g = []
    for n, g in zip(_SMALL, summed):
        if n in _SMALL_SHARDED:
            width = _SMALL_SHARDED[n] // NDEV
            g = lax.dynamic_slice_in_dim(g, me * width, width, axis=g.ndim - 1)
        small_g.append(g)
    flat2 = lambda a: a.reshape(-1, a.shape[-1])
    res = adamw_many([flat2(g) for g in small_g], *[[flat2(X[n]) for n in _SMALL] for X in (W, M, V)], "adamw_small")
    for n, r in zip(_SMALL, res):
        out[n] = [a.reshape(W[n].shape) for a in r]
    arrived("mix", ("w_in", "cv_pw", "w_out"), [out["ffn_down"][0], res[0][0]])
    update(("w_in", "cv_pw", "w_out"))
    for n in _TRANSPOSED:
        out[n] = [r.transpose(0, 2, 1) for r in out[n]]

    return (loss, dx[None], *[out[n][0] for n in _ORDER], *[out[n][1] for n in _ORDER],
            *[out[n][2] for n in _ORDER], *[out[n][3] for n in _ORDER])
```

```python
import numpy as np
import jax
import jax.numpy as jnp
from jax import lax
from jax.experimental import pallas as pl
from jax.experimental.pallas import tpu as pltpu

F32 = jnp.float32
BF16 = jnp.bfloat16

D = 1024
SEQ = 2048
DEPTH = 2
T = 128
L = SEQ + T
NB = L // T
N_META = 16
PAD = T - N_META
HD = 64
NQ = 8
NKV = 2
GQA = NQ // NKV
CV = 256
CONV_K = 31
RH = 4
D_FF = 2816
FFN_K = 3
IN_W = 2304
RMS_EPS = 1e-6
LN_EPS = 1e-5
NEG = -1e30
NDEV = 8
UPW = 2 * D_FF // NDEV
DNR = D_FF // NDEV
NPAIR = NDEV // 2

ADAM_LR, ADAM_B1, ADAM_B2, ADAM_EPS, ADAM_WD, ADAM_STEP = 0.001, 0.9, 0.999, 1e-08, 0.01, 10

VMEM_BIG = 56 * 1024 * 1024

MESH = pl.DeviceIdType.MESH


def _params(vmem=None):
    return pltpu.CompilerParams(vmem_limit_bytes=vmem) if vmem else None


def _dot(a, b, nt=False):
    return lax.dot_general(a, b, (((1,), (1 if nt else 0,)), ((), ())), preferred_element_type=F32)


def _sig(x):
    return 1.0 / (1.0 + jnp.exp(-x))


def _bf(x):
    return x.astype(BF16)


def _stack_rows(rows):
    idx = lax.broadcasted_iota(jnp.int32, (len(rows), rows[0].shape[1]), 0)
    out = jnp.zeros((len(rows), rows[0].shape[1]), F32)
    for r, v in enumerate(rows):
        out = jnp.where(idx == r, v, out)
    return out


def rms_bwd(x, g, dy, dres, name, after=None, last=False):
    n, w = x.shape
    has_res = dres is not None
    deps = [] if after is None else [after]

    def body(x_ref, g_ref, dy_ref, *rest):
        rest = rest[:len(rest) - 3 - len(deps)] + rest[len(rest) - 3:]
        if has_res:
            dres_ref, dx_ref, dxm_ref, dg_ref = rest
        else:
            dx_ref, dxm_ref, dg_ref = rest
        i = pl.program_id(0)
        xv = x_ref[...]
        r = lax.rsqrt(jnp.mean(xv * xv, axis=-1, keepdims=True) + RMS_EPS)
        xh = xv * r
        dyv = dy_ref[...]
        dxh = dyv * g_ref[...]
        dx = r * (dxh - xh * jnp.mean(dxh * xh, axis=-1, keepdims=True))
        if has_res:
            dx = dx + dres_ref[...]
        dx_ref[...] = dx
        if last:
            @pl.when(i == 0)
            def _():
                dxm_ref[...] = dx
        else:
            rows = i * T + lax.broadcasted_iota(jnp.int32, (T, 1), 0)
            dxm_ref[...] = jnp.where(rows >= PAD, dx, 0.0).astype(BF16)
        part = jnp.sum(dyv * xh, axis=0, keepdims=True)

        @pl.when(i == 0)
        def _():
            dg_ref[...] = part

        @pl.when(i > 0)
        def _():
            dg_ref[...] += part

    row = pl.BlockSpec((T, w), lambda i: (i, 0))
    vec = pl.BlockSpec((1, w), lambda i: (0, 0))
    ins = [x, g, dy] + ([dres] if has_res else []) + deps
    if last:
        out_shape = (jax.ShapeDtypeStruct((n - T, w), F32), jax.ShapeDtypeStruct((T, w), F32))
        out_specs = (pl.BlockSpec((T, w), lambda i: (jnp.maximum(i - 1, 0), 0)), pl.BlockSpec((T, w), lambda i: (0, 0)))
    else:
        out_shape = (jax.ShapeDtypeStruct((n, w), F32), jax.ShapeDtypeStruct((n, w), BF16))
        out_specs = (row, row)
    return pl.pallas_call(
        body, name=name, out_shape=(*out_shape, jax.ShapeDtypeStruct((1, w), F32)), grid=(n // T,),
        in_specs=[row, vec, row] + ([row] if has_res else [])
        + [_HBM] * len(deps),
        out_specs=(*out_specs, vec))(*ins)


def _mm(name, a, b, *, grid, a_spec, b_spec, o_spec, out_shape, nt=False, ta=False, red=False, res=None,
        res_spec=None):
    def body(a_ref, b_ref, *rest):
        o_ref = rest[-1]
        av = a_ref[...]
        bv = b_ref[...]
        if bv.ndim == 3:
            bv = bv.reshape(bv.shape[0] * bv.shape[1], bv.shape[2])
        if ta:
            acc = lax.dot_general(av, bv, (((0,), (0,)), ((), ())), preferred_element_type=F32)
        else:
            acc = _dot(av, bv, nt)
        if red:
            k = pl.program_id(0)

            @pl.when(k == 0)
            def _():
                o_ref[...] = acc

            @pl.when(k > 0)
            def _():
                o_ref[...] += acc
        else:
            if res is not None:
                rows = lax.broadcasted_iota(jnp.int32, (acc.shape[0], 1), 0)
                acc = rest[0][...] + jnp.where(rows >= PAD, acc, 0.0)
            o_ref[...] = acc.astype(o_ref.dtype)

    ins = [a, b] + ([res] if res is not None else [])
    specs = [a_spec, b_spec] + ([res_spec] if res is not None else [])
    return pl.pallas_call(body, name=name, out_shape=out_shape, grid=grid, in_specs=specs, out_specs=o_spec,
                          compiler_params=_params(VMEM_BIG))(*ins)


def mm_nn(name, a, b, tn, out_dtype=F32, res=None):
    m, k = a.shape
    n = b.shape[1]
    return _mm(name, a, b, grid=(n // tn,),
               a_spec=pl.BlockSpec((m, k), lambda j: (0, 0)), b_spec=pl.BlockSpec((k, tn), lambda j: (0, j)),
               o_spec=pl.BlockSpec((m, tn), lambda j: (0, j)), out_shape=jax.ShapeDtypeStruct((m, n), out_dtype),
               res=res, res_spec=pl.BlockSpec((m, tn), lambda j: (0, j)))


def mm_nt(name, a, b, tn):
    m, k = a.shape
    n = b.shape[0]
    return _mm(name, a, b, grid=(n // tn,), nt=True,
               a_spec=pl.BlockSpec((m, k), lambda j: (0, 0)), b_spec=pl.BlockSpec((tn, k), lambda j: (j, 0)),
               o_spec=pl.BlockSpec((m, tn), lambda j: (0, j)), out_shape=jax.ShapeDtypeStruct((m, n), F32))


def mm_ta(name, a, b, tm, out_dtype=BF16):
    k, m = a.shape
    n = b.shape[1]
    return _mm(name, a, b, grid=(m // tm,), ta=True,
               a_spec=pl.BlockSpec((k, tm), lambda j: (0, j)), b_spec=pl.BlockSpec((k, n), lambda j: (0, 0)),
               o_spec=pl.BlockSpec((tm, n), lambda j: (j, 0)), out_shape=jax.ShapeDtypeStruct((m, n), out_dtype))


def _rms_mm(name, h, g, b, *, grid, b_spec, o_spec, out_shape, after=None):
    deps = [] if after is None else [after]

    def body(h_ref, g_ref, b_ref, *rest):
        o_ref, u_ref = rest[len(deps):]

        @pl.when(pl.program_id(0) == 0)
        def _():
            gv = g_ref[...]
            for c in range(NB):
                rows = slice(c * T, (c + 1) * T)
                xv = h_ref[rows, :]
                r = lax.rsqrt(jnp.mean(xv * xv, axis=-1, keepdims=True) + RMS_EPS)
                u_ref[rows, :] = (xv * r * gv).astype(BF16)

        o_ref[...] = _dot(u_ref[...], b_ref[...], nt=True)

    whole = lambda r: pl.BlockSpec((r, D), lambda j: (0, 0))
    return pl.pallas_call(
        body, name=name, out_shape=(out_shape, jax.ShapeDtypeStruct((L, D), BF16)), grid=grid,
        in_specs=[whole(L), whole(1), b_spec] + [_HBM] * len(deps), out_specs=(o_spec, whole(L)),
        compiler_params=_params(VMEM_BIG))(h, g, b, *deps)


def out_proj_fwd(name, o, og, y_cv, y_ret, w, res, tn):
    wa = o.shape[1]

    def body(o_ref, og_ref, cv_ref, ret_ref, w_ref, r_ref, out_ref, ycat_ref):
        @pl.when(pl.program_id(0) == 0)
        def _():
            gv = og_ref[...]
            for c in range(NB):
                rows = slice(c * T, (c + 1) * T)
                ycat_ref[rows, 0:wa] = _rms_rows(o_ref[rows, :], gv)[0].astype(BF16)
                ycat_ref[rows, wa:wa + CV] = cv_ref[rows, :]
                ycat_ref[rows, wa + CV:D] = ret_ref[rows, :]

        acc = _dot(ycat_ref[...], w_ref[...])
        rows = lax.broadcasted_iota(jnp.int32, (L, 1), 0)
        out_ref[...] = r_ref[...] + jnp.where(rows >= PAD, acc, 0.0)

    whole = lambda w_: pl.BlockSpec((L, w_), lambda j: (0, 0))
    tile = pl.BlockSpec((L, tn), lambda j: (0, j))
    return pl.pallas_call(
        body, name=name, out_shape=(jax.ShapeDtypeStruct((L, D), F32), jax.ShapeDtypeStruct((L, D), BF16)),
        grid=(D // tn,),
        in_specs=[whole(wa), pl.BlockSpec((1, wa), lambda j: (0, 0)), whole(CV), whole(RW),
                  pl.BlockSpec((D, tn), lambda j: (0, j)), tile],
        out_specs=(tile, whole(D)), compiler_params=_params(VMEM_BIG))(o, og, y_cv, y_ret, w, res)


def in_proj_fwd(name, h, g, wt, tn, after=None):
    return _rms_mm(name, h, g, wt, grid=(IN_W // tn,), b_spec=pl.BlockSpec((tn, D), lambda j: (j, 0)),
                   o_spec=pl.BlockSpec((L, tn), lambda j: (0, j)), out_shape=jax.ShapeDtypeStruct((L, IN_W), F32),
                   after=after)


def ffn_up_fwd(name, h, g, wupt):
    return _rms_mm(name, h, g, wupt, grid=(NDEV,), b_spec=pl.BlockSpec((None, UPW, D), lambda j: (j, 0, 0)),
                   o_spec=pl.BlockSpec((None, L, UPW), lambda j: (j, 0, 0)),
                   out_shape=jax.ShapeDtypeStruct((NDEV, L, UPW), F32))


def _slab_specs():
    gate = pl.BlockSpec((None, L, UPW), lambda j: (jnp.minimum(j, NPAIR - 1), 0, 0))
    up = pl.BlockSpec((None, L, UPW), lambda j: (jnp.maximum(j - NPAIR, 0), 0, 0))
    return gate, up


def ffn_up_dx(name, dfg, dfu, wupt):
    def body(g_ref, u_ref, b_ref, o_ref):
        j = pl.program_id(0)

        @pl.when(j == 0)
        def _():
            o_ref[...] = _dot(g_ref[...], b_ref[...])

        @pl.when(jnp.logical_and(j > 0, j < NPAIR))
        def _():
            o_ref[...] += _dot(g_ref[...], b_ref[...])

        @pl.when(j >= NPAIR)
        def _():
            o_ref[...] += _dot(u_ref[...], b_ref[...])

    gate, up = _slab_specs()
    return pl.pallas_call(
        body, name=name, out_shape=jax.ShapeDtypeStruct((L, D), F32), grid=(NDEV,),
        in_specs=[gate, up, pl.BlockSpec((None, UPW, D), lambda j: (j, 0, 0))],
        out_specs=pl.BlockSpec((L, D), lambda j: (0, 0)), compiler_params=_params(VMEM_BIG))(dfg, dfu, wupt)


def ffn_up_dw(name, u2, dfg, dfu):
    tdot = lambda a, b: lax.dot_general(a, b, (((0,), (0,)), ((), ())), preferred_element_type=F32)

    def body(a_ref, g_ref, u_ref, o_ref):
        j = pl.program_id(0)

        @pl.when(j < NPAIR)
        def _():
            o_ref[...] = tdot(g_ref[...], a_ref[...]).astype(o_ref.dtype)

        @pl.when(j >= NPAIR)
        def _():
            o_ref[...] = tdot(u_ref[...], a_ref[...]).astype(o_ref.dtype)

    gate, up = _slab_specs()
    return pl.pallas_call(
        body, name=name, out_shape=jax.ShapeDtypeStruct((NDEV, UPW, D), BF16), grid=(NDEV,),
        in_specs=[pl.BlockSpec((L, D), lambda j: (0, 0)), gate, up],
        out_specs=pl.BlockSpec((None, UPW, D), lambda j: (j, 0, 0)), compiler_params=_params(VMEM_BIG))(u2, dfg, dfu)


def _ffn_down_tile(a_ref, b_ref, r_ref, tn):
    acc = jnp.zeros((L, tn), F32)
    for g in range(NPAIR):
        bv = b_ref[2 * g:2 * g + 2]
        acc = acc + _dot(a_ref[g], bv.reshape(2 * DNR, tn))
    rows = lax.broadcasted_iota(jnp.int32, (L, 1), 0)
    return r_ref[...] + jnp.where(rows >= PAD, acc, 0.0)


def _ffn_down_specs(tn):
    return [pl.BlockSpec((NPAIR, L, UPW), lambda j: (0, 0, 0)), pl.BlockSpec((NDEV, DNR, tn), lambda j: (0, 0, j)),
            pl.BlockSpec((L, tn), lambda j: (0, j))]


def ffn_down_fwd(name, a, wdn, res, tn=256):
    def body(a_ref, b_ref, r_ref, o_ref):
        o_ref[...] = _ffn_down_tile(a_ref, b_ref, r_ref, tn)

    return pl.pallas_call(
        body, name=name, out_shape=jax.ShapeDtypeStruct((L, D), F32), grid=(D // tn,),
        in_specs=_ffn_down_specs(tn), out_specs=pl.BlockSpec((L, tn), lambda j: (0, j)),
        compiler_params=_params(VMEM_BIG))(a, wdn, res)


def ffn_down_loss(name, a, wdn, res, target, tn=256):
    def body(a_ref, b_ref, r_ref, t_ref, dh_ref, dhm_ref, loss_ref):
        j = pl.program_id(0)
        hv = _ffn_down_tile(a_ref, b_ref, r_ref, tn)
        e = hv[T:] - t_ref[...]
        dh = e * (1.0 / D)
        dh_ref[0:T, :] = jnp.zeros((T, tn), F32)
        dh_ref[T:, :] = dh
        dhm_ref[0:T, :] = jnp.zeros((T, tn), BF16)
        dhm_ref[T:, :] = dh.astype(BF16)
        part = jnp.sum(jnp.sum(e * e, axis=1, keepdims=True), axis=0, keepdims=True) * (0.5 / D)

        @pl.when(j == 0)
        def _():
            loss_ref[...] = jnp.broadcast_to(part, loss_ref.shape)

        @pl.when(j > 0)
        def _():
            loss_ref[...] += jnp.broadcast_to(part, loss_ref.shape)

    tile = pl.BlockSpec((L, tn), lambda j: (0, j))
    return pl.pallas_call(
        body, name=name,
        out_shape=(jax.ShapeDtypeStruct((L, D), F32), jax.ShapeDtypeStruct((L, D), BF16),
                   jax.ShapeDtypeStruct((8, 128), F32)),
        grid=(D // tn,),
        in_specs=_ffn_down_specs(tn) + [pl.BlockSpec((SEQ, tn), lambda j: (0, j))],
        out_specs=(tile, tile, pl.BlockSpec((8, 128), lambda j: (0, 0))),
        compiler_params=_params(VMEM_BIG))(a, wdn, res, target)


def ffn_down_dx(name, dh, wdn):
    return _mm(name, dh, wdn, grid=(NPAIR,), nt=True,
               a_spec=pl.BlockSpec((L, D), lambda g: (0, 0)),
               b_spec=pl.BlockSpec((2, DNR, D), lambda g: (g, 0, 0)),
               o_spec=pl.BlockSpec((None, L, UPW), lambda g: (g, 0, 0)),
               out_shape=jax.ShapeDtypeStruct((NPAIR, L, UPW), F32))


def ffn_down_dw(name, a, dh):
    return _mm(name, a, dh, grid=(NPAIR,), ta=True,
               a_spec=pl.BlockSpec((None, L, UPW), lambda g: (g, 0, 0)),
               b_spec=pl.BlockSpec((L, D), lambda g: (0, 0)),
               o_spec=pl.BlockSpec((UPW, D), lambda g: (g, 0)),
               out_shape=jax.ShapeDtypeStruct((D_FF, D), BF16))


_SLOPES = [2.0 ** (-8.0 * (h + 1) / NQ) for h in range(NQ)]
_SCALE = HD ** -0.5


QR = GQA * T
KC = 3 * T


def _attn_geometry(g, n, sink_ref):
    row = lax.broadcasted_iota(jnp.int32, (QR, KC), 0)
    col = lax.broadcasted_iota(jnp.int32, (QR, KC), 1)
    i = row & (T - 1)
    j = col & (T - 1)
    blk = col // T
    d_meta = n * T + i - j
    ok_meta = (j >= PAD) & (d_meta >= 0)
    ok_prev = j > i + jnp.where(n >= 2, 0, T)
    ok_cur = j <= i - jnp.where(n >= 1, 0, T)
    ok = ((blk == 0) & ok_meta) | ((blk == 1) & ok_prev) | ((blk == 2) & ok_cur)
    dist = jnp.where(blk == 0, jnp.minimum(d_meta, T), jnp.where(blk == 1, T + i - j, i - j)).astype(F32)
    head = lax.broadcasted_iota(jnp.int32, (QR, 1), 0) // T
    slope = jnp.zeros((QR, 1), F32)
    sink = jnp.zeros((QR, 1), F32)
    for hh in range(GQA):
        slope = jnp.where(head == hh, jnp.where(g == 0, _SLOPES[hh], _SLOPES[GQA + hh]), slope)
        sink = jnp.where(head == hh, sink_ref[g * GQA + hh], sink)
    return ok, slope * dist, sink


def _attn_probs(qn, keys, geo):
    ok, penalty, sink = geo
    s = jnp.where(ok, _dot(qn, keys, nt=True) * _SCALE - penalty, NEG)
    m = jnp.maximum(jnp.max(s, axis=-1, keepdims=True), sink)
    e = jnp.exp(s - m)
    e_sink = jnp.exp(sink - m)
    inv = 1.0 / (e.sum(axis=-1, keepdims=True) + e_sink)
    return e, inv, e_sink * inv


def _rms_rows(x, g):
    r = lax.rsqrt(jnp.mean(x * x, axis=-1, keepdims=True) + RMS_EPS)
    xh = x * r
    return xh * g, xh, r


def _rms_rows_bwd(dy, g, xh, r):
    dxh = dy * g
    return r * (dxh - xh * jnp.mean(dxh * xh, axis=-1, keepdims=True))


def _rows3(m_ref, p_ref, c_ref):
    return jnp.concatenate([m_ref[...], p_ref[...], c_ref[...]], axis=0)


_ATT_K, _ATT_V = 4, 5


def _nat_specs():
    qspec = pl.BlockSpec((T, NQ * HD), lambda n: (n, 0))
    kv = lambda col: (pl.BlockSpec((T, NKV * HD), lambda n: (n, col)),
                      pl.BlockSpec((T, NKV * HD), lambda n: (jnp.maximum(n - 1, 0), col)),
                      pl.BlockSpec((T, NKV * HD), lambda n: (0, col)))
    return qspec, kv(_ATT_K), kv(_ATT_V)


def _group(g, x, w=HD):
    return x[:, g * w:(g + 1) * w]


def _stack_heads(x):
    return jnp.concatenate([x[:, hh * HD:(hh + 1) * HD] for hh in range(GQA)], axis=0)


def _unstack_heads(x):
    return jnp.concatenate([x[hh * T:(hh + 1) * T] for hh in range(GQA)], axis=1)


def attn_fwd(proj, qg, kg, sinks, name):
    def body(sink_ref, q_ref, kc_ref, kp_ref, km_ref, vc_ref, vp_ref, vm_ref, qg_ref, kg_ref, o_ref):
        n = pl.program_id(0)
        k_all = _rows3(km_ref, kp_ref, kc_ref)
        v_all = _rows3(vm_ref, vp_ref, vc_ref)
        q_all = q_ref[...]
        outs = []
        for g in range(NKV):
            geo = _attn_geometry(g, n, sink_ref)
            keys = _bf(_rms_rows(_group(g, k_all), kg_ref[...])[0])
            vals = _bf(_group(g, v_all))
            qn = _bf(_rms_rows(_stack_heads(_group(g, q_all, GQA * HD)), qg_ref[...])[0])
            e, inv, _ = _attn_probs(qn, keys, geo)
            outs.append(_unstack_heads(_dot(_bf(e), vals) * inv))
        o_ref[...] = jnp.concatenate(outs, axis=1)

    qspec, (kc, kp, km), (vc, vp, vm) = _nat_specs()
    vec = pl.BlockSpec((1, HD), lambda n: (0, 0))
    return pl.pallas_call(
        body, name=name, out_shape=jax.ShapeDtypeStruct((L, NQ * HD), F32), grid=(NB,),
        in_specs=[pl.BlockSpec(memory_space=pltpu.SMEM), qspec, kc, kp, km, vc, vp, vm, vec, vec],
        out_specs=qspec)(sinks, proj, proj, proj, proj, proj, proj, proj, qg, kg)


def attn_bwd(proj, o, dycat, og, qg, kg, sinks, name):
    def body(sink_ref, q_ref, kc_ref, kp_ref, km_ref, vc_ref, vp_ref, vm_ref, o_ref, dy_ref, og_ref, qg_ref, kg_ref,
             dq_ref, dkc_ref, dkp_ref, dkm_ref, dvc_ref, dvp_ref, dvm_ref, dqg_ref, dkg_ref, dsk_ref, dog_ref):
        n = pl.program_id(0)

        @pl.when(n == 0)
        def _():
            for r in (dkm_ref, dvm_ref, dqg_ref, dkg_ref, dsk_ref, dog_ref):
                r[...] = jnp.zeros_like(r)

        ogv = og_ref[...]
        dyv = dy_ref[...]
        _, oh, orr = _rms_rows(o_ref[...], ogv)
        do_all = _rms_rows_bwd(dyv, ogv, oh, orr)
        dog_ref[...] += jnp.sum(dyv * oh, axis=0, keepdims=True)

        kgv = kg_ref[...]
        qgv = qg_ref[...]
        k_all = _rows3(km_ref, kp_ref, kc_ref)
        v_all = _rows3(vm_ref, vp_ref, vc_ref)
        q_all = q_ref[...]
        dq, dk, dv, dsk_rows = [], [], [], []
        dqg_acc = jnp.zeros((1, HD), F32)
        dkg_acc = jnp.zeros((1, HD), F32)
        for g in range(NKV):
            geo = _attn_geometry(g, n, sink_ref)
            kn_f, kh, kr = _rms_rows(_group(g, k_all), kgv)
            keys = _bf(kn_f)
            vals = _bf(_group(g, v_all))
            qn_f, qh, qr = _rms_rows(_stack_heads(_group(g, q_all, GQA * HD)), qgv)
            qn = _bf(qn_f)
            e, inv, p_sink = _attn_probs(qn, keys, geo)
            dob = _bf(_stack_heads(_group(g, do_all, GQA * HD)))
            p = e * inv
            dp = _dot(dob, vals, nt=True)
            delta = (p * dp).sum(axis=-1, keepdims=True)
            for hh in range(GQA):
                part = -jnp.sum((p_sink * delta)[hh * T:(hh + 1) * T], axis=0, keepdims=True)
                dsk_rows.append(jnp.broadcast_to(part, (1, 128)))
            ds = p * (dp - delta)
            dqn = _dot(_bf(ds), keys) * _SCALE
            dq.append(_unstack_heads(_rms_rows_bwd(dqn, qgv, qh, qr)))
            dqg_acc = dqg_acc + jnp.sum(dqn * qh, axis=0, keepdims=True)
            dkn = _dot(_bf(ds.T), qn) * _SCALE
            dkg_acc = dkg_acc + jnp.sum(dkn * kh, axis=0, keepdims=True)
            dk.append(_rms_rows_bwd(dkn, kgv, kh, kr))
            dv.append(_dot(_bf(p.T), dob))
        dq_ref[...] = jnp.concatenate(dq, axis=1).astype(dq_ref.dtype)
        dk_all = jnp.concatenate(dk, axis=1)
        dv_all = jnp.concatenate(dv, axis=1)
        dkm_ref[...] += dk_all[0:T]
        dkp_ref[...] = dk_all[T:2 * T]
        dkc_ref[...] = dk_all[2 * T:3 * T]
        dvm_ref[...] += dv_all[0:T]
        dvp_ref[...] = dv_all[T:2 * T]
        dvc_ref[...] = dv_all[2 * T:3 * T]
        dsk_ref[...] += _stack_rows(dsk_rows)
        dqg_ref[...] += dqg_acc
        dkg_ref[...] += dkg_acc

    qspec, (kc, kp, km), (vc, vp, vm) = _nat_specs()
    vec = pl.BlockSpec((1, HD), lambda n: (0, 0))
    cur = pl.BlockSpec((T, NKV * HD), lambda n: (n, 0))
    meta = pl.BlockSpec((T, NKV * HD), lambda n: (0, 0))
    kv_shape = jax.ShapeDtypeStruct((L, NKV * HD), F32)
    meta_shape = jax.ShapeDtypeStruct((T, NKV * HD), F32)
    vec_shape = jax.ShapeDtypeStruct((1, HD), F32)
    ogvec = pl.BlockSpec((1, NQ * HD), lambda n: (0, 0))
    return pl.pallas_call(
        body, name=name,
        out_shape=(jax.ShapeDtypeStruct((L, NQ * HD), BF16), kv_shape, kv_shape, meta_shape, kv_shape, kv_shape,
                   meta_shape, vec_shape, vec_shape, jax.ShapeDtypeStruct((NQ, 128), F32),
                   jax.ShapeDtypeStruct((1, NQ * HD), F32)),
        grid=(NB,),
        in_specs=[pl.BlockSpec(memory_space=pltpu.SMEM), qspec, kc, kp, km, vc, vp, vm, qspec, qspec, ogvec, vec, vec],
        out_specs=(qspec, cur, cur, meta, cur, cur, meta, vec, vec,
                   pl.BlockSpec((NQ, 128), lambda n: (0, 0)), ogvec))(
            sinks, proj, proj, proj, proj, proj, proj, proj, o, dycat, og, qg, kg)


RW = RH * HD


def _ret_tables():
    h = np.arange(RH, dtype=np.float64)
    lg = np.log1p(-np.exp2(-5.0 - h))
    idx = np.arange(T, dtype=np.float64)
    diff = idx[:, None] - idx[None, :]
    decay = np.where(diff[None] >= 0, np.exp(np.maximum(diff, 0.0)[None] * lg[:, None, None]), 0.0)
    zeta = np.exp((T - 1 - idx)[None, :] * lg[:, None])
    xi = np.exp((idx + 1.0)[None, :] * lg[:, None])
    cd = np.exp(T * lg)
    lanes = lambda a: np.repeat(a.T, HD, axis=1)
    head_of = np.arange(RW) // HD
    same = (head_of[:, None] == head_of[None, :]).astype(np.float64)
    f = lambda a: jnp.asarray(a, F32)
    return dict(decay=f(decay), zeta=f(lanes(zeta)), xi=f(lanes(xi)), cd=f(np.repeat(cd, HD)[None, :]),
                head=f((head_of[None, :] == np.arange(RH)[:, None]).astype(np.float64)[:, None, :]),
                same=f(same), avg=jnp.asarray(same / HD, BF16))


def _seg_mean(x, avg):
    hi = _bf(x)
    lo = _bf(x - hi.astype(F32))
    return _dot(hi, avg) + _dot(lo, avg)


def _ret_specs(col0, order):
    return lambda col: pl.BlockSpec((T, RW), lambda i: (order(i), col0 + col))


def _ret_chunk(q, kf, v, s, tb):
    dec, xi, head = tb
    vb = _bf(v)
    kb = _bf(kf)
    y = _dot(_bf(q * xi), _bf(s))
    a = []
    for h in range(RH):
        a.append(_dot(_bf(q * head[h]), kb, nt=True) * dec[h])
        y = y + head[h] * _dot(_bf(a[h]), vb)
    return a, y


def _gn_rows(y):
    mu = jnp.mean(y, axis=-1, keepdims=True)
    yc = y - mu
    rstd = lax.rsqrt(jnp.mean(yc * yc, axis=-1, keepdims=True) + LN_EPS)
    return yc * rstd, rstd


_RET_COL0 = _RET_Q = 5


def _ret_consts(tb):
    names = ("decay", "zeta", "xi", "cd", "head", "same", "avg")
    full = lambda a: pl.BlockSpec(a.shape, lambda i: (0,) * a.ndim)
    return [tb[n] for n in names], [full(tb[n]) for n in names]


def _ret_inputs(n, q_ref, k_ref, v_ref):
    rows = n * T + lax.broadcasted_iota(jnp.int32, (T, 1), 0)
    valid = rows >= PAD
    return q_ref[...], jnp.where(valid, k_ref[...] * (HD ** -0.5), 0.0), v_ref[...], valid


def ret_fwd(proj, gng, tb, name):
    def body(q_ref, k_ref, v_ref, g_ref, dec_ref, zeta_ref, xi_ref, cd_ref, head_ref, same_ref, avg_ref, gng_ref,
             y_ref, st_ref, s_scr):
        n = pl.program_id(0)

        @pl.when(n == 0)
        def _():
            s_scr[...] = jnp.zeros_like(s_scr)

        s = s_scr[...]
        st_ref[...] = s
        q, kf, v, _ = _ret_inputs(n, q_ref, k_ref, v_ref)
        _, y = _ret_chunk(q, kf, v, s, (dec_ref, xi_ref[...], head_ref))
        s_scr[...] = cd_ref[...] * s + same_ref[...] * _dot(_bf((kf * zeta_ref[...]).T), _bf(v))
        avg = avg_ref[...]
        yc = y - _seg_mean(y, avg)
        yh = yc * lax.rsqrt(_seg_mean(yc * yc, avg) + LN_EPS)
        gv = g_ref[...]
        y_ref[...] = (gv * _sig(gv) * (yh * gng_ref[...])).astype(y_ref.dtype)

    col = _ret_specs(_RET_COL0, lambda i: i)
    consts, cspecs = _ret_consts(tb)
    return pl.pallas_call(
        body, name=name,
        out_shape=(jax.ShapeDtypeStruct((L, RW), BF16), jax.ShapeDtypeStruct((NB, RW, RW), F32)),
        grid=(NB,),
        in_specs=[col(0), col(1), col(2), col(3)] + cspecs + [pl.BlockSpec((1, RW), lambda i: (0, 0))],
        out_specs=(pl.BlockSpec((T, RW), lambda i: (i, 0)), pl.BlockSpec((None, RW, RW), lambda i: (i, 0, 0))),
        scratch_shapes=[pltpu.VMEM((RW, RW), F32)])(proj, proj, proj, proj, *consts, gng)


def ret_bwd(proj, dycat, states, gng, tb, name):
    def body(q_ref, k_ref, v_ref, g_ref, do_ref, st_ref, dec_ref, zeta_ref, xi_ref, cd_ref, head_ref, same_ref,
             avg_ref, gng_ref, d_ref, dgn_ref, ds_scr):
        i = pl.program_id(0)
        n = NB - 1 - i

        @pl.when(i == 0)
        def _():
            ds_scr[...] = jnp.zeros_like(ds_scr)

        dsn = _bf(ds_scr[...])
        s = st_ref[...]
        sb = _bf(s)
        q, kf, v, valid = _ret_inputs(n, q_ref, k_ref, v_ref)
        xi, zeta, avg = xi_ref[...], zeta_ref[...], avg_ref[...]
        a, y = _ret_chunk(q, kf, v, s, (dec_ref, xi, head_ref))
        yc = y - _seg_mean(y, avg)
        rstd = lax.rsqrt(_seg_mean(yc * yc, avg) + LN_EPS)
        yh = yc * rstd
        gv = g_ref[...]
        sg = _sig(gv)
        sil = gv * sg
        gn = gng_ref[...]
        dout = do_ref[...]
        d_ref[:, 3 * RW:4 * RW] = (dout * (yh * gn) * (sg * (1.0 + gv * (1.0 - sg)))).astype(d_ref.dtype)
        dyh = dout * sil * gn
        part = jnp.sum(dout * sil * yh, axis=0, keepdims=True)

        @pl.when(i == 0)
        def _():
            dgn_ref[...] = part

        @pl.when(i > 0)
        def _():
            dgn_ref[...] += part

        dy = rstd * (dyh - _seg_mean(dyh, avg) - yh * _seg_mean(dyh * yh, avg))
        dyb = _bf(dy)
        vb, kb, qb = _bf(v), _bf(kf), _bf(q)
        dq = _dot(dyb, sb, nt=True) * xi
        dkf = _dot(vb, dsn, nt=True) * zeta
        dv = _dot(_bf(kf * zeta), dsn)
        for h in range(RH):
            m = head_ref[h]
            da = _dot(_bf(dy * m), vb, nt=True) * dec_ref[h]
            dv = dv + m * _dot(_bf(a[h].T), dyb)
            dq = dq + m * _dot(_bf(da), kb)
            dkf = dkf + m * _dot(_bf(da.T), qb)
        d_ref[:, 0:RW] = dq.astype(d_ref.dtype)
        d_ref[:, RW:2 * RW] = jnp.where(valid, dkf * (HD ** -0.5), 0.0).astype(d_ref.dtype)
        d_ref[:, 2 * RW:3 * RW] = dv.astype(d_ref.dtype)
        ds_scr[...] = cd_ref[...] * ds_scr[...] + same_ref[...] * _dot(_bf((q * xi).T), dyb)

    back = lambda i: NB - 1 - i
    col = _ret_specs(_RET_COL0, back)
    consts, cspecs = _ret_consts(tb)
    vec = pl.BlockSpec((1, RW), lambda i: (0, 0))
    return pl.pallas_call(
        body, name=name,
        out_shape=(jax.ShapeDtypeStruct((L, 4 * RW), BF16), jax.ShapeDtypeStruct((1, RW), F32)),
        grid=(NB,),
        in_specs=[col(0), col(1), col(2), col(3), _ret_specs(3, back)(0),
                  pl.BlockSpec((None, RW, RW), lambda i: (back(i), 0, 0))] + cspecs + [vec],
        out_specs=(pl.BlockSpec((T, 4 * RW), lambda i: (back(i), 0)), vec),
        scratch_shapes=[pltpu.VMEM((RW, RW), F32)])(proj, proj, proj, proj, dycat, states, *consts, gng)


HALO = 32
TAP0 = HALO - (CONV_K - 1)


def _conv_specs():
    cur = lambda col=0: pl.BlockSpec((T, CV), lambda c: (c, col))
    before = lambda col=0: pl.BlockSpec((HALO, CV), lambda c: (jnp.maximum(c * (T // HALO) - 1, 0), col))
    after = pl.BlockSpec((HALO, CV), lambda c: (jnp.minimum((c + 1) * (T // HALO), L // HALO - 1), 0))
    full = lambda r, w: pl.BlockSpec((r, w), lambda c: (0, 0))
    return cur, before, after, full


_CONV_A, _CONV_B = 3, 4
_DY_CONV = 2


def _conv_post(xc, lg_ref, lb_ref):
    xh, rstd = _gn_rows(xc)
    z = xh * lg_ref[...] + lb_ref[...]
    return xh, rstd, z, _sig(z)


def conv_fwd(proj, w, b, lg, lb, pw, og, name):
    def body(ca_ref, cah_ref, cb_ref, cbh_ref, w_ref, b_ref, lg_ref, lb_ref, pw_ref, og_ref, y_ref, xc_ref, u_scr):
        c = pl.program_id(0)
        u_scr[0:HALO, :] = jnp.where(c > 0, cah_ref[...] * _sig(cbh_ref[...]), 0.0)
        u_scr[HALO:HALO + T, :] = ca_ref[...] * _sig(cb_ref[...])
        acc = jnp.zeros((T, CV), F32)
        for k in range(CONV_K):
            acc = acc + w_ref[k:k + 1, :] * u_scr[TAP0 + k:TAP0 + k + T, :]
        xc = acc + b_ref[...]
        xc_ref[...] = xc
        _, _, z, sg = _conv_post(xc, lg_ref, lb_ref)
        zp = _dot(_bf(z * sg), pw_ref[...])
        y_ref[...] = _rms_rows(zp, og_ref[...])[0].astype(y_ref.dtype)

    cur, before, _, full = _conv_specs()
    vec = full(1, CV)
    seq = jax.ShapeDtypeStruct((L, CV), F32)
    return pl.pallas_call(
        body, name=name, out_shape=(jax.ShapeDtypeStruct((L, CV), BF16), seq), grid=(NB,),
        in_specs=[cur(_CONV_A), before(_CONV_A), cur(_CONV_B), before(_CONV_B), full(CONV_K, CV), vec, vec, vec,
                  full(CV, CV), vec],
        out_specs=(cur(), cur()),
        scratch_shapes=[pltpu.VMEM((HALO + T, CV), F32)])(proj, proj, proj, proj, w, b, lg, lb, pw, og)


def conv_bwd(proj, xc, dycat, w, lg, lb, pw, og, name):
    def body1(xc_ref, dy_ref, lg_ref, lb_ref, pw_ref, og_ref, dxc_ref, db_ref, dlg_ref, dlb_ref, dog_ref, dpw_ref):
        c = pl.program_id(0)

        @pl.when(c == 0)
        def _():
            for r in (db_ref, dlg_ref, dlb_ref, dog_ref, dpw_ref):
                r[...] = jnp.zeros_like(r)

        xh, rstd, z, sg = _conv_post(xc_ref[...], lg_ref, lb_ref)
        s = z * sg
        zp = _dot(_bf(s), pw_ref[...])
        ogv = og_ref[...]
        _, zph, r2 = _rms_rows(zp, ogv)
        dyv = dy_ref[...]
        dog_ref[...] += jnp.sum(dyv * zph, axis=0, keepdims=True)
        dzpb = _bf(_rms_rows_bwd(dyv, ogv, zph, r2))
        dpw_ref[...] += _dot(_bf(s.T), dzpb)
        dz = _dot(dzpb, pw_ref[...], nt=True) * (sg * (1.0 + z * (1.0 - sg)))
        dlg_ref[...] += jnp.sum(dz * xh, axis=0, keepdims=True)
        dlb_ref[...] += jnp.sum(dz, axis=0, keepdims=True)
        dxh = dz * lg_ref[...]
        dxc = rstd * (dxh - jnp.mean(dxh, axis=-1, keepdims=True) - xh * jnp.mean(dxh * xh, axis=-1, keepdims=True))
        db_ref[...] += jnp.sum(dxc, axis=0, keepdims=True)
        dxc_ref[...] = dxc

    def body2(dx_ref, dxa_ref, ca_ref, cb_ref, w_ref, dca_ref, dcb_ref, dw_ref, d_scr):
        c = pl.program_id(0)

        @pl.when(c == 0)
        def _():
            dw_ref[...] = jnp.zeros_like(dw_ref)

        d_scr[0:T, :] = dx_ref[...]
        d_scr[T:T + HALO, :] = jnp.where(c < NB - 1, dxa_ref[...], 0.0)
        ca = ca_ref[...]
        sg = _sig(cb_ref[...])
        u = ca * sg
        du = jnp.zeros((T, CV), F32)
        for k in range(CONV_K):
            off = CONV_K - 1 - k
            dj = d_scr[off:off + T, :]
            du = du + w_ref[k:k + 1, :] * dj
            dw_ref[k:k + 1, :] += jnp.sum(u * dj, axis=0, keepdims=True)
        dca_ref[...] = (du * sg).astype(dca_ref.dtype)
        dcb_ref[...] = (du * ca * sg * (1.0 - sg)).astype(dcb_ref.dtype)

    cur, _, after, full = _conv_specs()
    seq = jax.ShapeDtypeStruct((L, CV), F32)
    vsh = jax.ShapeDtypeStruct((1, CV), F32)
    vec = full(1, CV)
    dxc, db, dlg, dlb, dog, dpw = pl.pallas_call(
        body1, name=name + "_a",
        out_shape=(seq, vsh, vsh, vsh, vsh, jax.ShapeDtypeStruct((CV, CV), F32)),
        grid=(NB,),
        in_specs=[cur(), cur(_DY_CONV), vec, vec, full(CV, CV), vec],
        out_specs=(cur(), vec, vec, vec, vec, full(CV, CV)))(xc, dycat, lg, lb, pw, og)
    dca, dcb, dw = pl.pallas_call(
        body2, name=name + "_b", grid=(NB,),
        out_shape=(jax.ShapeDtypeStruct((L, CV), BF16), jax.ShapeDtypeStruct((L, CV), BF16),
                   jax.ShapeDtypeStruct((CONV_K, CV), F32)),
        in_specs=[cur(), after, cur(_CONV_A), cur(_CONV_B), full(CONV_K, CV)],
        out_specs=(cur(), cur(), full(CONV_K, CV)),
        scratch_shapes=[pltpu.VMEM((T + HALO, CV), F32)])(dxc, dxc, proj, proj, w)
    return dca, dcb, dw, db, dlg, dlb, dog, dpw


HR = 8
PP = 2


def _ffn_specs():
    cur = lambda off: pl.BlockSpec((PP, T, UPW), lambda j, c: (j + off // PP, c, 0))
    before = lambda off: pl.BlockSpec(
        (PP, HR, UPW), lambda j, c: (j + off // PP, jnp.maximum(c * (T // HR) - 1, 0), 0))
    after = lambda off: pl.BlockSpec(
        (PP, HR, UPW), lambda j, c: (j + off // PP, jnp.minimum((c + 1) * (T // HR), L // HR - 1), 0))
    wspec = lambda off, r: pl.BlockSpec((PP, r, UPW), lambda j, c: (j + off // PP, 0, 0))
    return cur, before, after, wspec


HR16 = 16


def _ffn_after16():
    return pl.BlockSpec((PP, HR16, UPW), lambda j, c: (j, jnp.minimum((c + 1) * (T // HR16), L // HR16 - 1), 0))


def ffn_act_fwd(f, w, b, name):
    strip = 16

    def body(*refs):
        for p in range(PP):
            one(*[r.at[p] for r in refs[:-2]], *refs[-2:])

    def one(fg_ref, fgh_ref, fu_ref, fuh_ref, wg_ref, wu_ref, bg_ref, bu_ref, a_ref, gc_ref, uc_ref, hg_scr, hu_scr):
        c = pl.program_id(1)
        for scr, x_ref, xh_ref in ((hg_scr, fg_ref, fgh_ref), (hu_scr, fu_ref, fuh_ref)):
            scr[0:HR, :] = jnp.where(c > 0, xh_ref[...], 0.0)
            scr[HR:HR + strip, :] = x_ref[0:strip, :]
        wg = [wg_ref[k:k + 1, :] for k in range(FFN_K)]
        wu = [wu_ref[k:k + 1, :] for k in range(FFN_K)]
        bg, bu = bg_ref[...], bu_ref[...]

        def conv(src, base, w, b):
            acc = b
            for k in range(FFN_K):
                o = base - (FFN_K - 1) + k
                acc = acc + w[k] * src[o:o + strip, :]
            return acc

        for s in range(T // strip):
            r = s * strip
            gate = conv(hg_scr, HR, wg, bg) if s == 0 else conv(fg_ref, r, wg, bg)
            up = conv(hu_scr, HR, wu, bu) if s == 0 else conv(fu_ref, r, wu, bu)
            gc_ref[r:r + strip, :] = gate.astype(BF16)
            uc_ref[r:r + strip, :] = up.astype(BF16)
            a_ref[r:r + strip, :] = (gate * _sig(gate) * up).astype(BF16)

    cur, before, _, wspec = _ffn_specs()
    pre = jax.ShapeDtypeStruct((NPAIR, L, UPW), BF16)
    return pl.pallas_call(
        body, name=name, out_shape=(jax.ShapeDtypeStruct((NPAIR, L, UPW), BF16), pre, pre), grid=(NPAIR // PP, NB),
        in_specs=[cur(0), before(0), cur(NPAIR), before(NPAIR), wspec(0, FFN_K), wspec(NPAIR, FFN_K),
                  wspec(0, 1), wspec(NPAIR, 1)],
        out_specs=(cur(0), cur(0), cur(0)),
        scratch_shapes=[pltpu.VMEM((HR + strip, UPW), F32), pltpu.VMEM((HR + strip, UPW), F32)])(
            f, f, f, f, w, w, b, b)


def ffn_act_bwd(f, gc, uc, da, w, name):
    ext = T + HR
    sub = 8

    def body(*refs):
        for p in range(PP):
            one(*[r.at[p] for r in refs[:-2]], *refs[-2:])

    def one(fg_ref, fu_ref, gc_ref, gca_ref, uc_ref, uca_ref, da_ref, daa_ref, wg_ref, wu_ref,
            dfg_ref, dfu_ref, dwg_ref, dwu_ref, dbg_ref, dbu_ref, dg_scr, du_scr):
        c = pl.program_id(1)
        wg = [wg_ref[k:k + 1, :] for k in range(FFN_K)]
        wu = [wu_ref[k:k + 1, :] for k in range(FFN_K)]

        for r in range(0, ext, HR16):
            rows = min(HR16, ext - r)
            if r < T:
                gate, up, dav = gc_ref[r:r + HR16, :].astype(F32), uc_ref[r:r + HR16, :].astype(F32), da_ref[r:r + HR16, :]
            else:
                gate, up = gca_ref[...].astype(F32)[0:rows], uca_ref[...].astype(F32)[0:rows]
                dav = jnp.where(c < NB - 1, daa_ref[...], 0.0)
            sg = _sig(gate)
            dg_scr[r:r + rows, :] = dav * up * (sg * (1.0 + gate * (1.0 - sg)))
            du_scr[r:r + rows, :] = dav * gate * sg

        def back(d_scr, x_ref, w, df_ref, dw_ref, db_ref):
            accs = [jnp.zeros((sub, UPW), F32) for _ in range(FFN_K + 1)]
            for s in range(T // (2 * sub)):
                pieces = []
                for r in (2 * s * sub, (2 * s + 1) * sub):
                    xv = x_ref[r:r + sub, :]
                    df = jnp.zeros((sub, UPW), F32)
                    for k in range(FFN_K):
                        off = FFN_K - 1 - k
                        dj = d_scr[r + off:r + off + sub, :]
                        df = df + w[k] * dj
                        accs[k] = accs[k] + xv * dj
                        if off == 0:
                            accs[FFN_K] = accs[FFN_K] + dj
                    pieces.append(df)
                df_ref[2 * s * sub:2 * (s + 1) * sub, :] = jnp.concatenate(pieces, axis=0).astype(BF16)
            dwp = _stack_rows([jnp.sum(a, axis=0, keepdims=True) for a in accs[:FFN_K]])
            dbp = jnp.sum(accs[FFN_K], axis=0, keepdims=True)

            @pl.when(c == 0)
            def _():
                dw_ref[...] = dwp
                db_ref[...] = dbp

            @pl.when(c > 0)
            def _():
                dw_ref[...] += dwp
                db_ref[...] += dbp

        back(dg_scr, fg_ref, wg, dfg_ref, dwg_ref, dbg_ref)
        back(du_scr, fu_ref, wu, dfu_ref, dwu_ref, dbu_ref)

    cur, _, after, wspec = _ffn_specs()
    slab = jax.ShapeDtypeStruct((NPAIR, L, UPW), BF16)
    wsh = jax.ShapeDtypeStruct((NPAIR, FFN_K, UPW), F32)
    bsh = jax.ShapeDtypeStruct((NPAIR, 1, UPW), F32)
    return pl.pallas_call(
        body, name=name, out_shape=(slab, slab, wsh, wsh, bsh, bsh), grid=(NPAIR // PP, NB),
        in_specs=[cur(0), cur(NPAIR), cur(0), _ffn_after16(), cur(0), _ffn_after16(), cur(0), after(0),
                  wspec(0, FFN_K), wspec(NPAIR, FFN_K)],
        out_specs=(cur(0), cur(0), wspec(0, FFN_K), wspec(0, FFN_K), wspec(0, 1), wspec(0, 1)),
        scratch_shapes=[pltpu.VMEM((ext, UPW), F32), pltpu.VMEM((ext, UPW), F32)])(
            f, f, gc, gc, uc, uc, da, da, w, w)


def _mesh_pos():
    return lax.axis_index("x"), lax.axis_index("y"), lax.axis_index("c")


def _peer(pos, k):
    x, y, c = pos
    px = 1 - x if k & 4 else x
    py = 1 - y if k & 2 else y
    pc = 1 - c if k & 1 else c
    return (px, py, pc), 4 * px + 2 * py + pc


_CHIP_FLIPS = (4, 2, 6)
_HBM = pl.BlockSpec(memory_space=pl.ANY)


def all_gather(shards, name):
    nt = len(shards)

    def body(*refs):
        ins, outs = refs[:nt], refs[nt:2 * nt]
        send, recv, local = refs[2 * nt:]
        pos = _mesh_pos()
        me = 4 * pos[0] + 2 * pos[1] + pos[2]
        sib, sib_id = _peer(pos, 1)

        def copy(t, k, block_id, to, src=None):
            dst = outs[t].at[block_id]
            return pltpu.make_async_remote_copy(
                src_ref=dst if src is None else src, dst_ref=dst, send_sem=send.at[t, k], recv_sem=recv.at[t, k],
                device_id=to, device_id_type=MESH)

        locals_ = [pltpu.make_async_copy(ins[t], outs[t].at[me], local.at[t]) for t in range(nt)]
        for cp in locals_:
            cp.start()
        started = []
        for j, flip in enumerate(_CHIP_FLIPS):
            for t in range(nt):
                started.append(copy(t, 1 + j, me, _peer(pos, flip)[0], src=ins[t]))
        for t in range(nt):
            started.append(copy(t, 0, me, sib, src=ins[t]))
        for cp in started:
            cp.start()
        for j, flip in enumerate(_CHIP_FLIPS):
            _, pid = _peer(pos, flip)
            for t in range(nt):
                copy(t, 1 + j, pid, sib).wait_recv()
                fwd = copy(t, 4 + j, pid, sib)
                fwd.start()
                started.append(fwd)
        for t in range(nt):
            copy(t, 0, sib_id, sib).wait_recv()
        for j, flip in enumerate(_CHIP_FLIPS):
            _, pid = _peer(pos, flip | 1)
            for t in range(nt):
                copy(t, 4 + j, pid, sib).wait_recv()
        for cp in started:
            cp.wait_send()
        for cp in locals_:
            cp.wait()

    return pl.pallas_call(
        body, name=name,
        out_shape=tuple(jax.ShapeDtypeStruct((NDEV,) + s.shape, s.dtype) for s in shards),
        in_specs=[_HBM] * nt, out_specs=tuple([_HBM] * nt),
        scratch_shapes=[pltpu.SemaphoreType.DMA((nt, NDEV - 1)), pltpu.SemaphoreType.DMA((nt, NDEV - 1)),
                        pltpu.SemaphoreType.DMA((nt,))])(*shards)


_SPLIT = dict(has_side_effects=pltpu.SideEffectType.DATAFLOW_SIDE_EFFECTING)
_SEM = pl.BlockSpec(memory_space=pltpu.SEMAPHORE)


def _exchange_start(name, srcs, lands, copies, after):
    ns, nl = len(srcs), len(lands)
    ncopy = len(copies(None, [None] * ns, [None] * nl))

    def body(*refs):
        src_refs, land_refs = refs[:ns], refs[ns:ns + nl]
        send, recv = refs[ns + nl + len(after):ns + nl + len(after) + 2]
        token = refs[-1]
        for i, (src, dst, peer) in enumerate(copies(_mesh_pos(), src_refs, land_refs)):
            pltpu.make_async_remote_copy(src_ref=src, dst_ref=dst, send_sem=send.at[i], recv_sem=recv.at[i],
                                         device_id=peer, device_id_type=MESH).start()
        token[...] = jnp.zeros_like(token)

    hbm = lambda a: pltpu.HBM(a.shape, a.dtype)
    out = pl.pallas_call(
        body, name=name,
        out_shape=(pltpu.SemaphoreType.DMA((ncopy,)), pltpu.SemaphoreType.DMA((ncopy,)),
                   *[hbm(a) for a in srcs], *[hbm(a) for a in lands], jax.ShapeDtypeStruct((8, 128), F32)),
        in_specs=[pl.BlockSpec(memory_space=pltpu.HBM)] * (ns + nl) + [_HBM] * len(after),
        out_specs=(_SEM, _SEM, *[pl.BlockSpec(memory_space=pltpu.HBM)] * (ns + nl),
                   pl.BlockSpec(memory_space=pltpu.VMEM)),
        input_output_aliases={i: 2 + i for i in range(ns + nl)},
        compiler_params=pltpu.CompilerParams(**_SPLIT))(
            *[pltpu.with_memory_space_constraint(a, pltpu.HBM) for a in list(srcs) + list(lands)], *after)
    return out[0], out[1], list(out[2:2 + ns]), list(out[2 + ns:2 + ns + nl]), out[-1]


def _exchange_wait(name, send, recv, srcs, lands, copies, after):
    ns, nl = len(srcs), len(lands)

    def body(*refs):
        src_refs, land_refs = refs[:ns], refs[ns:ns + nl]
        send_ref, recv_ref = refs[ns + nl:ns + nl + 2]
        pos = _mesh_pos()
        for i, (src, dst, peer) in enumerate(copies(pos, src_refs, land_refs, arriving=True)):
            cp = pltpu.make_async_remote_copy(src_ref=src, dst_ref=dst, send_sem=send_ref.at[i], recv_sem=recv_ref.at[i],
                                              device_id=peer, device_id_type=MESH)
            cp.wait_send()
            cp.wait_recv()

    hbm = lambda a: pltpu.HBM(a.shape, a.dtype)
    out = pl.pallas_call(
        body, name=name, out_shape=tuple(hbm(a) for a in list(srcs) + list(lands)),
        in_specs=[pl.BlockSpec(memory_space=pltpu.HBM)] * (ns + nl) + [_SEM, _SEM] + [_HBM] * len(after),
        out_specs=tuple([pl.BlockSpec(memory_space=pltpu.HBM)] * (ns + nl)),
        input_output_aliases={i: i for i in range(ns + nl)},
        compiler_params=pltpu.CompilerParams(**_SPLIT))(*srcs, *lands, send, recv, *after)
    return list(out[:ns]), list(out[ns:])


def _gather_copies(pos, srcs, lands, arriving=False):
    if pos is None:
        return [None] * (len(srcs) * (NDEV - 1))
    me = 4 * pos[0] + 2 * pos[1] + pos[2]
    out = []
    for src, land in zip(srcs, lands):
        for k in range(1, NDEV):
            peer, pid = _peer(pos, k)
            out.append((src, land.at[pid if arriving else me], peer))
    return out


def _scatter_copies(pos, srcs, lands, arriving=False):
    if pos is None:
        return [None] * (len(srcs) * (NDEV - 1))
    out = []
    for src, land in zip(srcs, lands):
        for k in range(1, NDEV):
            peer, pid = _peer(pos, k)
            out.append((src.at[pid], land.at[k - 1], peer))
    return out


def gather_start(name, shards, lands, after=()):
    return _exchange_start(name, shards, lands, _gather_copies, list(after))


def gather_wait(name, handle, after=()):
    send, recv, srcs, lands, _ = handle
    return _exchange_wait(name, send, recv, srcs, lands, _gather_copies, list(after))


def scatter_start(name, grads, after=()):
    lands = [pltpu.with_memory_space_constraint(lax.empty((NDEV - 1,) + g.shape[1:], g.dtype), pltpu.HBM)
             for g in grads]
    return _exchange_start(name, grads, lands, _scatter_copies, list(after))


def scatter_wait(name, handle, after=()):
    send, recv, srcs, lands, _ = handle
    return _exchange_wait(name, send, recv, srcs, lands, _scatter_copies, list(after))


def sum_slots(parts, name):
    def body(p_ref, o_ref):
        acc = p_ref[0]
        for d in range(1, NDEV):
            acc = acc + p_ref[d]
        o_ref[...] = acc

    return pl.pallas_call(body, name=name, out_shape=jax.ShapeDtypeStruct(parts.shape[1:], F32))(parts)


def _adamw_update(g, w_ref, m_ref, v_ref, g_ref, d_ref, nm_ref, nv_ref):
    g_ref[...] = g
    nm = ADAM_B1 * m_ref[...] + (1.0 - ADAM_B1) * g
    nv = ADAM_B2 * v_ref[...] + (1.0 - ADAM_B2) * (g * g)
    nm_ref[...] = nm
    nv_ref[...] = nv
    m_hat = nm / (1.0 - ADAM_B1 ** ADAM_STEP)
    v_hat = nv / (1.0 - ADAM_B2 ** ADAM_STEP)
    d_ref[...] = -ADAM_LR * (m_hat / (jnp.sqrt(v_hat) + ADAM_EPS) + ADAM_WD * w_ref[...])


def adamw_big(own, lands, w, m, v, me, tr, name):
    _, r, c = own[0].shape
    nt = r // tr

    def body(me_ref, *refs):
        ins, (w_ref, m_ref, v_ref), outs = refs[:2 * DEPTH], refs[2 * DEPTH:2 * DEPTH + 3], refs[2 * DEPTH + 3:]
        layer = pl.program_id(0)
        for l in range(DEPTH):
            @pl.when(layer == l)
            def _(l=l):
                g = ins[2 * l][...].astype(F32)
                for s in range(NDEV - 1):
                    g = g + ins[2 * l + 1][s].astype(F32)
                _adamw_update(g, w_ref, m_ref, v_ref, *outs)

    pick = lambda l: (lambda layer, i, me_ref: jnp.where(layer == l, i, 0))
    in_specs = []
    for l in range(DEPTH):
        in_specs.append(pl.BlockSpec((None, tr, c), lambda layer, i, me_ref, f=pick(l): (me_ref[0], f(layer, i, me_ref), 0)))
        in_specs.append(pl.BlockSpec((NDEV - 1, tr, c), lambda layer, i, me_ref, f=pick(l): (0, f(layer, i, me_ref), 0)))
    blk = pl.BlockSpec((tr, c), lambda layer, i, me_ref: (layer * nt + i, 0))
    sh = jax.ShapeDtypeStruct((DEPTH * r, c), F32)
    args = [a for l in range(DEPTH) for a in (own[l], lands[l])]
    return pl.pallas_call(
        body, name=name, out_shape=(sh, sh, sh, sh),
        grid_spec=pltpu.PrefetchScalarGridSpec(
            num_scalar_prefetch=1, grid=(DEPTH, nt), in_specs=in_specs + [blk, blk, blk],
            out_specs=(blk, blk, blk, blk)),
        compiler_params=_params(VMEM_BIG))(me, *args, w, m, v)


def adamw_many(gs, ws, ms, vs, name):
    n = len(gs)

    def body(*refs):
        outs = refs[4 * n:]
        for i in range(n):
            _adamw_update(refs[i][...], refs[n + i], refs[2 * n + i], refs[3 * n + i], *outs[4 * i:4 * i + 4])

    out_shape = tuple(jax.ShapeDtypeStruct(w.shape, F32) for w in ws for _ in range(4))
    res = pl.pallas_call(body, name=name, out_shape=out_shape)(*gs, *ws, *ms, *vs)
    return [res[4 * i:4 * i + 4] for i in range(n)]


def local_step(x, target, P, hooks):
    tb = _ret_tables()
    h = jnp.concatenate([jnp.zeros((PAD, D), F32), P["meta"], x], axis=0)
    stash = []
    for l in range(DEPTH):
        s = {"h": h}
        s["w_in"], s["cv_pw"], s["w_out"] = hooks["mixer_w"](l, h)
        proj, s["u"] = in_proj_fwd(f"in_proj_{l}", h, P["norm_mix_g"][l], s["w_in"], 256,
                                   after=hooks["first_after"] if l == 0 else None)
        s["proj"] = proj
        s["o"] = attn_fwd(proj, P["q_norm_g"][l], P["k_norm_g"][l], P["attn_sinks"][l], f"attn_fwd_{l}")
        y_cv, s["xc"] = conv_fwd(proj, P["cv_dw_w"][l], P["cv_dw_b"][l], P["cv_ln_g"][l], P["cv_ln_b"][l],
                                 s["cv_pw"], P["cv_out_g"][l], f"conv_fwd_{l}")
        y_ret, s["states"] = ret_fwd(proj, P["ret_gn_g"][l], tb, f"ret_fwd_{l}")
        h, s["ycat"] = out_proj_fwd(f"out_proj_{l}", s["o"], P["attn_out_g"][l], y_cv, y_ret, s["w_out"], h, 512)
        s["h1"] = h
        s["ffn_up"], s["ffn_down"] = hooks["ffn_w"](l, h)
        s["f"], s["u2"] = ffn_up_fwd(f"ffn_up_{l}", h, P["norm_ffn_g"][l], s["ffn_up"])
        a, s["gc"], s["uc"] = ffn_act_fwd(s["f"], P["ffn_dw_w"][l], P["ffn_dw_b"][l], f"ffn_act_fwd_{l}")
        s["a"] = a
        if l < DEPTH - 1:
            h = ffn_down_fwd(f"ffn_down_{l}", a, s["ffn_down"], h)
        else:
            dh, dhm, loss = ffn_down_loss(f"ffn_down_{l}", a, s["ffn_down"], h, target)
        stash.append(s)

    G = {n: [None] * DEPTH for n in ("norm_mix_g", "q_norm_g", "k_norm_g", "attn_sinks", "attn_out_g", "cv_dw_w",
                                     "cv_dw_b", "cv_ln_g", "cv_ln_b", "cv_out_g", "ret_gn_g", "norm_ffn_g",
                                     "ffn_dw_w", "ffn_dw_b")}
    for l in reversed(range(DEPTH)):
        s = stash[l]
        da = ffn_down_dx(f"ffn_down_dx_{l}", dhm, s["ffn_down"])
        dw_down = ffn_down_dw(f"ffn_down_dw_{l}", s["a"], dhm)
        dfg, dfu, dwg, dwu, dbg, dbu = ffn_act_bwd(s["f"], s["gc"], s["uc"], da, P["ffn_dw_w"][l], f"ffn_act_bwd_{l}")
        G["ffn_dw_w"][l] = jnp.concatenate([dwg, dwu], axis=0)
        G["ffn_dw_b"][l] = jnp.concatenate([dbg, dbu], axis=0)
        du2 = ffn_up_dx(f"ffn_up_dx_{l}", dfg, dfu, s["ffn_up"])
        dw_up = ffn_up_dw(f"ffn_up_dw_{l}", s["u2"], dfg, dfu)
        after = hooks["ffn_grads"](l, dw_down, dw_up)
        dh, dhm, G["norm_ffn_g"][l] = rms_bwd(s["h1"], P["norm_ffn_g"][l], du2, dh, f"rms_ffn_bwd_{l}", after=after)
        dycat = mm_nt(f"out_proj_dx_{l}", dhm, s["w_out"], 512)
        dw_out = mm_ta(f"out_proj_dw_{l}", s["ycat"], dhm, 512)
        (dq, dk_c, dk_p, dk_m, dv_c, dv_p, dv_m, G["q_norm_g"][l], G["k_norm_g"][l], dsk,
         G["attn_out_g"][l]) = attn_bwd(s["proj"], s["o"], dycat, P["attn_out_g"][l], P["q_norm_g"][l],
                                        P["k_norm_g"][l], P["attn_sinks"][l], f"attn_bwd_{l}")
        G["attn_sinks"][l] = dsk[:, 0]
        shift = lambda z: jnp.concatenate([z[T:], jnp.zeros((T, NKV * HD), F32)], axis=0)
        dk = (dk_c + shift(dk_p)).at[:T].add(dk_m)
        dv = (dv_c + shift(dv_p)).at[:T].add(dv_m)
        (dca, dcb, G["cv_dw_w"][l], G["cv_dw_b"][l], G["cv_ln_g"][l], G["cv_ln_b"][l], G["cv_out_g"][l],
         dpw) = conv_bwd(s["proj"], s["xc"], dycat, P["cv_dw_w"][l], P["cv_ln_g"][l], P["cv_ln_b"][l], s["cv_pw"],
                         P["cv_out_g"][l], f"conv_bwd_{l}")
        dret, G["ret_gn_g"][l] = ret_bwd(s["proj"], dycat, s["states"], P["ret_gn_g"][l], tb, f"ret_bwd_{l}")
        dproj = jnp.concatenate([dq, dk.astype(BF16), dv.astype(BF16), dca, dcb, dret], axis=1)
        du = mm_nn(f"in_proj_dx_{l}", dproj, s["w_in"], 512)
        dw_in = mm_ta(f"in_proj_dw_{l}", dproj, s["u"], 768)
        after = hooks["mixer_grads"](l, dw_out, dpw, dw_in)
        dh, dhm, G["norm_mix_g"][l] = rms_bwd(s["h"], P["norm_mix_g"][l], du, dh, f"rms_mix_bwd_{l}", after=after,
                                              last=(l == 0))
    return loss[0, 0], dh, dhm[PAD:T], G


_SMALL = ("meta", "norm_mix_g", "q_norm_g", "k_norm_g", "attn_sinks", "attn_out_g", "cv_dw_w", "cv_dw_b", "cv_ln_g",
          "cv_ln_b", "cv_out_g", "ret_gn_g", "norm_ffn_g", "ffn_dw_w", "ffn_dw_b")
_BIG = ("w_in", "cv_pw", "w_out", "ffn_up", "ffn_down")
_TRANSPOSED = ("w_in", "ffn_up")
_ORDER = ("meta", "norm_mix_g", "w_in", "q_norm_g", "k_norm_g", "attn_sinks", "attn_out_g", "cv_dw_w", "cv_dw_b",
          "cv_ln_g", "cv_ln_b", "cv_pw", "cv_out_g", "ret_gn_g", "w_out", "norm_ffn_g", "ffn_up", "ffn_dw_w",
          "ffn_dw_b", "ffn_down")
_SMALL_SHARDED = {"meta": D, "cv_dw_w": CV, "ffn_dw_w": 2 * D_FF}


def _pack(arrs):
    flat = jnp.concatenate([a.reshape(-1) for a in arrs])
    n = flat.shape[0]
    rows = -(-n // 1024) * 8
    return jnp.pad(flat, (0, rows * 128 - n)).reshape(rows, 128)


def _unpack(packed, shapes):
    flat = packed.reshape(-1)
    out, off = [], 0
    for s in shapes:
        n = int(np.prod(s))
        out.append(flat[off:off + n].reshape(s))
        off += n
    return out


def kernel(x, meta, norm_mix_g, w_in, q_norm_g, k_norm_g, attn_sinks, attn_out_g, cv_dw_w, cv_dw_b, cv_ln_g, cv_ln_b, cv_pw, cv_out_g, ret_gn_g, w_out, norm_ffn_g, ffn_up, ffn_dw_w, ffn_dw_b, ffn_down, loss_target, m_meta, m_norm_mix_g, m_w_in, m_q_norm_g, m_k_norm_g, m_attn_sinks, m_attn_out_g, m_cv_dw_w, m_cv_dw_b, m_cv_ln_g, m_cv_ln_b, m_cv_pw, m_cv_out_g, m_ret_gn_g, m_w_out, m_norm_ffn_g, m_ffn_up, m_ffn_dw_w, m_ffn_dw_b, m_ffn_down, v_meta, v_norm_mix_g, v_w_in, v_q_norm_g, v_k_norm_g, v_attn_sinks, v_attn_out_g, v_cv_dw_w, v_cv_dw_b, v_cv_ln_g, v_cv_ln_b, v_cv_pw, v_cv_out_g, v_ret_gn_g, v_w_out, v_norm_ffn_g, v_ffn_up, v_ffn_dw_w, v_ffn_dw_b, v_ffn_down):
    W = dict(meta=meta, norm_mix_g=norm_mix_g, w_in=w_in, q_norm_g=q_norm_g, k_norm_g=k_norm_g,
             attn_sinks=attn_sinks, attn_out_g=attn_out_g, cv_dw_w=cv_dw_w, cv_dw_b=cv_dw_b, cv_ln_g=cv_ln_g,
             cv_ln_b=cv_ln_b, cv_pw=cv_pw, cv_out_g=cv_out_g, ret_gn_g=ret_gn_g, w_out=w_out,
             norm_ffn_g=norm_ffn_g, ffn_up=ffn_up, ffn_dw_w=ffn_dw_w, ffn_dw_b=ffn_dw_b, ffn_down=ffn_down)
    M = dict(meta=m_meta, norm_mix_g=m_norm_mix_g, w_in=m_w_in, q_norm_g=m_q_norm_g, k_norm_g=m_k_norm_g,
             attn_sinks=m_attn_sinks, attn_out_g=m_attn_out_g, cv_dw_w=m_cv_dw_w, cv_dw_b=m_cv_dw_b,
             cv_ln_g=m_cv_ln_g, cv_ln_b=m_cv_ln_b, cv_pw=m_cv_pw, cv_out_g=m_cv_out_g, ret_gn_g=m_ret_gn_g,
             w_out=m_w_out, norm_ffn_g=m_norm_ffn_g, ffn_up=m_ffn_up, ffn_dw_w=m_ffn_dw_w, ffn_dw_b=m_ffn_dw_b,
             ffn_down=m_ffn_down)
    V = dict(meta=v_meta, norm_mix_g=v_norm_mix_g, w_in=v_w_in, q_norm_g=v_q_norm_g, k_norm_g=v_k_norm_g,
             attn_sinks=v_attn_sinks, attn_out_g=v_attn_out_g, cv_dw_w=v_cv_dw_w, cv_dw_b=v_cv_dw_b,
             cv_ln_g=v_cv_ln_g, cv_ln_b=v_cv_ln_b, cv_pw=v_cv_pw, cv_out_g=v_cv_out_g, ret_gn_g=v_ret_gn_g,
             w_out=v_w_out, norm_ffn_g=v_norm_ffn_g, ffn_up=v_ffn_up, ffn_dw_w=v_ffn_dw_w, ffn_dw_b=v_ffn_dw_b,
             ffn_down=v_ffn_down)
    me = 4 * lax.axis_index("x") + 2 * lax.axis_index("y") + lax.axis_index("c")
    for n in _TRANSPOSED:
        W[n], M[n], V[n] = (a.transpose(0, 2, 1) for a in (W[n], M[n], V[n]))

    sh = {n: [W[n][l].astype(BF16) for l in range(DEPTH)] for n in _BIG}
    mix = lambda l: [sh["w_in"][l], sh["cv_pw"][l], sh["w_out"][l]]
    ffn = lambda l: [sh["ffn_up"][l], sh["ffn_down"][l]]
    first = all_gather(mix(0) + [meta, cv_dw_w, ffn_dw_w], "gather_first")
    g_meta, g_cdw, g_fdw = first[3:6]

    def landing(shards):
        return [lax.dynamic_update_slice(lax.empty((NDEV,) + s.shape, s.dtype), s[None], (me,) + (0,) * s.ndim)
                for s in shards]

    gathers = {("ffn", 0): gather_start("gather_ffn0_start", ffn(0), landing(ffn(0)), after=[first[0]])}
    gathers["mix", 1] = gather_start("gather_mix1_start", mix(1), landing(mix(1)), after=[gathers["ffn", 0][4]])
    gathers["ffn", 1] = gather_start("gather_ffn1_start", ffn(1), landing(ffn(1)), after=[gathers["mix", 1][4]])

    def mixer_w(l, h):
        g_in, g_pw, g_out = first[0:3] if l == 0 else gather_wait(f"gather_mix{l}_wait", gathers["mix", l], after=[h])[1]
        return g_in.reshape(IN_W, D), g_pw.reshape(CV, CV), g_out.reshape(D, D)

    def ffn_w(l, h1):
        return gather_wait(f"gather_ffn{l}_wait", gathers["ffn", l], after=[h1])[1]

    scatters = {}

    def ffn_grads(l, dw_down, dw_up):
        scatters["ffn", l] = scatter_start(f"scatter_ffn{l}_start", [dw_up, dw_down.reshape(NDEV, DNR, D)])
        return scatters["ffn", l][4]

    def mixer_grads(l, dw_out, dpw, dw_in):
        grads = [dw_in.reshape(NDEV, IN_W // NDEV, D), dpw.astype(BF16).reshape(NDEV, CV // NDEV, CV),
                 dw_out.reshape(NDEV, D // NDEV, D)]
        scatters["mix", l] = scatter_start(f"scatter_mix{l}_start", grads)
        return scatters["mix", l][4]

    P = dict(
        meta=g_meta.transpose(1, 0, 2).reshape(N_META, D),
        cv_dw_w=g_cdw.transpose(1, 2, 0, 3).reshape(DEPTH, CONV_K, CV),
        ffn_dw_w=g_fdw.transpose(1, 0, 2, 3),
        ffn_dw_b=ffn_dw_b.reshape(DEPTH, NDEV, 1, UPW),
        attn_sinks=attn_sinks,
    )
    for n in ("norm_mix_g", "q_norm_g", "k_norm_g", "attn_out_g", "cv_dw_b", "cv_ln_g", "cv_ln_b", "cv_out_g",
              "ret_gn_g", "norm_ffn_g"):
        P[n] = W[n].reshape(DEPTH, 1, -1)

    hooks = dict(mixer_w=mixer_w, ffn_w=ffn_w, ffn_grads=ffn_grads, mixer_grads=mixer_grads,
                 first_after=gathers["ffn", 1][4])
    loss_part, dx, dmeta, G = local_step(x[0], loss_target[0], P, hooks)

    small_full = {
        "meta": dmeta,
        "cv_dw_w": jnp.stack(G["cv_dw_w"]),
        "ffn_dw_w": jnp.stack([g.transpose(1, 0, 2).reshape(FFN_K, 2 * D_FF) for g in G["ffn_dw_w"]]),
        "ffn_dw_b": jnp.stack([g.reshape(2 * D_FF) for g in G["ffn_dw_b"]]),
        "attn_sinks": jnp.stack(G["attn_sinks"]),
    }
    for n in _SMALL:
        if n not in small_full:
            small_full[n] = jnp.stack([g.reshape(-1) for g in G[n]])
    shapes = [small_full[n].shape for n in _SMALL] + [(1,)]
    packed = _pack([small_full[n] for n in _SMALL] + [loss_part.reshape(1)])
    small_exchange = gather_start("reduce_small_start", [packed], landing([packed]), after=[dx])

    out = {}
    tiles = {"w_in": 144, "cv_pw": 32, "w_out": 128, "ffn_up": 176, "ffn_down": 176}
    me1 = me.astype(jnp.int32).reshape(1)
    own, lands = {n: [None] * DEPTH for n in _BIG}, {n: [None] * DEPTH for n in _BIG}

    def arrived(kind, names, after):
        for l in range(DEPTH):
            srcs, got = scatter_wait(f"scatter_{kind}{l}_wait", scatters[kind, l], after=after)
            for n, s_, g_ in zip(names, srcs, got):
                own[n][l], lands[n][l] = s_, g_

    def update(names):
        for n in names:
            shard = W[n].shape
            rows, cols = shard[0] * shard[1], shard[2]
            res = adamw_big(own[n], lands[n], W[n].reshape(rows, cols), M[n].reshape(rows, cols),
                            V[n].reshape(rows, cols), me1, tiles[n], f"adamw_{n}")
            out[n] = [r.reshape(shard) for r in res]

    arrived("ffn", ("ffn_up", "ffn_down"), [small_exchange[4]])
    update(("ffn_up", "ffn_down"))
    slots = gather_wait("reduce_small_wait", small_exchange, after=[out["ffn_down"][0]])[1][0]
    summed = _unpack(sum_slots(slots, "reduce_small_sum"), shapes)
    loss = summed[-1][0]
    small_g = []
    for n, g in zip(_SMALL, summed):
        if n in _SMALL_SHARDED:
            width = _SMALL_SHARDED[n] // NDEV
            g = lax.dynamic_slice_in_dim(g, me * width, width, axis=g.ndim - 1)
        small_g.append(g)
    flat2 = lambda a: a.reshape(-1, a.shape[-1])
    res = adamw_many([flat2(g) for g in small_g], *[[flat2(X[n]) for n in _SMALL] for X in (W, M, V)], "adamw_small")
    for n, r in zip(_SMALL, res):
        out[n] = [a.reshape(W[n].shape) for a in r]
    arrived("mix", ("w_in", "cv_pw", "w_out"), [out["ffn_down"][0], res[0][0]])
    update(("w_in", "cv_pw", "w_out"))
    for n in _TRANSPOSED:
        out[n] = [r.transpose(0, 2, 1) for r in out[n]]

    return (loss, dx[None], *[out[n][0] for n in _ORDER], *[out[n][1] for n in _ORDER],
            *[out[n][2] for n in _ORDER], *[out[n][3] for n in _ORDER])
```

```python
import numpy as np
import jax
import jax.numpy as jnp
from jax import lax
from jax.experimental import pallas as pl
from jax.experimental.pallas import tpu as pltpu

F32 = jnp.float32
BF16 = jnp.bfloat16

D = 1024
SEQ = 2048
DEPTH = 2
T = 128
L = SEQ + T
NB = L // T
N_META = 16
PAD = T - N_META
HD = 64
NQ = 8
NKV = 2
GQA = NQ // NKV
CV = 256
CONV_K = 31
RH = 4
D_FF = 2816
FFN_K = 3
IN_W = 2304
RMS_EPS = 1e-6
LN_EPS = 1e-5
NEG = -1e30
NDEV = 8
UPW = 2 * D_FF // NDEV
DNR = D_FF // NDEV
NPAIR = NDEV // 2

ADAM_LR, ADAM_B1, ADAM_B2, ADAM_EPS, ADAM_WD, ADAM_STEP = 0.001, 0.9, 0.999, 1e-08, 0.01, 10

VMEM_BIG = 56 * 1024 * 1024

MESH = pl.DeviceIdType.MESH


def _params(vmem=None):
    return pltpu.CompilerParams(vmem_limit_bytes=vmem) if vmem else None


def _dot(a, b, nt=False):
    return lax.dot_general(a, b, (((1,), (1 if nt else 0,)), ((), ())), preferred_element_type=F32)


def _sig(x):
    return 1.0 / (1.0 + jnp.exp(-x))


def _bf(x):
    return x.astype(BF16)


def _stack_rows(rows):
    idx = lax.broadcasted_iota(jnp.int32, (len(rows), rows[0].shape[1]), 0)
    out = jnp.zeros((len(rows), rows[0].shape[1]), F32)
    for r, v in enumerate(rows):
        out = jnp.where(idx == r, v, out)
    return out


def rms_bwd(x, g, dy, dres, name, after=None, last=False):
    n, w = x.shape
    has_res = dres is not None
    deps = [] if after is None else [after]

    def body(x_ref, g_ref, dy_ref, *rest):
        rest = rest[:len(rest) - 3 - len(deps)] + rest[len(rest) - 3:]
        if has_res:
            dres_ref, dx_ref, dxm_ref, dg_ref = rest
        else:
            dx_ref, dxm_ref, dg_ref = rest
        i = pl.program_id(0)
        xv = x_ref[...]
        r = lax.rsqrt(jnp.mean(xv * xv, axis=-1, keepdims=True) + RMS_EPS)
        xh = xv * r
        dyv = dy_ref[...]
        dxh = dyv * g_ref[...]
        dx = r * (dxh - xh * jnp.mean(dxh * xh, axis=-1, keepdims=True))
        if has_res:
            dx = dx + dres_ref[...]
        dx_ref[...] = dx
        if last:
            @pl.when(i == 0)
            def _():
                dxm_ref[...] = dx
        else:
            rows = i * T + lax.broadcasted_iota(jnp.int32, (T, 1), 0)
            dxm_ref[...] = jnp.where(rows >= PAD, dx, 0.0).astype(BF16)
        part = jnp.sum(dyv * xh, axis=0, keepdims=True)

        @pl.when(i == 0)
        def _():
            dg_ref[...] = part

        @pl.when(i > 0)
        def _():
            dg_ref[...] += part

    row = pl.BlockSpec((T, w), lambda i: (i, 0))
    vec = pl.BlockSpec((1, w), lambda i: (0, 0))
    ins = [x, g, dy] + ([dres] if has_res else []) + deps
    if last:
        out_shape = (jax.ShapeDtypeStruct((n - T, w), F32), jax.ShapeDtypeStruct((T, w), F32))
        out_specs = (pl.BlockSpec((T, w), lambda i: (jnp.maximum(i - 1, 0), 0)), pl.BlockSpec((T, w), lambda i: (0, 0)))
    else:
        out_shape = (jax.ShapeDtypeStruct((n, w), F32), jax.ShapeDtypeStruct((n, w), BF16))
        out_specs = (row, row)
    return pl.pallas_call(
        body, name=name, out_shape=(*out_shape, jax.ShapeDtypeStruct((1, w), F32)), grid=(n // T,),
        in_specs=[row, vec, row] + ([row] if has_res else [])
        + [_HBM] * len(deps),
        out_specs=(*out_specs, vec))(*ins)


def _mm(name, a, b, *, grid, a_spec, b_spec, o_spec, out_shape, nt=False, ta=False, red=False, res=None,
        res_spec=None):
    def body(a_ref, b_ref, *rest):
        o_ref = rest[-1]
        av = a_ref[...]
        bv = b_ref[...]
        if bv.ndim == 3:
            bv = bv.reshape(bv.shape[0] * bv.shape[1], bv.shape[2])
        if ta:
            acc = lax.dot_general(av, bv, (((0,), (0,)), ((), ())), preferred_element_type=F32)
        else:
            acc = _dot(av, bv, nt)
        if red:
            k = pl.program_id(0)

            @pl.when(k == 0)
            def _():
                o_ref[...] = acc

            @pl.when(k > 0)
            def _():
                o_ref[...] += acc
        else:
            if res is not None:
                rows = lax.broadcasted_iota(jnp.int32, (acc.shape[0], 1), 0)
                acc = rest[0][...] + jnp.where(rows >= PAD, acc, 0.0)
            o_ref[...] = acc.astype(o_ref.dtype)

    ins = [a, b] + ([res] if res is not None else [])
    specs = [a_spec, b_spec] + ([res_spec] if res is not None else [])
    return pl.pallas_call(body, name=name, out_shape=out_shape, grid=grid, in_specs=specs, out_specs=o_spec,
                          compiler_params=_params(VMEM_BIG))(*ins)


def mm_nn(name, a, b, tn, out_dtype=F32, res=None):
    m, k = a.shape
    n = b.shape[1]
    return _mm(name, a, b, grid=(n // tn,),
               a_spec=pl.BlockSpec((m, k), lambda j: (0, 0)), b_spec=pl.BlockSpec((k, tn), lambda j: (0, j)),
               o_spec=pl.BlockSpec((m, tn), lambda j: (0, j)), out_shape=jax.ShapeDtypeStruct((m, n), out_dtype),
               res=res, res_spec=pl.BlockSpec((m, tn), lambda j: (0, j)))


def mm_nt(name, a, b, tn):
    m, k = a.shape
    n = b.shape[0]
    return _mm(name, a, b, grid=(n // tn,), nt=True,
               a_spec=pl.BlockSpec((m, k), lambda j: (0, 0)), b_spec=pl.BlockSpec((tn, k), lambda j: (j, 0)),
               o_spec=pl.BlockSpec((m, tn), lambda j: (0, j)), out_shape=jax.ShapeDtypeStruct((m, n), F32))


def mm_ta(name, a, b, tm, out_dtype=BF16):
    k, m = a.shape
    n = b.shape[1]
    return _mm(name, a, b, grid=(m // tm,), ta=True,
               a_spec=pl.BlockSpec((k, tm), lambda j: (0, j)), b_spec=pl.BlockSpec((k, n), lambda j: (0, 0)),
               o_spec=pl.BlockSpec((tm, n), lambda j: (j, 0)), out_shape=jax.ShapeDtypeStruct((m, n), out_dtype))


def _rms_mm(name, h, g, b, *, grid, b_spec, o_spec, out_shape, after=None):
    deps = [] if after is None else [after]

    def body(h_ref, g_ref, b_ref, *rest):
        o_ref, u_ref = rest[len(deps):]

        @pl.when(pl.program_id(0) == 0)
        def _():
            gv = g_ref[...]
            for c in range(NB):
                rows = slice(c * T, (c + 1) * T)
                xv = h_ref[rows, :]
                r = lax.rsqrt(jnp.mean(xv * xv, axis=-1, keepdims=True) + RMS_EPS)
                u_ref[rows, :] = (xv * r * gv).astype(BF16)

        o_ref[...] = _dot(u_ref[...], b_ref[...], nt=True)

    whole = lambda r: pl.BlockSpec((r, D), lambda j: (0, 0))
    return pl.pallas_call(
        body, name=name, out_shape=(out_shape, jax.ShapeDtypeStruct((L, D), BF16)), grid=grid,
        in_specs=[whole(L), whole(1), b_spec] + [_HBM] * len(deps), out_specs=(o_spec, whole(L)),
        compiler_params=_params(VMEM_BIG))(h, g, b, *deps)


def out_proj_fwd(name, o, og, y_cv, y_ret, w, res, tn):
    wa = o.shape[1]

    def body(o_ref, og_ref, cv_ref, ret_ref, w_ref, r_ref, out_ref, ycat_ref):
        @pl.when(pl.program_id(0) == 0)
        def _():
            gv = og_ref[...]
            for c in range(NB):
                rows = slice(c * T, (c + 1) * T)
                ycat_ref[rows, 0:wa] = _rms_rows(o_ref[rows, :], gv)[0].astype(BF16)
                ycat_ref[rows, wa:wa + CV] = cv_ref[rows, :]
                ycat_ref[rows, wa + CV:D] = ret_ref[rows, :]

        acc = _dot(ycat_ref[...], w_ref[...])
        rows = lax.broadcasted_iota(jnp.int32, (L, 1), 0)
        out_ref[...] = r_ref[...] + jnp.where(rows >= PAD, acc, 0.0)

    whole = lambda w_: pl.BlockSpec((L, w_), lambda j: (0, 0))
    tile = pl.BlockSpec((L, tn), lambda j: (0, j))
    return pl.pallas_call(
        body, name=name, out_shape=(jax.ShapeDtypeStruct((L, D), F32), jax.ShapeDtypeStruct((L, D), BF16)),
        grid=(D // tn,),
        in_specs=[whole(wa), pl.BlockSpec((1, wa), lambda j: (0, 0)), whole(CV), whole(RW),
                  pl.BlockSpec((D, tn), lambda j: (0, j)), tile],
        out_specs=(tile, whole(D)), compiler_params=_params(VMEM_BIG))(o, og, y_cv, y_ret, w, res)


def in_proj_fwd(name, h, g, wt, tn, after=None):
    return _rms_mm(name, h, g, wt, grid=(IN_W // tn,), b_spec=pl.BlockSpec((tn, D), lambda j: (j, 0)),
                   o_spec=pl.BlockSpec((L, tn), lambda j: (0, j)), out_shape=jax.ShapeDtypeStruct((L, IN_W), F32),
                   after=after)


def ffn_up_fwd(name, h, g, wupt):
    return _rms_mm(name, h, g, wupt, grid=(NDEV,), b_spec=pl.BlockSpec((None, UPW, D), lambda j: (j, 0, 0)),
                   o_spec=pl.BlockSpec((None, L, UPW), lambda j: (j, 0, 0)),
                   out_shape=jax.ShapeDtypeStruct((NDEV, L, UPW), F32))


def _slab_specs():
    gate = pl.BlockSpec((None, L, UPW), lambda j: (jnp.minimum(j, NPAIR - 1), 0, 0))
    up = pl.BlockSpec((None, L, UPW), lambda j: (jnp.maximum(j - NPAIR, 0), 0, 0))
    return gate, up


def ffn_up_dx(name, dfg, dfu, wupt):
    def body(g_ref, u_ref, b_ref, o_ref):
        j = pl.program_id(0)

        @pl.when(j == 0)
        def _():
            o_ref[...] = _dot(g_ref[...], b_ref[...])

        @pl.when(jnp.logical_and(j > 0, j < NPAIR))
        def _():
            o_ref[...] += _dot(g_ref[...], b_ref[...])

        @pl.when(j >= NPAIR)
        def _():
            o_ref[...] += _dot(u_ref[...], b_ref[...])

    gate, up = _slab_specs()
    return pl.pallas_call(
        body, name=name, out_shape=jax.ShapeDtypeStruct((L, D), F32), grid=(NDEV,),
        in_specs=[gate, up, pl.BlockSpec((None, UPW, D), lambda j: (j, 0, 0))],
        out_specs=pl.BlockSpec((L, D), lambda j: (0, 0)), compiler_params=_params(VMEM_BIG))(dfg, dfu, wupt)


def ffn_up_dw(name, u2, dfg, dfu):
    tdot = lambda a, b: lax.dot_general(a, b, (((0,), (0,)), ((), ())), preferred_element_type=F32)

    def body(a_ref, g_ref, u_ref, o_ref):
        j = pl.program_id(0)

        @pl.when(j < NPAIR)
        def _():
            o_ref[...] = tdot(g_ref[...], a_ref[...]).astype(o_ref.dtype)

        @pl.when(j >= NPAIR)
        def _():
            o_ref[...] = tdot(u_ref[...], a_ref[...]).astype(o_ref.dtype)

    gate, up = _slab_specs()
    return pl.pallas_call(
        body, name=name, out_shape=jax.ShapeDtypeStruct((NDEV, UPW, D), BF16), grid=(NDEV,),
        in_specs=[pl.BlockSpec((L, D), lambda j: (0, 0)), gate, up],
        out_specs=pl.BlockSpec((None, UPW, D), lambda j: (j, 0, 0)), compiler_params=_params(VMEM_BIG))(u2, dfg, dfu)


def _ffn_down_tile(a_ref, b_ref, r_ref, tn):
    acc = jnp.zeros((L, tn), F32)
    for g in range(NPAIR):
        bv = b_ref[2 * g:2 * g + 2]
        acc = acc + _dot(a_ref[g], bv.reshape(2 * DNR, tn))
    rows = lax.broadcasted_iota(jnp.int32, (L, 1), 0)
    return r_ref[...] + jnp.where(rows >= PAD, acc, 0.0)


def _ffn_down_specs(tn):
    return [pl.BlockSpec((NPAIR, L, UPW), lambda j: (0, 0, 0)), pl.BlockSpec((NDEV, DNR, tn), lambda j: (0, 0, j)),
            pl.BlockSpec((L, tn), lambda j: (0, j))]


def ffn_down_fwd(name, a, wdn, res, tn=256):
    def body(a_ref, b_ref, r_ref, o_ref):
        o_ref[...] = _ffn_down_tile(a_ref, b_ref, r_ref, tn)

    return pl.pallas_call(
        body, name=name, out_shape=jax.ShapeDtypeStruct((L, D), F32), grid=(D // tn,),
        in_specs=_ffn_down_specs(tn), out_specs=pl.BlockSpec((L, tn), lambda j: (0, j)),
        compiler_params=_params(VMEM_BIG))(a, wdn, res)


def ffn_down_loss(name, a, wdn, res, target, tn=256):
    def body(a_ref, b_ref, r_ref, t_ref, dh_ref, dhm_ref, loss_ref):
        j = pl.program_id(0)
        hv = _ffn_down_tile(a_ref, b_ref, r_ref, tn)
        e = hv[T:] - t_ref[...]
        dh = e * (1.0 / D)
        dh_ref[0:T, :] = jnp.zeros((T, tn), F32)
        dh_ref[T:, :] = dh
        dhm_ref[0:T, :] = jnp.zeros((T, tn), BF16)
        dhm_ref[T:, :] = dh.astype(BF16)
        part = jnp.sum(jnp.sum(e * e, axis=1, keepdims=True), axis=0, keepdims=True) * (0.5 / D)

        @pl.when(j == 0)
        def _():
            loss_ref[...] = jnp.broadcast_to(part, loss_ref.shape)

        @pl.when(j > 0)
        def _():
            loss_ref[...] += jnp.broadcast_to(part, loss_ref.shape)

    tile = pl.BlockSpec((L, tn), lambda j: (0, j))
    return pl.pallas_call(
        body, name=name,
        out_shape=(jax.ShapeDtypeStruct((L, D), F32), jax.ShapeDtypeStruct((L, D), BF16),
                   jax.ShapeDtypeStruct((8, 128), F32)),
        grid=(D // tn,),
        in_specs=_ffn_down_specs(tn) + [pl.BlockSpec((SEQ, tn), lambda j: (0, j))],
        out_specs=(tile, tile, pl.BlockSpec((8, 128), lambda j: (0, 0))),
        compiler_params=_params(VMEM_BIG))(a, wdn, res, target)


def ffn_down_dx(name, dh, wdn):
    return _mm(name, dh, wdn, grid=(NPAIR,), nt=True,
               a_spec=pl.BlockSpec((L, D), lambda g: (0, 0)),
               b_spec=pl.BlockSpec((2, DNR, D), lambda g: (g, 0, 0)),
               o_spec=pl.BlockSpec((None, L, UPW), lambda g: (g, 0, 0)),
               out_shape=jax.ShapeDtypeStruct((NPAIR, L, UPW), F32))


def ffn_down_dw(name, a, dh):
    return _mm(name, a, dh, grid=(NPAIR,), ta=True,
               a_spec=pl.BlockSpec((None, L, UPW), lambda g: (g, 0, 0)),
               b_spec=pl.BlockSpec((L, D), lambda g: (0, 0)),
               o_spec=pl.BlockSpec((UPW, D), lambda g: (g, 0)),
               out_shape=jax.ShapeDtypeStruct((D_FF, D), BF16))


_SLOPES = [2.0 ** (-8.0 * (h + 1) / NQ) for h in range(NQ)]
_SCALE = HD ** -0.5


QR = GQA * T
KC = 3 * T


def _attn_geometry(g, n, sink_ref):
    row = lax.broadcasted_iota(jnp.int32, (QR, KC), 0)
    col = lax.broadcasted_iota(jnp.int32, (QR, KC), 1)
    i = row & (T - 1)
    j = col & (T - 1)
    blk = col // T
    d_meta = n * T + i - j
    ok_meta = (j >= PAD) & (d_meta >= 0)
    ok_prev = j > i + jnp.where(n >= 2, 0, T)
    ok_cur = j <= i - jnp.where(n >= 1, 0, T)
    ok = ((blk == 0) & ok_meta) | ((blk == 1) & ok_prev) | ((blk == 2) & ok_cur)
    dist = jnp.where(blk == 0, jnp.minimum(d_meta, T), jnp.where(blk == 1, T + i - j, i - j)).astype(F32)
    head = lax.broadcasted_iota(jnp.int32, (QR, 1), 0) // T
    slope = jnp.zeros((QR, 1), F32)
    sink = jnp.zeros((QR, 1), F32)
    for hh in range(GQA):
        slope = jnp.where(head == hh, jnp.where(g == 0, _SLOPES[hh], _SLOPES[GQA + hh]), slope)
        sink = jnp.where(head == hh, sink_ref[g * GQA + hh], sink)
    return ok, slope * dist, sink


def _attn_probs(qn, keys, geo):
    ok, penalty, sink = geo
    s = jnp.where(ok, _dot(qn, keys, nt=True) * _SCALE - penalty, NEG)
    m = jnp.maximum(jnp.max(s, axis=-1, keepdims=True), sink)
    e = jnp.exp(s - m)
    e_sink = jnp.exp(sink - m)
    inv = 1.0 / (e.sum(axis=-1, keepdims=True) + e_sink)
    return e, inv, e_sink * inv


def _rms_rows(x, g):
    r = lax.rsqrt(jnp.mean(x * x, axis=-1, keepdims=True) + RMS_EPS)
    xh = x * r
    return xh * g, xh, r


def _rms_rows_bwd(dy, g, xh, r):
    dxh = dy * g
    return r * (dxh - xh * jnp.mean(dxh * xh, axis=-1, keepdims=True))


def _rows3(m_ref, p_ref, c_ref):
    return jnp.concatenate([m_ref[...], p_ref[...], c_ref[...]], axis=0)


_ATT_K, _ATT_V = 4, 5


def _nat_specs():
    qspec = pl.BlockSpec((T, NQ * HD), lambda n: (n, 0))
    kv = lambda col: (pl.BlockSpec((T, NKV * HD), lambda n: (n, col)),
                      pl.BlockSpec((T, NKV * HD), lambda n: (jnp.maximum(n - 1, 0), col)),
                      pl.BlockSpec((T, NKV * HD), lambda n: (0, col)))
    return qspec, kv(_ATT_K), kv(_ATT_V)


def _group(g, x, w=HD):
    return x[:, g * w:(g + 1) * w]


def _stack_heads(x):
    return jnp.concatenate([x[:, hh * HD:(hh + 1) * HD] for hh in range(GQA)], axis=0)


def _unstack_heads(x):
    return jnp.concatenate([x[hh * T:(hh + 1) * T] for hh in range(GQA)], axis=1)


def attn_fwd(proj, qg, kg, sinks, name):
    def body(sink_ref, q_ref, kc_ref, kp_ref, km_ref, vc_ref, vp_ref, vm_ref, qg_ref, kg_ref, o_ref):
        n = pl.program_id(0)
        k_all = _rows3(km_ref, kp_ref, kc_ref)
        v_all = _rows3(vm_ref, vp_ref, vc_ref)
        q_all = q_ref[...]
        outs = []
        for g in range(NKV):
            geo = _attn_geometry(g, n, sink_ref)
            keys = _bf(_rms_rows(_group(g, k_all), kg_ref[...])[0])
            vals = _bf(_group(g, v_all))
            qn = _bf(_rms_rows(_stack_heads(_group(g, q_all, GQA * HD)), qg_ref[...])[0])
            e, inv, _ = _attn_probs(qn, keys, geo)
            outs.append(_unstack_heads(_dot(_bf(e), vals) * inv))
        o_ref[...] = jnp.concatenate(outs, axis=1)

    qspec, (kc, kp, km), (vc, vp, vm) = _nat_specs()
    vec = pl.BlockSpec((1, HD), lambda n: (0, 0))
    return pl.pallas_call(
        body, name=name, out_shape=jax.ShapeDtypeStruct((L, NQ * HD), F32), grid=(NB,),
        in_specs=[pl.BlockSpec(memory_space=pltpu.SMEM), qspec, kc, kp, km, vc, vp, vm, vec, vec],
        out_specs=qspec)(sinks, proj, proj, proj, proj, proj, proj, proj, qg, kg)


def attn_bwd(proj, o, dycat, og, qg, kg, sinks, name):
    def body(sink_ref, q_ref, kc_ref, kp_ref, km_ref, vc_ref, vp_ref, vm_ref, o_ref, dy_ref, og_ref, qg_ref, kg_ref,
             dq_ref, dkc_ref, dkp_ref, dkm_ref, dvc_ref, dvp_ref, dvm_ref, dqg_ref, dkg_ref, dsk_ref, dog_ref):
        n = pl.program_id(0)

        @pl.when(n == 0)
        def _():
            for r in (dkm_ref, dvm_ref, dqg_ref, dkg_ref, dsk_ref, dog_ref):
                r[...] = jnp.zeros_like(r)

        ogv = og_ref[...]
        dyv = dy_ref[...]
        _, oh, orr = _rms_rows(o_ref[...], ogv)
        do_all = _rms_rows_bwd(dyv, ogv, oh, orr)
        dog_ref[...] += jnp.sum(dyv * oh, axis=0, keepdims=True)

        kgv = kg_ref[...]
        qgv = qg_ref[...]
        k_all = _rows3(km_ref, kp_ref, kc_ref)
        v_all = _rows3(vm_ref, vp_ref, vc_ref)
        q_all = q_ref[...]
        dq, dk, dv, dsk_rows = [], [], [], []
        dqg_acc = jnp.zeros((1, HD), F32)
        dkg_acc = jnp.zeros((1, HD), F32)
        for g in range(NKV):
            geo = _attn_geometry(g, n, sink_ref)
            kn_f, kh, kr = _rms_rows(_group(g, k_all), kgv)
            keys = _bf(kn_f)
            vals = _bf(_group(g, v_all))
            qn_f, qh, qr = _rms_rows(_stack_heads(_group(g, q_all, GQA * HD)), qgv)
            qn = _bf(qn_f)
            e, inv, p_sink = _attn_probs(qn, keys, geo)
            dob = _bf(_stack_heads(_group(g, do_all, GQA * HD)))
            p = e * inv
            dp = _dot(dob, vals, nt=True)
            delta = (p * dp).sum(axis=-1, keepdims=True)
            for hh in range(GQA):
                part = -jnp.sum((p_sink * delta)[hh * T:(hh + 1) * T], axis=0, keepdims=True)
                dsk_rows.append(jnp.broadcast_to(part, (1, 128)))
            ds = p * (dp - delta)
            dqn = _dot(_bf(ds), keys) * _SCALE
            dq.append(_unstack_heads(_rms_rows_bwd(dqn, qgv, qh, qr)))
            dqg_acc = dqg_acc + jnp.sum(dqn * qh, axis=0, keepdims=True)
            dkn = _dot(_bf(ds.T), qn) * _SCALE
            dkg_acc = dkg_acc + jnp.sum(dkn * kh, axis=0, keepdims=True)
            dk.append(_rms_rows_bwd(dkn, kgv, kh, kr))
            dv.append(_dot(_bf(p.T), dob))
        dq_ref[...] = jnp.concatenate(dq, axis=1).astype(dq_ref.dtype)
        dk_all = jnp.concatenate(dk, axis=1)
        dv_all = jnp.concatenate(dv, axis=1)
        dkm_ref[...] += dk_all[0:T]
        dkp_ref[...] = dk_all[T:2 * T]
        dkc_ref[...] = dk_all[2 * T:3 * T]
        dvm_ref[...] += dv_all[0:T]
        dvp_ref[...] = dv_all[T:2 * T]
        dvc_ref[...] = dv_all[2 * T:3 * T]
        dsk_ref[...] += _stack_rows(dsk_rows)
        dqg_ref[...] += dqg_acc
        dkg_ref[...] += dkg_acc

    qspec, (kc, kp, km), (vc, vp, vm) = _nat_specs()
    vec = pl.BlockSpec((1, HD), lambda n: (0, 0))
    cur = pl.BlockSpec((T, NKV * HD), lambda n: (n, 0))
    meta = pl.BlockSpec((T, NKV * HD), lambda n: (0, 0))
    kv_shape = jax.ShapeDtypeStruct((L, NKV * HD), F32)
    meta_shape = jax.ShapeDtypeStruct((T, NKV * HD), F32)
    vec_shape = jax.ShapeDtypeStruct((1, HD), F32)
    ogvec = pl.BlockSpec((1, NQ * HD), lambda n: (0, 0))
    return pl.pallas_call(
        body, name=name,
        out_shape=(jax.ShapeDtypeStruct((L, NQ * HD), BF16), kv_shape, kv_shape, meta_shape, kv_shape, kv_shape,
                   meta_shape, vec_shape, vec_shape, jax.ShapeDtypeStruct((NQ, 128), F32),
                   jax.ShapeDtypeStruct((1, NQ * HD), F32)),
        grid=(NB,),
        in_specs=[pl.BlockSpec(memory_space=pltpu.SMEM), qspec, kc, kp, km, vc, vp, vm, qspec, qspec, ogvec, vec, vec],
        out_specs=(qspec, cur, cur, meta, cur, cur, meta, vec, vec,
                   pl.BlockSpec((NQ, 128), lambda n: (0, 0)), ogvec))(
            sinks, proj, proj, proj, proj, proj, proj, proj, o, dycat, og, qg, kg)


RW = RH * HD


def _ret_tables():
    h = np.arange(RH, dtype=np.float64)
    lg = np.log1p(-np.exp2(-5.0 - h))
    idx = np.arange(T, dtype=np.float64)
    diff = idx[:, None] - idx[None, :]
    decay = np.where(diff[None] >= 0, np.exp(np.maximum(diff, 0.0)[None] * lg[:, None, None]), 0.0)
    zeta = np.exp((T - 1 - idx)[None, :] * lg[:, None])
    xi = np.exp((idx + 1.0)[None, :] * lg[:, None])
    cd = np.exp(T * lg)
    lanes = lambda a: np.repeat(a.T, HD, axis=1)
    head_of = np.arange(RW) // HD
    same = (head_of[:, None] == head_of[None, :]).astype(np.float64)
    f = lambda a: jnp.asarray(a, F32)
    return dict(decay=f(decay), zeta=f(lanes(zeta)), xi=f(lanes(xi)), cd=f(np.repeat(cd, HD)[None, :]),
                head=f((head_of[None, :] == np.arange(RH)[:, None]).astype(np.float64)[:, None, :]),
                same=f(same), avg=jnp.asarray(same / HD, BF16))


def _seg_mean(x, avg):
    hi = _bf(x)
    lo = _bf(x - hi.astype(F32))
    return _dot(hi, avg) + _dot(lo, avg)


def _ret_specs(col0, order):
    return lambda col: pl.BlockSpec((T, RW), lambda i: (order(i), col0 + col))


def _ret_chunk(q, kf, v, s, tb):
    dec, xi, head = tb
    vb = _bf(v)
    kb = _bf(kf)
    y = _dot(_bf(q * xi), _bf(s))
    a = []
    for h in range(RH):
        a.append(_dot(_bf(q * head[h]), kb, nt=True) * dec[h])
        y = y + head[h] * _dot(_bf(a[h]), vb)
    return a, y


def _gn_rows(y):
    mu = jnp.mean(y, axis=-1, keepdims=True)
    yc = y - mu
    rstd = lax.rsqrt(jnp.mean(yc * yc, axis=-1, keepdims=True) + LN_EPS)
    return yc * rstd, rstd


_RET_COL0 = _RET_Q = 5


def _ret_consts(tb):
    names = ("decay", "zeta", "xi", "cd", "head", "same", "avg")
    full = lambda a: pl.BlockSpec(a.shape, lambda i: (0,) * a.ndim)
    return [tb[n] for n in names], [full(tb[n]) for n in names]


def _ret_inputs(n, q_ref, k_ref, v_ref):
    rows = n * T + lax.broadcasted_iota(jnp.int32, (T, 1), 0)
    valid = rows >= PAD
    return q_ref[...], jnp.where(valid, k_ref[...] * (HD ** -0.5), 0.0), v_ref[...], valid


def ret_fwd(proj, gng, tb, name):
    def body(q_ref, k_ref, v_ref, g_ref, dec_ref, zeta_ref, xi_ref, cd_ref, head_ref, same_ref, avg_ref, gng_ref,
             y_ref, st_ref, s_scr):
        n = pl.program_id(0)

        @pl.when(n == 0)
        def _():
            s_scr[...] = jnp.zeros_like(s_scr)

        s = s_scr[...]
        st_ref[...] = s
        q, kf, v, _ = _ret_inputs(n, q_ref, k_ref, v_ref)
        _, y = _ret_chunk(q, kf, v, s, (dec_ref, xi_ref[...], head_ref))
        s_scr[...] = cd_ref[...] * s + same_ref[...] * _dot(_bf((kf * zeta_ref[...]).T), _bf(v))
        avg = avg_ref[...]
        yc = y - _seg_mean(y, avg)
        yh = yc * lax.rsqrt(_seg_mean(yc * yc, avg) + LN_EPS)
        gv = g_ref[...]
        y_ref[...] = (gv * _sig(gv) * (yh * gng_ref[...])).astype(y_ref.dtype)

    col = _ret_specs(_RET_COL0, lambda i: i)
    consts, cspecs = _ret_consts(tb)
    return pl.pallas_call(
        body, name=name,
        out_shape=(jax.ShapeDtypeStruct((L, RW), BF16), jax.ShapeDtypeStruct((NB, RW, RW), F32)),
        grid=(NB,),
        in_specs=[col(0), col(1), col(2), col(3)] + cspecs + [pl.BlockSpec((1, RW), lambda i: (0, 0))],
        out_specs=(pl.BlockSpec((T, RW), lambda i: (i, 0)), pl.BlockSpec((None, RW, RW), lambda i: (i, 0, 0))),
        scratch_shapes=[pltpu.VMEM((RW, RW), F32)])(proj, proj, proj, proj, *consts, gng)


def ret_bwd(proj, dycat, states, gng, tb, name):
    def body(q_ref, k_ref, v_ref, g_ref, do_ref, st_ref, dec_ref, zeta_ref, xi_ref, cd_ref, head_ref, same_ref,
             avg_ref, gng_ref, d_ref, dgn_ref, ds_scr):
        i = pl.program_id(0)
        n = NB - 1 - i

        @pl.when(i == 0)
        def _():
            ds_scr[...] = jnp.zeros_like(ds_scr)

        dsn = _bf(ds_scr[...])
        s = st_ref[...]
        sb = _bf(s)
        q, kf, v, valid = _ret_inputs(n, q_ref, k_ref, v_ref)
        xi, zeta, avg = xi_ref[...], zeta_ref[...], avg_ref[...]
        a, y = _ret_chunk(q, kf, v, s, (dec_ref, xi, head_ref))
        yc = y - _seg_mean(y, avg)
        rstd = lax.rsqrt(_seg_mean(yc * yc, avg) + LN_EPS)
        yh = yc * rstd
        gv = g_ref[...]
        sg = _sig(gv)
        sil = gv * sg
        gn = gng_ref[...]
        dout = do_ref[...]
        d_ref[:, 3 * RW:4 * RW] = (dout * (yh * gn) * (sg * (1.0 + gv * (1.0 - sg)))).astype(d_ref.dtype)
        dyh = dout * sil * gn
        part = jnp.sum(dout * sil * yh, axis=0, keepdims=True)

        @pl.when(i == 0)
        def _():
            dgn_ref[...] = part

        @pl.when(i > 0)
        def _():
            dgn_ref[...] += part

        dy = rstd * (dyh - _seg_mean(dyh, avg) - yh * _seg_mean(dyh * yh, avg))
        dyb = _bf(dy)
        vb, kb, qb = _bf(v), _bf(kf), _bf(q)
        dq = _dot(dyb, sb, nt=True) * xi
        dkf = _dot(vb, dsn, nt=True) * zeta
        dv = _dot(_bf(kf * zeta), dsn)
        for h in range(RH):
            m = head_ref[h]
            da = _dot(_bf(dy * m), vb, nt=True) * dec_ref[h]
            dv = dv + m * _dot(_bf(a[h].T), dyb)
            dq = dq + m * _dot(_bf(da), kb)
            dkf = dkf + m * _dot(_bf(da.T), qb)
        d_ref[:, 0:RW] = dq.astype(d_ref.dtype)
        d_ref[:, RW:2 * RW] = jnp.where(valid, dkf * (HD ** -0.5), 0.0).astype(d_ref.dtype)
        d_ref[:, 2 * RW:3 * RW] = dv.astype(d_ref.dtype)
        ds_scr[...] = cd_ref[...] * ds_scr[...] + same_ref[...] * _dot(_bf((q * xi).T), dyb)

    back = lambda i: NB - 1 - i
    col = _ret_specs(_RET_COL0, back)
    consts, cspecs = _ret_consts(tb)
    vec = pl.BlockSpec((1, RW), lambda i: (0, 0))
    return pl.pallas_call(
        body, name=name,
        out_shape=(jax.ShapeDtypeStruct((L, 4 * RW), BF16), jax.ShapeDtypeStruct((1, RW), F32)),
        grid=(NB,),
        in_specs=[col(0), col(1), col(2), col(3), _ret_specs(3, back)(0),
                  pl.BlockSpec((None, RW, RW), lambda i: (back(i), 0, 0))] + cspecs + [vec],
        out_specs=(pl.BlockSpec((T, 4 * RW), lambda i: (back(i), 0)), vec),
        scratch_shapes=[pltpu.VMEM((RW, RW), F32)])(proj, proj, proj, proj, dycat, states, *consts, gng)


HALO = 32
TAP0 = HALO - (CONV_K - 1)


def _conv_specs():
    cur = lambda col=0: pl.BlockSpec((T, CV), lambda c: (c, col))
    before = lambda col=0: pl.BlockSpec((HALO, CV), lambda c: (jnp.maximum(c * (T // HALO) - 1, 0), col))
    after = pl.BlockSpec((HALO, CV), lambda c: (jnp.minimum((c + 1) * (T // HALO), L // HALO - 1), 0))
    full = lambda r, w: pl.BlockSpec((r, w), lambda c: (0, 0))
    return cur, before, after, full


_CONV_A, _CONV_B = 3, 4
_DY_CONV = 2


def _conv_post(xc, lg_ref, lb_ref):
    xh, rstd = _gn_rows(xc)
    z = xh * lg_ref[...] + lb_ref[...]
    return xh, rstd, z, _sig(z)


def conv_fwd(proj, w, b, lg, lb, pw, og, name):
    def body(ca_ref, cah_ref, cb_ref, cbh_ref, w_ref, b_ref, lg_ref, lb_ref, pw_ref, og_ref, y_ref, xc_ref, u_scr):
        c = pl.program_id(0)
        u_scr[0:HALO, :] = jnp.where(c > 0, cah_ref[...] * _sig(cbh_ref[...]), 0.0)
        u_scr[HALO:HALO + T, :] = ca_ref[...] * _sig(cb_ref[...])
        acc = jnp.zeros((T, CV), F32)
        for k in range(CONV_K):
            acc = acc + w_ref[k:k + 1, :] * u_scr[TAP0 + k:TAP0 + k + T, :]
        xc = acc + b_ref[...]
        xc_ref[...] = xc
        _, _, z, sg = _conv_post(xc, lg_ref, lb_ref)
        zp = _dot(_bf(z * sg), pw_ref[...])
        y_ref[...] = _rms_rows(zp, og_ref[...])[0].astype(y_ref.dtype)

    cur, before, _, full = _conv_specs()
    vec = full(1, CV)
    seq = jax.ShapeDtypeStruct((L, CV), F32)
    return pl.pallas_call(
        body, name=name, out_shape=(jax.ShapeDtypeStruct((L, CV), BF16), seq), grid=(NB,),
        in_specs=[cur(_CONV_A), before(_CONV_A), cur(_CONV_B), before(_CONV_B), full(CONV_K, CV), vec, vec, vec,
                  full(CV, CV), vec],
        out_specs=(cur(), cur()),
        scratch_shapes=[pltpu.VMEM((HALO + T, CV), F32)])(proj, proj, proj, proj, w, b, lg, lb, pw, og)


def conv_bwd(proj, xc, dycat, w, lg, lb, pw, og, name):
    def body1(xc_ref, dy_ref, lg_ref, lb_ref, pw_ref, og_ref, dxc_ref, db_ref, dlg_ref, dlb_ref, dog_ref, dpw_ref):
        c = pl.program_id(0)

        @pl.when(c == 0)
        def _():
            for r in (db_ref, dlg_ref, dlb_ref, dog_ref, dpw_ref):
                r[...] = jnp.zeros_like(r)

        xh, rstd, z, sg = _conv_post(xc_ref[...], lg_ref, lb_ref)
        s = z * sg
        zp = _dot(_bf(s), pw_ref[...])
        ogv = og_ref[...]
        _, zph, r2 = _rms_rows(zp, ogv)
        dyv = dy_ref[...]
        dog_ref[...] += jnp.sum(dyv * zph, axis=0, keepdims=True)
        dzpb = _bf(_rms_rows_bwd(dyv, ogv, zph, r2))
        dpw_ref[...] += _dot(_bf(s.T), dzpb)
        dz = _dot(dzpb, pw_ref[...], nt=True) * (sg * (1.0 + z * (1.0 - sg)))
        dlg_ref[...] += jnp.sum(dz * xh, axis=0, keepdims=True)
        dlb_ref[...] += jnp.sum(dz, axis=0, keepdims=True)
        dxh = dz * lg_ref[...]
        dxc = rstd * (dxh - jnp.mean(dxh, axis=-1, keepdims=True) - xh * jnp.mean(dxh * xh, axis=-1, keepdims=True))
        db_ref[...] += jnp.sum(dxc, axis=0, keepdims=True)
        dxc_ref[...] = dxc

    def body2(dx_ref, dxa_ref, ca_ref, cb_ref, w_ref, dca_ref, dcb_ref, dw_ref, d_scr):
        c = pl.program_id(0)

        @pl.when(c == 0)
        def _():
            dw_ref[...] = jnp.zeros_like(dw_ref)

        d_scr[0:T, :] = dx_ref[...]
        d_scr[T:T + HALO, :] = jnp.where(c < NB - 1, dxa_ref[...], 0.0)
        ca = ca_ref[...]
        sg = _sig(cb_ref[...])
        u = ca * sg
        du = jnp.zeros((T, CV), F32)
        for k in range(CONV_K):
            off = CONV_K - 1 - k
            dj = d_scr[off:off + T, :]
            du = du + w_ref[k:k + 1, :] * dj
            dw_ref[k:k + 1, :] += jnp.sum(u * dj, axis=0, keepdims=True)
        dca_ref[...] = (du * sg).astype(dca_ref.dtype)
        dcb_ref[...] = (du * ca * sg * (1.0 - sg)).astype(dcb_ref.dtype)

    cur, _, after, full = _conv_specs()
    seq = jax.ShapeDtypeStruct((L, CV), F32)
    vsh = jax.ShapeDtypeStruct((1, CV), F32)
    vec = full(1, CV)
    dxc, db, dlg, dlb, dog, dpw = pl.pallas_call(
        body1, name=name + "_a",
        out_shape=(seq, vsh, vsh, vsh, vsh, jax.ShapeDtypeStruct((CV, CV), F32)),
        grid=(NB,),
        in_specs=[cur(), cur(_DY_CONV), vec, vec, full(CV, CV), vec],
        out_specs=(cur(), vec, vec, vec, vec, full(CV, CV)))(xc, dycat, lg, lb, pw, og)
    dca, dcb, dw = pl.pallas_call(
        body2, name=name + "_b", grid=(NB,),
        out_shape=(jax.ShapeDtypeStruct((L, CV), BF16), jax.ShapeDtypeStruct((L, CV), BF16),
                   jax.ShapeDtypeStruct((CONV_K, CV), F32)),
        in_specs=[cur(), after, cur(_CONV_A), cur(_CONV_B), full(CONV_K, CV)],
        out_specs=(cur(), cur(), full(CONV_K, CV)),
        scratch_shapes=[pltpu.VMEM((T + HALO, CV), F32)])(dxc, dxc, proj, proj, w)
    return dca, dcb, dw, db, dlg, dlb, dog, dpw


HR = 8
PP = NPAIR


def _ffn_specs():
    cur = lambda off: pl.BlockSpec((PP, T, UPW), lambda j, c: (j + off // PP, c, 0))
    before = lambda off: pl.BlockSpec(
        (PP, HR, UPW), lambda j, c: (j + off // PP, jnp.maximum(c * (T // HR) - 1, 0), 0))
    after = lambda off: pl.BlockSpec(
        (PP, HR, UPW), lambda j, c: (j + off // PP, jnp.minimum((c + 1) * (T // HR), L // HR - 1), 0))
    wspec = lambda off, r: pl.BlockSpec((PP, r, UPW), lambda j, c: (j + off // PP, 0, 0))
    return cur, before, after, wspec


HR16 = 16


def _ffn_after16():
    return pl.BlockSpec((PP, HR16, UPW), lambda j, c: (j, jnp.minimum((c + 1) * (T // HR16), L // HR16 - 1), 0))


def ffn_act_fwd(f, w, b, name):
    strip = 16

    def body(*refs):
        for p in range(PP):
            one(*[r.at[p] for r in refs[:-2]], *refs[-2:])

    def one(fg_ref, fgh_ref, fu_ref, fuh_ref, wg_ref, wu_ref, bg_ref, bu_ref, a_ref, gc_ref, uc_ref, hg_scr, hu_scr):
        c = pl.program_id(1)
        for scr, x_ref, xh_ref in ((hg_scr, fg_ref, fgh_ref), (hu_scr, fu_ref, fuh_ref)):
            scr[0:HR, :] = jnp.where(c > 0, xh_ref[...], 0.0)
            scr[HR:HR + strip, :] = x_ref[0:strip, :]
        wg = [wg_ref[k:k + 1, :] for k in range(FFN_K)]
        wu = [wu_ref[k:k + 1, :] for k in range(FFN_K)]
        bg, bu = bg_ref[...], bu_ref[...]

        def conv(src, base, w, b):
            acc = b
            for k in range(FFN_K):
                o = base - (FFN_K - 1) + k
                acc = acc + w[k] * src[o:o + strip, :]
            return acc

        for s in range(T // strip):
            r = s * strip
            gate = conv(hg_scr, HR, wg, bg) if s == 0 else conv(fg_ref, r, wg, bg)
            up = conv(hu_scr, HR, wu, bu) if s == 0 else conv(fu_ref, r, wu, bu)
            gc_ref[r:r + strip, :] = gate.astype(BF16)
            uc_ref[r:r + strip, :] = up.astype(BF16)
            a_ref[r:r + strip, :] = (gate * _sig(gate) * up).astype(BF16)

    cur, before, _, wspec = _ffn_specs()
    pre = jax.ShapeDtypeStruct((NPAIR, L, UPW), BF16)
    return pl.pallas_call(
        body, name=name, out_shape=(jax.ShapeDtypeStruct((NPAIR, L, UPW), BF16), pre, pre), grid=(NPAIR // PP, NB),
        in_specs=[cur(0), before(0), cur(NPAIR), before(NPAIR), wspec(0, FFN_K), wspec(NPAIR, FFN_K),
                  wspec(0, 1), wspec(NPAIR, 1)],
        out_specs=(cur(0), cur(0), cur(0)),
        scratch_shapes=[pltpu.VMEM((HR + strip, UPW), F32), pltpu.VMEM((HR + strip, UPW), F32)])(
            f, f, f, f, w, w, b, b)


def ffn_act_bwd(f, gc, uc, da, w, name):
    ext = T + HR
    sub = 8

    def body(*refs):
        for p in range(PP):
            one(*[r.at[p] for r in refs[:-2]], *refs[-2:])

    def one(fg_ref, fu_ref, gc_ref, gca_ref, uc_ref, uca_ref, da_ref, daa_ref, wg_ref, wu_ref,
            dfg_ref, dfu_ref, dwg_ref, dwu_ref, dbg_ref, dbu_ref, dg_scr, du_scr):
        c = pl.program_id(1)
        wg = [wg_ref[k:k + 1, :] for k in range(FFN_K)]
        wu = [wu_ref[k:k + 1, :] for k in range(FFN_K)]

        for r in range(0, ext, HR16):
            rows = min(HR16, ext - r)
            if r < T:
                gate, up, dav = gc_ref[r:r + HR16, :].astype(F32), uc_ref[r:r + HR16, :].astype(F32), da_ref[r:r + HR16, :]
            else:
                gate, up = gca_ref[...].astype(F32)[0:rows], uca_ref[...].astype(F32)[0:rows]
                dav = jnp.where(c < NB - 1, daa_ref[...], 0.0)
            sg = _sig(gate)
            dg_scr[r:r + rows, :] = dav * up * (sg * (1.0 + gate * (1.0 - sg)))
            du_scr[r:r + rows, :] = dav * gate * sg

        def back(d_scr, x_ref, w, df_ref, dw_ref, db_ref):
            accs = [jnp.zeros((sub, UPW), F32) for _ in range(FFN_K + 1)]
            for s in range(T // (2 * sub)):
                pieces = []
                for r in (2 * s * sub, (2 * s + 1) * sub):
                    xv = x_ref[r:r + sub, :]
                    df = jnp.zeros((sub, UPW), F32)
                    for k in range(FFN_K):
                        off = FFN_K - 1 - k
                        dj = d_scr[r + off:r + off + sub, :]
                        df = df + w[k] * dj
                        accs[k] = accs[k] + xv * dj
                        if off == 0:
                            accs[FFN_K] = accs[FFN_K] + dj
                    pieces.append(df)
                df_ref[2 * s * sub:2 * (s + 1) * sub, :] = jnp.concatenate(pieces, axis=0).astype(BF16)
            dwp = _stack_rows([jnp.sum(a, axis=0, keepdims=True) for a in accs[:FFN_K]])
            dbp = jnp.sum(accs[FFN_K], axis=0, keepdims=True)

            @pl.when(c == 0)
            def _():
                dw_ref[...] = dwp
                db_ref[...] = dbp

            @pl.when(c > 0)
            def _():
                dw_ref[...] += dwp
                db_ref[...] += dbp

        back(dg_scr, fg_ref, wg, dfg_ref, dwg_ref, dbg_ref)
        back(du_scr, fu_ref, wu, dfu_ref, dwu_ref, dbu_ref)

    cur, _, after, wspec = _ffn_specs()
    slab = jax.ShapeDtypeStruct((NPAIR, L, UPW), BF16)
    wsh = jax.ShapeDtypeStruct((NPAIR, FFN_K, UPW), F32)
    bsh = jax.ShapeDtypeStruct((NPAIR, 1, UPW), F32)
    return pl.pallas_call(
        body, name=name, out_shape=(slab, slab, wsh, wsh, bsh, bsh), grid=(NPAIR // PP, NB),
        in_specs=[cur(0), cur(NPAIR), cur(0), _ffn_after16(), cur(0), _ffn_after16(), cur(0), after(0),
                  wspec(0, FFN_K), wspec(NPAIR, FFN_K)],
        out_specs=(cur(0), cur(0), wspec(0, FFN_K), wspec(0, FFN_K), wspec(0, 1), wspec(0, 1)),
        scratch_shapes=[pltpu.VMEM((ext, UPW), F32), pltpu.VMEM((ext, UPW), F32)])(
            f, f, gc, gc, uc, uc, da, da, w, w)


def _mesh_pos():
    return lax.axis_index("x"), lax.axis_index("y"), lax.axis_index("c")


def _peer(pos, k):
    x, y, c = pos
    px = 1 - x if k & 4 else x
    py = 1 - y if k & 2 else y
    pc = 1 - c if k & 1 else c
    return (px, py, pc), 4 * px + 2 * py + pc


_CHIP_FLIPS = (4, 2, 6)
_HBM = pl.BlockSpec(memory_space=pl.ANY)


def all_gather(shards, name):
    nt = len(shards)

    def body(*refs):
        ins, outs = refs[:nt], refs[nt:2 * nt]
        send, recv, local = refs[2 * nt:]
        pos = _mesh_pos()
        me = 4 * pos[0] + 2 * pos[1] + pos[2]
        sib, sib_id = _peer(pos, 1)

        def copy(t, k, block_id, to, src=None):
            dst = outs[t].at[block_id]
            return pltpu.make_async_remote_copy(
                src_ref=dst if src is None else src, dst_ref=dst, send_sem=send.at[t, k], recv_sem=recv.at[t, k],
                device_id=to, device_id_type=MESH)

        locals_ = [pltpu.make_async_copy(ins[t], outs[t].at[me], local.at[t]) for t in range(nt)]
        for cp in locals_:
            cp.start()
        started = []
        for j, flip in enumerate(_CHIP_FLIPS):
            for t in range(nt):
                started.append(copy(t, 1 + j, me, _peer(pos, flip)[0], src=ins[t]))
        for t in range(nt):
            started.append(copy(t, 0, me, sib, src=ins[t]))
        for cp in started:
            cp.start()
        for j, flip in enumerate(_CHIP_FLIPS):
            _, pid = _peer(pos, flip)
            for t in range(nt):
                copy(t, 1 + j, pid, sib).wait_recv()
                fwd = copy(t, 4 + j, pid, sib)
                fwd.start()
                started.append(fwd)
        for t in range(nt):
            copy(t, 0, sib_id, sib).wait_recv()
        for j, flip in enumerate(_CHIP_FLIPS):
            _, pid = _peer(pos, flip | 1)
            for t in range(nt):
                copy(t, 4 + j, pid, sib).wait_recv()
        for cp in started:
            cp.wait_send()
        for cp in locals_:
            cp.wait()

    return pl.pallas_call(
        body, name=name,
        out_shape=tuple(jax.ShapeDtypeStruct((NDEV,) + s.shape, s.dtype) for s in shards),
        in_specs=[_HBM] * nt, out_specs=tuple([_HBM] * nt),
        scratch_shapes=[pltpu.SemaphoreType.DMA((nt, NDEV - 1)), pltpu.SemaphoreType.DMA((nt, NDEV - 1)),
                        pltpu.SemaphoreType.DMA((nt,))])(*shards)


_SPLIT = dict(has_side_effects=pltpu.SideEffectType.DATAFLOW_SIDE_EFFECTING)
_SEM = pl.BlockSpec(memory_space=pltpu.SEMAPHORE)


def _exchange_start(name, srcs, lands, copies, after):
    ns, nl = len(srcs), len(lands)
    ncopy = len(copies(None, [None] * ns, [None] * nl))

    def body(*refs):
        src_refs, land_refs = refs[:ns], refs[ns:ns + nl]
        send, recv = refs[ns + nl + len(after):ns + nl + len(after) + 2]
        token = refs[-1]
        for i, (src, dst, peer) in enumerate(copies(_mesh_pos(), src_refs, land_refs)):
            pltpu.make_async_remote_copy(src_ref=src, dst_ref=dst, send_sem=send.at[i], recv_sem=recv.at[i],
                                         device_id=peer, device_id_type=MESH).start()
        token[...] = jnp.zeros_like(token)

    hbm = lambda a: pltpu.HBM(a.shape, a.dtype)
    out = pl.pallas_call(
        body, name=name,
        out_shape=(pltpu.SemaphoreType.DMA((ncopy,)), pltpu.SemaphoreType.DMA((ncopy,)),
                   *[hbm(a) for a in srcs], *[hbm(a) for a in lands], jax.ShapeDtypeStruct((8, 128), F32)),
        in_specs=[pl.BlockSpec(memory_space=pltpu.HBM)] * (ns + nl) + [_HBM] * len(after),
        out_specs=(_SEM, _SEM, *[pl.BlockSpec(memory_space=pltpu.HBM)] * (ns + nl),
                   pl.BlockSpec(memory_space=pltpu.VMEM)),
        input_output_aliases={i: 2 + i for i in range(ns + nl)},
        compiler_params=pltpu.CompilerParams(**_SPLIT))(
            *[pltpu.with_memory_space_constraint(a, pltpu.HBM) for a in list(srcs) + list(lands)], *after)
    return out[0], out[1], list(out[2:2 + ns]), list(out[2 + ns:2 + ns + nl]), out[-1]


def _exchange_wait(name, send, recv, srcs, lands, copies, after):
    ns, nl = len(srcs), len(lands)

    def body(*refs):
        src_refs, land_refs = refs[:ns], refs[ns:ns + nl]
        send_ref, recv_ref = refs[ns + nl:ns + nl + 2]
        pos = _mesh_pos()
        for i, (src, dst, peer) in enumerate(copies(pos, src_refs, land_refs, arriving=True)):
            cp = pltpu.make_async_remote_copy(src_ref=src, dst_ref=dst, send_sem=send_ref.at[i], recv_sem=recv_ref.at[i],
                                              device_id=peer, device_id_type=MESH)
            cp.wait_send()
            cp.wait_recv()

    hbm = lambda a: pltpu.HBM(a.shape, a.dtype)
    out = pl.pallas_call(
        body, name=name, out_shape=tuple(hbm(a) for a in list(srcs) + list(lands)),
        in_specs=[pl.BlockSpec(memory_space=pltpu.HBM)] * (ns + nl) + [_SEM, _SEM] + [_HBM] * len(after),
        out_specs=tuple([pl.BlockSpec(memory_space=pltpu.HBM)] * (ns + nl)),
        input_output_aliases={i: i for i in range(ns + nl)},
        compiler_params=pltpu.CompilerParams(**_SPLIT))(*srcs, *lands, send, recv, *after)
    return list(out[:ns]), list(out[ns:])


def _gather_copies(pos, srcs, lands, arriving=False):
    if pos is None:
        return [None] * (len(srcs) * (NDEV - 1))
    me = 4 * pos[0] + 2 * pos[1] + pos[2]
    out = []
    for src, land in zip(srcs, lands):
        for k in range(1, NDEV):
            peer, pid = _peer(pos, k)
            out.append((src, land.at[pid if arriving else me], peer))
    return out


def _scatter_copies(pos, srcs, lands, arriving=False):
    if pos is None:
        return [None] * (len(srcs) * (NDEV - 1))
    out = []
    for src, land in zip(srcs, lands):
        for k in range(1, NDEV):
            peer, pid = _peer(pos, k)
            out.append((src.at[pid], land.at[k - 1], peer))
    return out


def gather_start(name, shards, lands, after=()):
    return _exchange_start(name, shards, lands, _gather_copies, list(after))


def gather_wait(name, handle, after=()):
    send, recv, srcs, lands, _ = handle
    return _exchange_wait(name, send, recv, srcs, lands, _gather_copies, list(after))


def scatter_start(name, grads, after=()):
    lands = [pltpu.with_memory_space_constraint(lax.empty((NDEV - 1,) + g.shape[1:], g.dtype), pltpu.HBM)
             for g in grads]
    return _exchange_start(name, grads, lands, _scatter_copies, list(after))


def scatter_wait(name, handle, after=()):
    send, recv, srcs, lands, _ = handle
    return _exchange_wait(name, send, recv, srcs, lands, _scatter_copies, list(after))


def sum_slots(parts, name):
    def body(p_ref, o_ref):
        acc = p_ref[0]
        for d in range(1, NDEV):
            acc = acc + p_ref[d]
        o_ref[...] = acc

    return pl.pallas_call(body, name=name, out_shape=jax.ShapeDtypeStruct(parts.shape[1:], F32))(parts)


def _adamw_update(g, w_ref, m_ref, v_ref, g_ref, d_ref, nm_ref, nv_ref):
    g_ref[...] = g
    nm = ADAM_B1 * m_ref[...] + (1.0 - ADAM_B1) * g
    nv = ADAM_B2 * v_ref[...] + (1.0 - ADAM_B2) * (g * g)
    nm_ref[...] = nm
    nv_ref[...] = nv
    m_hat = nm / (1.0 - ADAM_B1 ** ADAM_STEP)
    v_hat = nv / (1.0 - ADAM_B2 ** ADAM_STEP)
    d_ref[...] = -ADAM_LR * (m_hat / (jnp.sqrt(v_hat) + ADAM_EPS) + ADAM_WD * w_ref[...])


def adamw_big(own, lands, w, m, v, me, tr, name):
    _, r, c = own[0].shape
    nt = r // tr

    def body(me_ref, *refs):
        ins, (w_ref, m_ref, v_ref), outs = refs[:2 * DEPTH], refs[2 * DEPTH:2 * DEPTH + 3], refs[2 * DEPTH + 3:]
        layer = pl.program_id(0)
        for l in range(DEPTH):
            @pl.when(layer == l)
            def _(l=l):
                g = ins[2 * l][...].astype(F32)
                for s in range(NDEV - 1):
                    g = g + ins[2 * l + 1][s].astype(F32)
                _adamw_update(g, w_ref, m_ref, v_ref, *outs)

    pick = lambda l: (lambda layer, i, me_ref: jnp.where(layer == l, i, 0))
    in_specs = []
    for l in range(DEPTH):
        in_specs.append(pl.BlockSpec((None, tr, c), lambda layer, i, me_ref, f=pick(l): (me_ref[0], f(layer, i, me_ref), 0)))
        in_specs.append(pl.BlockSpec((NDEV - 1, tr, c), lambda layer, i, me_ref, f=pick(l): (0, f(layer, i, me_ref), 0)))
    blk = pl.BlockSpec((tr, c), lambda layer, i, me_ref: (layer * nt + i, 0))
    sh = jax.ShapeDtypeStruct((DEPTH * r, c), F32)
    args = [a for l in range(DEPTH) for a in (own[l], lands[l])]
    return pl.pallas_call(
        body, name=name, out_shape=(sh, sh, sh, sh),
        grid_spec=pltpu.PrefetchScalarGridSpec(
            num_scalar_prefetch=1, grid=(DEPTH, nt), in_specs=in_specs + [blk, blk, blk],
            out_specs=(blk, blk, blk, blk)),
        compiler_params=_params(VMEM_BIG))(me, *args, w, m, v)


def adamw_many(gs, ws, ms, vs, name):
    n = len(gs)

    def body(*refs):
        outs = refs[4 * n:]
        for i in range(n):
            _adamw_update(refs[i][...], refs[n + i], refs[2 * n + i], refs[3 * n + i], *outs[4 * i:4 * i + 4])

    out_shape = tuple(jax.ShapeDtypeStruct(w.shape, F32) for w in ws for _ in range(4))
    res = pl.pallas_call(body, name=name, out_shape=out_shape)(*gs, *ws, *ms, *vs)
    return [res[4 * i:4 * i + 4] for i in range(n)]


def local_step(x, target, P, hooks):
    tb = _ret_tables()
    h = jnp.concatenate([jnp.zeros((PAD, D), F32), P["meta"], x], axis=0)
    stash = []
    for l in range(DEPTH):
        s = {"h": h}
        s["w_in"], s["cv_pw"], s["w_out"] = hooks["mixer_w"](l, h)
        proj, s["u"] = in_proj_fwd(f"in_proj_{l}", h, P["norm_mix_g"][l], s["w_in"], 256,
                                   after=hooks["first_after"] if l == 0 else None)
        s["proj"] = proj
        s["o"] = attn_fwd(proj, P["q_norm_g"][l], P["k_norm_g"][l], P["attn_sinks"][l], f"attn_fwd_{l}")
        y_cv, s["xc"] = conv_fwd(proj, P["cv_dw_w"][l], P["cv_dw_b"][l], P["cv_ln_g"][l], P["cv_ln_b"][l],
                                 s["cv_pw"], P["cv_out_g"][l], f"conv_fwd_{l}")
        y_ret, s["states"] = ret_fwd(proj, P["ret_gn_g"][l], tb, f"ret_fwd_{l}")
        h, s["ycat"] = out_proj_fwd(f"out_proj_{l}", s["o"], P["attn_out_g"][l], y_cv, y_ret, s["w_out"], h, 512)
        s["h1"] = h
        s["ffn_up"], s["ffn_down"] = hooks["ffn_w"](l, h)
        s["f"], s["u2"] = ffn_up_fwd(f"ffn_up_{l}", h, P["norm_ffn_g"][l], s["ffn_up"])
        a, s["gc"], s["uc"] = ffn_act_fwd(s["f"], P["ffn_dw_w"][l], P["ffn_dw_b"][l], f"ffn_act_fwd_{l}")
        s["a"] = a
        if l < DEPTH - 1:
            h = ffn_down_fwd(f"ffn_down_{l}", a, s["ffn_down"], h)
        else:
            dh, dhm, loss = ffn_down_loss(f"ffn_down_{l}", a, s["ffn_down"], h, target)
        stash.append(s)

    G = {n: [None] * DEPTH for n in ("norm_mix_g", "q_norm_g", "k_norm_g", "attn_sinks", "attn_out_g", "cv_dw_w",
                                     "cv_dw_b", "cv_ln_g", "cv_ln_b", "cv_out_g", "ret_gn_g", "norm_ffn_g",
                                     "ffn_dw_w", "ffn_dw_b")}
    for l in reversed(range(DEPTH)):
        s = stash[l]
        da = ffn_down_dx(f"ffn_down_dx_{l}", dhm, s["ffn_down"])
        dw_down = ffn_down_dw(f"ffn_down_dw_{l}", s["a"], dhm)
        dfg, dfu, dwg, dwu, dbg, dbu = ffn_act_bwd(s["f"], s["gc"], s["uc"], da, P["ffn_dw_w"][l], f"ffn_act_bwd_{l}")
        G["ffn_dw_w"][l] = jnp.concatenate([dwg, dwu], axis=0)
        G["ffn_dw_b"][l] = jnp.concatenate([dbg, dbu], axis=0)
        du2 = ffn_up_dx(f"ffn_up_dx_{l}", dfg, dfu, s["ffn_up"])
        dw_up = ffn_up_dw(f"ffn_up_dw_{l}", s["u2"], dfg, dfu)
        after = hooks["ffn_grads"](l, dw_down, dw_up)
        dh, dhm, G["norm_ffn_g"][l] = rms_bwd(s["h1"], P["norm_ffn_g"][l], du2, dh, f"rms_ffn_bwd_{l}", after=after)
        dycat = mm_nt(f"out_proj_dx_{l}", dhm, s["w_out"], 512)
        dw_out = mm_ta(f"out_proj_dw_{l}", s["ycat"], dhm, 512)
        (dq, dk_c, dk_p, dk_m, dv_c, dv_p, dv_m, G["q_norm_g"][l], G["k_norm_g"][l], dsk,
         G["attn_out_g"][l]) = attn_bwd(s["proj"], s["o"], dycat, P["attn_out_g"][l], P["q_norm_g"][l],
                                        P["k_norm_g"][l], P["attn_sinks"][l], f"attn_bwd_{l}")
        G["attn_sinks"][l] = dsk[:, 0]
        shift = lambda z: jnp.concatenate([z[T:], jnp.zeros((T, NKV * HD), F32)], axis=0)
        dk = (dk_c + shift(dk_p)).at[:T].add(dk_m)
        dv = (dv_c + shift(dv_p)).at[:T].add(dv_m)
        (dca, dcb, G["cv_dw_w"][l], G["cv_dw_b"][l], G["cv_ln_g"][l], G["cv_ln_b"][l], G["cv_out_g"][l],
         dpw) = conv_bwd(s["proj"], s["xc"], dycat, P["cv_dw_w"][l], P["cv_ln_g"][l], P["cv_ln_b"][l], s["cv_pw"],
                         P["cv_out_g"][l], f"conv_bwd_{l}")
        dret, G["ret_gn_g"][l] = ret_bwd(s["proj"], dycat, s["states"], P["ret_gn_g"][l], tb, f"ret_bwd_{l}")
        dproj = jnp.concatenate([dq, dk.astype(BF16), dv.astype(BF16), dca, dcb, dret], axis=1)
        du = mm_nn(f"in_proj_dx_{l}", dproj, s["w_in"], 512)
        dw_in = mm_ta(f"in_proj_dw_{l}", dproj, s["u"], 768)
        after = hooks["mixer_grads"](l, dw_out, dpw, dw_in)
        dh, dhm, G["norm_mix_g"][l] = rms_bwd(s["h"], P["norm_mix_g"][l], du, dh, f"rms_mix_bwd_{l}", after=after,
                                              last=(l == 0))
    return loss[0, 0], dh, dhm[PAD:T], G


_SMALL = ("meta", "norm_mix_g", "q_norm_g", "k_norm_g", "attn_sinks", "attn_out_g", "cv_dw_w", "cv_dw_b", "cv_ln_g",
          "cv_ln_b", "cv_out_g", "ret_gn_g", "norm_ffn_g", "ffn_dw_w", "ffn_dw_b")
_BIG = ("w_in", "cv_pw", "w_out", "ffn_up", "ffn_down")
_TRANSPOSED = ("w_in", "ffn_up")
_ORDER = ("meta", "norm_mix_g", "w_in", "q_norm_g", "k_norm_g", "attn_sinks", "attn_out_g", "cv_dw_w", "cv_dw_b",
          "cv_ln_g", "cv_ln_b", "cv_pw", "cv_out_g", "ret_gn_g", "w_out", "norm_ffn_g", "ffn_up", "ffn_dw_w",
          "ffn_dw_b", "ffn_down")
_SMALL_SHARDED = {"meta": D, "cv_dw_w": CV, "ffn_dw_w": 2 * D_FF}


def _pack(arrs):
    flat = jnp.concatenate([a.reshape(-1) for a in arrs])
    n = flat.shape[0]
    rows = -(-n // 1024) * 8
    return jnp.pad(flat, (0, rows * 128 - n)).reshape(rows, 128)


def _unpack(packed, shapes):
    flat = packed.reshape(-1)
    out, off = [], 0
    for s in shapes:
        n = int(np.prod(s))
        out.append(flat[off:off + n].reshape(s))
        off += n
    return out


def kernel(x, meta, norm_mix_g, w_in, q_norm_g, k_norm_g, attn_sinks, attn_out_g, cv_dw_w, cv_dw_b, cv_ln_g, cv_ln_b, cv_pw, cv_out_g, ret_gn_g, w_out, norm_ffn_g, ffn_up, ffn_dw_w, ffn_dw_b, ffn_down, loss_target, m_meta, m_norm_mix_g, m_w_in, m_q_norm_g, m_k_norm_g, m_attn_sinks, m_attn_out_g, m_cv_dw_w, m_cv_dw_b, m_cv_ln_g, m_cv_ln_b, m_cv_pw, m_cv_out_g, m_ret_gn_g, m_w_out, m_norm_ffn_g, m_ffn_up, m_ffn_dw_w, m_ffn_dw_b, m_ffn_down, v_meta, v_norm_mix_g, v_w_in, v_q_norm_g, v_k_norm_g, v_attn_sinks, v_attn_out_g, v_cv_dw_w, v_cv_dw_b, v_cv_ln_g, v_cv_ln_b, v_cv_pw, v_cv_out_g, v_ret_gn_g, v_w_out, v_norm_ffn_g, v_ffn_up, v_ffn_dw_w, v_ffn_dw_b, v_ffn_down):
    W = dict(meta=meta, norm_mix_g=norm_mix_g, w_in=w_in, q_norm_g=q_norm_g, k_norm_g=k_norm_g,
             attn_sinks=attn_sinks, attn_out_g=attn_out_g, cv_dw_w=cv_dw_w, cv_dw_b=cv_dw_b, cv_ln_g=cv_ln_g,
             cv_ln_b=cv_ln_b, cv_pw=cv_pw, cv_out_g=cv_out_g, ret_gn_g=ret_gn_g, w_out=w_out,
             norm_ffn_g=norm_ffn_g, ffn_up=ffn_up, ffn_dw_w=ffn_dw_w, ffn_dw_b=ffn_dw_b, ffn_down=ffn_down)
    M = dict(meta=m_meta, norm_mix_g=m_norm_mix_g, w_in=m_w_in, q_norm_g=m_q_norm_g, k_norm_g=m_k_norm_g,
             attn_sinks=m_attn_sinks, attn_out_g=m_attn_out_g, cv_dw_w=m_cv_dw_w, cv_dw_b=m_cv_dw_b,
             cv_ln_g=m_cv_ln_g, cv_ln_b=m_cv_ln_b, cv_pw=m_cv_pw, cv_out_g=m_cv_out_g, ret_gn_g=m_ret_gn_g,
             w_out=m_w_out, norm_ffn_g=m_norm_ffn_g, ffn_up=m_ffn_up, ffn_dw_w=m_ffn_dw_w, ffn_dw_b=m_ffn_dw_b,
             ffn_down=m_ffn_down)
    V = dict(meta=v_meta, norm_mix_g=v_norm_mix_g, w_in=v_w_in, q_norm_g=v_q_norm_g, k_norm_g=v_k_norm_g,
             attn_sinks=v_attn_sinks, attn_out_g=v_attn_out_g, cv_dw_w=v_cv_dw_w, cv_dw_b=v_cv_dw_b,
             cv_ln_g=v_cv_ln_g, cv_ln_b=v_cv_ln_b, cv_pw=v_cv_pw, cv_out_g=v_cv_out_g, ret_gn_g=v_ret_gn_g,
             w_out=v_w_out, norm_ffn_g=v_norm_ffn_g, ffn_up=v_ffn_up, ffn_dw_w=v_ffn_dw_w, ffn_dw_b=v_ffn_dw_b,
             ffn_down=v_ffn_down)
    me = 4 * lax.axis_index("x") + 2 * lax.axis_index("y") + lax.axis_index("c")
    for n in _TRANSPOSED:
        W[n], M[n], V[n] = (a.transpose(0, 2, 1) for a in (W[n], M[n], V[n]))

    sh = {n: [W[n][l].astype(BF16) for l in range(DEPTH)] for n in _BIG}
    mix = lambda l: [sh["w_in"][l], sh["cv_pw"][l], sh["w_out"][l]]
    ffn = lambda l: [sh["ffn_up"][l], sh["ffn_down"][l]]
    first = all_gather(mix(0) + [meta, cv_dw_w, ffn_dw_w], "gather_first")
    g_meta, g_cdw, g_fdw = first[3:6]

    def landing(shards):
        return [lax.dynamic_update_slice(lax.empty((NDEV,) + s.shape, s.dtype), s[None], (me,) + (0,) * s.ndim)
                for s in shards]

    gathers = {("ffn", 0): gather_start("gather_ffn0_start", ffn(0), landing(ffn(0)), after=[first[0]])}
    gathers["mix", 1] = gather_start("gather_mix1_start", mix(1), landing(mix(1)), after=[gathers["ffn", 0][4]])
    gathers["ffn", 1] = gather_start("gather_ffn1_start", ffn(1), landing(ffn(1)), after=[gathers["mix", 1][4]])

    def mixer_w(l, h):
        g_in, g_pw, g_out = first[0:3] if l == 0 else gather_wait(f"gather_mix{l}_wait", gathers["mix", l], after=[h])[1]
        return g_in.reshape(IN_W, D), g_pw.reshape(CV, CV), g_out.reshape(D, D)

    def ffn_w(l, h1):
        return gather_wait(f"gather_ffn{l}_wait", gathers["ffn", l], after=[h1])[1]

    scatters = {}

    def ffn_grads(l, dw_down, dw_up):
        scatters["ffn", l] = scatter_start(f"scatter_ffn{l}_start", [dw_up, dw_down.reshape(NDEV, DNR, D)])
        return scatters["ffn", l][4]

    def mixer_grads(l, dw_out, dpw, dw_in):
        grads = [dw_in.reshape(NDEV, IN_W // NDEV, D), dpw.astype(BF16).reshape(NDEV, CV // NDEV, CV),
                 dw_out.reshape(NDEV, D // NDEV, D)]
        scatters["mix", l] = scatter_start(f"scatter_mix{l}_start", grads)
        return scatters["mix", l][4]

    P = dict(
        meta=g_meta.transpose(1, 0, 2).reshape(N_META, D),
        cv_dw_w=g_cdw.transpose(1, 2, 0, 3).reshape(DEPTH, CONV_K, CV),
        ffn_dw_w=g_fdw.transpose(1, 0, 2, 3),
        ffn_dw_b=ffn_dw_b.reshape(DEPTH, NDEV, 1, UPW),
        attn_sinks=attn_sinks,
    )
    for n in ("norm_mix_g", "q_norm_g", "k_norm_g", "attn_out_g", "cv_dw_b", "cv_ln_g", "cv_ln_b", "cv_out_g",
              "ret_gn_g", "norm_ffn_g"):
        P[n] = W[n].reshape(DEPTH, 1, -1)

    hooks = dict(mixer_w=mixer_w, ffn_w=ffn_w, ffn_grads=ffn_grads, mixer_grads=mixer_grads,
                 first_after=gathers["ffn", 1][4])
    loss_part, dx, dmeta, G = local_step(x[0], loss_target[0], P, hooks)

    small_full = {
        "meta": dmeta,
        "cv_dw_w": jnp.stack(G["cv_dw_w"]),
        "ffn_dw_w": jnp.stack([g.transpose(1, 0, 2).reshape(FFN_K, 2 * D_FF) for g in G["ffn_dw_w"]]),
        "ffn_dw_b": jnp.stack([g.reshape(2 * D_FF) for g in G["ffn_dw_b"]]),
        "attn_sinks": jnp.stack(G["attn_sinks"]),
    }
    for n in _SMALL:
        if n not in small_full:
            small_full[n] = jnp.stack([g.reshape(-1) for g in G[n]])
    shapes = [small_full[n].shape for n in _SMALL] + [(1,)]
    packed = _pack([small_full[n] for n in _SMALL] + [loss_part.reshape(1)])
    small_exchange = gather_start("reduce_small_start", [packed], landing([packed]), after=[dx])

    out = {}
    tiles = {"w_in": 144, "cv_pw": 32, "w_out": 128, "ffn_up": 176, "ffn_down": 176}
    me1 = me.astype(jnp.int32).reshape(1)
    own, lands = {n: [None] * DEPTH for n in _BIG}, {n: [None] * DEPTH for n in _BIG}

    def arrived(kind, names, after):
        for l in range(DEPTH):
            srcs, got = scatter_wait(f"scatter_{kind}{l}_wait", scatters[kind, l], after=after)
            for n, s_, g_ in zip(names, srcs, got):
                own[n][l], lands[n][l] = s_, g_

    def update(names):
        for n in names:
            shard = W[n].shape
            rows, cols = shard[0] * shard[1], shard[2]
            res = adamw_big(own[n], lands[n], W[n].reshape(rows, cols), M[n].reshape(rows, cols),
                            V[n].reshape(rows, cols), me1, tiles[n], f"adamw_{n}")
            out[n] = [r.reshape(shard) for r in res]

    arrived("ffn", ("ffn_up", "ffn_down"), [small_exchange[4]])
    update(("ffn_up", "ffn_down"))
    slots = gather_wait("reduce_small_wait", small_exchange, after=[out["ffn_down"][0]])[1][0]
    summed = _unpack(sum_slots(slots, "reduce_small_sum"), shapes)
    loss = summed[-1][0]
    small_g = []
    for n, g in zip(_SMALL, summed):
        if n in _SMALL_SHARDED:
            width = _SMALL_SHARDED[n] // NDEV
            g = lax.dynamic_slice_in_dim(g, me * width, width, axis=g.ndim - 1)
        small_g.append(g)
    flat2 = lambda a: a.reshape(-1, a.shape[-1])
    res = adamw_many([flat2(g) for g in small_g], *[[flat2(X[n]) for n in _SMALL] for X in (W, M, V)], "adamw_small")
    for n, r in zip(_SMALL, res):
        out[n] = [a.reshape(W[n].shape) for a in r]
    arrived("mix", ("w_in", "cv_pw", "w_out"), [out["ffn_down"][0], res[0][0]])
    update(("w_in", "cv_pw", "w_out"))
    for n in _TRANSPOSED:
        out[n] = [r.transpose(0, 2, 1) for r in out[n]]

    return (loss, dx[None], *[out[n][0] for n in _ORDER], *[out[n][1] for n in _ORDER],
            *[out[n][2] for n in _ORDER], *[out[n][3] for n in _ORDER])
```

```python
import numpy as np
import jax
import jax.numpy as jnp
from jax import lax
from jax.experimental import pallas as pl
from jax.experimental.pallas import tpu as pltpu

F32 = jnp.float32
BF16 = jnp.bfloat16

D = 1024
SEQ = 2048
DEPTH = 2
T = 128
L = SEQ + T
NB = L // T
N_META = 16
PAD = T - N_META
HD = 64
NQ = 8
NKV = 2
GQA = NQ // NKV
CV = 256
CONV_K = 31
RH = 4
D_FF = 2816
FFN_K = 3
IN_W = 2304
RMS_EPS = 1e-6
LN_EPS = 1e-5
NEG = -1e30
NDEV = 8
UPW = 2 * D_FF // NDEV
DNR = D_FF // NDEV
NPAIR = NDEV // 2

ADAM_LR, ADAM_B1, ADAM_B2, ADAM_EPS, ADAM_WD, ADAM_STEP = 0.001, 0.9, 0.999, 1e-08, 0.01, 10

VMEM_BIG = 56 * 1024 * 1024

MESH = pl.DeviceIdType.MESH


def _params(vmem=None):
    return pltpu.CompilerParams(vmem_limit_bytes=vmem) if vmem else None


def _dot(a, b, nt=False):
    return lax.dot_general(a, b, (((1,), (1 if nt else 0,)), ((), ())), preferred_element_type=F32)


def _sig(x):
    return 1.0 / (1.0 + jnp.exp(-x))


def _bf(x):
    return x.astype(BF16)


def _stack_rows(rows):
    idx = lax.broadcasted_iota(jnp.int32, (len(rows), rows[0].shape[1]), 0)
    out = jnp.zeros((len(rows), rows[0].shape[1]), F32)
    for r, v in enumerate(rows):
        out = jnp.where(idx == r, v, out)
    return out


def rms_bwd(x, g, dy, dres, name, after=None, last=False):
    n, w = x.shape
    has_res = dres is not None
    deps = [] if after is None else [after]

    def body(x_ref, g_ref, dy_ref, *rest):
        rest = rest[:len(rest) - 3 - len(deps)] + rest[len(rest) - 3:]
        if has_res:
            dres_ref, dx_ref, dxm_ref, dg_ref = rest
        else:
            dx_ref, dxm_ref, dg_ref = rest
        i = pl.program_id(0)
        xv = x_ref[...]
        r = lax.rsqrt(jnp.mean(xv * xv, axis=-1, keepdims=True) + RMS_EPS)
        xh = xv * r
        dyv = dy_ref[...]
        dxh = dyv * g_ref[...]
        dx = r * (dxh - xh * jnp.mean(dxh * xh, axis=-1, keepdims=True))
        if has_res:
            dx = dx + dres_ref[...]
        dx_ref[...] = dx
        if last:
            @pl.when(i == 0)
            def _():
                dxm_ref[...] = dx
        else:
            rows = i * T + lax.broadcasted_iota(jnp.int32, (T, 1), 0)
            dxm_ref[...] = jnp.where(rows >= PAD, dx, 0.0).astype(BF16)
        part = jnp.sum(dyv * xh, axis=0, keepdims=True)

        @pl.when(i == 0)
        def _():
            dg_ref[...] = part

        @pl.when(i > 0)
        def _():
            dg_ref[...] += part

    row = pl.BlockSpec((T, w), lambda i: (i, 0))
    vec = pl.BlockSpec((1, w), lambda i: (0, 0))
    ins = [x, g, dy] + ([dres] if has_res else []) + deps
    if last:
        out_shape = (jax.ShapeDtypeStruct((n - T, w), F32), jax.ShapeDtypeStruct((T, w), F32))
        out_specs = (pl.BlockSpec((T, w), lambda i: (jnp.maximum(i - 1, 0), 0)), pl.BlockSpec((T, w), lambda i: (0, 0)))
    else:
        out_shape = (jax.ShapeDtypeStruct((n, w), F32), jax.ShapeDtypeStruct((n, w), BF16))
        out_specs = (row, row)
    return pl.pallas_call(
        body, name=name, out_shape=(*out_shape, jax.ShapeDtypeStruct((1, w), F32)), grid=(n // T,),
        in_specs=[row, vec, row] + ([row] if has_res else [])
        + [_HBM] * len(deps),
        out_specs=(*out_specs, vec))(*ins)


def _mm(name, a, b, *, grid, a_spec, b_spec, o_spec, out_shape, nt=False, ta=False, red=False, res=None,
        res_spec=None):
    def body(a_ref, b_ref, *rest):
        o_ref = rest[-1]
        av = a_ref[...]
        bv = b_ref[...]
        if bv.ndim == 3:
            bv = bv.reshape(bv.shape[0] * bv.shape[1], bv.shape[2])
        if ta:
            acc = lax.dot_general(av, bv, (((0,), (0,)), ((), ())), preferred_element_type=F32)
        else:
            acc = _dot(av, bv, nt)
        if red:
            k = pl.program_id(0)

            @pl.when(k == 0)
            def _():
                o_ref[...] = acc

            @pl.when(k > 0)
            def _():
                o_ref[...] += acc
        else:
            if res is not None:
                rows = lax.broadcasted_iota(jnp.int32, (acc.shape[0], 1), 0)
                acc = rest[0][...] + jnp.where(rows >= PAD, acc, 0.0)
            o_ref[...] = acc.astype(o_ref.dtype)

    ins = [a, b] + ([res] if res is not None else [])
    specs = [a_spec, b_spec] + ([res_spec] if res is not None else [])
    return pl.pallas_call(body, name=name, out_shape=out_shape, grid=grid, in_specs=specs, out_specs=o_spec,
                          compiler_params=_params(VMEM_BIG))(*ins)


def mm_nn(name, a, b, tn, out_dtype=F32, res=None):
    m, k = a.shape
    n = b.shape[1]
    return _mm(name, a, b, grid=(n // tn,),
               a_spec=pl.BlockSpec((m, k), lambda j: (0, 0)), b_spec=pl.BlockSpec((k, tn), lambda j: (0, j)),
               o_spec=pl.BlockSpec((m, tn), lambda j: (0, j)), out_shape=jax.ShapeDtypeStruct((m, n), out_dtype),
               res=res, res_spec=pl.BlockSpec((m, tn), lambda j: (0, j)))


def mm_nt(name, a, b, tn):
    m, k = a.shape
    n = b.shape[0]
    return _mm(name, a, b, grid=(n // tn,), nt=True,
               a_spec=pl.BlockSpec((m, k), lambda j: (0, 0)), b_spec=pl.BlockSpec((tn, k), lambda j: (j, 0)),
               o_spec=pl.BlockSpec((m, tn), lambda j: (0, j)), out_shape=jax.ShapeDtypeStruct((m, n), F32))


def mm_ta(name, a, b, tm, out_dtype=BF16):
    k, m = a.shape
    n = b.shape[1]
    return _mm(name, a, b, grid=(m // tm,), ta=True,
               a_spec=pl.BlockSpec((k, tm), lambda j: (0, j)), b_spec=pl.BlockSpec((k, n), lambda j: (0, 0)),
               o_spec=pl.BlockSpec((tm, n), lambda j: (j, 0)), out_shape=jax.ShapeDtypeStruct((m, n), out_dtype))


def _rms_mm(name, h, g, b, *, grid, b_spec, o_spec, out_shape, after=None):
    deps = [] if after is None else [after]

    def body(h_ref, g_ref, b_ref, *rest):
        o_ref, u_ref = rest[len(deps):]

        @pl.when(pl.program_id(0) == 0)
        def _():
            gv = g_ref[...]
            for c in range(NB):
                rows = slice(c * T, (c + 1) * T)
                xv = h_ref[rows, :]
                r = lax.rsqrt(jnp.mean(xv * xv, axis=-1, keepdims=True) + RMS_EPS)
                u_ref[rows, :] = (xv * r * gv).astype(BF16)

        o_ref[...] = _dot(u_ref[...], b_ref[...], nt=True)

    whole = lambda r: pl.BlockSpec((r, D), lambda j: (0, 0))
    return pl.pallas_call(
        body, name=name, out_shape=(out_shape, jax.ShapeDtypeStruct((L, D), BF16)), grid=grid,
        in_specs=[whole(L), whole(1), b_spec] + [_HBM] * len(deps), out_specs=(o_spec, whole(L)),
        compiler_params=_params(VMEM_BIG))(h, g, b, *deps)


def out_proj_fwd(name, o, og, y_cv, y_ret, w, res, tn):
    wa = o.shape[1]

    def body(o_ref, og_ref, cv_ref, ret_ref, w_ref, r_ref, out_ref, ycat_ref):
        @pl.when(pl.program_id(0) == 0)
        def _():
            gv = og_ref[...]
            for c in range(NB):
                rows = slice(c * T, (c + 1) * T)
                ycat_ref[rows, 0:wa] = _rms_rows(o_ref[rows, :], gv)[0].astype(BF16)
                ycat_ref[rows, wa:wa + CV] = cv_ref[rows, :]
                ycat_ref[rows, wa + CV:D] = ret_ref[rows, :]

        acc = _dot(ycat_ref[...], w_ref[...])
        rows = lax.broadcasted_iota(jnp.int32, (L, 1), 0)
        out_ref[...] = r_ref[...] + jnp.where(rows >= PAD, acc, 0.0)

    whole = lambda w_: pl.BlockSpec((L, w_), lambda j: (0, 0))
    tile = pl.BlockSpec((L, tn), lambda j: (0, j))
    return pl.pallas_call(
        body, name=name, out_shape=(jax.ShapeDtypeStruct((L, D), F32), jax.ShapeDtypeStruct((L, D), BF16)),
        grid=(D // tn,),
        in_specs=[whole(wa), pl.BlockSpec((1, wa), lambda j: (0, 0)), whole(CV), whole(RW),
                  pl.BlockSpec((D, tn), lambda j: (0, j)), tile],
        out_specs=(tile, whole(D)), compiler_params=_params(VMEM_BIG))(o, og, y_cv, y_ret, w, res)


def in_proj_fwd(name, h, g, wt, tn, after=None):
    return _rms_mm(name, h, g, wt, grid=(IN_W // tn,), b_spec=pl.BlockSpec((tn, D), lambda j: (j, 0)),
                   o_spec=pl.BlockSpec((L, tn), lambda j: (0, j)), out_shape=jax.ShapeDtypeStruct((L, IN_W), F32),
                   after=after)


def ffn_up_fwd(name, h, g, wupt):
    return _rms_mm(name, h, g, wupt, grid=(NDEV,), b_spec=pl.BlockSpec((None, UPW, D), lambda j: (j, 0, 0)),
                   o_spec=pl.BlockSpec((None, L, UPW), lambda j: (j, 0, 0)),
                   out_shape=jax.ShapeDtypeStruct((NDEV, L, UPW), F32))


def _slab_specs():
    gate = pl.BlockSpec((None, L, UPW), lambda j: (jnp.minimum(j, NPAIR - 1), 0, 0))
    up = pl.BlockSpec((None, L, UPW), lambda j: (jnp.maximum(j - NPAIR, 0), 0, 0))
    return gate, up


def ffn_up_dx(name, dfg, dfu, wupt):
    def body(g_ref, u_ref, b_ref, o_ref):
        j = pl.program_id(0)

        @pl.when(j == 0)
        def _():
            o_ref[...] = _dot(g_ref[...], b_ref[...])

        @pl.when(jnp.logical_and(j > 0, j < NPAIR))
        def _():
            o_ref[...] += _dot(g_ref[...], b_ref[...])

        @pl.when(j >= NPAIR)
        def _():
            o_ref[...] += _dot(u_ref[...], b_ref[...])

    gate, up = _slab_specs()
    return pl.pallas_call(
        body, name=name, out_shape=jax.ShapeDtypeStruct((L, D), F32), grid=(NDEV,),
        in_specs=[gate, up, pl.BlockSpec((None, UPW, D), lambda j: (j, 0, 0))],
        out_specs=pl.BlockSpec((L, D), lambda j: (0, 0)), compiler_params=_params(VMEM_BIG))(dfg, dfu, wupt)


def ffn_up_dw(name, u2, dfg, dfu):
    tdot = lambda a, b: lax.dot_general(a, b, (((0,), (0,)), ((), ())), preferred_element_type=F32)

    def body(a_ref, g_ref, u_ref, o_ref):
        j = pl.program_id(0)

        @pl.when(j < NPAIR)
        def _():
            o_ref[...] = tdot(g_ref[...], a_ref[...]).astype(o_ref.dtype)

        @pl.when(j >= NPAIR)
        def _():
            o_ref[...] = tdot(u_ref[...], a_ref[...]).astype(o_ref.dtype)

    gate, up = _slab_specs()
    return pl.pallas_call(
        body, name=name, out_shape=jax.ShapeDtypeStruct((NDEV, UPW, D), BF16), grid=(NDEV,),
        in_specs=[pl.BlockSpec((L, D), lambda j: (0, 0)), gate, up],
        out_specs=pl.BlockSpec((None, UPW, D), lambda j: (j, 0, 0)), compiler_params=_params(VMEM_BIG))(u2, dfg, dfu)


def _ffn_down_tile(a_ref, b_ref, r_ref, tn):
    acc = jnp.zeros((L, tn), F32)
    for g in range(NPAIR):
        bv = b_ref[2 * g:2 * g + 2]
        acc = acc + _dot(a_ref[g], bv.reshape(2 * DNR, tn))
    rows = lax.broadcasted_iota(jnp.int32, (L, 1), 0)
    return r_ref[...] + jnp.where(rows >= PAD, acc, 0.0)


def _ffn_down_specs(tn):
    return [pl.BlockSpec((NPAIR, L, UPW), lambda j: (0, 0, 0)), pl.BlockSpec((NDEV, DNR, tn), lambda j: (0, 0, j)),
            pl.BlockSpec((L, tn), lambda j: (0, j))]


def ffn_down_fwd(name, a, wdn, res, tn=256):
    def body(a_ref, b_ref, r_ref, o_ref):
        o_ref[...] = _ffn_down_tile(a_ref, b_ref, r_ref, tn)

    return pl.pallas_call(
        body, name=name, out_shape=jax.ShapeDtypeStruct((L, D), F32), grid=(D // tn,),
        in_specs=_ffn_down_specs(tn), out_specs=pl.BlockSpec((L, tn), lambda j: (0, j)),
        compiler_params=_params(VMEM_BIG))(a, wdn, res)


def ffn_down_loss(name, a, wdn, res, target, tn=256):
    def body(a_ref, b_ref, r_ref, t_ref, dh_ref, dhm_ref, loss_ref):
        j = pl.program_id(0)
        hv = _ffn_down_tile(a_ref, b_ref, r_ref, tn)
        e = hv[T:] - t_ref[...]
        dh = e * (1.0 / D)
        dh_ref[0:T, :] = jnp.zeros((T, tn), F32)
        dh_ref[T:, :] = dh
        dhm_ref[0:T, :] = jnp.zeros((T, tn), BF16)
        dhm_ref[T:, :] = dh.astype(BF16)
        part = jnp.sum(jnp.sum(e * e, axis=1, keepdims=True), axis=0, keepdims=True) * (0.5 / D)

        @pl.when(j == 0)
        def _():
            loss_ref[...] = jnp.broadcast_to(part, loss_ref.shape)

        @pl.when(j > 0)
        def _():
            loss_ref[...] += jnp.broadcast_to(part, loss_ref.shape)

    tile = pl.BlockSpec((L, tn), lambda j: (0, j))
    return pl.pallas_call(
        body, name=name,
        out_shape=(jax.ShapeDtypeStruct((L, D), F32), jax.ShapeDtypeStruct((L, D), BF16),
                   jax.ShapeDtypeStruct((8, 128), F32)),
        grid=(D // tn,),
        in_specs=_ffn_down_specs(tn) + [pl.BlockSpec((SEQ, tn), lambda j: (0, j))],
        out_specs=(tile, tile, pl.BlockSpec((8, 128), lambda j: (0, 0))),
        compiler_params=_params(VMEM_BIG))(a, wdn, res, target)


def ffn_down_dx(name, dh, wdn):
    return _mm(name, dh, wdn, grid=(NPAIR,), nt=True,
               a_spec=pl.BlockSpec((L, D), lambda g: (0, 0)),
               b_spec=pl.BlockSpec((2, DNR, D), lambda g: (g, 0, 0)),
               o_spec=pl.BlockSpec((None, L, UPW), lambda g: (g, 0, 0)),
               out_shape=jax.ShapeDtypeStruct((NPAIR, L, UPW), F32))


def ffn_down_dw(name, a, dh):
    return _mm(name, a, dh, grid=(NPAIR,), ta=True,
               a_spec=pl.BlockSpec((None, L, UPW), lambda g: (g, 0, 0)),
               b_spec=pl.BlockSpec((L, D), lambda g: (0, 0)),
               o_spec=pl.BlockSpec((UPW, D), lambda g: (g, 0)),
               out_shape=jax.ShapeDtypeStruct((D_FF, D), BF16))


_SLOPES = [2.0 ** (-8.0 * (h + 1) / NQ) for h in range(NQ)]
_SCALE = HD ** -0.5


QR = GQA * T
KC = 3 * T


def _attn_geometry(g, n, sink_ref):
    row = lax.broadcasted_iota(jnp.int32, (QR, KC), 0)
    col = lax.broadcasted_iota(jnp.int32, (QR, KC), 1)
    i = row & (T - 1)
    j = col & (T - 1)
    blk = col // T
    d_meta = n * T + i - j
    ok_meta = (j >= PAD) & (d_meta >= 0)
    ok_prev = j > i + jnp.where(n >= 2, 0, T)
    ok_cur = j <= i - jnp.where(n >= 1, 0, T)
    ok = ((blk == 0) & ok_meta) | ((blk == 1) & ok_prev) | ((blk == 2) & ok_cur)
    dist = jnp.where(blk == 0, jnp.minimum(d_meta, T), jnp.where(blk == 1, T + i - j, i - j)).astype(F32)
    head = lax.broadcasted_iota(jnp.int32, (QR, 1), 0) // T
    slope = jnp.zeros((QR, 1), F32)
    sink = jnp.zeros((QR, 1), F32)
    for hh in range(GQA):
        slope = jnp.where(head == hh, jnp.where(g == 0, _SLOPES[hh], _SLOPES[GQA + hh]), slope)
        sink = jnp.where(head == hh, sink_ref[g * GQA + hh], sink)
    return ok, slope * dist, sink


def _attn_probs(qn, keys, geo):
    ok, penalty, sink = geo
    s = jnp.where(ok, _dot(qn, keys, nt=True) * _SCALE - penalty, NEG)
    m = jnp.maximum(jnp.max(s, axis=-1, keepdims=True), sink)
    e = jnp.exp(s - m)
    e_sink = jnp.exp(sink - m)
    inv = 1.0 / (e.sum(axis=-1, keepdims=True) + e_sink)
    return e, inv, e_sink * inv


def _rms_rows(x, g):
    r = lax.rsqrt(jnp.mean(x * x, axis=-1, keepdims=True) + RMS_EPS)
    xh = x * r
    return xh * g, xh, r


def _rms_rows_bwd(dy, g, xh, r):
    dxh = dy * g
    return r * (dxh - xh * jnp.mean(dxh * xh, axis=-1, keepdims=True))


def _rows3(m_ref, p_ref, c_ref):
    return jnp.concatenate([m_ref[...], p_ref[...], c_ref[...]], axis=0)


_ATT_K, _ATT_V = 4, 5


def _nat_specs():
    qspec = pl.BlockSpec((T, NQ * HD), lambda n: (n, 0))
    kv = lambda col: (pl.BlockSpec((T, NKV * HD), lambda n: (n, col)),
                      pl.BlockSpec((T, NKV * HD), lambda n: (jnp.maximum(n - 1, 0), col)),
                      pl.BlockSpec((T, NKV * HD), lambda n: (0, col)))
    return qspec, kv(_ATT_K), kv(_ATT_V)


def _group(g, x, w=HD):
    return x[:, g * w:(g + 1) * w]


def _stack_heads(x):
    return jnp.concatenate([x[:, hh * HD:(hh + 1) * HD] for hh in range(GQA)], axis=0)


def _unstack_heads(x):
    return jnp.concatenate([x[hh * T:(hh + 1) * T] for hh in range(GQA)], axis=1)


def attn_fwd(proj, qg, kg, sinks, name):
    def body(sink_ref, q_ref, kc_ref, kp_ref, km_ref, vc_ref, vp_ref, vm_ref, qg_ref, kg_ref, o_ref):
        n = pl.program_id(0)
        k_all = _rows3(km_ref, kp_ref, kc_ref)
        v_all = _rows3(vm_ref, vp_ref, vc_ref)
        q_all = q_ref[...]
        outs = []
        for g in range(NKV):
            geo = _attn_geometry(g, n, sink_ref)
            keys = _bf(_rms_rows(_group(g, k_all), kg_ref[...])[0])
            vals = _bf(_group(g, v_all))
            qn = _bf(_rms_rows(_stack_heads(_group(g, q_all, GQA * HD)), qg_ref[...])[0])
            e, inv, _ = _attn_probs(qn, keys, geo)
            outs.append(_unstack_heads(_dot(_bf(e), vals) * inv))
        o_ref[...] = jnp.concatenate(outs, axis=1)

    qspec, (kc, kp, km), (vc, vp, vm) = _nat_specs()
    vec = pl.BlockSpec((1, HD), lambda n: (0, 0))
    return pl.pallas_call(
        body, name=name, out_shape=jax.ShapeDtypeStruct((L, NQ * HD), F32), grid=(NB,),
        in_specs=[pl.BlockSpec(memory_space=pltpu.SMEM), qspec, kc, kp, km, vc, vp, vm, vec, vec],
        out_specs=qspec)(sinks, proj, proj, proj, proj, proj, proj, proj, qg, kg)


def attn_bwd(proj, o, dycat, og, qg, kg, sinks, name):
    def body(sink_ref, q_ref, kc_ref, kp_ref, km_ref, vc_ref, vp_ref, vm_ref, o_ref, dy_ref, og_ref, qg_ref, kg_ref,
             dq_ref, dkc_ref, dkp_ref, dkm_ref, dvc_ref, dvp_ref, dvm_ref, dqg_ref, dkg_ref, dsk_ref, dog_ref):
        n = pl.program_id(0)

        @pl.when(n == 0)
        def _():
            for r in (dkm_ref, dvm_ref, dqg_ref, dkg_ref, dsk_ref, dog_ref):
                r[...] = jnp.zeros_like(r)

        ogv = og_ref[...]
        dyv = dy_ref[...]
        _, oh, orr = _rms_rows(o_ref[...], ogv)
        do_all = _rms_rows_bwd(dyv, ogv, oh, orr)
        dog_ref[...] += jnp.sum(dyv * oh, axis=0, keepdims=True)

        kgv = kg_ref[...]
        qgv = qg_ref[...]
        k_all = _rows3(km_ref, kp_ref, kc_ref)
        v_all = _rows3(vm_ref, vp_ref, vc_ref)
        q_all = q_ref[...]
        dq, dk, dv, dsk_rows = [], [], [], []
        dqg_acc = jnp.zeros((1, HD), F32)
        dkg_acc = jnp.zeros((1, HD), F32)
        for g in range(NKV):
            geo = _attn_geometry(g, n, sink_ref)
            kn_f, kh, kr = _rms_rows(_group(g, k_all), kgv)
            keys = _bf(kn_f)
            vals = _bf(_group(g, v_all))
            qn_f, qh, qr = _rms_rows(_stack_heads(_group(g, q_all, GQA * HD)), qgv)
            qn = _bf(qn_f)
            e, inv, p_sink = _attn_probs(qn, keys, geo)
            dob = _bf(_stack_heads(_group(g, do_all, GQA * HD)))
            p = e * inv
            dp = _dot(dob, vals, nt=True)
            delta = (p * dp).sum(axis=-1, keepdims=True)
            for hh in range(GQA):
                part = -jnp.sum((p_sink * delta)[hh * T:(hh + 1) * T], axis=0, keepdims=True)
                dsk_rows.append(jnp.broadcast_to(part, (1, 128)))
            ds = p * (dp - delta)
            dqn = _dot(_bf(ds), keys) * _SCALE
            dq.append(_unstack_heads(_rms_rows_bwd(dqn, qgv, qh, qr)))
            dqg_acc = dqg_acc + jnp.sum(dqn * qh, axis=0, keepdims=True)
            dkn = _dot(_bf(ds.T), qn) * _SCALE
            dkg_acc = dkg_acc + jnp.sum(dkn * kh, axis=0, keepdims=True)
            dk.append(_rms_rows_bwd(dkn, kgv, kh, kr))
            dv.append(_dot(_bf(p.T), dob))
        dq_ref[...] = jnp.concatenate(dq, axis=1).astype(dq_ref.dtype)
        dk_all = jnp.concatenate(dk, axis=1)
        dv_all = jnp.concatenate(dv, axis=1)
        dkm_ref[...] += dk_all[0:T]
        dkp_ref[...] = dk_all[T:2 * T]
        dkc_ref[...] = dk_all[2 * T:3 * T]
        dvm_ref[...] += dv_all[0:T]
        dvp_ref[...] = dv_all[T:2 * T]
        dvc_ref[...] = dv_all[2 * T:3 * T]
        dsk_ref[...] += _stack_rows(dsk_rows)
        dqg_ref[...] += dqg_acc
        dkg_ref[...] += dkg_acc

    qspec, (kc, kp, km), (vc, vp, vm) = _nat_specs()
    vec = pl.BlockSpec((1, HD), lambda n: (0, 0))
    cur = pl.BlockSpec((T, NKV * HD), lambda n: (n, 0))
    meta = pl.BlockSpec((T, NKV * HD), lambda n: (0, 0))
    kv_shape = jax.ShapeDtypeStruct((L, NKV * HD), F32)
    meta_shape = jax.ShapeDtypeStruct((T, NKV * HD), F32)
    vec_shape = jax.ShapeDtypeStruct((1, HD), F32)
    ogvec = pl.BlockSpec((1, NQ * HD), lambda n: (0, 0))
    return pl.pallas_call(
        body, name=name,
        out_shape=(jax.ShapeDtypeStruct((L, NQ * HD), BF16), kv_shape, kv_shape, meta_shape, kv_shape, kv_shape,
                   meta_shape, vec_shape, vec_shape, jax.ShapeDtypeStruct((NQ, 128), F32),
                   jax.ShapeDtypeStruct((1, NQ * HD), F32)),
        grid=(NB,),
        in_specs=[pl.BlockSpec(memory_space=pltpu.SMEM), qspec, kc, kp, km, vc, vp, vm, qspec, qspec, ogvec, vec, vec],
        out_specs=(qspec, cur, cur, meta, cur, cur, meta, vec, vec,
                   pl.BlockSpec((NQ, 128), lambda n: (0, 0)), ogvec))(
            sinks, proj, proj, proj, proj, proj, proj, proj, o, dycat, og, qg, kg)


RW = RH * HD


def _ret_tables():
    h = np.arange(RH, dtype=np.float64)
    lg = np.log1p(-np.exp2(-5.0 - h))
    idx = np.arange(T, dtype=np.float64)
    diff = idx[:, None] - idx[None, :]
    decay = np.where(diff[None] >= 0, np.exp(np.maximum(diff, 0.0)[None] * lg[:, None, None]), 0.0)
    zeta = np.exp((T - 1 - idx)[None, :] * lg[:, None])
    xi = np.exp((idx + 1.0)[None, :] * lg[:, None])
    cd = np.exp(T * lg)
    lanes = lambda a: np.repeat(a.T, HD, axis=1)
    head_of = np.arange(RW) // HD
    same = (head_of[:, None] == head_of[None, :]).astype(np.float64)
    f = lambda a: jnp.asarray(a, F32)
    return dict(decay=f(decay), zeta=f(lanes(zeta)), xi=f(lanes(xi)), cd=f(np.repeat(cd, HD)[None, :]),
                head=f((head_of[None, :] == np.arange(RH)[:, None]).astype(np.float64)[:, None, :]),
                same=f(same), avg=jnp.asarray(same / HD, BF16))


def _seg_mean(x, avg):
    hi = _bf(x)
    lo = _bf(x - hi.astype(F32))
    return _dot(hi, avg) + _dot(lo, avg)


def _ret_specs(col0, order):
    return lambda col: pl.BlockSpec((T, RW), lambda i: (order(i), col0 + col))


def _ret_chunk(q, kf, v, s, tb):
    dec, xi, head = tb
    vb = _bf(v)
    kb = _bf(kf)
    y = _dot(_bf(q * xi), _bf(s))
    a = []
    for h in range(RH):
        a.append(_dot(_bf(q * head[h]), kb, nt=True) * dec[h])
        y = y + head[h] * _dot(_bf(a[h]), vb)
    return a, y


def _gn_rows(y):
    mu = jnp.mean(y, axis=-1, keepdims=True)
    yc = y - mu
    rstd = lax.rsqrt(jnp.mean(yc * yc, axis=-1, keepdims=True) + LN_EPS)
    return yc * rstd, rstd


_RET_COL0 = _RET_Q = 5


def _ret_consts(tb):
    names = ("decay", "zeta", "xi", "cd", "head", "same", "avg")
    full = lambda a: pl.BlockSpec(a.shape, lambda i: (0,) * a.ndim)
    return [tb[n] for n in names], [full(tb[n]) for n in names]


def _ret_inputs(n, q_ref, k_ref, v_ref):
    rows = n * T + lax.broadcasted_iota(jnp.int32, (T, 1), 0)
    valid = rows >= PAD
    return q_ref[...], jnp.where(valid, k_ref[...] * (HD ** -0.5), 0.0), v_ref[...], valid


def ret_fwd(proj, gng, tb, name):
    def body(q_ref, k_ref, v_ref, g_ref, dec_ref, zeta_ref, xi_ref, cd_ref, head_ref, same_ref, avg_ref, gng_ref,
             y_ref, st_ref, s_scr):
        n = pl.program_id(0)

        @pl.when(n == 0)
        def _():
            s_scr[...] = jnp.zeros_like(s_scr)

        s = s_scr[...]
        st_ref[...] = s
        q, kf, v, _ = _ret_inputs(n, q_ref, k_ref, v_ref)
        _, y = _ret_chunk(q, kf, v, s, (dec_ref, xi_ref[...], head_ref))
        s_scr[...] = cd_ref[...] * s + same_ref[...] * _dot(_bf((kf * zeta_ref[...]).T), _bf(v))
        avg = avg_ref[...]
        yc = y - _seg_mean(y, avg)
        yh = yc * lax.rsqrt(_seg_mean(yc * yc, avg) + LN_EPS)
        gv = g_ref[...]
        y_ref[...] = (gv * _sig(gv) * (yh * gng_ref[...])).astype(y_ref.dtype)

    col = _ret_specs(_RET_COL0, lambda i: i)
    consts, cspecs = _ret_consts(tb)
    return pl.pallas_call(
        body, name=name,
        out_shape=(jax.ShapeDtypeStruct((L, RW), BF16), jax.ShapeDtypeStruct((NB, RW, RW), F32)),
        grid=(NB,),
        in_specs=[col(0), col(1), col(2), col(3)] + cspecs + [pl.BlockSpec((1, RW), lambda i: (0, 0))],
        out_specs=(pl.BlockSpec((T, RW), lambda i: (i, 0)), pl.BlockSpec((None, RW, RW), lambda i: (i, 0, 0))),
        scratch_shapes=[pltpu.VMEM((RW, RW), F32)])(proj, proj, proj, proj, *consts, gng)


def ret_bwd(proj, dycat, states, gng, tb, name):
    def body(q_ref, k_ref, v_ref, g_ref, do_ref, st_ref, dec_ref, zeta_ref, xi_ref, cd_ref, head_ref, same_ref,
             avg_ref, gng_ref, d_ref, dgn_ref, ds_scr):
        i = pl.program_id(0)
        n = NB - 1 - i

        @pl.when(i == 0)
        def _():
            ds_scr[...] = jnp.zeros_like(ds_scr)

        dsn = _bf(ds_scr[...])
        s = st_ref[...]
        sb = _bf(s)
        q, kf, v, valid = _ret_inputs(n, q_ref, k_ref, v_ref)
        xi, zeta, avg = xi_ref[...], zeta_ref[...], avg_ref[...]
        a, y = _ret_chunk(q, kf, v, s, (dec_ref, xi, head_ref))
        yc = y - _seg_mean(y, avg)
        rstd = lax.rsqrt(_seg_mean(yc * yc, avg) + LN_EPS)
        yh = yc * rstd
        gv = g_ref[...]
        sg = _sig(gv)
        sil = gv * sg
        gn = gng_ref[...]
        dout = do_ref[...]
        d_ref[:, 3 * RW:4 * RW] = (dout * (yh * gn) * (sg * (1.0 + gv * (1.0 - sg)))).astype(d_ref.dtype)
        dyh = dout * sil * gn
        part = jnp.sum(dout * sil * yh, axis=0, keepdims=True)

        @pl.when(i == 0)
        def _():
            dgn_ref[...] = part

        @pl.when(i > 0)
        def _():
            dgn_ref[...] += part

        dy = rstd * (dyh - _seg_mean(dyh, avg) - yh * _seg_mean(dyh * yh, avg))
        dyb = _bf(dy)
        vb, kb, qb = _bf(v), _bf(kf), _bf(q)
        dq = _dot(dyb, sb, nt=True) * xi
        dkf = _dot(vb, dsn, nt=True) * zeta
        dv = _dot(_bf(kf * zeta), dsn)
        for h in range(RH):
            m = head_ref[h]
            da = _dot(_bf(dy * m), vb, nt=True) * dec_ref[h]
            dv = dv + m * _dot(_bf(a[h].T), dyb)
            dq = dq + m * _dot(_bf(da), kb)
            dkf = dkf + m * _dot(_bf(da.T), qb)
        d_ref[:, 0:RW] = dq.astype(d_ref.dtype)
        d_ref[:, RW:2 * RW] = jnp.where(valid, dkf * (HD ** -0.5), 0.0).astype(d_ref.dtype)
        d_ref[:, 2 * RW:3 * RW] = dv.astype(d_ref.dtype)
        ds_scr[...] = cd_ref[...] * ds_scr[...] + same_ref[...] * _dot(_bf((q * xi).T), dyb)

    back = lambda i: NB - 1 - i
    col = _ret_specs(_RET_COL0, back)
    consts, cspecs = _ret_consts(tb)
    vec = pl.BlockSpec((1, RW), lambda i: (0, 0))
    return pl.pallas_call(
        body, name=name,
        out_shape=(jax.ShapeDtypeStruct((L, 4 * RW), BF16), jax.ShapeDtypeStruct((1, RW), F32)),
        grid=(NB,),
        in_specs=[col(0), col(1), col(2), col(3), _ret_specs(3, back)(0),
                  pl.BlockSpec((None, RW, RW), lambda i: (back(i), 0, 0))] + cspecs + [vec],
        out_specs=(pl.BlockSpec((T, 4 * RW), lambda i: (back(i), 0)), vec),
        scratch_shapes=[pltpu.VMEM((RW, RW), F32)])(proj, proj, proj, proj, dycat, states, *consts, gng)


HALO = 32
TAP0 = HALO - (CONV_K - 1)


def _conv_specs():
    cur = lambda col=0: pl.BlockSpec((T, CV), lambda c: (c, col))
    before = lambda col=0: pl.BlockSpec((HALO, CV), lambda c: (jnp.maximum(c * (T // HALO) - 1, 0), col))
    after = pl.BlockSpec((HALO, CV), lambda c: (jnp.minimum((c + 1) * (T // HALO), L // HALO - 1), 0))
    full = lambda r, w: pl.BlockSpec((r, w), lambda c: (0, 0))
    return cur, before, after, full


_CONV_A, _CONV_B = 3, 4
_DY_CONV = 2


def _conv_post(xc, lg_ref, lb_ref):
    xh, rstd = _gn_rows(xc)
    z = xh * lg_ref[...] + lb_ref[...]
    return xh, rstd, z, _sig(z)


def conv_fwd(proj, w, b, lg, lb, pw, og, name):
    def body(ca_ref, cah_ref, cb_ref, cbh_ref, w_ref, b_ref, lg_ref, lb_ref, pw_ref, og_ref, y_ref, xc_ref, u_scr):
        c = pl.program_id(0)
        u_scr[0:HALO, :] = jnp.where(c > 0, cah_ref[...] * _sig(cbh_ref[...]), 0.0)
        u_scr[HALO:HALO + T, :] = ca_ref[...] * _sig(cb_ref[...])
        acc = jnp.zeros((T, CV), F32)
        for k in range(CONV_K):
            acc = acc + w_ref[k:k + 1, :] * u_scr[TAP0 + k:TAP0 + k + T, :]
        xc = acc + b_ref[...]
        xc_ref[...] = xc
        _, _, z, sg = _conv_post(xc, lg_ref, lb_ref)
        zp = _dot(_bf(z * sg), pw_ref[...])
        y_ref[...] = _rms_rows(zp, og_ref[...])[0].astype(y_ref.dtype)

    cur, before, _, full = _conv_specs()
    vec = full(1, CV)
    seq = jax.ShapeDtypeStruct((L, CV), F32)
    return pl.pallas_call(
        body, name=name, out_shape=(jax.ShapeDtypeStruct((L, CV), BF16), seq), grid=(NB,),
        in_specs=[cur(_CONV_A), before(_CONV_A), cur(_CONV_B), before(_CONV_B), full(CONV_K, CV), vec, vec, vec,
                  full(CV, CV), vec],
        out_specs=(cur(), cur()),
        scratch_shapes=[pltpu.VMEM((HALO + T, CV), F32)])(proj, proj, proj, proj, w, b, lg, lb, pw, og)


def conv_bwd(proj, xc, dycat, w, lg, lb, pw, og, name):
    def body1(xc_ref, dy_ref, lg_ref, lb_ref, pw_ref, og_ref, dxc_ref, db_ref, dlg_ref, dlb_ref, dog_ref, dpw_ref):
        c = pl.program_id(0)

        @pl.when(c == 0)
        def _():
            for r in (db_ref, dlg_ref, dlb_ref, dog_ref, dpw_ref):
                r[...] = jnp.zeros_like(r)

        xh, rstd, z, sg = _conv_post(xc_ref[...], lg_ref, lb_ref)
        s = z * sg
        zp = _dot(_bf(s), pw_ref[...])
        ogv = og_ref[...]
        _, zph, r2 = _rms_rows(zp, ogv)
        dyv = dy_ref[...]
        dog_ref[...] += jnp.sum(dyv * zph, axis=0, keepdims=True)
        dzpb = _bf(_rms_rows_bwd(dyv, ogv, zph, r2))
        dpw_ref[...] += _dot(_bf(s.T), dzpb)
        dz = _dot(dzpb, pw_ref[...], nt=True) * (sg * (1.0 + z * (1.0 - sg)))
        dlg_ref[...] += jnp.sum(dz * xh, axis=0, keepdims=True)
        dlb_ref[...] += jnp.sum(dz, axis=0, keepdims=True)
        dxh = dz * lg_ref[...]
        dxc = rstd * (dxh - jnp.mean(dxh, axis=-1, keepdims=True) - xh * jnp.mean(dxh * xh, axis=-1, keepdims=True))
        db_ref[...] += jnp.sum(dxc, axis=0, keepdims=True)
        dxc_ref[...] = dxc

    def body2(dx_ref, dxa_ref, ca_ref, cb_ref, w_ref, dca_ref, dcb_ref, dw_ref, d_scr):
        c = pl.program_id(0)

        @pl.when(c == 0)
        def _():
            dw_ref[...] = jnp.zeros_like(dw_ref)

        d_scr[0:T, :] = dx_ref[...]
        d_scr[T:T + HALO, :] = jnp.where(c < NB - 1, dxa_ref[...], 0.0)
        ca = ca_ref[...]
        sg = _sig(cb_ref[...])
        u = ca * sg
        du = jnp.zeros((T, CV), F32)
        for k in range(CONV_K):
            off = CONV_K - 1 - k
            dj = d_scr[off:off + T, :]
            du = du + w_ref[k:k + 1, :] * dj
            dw_ref[k:k + 1, :] += jnp.sum(u * dj, axis=0, keepdims=True)
        dca_ref[...] = (du * sg).astype(dca_ref.dtype)
        dcb_ref[...] = (du * ca * sg * (1.0 - sg)).astype(dcb_ref.dtype)

    cur, _, after, full = _conv_specs()
    seq = jax.ShapeDtypeStruct((L, CV), F32)
    vsh = jax.ShapeDtypeStruct((1, CV), F32)
    vec = full(1, CV)
    dxc, db, dlg, dlb, dog, dpw = pl.pallas_call(
        body1, name=name + "_a",
        out_shape=(seq, vsh, vsh, vsh, vsh, jax.ShapeDtypeStruct((CV, CV), F32)),
        grid=(NB,),
        in_specs=[cur(), cur(_DY_CONV), vec, vec, full(CV, CV), vec],
        out_specs=(cur(), vec, vec, vec, vec, full(CV, CV)))(xc, dycat, lg, lb, pw, og)
    dca, dcb, dw = pl.pallas_call(
        body2, name=name + "_b", grid=(NB,),
        out_shape=(jax.ShapeDtypeStruct((L, CV), BF16), jax.ShapeDtypeStruct((L, CV), BF16),
                   jax.ShapeDtypeStruct((CONV_K, CV), F32)),
        in_specs=[cur(), after, cur(_CONV_A), cur(_CONV_B), full(CONV_K, CV)],
        out_specs=(cur(), cur(), full(CONV_K, CV)),
        scratch_shapes=[pltpu.VMEM((T + HALO, CV), F32)])(dxc, dxc, proj, proj, w)
    return dca, dcb, dw, db, dlg, dlb, dog, dpw


HR = 8
PP = NPAIR


def _ffn_specs():
    cur = lambda off: pl.BlockSpec((PP, T, UPW), lambda j, c: (j + off // PP, c, 0))
    before = lambda off: pl.BlockSpec(
        (PP, HR, UPW), lambda j, c: (j + off // PP, jnp.maximum(c * (T // HR) - 1, 0), 0))
    after = lambda off: pl.BlockSpec(
        (PP, HR, UPW), lambda j, c: (j + off // PP, jnp.minimum((c + 1) * (T // HR), L // HR - 1), 0))
    wspec = lambda off, r: pl.BlockSpec((PP, r, UPW), lambda j, c: (j + off // PP, 0, 0))
    return cur, before, after, wspec


HR16 = 16


def _ffn_after16():
    return pl.BlockSpec((PP, HR16, UPW), lambda j, c: (j, jnp.minimum((c + 1) * (T // HR16), L // HR16 - 1), 0))


def ffn_act_fwd(f, w, b, name):
    strip = 16

    def body(*refs):
        for p in range(PP):
            one(*[r.at[p] for r in refs[:-2]], *refs[-2:])

    def one(fg_ref, fgh_ref, fu_ref, fuh_ref, wg_ref, wu_ref, bg_ref, bu_ref, a_ref, gc_ref, uc_ref, hg_scr, hu_scr):
        c = pl.program_id(1)
        for scr, x_ref, xh_ref in ((hg_scr, fg_ref, fgh_ref), (hu_scr, fu_ref, fuh_ref)):
            scr[0:HR, :] = jnp.where(c > 0, xh_ref[...], 0.0)
            scr[HR:HR + strip, :] = x_ref[0:strip, :]
        wg = [wg_ref[k:k + 1, :] for k in range(FFN_K)]
        wu = [wu_ref[k:k + 1, :] for k in range(FFN_K)]
        bg, bu = bg_ref[...], bu_ref[...]

        def conv(src, base, w, b):
            acc = b
            for k in range(FFN_K):
                o = base - (FFN_K - 1) + k
                acc = acc + w[k] * src[o:o + strip, :]
            return acc

        for s in range(T // strip):
            r = s * strip
            gate = conv(hg_scr, HR, wg, bg) if s == 0 else conv(fg_ref, r, wg, bg)
            up = conv(hu_scr, HR, wu, bu) if s == 0 else conv(fu_ref, r, wu, bu)
            gc_ref[r:r + strip, :] = gate.astype(BF16)
            uc_ref[r:r + strip, :] = up.astype(BF16)
            a_ref[r:r + strip, :] = (gate * _sig(gate) * up).astype(BF16)

    cur, before, _, wspec = _ffn_specs()
    pre = jax.ShapeDtypeStruct((NPAIR, L, UPW), BF16)
    return pl.pallas_call(
        body, name=name, out_shape=(jax.ShapeDtypeStruct((NPAIR, L, UPW), BF16), pre, pre), grid=(NPAIR // PP, NB),
        in_specs=[cur(0), before(0), cur(NPAIR), before(NPAIR), wspec(0, FFN_K), wspec(NPAIR, FFN_K),
                  wspec(0, 1), wspec(NPAIR, 1)],
        out_specs=(cur(0), cur(0), cur(0)),
        scratch_shapes=[pltpu.VMEM((HR + strip, UPW), F32), pltpu.VMEM((HR + strip, UPW), F32)])(
            f, f, f, f, w, w, b, b)


def ffn_act_bwd(f, gc, uc, da, w, name):
    ext = T + HR
    sub = 8

    def body(*refs):
        for p in range(PP):
            one(*[r.at[p] for r in refs[:-2]], *refs[-2:])

    def one(fg_ref, fu_ref, gc_ref, gca_ref, uc_ref, uca_ref, da_ref, daa_ref, wg_ref, wu_ref,
            dfg_ref, dfu_ref, dwg_ref, dwu_ref, dbg_ref, dbu_ref, dg_scr, du_scr):
        c = pl.program_id(1)
        wg = [wg_ref[k:k + 1, :] for k in range(FFN_K)]
        wu = [wu_ref[k:k + 1, :] for k in range(FFN_K)]

        for r in range(0, ext, HR16):
            rows = min(HR16, ext - r)
            if r < T:
                gate, up, dav = gc_ref[r:r + HR16, :].astype(F32), uc_ref[r:r + HR16, :].astype(F32), da_ref[r:r + HR16, :]
            else:
                gate, up = gca_ref[...].astype(F32)[0:rows], uca_ref[...].astype(F32)[0:rows]
                dav = jnp.where(c < NB - 1, daa_ref[...], 0.0)
            sg = _sig(gate)
            dg_scr[r:r + rows, :] = dav * up * (sg * (1.0 + gate * (1.0 - sg)))
            du_scr[r:r + rows, :] = dav * gate * sg

        def back(d_scr, x_ref, w, df_ref, dw_ref, db_ref):
            accs = [jnp.zeros((sub, UPW), F32) for _ in range(FFN_K + 1)]
            for s in range(T // (2 * sub)):
                pieces = []
                for r in (2 * s * sub, (2 * s + 1) * sub):
                    xv = x_ref[r:r + sub, :]
                    df = jnp.zeros((sub, UPW), F32)
                    for k in range(FFN_K):
                        off = FFN_K - 1 - k
                        dj = d_scr[r + off:r + off + sub, :]
                        df = df + w[k] * dj
                        accs[k] = accs[k] + xv * dj
                        if off == 0:
                            accs[FFN_K] = accs[FFN_K] + dj
                    pieces.append(df)
                df_ref[2 * s * sub:2 * (s + 1) * sub, :] = jnp.concatenate(pieces, axis=0).astype(BF16)
            dwp = _stack_rows([jnp.sum(a, axis=0, keepdims=True) for a in accs[:FFN_K]])
            dbp = jnp.sum(accs[FFN_K], axis=0, keepdims=True)

            @pl.when(c == 0)
            def _():
                dw_ref[...] = dwp
                db_ref[...] = dbp

            @pl.when(c > 0)
            def _():
                dw_ref[...] += dwp
                db_ref[...] += dbp

        back(dg_scr, fg_ref, wg, dfg_ref, dwg_ref, dbg_ref)
        back(du_scr, fu_ref, wu, dfu_ref, dwu_ref, dbu_ref)

    cur, _, after, wspec = _ffn_specs()
    slab = jax.ShapeDtypeStruct((NPAIR, L, UPW), BF16)
    wsh = jax.ShapeDtypeStruct((NPAIR, FFN_K, UPW), F32)
    bsh = jax.ShapeDtypeStruct((NPAIR, 1, UPW), F32)
    return pl.pallas_call(
        body, name=name, out_shape=(slab, slab, wsh, wsh, bsh, bsh), grid=(NPAIR // PP, NB),
        in_specs=[cur(0), cur(NPAIR), cur(0), _ffn_after16(), cur(0), _ffn_after16(), cur(0), after(0),
                  wspec(0, FFN_K), wspec(NPAIR, FFN_K)],
        out_specs=(cur(0), cur(0), wspec(0, FFN_K), wspec(0, FFN_K), wspec(0, 1), wspec(0, 1)),
        scratch_shapes=[pltpu.VMEM((ext, UPW), F32), pltpu.VMEM((ext, UPW), F32)])(
            f, f, gc, gc, uc, uc, da, da, w, w)


def _mesh_pos():
    return lax.axis_index("x"), lax.axis_index("y"), lax.axis_index("c")


def _peer(pos, k):
    x, y, c = pos
    px = 1 - x if k & 4 else x
    py = 1 - y if k & 2 else y
    pc = 1 - c if k & 1 else c
    return (px, py, pc), 4 * px + 2 * py + pc


_CHIP_FLIPS = (4, 2, 6)
_HBM = pl.BlockSpec(memory_space=pl.ANY)


def all_gather(shards, name):
    nt = len(shards)

    def body(*refs):
        ins, outs = refs[:nt], refs[nt:2 * nt]
        send, recv, local = refs[2 * nt:]
        pos = _mesh_pos()
        me = 4 * pos[0] + 2 * pos[1] + pos[2]
        sib, sib_id = _peer(pos, 1)

        def copy(t, k, block_id, to, src=None):
            dst = outs[t].at[block_id]
            return pltpu.make_async_remote_copy(
                src_ref=dst if src is None else src, dst_ref=dst, send_sem=send.at[t, k], recv_sem=recv.at[t, k],
                device_id=to, device_id_type=MESH)

        locals_ = [pltpu.make_async_copy(ins[t], outs[t].at[me], local.at[t]) for t in range(nt)]
        for cp in locals_:
            cp.start()
        started = []
        for j, flip in enumerate(_CHIP_FLIPS):
            for t in range(nt):
                started.append(copy(t, 1 + j, me, _peer(pos, flip)[0], src=ins[t]))
        for t in range(nt):
            started.append(copy(t, 0, me, sib, src=ins[t]))
        for cp in started:
            cp.start()
        for j, flip in enumerate(_CHIP_FLIPS):
            _, pid = _peer(pos, flip)
            for t in range(nt):
                copy(t, 1 + j, pid, sib).wait_recv()
                fwd = copy(t, 4 + j, pid, sib)
                fwd.start()
                started.append(fwd)
        for t in range(nt):
            copy(t, 0, sib_id, sib).wait_recv()
        for j, flip in enumerate(_CHIP_FLIPS):
            _, pid = _peer(pos, flip | 1)
            for t in range(nt):
                copy(t, 4 + j, pid, sib).wait_recv()
        for cp in started:
            cp.wait_send()
        for cp in locals_:
            cp.wait()

    return pl.pallas_call(
        body, name=name,
        out_shape=tuple(jax.ShapeDtypeStruct((NDEV,) + s.shape, s.dtype) for s in shards),
        in_specs=[_HBM] * nt, out_specs=tuple([_HBM] * nt),
        scratch_shapes=[pltpu.SemaphoreType.DMA((nt, NDEV - 1)), pltpu.SemaphoreType.DMA((nt, NDEV - 1)),
                        pltpu.SemaphoreType.DMA((nt,))])(*shards)


_SPLIT = dict(has_side_effects=pltpu.SideEffectType.DATAFLOW_SIDE_EFFECTING)
_SEM = pl.BlockSpec(memory_space=pltpu.SEMAPHORE)


def _exchange_start(name, srcs, lands, copies, after):
    ns, nl = len(srcs), len(lands)
    ncopy = len(copies(None, [None] * ns, [None] * nl))

    def body(*refs):
        src_refs, land_refs = refs[:ns], refs[ns:ns + nl]
        send, recv = refs[ns + nl + len(after):ns + nl + len(after) + 2]
        token = refs[-1]
        for i, (src, dst, peer) in enumerate(copies(_mesh_pos(), src_refs, land_refs)):
            pltpu.make_async_remote_copy(src_ref=src, dst_ref=dst, send_sem=send.at[i], recv_sem=recv.at[i],
                                         device_id=peer, device_id_type=MESH).start()
        token[...] = jnp.zeros_like(token)

    hbm = lambda a: pltpu.HBM(a.shape, a.dtype)
    out = pl.pallas_call(
        body, name=name,
        out_shape=(pltpu.SemaphoreType.DMA((ncopy,)), pltpu.SemaphoreType.DMA((ncopy,)),
                   *[hbm(a) for a in srcs], *[hbm(a) for a in lands], jax.ShapeDtypeStruct((8, 128), F32)),
        in_specs=[pl.BlockSpec(memory_space=pltpu.HBM)] * (ns + nl) + [_HBM] * len(after),
        out_specs=(_SEM, _SEM, *[pl.BlockSpec(memory_space=pltpu.HBM)] * (ns + nl),
                   pl.BlockSpec(memory_space=pltpu.VMEM)),
        input_output_aliases={i: 2 + i for i in range(ns + nl)},
        compiler_params=pltpu.CompilerParams(**_SPLIT))(
            *[pltpu.with_memory_space_constraint(a, pltpu.HBM) for a in list(srcs) + list(lands)], *after)
    return out[0], out[1], list(out[2:2 + ns]), list(out[2 + ns:2 + ns + nl]), out[-1]


def _exchange_wait(name, send, recv, srcs, lands, copies, after):
    ns, nl = len(srcs), len(lands)

    def body(*refs):
        src_refs, land_refs = refs[:ns], refs[ns:ns + nl]
        send_ref, recv_ref = refs[ns + nl:ns + nl + 2]
        pos = _mesh_pos()
        for i, (src, dst, peer) in enumerate(copies(pos, src_refs, land_refs, arriving=True)):
            cp = pltpu.make_async_remote_copy(src_ref=src, dst_ref=dst, send_sem=send_ref.at[i], recv_sem=recv_ref.at[i],
                                              device_id=peer, device_id_type=MESH)
            cp.wait_send()
            cp.wait_recv()

    hbm = lambda a: pltpu.HBM(a.shape, a.dtype)
    out = pl.pallas_call(
        body, name=name, out_shape=tuple(hbm(a) for a in list(srcs) + list(lands)),
        in_specs=[pl.BlockSpec(memory_space=pltpu.HBM)] * (ns + nl) + [_SEM, _SEM] + [_HBM] * len(after),
        out_specs=tuple([pl.BlockSpec(memory_space=pltpu.HBM)] * (ns + nl)),
        input_output_aliases={i: i for i in range(ns + nl)},
        compiler_params=pltpu.CompilerParams(**_SPLIT))(*srcs, *lands, send, recv, *after)
    return list(out[:ns]), list(out[ns:])


def _gather_copies(pos, srcs, lands, arriving=False):
    if pos is None:
        return [None] * (len(srcs) * (NDEV - 1))
    me = 4 * pos[0] + 2 * pos[1] + pos[2]
    out = []
    for src, land in zip(srcs, lands):
        for k in range(1, NDEV):
            peer, pid = _peer(pos, k)
            out.append((src, land.at[pid if arriving else me], peer))
    return out


def _scatter_copies(pos, srcs, lands, arriving=False):
    if pos is None:
        return [None] * (len(srcs) * (NDEV - 1))
    out = []
    for src, land in zip(srcs, lands):
        for k in range(1, NDEV):
            peer, pid = _peer(pos, k)
            out.append((src.at[pid], land.at[k - 1], peer))
    return out


def gather_start(name, shards, lands, after=()):
    return _exchange_start(name, shards, lands, _gather_copies, list(after))


def gather_wait(name, handle, after=()):
    send, recv, srcs, lands, _ = handle
    return _exchange_wait(name, send, recv, srcs, lands, _gather_copies, list(after))


_CHIP_LEVEL = (1,) + _CHIP_FLIPS


def _gather_chip_copies(pos, srcs, lands, arriving=False):
    if pos is None:
        return [None] * (len(srcs) * len(_CHIP_LEVEL))
    me = 4 * pos[0] + 2 * pos[1] + pos[2]
    out = []
    for src, land in zip(srcs, lands):
        for k in _CHIP_LEVEL:
            peer, pid = _peer(pos, k)
            out.append((src, land.at[pid if arriving else me], peer))
    return out


def _gather_forward_copies(pos, srcs, lands, arriving=False):
    if pos is None:
        return [None] * (len(lands) * len(_CHIP_FLIPS))
    sibling, _ = _peer(pos, 1)
    out = []
    for land in lands:
        for k in _CHIP_FLIPS:
            (px, py, pc), _ = _peer(pos, k)
            slot = 4 * px + 2 * py + ((1 - pc) if arriving else pc)
            out.append((land.at[slot], land.at[slot], sibling))
    return out


def gather2_start(name, shards, lands, after=()):
    return _exchange_start(name, shards, lands, _gather_chip_copies, list(after))


def gather2_forward(name, handle, after=()):
    send, recv, srcs, lands, _ = handle
    _, lands = _exchange_wait(name + "_wait", send, recv, srcs, lands, _gather_chip_copies, list(after))
    return _exchange_start(name + "_start", [], lands, _gather_forward_copies, [])


def gather2_wait(name, handle, after=()):
    send, recv, srcs, lands, _ = handle
    return _exchange_wait(name, send, recv, srcs, lands, _gather_forward_copies, list(after))


def scatter_start(name, grads, after=()):
    lands = [pltpu.with_memory_space_constraint(lax.empty((NDEV - 1,) + g.shape[1:], g.dtype), pltpu.HBM)
             for g in grads]
    return _exchange_start(name, grads, lands, _scatter_copies, list(after))


def scatter_wait(name, handle, after=()):
    send, recv, srcs, lands, _ = handle
    return _exchange_wait(name, send, recv, srcs, lands, _scatter_copies, list(after))


def sum_slots(parts, name):
    def body(p_ref, o_ref):
        acc = p_ref[0]
        for d in range(1, NDEV):
            acc = acc + p_ref[d]
        o_ref[...] = acc

    return pl.pallas_call(body, name=name, out_shape=jax.ShapeDtypeStruct(parts.shape[1:], F32))(parts)


def _adamw_update(g, w_ref, m_ref, v_ref, g_ref, d_ref, nm_ref, nv_ref):
    g_ref[...] = g
    nm = ADAM_B1 * m_ref[...] + (1.0 - ADAM_B1) * g
    nv = ADAM_B2 * v_ref[...] + (1.0 - ADAM_B2) * (g * g)
    nm_ref[...] = nm
    nv_ref[...] = nv
    m_hat = nm / (1.0 - ADAM_B1 ** ADAM_STEP)
    v_hat = nv / (1.0 - ADAM_B2 ** ADAM_STEP)
    d_ref[...] = -ADAM_LR * (m_hat / (jnp.sqrt(v_hat) + ADAM_EPS) + ADAM_WD * w_ref[...])


def adamw_big(own, lands, w, m, v, me, tr, name):
    _, r, c = own[0].shape
    nt = r // tr

    def body(me_ref, *refs):
        ins, (w_ref, m_ref, v_ref), outs = refs[:2 * DEPTH], refs[2 * DEPTH:2 * DEPTH + 3], refs[2 * DEPTH + 3:]
        layer = pl.program_id(0)
        for l in range(DEPTH):
            @pl.when(layer == l)
            def _(l=l):
                g = ins[2 * l][...].astype(F32)
                for s in range(NDEV - 1):
                    g = g + ins[2 * l + 1][s].astype(F32)
                _adamw_update(g, w_ref, m_ref, v_ref, *outs)

    pick = lambda l: (lambda layer, i, me_ref: jnp.where(layer == l, i, 0))
    in_specs = []
    for l in range(DEPTH):
        in_specs.append(pl.BlockSpec((None, tr, c), lambda layer, i, me_ref, f=pick(l): (me_ref[0], f(layer, i, me_ref), 0)))
        in_specs.append(pl.BlockSpec((NDEV - 1, tr, c), lambda layer, i, me_ref, f=pick(l): (0, f(layer, i, me_ref), 0)))
    blk = pl.BlockSpec((tr, c), lambda layer, i, me_ref: (layer * nt + i, 0))
    sh = jax.ShapeDtypeStruct((DEPTH * r, c), F32)
    args = [a for l in range(DEPTH) for a in (own[l], lands[l])]
    return pl.pallas_call(
        body, name=name, out_shape=(sh, sh, sh, sh),
        grid_spec=pltpu.PrefetchScalarGridSpec(
            num_scalar_prefetch=1, grid=(DEPTH, nt), in_specs=in_specs + [blk, blk, blk],
            out_specs=(blk, blk, blk, blk)),
        compiler_params=_params(VMEM_BIG))(me, *args, w, m, v)


def adamw_many(gs, ws, ms, vs, name):
    n = len(gs)

    def body(*refs):
        outs = refs[4 * n:]
        for i in range(n):
            _adamw_update(refs[i][...], refs[n + i], refs[2 * n + i], refs[3 * n + i], *outs[4 * i:4 * i + 4])

    out_shape = tuple(jax.ShapeDtypeStruct(w.shape, F32) for w in ws for _ in range(4))
    res = pl.pallas_call(body, name=name, out_shape=out_shape)(*gs, *ws, *ms, *vs)
    return [res[4 * i:4 * i + 4] for i in range(n)]


def local_step(x, target, P, hooks):
    tb = _ret_tables()
    h = jnp.concatenate([jnp.zeros((PAD, D), F32), P["meta"], x], axis=0)
    stash = []
    for l in range(DEPTH):
        s = {"h": h}
        s["w_in"], s["cv_pw"], s["w_out"] = hooks["mixer_w"](l, h)
        proj, s["u"] = in_proj_fwd(f"in_proj_{l}", h, P["norm_mix_g"][l], s["w_in"], 256,
                                   after=hooks["first_after"] if l == 0 else None)
        s["proj"] = proj
        s["o"] = attn_fwd(proj, P["q_norm_g"][l], P["k_norm_g"][l], P["attn_sinks"][l], f"attn_fwd_{l}")
        y_cv, s["xc"] = conv_fwd(proj, P["cv_dw_w"][l], P["cv_dw_b"][l], P["cv_ln_g"][l], P["cv_ln_b"][l],
                                 s["cv_pw"], P["cv_out_g"][l], f"conv_fwd_{l}")
        y_ret, s["states"] = ret_fwd(proj, P["ret_gn_g"][l], tb, f"ret_fwd_{l}")
        if "mixer_mid" in hooks:
            hooks["mixer_mid"](l, s["o"])
        h, s["ycat"] = out_proj_fwd(f"out_proj_{l}", s["o"], P["attn_out_g"][l], y_cv, y_ret, s["w_out"], h, 512)
        s["h1"] = h
        s["ffn_up"], s["ffn_down"] = hooks["ffn_w"](l, h)
        s["f"], s["u2"] = ffn_up_fwd(f"ffn_up_{l}", h, P["norm_ffn_g"][l], s["ffn_up"])
        a, s["gc"], s["uc"] = ffn_act_fwd(s["f"], P["ffn_dw_w"][l], P["ffn_dw_b"][l], f"ffn_act_fwd_{l}")
        s["a"] = a
        if l < DEPTH - 1:
            h = ffn_down_fwd(f"ffn_down_{l}", a, s["ffn_down"], h)
        else:
            dh, dhm, loss = ffn_down_loss(f"ffn_down_{l}", a, s["ffn_down"], h, target)
        stash.append(s)

    G = {n: [None] * DEPTH for n in ("norm_mix_g", "q_norm_g", "k_norm_g", "attn_sinks", "attn_out_g", "cv_dw_w",
                                     "cv_dw_b", "cv_ln_g", "cv_ln_b", "cv_out_g", "ret_gn_g", "norm_ffn_g",
                                     "ffn_dw_w", "ffn_dw_b")}
    for l in reversed(range(DEPTH)):
        s = stash[l]
        da = ffn_down_dx(f"ffn_down_dx_{l}", dhm, s["ffn_down"])
        dw_down = ffn_down_dw(f"ffn_down_dw_{l}", s["a"], dhm)
        dfg, dfu, dwg, dwu, dbg, dbu = ffn_act_bwd(s["f"], s["gc"], s["uc"], da, P["ffn_dw_w"][l], f"ffn_act_bwd_{l}")
        G["ffn_dw_w"][l] = jnp.concatenate([dwg, dwu], axis=0)
        G["ffn_dw_b"][l] = jnp.concatenate([dbg, dbu], axis=0)
        du2 = ffn_up_dx(f"ffn_up_dx_{l}", dfg, dfu, s["ffn_up"])
        dw_up = ffn_up_dw(f"ffn_up_dw_{l}", s["u2"], dfg, dfu)
        after = hooks["ffn_grads"](l, dw_down, dw_up)
        dh, dhm, G["norm_ffn_g"][l] = rms_bwd(s["h1"], P["norm_ffn_g"][l], du2, dh, f"rms_ffn_bwd_{l}", after=after)
        dycat = mm_nt(f"out_proj_dx_{l}", dhm, s["w_out"], 512)
        dw_out = mm_ta(f"out_proj_dw_{l}", s["ycat"], dhm, 512)
        (dq, dk_c, dk_p, dk_m, dv_c, dv_p, dv_m, G["q_norm_g"][l], G["k_norm_g"][l], dsk,
         G["attn_out_g"][l]) = attn_bwd(s["proj"], s["o"], dycat, P["attn_out_g"][l], P["q_norm_g"][l],
                                        P["k_norm_g"][l], P["attn_sinks"][l], f"attn_bwd_{l}")
        G["attn_sinks"][l] = dsk[:, 0]
        shift = lambda z: jnp.concatenate([z[T:], jnp.zeros((T, NKV * HD), F32)], axis=0)
        dk = (dk_c + shift(dk_p)).at[:T].add(dk_m)
        dv = (dv_c + shift(dv_p)).at[:T].add(dv_m)
        (dca, dcb, G["cv_dw_w"][l], G["cv_dw_b"][l], G["cv_ln_g"][l], G["cv_ln_b"][l], G["cv_out_g"][l],
         dpw) = conv_bwd(s["proj"], s["xc"], dycat, P["cv_dw_w"][l], P["cv_ln_g"][l], P["cv_ln_b"][l], s["cv_pw"],
                         P["cv_out_g"][l], f"conv_bwd_{l}")
        dret, G["ret_gn_g"][l] = ret_bwd(s["proj"], dycat, s["states"], P["ret_gn_g"][l], tb, f"ret_bwd_{l}")
        dproj = jnp.concatenate([dq, dk.astype(BF16), dv.astype(BF16), dca, dcb, dret], axis=1)
        du = mm_nn(f"in_proj_dx_{l}", dproj, s["w_in"], 512)
        dw_in = mm_ta(f"in_proj_dw_{l}", dproj, s["u"], 768)
        after = hooks["mixer_grads"](l, dw_out, dpw, dw_in)
        dh, dhm, G["norm_mix_g"][l] = rms_bwd(s["h"], P["norm_mix_g"][l], du, dh, f"rms_mix_bwd_{l}", after=after,
                                              last=(l == 0))
    return loss[0, 0], dh, dhm[PAD:T], G


_SMALL = ("meta", "norm_mix_g", "q_norm_g", "k_norm_g", "attn_sinks", "attn_out_g", "cv_dw_w", "cv_dw_b", "cv_ln_g",
          "cv_ln_b", "cv_out_g", "ret_gn_g", "norm_ffn_g", "ffn_dw_w", "ffn_dw_b")
_BIG = ("w_in", "cv_pw", "w_out", "ffn_up", "ffn_down")
_TRANSPOSED = ("w_in", "ffn_up")
_ORDER = ("meta", "norm_mix_g", "w_in", "q_norm_g", "k_norm_g", "attn_sinks", "attn_out_g", "cv_dw_w", "cv_dw_b",
          "cv_ln_g", "cv_ln_b", "cv_pw", "cv_out_g", "ret_gn_g", "w_out", "norm_ffn_g", "ffn_up", "ffn_dw_w",
          "ffn_dw_b", "ffn_down")
_SMALL_SHARDED = {"meta": D, "cv_dw_w": CV, "ffn_dw_w": 2 * D_FF}


def _pack(arrs):
    flat = jnp.concatenate([a.reshape(-1) for a in arrs])
    n = flat.shape[0]
    rows = -(-n // 1024) * 8
    return jnp.pad(flat, (0, rows * 128 - n)).reshape(rows, 128)


def _unpack(packed, shapes):
    flat = packed.reshape(-1)
    out, off = [], 0
    for s in shapes:
        n = int(np.prod(s))
        out.append(flat[off:off + n].reshape(s))
        off += n
    return out


def kernel(x, meta, norm_mix_g, w_in, q_norm_g, k_norm_g, attn_sinks, attn_out_g, cv_dw_w, cv_dw_b, cv_ln_g, cv_ln_b, cv_pw, cv_out_g, ret_gn_g, w_out, norm_ffn_g, ffn_up, ffn_dw_w, ffn_dw_b, ffn_down, loss_target, m_meta, m_norm_mix_g, m_w_in, m_q_norm_g, m_k_norm_g, m_attn_sinks, m_attn_out_g, m_cv_dw_w, m_cv_dw_b, m_cv_ln_g, m_cv_ln_b, m_cv_pw, m_cv_out_g, m_ret_gn_g, m_w_out, m_norm_ffn_g, m_ffn_up, m_ffn_dw_w, m_ffn_dw_b, m_ffn_down, v_meta, v_norm_mix_g, v_w_in, v_q_norm_g, v_k_norm_g, v_attn_sinks, v_attn_out_g, v_cv_dw_w, v_cv_dw_b, v_cv_ln_g, v_cv_ln_b, v_cv_pw, v_cv_out_g, v_ret_gn_g, v_w_out, v_norm_ffn_g, v_ffn_up, v_ffn_dw_w, v_ffn_dw_b, v_ffn_down):
    W = dict(meta=meta, norm_mix_g=norm_mix_g, w_in=w_in, q_norm_g=q_norm_g, k_norm_g=k_norm_g,
             attn_sinks=attn_sinks, attn_out_g=attn_out_g, cv_dw_w=cv_dw_w, cv_dw_b=cv_dw_b, cv_ln_g=cv_ln_g,
             cv_ln_b=cv_ln_b, cv_pw=cv_pw, cv_out_g=cv_out_g, ret_gn_g=ret_gn_g, w_out=w_out,
             norm_ffn_g=norm_ffn_g, ffn_up=ffn_up, ffn_dw_w=ffn_dw_w, ffn_dw_b=ffn_dw_b, ffn_down=ffn_down)
    M = dict(meta=m_meta, norm_mix_g=m_norm_mix_g, w_in=m_w_in, q_norm_g=m_q_norm_g, k_norm_g=m_k_norm_g,
             attn_sinks=m_attn_sinks, attn_out_g=m_attn_out_g, cv_dw_w=m_cv_dw_w, cv_dw_b=m_cv_dw_b,
             cv_ln_g=m_cv_ln_g, cv_ln_b=m_cv_ln_b, cv_pw=m_cv_pw, cv_out_g=m_cv_out_g, ret_gn_g=m_ret_gn_g,
             w_out=m_w_out, norm_ffn_g=m_norm_ffn_g, ffn_up=m_ffn_up, ffn_dw_w=m_ffn_dw_w, ffn_dw_b=m_ffn_dw_b,
             ffn_down=m_ffn_down)
    V = dict(meta=v_meta, norm_mix_g=v_norm_mix_g, w_in=v_w_in, q_norm_g=v_q_norm_g, k_norm_g=v_k_norm_g,
             attn_sinks=v_attn_sinks, attn_out_g=v_attn_out_g, cv_dw_w=v_cv_dw_w, cv_dw_b=v_cv_dw_b,
             cv_ln_g=v_cv_ln_g, cv_ln_b=v_cv_ln_b, cv_pw=v_cv_pw, cv_out_g=v_cv_out_g, ret_gn_g=v_ret_gn_g,
             w_out=v_w_out, norm_ffn_g=v_norm_ffn_g, ffn_up=v_ffn_up, ffn_dw_w=v_ffn_dw_w, ffn_dw_b=v_ffn_dw_b,
             ffn_down=v_ffn_down)
    me = 4 * lax.axis_index("x") + 2 * lax.axis_index("y") + lax.axis_index("c")
    for n in _TRANSPOSED:
        W[n], M[n], V[n] = (a.transpose(0, 2, 1) for a in (W[n], M[n], V[n]))

    sh = {n: [W[n][l].astype(BF16) for l in range(DEPTH)] for n in _BIG}
    mix = lambda l: [sh["w_in"][l], sh["cv_pw"][l], sh["w_out"][l]]
    ffn = lambda l: [sh["ffn_up"][l], sh["ffn_down"][l]]
    first = all_gather(mix(0) + [meta, cv_dw_w, ffn_dw_w], "gather_first")
    g_meta, g_cdw, g_fdw = first[3:6]

    def landing(shards):
        return [lax.dynamic_update_slice(lax.empty((NDEV,) + s.shape, s.dtype), s[None], (me,) + (0,) * s.ndim)
                for s in shards]

    gathers = {("ffn", 0): gather2_start("gather_ffn0_start", ffn(0), landing(ffn(0)), after=[first[0]])}
    gathers["mix", 1] = gather2_start("gather_mix1_start", mix(1), landing(mix(1)), after=[gathers["ffn", 0][4]])
    gathers["ffn", 1] = gather2_start("gather_ffn1_start", ffn(1), landing(ffn(1)), after=[gathers["mix", 1][4]])

    def mixer_w(l, h):
        g_in, g_pw, g_out = first[0:3] if l == 0 else gather2_wait(f"gather_mix{l}_wait", gathers["mix", l], after=[h])[1]
        return g_in.reshape(IN_W, D), g_pw.reshape(CV, CV), g_out.reshape(D, D)

    def mixer_mid(l, o):
        gathers["ffn", l] = gather2_forward(f"gather_ffn{l}_forward", gathers["ffn", l], after=[o])

    def ffn_w(l, h1):
        if l + 1 < DEPTH:
            gathers["mix", l + 1] = gather2_forward(f"gather_mix{l + 1}_forward", gathers["mix", l + 1], after=[h1])
        return gather2_wait(f"gather_ffn{l}_wait", gathers["ffn", l], after=[h1])[1]

    scatters = {}

    def ffn_grads(l, dw_down, dw_up):
        scatters["ffn", l] = scatter_start(f"scatter_ffn{l}_start", [dw_up, dw_down.reshape(NDEV, DNR, D)])
        return scatters["ffn", l][4]

    def mixer_grads(l, dw_out, dpw, dw_in):
        grads = [dw_in.reshape(NDEV, IN_W // NDEV, D), dpw.astype(BF16).reshape(NDEV, CV // NDEV, CV),
                 dw_out.reshape(NDEV, D // NDEV, D)]
        scatters["mix", l] = scatter_start(f"scatter_mix{l}_start", grads)
        return scatters["mix", l][4]

    P = dict(
        meta=g_meta.transpose(1, 0, 2).reshape(N_META, D),
        cv_dw_w=g_cdw.transpose(1, 2, 0, 3).reshape(DEPTH, CONV_K, CV),
        ffn_dw_w=g_fdw.transpose(1, 0, 2, 3),
        ffn_dw_b=ffn_dw_b.reshape(DEPTH, NDEV, 1, UPW),
        attn_sinks=attn_sinks,
    )
    for n in ("norm_mix_g", "q_norm_g", "k_norm_g", "attn_out_g", "cv_dw_b", "cv_ln_g", "cv_ln_b", "cv_out_g",
              "ret_gn_g", "norm_ffn_g"):
        P[n] = W[n].reshape(DEPTH, 1, -1)

    hooks = dict(mixer_w=mixer_w, mixer_mid=mixer_mid, ffn_w=ffn_w, ffn_grads=ffn_grads, mixer_grads=mixer_grads,
                 first_after=gathers["ffn", 1][4])
    loss_part, dx, dmeta, G = local_step(x[0], loss_target[0], P, hooks)

    small_full = {
        "meta": dmeta,
        "cv_dw_w": jnp.stack(G["cv_dw_w"]),
        "ffn_dw_w": jnp.stack([g.transpose(1, 0, 2).reshape(FFN_K, 2 * D_FF) for g in G["ffn_dw_w"]]),
        "ffn_dw_b": jnp.stack([g.reshape(2 * D_FF) for g in G["ffn_dw_b"]]),
        "attn_sinks": jnp.stack(G["attn_sinks"]),
    }
    for n in _SMALL:
        if n not in small_full:
            small_full[n] = jnp.stack([g.reshape(-1) for g in G[n]])
    shapes = [small_full[n].shape for n in _SMALL] + [(1,)]
    packed = _pack([small_full[n] for n in _SMALL] + [loss_part.reshape(1)])
    small_exchange = gather_start("reduce_small_start", [packed], landing([packed]), after=[dx])

    out = {}
    tiles = {"w_in": 144, "cv_pw": 32, "w_out": 128, "ffn_up": 176, "ffn_down": 176}
    me1 = me.astype(jnp.int32).reshape(1)
    own, lands = {n: [None] * DEPTH for n in _BIG}, {n: [None] * DEPTH for n in _BIG}

    def arrived(kind, names, after):
        for l in range(DEPTH):
            srcs, got = scatter_wait(f"scatter_{kind}{l}_wait", scatters[kind, l], after=after)
            for n, s_, g_ in zip(names, srcs, got):
                own[n][l], lands[n][l] = s_, g_

    def update(names):
        for n in names:
            shard = W[n].shape
            rows, cols = shard[0] * shard[1], shard[2]
            res = adamw_big(own[n], lands[n], W[n].reshape(rows, cols), M[n].reshape(rows, cols),
                            V[n].reshape(rows, cols), me1, tiles[n], f"adamw_{n}")
            out[n] = [r.reshape(shard) for r in res]

    arrived("ffn", ("ffn_up", "ffn_down"), [small_exchange[4]])
    update(("ffn_up", "ffn_down"))
    slots = gather_wait("reduce_small_wait", small_exchange, after=[out["ffn_down"][0]])[1][0]
    summed = _unpack(sum_slots(slots, "reduce_small_sum"), shapes)
    loss = summed[-1][0]
    small_g = []
    for n, g in zip(_SMALL, summed):
        if n in _SMALL_SHARDED:
            width = _SMALL_SHARDED[n] // NDEV
            g = lax.dynamic_slice_in_dim(g, me * width, width, axis=g.ndim - 1)
        small_g.append(g)
    flat2 = lambda a: a.reshape(-1, a.shape[-1])
    res = adamw_many([flat2(g) for g in small_g], *[[flat2(X[n]) for n in _SMALL] for X in (W, M, V)], "adamw_small")
    for n, r in zip(_SMALL, res):
        out[n] = [a.reshape(W[n].shape) for a in r]
    arrived("mix", ("w_in", "cv_pw", "w_out"), [out["ffn_down"][0], res[0][0]])
    update(("w_in", "cv_pw", "w_out"))
    for n in _TRANSPOSED:
        out[n] = [r.transpose(0, 2, 1) for r in out[n]]

    return (loss, dx[None], *[out[n][0] for n in _ORDER], *[out[n][1] for n in _ORDER],
            *[out[n][2] for n in _ORDER], *[out[n][3] for n in _ORDER])
```

```python
import numpy as np
import jax
import jax.numpy as jnp
from jax import lax
from jax.experimental import pallas as pl
from jax.experimental.pallas import tpu as pltpu

F32 = jnp.float32
BF16 = jnp.bfloat16

D = 1024
SEQ = 2048
DEPTH = 2
T = 128
L = SEQ + T
NB = L // T
N_META = 16
PAD = T - N_META
HD = 64
NQ = 8
NKV = 2
GQA = NQ // NKV
CV = 256
CONV_K = 31
RH = 4
D_FF = 2816
FFN_K = 3
IN_W = 2304
RMS_EPS = 1e-6
LN_EPS = 1e-5
NEG = -1e30
NDEV = 8
UPW = 2 * D_FF // NDEV
DNR = D_FF // NDEV
NPAIR = NDEV // 2

ADAM_LR, ADAM_B1, ADAM_B2, ADAM_EPS, ADAM_WD, ADAM_STEP = 0.001, 0.9, 0.999, 1e-08, 0.01, 10

VMEM_BIG = 56 * 1024 * 1024

MESH = pl.DeviceIdType.MESH


def _params(vmem=None):
    return pltpu.CompilerParams(vmem_limit_bytes=vmem) if vmem else None


def _dot(a, b, nt=False):
    return lax.dot_general(a, b, (((1,), (1 if nt else 0,)), ((), ())), preferred_element_type=F32)


def _sig(x):
    return 1.0 / (1.0 + jnp.exp(-x))


def _bf(x):
    return x.astype(BF16)


def _stack_rows(rows):
    idx = lax.broadcasted_iota(jnp.int32, (len(rows), rows[0].shape[1]), 0)
    out = jnp.zeros((len(rows), rows[0].shape[1]), F32)
    for r, v in enumerate(rows):
        out = jnp.where(idx == r, v, out)
    return out


def rms_bwd(x, g, dy, dres, name, after=None, last=False):
    n, w = x.shape
    has_res = dres is not None
    deps = [] if after is None else [after]

    def body(x_ref, g_ref, dy_ref, *rest):
        rest = rest[:len(rest) - 3 - len(deps)] + rest[len(rest) - 3:]
        if has_res:
            dres_ref, dx_ref, dxm_ref, dg_ref = rest
        else:
            dx_ref, dxm_ref, dg_ref = rest
        i = pl.program_id(0)
        xv = x_ref[...]
        r = lax.rsqrt(jnp.mean(xv * xv, axis=-1, keepdims=True) + RMS_EPS)
        xh = xv * r
        dyv = dy_ref[...]
        dxh = dyv * g_ref[...]
        dx = r * (dxh - xh * jnp.mean(dxh * xh, axis=-1, keepdims=True))
        if has_res:
            dx = dx + dres_ref[...]
        dx_ref[...] = dx
        if last:
            @pl.when(i == 0)
            def _():
                dxm_ref[...] = dx
        else:
            rows = i * T + lax.broadcasted_iota(jnp.int32, (T, 1), 0)
            dxm_ref[...] = jnp.where(rows >= PAD, dx, 0.0).astype(BF16)
        part = jnp.sum(dyv * xh, axis=0, keepdims=True)

        @pl.when(i == 0)
        def _():
            dg_ref[...] = part

        @pl.when(i > 0)
        def _():
            dg_ref[...] += part

    row = pl.BlockSpec((T, w), lambda i: (i, 0))
    vec = pl.BlockSpec((1, w), lambda i: (0, 0))
    ins = [x, g, dy] + ([dres] if has_res else []) + deps
    if last:
        out_shape = (jax.ShapeDtypeStruct((n - T, w), F32), jax.ShapeDtypeStruct((T, w), F32))
        out_specs = (pl.BlockSpec((T, w), lambda i: (jnp.maximum(i - 1, 0), 0)), pl.BlockSpec((T, w), lambda i: (0, 0)))
    else:
        out_shape = (jax.ShapeDtypeStruct((n, w), F32), jax.ShapeDtypeStruct((n, w), BF16))
        out_specs = (row, row)
    return pl.pallas_call(
        body, name=name, out_shape=(*out_shape, jax.ShapeDtypeStruct((1, w), F32)), grid=(n // T,),
        in_specs=[row, vec, row] + ([row] if has_res else [])
        + [_HBM] * len(deps),
        out_specs=(*out_specs, vec))(*ins)


def _mm(name, a, b, *, grid, a_spec, b_spec, o_spec, out_shape, nt=False, ta=False, red=False, res=None,
        res_spec=None):
    def body(a_ref, b_ref, *rest):
        o_ref = rest[-1]
        av = a_ref[...]
        bv = b_ref[...]
        if bv.ndim == 3:
            bv = bv.reshape(bv.shape[0] * bv.shape[1], bv.shape[2])
        if ta:
            acc = lax.dot_general(av, bv, (((0,), (0,)), ((), ())), preferred_element_type=F32)
        else:
            acc = _dot(av, bv, nt)
        if red:
            k = pl.program_id(0)

            @pl.when(k == 0)
            def _():
                o_ref[...] = acc

            @pl.when(k > 0)
            def _():
                o_ref[...] += acc
        else:
            if res is not None:
                rows = lax.broadcasted_iota(jnp.int32, (acc.shape[0], 1), 0)
                acc = rest[0][...] + jnp.where(rows >= PAD, acc, 0.0)
            o_ref[...] = acc.astype(o_ref.dtype)

    ins = [a, b] + ([res] if res is not None else [])
    specs = [a_spec, b_spec] + ([res_spec] if res is not None else [])
    return pl.pallas_call(body, name=name, out_shape=out_shape, grid=grid, in_specs=specs, out_specs=o_spec,
                          compiler_params=_params(VMEM_BIG))(*ins)


def mm_nn(name, a, b, tn, out_dtype=F32, res=None):
    m, k = a.shape
    n = b.shape[1]
    return _mm(name, a, b, grid=(n // tn,),
               a_spec=pl.BlockSpec((m, k), lambda j: (0, 0)), b_spec=pl.BlockSpec((k, tn), lambda j: (0, j)),
               o_spec=pl.BlockSpec((m, tn), lambda j: (0, j)), out_shape=jax.ShapeDtypeStruct((m, n), out_dtype),
               res=res, res_spec=pl.BlockSpec((m, tn), lambda j: (0, j)))


def mm_nt(name, a, b, tn):
    m, k = a.shape
    n = b.shape[0]
    return _mm(name, a, b, grid=(n // tn,), nt=True,
               a_spec=pl.BlockSpec((m, k), lambda j: (0, 0)), b_spec=pl.BlockSpec((tn, k), lambda j: (j, 0)),
               o_spec=pl.BlockSpec((m, tn), lambda j: (0, j)), out_shape=jax.ShapeDtypeStruct((m, n), F32))


def mm_ta(name, a, b, tm, out_dtype=BF16):
    k, m = a.shape
    n = b.shape[1]
    return _mm(name, a, b, grid=(m // tm,), ta=True,
               a_spec=pl.BlockSpec((k, tm), lambda j: (0, j)), b_spec=pl.BlockSpec((k, n), lambda j: (0, 0)),
               o_spec=pl.BlockSpec((tm, n), lambda j: (j, 0)), out_shape=jax.ShapeDtypeStruct((m, n), out_dtype))


def _rms_mm(name, h, g, b, *, grid, b_spec, o_spec, out_shape, after=None):
    deps = [] if after is None else [after]

    def body(h_ref, g_ref, b_ref, *rest):
        o_ref, u_ref = rest[len(deps):]

        @pl.when(pl.program_id(0) == 0)
        def _():
            gv = g_ref[...]
            for c in range(NB):
                rows = slice(c * T, (c + 1) * T)
                xv = h_ref[rows, :]
                r = lax.rsqrt(jnp.mean(xv * xv, axis=-1, keepdims=True) + RMS_EPS)
                u_ref[rows, :] = (xv * r * gv).astype(BF16)

        o_ref[...] = _dot(u_ref[...], b_ref[...], nt=True)

    whole = lambda r: pl.BlockSpec((r, D), lambda j: (0, 0))
    return pl.pallas_call(
        body, name=name, out_shape=(out_shape, jax.ShapeDtypeStruct((L, D), BF16)), grid=grid,
        in_specs=[whole(L), whole(1), b_spec] + [_HBM] * len(deps), out_specs=(o_spec, whole(L)),
        compiler_params=_params(VMEM_BIG))(h, g, b, *deps)


def out_proj_fwd(name, o, og, y_cv, y_ret, w, res, tn):
    wa = o.shape[1]

    def body(o_ref, og_ref, cv_ref, ret_ref, w_ref, r_ref, out_ref, ycat_ref):
        @pl.when(pl.program_id(0) == 0)
        def _():
            gv = og_ref[...]
            for c in range(NB):
                rows = slice(c * T, (c + 1) * T)
                ycat_ref[rows, 0:wa] = _rms_rows(o_ref[rows, :], gv)[0].astype(BF16)
                ycat_ref[rows, wa:wa + CV] = cv_ref[rows, :]
                ycat_ref[rows, wa + CV:D] = ret_ref[rows, :]

        acc = _dot(ycat_ref[...], w_ref[...])
        rows = lax.broadcasted_iota(jnp.int32, (L, 1), 0)
        out_ref[...] = r_ref[...] + jnp.where(rows >= PAD, acc, 0.0)

    whole = lambda w_: pl.BlockSpec((L, w_), lambda j: (0, 0))
    tile = pl.BlockSpec((L, tn), lambda j: (0, j))
    return pl.pallas_call(
        body, name=name, out_shape=(jax.ShapeDtypeStruct((L, D), F32), jax.ShapeDtypeStruct((L, D), BF16)),
        grid=(D // tn,),
        in_specs=[whole(wa), pl.BlockSpec((1, wa), lambda j: (0, 0)), whole(CV), whole(RW),
                  pl.BlockSpec((D, tn), lambda j: (0, j)), tile],
        out_specs=(tile, whole(D)), compiler_params=_params(VMEM_BIG))(o, og, y_cv, y_ret, w, res)


def in_proj_fwd(name, h, g, wt, tn, after=None):
    return _rms_mm(name, h, g, wt, grid=(IN_W // tn,), b_spec=pl.BlockSpec((tn, D), lambda j: (j, 0)),
                   o_spec=pl.BlockSpec((L, tn), lambda j: (0, j)), out_shape=jax.ShapeDtypeStruct((L, IN_W), F32),
                   after=after)


def ffn_up_fwd(name, h, g, wupt):
    return _rms_mm(name, h, g, wupt, grid=(NDEV,), b_spec=pl.BlockSpec((None, UPW, D), lambda j: (j, 0, 0)),
                   o_spec=pl.BlockSpec((None, L, UPW), lambda j: (j, 0, 0)),
                   out_shape=jax.ShapeDtypeStruct((NDEV, L, UPW), F32))


def _slab_specs():
    gate = pl.BlockSpec((None, L, UPW), lambda j: (jnp.minimum(j, NPAIR - 1), 0, 0))
    up = pl.BlockSpec((None, L, UPW), lambda j: (jnp.maximum(j - NPAIR, 0), 0, 0))
    return gate, up


def ffn_up_dx(name, dfg, dfu, wupt):
    def body(g_ref, u_ref, b_ref, o_ref):
        j = pl.program_id(0)

        @pl.when(j == 0)
        def _():
            o_ref[...] = _dot(g_ref[...], b_ref[...])

        @pl.when(jnp.logical_and(j > 0, j < NPAIR))
        def _():
            o_ref[...] += _dot(g_ref[...], b_ref[...])

        @pl.when(j >= NPAIR)
        def _():
            o_ref[...] += _dot(u_ref[...], b_ref[...])

    gate, up = _slab_specs()
    return pl.pallas_call(
        body, name=name, out_shape=jax.ShapeDtypeStruct((L, D), F32), grid=(NDEV,),
        in_specs=[gate, up, pl.BlockSpec((None, UPW, D), lambda j: (j, 0, 0))],
        out_specs=pl.BlockSpec((L, D), lambda j: (0, 0)), compiler_params=_params(VMEM_BIG))(dfg, dfu, wupt)


def ffn_up_dw(name, u2, dfg, dfu):
    tdot = lambda a, b: lax.dot_general(a, b, (((0,), (0,)), ((), ())), preferred_element_type=F32)

    def body(a_ref, g_ref, u_ref, o_ref):
        j = pl.program_id(0)

        @pl.when(j < NPAIR)
        def _():
            o_ref[...] = tdot(g_ref[...], a_ref[...]).astype(o_ref.dtype)

        @pl.when(j >= NPAIR)
        def _():
            o_ref[...] = tdot(u_ref[...], a_ref[...]).astype(o_ref.dtype)

    gate, up = _slab_specs()
    return pl.pallas_call(
        body, name=name, out_shape=jax.ShapeDtypeStruct((NDEV, UPW, D), BF16), grid=(NDEV,),
        in_specs=[pl.BlockSpec((L, D), lambda j: (0, 0)), gate, up],
        out_specs=pl.BlockSpec((None, UPW, D), lambda j: (j, 0, 0)), compiler_params=_params(VMEM_BIG))(u2, dfg, dfu)


def _ffn_down_tile(a_ref, b_ref, r_ref, tn):
    acc = jnp.zeros((L, tn), F32)
    for g in range(NPAIR):
        bv = b_ref[2 * g:2 * g + 2]
        acc = acc + _dot(a_ref[g], bv.reshape(2 * DNR, tn))
    rows = lax.broadcasted_iota(jnp.int32, (L, 1), 0)
    return r_ref[...] + jnp.where(rows >= PAD, acc, 0.0)


def _ffn_down_specs(tn):
    return [pl.BlockSpec((NPAIR, L, UPW), lambda j: (0, 0, 0)), pl.BlockSpec((NDEV, DNR, tn), lambda j: (0, 0, j)),
            pl.BlockSpec((L, tn), lambda j: (0, j))]


def ffn_down_fwd(name, a, wdn, res, tn=256):
    def body(a_ref, b_ref, r_ref, o_ref):
        o_ref[...] = _ffn_down_tile(a_ref, b_ref, r_ref, tn)

    return pl.pallas_call(
        body, name=name, out_shape=jax.ShapeDtypeStruct((L, D), F32), grid=(D // tn,),
        in_specs=_ffn_down_specs(tn), out_specs=pl.BlockSpec((L, tn), lambda j: (0, j)),
        compiler_params=_params(VMEM_BIG))(a, wdn, res)


def ffn_down_loss(name, a, wdn, res, target, tn=256):
    def body(a_ref, b_ref, r_ref, t_ref, dh_ref, dhm_ref, loss_ref):
        j = pl.program_id(0)
        hv = _ffn_down_tile(a_ref, b_ref, r_ref, tn)
        e = hv[T:] - t_ref[...]
        dh = e * (1.0 / D)
        dh_ref[0:T, :] = jnp.zeros((T, tn), F32)
        dh_ref[T:, :] = dh
        dhm_ref[0:T, :] = jnp.zeros((T, tn), BF16)
        dhm_ref[T:, :] = dh.astype(BF16)
        part = jnp.sum(jnp.sum(e * e, axis=1, keepdims=True), axis=0, keepdims=True) * (0.5 / D)

        @pl.when(j == 0)
        def _():
            loss_ref[...] = jnp.broadcast_to(part, loss_ref.shape)

        @pl.when(j > 0)
        def _():
            loss_ref[...] += jnp.broadcast_to(part, loss_ref.shape)

    tile = pl.BlockSpec((L, tn), lambda j: (0, j))
    return pl.pallas_call(
        body, name=name,
        out_shape=(jax.ShapeDtypeStruct((L, D), F32), jax.ShapeDtypeStruct((L, D), BF16),
                   jax.ShapeDtypeStruct((8, 128), F32)),
        grid=(D // tn,),
        in_specs=_ffn_down_specs(tn) + [pl.BlockSpec((SEQ, tn), lambda j: (0, j))],
        out_specs=(tile, tile, pl.BlockSpec((8, 128), lambda j: (0, 0))),
        compiler_params=_params(VMEM_BIG))(a, wdn, res, target)


def ffn_down_dx(name, dh, wdn):
    return _mm(name, dh, wdn, grid=(NPAIR,), nt=True,
               a_spec=pl.BlockSpec((L, D), lambda g: (0, 0)),
               b_spec=pl.BlockSpec((2, DNR, D), lambda g: (g, 0, 0)),
               o_spec=pl.BlockSpec((None, L, UPW), lambda g: (g, 0, 0)),
               out_shape=jax.ShapeDtypeStruct((NPAIR, L, UPW), F32))


def ffn_down_dw(name, a, dh):
    return _mm(name, a, dh, grid=(NPAIR,), ta=True,
               a_spec=pl.BlockSpec((None, L, UPW), lambda g: (g, 0, 0)),
               b_spec=pl.BlockSpec((L, D), lambda g: (0, 0)),
               o_spec=pl.BlockSpec((UPW, D), lambda g: (g, 0)),
               out_shape=jax.ShapeDtypeStruct((D_FF, D), BF16))


_SLOPES = [2.0 ** (-8.0 * (h + 1) / NQ) for h in range(NQ)]
_SCALE = HD ** -0.5


QR = GQA * T
KC = 3 * T


def _attn_geometry(g, n, sink_ref):
    row = lax.broadcasted_iota(jnp.int32, (QR, KC), 0)
    col = lax.broadcasted_iota(jnp.int32, (QR, KC), 1)
    i = row & (T - 1)
    j = col & (T - 1)
    blk = col // T
    d_meta = n * T + i - j
    ok_meta = (j >= PAD) & (d_meta >= 0)
    ok_prev = j > i + jnp.where(n >= 2, 0, T)
    ok_cur = j <= i - jnp.where(n >= 1, 0, T)
    ok = ((blk == 0) & ok_meta) | ((blk == 1) & ok_prev) | ((blk == 2) & ok_cur)
    dist = jnp.where(blk == 0, jnp.minimum(d_meta, T), jnp.where(blk == 1, T + i - j, i - j)).astype(F32)
    head = lax.broadcasted_iota(jnp.int32, (QR, 1), 0) // T
    slope = jnp.zeros((QR, 1), F32)
    sink = jnp.zeros((QR, 1), F32)
    for hh in range(GQA):
        slope = jnp.where(head == hh, jnp.where(g == 0, _SLOPES[hh], _SLOPES[GQA + hh]), slope)
        sink = jnp.where(head == hh, sink_ref[g * GQA + hh], sink)
    return ok, slope * dist, sink


def _attn_probs(qn, keys, geo):
    ok, penalty, sink = geo
    s = jnp.where(ok, _dot(qn, keys, nt=True) * _SCALE - penalty, NEG)
    m = jnp.maximum(jnp.max(s, axis=-1, keepdims=True), sink)
    e = jnp.exp(s - m)
    e_sink = jnp.exp(sink - m)
    inv = 1.0 / (e.sum(axis=-1, keepdims=True) + e_sink)
    return e, inv, e_sink * inv


def _rms_rows(x, g):
    r = lax.rsqrt(jnp.mean(x * x, axis=-1, keepdims=True) + RMS_EPS)
    xh = x * r
    return xh * g, xh, r


def _rms_rows_bwd(dy, g, xh, r):
    dxh = dy * g
    return r * (dxh - xh * jnp.mean(dxh * xh, axis=-1, keepdims=True))


def _rows3(m_ref, p_ref, c_ref):
    return jnp.concatenate([m_ref[...], p_ref[...], c_ref[...]], axis=0)


_ATT_K, _ATT_V = 4, 5


def _nat_specs():
    qspec = pl.BlockSpec((T, NQ * HD), lambda n: (n, 0))
    kv = lambda col: (pl.BlockSpec((T, NKV * HD), lambda n: (n, col)),
                      pl.BlockSpec((T, NKV * HD), lambda n: (jnp.maximum(n - 1, 0), col)),
                      pl.BlockSpec((T, NKV * HD), lambda n: (0, col)))
    return qspec, kv(_ATT_K), kv(_ATT_V)


def _group(g, x, w=HD):
    return x[:, g * w:(g + 1) * w]


def _stack_heads(x):
    return jnp.concatenate([x[:, hh * HD:(hh + 1) * HD] for hh in range(GQA)], axis=0)


def _unstack_heads(x):
    return jnp.concatenate([x[hh * T:(hh + 1) * T] for hh in range(GQA)], axis=1)


def attn_fwd(proj, qg, kg, sinks, name):
    def body(sink_ref, q_ref, kc_ref, kp_ref, km_ref, vc_ref, vp_ref, vm_ref, qg_ref, kg_ref, o_ref):
        n = pl.program_id(0)
        k_all = _rows3(km_ref, kp_ref, kc_ref)
        v_all = _rows3(vm_ref, vp_ref, vc_ref)
        q_all = q_ref[...]
        outs = []
        for g in range(NKV):
            geo = _attn_geometry(g, n, sink_ref)
            keys = _bf(_rms_rows(_group(g, k_all), kg_ref[...])[0])
            vals = _bf(_group(g, v_all))
            qn = _bf(_rms_rows(_stack_heads(_group(g, q_all, GQA * HD)), qg_ref[...])[0])
            e, inv, _ = _attn_probs(qn, keys, geo)
            outs.append(_unstack_heads(_dot(_bf(e), vals) * inv))
        o_ref[...] = jnp.concatenate(outs, axis=1)

    qspec, (kc, kp, km), (vc, vp, vm) = _nat_specs()
    vec = pl.BlockSpec((1, HD), lambda n: (0, 0))
    return pl.pallas_call(
        body, name=name, out_shape=jax.ShapeDtypeStruct((L, NQ * HD), F32), grid=(NB,),
        in_specs=[pl.BlockSpec(memory_space=pltpu.SMEM), qspec, kc, kp, km, vc, vp, vm, vec, vec],
        out_specs=qspec)(sinks, proj, proj, proj, proj, proj, proj, proj, qg, kg)


def attn_bwd(proj, o, dycat, og, qg, kg, sinks, name):
    def body(sink_ref, q_ref, kc_ref, kp_ref, km_ref, vc_ref, vp_ref, vm_ref, o_ref, dy_ref, og_ref, qg_ref, kg_ref,
             dq_ref, dkc_ref, dkp_ref, dkm_ref, dvc_ref, dvp_ref, dvm_ref, dqg_ref, dkg_ref, dsk_ref, dog_ref):
        n = pl.program_id(0)

        @pl.when(n == 0)
        def _():
            for r in (dkm_ref, dvm_ref, dqg_ref, dkg_ref, dsk_ref, dog_ref):
                r[...] = jnp.zeros_like(r)

        ogv = og_ref[...]
        dyv = dy_ref[...]
        _, oh, orr = _rms_rows(o_ref[...], ogv)
        do_all = _rms_rows_bwd(dyv, ogv, oh, orr)
        dog_ref[...] += jnp.sum(dyv * oh, axis=0, keepdims=True)

        kgv = kg_ref[...]
        qgv = qg_ref[...]
        k_all = _rows3(km_ref, kp_ref, kc_ref)
        v_all = _rows3(vm_ref, vp_ref, vc_ref)
        q_all = q_ref[...]
        dq, dk, dv, dsk_rows = [], [], [], []
        dqg_acc = jnp.zeros((1, HD), F32)
        dkg_acc = jnp.zeros((1, HD), F32)
        for g in range(NKV):
            geo = _attn_geometry(g, n, sink_ref)
            kn_f, kh, kr = _rms_rows(_group(g, k_all), kgv)
            keys = _bf(kn_f)
            vals = _bf(_group(g, v_all))
            qn_f, qh, qr = _rms_rows(_stack_heads(_group(g, q_all, GQA * HD)), qgv)
            qn = _bf(qn_f)
            e, inv, p_sink = _attn_probs(qn, keys, geo)
            dob = _bf(_stack_heads(_group(g, do_all, GQA * HD)))
            p = e * inv
            dp = _dot(dob, vals, nt=True)
            delta = (p * dp).sum(axis=-1, keepdims=True)
            for hh in range(GQA):
                part = -jnp.sum((p_sink * delta)[hh * T:(hh + 1) * T], axis=0, keepdims=True)
                dsk_rows.append(jnp.broadcast_to(part, (1, 128)))
            ds = p * (dp - delta)
            dqn = _dot(_bf(ds), keys) * _SCALE
            dq.append(_unstack_heads(_rms_rows_bwd(dqn, qgv, qh, qr)))
            dqg_acc = dqg_acc + jnp.sum(dqn * qh, axis=0, keepdims=True)
            dkn = _dot(_bf(ds.T), qn) * _SCALE
            dkg_acc = dkg_acc + jnp.sum(dkn * kh, axis=0, keepdims=True)
            dk.append(_rms_rows_bwd(dkn, kgv, kh, kr))
            dv.append(_dot(_bf(p.T), dob))
        dq_ref[...] = jnp.concatenate(dq, axis=1).astype(dq_ref.dtype)
        dk_all = jnp.concatenate(dk, axis=1)
        dv_all = jnp.concatenate(dv, axis=1)
        dkm_ref[...] += dk_all[0:T]
        dkp_ref[...] = dk_all[T:2 * T]
        dkc_ref[...] = dk_all[2 * T:3 * T]
        dvm_ref[...] += dv_all[0:T]
        dvp_ref[...] = dv_all[T:2 * T]
        dvc_ref[...] = dv_all[2 * T:3 * T]
        dsk_ref[...] += _stack_rows(dsk_rows)
        dqg_ref[...] += dqg_acc
        dkg_ref[...] += dkg_acc

    qspec, (kc, kp, km), (vc, vp, vm) = _nat_specs()
    vec = pl.BlockSpec((1, HD), lambda n: (0, 0))
    cur = pl.BlockSpec((T, NKV * HD), lambda n: (n, 0))
    meta = pl.BlockSpec((T, NKV * HD), lambda n: (0, 0))
    kv_shape = jax.ShapeDtypeStruct((L, NKV * HD), F32)
    meta_shape = jax.ShapeDtypeStruct((T, NKV * HD), F32)
    vec_shape = jax.ShapeDtypeStruct((1, HD), F32)
    ogvec = pl.BlockSpec((1, NQ * HD), lambda n: (0, 0))
    return pl.pallas_call(
        body, name=name,
        out_shape=(jax.ShapeDtypeStruct((L, NQ * HD), BF16), kv_shape, kv_shape, meta_shape, kv_shape, kv_shape,
                   meta_shape, vec_shape, vec_shape, jax.ShapeDtypeStruct((NQ, 128), F32),
                   jax.ShapeDtypeStruct((1, NQ * HD), F32)),
        grid=(NB,),
        in_specs=[pl.BlockSpec(memory_space=pltpu.SMEM), qspec, kc, kp, km, vc, vp, vm, qspec, qspec, ogvec, vec, vec],
        out_specs=(qspec, cur, cur, meta, cur, cur, meta, vec, vec,
                   pl.BlockSpec((NQ, 128), lambda n: (0, 0)), ogvec))(
            sinks, proj, proj, proj, proj, proj, proj, proj, o, dycat, og, qg, kg)


RW = RH * HD


def _ret_tables():
    h = np.arange(RH, dtype=np.float64)
    lg = np.log1p(-np.exp2(-5.0 - h))
    idx = np.arange(T, dtype=np.float64)
    diff = idx[:, None] - idx[None, :]
    decay = np.where(diff[None] >= 0, np.exp(np.maximum(diff, 0.0)[None] * lg[:, None, None]), 0.0)
    zeta = np.exp((T - 1 - idx)[None, :] * lg[:, None])
    xi = np.exp((idx + 1.0)[None, :] * lg[:, None])
    cd = np.exp(T * lg)
    lanes = lambda a: np.repeat(a.T, HD, axis=1)
    head_of = np.arange(RW) // HD
    same = (head_of[:, None] == head_of[None, :]).astype(np.float64)
    f = lambda a: jnp.asarray(a, F32)
    return dict(decay=f(decay), zeta=f(lanes(zeta)), xi=f(lanes(xi)), cd=f(np.repeat(cd, HD)[None, :]),
                head=f((head_of[None, :] == np.arange(RH)[:, None]).astype(np.float64)[:, None, :]),
                same=f(same), avg=jnp.asarray(same / HD, BF16))


def _seg_mean(x, avg):
    hi = _bf(x)
    lo = _bf(x - hi.astype(F32))
    return _dot(hi, avg) + _dot(lo, avg)


def _ret_specs(col0, order):
    return lambda col: pl.BlockSpec((T, RW), lambda i: (order(i), col0 + col))


def _ret_chunk(q, kf, v, s, tb):
    dec, xi, head = tb
    vb = _bf(v)
    kb = _bf(kf)
    y = _dot(_bf(q * xi), _bf(s))
    a = []
    for h in range(RH):
        a.append(_dot(_bf(q * head[h]), kb, nt=True) * dec[h])
        y = y + head[h] * _dot(_bf(a[h]), vb)
    return a, y


def _gn_rows(y):
    mu = jnp.mean(y, axis=-1, keepdims=True)
    yc = y - mu
    rstd = lax.rsqrt(jnp.mean(yc * yc, axis=-1, keepdims=True) + LN_EPS)
    return yc * rstd, rstd


_RET_COL0 = _RET_Q = 5


def _ret_consts(tb):
    names = ("decay", "zeta", "xi", "cd", "head", "same", "avg")
    full = lambda a: pl.BlockSpec(a.shape, lambda i: (0,) * a.ndim)
    return [tb[n] for n in names], [full(tb[n]) for n in names]


def _ret_inputs(n, q_ref, k_ref, v_ref):
    rows = n * T + lax.broadcasted_iota(jnp.int32, (T, 1), 0)
    valid = rows >= PAD
    return q_ref[...], jnp.where(valid, k_ref[...] * (HD ** -0.5), 0.0), v_ref[...], valid


def ret_fwd(proj, gng, tb, name):
    def body(q_ref, k_ref, v_ref, g_ref, dec_ref, zeta_ref, xi_ref, cd_ref, head_ref, same_ref, avg_ref, gng_ref,
             y_ref, st_ref, s_scr):
        n = pl.program_id(0)

        @pl.when(n == 0)
        def _():
            s_scr[...] = jnp.zeros_like(s_scr)

        s = s_scr[...]
        st_ref[...] = s
        q, kf, v, _ = _ret_inputs(n, q_ref, k_ref, v_ref)
        _, y = _ret_chunk(q, kf, v, s, (dec_ref, xi_ref[...], head_ref))
        s_scr[...] = cd_ref[...] * s + same_ref[...] * _dot(_bf((kf * zeta_ref[...]).T), _bf(v))
        avg = avg_ref[...]
        yc = y - _seg_mean(y, avg)
        yh = yc * lax.rsqrt(_seg_mean(yc * yc, avg) + LN_EPS)
        gv = g_ref[...]
        y_ref[...] = (gv * _sig(gv) * (yh * gng_ref[...])).astype(y_ref.dtype)

    col = _ret_specs(_RET_COL0, lambda i: i)
    consts, cspecs = _ret_consts(tb)
    return pl.pallas_call(
        body, name=name,
        out_shape=(jax.ShapeDtypeStruct((L, RW), BF16), jax.ShapeDtypeStruct((NB, RW, RW), F32)),
        grid=(NB,),
        in_specs=[col(0), col(1), col(2), col(3)] + cspecs + [pl.BlockSpec((1, RW), lambda i: (0, 0))],
        out_specs=(pl.BlockSpec((T, RW), lambda i: (i, 0)), pl.BlockSpec((None, RW, RW), lambda i: (i, 0, 0))),
        scratch_shapes=[pltpu.VMEM((RW, RW), F32)])(proj, proj, proj, proj, *consts, gng)


def ret_bwd(proj, dycat, states, gng, tb, name):
    def body(q_ref, k_ref, v_ref, g_ref, do_ref, st_ref, dec_ref, zeta_ref, xi_ref, cd_ref, head_ref, same_ref,
             avg_ref, gng_ref, d_ref, dgn_ref, ds_scr):
        i = pl.program_id(0)
        n = NB - 1 - i

        @pl.when(i == 0)
        def _():
            ds_scr[...] = jnp.zeros_like(ds_scr)

        dsn = _bf(ds_scr[...])
        s = st_ref[...]
        sb = _bf(s)
        q, kf, v, valid = _ret_inputs(n, q_ref, k_ref, v_ref)
        xi, zeta, avg = xi_ref[...], zeta_ref[...], avg_ref[...]
        a, y = _ret_chunk(q, kf, v, s, (dec_ref, xi, head_ref))
        yc = y - _seg_mean(y, avg)
        rstd = lax.rsqrt(_seg_mean(yc * yc, avg) + LN_EPS)
        yh = yc * rstd
        gv = g_ref[...]
        sg = _sig(gv)
        sil = gv * sg
        gn = gng_ref[...]
        dout = do_ref[...]
        d_ref[:, 3 * RW:4 * RW] = (dout * (yh * gn) * (sg * (1.0 + gv * (1.0 - sg)))).astype(d_ref.dtype)
        dyh = dout * sil * gn
        part = jnp.sum(dout * sil * yh, axis=0, keepdims=True)

        @pl.when(i == 0)
        def _():
            dgn_ref[...] = part

        @pl.when(i > 0)
        def _():
            dgn_ref[...] += part

        dy = rstd * (dyh - _seg_mean(dyh, avg) - yh * _seg_mean(dyh * yh, avg))
        dyb = _bf(dy)
        vb, kb, qb = _bf(v), _bf(kf), _bf(q)
        dq = _dot(dyb, sb, nt=True) * xi
        dkf = _dot(vb, dsn, nt=True) * zeta
        dv = _dot(_bf(kf * zeta), dsn)
        for h in range(RH):
            m = head_ref[h]
            da = _dot(_bf(dy * m), vb, nt=True) * dec_ref[h]
            dv = dv + m * _dot(_bf(a[h].T), dyb)
            dq = dq + m * _dot(_bf(da), kb)
            dkf = dkf + m * _dot(_bf(da.T), qb)
        d_ref[:, 0:RW] = dq.astype(d_ref.dtype)
        d_ref[:, RW:2 * RW] = jnp.where(valid, dkf * (HD ** -0.5), 0.0).astype(d_ref.dtype)
        d_ref[:, 2 * RW:3 * RW] = dv.astype(d_ref.dtype)
        ds_scr[...] = cd_ref[...] * ds_scr[...] + same_ref[...] * _dot(_bf((q * xi).T), dyb)

    back = lambda i: NB - 1 - i
    col = _ret_specs(_RET_COL0, back)
    consts, cspecs = _ret_consts(tb)
    vec = pl.BlockSpec((1, RW), lambda i: (0, 0))
    return pl.pallas_call(
        body, name=name,
        out_shape=(jax.ShapeDtypeStruct((L, 4 * RW), BF16), jax.ShapeDtypeStruct((1, RW), F32)),
        grid=(NB,),
        in_specs=[col(0), col(1), col(2), col(3), _ret_specs(3, back)(0),
                  pl.BlockSpec((None, RW, RW), lambda i: (back(i), 0, 0))] + cspecs + [vec],
        out_specs=(pl.BlockSpec((T, 4 * RW), lambda i: (back(i), 0)), vec),
        scratch_shapes=[pltpu.VMEM((RW, RW), F32)])(proj, proj, proj, proj, dycat, states, *consts, gng)


HALO = 32
TAP0 = HALO - (CONV_K - 1)


def _conv_specs():
    cur = lambda col=0: pl.BlockSpec((T, CV), lambda c: (c, col))
    before = lambda col=0: pl.BlockSpec((HALO, CV), lambda c: (jnp.maximum(c * (T // HALO) - 1, 0), col))
    after = pl.BlockSpec((HALO, CV), lambda c: (jnp.minimum((c + 1) * (T // HALO), L // HALO - 1), 0))
    full = lambda r, w: pl.BlockSpec((r, w), lambda c: (0, 0))
    return cur, before, after, full


_CONV_A, _CONV_B = 3, 4
_DY_CONV = 2


def _conv_post(xc, lg_ref, lb_ref):
    xh, rstd = _gn_rows(xc)
    z = xh * lg_ref[...] + lb_ref[...]
    return xh, rstd, z, _sig(z)


def conv_fwd(proj, w, b, lg, lb, pw, og, name):
    def body(ca_ref, cah_ref, cb_ref, cbh_ref, w_ref, b_ref, lg_ref, lb_ref, pw_ref, og_ref, y_ref, xc_ref, u_scr):
        c = pl.program_id(0)
        u_scr[0:HALO, :] = jnp.where(c > 0, cah_ref[...] * _sig(cbh_ref[...]), 0.0)
        u_scr[HALO:HALO + T, :] = ca_ref[...] * _sig(cb_ref[...])
        acc = jnp.zeros((T, CV), F32)
        for k in range(CONV_K):
            acc = acc + w_ref[k:k + 1, :] * u_scr[TAP0 + k:TAP0 + k + T, :]
        xc = acc + b_ref[...]
        xc_ref[...] = xc
        _, _, z, sg = _conv_post(xc, lg_ref, lb_ref)
        zp = _dot(_bf(z * sg), pw_ref[...])
        y_ref[...] = _rms_rows(zp, og_ref[...])[0].astype(y_ref.dtype)

    cur, before, _, full = _conv_specs()
    vec = full(1, CV)
    seq = jax.ShapeDtypeStruct((L, CV), F32)
    return pl.pallas_call(
        body, name=name, out_shape=(jax.ShapeDtypeStruct((L, CV), BF16), seq), grid=(NB,),
        in_specs=[cur(_CONV_A), before(_CONV_A), cur(_CONV_B), before(_CONV_B), full(CONV_K, CV), vec, vec, vec,
                  full(CV, CV), vec],
        out_specs=(cur(), cur()),
        scratch_shapes=[pltpu.VMEM((HALO + T, CV), F32)])(proj, proj, proj, proj, w, b, lg, lb, pw, og)


def conv_bwd(proj, xc, dycat, w, lg, lb, pw, og, name):
    def body1(xc_ref, dy_ref, lg_ref, lb_ref, pw_ref, og_ref, dxc_ref, db_ref, dlg_ref, dlb_ref, dog_ref, dpw_ref):
        c = pl.program_id(0)

        @pl.when(c == 0)
        def _():
            for r in (db_ref, dlg_ref, dlb_ref, dog_ref, dpw_ref):
                r[...] = jnp.zeros_like(r)

        xh, rstd, z, sg = _conv_post(xc_ref[...], lg_ref, lb_ref)
        s = z * sg
        zp = _dot(_bf(s), pw_ref[...])
        ogv = og_ref[...]
        _, zph, r2 = _rms_rows(zp, ogv)
        dyv = dy_ref[...]
        dog_ref[...] += jnp.sum(dyv * zph, axis=0, keepdims=True)
        dzpb = _bf(_rms_rows_bwd(dyv, ogv, zph, r2))
        dpw_ref[...] += _dot(_bf(s.T), dzpb)
        dz = _dot(dzpb, pw_ref[...], nt=True) * (sg * (1.0 + z * (1.0 - sg)))
        dlg_ref[...] += jnp.sum(dz * xh, axis=0, keepdims=True)
        dlb_ref[...] += jnp.sum(dz, axis=0, keepdims=True)
        dxh = dz * lg_ref[...]
        dxc = rstd * (dxh - jnp.mean(dxh, axis=-1, keepdims=True) - xh * jnp.mean(dxh * xh, axis=-1, keepdims=True))
        db_ref[...] += jnp.sum(dxc, axis=0, keepdims=True)
        dxc_ref[...] = dxc

    def body2(dx_ref, dxa_ref, ca_ref, cb_ref, w_ref, dca_ref, dcb_ref, dw_ref, d_scr):
        c = pl.program_id(0)

        @pl.when(c == 0)
        def _():
            dw_ref[...] = jnp.zeros_like(dw_ref)

        d_scr[0:T, :] = dx_ref[...]
        d_scr[T:T + HALO, :] = jnp.where(c < NB - 1, dxa_ref[...], 0.0)
        ca = ca_ref[...]
        sg = _sig(cb_ref[...])
        u = ca * sg
        du = jnp.zeros((T, CV), F32)
        for k in range(CONV_K):
            off = CONV_K - 1 - k
            dj = d_scr[off:off + T, :]
            du = du + w_ref[k:k + 1, :] * dj
            dw_ref[k:k + 1, :] += jnp.sum(u * dj, axis=0, keepdims=True)
        dca_ref[...] = (du * sg).astype(dca_ref.dtype)
        dcb_ref[...] = (du * ca * sg * (1.0 - sg)).astype(dcb_ref.dtype)

    cur, _, after, full = _conv_specs()
    seq = jax.ShapeDtypeStruct((L, CV), F32)
    vsh = jax.ShapeDtypeStruct((1, CV), F32)
    vec = full(1, CV)
    dxc, db, dlg, dlb, dog, dpw = pl.pallas_call(
        body1, name=name + "_a",
        out_shape=(seq, vsh, vsh, vsh, vsh, jax.ShapeDtypeStruct((CV, CV), F32)),
        grid=(NB,),
        in_specs=[cur(), cur(_DY_CONV), vec, vec, full(CV, CV), vec],
        out_specs=(cur(), vec, vec, vec, vec, full(CV, CV)))(xc, dycat, lg, lb, pw, og)
    dca, dcb, dw = pl.pallas_call(
        body2, name=name + "_b", grid=(NB,),
        out_shape=(jax.ShapeDtypeStruct((L, CV), BF16), jax.ShapeDtypeStruct((L, CV), BF16),
                   jax.ShapeDtypeStruct((CONV_K, CV), F32)),
        in_specs=[cur(), after, cur(_CONV_A), cur(_CONV_B), full(CONV_K, CV)],
        out_specs=(cur(), cur(), full(CONV_K, CV)),
        scratch_shapes=[pltpu.VMEM((T + HALO, CV), F32)])(dxc, dxc, proj, proj, w)
    return dca, dcb, dw, db, dlg, dlb, dog, dpw


HR = 8
PP = NPAIR


def _ffn_specs():
    cur = lambda off: pl.BlockSpec((PP, T, UPW), lambda j, c: (j + off // PP, c, 0))
    before = lambda off: pl.BlockSpec(
        (PP, HR, UPW), lambda j, c: (j + off // PP, jnp.maximum(c * (T // HR) - 1, 0), 0))
    after = lambda off: pl.BlockSpec(
        (PP, HR, UPW), lambda j, c: (j + off // PP, jnp.minimum((c + 1) * (T // HR), L // HR - 1), 0))
    wspec = lambda off, r: pl.BlockSpec((PP, r, UPW), lambda j, c: (j + off // PP, 0, 0))
    return cur, before, after, wspec


HR16 = 16


def _ffn_after16():
    return pl.BlockSpec((PP, HR16, UPW), lambda j, c: (j, jnp.minimum((c + 1) * (T // HR16), L // HR16 - 1), 0))


def ffn_act_fwd(f, w, b, name):
    strip = 16

    def body(*refs):
        for p in range(PP):
            one(*[r.at[p] for r in refs[:-2]], *refs[-2:])

    def one(fg_ref, fgh_ref, fu_ref, fuh_ref, wg_ref, wu_ref, bg_ref, bu_ref, a_ref, gc_ref, uc_ref, hg_scr, hu_scr):
        c = pl.program_id(1)
        for scr, x_ref, xh_ref in ((hg_scr, fg_ref, fgh_ref), (hu_scr, fu_ref, fuh_ref)):
            scr[0:HR, :] = jnp.where(c > 0, xh_ref[...], 0.0)
            scr[HR:HR + strip, :] = x_ref[0:strip, :]
        wg = [wg_ref[k:k + 1, :] for k in range(FFN_K)]
        wu = [wu_ref[k:k + 1, :] for k in range(FFN_K)]
        bg, bu = bg_ref[...], bu_ref[...]

        def conv(src, base, w, b):
            acc = b
            for k in range(FFN_K):
                o = base - (FFN_K - 1) + k
                acc = acc + w[k] * src[o:o + strip, :]
            return acc

        for s in range(T // strip):
            r = s * strip
            gate = conv(hg_scr, HR, wg, bg) if s == 0 else conv(fg_ref, r, wg, bg)
            up = conv(hu_scr, HR, wu, bu) if s == 0 else conv(fu_ref, r, wu, bu)
            gc_ref[r:r + strip, :] = gate.astype(BF16)
            uc_ref[r:r + strip, :] = up.astype(BF16)
            a_ref[r:r + strip, :] = (gate * _sig(gate) * up).astype(BF16)

    cur, before, _, wspec = _ffn_specs()
    pre = jax.ShapeDtypeStruct((NPAIR, L, UPW), BF16)
    return pl.pallas_call(
        body, name=name, out_shape=(jax.ShapeDtypeStruct((NPAIR, L, UPW), BF16), pre, pre), grid=(NPAIR // PP, NB),
        in_specs=[cur(0), before(0), cur(NPAIR), before(NPAIR), wspec(0, FFN_K), wspec(NPAIR, FFN_K),
                  wspec(0, 1), wspec(NPAIR, 1)],
        out_specs=(cur(0), cur(0), cur(0)),
        scratch_shapes=[pltpu.VMEM((HR + strip, UPW), F32), pltpu.VMEM((HR + strip, UPW), F32)])(
            f, f, f, f, w, w, b, b)


def ffn_act_bwd(f, gc, uc, da, w, name):
    ext = T + HR
    sub = 8

    def body(*refs):
        for p in range(PP):
            one(*[r.at[p] for r in refs[:-2]], *refs[-2:])

    def one(fg_ref, fu_ref, gc_ref, gca_ref, uc_ref, uca_ref, da_ref, daa_ref, wg_ref, wu_ref,
            dfg_ref, dfu_ref, dwg_ref, dwu_ref, dbg_ref, dbu_ref, dg_scr, du_scr):
        c = pl.program_id(1)
        wg = [wg_ref[k:k + 1, :] for k in range(FFN_K)]
        wu = [wu_ref[k:k + 1, :] for k in range(FFN_K)]

        for r in range(0, ext, HR16):
            rows = min(HR16, ext - r)
            if r < T:
                gate, up, dav = gc_ref[r:r + HR16, :].astype(F32), uc_ref[r:r + HR16, :].astype(F32), da_ref[r:r + HR16, :]
            else:
                gate, up = gca_ref[...].astype(F32)[0:rows], uca_ref[...].astype(F32)[0:rows]
                dav = jnp.where(c < NB - 1, daa_ref[...], 0.0)
            sg = _sig(gate)
            dg_scr[r:r + rows, :] = dav * up * (sg * (1.0 + gate * (1.0 - sg)))
            du_scr[r:r + rows, :] = dav * gate * sg

        def back(d_scr, x_ref, w, df_ref, dw_ref, db_ref):
            accs = [jnp.zeros((sub, UPW), F32) for _ in range(FFN_K + 1)]
            for s in range(T // (2 * sub)):
                pieces = []
                for r in (2 * s * sub, (2 * s + 1) * sub):
                    xv = x_ref[r:r + sub, :]
                    df = jnp.zeros((sub, UPW), F32)
                    for k in range(FFN_K):
                        off = FFN_K - 1 - k
                        dj = d_scr[r + off:r + off + sub, :]
                        df = df + w[k] * dj
                        accs[k] = accs[k] + xv * dj
                        if off == 0:
                            accs[FFN_K] = accs[FFN_K] + dj
                    pieces.append(df)
                df_ref[2 * s * sub:2 * (s + 1) * sub, :] = jnp.concatenate(pieces, axis=0).astype(BF16)
            dwp = _stack_rows([jnp.sum(a, axis=0, keepdims=True) for a in accs[:FFN_K]])
            dbp = jnp.sum(accs[FFN_K], axis=0, keepdims=True)

            @pl.when(c == 0)
            def _():
                dw_ref[...] = dwp
                db_ref[...] = dbp

            @pl.when(c > 0)
            def _():
                dw_ref[...] += dwp
                db_ref[...] += dbp

        back(dg_scr, fg_ref, wg, dfg_ref, dwg_ref, dbg_ref)
        back(du_scr, fu_ref, wu, dfu_ref, dwu_ref, dbu_ref)

    cur, _, after, wspec = _ffn_specs()
    slab = jax.ShapeDtypeStruct((NPAIR, L, UPW), BF16)
    wsh = jax.ShapeDtypeStruct((NPAIR, FFN_K, UPW), F32)
    bsh = jax.ShapeDtypeStruct((NPAIR, 1, UPW), F32)
    return pl.pallas_call(
        body, name=name, out_shape=(slab, slab, wsh, wsh, bsh, bsh), grid=(NPAIR // PP, NB),
        in_specs=[cur(0), cur(NPAIR), cur(0), _ffn_after16(), cur(0), _ffn_after16(), cur(0), after(0),
                  wspec(0, FFN_K), wspec(NPAIR, FFN_K)],
        out_specs=(cur(0), cur(0), wspec(0, FFN_K), wspec(0, FFN_K), wspec(0, 1), wspec(0, 1)),
        scratch_shapes=[pltpu.VMEM((ext, UPW), F32), pltpu.VMEM((ext, UPW), F32)])(
            f, f, gc, gc, uc, uc, da, da, w, w)


def _mesh_pos():
    return lax.axis_index("x"), lax.axis_index("y"), lax.axis_index("c")


def _peer(pos, k):
    x, y, c = pos
    px = 1 - x if k & 4 else x
    py = 1 - y if k & 2 else y
    pc = 1 - c if k & 1 else c
    return (px, py, pc), 4 * px + 2 * py + pc


_CHIP_FLIPS = (4, 2, 6)
_HBM = pl.BlockSpec(memory_space=pl.ANY)


def all_gather(shards, name):
    nt = len(shards)

    def body(*refs):
        ins, outs = refs[:nt], refs[nt:2 * nt]
        send, recv, local = refs[2 * nt:]
        pos = _mesh_pos()
        me = 4 * pos[0] + 2 * pos[1] + pos[2]
        sib, sib_id = _peer(pos, 1)

        def copy(t, k, block_id, to, src=None):
            dst = outs[t].at[block_id]
            return pltpu.make_async_remote_copy(
                src_ref=dst if src is None else src, dst_ref=dst, send_sem=send.at[t, k], recv_sem=recv.at[t, k],
                device_id=to, device_id_type=MESH)

        locals_ = [pltpu.make_async_copy(ins[t], outs[t].at[me], local.at[t]) for t in range(nt)]
        for cp in locals_:
            cp.start()
        started = []
        for j, flip in enumerate(_CHIP_FLIPS):
            for t in range(nt):
                started.append(copy(t, 1 + j, me, _peer(pos, flip)[0], src=ins[t]))
        for t in range(nt):
            started.append(copy(t, 0, me, sib, src=ins[t]))
        for cp in started:
            cp.start()
        for j, flip in enumerate(_CHIP_FLIPS):
            _, pid = _peer(pos, flip)
            for t in range(nt):
                copy(t, 1 + j, pid, sib).wait_recv()
                fwd = copy(t, 4 + j, pid, sib)
                fwd.start()
                started.append(fwd)
        for t in range(nt):
            copy(t, 0, sib_id, sib).wait_recv()
        for j, flip in enumerate(_CHIP_FLIPS):
            _, pid = _peer(pos, flip | 1)
            for t in range(nt):
                copy(t, 4 + j, pid, sib).wait_recv()
        for cp in started:
            cp.wait_send()
        for cp in locals_:
            cp.wait()

    return pl.pallas_call(
        body, name=name,
        out_shape=tuple(jax.ShapeDtypeStruct((NDEV,) + s.shape, s.dtype) for s in shards),
        in_specs=[_HBM] * nt, out_specs=tuple([_HBM] * nt),
        scratch_shapes=[pltpu.SemaphoreType.DMA((nt, NDEV - 1)), pltpu.SemaphoreType.DMA((nt, NDEV - 1)),
                        pltpu.SemaphoreType.DMA((nt,))])(*shards)


_SPLIT = dict(has_side_effects=pltpu.SideEffectType.DATAFLOW_SIDE_EFFECTING)
_SEM = pl.BlockSpec(memory_space=pltpu.SEMAPHORE)


def _exchange_start(name, srcs, lands, copies, after):
    ns, nl = len(srcs), len(lands)
    ncopy = len(copies(None, [None] * ns, [None] * nl))

    def body(*refs):
        src_refs, land_refs = refs[:ns], refs[ns:ns + nl]
        send, recv = refs[ns + nl + len(after):ns + nl + len(after) + 2]
        token = refs[-1]
        for i, (src, dst, peer) in enumerate(copies(_mesh_pos(), src_refs, land_refs)):
            pltpu.make_async_remote_copy(src_ref=src, dst_ref=dst, send_sem=send.at[i], recv_sem=recv.at[i],
                                         device_id=peer, device_id_type=MESH).start()
        token[...] = jnp.zeros_like(token)

    hbm = lambda a: pltpu.HBM(a.shape, a.dtype)
    out = pl.pallas_call(
        body, name=name,
        out_shape=(pltpu.SemaphoreType.DMA((ncopy,)), pltpu.SemaphoreType.DMA((ncopy,)),
                   *[hbm(a) for a in srcs], *[hbm(a) for a in lands], jax.ShapeDtypeStruct((8, 128), F32)),
        in_specs=[pl.BlockSpec(memory_space=pltpu.HBM)] * (ns + nl) + [_HBM] * len(after),
        out_specs=(_SEM, _SEM, *[pl.BlockSpec(memory_space=pltpu.HBM)] * (ns + nl),
                   pl.BlockSpec(memory_space=pltpu.VMEM)),
        input_output_aliases={i: 2 + i for i in range(ns + nl)},
        compiler_params=pltpu.CompilerParams(**_SPLIT))(
            *[pltpu.with_memory_space_constraint(a, pltpu.HBM) for a in list(srcs) + list(lands)], *after)
    return out[0], out[1], list(out[2:2 + ns]), list(out[2 + ns:2 + ns + nl]), out[-1]


def _exchange_wait(name, send, recv, srcs, lands, copies, after):
    ns, nl = len(srcs), len(lands)

    def body(*refs):
        src_refs, land_refs = refs[:ns], refs[ns:ns + nl]
        send_ref, recv_ref = refs[ns + nl:ns + nl + 2]
        pos = _mesh_pos()
        for i, (src, dst, peer) in enumerate(copies(pos, src_refs, land_refs, arriving=True)):
            cp = pltpu.make_async_remote_copy(src_ref=src, dst_ref=dst, send_sem=send_ref.at[i], recv_sem=recv_ref.at[i],
                                              device_id=peer, device_id_type=MESH)
            cp.wait_send()
            cp.wait_recv()

    hbm = lambda a: pltpu.HBM(a.shape, a.dtype)
    out = pl.pallas_call(
        body, name=name, out_shape=tuple(hbm(a) for a in list(srcs) + list(lands)),
        in_specs=[pl.BlockSpec(memory_space=pltpu.HBM)] * (ns + nl) + [_SEM, _SEM] + [_HBM] * len(after),
        out_specs=tuple([pl.BlockSpec(memory_space=pltpu.HBM)] * (ns + nl)),
        input_output_aliases={i: i for i in range(ns + nl)},
        compiler_params=pltpu.CompilerParams(**_SPLIT))(*srcs, *lands, send, recv, *after)
    return list(out[:ns]), list(out[ns:])


def _gather_copies(pos, srcs, lands, arriving=False):
    if pos is None:
        return [None] * (len(srcs) * (NDEV - 1))
    me = 4 * pos[0] + 2 * pos[1] + pos[2]
    out = []
    for src, land in zip(srcs, lands):
        for k in range(1, NDEV):
            peer, pid = _peer(pos, k)
            out.append((src, land.at[pid if arriving else me], peer))
    return out


def _scatter_copies(pos, srcs, lands, arriving=False):
    if pos is None:
        return [None] * (len(srcs) * (NDEV - 1))
    out = []
    for src, land in zip(srcs, lands):
        for k in range(1, NDEV):
            peer, pid = _peer(pos, k)
            out.append((src.at[pid], land.at[k - 1], peer))
    return out


def gather_start(name, shards, lands, after=()):
    return _exchange_start(name, shards, lands, _gather_copies, list(after))


def gather_wait(name, handle, after=()):
    send, recv, srcs, lands, _ = handle
    return _exchange_wait(name, send, recv, srcs, lands, _gather_copies, list(after))


_CHIP_LEVEL = (1,) + _CHIP_FLIPS


def _gather_chip_copies(pos, srcs, lands, arriving=False):
    if pos is None:
        return [None] * (len(srcs) * len(_CHIP_LEVEL))
    me = 4 * pos[0] + 2 * pos[1] + pos[2]
    out = []
    for src, land in zip(srcs, lands):
        for k in _CHIP_LEVEL:
            peer, pid = _peer(pos, k)
            out.append((src, land.at[pid if arriving else me], peer))
    return out


def _gather_forward_copies(pos, srcs, lands, arriving=False):
    if pos is None:
        return [None] * (len(lands) * len(_CHIP_FLIPS))
    sibling, _ = _peer(pos, 1)
    out = []
    for land in lands:
        for k in _CHIP_FLIPS:
            (px, py, pc), _ = _peer(pos, k)
            slot = 4 * px + 2 * py + ((1 - pc) if arriving else pc)
            out.append((land.at[slot], land.at[slot], sibling))
    return out


def gather2_start(name, shards, lands, after=()):
    return _exchange_start(name, shards, lands, _gather_chip_copies, list(after))


def gather2_forward(name, handle, after=()):
    send, recv, srcs, lands, _ = handle
    _, lands = _exchange_wait(name + "_wait", send, recv, srcs, lands, _gather_chip_copies, list(after))
    return _exchange_start(name + "_start", [], lands, _gather_forward_copies, [])


def gather2_wait(name, handle, after=()):
    send, recv, srcs, lands, _ = handle
    return _exchange_wait(name, send, recv, srcs, lands, _gather_forward_copies, list(after))


def scatter_start(name, grads, after=()):
    lands = [pltpu.with_memory_space_constraint(lax.empty((NDEV - 1,) + g.shape[1:], g.dtype), pltpu.HBM)
             for g in grads]
    return _exchange_start(name, grads, lands, _scatter_copies, list(after))


def scatter_wait(name, handle, after=()):
    send, recv, srcs, lands, _ = handle
    return _exchange_wait(name, send, recv, srcs, lands, _scatter_copies, list(after))


def sum_slots(parts, name):
    def body(p_ref, o_ref):
        acc = p_ref[0]
        for d in range(1, NDEV):
            acc = acc + p_ref[d]
        o_ref[...] = acc

    return pl.pallas_call(body, name=name, out_shape=jax.ShapeDtypeStruct(parts.shape[1:], F32))(parts)


def _adamw_update(g, w_ref, m_ref, v_ref, g_ref, d_ref, nm_ref, nv_ref):
    g_ref[...] = g
    nm = ADAM_B1 * m_ref[...] + (1.0 - ADAM_B1) * g
    nv = ADAM_B2 * v_ref[...] + (1.0 - ADAM_B2) * (g * g)
    nm_ref[...] = nm
    nv_ref[...] = nv
    m_hat = nm / (1.0 - ADAM_B1 ** ADAM_STEP)
    v_hat = nv / (1.0 - ADAM_B2 ** ADAM_STEP)
    d_ref[...] = -ADAM_LR * (m_hat / (jnp.sqrt(v_hat) + ADAM_EPS) + ADAM_WD * w_ref[...])


def adamw_big(own, lands, w, m, v, me, tr, name):
    _, r, c = own[0].shape
    nt = r // tr

    def body(me_ref, *refs):
        ins, (w_ref, m_ref, v_ref), outs = refs[:2 * DEPTH], refs[2 * DEPTH:2 * DEPTH + 3], refs[2 * DEPTH + 3:]
        layer = pl.program_id(0)
        for l in range(DEPTH):
            @pl.when(layer == l)
            def _(l=l):
                g = ins[2 * l][...].astype(F32)
                for s in range(NDEV - 1):
                    g = g + ins[2 * l + 1][s].astype(F32)
                _adamw_update(g, w_ref, m_ref, v_ref, *outs)

    pick = lambda l: (lambda layer, i, me_ref: jnp.where(layer == l, i, 0))
    in_specs = []
    for l in range(DEPTH):
        in_specs.append(pl.BlockSpec((None, tr, c), lambda layer, i, me_ref, f=pick(l): (me_ref[0], f(layer, i, me_ref), 0)))
        in_specs.append(pl.BlockSpec((NDEV - 1, tr, c), lambda layer, i, me_ref, f=pick(l): (0, f(layer, i, me_ref), 0)))
    blk = pl.BlockSpec((tr, c), lambda layer, i, me_ref: (layer * nt + i, 0))
    sh = jax.ShapeDtypeStruct((DEPTH * r, c), F32)
    args = [a for l in range(DEPTH) for a in (own[l], lands[l])]
    return pl.pallas_call(
        body, name=name, out_shape=(sh, sh, sh, sh),
        grid_spec=pltpu.PrefetchScalarGridSpec(
            num_scalar_prefetch=1, grid=(DEPTH, nt), in_specs=in_specs + [blk, blk, blk],
            out_specs=(blk, blk, blk, blk)),
        compiler_params=_params(VMEM_BIG))(me, *args, w, m, v)


def adamw_many(gs, ws, ms, vs, name):
    n = len(gs)

    def body(*refs):
        outs = refs[4 * n:]
        for i in range(n):
            _adamw_update(refs[i][...], refs[n + i], refs[2 * n + i], refs[3 * n + i], *outs[4 * i:4 * i + 4])

    out_shape = tuple(jax.ShapeDtypeStruct(w.shape, F32) for w in ws for _ in range(4))
    res = pl.pallas_call(body, name=name, out_shape=out_shape)(*gs, *ws, *ms, *vs)
    return [res[4 * i:4 * i + 4] for i in range(n)]


def local_step(x, target, P, hooks):
    tb = _ret_tables()
    h = jnp.concatenate([jnp.zeros((PAD, D), F32), P["meta"], x], axis=0)
    stash = []
    for l in range(DEPTH):
        s = {"h": h}
        s["w_in"], s["cv_pw"], s["w_out"] = hooks["mixer_w"](l, h)
        proj, s["u"] = in_proj_fwd(f"in_proj_{l}", h, P["norm_mix_g"][l], s["w_in"], 256,
                                   after=hooks["first_after"] if l == 0 else None)
        s["proj"] = proj
        s["o"] = attn_fwd(proj, P["q_norm_g"][l], P["k_norm_g"][l], P["attn_sinks"][l], f"attn_fwd_{l}")
        y_cv, s["xc"] = conv_fwd(proj, P["cv_dw_w"][l], P["cv_dw_b"][l], P["cv_ln_g"][l], P["cv_ln_b"][l],
                                 s["cv_pw"], P["cv_out_g"][l], f"conv_fwd_{l}")
        y_ret, s["states"] = ret_fwd(proj, P["ret_gn_g"][l], tb, f"ret_fwd_{l}")
        if "mixer_mid" in hooks:
            hooks["mixer_mid"](l, y_ret)
        h, s["ycat"] = out_proj_fwd(f"out_proj_{l}", s["o"], P["attn_out_g"][l], y_cv, y_ret, s["w_out"], h, 512)
        s["h1"] = h
        s["ffn_up"], s["ffn_down"] = hooks["ffn_w"](l, h)
        s["f"], s["u2"] = ffn_up_fwd(f"ffn_up_{l}", h, P["norm_ffn_g"][l], s["ffn_up"])
        a, s["gc"], s["uc"] = ffn_act_fwd(s["f"], P["ffn_dw_w"][l], P["ffn_dw_b"][l], f"ffn_act_fwd_{l}")
        s["a"] = a
        if l < DEPTH - 1:
            h = ffn_down_fwd(f"ffn_down_{l}", a, s["ffn_down"], h)
        else:
            dh, dhm, loss = ffn_down_loss(f"ffn_down_{l}", a, s["ffn_down"], h, target)
        stash.append(s)

    G = {n: [None] * DEPTH for n in ("norm_mix_g", "q_norm_g", "k_norm_g", "attn_sinks", "attn_out_g", "cv_dw_w",
                                     "cv_dw_b", "cv_ln_g", "cv_ln_b", "cv_out_g", "ret_gn_g", "norm_ffn_g",
                                     "ffn_dw_w", "ffn_dw_b")}
    for l in reversed(range(DEPTH)):
        s = stash[l]
        da = ffn_down_dx(f"ffn_down_dx_{l}", dhm, s["ffn_down"])
        dw_down = ffn_down_dw(f"ffn_down_dw_{l}", s["a"], dhm)
        dfg, dfu, dwg, dwu, dbg, dbu = ffn_act_bwd(s["f"], s["gc"], s["uc"], da, P["ffn_dw_w"][l], f"ffn_act_bwd_{l}")
        G["ffn_dw_w"][l] = jnp.concatenate([dwg, dwu], axis=0)
        G["ffn_dw_b"][l] = jnp.concatenate([dbg, dbu], axis=0)
        du2 = ffn_up_dx(f"ffn_up_dx_{l}", dfg, dfu, s["ffn_up"])
        dw_up = ffn_up_dw(f"ffn_up_dw_{l}", s["u2"], dfg, dfu)
        after = hooks["ffn_grads"](l, dw_down, dw_up)
        dh, dhm, G["norm_ffn_g"][l] = rms_bwd(s["h1"], P["norm_ffn_g"][l], du2, dh, f"rms_ffn_bwd_{l}", after=after)
        dycat = mm_nt(f"out_proj_dx_{l}", dhm, s["w_out"], 512)
        dw_out = mm_ta(f"out_proj_dw_{l}", s["ycat"], dhm, 512)
        (dq, dk_c, dk_p, dk_m, dv_c, dv_p, dv_m, G["q_norm_g"][l], G["k_norm_g"][l], dsk,
         G["attn_out_g"][l]) = attn_bwd(s["proj"], s["o"], dycat, P["attn_out_g"][l], P["q_norm_g"][l],
                                        P["k_norm_g"][l], P["attn_sinks"][l], f"attn_bwd_{l}")
        G["attn_sinks"][l] = dsk[:, 0]
        shift = lambda z: jnp.concatenate([z[T:], jnp.zeros((T, NKV * HD), F32)], axis=0)
        dk = (dk_c + shift(dk_p)).at[:T].add(dk_m)
        dv = (dv_c + shift(dv_p)).at[:T].add(dv_m)
        (dca, dcb, G["cv_dw_w"][l], G["cv_dw_b"][l], G["cv_ln_g"][l], G["cv_ln_b"][l], G["cv_out_g"][l],
         dpw) = conv_bwd(s["proj"], s["xc"], dycat, P["cv_dw_w"][l], P["cv_ln_g"][l], P["cv_ln_b"][l], s["cv_pw"],
                         P["cv_out_g"][l], f"conv_bwd_{l}")
        dret, G["ret_gn_g"][l] = ret_bwd(s["proj"], dycat, s["states"], P["ret_gn_g"][l], tb, f"ret_bwd_{l}")
        dproj = jnp.concatenate([dq, dk.astype(BF16), dv.astype(BF16), dca, dcb, dret], axis=1)
        du = mm_nn(f"in_proj_dx_{l}", dproj, s["w_in"], 512)
        dw_in = mm_ta(f"in_proj_dw_{l}", dproj, s["u"], 768)
        after = hooks["mixer_grads"](l, dw_out, dpw, dw_in)
        dh, dhm, G["norm_mix_g"][l] = rms_bwd(s["h"], P["norm_mix_g"][l], du, dh, f"rms_mix_bwd_{l}", after=after,
                                              last=(l == 0))
    return loss[0, 0], dh, dhm[PAD:T], G


_SMALL = ("meta", "norm_mix_g", "q_norm_g", "k_norm_g", "attn_sinks", "attn_out_g", "cv_dw_w", "cv_dw_b", "cv_ln_g",
          "cv_ln_b", "cv_out_g", "ret_gn_g", "norm_ffn_g", "ffn_dw_w", "ffn_dw_b")
_BIG = ("w_in", "cv_pw", "w_out", "ffn_up", "ffn_down")
_TRANSPOSED = ("w_in", "ffn_up")
_ORDER = ("meta", "norm_mix_g", "w_in", "q_norm_g", "k_norm_g", "attn_sinks", "attn_out_g", "cv_dw_w", "cv_dw_b",
          "cv_ln_g", "cv_ln_b", "cv_pw", "cv_out_g", "ret_gn_g", "w_out", "norm_ffn_g", "ffn_up", "ffn_dw_w",
          "ffn_dw_b", "ffn_down")
_SMALL_SHARDED = {"meta": D, "cv_dw_w": CV, "ffn_dw_w": 2 * D_FF}


def _pack(arrs):
    flat = jnp.concatenate([a.reshape(-1) for a in arrs])
    n = flat.shape[0]
    rows = -(-n // 1024) * 8
    return jnp.pad(flat, (0, rows * 128 - n)).reshape(rows, 128)


def _unpack(packed, shapes):
    flat = packed.reshape(-1)
    out, off = [], 0
    for s in shapes:
        n = int(np.prod(s))
        out.append(flat[off:off + n].reshape(s))
        off += n
    return out


def kernel(x, meta, norm_mix_g, w_in, q_norm_g, k_norm_g, attn_sinks, attn_out_g, cv_dw_w, cv_dw_b, cv_ln_g, cv_ln_b, cv_pw, cv_out_g, ret_gn_g, w_out, norm_ffn_g, ffn_up, ffn_dw_w, ffn_dw_b, ffn_down, loss_target, m_meta, m_norm_mix_g, m_w_in, m_q_norm_g, m_k_norm_g, m_attn_sinks, m_attn_out_g, m_cv_dw_w, m_cv_dw_b, m_cv_ln_g, m_cv_ln_b, m_cv_pw, m_cv_out_g, m_ret_gn_g, m_w_out, m_norm_ffn_g, m_ffn_up, m_ffn_dw_w, m_ffn_dw_b, m_ffn_down, v_meta, v_norm_mix_g, v_w_in, v_q_norm_g, v_k_norm_g, v_attn_sinks, v_attn_out_g, v_cv_dw_w, v_cv_dw_b, v_cv_ln_g, v_cv_ln_b, v_cv_pw, v_cv_out_g, v_ret_gn_g, v_w_out, v_norm_ffn_g, v_ffn_up, v_ffn_dw_w, v_ffn_dw_b, v_ffn_down):
    W = dict(meta=meta, norm_mix_g=norm_mix_g, w_in=w_in, q_norm_g=q_norm_g, k_norm_g=k_norm_g,
             attn_sinks=attn_sinks, attn_out_g=attn_out_g, cv_dw_w=cv_dw_w, cv_dw_b=cv_dw_b, cv_ln_g=cv_ln_g,
             cv_ln_b=cv_ln_b, cv_pw=cv_pw, cv_out_g=cv_out_g, ret_gn_g=ret_gn_g, w_out=w_out,
             norm_ffn_g=norm_ffn_g, ffn_up=ffn_up, ffn_dw_w=ffn_dw_w, ffn_dw_b=ffn_dw_b, ffn_down=ffn_down)
    M = dict(meta=m_meta, norm_mix_g=m_norm_mix_g, w_in=m_w_in, q_norm_g=m_q_norm_g, k_norm_g=m_k_norm_g,
             attn_sinks=m_attn_sinks, attn_out_g=m_attn_out_g, cv_dw_w=m_cv_dw_w, cv_dw_b=m_cv_dw_b,
             cv_ln_g=m_cv_ln_g, cv_ln_b=m_cv_ln_b, cv_pw=m_cv_pw, cv_out_g=m_cv_out_g, ret_gn_g=m_ret_gn_g,
             w_out=m_w_out, norm_ffn_g=m_norm_ffn_g, ffn_up=m_ffn_up, ffn_dw_w=m_ffn_dw_w, ffn_dw_b=m_ffn_dw_b,
             ffn_down=m_ffn_down)
    V = dict(meta=v_meta, norm_mix_g=v_norm_mix_g, w_in=v_w_in, q_norm_g=v_q_norm_g, k_norm_g=v_k_norm_g,
             attn_sinks=v_attn_sinks, attn_out_g=v_attn_out_g, cv_dw_w=v_cv_dw_w, cv_dw_b=v_cv_dw_b,
             cv_ln_g=v_cv_ln_g, cv_ln_b=v_cv_ln_b, cv_pw=v_cv_pw, cv_out_g=v_cv_out_g, ret_gn_g=v_ret_gn_g,
             w_out=v_w_out, norm_ffn_g=v_norm_ffn_g, ffn_up=v_ffn_up, ffn_dw_w=v_ffn_dw_w, ffn_dw_b=v_ffn_dw_b,
             ffn_down=v_ffn_down)
    me = 4 * lax.axis_index("x") + 2 * lax.axis_index("y") + lax.axis_index("c")
    for n in _TRANSPOSED:
        W[n], M[n], V[n] = (a.transpose(0, 2, 1) for a in (W[n], M[n], V[n]))

    sh = {n: [W[n][l].astype(BF16) for l in range(DEPTH)] for n in _BIG}
    mix = lambda l: [sh["w_in"][l], sh["cv_pw"][l], sh["w_out"][l]]
    ffn = lambda l: [sh["ffn_up"][l], sh["ffn_down"][l]]
    first = all_gather(mix(0) + [meta, cv_dw_w, ffn_dw_w], "gather_first")
    g_meta, g_cdw, g_fdw = first[3:6]

    def landing(shards):
        return [lax.dynamic_update_slice(lax.empty((NDEV,) + s.shape, s.dtype), s[None], (me,) + (0,) * s.ndim)
                for s in shards]

    gathers = {("ffn", 0): gather2_start("gather_ffn0_start", ffn(0), landing(ffn(0)), after=[first[0]])}
    gathers["mix", 1] = gather2_start("gather_mix1_start", mix(1), landing(mix(1)), after=[gathers["ffn", 0][4]])
    gathers["ffn", 1] = gather2_start("gather_ffn1_start", ffn(1), landing(ffn(1)), after=[gathers["mix", 1][4]])

    def mixer_w(l, h):
        g_in, g_pw, g_out = first[0:3] if l == 0 else gather2_wait(f"gather_mix{l}_wait", gathers["mix", l], after=[h])[1]
        return g_in.reshape(IN_W, D), g_pw.reshape(CV, CV), g_out.reshape(D, D)

    def mixer_mid(l, o):
        gathers["ffn", l] = gather2_forward(f"gather_ffn{l}_forward", gathers["ffn", l], after=[o])

    def ffn_w(l, h1):
        if l + 1 < DEPTH:
            gathers["mix", l + 1] = gather2_forward(f"gather_mix{l + 1}_forward", gathers["mix", l + 1], after=[h1])
        return gather2_wait(f"gather_ffn{l}_wait", gathers["ffn", l], after=[h1])[1]

    scatters = {}

    def ffn_grads(l, dw_down, dw_up):
        scatters["ffn", l] = scatter_start(f"scatter_ffn{l}_start", [dw_up, dw_down.reshape(NDEV, DNR, D)])
        return scatters["ffn", l][4]

    def mixer_grads(l, dw_out, dpw, dw_in):
        grads = [dw_in.reshape(NDEV, IN_W // NDEV, D), dpw.astype(BF16).reshape(NDEV, CV // NDEV, CV),
                 dw_out.reshape(NDEV, D // NDEV, D)]
        scatters["mix", l] = scatter_start(f"scatter_mix{l}_start", grads)
        return scatters["mix", l][4]

    P = dict(
        meta=g_meta.transpose(1, 0, 2).reshape(N_META, D),
        cv_dw_w=g_cdw.transpose(1, 2, 0, 3).reshape(DEPTH, CONV_K, CV),
        ffn_dw_w=g_fdw.transpose(1, 0, 2, 3),
        ffn_dw_b=ffn_dw_b.reshape(DEPTH, NDEV, 1, UPW),
        attn_sinks=attn_sinks,
    )
    for n in ("norm_mix_g", "q_norm_g", "k_norm_g", "attn_out_g", "cv_dw_b", "cv_ln_g", "cv_ln_b", "cv_out_g",
              "ret_gn_g", "norm_ffn_g"):
        P[n] = W[n].reshape(DEPTH, 1, -1)

    hooks = dict(mixer_w=mixer_w, mixer_mid=mixer_mid, ffn_w=ffn_w, ffn_grads=ffn_grads, mixer_grads=mixer_grads,
                 first_after=gathers["ffn", 1][4])
    loss_part, dx, dmeta, G = local_step(x[0], loss_target[0], P, hooks)

    small_full = {
        "meta": dmeta,
        "cv_dw_w": jnp.stack(G["cv_dw_w"]),
        "ffn_dw_w": jnp.stack([g.transpose(1, 0, 2).reshape(FFN_K, 2 * D_FF) for g in G["ffn_dw_w"]]),
        "ffn_dw_b": jnp.stack([g.reshape(2 * D_FF) for g in G["ffn_dw_b"]]),
        "attn_sinks": jnp.stack(G["attn_sinks"]),
    }
    for n in _SMALL:
        if n not in small_full:
            small_full[n] = jnp.stack([g.reshape(-1) for g in G[n]])
    shapes = [small_full[n].shape for n in _SMALL] + [(1,)]
    packed = _pack([small_full[n] for n in _SMALL] + [loss_part.reshape(1)])
    small_exchange = gather_start("reduce_small_start", [packed], landing([packed]), after=[dx])

    out = {}
    tiles = {"w_in": 144, "cv_pw": 32, "w_out": 128, "ffn_up": 176, "ffn_down": 176}
    me1 = me.astype(jnp.int32).reshape(1)
    own, lands = {n: [None] * DEPTH for n in _BIG}, {n: [None] * DEPTH for n in _BIG}

    def arrived(kind, names, after):
        for l in range(DEPTH):
            srcs, got = scatter_wait(f"scatter_{kind}{l}_wait", scatters[kind, l], after=after)
            for n, s_, g_ in zip(names, srcs, got):
                own[n][l], lands[n][l] = s_, g_

    def update(names):
        for n in names:
            shard = W[n].shape
            rows, cols = shard[0] * shard[1], shard[2]
            res = adamw_big(own[n], lands[n], W[n].reshape(rows, cols), M[n].reshape(rows, cols),
                            V[n].reshape(rows, cols), me1, tiles[n], f"adamw_{n}")
            out[n] = [r.reshape(shard) for r in res]

    arrived("ffn", ("ffn_up", "ffn_down"), [small_exchange[4]])
    update(("ffn_up", "ffn_down"))
    slots = gather_wait("reduce_small_wait", small_exchange, after=[out["ffn_down"][0]])[1][0]
    summed = _unpack(sum_slots(slots, "reduce_small_sum"), shapes)
    loss = summed[-1][0]
    small_g = []
    for n, g in zip(_SMALL, summed):
        if n in _SMALL_SHARDED:
            width = _SMALL_SHARDED[n] // NDEV
            g = lax.dynamic_slice_in_dim(g, me * width, width, axis=g.ndim - 1)
        small_g.append(g)
    flat2 = lambda a: a.reshape(-1, a.shape[-1])
    res = adamw_many([flat2(g) for g in small_g], *[[flat2(X[n]) for n in _SMALL] for X in (W, M, V)], "adamw_small")
    for n, r in zip(_SMALL, res):
        out[n] = [a.reshape(W[n].shape) for a in r]
    arrived("mix", ("w_in", "cv_pw", "w_out"), [out["ffn_down"][0], res[0][0]])
    update(("w_in", "cv_pw", "w_out"))
    for n in _TRANSPOSED:
        out[n] = [r.transpose(0, 2, 1) for r in out[n]]

    return (loss, dx[None], *[out[n][0] for n in _ORDER], *[out[n][1] for n in _ORDER],
            *[out[n][2] for n in _ORDER], *[out[n][3] for n in _ORDER])
```
